```python
import jax, jax.numpy as jnp
from jax import lax
import numpy as np

D_MODEL = 1024
BATCH = 8
SEQ = 4096
DEPTH = 2

HEAD_DIM = 64
N_MIX_HEADS = 12
N_MEM_HEADS = 4
MEM_LEN = 256
DILATED_GROUPS = ((128, 1), (512, 4), (2048, 16))
HEADS_PER_GROUP = N_MIX_HEADS // len(DILATED_GROUPS)
ROT_DIM = HEAD_DIM // 4
ROT_HALF = ROT_DIM // 2
ROPE_THETA = 500000.0
D_FF = 2816
BLOCK = 128
N_MIXERS = 2
N_A_LAYERS = (DEPTH + 1) // 2
N_B_LAYERS = DEPTH // 2
DEEPNORM_ALPHA = (2 * DEPTH) ** 0.25
DEEPNORM_BETA = (8 * DEPTH) ** -0.25
LN_EPS = 1e-5
MIX_W = N_MIX_HEADS * HEAD_DIM
MEM_W = N_MEM_HEADS * HEAD_DIM
A_IN_W = 3 * MIX_W + MEM_W
B_IN_W = 3 * MIX_W + N_MIX_HEADS + MEM_W
A_OUT_IN = HEADS_PER_GROUP * HEAD_DIM + MEM_W
B_OUT_IN = MIX_W + MEM_W
ATTN_SCALE = HEAD_DIM ** -0.5

kernel_name = "hybrid_dilated_fox_macaron_deepnorm"


def layer_norm(x, g, b):
    xf = x.astype(jnp.float32)
    mu = jnp.mean(xf, axis=-1, keepdims=True)
    var = jnp.mean(jnp.square(xf - mu), axis=-1, keepdims=True)
    y = (xf - mu) * lax.rsqrt(var + LN_EPS) * g.astype(jnp.float32) + b.astype(jnp.float32)
    return y.astype(x.dtype)


def swiglu(x, w_gate_up, w_down):
    gate, up = jnp.split(x @ w_gate_up, 2, axis=-1)
    return (jax.nn.silu(gate) * up) @ w_down


def rope_partial(t, cos, sin):
    c = cos[None, :, None, :].astype(t.dtype)
    s = sin[None, :, None, :].astype(t.dtype)
    t1 = t[..., :ROT_HALF]
    t2 = t[..., ROT_HALF:ROT_DIM]
    return jnp.concatenate([t1 * c - t2 * s, t2 * c + t1 * s, t[..., ROT_DIM:]], axis=-1)


def banded_causal_attention(q, k, v, n_back):
    N, L, H, dh = q.shape
    nb = -(-L // BLOCK)
    pad = nb * BLOCK - L
    padw = ((0, 0), (0, pad), (0, 0), (0, 0))
    q = jnp.pad(q, padw)
    k = jnp.pad(k, padw)
    v = jnp.pad(v, padw)
    qb = q.reshape(N, nb, BLOCK, H, dh)

    def with_prev(t):
        tb = t.reshape(N, nb, BLOCK, H, dh)
        prev = jnp.concatenate([jnp.zeros_like(tb[:, :1]), tb[:, :-1]], axis=1)
        return jnp.concatenate([prev, tb], axis=2)

    kk = with_prev(k)
    vv = with_prev(v)
    s = jnp.einsum('nbqhd,nbkhd->nbhqk', qb, kk).astype(jnp.float32) * ATTN_SCALE
    qi = jnp.arange(BLOCK)[:, None]
    ki = jnp.arange(2 * BLOCK)[None, :]
    dist = qi + BLOCK - ki
    kpos = jnp.arange(nb)[:, None, None] * BLOCK - BLOCK + ki[None]
    mask = (dist >= 0) & (dist <= n_back) & (kpos >= 0)
    s = jnp.where(mask[None, :, None], s, -jnp.inf)
    lse = jax.nn.logsumexp(s, axis=-1)
    p = jnp.exp(s - lse[..., None])
    o = jnp.einsum('nbhqk,nbkhd->nbqhd', p.astype(v.dtype), vv)
    o = o.reshape(N, nb * BLOCK, H, dh)[:, :L]
    lse = lse.transpose(0, 1, 3, 2).reshape(N, nb * BLOCK, H)[:, :L]
    return o, lse


def dilated_group_attention(q, k, v, window, dilation):
    B, S, H, dh = q.shape
    r = dilation
    L = S // r

    def split(t):
        return t.reshape(B, L, r, H, dh).transpose(0, 2, 1, 3, 4).reshape(B * r, L, H, dh)

    o, lse = banded_causal_attention(split(q), split(k), split(v), window // r)
    o = o.reshape(B, r, L, H, dh).transpose(0, 2, 1, 3, 4).reshape(B, S, H, dh)
    lse = lse.reshape(B, r, L, H).transpose(0, 2, 1, 3).reshape(B, S, H)
    return o, lse


def memory_attention(q_mem, mem, w_kv):
    B, S, _ = q_mem.shape
    M = mem.shape[1]
    qm = q_mem.reshape(B, S, N_MEM_HEADS, HEAD_DIM)
    km, vm = jnp.split(mem @ w_kv, 2, axis=-1)
    km = km.reshape(B, M, N_MEM_HEADS, HEAD_DIM)
    vm = vm.reshape(B, M, N_MEM_HEADS, HEAD_DIM)
    s = jnp.einsum('bshd,bmhd->bhsm', qm, km).astype(jnp.float32) * ATTN_SCALE
    p = jax.nn.softmax(s, axis=-1)
    return jnp.einsum('bhsm,bmhd->bshd', p.astype(vm.dtype), vm)


def dilated_mixer(x, mem, w_in, w_mem_kv, w_out, cos, sin):
    B, S, _ = x.shape
    h = x @ w_in
    q = h[..., :MIX_W].reshape(B, S, N_MIX_HEADS, HEAD_DIM)
    k = h[..., MIX_W:2 * MIX_W].reshape(B, S, N_MIX_HEADS, HEAD_DIM)
    v = h[..., 2 * MIX_W:3 * MIX_W].reshape(B, S, N_MIX_HEADS, HEAD_DIM)
    q_mem = h[..., 3 * MIX_W:]
    q = rope_partial(q, cos, sin)
    k = rope_partial(k, cos, sin)
    outs, lses = [], []
    for g, (window, dilation) in enumerate(DILATED_GROUPS):
        sl = slice(g * HEADS_PER_GROUP, (g + 1) * HEADS_PER_GROUP)
        o, l = dilated_group_attention(q[:, :, sl], k[:, :, sl], v[:, :, sl], window, dilation)
        outs.append(o)
        lses.append(l)
    alpha = jax.nn.softmax(jnp.stack(lses, axis=0), axis=0)
    o_a = jnp.einsum('gbsh,gbshd->bshd', alpha.astype(x.dtype), jnp.stack(outs, axis=0))
    o_m = memory_attention(q_mem, mem, w_mem_kv)
    cat = jnp.concatenate([o_a.reshape(B, S, -1), o_m.reshape(B, S, -1)], axis=-1)
    return cat @ w_out


def fox_attention(q, k, v, logf):
    B, S, H, dh = q.shape
    nq = S // BLOCK
    c = jnp.cumsum(logf, axis=1).transpose(0, 2, 1)
    qb = q.reshape(B, nq, BLOCK, H, dh).transpose(1, 0, 2, 3, 4)
    cqb = c.reshape(B, H, nq, BLOCK).transpose(2, 0, 1, 3)
    kpos = jnp.arange(S)

    def block(args):
        i, q_i, cq_i = args
        s = jnp.einsum('bqhd,bkhd->bhqk', q_i, k).astype(jnp.float32) * ATTN_SCALE
        s = s + cq_i[..., None] - c[:, :, None, :]
        qpos = i * BLOCK + jnp.arange(BLOCK)
        s = jnp.where(kpos[None, :] <= qpos[:, None], s, -jnp.inf)
        p = jax.nn.softmax(s, axis=-1)
        return jnp.einsum('bhqk,bkhd->bqhd', p.astype(v.dtype), v)

    o = lax.map(block, (jnp.arange(nq), qb, cqb))
    return o.transpose(1, 0, 2, 3, 4).reshape(B, S, H, dh)


def forgetting_mixer(x, mem, w_in, forget_bias, w_mem_kv, w_out):
    B, S, _ = x.shape
    h = x @ w_in
    q = h[..., :MIX_W].reshape(B, S, N_MIX_HEADS, HEAD_DIM)
    k = h[..., MIX_W:2 * MIX_W].reshape(B, S, N_MIX_HEADS, HEAD_DIM)
    v = h[..., 2 * MIX_W:3 * MIX_W].reshape(B, S, N_MIX_HEADS, HEAD_DIM)
    f_logit = h[..., 3 * MIX_W:3 * MIX_W + N_MIX_HEADS].astype(jnp.float32)
    q_mem = h[..., 3 * MIX_W + N_MIX_HEADS:]
    logf = jax.nn.log_sigmoid(f_logit + forget_bias.astype(jnp.float32))
    o_b = fox_attention(q, k, v, logf)
    o_m = memory_attention(q_mem, mem, w_mem_kv)
    cat = jnp.concatenate([o_b.reshape(B, S, -1), o_m.reshape(B, S, -1)], axis=-1)
    return cat @ w_out


def _fwd_setup_inputs(seed: int = 0) -> dict:
    key = jax.random.key(seed)
    ks = jax.random.split(key, 16)

    def nrm(k, shape, fan_in):
        return jax.random.normal(k, shape, jnp.float32) * fan_in ** -0.5

    x = jax.random.normal(ks[0], (BATCH, SEQ, D_MODEL), jnp.float32)
    mem = jax.random.normal(ks[1], (BATCH, MEM_LEN, D_MODEL), jnp.float32)
    ffn1_w_gate_up = nrm(ks[2], (DEPTH, D_MODEL, 2 * D_FF), D_MODEL)
    ffn1_w_down = nrm(ks[3], (DEPTH, D_FF, D_MODEL), D_FF) * DEEPNORM_BETA
    ffn2_w_gate_up = nrm(ks[4], (DEPTH, D_MODEL, 2 * D_FF), D_MODEL)
    ffn2_w_down = nrm(ks[5], (DEPTH, D_FF, D_MODEL), D_FF) * DEEPNORM_BETA
    ln_gain = 1.0 + 0.02 * jax.random.normal(ks[6], (DEPTH, 3, D_MODEL), jnp.float32)
    ln_bias = 0.02 * jax.random.normal(ks[7], (DEPTH, 3, D_MODEL), jnp.float32)
    mem_w_kv = nrm(ks[8], (DEPTH, D_MODEL, 2 * MEM_W), D_MODEL)
    a_w_in = nrm(ks[9], (N_A_LAYERS, D_MODEL, A_IN_W), D_MODEL)
    a_w_out = nrm(ks[10], (N_A_LAYERS, A_OUT_IN, D_MODEL), A_OUT_IN) * DEEPNORM_BETA
    b_w_in = nrm(ks[11], (N_B_LAYERS, D_MODEL, B_IN_W), D_MODEL)
    b_forget_bias = jax.random.uniform(ks[12], (N_B_LAYERS, N_MIX_HEADS), jnp.float32, 1.0, 4.0)
    b_w_out = nrm(ks[13], (N_B_LAYERS, B_OUT_IN, D_MODEL), B_OUT_IN) * DEEPNORM_BETA
    return {"x": x, "mem": mem,
            "ffn1_w_gate_up": ffn1_w_gate_up, "ffn1_w_down": ffn1_w_down,
            "ffn2_w_gate_up": ffn2_w_gate_up, "ffn2_w_down": ffn2_w_down,
            "ln_gain": ln_gain, "ln_bias": ln_bias, "mem_w_kv": mem_w_kv,
            "a_w_in": a_w_in, "a_w_out": a_w_out,
            "b_w_in": b_w_in, "b_forget_bias": b_forget_bias, "b_w_out": b_w_out}


def _fwd_reference(x, mem, ffn1_w_gate_up, ffn1_w_down, ffn2_w_gate_up, ffn2_w_down,
              ln_gain, ln_bias, mem_w_kv, a_w_in, a_w_out, b_w_in, b_forget_bias, b_w_out):
    pos = jnp.arange(x.shape[1], dtype=jnp.float32)
    inv_freq = 1.0 / (ROPE_THETA ** (jnp.arange(ROT_HALF, dtype=jnp.float32) / ROT_HALF))
    ang = pos[:, None] * inv_freq[None, :]
    cos = jnp.cos(ang)
    sin = jnp.sin(ang)
    for i in range(DEPTH):
        j = i // N_MIXERS
        x = layer_norm(DEEPNORM_ALPHA * x + 0.5 * swiglu(x, ffn1_w_gate_up[i], ffn1_w_down[i]),
                       ln_gain[i, 0], ln_bias[i, 0])
        if i % N_MIXERS == 0:
            mix = dilated_mixer(x, mem, a_w_in[j], mem_w_kv[i], a_w_out[j], cos, sin)
        else:
            mix = forgetting_mixer(x, mem, b_w_in[j], b_forget_bias[j], mem_w_kv[i], b_w_out[j])
        x = layer_norm(DEEPNORM_ALPHA * x + mix, ln_gain[i, 1], ln_bias[i, 1])
        x = layer_norm(DEEPNORM_ALPHA * x + 0.5 * swiglu(x, ffn2_w_gate_up[i], ffn2_w_down[i]),
                       ln_gain[i, 2], ln_bias[i, 2])
    return x


import jax as _jax
import jax.numpy as _jnp

TWIN_FORMAT = 'train_step'
FWD_PARAMS = ['x', 'mem', 'ffn1_w_gate_up', 'ffn1_w_down', 'ffn2_w_gate_up', 'ffn2_w_down', 'ln_gain', 'ln_bias', 'mem_w_kv', 'a_w_in', 'a_w_out', 'b_w_in', 'b_forget_bias', 'b_w_out']
TWIN_WEIGHTS = ['ffn1_w_gate_up', 'ffn1_w_down', 'ffn2_w_gate_up', 'ffn2_w_down', 'ln_gain', 'ln_bias', 'mem_w_kv', 'a_w_in', 'a_w_out', 'b_w_in', 'b_forget_bias', 'b_w_out']
TWIN_DIFF_INPUT = 'x'
TWIN_INPUTS = ['x', 'mem', 'ffn1_w_gate_up', 'ffn1_w_down', 'ffn2_w_gate_up', 'ffn2_w_down', 'ln_gain', 'ln_bias', 'mem_w_kv', 'a_w_in', 'a_w_out', 'b_w_in', 'b_forget_bias', 'b_w_out', 'loss_target', 'm_ffn1_w_gate_up', 'm_ffn1_w_down', 'm_ffn2_w_gate_up', 'm_ffn2_w_down', 'm_ln_gain', 'm_ln_bias', 'm_mem_w_kv', 'm_a_w_in', 'm_a_w_out', 'm_b_w_in', 'm_b_forget_bias', 'm_b_w_out', 'v_ffn1_w_gate_up', 'v_ffn1_w_down', 'v_ffn2_w_gate_up', 'v_ffn2_w_down', 'v_ln_gain', 'v_ln_bias', 'v_mem_w_kv', 'v_a_w_in', 'v_a_w_out', 'v_b_w_in', 'v_b_forget_bias', 'v_b_w_out']
TWIN_OUTPUTS = ['loss', 'grad_x', 'grad_ffn1_w_gate_up', 'grad_ffn1_w_down', 'grad_ffn2_w_gate_up', 'grad_ffn2_w_down', 'grad_ln_gain', 'grad_ln_bias', 'grad_mem_w_kv', 'grad_a_w_in', 'grad_a_w_out', 'grad_b_w_in', 'grad_b_forget_bias', 'grad_b_w_out', 'delta_ffn1_w_gate_up', 'delta_ffn1_w_down', 'delta_ffn2_w_gate_up', 'delta_ffn2_w_down', 'delta_ln_gain', 'delta_ln_bias', 'delta_mem_w_kv', 'delta_a_w_in', 'delta_a_w_out', 'delta_b_w_in', 'delta_b_forget_bias', 'delta_b_w_out', 'new_m_ffn1_w_gate_up', 'new_m_ffn1_w_down', 'new_m_ffn2_w_gate_up', 'new_m_ffn2_w_down', 'new_m_ln_gain', 'new_m_ln_bias', 'new_m_mem_w_kv', 'new_m_a_w_in', 'new_m_a_w_out', 'new_m_b_w_in', 'new_m_b_forget_bias', 'new_m_b_w_out', 'new_v_ffn1_w_gate_up', 'new_v_ffn1_w_down', 'new_v_ffn2_w_gate_up', 'new_v_ffn2_w_down', 'new_v_ln_gain', 'new_v_ln_bias', 'new_v_mem_w_kv', 'new_v_a_w_in', 'new_v_a_w_out', 'new_v_b_w_in', 'new_v_b_forget_bias', 'new_v_b_w_out']
TWIN_LEAF_KINDS = {'loss': 'loss', 'grad_x': 'grad_x', 'grad_ffn1_w_gate_up': 'grad_w', 'grad_ffn1_w_down': 'grad_w', 'grad_ffn2_w_gate_up': 'grad_w', 'grad_ffn2_w_down': 'grad_w', 'grad_ln_gain': 'grad_w', 'grad_ln_bias': 'grad_w', 'grad_mem_w_kv': 'grad_w', 'grad_a_w_in': 'grad_w', 'grad_a_w_out': 'grad_w', 'grad_b_w_in': 'grad_w', 'grad_b_forget_bias': 'grad_w', 'grad_b_w_out': 'grad_w', 'delta_ffn1_w_gate_up': 'delta_w', 'delta_ffn1_w_down': 'delta_w', 'delta_ffn2_w_gate_up': 'delta_w', 'delta_ffn2_w_down': 'delta_w', 'delta_ln_gain': 'delta_w', 'delta_ln_bias': 'delta_w', 'delta_mem_w_kv': 'delta_w', 'delta_a_w_in': 'delta_w', 'delta_a_w_out': 'delta_w', 'delta_b_w_in': 'delta_w', 'delta_b_forget_bias': 'delta_w', 'delta_b_w_out': 'delta_w', 'new_m_ffn1_w_gate_up': 'new_m', 'new_m_ffn1_w_down': 'new_m', 'new_m_ffn2_w_gate_up': 'new_m', 'new_m_ffn2_w_down': 'new_m', 'new_m_ln_gain': 'new_m', 'new_m_ln_bias': 'new_m', 'new_m_mem_w_kv': 'new_m', 'new_m_a_w_in': 'new_m', 'new_m_a_w_out': 'new_m', 'new_m_b_w_in': 'new_m', 'new_m_b_forget_bias': 'new_m', 'new_m_b_w_out': 'new_m', 'new_v_ffn1_w_gate_up': 'new_v', 'new_v_ffn1_w_down': 'new_v', 'new_v_ffn2_w_gate_up': 'new_v', 'new_v_ffn2_w_down': 'new_v', 'new_v_ln_gain': 'new_v', 'new_v_ln_bias': 'new_v', 'new_v_mem_w_kv': 'new_v', 'new_v_a_w_in': 'new_v', 'new_v_a_w_out': 'new_v', 'new_v_b_w_in': 'new_v', 'new_v_b_forget_bias': 'new_v', 'new_v_b_w_out': 'new_v'}


def _forward(args):
    return _fwd_reference(*[args[k] for k in FWD_PARAMS])


def _output_shape():
    out = _jax.eval_shape(lambda: _forward(_fwd_setup_inputs(0)))
    return out.shape, out.dtype

N_MICROBATCH = 1
ADAM_LR = 0.001
ADAM_B1 = 0.9
ADAM_B2 = 0.999
ADAM_EPS = 1e-08
ADAM_WD = 0.01
ADAM_STEP = 10
PER_EXAMPLE_BATCH_AXIS = {'x': 0, 'mem': 0, 'loss_target': 0}
SHARED_INPUTS = []
_WEIGHT_DTYPES = {'ffn1_w_gate_up': _jnp.float32, 'ffn1_w_down': _jnp.float32, 'ffn2_w_gate_up': _jnp.float32, 'ffn2_w_down': _jnp.float32, 'ln_gain': _jnp.float32, 'ln_bias': _jnp.float32, 'mem_w_kv': _jnp.float32, 'a_w_in': _jnp.float32, 'a_w_out': _jnp.float32, 'b_w_in': _jnp.float32, 'b_forget_bias': _jnp.float32, 'b_w_out': _jnp.float32}
MOMENT_SCALE = {'ffn1_w_gate_up': 1.162197e-02, 'ffn1_w_down': 3.783991e-02, 'ffn2_w_gate_up': 1.149346e-02, 'ffn2_w_down': 3.749531e-02, 'ln_gain': 1.311948e+01, 'ln_bias': 6.383648e-01, 'mem_w_kv': 8.740520e-03, 'a_w_in': 9.246718e-03, 'a_w_out': 2.014186e-02, 'b_w_in': 1.928561e-02, 'b_forget_bias': 1.222443e-01, 'b_w_out': 4.191636e-02}


def _to_microbatches(a, axis):
    t = _jnp.moveaxis(a, axis, 0)
    t = t.reshape((N_MICROBATCH, t.shape[0] // N_MICROBATCH) + t.shape[1:])
    return _jnp.moveaxis(t, 1, axis + 1)


def setup_inputs(seed: int = 0) -> dict:
    inp = _fwd_setup_inputs(seed)
    key = _jax.random.fold_in(_jax.random.key(seed), 7919)
    shape, _ = _output_shape()
    out = dict(inp)
    out["loss_target"] = _jax.random.normal(_jax.random.fold_in(key, 0), shape, _jnp.float32)
    for i, name in enumerate(TWIN_WEIGHTS):
        w = inp[name].astype(_jnp.float32)
        if MOMENT_SCALE is None:
            s = _jnp.sqrt(_jnp.mean(_jnp.square(w)) + 1e-30)
        else:
            s = MOMENT_SCALE[name]
        km, kv = _jax.random.split(_jax.random.fold_in(key, i + 1))
        out[name] = w
        out["m_" + name] = s * _jax.random.normal(km, w.shape, _jnp.float32)
        out["v_" + name] = (s * s) * _jax.random.uniform(kv, w.shape, _jnp.float32, 0.5, 1.5)
    if N_MICROBATCH > 1:
        for name, axis in PER_EXAMPLE_BATCH_AXIS.items():
            out[name] = _to_microbatches(out[name], axis)
    return {'x': out['x'], 'mem': out['mem'], 'ffn1_w_gate_up': out['ffn1_w_gate_up'], 'ffn1_w_down': out['ffn1_w_down'], 'ffn2_w_gate_up': out['ffn2_w_gate_up'], 'ffn2_w_down': out['ffn2_w_down'], 'ln_gain': out['ln_gain'], 'ln_bias': out['ln_bias'], 'mem_w_kv': out['mem_w_kv'], 'a_w_in': out['a_w_in'], 'a_w_out': out['a_w_out'], 'b_w_in': out['b_w_in'], 'b_forget_bias': out['b_forget_bias'], 'b_w_out': out['b_w_out'], 'loss_target': out['loss_target'], 'm_ffn1_w_gate_up': out['m_ffn1_w_gate_up'], 'm_ffn1_w_down': out['m_ffn1_w_down'], 'm_ffn2_w_gate_up': out['m_ffn2_w_gate_up'], 'm_ffn2_w_down': out['m_ffn2_w_down'], 'm_ln_gain': out['m_ln_gain'], 'm_ln_bias': out['m_ln_bias'], 'm_mem_w_kv': out['m_mem_w_kv'], 'm_a_w_in': out['m_a_w_in'], 'm_a_w_out': out['m_a_w_out'], 'm_b_w_in': out['m_b_w_in'], 'm_b_forget_bias': out['m_b_forget_bias'], 'm_b_w_out': out['m_b_w_out'], 'v_ffn1_w_gate_up': out['v_ffn1_w_gate_up'], 'v_ffn1_w_down': out['v_ffn1_w_down'], 'v_ffn2_w_gate_up': out['v_ffn2_w_gate_up'], 'v_ffn2_w_down': out['v_ffn2_w_down'], 'v_ln_gain': out['v_ln_gain'], 'v_ln_bias': out['v_ln_bias'], 'v_mem_w_kv': out['v_mem_w_kv'], 'v_a_w_in': out['v_a_w_in'], 'v_a_w_out': out['v_a_w_out'], 'v_b_w_in': out['v_b_w_in'], 'v_b_forget_bias': out['v_b_forget_bias'], 'v_b_w_out': out['v_b_w_out']}


def _loss(weights, diff, rest, loss_target):
    with _jax.named_scope("forward"):
        args = {**rest, TWIN_DIFF_INPUT: diff, **{k: w.astype(_WEIGHT_DTYPES[k]) for k, w in weights.items()}}
        y = _forward(args)
    with _jax.named_scope("loss_head"):
        err = _jnp.square(y.astype(_jnp.float32) - loss_target)
        return 0.5 * _jnp.sum(_jnp.mean(err, axis=-1)) if err.ndim else 0.5 * err


def _adamw(w, g, m, v):
    m = ADAM_B1 * m + (1.0 - ADAM_B1) * g
    v = ADAM_B2 * v + (1.0 - ADAM_B2) * _jnp.square(g)
    m_hat = m / (1.0 - ADAM_B1 ** ADAM_STEP)
    v_hat = v / (1.0 - ADAM_B2 ** ADAM_STEP)
    delta = -ADAM_LR * (m_hat / (_jnp.sqrt(v_hat) + ADAM_EPS) + ADAM_WD * w)
    return delta, m, v


def reference(x, mem, ffn1_w_gate_up, ffn1_w_down, ffn2_w_gate_up, ffn2_w_down, ln_gain, ln_bias, mem_w_kv, a_w_in, a_w_out, b_w_in, b_forget_bias, b_w_out, loss_target, m_ffn1_w_gate_up, m_ffn1_w_down, m_ffn2_w_gate_up, m_ffn2_w_down, m_ln_gain, m_ln_bias, m_mem_w_kv, m_a_w_in, m_a_w_out, m_b_w_in, m_b_forget_bias, m_b_w_out, v_ffn1_w_gate_up, v_ffn1_w_down, v_ffn2_w_gate_up, v_ffn2_w_down, v_ln_gain, v_ln_bias, v_mem_w_kv, v_a_w_in, v_a_w_out, v_b_w_in, v_b_forget_bias, v_b_w_out):
    given = dict(x=x, mem=mem, ffn1_w_gate_up=ffn1_w_gate_up, ffn1_w_down=ffn1_w_down, ffn2_w_gate_up=ffn2_w_gate_up, ffn2_w_down=ffn2_w_down, ln_gain=ln_gain, ln_bias=ln_bias, mem_w_kv=mem_w_kv, a_w_in=a_w_in, a_w_out=a_w_out, b_w_in=b_w_in, b_forget_bias=b_forget_bias, b_w_out=b_w_out, loss_target=loss_target, m_ffn1_w_gate_up=m_ffn1_w_gate_up, m_ffn1_w_down=m_ffn1_w_down, m_ffn2_w_gate_up=m_ffn2_w_gate_up, m_ffn2_w_down=m_ffn2_w_down, m_ln_gain=m_ln_gain, m_ln_bias=m_ln_bias, m_mem_w_kv=m_mem_w_kv, m_a_w_in=m_a_w_in, m_a_w_out=m_a_w_out, m_b_w_in=m_b_w_in, m_b_forget_bias=m_b_forget_bias, m_b_w_out=m_b_w_out, v_ffn1_w_gate_up=v_ffn1_w_gate_up, v_ffn1_w_down=v_ffn1_w_down, v_ffn2_w_gate_up=v_ffn2_w_gate_up, v_ffn2_w_down=v_ffn2_w_down, v_ln_gain=v_ln_gain, v_ln_bias=v_ln_bias, v_mem_w_kv=v_mem_w_kv, v_a_w_in=v_a_w_in, v_a_w_out=v_a_w_out, v_b_w_in=v_b_w_in, v_b_forget_bias=v_b_forget_bias, v_b_w_out=v_b_w_out)
    weights = {n: given[n] for n in TWIN_WEIGHTS}
    shared = {n: given[n] for n in SHARED_INPUTS}
    per_example = {n: given[n] for n in ['x', 'mem']}
    grad_fn = _jax.value_and_grad(_loss, argnums=(0, 1))

    def one_microbatch(ex, loss_target):
        ex = dict(ex)
        diff = ex.pop(TWIN_DIFF_INPUT)
        return grad_fn(weights, diff, {**shared, **ex}, loss_target)

    if N_MICROBATCH == 1:
        loss, (grad_w, grad_x) = one_microbatch(per_example, given["loss_target"])
    else:
        def body(carry, xs):
            loss_sum, grad_sum = carry
            l_k, (gw_k, gx_k) = one_microbatch(xs[0], xs[1])
            with _jax.named_scope("update"):
                return (loss_sum + l_k, _jax.tree.map(_jnp.add, grad_sum, gw_k)), gx_k

        init = (_jnp.zeros((), _jnp.float32), _jax.tree.map(_jnp.zeros_like, weights))
        (loss, grad_w), grad_x = _jax.lax.scan(body, init, (per_example, given["loss_target"]))
    with _jax.named_scope("update"):
        delta_w, new_m, new_v = {}, {}, {}
        for n in TWIN_WEIGHTS:
            delta_w[n], new_m[n], new_v[n] = _adamw(weights[n], grad_w[n], given["m_" + n], given["v_" + n])
    return (loss, grad_x, *[grad_w[n] for n in TWIN_WEIGHTS], *[delta_w[n] for n in TWIN_WEIGHTS],
            *[new_m[n] for n in TWIN_WEIGHTS], *[new_v[n] for n in TWIN_WEIGHTS])
```

```python
import functools

import jax
import jax.numpy as jnp
from jax import lax
from jax.experimental import pallas as pl
from jax.experimental.pallas import tpu as pltpu

F32 = jnp.float32
BF = jnp.bfloat16
MESH_ID = pl.DeviceIdType.MESH

N_DEV = 8
DEPTH = 2
HEAD_DIM = 64
LANES = 128
N_MIX_HEADS = 12
N_MEM_HEADS = 4
MIX_W = N_MIX_HEADS * HEAD_DIM
MEM_W = N_MEM_HEADS * HEAD_DIM
N_GROUPS = 3
GROUP_W = MIX_W // N_GROUPS
BLOCK = 128
ROT_HALF = 8
ROPE_THETA = 500000.0
ALPHA = (2 * DEPTH) ** 0.25
LN_EPS = 1e-5
SCALE = HEAD_DIM ** -0.5
NEG = -1e30
B_IN_PAD = 2688
ADAM_LR, ADAM_B1, ADAM_B2, ADAM_EPS, ADAM_WD, ADAM_STEP = 0.001, 0.9, 0.999, 1e-08, 0.01, 10
VMEM_LIMIT = 56 * 1024 * 1024


def _cp(*sem):
    return pltpu.CompilerParams(dimension_semantics=sem, vmem_limit_bytes=VMEM_LIMIT)


def _dot(a, b):
    return jnp.dot(a, b, preferred_element_type=F32)


def _dot_nt(a, b):
    return lax.dot_general(a, b, (((1,), (1,)), ((), ())), preferred_element_type=F32)


def _dot_tn(a, b):
    return lax.dot_general(a, b, (((0,), (0,)), ((), ())), preferred_element_type=F32)


def _sigmoid(x):
    return 1.0 / (1.0 + jnp.exp(-x))


def _tile(n, cap=1024):
    if n <= cap:
        return n
    best = LANES
    for t in range(LANES, cap + 1, LANES):
        if n % t == 0:
            best = t
    return best


def _rows(s, cap=512):
    return s if s <= cap else cap


def _mm_nn(a, b, out_dtype, name):
    m, k = a.shape
    n = b.shape[1]
    tm, tn = _rows(m), _tile(n)

    def body(a_ref, b_ref, o_ref):
        o_ref[...] = _dot(a_ref[...], b_ref[...]).astype(o_ref.dtype)

    return pl.pallas_call(
        body, name=name, grid=(n // tn, m // tm),
        in_specs=[pl.BlockSpec((tm, k), lambda j, i: (i, 0)), pl.BlockSpec((k, tn), lambda j, i: (0, j))],
        out_specs=pl.BlockSpec((tm, tn), lambda j, i: (i, j)),
        out_shape=jax.ShapeDtypeStruct((m, n), out_dtype),
        compiler_params=_cp("parallel", "parallel"))(a, b)


def _mm_tn(a, b, name, out_dtype=BF):
    na, s, m = a.shape
    nb, _, n = b.shape
    no = max(na, nb)
    ts, tn = _rows(s), _tile(n)
    ns = s // ts

    def body(a_ref, b_ref, o_ref, acc):
        k = pl.program_id(2)

        @pl.when(k == 0)
        def _():
            acc[...] = jnp.zeros_like(acc)

        acc[...] += _dot_tn(a_ref[...], b_ref[...])

        @pl.when(k == ns - 1)
        def _():
            o_ref[...] = acc[...].astype(o_ref.dtype)

    return pl.pallas_call(
        body, name=name, grid=(no, n // tn, ns),
        in_specs=[pl.BlockSpec((None, ts, m), lambda j, c, k: (j if na > 1 else 0, k, 0)),
                  pl.BlockSpec((None, ts, tn), lambda j, c, k: (j if nb > 1 else 0, k, c))],
        out_specs=pl.BlockSpec((None, m, tn), lambda j, c, k: (j, 0, c)),
        out_shape=jax.ShapeDtypeStruct((no, m, n), out_dtype),
        scratch_shapes=[pltpu.VMEM((m, tn), F32)],
        compiler_params=_cp("parallel", "parallel", "arbitrary"))(a, b)


def _mm_nt(dh, w, name, res=None, out_dtype=F32):
    nc, s, kc = dh.shape
    d = w.shape[1]
    ts = _rows(s)
    has_res = res is not None

    def body(*refs):
        if has_res:
            dh_ref, w_ref, r_ref, o_ref, acc = refs
        else:
            dh_ref, w_ref, o_ref, acc = refs
        j = pl.program_id(1)

        @pl.when(j == 0)
        def _():
            acc[...] = jnp.zeros_like(acc)

        acc[...] += _dot_nt(dh_ref[...], w_ref[...])

        @pl.when(j == nc - 1)
        def _():
            out = acc[...]
            if has_res:
                out = out + ALPHA * r_ref[...]
            o_ref[...] = out.astype(o_ref.dtype)

    in_specs = [pl.BlockSpec((None, ts, kc), lambda i, j: (j, i, 0)),
                pl.BlockSpec((None, d, kc), lambda i, j: (j, 0, 0))]
    args = [dh, w]
    if has_res:
        in_specs.append(pl.BlockSpec((ts, d), lambda i, j: (i, 0)))
        args.append(res)
    return pl.pallas_call(
        body, name=name, grid=(s // ts, nc), in_specs=in_specs,
        out_specs=pl.BlockSpec((ts, d), lambda i, j: (i, 0)),
        out_shape=jax.ShapeDtypeStruct((s, d), out_dtype),
        scratch_shapes=[pltpu.VMEM((ts, d), F32)],
        compiler_params=_cp("parallel", "arbitrary"))(*args)


def _mm_res_ln(a, w, x, gain, bias, fscale, name):
    nc, s, kc = a.shape
    d = w.shape[2]
    ts = _rows(s)

    def body(a_ref, w_ref, x_ref, g_ref, b_ref, y_ref, yb_ref, xh_ref, r_ref, acc):
        j = pl.program_id(1)

        @pl.when(j == 0)
        def _():
            acc[...] = jnp.zeros_like(acc)

        acc[...] += _dot(a_ref[...], w_ref[...])

        @pl.when(j == nc - 1)
        def _():
            z = ALPHA * x_ref[...] + fscale * acc[...]
            mu = jnp.mean(z, axis=-1, keepdims=True)
            zc = z - mu
            var = jnp.mean(zc * zc, axis=-1, keepdims=True)
            r = lax.rsqrt(var + LN_EPS)
            xh = zc * r
            y = xh * g_ref[...] + b_ref[...]
            y_ref[...] = y
            yb_ref[...] = y.astype(BF)
            xh_ref[...] = xh
            r_ref[...] = r

    row = pl.BlockSpec((ts, d), lambda i, j: (i, 0))
    vec = pl.BlockSpec((1, d), lambda i, j: (0, 0))
    return pl.pallas_call(
        body, name=name, grid=(s // ts, nc),
        in_specs=[pl.BlockSpec((None, ts, kc), lambda i, j: (j, i, 0)),
                  pl.BlockSpec((None, kc, d), lambda i, j: (j, 0, 0)), row, vec, vec],
        out_specs=[row, row, row, pl.BlockSpec((ts, 1), lambda i, j: (i, 0))],
        out_shape=[jax.ShapeDtypeStruct((s, d), F32), jax.ShapeDtypeStruct((s, d), BF),
                   jax.ShapeDtypeStruct((s, d), F32), jax.ShapeDtypeStruct((s, 1), F32)],
        scratch_shapes=[pltpu.VMEM((ts, d), F32)],
        compiler_params=_cp("parallel", "arbitrary"))(a, w, x, gain, bias)


def _ln_bwd(dy, xh, rstd, gain, fscale, name):
    s, d = dy.shape
    ts = _rows(s)

    def body(dy_ref, xh_ref, r_ref, g_ref, dz_ref, dzb_ref, dg_ref, db_ref):
        i = pl.program_id(0)
        dyv = dy_ref[...]
        xhv = xh_ref[...]
        dxh = dyv * g_ref[...]
        m1 = jnp.mean(dxh, axis=-1, keepdims=True)
        m2 = jnp.mean(dxh * xhv, axis=-1, keepdims=True)
        dz = r_ref[...] * (dxh - m1 - xhv * m2)
        dz_ref[...] = dz
        dzb_ref[...] = (fscale * dz).astype(BF)

        @pl.when(i == 0)
        def _():
            dg_ref[...] = jnp.zeros_like(dg_ref)
            db_ref[...] = jnp.zeros_like(db_ref)

        dg_ref[...] += jnp.sum(dyv * xhv, axis=0, keepdims=True)
        db_ref[...] += jnp.sum(dyv, axis=0, keepdims=True)

    row = pl.BlockSpec((ts, d), lambda i: (i, 0))
    vec = pl.BlockSpec((1, d), lambda i: (0, 0))
    return pl.pallas_call(
        body, name=name, grid=(s // ts,),
        in_specs=[row, row, pl.BlockSpec((ts, 1), lambda i: (i, 0)), vec],
        out_specs=[row, row, vec, vec],
        out_shape=[jax.ShapeDtypeStruct((s, d), F32), jax.ShapeDtypeStruct((s, d), BF),
                   jax.ShapeDtypeStruct((1, d), F32), jax.ShapeDtypeStruct((1, d), F32)],
        compiler_params=_cp("arbitrary"))(dy, xh, rstd, gain)


def _ffn_up(xb, wgu, name):
    s, d = xb.shape
    c = wgu.shape[2]
    nch = wgu.shape[0] // 2
    ts = _rows(s)
    w4 = wgu.reshape(2, nch, d, c)

    def body(x_ref, w_ref, gu_ref, a_ref):
        x = x_ref[...]
        g = _dot(x, w_ref[0])
        u = _dot(x, w_ref[1])
        gu_ref[0] = g.astype(BF)
        gu_ref[1] = u.astype(BF)
        a_ref[...] = (g * _sigmoid(g) * u).astype(BF)

    return pl.pallas_call(
        body, name=name, grid=(nch, s // ts),
        in_specs=[pl.BlockSpec((ts, d), lambda j, i: (i, 0)),
                  pl.BlockSpec((2, None, d, c), lambda j, i: (0, j, 0, 0))],
        out_specs=[pl.BlockSpec((2, None, ts, c), lambda j, i: (0, j, i, 0)),
                   pl.BlockSpec((None, ts, c), lambda j, i: (j, i, 0))],
        out_shape=[jax.ShapeDtypeStruct((2, nch, s, c), BF), jax.ShapeDtypeStruct((nch, s, c), BF)],
        compiler_params=_cp("parallel", "parallel"))(xb, w4)


def _ffn_bwd_act(dzb, wd4, gu, name):
    s, d = dzb.shape
    nch, c = wd4.shape[0], wd4.shape[1]
    ts = _rows(s)

    def body(dz_ref, w_ref, gu_ref, dh_ref):
        da = _dot_nt(dz_ref[...], w_ref[...])
        g = gu_ref[0].astype(F32)
        u = gu_ref[1].astype(F32)
        sg = _sigmoid(g)
        dh_ref[0] = (da * u * sg * (1.0 + g * (1.0 - sg))).astype(BF)
        dh_ref[1] = (da * g * sg).astype(BF)

    return pl.pallas_call(
        body, name=name, grid=(nch, s // ts),
        in_specs=[pl.BlockSpec((ts, d), lambda j, i: (i, 0)),
                  pl.BlockSpec((None, c, d), lambda j, i: (j, 0, 0)),
                  pl.BlockSpec((2, None, ts, c), lambda j, i: (0, j, i, 0))],
        out_specs=pl.BlockSpec((2, None, ts, c), lambda j, i: (0, j, i, 0)),
        out_shape=jax.ShapeDtypeStruct((2, nch, s, c), BF),
        compiler_params=_cp("parallel", "parallel"))(dzb, wd4, gu)


def _rope_tables(s, sign):
    pos = jnp.arange(s, dtype=F32)
    inv_freq = 1.0 / (ROPE_THETA ** (jnp.arange(ROT_HALF, dtype=F32) / ROT_HALF))
    ang = pos[:, None] * inv_freq[None, :]
    cos, sin = jnp.cos(ang), jnp.sin(ang) * sign
    one = jnp.ones((s, HEAD_DIM - 2 * ROT_HALF), F32)
    zero = jnp.zeros((s, HEAD_DIM - 2 * ROT_HALF), F32)
    zh = jnp.zeros((s, ROT_HALF), F32)
    cos_f = jnp.concatenate([cos, cos, one], axis=1)
    sin_a = jnp.concatenate([-sin, zh, zero], axis=1)
    sin_b = jnp.concatenate([zh, sin, zero], axis=1)
    rep = LANES // HEAD_DIM
    return tuple(jnp.tile(t, (1, rep)) for t in (cos_f, sin_a, sin_b))


def _rope_cast(h, tabs, n_rope, name):
    s, n = h.shape
    ts = _rows(s, 1024)

    def body(h_ref, c_ref, sa_ref, sb_ref, o_ref):
        cb = pl.program_id(1)
        t = h_ref[...]

        @pl.when(cb < n_rope)
        def _():
            r = (t * c_ref[...] + pltpu.roll(t, LANES - ROT_HALF, 1) * sa_ref[...]
                 + pltpu.roll(t, ROT_HALF, 1) * sb_ref[...])
            o_ref[...] = r.astype(BF)

        @pl.when(cb >= n_rope)
        def _():
            o_ref[...] = t.astype(BF)

    blk = pl.BlockSpec((ts, LANES), lambda i, j: (i, j))
    tab = pl.BlockSpec((ts, LANES), lambda i, j: (i, 0))
    return pl.pallas_call(
        body, name=name, grid=(s // ts, n // LANES),
        in_specs=[blk, tab, tab, tab], out_specs=blk,
        out_shape=jax.ShapeDtypeStruct((s, n), BF),
        compiler_params=_cp("parallel", "parallel"))(h, *tabs)


def _head_masks():
    lane = lax.broadcasted_iota(jnp.int32, (1, LANES), 1)
    return [lane < HEAD_DIM, lane >= HEAD_DIM]


def _sel(mask, v):
    return jnp.where(mask, v, jnp.zeros_like(v))


def _pick(mask, wide, fill):
    return jnp.max(jnp.where(mask, wide, fill), axis=1, keepdims=True)


def _band_masks(has_other, prev):
    qi = lax.broadcasted_iota(jnp.int32, (BLOCK, BLOCK), 0)
    kj = lax.broadcasted_iota(jnp.int32, (BLOCK, BLOCK), 1)
    if prev:
        return kj >= qi + jnp.where(has_other, 0, BLOCK)
    return kj <= qi


def _band_fwd(q3, k3, v3, name):
    ng, s, w = q3.shape
    nb = s // BLOCK
    npair = w // LANES

    def body(q_ref, kc_ref, kp_ref, vc_ref, vp_ref, o_ref, l_ref):
        g = pl.program_id(0)
        b = pl.program_id(2)
        nbl = jnp.right_shift(nb, 2 * g)
        has_prev = jnp.bitwise_and(b, nbl - 1) != 0
        mc = _band_masks(None, False)
        mp = _band_masks(has_prev, True)
        q, kc, kp, vc, vp = q_ref[...], kc_ref[...], kp_ref[...], vc_ref[...], vp_ref[...]
        hm = _head_masks()
        o = jnp.zeros((BLOCK, LANES), F32)
        lse_w = jnp.zeros((BLOCK, LANES), F32)
        for h in range(2):
            qh = _sel(hm[h], q)
            sc = jnp.where(mc, _dot_nt(qh, kc) * SCALE, NEG)
            sp = jnp.where(mp, _dot_nt(qh, kp) * SCALE, NEG)
            m = jnp.maximum(jnp.max(sc, axis=1, keepdims=True), jnp.max(sp, axis=1, keepdims=True))
            pc = jnp.exp(sc - m)
            pp = jnp.exp(sp - m)
            l = jnp.sum(pc, axis=1, keepdims=True) + jnp.sum(pp, axis=1, keepdims=True)
            oh = _dot(pc.astype(BF), _sel(hm[h], vc)) + _dot(pp.astype(BF), _sel(hm[h], vp))
            o = o + oh / l
            lse_w = jnp.where(hm[h], m + jnp.log(l), lse_w)
        o_ref[...] = o
        l_ref[...] = lse_w

    cur = pl.BlockSpec((None, BLOCK, LANES), lambda g, p, b: (g, b, p))
    prv = pl.BlockSpec((None, BLOCK, LANES), lambda g, p, b: (g, jnp.maximum(b - 1, 0), p))
    return pl.pallas_call(
        body, name=name, grid=(ng, npair, nb),
        in_specs=[cur, cur, prv, cur, prv], out_specs=[cur, cur],
        out_shape=[jax.ShapeDtypeStruct((ng, s, w), F32), jax.ShapeDtypeStruct((ng, s, w), F32)],
        compiler_params=_cp("parallel", "parallel", "parallel"))(q3, k3, k3, v3, v3)


def _band_combine(o3, l3, name):
    ng, s, w = o3.shape
    ts = _rows(s)

    def body(o_ref, l_ref, oa_ref, lt_ref):
        ls = [l_ref[g] for g in range(ng)]
        m = functools.reduce(jnp.maximum, ls)
        es = [jnp.exp(l - m) for l in ls]
        den = functools.reduce(lambda a, b: a + b, es)
        num = functools.reduce(lambda a, b: a + b, [es[g] * o_ref[g] for g in range(ng)])
        oa_ref[...] = (num / den).astype(BF)
        lt_ref[...] = m + jnp.log(den)

    blk3 = pl.BlockSpec((ng, ts, w), lambda i: (0, i, 0))
    blk = pl.BlockSpec((ts, w), lambda i: (i, 0))
    return pl.pallas_call(
        body, name=name, grid=(s // ts,), in_specs=[blk3, blk3], out_specs=[blk, blk],
        out_shape=[jax.ShapeDtypeStruct((s, w), BF), jax.ShapeDtypeStruct((s, w), F32)],
        compiler_params=_cp("parallel"))(o3, l3)


def _band_bwd(q3, k3, v3, do3, oa3, lt3, name):
    ng, s, w = q3.shape
    nb = s // BLOCK
    npair = w // LANES

    def body(q_ref, qn_ref, kc_ref, kp_ref, vc_ref, vp_ref, do_ref, don_ref, oa_ref, oan_ref, lt_ref, ltn_ref,
             dq_ref, dk_ref, dv_ref):
        g = pl.program_id(0)
        b = pl.program_id(2)
        nbl = jnp.right_shift(nb, 2 * g)
        has_prev = jnp.bitwise_and(b, nbl - 1) != 0
        has_next = jnp.bitwise_and(b + 1, nbl - 1) != 0
        mc = _band_masks(None, False)
        mp = _band_masks(has_prev, True)
        mn = _band_masks(has_next, True)
        q, qn, kc, kp, vc, vp = q_ref[...], qn_ref[...], kc_ref[...], kp_ref[...], vc_ref[...], vp_ref[...]
        do, don = do_ref[...], don_ref[...]
        dd = do.astype(F32) * oa_ref[...].astype(F32)
        ddn = don.astype(F32) * oan_ref[...].astype(F32)
        lt, ltn = lt_ref[...], ltn_ref[...]
        hm = _head_masks()
        dq = jnp.zeros((BLOCK, LANES), F32)
        dk = jnp.zeros((BLOCK, LANES), F32)
        dv = jnp.zeros((BLOCK, LANES), F32)
        for h in range(2):
            qh, doh = _sel(hm[h], q), _sel(hm[h], do)
            qnh, donh = _sel(hm[h], qn), _sel(hm[h], don)
            kch, kph = _sel(hm[h], kc), _sel(hm[h], kp)
            lse = _pick(hm[h], lt, NEG)
            lsen = _pick(hm[h], ltn, NEG)
            dsum = jnp.sum(_sel(hm[h], dd), axis=1, keepdims=True)
            dsumn = jnp.sum(_sel(hm[h], ddn), axis=1, keepdims=True)
            pc = jnp.exp(jnp.where(mc, _dot_nt(qh, kc) * SCALE, NEG) - lse)
            pp = jnp.exp(jnp.where(mp, _dot_nt(qh, kp) * SCALE, NEG) - lse)
            dsc = pc * (_dot_nt(doh, vc) - dsum)
            dsp = pp * (_dot_nt(doh, vp) - dsum)
            dq = dq + SCALE * (_dot(dsc.astype(BF), kch) + _dot(dsp.astype(BF), kph))
            pn = jnp.exp(jnp.where(mn, _dot_nt(qnh, kc) * SCALE, NEG) - lsen)
            dsn = pn * (_dot_nt(donh, vc) - dsumn)
            dk = dk + SCALE * (_dot_tn(dsc.astype(BF), qh) + _dot_tn(dsn.astype(BF), qnh))
            dv = dv + _dot_tn(pc.astype(BF), doh) + _dot_tn(pn.astype(BF), donh)
        dq_ref[...] = dq
        dk_ref[...] = dk
        dv_ref[...] = dv

    cur = pl.BlockSpec((None, BLOCK, LANES), lambda g, p, b: (g, b, p))
    prv = pl.BlockSpec((None, BLOCK, LANES), lambda g, p, b: (g, jnp.maximum(b - 1, 0), p))
    nxt = pl.BlockSpec((None, BLOCK, LANES), lambda g, p, b: (g, jnp.minimum(b + 1, nb - 1), p))
    out = jax.ShapeDtypeStruct((ng, s, w), F32)
    return pl.pallas_call(
        body, name=name, grid=(ng, npair, nb),
        in_specs=[cur, nxt, cur, prv, cur, prv, cur, nxt, cur, nxt, cur, nxt],
        out_specs=[cur, cur, cur], out_shape=[out, out, out],
        compiler_params=_cp("parallel", "parallel", "parallel"))(
            q3, q3, k3, k3, v3, v3, do3, do3, oa3, oa3, lt3, lt3)


def _mem_fwd(hb, q_blk0, kv, name):
    s = hb.shape[0]
    m = kv.shape[0]
    tq = _rows(s)
    npair = MEM_W // LANES

    def body(q_ref, k_ref, v_ref, o_ref, l_ref):
        q, k, v = q_ref[...], k_ref[...], v_ref[...]
        hm = _head_masks()
        o = jnp.zeros((tq, LANES), F32)
        lse_w = jnp.zeros((tq, LANES), F32)
        for h in range(2):
            sc = _dot_nt(_sel(hm[h], q), k) * SCALE
            mx = jnp.max(sc, axis=1, keepdims=True)
            p = jnp.exp(sc - mx)
            l = jnp.sum(p, axis=1, keepdims=True)
            o = o + _dot(p.astype(BF), _sel(hm[h], v)) / l
            lse_w = jnp.where(hm[h], mx + jnp.log(l), lse_w)
        o_ref[...] = o.astype(BF)
        l_ref[...] = lse_w

    blk = pl.BlockSpec((tq, LANES), lambda p, i: (i, p))
    return pl.pallas_call(
        body, name=name, grid=(npair, s // tq),
        in_specs=[pl.BlockSpec((tq, LANES), lambda p, i: (i, q_blk0 + p)),
                  pl.BlockSpec((m, LANES), lambda p, i: (0, p)),
                  pl.BlockSpec((m, LANES), lambda p, i: (0, npair + p))],
        out_specs=[blk, blk],
        out_shape=[jax.ShapeDtypeStruct((s, MEM_W), BF), jax.ShapeDtypeStruct((s, MEM_W), F32)],
        compiler_params=_cp("parallel", "parallel"))(hb, kv, kv)


def _mem_bwd(hb, q_blk0, kv, dcat, cat, o_blk0, lse, name):
    s = hb.shape[0]
    m = kv.shape[0]
    tq = _rows(s)
    npair = MEM_W // LANES

    def body(q_ref, k_ref, v_ref, do_ref, o_ref, l_ref, dq_ref, dk_ref, dv_ref):
        i = pl.program_id(1)

        @pl.when(i == 0)
        def _():
            dk_ref[...] = jnp.zeros_like(dk_ref)
            dv_ref[...] = jnp.zeros_like(dv_ref)

        q, k, v, do = q_ref[...], k_ref[...], v_ref[...], do_ref[...]
        dd = do.astype(F32) * o_ref[...].astype(F32)
        lt = l_ref[...]
        hm = _head_masks()
        dq = jnp.zeros((tq, LANES), F32)
        dk = jnp.zeros((m, LANES), F32)
        dv = jnp.zeros((m, LANES), F32)
        for h in range(2):
            qh, doh = _sel(hm[h], q), _sel(hm[h], do)
            p = jnp.exp(_dot_nt(qh, k) * SCALE - _pick(hm[h], lt, NEG))
            ds = p * (_dot_nt(doh, v) - jnp.sum(_sel(hm[h], dd), axis=1, keepdims=True))
            dq = dq + SCALE * _dot(ds.astype(BF), _sel(hm[h], k))
            dk = dk + SCALE * _dot_tn(ds.astype(BF), qh)
            dv = dv + _dot_tn(p.astype(BF), doh)
        dq_ref[...] = dq
        dk_ref[...] += dk
        dv_ref[...] += dv

    row = pl.BlockSpec((tq, LANES), lambda p, i: (i, p))
    orow = pl.BlockSpec((tq, LANES), lambda p, i: (i, o_blk0 + p))
    acc = pl.BlockSpec((m, LANES), lambda p, i: (0, p))
    return pl.pallas_call(
        body, name=name, grid=(npair, s // tq),
        in_specs=[pl.BlockSpec((tq, LANES), lambda p, i: (i, q_blk0 + p)),
                  pl.BlockSpec((m, LANES), lambda p, i: (0, p)),
                  pl.BlockSpec((m, LANES), lambda p, i: (0, npair + p)), orow, orow, row],
        out_specs=[row, acc, acc],
        out_shape=[jax.ShapeDtypeStruct((s, MEM_W), F32), jax.ShapeDtypeStruct((m, MEM_W), F32),
                   jax.ShapeDtypeStruct((m, MEM_W), F32)],
        compiler_params=_cp("parallel", "arbitrary"))(hb, kv, kv, dcat, cat, lse)


def _gate_fwd(f_t, bias, name):
    hp, s = f_t.shape
    nblk = s // LANES

    def body(f_ref, b_ref, c_ref):
        lane = lax.broadcasted_iota(jnp.int32, (hp, LANES), 1)

        def step(i, carry):
            off = pl.multiple_of(i * LANES, LANES)
            x = f_ref[:, pl.ds(off, LANES)] + b_ref[...]
            acc = jnp.minimum(x, 0.0) - jnp.log(1.0 + jnp.exp(-jnp.abs(x)))
            sh = 1
            while sh < LANES:
                acc = acc + jnp.where(lane >= sh, pltpu.roll(acc, sh, 1), 0.0)
                sh *= 2
            acc = acc + carry
            c_ref[:, pl.ds(off, LANES)] = acc
            return acc[:, LANES - 1:LANES]

        lax.fori_loop(0, nblk, step, jnp.zeros((hp, 1), F32))

    vm = pl.BlockSpec(memory_space=pltpu.VMEM)
    return pl.pallas_call(body, name=name, in_specs=[vm, vm], out_specs=vm,
                          out_shape=jax.ShapeDtypeStruct((hp, s), F32),
                          compiler_params=pltpu.CompilerParams(vmem_limit_bytes=VMEM_LIMIT))(f_t, bias)


def _gate_bwd(dc_t, f_t, bias, name):
    hp, s = f_t.shape
    nblk = s // LANES

    def body(dc_ref, f_ref, b_ref, df_ref, db_ref):
        lane = lax.broadcasted_iota(jnp.int32, (hp, LANES), 1)

        def step(t, carry):
            suffix, dbias = carry
            off = pl.multiple_of((nblk - 1 - t) * LANES, LANES)
            acc = dc_ref[:, pl.ds(off, LANES)]
            sh = 1
            while sh < LANES:
                acc = acc + jnp.where(lane < LANES - sh, pltpu.roll(acc, LANES - sh, 1), 0.0)
                sh *= 2
            acc = acc + suffix
            x = f_ref[:, pl.ds(off, LANES)] + b_ref[...]
            df = acc * _sigmoid(-x)
            df_ref[:, pl.ds(off, LANES)] = df
            return acc[:, 0:1], dbias + jnp.sum(df, axis=1, keepdims=True)

        _, dbias = lax.fori_loop(0, nblk, step, (jnp.zeros((hp, 1), F32), jnp.zeros((hp, 1), F32)))
        db_ref[...] = dbias

    vm = pl.BlockSpec(memory_space=pltpu.VMEM)
    return pl.pallas_call(body, name=name, in_specs=[vm, vm, vm], out_specs=[vm, vm],
                          out_shape=[jax.ShapeDtypeStruct((hp, s), F32), jax.ShapeDtypeStruct((hp, 1), F32)],
                          compiler_params=pltpu.CompilerParams(vmem_limit_bytes=VMEM_LIMIT))(dc_t, f_t, bias)


def _causal(qi, kj, tq, tk):
    qpos = qi * tq + lax.broadcasted_iota(jnp.int32, (tq, tk), 0)
    kpos = kj * tk + lax.broadcasted_iota(jnp.int32, (tq, tk), 1)
    return kpos <= qpos


def _fox_fwd(hb, c_pad, c_t3, name):
    s = hb.shape[0]
    npair = MIX_W // LANES
    tq = tk = _rows(s)
    nq = s // tq

    def body(q_ref, k_ref, v_ref, cq_ref, ck_ref, o_ref, l_ref, m_s, l_s, cq_s, acc):
        p = pl.program_id(0)
        qi = pl.program_id(1)
        kj = pl.program_id(2)
        hm = _head_masks()

        @pl.when(kj == 0)
        def _():
            lane = lax.broadcasted_iota(jnp.int32, (1, LANES), 1)
            cq = cq_ref[...]
            for h in range(2):
                m_s[h] = jnp.full((tq, 1), NEG, F32)
                l_s[h] = jnp.zeros((tq, 1), F32)
                cq_s[h] = jnp.sum(jnp.where(lane == 2 * p + h, cq, 0.0), axis=1, keepdims=True)
            acc[...] = jnp.zeros_like(acc)

        @pl.when(kj <= qi)
        def _():
            q, k, v = q_ref[...], k_ref[...], v_ref[...]
            ck = ck_ref[...]
            mask = _causal(qi, kj, tq, tk)
            a = acc[...]
            for h in range(2):
                sc = _dot_nt(_sel(hm[h], q), k) * SCALE + (cq_s[h] - ck[h:h + 1, :])
                sc = jnp.where(mask, sc, NEG)
                m_old = m_s[h]
                m_new = jnp.maximum(m_old, jnp.max(sc, axis=1, keepdims=True))
                corr = jnp.exp(m_old - m_new)
                pr = jnp.exp(sc - m_new)
                l_s[h] = corr * l_s[h] + jnp.sum(pr, axis=1, keepdims=True)
                m_s[h] = m_new
                a = a * jnp.where(hm[h], corr, 1.0) + _dot(pr.astype(BF), _sel(hm[h], v))
            acc[...] = a

        @pl.when(kj == qi)
        def _():
            l_w = jnp.where(hm[0], l_s[0], l_s[1])
            m_w = jnp.where(hm[0], m_s[0], m_s[1])
            o_ref[...] = (acc[...] / l_w).astype(BF)
            l_ref[...] = m_w + jnp.log(l_w)

    def kv_map(off):
        return lambda p, i, j: (jnp.minimum(j, i), off + p)

    blk = pl.BlockSpec((tq, LANES), lambda p, i, j: (i, p))
    return pl.pallas_call(
        body, name=name, grid=(npair, nq, nq),
        in_specs=[blk, pl.BlockSpec((tk, LANES), kv_map(npair)), pl.BlockSpec((tk, LANES), kv_map(2 * npair)),
                  pl.BlockSpec((tq, LANES), lambda p, i, j: (i, 0)),
                  pl.BlockSpec((None, 2, tk), lambda p, i, j: (p, 0, jnp.minimum(j, i)))],
        out_specs=[blk, blk],
        out_shape=[jax.ShapeDtypeStruct((s, MIX_W), BF), jax.ShapeDtypeStruct((s, MIX_W), F32)],
        scratch_shapes=[pltpu.VMEM((2, tq, 1), F32), pltpu.VMEM((2, tq, 1), F32), pltpu.VMEM((2, tq, 1), F32),
                        pltpu.VMEM((tq, LANES), F32)],
        compiler_params=_cp("parallel", "parallel", "arbitrary"))(hb, hb, hb, c_pad, c_t3)


def _fox_dsum(hb, dcat, lse, c_pad, c_t3, name):
    s = hb.shape[0]
    npair = MIX_W // LANES
    tq = tk = _rows(s)
    nq = s // tq

    def body(q_ref, k_ref, v_ref, do_ref, l_ref, cq_ref, ck_ref, d_ref, acc):
        p = pl.program_id(0)
        qi = pl.program_id(1)
        kj = pl.program_id(2)
        hm = _head_masks()

        @pl.when(kj == 0)
        def _():
            acc[...] = jnp.zeros_like(acc)

        @pl.when(kj <= qi)
        def _():
            q, k, v, do = q_ref[...], k_ref[...], v_ref[...], do_ref[...]
            lt = l_ref[...]
            cq = cq_ref[...]
            ck = ck_ref[...]
            lane = lax.broadcasted_iota(jnp.int32, (1, LANES), 1)
            mask = _causal(qi, kj, tq, tk)
            for h in range(2):
                cqh = jnp.sum(jnp.where(lane == 2 * p + h, cq, 0.0), axis=1, keepdims=True)
                sc = _dot_nt(_sel(hm[h], q), k) * SCALE + (cqh - ck[h:h + 1, :])
                pr = jnp.exp(jnp.where(mask, sc, NEG) - _pick(hm[h], lt, NEG))
                acc[h] += jnp.sum(pr * _dot_nt(_sel(hm[h], do), v), axis=1, keepdims=True)

        @pl.when(kj == qi)
        def _():
            d_ref[...] = jnp.where(hm[0], acc[0], acc[1])

    def kv_map(off):
        return lambda p, i, j: (jnp.minimum(j, i), off + p)

    blk = pl.BlockSpec((tq, LANES), lambda p, i, j: (i, p))
    return pl.pallas_call(
        body, name=name, grid=(npair, nq, nq),
        in_specs=[blk, pl.BlockSpec((tk, LANES), kv_map(npair)), pl.BlockSpec((tk, LANES), kv_map(2 * npair)),
                  blk, blk, pl.BlockSpec((tq, LANES), lambda p, i, j: (i, 0)),
                  pl.BlockSpec((None, 2, tk), lambda p, i, j: (p, 0, jnp.minimum(j, i)))],
        out_specs=blk, out_shape=jax.ShapeDtypeStruct((s, MIX_W), F32),
        scratch_shapes=[pltpu.VMEM((2, tq, 1), F32)],
        compiler_params=_cp("parallel", "parallel", "arbitrary"))(hb, hb, hb, dcat, lse, c_pad, c_t3)


def _fox_bwd(hb, dcat, dsum, lse, c_pad, c_t3, name):
    s = hb.shape[0]
    npair = MIX_W // LANES
    tq = tk = _rows(s)
    nq = s // tq

    def body(q_ref, k_ref, v_ref, do_ref, d_ref, l_ref, cq_ref, ck_ref, dq_ref, dk_ref, dv_ref, dc_ref):
        p = pl.program_id(0)
        kj = pl.program_id(1)
        qi = pl.program_id(2)
        hm = _head_masks()

        @pl.when(qi == 0)
        def _():
            dk_ref[...] = jnp.zeros_like(dk_ref)
            dv_ref[...] = jnp.zeros_like(dv_ref)
            dc_ref[...] = jnp.zeros_like(dc_ref)

        @pl.when((qi == 0) & (kj == 0))
        def _():
            dq_ref[...] = jnp.zeros_like(dq_ref)

        @pl.when(qi >= kj)
        def _():
            q, k, v, do = q_ref[...], k_ref[...], v_ref[...], do_ref[...]
            dw = d_ref[...]
            lt = l_ref[...]
            cq = cq_ref[...]
            ck = ck_ref[...]
            lane = lax.broadcasted_iota(jnp.int32, (1, LANES), 1)
            mask = _causal(qi, kj, tq, tk)
            dq = jnp.zeros((tq, LANES), F32)
            dk = jnp.zeros((tk, LANES), F32)
            dv = jnp.zeros((tk, LANES), F32)
            dcs = []
            for h in range(2):
                qh, doh = _sel(hm[h], q), _sel(hm[h], do)
                cqh = jnp.sum(jnp.where(lane == 2 * p + h, cq, 0.0), axis=1, keepdims=True)
                sc = _dot_nt(qh, k) * SCALE + (cqh - ck[h:h + 1, :])
                pr = jnp.exp(jnp.where(mask, sc, NEG) - _pick(hm[h], lt, NEG))
                ds = pr * (_dot_nt(doh, v) - _pick(hm[h], dw, NEG))
                dsb = ds.astype(BF)
                dq = dq + SCALE * _dot(dsb, _sel(hm[h], k))
                dk = dk + SCALE * _dot_tn(dsb, qh)
                dv = dv + _dot_tn(pr.astype(BF), doh)
                dcs.append(jnp.sum(ds, axis=0, keepdims=True))
            rows = pl.ds(pl.multiple_of(qi * tq, tq), tq)
            dq_ref[rows, :] += dq
            dk_ref[...] += dk
            dv_ref[...] += dv
            dc_ref[...] -= jnp.concatenate(dcs, axis=0)

    def q_map(off):
        return lambda p, j, i: (jnp.maximum(i, j), off + p)

    kblk = pl.BlockSpec((tk, LANES), lambda p, j, i: (j, p))
    return pl.pallas_call(
        body, name=name, grid=(npair, nq, nq),
        in_specs=[pl.BlockSpec((tq, LANES), q_map(0)),
                  pl.BlockSpec((tk, LANES), lambda p, j, i: (j, npair + p)),
                  pl.BlockSpec((tk, LANES), lambda p, j, i: (j, 2 * npair + p)),
                  pl.BlockSpec((tq, LANES), q_map(0)), pl.BlockSpec((tq, LANES), q_map(0)),
                  pl.BlockSpec((tq, LANES), q_map(0)),
                  pl.BlockSpec((tq, LANES), lambda p, j, i: (jnp.maximum(i, j), 0)),
                  pl.BlockSpec((None, 2, tk), lambda p, j, i: (p, 0, j))],
        out_specs=[pl.BlockSpec((s, LANES), lambda p, j, i: (0, p)), kblk, kblk,
                   pl.BlockSpec((None, 2, tk), lambda p, j, i: (p, 0, j))],
        out_shape=[jax.ShapeDtypeStruct((s, MIX_W), F32), jax.ShapeDtypeStruct((s, MIX_W), F32),
                   jax.ShapeDtypeStruct((s, MIX_W), F32), jax.ShapeDtypeStruct((npair, 2, s), F32)],
        compiler_params=_cp("arbitrary", "arbitrary", "arbitrary"))(hb, hb, hb, dcat, dsum, lse, c_pad, c_t3)


def _loss_head(y, target, name):
    s, d = y.shape
    ts = _rows(s)

    def body(y_ref, t_ref, dy_ref, l_ref):
        i = pl.program_id(0)
        e = y_ref[...] - t_ref[...]
        dy_ref[...] = e * (1.0 / d)

        @pl.when(i == 0)
        def _():
            l_ref[...] = jnp.zeros_like(l_ref)

        part = jnp.sum(jnp.sum(e * e, axis=1, keepdims=True), axis=0, keepdims=True)
        l_ref[...] += part * (0.5 / d)

    row = pl.BlockSpec((ts, d), lambda i: (i, 0))
    return pl.pallas_call(
        body, name=name, grid=(s // ts,), in_specs=[row, row],
        out_specs=[row, pl.BlockSpec((1, 1), lambda i: (0, 0))],
        out_shape=[jax.ShapeDtypeStruct((s, d), F32), jax.ShapeDtypeStruct((1, 1), F32)],
        compiler_params=_cp("arbitrary"))(y, target)


def _adam_rows(r, c):
    cap = max(8, (1 << 20) // (4 * c))
    if r <= cap:
        return r
    best = None
    for t in range(8, cap + 1, 8):
        if r % t == 0:
            best = t
    return best if best is not None else r


def _reduce_adamw(contrib, w, m, v, name):
    nl, nd, r, c = contrib.shape
    tr = _adam_rows(r, c)
    bc1 = 1.0 - ADAM_B1 ** ADAM_STEP
    bc2 = 1.0 - ADAM_B2 ** ADAM_STEP

    def body(c_ref, w_ref, m_ref, v_ref, g_ref, d_ref, nm_ref, nv_ref):
        g = c_ref[0].astype(F32)
        for k in range(1, nd):
            g = g + c_ref[k].astype(F32)
        nm = ADAM_B1 * m_ref[...] + (1.0 - ADAM_B1) * g
        nv = ADAM_B2 * v_ref[...] + (1.0 - ADAM_B2) * (g * g)
        g_ref[...] = g
        nm_ref[...] = nm
        nv_ref[...] = nv
        d_ref[...] = -ADAM_LR * ((nm / bc1) / (jnp.sqrt(nv / bc2) + ADAM_EPS) + ADAM_WD * w_ref[...])

    blk = pl.BlockSpec((None, tr, c), lambda l, i: (l, i, 0))
    out = jax.ShapeDtypeStruct((nl, r, c), F32)
    return pl.pallas_call(
        body, name=name, grid=(nl, r // tr),
        in_specs=[pl.BlockSpec((None, nd, tr, c), lambda l, i: (l, 0, i, 0)), blk, blk, blk],
        out_specs=[blk, blk, blk, blk], out_shape=[out, out, out, out],
        compiler_params=_cp("parallel", "parallel"))(contrib, w, m, v)


def _mesh_pos():
    return lax.axis_index("x"), lax.axis_index("y"), lax.axis_index("c")


def _peer(pos, k):
    x, y, c = pos
    return (1 - x if k & 4 else x, 1 - y if k & 2 else y, 1 - c if k & 1 else c)


def _linear(pos):
    return 4 * pos[0] + 2 * pos[1] + pos[2]


def _all_gather(shards, name):
    n = len(shards)

    def body(*refs):
        ins, outs = refs[:n], refs[n:2 * n]
        send_sems, recv_sems, local_sems = refs[2 * n:]
        pos = _mesh_pos()
        me = _linear(pos)
        local, remote = [], []
        for i in range(n):
            cp = pltpu.make_async_copy(ins[i], outs[i].at[me], local_sems.at[i])
            cp.start()
            local.append(cp)
            for k in range(1, N_DEV):
                rc = pltpu.make_async_remote_copy(
                    src_ref=ins[i], dst_ref=outs[i].at[me],
                    send_sem=send_sems.at[i * (N_DEV - 1) + k - 1], recv_sem=recv_sems.at[i * (N_DEV - 1) + k - 1],
                    device_id=_peer(pos, k), device_id_type=MESH_ID)
                rc.start()
                remote.append(rc)
        for cp in local:
            cp.wait()
        for rc in remote:
            rc.wait()

    hbm = pl.BlockSpec(memory_space=pltpu.HBM)
    return pl.pallas_call(
        body, name=name, in_specs=[hbm] * n, out_specs=[hbm] * n,
        out_shape=[jax.ShapeDtypeStruct((N_DEV,) + a.shape, a.dtype) for a in shards],
        scratch_shapes=[pltpu.SemaphoreType.DMA((n * (N_DEV - 1),)), pltpu.SemaphoreType.DMA((n * (N_DEV - 1),)),
                        pltpu.SemaphoreType.DMA((n,))],
        )(*shards)


def _all_to_all(pieces, name):
    flat = [(gi, li, a) for gi, grp in enumerate(pieces) for li, a in enumerate(grp)]
    n = len(flat)
    ng = len(pieces)

    def body(*refs):
        ins, outs = refs[:n], refs[n:n + ng]
        send_sems, recv_sems, local_sems = refs[n + ng:]
        pos = _mesh_pos()
        me = _linear(pos)
        local, remote = [], []
        for i, (gi, li, _) in enumerate(flat):
            cp = pltpu.make_async_copy(ins[i].at[me], outs[gi].at[li, me], local_sems.at[i])
            cp.start()
            local.append(cp)
            for k in range(1, N_DEV):
                peer = _peer(pos, k)
                rc = pltpu.make_async_remote_copy(
                    src_ref=ins[i].at[_linear(peer)], dst_ref=outs[gi].at[li, me],
                    send_sem=send_sems.at[i * (N_DEV - 1) + k - 1], recv_sem=recv_sems.at[i * (N_DEV - 1) + k - 1],
                    device_id=peer, device_id_type=MESH_ID)
                rc.start()
                remote.append(rc)
        for cp in local:
            cp.wait()
        for rc in remote:
            rc.wait()

    hbm = pl.BlockSpec(memory_space=pltpu.HBM)
    return pl.pallas_call(
        body, name=name, in_specs=[hbm] * n, out_specs=[hbm] * ng,
        out_shape=[jax.ShapeDtypeStruct((len(grp),) + grp[0].shape, grp[0].dtype) for grp in pieces],
        scratch_shapes=[pltpu.SemaphoreType.DMA((n * (N_DEV - 1),)), pltpu.SemaphoreType.DMA((n * (N_DEV - 1),)),
                        pltpu.SemaphoreType.DMA((n,))],
        )(*[a for _, _, a in flat])


def _cols_full(g):
    nd, r, c = g.shape
    return jnp.transpose(g, (1, 0, 2)).reshape(r, nd * c)


def _cols_split(full):
    r, n = full.shape
    return jnp.transpose(full.reshape(r, N_DEV, n // N_DEV), (1, 0, 2))


def _pack_b_in(w):
    qkv = 3 * MIX_W
    pad = jnp.zeros((w.shape[0], B_IN_PAD - w.shape[1]), w.dtype)
    return jnp.concatenate([w[:, :qkv], w[:, qkv + N_MIX_HEADS:], w[:, qkv:qkv + N_MIX_HEADS], pad], axis=1)


def _unpack_b_in(w):
    qkv = 3 * MIX_W
    return jnp.concatenate([w[:, :qkv], w[:, qkv + MEM_W:qkv + MEM_W + N_MIX_HEADS], w[:, qkv:qkv + MEM_W]], axis=1)


def _to_classes(t, g):
    r = 4 ** g
    s, w = t.shape
    return jnp.transpose(t.reshape(s // r, r, w), (1, 0, 2)).reshape(s, w)


def _from_classes(t, g):
    r = 4 ** g
    s, w = t.shape
    return jnp.transpose(t.reshape(r, s // r, w), (1, 0, 2)).reshape(s, w)


def _group_stack(t):
    return jnp.stack([_to_classes(t[:, g * GROUP_W:(g + 1) * GROUP_W], g) for g in range(N_GROUPS)])


def _group_unstack(t3):
    return jnp.concatenate([_from_classes(t3[g], g) for g in range(N_GROUPS)], axis=1)


def _same_stack(t):
    return jnp.stack([_to_classes(t, g) for g in range(N_GROUPS)])


def _same_unstack(t3):
    return jnp.stack([_from_classes(t3[g], g) for g in range(N_GROUPS)])


def _ffn_forward(x, xb, wgu, wd4, gain, bias, tag):
    gu, a = _ffn_up(xb, wgu, f"{tag}_up")
    y, yb, xh, rstd = _mm_res_ln(a, wd4, x, gain, bias, 0.5, f"{tag}_down_ln")
    return y, yb, (xb, gu, a, xh, rstd)


def _ffn_backward(dy, saved, wgu, wd4, gain, tag):
    xb, gu, a, xh, rstd = saved
    s = xb.shape[0]
    nd, d, c = wgu.shape
    dz, dzb, dgain, dbias = _ln_bwd(dy, xh, rstd, gain, 0.5, f"{tag}_ln_bwd")
    dh = _ffn_bwd_act(dzb, wd4, gu, f"{tag}_act_bwd").reshape(nd, s, c)
    dwd = _mm_tn(a, dzb[None], f"{tag}_dwd").reshape(nd, wd4.shape[1] // 2, d)
    dx = _mm_nt(dh, wgu, f"{tag}_dx", res=dz)
    dwgu = _mm_tn(xb[None], dh, f"{tag}_dwgu")
    return dx, dwgu, dwd, dgain, dbias


def _mixer_a_forward(x, xb, memb, w_in, w_kv, w_out, gain, bias, tabs):
    h = _mm_nn(xb, w_in, F32, "a_in")
    hb = _rope_cast(h, tabs, 2 * MIX_W // LANES, "a_rope")
    q3 = _group_stack(hb[:, :MIX_W])
    k3 = _group_stack(hb[:, MIX_W:2 * MIX_W])
    v3 = _group_stack(hb[:, 2 * MIX_W:3 * MIX_W])
    o3, l3 = _band_fwd(q3, k3, v3, "a_band_fwd")
    oa, lt = _band_combine(_same_unstack(o3), _same_unstack(l3), "a_combine")
    kv = _mm_nn(memb, w_kv, BF, "a_mem_kv")
    om, lm = _mem_fwd(hb, 3 * MIX_W // LANES, kv, "a_mem_fwd")
    cat = jnp.concatenate([oa, om], axis=1)
    y, yb, xh, rstd = _mm_res_ln(cat[None], w_out[None], x, gain, bias, 1.0, "a_out_ln")
    return y, yb, (xb, hb, q3, k3, v3, oa, lt, kv, lm, cat, xh, rstd)


def _mixer_a_backward(dy, saved, memb, w_in, w_kv, w_out, gain, tabs_neg):
    xb, hb, q3, k3, v3, oa, lt, kv, lm, cat, xh, rstd = saved
    dz, dzb, dgain, dbias = _ln_bwd(dy, xh, rstd, gain, 1.0, "a_ln_bwd")
    dcat = _mm_nt(dzb[None], w_out[None], "a_dcat", out_dtype=BF)
    dw_out = _mm_tn(cat[None], dzb[None], "a_dwout")[0]
    dqm, dkm, dvm = _mem_bwd(hb, 3 * MIX_W // LANES, kv, dcat, cat, GROUP_W // LANES, lm, "a_mem_bwd")
    dkv = jnp.concatenate([dkm, dvm], axis=1).astype(BF)
    dw_kv = _mm_tn(memb[None], dkv[None], "a_dwkv")[0]
    dq3, dk3, dv3 = _band_bwd(q3, k3, v3, _same_stack(dcat[:, :GROUP_W]), _same_stack(oa), _same_stack(lt),
                              "a_band_bwd")
    dh = jnp.concatenate([_group_unstack(dq3), _group_unstack(dk3), _group_unstack(dv3), dqm], axis=1)
    dhb = _rope_cast(dh, tabs_neg, 2 * MIX_W // LANES, "a_rope_bwd")
    dw_in = _mm_tn(xb[None], dhb[None], "a_dwin")[0]
    dx = _mm_nt(dhb[None], w_in[None], "a_dx", res=dz)
    return dx, dw_in, dw_kv, dw_out, dgain, dbias


def _pad_rows(t, rows):
    return jnp.concatenate([t, jnp.zeros((rows - t.shape[0], t.shape[1]), t.dtype)], axis=0)


def _pad_cols(t, cols):
    return jnp.concatenate([t, jnp.zeros((t.shape[0], cols - t.shape[1]), t.dtype)], axis=1)


def _mixer_b_forward(x, xb, memb, w_in, fbias, w_kv, w_out, gain, bias, tabs):
    s = x.shape[0]
    h = _mm_nn(xb, w_in, F32, "b_in")
    hb = _rope_cast(h, tabs, 0, "b_cast")
    f0 = 3 * MIX_W + MEM_W
    f_t = _pad_rows(jnp.transpose(h[:, f0:f0 + N_MIX_HEADS]), 16)
    bias16 = _pad_rows(jnp.transpose(fbias), 16)
    c_t = _gate_fwd(f_t, bias16, "b_gate_fwd")
    c_pad = _pad_cols(jnp.transpose(c_t[:N_MIX_HEADS]), LANES)
    c_t3 = c_t[:N_MIX_HEADS].reshape(N_MIX_HEADS // 2, 2, s)
    ob, lb = _fox_fwd(hb, c_pad, c_t3, "b_fox_fwd")
    kv = _mm_nn(memb, w_kv, BF, "b_mem_kv")
    om, lm = _mem_fwd(hb, 3 * MIX_W // LANES, kv, "b_mem_fwd")
    cat = jnp.concatenate([ob, om], axis=1)
    y, yb, xh, rstd = _mm_res_ln(cat[None], w_out[None], x, gain, bias, 1.0, "b_out_ln")
    return y, yb, (xb, hb, f_t, bias16, c_pad, c_t3, lb, kv, lm, cat, xh, rstd)


def _mixer_b_backward(dy, saved, memb, w_in, w_kv, w_out, gain, tabs):
    xb, hb, f_t, bias16, c_pad, c_t3, lb, kv, lm, cat, xh, rstd = saved
    s = xb.shape[0]
    dz, dzb, dgain, dbias = _ln_bwd(dy, xh, rstd, gain, 1.0, "b_ln_bwd")
    dcat = _mm_nt(dzb[None], w_out[None], "b_dcat", out_dtype=BF)
    dw_out = _mm_tn(cat[None], dzb[None], "b_dwout")[0]
    dqm, dkm, dvm = _mem_bwd(hb, 3 * MIX_W // LANES, kv, dcat, cat, MIX_W // LANES, lm, "b_mem_bwd")
    dkv = jnp.concatenate([dkm, dvm], axis=1).astype(BF)
    dw_kv = _mm_tn(memb[None], dkv[None], "b_dwkv")[0]
    dsum = _fox_dsum(hb, dcat, lb, c_pad, c_t3, "b_fox_dsum")
    dq, dk, dv, dc3 = _fox_bwd(hb, dcat, dsum, lb, c_pad, c_t3, "b_fox_bwd")
    df_t, dfb = _gate_bwd(_pad_rows(dc3.reshape(N_MIX_HEADS, s), 16), f_t, bias16, "b_gate_bwd")
    df = _pad_cols(jnp.transpose(df_t[:N_MIX_HEADS]), B_IN_PAD - 3 * MIX_W - MEM_W)
    dh = jnp.concatenate([dq, dk, dv, dqm, df], axis=1)
    dhb = _rope_cast(dh, tabs, 0, "b_cast_bwd")
    dw_in = _mm_tn(xb[None], dhb[None], "b_dwin")[0]
    dx = _mm_nt(dhb[None], w_in[None], "b_dx", res=dz)
    return dx, dw_in, jnp.transpose(dfb[:N_MIX_HEADS]), dw_kv, dw_out, dgain, dbias


def _local_step(x, mem, target, wts):
    s = x.shape[0]
    tabs = _rope_tables(s, 1.0)
    tabs_neg = _rope_tables(s, -1.0)
    memb = mem.astype(BF)
    ln_g, ln_b = wts["ln_gain"], wts["ln_bias"]
    saved = []
    cur, curb = x, x.astype(BF)
    for i in range(DEPTH):
        cur, curb, s1 = _ffn_forward(cur, curb, wts["ffn1_gu"][i], wts["ffn1_d"][i], ln_g[i, 0], ln_b[i, 0],
                                     f"l{i}_ffn1")
        if i == 0:
            cur, curb, s2 = _mixer_a_forward(cur, curb, memb, wts["a_in"], wts["mem_kv"][i], wts["a_out"],
                                             ln_g[i, 1], ln_b[i, 1], tabs)
        else:
            cur, curb, s2 = _mixer_b_forward(cur, curb, memb, wts["b_in"], wts["b_fbias"], wts["mem_kv"][i],
                                             wts["b_out"], ln_g[i, 1], ln_b[i, 1], tabs)
        cur, curb, s3 = _ffn_forward(cur, curb, wts["ffn2_gu"][i], wts["ffn2_d"][i], ln_g[i, 2], ln_b[i, 2],
                                     f"l{i}_ffn2")
        saved.append((s1, s2, s3))

    dy, loss = _loss_head(cur, target, "loss_head")

    grads = {"ffn1_gu": [None] * DEPTH, "ffn1_d": [None] * DEPTH, "ffn2_gu": [None] * DEPTH,
             "ffn2_d": [None] * DEPTH, "mem_kv": [None] * DEPTH,
             "ln_gain": [[None] * 3 for _ in range(DEPTH)], "ln_bias": [[None] * 3 for _ in range(DEPTH)]}
    for i in reversed(range(DEPTH)):
        s1, s2, s3 = saved[i]
        dy, grads["ffn2_gu"][i], grads["ffn2_d"][i], dg, db = _ffn_backward(
            dy, s3, wts["ffn2_gu"][i], wts["ffn2_d"][i], ln_g[i, 2], f"l{i}_ffn2")
        grads["ln_gain"][i][2], grads["ln_bias"][i][2] = dg, db
        if i == 0:
            dy, grads["a_in"], grads["mem_kv"][i], grads["a_out"], dg, db = _mixer_a_backward(
                dy, s2, memb, wts["a_in"], wts["mem_kv"][i], wts["a_out"], ln_g[i, 1], tabs_neg)
        else:
            dy, grads["b_in"], grads["b_fbias"], grads["mem_kv"][i], grads["b_out"], dg, db = _mixer_b_backward(
                dy, s2, memb, wts["b_in"], wts["mem_kv"][i], wts["b_out"], ln_g[i, 1], tabs)
        grads["ln_gain"][i][1], grads["ln_bias"][i][1] = dg, db
        dy, grads["ffn1_gu"][i], grads["ffn1_d"][i], dg, db = _ffn_backward(
            dy, s1, wts["ffn1_gu"][i], wts["ffn1_d"][i], ln_g[i, 0], f"l{i}_ffn1")
        grads["ln_gain"][i][0], grads["ln_bias"][i][0] = dg, db
    return loss, dy, grads


def _full_weights(g):
    d = g["ffn1_w_gate_up"].shape[2]
    wts = {}
    for k in ("ffn1", "ffn2"):
        gu, dn = g[f"{k}_w_gate_up"], g[f"{k}_w_down"]
        wts[f"{k}_gu"] = [gu[:, i] for i in range(DEPTH)]
        wts[f"{k}_d"] = [dn[:, i].reshape(N_DEV // 2, 2 * dn.shape[2], d) for i in range(DEPTH)]
    wts["mem_kv"] = [g["mem_w_kv"][:, i].reshape(d, -1) for i in range(DEPTH)]
    for k in ("ln_gain", "ln_bias"):
        t = jnp.transpose(g[k], (1, 2, 0, 3))
        wts[k] = t.reshape(DEPTH, 3, 1, d)
    wts["a_in"] = _cols_full(g["a_w_in"][:, 0])
    wts["a_out"] = _cols_full(g["a_w_out"][:, 0])
    wts["b_in"] = _pack_b_in(g["b_w_in"][:, 0].reshape(d, -1))
    wts["b_out"] = g["b_w_out"][:, 0].reshape(d, -1)
    return wts


def _grad_pieces(grads):
    d = grads["a_in"].shape[0]
    out = {}
    for k in ("ffn1", "ffn2"):
        out[f"{k}_w_gate_up"] = grads[f"{k}_gu"]
        out[f"{k}_w_down"] = grads[f"{k}_d"]
    out["mem_w_kv"] = [t.reshape(N_DEV, d // N_DEV, -1) for t in grads["mem_kv"]]
    for k in ("ln_gain", "ln_bias"):
        t = jnp.concatenate([grads[k][i][j] for i in range(DEPTH) for j in range(3)], axis=0)
        out[k] = [jnp.transpose(t.reshape(DEPTH * 3, N_DEV, d // N_DEV), (1, 0, 2))]
    out["a_w_in"] = [_cols_split(grads["a_in"])]
    out["a_w_out"] = [_cols_split(grads["a_out"])]
    out["b_w_in"] = [_unpack_b_in(grads["b_in"]).reshape(N_DEV, d // N_DEV, -1)]
    out["b_forget_bias"] = [jnp.broadcast_to(grads["b_fbias"][None], (N_DEV,) + grads["b_fbias"].shape)]
    out["b_w_out"] = [grads["b_out"].reshape(N_DEV, d // N_DEV, -1)]
    return out


WEIGHTS = ("ffn1_w_gate_up", "ffn1_w_down", "ffn2_w_gate_up", "ffn2_w_down", "ln_gain", "ln_bias", "mem_w_kv",
           "a_w_in", "a_w_out", "b_w_in", "b_forget_bias", "b_w_out")
F32_COMM = ("ln_gain", "ln_bias", "b_forget_bias")


def kernel(x, mem, ffn1_w_gate_up, ffn1_w_down, ffn2_w_gate_up, ffn2_w_down, ln_gain, ln_bias, mem_w_kv, a_w_in, a_w_out, b_w_in, b_forget_bias, b_w_out, loss_target, m_ffn1_w_gate_up, m_ffn1_w_down, m_ffn2_w_gate_up, m_ffn2_w_down, m_ln_gain, m_ln_bias, m_mem_w_kv, m_a_w_in, m_a_w_out, m_b_w_in, m_b_forget_bias, m_b_w_out, v_ffn1_w_gate_up, v_ffn1_w_down, v_ffn2_w_gate_up, v_ffn2_w_down, v_ln_gain, v_ln_bias, v_mem_w_kv, v_a_w_in, v_a_w_out, v_b_w_in, v_b_forget_bias, v_b_w_out):
    w = dict(zip(WEIGHTS, (ffn1_w_gate_up, ffn1_w_down, ffn2_w_gate_up, ffn2_w_down, ln_gain, ln_bias, mem_w_kv,
                           a_w_in, a_w_out, b_w_in, b_forget_bias, b_w_out)))
    m = dict(zip(WEIGHTS, (m_ffn1_w_gate_up, m_ffn1_w_down, m_ffn2_w_gate_up, m_ffn2_w_down, m_ln_gain, m_ln_bias,
                           m_mem_w_kv, m_a_w_in, m_a_w_out, m_b_w_in, m_b_forget_bias, m_b_w_out)))
    v = dict(zip(WEIGHTS, (v_ffn1_w_gate_up, v_ffn1_w_down, v_ffn2_w_gate_up, v_ffn2_w_down, v_ln_gain, v_ln_bias,
                           v_mem_w_kv, v_a_w_in, v_a_w_out, v_b_w_in, v_b_forget_bias, v_b_w_out)))

    gather_names = [n for n in WEIGHTS if n != "b_forget_bias"]
    shards = [w[n] if n in F32_COMM else w[n].astype(BF) for n in gather_names]
    gathered = dict(zip(gather_names, _all_gather(shards, "gather_weights")))
    wts = _full_weights(gathered)
    wts["b_fbias"] = b_forget_bias

    loss, grad_x, grads = _local_step(x[0], mem[0], loss_target[0], wts)
    loss = lax.psum(loss[0, 0], ("x", "y", "c"))

    pieces = _grad_pieces(grads)
    groups = [[p if n in F32_COMM else p.astype(BF) for p in pieces[n]] for n in WEIGHTS]
    received = dict(zip(WEIGHTS, _all_to_all(groups, "exchange_grads")))

    outs = {}
    for n in WEIGHTS:
        contrib = received[n]
        shape = w[n].shape
        nl, _, r, c = contrib.shape
        view = (nl, r, c)
        outs[n] = [t.reshape(shape) for t in _reduce_adamw(
            contrib, w[n].reshape(view), m[n].reshape(view), v[n].reshape(view), f"adamw_{n}")]
    return (loss, grad_x[None], *[outs[n][0] for n in WEIGHTS], *[outs[n][1] for n in WEIGHTS],
            *[outs[n][2] for n in WEIGHTS], *[outs[n][3] for n in WEIGHTS])
```

```python
import functools

import jax
import jax.numpy as jnp
from jax import lax
from jax.experimental import pallas as pl
from jax.experimental.pallas import tpu as pltpu

F32 = jnp.float32
BF = jnp.bfloat16
MESH_ID = pl.DeviceIdType.MESH

N_DEV = 8
DEPTH = 2
HEAD_DIM = 64
LANES = 128
N_MIX_HEADS = 12
N_MEM_HEADS = 4
MIX_W = N_MIX_HEADS * HEAD_DIM
MEM_W = N_MEM_HEADS * HEAD_DIM
N_GROUPS = 3
GROUP_W = MIX_W // N_GROUPS
BLOCK = 128
ROT_HALF = 8
ROPE_THETA = 500000.0
ALPHA = (2 * DEPTH) ** 0.25
LN_EPS = 1e-5
SCALE = HEAD_DIM ** -0.5
NEG = -1e30
B_IN_PAD = 2688
ADAM_LR, ADAM_B1, ADAM_B2, ADAM_EPS, ADAM_WD, ADAM_STEP = 0.001, 0.9, 0.999, 1e-08, 0.01, 10
VMEM_LIMIT = 56 * 1024 * 1024


def _cp(*sem):
    return pltpu.CompilerParams(dimension_semantics=sem, vmem_limit_bytes=VMEM_LIMIT)


def _dot(a, b):
    return jnp.dot(a, b, preferred_element_type=F32)


def _dot_nt(a, b):
    return lax.dot_general(a, b, (((1,), (1,)), ((), ())), preferred_element_type=F32)


def _dot_tn(a, b):
    return lax.dot_general(a, b, (((0,), (0,)), ((), ())), preferred_element_type=F32)


def _sigmoid(x):
    return 1.0 / (1.0 + jnp.exp(-x))


def _tile(n, cap=1024):
    if n <= cap:
        return n
    best = LANES
    for t in range(LANES, cap + 1, LANES):
        if n % t == 0:
            best = t
    return best


def _rows(s, cap=512):
    return s if s <= cap else cap


def _mm_nn(a, b, out_dtype, name):
    m, k = a.shape
    n = b.shape[1]
    tm, tn = _rows(m), _tile(n)

    def body(a_ref, b_ref, o_ref):
        o_ref[...] = _dot(a_ref[...], b_ref[...]).astype(o_ref.dtype)

    return pl.pallas_call(
        body, name=name, grid=(n // tn, m // tm),
        in_specs=[pl.BlockSpec((tm, k), lambda j, i: (i, 0)), pl.BlockSpec((k, tn), lambda j, i: (0, j))],
        out_specs=pl.BlockSpec((tm, tn), lambda j, i: (i, j)),
        out_shape=jax.ShapeDtypeStruct((m, n), out_dtype),
        compiler_params=_cp("parallel", "parallel"))(a, b)


def _mm_tn(a, b, name, out_dtype=BF):
    na, s, m = a.shape
    nb, _, n = b.shape
    no = max(na, nb)
    ts, tn = _rows(s), _tile(n)
    ns = s // ts

    def body(a_ref, b_ref, o_ref, acc):
        k = pl.program_id(2)

        @pl.when(k == 0)
        def _():
            acc[...] = jnp.zeros_like(acc)

        acc[...] += _dot_tn(a_ref[...], b_ref[...])

        @pl.when(k == ns - 1)
        def _():
            o_ref[...] = acc[...].astype(o_ref.dtype)

    return pl.pallas_call(
        body, name=name, grid=(no, n // tn, ns),
        in_specs=[pl.BlockSpec((None, ts, m), lambda j, c, k: (j if na > 1 else 0, k, 0)),
                  pl.BlockSpec((None, ts, tn), lambda j, c, k: (j if nb > 1 else 0, k, c))],
        out_specs=pl.BlockSpec((None, m, tn), lambda j, c, k: (j, 0, c)),
        out_shape=jax.ShapeDtypeStruct((no, m, n), out_dtype),
        scratch_shapes=[pltpu.VMEM((m, tn), F32)],
        compiler_params=_cp("parallel", "parallel", "arbitrary"))(a, b)


def _mm_nt(dh, w, name, res=None, out_dtype=F32):
    nc, s, kc = dh.shape
    d = w.shape[1]
    ts = _rows(s)
    has_res = res is not None

    def body(*refs):
        if has_res:
            dh_ref, w_ref, r_ref, o_ref, acc = refs
        else:
            dh_ref, w_ref, o_ref, acc = refs
        j = pl.program_id(1)

        @pl.when(j == 0)
        def _():
            acc[...] = jnp.zeros_like(acc)

        acc[...] += _dot_nt(dh_ref[...], w_ref[...])

        @pl.when(j == nc - 1)
        def _():
            out = acc[...]
            if has_res:
                out = out + ALPHA * r_ref[...]
            o_ref[...] = out.astype(o_ref.dtype)

    in_specs = [pl.BlockSpec((None, ts, kc), lambda i, j: (j, i, 0)),
                pl.BlockSpec((None, d, kc), lambda i, j: (j, 0, 0))]
    args = [dh, w]
    if has_res:
        in_specs.append(pl.BlockSpec((ts, d), lambda i, j: (i, 0)))
        args.append(res)
    return pl.pallas_call(
        body, name=name, grid=(s // ts, nc), in_specs=in_specs,
        out_specs=pl.BlockSpec((ts, d), lambda i, j: (i, 0)),
        out_shape=jax.ShapeDtypeStruct((s, d), out_dtype),
        scratch_shapes=[pltpu.VMEM((ts, d), F32)],
        compiler_params=_cp("parallel", "arbitrary"))(*args)


def _mm_res_ln(a, w, x, gain, bias, fscale, name):
    nc, s, kc = a.shape
    d = w.shape[2]
    ts = _rows(s)

    def body(a_ref, w_ref, x_ref, g_ref, b_ref, y_ref, yb_ref, xh_ref, r_ref, acc):
        j = pl.program_id(1)

        @pl.when(j == 0)
        def _():
            acc[...] = jnp.zeros_like(acc)

        acc[...] += _dot(a_ref[...], w_ref[...])

        @pl.when(j == nc - 1)
        def _():
            z = ALPHA * x_ref[...] + fscale * acc[...]
            mu = jnp.mean(z, axis=-1, keepdims=True)
            zc = z - mu
            var = jnp.mean(zc * zc, axis=-1, keepdims=True)
            r = lax.rsqrt(var + LN_EPS)
            xh = zc * r
            y = xh * g_ref[...] + b_ref[...]
            y_ref[...] = y
            yb_ref[...] = y.astype(BF)
            xh_ref[...] = xh
            r_ref[...] = r

    row = pl.BlockSpec((ts, d), lambda i, j: (i, 0))
    vec = pl.BlockSpec((1, d), lambda i, j: (0, 0))
    return pl.pallas_call(
        body, name=name, grid=(s // ts, nc),
        in_specs=[pl.BlockSpec((None, ts, kc), lambda i, j: (j, i, 0)),
                  pl.BlockSpec((None, kc, d), lambda i, j: (j, 0, 0)), row, vec, vec],
        out_specs=[row, row, row, pl.BlockSpec((ts, 1), lambda i, j: (i, 0))],
        out_shape=[jax.ShapeDtypeStruct((s, d), F32), jax.ShapeDtypeStruct((s, d), BF),
                   jax.ShapeDtypeStruct((s, d), F32), jax.ShapeDtypeStruct((s, 1), F32)],
        scratch_shapes=[pltpu.VMEM((ts, d), F32)],
        compiler_params=_cp("parallel", "arbitrary"))(a, w, x, gain, bias)


def _ln_bwd(dy, xh, rstd, gain, fscale, name, after=()):
    s, d = dy.shape
    ts = _rows(s)
    na = len(after)

    def body(*refs):
        dy_ref, xh_ref, r_ref, g_ref = refs[:4]
        dz_ref, dzb_ref, dg_ref, db_ref = refs[4 + na:]
        i = pl.program_id(0)
        dyv = dy_ref[...]
        xhv = xh_ref[...]
        dxh = dyv * g_ref[...]
        m1 = jnp.mean(dxh, axis=-1, keepdims=True)
        m2 = jnp.mean(dxh * xhv, axis=-1, keepdims=True)
        dz = r_ref[...] * (dxh - m1 - xhv * m2)
        dz_ref[...] = dz
        dzb_ref[...] = (fscale * dz).astype(BF)

        @pl.when(i == 0)
        def _():
            dg_ref[...] = jnp.zeros_like(dg_ref)
            db_ref[...] = jnp.zeros_like(db_ref)

        dg_ref[...] += jnp.sum(dyv * xhv, axis=0, keepdims=True)
        db_ref[...] += jnp.sum(dyv, axis=0, keepdims=True)

    row = pl.BlockSpec((ts, d), lambda i: (i, 0))
    vec = pl.BlockSpec((1, d), lambda i: (0, 0))
    return pl.pallas_call(
        body, name=name, grid=(s // ts,),
        in_specs=[row, row, pl.BlockSpec((ts, 1), lambda i: (i, 0)), vec] + [pl.BlockSpec(memory_space=pl.ANY)] * na,
        out_specs=[row, row, vec, vec],
        out_shape=[jax.ShapeDtypeStruct((s, d), F32), jax.ShapeDtypeStruct((s, d), BF),
                   jax.ShapeDtypeStruct((1, d), F32), jax.ShapeDtypeStruct((1, d), F32)],
        compiler_params=_cp("arbitrary"))(dy, xh, rstd, gain, *after)


def _ffn_up(xb, wgu, name):
    s, d = xb.shape
    c = wgu.shape[2]
    nch = wgu.shape[0] // 2
    ts = _rows(s)
    w4 = wgu.reshape(2, nch, d, c)

    def body(x_ref, w_ref, gu_ref, a_ref):
        x = x_ref[...]
        g = _dot(x, w_ref[0])
        u = _dot(x, w_ref[1])
        gu_ref[0] = g.astype(BF)
        gu_ref[1] = u.astype(BF)
        a_ref[...] = (g * _sigmoid(g) * u).astype(BF)

    return pl.pallas_call(
        body, name=name, grid=(nch, s // ts),
        in_specs=[pl.BlockSpec((ts, d), lambda j, i: (i, 0)),
                  pl.BlockSpec((2, None, d, c), lambda j, i: (0, j, 0, 0))],
        out_specs=[pl.BlockSpec((2, None, ts, c), lambda j, i: (0, j, i, 0)),
                   pl.BlockSpec((None, ts, c), lambda j, i: (j, i, 0))],
        out_shape=[jax.ShapeDtypeStruct((2, nch, s, c), BF), jax.ShapeDtypeStruct((nch, s, c), BF)],
        compiler_params=_cp("parallel", "parallel"))(xb, w4)


def _ffn_bwd_act(dzb, wd4, gu, name):
    s, d = dzb.shape
    nch, c = wd4.shape[0], wd4.shape[1]
    ts = _rows(s)

    def body(dz_ref, w_ref, gu_ref, dh_ref):
        da = _dot_nt(dz_ref[...], w_ref[...])
        g = gu_ref[0].astype(F32)
        u = gu_ref[1].astype(F32)
        sg = _sigmoid(g)
        dh_ref[0] = (da * u * sg * (1.0 + g * (1.0 - sg))).astype(BF)
        dh_ref[1] = (da * g * sg).astype(BF)

    return pl.pallas_call(
        body, name=name, grid=(nch, s // ts),
        in_specs=[pl.BlockSpec((ts, d), lambda j, i: (i, 0)),
                  pl.BlockSpec((None, c, d), lambda j, i: (j, 0, 0)),
                  pl.BlockSpec((2, None, ts, c), lambda j, i: (0, j, i, 0))],
        out_specs=pl.BlockSpec((2, None, ts, c), lambda j, i: (0, j, i, 0)),
        out_shape=jax.ShapeDtypeStruct((2, nch, s, c), BF),
        compiler_params=_cp("parallel", "parallel"))(dzb, wd4, gu)


def _rope_tables(s, sign):
    pos = jnp.arange(s, dtype=F32)
    inv_freq = 1.0 / (ROPE_THETA ** (jnp.arange(ROT_HALF, dtype=F32) / ROT_HALF))
    ang = pos[:, None] * inv_freq[None, :]
    cos, sin = jnp.cos(ang), jnp.sin(ang) * sign
    one = jnp.ones((s, HEAD_DIM - 2 * ROT_HALF), F32)
    zero = jnp.zeros((s, HEAD_DIM - 2 * ROT_HALF), F32)
    zh = jnp.zeros((s, ROT_HALF), F32)
    cos_f = jnp.concatenate([cos, cos, one], axis=1)
    sin_a = jnp.concatenate([-sin, zh, zero], axis=1)
    sin_b = jnp.concatenate([zh, sin, zero], axis=1)
    rep = LANES // HEAD_DIM
    return tuple(jnp.tile(t, (1, rep)) for t in (cos_f, sin_a, sin_b))


def _rope_cast(h, tabs, n_rope, name):
    s, n = h.shape
    ts = _rows(s, 1024)

    def body(h_ref, c_ref, sa_ref, sb_ref, o_ref):
        cb = pl.program_id(1)
        t = h_ref[...]

        @pl.when(cb < n_rope)
        def _():
            r = (t * c_ref[...] + pltpu.roll(t, LANES - ROT_HALF, 1) * sa_ref[...]
                 + pltpu.roll(t, ROT_HALF, 1) * sb_ref[...])
            o_ref[...] = r.astype(BF)

        @pl.when(cb >= n_rope)
        def _():
            o_ref[...] = t.astype(BF)

    blk = pl.BlockSpec((ts, LANES), lambda i, j: (i, j))
    tab = pl.BlockSpec((ts, LANES), lambda i, j: (i, 0))
    return pl.pallas_call(
        body, name=name, grid=(s // ts, n // LANES),
        in_specs=[blk, tab, tab, tab], out_specs=blk,
        out_shape=jax.ShapeDtypeStruct((s, n), BF),
        compiler_params=_cp("parallel", "parallel"))(h, *tabs)


def _head_masks():
    lane = lax.broadcasted_iota(jnp.int32, (1, LANES), 1)
    return [lane < HEAD_DIM, lane >= HEAD_DIM]


def _sel(mask, v):
    return jnp.where(mask, v, jnp.zeros_like(v))


def _pick(mask, wide, fill):
    return jnp.max(jnp.where(mask, wide, fill), axis=1, keepdims=True)


def _band_masks(has_other, prev):
    qi = lax.broadcasted_iota(jnp.int32, (BLOCK, BLOCK), 0)
    kj = lax.broadcasted_iota(jnp.int32, (BLOCK, BLOCK), 1)
    if prev:
        return kj >= qi + jnp.where(has_other, 0, BLOCK)
    return kj <= qi


def _band_fwd(q3, k3, v3, name):
    ng, s, w = q3.shape
    nb = s // BLOCK
    npair = w // LANES

    def body(q_ref, kc_ref, kp_ref, vc_ref, vp_ref, o_ref, l_ref):
        g = pl.program_id(0)
        b = pl.program_id(2)
        nbl = jnp.right_shift(nb, 2 * g)
        has_prev = jnp.bitwise_and(b, nbl - 1) != 0
        mc = _band_masks(None, False)
        mp = _band_masks(has_prev, True)
        q, kc, kp, vc, vp = q_ref[...], kc_ref[...], kp_ref[...], vc_ref[...], vp_ref[...]
        hm = _head_masks()
        o = jnp.zeros((BLOCK, LANES), F32)
        lse_w = jnp.zeros((BLOCK, LANES), F32)
        for h in range(2):
            qh = _sel(hm[h], q)
            sc = jnp.where(mc, _dot_nt(qh, kc) * SCALE, NEG)
            sp = jnp.where(mp, _dot_nt(qh, kp) * SCALE, NEG)
            m = jnp.maximum(jnp.max(sc, axis=1, keepdims=True), jnp.max(sp, axis=1, keepdims=True))
            pc = jnp.exp(sc - m)
            pp = jnp.exp(sp - m)
            l = jnp.sum(pc, axis=1, keepdims=True) + jnp.sum(pp, axis=1, keepdims=True)
            oh = _dot(pc.astype(BF), _sel(hm[h], vc)) + _dot(pp.astype(BF), _sel(hm[h], vp))
            o = o + oh / l
            lse_w = jnp.where(hm[h], m + jnp.log(l), lse_w)
        o_ref[...] = o
        l_ref[...] = lse_w

    cur = pl.BlockSpec((None, BLOCK, LANES), lambda g, p, b: (g, b, p))
    prv = pl.BlockSpec((None, BLOCK, LANES), lambda g, p, b: (g, jnp.maximum(b - 1, 0), p))
    return pl.pallas_call(
        body, name=name, grid=(ng, npair, nb),
        in_specs=[cur, cur, prv, cur, prv], out_specs=[cur, cur],
        out_shape=[jax.ShapeDtypeStruct((ng, s, w), F32), jax.ShapeDtypeStruct((ng, s, w), F32)],
        compiler_params=_cp("parallel", "parallel", "parallel"))(q3, k3, k3, v3, v3)


def _band_combine(o3, l3, name):
    ng, s, w = o3.shape
    ts = _rows(s)

    def body(o_ref, l_ref, oa_ref, lt_ref):
        ls = [l_ref[g] for g in range(ng)]
        m = functools.reduce(jnp.maximum, ls)
        es = [jnp.exp(l - m) for l in ls]
        den = functools.reduce(lambda a, b: a + b, es)
        num = functools.reduce(lambda a, b: a + b, [es[g] * o_ref[g] for g in range(ng)])
        oa_ref[...] = (num / den).astype(BF)
        lt_ref[...] = m + jnp.log(den)

    blk3 = pl.BlockSpec((ng, ts, w), lambda i: (0, i, 0))
    blk = pl.BlockSpec((ts, w), lambda i: (i, 0))
    return pl.pallas_call(
        body, name=name, grid=(s // ts,), in_specs=[blk3, blk3], out_specs=[blk, blk],
        out_shape=[jax.ShapeDtypeStruct((s, w), BF), jax.ShapeDtypeStruct((s, w), F32)],
        compiler_params=_cp("parallel"))(o3, l3)


def _band_bwd(q3, k3, v3, do3, oa3, lt3, name):
    ng, s, w = q3.shape
    nb = s // BLOCK
    npair = w // LANES

    def body(q_ref, qn_ref, kc_ref, kp_ref, vc_ref, vp_ref, do_ref, don_ref, oa_ref, oan_ref, lt_ref, ltn_ref,
             dq_ref, dk_ref, dv_ref):
        g = pl.program_id(0)
        b = pl.program_id(2)
        nbl = jnp.right_shift(nb, 2 * g)
        has_prev = jnp.bitwise_and(b, nbl - 1) != 0
        has_next = jnp.bitwise_and(b + 1, nbl - 1) != 0
        mc = _band_masks(None, False)
        mp = _band_masks(has_prev, True)
        mn = _band_masks(has_next, True)
        q, qn, kc, kp, vc, vp = q_ref[...], qn_ref[...], kc_ref[...], kp_ref[...], vc_ref[...], vp_ref[...]
        do, don = do_ref[...], don_ref[...]
        dd = do.astype(F32) * oa_ref[...].astype(F32)
        ddn = don.astype(F32) * oan_ref[...].astype(F32)
        lt, ltn = lt_ref[...], ltn_ref[...]
        hm = _head_masks()
        dq = jnp.zeros((BLOCK, LANES), F32)
        dk = jnp.zeros((BLOCK, LANES), F32)
        dv = jnp.zeros((BLOCK, LANES), F32)
        for h in range(2):
            qh, doh = _sel(hm[h], q), _sel(hm[h], do)
            qnh, donh = _sel(hm[h], qn), _sel(hm[h], don)
            kch, kph = _sel(hm[h], kc), _sel(hm[h], kp)
            lse = _pick(hm[h], lt, NEG)
            lsen = _pick(hm[h], ltn, NEG)
            dsum = jnp.sum(_sel(hm[h], dd), axis=1, keepdims=True)
            dsumn = jnp.sum(_sel(hm[h], ddn), axis=1, keepdims=True)
            pc = jnp.exp(jnp.where(mc, _dot_nt(qh, kc) * SCALE, NEG) - lse)
            pp = jnp.exp(jnp.where(mp, _dot_nt(qh, kp) * SCALE, NEG) - lse)
            dsc = pc * (_dot_nt(doh, vc) - dsum)
            dsp = pp * (_dot_nt(doh, vp) - dsum)
            dq = dq + SCALE * (_dot(dsc.astype(BF), kch) + _dot(dsp.astype(BF), kph))
            pn = jnp.exp(jnp.where(mn, _dot_nt(qnh, kc) * SCALE, NEG) - lsen)
            dsn = pn * (_dot_nt(donh, vc) - dsumn)
            dk = dk + SCALE * (_dot_tn(dsc.astype(BF), qh) + _dot_tn(dsn.astype(BF), qnh))
            dv = dv + _dot_tn(pc.astype(BF), doh) + _dot_tn(pn.astype(BF), donh)
        dq_ref[...] = dq
        dk_ref[...] = dk
        dv_ref[...] = dv

    cur = pl.BlockSpec((None, BLOCK, LANES), lambda g, p, b: (g, b, p))
    prv = pl.BlockSpec((None, BLOCK, LANES), lambda g, p, b: (g, jnp.maximum(b - 1, 0), p))
    nxt = pl.BlockSpec((None, BLOCK, LANES), lambda g, p, b: (g, jnp.minimum(b + 1, nb - 1), p))
    out = jax.ShapeDtypeStruct((ng, s, w), F32)
    return pl.pallas_call(
        body, name=name, grid=(ng, npair, nb),
        in_specs=[cur, nxt, cur, prv, cur, prv, cur, nxt, cur, nxt, cur, nxt],
        out_specs=[cur, cur, cur], out_shape=[out, out, out],
        compiler_params=_cp("parallel", "parallel", "parallel"))(
            q3, q3, k3, k3, v3, v3, do3, do3, oa3, oa3, lt3, lt3)


def _mem_fwd(hb, q_blk0, kv, name):
    s = hb.shape[0]
    m = kv.shape[0]
    tq = _rows(s)
    npair = MEM_W // LANES

    def body(q_ref, k_ref, v_ref, o_ref, l_ref):
        q, k, v = q_ref[...], k_ref[...], v_ref[...]
        hm = _head_masks()
        o = jnp.zeros((tq, LANES), F32)
        lse_w = jnp.zeros((tq, LANES), F32)
        for h in range(2):
            sc = _dot_nt(_sel(hm[h], q), k) * SCALE
            mx = jnp.max(sc, axis=1, keepdims=True)
            p = jnp.exp(sc - mx)
            l = jnp.sum(p, axis=1, keepdims=True)
            o = o + _dot(p.astype(BF), _sel(hm[h], v)) / l
            lse_w = jnp.where(hm[h], mx + jnp.log(l), lse_w)
        o_ref[...] = o.astype(BF)
        l_ref[...] = lse_w

    blk = pl.BlockSpec((tq, LANES), lambda p, i: (i, p))
    return pl.pallas_call(
        body, name=name, grid=(npair, s // tq),
        in_specs=[pl.BlockSpec((tq, LANES), lambda p, i: (i, q_blk0 + p)),
                  pl.BlockSpec((m, LANES), lambda p, i: (0, p)),
                  pl.BlockSpec((m, LANES), lambda p, i: (0, npair + p))],
        out_specs=[blk, blk],
        out_shape=[jax.ShapeDtypeStruct((s, MEM_W), BF), jax.ShapeDtypeStruct((s, MEM_W), F32)],
        compiler_params=_cp("parallel", "parallel"))(hb, kv, kv)


def _mem_bwd(hb, q_blk0, kv, dcat, cat, o_blk0, lse, name):
    s = hb.shape[0]
    m = kv.shape[0]
    tq = _rows(s)
    npair = MEM_W // LANES

    def body(q_ref, k_ref, v_ref, do_ref, o_ref, l_ref, dq_ref, dk_ref, dv_ref):
        i = pl.program_id(1)

        @pl.when(i == 0)
        def _():
            dk_ref[...] = jnp.zeros_like(dk_ref)
            dv_ref[...] = jnp.zeros_like(dv_ref)

        q, k, v, do = q_ref[...], k_ref[...], v_ref[...], do_ref[...]
        dd = do.astype(F32) * o_ref[...].astype(F32)
        lt = l_ref[...]
        hm = _head_masks()
        dq = jnp.zeros((tq, LANES), F32)
        dk = jnp.zeros((m, LANES), F32)
        dv = jnp.zeros((m, LANES), F32)
        for h in range(2):
            qh, doh = _sel(hm[h], q), _sel(hm[h], do)
            p = jnp.exp(_dot_nt(qh, k) * SCALE - _pick(hm[h], lt, NEG))
            ds = p * (_dot_nt(doh, v) - jnp.sum(_sel(hm[h], dd), axis=1, keepdims=True))
            dq = dq + SCALE * _dot(ds.astype(BF), _sel(hm[h], k))
            dk = dk + SCALE * _dot_tn(ds.astype(BF), qh)
            dv = dv + _dot_tn(p.astype(BF), doh)
        dq_ref[...] = dq
        dk_ref[...] += dk
        dv_ref[...] += dv

    row = pl.BlockSpec((tq, LANES), lambda p, i: (i, p))
    orow = pl.BlockSpec((tq, LANES), lambda p, i: (i, o_blk0 + p))
    acc = pl.BlockSpec((m, LANES), lambda p, i: (0, p))
    return pl.pallas_call(
        body, name=name, grid=(npair, s // tq),
        in_specs=[pl.BlockSpec((tq, LANES), lambda p, i: (i, q_blk0 + p)),
                  pl.BlockSpec((m, LANES), lambda p, i: (0, p)),
                  pl.BlockSpec((m, LANES), lambda p, i: (0, npair + p)), orow, orow, row],
        out_specs=[row, acc, acc],
        out_shape=[jax.ShapeDtypeStruct((s, MEM_W), F32), jax.ShapeDtypeStruct((m, MEM_W), F32),
                   jax.ShapeDtypeStruct((m, MEM_W), F32)],
        compiler_params=_cp("parallel", "arbitrary"))(hb, kv, kv, dcat, cat, lse)


def _gate_fwd(f_t, bias, name):
    hp, s = f_t.shape
    nblk = s // LANES

    def body(f_ref, b_ref, c_ref):
        lane = lax.broadcasted_iota(jnp.int32, (hp, LANES), 1)

        def step(i, carry):
            off = pl.multiple_of(i * LANES, LANES)
            x = f_ref[:, pl.ds(off, LANES)] + b_ref[...]
            acc = jnp.minimum(x, 0.0) - jnp.log(1.0 + jnp.exp(-jnp.abs(x)))
            sh = 1
            while sh < LANES:
                acc = acc + jnp.where(lane >= sh, pltpu.roll(acc, sh, 1), 0.0)
                sh *= 2
            acc = acc + carry
            c_ref[:, pl.ds(off, LANES)] = acc
            return acc[:, LANES - 1:LANES]

        lax.fori_loop(0, nblk, step, jnp.zeros((hp, 1), F32))

    vm = pl.BlockSpec(memory_space=pltpu.VMEM)
    return pl.pallas_call(body, name=name, in_specs=[vm, vm], out_specs=vm,
                          out_shape=jax.ShapeDtypeStruct((hp, s), F32),
                          compiler_params=pltpu.CompilerParams(vmem_limit_bytes=VMEM_LIMIT))(f_t, bias)


def _gate_bwd(dc_t, f_t, bias, name):
    hp, s = f_t.shape
    nblk = s // LANES

    def body(dc_ref, f_ref, b_ref, df_ref, db_ref):
        lane = lax.broadcasted_iota(jnp.int32, (hp, LANES), 1)

        def step(t, carry):
            suffix, dbias = carry
            off = pl.multiple_of((nblk - 1 - t) * LANES, LANES)
            acc = dc_ref[:, pl.ds(off, LANES)]
            sh = 1
            while sh < LANES:
                acc = acc + jnp.where(lane < LANES - sh, pltpu.roll(acc, LANES - sh, 1), 0.0)
                sh *= 2
            acc = acc + suffix
            x = f_ref[:, pl.ds(off, LANES)] + b_ref[...]
            df = acc * _sigmoid(-x)
            df_ref[:, pl.ds(off, LANES)] = df
            return acc[:, 0:1], dbias + jnp.sum(df, axis=1, keepdims=True)

        _, dbias = lax.fori_loop(0, nblk, step, (jnp.zeros((hp, 1), F32), jnp.zeros((hp, 1), F32)))
        db_ref[...] = dbias

    vm = pl.BlockSpec(memory_space=pltpu.VMEM)
    return pl.pallas_call(body, name=name, in_specs=[vm, vm, vm], out_specs=[vm, vm],
                          out_shape=[jax.ShapeDtypeStruct((hp, s), F32), jax.ShapeDtypeStruct((hp, 1), F32)],
                          compiler_params=pltpu.CompilerParams(vmem_limit_bytes=VMEM_LIMIT))(dc_t, f_t, bias)


def _causal(qi, kj, tq, tk):
    qpos = qi * tq + lax.broadcasted_iota(jnp.int32, (tq, tk), 0)
    kpos = kj * tk + lax.broadcasted_iota(jnp.int32, (tq, tk), 1)
    return kpos <= qpos


def _fox_fwd(hb, c_pad, c_t3, name):
    s = hb.shape[0]
    npair = MIX_W // LANES
    tq = tk = _rows(s)
    nq = s // tq

    def body(q_ref, k_ref, v_ref, cq_ref, ck_ref, o_ref, l_ref, m_s, l_s, cq_s, acc):
        p = pl.program_id(0)
        qi = pl.program_id(1)
        kj = pl.program_id(2)
        hm = _head_masks()

        @pl.when(kj == 0)
        def _():
            lane = lax.broadcasted_iota(jnp.int32, (1, LANES), 1)
            cq = cq_ref[...]
            for h in range(2):
                m_s[h] = jnp.full((tq, 1), NEG, F32)
                l_s[h] = jnp.zeros((tq, 1), F32)
                cq_s[h] = jnp.sum(jnp.where(lane == 2 * p + h, cq, 0.0), axis=1, keepdims=True)
            acc[...] = jnp.zeros_like(acc)

        @pl.when(kj <= qi)
        def _():
            q, k, v = q_ref[...], k_ref[...], v_ref[...]
            ck = ck_ref[...]
            mask = _causal(qi, kj, tq, tk)
            a = acc[...]
            for h in range(2):
                sc = _dot_nt(_sel(hm[h], q), k) * SCALE + (cq_s[h] - ck[h:h + 1, :])
                sc = jnp.where(mask, sc, NEG)
                m_old = m_s[h]
                m_new = jnp.maximum(m_old, jnp.max(sc, axis=1, keepdims=True))
                corr = jnp.exp(m_old - m_new)
                pr = jnp.exp(sc - m_new)
                l_s[h] = corr * l_s[h] + jnp.sum(pr, axis=1, keepdims=True)
                m_s[h] = m_new
                a = a * jnp.where(hm[h], corr, 1.0) + _dot(pr.astype(BF), _sel(hm[h], v))
            acc[...] = a

        @pl.when(kj == qi)
        def _():
            l_w = jnp.where(hm[0], l_s[0], l_s[1])
            m_w = jnp.where(hm[0], m_s[0], m_s[1])
            o_ref[...] = (acc[...] / l_w).astype(BF)
            l_ref[...] = m_w + jnp.log(l_w)

    def kv_map(off):
        return lambda p, i, j: (jnp.minimum(j, i), off + p)

    blk = pl.BlockSpec((tq, LANES), lambda p, i, j: (i, p))
    return pl.pallas_call(
        body, name=name, grid=(npair, nq, nq),
        in_specs=[blk, pl.BlockSpec((tk, LANES), kv_map(npair)), pl.BlockSpec((tk, LANES), kv_map(2 * npair)),
                  pl.BlockSpec((tq, LANES), lambda p, i, j: (i, 0)),
                  pl.BlockSpec((None, 2, tk), lambda p, i, j: (p, 0, jnp.minimum(j, i)))],
        out_specs=[blk, blk],
        out_shape=[jax.ShapeDtypeStruct((s, MIX_W), BF), jax.ShapeDtypeStruct((s, MIX_W), F32)],
        scratch_shapes=[pltpu.VMEM((2, tq, 1), F32), pltpu.VMEM((2, tq, 1), F32), pltpu.VMEM((2, tq, 1), F32),
                        pltpu.VMEM((tq, LANES), F32)],
        compiler_params=_cp("parallel", "parallel", "arbitrary"))(hb, hb, hb, c_pad, c_t3)


def _fox_dsum(hb, dcat, lse, c_pad, c_t3, name):
    s = hb.shape[0]
    npair = MIX_W // LANES
    tq = tk = _rows(s)
    nq = s // tq

    def body(q_ref, k_ref, v_ref, do_ref, l_ref, cq_ref, ck_ref, d_ref, acc):
        p = pl.program_id(0)
        qi = pl.program_id(1)
        kj = pl.program_id(2)
        hm = _head_masks()

        @pl.when(kj == 0)
        def _():
            acc[...] = jnp.zeros_like(acc)

        @pl.when(kj <= qi)
        def _():
            q, k, v, do = q_ref[...], k_ref[...], v_ref[...], do_ref[...]
            lt = l_ref[...]
            cq = cq_ref[...]
            ck = ck_ref[...]
            lane = lax.broadcasted_iota(jnp.int32, (1, LANES), 1)
            mask = _causal(qi, kj, tq, tk)
            for h in range(2):
                cqh = jnp.sum(jnp.where(lane == 2 * p + h, cq, 0.0), axis=1, keepdims=True)
                sc = _dot_nt(_sel(hm[h], q), k) * SCALE + (cqh - ck[h:h + 1, :])
                pr = jnp.exp(jnp.where(mask, sc, NEG) - _pick(hm[h], lt, NEG))
                acc[h] += jnp.sum(pr * _dot_nt(_sel(hm[h], do), v), axis=1, keepdims=True)

        @pl.when(kj == qi)
        def _():
            d_ref[...] = jnp.where(hm[0], acc[0], acc[1])

    def kv_map(off):
        return lambda p, i, j: (jnp.minimum(j, i), off + p)

    blk = pl.BlockSpec((tq, LANES), lambda p, i, j: (i, p))
    return pl.pallas_call(
        body, name=name, grid=(npair, nq, nq),
        in_specs=[blk, pl.BlockSpec((tk, LANES), kv_map(npair)), pl.BlockSpec((tk, LANES), kv_map(2 * npair)),
                  blk, blk, pl.BlockSpec((tq, LANES), lambda p, i, j: (i, 0)),
                  pl.BlockSpec((None, 2, tk), lambda p, i, j: (p, 0, jnp.minimum(j, i)))],
        out_specs=blk, out_shape=jax.ShapeDtypeStruct((s, MIX_W), F32),
        scratch_shapes=[pltpu.VMEM((2, tq, 1), F32)],
        compiler_params=_cp("parallel", "parallel", "arbitrary"))(hb, hb, hb, dcat, lse, c_pad, c_t3)


def _fox_bwd(hb, dcat, dsum, lse, c_pad, c_t3, name):
    s = hb.shape[0]
    npair = MIX_W // LANES
    tq = tk = _rows(s)
    nq = s // tq

    def body(q_ref, k_ref, v_ref, do_ref, d_ref, l_ref, cq_ref, ck_ref, dq_ref, dk_ref, dv_ref, dc_ref):
        p = pl.program_id(0)
        kj = pl.program_id(1)
        qi = pl.program_id(2)
        hm = _head_masks()

        @pl.when(qi == 0)
        def _():
            dk_ref[...] = jnp.zeros_like(dk_ref)
            dv_ref[...] = jnp.zeros_like(dv_ref)
            dc_ref[...] = jnp.zeros_like(dc_ref)

        @pl.when((qi == 0) & (kj == 0))
        def _():
            dq_ref[...] = jnp.zeros_like(dq_ref)

        @pl.when(qi >= kj)
        def _():
            q, k, v, do = q_ref[...], k_ref[...], v_ref[...], do_ref[...]
            dw = d_ref[...]
            lt = l_ref[...]
            cq = cq_ref[...]
            ck = ck_ref[...]
            lane = lax.broadcasted_iota(jnp.int32, (1, LANES), 1)
            mask = _causal(qi, kj, tq, tk)
            dq = jnp.zeros((tq, LANES), F32)
            dk = jnp.zeros((tk, LANES), F32)
            dv = jnp.zeros((tk, LANES), F32)
            dcs = []
            for h in range(2):
                qh, doh = _sel(hm[h], q), _sel(hm[h], do)
                cqh = jnp.sum(jnp.where(lane == 2 * p + h, cq, 0.0), axis=1, keepdims=True)
                sc = _dot_nt(qh, k) * SCALE + (cqh - ck[h:h + 1, :])
                pr = jnp.exp(jnp.where(mask, sc, NEG) - _pick(hm[h], lt, NEG))
                ds = pr * (_dot_nt(doh, v) - _pick(hm[h], dw, NEG))
                dsb = ds.astype(BF)
                dq = dq + SCALE * _dot(dsb, _sel(hm[h], k))
                dk = dk + SCALE * _dot_tn(dsb, qh)
                dv = dv + _dot_tn(pr.astype(BF), doh)
                dcs.append(jnp.sum(ds, axis=0, keepdims=True))
            rows = pl.ds(pl.multiple_of(qi * tq, tq), tq)
            dq_ref[rows, :] += dq
            dk_ref[...] += dk
            dv_ref[...] += dv
            dc_ref[...] -= jnp.concatenate(dcs, axis=0)

    def q_map(off):
        return lambda p, j, i: (jnp.maximum(i, j), off + p)

    kblk = pl.BlockSpec((tk, LANES), lambda p, j, i: (j, p))
    return pl.pallas_call(
        body, name=name, grid=(npair, nq, nq),
        in_specs=[pl.BlockSpec((tq, LANES), q_map(0)),
                  pl.BlockSpec((tk, LANES), lambda p, j, i: (j, npair + p)),
                  pl.BlockSpec((tk, LANES), lambda p, j, i: (j, 2 * npair + p)),
                  pl.BlockSpec((tq, LANES), q_map(0)), pl.BlockSpec((tq, LANES), q_map(0)),
                  pl.BlockSpec((tq, LANES), q_map(0)),
                  pl.BlockSpec((tq, LANES), lambda p, j, i: (jnp.maximum(i, j), 0)),
                  pl.BlockSpec((None, 2, tk), lambda p, j, i: (p, 0, j))],
        out_specs=[pl.BlockSpec((s, LANES), lambda p, j, i: (0, p)), kblk, kblk,
                   pl.BlockSpec((None, 2, tk), lambda p, j, i: (p, 0, j))],
        out_shape=[jax.ShapeDtypeStruct((s, MIX_W), F32), jax.ShapeDtypeStruct((s, MIX_W), F32),
                   jax.ShapeDtypeStruct((s, MIX_W), F32), jax.ShapeDtypeStruct((npair, 2, s), F32)],
        compiler_params=_cp("arbitrary", "arbitrary", "arbitrary"))(hb, hb, hb, dcat, dsum, lse, c_pad, c_t3)


def _loss_head(y, target, name):
    s, d = y.shape
    ts = _rows(s)

    def body(y_ref, t_ref, dy_ref, l_ref):
        i = pl.program_id(0)
        e = y_ref[...] - t_ref[...]
        dy_ref[...] = e * (1.0 / d)

        @pl.when(i == 0)
        def _():
            l_ref[...] = jnp.zeros_like(l_ref)

        part = jnp.sum(jnp.sum(e * e, axis=1, keepdims=True), axis=0, keepdims=True)
        l_ref[...] += part * (0.5 / d)

    row = pl.BlockSpec((ts, d), lambda i: (i, 0))
    return pl.pallas_call(
        body, name=name, grid=(s // ts,), in_specs=[row, row],
        out_specs=[row, pl.BlockSpec((1, 1), lambda i: (0, 0))],
        out_shape=[jax.ShapeDtypeStruct((s, d), F32), jax.ShapeDtypeStruct((1, 1), F32)],
        compiler_params=_cp("arbitrary"))(y, target)


def _adam_rows(r, c):
    cap = max(8, (1 << 20) // (4 * c))
    if r <= cap:
        return r
    best = None
    for t in range(8, cap + 1, 8):
        if r % t == 0:
            best = t
    return best if best is not None else r


def _reduce_adamw(contribs, w, m, v, name):
    nl = len(contribs)
    nd, r, c = contribs[0].shape
    tr = _adam_rows(r, c)
    bc1 = 1.0 - ADAM_B1 ** ADAM_STEP
    bc2 = 1.0 - ADAM_B2 ** ADAM_STEP

    def body(*refs):
        c_refs = refs[:nl]
        w_ref, m_ref, v_ref, g_ref, d_ref, nm_ref, nv_ref = refs[nl:]
        l = pl.program_id(0)
        for li in range(nl):
            @pl.when(l == li)
            def _(c_ref=c_refs[li]):
                g = c_ref[0].astype(F32)
                for k in range(1, nd):
                    g = g + c_ref[k].astype(F32)
                nm = ADAM_B1 * m_ref[...] + (1.0 - ADAM_B1) * g
                nv = ADAM_B2 * v_ref[...] + (1.0 - ADAM_B2) * (g * g)
                g_ref[...] = g
                nm_ref[...] = nm
                nv_ref[...] = nv
                d_ref[...] = -ADAM_LR * ((nm / bc1) / (jnp.sqrt(nv / bc2) + ADAM_EPS) + ADAM_WD * w_ref[...])

    def c_spec(li):
        return pl.BlockSpec((nd, tr, c), lambda l, i: (0, jnp.where(l == li, i, 0), 0))

    blk = pl.BlockSpec((None, tr, c), lambda l, i: (l, i, 0))
    out = jax.ShapeDtypeStruct((nl, r, c), F32)
    return pl.pallas_call(
        body, name=name, grid=(nl, r // tr),
        in_specs=[c_spec(li) for li in range(nl)] + [blk, blk, blk],
        out_specs=[blk, blk, blk, blk], out_shape=[out, out, out, out],
        compiler_params=_cp("arbitrary", "arbitrary"))(*contribs, w, m, v)


def _mesh_pos():
    return lax.axis_index("x"), lax.axis_index("y"), lax.axis_index("c")


def _peer(pos, k):
    x, y, c = pos
    return (1 - x if k & 4 else x, 1 - y if k & 2 else y, 1 - c if k & 1 else c)


def _linear(pos):
    return 4 * pos[0] + 2 * pos[1] + pos[2]


def _xfer_copies(srcs, lands, send_sems, recv_sems, local_sems, gather):
    pos = _mesh_pos()
    me = _linear(pos)
    local, remote = [], []
    for i, (src, land) in enumerate(zip(srcs, lands)):
        local.append(pltpu.make_async_copy(src if gather else src.at[me], land.at[me], local_sems.at[i]))
        for k in range(1, N_DEV):
            peer = _peer(pos, k)
            remote.append(pltpu.make_async_remote_copy(
                src_ref=src if gather else src.at[_linear(peer)], dst_ref=land.at[me],
                send_sem=send_sems.at[i * (N_DEV - 1) + k - 1], recv_sem=recv_sems.at[i * (N_DEV - 1) + k - 1],
                device_id=peer, device_id_type=MESH_ID))
    return local, remote


_HBM = pl.BlockSpec(memory_space=pltpu.HBM)
_SEM = pl.BlockSpec(memory_space=pltpu.SEMAPHORE)
_EFFECT = pltpu.SideEffectType.DATAFLOW_SIDE_EFFECTING


def _xfer_start(srcs, gather, name, after=()):
    n = len(srcs)
    na = len(after)
    lands = [lax.empty(((N_DEV,) + a.shape) if gather else a.shape, a.dtype) for a in srcs]

    def body(*refs):
        src, land = refs[:n], refs[n:2 * n]
        send_sems, recv_sems, local_sems = refs[2 * n + na:2 * n + na + 3]
        local, remote = _xfer_copies(src, land, send_sems, recv_sems, local_sems, gather)
        for cp in local + remote:
            cp.start()
        refs[-1][...] = jnp.zeros_like(refs[-1])

    nsem = n * (N_DEV - 1)
    out = pl.pallas_call(
        body, name=name,
        out_shape=(pltpu.SemaphoreType.DMA((nsem,)), pltpu.SemaphoreType.DMA((nsem,)), pltpu.SemaphoreType.DMA((n,)),
                   *[pltpu.HBM(a.shape, a.dtype) for a in srcs], *[pltpu.HBM(a.shape, a.dtype) for a in lands],
                   jax.ShapeDtypeStruct((8, LANES), F32)),
        in_specs=[_HBM] * (2 * n) + [pl.BlockSpec(memory_space=pl.ANY)] * na,
        out_specs=(_SEM, _SEM, _SEM, *[_HBM] * (2 * n), pl.BlockSpec(memory_space=pltpu.VMEM)),
        input_output_aliases={i: 3 + i for i in range(2 * n)},
        compiler_params=pltpu.CompilerParams(has_side_effects=_EFFECT))(
            *[pltpu.with_memory_space_constraint(a, pltpu.HBM) for a in srcs],
            *[pltpu.with_memory_space_constraint(a, pltpu.HBM) for a in lands], *after)
    return out[:3], list(out[3:3 + n]), list(out[3 + n:3 + 2 * n]), out[-1]


def _started(handle):
    return handle[3]


def _xfer_wait(handle, after, gather, name):
    sems, srcs, lands, _ = handle
    n = len(srcs)

    def body(*refs):
        src, land = refs[:n], refs[n:2 * n]
        send_sems, recv_sems, local_sems = refs[2 * n:2 * n + 3]
        local, remote = _xfer_copies(src, land, send_sems, recv_sems, local_sems, gather)
        for cp in local:
            cp.wait()
        for cp in remote:
            cp.wait_send()
            cp.wait_recv()

    out = pl.pallas_call(
        body, name=name,
        out_shape=(*[pltpu.HBM(a.shape, a.dtype) for a in srcs], *[pltpu.HBM(a.shape, a.dtype) for a in lands]),
        in_specs=[_HBM] * (2 * n) + [_SEM] * 3 + [pl.BlockSpec(memory_space=pl.ANY)] * len(after),
        out_specs=tuple([_HBM] * (2 * n)), input_output_aliases={i: i for i in range(2 * n)},
        compiler_params=pltpu.CompilerParams(has_side_effects=_EFFECT))(*srcs, *lands, *sems, *after)
    return list(out[n:])


def _cols_full(g):
    nd, r, c = g.shape
    return jnp.transpose(g, (1, 0, 2)).reshape(r, nd * c)


def _cols_split(full):
    r, n = full.shape
    return jnp.transpose(full.reshape(r, N_DEV, n // N_DEV), (1, 0, 2))


def _pack_b_in(w):
    qkv = 3 * MIX_W
    pad = jnp.zeros((w.shape[0], B_IN_PAD - w.shape[1]), w.dtype)
    return jnp.concatenate([w[:, :qkv], w[:, qkv + N_MIX_HEADS:], w[:, qkv:qkv + N_MIX_HEADS], pad], axis=1)


def _unpack_b_in(w):
    qkv = 3 * MIX_W
    return jnp.concatenate([w[:, :qkv], w[:, qkv + MEM_W:qkv + MEM_W + N_MIX_HEADS], w[:, qkv:qkv + MEM_W]], axis=1)


def _to_classes(t, g):
    r = 4 ** g
    s, w = t.shape
    return jnp.transpose(t.reshape(s // r, r, w), (1, 0, 2)).reshape(s, w)


def _from_classes(t, g):
    r = 4 ** g
    s, w = t.shape
    return jnp.transpose(t.reshape(r, s // r, w), (1, 0, 2)).reshape(s, w)


def _group_stack(t):
    return jnp.stack([_to_classes(t[:, g * GROUP_W:(g + 1) * GROUP_W], g) for g in range(N_GROUPS)])


def _group_unstack(t3):
    return jnp.concatenate([_from_classes(t3[g], g) for g in range(N_GROUPS)], axis=1)


def _same_stack(t):
    return jnp.stack([_to_classes(t, g) for g in range(N_GROUPS)])


def _same_unstack(t3):
    return jnp.stack([_from_classes(t3[g], g) for g in range(N_GROUPS)])


def _ffn_forward(x, xb, wgu, wd4, gain, bias, tag):
    gu, a = _ffn_up(xb, wgu, f"{tag}_up")
    y, yb, xh, rstd = _mm_res_ln(a, wd4, x, gain, bias, 0.5, f"{tag}_down_ln")
    return y, yb, (xb, gu, a, xh, rstd)


def _ffn_backward(dy, saved, wgu, wd4, gain, tag, after=()):
    xb, gu, a, xh, rstd = saved
    s = xb.shape[0]
    nd, d, c = wgu.shape
    dz, dzb, dgain, dbias = _ln_bwd(dy, xh, rstd, gain, 0.5, f"{tag}_ln_bwd", after)
    dh = _ffn_bwd_act(dzb, wd4, gu, f"{tag}_act_bwd").reshape(nd, s, c)
    dwd = _mm_tn(a, dzb[None], f"{tag}_dwd").reshape(nd, wd4.shape[1] // 2, d)
    dx = _mm_nt(dh, wgu, f"{tag}_dx", res=dz)
    dwgu = _mm_tn(xb[None], dh, f"{tag}_dwgu")
    return dx, dwgu, dwd, dgain, dbias


def _mixer_a_forward(x, xb, memb, w_in, w_kv, w_out, gain, bias, tabs):
    h = _mm_nn(xb, w_in, F32, "a_in")
    hb = _rope_cast(h, tabs, 2 * MIX_W // LANES, "a_rope")
    q3 = _group_stack(hb[:, :MIX_W])
    k3 = _group_stack(hb[:, MIX_W:2 * MIX_W])
    v3 = _group_stack(hb[:, 2 * MIX_W:3 * MIX_W])
    o3, l3 = _band_fwd(q3, k3, v3, "a_band_fwd")
    oa, lt = _band_combine(_same_unstack(o3), _same_unstack(l3), "a_combine")
    kv = _mm_nn(memb, w_kv, BF, "a_mem_kv")
    om, lm = _mem_fwd(hb, 3 * MIX_W // LANES, kv, "a_mem_fwd")
    cat = jnp.concatenate([oa, om], axis=1)
    y, yb, xh, rstd = _mm_res_ln(cat[None], w_out[None], x, gain, bias, 1.0, "a_out_ln")
    return y, yb, (xb, hb, q3, k3, v3, oa, lt, kv, lm, cat, xh, rstd)


def _mixer_a_backward(dy, saved, memb, w_in, w_kv, w_out, gain, tabs_neg, after=()):
    xb, hb, q3, k3, v3, oa, lt, kv, lm, cat, xh, rstd = saved
    dz, dzb, dgain, dbias = _ln_bwd(dy, xh, rstd, gain, 1.0, "a_ln_bwd", after)
    dcat = _mm_nt(dzb[None], w_out[None], "a_dcat", out_dtype=BF)
    dw_out = _mm_tn(cat[None], dzb[None], "a_dwout")[0]
    dqm, dkm, dvm = _mem_bwd(hb, 3 * MIX_W // LANES, kv, dcat, cat, GROUP_W // LANES, lm, "a_mem_bwd")
    dkv = jnp.concatenate([dkm, dvm], axis=1).astype(BF)
    dw_kv = _mm_tn(memb[None], dkv[None], "a_dwkv")[0]
    dq3, dk3, dv3 = _band_bwd(q3, k3, v3, _same_stack(dcat[:, :GROUP_W]), _same_stack(oa), _same_stack(lt),
                              "a_band_bwd")
    dh = jnp.concatenate([_group_unstack(dq3), _group_unstack(dk3), _group_unstack(dv3), dqm], axis=1)
    dhb = _rope_cast(dh, tabs_neg, 2 * MIX_W // LANES, "a_rope_bwd")
    dw_in = _mm_tn(xb[None], dhb[None], "a_dwin")[0]
    dx = _mm_nt(dhb[None], w_in[None], "a_dx", res=dz)
    return dx, dw_in, dw_kv, dw_out, dgain, dbias


def _pad_rows(t, rows):
    return jnp.concatenate([t, jnp.zeros((rows - t.shape[0], t.shape[1]), t.dtype)], axis=0)


def _pad_cols(t, cols):
    return jnp.concatenate([t, jnp.zeros((t.shape[0], cols - t.shape[1]), t.dtype)], axis=1)


def _mixer_b_forward(x, xb, memb, w_in, fbias, w_kv, w_out, gain, bias, tabs):
    s = x.shape[0]
    h = _mm_nn(xb, w_in, F32, "b_in")
    hb = _rope_cast(h, tabs, 0, "b_cast")
    f0 = 3 * MIX_W + MEM_W
    f_t = _pad_rows(jnp.transpose(h[:, f0:f0 + N_MIX_HEADS]), 16)
    bias16 = _pad_rows(jnp.transpose(fbias), 16)
    c_t = _gate_fwd(f_t, bias16, "b_gate_fwd")
    c_pad = _pad_cols(jnp.transpose(c_t[:N_MIX_HEADS]), LANES)
    c_t3 = c_t[:N_MIX_HEADS].reshape(N_MIX_HEADS // 2, 2, s)
    ob, lb = _fox_fwd(hb, c_pad, c_t3, "b_fox_fwd")
    kv = _mm_nn(memb, w_kv, BF, "b_mem_kv")
    om, lm = _mem_fwd(hb, 3 * MIX_W // LANES, kv, "b_mem_fwd")
    cat = jnp.concatenate([ob, om], axis=1)
    y, yb, xh, rstd = _mm_res_ln(cat[None], w_out[None], x, gain, bias, 1.0, "b_out_ln")
    return y, yb, (xb, hb, f_t, bias16, c_pad, c_t3, lb, kv, lm, cat, xh, rstd)


def _mixer_b_backward(dy, saved, memb, w_in, w_kv, w_out, gain, tabs, after=()):
    xb, hb, f_t, bias16, c_pad, c_t3, lb, kv, lm, cat, xh, rstd = saved
    s = xb.shape[0]
    dz, dzb, dgain, dbias = _ln_bwd(dy, xh, rstd, gain, 1.0, "b_ln_bwd", after)
    dcat = _mm_nt(dzb[None], w_out[None], "b_dcat", out_dtype=BF)
    dw_out = _mm_tn(cat[None], dzb[None], "b_dwout")[0]
    dqm, dkm, dvm = _mem_bwd(hb, 3 * MIX_W // LANES, kv, dcat, cat, MIX_W // LANES, lm, "b_mem_bwd")
    dkv = jnp.concatenate([dkm, dvm], axis=1).astype(BF)
    dw_kv = _mm_tn(memb[None], dkv[None], "b_dwkv")[0]
    dsum = _fox_dsum(hb, dcat, lb, c_pad, c_t3, "b_fox_dsum")
    dq, dk, dv, dc3 = _fox_bwd(hb, dcat, dsum, lb, c_pad, c_t3, "b_fox_bwd")
    df_t, dfb = _gate_bwd(_pad_rows(dc3.reshape(N_MIX_HEADS, s), 16), f_t, bias16, "b_gate_bwd")
    df = _pad_cols(jnp.transpose(df_t[:N_MIX_HEADS]), B_IN_PAD - 3 * MIX_W - MEM_W)
    dh = jnp.concatenate([dq, dk, dv, dqm, df], axis=1)
    dhb = _rope_cast(dh, tabs, 0, "b_cast_bwd")
    dw_in = _mm_tn(xb[None], dhb[None], "b_dwin")[0]
    dx = _mm_nt(dhb[None], w_in[None], "b_dx", res=dz)
    return dx, dw_in, jnp.transpose(dfb[:N_MIX_HEADS]), dw_kv, dw_out, dgain, dbias


def _weight_groups(w):
    b = {n: w[n].astype(BF) for n in WEIGHTS if n not in F32_COMM}
    return [
        [b["ffn1_w_gate_up"][0], b["ffn1_w_down"][0], w["ln_gain"], w["ln_bias"]],
        [b["a_w_in"][0], b["a_w_out"][0], b["mem_w_kv"][0]],
        [b["ffn2_w_gate_up"][0], b["ffn2_w_down"][0]],
        [b["ffn1_w_gate_up"][1], b["ffn1_w_down"][1]],
        [b["b_w_in"][0], b["b_w_out"][0], b["mem_w_kv"][1]],
        [b["ffn2_w_gate_up"][1], b["ffn2_w_down"][1]],
    ]


def _local_step(x, mem, target, fbias, get_w, put_g):
    s, d = x.shape
    tabs = _rope_tables(s, 1.0)
    tabs_neg = _rope_tables(s, -1.0)
    memb = mem.astype(BF)
    saved, wl = [], []
    cur, curb = x, x.astype(BF)
    ln_g = ln_b = None
    for i in range(DEPTH):
        g = get_w(3 * i, cur)
        if i == 0:
            ln_g, ln_b = (jnp.transpose(t, (1, 2, 0, 3)).reshape(DEPTH, 3, 1, d) for t in g[2:4])
        w1 = (g[0], g[1].reshape(N_DEV // 2, -1, d))
        cur, curb, s1 = _ffn_forward(cur, curb, w1[0], w1[1], ln_g[i, 0], ln_b[i, 0], f"l{i}_ffn1")
        g = get_w(3 * i + 1, cur)
        if i == 0:
            wm = (_cols_full(g[0]), g[2].reshape(d, -1), _cols_full(g[1]))
            cur, curb, s2 = _mixer_a_forward(cur, curb, memb, wm[0], wm[1], wm[2], ln_g[i, 1], ln_b[i, 1], tabs)
        else:
            wm = (_pack_b_in(g[0].reshape(d, -1)), g[2].reshape(d, -1), g[1].reshape(d, -1))
            cur, curb, s2 = _mixer_b_forward(cur, curb, memb, wm[0], fbias, wm[1], wm[2], ln_g[i, 1], ln_b[i, 1],
                                             tabs)
        g = get_w(3 * i + 2, cur)
        w3 = (g[0], g[1].reshape(N_DEV // 2, -1, d))
        cur, curb, s3 = _ffn_forward(cur, curb, w3[0], w3[1], ln_g[i, 2], ln_b[i, 2], f"l{i}_ffn2")
        saved.append((s1, s2, s3))
        wl.append((w1, wm, w3))

    dy, loss = _loss_head(cur, target, "loss_head")

    dgs = [[None] * 3 for _ in range(DEPTH)]
    dbs = [[None] * 3 for _ in range(DEPTH)]
    sent = ()
    for i in reversed(range(DEPTH)):
        s1, s2, s3 = saved[i]
        w1, wm, w3 = wl[i]
        dy, dgu, dd, dgs[i][2], dbs[i][2] = _ffn_backward(dy, s3, w3[0], w3[1], ln_g[i, 2], f"l{i}_ffn2", sent)
        sent = put_g(3 * i + 2, [dgu, dd])
        if i == 0:
            dy, dw_in, dw_kv, dw_out, dgs[i][1], dbs[i][1] = _mixer_a_backward(
                dy, s2, memb, wm[0], wm[1], wm[2], ln_g[i, 1], tabs_neg, sent)
            sent = put_g(1, [_cols_split(dw_in), _cols_split(dw_out), dw_kv.reshape(N_DEV, d // N_DEV, -1)])
        else:
            dy, dw_in, dfb, dw_kv, dw_out, dgs[i][1], dbs[i][1] = _mixer_b_backward(
                dy, s2, memb, wm[0], wm[1], wm[2], ln_g[i, 1], tabs, sent)
            sent = put_g(4, [_unpack_b_in(dw_in).reshape(N_DEV, d // N_DEV, -1),
                             dw_out.reshape(N_DEV, d // N_DEV, -1), dw_kv.reshape(N_DEV, d // N_DEV, -1),
                             jnp.broadcast_to(dfb[None], (N_DEV,) + dfb.shape)])
        dy, dgu, dd, dgs[i][0], dbs[i][0] = _ffn_backward(dy, s1, w1[0], w1[1], ln_g[i, 0], f"l{i}_ffn1", sent)
        if i == 0:
            ln_pieces = []
            for parts in (dgs, dbs):
                t = jnp.concatenate([parts[a][b] for a in range(DEPTH) for b in range(3)], axis=0)
                ln_pieces.append(jnp.transpose(t.reshape(DEPTH * 3, N_DEV, d // N_DEV), (1, 0, 2)))
            sent = put_g(0, [dgu, dd] + ln_pieces)
        else:
            sent = put_g(3, [dgu, dd])
    return loss, dy


WEIGHTS = ("ffn1_w_gate_up", "ffn1_w_down", "ffn2_w_gate_up", "ffn2_w_down", "ln_gain", "ln_bias", "mem_w_kv",
           "a_w_in", "a_w_out", "b_w_in", "b_forget_bias", "b_w_out")
F32_COMM = ("ln_gain", "ln_bias", "b_forget_bias")
GRAD_SLOTS = {
    "ffn1_w_gate_up": [(0, 0), (3, 0)], "ffn1_w_down": [(0, 1), (3, 1)],
    "ffn2_w_gate_up": [(2, 0), (5, 0)], "ffn2_w_down": [(2, 1), (5, 1)],
    "ln_gain": [(0, 2)], "ln_bias": [(0, 3)], "mem_w_kv": [(1, 2), (4, 2)],
    "a_w_in": [(1, 0)], "a_w_out": [(1, 1)], "b_w_in": [(4, 0)], "b_forget_bias": [(4, 3)], "b_w_out": [(4, 1)],
}


def kernel(x, mem, ffn1_w_gate_up, ffn1_w_down, ffn2_w_gate_up, ffn2_w_down, ln_gain, ln_bias, mem_w_kv, a_w_in, a_w_out, b_w_in, b_forget_bias, b_w_out, loss_target, m_ffn1_w_gate_up, m_ffn1_w_down, m_ffn2_w_gate_up, m_ffn2_w_down, m_ln_gain, m_ln_bias, m_mem_w_kv, m_a_w_in, m_a_w_out, m_b_w_in, m_b_forget_bias, m_b_w_out, v_ffn1_w_gate_up, v_ffn1_w_down, v_ffn2_w_gate_up, v_ffn2_w_down, v_ln_gain, v_ln_bias, v_mem_w_kv, v_a_w_in, v_a_w_out, v_b_w_in, v_b_forget_bias, v_b_w_out):
    w = dict(zip(WEIGHTS, (ffn1_w_gate_up, ffn1_w_down, ffn2_w_gate_up, ffn2_w_down, ln_gain, ln_bias, mem_w_kv,
                           a_w_in, a_w_out, b_w_in, b_forget_bias, b_w_out)))
    m = dict(zip(WEIGHTS, (m_ffn1_w_gate_up, m_ffn1_w_down, m_ffn2_w_gate_up, m_ffn2_w_down, m_ln_gain, m_ln_bias,
                           m_mem_w_kv, m_a_w_in, m_a_w_out, m_b_w_in, m_b_forget_bias, m_b_w_out)))
    v = dict(zip(WEIGHTS, (v_ffn1_w_gate_up, v_ffn1_w_down, v_ffn2_w_gate_up, v_ffn2_w_down, v_ln_gain, v_ln_bias,
                           v_mem_w_kv, v_a_w_in, v_a_w_out, v_b_w_in, v_b_forget_bias, v_b_w_out)))

    gathers = []
    for k, grp in enumerate(_weight_groups(w)):
        gathers.append(_xfer_start(grp, True, f"gather{k}_start", [_started(h) for h in gathers[-1:]]))
    exchanges = {}

    def get_w(k, after):
        behind = [after] + ([_started(h) for h in gathers] if k == 0 else [])
        return _xfer_wait(gathers[k], behind, True, f"gather{k}_wait")

    def put_g(k, pieces):
        exchanges[k] = _xfer_start(pieces, False, f"grads{k}_start")
        return (_started(exchanges[k]),)

    loss, grad_x = _local_step(x[0], mem[0], loss_target[0], b_forget_bias, get_w, put_g)
    loss = lax.psum(loss[0, 0], ("x", "y", "c"))

    outs, landed = {}, {}

    def adamw(names, after):
        for n in names:
            contribs = [landed[g][j] for g, j in GRAD_SLOTS[n]]
            view = (len(contribs),) + contribs[0].shape[1:]
            outs[n] = [t.reshape(w[n].shape) for t in _reduce_adamw(
                contribs, w[n].reshape(view), m[n].reshape(view), v[n].reshape(view), f"adamw_{n}")]
            after = outs[n][0]
        return after

    after = [grad_x, _started(exchanges[0])]
    for k in (5, 4, 3, 2, 1):
        landed[k] = _xfer_wait(exchanges[k], after, False, f"grads{k}_wait")
        after = [landed[k][0]]
    done = adamw(("ffn2_w_gate_up", "ffn2_w_down", "mem_w_kv", "a_w_in", "a_w_out", "b_w_in", "b_forget_bias",
                  "b_w_out"), None)
    landed[0] = _xfer_wait(exchanges[0], [done], False, "grads0_wait")
    adamw(("ffn1_w_gate_up", "ffn1_w_down", "ln_gain", "ln_bias"), None)
    return (loss, grad_x[None], *[outs[n][0] for n in WEIGHTS], *[outs[n][1] for n in WEIGHTS],
            *[outs[n][2] for n in WEIGHTS], *[outs[n][3] for n in WEIGHTS])
```

```python
import functools

import jax
import jax.numpy as jnp
from jax import lax
from jax.experimental import pallas as pl
from jax.experimental.pallas import tpu as pltpu

F32 = jnp.float32
BF = jnp.bfloat16
MESH_ID = pl.DeviceIdType.MESH

N_DEV = 8
DEPTH = 2
HEAD_DIM = 64
LANES = 128
N_MIX_HEADS = 12
N_MEM_HEADS = 4
MIX_W = N_MIX_HEADS * HEAD_DIM
MEM_W = N_MEM_HEADS * HEAD_DIM
N_GROUPS = 3
GROUP_W = MIX_W // N_GROUPS
BLOCK = 128
ROT_HALF = 8
ROPE_THETA = 500000.0
ALPHA = (2 * DEPTH) ** 0.25
LN_EPS = 1e-5
SCALE = HEAD_DIM ** -0.5
NEG = -1e30
B_IN_PAD = 2688
ADAM_LR, ADAM_B1, ADAM_B2, ADAM_EPS, ADAM_WD, ADAM_STEP = 0.001, 0.9, 0.999, 1e-08, 0.01, 10
VMEM_LIMIT = 56 * 1024 * 1024


def _cp(*sem):
    return pltpu.CompilerParams(dimension_semantics=sem, vmem_limit_bytes=VMEM_LIMIT)


def _dot(a, b):
    return jnp.dot(a, b, preferred_element_type=F32)


def _dot_nt(a, b):
    return lax.dot_general(a, b, (((1,), (1,)), ((), ())), preferred_element_type=F32)


def _dot_tn(a, b):
    return lax.dot_general(a, b, (((0,), (0,)), ((), ())), preferred_element_type=F32)


def _sigmoid(x):
    return 1.0 / (1.0 + jnp.exp(-x))


def _tile(n, cap=1024):
    if n <= cap:
        return n
    best = LANES
    for t in range(LANES, cap + 1, LANES):
        if n % t == 0:
            best = t
    return best


def _rows(s, cap=512):
    return s if s <= cap else cap


def _mm_nn(a, b, out_dtype, name):
    m, k = a.shape
    n = b.shape[1]
    tm, tn = _rows(m), _tile(n)

    def body(a_ref, b_ref, o_ref):
        o_ref[...] = _dot(a_ref[...], b_ref[...]).astype(o_ref.dtype)

    return pl.pallas_call(
        body, name=name, grid=(n // tn, m // tm),
        in_specs=[pl.BlockSpec((tm, k), lambda j, i: (i, 0)), pl.BlockSpec((k, tn), lambda j, i: (0, j))],
        out_specs=pl.BlockSpec((tm, tn), lambda j, i: (i, j)),
        out_shape=jax.ShapeDtypeStruct((m, n), out_dtype),
        compiler_params=_cp("parallel", "parallel"))(a, b)


def _mm_tn(a, b, name, out_dtype=BF):
    na, s, m = a.shape
    nb, _, n = b.shape
    no = max(na, nb)
    ts, tn = _rows(s), _tile(n)
    ns = s // ts

    def body(a_ref, b_ref, o_ref, acc):
        k = pl.program_id(2)

        @pl.when(k == 0)
        def _():
            acc[...] = jnp.zeros_like(acc)

        acc[...] += _dot_tn(a_ref[...], b_ref[...])

        @pl.when(k == ns - 1)
        def _():
            o_ref[...] = acc[...].astype(o_ref.dtype)

    return pl.pallas_call(
        body, name=name, grid=(no, n // tn, ns),
        in_specs=[pl.BlockSpec((None, ts, m), lambda j, c, k: (j if na > 1 else 0, k, 0)),
                  pl.BlockSpec((None, ts, tn), lambda j, c, k: (j if nb > 1 else 0, k, c))],
        out_specs=pl.BlockSpec((None, m, tn), lambda j, c, k: (j, 0, c)),
        out_shape=jax.ShapeDtypeStruct((no, m, n), out_dtype),
        scratch_shapes=[pltpu.VMEM((m, tn), F32)],
        compiler_params=_cp("parallel", "parallel", "arbitrary"))(a, b)


def _mm_nt(dh, w, name, res=None, out_dtype=F32):
    nc, s, kc = dh.shape
    d = w.shape[1]
    ts = _rows(s)
    has_res = res is not None

    def body(*refs):
        if has_res:
            dh_ref, w_ref, r_ref, o_ref, acc = refs
        else:
            dh_ref, w_ref, o_ref, acc = refs
        j = pl.program_id(1)

        @pl.when(j == 0)
        def _():
            acc[...] = jnp.zeros_like(acc)

        acc[...] += _dot_nt(dh_ref[...], w_ref[...])

        @pl.when(j == nc - 1)
        def _():
            out = acc[...]
            if has_res:
                out = out + ALPHA * r_ref[...]
            o_ref[...] = out.astype(o_ref.dtype)

    in_specs = [pl.BlockSpec((None, ts, kc), lambda i, j: (j, i, 0)),
                pl.BlockSpec((None, d, kc), lambda i, j: (j, 0, 0))]
    args = [dh, w]
    if has_res:
        in_specs.append(pl.BlockSpec((ts, d), lambda i, j: (i, 0)))
        args.append(res)
    return pl.pallas_call(
        body, name=name, grid=(s // ts, nc), in_specs=in_specs,
        out_specs=pl.BlockSpec((ts, d), lambda i, j: (i, 0)),
        out_shape=jax.ShapeDtypeStruct((s, d), out_dtype),
        scratch_shapes=[pltpu.VMEM((ts, d), F32)],
        compiler_params=_cp("parallel", "arbitrary"))(*args)


def _mm_res_ln(a, w, x, gain, bias, fscale, name):
    nc, s, kc = a.shape
    d = w.shape[2]
    ts = _rows(s)

    def body(a_ref, w_ref, x_ref, g_ref, b_ref, y_ref, yb_ref, xh_ref, r_ref, acc):
        j = pl.program_id(1)

        @pl.when(j == 0)
        def _():
            acc[...] = jnp.zeros_like(acc)

        acc[...] += _dot(a_ref[...], w_ref[...])

        @pl.when(j == nc - 1)
        def _():
            z = ALPHA * x_ref[...] + fscale * acc[...]
            mu = jnp.mean(z, axis=-1, keepdims=True)
            zc = z - mu
            var = jnp.mean(zc * zc, axis=-1, keepdims=True)
            r = lax.rsqrt(var + LN_EPS)
            xh = zc * r
            y = xh * g_ref[...] + b_ref[...]
            y_ref[...] = y
            yb_ref[...] = y.astype(BF)
            xh_ref[...] = xh
            r_ref[...] = r

    row = pl.BlockSpec((ts, d), lambda i, j: (i, 0))
    vec = pl.BlockSpec((1, d), lambda i, j: (0, 0))
    return pl.pallas_call(
        body, name=name, grid=(s // ts, nc),
        in_specs=[pl.BlockSpec((None, ts, kc), lambda i, j: (j, i, 0)),
                  pl.BlockSpec((None, kc, d), lambda i, j: (j, 0, 0)), row, vec, vec],
        out_specs=[row, row, row, pl.BlockSpec((ts, 1), lambda i, j: (i, 0))],
        out_shape=[jax.ShapeDtypeStruct((s, d), F32), jax.ShapeDtypeStruct((s, d), BF),
                   jax.ShapeDtypeStruct((s, d), F32), jax.ShapeDtypeStruct((s, 1), F32)],
        scratch_shapes=[pltpu.VMEM((ts, d), F32)],
        compiler_params=_cp("parallel", "arbitrary"))(a, w, x, gain, bias)


def _ln_bwd(dy, xh, rstd, gain, fscale, name, after=()):
    s, d = dy.shape
    ts = _rows(s)
    na = len(after)

    def body(*refs):
        dy_ref, xh_ref, r_ref, g_ref = refs[:4]
        dz_ref, dzb_ref, dg_ref, db_ref = refs[4 + na:]
        i = pl.program_id(0)
        dyv = dy_ref[...]
        xhv = xh_ref[...]
        dxh = dyv * g_ref[...]
        m1 = jnp.mean(dxh, axis=-1, keepdims=True)
        m2 = jnp.mean(dxh * xhv, axis=-1, keepdims=True)
        dz = r_ref[...] * (dxh - m1 - xhv * m2)
        dz_ref[...] = dz
        dzb_ref[...] = (fscale * dz).astype(BF)

        @pl.when(i == 0)
        def _():
            dg_ref[...] = jnp.zeros_like(dg_ref)
            db_ref[...] = jnp.zeros_like(db_ref)

        dg_ref[...] += jnp.sum(dyv * xhv, axis=0, keepdims=True)
        db_ref[...] += jnp.sum(dyv, axis=0, keepdims=True)

    row = pl.BlockSpec((ts, d), lambda i: (i, 0))
    vec = pl.BlockSpec((1, d), lambda i: (0, 0))
    return pl.pallas_call(
        body, name=name, grid=(s // ts,),
        in_specs=[row, row, pl.BlockSpec((ts, 1), lambda i: (i, 0)), vec] + [pl.BlockSpec(memory_space=pl.ANY)] * na,
        out_specs=[row, row, vec, vec],
        out_shape=[jax.ShapeDtypeStruct((s, d), F32), jax.ShapeDtypeStruct((s, d), BF),
                   jax.ShapeDtypeStruct((1, d), F32), jax.ShapeDtypeStruct((1, d), F32)],
        compiler_params=_cp("arbitrary"))(dy, xh, rstd, gain, *after)


def _ffn_up(xb, wgu, name):
    s, d = xb.shape
    c = wgu.shape[2]
    nch = wgu.shape[0] // 2
    ts = _rows(s)
    w4 = wgu.reshape(2, nch, d, c)

    def body(x_ref, w_ref, gu_ref, a_ref):
        x = x_ref[...]
        g = _dot(x, w_ref[0])
        u = _dot(x, w_ref[1])
        gu_ref[0] = g.astype(BF)
        gu_ref[1] = u.astype(BF)
        a_ref[...] = (g * _sigmoid(g) * u).astype(BF)

    return pl.pallas_call(
        body, name=name, grid=(nch, s // ts),
        in_specs=[pl.BlockSpec((ts, d), lambda j, i: (i, 0)),
                  pl.BlockSpec((2, None, d, c), lambda j, i: (0, j, 0, 0))],
        out_specs=[pl.BlockSpec((2, None, ts, c), lambda j, i: (0, j, i, 0)),
                   pl.BlockSpec((None, ts, c), lambda j, i: (j, i, 0))],
        out_shape=[jax.ShapeDtypeStruct((2, nch, s, c), BF), jax.ShapeDtypeStruct((nch, s, c), BF)],
        compiler_params=_cp("parallel", "parallel"))(xb, w4)


def _ffn_bwd_act(dzb, wd4, gu, name):
    s, d = dzb.shape
    nch, c = wd4.shape[0], wd4.shape[1]
    ts = _rows(s)

    def body(dz_ref, w_ref, gu_ref, dh_ref):
        da = _dot_nt(dz_ref[...], w_ref[...])
        g = gu_ref[0].astype(F32)
        u = gu_ref[1].astype(F32)
        sg = _sigmoid(g)
        dh_ref[0] = (da * u * sg * (1.0 + g * (1.0 - sg))).astype(BF)
        dh_ref[1] = (da * g * sg).astype(BF)

    return pl.pallas_call(
        body, name=name, grid=(nch, s // ts),
        in_specs=[pl.BlockSpec((ts, d), lambda j, i: (i, 0)),
                  pl.BlockSpec((None, c, d), lambda j, i: (j, 0, 0)),
                  pl.BlockSpec((2, None, ts, c), lambda j, i: (0, j, i, 0))],
        out_specs=pl.BlockSpec((2, None, ts, c), lambda j, i: (0, j, i, 0)),
        out_shape=jax.ShapeDtypeStruct((2, nch, s, c), BF),
        compiler_params=_cp("parallel", "parallel"))(dzb, wd4, gu)


def _rope_tables(s, sign):
    pos = jnp.arange(s, dtype=F32)
    inv_freq = 1.0 / (ROPE_THETA ** (jnp.arange(ROT_HALF, dtype=F32) / ROT_HALF))
    ang = pos[:, None] * inv_freq[None, :]
    cos, sin = jnp.cos(ang), jnp.sin(ang) * sign
    one = jnp.ones((s, HEAD_DIM - 2 * ROT_HALF), F32)
    zero = jnp.zeros((s, HEAD_DIM - 2 * ROT_HALF), F32)
    zh = jnp.zeros((s, ROT_HALF), F32)
    cos_f = jnp.concatenate([cos, cos, one], axis=1)
    sin_a = jnp.concatenate([-sin, zh, zero], axis=1)
    sin_b = jnp.concatenate([zh, sin, zero], axis=1)
    rep = LANES // HEAD_DIM
    return tuple(jnp.tile(t, (1, rep)) for t in (cos_f, sin_a, sin_b))


def _rope_cast(parts, tabs, n_rope, name):
    s = parts[0].shape[0]
    widths = [p.shape[1] for p in parts]
    n = sum(widths)
    npart = len(parts)
    ts = _rows(s, 256)

    def body(*refs):
        part_refs = refs[:npart]
        c_ref, sa_ref, sb_ref, o_ref = refs[npart:]
        col = 0
        for ref, w in zip(part_refs, widths):
            for j in range(w // LANES):
                t = ref[:, j * LANES:(j + 1) * LANES]
                if col < n_rope:
                    t = (t * c_ref[...] + pltpu.roll(t, LANES - ROT_HALF, 1) * sa_ref[...]
                         + pltpu.roll(t, ROT_HALF, 1) * sb_ref[...])
                o_ref[:, col * LANES:(col + 1) * LANES] = t.astype(BF)
                col += 1

    tab = pl.BlockSpec((ts, LANES), lambda i: (i, 0))
    return pl.pallas_call(
        body, name=name, grid=(s // ts,),
        in_specs=[pl.BlockSpec((ts, w), lambda i: (i, 0)) for w in widths] + [tab, tab, tab],
        out_specs=pl.BlockSpec((ts, n), lambda i: (i, 0)),
        out_shape=jax.ShapeDtypeStruct((s, n), BF),
        compiler_params=_cp("parallel"))(*parts, *tabs)


def _head_masks():
    lane = lax.broadcasted_iota(jnp.int32, (1, LANES), 1)
    return [lane < HEAD_DIM, lane >= HEAD_DIM]


def _sel(mask, v):
    return jnp.where(mask, v, jnp.zeros_like(v))


def _pick(mask, wide, fill):
    return jnp.max(jnp.where(mask, wide, fill), axis=1, keepdims=True)


def _band_masks(has_other, prev):
    qi = lax.broadcasted_iota(jnp.int32, (BLOCK, BLOCK), 0)
    kj = lax.broadcasted_iota(jnp.int32, (BLOCK, BLOCK), 1)
    if prev:
        return kj >= qi + jnp.where(has_other, 0, BLOCK)
    return kj <= qi


def _band_fwd(q3, k3, v3, name):
    ng, s, w = q3.shape
    nb = s // BLOCK
    npair = w // LANES

    def body(q_ref, kc_ref, kp_ref, vc_ref, vp_ref, o_ref, l_ref):
        g = pl.program_id(0)
        b = pl.program_id(2)
        nbl = jnp.right_shift(nb, 2 * g)
        has_prev = jnp.bitwise_and(b, nbl - 1) != 0
        mc = _band_masks(None, False)
        mp = _band_masks(has_prev, True)
        q, kc, kp, vc, vp = q_ref[...], kc_ref[...], kp_ref[...], vc_ref[...], vp_ref[...]
        hm = _head_masks()
        o = jnp.zeros((BLOCK, LANES), F32)
        lse_w = jnp.zeros((BLOCK, LANES), F32)
        for h in range(2):
            qh = _sel(hm[h], q)
            sc = jnp.where(mc, _dot_nt(qh, kc) * SCALE, NEG)
            sp = jnp.where(mp, _dot_nt(qh, kp) * SCALE, NEG)
            m = jnp.maximum(jnp.max(sc, axis=1, keepdims=True), jnp.max(sp, axis=1, keepdims=True))
            pc = jnp.exp(sc - m)
            pp = jnp.exp(sp - m)
            l = jnp.sum(pc, axis=1, keepdims=True) + jnp.sum(pp, axis=1, keepdims=True)
            oh = _dot(pc.astype(BF), _sel(hm[h], vc)) + _dot(pp.astype(BF), _sel(hm[h], vp))
            o = o + oh / l
            lse_w = jnp.where(hm[h], m + jnp.log(l), lse_w)
        o_ref[...] = o
        l_ref[...] = lse_w

    cur = pl.BlockSpec((None, BLOCK, LANES), lambda g, p, b: (g, b, p))
    prv = pl.BlockSpec((None, BLOCK, LANES), lambda g, p, b: (g, jnp.maximum(b - 1, 0), p))
    return pl.pallas_call(
        body, name=name, grid=(ng, npair, nb),
        in_specs=[cur, cur, prv, cur, prv], out_specs=[cur, cur],
        out_shape=[jax.ShapeDtypeStruct((ng, s, w), F32), jax.ShapeDtypeStruct((ng, s, w), F32)],
        compiler_params=_cp("parallel", "parallel", "parallel"))(q3, k3, k3, v3, v3)


def _band_combine(o3, l3, name):
    ng, s, w = o3.shape
    ts = _rows(s)

    def body(o_ref, l_ref, oa_ref, lt_ref):
        ls = [l_ref[g] for g in range(ng)]
        m = functools.reduce(jnp.maximum, ls)
        es = [jnp.exp(l - m) for l in ls]
        den = functools.reduce(lambda a, b: a + b, es)
        num = functools.reduce(lambda a, b: a + b, [es[g] * o_ref[g] for g in range(ng)])
        oa_ref[...] = (num / den).astype(BF)
        lt_ref[...] = m + jnp.log(den)

    blk3 = pl.BlockSpec((ng, ts, w), lambda i: (0, i, 0))
    blk = pl.BlockSpec((ts, w), lambda i: (i, 0))
    return pl.pallas_call(
        body, name=name, grid=(s // ts,), in_specs=[blk3, blk3], out_specs=[blk, blk],
        out_shape=[jax.ShapeDtypeStruct((s, w), BF), jax.ShapeDtypeStruct((s, w), F32)],
        compiler_params=_cp("parallel"))(o3, l3)


def _band_bwd(q3, k3, v3, do3, oa3, lt3, name):
    ng, s, w = q3.shape
    nb = s // BLOCK
    npair = w // LANES

    def body(q_ref, qn_ref, kc_ref, kp_ref, vc_ref, vp_ref, do_ref, don_ref, oa_ref, oan_ref, lt_ref, ltn_ref,
             dq_ref, dk_ref, dv_ref):
        g = pl.program_id(0)
        b = pl.program_id(2)
        nbl = jnp.right_shift(nb, 2 * g)
        has_prev = jnp.bitwise_and(b, nbl - 1) != 0
        has_next = jnp.bitwise_and(b + 1, nbl - 1) != 0
        mc = _band_masks(None, False)
        mp = _band_masks(has_prev, True)
        mn = _band_masks(has_next, True)
        q, qn, kc, kp, vc, vp = q_ref[...], qn_ref[...], kc_ref[...], kp_ref[...], vc_ref[...], vp_ref[...]
        do, don = do_ref[...], don_ref[...]
        dd = do.astype(F32) * oa_ref[...].astype(F32)
        ddn = don.astype(F32) * oan_ref[...].astype(F32)
        lt, ltn = lt_ref[...], ltn_ref[...]
        hm = _head_masks()
        dq = jnp.zeros((BLOCK, LANES), F32)
        dk = jnp.zeros((BLOCK, LANES), F32)
        dv = jnp.zeros((BLOCK, LANES), F32)
        for h in range(2):
            qh, doh = _sel(hm[h], q), _sel(hm[h], do)
            qnh, donh = _sel(hm[h], qn), _sel(hm[h], don)
            kch, kph = _sel(hm[h], kc), _sel(hm[h], kp)
            lse = _pick(hm[h], lt, NEG)
            lsen = _pick(hm[h], ltn, NEG)
            dsum = jnp.sum(_sel(hm[h], dd), axis=1, keepdims=True)
            dsumn = jnp.sum(_sel(hm[h], ddn), axis=1, keepdims=True)
            pc = jnp.exp(jnp.where(mc, _dot_nt(qh, kc) * SCALE, NEG) - lse)
            pp = jnp.exp(jnp.where(mp, _dot_nt(qh, kp) * SCALE, NEG) - lse)
            dsc = pc * (_dot_nt(doh, vc) - dsum)
            dsp = pp * (_dot_nt(doh, vp) - dsum)
            dq = dq + SCALE * (_dot(dsc.astype(BF), kch) + _dot(dsp.astype(BF), kph))
            pn = jnp.exp(jnp.where(mn, _dot_nt(qnh, kc) * SCALE, NEG) - lsen)
            dsn = pn * (_dot_nt(donh, vc) - dsumn)
            dk = dk + SCALE * (_dot_tn(dsc.astype(BF), qh) + _dot_tn(dsn.astype(BF), qnh))
            dv = dv + _dot_tn(pc.astype(BF), doh) + _dot_tn(pn.astype(BF), donh)
        dq_ref[...] = dq
        dk_ref[...] = dk
        dv_ref[...] = dv

    cur = pl.BlockSpec((None, BLOCK, LANES), lambda g, p, b: (g, b, p))
    prv = pl.BlockSpec((None, BLOCK, LANES), lambda g, p, b: (g, jnp.maximum(b - 1, 0), p))
    nxt = pl.BlockSpec((None, BLOCK, LANES), lambda g, p, b: (g, jnp.minimum(b + 1, nb - 1), p))
    out = jax.ShapeDtypeStruct((ng, s, w), F32)
    return pl.pallas_call(
        body, name=name, grid=(ng, npair, nb),
        in_specs=[cur, nxt, cur, prv, cur, prv, cur, nxt, cur, nxt, cur, nxt],
        out_specs=[cur, cur, cur], out_shape=[out, out, out],
        compiler_params=_cp("parallel", "parallel", "parallel"))(
            q3, q3, k3, k3, v3, v3, do3, do3, oa3, oa3, lt3, lt3)


def _mem_fwd(hb, q_blk0, kv, name):
    s = hb.shape[0]
    m = kv.shape[0]
    tq = _rows(s)
    npair = MEM_W // LANES

    def body(q_ref, k_ref, v_ref, o_ref, l_ref):
        q, k, v = q_ref[...], k_ref[...], v_ref[...]
        hm = _head_masks()
        o = jnp.zeros((tq, LANES), F32)
        lse_w = jnp.zeros((tq, LANES), F32)
        for h in range(2):
            sc = _dot_nt(_sel(hm[h], q), k) * SCALE
            mx = jnp.max(sc, axis=1, keepdims=True)
            p = jnp.exp(sc - mx)
            l = jnp.sum(p, axis=1, keepdims=True)
            o = o + _dot(p.astype(BF), _sel(hm[h], v)) / l
            lse_w = jnp.where(hm[h], mx + jnp.log(l), lse_w)
        o_ref[...] = o.astype(BF)
        l_ref[...] = lse_w

    blk = pl.BlockSpec((tq, LANES), lambda p, i: (i, p))
    return pl.pallas_call(
        body, name=name, grid=(npair, s // tq),
        in_specs=[pl.BlockSpec((tq, LANES), lambda p, i: (i, q_blk0 + p)),
                  pl.BlockSpec((m, LANES), lambda p, i: (0, p)),
                  pl.BlockSpec((m, LANES), lambda p, i: (0, npair + p))],
        out_specs=[blk, blk],
        out_shape=[jax.ShapeDtypeStruct((s, MEM_W), BF), jax.ShapeDtypeStruct((s, MEM_W), F32)],
        compiler_params=_cp("parallel", "parallel"))(hb, kv, kv)


def _mem_bwd(hb, q_blk0, kv, dcat, cat, o_blk0, lse, name):
    s = hb.shape[0]
    m = kv.shape[0]
    tq = _rows(s)
    npair = MEM_W // LANES

    def body(q_ref, k_ref, v_ref, do_ref, o_ref, l_ref, dq_ref, dk_ref, dv_ref):
        i = pl.program_id(1)

        @pl.when(i == 0)
        def _():
            dk_ref[...] = jnp.zeros_like(dk_ref)
            dv_ref[...] = jnp.zeros_like(dv_ref)

        q, k, v, do = q_ref[...], k_ref[...], v_ref[...], do_ref[...]
        dd = do.astype(F32) * o_ref[...].astype(F32)
        lt = l_ref[...]
        hm = _head_masks()
        dq = jnp.zeros((tq, LANES), F32)
        dk = jnp.zeros((m, LANES), F32)
        dv = jnp.zeros((m, LANES), F32)
        for h in range(2):
            qh, doh = _sel(hm[h], q), _sel(hm[h], do)
            p = jnp.exp(_dot_nt(qh, k) * SCALE - _pick(hm[h], lt, NEG))
            ds = p * (_dot_nt(doh, v) - jnp.sum(_sel(hm[h], dd), axis=1, keepdims=True))
            dq = dq + SCALE * _dot(ds.astype(BF), _sel(hm[h], k))
            dk = dk + SCALE * _dot_tn(ds.astype(BF), qh)
            dv = dv + _dot_tn(p.astype(BF), doh)
        dq_ref[...] = dq
        dk_ref[...] += dk
        dv_ref[...] += dv

    row = pl.BlockSpec((tq, LANES), lambda p, i: (i, p))
    orow = pl.BlockSpec((tq, LANES), lambda p, i: (i, o_blk0 + p))
    acc = pl.BlockSpec((m, LANES), lambda p, i: (0, p))
    return pl.pallas_call(
        body, name=name, grid=(npair, s // tq),
        in_specs=[pl.BlockSpec((tq, LANES), lambda p, i: (i, q_blk0 + p)),
                  pl.BlockSpec((m, LANES), lambda p, i: (0, p)),
                  pl.BlockSpec((m, LANES), lambda p, i: (0, npair + p)), orow, orow, row],
        out_specs=[row, acc, acc],
        out_shape=[jax.ShapeDtypeStruct((s, MEM_W), F32), jax.ShapeDtypeStruct((m, MEM_W), F32),
                   jax.ShapeDtypeStruct((m, MEM_W), F32)],
        compiler_params=_cp("parallel", "arbitrary"))(hb, kv, kv, dcat, cat, lse)


def _gate_fwd(f_t, bias, name):
    hp, s = f_t.shape
    nblk = s // LANES

    def body(f_ref, b_ref, c_ref):
        lane = lax.broadcasted_iota(jnp.int32, (hp, LANES), 1)

        def step(i, carry):
            off = pl.multiple_of(i * LANES, LANES)
            x = f_ref[:, pl.ds(off, LANES)] + b_ref[...]
            acc = jnp.minimum(x, 0.0) - jnp.log(1.0 + jnp.exp(-jnp.abs(x)))
            sh = 1
            while sh < LANES:
                acc = acc + jnp.where(lane >= sh, pltpu.roll(acc, sh, 1), 0.0)
                sh *= 2
            acc = acc + carry
            c_ref[:, pl.ds(off, LANES)] = acc
            return acc[:, LANES - 1:LANES]

        lax.fori_loop(0, nblk, step, jnp.zeros((hp, 1), F32))

    vm = pl.BlockSpec(memory_space=pltpu.VMEM)
    return pl.pallas_call(body, name=name, in_specs=[vm, vm], out_specs=vm,
                          out_shape=jax.ShapeDtypeStruct((hp, s), F32),
                          compiler_params=pltpu.CompilerParams(vmem_limit_bytes=VMEM_LIMIT))(f_t, bias)


def _gate_bwd(dc_t, f_t, bias, name):
    hp, s = f_t.shape
    nblk = s // LANES

    def body(dc_ref, f_ref, b_ref, df_ref, db_ref):
        lane = lax.broadcasted_iota(jnp.int32, (hp, LANES), 1)

        def step(t, carry):
            suffix, dbias = carry
            off = pl.multiple_of((nblk - 1 - t) * LANES, LANES)
            acc = dc_ref[:, pl.ds(off, LANES)]
            sh = 1
            while sh < LANES:
                acc = acc + jnp.where(lane < LANES - sh, pltpu.roll(acc, LANES - sh, 1), 0.0)
                sh *= 2
            acc = acc + suffix
            x = f_ref[:, pl.ds(off, LANES)] + b_ref[...]
            df = acc * _sigmoid(-x)
            df_ref[:, pl.ds(off, LANES)] = df
            return acc[:, 0:1], dbias + jnp.sum(df, axis=1, keepdims=True)

        _, dbias = lax.fori_loop(0, nblk, step, (jnp.zeros((hp, 1), F32), jnp.zeros((hp, 1), F32)))
        db_ref[...] = dbias

    vm = pl.BlockSpec(memory_space=pltpu.VMEM)
    return pl.pallas_call(body, name=name, in_specs=[vm, vm, vm], out_specs=[vm, vm],
                          out_shape=[jax.ShapeDtypeStruct((hp, s), F32), jax.ShapeDtypeStruct((hp, 1), F32)],
                          compiler_params=pltpu.CompilerParams(vmem_limit_bytes=VMEM_LIMIT))(dc_t, f_t, bias)


def _wide(rep, width):
    return jnp.tile(rep, (1, width // LANES))


def _fold(t):
    part = t[:, :LANES]
    for c in range(1, t.shape[1] // LANES):
        part = part + t[:, c * LANES:(c + 1) * LANES]
    return part


def _fox_logits(q, k, cq_rep, ck_row, mask, hmask):
    s = _dot_nt(_sel(hmask, q), k) + (_wide(cq_rep, ck_row.shape[1]) - ck_row)
    if mask is not None:
        s = jnp.where(mask, s, NEG)
    return s


def _diag_mask(t):
    return lax.broadcasted_iota(jnp.int32, (t, t), 1) <= lax.broadcasted_iota(jnp.int32, (t, t), 0)


def _fox_fwd(hb, c_rep, c_t3, name):
    s = hb.shape[0]
    npair = MIX_W // LANES
    tq = tk = _rows(s)
    nq = s // tq

    def body(q_ref, k_ref, v_ref, cq_ref, ck_ref, o_ref, l_ref, m_s, l_s, acc):
        qi = pl.program_id(1)
        kj = pl.program_id(2)
        hm = _head_masks()

        @pl.when(kj == 0)
        def _():
            m_s[...] = jnp.full_like(m_s, NEG)
            l_s[...] = jnp.zeros_like(l_s)
            acc[...] = jnp.zeros_like(acc)

        def step(mask):
            q, k, v = q_ref[...] * SCALE, k_ref[...], v_ref[...]
            ck = ck_ref[...]
            for h in range(2):
                sc = _fox_logits(q, k, cq_ref[h], ck[h:h + 1, :], mask, hm[h])
                m_old = m_s[h]
                m_new = jnp.maximum(m_old, jnp.max(sc, axis=1, keepdims=True))
                pr = jnp.exp(sc - _wide(m_new, tk))
                corr = jnp.exp(m_old - m_new)
                l_s[h] = l_s[h] * corr + _fold(pr)
                acc[h] = acc[h] * corr + _dot(pr.astype(BF), _sel(hm[h], v))
                m_s[h] = m_new

        @pl.when(kj < qi)
        def _():
            step(None)

        @pl.when(kj == qi)
        def _():
            step(_diag_mask(tq))
            outs = []
            for h in range(2):
                den = jnp.sum(l_s[h], axis=1, keepdims=True)
                outs.append(acc[h] / den)
                l_ref[h] = m_s[h] + jnp.log(den)
            o_ref[...] = jnp.where(hm[0], outs[0], outs[1]).astype(BF)

    def kv_map(off):
        return lambda p, i, j: (jnp.minimum(j, i), off + p)

    blk = pl.BlockSpec((tq, LANES), lambda p, i, j: (i, p))
    return pl.pallas_call(
        body, name=name, grid=(npair, nq, nq),
        in_specs=[blk, pl.BlockSpec((tk, LANES), kv_map(npair)), pl.BlockSpec((tk, LANES), kv_map(2 * npair)),
                  pl.BlockSpec((2, tq, LANES), lambda p, i, j: (p, i, 0)),
                  pl.BlockSpec((None, 2, tk), lambda p, i, j: (p, 0, jnp.minimum(j, i)))],
        out_specs=[blk, pl.BlockSpec((2, tq, LANES), lambda p, i, j: (p, i, 0))],
        out_shape=[jax.ShapeDtypeStruct((s, MIX_W), BF), jax.ShapeDtypeStruct((2 * npair, s, LANES), F32)],
        scratch_shapes=[pltpu.VMEM((2, tq, LANES), F32), pltpu.VMEM((2, tq, LANES), F32),
                        pltpu.VMEM((2, tq, LANES), F32)],
        compiler_params=_cp("parallel", "parallel", "arbitrary"))(hb, hb, hb, c_rep, c_t3)


def _fox_dsum(hb, dcat, lse, c_rep, c_t3, name):
    s = hb.shape[0]
    npair = MIX_W // LANES
    tq = tk = _rows(s)
    nq = s // tq

    def body(q_ref, k_ref, v_ref, do_ref, l_ref, cq_ref, ck_ref, d_ref, acc):
        qi = pl.program_id(1)
        kj = pl.program_id(2)
        hm = _head_masks()

        @pl.when(kj == 0)
        def _():
            acc[...] = jnp.zeros_like(acc)

        def step(mask):
            q, k, v, do = q_ref[...] * SCALE, k_ref[...], v_ref[...], do_ref[...]
            ck = ck_ref[...]
            for h in range(2):
                pr = jnp.exp(_fox_logits(q, k, cq_ref[h], ck[h:h + 1, :], mask, hm[h]) - _wide(l_ref[h], tk))
                acc[h] += _fold(pr * _dot_nt(_sel(hm[h], do), v))

        @pl.when(kj < qi)
        def _():
            step(None)

        @pl.when(kj == qi)
        def _():
            step(_diag_mask(tq))
            for h in range(2):
                d_ref[h] = jnp.broadcast_to(jnp.sum(acc[h], axis=1, keepdims=True), (tq, LANES))

    def kv_map(off):
        return lambda p, i, j: (jnp.minimum(j, i), off + p)

    blk = pl.BlockSpec((tq, LANES), lambda p, i, j: (i, p))
    rep = pl.BlockSpec((2, tq, LANES), lambda p, i, j: (p, i, 0))
    return pl.pallas_call(
        body, name=name, grid=(npair, nq, nq),
        in_specs=[blk, pl.BlockSpec((tk, LANES), kv_map(npair)), pl.BlockSpec((tk, LANES), kv_map(2 * npair)),
                  blk, rep, rep, pl.BlockSpec((None, 2, tk), lambda p, i, j: (p, 0, jnp.minimum(j, i)))],
        out_specs=rep, out_shape=jax.ShapeDtypeStruct((2 * npair, s, LANES), F32),
        scratch_shapes=[pltpu.VMEM((2, tq, LANES), F32)],
        compiler_params=_cp("parallel", "parallel", "arbitrary"))(hb, hb, hb, dcat, lse, c_rep, c_t3)


def _fox_bwd(hb, dcat, dsum, lse, c_rep, c_t3, name):
    s = hb.shape[0]
    npair = MIX_W // LANES
    tq = tk = _rows(s)
    nq = s // tq

    def body(q_ref, k_ref, v_ref, do_ref, d_ref, l_ref, cq_ref, ck_ref, dq_ref, dk_ref, dv_ref, dc_ref):
        kj = pl.program_id(1)
        qi = pl.program_id(2)
        hm = _head_masks()

        @pl.when(qi == 0)
        def _():
            dk_ref[...] = jnp.zeros_like(dk_ref)
            dv_ref[...] = jnp.zeros_like(dv_ref)
            dc_ref[...] = jnp.zeros_like(dc_ref)

        @pl.when((qi == 0) & (kj == 0))
        def _():
            dq_ref[...] = jnp.zeros_like(dq_ref)

        def step(mask):
            q, k, v, do = q_ref[...] * SCALE, k_ref[...], v_ref[...], do_ref[...]
            ck = ck_ref[...]
            dq = jnp.zeros((tq, LANES), F32)
            dk = jnp.zeros((tk, LANES), F32)
            dv = jnp.zeros((tk, LANES), F32)
            dcs = []
            for h in range(2):
                qh, doh = _sel(hm[h], q), _sel(hm[h], do)
                pr = jnp.exp(_fox_logits(q, k, cq_ref[h], ck[h:h + 1, :], mask, hm[h]) - _wide(l_ref[h], tk))
                ds = pr * (_dot_nt(doh, v) - _wide(d_ref[h], tk))
                dsb = ds.astype(BF)
                dq = dq + _dot(dsb, _sel(hm[h], k))
                dk = dk + _dot_tn(dsb, qh)
                dv = dv + _dot_tn(pr.astype(BF), doh)
                dcs.append(jnp.sum(ds, axis=0, keepdims=True))
            rows = pl.ds(pl.multiple_of(qi * tq, tq), tq)
            dq_ref[rows, :] += SCALE * dq
            dk_ref[...] += dk
            dv_ref[...] += dv
            dc_ref[...] -= jnp.concatenate(dcs, axis=0)

        @pl.when(qi > kj)
        def _():
            step(None)

        @pl.when(qi == kj)
        def _():
            step(_diag_mask(tq))

    def q_map(p, j, i):
        return (jnp.maximum(i, j), p)

    kblk = pl.BlockSpec((tk, LANES), lambda p, j, i: (j, p))
    rep = pl.BlockSpec((2, tq, LANES), lambda p, j, i: (p, jnp.maximum(i, j), 0))
    return pl.pallas_call(
        body, name=name, grid=(npair, nq, nq),
        in_specs=[pl.BlockSpec((tq, LANES), q_map),
                  pl.BlockSpec((tk, LANES), lambda p, j, i: (j, npair + p)),
                  pl.BlockSpec((tk, LANES), lambda p, j, i: (j, 2 * npair + p)),
                  pl.BlockSpec((tq, LANES), q_map), rep, rep, rep,
                  pl.BlockSpec((None, 2, tk), lambda p, j, i: (p, 0, j))],
        out_specs=[pl.BlockSpec((s, LANES), lambda p, j, i: (0, p)), kblk, kblk,
                   pl.BlockSpec((None, 2, tk), lambda p, j, i: (p, 0, j))],
        out_shape=[jax.ShapeDtypeStruct((s, MIX_W), F32), jax.ShapeDtypeStruct((s, MIX_W), F32),
                   jax.ShapeDtypeStruct((s, MIX_W), F32), jax.ShapeDtypeStruct((npair, 2, s), F32)],
        compiler_params=_cp("arbitrary", "arbitrary", "arbitrary"))(hb, hb, hb, dcat, dsum, lse, c_rep, c_t3)


def _loss_head(y, target, name):
    s, d = y.shape
    ts = _rows(s)

    def body(y_ref, t_ref, dy_ref, l_ref):
        i = pl.program_id(0)
        e = y_ref[...] - t_ref[...]
        dy_ref[...] = e * (1.0 / d)

        @pl.when(i == 0)
        def _():
            l_ref[...] = jnp.zeros_like(l_ref)

        part = jnp.sum(jnp.sum(e * e, axis=1, keepdims=True), axis=0, keepdims=True)
        l_ref[...] += part * (0.5 / d)

    row = pl.BlockSpec((ts, d), lambda i: (i, 0))
    return pl.pallas_call(
        body, name=name, grid=(s // ts,), in_specs=[row, row],
        out_specs=[row, pl.BlockSpec((1, 1), lambda i: (0, 0))],
        out_shape=[jax.ShapeDtypeStruct((s, d), F32), jax.ShapeDtypeStruct((1, 1), F32)],
        compiler_params=_cp("arbitrary"))(y, target)


def _adam_rows(r, c):
    cap = max(8, (1 << 20) // (4 * c))
    if r <= cap:
        return r
    best = None
    for t in range(8, cap + 1, 8):
        if r % t == 0:
            best = t
    return best if best is not None else r


def _reduce_adamw(contribs, w, m, v, name):
    nl = len(contribs)
    nd, r, c = contribs[0].shape
    tr = _adam_rows(r, c)
    bc1 = 1.0 - ADAM_B1 ** ADAM_STEP
    bc2 = 1.0 - ADAM_B2 ** ADAM_STEP

    def body(*refs):
        c_refs = refs[:nl]
        w_ref, m_ref, v_ref, g_ref, d_ref, nm_ref, nv_ref = refs[nl:]
        l = pl.program_id(0)
        for li in range(nl):
            @pl.when(l == li)
            def _(c_ref=c_refs[li]):
                g = c_ref[0].astype(F32)
                for k in range(1, nd):
                    g = g + c_ref[k].astype(F32)
                nm = ADAM_B1 * m_ref[...] + (1.0 - ADAM_B1) * g
                nv = ADAM_B2 * v_ref[...] + (1.0 - ADAM_B2) * (g * g)
                g_ref[...] = g
                nm_ref[...] = nm
                nv_ref[...] = nv
                d_ref[...] = -ADAM_LR * ((nm / bc1) / (jnp.sqrt(nv / bc2) + ADAM_EPS) + ADAM_WD * w_ref[...])

    def c_spec(li):
        return pl.BlockSpec((nd, tr, c), lambda l, i: (0, jnp.where(l == li, i, 0), 0))

    blk = pl.BlockSpec((None, tr, c), lambda l, i: (l, i, 0))
    out = jax.ShapeDtypeStruct((nl, r, c), F32)
    return pl.pallas_call(
        body, name=name, grid=(nl, r // tr),
        in_specs=[c_spec(li) for li in range(nl)] + [blk, blk, blk],
        out_specs=[blk, blk, blk, blk], out_shape=[out, out, out, out],
        compiler_params=_cp("arbitrary", "arbitrary"))(*contribs, w, m, v)


def _mesh_pos():
    return lax.axis_index("x"), lax.axis_index("y"), lax.axis_index("c")


def _peer(pos, k):
    x, y, c = pos
    return (1 - x if k & 4 else x, 1 - y if k & 2 else y, 1 - c if k & 1 else c)


def _linear(pos):
    return 4 * pos[0] + 2 * pos[1] + pos[2]


def _xfer_copies(srcs, lands, send_sems, recv_sems, local_sems, gather):
    pos = _mesh_pos()
    me = _linear(pos)
    local, remote = [], []
    for i, (src, land) in enumerate(zip(srcs, lands)):
        local.append(pltpu.make_async_copy(src if gather else src.at[me], land.at[me], local_sems.at[i]))
        for k in range(1, N_DEV):
            peer = _peer(pos, k)
            remote.append(pltpu.make_async_remote_copy(
                src_ref=src if gather else src.at[_linear(peer)], dst_ref=land.at[me],
                send_sem=send_sems.at[i * (N_DEV - 1) + k - 1], recv_sem=recv_sems.at[i * (N_DEV - 1) + k - 1],
                device_id=peer, device_id_type=MESH_ID))
    return local, remote


_HBM = pl.BlockSpec(memory_space=pltpu.HBM)
_SEM = pl.BlockSpec(memory_space=pltpu.SEMAPHORE)
_EFFECT = pltpu.SideEffectType.DATAFLOW_SIDE_EFFECTING


def _xfer_start(srcs, gather, name, after=()):
    n = len(srcs)
    na = len(after)
    lands = [lax.empty(((N_DEV,) + a.shape) if gather else a.shape, a.dtype) for a in srcs]

    def body(*refs):
        src, land = refs[:n], refs[n:2 * n]
        send_sems, recv_sems, local_sems = refs[2 * n + na:2 * n + na + 3]
        local, remote = _xfer_copies(src, land, send_sems, recv_sems, local_sems, gather)
        for cp in local + remote:
            cp.start()
        refs[-1][...] = jnp.zeros_like(refs[-1])

    nsem = n * (N_DEV - 1)
    out = pl.pallas_call(
        body, name=name,
        out_shape=(pltpu.SemaphoreType.DMA((nsem,)), pltpu.SemaphoreType.DMA((nsem,)), pltpu.SemaphoreType.DMA((n,)),
                   *[pltpu.HBM(a.shape, a.dtype) for a in srcs], *[pltpu.HBM(a.shape, a.dtype) for a in lands],
                   jax.ShapeDtypeStruct((8, LANES), F32)),
        in_specs=[_HBM] * (2 * n) + [pl.BlockSpec(memory_space=pl.ANY)] * na,
        out_specs=(_SEM, _SEM, _SEM, *[_HBM] * (2 * n), pl.BlockSpec(memory_space=pltpu.VMEM)),
        input_output_aliases={i: 3 + i for i in range(2 * n)},
        compiler_params=pltpu.CompilerParams(has_side_effects=_EFFECT))(
            *[pltpu.with_memory_space_constraint(a, pltpu.HBM) for a in srcs],
            *[pltpu.with_memory_space_constraint(a, pltpu.HBM) for a in lands], *after)
    return out[:3], list(out[3:3 + n]), list(out[3 + n:3 + 2 * n]), out[-1]


def _started(handle):
    return handle[3]


def _xfer_wait(handle, after, gather, name):
    sems, srcs, lands, _ = handle
    n = len(srcs)

    def body(*refs):
        src, land = refs[:n], refs[n:2 * n]
        send_sems, recv_sems, local_sems = refs[2 * n:2 * n + 3]
        local, remote = _xfer_copies(src, land, send_sems, recv_sems, local_sems, gather)
        for cp in local:
            cp.wait()
        for cp in remote:
            cp.wait_send()
            cp.wait_recv()

    out = pl.pallas_call(
        body, name=name,
        out_shape=(*[pltpu.HBM(a.shape, a.dtype) for a in srcs], *[pltpu.HBM(a.shape, a.dtype) for a in lands]),
        in_specs=[_HBM] * (2 * n) + [_SEM] * 3 + [pl.BlockSpec(memory_space=pl.ANY)] * len(after),
        out_specs=tuple([_HBM] * (2 * n)), input_output_aliases={i: i for i in range(2 * n)},
        compiler_params=pltpu.CompilerParams(has_side_effects=_EFFECT))(*srcs, *lands, *sems, *after)
    return list(out[n:])


def _cols_full(g):
    nd, r, c = g.shape
    return jnp.transpose(g, (1, 0, 2)).reshape(r, nd * c)


def _cols_split(full):
    r, n = full.shape
    return jnp.transpose(full.reshape(r, N_DEV, n // N_DEV), (1, 0, 2))


def _pack_b_in(w):
    qkv = 3 * MIX_W
    pad = jnp.zeros((w.shape[0], B_IN_PAD - w.shape[1]), w.dtype)
    return jnp.concatenate([w[:, :qkv], w[:, qkv + N_MIX_HEADS:], w[:, qkv:qkv + N_MIX_HEADS], pad], axis=1)


def _unpack_b_in(w):
    qkv = 3 * MIX_W
    return jnp.concatenate([w[:, :qkv], w[:, qkv + MEM_W:qkv + MEM_W + N_MIX_HEADS], w[:, qkv:qkv + MEM_W]], axis=1)


def _to_classes(t, g):
    r = 4 ** g
    s, w = t.shape
    return jnp.transpose(t.reshape(s // r, r, w), (1, 0, 2)).reshape(s, w)


def _from_classes(t, g):
    r = 4 ** g
    s, w = t.shape
    return jnp.transpose(t.reshape(r, s // r, w), (1, 0, 2)).reshape(s, w)


def _group_stack(t):
    return jnp.stack([_to_classes(t[:, g * GROUP_W:(g + 1) * GROUP_W], g) for g in range(N_GROUPS)])


def _group_unstack(t3):
    return jnp.concatenate([_from_classes(t3[g], g) for g in range(N_GROUPS)], axis=1)


def _same_stack(t):
    return jnp.stack([_to_classes(t, g) for g in range(N_GROUPS)])


def _same_unstack(t3):
    return jnp.stack([_from_classes(t3[g], g) for g in range(N_GROUPS)])


def _ffn_forward(x, xb, wgu, wd4, gain, bias, tag):
    gu, a = _ffn_up(xb, wgu, f"{tag}_up")
    y, yb, xh, rstd = _mm_res_ln(a, wd4, x, gain, bias, 0.5, f"{tag}_down_ln")
    return y, yb, (xb, gu, a, xh, rstd)


def _ffn_backward(dy, saved, wgu, wd4, gain, tag, after=()):
    xb, gu, a, xh, rstd = saved
    s = xb.shape[0]
    nd, d, c = wgu.shape
    dz, dzb, dgain, dbias = _ln_bwd(dy, xh, rstd, gain, 0.5, f"{tag}_ln_bwd", after)
    dh = _ffn_bwd_act(dzb, wd4, gu, f"{tag}_act_bwd").reshape(nd, s, c)
    dwd = _mm_tn(a, dzb[None], f"{tag}_dwd").reshape(nd, wd4.shape[1] // 2, d)
    dx = _mm_nt(dh, wgu, f"{tag}_dx", res=dz)
    dwgu = _mm_tn(xb[None], dh, f"{tag}_dwgu")
    return dx, dwgu, dwd, dgain, dbias


def _mixer_a_forward(x, xb, memb, w_in, w_kv, w_out, gain, bias, tabs):
    h = _mm_nn(xb, w_in, F32, "a_in")
    hb = _rope_cast([h], tabs, 2 * MIX_W // LANES, "a_rope")
    q3 = _group_stack(hb[:, :MIX_W])
    k3 = _group_stack(hb[:, MIX_W:2 * MIX_W])
    v3 = _group_stack(hb[:, 2 * MIX_W:3 * MIX_W])
    o3, l3 = _band_fwd(q3, k3, v3, "a_band_fwd")
    oa, lt = _band_combine(_same_unstack(o3), _same_unstack(l3), "a_combine")
    kv = _mm_nn(memb, w_kv, BF, "a_mem_kv")
    om, lm = _mem_fwd(hb, 3 * MIX_W // LANES, kv, "a_mem_fwd")
    cat = jnp.concatenate([oa, om], axis=1)
    y, yb, xh, rstd = _mm_res_ln(cat[None], w_out[None], x, gain, bias, 1.0, "a_out_ln")
    return y, yb, (xb, hb, q3, k3, v3, oa, lt, kv, lm, cat, xh, rstd)


def _mixer_a_backward(dy, saved, memb, w_in, w_kv, w_out, gain, tabs_neg, after=()):
    xb, hb, q3, k3, v3, oa, lt, kv, lm, cat, xh, rstd = saved
    dz, dzb, dgain, dbias = _ln_bwd(dy, xh, rstd, gain, 1.0, "a_ln_bwd", after)
    dcat = _mm_nt(dzb[None], w_out[None], "a_dcat", out_dtype=BF)
    dw_out = _mm_tn(cat[None], dzb[None], "a_dwout")[0]
    dqm, dkm, dvm = _mem_bwd(hb, 3 * MIX_W // LANES, kv, dcat, cat, GROUP_W // LANES, lm, "a_mem_bwd")
    dkv = jnp.concatenate([dkm, dvm], axis=1).astype(BF)
    dw_kv = _mm_tn(memb[None], dkv[None], "a_dwkv")[0]
    dq3, dk3, dv3 = _band_bwd(q3, k3, v3, _same_stack(dcat[:, :GROUP_W]), _same_stack(oa), _same_stack(lt),
                              "a_band_bwd")
    dhb = _rope_cast([_group_unstack(dq3), _group_unstack(dk3), _group_unstack(dv3), dqm], tabs_neg,
                     2 * MIX_W // LANES, "a_rope_bwd")
    dw_in = _mm_tn(xb[None], dhb[None], "a_dwin")[0]
    dx = _mm_nt(dhb[None], w_in[None], "a_dx", res=dz)
    return dx, dw_in, dw_kv, dw_out, dgain, dbias


def _pad_rows(t, rows):
    return jnp.concatenate([t, jnp.zeros((rows - t.shape[0], t.shape[1]), t.dtype)], axis=0)


def _pad_cols(t, cols):
    return jnp.concatenate([t, jnp.zeros((t.shape[0], cols - t.shape[1]), t.dtype)], axis=1)


def _mixer_b_forward(x, xb, memb, w_in, fbias, w_kv, w_out, gain, bias, tabs):
    s = x.shape[0]
    h = _mm_nn(xb, w_in, F32, "b_in")
    hb = _rope_cast([h], tabs, 0, "b_cast")
    f0 = 3 * MIX_W + MEM_W
    f_t = _pad_rows(jnp.transpose(h[:, f0:f0 + N_MIX_HEADS]), 16)
    bias16 = _pad_rows(jnp.transpose(fbias), 16)
    c_t = _gate_fwd(f_t, bias16, "b_gate_fwd")
    c_t3 = c_t[:N_MIX_HEADS].reshape(N_MIX_HEADS // 2, 2, s)
    c_rep = jnp.broadcast_to(c_t[:N_MIX_HEADS, :, None], (N_MIX_HEADS, s, LANES))
    ob, lb = _fox_fwd(hb, c_rep, c_t3, "b_fox_fwd")
    kv = _mm_nn(memb, w_kv, BF, "b_mem_kv")
    om, lm = _mem_fwd(hb, 3 * MIX_W // LANES, kv, "b_mem_fwd")
    cat = jnp.concatenate([ob, om], axis=1)
    y, yb, xh, rstd = _mm_res_ln(cat[None], w_out[None], x, gain, bias, 1.0, "b_out_ln")
    return y, yb, (xb, hb, f_t, bias16, c_rep, c_t3, lb, kv, lm, cat, xh, rstd)


def _mixer_b_backward(dy, saved, memb, w_in, w_kv, w_out, gain, tabs, after=()):
    xb, hb, f_t, bias16, c_rep, c_t3, lb, kv, lm, cat, xh, rstd = saved
    s = xb.shape[0]
    dz, dzb, dgain, dbias = _ln_bwd(dy, xh, rstd, gain, 1.0, "b_ln_bwd", after)
    dcat = _mm_nt(dzb[None], w_out[None], "b_dcat", out_dtype=BF)
    dw_out = _mm_tn(cat[None], dzb[None], "b_dwout")[0]
    dqm, dkm, dvm = _mem_bwd(hb, 3 * MIX_W // LANES, kv, dcat, cat, MIX_W // LANES, lm, "b_mem_bwd")
    dkv = jnp.concatenate([dkm, dvm], axis=1).astype(BF)
    dw_kv = _mm_tn(memb[None], dkv[None], "b_dwkv")[0]
    dsum = _fox_dsum(hb, dcat, lb, c_rep, c_t3, "b_fox_dsum")
    dq, dk, dv, dc3 = _fox_bwd(hb, dcat, dsum, lb, c_rep, c_t3, "b_fox_bwd")
    df_t, dfb = _gate_bwd(_pad_rows(dc3.reshape(N_MIX_HEADS, s), 16), f_t, bias16, "b_gate_bwd")
    df = _pad_cols(jnp.transpose(df_t[:N_MIX_HEADS]), B_IN_PAD - 3 * MIX_W - MEM_W)
    dhb = _rope_cast([dq, dk, dv, dqm, df], tabs, 0, "b_cast_bwd")
    dw_in = _mm_tn(xb[None], dhb[None], "b_dwin")[0]
    dx = _mm_nt(dhb[None], w_in[None], "b_dx", res=dz)
    return dx, dw_in, jnp.transpose(dfb[:N_MIX_HEADS]), dw_kv, dw_out, dgain, dbias


def _weight_groups(w):
    b = {n: w[n].astype(BF) for n in WEIGHTS if n not in F32_COMM}
    return [
        [b["ffn1_w_gate_up"][0], b["ffn1_w_down"][0], w["ln_gain"], w["ln_bias"]],
        [b["a_w_in"][0], b["a_w_out"][0], b["mem_w_kv"][0]],
        [b["ffn2_w_gate_up"][0], b["ffn2_w_down"][0]],
        [b["ffn1_w_gate_up"][1], b["ffn1_w_down"][1]],
        [b["b_w_in"][0], b["b_w_out"][0], b["mem_w_kv"][1]],
        [b["ffn2_w_gate_up"][1], b["ffn2_w_down"][1]],
    ]


def _local_step(x, mem, target, fbias, get_w, put_g):
    s, d = x.shape
    tabs = _rope_tables(s, 1.0)
    tabs_neg = _rope_tables(s, -1.0)
    memb = mem.astype(BF)
    saved, wl = [], []
    cur, curb = x, x.astype(BF)
    ln_g = ln_b = None
    for i in range(DEPTH):
        g = get_w(3 * i, cur)
        if i == 0:
            ln_g, ln_b = (jnp.transpose(t, (1, 2, 0, 3)).reshape(DEPTH, 3, 1, d) for t in g[2:4])
        w1 = (g[0], g[1].reshape(N_DEV // 2, -1, d))
        cur, curb, s1 = _ffn_forward(cur, curb, w1[0], w1[1], ln_g[i, 0], ln_b[i, 0], f"l{i}_ffn1")
        g = get_w(3 * i + 1, cur)
        if i == 0:
            wm = (_cols_full(g[0]), g[2].reshape(d, -1), _cols_full(g[1]))
            cur, curb, s2 = _mixer_a_forward(cur, curb, memb, wm[0], wm[1], wm[2], ln_g[i, 1], ln_b[i, 1], tabs)
        else:
            wm = (_pack_b_in(g[0].reshape(d, -1)), g[2].reshape(d, -1), g[1].reshape(d, -1))
            cur, curb, s2 = _mixer_b_forward(cur, curb, memb, wm[0], fbias, wm[1], wm[2], ln_g[i, 1], ln_b[i, 1],
                                             tabs)
        g = get_w(3 * i + 2, cur)
        w3 = (g[0], g[1].reshape(N_DEV // 2, -1, d))
        cur, curb, s3 = _ffn_forward(cur, curb, w3[0], w3[1], ln_g[i, 2], ln_b[i, 2], f"l{i}_ffn2")
        saved.append((s1, s2, s3))
        wl.append((w1, wm, w3))

    dy, loss = _loss_head(cur, target, "loss_head")

    dgs = [[None] * 3 for _ in range(DEPTH)]
    dbs = [[None] * 3 for _ in range(DEPTH)]
    sent = ()
    for i in reversed(range(DEPTH)):
        s1, s2, s3 = saved[i]
        w1, wm, w3 = wl[i]
        dy, dgu, dd, dgs[i][2], dbs[i][2] = _ffn_backward(dy, s3, w3[0], w3[1], ln_g[i, 2], f"l{i}_ffn2", sent)
        sent = put_g(3 * i + 2, [dgu, dd])
        if i == 0:
            dy, dw_in, dw_kv, dw_out, dgs[i][1], dbs[i][1] = _mixer_a_backward(
                dy, s2, memb, wm[0], wm[1], wm[2], ln_g[i, 1], tabs_neg, sent)
            sent = put_g(1, [_cols_split(dw_in), _cols_split(dw_out), dw_kv.reshape(N_DEV, d // N_DEV, -1)])
        else:
            dy, dw_in, dfb, dw_kv, dw_out, dgs[i][1], dbs[i][1] = _mixer_b_backward(
                dy, s2, memb, wm[0], wm[1], wm[2], ln_g[i, 1], tabs, sent)
            sent = put_g(4, [_unpack_b_in(dw_in).reshape(N_DEV, d // N_DEV, -1),
                             dw_out.reshape(N_DEV, d // N_DEV, -1), dw_kv.reshape(N_DEV, d // N_DEV, -1),
                             jnp.broadcast_to(dfb[None], (N_DEV,) + dfb.shape)])
        dy, dgu, dd, dgs[i][0], dbs[i][0] = _ffn_backward(dy, s1, w1[0], w1[1], ln_g[i, 0], f"l{i}_ffn1", sent)
        if i == 0:
            ln_pieces = []
            for parts in (dgs, dbs):
                t = jnp.concatenate([parts[a][b] for a in range(DEPTH) for b in range(3)], axis=0)
                ln_pieces.append(jnp.transpose(t.reshape(DEPTH * 3, N_DEV, d // N_DEV), (1, 0, 2)))
            sent = put_g(0, [dgu, dd] + ln_pieces)
        else:
            sent = put_g(3, [dgu, dd])
    return loss, dy


WEIGHTS = ("ffn1_w_gate_up", "ffn1_w_down", "ffn2_w_gate_up", "ffn2_w_down", "ln_gain", "ln_bias", "mem_w_kv",
           "a_w_in", "a_w_out", "b_w_in", "b_forget_bias", "b_w_out")
F32_COMM = ("ln_gain", "ln_bias", "b_forget_bias")
GRAD_SLOTS = {
    "ffn1_w_gate_up": [(0, 0), (3, 0)], "ffn1_w_down": [(0, 1), (3, 1)],
    "ffn2_w_gate_up": [(2, 0), (5, 0)], "ffn2_w_down": [(2, 1), (5, 1)],
    "ln_gain": [(0, 2)], "ln_bias": [(0, 3)], "mem_w_kv": [(1, 2), (4, 2)],
    "a_w_in": [(1, 0)], "a_w_out": [(1, 1)], "b_w_in": [(4, 0)], "b_forget_bias": [(4, 3)], "b_w_out": [(4, 1)],
}


def kernel(x, mem, ffn1_w_gate_up, ffn1_w_down, ffn2_w_gate_up, ffn2_w_down, ln_gain, ln_bias, mem_w_kv, a_w_in, a_w_out, b_w_in, b_forget_bias, b_w_out, loss_target, m_ffn1_w_gate_up, m_ffn1_w_down, m_ffn2_w_gate_up, m_ffn2_w_down, m_ln_gain, m_ln_bias, m_mem_w_kv, m_a_w_in, m_a_w_out, m_b_w_in, m_b_forget_bias, m_b_w_out, v_ffn1_w_gate_up, v_ffn1_w_down, v_ffn2_w_gate_up, v_ffn2_w_down, v_ln_gain, v_ln_bias, v_mem_w_kv, v_a_w_in, v_a_w_out, v_b_w_in, v_b_forget_bias, v_b_w_out):
    w = dict(zip(WEIGHTS, (ffn1_w_gate_up, ffn1_w_down, ffn2_w_gate_up, ffn2_w_down, ln_gain, ln_bias, mem_w_kv,
                           a_w_in, a_w_out, b_w_in, b_forget_bias, b_w_out)))
    m = dict(zip(WEIGHTS, (m_ffn1_w_gate_up, m_ffn1_w_down, m_ffn2_w_gate_up, m_ffn2_w_down, m_ln_gain, m_ln_bias,
                           m_mem_w_kv, m_a_w_in, m_a_w_out, m_b_w_in, m_b_forget_bias, m_b_w_out)))
    v = dict(zip(WEIGHTS, (v_ffn1_w_gate_up, v_ffn1_w_down, v_ffn2_w_gate_up, v_ffn2_w_down, v_ln_gain, v_ln_bias,
                           v_mem_w_kv, v_a_w_in, v_a_w_out, v_b_w_in, v_b_forget_bias, v_b_w_out)))

    gathers = []
    for k, grp in enumerate(_weight_groups(w)):
        gathers.append(_xfer_start(grp, True, f"gather{k}_start", [_started(h) for h in gathers[-1:]]))
    exchanges = {}

    def get_w(k, after):
        behind = [after] + ([_started(h) for h in gathers] if k == 0 else [])
        return _xfer_wait(gathers[k], behind, True, f"gather{k}_wait")

    def put_g(k, pieces):
        exchanges[k] = _xfer_start(pieces, False, f"grads{k}_start")
        return (_started(exchanges[k]),)

    loss, grad_x = _local_step(x[0], mem[0], loss_target[0], b_forget_bias, get_w, put_g)
    loss = lax.psum(loss[0, 0], ("x", "y", "c"))

    outs, landed = {}, {}

    def adamw(names, after):
        for n in names:
            contribs = [landed[g][j] for g, j in GRAD_SLOTS[n]]
            view = (len(contribs),) + contribs[0].shape[1:]
            outs[n] = [t.reshape(w[n].shape) for t in _reduce_adamw(
                contribs, w[n].reshape(view), m[n].reshape(view), v[n].reshape(view), f"adamw_{n}")]
            after = outs[n][0]
        return after

    after = [grad_x, _started(exchanges[0])]
    for k in (5, 4, 3, 2, 1):
        landed[k] = _xfer_wait(exchanges[k], after, False, f"grads{k}_wait")
        after = [landed[k][0]]
    done = adamw(("ffn2_w_gate_up", "ffn2_w_down", "mem_w_kv", "a_w_in", "a_w_out", "b_w_in", "b_forget_bias",
                  "b_w_out"), None)
    landed[0] = _xfer_wait(exchanges[0], [done], False, "grads0_wait")
    adamw(("ffn1_w_gate_up", "ffn1_w_down", "ln_gain", "ln_bias"), None)
    return (loss, grad_x[None], *[outs[n][0] for n in WEIGHTS], *[outs[n][1] for n in WEIGHTS],
            *[outs[n][2] for n in WEIGHTS], *[outs[n][3] for n in WEIGHTS])
```

```python
import functools

import jax
import jax.numpy as jnp
from jax import lax
from jax.experimental import pallas as pl
from jax.experimental.pallas import tpu as pltpu

F32 = jnp.float32
BF = jnp.bfloat16
MESH_ID = pl.DeviceIdType.MESH

N_DEV = 8
DEPTH = 2
HEAD_DIM = 64
LANES = 128
N_MIX_HEADS = 12
N_MEM_HEADS = 4
MIX_W = N_MIX_HEADS * HEAD_DIM
MEM_W = N_MEM_HEADS * HEAD_DIM
N_GROUPS = 3
GROUP_W = MIX_W // N_GROUPS
BLOCK = 128
ROT_HALF = 8
ROPE_THETA = 500000.0
ALPHA = (2 * DEPTH) ** 0.25
LN_EPS = 1e-5
SCALE = HEAD_DIM ** -0.5
NEG = -1e30
B_IN_PAD = 2688
ADAM_LR, ADAM_B1, ADAM_B2, ADAM_EPS, ADAM_WD, ADAM_STEP = 0.001, 0.9, 0.999, 1e-08, 0.01, 10
VMEM_LIMIT = 56 * 1024 * 1024


def _cp(*sem):
    return pltpu.CompilerParams(dimension_semantics=sem, vmem_limit_bytes=VMEM_LIMIT)


def _dot(a, b):
    return jnp.dot(a, b, preferred_element_type=F32)


def _dot_nt(a, b):
    return lax.dot_general(a, b, (((1,), (1,)), ((), ())), preferred_element_type=F32)


def _dot_tn(a, b):
    return lax.dot_general(a, b, (((0,), (0,)), ((), ())), preferred_element_type=F32)


def _sigmoid(x):
    return 1.0 / (1.0 + jnp.exp(-x))


def _tile(n, cap=1024):
    if n <= cap:
        return n
    best = LANES
    for t in range(LANES, cap + 1, LANES):
        if n % t == 0:
            best = t
    return best


def _rows(s, cap=512):
    return s if s <= cap else cap


def _mm_nn(a, b, out_dtype, name):
    m, k = a.shape
    n = b.shape[1]
    tm, tn = _rows(m), _tile(n)

    def body(a_ref, b_ref, o_ref):
        o_ref[...] = _dot(a_ref[...], b_ref[...]).astype(o_ref.dtype)

    return pl.pallas_call(
        body, name=name, grid=(n // tn, m // tm),
        in_specs=[pl.BlockSpec((tm, k), lambda j, i: (i, 0)), pl.BlockSpec((k, tn), lambda j, i: (0, j))],
        out_specs=pl.BlockSpec((tm, tn), lambda j, i: (i, j)),
        out_shape=jax.ShapeDtypeStruct((m, n), out_dtype),
        compiler_params=_cp("parallel", "parallel"))(a, b)


def _resident(shape, index_map):
    return pl.BlockSpec(shape, index_map, pipeline_mode=pl.Buffered(1))


def _mm_tn(a, b, name, out_dtype=BF):
    na, s, m = a.shape
    nb, _, n = b.shape
    no = max(na, nb)
    tn = _tile(n)

    def body(a_ref, b_ref, o_ref):
        o_ref[...] = _dot_tn(a_ref[...], b_ref[...]).astype(o_ref.dtype)

    a_spec = (pl.BlockSpec((None, s, m), lambda j, c: (j, 0, 0)) if na > 1
              else _resident((None, s, m), lambda j, c: (0, 0, 0)))
    return pl.pallas_call(
        body, name=name, grid=(no, n // tn),
        in_specs=[a_spec, pl.BlockSpec((None, s, tn), lambda j, c: (j if nb > 1 else 0, 0, c))],
        out_specs=pl.BlockSpec((None, m, tn), lambda j, c: (j, 0, c)),
        out_shape=jax.ShapeDtypeStruct((no, m, n), out_dtype),
        compiler_params=_cp("parallel", "parallel"))(a, b)


def _mm_nt(dh, w, name, res=None, out_dtype=F32):
    nc, s, kc = dh.shape
    d = w.shape[1]
    ts = _rows(s, 256 if nc > 1 else 512)
    has_res = res is not None

    def body(*refs):
        if has_res:
            dh_ref, w_ref, r_ref, o_ref = refs
        else:
            dh_ref, w_ref, o_ref = refs
        out = _dot_nt(dh_ref[0], w_ref[0])
        for j in range(1, nc):
            out = out + _dot_nt(dh_ref[j], w_ref[j])
        if has_res:
            out = out + ALPHA * r_ref[...]
        o_ref[...] = out.astype(o_ref.dtype)

    in_specs = [pl.BlockSpec((nc, ts, kc), lambda i: (0, i, 0)), _resident((nc, d, kc), lambda i: (0, 0, 0))]
    args = [dh, w]
    if has_res:
        in_specs.append(pl.BlockSpec((ts, d), lambda i: (i, 0)))
        args.append(res)
    return pl.pallas_call(
        body, name=name, grid=(s // ts,), in_specs=in_specs,
        out_specs=pl.BlockSpec((ts, d), lambda i: (i, 0)),
        out_shape=jax.ShapeDtypeStruct((s, d), out_dtype),
        compiler_params=_cp("parallel"))(*args)


def _mm_res_ln(a, w, x, gain, bias, fscale, name):
    nc, s, kc = a.shape
    d = w.shape[2]
    ts = _rows(s, 256 if nc > 1 else 512)

    def body(a_ref, w_ref, x_ref, g_ref, b_ref, y_ref, yb_ref, xh_ref, r_ref):
        f = _dot(a_ref[0], w_ref[0])
        for j in range(1, nc):
            f = f + _dot(a_ref[j], w_ref[j])
        z = ALPHA * x_ref[...] + fscale * f
        mu = jnp.mean(z, axis=-1, keepdims=True)
        zc = z - mu
        var = jnp.mean(zc * zc, axis=-1, keepdims=True)
        r = lax.rsqrt(var + LN_EPS)
        xh = zc * r
        y = xh * g_ref[...] + b_ref[...]
        y_ref[...] = y
        yb_ref[...] = y.astype(BF)
        xh_ref[...] = xh
        r_ref[...] = r

    row = pl.BlockSpec((ts, d), lambda i: (i, 0))
    vec = pl.BlockSpec((1, d), lambda i: (0, 0))
    return pl.pallas_call(
        body, name=name, grid=(s // ts,),
        in_specs=[pl.BlockSpec((nc, ts, kc), lambda i: (0, i, 0)), _resident((nc, kc, d), lambda i: (0, 0, 0)),
                  row, vec, vec],
        out_specs=[row, row, row, pl.BlockSpec((ts, 1), lambda i: (i, 0))],
        out_shape=[jax.ShapeDtypeStruct((s, d), F32), jax.ShapeDtypeStruct((s, d), BF),
                   jax.ShapeDtypeStruct((s, d), F32), jax.ShapeDtypeStruct((s, 1), F32)],
        compiler_params=_cp("parallel"))(a, w, x, gain, bias)


def _ln_bwd(dy, xh, rstd, gain, fscale, name, after=()):
    s, d = dy.shape
    ts = _rows(s)
    na = len(after)

    def body(*refs):
        dy_ref, xh_ref, r_ref, g_ref = refs[:4]
        dz_ref, dzb_ref, dg_ref, db_ref = refs[4 + na:]
        i = pl.program_id(0)
        dyv = dy_ref[...]
        xhv = xh_ref[...]
        dxh = dyv * g_ref[...]
        m1 = jnp.mean(dxh, axis=-1, keepdims=True)
        m2 = jnp.mean(dxh * xhv, axis=-1, keepdims=True)
        dz = r_ref[...] * (dxh - m1 - xhv * m2)
        dz_ref[...] = dz
        dzb_ref[...] = (fscale * dz).astype(BF)

        @pl.when(i == 0)
        def _():
            dg_ref[...] = jnp.zeros_like(dg_ref)
            db_ref[...] = jnp.zeros_like(db_ref)

        dg_ref[...] += jnp.sum(dyv * xhv, axis=0, keepdims=True)
        db_ref[...] += jnp.sum(dyv, axis=0, keepdims=True)

    row = pl.BlockSpec((ts, d), lambda i: (i, 0))
    vec = pl.BlockSpec((1, d), lambda i: (0, 0))
    return pl.pallas_call(
        body, name=name, grid=(s // ts,),
        in_specs=[row, row, pl.BlockSpec((ts, 1), lambda i: (i, 0)), vec] + [pl.BlockSpec(memory_space=pl.ANY)] * na,
        out_specs=[row, row, vec, vec],
        out_shape=[jax.ShapeDtypeStruct((s, d), F32), jax.ShapeDtypeStruct((s, d), BF),
                   jax.ShapeDtypeStruct((1, d), F32), jax.ShapeDtypeStruct((1, d), F32)],
        compiler_params=_cp("arbitrary"))(dy, xh, rstd, gain, *after)


def _ffn_up(xb, wgu, name):
    s, d = xb.shape
    c = wgu.shape[2]
    nch = wgu.shape[0] // 2
    ts = _rows(s)
    w4 = wgu.reshape(2, nch, d, c)

    def body(x_ref, w_ref, gu_ref, a_ref):
        x = x_ref[...]
        g = _dot(x, w_ref[0])
        u = _dot(x, w_ref[1])
        gu_ref[0] = g.astype(BF)
        gu_ref[1] = u.astype(BF)
        a_ref[...] = (g * _sigmoid(g) * u).astype(BF)

    return pl.pallas_call(
        body, name=name, grid=(nch, s // ts),
        in_specs=[pl.BlockSpec((ts, d), lambda j, i: (i, 0)),
                  pl.BlockSpec((2, None, d, c), lambda j, i: (0, j, 0, 0))],
        out_specs=[pl.BlockSpec((2, None, ts, c), lambda j, i: (0, j, i, 0)),
                   pl.BlockSpec((None, ts, c), lambda j, i: (j, i, 0))],
        out_shape=[jax.ShapeDtypeStruct((2, nch, s, c), BF), jax.ShapeDtypeStruct((nch, s, c), BF)],
        compiler_params=_cp("parallel", "parallel"))(xb, w4)


def _ffn_bwd_act(dzb, wd4, gu, name):
    s, d = dzb.shape
    nch, c = wd4.shape[0], wd4.shape[1]
    ts = _rows(s)

    def body(dz_ref, w_ref, gu_ref, dh_ref):
        da = _dot_nt(dz_ref[...], w_ref[...])
        g = gu_ref[0].astype(F32)
        u = gu_ref[1].astype(F32)
        sg = _sigmoid(g)
        dh_ref[0] = (da * u * sg * (1.0 + g * (1.0 - sg))).astype(BF)
        dh_ref[1] = (da * g * sg).astype(BF)

    return pl.pallas_call(
        body, name=name, grid=(nch, s // ts),
        in_specs=[pl.BlockSpec((ts, d), lambda j, i: (i, 0)),
                  pl.BlockSpec((None, c, d), lambda j, i: (j, 0, 0)),
                  pl.BlockSpec((2, None, ts, c), lambda j, i: (0, j, i, 0))],
        out_specs=pl.BlockSpec((2, None, ts, c), lambda j, i: (0, j, i, 0)),
        out_shape=jax.ShapeDtypeStruct((2, nch, s, c), BF),
        compiler_params=_cp("parallel", "parallel"))(dzb, wd4, gu)


def _rope_tables(s, sign):
    pos = jnp.arange(s, dtype=F32)
    inv_freq = 1.0 / (ROPE_THETA ** (jnp.arange(ROT_HALF, dtype=F32) / ROT_HALF))
    ang = pos[:, None] * inv_freq[None, :]
    cos, sin = jnp.cos(ang), jnp.sin(ang) * sign
    one = jnp.ones((s, HEAD_DIM - 2 * ROT_HALF), F32)
    zero = jnp.zeros((s, HEAD_DIM - 2 * ROT_HALF), F32)
    zh = jnp.zeros((s, ROT_HALF), F32)
    cos_f = jnp.concatenate([cos, cos, one], axis=1)
    sin_a = jnp.concatenate([-sin, zh, zero], axis=1)
    sin_b = jnp.concatenate([zh, sin, zero], axis=1)
    rep = LANES // HEAD_DIM
    return tuple(jnp.tile(t, (1, rep)) for t in (cos_f, sin_a, sin_b))


def _rope_cast(parts, tabs, n_rope, name):
    s = parts[0].shape[0]
    widths = [p.shape[1] for p in parts]
    n = sum(widths)
    npart = len(parts)
    ts = _rows(s, 256)

    def body(*refs):
        part_refs = refs[:npart]
        c_ref, sa_ref, sb_ref, o_ref = refs[npart:]
        col = 0
        for ref, w in zip(part_refs, widths):
            for j in range(w // LANES):
                t = ref[:, j * LANES:(j + 1) * LANES]
                if col < n_rope:
                    t = (t * c_ref[...] + pltpu.roll(t, LANES - ROT_HALF, 1) * sa_ref[...]
                         + pltpu.roll(t, ROT_HALF, 1) * sb_ref[...])
                o_ref[:, col * LANES:(col + 1) * LANES] = t.astype(BF)
                col += 1

    tab = pl.BlockSpec((ts, LANES), lambda i: (i, 0))
    return pl.pallas_call(
        body, name=name, grid=(s // ts,),
        in_specs=[pl.BlockSpec((ts, w), lambda i: (i, 0)) for w in widths] + [tab, tab, tab],
        out_specs=pl.BlockSpec((ts, n), lambda i: (i, 0)),
        out_shape=jax.ShapeDtypeStruct((s, n), BF),
        compiler_params=_cp("parallel"))(*parts, *tabs)


def _head_masks():
    lane = lax.broadcasted_iota(jnp.int32, (1, LANES), 1)
    return [lane < HEAD_DIM, lane >= HEAD_DIM]


def _sel(mask, v):
    return jnp.where(mask, v, jnp.zeros_like(v))


def _pick(mask, wide, fill):
    return jnp.max(jnp.where(mask, wide, fill), axis=1, keepdims=True)


def _band_masks(has_other, prev):
    qi = lax.broadcasted_iota(jnp.int32, (BLOCK, BLOCK), 0)
    kj = lax.broadcasted_iota(jnp.int32, (BLOCK, BLOCK), 1)
    if prev:
        return kj >= qi + jnp.where(has_other, 0, BLOCK)
    return kj <= qi


def _band_fwd(q3, k3, v3, name):
    ng, s, w = q3.shape
    nb = s // BLOCK
    npair = w // LANES

    def body(q_ref, kc_ref, kp_ref, vc_ref, vp_ref, o_ref, l_ref):
        g = pl.program_id(0)
        b = pl.program_id(2)
        nbl = jnp.right_shift(nb, 2 * g)
        has_prev = jnp.bitwise_and(b, nbl - 1) != 0
        mc = _band_masks(None, False)
        mp = _band_masks(has_prev, True)
        q, kc, kp, vc, vp = q_ref[...], kc_ref[...], kp_ref[...], vc_ref[...], vp_ref[...]
        hm = _head_masks()
        o = jnp.zeros((BLOCK, LANES), F32)
        lse_w = jnp.zeros((BLOCK, LANES), F32)
        for h in range(2):
            qh = _sel(hm[h], q)
            sc = jnp.where(mc, _dot_nt(qh, kc) * SCALE, NEG)
            sp = jnp.where(mp, _dot_nt(qh, kp) * SCALE, NEG)
            m = jnp.maximum(jnp.max(sc, axis=1, keepdims=True), jnp.max(sp, axis=1, keepdims=True))
            pc = jnp.exp(sc - m)
            pp = jnp.exp(sp - m)
            l = jnp.sum(pc, axis=1, keepdims=True) + jnp.sum(pp, axis=1, keepdims=True)
            oh = _dot(pc.astype(BF), _sel(hm[h], vc)) + _dot(pp.astype(BF), _sel(hm[h], vp))
            o = o + oh / l
            lse_w = jnp.where(hm[h], m + jnp.log(l), lse_w)
        o_ref[...] = o
        l_ref[...] = lse_w

    cur = pl.BlockSpec((None, BLOCK, LANES), lambda g, p, b: (g, b, p))
    prv = pl.BlockSpec((None, BLOCK, LANES), lambda g, p, b: (g, jnp.maximum(b - 1, 0), p))
    return pl.pallas_call(
        body, name=name, grid=(ng, npair, nb),
        in_specs=[cur, cur, prv, cur, prv], out_specs=[cur, cur],
        out_shape=[jax.ShapeDtypeStruct((ng, s, w), F32), jax.ShapeDtypeStruct((ng, s, w), F32)],
        compiler_params=_cp("parallel", "parallel", "parallel"))(q3, k3, k3, v3, v3)


def _band_combine(o3, l3, name):
    ng, s, w = o3.shape
    ts = _rows(s)

    def body(o_ref, l_ref, oa_ref, lt_ref):
        ls = [l_ref[g] for g in range(ng)]
        m = functools.reduce(jnp.maximum, ls)
        es = [jnp.exp(l - m) for l in ls]
        den = functools.reduce(lambda a, b: a + b, es)
        num = functools.reduce(lambda a, b: a + b, [es[g] * o_ref[g] for g in range(ng)])
        oa_ref[...] = (num / den).astype(BF)
        lt_ref[...] = m + jnp.log(den)

    blk3 = pl.BlockSpec((ng, ts, w), lambda i: (0, i, 0))
    blk = pl.BlockSpec((ts, w), lambda i: (i, 0))
    return pl.pallas_call(
        body, name=name, grid=(s // ts,), in_specs=[blk3, blk3], out_specs=[blk, blk],
        out_shape=[jax.ShapeDtypeStruct((s, w), BF), jax.ShapeDtypeStruct((s, w), F32)],
        compiler_params=_cp("parallel"))(o3, l3)


def _band_bwd(q3, k3, v3, do3, oa3, lt3, name):
    ng, s, w = q3.shape
    nb = s // BLOCK
    npair = w // LANES

    def body(q_ref, qn_ref, kc_ref, kp_ref, vc_ref, vp_ref, do_ref, don_ref, oa_ref, oan_ref, lt_ref, ltn_ref,
             dq_ref, dk_ref, dv_ref):
        g = pl.program_id(0)
        b = pl.program_id(2)
        nbl = jnp.right_shift(nb, 2 * g)
        has_prev = jnp.bitwise_and(b, nbl - 1) != 0
        has_next = jnp.bitwise_and(b + 1, nbl - 1) != 0
        mc = _band_masks(None, False)
        mp = _band_masks(has_prev, True)
        mn = _band_masks(has_next, True)
        q, qn, kc, kp, vc, vp = q_ref[...], qn_ref[...], kc_ref[...], kp_ref[...], vc_ref[...], vp_ref[...]
        do, don = do_ref[...], don_ref[...]
        dd = do.astype(F32) * oa_ref[...].astype(F32)
        ddn = don.astype(F32) * oan_ref[...].astype(F32)
        lt, ltn = lt_ref[...], ltn_ref[...]
        hm = _head_masks()
        dq = jnp.zeros((BLOCK, LANES), F32)
        dk = jnp.zeros((BLOCK, LANES), F32)
        dv = jnp.zeros((BLOCK, LANES), F32)
        for h in range(2):
            qh, doh = _sel(hm[h], q), _sel(hm[h], do)
            qnh, donh = _sel(hm[h], qn), _sel(hm[h], don)
            kch, kph = _sel(hm[h], kc), _sel(hm[h], kp)
            lse = _pick(hm[h], lt, NEG)
            lsen = _pick(hm[h], ltn, NEG)
            dsum = jnp.sum(_sel(hm[h], dd), axis=1, keepdims=True)
            dsumn = jnp.sum(_sel(hm[h], ddn), axis=1, keepdims=True)
            pc = jnp.exp(jnp.where(mc, _dot_nt(qh, kc) * SCALE, NEG) - lse)
            pp = jnp.exp(jnp.where(mp, _dot_nt(qh, kp) * SCALE, NEG) - lse)
            dsc = pc * (_dot_nt(doh, vc) - dsum)
            dsp = pp * (_dot_nt(doh, vp) - dsum)
            dq = dq + SCALE * (_dot(dsc.astype(BF), kch) + _dot(dsp.astype(BF), kph))
            pn = jnp.exp(jnp.where(mn, _dot_nt(qnh, kc) * SCALE, NEG) - lsen)
            dsn = pn * (_dot_nt(donh, vc) - dsumn)
            dk = dk + SCALE * (_dot_tn(dsc.astype(BF), qh) + _dot_tn(dsn.astype(BF), qnh))
            dv = dv + _dot_tn(pc.astype(BF), doh) + _dot_tn(pn.astype(BF), donh)
        dq_ref[...] = dq
        dk_ref[...] = dk
        dv_ref[...] = dv

    cur = pl.BlockSpec((None, BLOCK, LANES), lambda g, p, b: (g, b, p))
    prv = pl.BlockSpec((None, BLOCK, LANES), lambda g, p, b: (g, jnp.maximum(b - 1, 0), p))
    nxt = pl.BlockSpec((None, BLOCK, LANES), lambda g, p, b: (g, jnp.minimum(b + 1, nb - 1), p))
    out = jax.ShapeDtypeStruct((ng, s, w), F32)
    return pl.pallas_call(
        body, name=name, grid=(ng, npair, nb),
        in_specs=[cur, nxt, cur, prv, cur, prv, cur, nxt, cur, nxt, cur, nxt],
        out_specs=[cur, cur, cur], out_shape=[out, out, out],
        compiler_params=_cp("parallel", "parallel", "parallel"))(
            q3, q3, k3, k3, v3, v3, do3, do3, oa3, oa3, lt3, lt3)


def _mem_fwd(hb, q_blk0, kv, name):
    s = hb.shape[0]
    m = kv.shape[0]
    tq = _rows(s)
    npair = MEM_W // LANES

    def body(q_ref, k_ref, v_ref, o_ref, l_ref):
        q, k, v = q_ref[...], k_ref[...], v_ref[...]
        hm = _head_masks()
        o = jnp.zeros((tq, LANES), F32)
        lse_w = jnp.zeros((tq, LANES), F32)
        for h in range(2):
            sc = _dot_nt(_sel(hm[h], q), k) * SCALE
            mx = jnp.max(sc, axis=1, keepdims=True)
            p = jnp.exp(sc - mx)
            l = jnp.sum(p, axis=1, keepdims=True)
            o = o + _dot(p.astype(BF), _sel(hm[h], v)) / l
            lse_w = jnp.where(hm[h], mx + jnp.log(l), lse_w)
        o_ref[...] = o.astype(BF)
        l_ref[...] = lse_w

    blk = pl.BlockSpec((tq, LANES), lambda p, i: (i, p))
    return pl.pallas_call(
        body, name=name, grid=(npair, s // tq),
        in_specs=[pl.BlockSpec((tq, LANES), lambda p, i: (i, q_blk0 + p)),
                  pl.BlockSpec((m, LANES), lambda p, i: (0, p)),
                  pl.BlockSpec((m, LANES), lambda p, i: (0, npair + p))],
        out_specs=[blk, blk],
        out_shape=[jax.ShapeDtypeStruct((s, MEM_W), BF), jax.ShapeDtypeStruct((s, MEM_W), F32)],
        compiler_params=_cp("parallel", "parallel"))(hb, kv, kv)


def _mem_bwd(hb, q_blk0, kv, dcat, cat, o_blk0, lse, name):
    s = hb.shape[0]
    m = kv.shape[0]
    tq = _rows(s)
    npair = MEM_W // LANES

    def body(q_ref, k_ref, v_ref, do_ref, o_ref, l_ref, dq_ref, dk_ref, dv_ref):
        i = pl.program_id(1)

        @pl.when(i == 0)
        def _():
            dk_ref[...] = jnp.zeros_like(dk_ref)
            dv_ref[...] = jnp.zeros_like(dv_ref)

        q, k, v, do = q_ref[...], k_ref[...], v_ref[...], do_ref[...]
        dd = do.astype(F32) * o_ref[...].astype(F32)
        lt = l_ref[...]
        hm = _head_masks()
        dq = jnp.zeros((tq, LANES), F32)
        dk = jnp.zeros((m, LANES), F32)
        dv = jnp.zeros((m, LANES), F32)
        for h in range(2):
            qh, doh = _sel(hm[h], q), _sel(hm[h], do)
            p = jnp.exp(_dot_nt(qh, k) * SCALE - _pick(hm[h], lt, NEG))
            ds = p * (_dot_nt(doh, v) - jnp.sum(_sel(hm[h], dd), axis=1, keepdims=True))
            dq = dq + SCALE * _dot(ds.astype(BF), _sel(hm[h], k))
            dk = dk + SCALE * _dot_tn(ds.astype(BF), qh)
            dv = dv + _dot_tn(p.astype(BF), doh)
        dq_ref[...] = dq
        dk_ref[...] += dk
        dv_ref[...] += dv

    row = pl.BlockSpec((tq, LANES), lambda p, i: (i, p))
    orow = pl.BlockSpec((tq, LANES), lambda p, i: (i, o_blk0 + p))
    acc = pl.BlockSpec((m, LANES), lambda p, i: (0, p))
    return pl.pallas_call(
        body, name=name, grid=(npair, s // tq),
        in_specs=[pl.BlockSpec((tq, LANES), lambda p, i: (i, q_blk0 + p)),
                  pl.BlockSpec((m, LANES), lambda p, i: (0, p)),
                  pl.BlockSpec((m, LANES), lambda p, i: (0, npair + p)), orow, orow, row],
        out_specs=[row, acc, acc],
        out_shape=[jax.ShapeDtypeStruct((s, MEM_W), F32), jax.ShapeDtypeStruct((m, MEM_W), F32),
                   jax.ShapeDtypeStruct((m, MEM_W), F32)],
        compiler_params=_cp("parallel", "arbitrary"))(hb, kv, kv, dcat, cat, lse)


def _gate_fwd(f_t, bias, name):
    hp, s = f_t.shape
    nblk = s // LANES

    def body(f_ref, b_ref, c_ref):
        lane = lax.broadcasted_iota(jnp.int32, (hp, LANES), 1)

        def step(i, carry):
            off = pl.multiple_of(i * LANES, LANES)
            x = f_ref[:, pl.ds(off, LANES)] + b_ref[...]
            acc = jnp.minimum(x, 0.0) - jnp.log(1.0 + jnp.exp(-jnp.abs(x)))
            sh = 1
            while sh < LANES:
                acc = acc + jnp.where(lane >= sh, pltpu.roll(acc, sh, 1), 0.0)
                sh *= 2
            acc = acc + carry
            c_ref[:, pl.ds(off, LANES)] = acc
            return acc[:, LANES - 1:LANES]

        lax.fori_loop(0, nblk, step, jnp.zeros((hp, 1), F32))

    vm = pl.BlockSpec(memory_space=pltpu.VMEM)
    return pl.pallas_call(body, name=name, in_specs=[vm, vm], out_specs=vm,
                          out_shape=jax.ShapeDtypeStruct((hp, s), F32),
                          compiler_params=pltpu.CompilerParams(vmem_limit_bytes=VMEM_LIMIT))(f_t, bias)


def _gate_bwd(dc_t, f_t, bias, name):
    hp, s = f_t.shape
    nblk = s // LANES

    def body(dc_ref, f_ref, b_ref, df_ref, db_ref):
        lane = lax.broadcasted_iota(jnp.int32, (hp, LANES), 1)

        def step(t, carry):
            suffix, dbias = carry
            off = pl.multiple_of((nblk - 1 - t) * LANES, LANES)
            acc = dc_ref[:, pl.ds(off, LANES)]
            sh = 1
            while sh < LANES:
                acc = acc + jnp.where(lane < LANES - sh, pltpu.roll(acc, LANES - sh, 1), 0.0)
                sh *= 2
            acc = acc + suffix
            x = f_ref[:, pl.ds(off, LANES)] + b_ref[...]
            df = acc * _sigmoid(-x)
            df_ref[:, pl.ds(off, LANES)] = df
            return acc[:, 0:1], dbias + jnp.sum(df, axis=1, keepdims=True)

        _, dbias = lax.fori_loop(0, nblk, step, (jnp.zeros((hp, 1), F32), jnp.zeros((hp, 1), F32)))
        db_ref[...] = dbias

    vm = pl.BlockSpec(memory_space=pltpu.VMEM)
    return pl.pallas_call(body, name=name, in_specs=[vm, vm, vm], out_specs=[vm, vm],
                          out_shape=[jax.ShapeDtypeStruct((hp, s), F32), jax.ShapeDtypeStruct((hp, 1), F32)],
                          compiler_params=pltpu.CompilerParams(vmem_limit_bytes=VMEM_LIMIT))(dc_t, f_t, bias)


def _wide(rep, width):
    return jnp.tile(rep, (1, width // LANES))


def _fold(t):
    part = t[:, :LANES]
    for c in range(1, t.shape[1] // LANES):
        part = part + t[:, c * LANES:(c + 1) * LANES]
    return part


def _fox_logits(q, k, cq_rep, ck_row, mask, hmask):
    s = _dot_nt(_sel(hmask, q), k) + (_wide(cq_rep, ck_row.shape[1]) - ck_row)
    if mask is not None:
        s = jnp.where(mask, s, NEG)
    return s


def _diag_mask(t):
    return lax.broadcasted_iota(jnp.int32, (t, t), 1) <= lax.broadcasted_iota(jnp.int32, (t, t), 0)


def _fox_fwd(hb, c_rep, c_t3, name):
    s = hb.shape[0]
    npair = MIX_W // LANES
    tq = tk = _rows(s)
    nq = s // tq

    def body(q_ref, k_ref, v_ref, cq_ref, ck_ref, o_ref, l_ref, m_s, l_s, acc):
        qi = pl.program_id(1)
        kj = pl.program_id(2)
        hm = _head_masks()

        @pl.when(kj == 0)
        def _():
            m_s[...] = jnp.full_like(m_s, NEG)
            l_s[...] = jnp.zeros_like(l_s)
            acc[...] = jnp.zeros_like(acc)

        def step(mask):
            q, k, v = q_ref[...] * SCALE, k_ref[...], v_ref[...]
            ck = ck_ref[...]
            for h in range(2):
                sc = _fox_logits(q, k, cq_ref[h], ck[h:h + 1, :], mask, hm[h])
                m_old = m_s[h]
                m_new = jnp.maximum(m_old, jnp.max(sc, axis=1, keepdims=True))
                pr = jnp.exp(sc - _wide(m_new, tk))
                corr = jnp.exp(m_old - m_new)
                l_s[h] = l_s[h] * corr + _fold(pr)
                acc[h] = acc[h] * corr + _dot(pr.astype(BF), _sel(hm[h], v))
                m_s[h] = m_new

        @pl.when(kj < qi)
        def _():
            step(None)

        @pl.when(kj == qi)
        def _():
            step(_diag_mask(tq))
            outs = []
            for h in range(2):
                den = jnp.sum(l_s[h], axis=1, keepdims=True)
                outs.append(acc[h] / den)
                l_ref[h] = m_s[h] + jnp.log(den)
            o_ref[...] = jnp.where(hm[0], outs[0], outs[1]).astype(BF)

    def kv_map(off):
        return lambda p, i, j: (jnp.minimum(j, i), off + p)

    blk = pl.BlockSpec((tq, LANES), lambda p, i, j: (i, p))
    return pl.pallas_call(
        body, name=name, grid=(npair, nq, nq),
        in_specs=[blk, pl.BlockSpec((tk, LANES), kv_map(npair)), pl.BlockSpec((tk, LANES), kv_map(2 * npair)),
                  pl.BlockSpec((2, tq, LANES), lambda p, i, j: (p, i, 0)),
                  pl.BlockSpec((None, 2, tk), lambda p, i, j: (p, 0, jnp.minimum(j, i)))],
        out_specs=[blk, pl.BlockSpec((2, tq, LANES), lambda p, i, j: (p, i, 0))],
        out_shape=[jax.ShapeDtypeStruct((s, MIX_W), BF), jax.ShapeDtypeStruct((2 * npair, s, LANES), F32)],
        scratch_shapes=[pltpu.VMEM((2, tq, LANES), F32), pltpu.VMEM((2, tq, LANES), F32),
                        pltpu.VMEM((2, tq, LANES), F32)],
        compiler_params=_cp("parallel", "parallel", "arbitrary"))(hb, hb, hb, c_rep, c_t3)


def _fox_dsum(hb, dcat, lse, c_rep, c_t3, name):
    s = hb.shape[0]
    npair = MIX_W // LANES
    tq = tk = _rows(s)
    nq = s // tq

    def body(q_ref, k_ref, v_ref, do_ref, l_ref, cq_ref, ck_ref, d_ref, acc):
        qi = pl.program_id(1)
        kj = pl.program_id(2)
        hm = _head_masks()

        @pl.when(kj == 0)
        def _():
            acc[...] = jnp.zeros_like(acc)

        def step(mask):
            q, k, v, do = q_ref[...] * SCALE, k_ref[...], v_ref[...], do_ref[...]
            ck = ck_ref[...]
            for h in range(2):
                pr = jnp.exp(_fox_logits(q, k, cq_ref[h], ck[h:h + 1, :], mask, hm[h]) - _wide(l_ref[h], tk))
                acc[h] += _fold(pr * _dot_nt(_sel(hm[h], do), v))

        @pl.when(kj < qi)
        def _():
            step(None)

        @pl.when(kj == qi)
        def _():
            step(_diag_mask(tq))
            for h in range(2):
                d_ref[h] = jnp.broadcast_to(jnp.sum(acc[h], axis=1, keepdims=True), (tq, LANES))

    def kv_map(off):
        return lambda p, i, j: (jnp.minimum(j, i), off + p)

    blk = pl.BlockSpec((tq, LANES), lambda p, i, j: (i, p))
    rep = pl.BlockSpec((2, tq, LANES), lambda p, i, j: (p, i, 0))
    return pl.pallas_call(
        body, name=name, grid=(npair, nq, nq),
        in_specs=[blk, pl.BlockSpec((tk, LANES), kv_map(npair)), pl.BlockSpec((tk, LANES), kv_map(2 * npair)),
                  blk, rep, rep, pl.BlockSpec((None, 2, tk), lambda p, i, j: (p, 0, jnp.minimum(j, i)))],
        out_specs=rep, out_shape=jax.ShapeDtypeStruct((2 * npair, s, LANES), F32),
        scratch_shapes=[pltpu.VMEM((2, tq, LANES), F32)],
        compiler_params=_cp("parallel", "parallel", "arbitrary"))(hb, hb, hb, dcat, lse, c_rep, c_t3)


def _fox_bwd(hb, dcat, dsum, lse, c_rep, c_t3, name):
    s = hb.shape[0]
    npair = MIX_W // LANES
    tq = tk = _rows(s)
    nq = s // tq

    def body(q_ref, k_ref, v_ref, do_ref, d_ref, l_ref, cq_ref, ck_ref, dq_ref, dk_ref, dv_ref, dc_ref):
        kj = pl.program_id(1)
        qi = pl.program_id(2)
        hm = _head_masks()

        @pl.when(qi == 0)
        def _():
            dk_ref[...] = jnp.zeros_like(dk_ref)
            dv_ref[...] = jnp.zeros_like(dv_ref)
            dc_ref[...] = jnp.zeros_like(dc_ref)

        @pl.when((qi == 0) & (kj == 0))
        def _():
            dq_ref[...] = jnp.zeros_like(dq_ref)

        def step(mask):
            q, k, v, do = q_ref[...] * SCALE, k_ref[...], v_ref[...], do_ref[...]
            ck = ck_ref[...]
            dq = jnp.zeros((tq, LANES), F32)
            dk = jnp.zeros((tk, LANES), F32)
            dv = jnp.zeros((tk, LANES), F32)
            dcs = []
            for h in range(2):
                qh, doh = _sel(hm[h], q), _sel(hm[h], do)
                pr = jnp.exp(_fox_logits(q, k, cq_ref[h], ck[h:h + 1, :], mask, hm[h]) - _wide(l_ref[h], tk))
                ds = pr * (_dot_nt(doh, v) - _wide(d_ref[h], tk))
                dsb = ds.astype(BF)
                dq = dq + _dot(dsb, _sel(hm[h], k))
                dk = dk + _dot_tn(dsb, qh)
                dv = dv + _dot_tn(pr.astype(BF), doh)
                dcs.append(jnp.sum(ds, axis=0, keepdims=True))
            rows = pl.ds(pl.multiple_of(qi * tq, tq), tq)
            dq_ref[rows, :] += SCALE * dq
            dk_ref[...] += dk
            dv_ref[...] += dv
            dc_ref[...] -= jnp.concatenate(dcs, axis=0)

        @pl.when(qi > kj)
        def _():
            step(None)

        @pl.when(qi == kj)
        def _():
            step(_diag_mask(tq))

    def q_map(p, j, i):
        return (jnp.maximum(i, j), p)

    kblk = pl.BlockSpec((tk, LANES), lambda p, j, i: (j, p))
    rep = pl.BlockSpec((2, tq, LANES), lambda p, j, i: (p, jnp.maximum(i, j), 0))
    return pl.pallas_call(
        body, name=name, grid=(npair, nq, nq),
        in_specs=[pl.BlockSpec((tq, LANES), q_map),
                  pl.BlockSpec((tk, LANES), lambda p, j, i: (j, npair + p)),
                  pl.BlockSpec((tk, LANES), lambda p, j, i: (j, 2 * npair + p)),
                  pl.BlockSpec((tq, LANES), q_map), rep, rep, rep,
                  pl.BlockSpec((None, 2, tk), lambda p, j, i: (p, 0, j))],
        out_specs=[pl.BlockSpec((s, LANES), lambda p, j, i: (0, p)), kblk, kblk,
                   pl.BlockSpec((None, 2, tk), lambda p, j, i: (p, 0, j))],
        out_shape=[jax.ShapeDtypeStruct((s, MIX_W), F32), jax.ShapeDtypeStruct((s, MIX_W), F32),
                   jax.ShapeDtypeStruct((s, MIX_W), F32), jax.ShapeDtypeStruct((npair, 2, s), F32)],
        compiler_params=_cp("arbitrary", "arbitrary", "arbitrary"))(hb, hb, hb, dcat, dsum, lse, c_rep, c_t3)


def _loss_head(y, target, name):
    s, d = y.shape
    ts = _rows(s)

    def body(y_ref, t_ref, dy_ref, l_ref):
        i = pl.program_id(0)
        e = y_ref[...] - t_ref[...]
        dy_ref[...] = e * (1.0 / d)

        @pl.when(i == 0)
        def _():
            l_ref[...] = jnp.zeros_like(l_ref)

        part = jnp.sum(jnp.sum(e * e, axis=1, keepdims=True), axis=0, keepdims=True)
        l_ref[...] += part * (0.5 / d)

    row = pl.BlockSpec((ts, d), lambda i: (i, 0))
    return pl.pallas_call(
        body, name=name, grid=(s // ts,), in_specs=[row, row],
        out_specs=[row, pl.BlockSpec((1, 1), lambda i: (0, 0))],
        out_shape=[jax.ShapeDtypeStruct((s, d), F32), jax.ShapeDtypeStruct((1, 1), F32)],
        compiler_params=_cp("arbitrary"))(y, target)


def _adam_rows(r, c):
    cap = max(8, (1 << 20) // (4 * c))
    if r <= cap:
        return r
    best = None
    for t in range(8, cap + 1, 8):
        if r % t == 0:
            best = t
    return best if best is not None else r


def _reduce_adamw(contribs, w, m, v, name):
    nl = len(contribs)
    nd, r, c = contribs[0].shape
    tr = _adam_rows(r, c)
    bc1 = 1.0 - ADAM_B1 ** ADAM_STEP
    bc2 = 1.0 - ADAM_B2 ** ADAM_STEP

    def body(*refs):
        c_refs = refs[:nl]
        w_ref, m_ref, v_ref, g_ref, d_ref, nm_ref, nv_ref = refs[nl:]
        l = pl.program_id(0)
        for li in range(nl):
            @pl.when(l == li)
            def _(c_ref=c_refs[li]):
                g = c_ref[0].astype(F32)
                for k in range(1, nd):
                    g = g + c_ref[k].astype(F32)
                nm = ADAM_B1 * m_ref[...] + (1.0 - ADAM_B1) * g
                nv = ADAM_B2 * v_ref[...] + (1.0 - ADAM_B2) * (g * g)
                g_ref[...] = g
                nm_ref[...] = nm
                nv_ref[...] = nv
                d_ref[...] = -ADAM_LR * ((nm / bc1) / (jnp.sqrt(nv / bc2) + ADAM_EPS) + ADAM_WD * w_ref[...])

    def c_spec(li):
        return pl.BlockSpec((nd, tr, c), lambda l, i: (0, jnp.where(l == li, i, 0), 0))

    blk = pl.BlockSpec((None, tr, c), lambda l, i: (l, i, 0))
    out = jax.ShapeDtypeStruct((nl, r, c), F32)
    return pl.pallas_call(
        body, name=name, grid=(nl, r // tr),
        in_specs=[c_spec(li) for li in range(nl)] + [blk, blk, blk],
        out_specs=[blk, blk, blk, blk], out_shape=[out, out, out, out],
        compiler_params=_cp("arbitrary", "arbitrary"))(*contribs, w, m, v)


def _mesh_pos():
    return lax.axis_index("x"), lax.axis_index("y"), lax.axis_index("c")


def _peer(pos, k):
    x, y, c = pos
    return (1 - x if k & 4 else x, 1 - y if k & 2 else y, 1 - c if k & 1 else c)


def _linear(pos):
    return 4 * pos[0] + 2 * pos[1] + pos[2]


def _xfer_copies(srcs, lands, send_sems, recv_sems, local_sems, gather):
    pos = _mesh_pos()
    me = _linear(pos)
    local, remote = [], []
    for i, (src, land) in enumerate(zip(srcs, lands)):
        local.append(pltpu.make_async_copy(src if gather else src.at[me], land.at[me], local_sems.at[i]))
        for k in range(1, N_DEV):
            peer = _peer(pos, k)
            remote.append(pltpu.make_async_remote_copy(
                src_ref=src if gather else src.at[_linear(peer)], dst_ref=land.at[me],
                send_sem=send_sems.at[i * (N_DEV - 1) + k - 1], recv_sem=recv_sems.at[i * (N_DEV - 1) + k - 1],
                device_id=peer, device_id_type=MESH_ID))
    return local, remote


_HBM = pl.BlockSpec(memory_space=pltpu.HBM)
_SEM = pl.BlockSpec(memory_space=pltpu.SEMAPHORE)
_EFFECT = pltpu.SideEffectType.DATAFLOW_SIDE_EFFECTING


def _xfer_start(srcs, gather, name, after=()):
    n = len(srcs)
    na = len(after)
    lands = [lax.empty(((N_DEV,) + a.shape) if gather else a.shape, a.dtype) for a in srcs]

    def body(*refs):
        src, land = refs[:n], refs[n:2 * n]
        send_sems, recv_sems, local_sems = refs[2 * n + na:2 * n + na + 3]
        local, remote = _xfer_copies(src, land, send_sems, recv_sems, local_sems, gather)
        for cp in local + remote:
            cp.start()
        refs[-1][...] = jnp.zeros_like(refs[-1])

    nsem = n * (N_DEV - 1)
    out = pl.pallas_call(
        body, name=name,
        out_shape=(pltpu.SemaphoreType.DMA((nsem,)), pltpu.SemaphoreType.DMA((nsem,)), pltpu.SemaphoreType.DMA((n,)),
                   *[pltpu.HBM(a.shape, a.dtype) for a in srcs], *[pltpu.HBM(a.shape, a.dtype) for a in lands],
                   jax.ShapeDtypeStruct((8, LANES), F32)),
        in_specs=[_HBM] * (2 * n) + [pl.BlockSpec(memory_space=pl.ANY)] * na,
        out_specs=(_SEM, _SEM, _SEM, *[_HBM] * (2 * n), pl.BlockSpec(memory_space=pltpu.VMEM)),
        input_output_aliases={i: 3 + i for i in range(2 * n)},
        compiler_params=pltpu.CompilerParams(has_side_effects=_EFFECT))(
            *[pltpu.with_memory_space_constraint(a, pltpu.HBM) for a in srcs],
            *[pltpu.with_memory_space_constraint(a, pltpu.HBM) for a in lands], *after)
    return out[:3], list(out[3:3 + n]), list(out[3 + n:3 + 2 * n]), out[-1]


def _started(handle):
    return handle[3]


def _xfer_wait(handle, after, gather, name):
    sems, srcs, lands, _ = handle
    n = len(srcs)

    def body(*refs):
        src, land = refs[:n], refs[n:2 * n]
        send_sems, recv_sems, local_sems = refs[2 * n:2 * n + 3]
        local, remote = _xfer_copies(src, land, send_sems, recv_sems, local_sems, gather)
        for cp in local:
            cp.wait()
        for cp in remote:
            cp.wait_send()
            cp.wait_recv()

    out = pl.pallas_call(
        body, name=name,
        out_shape=(*[pltpu.HBM(a.shape, a.dtype) for a in srcs], *[pltpu.HBM(a.shape, a.dtype) for a in lands]),
        in_specs=[_HBM] * (2 * n) + [_SEM] * 3 + [pl.BlockSpec(memory_space=pl.ANY)] * len(after),
        out_specs=tuple([_HBM] * (2 * n)), input_output_aliases={i: i for i in range(2 * n)},
        compiler_params=pltpu.CompilerParams(has_side_effects=_EFFECT))(*srcs, *lands, *sems, *after)
    return list(out[n:])


def _cols_full(g):
    nd, r, c = g.shape
    return jnp.transpose(g, (1, 0, 2)).reshape(r, nd * c)


def _cols_split(full):
    r, n = full.shape
    return jnp.transpose(full.reshape(r, N_DEV, n // N_DEV), (1, 0, 2))


def _pack_b_in(w):
    qkv = 3 * MIX_W
    pad = jnp.zeros((w.shape[0], B_IN_PAD - w.shape[1]), w.dtype)
    return jnp.concatenate([w[:, :qkv], w[:, qkv + N_MIX_HEADS:], w[:, qkv:qkv + N_MIX_HEADS], pad], axis=1)


def _unpack_b_in(w):
    qkv = 3 * MIX_W
    return jnp.concatenate([w[:, :qkv], w[:, qkv + MEM_W:qkv + MEM_W + N_MIX_HEADS], w[:, qkv:qkv + MEM_W]], axis=1)


def _to_classes(t, g):
    r = 4 ** g
    s, w = t.shape
    return jnp.transpose(t.reshape(s // r, r, w), (1, 0, 2)).reshape(s, w)


def _from_classes(t, g):
    r = 4 ** g
    s, w = t.shape
    return jnp.transpose(t.reshape(r, s // r, w), (1, 0, 2)).reshape(s, w)


def _group_stack(t):
    return jnp.stack([_to_classes(t[:, g * GROUP_W:(g + 1) * GROUP_W], g) for g in range(N_GROUPS)])


def _group_unstack(t3):
    return jnp.concatenate([_from_classes(t3[g], g) for g in range(N_GROUPS)], axis=1)


def _same_stack(t):
    return jnp.stack([_to_classes(t, g) for g in range(N_GROUPS)])


def _same_unstack(t3):
    return jnp.stack([_from_classes(t3[g], g) for g in range(N_GROUPS)])


def _ffn_forward(x, xb, wgu, wd4, gain, bias, tag):
    gu, a = _ffn_up(xb, wgu, f"{tag}_up")
    y, yb, xh, rstd = _mm_res_ln(a, wd4, x, gain, bias, 0.5, f"{tag}_down_ln")
    return y, yb, (xb, gu, a, xh, rstd)


def _ffn_backward(dy, saved, wgu, wd4, gain, tag, after=()):
    xb, gu, a, xh, rstd = saved
    s = xb.shape[0]
    nd, d, c = wgu.shape
    dz, dzb, dgain, dbias = _ln_bwd(dy, xh, rstd, gain, 0.5, f"{tag}_ln_bwd", after)
    dh = _ffn_bwd_act(dzb, wd4, gu, f"{tag}_act_bwd").reshape(nd, s, c)
    dwd = _mm_tn(a, dzb[None], f"{tag}_dwd").reshape(nd, wd4.shape[1] // 2, d)
    dx = _mm_nt(dh, wgu, f"{tag}_dx", res=dz)
    dwgu = _mm_tn(xb[None], dh, f"{tag}_dwgu")
    return dx, dwgu, dwd, dgain, dbias


def _mixer_a_forward(x, xb, memb, w_in, w_kv, w_out, gain, bias, tabs):
    h = _mm_nn(xb, w_in, F32, "a_in")
    hb = _rope_cast([h], tabs, 2 * MIX_W // LANES, "a_rope")
    q3 = _group_stack(hb[:, :MIX_W])
    k3 = _group_stack(hb[:, MIX_W:2 * MIX_W])
    v3 = _group_stack(hb[:, 2 * MIX_W:3 * MIX_W])
    o3, l3 = _band_fwd(q3, k3, v3, "a_band_fwd")
    oa, lt = _band_combine(_same_unstack(o3), _same_unstack(l3), "a_combine")
    kv = _mm_nn(memb, w_kv, BF, "a_mem_kv")
    om, lm = _mem_fwd(hb, 3 * MIX_W // LANES, kv, "a_mem_fwd")
    cat = jnp.concatenate([oa, om], axis=1)
    y, yb, xh, rstd = _mm_res_ln(cat[None], w_out[None], x, gain, bias, 1.0, "a_out_ln")
    return y, yb, (xb, hb, q3, k3, v3, oa, lt, kv, lm, cat, xh, rstd)


def _mixer_a_backward(dy, saved, memb, w_in, w_kv, w_out, gain, tabs_neg, after=()):
    xb, hb, q3, k3, v3, oa, lt, kv, lm, cat, xh, rstd = saved
    dz, dzb, dgain, dbias = _ln_bwd(dy, xh, rstd, gain, 1.0, "a_ln_bwd", after)
    dcat = _mm_nt(dzb[None], w_out[None], "a_dcat", out_dtype=BF)
    dw_out = _mm_tn(cat[None], dzb[None], "a_dwout")[0]
    dqm, dkm, dvm = _mem_bwd(hb, 3 * MIX_W // LANES, kv, dcat, cat, GROUP_W // LANES, lm, "a_mem_bwd")
    dkv = jnp.concatenate([dkm, dvm], axis=1).astype(BF)
    dw_kv = _mm_tn(memb[None], dkv[None], "a_dwkv")[0]
    dq3, dk3, dv3 = _band_bwd(q3, k3, v3, _same_stack(dcat[:, :GROUP_W]), _same_stack(oa), _same_stack(lt),
                              "a_band_bwd")
    dhb = _rope_cast([_group_unstack(dq3), _group_unstack(dk3), _group_unstack(dv3), dqm], tabs_neg,
                     2 * MIX_W // LANES, "a_rope_bwd")
    dw_in = _mm_tn(xb[None], dhb[None], "a_dwin")[0]
    dx = _mm_nt(dhb[None], w_in[None], "a_dx", res=dz)
    return dx, dw_in, dw_kv, dw_out, dgain, dbias


def _pad_rows(t, rows):
    return jnp.concatenate([t, jnp.zeros((rows - t.shape[0], t.shape[1]), t.dtype)], axis=0)


def _pad_cols(t, cols):
    return jnp.concatenate([t, jnp.zeros((t.shape[0], cols - t.shape[1]), t.dtype)], axis=1)


def _mixer_b_forward(x, xb, memb, w_in, fbias, w_kv, w_out, gain, bias, tabs):
    s = x.shape[0]
    h = _mm_nn(xb, w_in, F32, "b_in")
    hb = _rope_cast([h], tabs, 0, "b_cast")
    f0 = 3 * MIX_W + MEM_W
    f_t = _pad_rows(jnp.transpose(h[:, f0:f0 + N_MIX_HEADS]), 16)
    bias16 = _pad_rows(jnp.transpose(fbias), 16)
    c_t = _gate_fwd(f_t, bias16, "b_gate_fwd")
    c_t3 = c_t[:N_MIX_HEADS].reshape(N_MIX_HEADS // 2, 2, s)
    c_rep = jnp.broadcast_to(c_t[:N_MIX_HEADS, :, None], (N_MIX_HEADS, s, LANES))
    ob, lb = _fox_fwd(hb, c_rep, c_t3, "b_fox_fwd")
    kv = _mm_nn(memb, w_kv, BF, "b_mem_kv")
    om, lm = _mem_fwd(hb, 3 * MIX_W // LANES, kv, "b_mem_fwd")
    cat = jnp.concatenate([ob, om], axis=1)
    y, yb, xh, rstd = _mm_res_ln(cat[None], w_out[None], x, gain, bias, 1.0, "b_out_ln")
    return y, yb, (xb, hb, f_t, bias16, c_rep, c_t3, lb, kv, lm, cat, xh, rstd)


def _mixer_b_backward(dy, saved, memb, w_in, w_kv, w_out, gain, tabs, after=()):
    xb, hb, f_t, bias16, c_rep, c_t3, lb, kv, lm, cat, xh, rstd = saved
    s = xb.shape[0]
    dz, dzb, dgain, dbias = _ln_bwd(dy, xh, rstd, gain, 1.0, "b_ln_bwd", after)
    dcat = _mm_nt(dzb[None], w_out[None], "b_dcat", out_dtype=BF)
    dw_out = _mm_tn(cat[None], dzb[None], "b_dwout")[0]
    dqm, dkm, dvm = _mem_bwd(hb, 3 * MIX_W // LANES, kv, dcat, cat, MIX_W // LANES, lm, "b_mem_bwd")
    dkv = jnp.concatenate([dkm, dvm], axis=1).astype(BF)
    dw_kv = _mm_tn(memb[None], dkv[None], "b_dwkv")[0]
    dsum = _fox_dsum(hb, dcat, lb, c_rep, c_t3, "b_fox_dsum")
    dq, dk, dv, dc3 = _fox_bwd(hb, dcat, dsum, lb, c_rep, c_t3, "b_fox_bwd")
    df_t, dfb = _gate_bwd(_pad_rows(dc3.reshape(N_MIX_HEADS, s), 16), f_t, bias16, "b_gate_bwd")
    df = _pad_cols(jnp.transpose(df_t[:N_MIX_HEADS]), B_IN_PAD - 3 * MIX_W - MEM_W)
    dhb = _rope_cast([dq, dk, dv, dqm, df], tabs, 0, "b_cast_bwd")
    dw_in = _mm_tn(xb[None], dhb[None], "b_dwin")[0]
    dx = _mm_nt(dhb[None], w_in[None], "b_dx", res=dz)
    return dx, dw_in, jnp.transpose(dfb[:N_MIX_HEADS]), dw_kv, dw_out, dgain, dbias


def _weight_groups(w):
    b = {n: w[n].astype(BF) for n in WEIGHTS if n not in F32_COMM}
    return [
        [b["ffn1_w_gate_up"][0], b["ffn1_w_down"][0], w["ln_gain"], w["ln_bias"]],
        [b["a_w_in"][0], b["a_w_out"][0], b["mem_w_kv"][0]],
        [b["ffn2_w_gate_up"][0], b["ffn2_w_down"][0]],
        [b["ffn1_w_gate_up"][1], b["ffn1_w_down"][1]],
        [b["b_w_in"][0], b["b_w_out"][0], b["mem_w_kv"][1]],
        [b["ffn2_w_gate_up"][1], b["ffn2_w_down"][1]],
    ]


def _local_step(x, mem, target, fbias, get_w, put_g):
    s, d = x.shape
    tabs = _rope_tables(s, 1.0)
    tabs_neg = _rope_tables(s, -1.0)
    memb = mem.astype(BF)
    saved, wl = [], []
    cur, curb = x, x.astype(BF)
    ln_g = ln_b = None
    for i in range(DEPTH):
        g = get_w(3 * i, cur)
        if i == 0:
            ln_g, ln_b = (jnp.transpose(t, (1, 2, 0, 3)).reshape(DEPTH, 3, 1, d) for t in g[2:4])
        w1 = (g[0], g[1].reshape(N_DEV // 2, -1, d))
        cur, curb, s1 = _ffn_forward(cur, curb, w1[0], w1[1], ln_g[i, 0], ln_b[i, 0], f"l{i}_ffn1")
        g = get_w(3 * i + 1, cur)
        if i == 0:
            wm = (_cols_full(g[0]), g[2].reshape(d, -1), _cols_full(g[1]))
            cur, curb, s2 = _mixer_a_forward(cur, curb, memb, wm[0], wm[1], wm[2], ln_g[i, 1], ln_b[i, 1], tabs)
        else:
            wm = (_pack_b_in(g[0].reshape(d, -1)), g[2].reshape(d, -1), g[1].reshape(d, -1))
            cur, curb, s2 = _mixer_b_forward(cur, curb, memb, wm[0], fbias, wm[1], wm[2], ln_g[i, 1], ln_b[i, 1],
                                             tabs)
        g = get_w(3 * i + 2, cur)
        w3 = (g[0], g[1].reshape(N_DEV // 2, -1, d))
        cur, curb, s3 = _ffn_forward(cur, curb, w3[0], w3[1], ln_g[i, 2], ln_b[i, 2], f"l{i}_ffn2")
        saved.append((s1, s2, s3))
        wl.append((w1, wm, w3))

    dy, loss = _loss_head(cur, target, "loss_head")

    dgs = [[None] * 3 for _ in range(DEPTH)]
    dbs = [[None] * 3 for _ in range(DEPTH)]
    sent = ()
    for i in reversed(range(DEPTH)):
        s1, s2, s3 = saved[i]
        w1, wm, w3 = wl[i]
        dy, dgu, dd, dgs[i][2], dbs[i][2] = _ffn_backward(dy, s3, w3[0], w3[1], ln_g[i, 2], f"l{i}_ffn2", sent)
        sent = put_g(3 * i + 2, [dgu, dd])
        if i == 0:
            dy, dw_in, dw_kv, dw_out, dgs[i][1], dbs[i][1] = _mixer_a_backward(
                dy, s2, memb, wm[0], wm[1], wm[2], ln_g[i, 1], tabs_neg, sent)
            sent = put_g(1, [_cols_split(dw_in), _cols_split(dw_out), dw_kv.reshape(N_DEV, d // N_DEV, -1)])
        else:
            dy, dw_in, dfb, dw_kv, dw_out, dgs[i][1], dbs[i][1] = _mixer_b_backward(
                dy, s2, memb, wm[0], wm[1], wm[2], ln_g[i, 1], tabs, sent)
            sent = put_g(4, [_unpack_b_in(dw_in).reshape(N_DEV, d // N_DEV, -1),
                             dw_out.reshape(N_DEV, d // N_DEV, -1), dw_kv.reshape(N_DEV, d // N_DEV, -1),
                             jnp.broadcast_to(dfb[None], (N_DEV,) + dfb.shape)])
        dy, dgu, dd, dgs[i][0], dbs[i][0] = _ffn_backward(dy, s1, w1[0], w1[1], ln_g[i, 0], f"l{i}_ffn1", sent)
        if i == 0:
            ln_pieces = []
            for parts in (dgs, dbs):
                t = jnp.concatenate([parts[a][b] for a in range(DEPTH) for b in range(3)], axis=0)
                ln_pieces.append(jnp.transpose(t.reshape(DEPTH * 3, N_DEV, d // N_DEV), (1, 0, 2)))
            sent = put_g(0, [dgu, dd] + ln_pieces)
        else:
            sent = put_g(3, [dgu, dd])
    return loss, dy


WEIGHTS = ("ffn1_w_gate_up", "ffn1_w_down", "ffn2_w_gate_up", "ffn2_w_down", "ln_gain", "ln_bias", "mem_w_kv",
           "a_w_in", "a_w_out", "b_w_in", "b_forget_bias", "b_w_out")
F32_COMM = ("ln_gain", "ln_bias", "b_forget_bias")
GRAD_SLOTS = {
    "ffn1_w_gate_up": [(0, 0), (3, 0)], "ffn1_w_down": [(0, 1), (3, 1)],
    "ffn2_w_gate_up": [(2, 0), (5, 0)], "ffn2_w_down": [(2, 1), (5, 1)],
    "ln_gain": [(0, 2)], "ln_bias": [(0, 3)], "mem_w_kv": [(1, 2), (4, 2)],
    "a_w_in": [(1, 0)], "a_w_out": [(1, 1)], "b_w_in": [(4, 0)], "b_forget_bias": [(4, 3)], "b_w_out": [(4, 1)],
}


def kernel(x, mem, ffn1_w_gate_up, ffn1_w_down, ffn2_w_gate_up, ffn2_w_down, ln_gain, ln_bias, mem_w_kv, a_w_in, a_w_out, b_w_in, b_forget_bias, b_w_out, loss_target, m_ffn1_w_gate_up, m_ffn1_w_down, m_ffn2_w_gate_up, m_ffn2_w_down, m_ln_gain, m_ln_bias, m_mem_w_kv, m_a_w_in, m_a_w_out, m_b_w_in, m_b_forget_bias, m_b_w_out, v_ffn1_w_gate_up, v_ffn1_w_down, v_ffn2_w_gate_up, v_ffn2_w_down, v_ln_gain, v_ln_bias, v_mem_w_kv, v_a_w_in, v_a_w_out, v_b_w_in, v_b_forget_bias, v_b_w_out):
    w = dict(zip(WEIGHTS, (ffn1_w_gate_up, ffn1_w_down, ffn2_w_gate_up, ffn2_w_down, ln_gain, ln_bias, mem_w_kv,
                           a_w_in, a_w_out, b_w_in, b_forget_bias, b_w_out)))
    m = dict(zip(WEIGHTS, (m_ffn1_w_gate_up, m_ffn1_w_down, m_ffn2_w_gate_up, m_ffn2_w_down, m_ln_gain, m_ln_bias,
                           m_mem_w_kv, m_a_w_in, m_a_w_out, m_b_w_in, m_b_forget_bias, m_b_w_out)))
    v = dict(zip(WEIGHTS, (v_ffn1_w_gate_up, v_ffn1_w_down, v_ffn2_w_gate_up, v_ffn2_w_down, v_ln_gain, v_ln_bias,
                           v_mem_w_kv, v_a_w_in, v_a_w_out, v_b_w_in, v_b_forget_bias, v_b_w_out)))

    gathers = []
    for k, grp in enumerate(_weight_groups(w)):
        gathers.append(_xfer_start(grp, True, f"gather{k}_start", [_started(h) for h in gathers[-1:]]))
    exchanges = {}

    def get_w(k, after):
        behind = [after] + ([_started(h) for h in gathers] if k == 0 else [])
        return _xfer_wait(gathers[k], behind, True, f"gather{k}_wait")

    def put_g(k, pieces):
        exchanges[k] = _xfer_start(pieces, False, f"grads{k}_start")
        return (_started(exchanges[k]),)

    loss, grad_x = _local_step(x[0], mem[0], loss_target[0], b_forget_bias, get_w, put_g)
    loss = lax.psum(loss[0, 0], ("x", "y", "c"))

    outs, landed = {}, {}

    def adamw(names, after):
        for n in names:
            contribs = [landed[g][j] for g, j in GRAD_SLOTS[n]]
            view = (len(contribs),) + contribs[0].shape[1:]
            outs[n] = [t.reshape(w[n].shape) for t in _reduce_adamw(
                contribs, w[n].reshape(view), m[n].reshape(view), v[n].reshape(view), f"adamw_{n}")]
            after = outs[n][0]
        return after

    after = [grad_x, _started(exchanges[0])]
    for k in (5, 4, 3, 2, 1):
        landed[k] = _xfer_wait(exchanges[k], after, False, f"grads{k}_wait")
        after = [landed[k][0]]
    done = adamw(("ffn2_w_gate_up", "ffn2_w_down", "mem_w_kv", "a_w_in", "a_w_out", "b_w_in", "b_forget_bias",
                  "b_w_out"), None)
    landed[0] = _xfer_wait(exchanges[0], [done], False, "grads0_wait")
    adamw(("ffn1_w_gate_up", "ffn1_w_down", "ln_gain", "ln_bias"), None)
    return (loss, grad_x[None], *[outs[n][0] for n in WEIGHTS], *[outs[n][1] for n in WEIGHTS],
            *[outs[n][2] for n in WEIGHTS], *[outs[n][3] for n in WEIGHTS])
```

```python
import functools

import jax
import jax.numpy as jnp
from jax import lax
from jax.experimental import pallas as pl
from jax.experimental.pallas import tpu as pltpu

F32 = jnp.float32
BF = jnp.bfloat16
MESH_ID = pl.DeviceIdType.MESH

N_DEV = 8
DEPTH = 2
HEAD_DIM = 64
LANES = 128
N_MIX_HEADS = 12
N_MEM_HEADS = 4
MIX_W = N_MIX_HEADS * HEAD_DIM
MEM_W = N_MEM_HEADS * HEAD_DIM
N_GROUPS = 3
GROUP_W = MIX_W // N_GROUPS
BLOCK = 128
BAND_SUB = 4
ROT_HALF = 8
ROPE_THETA = 500000.0
ALPHA = (2 * DEPTH) ** 0.25
LN_EPS = 1e-5
SCALE = HEAD_DIM ** -0.5
NEG = -1e30
B_IN_PAD = 2688
ADAM_LR, ADAM_B1, ADAM_B2, ADAM_EPS, ADAM_WD, ADAM_STEP = 0.001, 0.9, 0.999, 1e-08, 0.01, 10
VMEM_LIMIT = 56 * 1024 * 1024


def _cp(*sem):
    return pltpu.CompilerParams(dimension_semantics=sem, vmem_limit_bytes=VMEM_LIMIT)


def _dot(a, b):
    return jnp.dot(a, b, preferred_element_type=F32)


def _dot_nt(a, b):
    return lax.dot_general(a, b, (((1,), (1,)), ((), ())), preferred_element_type=F32)


def _dot_tn(a, b):
    return lax.dot_general(a, b, (((0,), (0,)), ((), ())), preferred_element_type=F32)


def _sigmoid(x):
    return 1.0 / (1.0 + jnp.exp(-x))


def _tile(n, cap=1024):
    if n <= cap:
        return n
    best = LANES
    for t in range(LANES, cap + 1, LANES):
        if n % t == 0:
            best = t
    return best


def _rows(s, cap=512):
    return s if s <= cap else cap


def _mm_nn(a, b, out_dtype, name):
    m, k = a.shape
    n = b.shape[1]
    tm, tn = _rows(m), _tile(n)

    def body(a_ref, b_ref, o_ref):
        o_ref[...] = _dot(a_ref[...], b_ref[...]).astype(o_ref.dtype)

    return pl.pallas_call(
        body, name=name, grid=(n // tn, m // tm),
        in_specs=[pl.BlockSpec((tm, k), lambda j, i: (i, 0)), pl.BlockSpec((k, tn), lambda j, i: (0, j))],
        out_specs=pl.BlockSpec((tm, tn), lambda j, i: (i, j)),
        out_shape=jax.ShapeDtypeStruct((m, n), out_dtype),
        compiler_params=_cp("parallel", "parallel"))(a, b)


def _resident(shape, index_map):
    return pl.BlockSpec(shape, index_map, pipeline_mode=pl.Buffered(1))


def _mm_tn(a, b, name, out_dtype=BF):
    na, s, m = a.shape
    nb, _, n = b.shape
    no = max(na, nb)
    tn = _tile(n)

    def body(a_ref, b_ref, o_ref):
        o_ref[...] = _dot_tn(a_ref[...], b_ref[...]).astype(o_ref.dtype)

    a_spec = (pl.BlockSpec((None, s, m), lambda j, c: (j, 0, 0)) if na > 1
              else _resident((None, s, m), lambda j, c: (0, 0, 0)))
    return pl.pallas_call(
        body, name=name, grid=(no, n // tn),
        in_specs=[a_spec, pl.BlockSpec((None, s, tn), lambda j, c: (j if nb > 1 else 0, 0, c))],
        out_specs=pl.BlockSpec((None, m, tn), lambda j, c: (j, 0, c)),
        out_shape=jax.ShapeDtypeStruct((no, m, n), out_dtype),
        compiler_params=_cp("parallel", "parallel"))(a, b)


def _mm_nt(dh, w, name, res=None, out_dtype=F32):
    nc, s, kc = dh.shape
    d = w.shape[1]
    ts = _rows(s, 256 if nc > 1 else 512)
    has_res = res is not None

    def body(*refs):
        if has_res:
            dh_ref, w_ref, r_ref, o_ref = refs
        else:
            dh_ref, w_ref, o_ref = refs
        out = _dot_nt(dh_ref[0], w_ref[0])
        for j in range(1, nc):
            out = out + _dot_nt(dh_ref[j], w_ref[j])
        if has_res:
            out = out + ALPHA * r_ref[...]
        o_ref[...] = out.astype(o_ref.dtype)

    in_specs = [pl.BlockSpec((nc, ts, kc), lambda i: (0, i, 0)), _resident((nc, d, kc), lambda i: (0, 0, 0))]
    args = [dh, w]
    if has_res:
        in_specs.append(pl.BlockSpec((ts, d), lambda i: (i, 0)))
        args.append(res)
    return pl.pallas_call(
        body, name=name, grid=(s // ts,), in_specs=in_specs,
        out_specs=pl.BlockSpec((ts, d), lambda i: (i, 0)),
        out_shape=jax.ShapeDtypeStruct((s, d), out_dtype),
        compiler_params=_cp("parallel"))(*args)


def _mm_res_ln(a, w, x, gain, bias, fscale, name):
    nc, s, kc = a.shape
    d = w.shape[2]
    ts = _rows(s, 256 if nc > 1 else 512)

    def body(a_ref, w_ref, x_ref, g_ref, b_ref, y_ref, yb_ref, xh_ref, r_ref):
        f = _dot(a_ref[0], w_ref[0])
        for j in range(1, nc):
            f = f + _dot(a_ref[j], w_ref[j])
        z = ALPHA * x_ref[...] + fscale * f
        mu = jnp.mean(z, axis=-1, keepdims=True)
        zc = z - mu
        var = jnp.mean(zc * zc, axis=-1, keepdims=True)
        r = lax.rsqrt(var + LN_EPS)
        xh = zc * r
        y = xh * g_ref[...] + b_ref[...]
        y_ref[...] = y
        yb_ref[...] = y.astype(BF)
        xh_ref[...] = xh
        r_ref[...] = r

    row = pl.BlockSpec((ts, d), lambda i: (i, 0))
    vec = pl.BlockSpec((1, d), lambda i: (0, 0))
    return pl.pallas_call(
        body, name=name, grid=(s // ts,),
        in_specs=[pl.BlockSpec((nc, ts, kc), lambda i: (0, i, 0)), _resident((nc, kc, d), lambda i: (0, 0, 0)),
                  row, vec, vec],
        out_specs=[row, row, row, pl.BlockSpec((ts, 1), lambda i: (i, 0))],
        out_shape=[jax.ShapeDtypeStruct((s, d), F32), jax.ShapeDtypeStruct((s, d), BF),
                   jax.ShapeDtypeStruct((s, d), F32), jax.ShapeDtypeStruct((s, 1), F32)],
        compiler_params=_cp("parallel"))(a, w, x, gain, bias)


def _ln_bwd(dy, xh, rstd, gain, fscale, name, after=()):
    s, d = dy.shape
    ts = _rows(s)
    na = len(after)

    def body(*refs):
        dy_ref, xh_ref, r_ref, g_ref = refs[:4]
        dz_ref, dzb_ref, dg_ref, db_ref = refs[4 + na:]
        i = pl.program_id(0)
        dyv = dy_ref[...]
        xhv = xh_ref[...]
        dxh = dyv * g_ref[...]
        m1 = jnp.mean(dxh, axis=-1, keepdims=True)
        m2 = jnp.mean(dxh * xhv, axis=-1, keepdims=True)
        dz = r_ref[...] * (dxh - m1 - xhv * m2)
        dz_ref[...] = dz
        dzb_ref[...] = (fscale * dz).astype(BF)

        @pl.when(i == 0)
        def _():
            dg_ref[...] = jnp.zeros_like(dg_ref)
            db_ref[...] = jnp.zeros_like(db_ref)

        dg_ref[...] += jnp.sum(dyv * xhv, axis=0, keepdims=True)
        db_ref[...] += jnp.sum(dyv, axis=0, keepdims=True)

    row = pl.BlockSpec((ts, d), lambda i: (i, 0))
    vec = pl.BlockSpec((1, d), lambda i: (0, 0))
    return pl.pallas_call(
        body, name=name, grid=(s // ts,),
        in_specs=[row, row, pl.BlockSpec((ts, 1), lambda i: (i, 0)), vec] + [pl.BlockSpec(memory_space=pl.ANY)] * na,
        out_specs=[row, row, vec, vec],
        out_shape=[jax.ShapeDtypeStruct((s, d), F32), jax.ShapeDtypeStruct((s, d), BF),
                   jax.ShapeDtypeStruct((1, d), F32), jax.ShapeDtypeStruct((1, d), F32)],
        compiler_params=_cp("arbitrary"))(dy, xh, rstd, gain, *after)


def _ffn_up(xb, wgu, name):
    s, d = xb.shape
    c = wgu.shape[2]
    nch = wgu.shape[0] // 2
    ts = _rows(s)
    w4 = wgu.reshape(2, nch, d, c)

    def body(x_ref, w_ref, gu_ref, a_ref):
        x = x_ref[...]
        g = _dot(x, w_ref[0])
        u = _dot(x, w_ref[1])
        gu_ref[0] = g.astype(BF)
        gu_ref[1] = u.astype(BF)
        a_ref[...] = (g * _sigmoid(g) * u).astype(BF)

    return pl.pallas_call(
        body, name=name, grid=(nch, s // ts),
        in_specs=[pl.BlockSpec((ts, d), lambda j, i: (i, 0)),
                  pl.BlockSpec((2, None, d, c), lambda j, i: (0, j, 0, 0))],
        out_specs=[pl.BlockSpec((2, None, ts, c), lambda j, i: (0, j, i, 0)),
                   pl.BlockSpec((None, ts, c), lambda j, i: (j, i, 0))],
        out_shape=[jax.ShapeDtypeStruct((2, nch, s, c), BF), jax.ShapeDtypeStruct((nch, s, c), BF)],
        compiler_params=_cp("parallel", "parallel"))(xb, w4)


def _ffn_bwd_act(dzb, wd4, gu, name):
    s, d = dzb.shape
    nch, c = wd4.shape[0], wd4.shape[1]
    ts = _rows(s)

    def body(dz_ref, w_ref, gu_ref, dh_ref):
        da = _dot_nt(dz_ref[...], w_ref[...])
        g = gu_ref[0].astype(F32)
        u = gu_ref[1].astype(F32)
        sg = _sigmoid(g)
        dh_ref[0] = (da * u * sg * (1.0 + g * (1.0 - sg))).astype(BF)
        dh_ref[1] = (da * g * sg).astype(BF)

    return pl.pallas_call(
        body, name=name, grid=(nch, s // ts),
        in_specs=[pl.BlockSpec((ts, d), lambda j, i: (i, 0)),
                  pl.BlockSpec((None, c, d), lambda j, i: (j, 0, 0)),
                  pl.BlockSpec((2, None, ts, c), lambda j, i: (0, j, i, 0))],
        out_specs=pl.BlockSpec((2, None, ts, c), lambda j, i: (0, j, i, 0)),
        out_shape=jax.ShapeDtypeStruct((2, nch, s, c), BF),
        compiler_params=_cp("parallel", "parallel"))(dzb, wd4, gu)


def _rope_tables(s, sign):
    pos = jnp.arange(s, dtype=F32)
    inv_freq = 1.0 / (ROPE_THETA ** (jnp.arange(ROT_HALF, dtype=F32) / ROT_HALF))
    ang = pos[:, None] * inv_freq[None, :]
    cos, sin = jnp.cos(ang), jnp.sin(ang) * sign
    one = jnp.ones((s, HEAD_DIM - 2 * ROT_HALF), F32)
    zero = jnp.zeros((s, HEAD_DIM - 2 * ROT_HALF), F32)
    zh = jnp.zeros((s, ROT_HALF), F32)
    cos_f = jnp.concatenate([cos, cos, one], axis=1)
    sin_a = jnp.concatenate([-sin, zh, zero], axis=1)
    sin_b = jnp.concatenate([zh, sin, zero], axis=1)
    rep = LANES // HEAD_DIM
    return tuple(jnp.tile(t, (1, rep)) for t in (cos_f, sin_a, sin_b))


def _rope_cast(parts, tabs, n_rope, name):
    s = parts[0].shape[0]
    widths = [p.shape[1] for p in parts]
    n = sum(widths)
    npart = len(parts)
    ts = _rows(s, 256)

    def body(*refs):
        part_refs = refs[:npart]
        c_ref, sa_ref, sb_ref, o_ref = refs[npart:]
        col = 0
        for ref, w in zip(part_refs, widths):
            for j in range(w // LANES):
                t = ref[:, j * LANES:(j + 1) * LANES]
                if col < n_rope:
                    t = (t * c_ref[...] + pltpu.roll(t, LANES - ROT_HALF, 1) * sa_ref[...]
                         + pltpu.roll(t, ROT_HALF, 1) * sb_ref[...])
                o_ref[:, col * LANES:(col + 1) * LANES] = t.astype(BF)
                col += 1

    tab = pl.BlockSpec((ts, LANES), lambda i: (i, 0))
    return pl.pallas_call(
        body, name=name, grid=(s // ts,),
        in_specs=[pl.BlockSpec((ts, w), lambda i: (i, 0)) for w in widths] + [tab, tab, tab],
        out_specs=pl.BlockSpec((ts, n), lambda i: (i, 0)),
        out_shape=jax.ShapeDtypeStruct((s, n), BF),
        compiler_params=_cp("parallel"))(*parts, *tabs)


def _head_masks():
    lane = lax.broadcasted_iota(jnp.int32, (1, LANES), 1)
    return [lane < HEAD_DIM, lane >= HEAD_DIM]


def _sel(mask, v):
    return jnp.where(mask, v, jnp.zeros_like(v))


def _pick(mask, wide, fill):
    return jnp.max(jnp.where(mask, wide, fill), axis=1, keepdims=True)


def _band_masks(has_other, prev):
    qi = lax.broadcasted_iota(jnp.int32, (BLOCK, BLOCK), 0)
    kj = lax.broadcasted_iota(jnp.int32, (BLOCK, BLOCK), 1)
    if prev:
        return kj >= qi + jnp.where(has_other, 0, BLOCK)
    return kj <= qi


def _band_fwd(q3, k3, v3, name):
    ng, s, w = q3.shape
    nb = s // BLOCK
    npair = w // LANES
    nsub = BAND_SUB
    tile = nsub * BLOCK

    def body(q_ref, kc_ref, kp_ref, vc_ref, vp_ref, o_ref, l_ref):
        g = pl.program_id(0)
        t = pl.program_id(2)
        nbl = jnp.right_shift(nb, 2 * g)
        mc = _band_masks(None, False)
        hm = _head_masks()
        for i in range(nsub):
            rows = slice(i * BLOCK, (i + 1) * BLOCK)
            has_prev = jnp.bitwise_and(t * nsub + i, nbl - 1) != 0
            mp = _band_masks(has_prev, True)
            q, kc, vc = q_ref[rows, :], kc_ref[rows, :], vc_ref[rows, :]
            if i == 0:
                kp, vp = kp_ref[...], vp_ref[...]
            else:
                prev = slice((i - 1) * BLOCK, i * BLOCK)
                kp, vp = kc_ref[prev, :], vc_ref[prev, :]
            o = jnp.zeros((BLOCK, LANES), F32)
            lse_w = jnp.zeros((BLOCK, LANES), F32)
            for h in range(2):
                qh = _sel(hm[h], q)
                sc = jnp.where(mc, _dot_nt(qh, kc) * SCALE, NEG)
                sp = jnp.where(mp, _dot_nt(qh, kp) * SCALE, NEG)
                m = jnp.maximum(jnp.max(sc, axis=1, keepdims=True), jnp.max(sp, axis=1, keepdims=True))
                pc = jnp.exp(sc - m)
                pp = jnp.exp(sp - m)
                l = jnp.sum(pc, axis=1, keepdims=True) + jnp.sum(pp, axis=1, keepdims=True)
                oh = _dot(pc.astype(BF), _sel(hm[h], vc)) + _dot(pp.astype(BF), _sel(hm[h], vp))
                o = o + oh / l
                lse_w = jnp.where(hm[h], m + jnp.log(l), lse_w)
            o_ref[rows, :] = o
            l_ref[rows, :] = lse_w

    cur = pl.BlockSpec((None, tile, LANES), lambda g, p, t: (g, t, p))
    prv = pl.BlockSpec((None, BLOCK, LANES), lambda g, p, t: (g, jnp.maximum(t * nsub - 1, 0), p))
    return pl.pallas_call(
        body, name=name, grid=(ng, npair, nb // nsub),
        in_specs=[cur, cur, prv, cur, prv], out_specs=[cur, cur],
        out_shape=[jax.ShapeDtypeStruct((ng, s, w), F32), jax.ShapeDtypeStruct((ng, s, w), F32)],
        compiler_params=_cp("parallel", "parallel", "parallel"))(q3, k3, k3, v3, v3)


def _band_combine(o3, l3, name):
    ng, s, w = o3.shape
    ts = _rows(s)

    def body(o_ref, l_ref, oa_ref, lt_ref):
        ls = [l_ref[g] for g in range(ng)]
        m = functools.reduce(jnp.maximum, ls)
        es = [jnp.exp(l - m) for l in ls]
        den = functools.reduce(lambda a, b: a + b, es)
        num = functools.reduce(lambda a, b: a + b, [es[g] * o_ref[g] for g in range(ng)])
        oa_ref[...] = (num / den).astype(BF)
        lt_ref[...] = m + jnp.log(den)

    blk3 = pl.BlockSpec((ng, ts, w), lambda i: (0, i, 0))
    blk = pl.BlockSpec((ts, w), lambda i: (i, 0))
    return pl.pallas_call(
        body, name=name, grid=(s // ts,), in_specs=[blk3, blk3], out_specs=[blk, blk],
        out_shape=[jax.ShapeDtypeStruct((s, w), BF), jax.ShapeDtypeStruct((s, w), F32)],
        compiler_params=_cp("parallel"))(o3, l3)


def _band_bwd(q3, k3, v3, do3, oa3, lt3, name):
    ng, s, w = q3.shape
    nb = s // BLOCK
    npair = w // LANES
    nsub = BAND_SUB
    tile = nsub * BLOCK

    def body(q_ref, qn_ref, kc_ref, kp_ref, vc_ref, vp_ref, do_ref, don_ref, oa_ref, oan_ref, lt_ref, ltn_ref,
             dq_ref, dk_ref, dv_ref):
        g = pl.program_id(0)
        t = pl.program_id(2)
        nbl = jnp.right_shift(nb, 2 * g)
        mc = _band_masks(None, False)
        hm = _head_masks()

        def block(ref, edge_ref, i):
            if i < 0 or i >= nsub:
                return edge_ref[...]
            return ref[i * BLOCK:(i + 1) * BLOCK, :]

        for i in range(nsub):
            b = t * nsub + i
            mp = _band_masks(jnp.bitwise_and(b, nbl - 1) != 0, True)
            mn = _band_masks(jnp.bitwise_and(b + 1, nbl - 1) != 0, True)
            q, qn = block(q_ref, None, i), block(q_ref, qn_ref, i + 1)
            kc, kp = block(kc_ref, None, i), block(kc_ref, kp_ref, i - 1)
            vc, vp = block(vc_ref, None, i), block(vc_ref, vp_ref, i - 1)
            do, don = block(do_ref, None, i), block(do_ref, don_ref, i + 1)
            dd = do.astype(F32) * block(oa_ref, None, i).astype(F32)
            ddn = don.astype(F32) * block(oa_ref, oan_ref, i + 1).astype(F32)
            lt, ltn = block(lt_ref, None, i), block(lt_ref, ltn_ref, i + 1)
            dq = jnp.zeros((BLOCK, LANES), F32)
            dk = jnp.zeros((BLOCK, LANES), F32)
            dv = jnp.zeros((BLOCK, LANES), F32)
            for h in range(2):
                qh, doh = _sel(hm[h], q), _sel(hm[h], do)
                qnh, donh = _sel(hm[h], qn), _sel(hm[h], don)
                kch, kph = _sel(hm[h], kc), _sel(hm[h], kp)
                lse = _pick(hm[h], lt, NEG)
                lsen = _pick(hm[h], ltn, NEG)
                dsum = jnp.sum(_sel(hm[h], dd), axis=1, keepdims=True)
                dsumn = jnp.sum(_sel(hm[h], ddn), axis=1, keepdims=True)
                pc = jnp.exp(jnp.where(mc, _dot_nt(qh, kc) * SCALE, NEG) - lse)
                pp = jnp.exp(jnp.where(mp, _dot_nt(qh, kp) * SCALE, NEG) - lse)
                dsc = pc * (_dot_nt(doh, vc) - dsum)
                dsp = pp * (_dot_nt(doh, vp) - dsum)
                dq = dq + SCALE * (_dot(dsc.astype(BF), kch) + _dot(dsp.astype(BF), kph))
                pn = jnp.exp(jnp.where(mn, _dot_nt(qnh, kc) * SCALE, NEG) - lsen)
                dsn = pn * (_dot_nt(donh, vc) - dsumn)
                dk = dk + SCALE * (_dot_tn(dsc.astype(BF), qh) + _dot_tn(dsn.astype(BF), qnh))
                dv = dv + _dot_tn(pc.astype(BF), doh) + _dot_tn(pn.astype(BF), donh)
            rows = slice(i * BLOCK, (i + 1) * BLOCK)
            dq_ref[rows, :] = dq
            dk_ref[rows, :] = dk
            dv_ref[rows, :] = dv

    cur = pl.BlockSpec((None, tile, LANES), lambda g, p, t: (g, t, p))
    prv = pl.BlockSpec((None, BLOCK, LANES), lambda g, p, t: (g, jnp.maximum(t * nsub - 1, 0), p))
    nxt = pl.BlockSpec((None, BLOCK, LANES), lambda g, p, t: (g, jnp.minimum(t * nsub + nsub, nb - 1), p))
    out = jax.ShapeDtypeStruct((ng, s, w), F32)
    return pl.pallas_call(
        body, name=name, grid=(ng, npair, nb // nsub),
        in_specs=[cur, nxt, cur, prv, cur, prv, cur, nxt, cur, nxt, cur, nxt],
        out_specs=[cur, cur, cur], out_shape=[out, out, out],
        compiler_params=_cp("parallel", "parallel", "parallel"))(
            q3, q3, k3, k3, v3, v3, do3, do3, oa3, oa3, lt3, lt3)


def _mem_fwd(hb, q_blk0, kv, name):
    s = hb.shape[0]
    m = kv.shape[0]
    tq = _rows(s)
    npair = MEM_W // LANES

    def body(q_ref, k_ref, v_ref, o_ref, l_ref):
        q, k, v = q_ref[...], k_ref[...], v_ref[...]
        hm = _head_masks()
        o = jnp.zeros((tq, LANES), F32)
        lse_w = jnp.zeros((tq, LANES), F32)
        for h in range(2):
            sc = _dot_nt(_sel(hm[h], q), k) * SCALE
            mx = jnp.max(sc, axis=1, keepdims=True)
            p = jnp.exp(sc - mx)
            l = jnp.sum(p, axis=1, keepdims=True)
            o = o + _dot(p.astype(BF), _sel(hm[h], v)) / l
            lse_w = jnp.where(hm[h], mx + jnp.log(l), lse_w)
        o_ref[...] = o.astype(BF)
        l_ref[...] = lse_w

    blk = pl.BlockSpec((tq, LANES), lambda p, i: (i, p))
    return pl.pallas_call(
        body, name=name, grid=(npair, s // tq),
        in_specs=[pl.BlockSpec((tq, LANES), lambda p, i: (i, q_blk0 + p)),
                  pl.BlockSpec((m, LANES), lambda p, i: (0, p)),
                  pl.BlockSpec((m, LANES), lambda p, i: (0, npair + p))],
        out_specs=[blk, blk],
        out_shape=[jax.ShapeDtypeStruct((s, MEM_W), BF), jax.ShapeDtypeStruct((s, MEM_W), F32)],
        compiler_params=_cp("parallel", "parallel"))(hb, kv, kv)


def _mem_bwd(hb, q_blk0, kv, dcat, cat, o_blk0, lse, name):
    s = hb.shape[0]
    m = kv.shape[0]
    tq = _rows(s)
    npair = MEM_W // LANES

    def body(q_ref, k_ref, v_ref, do_ref, o_ref, l_ref, dq_ref, dk_ref, dv_ref):
        i = pl.program_id(1)

        @pl.when(i == 0)
        def _():
            dk_ref[...] = jnp.zeros_like(dk_ref)
            dv_ref[...] = jnp.zeros_like(dv_ref)

        q, k, v, do = q_ref[...], k_ref[...], v_ref[...], do_ref[...]
        dd = do.astype(F32) * o_ref[...].astype(F32)
        lt = l_ref[...]
        hm = _head_masks()
        dq = jnp.zeros((tq, LANES), F32)
        dk = jnp.zeros((m, LANES), F32)
        dv = jnp.zeros((m, LANES), F32)
        for h in range(2):
            qh, doh = _sel(hm[h], q), _sel(hm[h], do)
            p = jnp.exp(_dot_nt(qh, k) * SCALE - _pick(hm[h], lt, NEG))
            ds = p * (_dot_nt(doh, v) - jnp.sum(_sel(hm[h], dd), axis=1, keepdims=True))
            dq = dq + SCALE * _dot(ds.astype(BF), _sel(hm[h], k))
            dk = dk + SCALE * _dot_tn(ds.astype(BF), qh)
            dv = dv + _dot_tn(p.astype(BF), doh)
        dq_ref[...] = dq
        dk_ref[...] += dk
        dv_ref[...] += dv

    row = pl.BlockSpec((tq, LANES), lambda p, i: (i, p))
    orow = pl.BlockSpec((tq, LANES), lambda p, i: (i, o_blk0 + p))
    acc = pl.BlockSpec((m, LANES), lambda p, i: (0, p))
    return pl.pallas_call(
        body, name=name, grid=(npair, s // tq),
        in_specs=[pl.BlockSpec((tq, LANES), lambda p, i: (i, q_blk0 + p)),
                  pl.BlockSpec((m, LANES), lambda p, i: (0, p)),
                  pl.BlockSpec((m, LANES), lambda p, i: (0, npair + p)), orow, orow, row],
        out_specs=[row, acc, acc],
        out_shape=[jax.ShapeDtypeStruct((s, MEM_W), F32), jax.ShapeDtypeStruct((m, MEM_W), F32),
                   jax.ShapeDtypeStruct((m, MEM_W), F32)],
        compiler_params=_cp("parallel", "arbitrary"))(hb, kv, kv, dcat, cat, lse)


def _gate_fwd(f_t, bias, name):
    hp, s = f_t.shape
    nblk = s // LANES

    def body(f_ref, b_ref, c_ref):
        lane = lax.broadcasted_iota(jnp.int32, (hp, LANES), 1)

        def step(i, carry):
            off = pl.multiple_of(i * LANES, LANES)
            x = f_ref[:, pl.ds(off, LANES)] + b_ref[...]
            acc = jnp.minimum(x, 0.0) - jnp.log(1.0 + jnp.exp(-jnp.abs(x)))
            sh = 1
            while sh < LANES:
                acc = acc + jnp.where(lane >= sh, pltpu.roll(acc, sh, 1), 0.0)
                sh *= 2
            acc = acc + carry
            c_ref[:, pl.ds(off, LANES)] = acc
            return acc[:, LANES - 1:LANES]

        lax.fori_loop(0, nblk, step, jnp.zeros((hp, 1), F32))

    vm = pl.BlockSpec(memory_space=pltpu.VMEM)
    return pl.pallas_call(body, name=name, in_specs=[vm, vm], out_specs=vm,
                          out_shape=jax.ShapeDtypeStruct((hp, s), F32),
                          compiler_params=pltpu.CompilerParams(vmem_limit_bytes=VMEM_LIMIT))(f_t, bias)


def _gate_bwd(dc_t, f_t, bias, name):
    hp, s = f_t.shape
    nblk = s // LANES

    def body(dc_ref, f_ref, b_ref, df_ref, db_ref):
        lane = lax.broadcasted_iota(jnp.int32, (hp, LANES), 1)

        def step(t, carry):
            suffix, dbias = carry
            off = pl.multiple_of((nblk - 1 - t) * LANES, LANES)
            acc = dc_ref[:, pl.ds(off, LANES)]
            sh = 1
            while sh < LANES:
                acc = acc + jnp.where(lane < LANES - sh, pltpu.roll(acc, LANES - sh, 1), 0.0)
                sh *= 2
            acc = acc + suffix
            x = f_ref[:, pl.ds(off, LANES)] + b_ref[...]
            df = acc * _sigmoid(-x)
            df_ref[:, pl.ds(off, LANES)] = df
            return acc[:, 0:1], dbias + jnp.sum(df, axis=1, keepdims=True)

        _, dbias = lax.fori_loop(0, nblk, step, (jnp.zeros((hp, 1), F32), jnp.zeros((hp, 1), F32)))
        db_ref[...] = dbias

    vm = pl.BlockSpec(memory_space=pltpu.VMEM)
    return pl.pallas_call(body, name=name, in_specs=[vm, vm, vm], out_specs=[vm, vm],
                          out_shape=[jax.ShapeDtypeStruct((hp, s), F32), jax.ShapeDtypeStruct((hp, 1), F32)],
                          compiler_params=pltpu.CompilerParams(vmem_limit_bytes=VMEM_LIMIT))(dc_t, f_t, bias)


def _wide(rep, width):
    return jnp.tile(rep, (1, width // LANES))


def _fold(t):
    part = t[:, :LANES]
    for c in range(1, t.shape[1] // LANES):
        part = part + t[:, c * LANES:(c + 1) * LANES]
    return part


def _fox_logits(q, k, cq_rep, ck_row, mask, hmask):
    s = _dot_nt(_sel(hmask, q), k) + (_wide(cq_rep, ck_row.shape[1]) - ck_row)
    if mask is not None:
        s = jnp.where(mask, s, NEG)
    return s


def _diag_mask(t):
    return lax.broadcasted_iota(jnp.int32, (t, t), 1) <= lax.broadcasted_iota(jnp.int32, (t, t), 0)


def _fox_fwd(hb, c_rep, c_t3, name):
    s = hb.shape[0]
    npair = MIX_W // LANES
    tq = tk = _rows(s)
    nq = s // tq

    def body(q_ref, k_ref, v_ref, cq_ref, ck_ref, o_ref, l_ref, m_s, l_s, acc):
        qi = pl.program_id(1)
        kj = pl.program_id(2)
        hm = _head_masks()

        @pl.when(kj == 0)
        def _():
            m_s[...] = jnp.full_like(m_s, NEG)
            l_s[...] = jnp.zeros_like(l_s)
            acc[...] = jnp.zeros_like(acc)

        def step(mask):
            q, k, v = q_ref[...] * SCALE, k_ref[...], v_ref[...]
            ck = ck_ref[...]
            for h in range(2):
                sc = _fox_logits(q, k, cq_ref[h], ck[h:h + 1, :], mask, hm[h])
                m_old = m_s[h]
                m_new = jnp.maximum(m_old, jnp.max(sc, axis=1, keepdims=True))
                pr = jnp.exp(sc - _wide(m_new, tk))
                corr = jnp.exp(m_old - m_new)
                l_s[h] = l_s[h] * corr + _fold(pr)
                acc[h] = acc[h] * corr + _dot(pr.astype(BF), _sel(hm[h], v))
                m_s[h] = m_new

        @pl.when(kj < qi)
        def _():
            step(None)

        @pl.when(kj == qi)
        def _():
            step(_diag_mask(tq))
            outs = []
            for h in range(2):
                den = jnp.sum(l_s[h], axis=1, keepdims=True)
                outs.append(acc[h] / den)
                l_ref[h] = m_s[h] + jnp.log(den)
            o_ref[...] = jnp.where(hm[0], outs[0], outs[1]).astype(BF)

    def kv_map(off):
        return lambda p, i, j: (jnp.minimum(j, i), off + p)

    blk = pl.BlockSpec((tq, LANES), lambda p, i, j: (i, p))
    return pl.pallas_call(
        body, name=name, grid=(npair, nq, nq),
        in_specs=[blk, pl.BlockSpec((tk, LANES), kv_map(npair)), pl.BlockSpec((tk, LANES), kv_map(2 * npair)),
                  pl.BlockSpec((2, tq, LANES), lambda p, i, j: (p, i, 0)),
                  pl.BlockSpec((None, 2, tk), lambda p, i, j: (p, 0, jnp.minimum(j, i)))],
        out_specs=[blk, pl.BlockSpec((2, tq, LANES), lambda p, i, j: (p, i, 0))],
        out_shape=[jax.ShapeDtypeStruct((s, MIX_W), BF), jax.ShapeDtypeStruct((2 * npair, s, LANES), F32)],
        scratch_shapes=[pltpu.VMEM((2, tq, LANES), F32), pltpu.VMEM((2, tq, LANES), F32),
                        pltpu.VMEM((2, tq, LANES), F32)],
        compiler_params=_cp("parallel", "parallel", "arbitrary"))(hb, hb, hb, c_rep, c_t3)


def _fox_dsum(hb, dcat, lse, c_rep, c_t3, name):
    s = hb.shape[0]
    npair = MIX_W // LANES
    tq = tk = _rows(s)
    nq = s // tq

    def body(q_ref, k_ref, v_ref, do_ref, l_ref, cq_ref, ck_ref, d_ref, acc):
        qi = pl.program_id(1)
        kj = pl.program_id(2)
        hm = _head_masks()

        @pl.when(kj == 0)
        def _():
            acc[...] = jnp.zeros_like(acc)

        def step(mask):
            q, k, v, do = q_ref[...] * SCALE, k_ref[...], v_ref[...], do_ref[...]
            ck = ck_ref[...]
            for h in range(2):
                pr = jnp.exp(_fox_logits(q, k, cq_ref[h], ck[h:h + 1, :], mask, hm[h]) - _wide(l_ref[h], tk))
                acc[h] += _fold(pr * _dot_nt(_sel(hm[h], do), v))

        @pl.when(kj < qi)
        def _():
            step(None)

        @pl.when(kj == qi)
        def _():
            step(_diag_mask(tq))
            for h in range(2):
                d_ref[h] = jnp.broadcast_to(jnp.sum(acc[h], axis=1, keepdims=True), (tq, LANES))

    def kv_map(off):
        return lambda p, i, j: (jnp.minimum(j, i), off + p)

    blk = pl.BlockSpec((tq, LANES), lambda p, i, j: (i, p))
    rep = pl.BlockSpec((2, tq, LANES), lambda p, i, j: (p, i, 0))
    return pl.pallas_call(
        body, name=name, grid=(npair, nq, nq),
        in_specs=[blk, pl.BlockSpec((tk, LANES), kv_map(npair)), pl.BlockSpec((tk, LANES), kv_map(2 * npair)),
                  blk, rep, rep, pl.BlockSpec((None, 2, tk), lambda p, i, j: (p, 0, jnp.minimum(j, i)))],
        out_specs=rep, out_shape=jax.ShapeDtypeStruct((2 * npair, s, LANES), F32),
        scratch_shapes=[pltpu.VMEM((2, tq, LANES), F32)],
        compiler_params=_cp("parallel", "parallel", "arbitrary"))(hb, hb, hb, dcat, lse, c_rep, c_t3)


def _fox_bwd(hb, dcat, dsum, lse, c_rep, c_t3, name):
    s = hb.shape[0]
    npair = MIX_W // LANES
    tq = tk = _rows(s)
    nq = s // tq

    def body(q_ref, k_ref, v_ref, do_ref, d_ref, l_ref, cq_ref, ck_ref, dq_ref, dk_ref, dv_ref, dc_ref):
        kj = pl.program_id(1)
        qi = pl.program_id(2)
        hm = _head_masks()

        @pl.when(qi == 0)
        def _():
            dk_ref[...] = jnp.zeros_like(dk_ref)
            dv_ref[...] = jnp.zeros_like(dv_ref)
            dc_ref[...] = jnp.zeros_like(dc_ref)

        @pl.when((qi == 0) & (kj == 0))
        def _():
            dq_ref[...] = jnp.zeros_like(dq_ref)

        def step(mask):
            q, k, v, do = q_ref[...] * SCALE, k_ref[...], v_ref[...], do_ref[...]
            ck = ck_ref[...]
            dq = jnp.zeros((tq, LANES), F32)
            dk = jnp.zeros((tk, LANES), F32)
            dv = jnp.zeros((tk, LANES), F32)
            dcs = []
            for h in range(2):
                qh, doh = _sel(hm[h], q), _sel(hm[h], do)
                pr = jnp.exp(_fox_logits(q, k, cq_ref[h], ck[h:h + 1, :], mask, hm[h]) - _wide(l_ref[h], tk))
                ds = pr * (_dot_nt(doh, v) - _wide(d_ref[h], tk))
                dsb = ds.astype(BF)
                dq = dq + _dot(dsb, _sel(hm[h], k))
                dk = dk + _dot_tn(dsb, qh)
                dv = dv + _dot_tn(pr.astype(BF), doh)
                dcs.append(jnp.sum(ds, axis=0, keepdims=True))
            rows = pl.ds(pl.multiple_of(qi * tq, tq), tq)
            dq_ref[rows, :] += SCALE * dq
            dk_ref[...] += dk
            dv_ref[...] += dv
            dc_ref[...] -= jnp.concatenate(dcs, axis=0)

        @pl.when(qi > kj)
        def _():
            step(None)

        @pl.when(qi == kj)
        def _():
            step(_diag_mask(tq))

    def q_map(p, j, i):
        return (jnp.maximum(i, j), p)

    kblk = pl.BlockSpec((tk, LANES), lambda p, j, i: (j, p))
    rep = pl.BlockSpec((2, tq, LANES), lambda p, j, i: (p, jnp.maximum(i, j), 0))
    return pl.pallas_call(
        body, name=name, grid=(npair, nq, nq),
        in_specs=[pl.BlockSpec((tq, LANES), q_map),
                  pl.BlockSpec((tk, LANES), lambda p, j, i: (j, npair + p)),
                  pl.BlockSpec((tk, LANES), lambda p, j, i: (j, 2 * npair + p)),
                  pl.BlockSpec((tq, LANES), q_map), rep, rep, rep,
                  pl.BlockSpec((None, 2, tk), lambda p, j, i: (p, 0, j))],
        out_specs=[pl.BlockSpec((s, LANES), lambda p, j, i: (0, p)), kblk, kblk,
                   pl.BlockSpec((None, 2, tk), lambda p, j, i: (p, 0, j))],
        out_shape=[jax.ShapeDtypeStruct((s, MIX_W), F32), jax.ShapeDtypeStruct((s, MIX_W), F32),
                   jax.ShapeDtypeStruct((s, MIX_W), F32), jax.ShapeDtypeStruct((npair, 2, s), F32)],
        compiler_params=_cp("arbitrary", "arbitrary", "arbitrary"))(hb, hb, hb, dcat, dsum, lse, c_rep, c_t3)


def _loss_head(y, target, name):
    s, d = y.shape
    ts = _rows(s)

    def body(y_ref, t_ref, dy_ref, l_ref):
        i = pl.program_id(0)
        e = y_ref[...] - t_ref[...]
        dy_ref[...] = e * (1.0 / d)

        @pl.when(i == 0)
        def _():
            l_ref[...] = jnp.zeros_like(l_ref)

        part = jnp.sum(jnp.sum(e * e, axis=1, keepdims=True), axis=0, keepdims=True)
        l_ref[...] += part * (0.5 / d)

    row = pl.BlockSpec((ts, d), lambda i: (i, 0))
    return pl.pallas_call(
        body, name=name, grid=(s // ts,), in_specs=[row, row],
        out_specs=[row, pl.BlockSpec((1, 1), lambda i: (0, 0))],
        out_shape=[jax.ShapeDtypeStruct((s, d), F32), jax.ShapeDtypeStruct((1, 1), F32)],
        compiler_params=_cp("arbitrary"))(y, target)


def _adam_rows(r, c):
    cap = max(8, (1 << 20) // (4 * c))
    if r <= cap:
        return r
    best = None
    for t in range(8, cap + 1, 8):
        if r % t == 0:
            best = t
    return best if best is not None else r


def _reduce_adamw(contribs, w, m, v, name):
    nl = len(contribs)
    nd, r, c = contribs[0].shape
    tr = _adam_rows(r, c)
    bc1 = 1.0 - ADAM_B1 ** ADAM_STEP
    bc2 = 1.0 - ADAM_B2 ** ADAM_STEP

    def body(*refs):
        c_refs = refs[:nl]
        w_ref, m_ref, v_ref, g_ref, d_ref, nm_ref, nv_ref = refs[nl:]
        l = pl.program_id(0)
        for li in range(nl):
            @pl.when(l == li)
            def _(c_ref=c_refs[li]):
                g = c_ref[0].astype(F32)
                for k in range(1, nd):
                    g = g + c_ref[k].astype(F32)
                nm = ADAM_B1 * m_ref[...] + (1.0 - ADAM_B1) * g
                nv = ADAM_B2 * v_ref[...] + (1.0 - ADAM_B2) * (g * g)
                g_ref[...] = g
                nm_ref[...] = nm
                nv_ref[...] = nv
                d_ref[...] = -ADAM_LR * ((nm / bc1) / (jnp.sqrt(nv / bc2) + ADAM_EPS) + ADAM_WD * w_ref[...])

    def c_spec(li):
        return pl.BlockSpec((nd, tr, c), lambda l, i: (0, jnp.where(l == li, i, 0), 0))

    blk = pl.BlockSpec((None, tr, c), lambda l, i: (l, i, 0))
    out = jax.ShapeDtypeStruct((nl, r, c), F32)
    return pl.pallas_call(
        body, name=name, grid=(nl, r // tr),
        in_specs=[c_spec(li) for li in range(nl)] + [blk, blk, blk],
        out_specs=[blk, blk, blk, blk], out_shape=[out, out, out, out],
        compiler_params=_cp("arbitrary", "arbitrary"))(*contribs, w, m, v)


def _mesh_pos():
    return lax.axis_index("x"), lax.axis_index("y"), lax.axis_index("c")


def _peer(pos, k):
    x, y, c = pos
    return (1 - x if k & 4 else x, 1 - y if k & 2 else y, 1 - c if k & 1 else c)


def _linear(pos):
    return 4 * pos[0] + 2 * pos[1] + pos[2]


def _xfer_copies(srcs, lands, send_sems, recv_sems, local_sems, gather):
    pos = _mesh_pos()
    me = _linear(pos)
    local, remote = [], []
    for i, (src, land) in enumerate(zip(srcs, lands)):
        local.append(pltpu.make_async_copy(src if gather else src.at[me], land.at[me], local_sems.at[i]))
        for k in range(1, N_DEV):
            peer = _peer(pos, k)
            remote.append(pltpu.make_async_remote_copy(
                src_ref=src if gather else src.at[_linear(peer)], dst_ref=land.at[me],
                send_sem=send_sems.at[i * (N_DEV - 1) + k - 1], recv_sem=recv_sems.at[i * (N_DEV - 1) + k - 1],
                device_id=peer, device_id_type=MESH_ID))
    return local, remote


_HBM = pl.BlockSpec(memory_space=pltpu.HBM)
_SEM = pl.BlockSpec(memory_space=pltpu.SEMAPHORE)
_EFFECT = pltpu.SideEffectType.DATAFLOW_SIDE_EFFECTING


def _xfer_start(srcs, gather, name, after=()):
    n = len(srcs)
    na = len(after)
    lands = [lax.empty(((N_DEV,) + a.shape) if gather else a.shape, a.dtype) for a in srcs]

    def body(*refs):
        src, land = refs[:n], refs[n:2 * n]
        send_sems, recv_sems, local_sems = refs[2 * n + na:2 * n + na + 3]
        local, remote = _xfer_copies(src, land, send_sems, recv_sems, local_sems, gather)
        for cp in local + remote:
            cp.start()
        refs[-1][...] = jnp.zeros_like(refs[-1])

    nsem = n * (N_DEV - 1)
    out = pl.pallas_call(
        body, name=name,
        out_shape=(pltpu.SemaphoreType.DMA((nsem,)), pltpu.SemaphoreType.DMA((nsem,)), pltpu.SemaphoreType.DMA((n,)),
                   *[pltpu.HBM(a.shape, a.dtype) for a in srcs], *[pltpu.HBM(a.shape, a.dtype) for a in lands],
                   jax.ShapeDtypeStruct((8, LANES), F32)),
        in_specs=[_HBM] * (2 * n) + [pl.BlockSpec(memory_space=pl.ANY)] * na,
        out_specs=(_SEM, _SEM, _SEM, *[_HBM] * (2 * n), pl.BlockSpec(memory_space=pltpu.VMEM)),
        input_output_aliases={i: 3 + i for i in range(2 * n)},
        compiler_params=pltpu.CompilerParams(has_side_effects=_EFFECT))(
            *[pltpu.with_memory_space_constraint(a, pltpu.HBM) for a in srcs],
            *[pltpu.with_memory_space_constraint(a, pltpu.HBM) for a in lands], *after)
    return out[:3], list(out[3:3 + n]), list(out[3 + n:3 + 2 * n]), out[-1]


def _started(handle):
    return handle[3]


def _xfer_wait(handle, after, gather, name):
    sems, srcs, lands, _ = handle
    n = len(srcs)

    def body(*refs):
        src, land = refs[:n], refs[n:2 * n]
        send_sems, recv_sems, local_sems = refs[2 * n:2 * n + 3]
        local, remote = _xfer_copies(src, land, send_sems, recv_sems, local_sems, gather)
        for cp in local:
            cp.wait()
        for cp in remote:
            cp.wait_send()
            cp.wait_recv()

    out = pl.pallas_call(
        body, name=name,
        out_shape=(*[pltpu.HBM(a.shape, a.dtype) for a in srcs], *[pltpu.HBM(a.shape, a.dtype) for a in lands]),
        in_specs=[_HBM] * (2 * n) + [_SEM] * 3 + [pl.BlockSpec(memory_space=pl.ANY)] * len(after),
        out_specs=tuple([_HBM] * (2 * n)), input_output_aliases={i: i for i in range(2 * n)},
        compiler_params=pltpu.CompilerParams(has_side_effects=_EFFECT))(*srcs, *lands, *sems, *after)
    return list(out[n:])


def _cols_full(g):
    nd, r, c = g.shape
    return jnp.transpose(g, (1, 0, 2)).reshape(r, nd * c)


def _cols_split(full):
    r, n = full.shape
    return jnp.transpose(full.reshape(r, N_DEV, n // N_DEV), (1, 0, 2))


def _pack_b_in(w):
    qkv = 3 * MIX_W
    pad = jnp.zeros((w.shape[0], B_IN_PAD - w.shape[1]), w.dtype)
    return jnp.concatenate([w[:, :qkv], w[:, qkv + N_MIX_HEADS:], w[:, qkv:qkv + N_MIX_HEADS], pad], axis=1)


def _unpack_b_in(w):
    qkv = 3 * MIX_W
    return jnp.concatenate([w[:, :qkv], w[:, qkv + MEM_W:qkv + MEM_W + N_MIX_HEADS], w[:, qkv:qkv + MEM_W]], axis=1)


def _to_classes(t, g):
    r = 4 ** g
    s, w = t.shape
    return jnp.transpose(t.reshape(s // r, r, w), (1, 0, 2)).reshape(s, w)


def _from_classes(t, g):
    r = 4 ** g
    s, w = t.shape
    return jnp.transpose(t.reshape(r, s // r, w), (1, 0, 2)).reshape(s, w)


def _group_stack(t):
    return jnp.stack([_to_classes(t[:, g * GROUP_W:(g + 1) * GROUP_W], g) for g in range(N_GROUPS)])


def _group_unstack(t3):
    return jnp.concatenate([_from_classes(t3[g], g) for g in range(N_GROUPS)], axis=1)


def _same_stack(t):
    return jnp.stack([_to_classes(t, g) for g in range(N_GROUPS)])


def _same_unstack(t3):
    return jnp.stack([_from_classes(t3[g], g) for g in range(N_GROUPS)])


def _ffn_forward(x, xb, wgu, wd4, gain, bias, tag):
    gu, a = _ffn_up(xb, wgu, f"{tag}_up")
    y, yb, xh, rstd = _mm_res_ln(a, wd4, x, gain, bias, 0.5, f"{tag}_down_ln")
    return y, yb, (xb, gu, a, xh, rstd)


def _ffn_backward(dy, saved, wgu, wd4, gain, tag, after=()):
    xb, gu, a, xh, rstd = saved
    s = xb.shape[0]
    nd, d, c = wgu.shape
    dz, dzb, dgain, dbias = _ln_bwd(dy, xh, rstd, gain, 0.5, f"{tag}_ln_bwd", after)
    dh = _ffn_bwd_act(dzb, wd4, gu, f"{tag}_act_bwd").reshape(nd, s, c)
    dwd = _mm_tn(a, dzb[None], f"{tag}_dwd").reshape(nd, wd4.shape[1] // 2, d)
    dx = _mm_nt(dh, wgu, f"{tag}_dx", res=dz)
    dwgu = _mm_tn(xb[None], dh, f"{tag}_dwgu")
    return dx, dwgu, dwd, dgain, dbias


def _mixer_a_forward(x, xb, memb, w_in, w_kv, w_out, gain, bias, tabs):
    h = _mm_nn(xb, w_in, F32, "a_in")
    hb = _rope_cast([h], tabs, 2 * MIX_W // LANES, "a_rope")
    q3 = _group_stack(hb[:, :MIX_W])
    k3 = _group_stack(hb[:, MIX_W:2 * MIX_W])
    v3 = _group_stack(hb[:, 2 * MIX_W:3 * MIX_W])
    o3, l3 = _band_fwd(q3, k3, v3, "a_band_fwd")
    oa, lt = _band_combine(_same_unstack(o3), _same_unstack(l3), "a_combine")
    kv = _mm_nn(memb, w_kv, BF, "a_mem_kv")
    om, lm = _mem_fwd(hb, 3 * MIX_W // LANES, kv, "a_mem_fwd")
    cat = jnp.concatenate([oa, om], axis=1)
    y, yb, xh, rstd = _mm_res_ln(cat[None], w_out[None], x, gain, bias, 1.0, "a_out_ln")
    return y, yb, (xb, hb, q3, k3, v3, oa, lt, kv, lm, cat, xh, rstd)


def _mixer_a_backward(dy, saved, memb, w_in, w_kv, w_out, gain, tabs_neg, after=()):
    xb, hb, q3, k3, v3, oa, lt, kv, lm, cat, xh, rstd = saved
    dz, dzb, dgain, dbias = _ln_bwd(dy, xh, rstd, gain, 1.0, "a_ln_bwd", after)
    dcat = _mm_nt(dzb[None], w_out[None], "a_dcat", out_dtype=BF)
    dw_out = _mm_tn(cat[None], dzb[None], "a_dwout")[0]
    dqm, dkm, dvm = _mem_bwd(hb, 3 * MIX_W // LANES, kv, dcat, cat, GROUP_W // LANES, lm, "a_mem_bwd")
    dkv = jnp.concatenate([dkm, dvm], axis=1).astype(BF)
    dw_kv = _mm_tn(memb[None], dkv[None], "a_dwkv")[0]
    dq3, dk3, dv3 = _band_bwd(q3, k3, v3, _same_stack(dcat[:, :GROUP_W]), _same_stack(oa), _same_stack(lt),
                              "a_band_bwd")
    dhb = _rope_cast([_group_unstack(dq3), _group_unstack(dk3), _group_unstack(dv3), dqm], tabs_neg,
                     2 * MIX_W // LANES, "a_rope_bwd")
    dw_in = _mm_tn(xb[None], dhb[None], "a_dwin")[0]
    dx = _mm_nt(dhb[None], w_in[None], "a_dx", res=dz)
    return dx, dw_in, dw_kv, dw_out, dgain, dbias


def _pad_rows(t, rows):
    return jnp.concatenate([t, jnp.zeros((rows - t.shape[0], t.shape[1]), t.dtype)], axis=0)


def _pad_cols(t, cols):
    return jnp.concatenate([t, jnp.zeros((t.shape[0], cols - t.shape[1]), t.dtype)], axis=1)


def _mixer_b_forward(x, xb, memb, w_in, fbias, w_kv, w_out, gain, bias, tabs):
    s = x.shape[0]
    h = _mm_nn(xb, w_in, F32, "b_in")
    hb = _rope_cast([h], tabs, 0, "b_cast")
    f0 = 3 * MIX_W + MEM_W
    f_t = _pad_rows(jnp.transpose(h[:, f0:f0 + N_MIX_HEADS]), 16)
    bias16 = _pad_rows(jnp.transpose(fbias), 16)
    c_t = _gate_fwd(f_t, bias16, "b_gate_fwd")
    c_t3 = c_t[:N_MIX_HEADS].reshape(N_MIX_HEADS // 2, 2, s)
    c_rep = jnp.broadcast_to(c_t[:N_MIX_HEADS, :, None], (N_MIX_HEADS, s, LANES))
    ob, lb = _fox_fwd(hb, c_rep, c_t3, "b_fox_fwd")
    kv = _mm_nn(memb, w_kv, BF, "b_mem_kv")
    om, lm = _mem_fwd(hb, 3 * MIX_W // LANES, kv, "b_mem_fwd")
    cat = jnp.concatenate([ob, om], axis=1)
    y, yb, xh, rstd = _mm_res_ln(cat[None], w_out[None], x, gain, bias, 1.0, "b_out_ln")
    return y, yb, (xb, hb, f_t, bias16, c_rep, c_t3, lb, kv, lm, cat, xh, rstd)


def _mixer_b_backward(dy, saved, memb, w_in, w_kv, w_out, gain, tabs, after=()):
    xb, hb, f_t, bias16, c_rep, c_t3, lb, kv, lm, cat, xh, rstd = saved
    s = xb.shape[0]
    dz, dzb, dgain, dbias = _ln_bwd(dy, xh, rstd, gain, 1.0, "b_ln_bwd", after)
    dcat = _mm_nt(dzb[None], w_out[None], "b_dcat", out_dtype=BF)
    dw_out = _mm_tn(cat[None], dzb[None], "b_dwout")[0]
    dqm, dkm, dvm = _mem_bwd(hb, 3 * MIX_W // LANES, kv, dcat, cat, MIX_W // LANES, lm, "b_mem_bwd")
    dkv = jnp.concatenate([dkm, dvm], axis=1).astype(BF)
    dw_kv = _mm_tn(memb[None], dkv[None], "b_dwkv")[0]
    dsum = _fox_dsum(hb, dcat, lb, c_rep, c_t3, "b_fox_dsum")
    dq, dk, dv, dc3 = _fox_bwd(hb, dcat, dsum, lb, c_rep, c_t3, "b_fox_bwd")
    df_t, dfb = _gate_bwd(_pad_rows(dc3.reshape(N_MIX_HEADS, s), 16), f_t, bias16, "b_gate_bwd")
    df = _pad_cols(jnp.transpose(df_t[:N_MIX_HEADS]), B_IN_PAD - 3 * MIX_W - MEM_W)
    dhb = _rope_cast([dq, dk, dv, dqm, df], tabs, 0, "b_cast_bwd")
    dw_in = _mm_tn(xb[None], dhb[None], "b_dwin")[0]
    dx = _mm_nt(dhb[None], w_in[None], "b_dx", res=dz)
    return dx, dw_in, jnp.transpose(dfb[:N_MIX_HEADS]), dw_kv, dw_out, dgain, dbias


def _weight_groups(w):
    b = {n: w[n].astype(BF) for n in WEIGHTS if n not in F32_COMM}
    return [
        [b["ffn1_w_gate_up"][0], b["ffn1_w_down"][0], w["ln_gain"], w["ln_bias"]],
        [b["a_w_in"][0], b["a_w_out"][0], b["mem_w_kv"][0]],
        [b["ffn2_w_gate_up"][0], b["ffn2_w_down"][0]],
        [b["ffn1_w_gate_up"][1], b["ffn1_w_down"][1]],
        [b["b_w_in"][0], b["b_w_out"][0], b["mem_w_kv"][1]],
        [b["ffn2_w_gate_up"][1], b["ffn2_w_down"][1]],
    ]


def _local_step(x, mem, target, fbias, get_w, put_g):
    s, d = x.shape
    tabs = _rope_tables(s, 1.0)
    tabs_neg = _rope_tables(s, -1.0)
    memb = mem.astype(BF)
    saved, wl = [], []
    cur, curb = x, x.astype(BF)
    ln_g = ln_b = None
    for i in range(DEPTH):
        g = get_w(3 * i, cur)
        if i == 0:
            ln_g, ln_b = (jnp.transpose(t, (1, 2, 0, 3)).reshape(DEPTH, 3, 1, d) for t in g[2:4])
        w1 = (g[0], g[1].reshape(N_DEV // 2, -1, d))
        cur, curb, s1 = _ffn_forward(cur, curb, w1[0], w1[1], ln_g[i, 0], ln_b[i, 0], f"l{i}_ffn1")
        g = get_w(3 * i + 1, cur)
        if i == 0:
            wm = (_cols_full(g[0]), g[2].reshape(d, -1), _cols_full(g[1]))
            cur, curb, s2 = _mixer_a_forward(cur, curb, memb, wm[0], wm[1], wm[2], ln_g[i, 1], ln_b[i, 1], tabs)
        else:
            wm = (_pack_b_in(g[0].reshape(d, -1)), g[2].reshape(d, -1), g[1].reshape(d, -1))
            cur, curb, s2 = _mixer_b_forward(cur, curb, memb, wm[0], fbias, wm[1], wm[2], ln_g[i, 1], ln_b[i, 1],
                                             tabs)
        g = get_w(3 * i + 2, cur)
        w3 = (g[0], g[1].reshape(N_DEV // 2, -1, d))
        cur, curb, s3 = _ffn_forward(cur, curb, w3[0], w3[1], ln_g[i, 2], ln_b[i, 2], f"l{i}_ffn2")
        saved.append((s1, s2, s3))
        wl.append((w1, wm, w3))

    dy, loss = _loss_head(cur, target, "loss_head")

    dgs = [[None] * 3 for _ in range(DEPTH)]
    dbs = [[None] * 3 for _ in range(DEPTH)]
    sent = ()
    for i in reversed(range(DEPTH)):
        s1, s2, s3 = saved[i]
        w1, wm, w3 = wl[i]
        dy, dgu, dd, dgs[i][2], dbs[i][2] = _ffn_backward(dy, s3, w3[0], w3[1], ln_g[i, 2], f"l{i}_ffn2", sent)
        sent = put_g(3 * i + 2, [dgu, dd])
        if i == 0:
            dy, dw_in, dw_kv, dw_out, dgs[i][1], dbs[i][1] = _mixer_a_backward(
                dy, s2, memb, wm[0], wm[1], wm[2], ln_g[i, 1], tabs_neg, sent)
            sent = put_g(1, [_cols_split(dw_in), _cols_split(dw_out), dw_kv.reshape(N_DEV, d // N_DEV, -1)])
        else:
            dy, dw_in, dfb, dw_kv, dw_out, dgs[i][1], dbs[i][1] = _mixer_b_backward(
                dy, s2, memb, wm[0], wm[1], wm[2], ln_g[i, 1], tabs, sent)
            sent = put_g(4, [_unpack_b_in(dw_in).reshape(N_DEV, d // N_DEV, -1),
                             dw_out.reshape(N_DEV, d // N_DEV, -1), dw_kv.reshape(N_DEV, d // N_DEV, -1),
                             jnp.broadcast_to(dfb[None], (N_DEV,) + dfb.shape)])
        dy, dgu, dd, dgs[i][0], dbs[i][0] = _ffn_backward(dy, s1, w1[0], w1[1], ln_g[i, 0], f"l{i}_ffn1", sent)
        if i == 0:
            ln_pieces = []
            for parts in (dgs, dbs):
                t = jnp.concatenate([parts[a][b] for a in range(DEPTH) for b in range(3)], axis=0)
                ln_pieces.append(jnp.transpose(t.reshape(DEPTH * 3, N_DEV, d // N_DEV), (1, 0, 2)))
            sent = put_g(0, [dgu, dd] + ln_pieces)
        else:
            sent = put_g(3, [dgu, dd])
    return loss, dy


WEIGHTS = ("ffn1_w_gate_up", "ffn1_w_down", "ffn2_w_gate_up", "ffn2_w_down", "ln_gain", "ln_bias", "mem_w_kv",
           "a_w_in", "a_w_out", "b_w_in", "b_forget_bias", "b_w_out")
F32_COMM = ("ln_gain", "ln_bias", "b_forget_bias")
GRAD_SLOTS = {
    "ffn1_w_gate_up": [(0, 0), (3, 0)], "ffn1_w_down": [(0, 1), (3, 1)],
    "ffn2_w_gate_up": [(2, 0), (5, 0)], "ffn2_w_down": [(2, 1), (5, 1)],
    "ln_gain": [(0, 2)], "ln_bias": [(0, 3)], "mem_w_kv": [(1, 2), (4, 2)],
    "a_w_in": [(1, 0)], "a_w_out": [(1, 1)], "b_w_in": [(4, 0)], "b_forget_bias": [(4, 3)], "b_w_out": [(4, 1)],
}


def kernel(x, mem, ffn1_w_gate_up, ffn1_w_down, ffn2_w_gate_up, ffn2_w_down, ln_gain, ln_bias, mem_w_kv, a_w_in, a_w_out, b_w_in, b_forget_bias, b_w_out, loss_target, m_ffn1_w_gate_up, m_ffn1_w_down, m_ffn2_w_gate_up, m_ffn2_w_down, m_ln_gain, m_ln_bias, m_mem_w_kv, m_a_w_in, m_a_w_out, m_b_w_in, m_b_forget_bias, m_b_w_out, v_ffn1_w_gate_up, v_ffn1_w_down, v_ffn2_w_gate_up, v_ffn2_w_down, v_ln_gain, v_ln_bias, v_mem_w_kv, v_a_w_in, v_a_w_out, v_b_w_in, v_b_forget_bias, v_b_w_out):
    w = dict(zip(WEIGHTS, (ffn1_w_gate_up, ffn1_w_down, ffn2_w_gate_up, ffn2_w_down, ln_gain, ln_bias, mem_w_kv,
                           a_w_in, a_w_out, b_w_in, b_forget_bias, b_w_out)))
    m = dict(zip(WEIGHTS, (m_ffn1_w_gate_up, m_ffn1_w_down, m_ffn2_w_gate_up, m_ffn2_w_down, m_ln_gain, m_ln_bias,
                           m_mem_w_kv, m_a_w_in, m_a_w_out, m_b_w_in, m_b_forget_bias, m_b_w_out)))
    v = dict(zip(WEIGHTS, (v_ffn1_w_gate_up, v_ffn1_w_down, v_ffn2_w_gate_up, v_ffn2_w_down, v_ln_gain, v_ln_bias,
                           v_mem_w_kv, v_a_w_in, v_a_w_out, v_b_w_in, v_b_forget_bias, v_b_w_out)))

    gathers = []
    for k, grp in enumerate(_weight_groups(w)):
        gathers.append(_xfer_start(grp, True, f"gather{k}_start", [_started(h) for h in gathers[-1:]]))
    exchanges = {}

    def get_w(k, after):
        behind = [after] + ([_started(h) for h in gathers] if k == 0 else [])
        return _xfer_wait(gathers[k], behind, True, f"gather{k}_wait")

    def put_g(k, pieces):
        exchanges[k] = _xfer_start(pieces, False, f"grads{k}_start")
        return (_started(exchanges[k]),)

    loss, grad_x = _local_step(x[0], mem[0], loss_target[0], b_forget_bias, get_w, put_g)
    loss = lax.psum(loss[0, 0], ("x", "y", "c"))

    outs, landed = {}, {}

    def adamw(names, after):
        for n in names:
            contribs = [landed[g][j] for g, j in GRAD_SLOTS[n]]
            view = (len(contribs),) + contribs[0].shape[1:]
            outs[n] = [t.reshape(w[n].shape) for t in _reduce_adamw(
                contribs, w[n].reshape(view), m[n].reshape(view), v[n].reshape(view), f"adamw_{n}")]
            after = outs[n][0]
        return after

    after = [grad_x, _started(exchanges[0])]
    for k in (5, 4, 3, 2, 1):
        landed[k] = _xfer_wait(exchanges[k], after, False, f"grads{k}_wait")
        after = [landed[k][0]]
    done = adamw(("ffn2_w_gate_up", "ffn2_w_down", "mem_w_kv", "a_w_in", "a_w_out", "b_w_in", "b_forget_bias",
                  "b_w_out"), None)
    landed[0] = _xfer_wait(exchanges[0], [done], False, "grads0_wait")
    adamw(("ffn1_w_gate_up", "ffn1_w_down", "ln_gain", "ln_bias"), None)
    return (loss, grad_x[None], *[outs[n][0] for n in WEIGHTS], *[outs[n][1] for n in WEIGHTS],
            *[outs[n][2] for n in WEIGHTS], *[outs[n][3] for n in WEIGHTS])
```

```python
import functools

import jax
import jax.numpy as jnp
from jax import lax
from jax.experimental import pallas as pl
from jax.experimental.pallas import tpu as pltpu

F32 = jnp.float32
BF = jnp.bfloat16
MESH_ID = pl.DeviceIdType.MESH

N_DEV = 8
DEPTH = 2
HEAD_DIM = 64
LANES = 128
N_MIX_HEADS = 12
N_MEM_HEADS = 4
MIX_W = N_MIX_HEADS * HEAD_DIM
MEM_W = N_MEM_HEADS * HEAD_DIM
N_GROUPS = 3
GROUP_W = MIX_W // N_GROUPS
BLOCK = 128
BAND_SUB = 4
ROT_HALF = 8
ROPE_THETA = 500000.0
ALPHA = (2 * DEPTH) ** 0.25
LN_EPS = 1e-5
SCALE = HEAD_DIM ** -0.5
NEG = -1e30
B_IN_PAD = 2688
ADAM_LR, ADAM_B1, ADAM_B2, ADAM_EPS, ADAM_WD, ADAM_STEP = 0.001, 0.9, 0.999, 1e-08, 0.01, 10
VMEM_LIMIT = 56 * 1024 * 1024


def _cp(*sem):
    return pltpu.CompilerParams(dimension_semantics=sem, vmem_limit_bytes=VMEM_LIMIT)


def _dot(a, b):
    return jnp.dot(a, b, preferred_element_type=F32)


def _dot_nt(a, b):
    return lax.dot_general(a, b, (((1,), (1,)), ((), ())), preferred_element_type=F32)


def _dot_tn(a, b):
    return lax.dot_general(a, b, (((0,), (0,)), ((), ())), preferred_element_type=F32)


def _sigmoid(x):
    return 1.0 / (1.0 + jnp.exp(-x))


def _tile(n, cap=1024):
    if n <= cap:
        return n
    best = LANES
    for t in range(LANES, cap + 1, LANES):
        if n % t == 0:
            best = t
    return best


def _rows(s, cap=512):
    return s if s <= cap else cap


def _mm_nn(a, b, out_dtype, name, b_rows_out=False):
    m, k = a.shape
    n = b.shape[0] if b_rows_out else b.shape[1]
    tm, tn = _rows(m), _tile(n)

    def body(a_ref, b_ref, o_ref):
        prod = _dot_nt(a_ref[...], b_ref[...]) if b_rows_out else _dot(a_ref[...], b_ref[...])
        o_ref[...] = prod.astype(o_ref.dtype)

    b_spec = (pl.BlockSpec((tn, k), lambda j, i: (j, 0)) if b_rows_out
              else pl.BlockSpec((k, tn), lambda j, i: (0, j)))
    return pl.pallas_call(
        body, name=name, grid=(n // tn, m // tm),
        in_specs=[pl.BlockSpec((tm, k), lambda j, i: (i, 0)), b_spec],
        out_specs=pl.BlockSpec((tm, tn), lambda j, i: (i, j)),
        out_shape=jax.ShapeDtypeStruct((m, n), out_dtype),
        compiler_params=_cp("parallel", "parallel"))(a, b)


def _resident(shape, index_map):
    return pl.BlockSpec(shape, index_map, pipeline_mode=pl.Buffered(1))


def _mm_tn(a, b, name, out_dtype=BF):
    na, s, m = a.shape
    nb, _, n = b.shape
    no = max(na, nb)
    tm, tn = _tile(m), _tile(n)

    def body(a_ref, b_ref, o_ref):
        o_ref[...] = _dot_tn(a_ref[...], b_ref[...]).astype(o_ref.dtype)

    def spec(nbatch, width, tile, index_map):
        fixed = nbatch == 1 and width == tile
        return _resident((None, s, tile), index_map) if fixed else pl.BlockSpec((None, s, tile), index_map)

    return pl.pallas_call(
        body, name=name, grid=(no, m // tm, n // tn),
        in_specs=[spec(na, m, tm, lambda j, r, c: (j if na > 1 else 0, 0, r)),
                  spec(nb, n, tn, lambda j, r, c: (j if nb > 1 else 0, 0, c))],
        out_specs=pl.BlockSpec((None, tm, tn), lambda j, r, c: (j, r, c)),
        out_shape=jax.ShapeDtypeStruct((no, m, n), out_dtype),
        compiler_params=_cp("parallel", "parallel", "parallel"))(a, b)


def _mm_nt(dh, w, name, res=None, out_dtype=F32, w_rows_out=True):
    nc, s, kc = dh.shape
    d = w.shape[1] if w_rows_out else w.shape[2]
    ts = _rows(s, 256 if nc > 1 else 512)
    has_res = res is not None
    mm = _dot_nt if w_rows_out else _dot

    def body(*refs):
        if has_res:
            dh_ref, w_ref, r_ref, o_ref = refs
        else:
            dh_ref, w_ref, o_ref = refs
        out = mm(dh_ref[0], w_ref[0])
        for j in range(1, nc):
            out = out + mm(dh_ref[j], w_ref[j])
        if has_res:
            out = out + ALPHA * r_ref[...]
        o_ref[...] = out.astype(o_ref.dtype)

    in_specs = [pl.BlockSpec((nc, ts, kc), lambda i: (0, i, 0)), _resident(w.shape, lambda i: (0, 0, 0))]
    args = [dh, w]
    if has_res:
        in_specs.append(pl.BlockSpec((ts, d), lambda i: (i, 0)))
        args.append(res)
    return pl.pallas_call(
        body, name=name, grid=(s // ts,), in_specs=in_specs,
        out_specs=pl.BlockSpec((ts, d), lambda i: (i, 0)),
        out_shape=jax.ShapeDtypeStruct((s, d), out_dtype),
        compiler_params=_cp("parallel"))(*args)


def _mm_res_ln(a, w, x, gain, bias, fscale, name):
    nc, s, kc = a.shape
    d = w.shape[2]
    ts = _rows(s, 256 if nc > 1 else 512)

    def body(a_ref, w_ref, x_ref, g_ref, b_ref, y_ref, yb_ref, xh_ref, r_ref):
        f = _dot(a_ref[0], w_ref[0])
        for j in range(1, nc):
            f = f + _dot(a_ref[j], w_ref[j])
        z = ALPHA * x_ref[...] + fscale * f
        mu = jnp.mean(z, axis=-1, keepdims=True)
        zc = z - mu
        var = jnp.mean(zc * zc, axis=-1, keepdims=True)
        r = lax.rsqrt(var + LN_EPS)
        xh = zc * r
        y = xh * g_ref[...] + b_ref[...]
        y_ref[...] = y
        yb_ref[...] = y.astype(BF)
        xh_ref[...] = xh
        r_ref[...] = r

    row = pl.BlockSpec((ts, d), lambda i: (i, 0))
    vec = pl.BlockSpec((1, d), lambda i: (0, 0))
    return pl.pallas_call(
        body, name=name, grid=(s // ts,),
        in_specs=[pl.BlockSpec((nc, ts, kc), lambda i: (0, i, 0)), _resident((nc, kc, d), lambda i: (0, 0, 0)),
                  row, vec, vec],
        out_specs=[row, row, row, pl.BlockSpec((ts, 1), lambda i: (i, 0))],
        out_shape=[jax.ShapeDtypeStruct((s, d), F32), jax.ShapeDtypeStruct((s, d), BF),
                   jax.ShapeDtypeStruct((s, d), F32), jax.ShapeDtypeStruct((s, 1), F32)],
        compiler_params=_cp("parallel"))(a, w, x, gain, bias)


def _ln_bwd(dy, xh, rstd, gain, fscale, name, after=()):
    s, d = dy.shape
    ts = _rows(s)
    na = len(after)

    def body(*refs):
        dy_ref, xh_ref, r_ref, g_ref = refs[:4]
        dz_ref, dzb_ref, dg_ref, db_ref = refs[4 + na:]
        i = pl.program_id(0)
        dyv = dy_ref[...]
        xhv = xh_ref[...]
        dxh = dyv * g_ref[...]
        m1 = jnp.mean(dxh, axis=-1, keepdims=True)
        m2 = jnp.mean(dxh * xhv, axis=-1, keepdims=True)
        dz = r_ref[...] * (dxh - m1 - xhv * m2)
        dz_ref[...] = dz
        dzb_ref[...] = (fscale * dz).astype(BF)

        @pl.when(i == 0)
        def _():
            dg_ref[...] = jnp.zeros_like(dg_ref)
            db_ref[...] = jnp.zeros_like(db_ref)

        dg_ref[...] += jnp.sum(dyv * xhv, axis=0, keepdims=True)
        db_ref[...] += jnp.sum(dyv, axis=0, keepdims=True)

    row = pl.BlockSpec((ts, d), lambda i: (i, 0))
    vec = pl.BlockSpec((1, d), lambda i: (0, 0))
    return pl.pallas_call(
        body, name=name, grid=(s // ts,),
        in_specs=[row, row, pl.BlockSpec((ts, 1), lambda i: (i, 0)), vec] + [pl.BlockSpec(memory_space=pl.ANY)] * na,
        out_specs=[row, row, vec, vec],
        out_shape=[jax.ShapeDtypeStruct((s, d), F32), jax.ShapeDtypeStruct((s, d), BF),
                   jax.ShapeDtypeStruct((1, d), F32), jax.ShapeDtypeStruct((1, d), F32)],
        compiler_params=_cp("arbitrary"))(dy, xh, rstd, gain, *after)


def _ffn_up(xb, wgu, name):
    s, d = xb.shape
    c = wgu.shape[1]
    nch = wgu.shape[0] // 2
    ts = _rows(s)
    w4 = wgu.reshape(2, nch, c, d)

    def body(x_ref, w_ref, gu_ref, a_ref):
        x = x_ref[...]
        g = _dot_nt(x, w_ref[0])
        u = _dot_nt(x, w_ref[1])
        gu_ref[0] = g.astype(BF)
        gu_ref[1] = u.astype(BF)
        a_ref[...] = (g * _sigmoid(g) * u).astype(BF)

    return pl.pallas_call(
        body, name=name, grid=(nch, s // ts),
        in_specs=[pl.BlockSpec((ts, d), lambda j, i: (i, 0)),
                  pl.BlockSpec((2, None, c, d), lambda j, i: (0, j, 0, 0))],
        out_specs=[pl.BlockSpec((2, None, ts, c), lambda j, i: (0, j, i, 0)),
                   pl.BlockSpec((None, ts, c), lambda j, i: (j, i, 0))],
        out_shape=[jax.ShapeDtypeStruct((2, nch, s, c), BF), jax.ShapeDtypeStruct((nch, s, c), BF)],
        compiler_params=_cp("parallel", "parallel"))(xb, w4)


def _ffn_bwd_act(dzb, wd4, gu, name):
    s, d = dzb.shape
    nch, c = wd4.shape[0], wd4.shape[1]
    ts = _rows(s)

    def body(dz_ref, w_ref, gu_ref, dh_ref):
        da = _dot_nt(dz_ref[...], w_ref[...])
        g = gu_ref[0].astype(F32)
        u = gu_ref[1].astype(F32)
        sg = _sigmoid(g)
        dh_ref[0] = (da * u * sg * (1.0 + g * (1.0 - sg))).astype(BF)
        dh_ref[1] = (da * g * sg).astype(BF)

    return pl.pallas_call(
        body, name=name, grid=(nch, s // ts),
        in_specs=[pl.BlockSpec((ts, d), lambda j, i: (i, 0)),
                  pl.BlockSpec((None, c, d), lambda j, i: (j, 0, 0)),
                  pl.BlockSpec((2, None, ts, c), lambda j, i: (0, j, i, 0))],
        out_specs=pl.BlockSpec((2, None, ts, c), lambda j, i: (0, j, i, 0)),
        out_shape=jax.ShapeDtypeStruct((2, nch, s, c), BF),
        compiler_params=_cp("parallel", "parallel"))(dzb, wd4, gu)


def _rope_tables(s, sign):
    pos = jnp.arange(s, dtype=F32)
    inv_freq = 1.0 / (ROPE_THETA ** (jnp.arange(ROT_HALF, dtype=F32) / ROT_HALF))
    ang = pos[:, None] * inv_freq[None, :]
    cos, sin = jnp.cos(ang), jnp.sin(ang) * sign
    one = jnp.ones((s, HEAD_DIM - 2 * ROT_HALF), F32)
    zero = jnp.zeros((s, HEAD_DIM - 2 * ROT_HALF), F32)
    zh = jnp.zeros((s, ROT_HALF), F32)
    cos_f = jnp.concatenate([cos, cos, one], axis=1)
    sin_a = jnp.concatenate([-sin, zh, zero], axis=1)
    sin_b = jnp.concatenate([zh, sin, zero], axis=1)
    rep = LANES // HEAD_DIM
    return tuple(jnp.tile(t, (1, rep)) for t in (cos_f, sin_a, sin_b))


def _rope_cast(parts, tabs, n_rope, name):
    s = parts[0].shape[0]
    widths = [p.shape[1] for p in parts]
    n = sum(widths)
    npart = len(parts)
    ts = _rows(s, 256)

    def body(*refs):
        part_refs = refs[:npart]
        c_ref, sa_ref, sb_ref, o_ref = refs[npart:]
        col = 0
        for ref, w in zip(part_refs, widths):
            for j in range(w // LANES):
                t = ref[:, j * LANES:(j + 1) * LANES]
                if col < n_rope:
                    t = (t * c_ref[...] + pltpu.roll(t, LANES - ROT_HALF, 1) * sa_ref[...]
                         + pltpu.roll(t, ROT_HALF, 1) * sb_ref[...])
                o_ref[:, col * LANES:(col + 1) * LANES] = t.astype(BF)
                col += 1

    tab = pl.BlockSpec((ts, LANES), lambda i: (i, 0))
    return pl.pallas_call(
        body, name=name, grid=(s // ts,),
        in_specs=[pl.BlockSpec((ts, w), lambda i: (i, 0)) for w in widths] + [tab, tab, tab],
        out_specs=pl.BlockSpec((ts, n), lambda i: (i, 0)),
        out_shape=jax.ShapeDtypeStruct((s, n), BF),
        compiler_params=_cp("parallel"))(*parts, *tabs)


def _head_masks():
    lane = lax.broadcasted_iota(jnp.int32, (1, LANES), 1)
    return [lane < HEAD_DIM, lane >= HEAD_DIM]


def _sel(mask, v):
    return jnp.where(mask, v, jnp.zeros_like(v))


def _pick(mask, wide, fill):
    return jnp.max(jnp.where(mask, wide, fill), axis=1, keepdims=True)


def _band_masks(has_other, prev):
    qi = lax.broadcasted_iota(jnp.int32, (BLOCK, BLOCK), 0)
    kj = lax.broadcasted_iota(jnp.int32, (BLOCK, BLOCK), 1)
    if prev:
        return kj >= qi + jnp.where(has_other, 0, BLOCK)
    return kj <= qi


def _band_fwd(q3, k3, v3, name):
    ng, s, w = q3.shape
    nb = s // BLOCK
    npair = w // LANES
    nsub = BAND_SUB
    tile = nsub * BLOCK

    def body(q_ref, kc_ref, kp_ref, vc_ref, vp_ref, o_ref, l_ref):
        g = pl.program_id(0)
        t = pl.program_id(2)
        nbl = jnp.right_shift(nb, 2 * g)
        mc = _band_masks(None, False)
        hm = _head_masks()
        for i in range(nsub):
            rows = slice(i * BLOCK, (i + 1) * BLOCK)
            has_prev = jnp.bitwise_and(t * nsub + i, nbl - 1) != 0
            mp = _band_masks(has_prev, True)
            q, kc, vc = q_ref[rows, :], kc_ref[rows, :], vc_ref[rows, :]
            if i == 0:
                kp, vp = kp_ref[...], vp_ref[...]
            else:
                prev = slice((i - 1) * BLOCK, i * BLOCK)
                kp, vp = kc_ref[prev, :], vc_ref[prev, :]
            o = jnp.zeros((BLOCK, LANES), F32)
            lse_w = jnp.zeros((BLOCK, LANES), F32)
            for h in range(2):
                qh = _sel(hm[h], q)
                sc = jnp.where(mc, _dot_nt(qh, kc) * SCALE, NEG)
                sp = jnp.where(mp, _dot_nt(qh, kp) * SCALE, NEG)
                m = jnp.maximum(jnp.max(sc, axis=1, keepdims=True), jnp.max(sp, axis=1, keepdims=True))
                pc = jnp.exp(sc - m)
                pp = jnp.exp(sp - m)
                l = jnp.sum(pc, axis=1, keepdims=True) + jnp.sum(pp, axis=1, keepdims=True)
                oh = _dot(pc.astype(BF), _sel(hm[h], vc)) + _dot(pp.astype(BF), _sel(hm[h], vp))
                o = o + oh / l
                lse_w = jnp.where(hm[h], m + jnp.log(l), lse_w)
            o_ref[rows, :] = o
            l_ref[rows, :] = lse_w

    cur = pl.BlockSpec((None, tile, LANES), lambda g, p, t: (g, t, p))
    prv = pl.BlockSpec((None, BLOCK, LANES), lambda g, p, t: (g, jnp.maximum(t * nsub - 1, 0), p))
    return pl.pallas_call(
        body, name=name, grid=(ng, npair, nb // nsub),
        in_specs=[cur, cur, prv, cur, prv], out_specs=[cur, cur],
        out_shape=[jax.ShapeDtypeStruct((ng, s, w), F32), jax.ShapeDtypeStruct((ng, s, w), F32)],
        compiler_params=_cp("parallel", "parallel", "parallel"))(q3, k3, k3, v3, v3)


def _band_combine(o3, l3, name):
    ng, s, w = o3.shape
    ts = _rows(s)

    def body(o_ref, l_ref, oa_ref, lt_ref):
        ls = [l_ref[g] for g in range(ng)]
        m = functools.reduce(jnp.maximum, ls)
        es = [jnp.exp(l - m) for l in ls]
        den = functools.reduce(lambda a, b: a + b, es)
        num = functools.reduce(lambda a, b: a + b, [es[g] * o_ref[g] for g in range(ng)])
        oa_ref[...] = (num / den).astype(BF)
        lt_ref[...] = m + jnp.log(den)

    blk3 = pl.BlockSpec((ng, ts, w), lambda i: (0, i, 0))
    blk = pl.BlockSpec((ts, w), lambda i: (i, 0))
    return pl.pallas_call(
        body, name=name, grid=(s // ts,), in_specs=[blk3, blk3], out_specs=[blk, blk],
        out_shape=[jax.ShapeDtypeStruct((s, w), BF), jax.ShapeDtypeStruct((s, w), F32)],
        compiler_params=_cp("parallel"))(o3, l3)


def _band_bwd(q3, k3, v3, do3, oa3, lt3, name):
    ng, s, w = q3.shape
    nb = s // BLOCK
    npair = w // LANES
    nsub = BAND_SUB
    tile = nsub * BLOCK

    def body(q_ref, qn_ref, kc_ref, kp_ref, vc_ref, vp_ref, do_ref, don_ref, oa_ref, oan_ref, lt_ref, ltn_ref,
             dq_ref, dk_ref, dv_ref):
        g = pl.program_id(0)
        t = pl.program_id(2)
        nbl = jnp.right_shift(nb, 2 * g)
        mc = _band_masks(None, False)
        hm = _head_masks()

        def block(ref, edge_ref, i):
            if i < 0 or i >= nsub:
                return edge_ref[...]
            return ref[i * BLOCK:(i + 1) * BLOCK, :]

        for i in range(nsub):
            b = t * nsub + i
            mp = _band_masks(jnp.bitwise_and(b, nbl - 1) != 0, True)
            mn = _band_masks(jnp.bitwise_and(b + 1, nbl - 1) != 0, True)
            q, qn = block(q_ref, None, i), block(q_ref, qn_ref, i + 1)
            kc, kp = block(kc_ref, None, i), block(kc_ref, kp_ref, i - 1)
            vc, vp = block(vc_ref, None, i), block(vc_ref, vp_ref, i - 1)
            do, don = block(do_ref, None, i), block(do_ref, don_ref, i + 1)
            dd = do.astype(F32) * block(oa_ref, None, i).astype(F32)
            ddn = don.astype(F32) * block(oa_ref, oan_ref, i + 1).astype(F32)
            lt, ltn = block(lt_ref, None, i), block(lt_ref, ltn_ref, i + 1)
            dq = jnp.zeros((BLOCK, LANES), F32)
            dk = jnp.zeros((BLOCK, LANES), F32)
            dv = jnp.zeros((BLOCK, LANES), F32)
            for h in range(2):
                qh, doh = _sel(hm[h], q), _sel(hm[h], do)
                qnh, donh = _sel(hm[h], qn), _sel(hm[h], don)
                kch, kph = _sel(hm[h], kc), _sel(hm[h], kp)
                lse = _pick(hm[h], lt, NEG)
                lsen = _pick(hm[h], ltn, NEG)
                dsum = jnp.sum(_sel(hm[h], dd), axis=1, keepdims=True)
                dsumn = jnp.sum(_sel(hm[h], ddn), axis=1, keepdims=True)
                pc = jnp.exp(jnp.where(mc, _dot_nt(qh, kc) * SCALE, NEG) - lse)
                pp = jnp.exp(jnp.where(mp, _dot_nt(qh, kp) * SCALE, NEG) - lse)
                dsc = pc * (_dot_nt(doh, vc) - dsum)
                dsp = pp * (_dot_nt(doh, vp) - dsum)
                dq = dq + SCALE * (_dot(dsc.astype(BF), kch) + _dot(dsp.astype(BF), kph))
                pn = jnp.exp(jnp.where(mn, _dot_nt(qnh, kc) * SCALE, NEG) - lsen)
                dsn = pn * (_dot_nt(donh, vc) - dsumn)
                dk = dk + SCALE * (_dot_tn(dsc.astype(BF), qh) + _dot_tn(dsn.astype(BF), qnh))
                dv = dv + _dot_tn(pc.astype(BF), doh) + _dot_tn(pn.astype(BF), donh)
            rows = slice(i * BLOCK, (i + 1) * BLOCK)
            dq_ref[rows, :] = dq
            dk_ref[rows, :] = dk
            dv_ref[rows, :] = dv

    cur = pl.BlockSpec((None, tile, LANES), lambda g, p, t: (g, t, p))
    prv = pl.BlockSpec((None, BLOCK, LANES), lambda g, p, t: (g, jnp.maximum(t * nsub - 1, 0), p))
    nxt = pl.BlockSpec((None, BLOCK, LANES), lambda g, p, t: (g, jnp.minimum(t * nsub + nsub, nb - 1), p))
    out = jax.ShapeDtypeStruct((ng, s, w), F32)
    return pl.pallas_call(
        body, name=name, grid=(ng, npair, nb // nsub),
        in_specs=[cur, nxt, cur, prv, cur, prv, cur, nxt, cur, nxt, cur, nxt],
        out_specs=[cur, cur, cur], out_shape=[out, out, out],
        compiler_params=_cp("parallel", "parallel", "parallel"))(
            q3, q3, k3, k3, v3, v3, do3, do3, oa3, oa3, lt3, lt3)


def _mem_fwd(hb, q_blk0, kv, name):
    s = hb.shape[0]
    m = kv.shape[0]
    tq = _rows(s)
    npair = MEM_W // LANES

    def body(q_ref, k_ref, v_ref, o_ref, l_ref):
        q, k, v = q_ref[...], k_ref[...], v_ref[...]
        hm = _head_masks()
        o = jnp.zeros((tq, LANES), F32)
        lse_w = jnp.zeros((tq, LANES), F32)
        for h in range(2):
            sc = _dot_nt(_sel(hm[h], q), k) * SCALE
            mx = jnp.max(sc, axis=1, keepdims=True)
            p = jnp.exp(sc - mx)
            l = jnp.sum(p, axis=1, keepdims=True)
            o = o + _dot(p.astype(BF), _sel(hm[h], v)) / l
            lse_w = jnp.where(hm[h], mx + jnp.log(l), lse_w)
        o_ref[...] = o.astype(BF)
        l_ref[...] = lse_w

    blk = pl.BlockSpec((tq, LANES), lambda p, i: (i, p))
    return pl.pallas_call(
        body, name=name, grid=(npair, s // tq),
        in_specs=[pl.BlockSpec((tq, LANES), lambda p, i: (i, q_blk0 + p)),
                  pl.BlockSpec((m, LANES), lambda p, i: (0, p)),
                  pl.BlockSpec((m, LANES), lambda p, i: (0, npair + p))],
        out_specs=[blk, blk],
        out_shape=[jax.ShapeDtypeStruct((s, MEM_W), BF), jax.ShapeDtypeStruct((s, MEM_W), F32)],
        compiler_params=_cp("parallel", "parallel"))(hb, kv, kv)


def _mem_bwd(hb, q_blk0, kv, dcat, cat, o_blk0, lse, name):
    s = hb.shape[0]
    m = kv.shape[0]
    tq = _rows(s)
    npair = MEM_W // LANES

    def body(q_ref, k_ref, v_ref, do_ref, o_ref, l_ref, dq_ref, dk_ref, dv_ref):
        i = pl.program_id(1)

        @pl.when(i == 0)
        def _():
            dk_ref[...] = jnp.zeros_like(dk_ref)
            dv_ref[...] = jnp.zeros_like(dv_ref)

        q, k, v, do = q_ref[...], k_ref[...], v_ref[...], do_ref[...]
        dd = do.astype(F32) * o_ref[...].astype(F32)
        lt = l_ref[...]
        hm = _head_masks()
        dq = jnp.zeros((tq, LANES), F32)
        dk = jnp.zeros((m, LANES), F32)
        dv = jnp.zeros((m, LANES), F32)
        for h in range(2):
            qh, doh = _sel(hm[h], q), _sel(hm[h], do)
            p = jnp.exp(_dot_nt(qh, k) * SCALE - _pick(hm[h], lt, NEG))
            ds = p * (_dot_nt(doh, v) - jnp.sum(_sel(hm[h], dd), axis=1, keepdims=True))
            dq = dq + SCALE * _dot(ds.astype(BF), _sel(hm[h], k))
            dk = dk + SCALE * _dot_tn(ds.astype(BF), qh)
            dv = dv + _dot_tn(p.astype(BF), doh)
        dq_ref[...] = dq
        dk_ref[...] += dk
        dv_ref[...] += dv

    row = pl.BlockSpec((tq, LANES), lambda p, i: (i, p))
    orow = pl.BlockSpec((tq, LANES), lambda p, i: (i, o_blk0 + p))
    acc = pl.BlockSpec((m, LANES), lambda p, i: (0, p))
    return pl.pallas_call(
        body, name=name, grid=(npair, s // tq),
        in_specs=[pl.BlockSpec((tq, LANES), lambda p, i: (i, q_blk0 + p)),
                  pl.BlockSpec((m, LANES), lambda p, i: (0, p)),
                  pl.BlockSpec((m, LANES), lambda p, i: (0, npair + p)), orow, orow, row],
        out_specs=[row, acc, acc],
        out_shape=[jax.ShapeDtypeStruct((s, MEM_W), F32), jax.ShapeDtypeStruct((m, MEM_W), F32),
                   jax.ShapeDtypeStruct((m, MEM_W), F32)],
        compiler_params=_cp("parallel", "arbitrary"))(hb, kv, kv, dcat, cat, lse)


def _gate_fwd(f_t, bias, name):
    hp, s = f_t.shape
    nblk = s // LANES

    def body(f_ref, b_ref, c_ref):
        lane = lax.broadcasted_iota(jnp.int32, (hp, LANES), 1)

        def step(i, carry):
            off = pl.multiple_of(i * LANES, LANES)
            x = f_ref[:, pl.ds(off, LANES)] + b_ref[...]
            acc = jnp.minimum(x, 0.0) - jnp.log(1.0 + jnp.exp(-jnp.abs(x)))
            sh = 1
            while sh < LANES:
                acc = acc + jnp.where(lane >= sh, pltpu.roll(acc, sh, 1), 0.0)
                sh *= 2
            acc = acc + carry
            c_ref[:, pl.ds(off, LANES)] = acc
            return acc[:, LANES - 1:LANES]

        lax.fori_loop(0, nblk, step, jnp.zeros((hp, 1), F32))

    vm = pl.BlockSpec(memory_space=pltpu.VMEM)
    return pl.pallas_call(body, name=name, in_specs=[vm, vm], out_specs=vm,
                          out_shape=jax.ShapeDtypeStruct((hp, s), F32),
                          compiler_params=pltpu.CompilerParams(vmem_limit_bytes=VMEM_LIMIT))(f_t, bias)


def _gate_bwd(dc_t, f_t, bias, name):
    hp, s = f_t.shape
    nblk = s // LANES

    def body(dc_ref, f_ref, b_ref, df_ref, db_ref):
        lane = lax.broadcasted_iota(jnp.int32, (hp, LANES), 1)

        def step(t, carry):
            suffix, dbias = carry
            off = pl.multiple_of((nblk - 1 - t) * LANES, LANES)
            acc = dc_ref[:, pl.ds(off, LANES)]
            sh = 1
            while sh < LANES:
                acc = acc + jnp.where(lane < LANES - sh, pltpu.roll(acc, LANES - sh, 1), 0.0)
                sh *= 2
            acc = acc + suffix
            x = f_ref[:, pl.ds(off, LANES)] + b_ref[...]
            df = acc * _sigmoid(-x)
            df_ref[:, pl.ds(off, LANES)] = df
            return acc[:, 0:1], dbias + jnp.sum(df, axis=1, keepdims=True)

        _, dbias = lax.fori_loop(0, nblk, step, (jnp.zeros((hp, 1), F32), jnp.zeros((hp, 1), F32)))
        db_ref[...] = dbias

    vm = pl.BlockSpec(memory_space=pltpu.VMEM)
    return pl.pallas_call(body, name=name, in_specs=[vm, vm, vm], out_specs=[vm, vm],
                          out_shape=[jax.ShapeDtypeStruct((hp, s), F32), jax.ShapeDtypeStruct((hp, 1), F32)],
                          compiler_params=pltpu.CompilerParams(vmem_limit_bytes=VMEM_LIMIT))(dc_t, f_t, bias)


def _wide(rep, width):
    return jnp.tile(rep, (1, width // LANES))


def _fold(t):
    part = t[:, :LANES]
    for c in range(1, t.shape[1] // LANES):
        part = part + t[:, c * LANES:(c + 1) * LANES]
    return part


def _fox_logits(q, k, cq_rep, ck_row, mask, hmask):
    s = _dot_nt(_sel(hmask, q), k) + (_wide(cq_rep, ck_row.shape[1]) - ck_row)
    if mask is not None:
        s = jnp.where(mask, s, NEG)
    return s


def _diag_mask(t):
    return lax.broadcasted_iota(jnp.int32, (t, t), 1) <= lax.broadcasted_iota(jnp.int32, (t, t), 0)


def _fox_fwd(hb, c_rep, c_t3, name):
    s = hb.shape[0]
    npair = MIX_W // LANES
    tq = tk = _rows(s)
    nq = s // tq

    def body(q_ref, k_ref, v_ref, cq_ref, ck_ref, o_ref, l_ref, m_s, l_s, acc):
        qi = pl.program_id(1)
        kj = pl.program_id(2)
        hm = _head_masks()

        @pl.when(kj == 0)
        def _():
            m_s[...] = jnp.full_like(m_s, NEG)
            l_s[...] = jnp.zeros_like(l_s)
            acc[...] = jnp.zeros_like(acc)

        def step(mask):
            q, k, v = q_ref[...] * SCALE, k_ref[...], v_ref[...]
            ck = ck_ref[...]
            for h in range(2):
                sc = _fox_logits(q, k, cq_ref[h], ck[h:h + 1, :], mask, hm[h])
                m_old = m_s[h]
                m_new = jnp.maximum(m_old, jnp.max(sc, axis=1, keepdims=True))
                pr = jnp.exp(sc - _wide(m_new, tk))
                corr = jnp.exp(m_old - m_new)
                l_s[h] = l_s[h] * corr + _fold(pr)
                acc[h] = acc[h] * corr + _dot(pr.astype(BF), _sel(hm[h], v))
                m_s[h] = m_new

        @pl.when(kj < qi)
        def _():
            step(None)

        @pl.when(kj == qi)
        def _():
            step(_diag_mask(tq))
            outs = []
            for h in range(2):
                den = jnp.sum(l_s[h], axis=1, keepdims=True)
                outs.append(acc[h] / den)
                l_ref[h] = m_s[h] + jnp.log(den)
            o_ref[...] = jnp.where(hm[0], outs[0], outs[1]).astype(BF)

    def kv_map(off):
        return lambda p, i, j: (jnp.minimum(j, i), off + p)

    blk = pl.BlockSpec((tq, LANES), lambda p, i, j: (i, p))
    return pl.pallas_call(
        body, name=name, grid=(npair, nq, nq),
        in_specs=[blk, pl.BlockSpec((tk, LANES), kv_map(npair)), pl.BlockSpec((tk, LANES), kv_map(2 * npair)),
                  pl.BlockSpec((2, tq, LANES), lambda p, i, j: (p, i, 0)),
                  pl.BlockSpec((None, 2, tk), lambda p, i, j: (p, 0, jnp.minimum(j, i)))],
        out_specs=[blk, pl.BlockSpec((2, tq, LANES), lambda p, i, j: (p, i, 0))],
        out_shape=[jax.ShapeDtypeStruct((s, MIX_W), BF), jax.ShapeDtypeStruct((2 * npair, s, LANES), F32)],
        scratch_shapes=[pltpu.VMEM((2, tq, LANES), F32), pltpu.VMEM((2, tq, LANES), F32),
                        pltpu.VMEM((2, tq, LANES), F32)],
        compiler_params=_cp("parallel", "parallel", "arbitrary"))(hb, hb, hb, c_rep, c_t3)


def _fox_dsum(hb, dcat, lse, c_rep, c_t3, name):
    s = hb.shape[0]
    npair = MIX_W // LANES
    tq = tk = _rows(s)
    nq = s // tq

    def body(q_ref, k_ref, v_ref, do_ref, l_ref, cq_ref, ck_ref, d_ref, acc):
        qi = pl.program_id(1)
        kj = pl.program_id(2)
        hm = _head_masks()

        @pl.when(kj == 0)
        def _():
            acc[...] = jnp.zeros_like(acc)

        def step(mask):
            q, k, v, do = q_ref[...] * SCALE, k_ref[...], v_ref[...], do_ref[...]
            ck = ck_ref[...]
            for h in range(2):
                pr = jnp.exp(_fox_logits(q, k, cq_ref[h], ck[h:h + 1, :], mask, hm[h]) - _wide(l_ref[h], tk))
                acc[h] += _fold(pr * _dot_nt(_sel(hm[h], do), v))

        @pl.when(kj < qi)
        def _():
            step(None)

        @pl.when(kj == qi)
        def _():
            step(_diag_mask(tq))
            for h in range(2):
                d_ref[h] = jnp.broadcast_to(jnp.sum(acc[h], axis=1, keepdims=True), (tq, LANES))

    def kv_map(off):
        return lambda p, i, j: (jnp.minimum(j, i), off + p)

    blk = pl.BlockSpec((tq, LANES), lambda p, i, j: (i, p))
    rep = pl.BlockSpec((2, tq, LANES), lambda p, i, j: (p, i, 0))
    return pl.pallas_call(
        body, name=name, grid=(npair, nq, nq),
        in_specs=[blk, pl.BlockSpec((tk, LANES), kv_map(npair)), pl.BlockSpec((tk, LANES), kv_map(2 * npair)),
                  blk, rep, rep, pl.BlockSpec((None, 2, tk), lambda p, i, j: (p, 0, jnp.minimum(j, i)))],
        out_specs=rep, out_shape=jax.ShapeDtypeStruct((2 * npair, s, LANES), F32),
        scratch_shapes=[pltpu.VMEM((2, tq, LANES), F32)],
        compiler_params=_cp("parallel", "parallel", "arbitrary"))(hb, hb, hb, dcat, lse, c_rep, c_t3)


def _fox_bwd(hb, dcat, dsum, lse, c_rep, c_t3, name):
    s = hb.shape[0]
    npair = MIX_W // LANES
    tq = tk = _rows(s)
    nq = s // tq

    def body(q_ref, k_ref, v_ref, do_ref, d_ref, l_ref, cq_ref, ck_ref, dq_ref, dk_ref, dv_ref, dc_ref):
        kj = pl.program_id(1)
        qi = pl.program_id(2)
        hm = _head_masks()

        @pl.when(qi == 0)
        def _():
            dk_ref[...] = jnp.zeros_like(dk_ref)
            dv_ref[...] = jnp.zeros_like(dv_ref)
            dc_ref[...] = jnp.zeros_like(dc_ref)

        @pl.when((qi == 0) & (kj == 0))
        def _():
            dq_ref[...] = jnp.zeros_like(dq_ref)

        def step(mask):
            q, k, v, do = q_ref[...] * SCALE, k_ref[...], v_ref[...], do_ref[...]
            ck = ck_ref[...]
            dq = jnp.zeros((tq, LANES), F32)
            dk = jnp.zeros((tk, LANES), F32)
            dv = jnp.zeros((tk, LANES), F32)
            dcs = []
            for h in range(2):
                qh, doh = _sel(hm[h], q), _sel(hm[h], do)
                pr = jnp.exp(_fox_logits(q, k, cq_ref[h], ck[h:h + 1, :], mask, hm[h]) - _wide(l_ref[h], tk))
                ds = pr * (_dot_nt(doh, v) - _wide(d_ref[h], tk))
                dsb = ds.astype(BF)
                dq = dq + _dot(dsb, _sel(hm[h], k))
                dk = dk + _dot_tn(dsb, qh)
                dv = dv + _dot_tn(pr.astype(BF), doh)
                dcs.append(jnp.sum(ds, axis=0, keepdims=True))
            rows = pl.ds(pl.multiple_of(qi * tq, tq), tq)
            dq_ref[rows, :] += SCALE * dq
            dk_ref[...] += dk
            dv_ref[...] += dv
            dc_ref[...] -= jnp.concatenate(dcs, axis=0)

        @pl.when(qi > kj)
        def _():
            step(None)

        @pl.when(qi == kj)
        def _():
            step(_diag_mask(tq))

    def q_map(p, j, i):
        return (jnp.maximum(i, j), p)

    kblk = pl.BlockSpec((tk, LANES), lambda p, j, i: (j, p))
    rep = pl.BlockSpec((2, tq, LANES), lambda p, j, i: (p, jnp.maximum(i, j), 0))
    return pl.pallas_call(
        body, name=name, grid=(npair, nq, nq),
        in_specs=[pl.BlockSpec((tq, LANES), q_map),
                  pl.BlockSpec((tk, LANES), lambda p, j, i: (j, npair + p)),
                  pl.BlockSpec((tk, LANES), lambda p, j, i: (j, 2 * npair + p)),
                  pl.BlockSpec((tq, LANES), q_map), rep, rep, rep,
                  pl.BlockSpec((None, 2, tk), lambda p, j, i: (p, 0, j))],
        out_specs=[pl.BlockSpec((s, LANES), lambda p, j, i: (0, p)), kblk, kblk,
                   pl.BlockSpec((None, 2, tk), lambda p, j, i: (p, 0, j))],
        out_shape=[jax.ShapeDtypeStruct((s, MIX_W), F32), jax.ShapeDtypeStruct((s, MIX_W), F32),
                   jax.ShapeDtypeStruct((s, MIX_W), F32), jax.ShapeDtypeStruct((npair, 2, s), F32)],
        compiler_params=_cp("arbitrary", "arbitrary", "arbitrary"))(hb, hb, hb, dcat, dsum, lse, c_rep, c_t3)


def _loss_head(y, target, name):
    s, d = y.shape
    ts = _rows(s)

    def body(y_ref, t_ref, dy_ref, l_ref):
        i = pl.program_id(0)
        e = y_ref[...] - t_ref[...]
        dy_ref[...] = e * (1.0 / d)

        @pl.when(i == 0)
        def _():
            l_ref[...] = jnp.zeros_like(l_ref)

        part = jnp.sum(jnp.sum(e * e, axis=1, keepdims=True), axis=0, keepdims=True)
        l_ref[...] += part * (0.5 / d)

    row = pl.BlockSpec((ts, d), lambda i: (i, 0))
    return pl.pallas_call(
        body, name=name, grid=(s // ts,), in_specs=[row, row],
        out_specs=[row, pl.BlockSpec((1, 1), lambda i: (0, 0))],
        out_shape=[jax.ShapeDtypeStruct((s, d), F32), jax.ShapeDtypeStruct((1, 1), F32)],
        compiler_params=_cp("arbitrary"))(y, target)


def _adam_rows(r, c):
    cap = max(8, (1 << 20) // (4 * c))
    if r <= cap:
        return r
    best = None
    for t in range(8, cap + 1, 8):
        if r % t == 0:
            best = t
    return best if best is not None else r


def _reduce_adamw(contribs, w, m, v, name):
    nl = len(contribs)
    nd, r, c = contribs[0].shape
    tr = _adam_rows(r, c)
    bc1 = 1.0 - ADAM_B1 ** ADAM_STEP
    bc2 = 1.0 - ADAM_B2 ** ADAM_STEP

    def body(*refs):
        c_refs = refs[:nl]
        w_ref, m_ref, v_ref, g_ref, d_ref, nm_ref, nv_ref = refs[nl:]
        l = pl.program_id(0)
        for li in range(nl):
            @pl.when(l == li)
            def _(c_ref=c_refs[li]):
                g = c_ref[0].astype(F32)
                for k in range(1, nd):
                    g = g + c_ref[k].astype(F32)
                nm = ADAM_B1 * m_ref[...] + (1.0 - ADAM_B1) * g
                nv = ADAM_B2 * v_ref[...] + (1.0 - ADAM_B2) * (g * g)
                g_ref[...] = g
                nm_ref[...] = nm
                nv_ref[...] = nv
                d_ref[...] = -ADAM_LR * ((nm / bc1) / (jnp.sqrt(nv / bc2) + ADAM_EPS) + ADAM_WD * w_ref[...])

    def c_spec(li):
        return pl.BlockSpec((nd, tr, c), lambda l, i: (0, jnp.where(l == li, i, 0), 0))

    blk = pl.BlockSpec((None, tr, c), lambda l, i: (l, i, 0))
    out = jax.ShapeDtypeStruct((nl, r, c), F32)
    return pl.pallas_call(
        body, name=name, grid=(nl, r // tr),
        in_specs=[c_spec(li) for li in range(nl)] + [blk, blk, blk],
        out_specs=[blk, blk, blk, blk], out_shape=[out, out, out, out],
        compiler_params=_cp("arbitrary", "arbitrary"))(*contribs, w, m, v)


def _mesh_pos():
    return lax.axis_index("x"), lax.axis_index("y"), lax.axis_index("c")


def _peer(pos, k):
    x, y, c = pos
    return (1 - x if k & 4 else x, 1 - y if k & 2 else y, 1 - c if k & 1 else c)


def _linear(pos):
    return 4 * pos[0] + 2 * pos[1] + pos[2]


def _xfer_copies(srcs, lands, send_sems, recv_sems, local_sems, gather):
    pos = _mesh_pos()
    me = _linear(pos)
    local, remote = [], []
    for i, (src, land) in enumerate(zip(srcs, lands)):
        local.append(pltpu.make_async_copy(src if gather else src.at[me], land.at[me], local_sems.at[i]))
        for k in range(1, N_DEV):
            peer = _peer(pos, k)
            remote.append(pltpu.make_async_remote_copy(
                src_ref=src if gather else src.at[_linear(peer)], dst_ref=land.at[me],
                send_sem=send_sems.at[i * (N_DEV - 1) + k - 1], recv_sem=recv_sems.at[i * (N_DEV - 1) + k - 1],
                device_id=peer, device_id_type=MESH_ID))
    return local, remote


_HBM = pl.BlockSpec(memory_space=pltpu.HBM)
_SEM = pl.BlockSpec(memory_space=pltpu.SEMAPHORE)
_EFFECT = pltpu.SideEffectType.DATAFLOW_SIDE_EFFECTING


def _xfer_start(srcs, gather, name, after=()):
    n = len(srcs)
    na = len(after)
    lands = [lax.empty(((N_DEV,) + a.shape) if gather else a.shape, a.dtype) for a in srcs]

    def body(*refs):
        src, land = refs[:n], refs[n:2 * n]
        send_sems, recv_sems, local_sems = refs[2 * n + na:2 * n + na + 3]
        local, remote = _xfer_copies(src, land, send_sems, recv_sems, local_sems, gather)
        for cp in local + remote:
            cp.start()
        refs[-1][...] = jnp.zeros_like(refs[-1])

    nsem = n * (N_DEV - 1)
    out = pl.pallas_call(
        body, name=name,
        out_shape=(pltpu.SemaphoreType.DMA((nsem,)), pltpu.SemaphoreType.DMA((nsem,)), pltpu.SemaphoreType.DMA((n,)),
                   *[pltpu.HBM(a.shape, a.dtype) for a in srcs], *[pltpu.HBM(a.shape, a.dtype) for a in lands],
                   jax.ShapeDtypeStruct((8, LANES), F32)),
        in_specs=[_HBM] * (2 * n) + [pl.BlockSpec(memory_space=pl.ANY)] * na,
        out_specs=(_SEM, _SEM, _SEM, *[_HBM] * (2 * n), pl.BlockSpec(memory_space=pltpu.VMEM)),
        input_output_aliases={i: 3 + i for i in range(2 * n)},
        compiler_params=pltpu.CompilerParams(has_side_effects=_EFFECT))(
            *[pltpu.with_memory_space_constraint(a, pltpu.HBM) for a in srcs],
            *[pltpu.with_memory_space_constraint(a, pltpu.HBM) for a in lands], *after)
    return out[:3], list(out[3:3 + n]), list(out[3 + n:3 + 2 * n]), out[-1]


def _started(handle):
    return handle[3]


def _xfer_wait(handle, after, gather, name):
    sems, srcs, lands, _ = handle
    n = len(srcs)

    def body(*refs):
        src, land = refs[:n], refs[n:2 * n]
        send_sems, recv_sems, local_sems = refs[2 * n:2 * n + 3]
        local, remote = _xfer_copies(src, land, send_sems, recv_sems, local_sems, gather)
        for cp in local:
            cp.wait()
        for cp in remote:
            cp.wait_send()
            cp.wait_recv()

    out = pl.pallas_call(
        body, name=name,
        out_shape=(*[pltpu.HBM(a.shape, a.dtype) for a in srcs], *[pltpu.HBM(a.shape, a.dtype) for a in lands]),
        in_specs=[_HBM] * (2 * n) + [_SEM] * 3 + [pl.BlockSpec(memory_space=pl.ANY)] * len(after),
        out_specs=tuple([_HBM] * (2 * n)), input_output_aliases={i: i for i in range(2 * n)},
        compiler_params=pltpu.CompilerParams(has_side_effects=_EFFECT))(*srcs, *lands, *sems, *after)
    return list(out[n:])


def _cols_full(g):
    nd, r, c = g.shape
    return jnp.transpose(g, (1, 0, 2)).reshape(r, nd * c)


def _cols_split(full):
    r, n = full.shape
    return jnp.transpose(full.reshape(r, N_DEV, n // N_DEV), (1, 0, 2))


def _pack_b_in(w):
    qkv = 3 * MIX_W
    pad = jnp.zeros((w.shape[0], B_IN_PAD - w.shape[1]), w.dtype)
    return jnp.concatenate([w[:, :qkv], w[:, qkv + N_MIX_HEADS:], w[:, qkv:qkv + N_MIX_HEADS], pad], axis=1)


def _unpack_b_in(w):
    qkv = 3 * MIX_W
    return jnp.concatenate([w[:, :qkv], w[:, qkv + MEM_W:qkv + MEM_W + N_MIX_HEADS], w[:, qkv:qkv + MEM_W]], axis=1)


def _to_classes(t, g):
    r = 4 ** g
    s, w = t.shape
    return jnp.transpose(t.reshape(s // r, r, w), (1, 0, 2)).reshape(s, w)


def _from_classes(t, g):
    r = 4 ** g
    s, w = t.shape
    return jnp.transpose(t.reshape(r, s // r, w), (1, 0, 2)).reshape(s, w)


def _group_stack(t):
    return jnp.stack([_to_classes(t[:, g * GROUP_W:(g + 1) * GROUP_W], g) for g in range(N_GROUPS)])


def _group_unstack(t3):
    return jnp.concatenate([_from_classes(t3[g], g) for g in range(N_GROUPS)], axis=1)


def _same_stack(t):
    return jnp.stack([_to_classes(t, g) for g in range(N_GROUPS)])


def _same_unstack(t3):
    return jnp.stack([_from_classes(t3[g], g) for g in range(N_GROUPS)])


def _ffn_forward(x, xb, wgu, get_rest, tag):
    gu, a = _ffn_up(xb, wgu, f"{tag}_up")
    wd4, gain, bias = get_rest(a)
    y, yb, xh, rstd = _mm_res_ln(a, wd4, x, gain, bias, 0.5, f"{tag}_down_ln")
    return y, yb, (xb, gu, a, xh, rstd), wd4


def _ffn_backward(dy, saved, wgu, wd4, gain, tag, after=()):
    xb, gu, a, xh, rstd = saved
    s = xb.shape[0]
    nd, c, d = wgu.shape
    dz, dzb, dgain, dbias = _ln_bwd(dy, xh, rstd, gain, 0.5, f"{tag}_ln_bwd", after)
    dh = _ffn_bwd_act(dzb, wd4, gu, f"{tag}_act_bwd").reshape(nd, s, c)
    dwd = _mm_tn(a, dzb[None], f"{tag}_dwd").reshape(nd, wd4.shape[1] // 2, d)
    dx = _mm_nt(dh, wgu, f"{tag}_dx", res=dz, w_rows_out=False)
    dwgu = _mm_tn(dh, xb[None], f"{tag}_dwgu")
    return dx, dwgu, dwd, dgain, dbias


def _mixer_a_forward(x, xb, memb, w_in, w_kv, w_out, gain, bias, tabs):
    h = _mm_nn(xb, w_in, F32, "a_in", b_rows_out=True)
    hb = _rope_cast([h], tabs, 2 * MIX_W // LANES, "a_rope")
    q3 = _group_stack(hb[:, :MIX_W])
    k3 = _group_stack(hb[:, MIX_W:2 * MIX_W])
    v3 = _group_stack(hb[:, 2 * MIX_W:3 * MIX_W])
    o3, l3 = _band_fwd(q3, k3, v3, "a_band_fwd")
    oa, lt = _band_combine(_same_unstack(o3), _same_unstack(l3), "a_combine")
    kv = _mm_nn(memb, w_kv, BF, "a_mem_kv")
    om, lm = _mem_fwd(hb, 3 * MIX_W // LANES, kv, "a_mem_fwd")
    cat = jnp.concatenate([oa, om], axis=1)
    y, yb, xh, rstd = _mm_res_ln(cat[None], w_out[None], x, gain, bias, 1.0, "a_out_ln")
    return y, yb, (xb, hb, q3, k3, v3, oa, lt, kv, lm, cat, xh, rstd)


def _mixer_a_backward(dy, saved, memb, w_in, w_kv, w_out, gain, tabs_neg, after=()):
    xb, hb, q3, k3, v3, oa, lt, kv, lm, cat, xh, rstd = saved
    dz, dzb, dgain, dbias = _ln_bwd(dy, xh, rstd, gain, 1.0, "a_ln_bwd", after)
    dcat = _mm_nt(dzb[None], w_out[None], "a_dcat", out_dtype=BF)
    dw_out = _mm_tn(cat[None], dzb[None], "a_dwout")[0]
    dqm, dkm, dvm = _mem_bwd(hb, 3 * MIX_W // LANES, kv, dcat, cat, GROUP_W // LANES, lm, "a_mem_bwd")
    dkv = jnp.concatenate([dkm, dvm], axis=1).astype(BF)
    dw_kv = _mm_tn(memb[None], dkv[None], "a_dwkv")[0]
    dq3, dk3, dv3 = _band_bwd(q3, k3, v3, _same_stack(dcat[:, :GROUP_W]), _same_stack(oa), _same_stack(lt),
                              "a_band_bwd")
    dhb = _rope_cast([_group_unstack(dq3), _group_unstack(dk3), _group_unstack(dv3), dqm], tabs_neg,
                     2 * MIX_W // LANES, "a_rope_bwd")
    dw_in = _mm_tn(dhb[None], xb[None], "a_dwin")[0]
    dx = _mm_nt(dhb[None], w_in[None], "a_dx", res=dz, w_rows_out=False)
    return dx, dw_in, dw_kv, dw_out, dgain, dbias


def _pad_rows(t, rows):
    return jnp.concatenate([t, jnp.zeros((rows - t.shape[0], t.shape[1]), t.dtype)], axis=0)


def _pad_cols(t, cols):
    return jnp.concatenate([t, jnp.zeros((t.shape[0], cols - t.shape[1]), t.dtype)], axis=1)


def _mixer_b_forward(x, xb, memb, w_in, fbias, w_kv, w_out, gain, bias, tabs):
    s = x.shape[0]
    h = _mm_nn(xb, w_in, F32, "b_in")
    hb = _rope_cast([h], tabs, 0, "b_cast")
    f0 = 3 * MIX_W + MEM_W
    f_t = _pad_rows(jnp.transpose(h[:, f0:f0 + N_MIX_HEADS]), 16)
    bias16 = _pad_rows(jnp.transpose(fbias), 16)
    c_t = _gate_fwd(f_t, bias16, "b_gate_fwd")
    c_t3 = c_t[:N_MIX_HEADS].reshape(N_MIX_HEADS // 2, 2, s)
    c_rep = jnp.broadcast_to(c_t[:N_MIX_HEADS, :, None], (N_MIX_HEADS, s, LANES))
    ob, lb = _fox_fwd(hb, c_rep, c_t3, "b_fox_fwd")
    kv = _mm_nn(memb, w_kv, BF, "b_mem_kv")
    om, lm = _mem_fwd(hb, 3 * MIX_W // LANES, kv, "b_mem_fwd")
    cat = jnp.concatenate([ob, om], axis=1)
    y, yb, xh, rstd = _mm_res_ln(cat[None], w_out[None], x, gain, bias, 1.0, "b_out_ln")
    return y, yb, (xb, hb, f_t, bias16, c_rep, c_t3, lb, kv, lm, cat, xh, rstd)


def _mixer_b_backward(dy, saved, memb, w_in, w_kv, w_out, gain, tabs, after=()):
    xb, hb, f_t, bias16, c_rep, c_t3, lb, kv, lm, cat, xh, rstd = saved
    s = xb.shape[0]
    dz, dzb, dgain, dbias = _ln_bwd(dy, xh, rstd, gain, 1.0, "b_ln_bwd", after)
    dcat = _mm_nt(dzb[None], w_out[None], "b_dcat", out_dtype=BF)
    dw_out = _mm_tn(cat[None], dzb[None], "b_dwout")[0]
    dqm, dkm, dvm = _mem_bwd(hb, 3 * MIX_W // LANES, kv, dcat, cat, MIX_W // LANES, lm, "b_mem_bwd")
    dkv = jnp.concatenate([dkm, dvm], axis=1).astype(BF)
    dw_kv = _mm_tn(memb[None], dkv[None], "b_dwkv")[0]
    dsum = _fox_dsum(hb, dcat, lb, c_rep, c_t3, "b_fox_dsum")
    dq, dk, dv, dc3 = _fox_bwd(hb, dcat, dsum, lb, c_rep, c_t3, "b_fox_bwd")
    df_t, dfb = _gate_bwd(_pad_rows(dc3.reshape(N_MIX_HEADS, s), 16), f_t, bias16, "b_gate_bwd")
    df = _pad_cols(jnp.transpose(df_t[:N_MIX_HEADS]), B_IN_PAD - 3 * MIX_W - MEM_W)
    dhb = _rope_cast([dq, dk, dv, dqm, df], tabs, 0, "b_cast_bwd")
    dw_in = _mm_tn(xb[None], dhb[None], "b_dwin")[0]
    dx = _mm_nt(dhb[None], w_in[None], "b_dx", res=dz)
    return dx, dw_in, jnp.transpose(dfb[:N_MIX_HEADS]), dw_kv, dw_out, dgain, dbias


def _stored(t, name):
    return jnp.transpose(t, (0, 2, 1)) if name in ROWS_OUT else t


def _weight_groups(w):
    b = {n: _stored(w[n], n).astype(BF) for n in WEIGHTS if n not in F32_COMM}
    return [
        [b["ffn1_w_gate_up"][0]],
        [b["ffn1_w_down"][0], w["ln_gain"], w["ln_bias"]],
        [b["a_w_in"][0], b["a_w_out"][0], b["mem_w_kv"][0]],
        [b["ffn2_w_gate_up"][0], b["ffn2_w_down"][0]],
        [b["ffn1_w_gate_up"][1], b["ffn1_w_down"][1]],
        [b["b_w_in"][0], b["b_w_out"][0], b["mem_w_kv"][1]],
        [b["ffn2_w_gate_up"][1], b["ffn2_w_down"][1]],
    ]


def _local_step(x, mem, target, fbias, get_w, put_g):
    s, d = x.shape
    tabs = _rope_tables(s, 1.0)
    tabs_neg = _rope_tables(s, -1.0)
    memb = mem.astype(BF)
    saved, wl = [], []
    cur, curb = x, x.astype(BF)
    ln = []

    def down4(t):
        return t.reshape(N_DEV // 2, -1, d)

    for i in range(DEPTH):
        if i == 0:
            def first_rest(a):
                g = get_w(1, a)
                ln.extend(jnp.transpose(t, (1, 2, 0, 3)).reshape(DEPTH, 3, 1, d) for t in g[1:3])
                return down4(g[0]), ln[0][0, 0], ln[1][0, 0]

            wgu = get_w(0, cur)[0]
            cur, curb, s1, wd = _ffn_forward(cur, curb, wgu, first_rest, "l0_ffn1")
        else:
            g = get_w(3 * i + 1, cur)
            wgu = g[0]
            cur, curb, s1, wd = _ffn_forward(cur, curb, wgu, lambda a, g=g: (down4(g[1]), ln[0][i, 0], ln[1][i, 0]),
                                             f"l{i}_ffn1")
        w1 = (wgu, wd)
        ln_g, ln_b = ln
        g = get_w(3 * i + 2, cur)
        if i == 0:
            wm = (g[0].reshape(-1, d), g[2].reshape(d, -1), _cols_full(g[1]))
            cur, curb, s2 = _mixer_a_forward(cur, curb, memb, wm[0], wm[1], wm[2], ln_g[i, 1], ln_b[i, 1], tabs)
        else:
            wm = (_pack_b_in(g[0].reshape(d, -1)), g[2].reshape(d, -1), g[1].reshape(d, -1))
            cur, curb, s2 = _mixer_b_forward(cur, curb, memb, wm[0], fbias, wm[1], wm[2], ln_g[i, 1], ln_b[i, 1],
                                             tabs)
        g = get_w(3 * i + 3, cur)
        cur, curb, s3, wd = _ffn_forward(cur, curb, g[0], lambda a, g=g: (down4(g[1]), ln_g[i, 2], ln_b[i, 2]),
                                         f"l{i}_ffn2")
        w3 = (g[0], wd)
        saved.append((s1, s2, s3))
        wl.append((w1, wm, w3))

    dy, loss = _loss_head(cur, target, "loss_head")

    dgs = [[None] * 3 for _ in range(DEPTH)]
    dbs = [[None] * 3 for _ in range(DEPTH)]
    sent = ()
    for i in reversed(range(DEPTH)):
        s1, s2, s3 = saved[i]
        w1, wm, w3 = wl[i]
        dy, dgu, dd, dgs[i][2], dbs[i][2] = _ffn_backward(dy, s3, w3[0], w3[1], ln_g[i, 2], f"l{i}_ffn2", sent)
        sent = put_g(3 * i + 2, [dgu, dd])
        if i == 0:
            dy, dw_in, dw_kv, dw_out, dgs[i][1], dbs[i][1] = _mixer_a_backward(
                dy, s2, memb, wm[0], wm[1], wm[2], ln_g[i, 1], tabs_neg, sent)
            sent = put_g(1, [dw_in.reshape(N_DEV, -1, d), _cols_split(dw_out),
                             dw_kv.reshape(N_DEV, d // N_DEV, -1)])
        else:
            dy, dw_in, dfb, dw_kv, dw_out, dgs[i][1], dbs[i][1] = _mixer_b_backward(
                dy, s2, memb, wm[0], wm[1], wm[2], ln_g[i, 1], tabs, sent)
            sent = put_g(4, [_unpack_b_in(dw_in).reshape(N_DEV, d // N_DEV, -1),
                             dw_out.reshape(N_DEV, d // N_DEV, -1), dw_kv.reshape(N_DEV, d // N_DEV, -1),
                             jnp.broadcast_to(dfb[None], (N_DEV,) + dfb.shape)])
        dy, dgu, dd, dgs[i][0], dbs[i][0] = _ffn_backward(dy, s1, w1[0], w1[1], ln_g[i, 0], f"l{i}_ffn1", sent)
        if i == 0:
            ln_pieces = []
            for parts in (dgs, dbs):
                t = jnp.concatenate([parts[a][b] for a in range(DEPTH) for b in range(3)], axis=0)
                ln_pieces.append(jnp.transpose(t.reshape(DEPTH * 3, N_DEV, d // N_DEV), (1, 0, 2)))
            sent = put_g(0, [dgu, dd] + ln_pieces)
        else:
            sent = put_g(3, [dgu, dd])
    return loss, dy


WEIGHTS = ("ffn1_w_gate_up", "ffn1_w_down", "ffn2_w_gate_up", "ffn2_w_down", "ln_gain", "ln_bias", "mem_w_kv",
           "a_w_in", "a_w_out", "b_w_in", "b_forget_bias", "b_w_out")
F32_COMM = ("ln_gain", "ln_bias", "b_forget_bias")
ROWS_OUT = ("ffn1_w_gate_up", "ffn2_w_gate_up", "a_w_in")
GRAD_SLOTS = {
    "ffn1_w_gate_up": [(0, 0), (3, 0)], "ffn1_w_down": [(0, 1), (3, 1)],
    "ffn2_w_gate_up": [(2, 0), (5, 0)], "ffn2_w_down": [(2, 1), (5, 1)],
    "ln_gain": [(0, 2)], "ln_bias": [(0, 3)], "mem_w_kv": [(1, 2), (4, 2)],
    "a_w_in": [(1, 0)], "a_w_out": [(1, 1)], "b_w_in": [(4, 0)], "b_forget_bias": [(4, 3)], "b_w_out": [(4, 1)],
}


def kernel(x, mem, ffn1_w_gate_up, ffn1_w_down, ffn2_w_gate_up, ffn2_w_down, ln_gain, ln_bias, mem_w_kv, a_w_in, a_w_out, b_w_in, b_forget_bias, b_w_out, loss_target, m_ffn1_w_gate_up, m_ffn1_w_down, m_ffn2_w_gate_up, m_ffn2_w_down, m_ln_gain, m_ln_bias, m_mem_w_kv, m_a_w_in, m_a_w_out, m_b_w_in, m_b_forget_bias, m_b_w_out, v_ffn1_w_gate_up, v_ffn1_w_down, v_ffn2_w_gate_up, v_ffn2_w_down, v_ln_gain, v_ln_bias, v_mem_w_kv, v_a_w_in, v_a_w_out, v_b_w_in, v_b_forget_bias, v_b_w_out):
    w = dict(zip(WEIGHTS, (ffn1_w_gate_up, ffn1_w_down, ffn2_w_gate_up, ffn2_w_down, ln_gain, ln_bias, mem_w_kv,
                           a_w_in, a_w_out, b_w_in, b_forget_bias, b_w_out)))
    m = dict(zip(WEIGHTS, (m_ffn1_w_gate_up, m_ffn1_w_down, m_ffn2_w_gate_up, m_ffn2_w_down, m_ln_gain, m_ln_bias,
                           m_mem_w_kv, m_a_w_in, m_a_w_out, m_b_w_in, m_b_forget_bias, m_b_w_out)))
    v = dict(zip(WEIGHTS, (v_ffn1_w_gate_up, v_ffn1_w_down, v_ffn2_w_gate_up, v_ffn2_w_down, v_ln_gain, v_ln_bias,
                           v_mem_w_kv, v_a_w_in, v_a_w_out, v_b_w_in, v_b_forget_bias, v_b_w_out)))

    gathers = []
    for k, grp in enumerate(_weight_groups(w)):
        gathers.append(_xfer_start(grp, True, f"gather{k}_start", [_started(h) for h in gathers[-1:]]))
    exchanges = {}

    def get_w(k, after):
        behind = [after] + ([_started(h) for h in gathers] if k == 0 else [])
        return _xfer_wait(gathers[k], behind, True, f"gather{k}_wait")

    def put_g(k, pieces):
        exchanges[k] = _xfer_start(pieces, False, f"grads{k}_start")
        return (_started(exchanges[k]),)

    loss, grad_x = _local_step(x[0], mem[0], loss_target[0], b_forget_bias, get_w, put_g)
    loss = lax.psum(loss[0, 0], ("x", "y", "c"))

    outs, landed = {}, {}

    def adamw(names, after):
        for n in names:
            contribs = [landed[g][j] for g, j in GRAD_SLOTS[n]]
            view = (len(contribs),) + contribs[0].shape[1:]
            shape = _stored(w[n], n).shape
            res = _reduce_adamw(contribs, *[_stored(t[n], n).reshape(view) for t in (w, m, v)], f"adamw_{n}")
            outs[n] = [_stored(t.reshape(shape), n) for t in res]
            after = outs[n][0]
        return after

    after = [grad_x, _started(exchanges[0])]
    for k in (5, 4, 3, 2, 1):
        landed[k] = _xfer_wait(exchanges[k], after, False, f"grads{k}_wait")
        after = [landed[k][0]]
    done = adamw(("ffn2_w_gate_up", "ffn2_w_down", "mem_w_kv", "a_w_in", "a_w_out", "b_w_in", "b_forget_bias",
                  "b_w_out"), None)
    landed[0] = _xfer_wait(exchanges[0], [done], False, "grads0_wait")
    adamw(("ffn1_w_gate_up", "ffn1_w_down", "ln_gain", "ln_bias"), None)
    return (loss, grad_x[None], *[outs[n][0] for n in WEIGHTS], *[outs[n][1] for n in WEIGHTS],
            *[outs[n][2] for n in WEIGHTS], *[outs[n][3] for n in WEIGHTS])
```

```python
import functools

import jax
import jax.numpy as jnp
from jax import lax
from jax.experimental import pallas as pl
from jax.experimental.pallas import tpu as pltpu

F32 = jnp.float32
BF = jnp.bfloat16
MESH_ID = pl.DeviceIdType.MESH

N_DEV = 8
DEPTH = 2
HEAD_DIM = 64
LANES = 128
N_MIX_HEADS = 12
N_MEM_HEADS = 4
MIX_W = N_MIX_HEADS * HEAD_DIM
MEM_W = N_MEM_HEADS * HEAD_DIM
N_GROUPS = 3
GROUP_W = MIX_W // N_GROUPS
BLOCK = 128
BAND_SUB = 4
ROT_HALF = 8
ROPE_THETA = 500000.0
ALPHA = (2 * DEPTH) ** 0.25
LN_EPS = 1e-5
SCALE = HEAD_DIM ** -0.5
NEG = -1e30
B_IN_PAD = 2688
ADAM_LR, ADAM_B1, ADAM_B2, ADAM_EPS, ADAM_WD, ADAM_STEP = 0.001, 0.9, 0.999, 1e-08, 0.01, 10
VMEM_LIMIT = 56 * 1024 * 1024


def _cp(*sem):
    return pltpu.CompilerParams(dimension_semantics=sem, vmem_limit_bytes=VMEM_LIMIT)


def _dot(a, b):
    return jnp.dot(a, b, preferred_element_type=F32)


def _dot_nt(a, b):
    return lax.dot_general(a, b, (((1,), (1,)), ((), ())), preferred_element_type=F32)


def _dot_tn(a, b):
    return lax.dot_general(a, b, (((0,), (0,)), ((), ())), preferred_element_type=F32)


def _sigmoid(x):
    return 1.0 / (1.0 + jnp.exp(-x))


def _tile(n, cap=1024):
    if n <= cap:
        return n
    best = LANES
    for t in range(LANES, cap + 1, LANES):
        if n % t == 0:
            best = t
    return best


def _rows(s, cap=512):
    return s if s <= cap else cap


def _mm_nn(a, b, out_dtype, name, b_rows_out=False):
    m, k = a.shape
    n = b.shape[0] if b_rows_out else b.shape[1]
    tm, tn = _rows(m), _tile(n)

    def body(a_ref, b_ref, o_ref):
        prod = _dot_nt(a_ref[...], b_ref[...]) if b_rows_out else _dot(a_ref[...], b_ref[...])
        o_ref[...] = prod.astype(o_ref.dtype)

    b_spec = (pl.BlockSpec((tn, k), lambda j, i: (j, 0)) if b_rows_out
              else pl.BlockSpec((k, tn), lambda j, i: (0, j)))
    return pl.pallas_call(
        body, name=name, grid=(n // tn, m // tm),
        in_specs=[pl.BlockSpec((tm, k), lambda j, i: (i, 0)), b_spec],
        out_specs=pl.BlockSpec((tm, tn), lambda j, i: (i, j)),
        out_shape=jax.ShapeDtypeStruct((m, n), out_dtype),
        compiler_params=_cp("parallel", "parallel"))(a, b)


def _resident(shape, index_map):
    return pl.BlockSpec(shape, index_map, pipeline_mode=pl.Buffered(1))


def _mm_tn(a, b, name, out_dtype=BF):
    na, s, m = a.shape
    nb, _, n = b.shape
    no = max(na, nb)
    tm, tn = _tile(m), _tile(n)

    def body(a_ref, b_ref, o_ref):
        o_ref[...] = _dot_tn(a_ref[...], b_ref[...]).astype(o_ref.dtype)

    def spec(nbatch, width, tile, index_map):
        fixed = nbatch == 1 and width == tile
        return _resident((None, s, tile), index_map) if fixed else pl.BlockSpec((None, s, tile), index_map)

    return pl.pallas_call(
        body, name=name, grid=(no, m // tm, n // tn),
        in_specs=[spec(na, m, tm, lambda j, r, c: (j if na > 1 else 0, 0, r)),
                  spec(nb, n, tn, lambda j, r, c: (j if nb > 1 else 0, 0, c))],
        out_specs=pl.BlockSpec((None, tm, tn), lambda j, r, c: (j, r, c)),
        out_shape=jax.ShapeDtypeStruct((no, m, n), out_dtype),
        compiler_params=_cp("parallel", "parallel", "parallel"))(a, b)


def _mm_nt(dh, w, name, res=None, out_dtype=F32, w_rows_out=True):
    nc, s, kc = dh.shape
    d = w.shape[1] if w_rows_out else w.shape[2]
    ts = _rows(s, 256 if nc > 1 else 512)
    has_res = res is not None
    mm = _dot_nt if w_rows_out else _dot

    def body(*refs):
        if has_res:
            dh_ref, w_ref, r_ref, o_ref = refs
        else:
            dh_ref, w_ref, o_ref = refs
        out = mm(dh_ref[0], w_ref[0])
        for j in range(1, nc):
            out = out + mm(dh_ref[j], w_ref[j])
        if has_res:
            out = out + ALPHA * r_ref[...]
        o_ref[...] = out.astype(o_ref.dtype)

    in_specs = [pl.BlockSpec((nc, ts, kc), lambda i: (0, i, 0)), _resident(w.shape, lambda i: (0, 0, 0))]
    args = [dh, w]
    if has_res:
        in_specs.append(pl.BlockSpec((ts, d), lambda i: (i, 0)))
        args.append(res)
    return pl.pallas_call(
        body, name=name, grid=(s // ts,), in_specs=in_specs,
        out_specs=pl.BlockSpec((ts, d), lambda i: (i, 0)),
        out_shape=jax.ShapeDtypeStruct((s, d), out_dtype),
        compiler_params=_cp("parallel"))(*args)


def _mm_res_ln(a, w, x, gain, bias, fscale, name):
    nc, s, kc = a.shape
    d = w.shape[2]
    ts = _rows(s, 256 if nc > 1 else 512)

    def body(a_ref, w_ref, x_ref, g_ref, b_ref, y_ref, yb_ref, xh_ref, r_ref):
        f = _dot(a_ref[0], w_ref[0])
        for j in range(1, nc):
            f = f + _dot(a_ref[j], w_ref[j])
        z = ALPHA * x_ref[...] + fscale * f
        mu = jnp.mean(z, axis=-1, keepdims=True)
        zc = z - mu
        var = jnp.mean(zc * zc, axis=-1, keepdims=True)
        r = lax.rsqrt(var + LN_EPS)
        xh = zc * r
        y = xh * g_ref[...] + b_ref[...]
        y_ref[...] = y
        yb_ref[...] = y.astype(BF)
        xh_ref[...] = xh
        r_ref[...] = r

    row = pl.BlockSpec((ts, d), lambda i: (i, 0))
    vec = pl.BlockSpec((1, d), lambda i: (0, 0))
    return pl.pallas_call(
        body, name=name, grid=(s // ts,),
        in_specs=[pl.BlockSpec((nc, ts, kc), lambda i: (0, i, 0)), _resident((nc, kc, d), lambda i: (0, 0, 0)),
                  row, vec, vec],
        out_specs=[row, row, row, pl.BlockSpec((ts, 1), lambda i: (i, 0))],
        out_shape=[jax.ShapeDtypeStruct((s, d), F32), jax.ShapeDtypeStruct((s, d), BF),
                   jax.ShapeDtypeStruct((s, d), F32), jax.ShapeDtypeStruct((s, 1), F32)],
        compiler_params=_cp("parallel"))(a, w, x, gain, bias)


def _ln_bwd(dy, xh, rstd, gain, fscale, name, after=()):
    s, d = dy.shape
    ts = _rows(s)
    na = len(after)

    def body(*refs):
        dy_ref, xh_ref, r_ref, g_ref = refs[:4]
        dz_ref, dzb_ref, dg_ref, db_ref = refs[4 + na:]
        i = pl.program_id(0)
        dyv = dy_ref[...]
        xhv = xh_ref[...]
        dxh = dyv * g_ref[...]
        m1 = jnp.mean(dxh, axis=-1, keepdims=True)
        m2 = jnp.mean(dxh * xhv, axis=-1, keepdims=True)
        dz = r_ref[...] * (dxh - m1 - xhv * m2)
        dz_ref[...] = dz
        dzb_ref[...] = (fscale * dz).astype(BF)

        @pl.when(i == 0)
        def _():
            dg_ref[...] = jnp.zeros_like(dg_ref)
            db_ref[...] = jnp.zeros_like(db_ref)

        dg_ref[...] += jnp.sum(dyv * xhv, axis=0, keepdims=True)
        db_ref[...] += jnp.sum(dyv, axis=0, keepdims=True)

    row = pl.BlockSpec((ts, d), lambda i: (i, 0))
    vec = pl.BlockSpec((1, d), lambda i: (0, 0))
    return pl.pallas_call(
        body, name=name, grid=(s // ts,),
        in_specs=[row, row, pl.BlockSpec((ts, 1), lambda i: (i, 0)), vec] + [pl.BlockSpec(memory_space=pl.ANY)] * na,
        out_specs=[row, row, vec, vec],
        out_shape=[jax.ShapeDtypeStruct((s, d), F32), jax.ShapeDtypeStruct((s, d), BF),
                   jax.ShapeDtypeStruct((1, d), F32), jax.ShapeDtypeStruct((1, d), F32)],
        compiler_params=_cp("arbitrary"))(dy, xh, rstd, gain, *after)


def _ffn_up(xb, wgu, name):
    s, d = xb.shape
    c = wgu.shape[1]
    nch = wgu.shape[0] // 2
    ts = _rows(s)
    w4 = wgu.reshape(2, nch, c, d)

    def body(x_ref, w_ref, gu_ref, a_ref):
        x = x_ref[...]
        g = _dot_nt(x, w_ref[0])
        u = _dot_nt(x, w_ref[1])
        gu_ref[0] = g.astype(BF)
        gu_ref[1] = u.astype(BF)
        a_ref[...] = (g * _sigmoid(g) * u).astype(BF)

    return pl.pallas_call(
        body, name=name, grid=(nch, s // ts),
        in_specs=[pl.BlockSpec((ts, d), lambda j, i: (i, 0)),
                  pl.BlockSpec((2, None, c, d), lambda j, i: (0, j, 0, 0))],
        out_specs=[pl.BlockSpec((2, None, ts, c), lambda j, i: (0, j, i, 0)),
                   pl.BlockSpec((None, ts, c), lambda j, i: (j, i, 0))],
        out_shape=[jax.ShapeDtypeStruct((2, nch, s, c), BF), jax.ShapeDtypeStruct((nch, s, c), BF)],
        compiler_params=_cp("parallel", "parallel"))(xb, w4)


def _ffn_bwd_act(dzb, wd4, gu, name):
    s, d = dzb.shape
    nch, c = wd4.shape[0], wd4.shape[1]
    ts = _rows(s)

    def body(dz_ref, w_ref, gu_ref, dh_ref):
        da = _dot_nt(dz_ref[...], w_ref[...])
        g = gu_ref[0].astype(F32)
        u = gu_ref[1].astype(F32)
        sg = _sigmoid(g)
        dh_ref[0] = (da * u * sg * (1.0 + g * (1.0 - sg))).astype(BF)
        dh_ref[1] = (da * g * sg).astype(BF)

    return pl.pallas_call(
        body, name=name, grid=(nch, s // ts),
        in_specs=[pl.BlockSpec((ts, d), lambda j, i: (i, 0)),
                  pl.BlockSpec((None, c, d), lambda j, i: (j, 0, 0)),
                  pl.BlockSpec((2, None, ts, c), lambda j, i: (0, j, i, 0))],
        out_specs=pl.BlockSpec((2, None, ts, c), lambda j, i: (0, j, i, 0)),
        out_shape=jax.ShapeDtypeStruct((2, nch, s, c), BF),
        compiler_params=_cp("parallel", "parallel"))(dzb, wd4, gu)


def _rope_tables(s, sign):
    pos = jnp.arange(s, dtype=F32)
    inv_freq = 1.0 / (ROPE_THETA ** (jnp.arange(ROT_HALF, dtype=F32) / ROT_HALF))
    ang = pos[:, None] * inv_freq[None, :]
    cos, sin = jnp.cos(ang), jnp.sin(ang) * sign
    one = jnp.ones((s, HEAD_DIM - 2 * ROT_HALF), F32)
    zero = jnp.zeros((s, HEAD_DIM - 2 * ROT_HALF), F32)
    zh = jnp.zeros((s, ROT_HALF), F32)
    cos_f = jnp.concatenate([cos, cos, one], axis=1)
    sin_a = jnp.concatenate([-sin, zh, zero], axis=1)
    sin_b = jnp.concatenate([zh, sin, zero], axis=1)
    rep = LANES // HEAD_DIM
    return tuple(jnp.tile(t, (1, rep)) for t in (cos_f, sin_a, sin_b))


def _rope_cast(parts, tabs, n_rope, name, transposed=()):
    s = tabs[0].shape[0]
    flip = [i in transposed for i in range(len(parts))]
    widths = [p.shape[0] if f else p.shape[1] for p, f in zip(parts, flip)]
    n = sum(widths)
    npart = len(parts)
    ts = _rows(s, 256)

    def body(*refs):
        part_refs = refs[:npart]
        c_ref, sa_ref, sb_ref, o_ref = refs[npart:]
        col = 0
        for ref, w, f in zip(part_refs, widths, flip):
            for j in range(w // LANES):
                if f:
                    t = jnp.transpose(ref[j * LANES:(j + 1) * LANES, :])
                else:
                    t = ref[:, j * LANES:(j + 1) * LANES]
                if col < n_rope:
                    t = (t * c_ref[...] + pltpu.roll(t, LANES - ROT_HALF, 1) * sa_ref[...]
                         + pltpu.roll(t, ROT_HALF, 1) * sb_ref[...])
                o_ref[:, col * LANES:(col + 1) * LANES] = t.astype(BF)
                col += 1

    tab = pl.BlockSpec((ts, LANES), lambda i: (i, 0))
    return pl.pallas_call(
        body, name=name, grid=(s // ts,),
        in_specs=[pl.BlockSpec((w, ts), lambda i: (0, i)) if f else pl.BlockSpec((ts, w), lambda i: (i, 0))
                  for w, f in zip(widths, flip)] + [tab, tab, tab],
        out_specs=pl.BlockSpec((ts, n), lambda i: (i, 0)),
        out_shape=jax.ShapeDtypeStruct((s, n), BF),
        compiler_params=_cp("parallel"))(*parts, *tabs)


def _head_masks():
    lane = lax.broadcasted_iota(jnp.int32, (1, LANES), 1)
    return [lane < HEAD_DIM, lane >= HEAD_DIM]


def _sel(mask, v):
    return jnp.where(mask, v, jnp.zeros_like(v))


def _pick(mask, wide, fill):
    return jnp.max(jnp.where(mask, wide, fill), axis=1, keepdims=True)


def _band_masks(has_other, prev):
    qi = lax.broadcasted_iota(jnp.int32, (BLOCK, BLOCK), 0)
    kj = lax.broadcasted_iota(jnp.int32, (BLOCK, BLOCK), 1)
    if prev:
        return kj >= qi + jnp.where(has_other, 0, BLOCK)
    return kj <= qi


def _band_fwd(q3, k3, v3, name):
    ng, s, w = q3.shape
    nb = s // BLOCK
    npair = w // LANES
    nsub = BAND_SUB
    tile = nsub * BLOCK

    def body(q_ref, kc_ref, kp_ref, vc_ref, vp_ref, o_ref, l_ref):
        g = pl.program_id(0)
        t = pl.program_id(2)
        nbl = jnp.right_shift(nb, 2 * g)
        mc = _band_masks(None, False)
        hm = _head_masks()
        for i in range(nsub):
            rows = slice(i * BLOCK, (i + 1) * BLOCK)
            has_prev = jnp.bitwise_and(t * nsub + i, nbl - 1) != 0
            mp = _band_masks(has_prev, True)
            q, kc, vc = q_ref[rows, :], kc_ref[rows, :], vc_ref[rows, :]
            if i == 0:
                kp, vp = kp_ref[...], vp_ref[...]
            else:
                prev = slice((i - 1) * BLOCK, i * BLOCK)
                kp, vp = kc_ref[prev, :], vc_ref[prev, :]
            o = jnp.zeros((BLOCK, LANES), F32)
            lse_w = jnp.zeros((BLOCK, LANES), F32)
            for h in range(2):
                qh = _sel(hm[h], q)
                sc = jnp.where(mc, _dot_nt(qh, kc) * SCALE, NEG)
                sp = jnp.where(mp, _dot_nt(qh, kp) * SCALE, NEG)
                m = jnp.maximum(jnp.max(sc, axis=1, keepdims=True), jnp.max(sp, axis=1, keepdims=True))
                pc = jnp.exp(sc - m)
                pp = jnp.exp(sp - m)
                l = jnp.sum(pc, axis=1, keepdims=True) + jnp.sum(pp, axis=1, keepdims=True)
                oh = _dot(pc.astype(BF), _sel(hm[h], vc)) + _dot(pp.astype(BF), _sel(hm[h], vp))
                o = o + oh / l
                lse_w = jnp.where(hm[h], m + jnp.log(l), lse_w)
            o_ref[rows, :] = o
            l_ref[rows, :] = lse_w

    cur = pl.BlockSpec((None, tile, LANES), lambda g, p, t: (g, t, p))
    prv = pl.BlockSpec((None, BLOCK, LANES), lambda g, p, t: (g, jnp.maximum(t * nsub - 1, 0), p))
    return pl.pallas_call(
        body, name=name, grid=(ng, npair, nb // nsub),
        in_specs=[cur, cur, prv, cur, prv], out_specs=[cur, cur],
        out_shape=[jax.ShapeDtypeStruct((ng, s, w), F32), jax.ShapeDtypeStruct((ng, s, w), F32)],
        compiler_params=_cp("parallel", "parallel", "parallel"))(q3, k3, k3, v3, v3)


def _band_combine(o3, l3, name):
    ng, s, w = o3.shape
    ts = _rows(s)

    def body(o_ref, l_ref, oa_ref, lt_ref):
        ls = [l_ref[g] for g in range(ng)]
        m = functools.reduce(jnp.maximum, ls)
        es = [jnp.exp(l - m) for l in ls]
        den = functools.reduce(lambda a, b: a + b, es)
        num = functools.reduce(lambda a, b: a + b, [es[g] * o_ref[g] for g in range(ng)])
        oa_ref[...] = (num / den).astype(BF)
        lt_ref[...] = m + jnp.log(den)

    blk3 = pl.BlockSpec((ng, ts, w), lambda i: (0, i, 0))
    blk = pl.BlockSpec((ts, w), lambda i: (i, 0))
    return pl.pallas_call(
        body, name=name, grid=(s // ts,), in_specs=[blk3, blk3], out_specs=[blk, blk],
        out_shape=[jax.ShapeDtypeStruct((s, w), BF), jax.ShapeDtypeStruct((s, w), F32)],
        compiler_params=_cp("parallel"))(o3, l3)


def _band_bwd(q3, k3, v3, do3, oa3, lt3, name):
    ng, s, w = q3.shape
    nb = s // BLOCK
    npair = w // LANES
    nsub = BAND_SUB
    tile = nsub * BLOCK

    def body(q_ref, qn_ref, kc_ref, kp_ref, vc_ref, vp_ref, do_ref, don_ref, oa_ref, oan_ref, lt_ref, ltn_ref,
             dq_ref, dk_ref, dv_ref):
        g = pl.program_id(0)
        t = pl.program_id(2)
        nbl = jnp.right_shift(nb, 2 * g)
        mc = _band_masks(None, False)
        hm = _head_masks()

        def block(ref, edge_ref, i):
            if i < 0 or i >= nsub:
                return edge_ref[...]
            return ref[i * BLOCK:(i + 1) * BLOCK, :]

        for i in range(nsub):
            b = t * nsub + i
            mp = _band_masks(jnp.bitwise_and(b, nbl - 1) != 0, True)
            mn = _band_masks(jnp.bitwise_and(b + 1, nbl - 1) != 0, True)
            q, qn = block(q_ref, None, i), block(q_ref, qn_ref, i + 1)
            kc, kp = block(kc_ref, None, i), block(kc_ref, kp_ref, i - 1)
            vc, vp = block(vc_ref, None, i), block(vc_ref, vp_ref, i - 1)
            do, don = block(do_ref, None, i), block(do_ref, don_ref, i + 1)
            dd = do.astype(F32) * block(oa_ref, None, i).astype(F32)
            ddn = don.astype(F32) * block(oa_ref, oan_ref, i + 1).astype(F32)
            lt, ltn = block(lt_ref, None, i), block(lt_ref, ltn_ref, i + 1)
            dq = jnp.zeros((BLOCK, LANES), F32)
            dk = jnp.zeros((BLOCK, LANES), F32)
            dv = jnp.zeros((BLOCK, LANES), F32)
            for h in range(2):
                qh, doh = _sel(hm[h], q), _sel(hm[h], do)
                qnh, donh = _sel(hm[h], qn), _sel(hm[h], don)
                kch, kph = _sel(hm[h], kc), _sel(hm[h], kp)
                lse = _pick(hm[h], lt, NEG)
                lsen = _pick(hm[h], ltn, NEG)
                dsum = jnp.sum(_sel(hm[h], dd), axis=1, keepdims=True)
                dsumn = jnp.sum(_sel(hm[h], ddn), axis=1, keepdims=True)
                pc = jnp.exp(jnp.where(mc, _dot_nt(qh, kc) * SCALE, NEG) - lse)
                pp = jnp.exp(jnp.where(mp, _dot_nt(qh, kp) * SCALE, NEG) - lse)
                dsc = pc * (_dot_nt(doh, vc) - dsum)
                dsp = pp * (_dot_nt(doh, vp) - dsum)
                dq = dq + SCALE * (_dot(dsc.astype(BF), kch) + _dot(dsp.astype(BF), kph))
                pn = jnp.exp(jnp.where(mn, _dot_nt(qnh, kc) * SCALE, NEG) - lsen)
                dsn = pn * (_dot_nt(donh, vc) - dsumn)
                dk = dk + SCALE * (_dot_tn(dsc.astype(BF), qh) + _dot_tn(dsn.astype(BF), qnh))
                dv = dv + _dot_tn(pc.astype(BF), doh) + _dot_tn(pn.astype(BF), donh)
            rows = slice(i * BLOCK, (i + 1) * BLOCK)
            dq_ref[rows, :] = dq
            dk_ref[rows, :] = dk
            dv_ref[rows, :] = dv

    cur = pl.BlockSpec((None, tile, LANES), lambda g, p, t: (g, t, p))
    prv = pl.BlockSpec((None, BLOCK, LANES), lambda g, p, t: (g, jnp.maximum(t * nsub - 1, 0), p))
    nxt = pl.BlockSpec((None, BLOCK, LANES), lambda g, p, t: (g, jnp.minimum(t * nsub + nsub, nb - 1), p))
    out = jax.ShapeDtypeStruct((ng, s, w), F32)
    return pl.pallas_call(
        body, name=name, grid=(ng, npair, nb // nsub),
        in_specs=[cur, nxt, cur, prv, cur, prv, cur, nxt, cur, nxt, cur, nxt],
        out_specs=[cur, cur, cur], out_shape=[out, out, out],
        compiler_params=_cp("parallel", "parallel", "parallel"))(
            q3, q3, k3, k3, v3, v3, do3, do3, oa3, oa3, lt3, lt3)


def _mem_fwd(hb, q_blk0, kv, name):
    s = hb.shape[0]
    m = kv.shape[0]
    tq = _rows(s)
    npair = MEM_W // LANES

    def body(q_ref, k_ref, v_ref, o_ref, l_ref):
        q, k, v = q_ref[...], k_ref[...], v_ref[...]
        hm = _head_masks()
        o = jnp.zeros((tq, LANES), F32)
        lse_w = jnp.zeros((tq, LANES), F32)
        for h in range(2):
            sc = _dot_nt(_sel(hm[h], q), k) * SCALE
            mx = jnp.max(sc, axis=1, keepdims=True)
            p = jnp.exp(sc - mx)
            l = jnp.sum(p, axis=1, keepdims=True)
            o = o + _dot(p.astype(BF), _sel(hm[h], v)) / l
            lse_w = jnp.where(hm[h], mx + jnp.log(l), lse_w)
        o_ref[...] = o.astype(BF)
        l_ref[...] = lse_w

    blk = pl.BlockSpec((tq, LANES), lambda p, i: (i, p))
    return pl.pallas_call(
        body, name=name, grid=(npair, s // tq),
        in_specs=[pl.BlockSpec((tq, LANES), lambda p, i: (i, q_blk0 + p)),
                  pl.BlockSpec((m, LANES), lambda p, i: (0, p)),
                  pl.BlockSpec((m, LANES), lambda p, i: (0, npair + p))],
        out_specs=[blk, blk],
        out_shape=[jax.ShapeDtypeStruct((s, MEM_W), BF), jax.ShapeDtypeStruct((s, MEM_W), F32)],
        compiler_params=_cp("parallel", "parallel"))(hb, kv, kv)


def _mem_bwd(hb, q_blk0, kv, dcat, cat, o_blk0, lse, name):
    s = hb.shape[0]
    m = kv.shape[0]
    tq = _rows(s)
    npair = MEM_W // LANES

    def body(q_ref, k_ref, v_ref, do_ref, o_ref, l_ref, dq_ref, dk_ref, dv_ref):
        i = pl.program_id(1)

        @pl.when(i == 0)
        def _():
            dk_ref[...] = jnp.zeros_like(dk_ref)
            dv_ref[...] = jnp.zeros_like(dv_ref)

        q, k, v, do = q_ref[...], k_ref[...], v_ref[...], do_ref[...]
        dd = do.astype(F32) * o_ref[...].astype(F32)
        lt = l_ref[...]
        hm = _head_masks()
        dq = jnp.zeros((tq, LANES), F32)
        dk = jnp.zeros((m, LANES), F32)
        dv = jnp.zeros((m, LANES), F32)
        for h in range(2):
            qh, doh = _sel(hm[h], q), _sel(hm[h], do)
            p = jnp.exp(_dot_nt(qh, k) * SCALE - _pick(hm[h], lt, NEG))
            ds = p * (_dot_nt(doh, v) - jnp.sum(_sel(hm[h], dd), axis=1, keepdims=True))
            dq = dq + SCALE * _dot(ds.astype(BF), _sel(hm[h], k))
            dk = dk + SCALE * _dot_tn(ds.astype(BF), qh)
            dv = dv + _dot_tn(p.astype(BF), doh)
        dq_ref[...] = dq
        dk_ref[...] += dk
        dv_ref[...] += dv

    row = pl.BlockSpec((tq, LANES), lambda p, i: (i, p))
    orow = pl.BlockSpec((tq, LANES), lambda p, i: (i, o_blk0 + p))
    acc = pl.BlockSpec((m, LANES), lambda p, i: (0, p))
    return pl.pallas_call(
        body, name=name, grid=(npair, s // tq),
        in_specs=[pl.BlockSpec((tq, LANES), lambda p, i: (i, q_blk0 + p)),
                  pl.BlockSpec((m, LANES), lambda p, i: (0, p)),
                  pl.BlockSpec((m, LANES), lambda p, i: (0, npair + p)), orow, orow, row],
        out_specs=[row, acc, acc],
        out_shape=[jax.ShapeDtypeStruct((s, MEM_W), F32), jax.ShapeDtypeStruct((m, MEM_W), F32),
                   jax.ShapeDtypeStruct((m, MEM_W), F32)],
        compiler_params=_cp("parallel", "arbitrary"))(hb, kv, kv, dcat, cat, lse)


def _gate_fwd(f_t, bias, name):
    hp, s = f_t.shape
    nblk = s // LANES

    def body(f_ref, b_ref, c_ref):
        lane = lax.broadcasted_iota(jnp.int32, (hp, LANES), 1)

        def step(i, carry):
            off = pl.multiple_of(i * LANES, LANES)
            x = f_ref[:, pl.ds(off, LANES)] + b_ref[...]
            acc = jnp.minimum(x, 0.0) - jnp.log(1.0 + jnp.exp(-jnp.abs(x)))
            sh = 1
            while sh < LANES:
                acc = acc + jnp.where(lane >= sh, pltpu.roll(acc, sh, 1), 0.0)
                sh *= 2
            acc = acc + carry
            c_ref[:, pl.ds(off, LANES)] = acc
            return acc[:, LANES - 1:LANES]

        lax.fori_loop(0, nblk, step, jnp.zeros((hp, 1), F32))

    vm = pl.BlockSpec(memory_space=pltpu.VMEM)
    return pl.pallas_call(body, name=name, in_specs=[vm, vm], out_specs=vm,
                          out_shape=jax.ShapeDtypeStruct((hp, s), F32),
                          compiler_params=pltpu.CompilerParams(vmem_limit_bytes=VMEM_LIMIT))(f_t, bias)


def _gate_bwd(dc_t, f_t, bias, name):
    hp, s = f_t.shape
    nblk = s // LANES

    def body(dc_ref, f_ref, b_ref, df_ref, db_ref):
        lane = lax.broadcasted_iota(jnp.int32, (hp, LANES), 1)

        def step(t, carry):
            suffix, dbias = carry
            off = pl.multiple_of((nblk - 1 - t) * LANES, LANES)
            acc = dc_ref[:, pl.ds(off, LANES)]
            sh = 1
            while sh < LANES:
                acc = acc + jnp.where(lane < LANES - sh, pltpu.roll(acc, LANES - sh, 1), 0.0)
                sh *= 2
            acc = acc + suffix
            x = f_ref[:, pl.ds(off, LANES)] + b_ref[...]
            df = acc * _sigmoid(-x)
            df_ref[:, pl.ds(off, LANES)] = df
            return acc[:, 0:1], dbias + jnp.sum(df, axis=1, keepdims=True)

        _, dbias = lax.fori_loop(0, nblk, step, (jnp.zeros((hp, 1), F32), jnp.zeros((hp, 1), F32)))
        db_ref[...] = dbias

    vm = pl.BlockSpec(memory_space=pltpu.VMEM)
    return pl.pallas_call(body, name=name, in_specs=[vm, vm, vm], out_specs=[vm, vm],
                          out_shape=[jax.ShapeDtypeStruct((hp, s), F32), jax.ShapeDtypeStruct((hp, 1), F32)],
                          compiler_params=pltpu.CompilerParams(vmem_limit_bytes=VMEM_LIMIT))(dc_t, f_t, bias)


def _wide(rep, width):
    return jnp.tile(rep, (1, width // LANES))


def _fold(t):
    part = t[:, :LANES]
    for c in range(1, t.shape[1] // LANES):
        part = part + t[:, c * LANES:(c + 1) * LANES]
    return part


def _fox_logits(q, k, cq_rep, ck_row, mask, hmask):
    s = _dot_nt(_sel(hmask, q), k) + (_wide(cq_rep, ck_row.shape[1]) - ck_row)
    if mask is not None:
        s = jnp.where(mask, s, NEG)
    return s


def _diag_mask(t):
    return lax.broadcasted_iota(jnp.int32, (t, t), 1) <= lax.broadcasted_iota(jnp.int32, (t, t), 0)


def _fox_fwd(hb, c_rep, c_t3, name):
    s = hb.shape[0]
    npair = MIX_W // LANES
    tq = tk = _rows(s)
    nq = s // tq

    def body(q_ref, k_ref, v_ref, cq_ref, ck_ref, o_ref, l_ref, m_s, l_s, acc):
        qi = pl.program_id(1)
        kj = pl.program_id(2)
        hm = _head_masks()

        @pl.when(kj == 0)
        def _():
            m_s[...] = jnp.full_like(m_s, NEG)
            l_s[...] = jnp.zeros_like(l_s)
            acc[...] = jnp.zeros_like(acc)

        def step(mask):
            q, k, v = q_ref[...] * SCALE, k_ref[...], v_ref[...]
            ck = ck_ref[...]
            for h in range(2):
                sc = _fox_logits(q, k, cq_ref[h], ck[h:h + 1, :], mask, hm[h])
                m_old = m_s[h]
                m_new = jnp.maximum(m_old, jnp.max(sc, axis=1, keepdims=True))
                pr = jnp.exp(sc - _wide(m_new, tk))
                corr = jnp.exp(m_old - m_new)
                l_s[h] = l_s[h] * corr + _fold(pr)
                acc[h] = acc[h] * corr + _dot(pr.astype(BF), _sel(hm[h], v))
                m_s[h] = m_new

        @pl.when(kj < qi)
        def _():
            step(None)

        @pl.when(kj == qi)
        def _():
            step(_diag_mask(tq))
            outs = []
            for h in range(2):
                den = jnp.sum(l_s[h], axis=1, keepdims=True)
                outs.append(acc[h] / den)
                l_ref[h] = m_s[h] + jnp.log(den)
            o_ref[...] = jnp.where(hm[0], outs[0], outs[1]).astype(BF)

    def kv_map(off):
        return lambda p, i, j: (jnp.minimum(j, i), off + p)

    blk = pl.BlockSpec((tq, LANES), lambda p, i, j: (i, p))
    return pl.pallas_call(
        body, name=name, grid=(npair, nq, nq),
        in_specs=[blk, pl.BlockSpec((tk, LANES), kv_map(npair)), pl.BlockSpec((tk, LANES), kv_map(2 * npair)),
                  pl.BlockSpec((2, tq, LANES), lambda p, i, j: (p, i, 0)),
                  pl.BlockSpec((None, 2, tk), lambda p, i, j: (p, 0, jnp.minimum(j, i)))],
        out_specs=[blk, pl.BlockSpec((2, tq, LANES), lambda p, i, j: (p, i, 0))],
        out_shape=[jax.ShapeDtypeStruct((s, MIX_W), BF), jax.ShapeDtypeStruct((2 * npair, s, LANES), F32)],
        scratch_shapes=[pltpu.VMEM((2, tq, LANES), F32), pltpu.VMEM((2, tq, LANES), F32),
                        pltpu.VMEM((2, tq, LANES), F32)],
        compiler_params=_cp("parallel", "parallel", "arbitrary"))(hb, hb, hb, c_rep, c_t3)


def _fox_dsum(hb, dcat, lse, c_rep, c_t3, name):
    s = hb.shape[0]
    npair = MIX_W // LANES
    tq = tk = _rows(s)
    nq = s // tq

    def body(q_ref, k_ref, v_ref, do_ref, l_ref, cq_ref, ck_ref, d_ref, acc):
        qi = pl.program_id(1)
        kj = pl.program_id(2)
        hm = _head_masks()

        @pl.when(kj == 0)
        def _():
            acc[...] = jnp.zeros_like(acc)

        def step(mask):
            q, k, v, do = q_ref[...] * SCALE, k_ref[...], v_ref[...], do_ref[...]
            ck = ck_ref[...]
            for h in range(2):
                pr = jnp.exp(_fox_logits(q, k, cq_ref[h], ck[h:h + 1, :], mask, hm[h]) - _wide(l_ref[h], tk))
                acc[h] += _fold(pr * _dot_nt(_sel(hm[h], do), v))

        @pl.when(kj < qi)
        def _():
            step(None)

        @pl.when(kj == qi)
        def _():
            step(_diag_mask(tq))
            for h in range(2):
                d_ref[h] = jnp.broadcast_to(jnp.sum(acc[h], axis=1, keepdims=True), (tq, LANES))

    def kv_map(off):
        return lambda p, i, j: (jnp.minimum(j, i), off + p)

    blk = pl.BlockSpec((tq, LANES), lambda p, i, j: (i, p))
    rep = pl.BlockSpec((2, tq, LANES), lambda p, i, j: (p, i, 0))
    return pl.pallas_call(
        body, name=name, grid=(npair, nq, nq),
        in_specs=[blk, pl.BlockSpec((tk, LANES), kv_map(npair)), pl.BlockSpec((tk, LANES), kv_map(2 * npair)),
                  blk, rep, rep, pl.BlockSpec((None, 2, tk), lambda p, i, j: (p, 0, jnp.minimum(j, i)))],
        out_specs=rep, out_shape=jax.ShapeDtypeStruct((2 * npair, s, LANES), F32),
        scratch_shapes=[pltpu.VMEM((2, tq, LANES), F32)],
        compiler_params=_cp("parallel", "parallel", "arbitrary"))(hb, hb, hb, dcat, lse, c_rep, c_t3)


def _fox_bwd(hb, dcat, dsum, lse, c_rep, c_t3, name):
    s = hb.shape[0]
    npair = MIX_W // LANES
    tq = tk = _rows(s)
    nq = s // tq

    def body(q_ref, k_ref, v_ref, do_ref, d_ref, l_ref, cq_ref, ck_ref, dq_ref, dk_ref, dv_ref, dc_ref):
        kj = pl.program_id(1)
        qi = pl.program_id(2)
        hm = _head_masks()

        @pl.when(qi == 0)
        def _():
            dk_ref[...] = jnp.zeros_like(dk_ref)
            dv_ref[...] = jnp.zeros_like(dv_ref)
            dc_ref[...] = jnp.zeros_like(dc_ref)

        @pl.when((qi == 0) & (kj == 0))
        def _():
            dq_ref[...] = jnp.zeros_like(dq_ref)

        def step(mask):
            q, k, v, do = q_ref[...] * SCALE, k_ref[...], v_ref[...], do_ref[...]
            ck = ck_ref[...]
            dq = jnp.zeros((tq, LANES), F32)
            dk = jnp.zeros((tk, LANES), F32)
            dv = jnp.zeros((tk, LANES), F32)
            dcs = []
            for h in range(2):
                qh, doh = _sel(hm[h], q), _sel(hm[h], do)
                pr = jnp.exp(_fox_logits(q, k, cq_ref[h], ck[h:h + 1, :], mask, hm[h]) - _wide(l_ref[h], tk))
                ds = pr * (_dot_nt(doh, v) - _wide(d_ref[h], tk))
                dsb = ds.astype(BF)
                dq = dq + _dot(dsb, _sel(hm[h], k))
                dk = dk + _dot_tn(dsb, qh)
                dv = dv + _dot_tn(pr.astype(BF), doh)
                dcs.append(jnp.sum(ds, axis=0, keepdims=True))
            rows = pl.ds(pl.multiple_of(qi * tq, tq), tq)
            dq_ref[rows, :] += SCALE * dq
            dk_ref[...] += dk
            dv_ref[...] += dv
            dc_ref[...] -= jnp.concatenate(dcs, axis=0)

        @pl.when(qi > kj)
        def _():
            step(None)

        @pl.when(qi == kj)
        def _():
            step(_diag_mask(tq))

    def q_map(p, j, i):
        return (jnp.maximum(i, j), p)

    kblk = pl.BlockSpec((tk, LANES), lambda p, j, i: (j, p))
    rep = pl.BlockSpec((2, tq, LANES), lambda p, j, i: (p, jnp.maximum(i, j), 0))
    return pl.pallas_call(
        body, name=name, grid=(npair, nq, nq),
        in_specs=[pl.BlockSpec((tq, LANES), q_map),
                  pl.BlockSpec((tk, LANES), lambda p, j, i: (j, npair + p)),
                  pl.BlockSpec((tk, LANES), lambda p, j, i: (j, 2 * npair + p)),
                  pl.BlockSpec((tq, LANES), q_map), rep, rep, rep,
                  pl.BlockSpec((None, 2, tk), lambda p, j, i: (p, 0, j))],
        out_specs=[pl.BlockSpec((s, LANES), lambda p, j, i: (0, p)), kblk, kblk,
                   pl.BlockSpec((None, 2, tk), lambda p, j, i: (p, 0, j))],
        out_shape=[jax.ShapeDtypeStruct((s, MIX_W), F32), jax.ShapeDtypeStruct((s, MIX_W), F32),
                   jax.ShapeDtypeStruct((s, MIX_W), F32), jax.ShapeDtypeStruct((npair, 2, s), F32)],
        compiler_params=_cp("arbitrary", "arbitrary", "arbitrary"))(hb, hb, hb, dcat, dsum, lse, c_rep, c_t3)


def _foxt_logits(q, k, cq_row, ck_rep, mask, hmask):
    s = _dot_nt(_sel(hmask, k), q) + (cq_row - _wide(ck_rep, q.shape[0]))
    if mask is not None:
        s = jnp.where(mask, s, NEG)
    return s


def _diag_mask_t(t):
    return lax.broadcasted_iota(jnp.int32, (t, t), 0) <= lax.broadcasted_iota(jnp.int32, (t, t), 1)


def _foxt_fwd(hb, c_rep, c_t3, name):
    s = hb.shape[0]
    npair = MIX_W // LANES
    tq = tk = _rows(s)
    nq = s // tq

    def body(q_ref, k_ref, v_ref, cq_ref, ck_ref, o_ref, l_ref, m_s, l_s, acc):
        qi = pl.program_id(1)
        kj = pl.program_id(2)
        hm = _head_masks()

        @pl.when(kj == 0)
        def _():
            m_s[...] = jnp.full_like(m_s, NEG)
            l_s[...] = jnp.zeros_like(l_s)
            acc[...] = jnp.zeros_like(acc)

        def step(mask):
            q, k = q_ref[...] * SCALE, k_ref[...]
            vt = jnp.transpose(v_ref[...])
            cq = cq_ref[...]
            for h in range(2):
                st = _foxt_logits(q, k, cq[h:h + 1, :], ck_ref[h], mask, hm[h])
                m_old = m_s[h]
                m_new = jnp.maximum(m_old, jnp.max(st, axis=0, keepdims=True))
                pt = jnp.exp(st - m_new)
                corr = jnp.exp(m_old - m_new)
                l_s[h] = l_s[h] * corr + jnp.sum(pt, axis=0, keepdims=True)
                acc[h] = acc[h] * corr + _dot(vt[h * HEAD_DIM:(h + 1) * HEAD_DIM, :], pt.astype(BF))
                m_s[h] = m_new

        @pl.when(kj < qi)
        def _():
            step(None)

        @pl.when(kj == qi)
        def _():
            step(_diag_mask_t(tq))
            outs = []
            for h in range(2):
                outs.append(acc[h] / l_s[h])
                l_ref[h:h + 1, :] = m_s[h] + jnp.log(l_s[h])
            o_ref[...] = jnp.transpose(jnp.concatenate(outs, axis=0)).astype(BF)

    def kv_map(off):
        return lambda p, i, j: (jnp.minimum(j, i), off + p)

    blk = pl.BlockSpec((tq, LANES), lambda p, i, j: (i, p))
    row = pl.BlockSpec((None, 2, tq), lambda p, i, j: (p, 0, i))
    return pl.pallas_call(
        body, name=name, grid=(npair, nq, nq),
        in_specs=[blk, pl.BlockSpec((tk, LANES), kv_map(npair)), pl.BlockSpec((tk, LANES), kv_map(2 * npair)), row,
                  pl.BlockSpec((2, tk, LANES), lambda p, i, j: (p, jnp.minimum(j, i), 0))],
        out_specs=[blk, row],
        out_shape=[jax.ShapeDtypeStruct((s, MIX_W), BF), jax.ShapeDtypeStruct((npair, 2, s), F32)],
        scratch_shapes=[pltpu.VMEM((2, 1, tq), F32), pltpu.VMEM((2, 1, tq), F32),
                        pltpu.VMEM((2, HEAD_DIM, tq), F32)],
        compiler_params=_cp("parallel", "parallel", "arbitrary"))(hb, hb, hb, c_t3, c_rep)


def _foxt_dsum(hb, dcat, lse, c_rep, c_t3, name):
    s = hb.shape[0]
    npair = MIX_W // LANES
    tq = tk = _rows(s)
    nq = s // tq

    def body(q_ref, k_ref, v_ref, do_ref, l_ref, cq_ref, ck_ref, d_ref, acc):
        qi = pl.program_id(1)
        kj = pl.program_id(2)
        hm = _head_masks()

        @pl.when(kj == 0)
        def _():
            acc[...] = jnp.zeros_like(acc)

        def step(mask):
            q, k, v, do = q_ref[...] * SCALE, k_ref[...], v_ref[...], do_ref[...]
            cq, lse_rows = cq_ref[...], l_ref[...]
            for h in range(2):
                pt = jnp.exp(_foxt_logits(q, k, cq[h:h + 1, :], ck_ref[h], mask, hm[h]) - lse_rows[h:h + 1, :])
                acc[h] += jnp.sum(pt * _dot_nt(_sel(hm[h], v), do), axis=0, keepdims=True)

        @pl.when(kj < qi)
        def _():
            step(None)

        @pl.when(kj == qi)
        def _():
            step(_diag_mask_t(tq))
            for h in range(2):
                d_ref[h:h + 1, :] = acc[h]

    def kv_map(off):
        return lambda p, i, j: (jnp.minimum(j, i), off + p)

    blk = pl.BlockSpec((tq, LANES), lambda p, i, j: (i, p))
    row = pl.BlockSpec((None, 2, tq), lambda p, i, j: (p, 0, i))
    return pl.pallas_call(
        body, name=name, grid=(npair, nq, nq),
        in_specs=[blk, pl.BlockSpec((tk, LANES), kv_map(npair)), pl.BlockSpec((tk, LANES), kv_map(2 * npair)),
                  blk, row, row, pl.BlockSpec((2, tk, LANES), lambda p, i, j: (p, jnp.minimum(j, i), 0))],
        out_specs=row, out_shape=jax.ShapeDtypeStruct((npair, 2, s), F32),
        scratch_shapes=[pltpu.VMEM((2, 1, tq), F32)],
        compiler_params=_cp("parallel", "parallel", "arbitrary"))(hb, hb, hb, dcat, lse, c_t3, c_rep)


def _foxt_bwd(hb, dcat, dsum, lse, c_rep, c_t3, name):
    s = hb.shape[0]
    npair = MIX_W // LANES
    tq = tk = _rows(s)
    nq = s // tq

    def body(q_ref, k_ref, v_ref, do_ref, d_ref, l_ref, cq_ref, ck_ref, dq_ref, dk_ref, dv_ref, dc_ref, dc_s):
        kj = pl.program_id(1)
        qi = pl.program_id(2)
        hm = _head_masks()

        @pl.when(qi == 0)
        def _():
            dk_ref[...] = jnp.zeros_like(dk_ref)
            dv_ref[...] = jnp.zeros_like(dv_ref)
            dc_s[...] = jnp.zeros_like(dc_s)

        @pl.when((qi == 0) & (kj == 0))
        def _():
            dq_ref[...] = jnp.zeros_like(dq_ref)

        def step(mask):
            q, k, v, do = q_ref[...] * SCALE, k_ref[...], v_ref[...], do_ref[...]
            qt, kt, dot = jnp.transpose(q), jnp.transpose(k), jnp.transpose(do)
            cq, lse_rows, d_rows = cq_ref[...], l_ref[...], d_ref[...]
            dqs, dks, dvs = [], [], []
            for h in range(2):
                rows = slice(h * HEAD_DIM, (h + 1) * HEAD_DIM)
                pt = jnp.exp(_foxt_logits(q, k, cq[h:h + 1, :], ck_ref[h], mask, hm[h]) - lse_rows[h:h + 1, :])
                dst = pt * (_dot_nt(_sel(hm[h], v), do) - d_rows[h:h + 1, :])
                dsb = dst.astype(BF)
                dqs.append(_dot(kt[rows, :], dsb))
                dks.append(_dot_nt(qt[rows, :], dsb))
                dvs.append(_dot_nt(dot[rows, :], pt.astype(BF)))
                dc_s[h] += _fold(dst)
            cols = pl.ds(pl.multiple_of(qi * tq, tq), tq)
            dq_ref[:, cols] += SCALE * jnp.concatenate(dqs, axis=0)
            dk_ref[...] += jnp.concatenate(dks, axis=0)
            dv_ref[...] += jnp.concatenate(dvs, axis=0)

        @pl.when(qi > kj)
        def _():
            step(None)

        @pl.when(qi == kj)
        def _():
            step(_diag_mask_t(tq))

        @pl.when(qi == nq - 1)
        def _():
            for h in range(2):
                dc_ref[h:h + 1, :] = -jnp.sum(jnp.transpose(dc_s[h]), axis=0, keepdims=True)

    def q_map(p, j, i):
        return (jnp.maximum(i, j), p)

    row = pl.BlockSpec((None, 2, tq), lambda p, j, i: (p, 0, jnp.maximum(i, j)))
    kblk = pl.BlockSpec((LANES, tk), lambda p, j, i: (p, j))
    rep = pl.BlockSpec((2, tk, LANES), lambda p, j, i: (p, j, 0))
    return pl.pallas_call(
        body, name=name, grid=(npair, nq, nq),
        in_specs=[pl.BlockSpec((tq, LANES), q_map),
                  pl.BlockSpec((tk, LANES), lambda p, j, i: (j, npair + p)),
                  pl.BlockSpec((tk, LANES), lambda p, j, i: (j, 2 * npair + p)),
                  pl.BlockSpec((tq, LANES), q_map), row, row, row, rep],
        out_specs=[pl.BlockSpec((LANES, s), lambda p, j, i: (p, 0)), kblk, kblk,
                   pl.BlockSpec((None, 2, tk), lambda p, j, i: (p, 0, j))],
        out_shape=[jax.ShapeDtypeStruct((MIX_W, s), F32), jax.ShapeDtypeStruct((MIX_W, s), F32),
                   jax.ShapeDtypeStruct((MIX_W, s), F32), jax.ShapeDtypeStruct((npair, 2, s), F32)],
        scratch_shapes=[pltpu.VMEM((2, tk, LANES), F32)],
        compiler_params=_cp("arbitrary", "arbitrary", "arbitrary"))(hb, hb, hb, dcat, dsum, lse, c_t3, c_rep)


def _loss_head(y, target, name):
    s, d = y.shape
    ts = _rows(s)

    def body(y_ref, t_ref, dy_ref, l_ref):
        i = pl.program_id(0)
        e = y_ref[...] - t_ref[...]
        dy_ref[...] = e * (1.0 / d)

        @pl.when(i == 0)
        def _():
            l_ref[...] = jnp.zeros_like(l_ref)

        part = jnp.sum(jnp.sum(e * e, axis=1, keepdims=True), axis=0, keepdims=True)
        l_ref[...] += part * (0.5 / d)

    row = pl.BlockSpec((ts, d), lambda i: (i, 0))
    return pl.pallas_call(
        body, name=name, grid=(s // ts,), in_specs=[row, row],
        out_specs=[row, pl.BlockSpec((1, 1), lambda i: (0, 0))],
        out_shape=[jax.ShapeDtypeStruct((s, d), F32), jax.ShapeDtypeStruct((1, 1), F32)],
        compiler_params=_cp("arbitrary"))(y, target)


def _adam_rows(r, c):
    cap = max(8, (1 << 20) // (4 * c))
    if r <= cap:
        return r
    best = None
    for t in range(8, cap + 1, 8):
        if r % t == 0:
            best = t
    return best if best is not None else r


def _reduce_adamw(contribs, w, m, v, name):
    nl = len(contribs)
    nd, r, c = contribs[0].shape
    tr = _adam_rows(r, c)
    bc1 = 1.0 - ADAM_B1 ** ADAM_STEP
    bc2 = 1.0 - ADAM_B2 ** ADAM_STEP

    def body(*refs):
        c_refs = refs[:nl]
        w_ref, m_ref, v_ref, g_ref, d_ref, nm_ref, nv_ref = refs[nl:]
        l = pl.program_id(0)
        for li in range(nl):
            @pl.when(l == li)
            def _(c_ref=c_refs[li]):
                g = c_ref[0].astype(F32)
                for k in range(1, nd):
                    g = g + c_ref[k].astype(F32)
                nm = ADAM_B1 * m_ref[...] + (1.0 - ADAM_B1) * g
                nv = ADAM_B2 * v_ref[...] + (1.0 - ADAM_B2) * (g * g)
                g_ref[...] = g
                nm_ref[...] = nm
                nv_ref[...] = nv
                d_ref[...] = -ADAM_LR * ((nm / bc1) / (jnp.sqrt(nv / bc2) + ADAM_EPS) + ADAM_WD * w_ref[...])

    def c_spec(li):
        return pl.BlockSpec((nd, tr, c), lambda l, i: (0, jnp.where(l == li, i, 0), 0))

    blk = pl.BlockSpec((None, tr, c), lambda l, i: (l, i, 0))
    out = jax.ShapeDtypeStruct((nl, r, c), F32)
    return pl.pallas_call(
        body, name=name, grid=(nl, r // tr),
        in_specs=[c_spec(li) for li in range(nl)] + [blk, blk, blk],
        out_specs=[blk, blk, blk, blk], out_shape=[out, out, out, out],
        compiler_params=_cp("arbitrary", "arbitrary"))(*contribs, w, m, v)


def _mesh_pos():
    return lax.axis_index("x"), lax.axis_index("y"), lax.axis_index("c")


def _peer(pos, k):
    x, y, c = pos
    return (1 - x if k & 4 else x, 1 - y if k & 2 else y, 1 - c if k & 1 else c)


def _linear(pos):
    return 4 * pos[0] + 2 * pos[1] + pos[2]


def _xfer_copies(srcs, lands, send_sems, recv_sems, local_sems, gather):
    pos = _mesh_pos()
    me = _linear(pos)
    local, remote = [], []
    for i, (src, land) in enumerate(zip(srcs, lands)):
        local.append(pltpu.make_async_copy(src if gather else src.at[me], land.at[me], local_sems.at[i]))
        for k in range(1, N_DEV):
            peer = _peer(pos, k)
            remote.append(pltpu.make_async_remote_copy(
                src_ref=src if gather else src.at[_linear(peer)], dst_ref=land.at[me],
                send_sem=send_sems.at[i * (N_DEV - 1) + k - 1], recv_sem=recv_sems.at[i * (N_DEV - 1) + k - 1],
                device_id=peer, device_id_type=MESH_ID))
    return local, remote


_HBM = pl.BlockSpec(memory_space=pltpu.HBM)
_SEM = pl.BlockSpec(memory_space=pltpu.SEMAPHORE)
_EFFECT = pltpu.SideEffectType.DATAFLOW_SIDE_EFFECTING


def _xfer_start(srcs, gather, name, after=()):
    n = len(srcs)
    na = len(after)
    lands = [lax.empty(((N_DEV,) + a.shape) if gather else a.shape, a.dtype) for a in srcs]

    def body(*refs):
        src, land = refs[:n], refs[n:2 * n]
        send_sems, recv_sems, local_sems = refs[2 * n + na:2 * n + na + 3]
        local, remote = _xfer_copies(src, land, send_sems, recv_sems, local_sems, gather)
        for cp in local + remote:
            cp.start()
        refs[-1][...] = jnp.zeros_like(refs[-1])

    nsem = n * (N_DEV - 1)
    out = pl.pallas_call(
        body, name=name,
        out_shape=(pltpu.SemaphoreType.DMA((nsem,)), pltpu.SemaphoreType.DMA((nsem,)), pltpu.SemaphoreType.DMA((n,)),
                   *[pltpu.HBM(a.shape, a.dtype) for a in srcs], *[pltpu.HBM(a.shape, a.dtype) for a in lands],
                   jax.ShapeDtypeStruct((8, LANES), F32)),
        in_specs=[_HBM] * (2 * n) + [pl.BlockSpec(memory_space=pl.ANY)] * na,
        out_specs=(_SEM, _SEM, _SEM, *[_HBM] * (2 * n), pl.BlockSpec(memory_space=pltpu.VMEM)),
        input_output_aliases={i: 3 + i for i in range(2 * n)},
        compiler_params=pltpu.CompilerParams(has_side_effects=_EFFECT))(
            *[pltpu.with_memory_space_constraint(a, pltpu.HBM) for a in srcs],
            *[pltpu.with_memory_space_constraint(a, pltpu.HBM) for a in lands], *after)
    return out[:3], list(out[3:3 + n]), list(out[3 + n:3 + 2 * n]), out[-1]


def _started(handle):
    return handle[3]


def _xfer_wait(handle, after, gather, name):
    sems, srcs, lands, _ = handle
    n = len(srcs)

    def body(*refs):
        src, land = refs[:n], refs[n:2 * n]
        send_sems, recv_sems, local_sems = refs[2 * n:2 * n + 3]
        local, remote = _xfer_copies(src, land, send_sems, recv_sems, local_sems, gather)
        for cp in local:
            cp.wait()
        for cp in remote:
            cp.wait_send()
            cp.wait_recv()

    out = pl.pallas_call(
        body, name=name,
        out_shape=(*[pltpu.HBM(a.shape, a.dtype) for a in srcs], *[pltpu.HBM(a.shape, a.dtype) for a in lands]),
        in_specs=[_HBM] * (2 * n) + [_SEM] * 3 + [pl.BlockSpec(memory_space=pl.ANY)] * len(after),
        out_specs=tuple([_HBM] * (2 * n)), input_output_aliases={i: i for i in range(2 * n)},
        compiler_params=pltpu.CompilerParams(has_side_effects=_EFFECT))(*srcs, *lands, *sems, *after)
    return list(out[n:])


def _cols_full(g):
    nd, r, c = g.shape
    return jnp.transpose(g, (1, 0, 2)).reshape(r, nd * c)


def _cols_split(full):
    r, n = full.shape
    return jnp.transpose(full.reshape(r, N_DEV, n // N_DEV), (1, 0, 2))


def _pack_b_in(w):
    qkv = 3 * MIX_W
    pad = jnp.zeros((w.shape[0], B_IN_PAD - w.shape[1]), w.dtype)
    return jnp.concatenate([w[:, :qkv], w[:, qkv + N_MIX_HEADS:], w[:, qkv:qkv + N_MIX_HEADS], pad], axis=1)


def _unpack_b_in(w):
    qkv = 3 * MIX_W
    return jnp.concatenate([w[:, :qkv], w[:, qkv + MEM_W:qkv + MEM_W + N_MIX_HEADS], w[:, qkv:qkv + MEM_W]], axis=1)


def _to_classes(t, g):
    r = 4 ** g
    s, w = t.shape
    return jnp.transpose(t.reshape(s // r, r, w), (1, 0, 2)).reshape(s, w)


def _from_classes(t, g):
    r = 4 ** g
    s, w = t.shape
    return jnp.transpose(t.reshape(r, s // r, w), (1, 0, 2)).reshape(s, w)


def _group_stack(t):
    return jnp.stack([_to_classes(t[:, g * GROUP_W:(g + 1) * GROUP_W], g) for g in range(N_GROUPS)])


def _group_unstack(t3):
    return jnp.concatenate([_from_classes(t3[g], g) for g in range(N_GROUPS)], axis=1)


def _same_stack(t):
    return jnp.stack([_to_classes(t, g) for g in range(N_GROUPS)])


def _same_unstack(t3):
    return jnp.stack([_from_classes(t3[g], g) for g in range(N_GROUPS)])


def _ffn_forward(x, xb, wgu, get_rest, tag):
    gu, a = _ffn_up(xb, wgu, f"{tag}_up")
    wd4, gain, bias = get_rest(a)
    y, yb, xh, rstd = _mm_res_ln(a, wd4, x, gain, bias, 0.5, f"{tag}_down_ln")
    return y, yb, (xb, gu, a, xh, rstd), wd4


def _ffn_backward(dy, saved, wgu, wd4, gain, tag, after=()):
    xb, gu, a, xh, rstd = saved
    s = xb.shape[0]
    nd, c, d = wgu.shape
    dz, dzb, dgain, dbias = _ln_bwd(dy, xh, rstd, gain, 0.5, f"{tag}_ln_bwd", after)
    dh = _ffn_bwd_act(dzb, wd4, gu, f"{tag}_act_bwd").reshape(nd, s, c)
    dwd = _mm_tn(a, dzb[None], f"{tag}_dwd").reshape(nd, wd4.shape[1] // 2, d)
    dx = _mm_nt(dh, wgu, f"{tag}_dx", res=dz, w_rows_out=False)
    dwgu = _mm_tn(dh, xb[None], f"{tag}_dwgu")
    return dx, dwgu, dwd, dgain, dbias


def _mixer_a_forward(x, xb, memb, w_in, w_kv, w_out, gain, bias, tabs):
    h = _mm_nn(xb, w_in, F32, "a_in", b_rows_out=True)
    hb = _rope_cast([h], tabs, 2 * MIX_W // LANES, "a_rope")
    q3 = _group_stack(hb[:, :MIX_W])
    k3 = _group_stack(hb[:, MIX_W:2 * MIX_W])
    v3 = _group_stack(hb[:, 2 * MIX_W:3 * MIX_W])
    o3, l3 = _band_fwd(q3, k3, v3, "a_band_fwd")
    oa, lt = _band_combine(_same_unstack(o3), _same_unstack(l3), "a_combine")
    kv = _mm_nn(memb, w_kv, BF, "a_mem_kv")
    om, lm = _mem_fwd(hb, 3 * MIX_W // LANES, kv, "a_mem_fwd")
    cat = jnp.concatenate([oa, om], axis=1)
    y, yb, xh, rstd = _mm_res_ln(cat[None], w_out[None], x, gain, bias, 1.0, "a_out_ln")
    return y, yb, (xb, hb, q3, k3, v3, oa, lt, kv, lm, cat, xh, rstd)


def _mixer_a_backward(dy, saved, memb, w_in, w_kv, w_out, gain, tabs_neg, after=()):
    xb, hb, q3, k3, v3, oa, lt, kv, lm, cat, xh, rstd = saved
    dz, dzb, dgain, dbias = _ln_bwd(dy, xh, rstd, gain, 1.0, "a_ln_bwd", after)
    dcat = _mm_nt(dzb[None], w_out[None], "a_dcat", out_dtype=BF)
    dw_out = _mm_tn(cat[None], dzb[None], "a_dwout")[0]
    dqm, dkm, dvm = _mem_bwd(hb, 3 * MIX_W // LANES, kv, dcat, cat, GROUP_W // LANES, lm, "a_mem_bwd")
    dkv = jnp.concatenate([dkm, dvm], axis=1).astype(BF)
    dw_kv = _mm_tn(memb[None], dkv[None], "a_dwkv")[0]
    dq3, dk3, dv3 = _band_bwd(q3, k3, v3, _same_stack(dcat[:, :GROUP_W]), _same_stack(oa), _same_stack(lt),
                              "a_band_bwd")
    dhb = _rope_cast([_group_unstack(dq3), _group_unstack(dk3), _group_unstack(dv3), dqm], tabs_neg,
                     2 * MIX_W // LANES, "a_rope_bwd")
    dw_in = _mm_tn(dhb[None], xb[None], "a_dwin")[0]
    dx = _mm_nt(dhb[None], w_in[None], "a_dx", res=dz, w_rows_out=False)
    return dx, dw_in, dw_kv, dw_out, dgain, dbias


def _pad_rows(t, rows):
    return jnp.concatenate([t, jnp.zeros((rows - t.shape[0], t.shape[1]), t.dtype)], axis=0)


def _pad_cols(t, cols):
    return jnp.concatenate([t, jnp.zeros((t.shape[0], cols - t.shape[1]), t.dtype)], axis=1)


def _mixer_b_forward(x, xb, memb, w_in, fbias, w_kv, w_out, gain, bias, tabs):
    s = x.shape[0]
    h = _mm_nn(xb, w_in, F32, "b_in")
    hb = _rope_cast([h], tabs, 0, "b_cast")
    f0 = 3 * MIX_W + MEM_W
    f_t = _pad_rows(jnp.transpose(h[:, f0:f0 + N_MIX_HEADS]), 16)
    bias16 = _pad_rows(jnp.transpose(fbias), 16)
    c_t = _gate_fwd(f_t, bias16, "b_gate_fwd")
    c_t3 = c_t[:N_MIX_HEADS].reshape(N_MIX_HEADS // 2, 2, s)
    c_rep = jnp.broadcast_to(c_t[:N_MIX_HEADS, :, None], (N_MIX_HEADS, s, LANES))
    ob, lb = _foxt_fwd(hb, c_rep, c_t3, "b_fox_fwd")
    kv = _mm_nn(memb, w_kv, BF, "b_mem_kv")
    om, lm = _mem_fwd(hb, 3 * MIX_W // LANES, kv, "b_mem_fwd")
    cat = jnp.concatenate([ob, om], axis=1)
    y, yb, xh, rstd = _mm_res_ln(cat[None], w_out[None], x, gain, bias, 1.0, "b_out_ln")
    return y, yb, (xb, hb, f_t, bias16, c_rep, c_t3, lb, kv, lm, cat, xh, rstd)


def _mixer_b_backward(dy, saved, memb, w_in, w_kv, w_out, gain, tabs, after=()):
    xb, hb, f_t, bias16, c_rep, c_t3, lb, kv, lm, cat, xh, rstd = saved
    s = xb.shape[0]
    dz, dzb, dgain, dbias = _ln_bwd(dy, xh, rstd, gain, 1.0, "b_ln_bwd", after)
    dcat = _mm_nt(dzb[None], w_out[None], "b_dcat", out_dtype=BF)
    dw_out = _mm_tn(cat[None], dzb[None], "b_dwout")[0]
    dqm, dkm, dvm = _mem_bwd(hb, 3 * MIX_W // LANES, kv, dcat, cat, MIX_W // LANES, lm, "b_mem_bwd")
    dkv = jnp.concatenate([dkm, dvm], axis=1).astype(BF)
    dw_kv = _mm_tn(memb[None], dkv[None], "b_dwkv")[0]
    dsum = _foxt_dsum(hb, dcat, lb, c_rep, c_t3, "b_fox_dsum")
    dq, dk, dv, dc3 = _foxt_bwd(hb, dcat, dsum, lb, c_rep, c_t3, "b_fox_bwd")
    df_t, dfb = _gate_bwd(_pad_rows(dc3.reshape(N_MIX_HEADS, s), 16), f_t, bias16, "b_gate_bwd")
    df = _pad_cols(jnp.transpose(df_t[:N_MIX_HEADS]), B_IN_PAD - 3 * MIX_W - MEM_W)
    dhb = _rope_cast([dq, dk, dv, dqm, df], tabs, 0, "b_cast_bwd", transposed=(0, 1, 2))
    dw_in = _mm_tn(xb[None], dhb[None], "b_dwin")[0]
    dx = _mm_nt(dhb[None], w_in[None], "b_dx", res=dz)
    return dx, dw_in, jnp.transpose(dfb[:N_MIX_HEADS]), dw_kv, dw_out, dgain, dbias


def _stored(t, name):
    return jnp.transpose(t, (0, 2, 1)) if name in ROWS_OUT else t


def _weight_groups(w):
    b = {n: _stored(w[n], n).astype(BF) for n in WEIGHTS if n not in F32_COMM}
    return [
        [b["ffn1_w_gate_up"][0]],
        [b["ffn1_w_down"][0], w["ln_gain"], w["ln_bias"]],
        [b["a_w_in"][0], b["a_w_out"][0], b["mem_w_kv"][0]],
        [b["ffn2_w_gate_up"][0], b["ffn2_w_down"][0]],
        [b["ffn1_w_gate_up"][1], b["ffn1_w_down"][1]],
        [b["b_w_in"][0], b["b_w_out"][0], b["mem_w_kv"][1]],
        [b["ffn2_w_gate_up"][1], b["ffn2_w_down"][1]],
    ]


def _local_step(x, mem, target, fbias, get_w, put_g):
    s, d = x.shape
    tabs = _rope_tables(s, 1.0)
    tabs_neg = _rope_tables(s, -1.0)
    memb = mem.astype(BF)
    saved, wl = [], []
    cur, curb = x, x.astype(BF)
    ln = []

    def down4(t):
        return t.reshape(N_DEV // 2, -1, d)

    for i in range(DEPTH):
        if i == 0:
            def first_rest(a):
                g = get_w(1, a)
                ln.extend(jnp.transpose(t, (1, 2, 0, 3)).reshape(DEPTH, 3, 1, d) for t in g[1:3])
                return down4(g[0]), ln[0][0, 0], ln[1][0, 0]

            wgu = get_w(0, cur)[0]
            cur, curb, s1, wd = _ffn_forward(cur, curb, wgu, first_rest, "l0_ffn1")
        else:
            g = get_w(3 * i + 1, cur)
            wgu = g[0]
            cur, curb, s1, wd = _ffn_forward(cur, curb, wgu, lambda a, g=g: (down4(g[1]), ln[0][i, 0], ln[1][i, 0]),
                                             f"l{i}_ffn1")
        w1 = (wgu, wd)
        ln_g, ln_b = ln
        g = get_w(3 * i + 2, cur)
        if i == 0:
            wm = (g[0].reshape(-1, d), g[2].reshape(d, -1), _cols_full(g[1]))
            cur, curb, s2 = _mixer_a_forward(cur, curb, memb, wm[0], wm[1], wm[2], ln_g[i, 1], ln_b[i, 1], tabs)
        else:
            wm = (_pack_b_in(g[0].reshape(d, -1)), g[2].reshape(d, -1), g[1].reshape(d, -1))
            cur, curb, s2 = _mixer_b_forward(cur, curb, memb, wm[0], fbias, wm[1], wm[2], ln_g[i, 1], ln_b[i, 1],
                                             tabs)
        g = get_w(3 * i + 3, cur)
        cur, curb, s3, wd = _ffn_forward(cur, curb, g[0], lambda a, g=g: (down4(g[1]), ln_g[i, 2], ln_b[i, 2]),
                                         f"l{i}_ffn2")
        w3 = (g[0], wd)
        saved.append((s1, s2, s3))
        wl.append((w1, wm, w3))

    dy, loss = _loss_head(cur, target, "loss_head")

    dgs = [[None] * 3 for _ in range(DEPTH)]
    dbs = [[None] * 3 for _ in range(DEPTH)]
    sent = ()
    for i in reversed(range(DEPTH)):
        s1, s2, s3 = saved[i]
        w1, wm, w3 = wl[i]
        dy, dgu, dd, dgs[i][2], dbs[i][2] = _ffn_backward(dy, s3, w3[0], w3[1], ln_g[i, 2], f"l{i}_ffn2", sent)
        sent = put_g(3 * i + 2, [dgu, dd])
        if i == 0:
            dy, dw_in, dw_kv, dw_out, dgs[i][1], dbs[i][1] = _mixer_a_backward(
                dy, s2, memb, wm[0], wm[1], wm[2], ln_g[i, 1], tabs_neg, sent)
            sent = put_g(1, [dw_in.reshape(N_DEV, -1, d), _cols_split(dw_out),
                             dw_kv.reshape(N_DEV, d // N_DEV, -1)])
        else:
            dy, dw_in, dfb, dw_kv, dw_out, dgs[i][1], dbs[i][1] = _mixer_b_backward(
                dy, s2, memb, wm[0], wm[1], wm[2], ln_g[i, 1], tabs, sent)
            sent = put_g(4, [_unpack_b_in(dw_in).reshape(N_DEV, d // N_DEV, -1),
                             dw_out.reshape(N_DEV, d // N_DEV, -1), dw_kv.reshape(N_DEV, d // N_DEV, -1),
                             jnp.broadcast_to(dfb[None], (N_DEV,) + dfb.shape)])
        dy, dgu, dd, dgs[i][0], dbs[i][0] = _ffn_backward(dy, s1, w1[0], w1[1], ln_g[i, 0], f"l{i}_ffn1", sent)
        if i == 0:
            ln_pieces = []
            for parts in (dgs, dbs):
                t = jnp.concatenate([parts[a][b] for a in range(DEPTH) for b in range(3)], axis=0)
                ln_pieces.append(jnp.transpose(t.reshape(DEPTH * 3, N_DEV, d // N_DEV), (1, 0, 2)))
            sent = put_g(0, [dgu, dd] + ln_pieces)
        else:
            sent = put_g(3, [dgu, dd])
    return loss, dy


WEIGHTS = ("ffn1_w_gate_up", "ffn1_w_down", "ffn2_w_gate_up", "ffn2_w_down", "ln_gain", "ln_bias", "mem_w_kv",
           "a_w_in", "a_w_out", "b_w_in", "b_forget_bias", "b_w_out")
F32_COMM = ("ln_gain", "ln_bias", "b_forget_bias")
ROWS_OUT = ("ffn1_w_gate_up", "ffn2_w_gate_up", "a_w_in")
GRAD_SLOTS = {
    "ffn1_w_gate_up": [(0, 0), (3, 0)], "ffn1_w_down": [(0, 1), (3, 1)],
    "ffn2_w_gate_up": [(2, 0), (5, 0)], "ffn2_w_down": [(2, 1), (5, 1)],
    "ln_gain": [(0, 2)], "ln_bias": [(0, 3)], "mem_w_kv": [(1, 2), (4, 2)],
    "a_w_in": [(1, 0)], "a_w_out": [(1, 1)], "b_w_in": [(4, 0)], "b_forget_bias": [(4, 3)], "b_w_out": [(4, 1)],
}


def kernel(x, mem, ffn1_w_gate_up, ffn1_w_down, ffn2_w_gate_up, ffn2_w_down, ln_gain, ln_bias, mem_w_kv, a_w_in, a_w_out, b_w_in, b_forget_bias, b_w_out, loss_target, m_ffn1_w_gate_up, m_ffn1_w_down, m_ffn2_w_gate_up, m_ffn2_w_down, m_ln_gain, m_ln_bias, m_mem_w_kv, m_a_w_in, m_a_w_out, m_b_w_in, m_b_forget_bias, m_b_w_out, v_ffn1_w_gate_up, v_ffn1_w_down, v_ffn2_w_gate_up, v_ffn2_w_down, v_ln_gain, v_ln_bias, v_mem_w_kv, v_a_w_in, v_a_w_out, v_b_w_in, v_b_forget_bias, v_b_w_out):
    w = dict(zip(WEIGHTS, (ffn1_w_gate_up, ffn1_w_down, ffn2_w_gate_up, ffn2_w_down, ln_gain, ln_bias, mem_w_kv,
                           a_w_in, a_w_out, b_w_in, b_forget_bias, b_w_out)))
    m = dict(zip(WEIGHTS, (m_ffn1_w_gate_up, m_ffn1_w_down, m_ffn2_w_gate_up, m_ffn2_w_down, m_ln_gain, m_ln_bias,
                           m_mem_w_kv, m_a_w_in, m_a_w_out, m_b_w_in, m_b_forget_bias, m_b_w_out)))
    v = dict(zip(WEIGHTS, (v_ffn1_w_gate_up, v_ffn1_w_down, v_ffn2_w_gate_up, v_ffn2_w_down, v_ln_gain, v_ln_bias,
                           v_mem_w_kv, v_a_w_in, v_a_w_out, v_b_w_in, v_b_forget_bias, v_b_w_out)))

    gathers = []
    for k, grp in enumerate(_weight_groups(w)):
        gathers.append(_xfer_start(grp, True, f"gather{k}_start", [_started(h) for h in gathers[-1:]]))
    exchanges = {}

    def get_w(k, after):
        behind = [after] + ([_started(h) for h in gathers] if k == 0 else [])
        return _xfer_wait(gathers[k], behind, True, f"gather{k}_wait")

    def put_g(k, pieces):
        exchanges[k] = _xfer_start(pieces, False, f"grads{k}_start")
        return (_started(exchanges[k]),)

    loss, grad_x = _local_step(x[0], mem[0], loss_target[0], b_forget_bias, get_w, put_g)
    loss = lax.psum(loss[0, 0], ("x", "y", "c"))

    outs, landed = {}, {}

    def adamw(names, after):
        for n in names:
            contribs = [landed[g][j] for g, j in GRAD_SLOTS[n]]
            view = (len(contribs),) + contribs[0].shape[1:]
            shape = _stored(w[n], n).shape
            res = _reduce_adamw(contribs, *[_stored(t[n], n).reshape(view) for t in (w, m, v)], f"adamw_{n}")
            outs[n] = [_stored(t.reshape(shape), n) for t in res]
            after = outs[n][0]
        return after

    after = [grad_x, _started(exchanges[0])]
    for k in (5, 4, 3, 2, 1):
        landed[k] = _xfer_wait(exchanges[k], after, False, f"grads{k}_wait")
        after = [landed[k][0]]
    done = adamw(("ffn2_w_gate_up", "ffn2_w_down", "mem_w_kv", "a_w_in", "a_w_out", "b_w_in", "b_forget_bias",
                  "b_w_out"), None)
    landed[0] = _xfer_wait(exchanges[0], [done], False, "grads0_wait")
    adamw(("ffn1_w_gate_up", "ffn1_w_down", "ln_gain", "ln_bias"), None)
    return (loss, grad_x[None], *[outs[n][0] for n in WEIGHTS], *[outs[n][1] for n in WEIGHTS],
            *[outs[n][2] for n in WEIGHTS], *[outs[n][3] for n in WEIGHTS])
```

```python
import functools

import jax
import jax.numpy as jnp
from jax import lax
from jax.experimental import pallas as pl
from jax.experimental.pallas import tpu as pltpu

F32 = jnp.float32
BF = jnp.bfloat16
MESH_ID = pl.DeviceIdType.MESH

N_DEV = 8
DEPTH = 2
HEAD_DIM = 64
LANES = 128
N_MIX_HEADS = 12
N_MEM_HEADS = 4
MIX_W = N_MIX_HEADS * HEAD_DIM
MEM_W = N_MEM_HEADS * HEAD_DIM
N_GROUPS = 3
GROUP_W = MIX_W // N_GROUPS
BLOCK = 128
BAND_SUB = 4
ROT_HALF = 8
ROPE_THETA = 500000.0
ALPHA = (2 * DEPTH) ** 0.25
LN_EPS = 1e-5
SCALE = HEAD_DIM ** -0.5
NEG = -1e30
B_IN_PAD = 2688
ADAM_LR, ADAM_B1, ADAM_B2, ADAM_EPS, ADAM_WD, ADAM_STEP = 0.001, 0.9, 0.999, 1e-08, 0.01, 10
VMEM_LIMIT = 56 * 1024 * 1024


def _cp(*sem):
    return pltpu.CompilerParams(dimension_semantics=sem, vmem_limit_bytes=VMEM_LIMIT)


def _dot(a, b):
    return jnp.dot(a, b, preferred_element_type=F32)


def _dot_nt(a, b):
    return lax.dot_general(a, b, (((1,), (1,)), ((), ())), preferred_element_type=F32)


def _dot_tn(a, b):
    return lax.dot_general(a, b, (((0,), (0,)), ((), ())), preferred_element_type=F32)


def _sigmoid(x):
    return 1.0 / (1.0 + jnp.exp(-x))


def _tile(n, cap=1024):
    if n <= cap:
        return n
    best = LANES
    for t in range(LANES, cap + 1, LANES):
        if n % t == 0:
            best = t
    return best


def _rows(s, cap=512):
    return s if s <= cap else cap


def _mm_nn(a, b, out_dtype, name, b_rows_out=False):
    m, k = a.shape
    n = b.shape[0] if b_rows_out else b.shape[1]
    tm, tn = _rows(m), _tile(n)

    def body(a_ref, b_ref, o_ref):
        prod = _dot_nt(a_ref[...], b_ref[...]) if b_rows_out else _dot(a_ref[...], b_ref[...])
        o_ref[...] = prod.astype(o_ref.dtype)

    b_spec = (pl.BlockSpec((tn, k), lambda j, i: (j, 0)) if b_rows_out
              else pl.BlockSpec((k, tn), lambda j, i: (0, j)))
    return pl.pallas_call(
        body, name=name, grid=(n // tn, m // tm),
        in_specs=[pl.BlockSpec((tm, k), lambda j, i: (i, 0)), b_spec],
        out_specs=pl.BlockSpec((tm, tn), lambda j, i: (i, j)),
        out_shape=jax.ShapeDtypeStruct((m, n), out_dtype),
        compiler_params=_cp("parallel", "parallel"))(a, b)


def _resident(shape, index_map):
    return pl.BlockSpec(shape, index_map, pipeline_mode=pl.Buffered(1))


def _mm_tn(a, b, name, out_dtype=BF):
    na, s, m = a.shape
    nb, _, n = b.shape
    no = max(na, nb)
    tm, tn = _tile(m), _tile(n)

    def body(a_ref, b_ref, o_ref):
        o_ref[...] = _dot_tn(a_ref[...], b_ref[...]).astype(o_ref.dtype)

    def spec(nbatch, width, tile, index_map):
        fixed = nbatch == 1 and width == tile
        return _resident((None, s, tile), index_map) if fixed else pl.BlockSpec((None, s, tile), index_map)

    return pl.pallas_call(
        body, name=name, grid=(no, m // tm, n // tn),
        in_specs=[spec(na, m, tm, lambda j, r, c: (j if na > 1 else 0, 0, r)),
                  spec(nb, n, tn, lambda j, r, c: (j if nb > 1 else 0, 0, c))],
        out_specs=pl.BlockSpec((None, tm, tn), lambda j, r, c: (j, r, c)),
        out_shape=jax.ShapeDtypeStruct((no, m, n), out_dtype),
        compiler_params=_cp("parallel", "parallel", "parallel"))(a, b)


def _mm_nt(dh, w, name, res=None, out_dtype=F32, w_rows_out=True):
    nc, s, kc = dh.shape
    d = w.shape[1] if w_rows_out else w.shape[2]
    ts = _rows(s, 256 if nc > 1 else 512)
    has_res = res is not None
    mm = _dot_nt if w_rows_out else _dot

    def body(*refs):
        if has_res:
            dh_ref, w_ref, r_ref, o_ref = refs
        else:
            dh_ref, w_ref, o_ref = refs
        out = mm(dh_ref[0], w_ref[0])
        for j in range(1, nc):
            out = out + mm(dh_ref[j], w_ref[j])
        if has_res:
            out = out + ALPHA * r_ref[...]
        o_ref[...] = out.astype(o_ref.dtype)

    in_specs = [pl.BlockSpec((nc, ts, kc), lambda i: (0, i, 0)), _resident(w.shape, lambda i: (0, 0, 0))]
    args = [dh, w]
    if has_res:
        in_specs.append(pl.BlockSpec((ts, d), lambda i: (i, 0)))
        args.append(res)
    return pl.pallas_call(
        body, name=name, grid=(s // ts,), in_specs=in_specs,
        out_specs=pl.BlockSpec((ts, d), lambda i: (i, 0)),
        out_shape=jax.ShapeDtypeStruct((s, d), out_dtype),
        compiler_params=_cp("parallel"))(*args)


def _mm_res_ln(a, w, x, gain, bias, fscale, name):
    nc, s, kc = a.shape
    d = w.shape[2]
    ts = _rows(s, 256 if nc > 1 else 512)

    def body(a_ref, w_ref, x_ref, g_ref, b_ref, y_ref, yb_ref, xh_ref, r_ref):
        f = _dot(a_ref[0], w_ref[0])
        for j in range(1, nc):
            f = f + _dot(a_ref[j], w_ref[j])
        z = ALPHA * x_ref[...] + fscale * f
        mu = jnp.mean(z, axis=-1, keepdims=True)
        zc = z - mu
        var = jnp.mean(zc * zc, axis=-1, keepdims=True)
        r = lax.rsqrt(var + LN_EPS)
        xh = zc * r
        y = xh * g_ref[...] + b_ref[...]
        y_ref[...] = y
        yb_ref[...] = y.astype(BF)
        xh_ref[...] = xh
        r_ref[...] = r

    row = pl.BlockSpec((ts, d), lambda i: (i, 0))
    vec = pl.BlockSpec((1, d), lambda i: (0, 0))
    return pl.pallas_call(
        body, name=name, grid=(s // ts,),
        in_specs=[pl.BlockSpec((nc, ts, kc), lambda i: (0, i, 0)), _resident((nc, kc, d), lambda i: (0, 0, 0)),
                  row, vec, vec],
        out_specs=[row, row, row, pl.BlockSpec((ts, 1), lambda i: (i, 0))],
        out_shape=[jax.ShapeDtypeStruct((s, d), F32), jax.ShapeDtypeStruct((s, d), BF),
                   jax.ShapeDtypeStruct((s, d), F32), jax.ShapeDtypeStruct((s, 1), F32)],
        compiler_params=_cp("parallel"))(a, w, x, gain, bias)


def _ln_bwd(dy, xh, rstd, gain, fscale, name, after=()):
    s, d = dy.shape
    ts = _rows(s)
    na = len(after)

    def body(*refs):
        dy_ref, xh_ref, r_ref, g_ref = refs[:4]
        dz_ref, dzb_ref, dg_ref, db_ref = refs[4 + na:]
        i = pl.program_id(0)
        dyv = dy_ref[...]
        xhv = xh_ref[...]
        dxh = dyv * g_ref[...]
        m1 = jnp.mean(dxh, axis=-1, keepdims=True)
        m2 = jnp.mean(dxh * xhv, axis=-1, keepdims=True)
        dz = r_ref[...] * (dxh - m1 - xhv * m2)
        dz_ref[...] = dz
        dzb_ref[...] = (fscale * dz).astype(BF)

        @pl.when(i == 0)
        def _():
            dg_ref[...] = jnp.zeros_like(dg_ref)
            db_ref[...] = jnp.zeros_like(db_ref)

        dg_ref[...] += jnp.sum(dyv * xhv, axis=0, keepdims=True)
        db_ref[...] += jnp.sum(dyv, axis=0, keepdims=True)

    row = pl.BlockSpec((ts, d), lambda i: (i, 0))
    vec = pl.BlockSpec((1, d), lambda i: (0, 0))
    return pl.pallas_call(
        body, name=name, grid=(s // ts,),
        in_specs=[row, row, pl.BlockSpec((ts, 1), lambda i: (i, 0)), vec] + [pl.BlockSpec(memory_space=pl.ANY)] * na,
        out_specs=[row, row, vec, vec],
        out_shape=[jax.ShapeDtypeStruct((s, d), F32), jax.ShapeDtypeStruct((s, d), BF),
                   jax.ShapeDtypeStruct((1, d), F32), jax.ShapeDtypeStruct((1, d), F32)],
        compiler_params=_cp("arbitrary"))(dy, xh, rstd, gain, *after)


def _ffn_up(xb, wgu, name):
    s, d = xb.shape
    c = wgu.shape[1]
    nch = wgu.shape[0] // 2
    ts = _rows(s)
    w4 = wgu.reshape(2, nch, c, d)

    def body(x_ref, w_ref, gu_ref, a_ref):
        x = x_ref[...]
        g = _dot_nt(x, w_ref[0])
        u = _dot_nt(x, w_ref[1])
        gu_ref[0] = g.astype(BF)
        gu_ref[1] = u.astype(BF)
        a_ref[...] = (g * _sigmoid(g) * u).astype(BF)

    return pl.pallas_call(
        body, name=name, grid=(nch, s // ts),
        in_specs=[pl.BlockSpec((ts, d), lambda j, i: (i, 0)),
                  pl.BlockSpec((2, None, c, d), lambda j, i: (0, j, 0, 0))],
        out_specs=[pl.BlockSpec((2, None, ts, c), lambda j, i: (0, j, i, 0)),
                   pl.BlockSpec((None, ts, c), lambda j, i: (j, i, 0))],
        out_shape=[jax.ShapeDtypeStruct((2, nch, s, c), BF), jax.ShapeDtypeStruct((nch, s, c), BF)],
        compiler_params=_cp("parallel", "parallel"))(xb, w4)


def _ffn_bwd_act(dzb, wd4, gu, name):
    s, d = dzb.shape
    nch, c = wd4.shape[0], wd4.shape[1]
    ts = _rows(s)

    def body(dz_ref, w_ref, gu_ref, dh_ref):
        da = _dot_nt(dz_ref[...], w_ref[...])
        g = gu_ref[0].astype(F32)
        u = gu_ref[1].astype(F32)
        sg = _sigmoid(g)
        dh_ref[0] = (da * u * sg * (1.0 + g * (1.0 - sg))).astype(BF)
        dh_ref[1] = (da * g * sg).astype(BF)

    return pl.pallas_call(
        body, name=name, grid=(nch, s // ts),
        in_specs=[pl.BlockSpec((ts, d), lambda j, i: (i, 0)),
                  pl.BlockSpec((None, c, d), lambda j, i: (j, 0, 0)),
                  pl.BlockSpec((2, None, ts, c), lambda j, i: (0, j, i, 0))],
        out_specs=pl.BlockSpec((2, None, ts, c), lambda j, i: (0, j, i, 0)),
        out_shape=jax.ShapeDtypeStruct((2, nch, s, c), BF),
        compiler_params=_cp("parallel", "parallel"))(dzb, wd4, gu)


def _rope_tables(s, sign):
    pos = jnp.arange(s, dtype=F32)
    inv_freq = 1.0 / (ROPE_THETA ** (jnp.arange(ROT_HALF, dtype=F32) / ROT_HALF))
    ang = pos[:, None] * inv_freq[None, :]
    cos, sin = jnp.cos(ang), jnp.sin(ang) * sign
    one = jnp.ones((s, HEAD_DIM - 2 * ROT_HALF), F32)
    zero = jnp.zeros((s, HEAD_DIM - 2 * ROT_HALF), F32)
    zh = jnp.zeros((s, ROT_HALF), F32)
    cos_f = jnp.concatenate([cos, cos, one], axis=1)
    sin_a = jnp.concatenate([-sin, zh, zero], axis=1)
    sin_b = jnp.concatenate([zh, sin, zero], axis=1)
    rep = LANES // HEAD_DIM
    return tuple(jnp.tile(t, (1, rep)) for t in (cos_f, sin_a, sin_b))


def _rope_cast(parts, tabs, n_rope, name, transposed=()):
    s = tabs[0].shape[0]
    flip = [i in transposed for i in range(len(parts))]
    widths = [p.shape[0] if f else p.shape[1] for p, f in zip(parts, flip)]
    n = sum(widths)
    npart = len(parts)
    ts = _rows(s, 256)

    def body(*refs):
        part_refs = refs[:npart]
        c_ref, sa_ref, sb_ref, o_ref = refs[npart:]
        col = 0
        for ref, w, f in zip(part_refs, widths, flip):
            for j in range(w // LANES):
                if f:
                    t = jnp.transpose(ref[j * LANES:(j + 1) * LANES, :])
                else:
                    t = ref[:, j * LANES:(j + 1) * LANES]
                if col < n_rope:
                    t = (t * c_ref[...] + pltpu.roll(t, LANES - ROT_HALF, 1) * sa_ref[...]
                         + pltpu.roll(t, ROT_HALF, 1) * sb_ref[...])
                o_ref[:, col * LANES:(col + 1) * LANES] = t.astype(BF)
                col += 1

    tab = pl.BlockSpec((ts, LANES), lambda i: (i, 0))
    return pl.pallas_call(
        body, name=name, grid=(s // ts,),
        in_specs=[pl.BlockSpec((w, ts), lambda i: (0, i)) if f else pl.BlockSpec((ts, w), lambda i: (i, 0))
                  for w, f in zip(widths, flip)] + [tab, tab, tab],
        out_specs=pl.BlockSpec((ts, n), lambda i: (i, 0)),
        out_shape=jax.ShapeDtypeStruct((s, n), BF),
        compiler_params=_cp("parallel"))(*parts, *tabs)


def _head_masks():
    lane = lax.broadcasted_iota(jnp.int32, (1, LANES), 1)
    return [lane < HEAD_DIM, lane >= HEAD_DIM]


def _sel(mask, v):
    return jnp.where(mask, v, jnp.zeros_like(v))


def _pick(mask, wide, fill):
    return jnp.max(jnp.where(mask, wide, fill), axis=1, keepdims=True)


def _band_masks(has_other, prev):
    qi = lax.broadcasted_iota(jnp.int32, (BLOCK, BLOCK), 0)
    kj = lax.broadcasted_iota(jnp.int32, (BLOCK, BLOCK), 1)
    if prev:
        return kj >= qi + jnp.where(has_other, 0, BLOCK)
    return kj <= qi


def _band_fwd(q3, k3, v3, name):
    ng, s, w = q3.shape
    nb = s // BLOCK
    npair = w // LANES
    nsub = BAND_SUB
    tile = nsub * BLOCK

    def body(q_ref, kc_ref, kp_ref, vc_ref, vp_ref, o_ref, l_ref):
        g = pl.program_id(0)
        t = pl.program_id(2)
        nbl = jnp.right_shift(nb, 2 * g)
        mc = _band_masks(None, False)
        hm = _head_masks()
        for i in range(nsub):
            rows = slice(i * BLOCK, (i + 1) * BLOCK)
            has_prev = jnp.bitwise_and(t * nsub + i, nbl - 1) != 0
            mp = _band_masks(has_prev, True)
            q, kc, vc = q_ref[rows, :], kc_ref[rows, :], vc_ref[rows, :]
            if i == 0:
                kp, vp = kp_ref[...], vp_ref[...]
            else:
                prev = slice((i - 1) * BLOCK, i * BLOCK)
                kp, vp = kc_ref[prev, :], vc_ref[prev, :]
            o = jnp.zeros((BLOCK, LANES), F32)
            lse_w = jnp.zeros((BLOCK, LANES), F32)
            for h in range(2):
                qh = _sel(hm[h], q)
                sc = jnp.where(mc, _dot_nt(qh, kc) * SCALE, NEG)
                sp = jnp.where(mp, _dot_nt(qh, kp) * SCALE, NEG)
                m = jnp.maximum(jnp.max(sc, axis=1, keepdims=True), jnp.max(sp, axis=1, keepdims=True))
                pc = jnp.exp(sc - m)
                pp = jnp.exp(sp - m)
                l = jnp.sum(pc, axis=1, keepdims=True) + jnp.sum(pp, axis=1, keepdims=True)
                oh = _dot(pc.astype(BF), _sel(hm[h], vc)) + _dot(pp.astype(BF), _sel(hm[h], vp))
                o = o + oh / l
                lse_w = jnp.where(hm[h], m + jnp.log(l), lse_w)
            o_ref[rows, :] = o
            l_ref[rows, :] = lse_w

    cur = pl.BlockSpec((None, tile, LANES), lambda g, p, t: (g, t, p))
    prv = pl.BlockSpec((None, BLOCK, LANES), lambda g, p, t: (g, jnp.maximum(t * nsub - 1, 0), p))
    return pl.pallas_call(
        body, name=name, grid=(ng, npair, nb // nsub),
        in_specs=[cur, cur, prv, cur, prv], out_specs=[cur, cur],
        out_shape=[jax.ShapeDtypeStruct((ng, s, w), F32), jax.ShapeDtypeStruct((ng, s, w), F32)],
        compiler_params=_cp("parallel", "parallel", "parallel"))(q3, k3, k3, v3, v3)


def _band_combine(o3, l3, name):
    ng, s, w = o3.shape
    ts = _rows(s)

    def body(o_ref, l_ref, oa_ref, lt_ref):
        ls = [l_ref[g] for g in range(ng)]
        m = functools.reduce(jnp.maximum, ls)
        es = [jnp.exp(l - m) for l in ls]
        den = functools.reduce(lambda a, b: a + b, es)
        num = functools.reduce(lambda a, b: a + b, [es[g] * o_ref[g] for g in range(ng)])
        oa_ref[...] = (num / den).astype(BF)
        lt_ref[...] = m + jnp.log(den)

    blk3 = pl.BlockSpec((ng, ts, w), lambda i: (0, i, 0))
    blk = pl.BlockSpec((ts, w), lambda i: (i, 0))
    return pl.pallas_call(
        body, name=name, grid=(s // ts,), in_specs=[blk3, blk3], out_specs=[blk, blk],
        out_shape=[jax.ShapeDtypeStruct((s, w), BF), jax.ShapeDtypeStruct((s, w), F32)],
        compiler_params=_cp("parallel"))(o3, l3)


def _band_bwd(q3, k3, v3, do3, oa3, lt3, name):
    ng, s, w = q3.shape
    nb = s // BLOCK
    npair = w // LANES
    nsub = BAND_SUB
    tile = nsub * BLOCK

    def body(q_ref, qn_ref, kc_ref, kp_ref, vc_ref, vp_ref, do_ref, don_ref, oa_ref, oan_ref, lt_ref, ltn_ref,
             dq_ref, dk_ref, dv_ref):
        g = pl.program_id(0)
        t = pl.program_id(2)
        nbl = jnp.right_shift(nb, 2 * g)
        mc = _band_masks(None, False)
        hm = _head_masks()

        def block(ref, edge_ref, i):
            if i < 0 or i >= nsub:
                return edge_ref[...]
            return ref[i * BLOCK:(i + 1) * BLOCK, :]

        for i in range(nsub):
            b = t * nsub + i
            mp = _band_masks(jnp.bitwise_and(b, nbl - 1) != 0, True)
            mn = _band_masks(jnp.bitwise_and(b + 1, nbl - 1) != 0, True)
            q, qn = block(q_ref, None, i), block(q_ref, qn_ref, i + 1)
            kc, kp = block(kc_ref, None, i), block(kc_ref, kp_ref, i - 1)
            vc, vp = block(vc_ref, None, i), block(vc_ref, vp_ref, i - 1)
            do, don = block(do_ref, None, i), block(do_ref, don_ref, i + 1)
            dd = do.astype(F32) * block(oa_ref, None, i).astype(F32)
            ddn = don.astype(F32) * block(oa_ref, oan_ref, i + 1).astype(F32)
            lt, ltn = block(lt_ref, None, i), block(lt_ref, ltn_ref, i + 1)
            dq = jnp.zeros((BLOCK, LANES), F32)
            dk = jnp.zeros((BLOCK, LANES), F32)
            dv = jnp.zeros((BLOCK, LANES), F32)
            for h in range(2):
                qh, doh = _sel(hm[h], q), _sel(hm[h], do)
                qnh, donh = _sel(hm[h], qn), _sel(hm[h], don)
                kch, kph = _sel(hm[h], kc), _sel(hm[h], kp)
                lse = _pick(hm[h], lt, NEG)
                lsen = _pick(hm[h], ltn, NEG)
                dsum = jnp.sum(_sel(hm[h], dd), axis=1, keepdims=True)
                dsumn = jnp.sum(_sel(hm[h], ddn), axis=1, keepdims=True)
                pc = jnp.exp(jnp.where(mc, _dot_nt(qh, kc) * SCALE, NEG) - lse)
                pp = jnp.exp(jnp.where(mp, _dot_nt(qh, kp) * SCALE, NEG) - lse)
                dsc = pc * (_dot_nt(doh, vc) - dsum)
                dsp = pp * (_dot_nt(doh, vp) - dsum)
                dq = dq + SCALE * (_dot(dsc.astype(BF), kch) + _dot(dsp.astype(BF), kph))
                pn = jnp.exp(jnp.where(mn, _dot_nt(qnh, kc) * SCALE, NEG) - lsen)
                dsn = pn * (_dot_nt(donh, vc) - dsumn)
                dk = dk + SCALE * (_dot_tn(dsc.astype(BF), qh) + _dot_tn(dsn.astype(BF), qnh))
                dv = dv + _dot_tn(pc.astype(BF), doh) + _dot_tn(pn.astype(BF), donh)
            rows = slice(i * BLOCK, (i + 1) * BLOCK)
            dq_ref[rows, :] = dq
            dk_ref[rows, :] = dk
            dv_ref[rows, :] = dv

    cur = pl.BlockSpec((None, tile, LANES), lambda g, p, t: (g, t, p))
    prv = pl.BlockSpec((None, BLOCK, LANES), lambda g, p, t: (g, jnp.maximum(t * nsub - 1, 0), p))
    nxt = pl.BlockSpec((None, BLOCK, LANES), lambda g, p, t: (g, jnp.minimum(t * nsub + nsub, nb - 1), p))
    out = jax.ShapeDtypeStruct((ng, s, w), F32)
    return pl.pallas_call(
        body, name=name, grid=(ng, npair, nb // nsub),
        in_specs=[cur, nxt, cur, prv, cur, prv, cur, nxt, cur, nxt, cur, nxt],
        out_specs=[cur, cur, cur], out_shape=[out, out, out],
        compiler_params=_cp("parallel", "parallel", "parallel"))(
            q3, q3, k3, k3, v3, v3, do3, do3, oa3, oa3, lt3, lt3)


def _mem_fwd(hb, q_blk0, kv, name):
    s = hb.shape[0]
    m = kv.shape[0]
    tq = _rows(s)
    npair = MEM_W // LANES

    def body(q_ref, k_ref, v_ref, o_ref, l_ref):
        q, k, v = q_ref[...], k_ref[...], v_ref[...]
        hm = _head_masks()
        o = jnp.zeros((tq, LANES), F32)
        lse_w = jnp.zeros((tq, LANES), F32)
        for h in range(2):
            sc = _dot_nt(_sel(hm[h], q), k) * SCALE
            mx = jnp.max(sc, axis=1, keepdims=True)
            p = jnp.exp(sc - mx)
            l = jnp.sum(p, axis=1, keepdims=True)
            o = o + _dot(p.astype(BF), _sel(hm[h], v)) / l
            lse_w = jnp.where(hm[h], mx + jnp.log(l), lse_w)
        o_ref[...] = o.astype(BF)
        l_ref[...] = lse_w

    blk = pl.BlockSpec((tq, LANES), lambda p, i: (i, p))
    return pl.pallas_call(
        body, name=name, grid=(npair, s // tq),
        in_specs=[pl.BlockSpec((tq, LANES), lambda p, i: (i, q_blk0 + p)),
                  pl.BlockSpec((m, LANES), lambda p, i: (0, p)),
                  pl.BlockSpec((m, LANES), lambda p, i: (0, npair + p))],
        out_specs=[blk, blk],
        out_shape=[jax.ShapeDtypeStruct((s, MEM_W), BF), jax.ShapeDtypeStruct((s, MEM_W), F32)],
        compiler_params=_cp("parallel", "parallel"))(hb, kv, kv)


def _mem_bwd(hb, q_blk0, kv, dcat, cat, o_blk0, lse, name):
    s = hb.shape[0]
    m = kv.shape[0]
    tq = _rows(s)
    npair = MEM_W // LANES

    def body(q_ref, k_ref, v_ref, do_ref, o_ref, l_ref, dq_ref, dk_ref, dv_ref):
        i = pl.program_id(1)

        @pl.when(i == 0)
        def _():
            dk_ref[...] = jnp.zeros_like(dk_ref)
            dv_ref[...] = jnp.zeros_like(dv_ref)

        q, k, v, do = q_ref[...], k_ref[...], v_ref[...], do_ref[...]
        dd = do.astype(F32) * o_ref[...].astype(F32)
        lt = l_ref[...]
        hm = _head_masks()
        dq = jnp.zeros((tq, LANES), F32)
        dk = jnp.zeros((m, LANES), F32)
        dv = jnp.zeros((m, LANES), F32)
        for h in range(2):
            qh, doh = _sel(hm[h], q), _sel(hm[h], do)
            p = jnp.exp(_dot_nt(qh, k) * SCALE - _pick(hm[h], lt, NEG))
            ds = p * (_dot_nt(doh, v) - jnp.sum(_sel(hm[h], dd), axis=1, keepdims=True))
            dq = dq + SCALE * _dot(ds.astype(BF), _sel(hm[h], k))
            dk = dk + SCALE * _dot_tn(ds.astype(BF), qh)
            dv = dv + _dot_tn(p.astype(BF), doh)
        dq_ref[...] = dq
        dk_ref[...] += dk
        dv_ref[...] += dv

    row = pl.BlockSpec((tq, LANES), lambda p, i: (i, p))
    orow = pl.BlockSpec((tq, LANES), lambda p, i: (i, o_blk0 + p))
    acc = pl.BlockSpec((m, LANES), lambda p, i: (0, p))
    return pl.pallas_call(
        body, name=name, grid=(npair, s // tq),
        in_specs=[pl.BlockSpec((tq, LANES), lambda p, i: (i, q_blk0 + p)),
                  pl.BlockSpec((m, LANES), lambda p, i: (0, p)),
                  pl.BlockSpec((m, LANES), lambda p, i: (0, npair + p)), orow, orow, row],
        out_specs=[row, acc, acc],
        out_shape=[jax.ShapeDtypeStruct((s, MEM_W), F32), jax.ShapeDtypeStruct((m, MEM_W), F32),
                   jax.ShapeDtypeStruct((m, MEM_W), F32)],
        compiler_params=_cp("parallel", "arbitrary"))(hb, kv, kv, dcat, cat, lse)


def _gate_fwd(f_t, bias, name):
    hp, s = f_t.shape
    nblk = s // LANES

    def body(f_ref, b_ref, c_ref):
        lane = lax.broadcasted_iota(jnp.int32, (hp, LANES), 1)

        def step(i, carry):
            off = pl.multiple_of(i * LANES, LANES)
            x = f_ref[:, pl.ds(off, LANES)] + b_ref[...]
            acc = jnp.minimum(x, 0.0) - jnp.log(1.0 + jnp.exp(-jnp.abs(x)))
            sh = 1
            while sh < LANES:
                acc = acc + jnp.where(lane >= sh, pltpu.roll(acc, sh, 1), 0.0)
                sh *= 2
            acc = acc + carry
            c_ref[:, pl.ds(off, LANES)] = acc
            return acc[:, LANES - 1:LANES]

        lax.fori_loop(0, nblk, step, jnp.zeros((hp, 1), F32))

    vm = pl.BlockSpec(memory_space=pltpu.VMEM)
    return pl.pallas_call(body, name=name, in_specs=[vm, vm], out_specs=vm,
                          out_shape=jax.ShapeDtypeStruct((hp, s), F32),
                          compiler_params=pltpu.CompilerParams(vmem_limit_bytes=VMEM_LIMIT))(f_t, bias)


def _gate_bwd(dc_t, f_t, bias, name):
    hp, s = f_t.shape
    nblk = s // LANES

    def body(dc_ref, f_ref, b_ref, df_ref, db_ref):
        lane = lax.broadcasted_iota(jnp.int32, (hp, LANES), 1)

        def step(t, carry):
            suffix, dbias = carry
            off = pl.multiple_of((nblk - 1 - t) * LANES, LANES)
            acc = dc_ref[:, pl.ds(off, LANES)]
            sh = 1
            while sh < LANES:
                acc = acc + jnp.where(lane < LANES - sh, pltpu.roll(acc, LANES - sh, 1), 0.0)
                sh *= 2
            acc = acc + suffix
            x = f_ref[:, pl.ds(off, LANES)] + b_ref[...]
            df = acc * _sigmoid(-x)
            df_ref[:, pl.ds(off, LANES)] = df
            return acc[:, 0:1], dbias + jnp.sum(df, axis=1, keepdims=True)

        _, dbias = lax.fori_loop(0, nblk, step, (jnp.zeros((hp, 1), F32), jnp.zeros((hp, 1), F32)))
        db_ref[...] = dbias

    vm = pl.BlockSpec(memory_space=pltpu.VMEM)
    return pl.pallas_call(body, name=name, in_specs=[vm, vm, vm], out_specs=[vm, vm],
                          out_shape=[jax.ShapeDtypeStruct((hp, s), F32), jax.ShapeDtypeStruct((hp, 1), F32)],
                          compiler_params=pltpu.CompilerParams(vmem_limit_bytes=VMEM_LIMIT))(dc_t, f_t, bias)


def _wide(rep, width):
    return jnp.tile(rep, (1, width // LANES))


def _fold(t):
    part = t[:, :LANES]
    for c in range(1, t.shape[1] // LANES):
        part = part + t[:, c * LANES:(c + 1) * LANES]
    return part


def _fox_logits(q, k, cq_rep, ck_row, mask, hmask):
    s = _dot_nt(_sel(hmask, q), k) + (_wide(cq_rep, ck_row.shape[1]) - ck_row)
    if mask is not None:
        s = jnp.where(mask, s, NEG)
    return s


def _diag_mask(t):
    return lax.broadcasted_iota(jnp.int32, (t, t), 1) <= lax.broadcasted_iota(jnp.int32, (t, t), 0)


def _fox_fwd(hb, c_rep, c_t3, name):
    s = hb.shape[0]
    npair = MIX_W // LANES
    tq = tk = _rows(s)
    nq = s // tq

    def body(q_ref, k_ref, v_ref, cq_ref, ck_ref, o_ref, l_ref, m_s, l_s, acc):
        qi = pl.program_id(1)
        kj = pl.program_id(2)
        hm = _head_masks()

        @pl.when(kj == 0)
        def _():
            m_s[...] = jnp.full_like(m_s, NEG)
            l_s[...] = jnp.zeros_like(l_s)
            acc[...] = jnp.zeros_like(acc)

        def step(mask):
            q, k, v = q_ref[...] * SCALE, k_ref[...], v_ref[...]
            ck = ck_ref[...]
            for h in range(2):
                sc = _fox_logits(q, k, cq_ref[h], ck[h:h + 1, :], mask, hm[h])
                m_old = m_s[h]
                m_new = jnp.maximum(m_old, jnp.max(sc, axis=1, keepdims=True))
                pr = jnp.exp(sc - _wide(m_new, tk))
                corr = jnp.exp(m_old - m_new)
                l_s[h] = l_s[h] * corr + _fold(pr)
                acc[h] = acc[h] * corr + _dot(pr.astype(BF), _sel(hm[h], v))
                m_s[h] = m_new

        @pl.when(kj < qi)
        def _():
            step(None)

        @pl.when(kj == qi)
        def _():
            step(_diag_mask(tq))
            outs = []
            for h in range(2):
                den = jnp.sum(l_s[h], axis=1, keepdims=True)
                outs.append(acc[h] / den)
                l_ref[h] = m_s[h] + jnp.log(den)
            o_ref[...] = jnp.where(hm[0], outs[0], outs[1]).astype(BF)

    def kv_map(off):
        return lambda p, i, j: (jnp.minimum(j, i), off + p)

    blk = pl.BlockSpec((tq, LANES), lambda p, i, j: (i, p))
    return pl.pallas_call(
        body, name=name, grid=(npair, nq, nq),
        in_specs=[blk, pl.BlockSpec((tk, LANES), kv_map(npair)), pl.BlockSpec((tk, LANES), kv_map(2 * npair)),
                  pl.BlockSpec((2, tq, LANES), lambda p, i, j: (p, i, 0)),
                  pl.BlockSpec((None, 2, tk), lambda p, i, j: (p, 0, jnp.minimum(j, i)))],
        out_specs=[blk, pl.BlockSpec((2, tq, LANES), lambda p, i, j: (p, i, 0))],
        out_shape=[jax.ShapeDtypeStruct((s, MIX_W), BF), jax.ShapeDtypeStruct((2 * npair, s, LANES), F32)],
        scratch_shapes=[pltpu.VMEM((2, tq, LANES), F32), pltpu.VMEM((2, tq, LANES), F32),
                        pltpu.VMEM((2, tq, LANES), F32)],
        compiler_params=_cp("parallel", "parallel", "arbitrary"))(hb, hb, hb, c_rep, c_t3)


def _fox_dsum(hb, dcat, lse, c_rep, c_t3, name):
    s = hb.shape[0]
    npair = MIX_W // LANES
    tq = tk = _rows(s)
    nq = s // tq

    def body(q_ref, k_ref, v_ref, do_ref, l_ref, cq_ref, ck_ref, d_ref, acc):
        qi = pl.program_id(1)
        kj = pl.program_id(2)
        hm = _head_masks()

        @pl.when(kj == 0)
        def _():
            acc[...] = jnp.zeros_like(acc)

        def step(mask):
            q, k, v, do = q_ref[...] * SCALE, k_ref[...], v_ref[...], do_ref[...]
            ck = ck_ref[...]
            for h in range(2):
                pr = jnp.exp(_fox_logits(q, k, cq_ref[h], ck[h:h + 1, :], mask, hm[h]) - _wide(l_ref[h], tk))
                acc[h] += _fold(pr * _dot_nt(_sel(hm[h], do), v))

        @pl.when(kj < qi)
        def _():
            step(None)

        @pl.when(kj == qi)
        def _():
            step(_diag_mask(tq))
            for h in range(2):
                d_ref[h] = jnp.broadcast_to(jnp.sum(acc[h], axis=1, keepdims=True), (tq, LANES))

    def kv_map(off):
        return lambda p, i, j: (jnp.minimum(j, i), off + p)

    blk = pl.BlockSpec((tq, LANES), lambda p, i, j: (i, p))
    rep = pl.BlockSpec((2, tq, LANES), lambda p, i, j: (p, i, 0))
    return pl.pallas_call(
        body, name=name, grid=(npair, nq, nq),
        in_specs=[blk, pl.BlockSpec((tk, LANES), kv_map(npair)), pl.BlockSpec((tk, LANES), kv_map(2 * npair)),
                  blk, rep, rep, pl.BlockSpec((None, 2, tk), lambda p, i, j: (p, 0, jnp.minimum(j, i)))],
        out_specs=rep, out_shape=jax.ShapeDtypeStruct((2 * npair, s, LANES), F32),
        scratch_shapes=[pltpu.VMEM((2, tq, LANES), F32)],
        compiler_params=_cp("parallel", "parallel", "arbitrary"))(hb, hb, hb, dcat, lse, c_rep, c_t3)


def _fox_bwd(hb, dcat, dsum, lse, c_rep, c_t3, name):
    s = hb.shape[0]
    npair = MIX_W // LANES
    tq = tk = _rows(s)
    nq = s // tq

    def body(q_ref, k_ref, v_ref, do_ref, d_ref, l_ref, cq_ref, ck_ref, dq_ref, dk_ref, dv_ref, dc_ref):
        kj = pl.program_id(1)
        qi = pl.program_id(2)
        hm = _head_masks()

        @pl.when(qi == 0)
        def _():
            dk_ref[...] = jnp.zeros_like(dk_ref)
            dv_ref[...] = jnp.zeros_like(dv_ref)
            dc_ref[...] = jnp.zeros_like(dc_ref)

        @pl.when((qi == 0) & (kj == 0))
        def _():
            dq_ref[...] = jnp.zeros_like(dq_ref)

        def step(mask):
            q, k, v, do = q_ref[...] * SCALE, k_ref[...], v_ref[...], do_ref[...]
            ck = ck_ref[...]
            dq = jnp.zeros((tq, LANES), F32)
            dk = jnp.zeros((tk, LANES), F32)
            dv = jnp.zeros((tk, LANES), F32)
            dcs = []
            for h in range(2):
                qh, doh = _sel(hm[h], q), _sel(hm[h], do)
                pr = jnp.exp(_fox_logits(q, k, cq_ref[h], ck[h:h + 1, :], mask, hm[h]) - _wide(l_ref[h], tk))
                ds = pr * (_dot_nt(doh, v) - _wide(d_ref[h], tk))
                dsb = ds.astype(BF)
                dq = dq + _dot(dsb, _sel(hm[h], k))
                dk = dk + _dot_tn(dsb, qh)
                dv = dv + _dot_tn(pr.astype(BF), doh)
                dcs.append(jnp.sum(ds, axis=0, keepdims=True))
            rows = pl.ds(pl.multiple_of(qi * tq, tq), tq)
            dq_ref[rows, :] += SCALE * dq
            dk_ref[...] += dk
            dv_ref[...] += dv
            dc_ref[...] -= jnp.concatenate(dcs, axis=0)

        @pl.when(qi > kj)
        def _():
            step(None)

        @pl.when(qi == kj)
        def _():
            step(_diag_mask(tq))

    def q_map(p, j, i):
        return (jnp.maximum(i, j), p)

    kblk = pl.BlockSpec((tk, LANES), lambda p, j, i: (j, p))
    rep = pl.BlockSpec((2, tq, LANES), lambda p, j, i: (p, jnp.maximum(i, j), 0))
    return pl.pallas_call(
        body, name=name, grid=(npair, nq, nq),
        in_specs=[pl.BlockSpec((tq, LANES), q_map),
                  pl.BlockSpec((tk, LANES), lambda p, j, i: (j, npair + p)),
                  pl.BlockSpec((tk, LANES), lambda p, j, i: (j, 2 * npair + p)),
                  pl.BlockSpec((tq, LANES), q_map), rep, rep, rep,
                  pl.BlockSpec((None, 2, tk), lambda p, j, i: (p, 0, j))],
        out_specs=[pl.BlockSpec((s, LANES), lambda p, j, i: (0, p)), kblk, kblk,
                   pl.BlockSpec((None, 2, tk), lambda p, j, i: (p, 0, j))],
        out_shape=[jax.ShapeDtypeStruct((s, MIX_W), F32), jax.ShapeDtypeStruct((s, MIX_W), F32),
                   jax.ShapeDtypeStruct((s, MIX_W), F32), jax.ShapeDtypeStruct((npair, 2, s), F32)],
        compiler_params=_cp("arbitrary", "arbitrary", "arbitrary"))(hb, hb, hb, dcat, dsum, lse, c_rep, c_t3)


def _foxt_logits(q, k, cq_row, ck_rep, mask, hmask):
    s = _dot_nt(_sel(hmask, k), q) + (cq_row - _wide(ck_rep, q.shape[0]))
    if mask is not None:
        s = jnp.where(mask, s, NEG)
    return s


def _causal_t(qi, kj, tq, tk):
    return (kj * tk + lax.broadcasted_iota(jnp.int32, (tk, tq), 0)
            <= qi * tq + lax.broadcasted_iota(jnp.int32, (tk, tq), 1))


def _fox_tiles(s):
    tq = _rows(s, 1024)
    return tq, tq // 2, s // tq


def _count_ge(t, bounds):
    return sum([(t >= b).astype(jnp.int32) for b in bounds], jnp.int32(0))


def _sweep_q_major(t, nq):
    qi = _count_ge(t, [r * (r + 1) for r in range(1, nq)])
    return qi, t - qi * (qi + 1)


def _sweep_k_major(t, nq):
    counts = [nq - j // 2 for j in range(2 * nq)]
    offs = [sum(counts[:j]) for j in range(1, 2 * nq)]
    kj = _count_ge(t, offs)
    start = sum([jnp.where(t >= o, c, 0) for o, c in zip(offs, counts)], jnp.int32(0))
    qi = kj // 2 + (t - start)
    return kj, qi, t == start, qi == nq - 1


def _foxt_fwd(hb, c_rep, c_t3, name):
    s = hb.shape[0]
    npair = MIX_W // LANES
    tq, tk, nq = _fox_tiles(s)

    def body(q_ref, k_ref, v_ref, cq_ref, ck_ref, o_ref, l_ref, m_s, l_s, acc):
        qi, kj = _sweep_q_major(pl.program_id(1), nq)
        hm = _head_masks()

        @pl.when(kj == 0)
        def _():
            m_s[...] = jnp.full_like(m_s, NEG)
            l_s[...] = jnp.zeros_like(l_s)
            acc[...] = jnp.zeros_like(acc)

        def step(mask):
            q, k = q_ref[...] * SCALE, k_ref[...]
            vt = jnp.transpose(v_ref[...])
            cq = cq_ref[...]
            for h in range(2):
                st = _foxt_logits(q, k, cq[h:h + 1, :], ck_ref[h], mask, hm[h])
                m_old = m_s[h]
                m_new = jnp.maximum(m_old, jnp.max(st, axis=0, keepdims=True))
                pt = jnp.exp(st - m_new)
                corr = jnp.exp(m_old - m_new)
                l_s[h] = l_s[h] * corr + jnp.sum(pt, axis=0, keepdims=True)
                acc[h] = acc[h] * corr + _dot(vt[h * HEAD_DIM:(h + 1) * HEAD_DIM, :], pt.astype(BF))
                m_s[h] = m_new

        @pl.when(kj < 2 * qi)
        def _():
            step(None)

        @pl.when(kj >= 2 * qi)
        def _():
            step(_causal_t(qi, kj, tq, tk))

        @pl.when(kj == 2 * qi + 1)
        def _():
            outs = []
            for h in range(2):
                outs.append(acc[h] / l_s[h])
                l_ref[h:h + 1, :] = m_s[h] + jnp.log(l_s[h])
            o_ref[...] = jnp.transpose(jnp.concatenate(outs, axis=0)).astype(BF)

    def q_map(p, t):
        return (_sweep_q_major(t, nq)[0], p)

    def kv_map(off):
        return lambda p, t: (_sweep_q_major(t, nq)[1], off + p)

    blk = pl.BlockSpec((tq, LANES), q_map)
    row = pl.BlockSpec((None, 2, tq), lambda p, t: (p, 0, _sweep_q_major(t, nq)[0]))
    return pl.pallas_call(
        body, name=name, grid=(npair, nq * (nq + 1)),
        in_specs=[blk, pl.BlockSpec((tk, LANES), kv_map(npair)), pl.BlockSpec((tk, LANES), kv_map(2 * npair)), row,
                  pl.BlockSpec((2, tk, LANES), lambda p, t: (p, _sweep_q_major(t, nq)[1], 0))],
        out_specs=[blk, row],
        out_shape=[jax.ShapeDtypeStruct((s, MIX_W), BF), jax.ShapeDtypeStruct((npair, 2, s), F32)],
        scratch_shapes=[pltpu.VMEM((2, 1, tq), F32), pltpu.VMEM((2, 1, tq), F32),
                        pltpu.VMEM((2, HEAD_DIM, tq), F32)],
        compiler_params=_cp("parallel", "arbitrary"))(hb, hb, hb, c_t3, c_rep)


def _foxt_dsum(hb, dcat, lse, c_rep, c_t3, name):
    s = hb.shape[0]
    npair = MIX_W // LANES
    tq, tk, nq = _fox_tiles(s)

    def body(q_ref, k_ref, v_ref, do_ref, l_ref, cq_ref, ck_ref, d_ref, acc):
        qi, kj = _sweep_q_major(pl.program_id(1), nq)
        hm = _head_masks()

        @pl.when(kj == 0)
        def _():
            acc[...] = jnp.zeros_like(acc)

        def step(mask):
            q, k, v, do = q_ref[...] * SCALE, k_ref[...], v_ref[...], do_ref[...]
            cq, lse_rows = cq_ref[...], l_ref[...]
            for h in range(2):
                pt = jnp.exp(_foxt_logits(q, k, cq[h:h + 1, :], ck_ref[h], mask, hm[h]) - lse_rows[h:h + 1, :])
                acc[h] += jnp.sum(pt * _dot_nt(_sel(hm[h], v), do), axis=0, keepdims=True)

        @pl.when(kj < 2 * qi)
        def _():
            step(None)

        @pl.when(kj >= 2 * qi)
        def _():
            step(_causal_t(qi, kj, tq, tk))

        @pl.when(kj == 2 * qi + 1)
        def _():
            for h in range(2):
                d_ref[h:h + 1, :] = acc[h]

    def q_map(p, t):
        return (_sweep_q_major(t, nq)[0], p)

    def kv_map(off):
        return lambda p, t: (_sweep_q_major(t, nq)[1], off + p)

    blk = pl.BlockSpec((tq, LANES), q_map)
    row = pl.BlockSpec((None, 2, tq), lambda p, t: (p, 0, _sweep_q_major(t, nq)[0]))
    return pl.pallas_call(
        body, name=name, grid=(npair, nq * (nq + 1)),
        in_specs=[blk, pl.BlockSpec((tk, LANES), kv_map(npair)), pl.BlockSpec((tk, LANES), kv_map(2 * npair)),
                  blk, row, row, pl.BlockSpec((2, tk, LANES), lambda p, t: (p, _sweep_q_major(t, nq)[1], 0))],
        out_specs=row, out_shape=jax.ShapeDtypeStruct((npair, 2, s), F32),
        scratch_shapes=[pltpu.VMEM((2, 1, tq), F32)],
        compiler_params=_cp("parallel", "arbitrary"))(hb, hb, hb, dcat, lse, c_t3, c_rep)


def _foxt_bwd(hb, dcat, dsum, lse, c_rep, c_t3, name):
    s = hb.shape[0]
    npair = MIX_W // LANES
    tq, tk, nq = _fox_tiles(s)

    def body(q_ref, k_ref, v_ref, do_ref, d_ref, l_ref, cq_ref, ck_ref, dq_ref, dk_ref, dv_ref, dc_ref, dc_s):
        t = pl.program_id(1)
        kj, qi, first, last = _sweep_k_major(t, nq)
        hm = _head_masks()

        @pl.when(first)
        def _():
            dk_ref[...] = jnp.zeros_like(dk_ref)
            dv_ref[...] = jnp.zeros_like(dv_ref)
            dc_s[...] = jnp.zeros_like(dc_s)

        @pl.when(t == 0)
        def _():
            dq_ref[...] = jnp.zeros_like(dq_ref)

        def step(mask):
            q, k, v, do = q_ref[...] * SCALE, k_ref[...], v_ref[...], do_ref[...]
            qt, kt, dot = jnp.transpose(q), jnp.transpose(k), jnp.transpose(do)
            cq, lse_rows, d_rows = cq_ref[...], l_ref[...], d_ref[...]
            dqs, dks, dvs = [], [], []
            for h in range(2):
                rows = slice(h * HEAD_DIM, (h + 1) * HEAD_DIM)
                pt = jnp.exp(_foxt_logits(q, k, cq[h:h + 1, :], ck_ref[h], mask, hm[h]) - lse_rows[h:h + 1, :])
                dst = pt * (_dot_nt(_sel(hm[h], v), do) - d_rows[h:h + 1, :])
                dsb = dst.astype(BF)
                dqs.append(_dot(kt[rows, :], dsb))
                dks.append(_dot_nt(qt[rows, :], dsb))
                dvs.append(_dot_nt(dot[rows, :], pt.astype(BF)))
                dc_s[h] += _fold(dst)
            cols = pl.ds(pl.multiple_of(qi * tq, tq), tq)
            dq_ref[:, cols] += SCALE * jnp.concatenate(dqs, axis=0)
            dk_ref[...] += jnp.concatenate(dks, axis=0)
            dv_ref[...] += jnp.concatenate(dvs, axis=0)

        @pl.when(kj < 2 * qi)
        def _():
            step(None)

        @pl.when(kj >= 2 * qi)
        def _():
            step(_causal_t(qi, kj, tq, tk))

        @pl.when(last)
        def _():
            for h in range(2):
                dc_ref[h:h + 1, :] = -jnp.sum(jnp.transpose(dc_s[h]), axis=0, keepdims=True)

    def kj_of(t):
        return _sweep_k_major(t, nq)[0]

    def qi_of(t):
        return _sweep_k_major(t, nq)[1]

    qblk = pl.BlockSpec((tq, LANES), lambda p, t: (qi_of(t), p))
    row = pl.BlockSpec((None, 2, tq), lambda p, t: (p, 0, qi_of(t)))
    kblk = pl.BlockSpec((LANES, tk), lambda p, t: (p, kj_of(t)))
    rep = pl.BlockSpec((2, tk, LANES), lambda p, t: (p, kj_of(t), 0))
    return pl.pallas_call(
        body, name=name, grid=(npair, nq * (nq + 1)),
        in_specs=[qblk,
                  pl.BlockSpec((tk, LANES), lambda p, t: (kj_of(t), npair + p)),
                  pl.BlockSpec((tk, LANES), lambda p, t: (kj_of(t), 2 * npair + p)),
                  qblk, row, row, row, rep],
        out_specs=[pl.BlockSpec((LANES, s), lambda p, t: (p, 0)), kblk, kblk,
                   pl.BlockSpec((None, 2, tk), lambda p, t: (p, 0, kj_of(t)))],
        out_shape=[jax.ShapeDtypeStruct((MIX_W, s), F32), jax.ShapeDtypeStruct((MIX_W, s), F32),
                   jax.ShapeDtypeStruct((MIX_W, s), F32), jax.ShapeDtypeStruct((npair, 2, s), F32)],
        scratch_shapes=[pltpu.VMEM((2, tk, LANES), F32)],
        compiler_params=_cp("arbitrary", "arbitrary"))(hb, hb, hb, dcat, dsum, lse, c_t3, c_rep)


def _loss_head(y, target, name):
    s, d = y.shape
    ts = _rows(s)

    def body(y_ref, t_ref, dy_ref, l_ref):
        i = pl.program_id(0)
        e = y_ref[...] - t_ref[...]
        dy_ref[...] = e * (1.0 / d)

        @pl.when(i == 0)
        def _():
            l_ref[...] = jnp.zeros_like(l_ref)

        part = jnp.sum(jnp.sum(e * e, axis=1, keepdims=True), axis=0, keepdims=True)
        l_ref[...] += part * (0.5 / d)

    row = pl.BlockSpec((ts, d), lambda i: (i, 0))
    return pl.pallas_call(
        body, name=name, grid=(s // ts,), in_specs=[row, row],
        out_specs=[row, pl.BlockSpec((1, 1), lambda i: (0, 0))],
        out_shape=[jax.ShapeDtypeStruct((s, d), F32), jax.ShapeDtypeStruct((1, 1), F32)],
        compiler_params=_cp("arbitrary"))(y, target)


def _adam_rows(r, c):
    cap = max(8, (1 << 20) // (4 * c))
    if r <= cap:
        return r
    best = None
    for t in range(8, cap + 1, 8):
        if r % t == 0:
            best = t
    return best if best is not None else r


def _reduce_adamw(contribs, w, m, v, name):
    nl = len(contribs)
    nd, r, c = contribs[0].shape
    tr = _adam_rows(r, c)
    bc1 = 1.0 - ADAM_B1 ** ADAM_STEP
    bc2 = 1.0 - ADAM_B2 ** ADAM_STEP

    def body(*refs):
        c_refs = refs[:nl]
        w_ref, m_ref, v_ref, g_ref, d_ref, nm_ref, nv_ref = refs[nl:]
        l = pl.program_id(0)
        for li in range(nl):
            @pl.when(l == li)
            def _(c_ref=c_refs[li]):
                g = c_ref[0].astype(F32)
                for k in range(1, nd):
                    g = g + c_ref[k].astype(F32)
                nm = ADAM_B1 * m_ref[...] + (1.0 - ADAM_B1) * g
                nv = ADAM_B2 * v_ref[...] + (1.0 - ADAM_B2) * (g * g)
                g_ref[...] = g
                nm_ref[...] = nm
                nv_ref[...] = nv
                d_ref[...] = -ADAM_LR * ((nm / bc1) / (jnp.sqrt(nv / bc2) + ADAM_EPS) + ADAM_WD * w_ref[...])

    def c_spec(li):
        return pl.BlockSpec((nd, tr, c), lambda l, i: (0, jnp.where(l == li, i, 0), 0))

    blk = pl.BlockSpec((None, tr, c), lambda l, i: (l, i, 0))
    out = jax.ShapeDtypeStruct((nl, r, c), F32)
    return pl.pallas_call(
        body, name=name, grid=(nl, r // tr),
        in_specs=[c_spec(li) for li in range(nl)] + [blk, blk, blk],
        out_specs=[blk, blk, blk, blk], out_shape=[out, out, out, out],
        compiler_params=_cp("arbitrary", "arbitrary"))(*contribs, w, m, v)


def _mesh_pos():
    return lax.axis_index("x"), lax.axis_index("y"), lax.axis_index("c")


def _peer(pos, k):
    x, y, c = pos
    return (1 - x if k & 4 else x, 1 - y if k & 2 else y, 1 - c if k & 1 else c)


def _linear(pos):
    return 4 * pos[0] + 2 * pos[1] + pos[2]


def _xfer_copies(srcs, lands, send_sems, recv_sems, local_sems, gather):
    pos = _mesh_pos()
    me = _linear(pos)
    local, remote = [], []
    for i, (src, land) in enumerate(zip(srcs, lands)):
        local.append(pltpu.make_async_copy(src if gather else src.at[me], land.at[me], local_sems.at[i]))
        for k in range(1, N_DEV):
            peer = _peer(pos, k)
            remote.append(pltpu.make_async_remote_copy(
                src_ref=src if gather else src.at[_linear(peer)], dst_ref=land.at[me],
                send_sem=send_sems.at[i * (N_DEV - 1) + k - 1], recv_sem=recv_sems.at[i * (N_DEV - 1) + k - 1],
                device_id=peer, device_id_type=MESH_ID))
    return local, remote


_HBM = pl.BlockSpec(memory_space=pltpu.HBM)
_SEM = pl.BlockSpec(memory_space=pltpu.SEMAPHORE)
_EFFECT = pltpu.SideEffectType.DATAFLOW_SIDE_EFFECTING


def _xfer_start(srcs, gather, name, after=()):
    n = len(srcs)
    na = len(after)
    lands = [lax.empty(((N_DEV,) + a.shape) if gather else a.shape, a.dtype) for a in srcs]

    def body(*refs):
        src, land = refs[:n], refs[n:2 * n]
        send_sems, recv_sems, local_sems = refs[2 * n + na:2 * n + na + 3]
        local, remote = _xfer_copies(src, land, send_sems, recv_sems, local_sems, gather)
        for cp in local + remote:
            cp.start()
        refs[-1][...] = jnp.zeros_like(refs[-1])

    nsem = n * (N_DEV - 1)
    out = pl.pallas_call(
        body, name=name,
        out_shape=(pltpu.SemaphoreType.DMA((nsem,)), pltpu.SemaphoreType.DMA((nsem,)), pltpu.SemaphoreType.DMA((n,)),
                   *[pltpu.HBM(a.shape, a.dtype) for a in srcs], *[pltpu.HBM(a.shape, a.dtype) for a in lands],
                   jax.ShapeDtypeStruct((8, LANES), F32)),
        in_specs=[_HBM] * (2 * n) + [pl.BlockSpec(memory_space=pl.ANY)] * na,
        out_specs=(_SEM, _SEM, _SEM, *[_HBM] * (2 * n), pl.BlockSpec(memory_space=pltpu.VMEM)),
        input_output_aliases={i: 3 + i for i in range(2 * n)},
        compiler_params=pltpu.CompilerParams(has_side_effects=_EFFECT))(
            *[pltpu.with_memory_space_constraint(a, pltpu.HBM) for a in srcs],
            *[pltpu.with_memory_space_constraint(a, pltpu.HBM) for a in lands], *after)
    return out[:3], list(out[3:3 + n]), list(out[3 + n:3 + 2 * n]), out[-1]


def _started(handle):
    return handle[3]


def _xfer_wait(handle, after, gather, name):
    sems, srcs, lands, _ = handle
    n = len(srcs)

    def body(*refs):
        src, land = refs[:n], refs[n:2 * n]
        send_sems, recv_sems, local_sems = refs[2 * n:2 * n + 3]
        local, remote = _xfer_copies(src, land, send_sems, recv_sems, local_sems, gather)
        for cp in local:
            cp.wait()
        for cp in remote:
            cp.wait_send()
            cp.wait_recv()

    out = pl.pallas_call(
        body, name=name,
        out_shape=(*[pltpu.HBM(a.shape, a.dtype) for a in srcs], *[pltpu.HBM(a.shape, a.dtype) for a in lands]),
        in_specs=[_HBM] * (2 * n) + [_SEM] * 3 + [pl.BlockSpec(memory_space=pl.ANY)] * len(after),
        out_specs=tuple([_HBM] * (2 * n)), input_output_aliases={i: i for i in range(2 * n)},
        compiler_params=pltpu.CompilerParams(has_side_effects=_EFFECT))(*srcs, *lands, *sems, *after)
    return list(out[n:])


def _cols_full(g):
    nd, r, c = g.shape
    return jnp.transpose(g, (1, 0, 2)).reshape(r, nd * c)


def _cols_split(full):
    r, n = full.shape
    return jnp.transpose(full.reshape(r, N_DEV, n // N_DEV), (1, 0, 2))


def _pack_b_in(w):
    qkv = 3 * MIX_W
    pad = jnp.zeros((w.shape[0], B_IN_PAD - w.shape[1]), w.dtype)
    return jnp.concatenate([w[:, :qkv], w[:, qkv + N_MIX_HEADS:], w[:, qkv:qkv + N_MIX_HEADS], pad], axis=1)


def _unpack_b_in(w):
    qkv = 3 * MIX_W
    return jnp.concatenate([w[:, :qkv], w[:, qkv + MEM_W:qkv + MEM_W + N_MIX_HEADS], w[:, qkv:qkv + MEM_W]], axis=1)


def _to_classes(t, g):
    r = 4 ** g
    s, w = t.shape
    return jnp.transpose(t.reshape(s // r, r, w), (1, 0, 2)).reshape(s, w)


def _from_classes(t, g):
    r = 4 ** g
    s, w = t.shape
    return jnp.transpose(t.reshape(r, s // r, w), (1, 0, 2)).reshape(s, w)


def _group_stack(t):
    return jnp.stack([_to_classes(t[:, g * GROUP_W:(g + 1) * GROUP_W], g) for g in range(N_GROUPS)])


def _group_unstack(t3):
    return jnp.concatenate([_from_classes(t3[g], g) for g in range(N_GROUPS)], axis=1)


def _same_stack(t):
    return jnp.stack([_to_classes(t, g) for g in range(N_GROUPS)])


def _same_unstack(t3):
    return jnp.stack([_from_classes(t3[g], g) for g in range(N_GROUPS)])


def _ffn_forward(x, xb, wgu, get_rest, tag):
    gu, a = _ffn_up(xb, wgu, f"{tag}_up")
    wd4, gain, bias = get_rest(a)
    y, yb, xh, rstd = _mm_res_ln(a, wd4, x, gain, bias, 0.5, f"{tag}_down_ln")
    return y, yb, (xb, gu, a, xh, rstd), wd4


def _ffn_backward(dy, saved, wgu, wd4, gain, tag, after=()):
    xb, gu, a, xh, rstd = saved
    s = xb.shape[0]
    nd, c, d = wgu.shape
    dz, dzb, dgain, dbias = _ln_bwd(dy, xh, rstd, gain, 0.5, f"{tag}_ln_bwd", after)
    dh = _ffn_bwd_act(dzb, wd4, gu, f"{tag}_act_bwd").reshape(nd, s, c)
    dwd = _mm_tn(a, dzb[None], f"{tag}_dwd").reshape(nd, wd4.shape[1] // 2, d)
    dx = _mm_nt(dh, wgu, f"{tag}_dx", res=dz, w_rows_out=False)
    dwgu = _mm_tn(dh, xb[None], f"{tag}_dwgu")
    return dx, dwgu, dwd, dgain, dbias


def _mixer_a_forward(x, xb, memb, w_in, w_kv, w_out, gain, bias, tabs):
    h = _mm_nn(xb, w_in, F32, "a_in", b_rows_out=True)
    hb = _rope_cast([h], tabs, 2 * MIX_W // LANES, "a_rope")
    q3 = _group_stack(hb[:, :MIX_W])
    k3 = _group_stack(hb[:, MIX_W:2 * MIX_W])
    v3 = _group_stack(hb[:, 2 * MIX_W:3 * MIX_W])
    o3, l3 = _band_fwd(q3, k3, v3, "a_band_fwd")
    oa, lt = _band_combine(_same_unstack(o3), _same_unstack(l3), "a_combine")
    kv = _mm_nn(memb, w_kv, BF, "a_mem_kv")
    om, lm = _mem_fwd(hb, 3 * MIX_W // LANES, kv, "a_mem_fwd")
    cat = jnp.concatenate([oa, om], axis=1)
    y, yb, xh, rstd = _mm_res_ln(cat[None], w_out[None], x, gain, bias, 1.0, "a_out_ln")
    return y, yb, (xb, hb, q3, k3, v3, oa, lt, kv, lm, cat, xh, rstd)


def _mixer_a_backward(dy, saved, memb, w_in, w_kv, w_out, gain, tabs_neg, after=()):
    xb, hb, q3, k3, v3, oa, lt, kv, lm, cat, xh, rstd = saved
    dz, dzb, dgain, dbias = _ln_bwd(dy, xh, rstd, gain, 1.0, "a_ln_bwd", after)
    dcat = _mm_nt(dzb[None], w_out[None], "a_dcat", out_dtype=BF)
    dw_out = _mm_tn(cat[None], dzb[None], "a_dwout")[0]
    dqm, dkm, dvm = _mem_bwd(hb, 3 * MIX_W // LANES, kv, dcat, cat, GROUP_W // LANES, lm, "a_mem_bwd")
    dkv = jnp.concatenate([dkm, dvm], axis=1).astype(BF)
    dw_kv = _mm_tn(memb[None], dkv[None], "a_dwkv")[0]
    dq3, dk3, dv3 = _band_bwd(q3, k3, v3, _same_stack(dcat[:, :GROUP_W]), _same_stack(oa), _same_stack(lt),
                              "a_band_bwd")
    dhb = _rope_cast([_group_unstack(dq3), _group_unstack(dk3), _group_unstack(dv3), dqm], tabs_neg,
                     2 * MIX_W // LANES, "a_rope_bwd")
    dw_in = _mm_tn(dhb[None], xb[None], "a_dwin")[0]
    dx = _mm_nt(dhb[None], w_in[None], "a_dx", res=dz, w_rows_out=False)
    return dx, dw_in, dw_kv, dw_out, dgain, dbias


def _pad_rows(t, rows):
    return jnp.concatenate([t, jnp.zeros((rows - t.shape[0], t.shape[1]), t.dtype)], axis=0)


def _pad_cols(t, cols):
    return jnp.concatenate([t, jnp.zeros((t.shape[0], cols - t.shape[1]), t.dtype)], axis=1)


def _mixer_b_forward(x, xb, memb, w_in, fbias, w_kv, w_out, gain, bias, tabs):
    s = x.shape[0]
    h = _mm_nn(xb, w_in, F32, "b_in")
    hb = _rope_cast([h], tabs, 0, "b_cast")
    f0 = 3 * MIX_W + MEM_W
    f_t = _pad_rows(jnp.transpose(h[:, f0:f0 + N_MIX_HEADS]), 16)
    bias16 = _pad_rows(jnp.transpose(fbias), 16)
    c_t = _gate_fwd(f_t, bias16, "b_gate_fwd")
    c_t3 = c_t[:N_MIX_HEADS].reshape(N_MIX_HEADS // 2, 2, s)
    c_rep = jnp.broadcast_to(c_t[:N_MIX_HEADS, :, None], (N_MIX_HEADS, s, LANES))
    ob, lb = _foxt_fwd(hb, c_rep, c_t3, "b_fox_fwd")
    kv = _mm_nn(memb, w_kv, BF, "b_mem_kv")
    om, lm = _mem_fwd(hb, 3 * MIX_W // LANES, kv, "b_mem_fwd")
    cat = jnp.concatenate([ob, om], axis=1)
    y, yb, xh, rstd = _mm_res_ln(cat[None], w_out[None], x, gain, bias, 1.0, "b_out_ln")
    return y, yb, (xb, hb, f_t, bias16, c_rep, c_t3, lb, kv, lm, cat, xh, rstd)


def _mixer_b_backward(dy, saved, memb, w_in, w_kv, w_out, gain, tabs, after=()):
    xb, hb, f_t, bias16, c_rep, c_t3, lb, kv, lm, cat, xh, rstd = saved
    s = xb.shape[0]
    dz, dzb, dgain, dbias = _ln_bwd(dy, xh, rstd, gain, 1.0, "b_ln_bwd", after)
    dcat = _mm_nt(dzb[None], w_out[None], "b_dcat", out_dtype=BF)
    dw_out = _mm_tn(cat[None], dzb[None], "b_dwout")[0]
    dqm, dkm, dvm = _mem_bwd(hb, 3 * MIX_W // LANES, kv, dcat, cat, MIX_W // LANES, lm, "b_mem_bwd")
    dkv = jnp.concatenate([dkm, dvm], axis=1).astype(BF)
    dw_kv = _mm_tn(memb[None], dkv[None], "b_dwkv")[0]
    dsum = _foxt_dsum(hb, dcat, lb, c_rep, c_t3, "b_fox_dsum")
    dq, dk, dv, dc3 = _foxt_bwd(hb, dcat, dsum, lb, c_rep, c_t3, "b_fox_bwd")
    df_t, dfb = _gate_bwd(_pad_rows(dc3.reshape(N_MIX_HEADS, s), 16), f_t, bias16, "b_gate_bwd")
    df = _pad_cols(jnp.transpose(df_t[:N_MIX_HEADS]), B_IN_PAD - 3 * MIX_W - MEM_W)
    dhb = _rope_cast([dq, dk, dv, dqm, df], tabs, 0, "b_cast_bwd", transposed=(0, 1, 2))
    dw_in = _mm_tn(xb[None], dhb[None], "b_dwin")[0]
    dx = _mm_nt(dhb[None], w_in[None], "b_dx", res=dz)
    return dx, dw_in, jnp.transpose(dfb[:N_MIX_HEADS]), dw_kv, dw_out, dgain, dbias


def _stored(t, name):
    return jnp.transpose(t, (0, 2, 1)) if name in ROWS_OUT else t


GATHER_GROUPS = (
    (("ffn1_w_gate_up", 0),),
    (("ffn1_w_down", 0), ("ln_gain", None), ("ln_bias", None)),
    (("a_w_in", 0), ("a_w_out", 0), ("mem_w_kv", 0)),
    (("ffn2_w_gate_up", 0), ("ffn2_w_down", 0)),
    (("ffn1_w_gate_up", 1), ("ffn1_w_down", 1)),
    (("b_w_in", 0), ("b_w_out", 0), ("mem_w_kv", 1)),
    (("ffn2_w_gate_up", 1), ("ffn2_w_down", 1)),
)


def _group_shards(group, params):
    return [t if n in F32_COMM else _stored(t, n)[l].astype(BF) for (n, l), t in zip(group, params)]


def _weight_groups(w):
    return [_group_shards(grp, [w[n] for n, _ in grp]) for grp in GATHER_GROUPS]


def _local_step(x, mem, target, fbias, get_w, put_g):
    s, d = x.shape
    tabs = _rope_tables(s, 1.0)
    tabs_neg = _rope_tables(s, -1.0)
    memb = mem.astype(BF)
    saved, wl = [], []
    cur, curb = x, x.astype(BF)
    ln = []

    def down4(t):
        return t.reshape(N_DEV // 2, -1, d)

    for i in range(DEPTH):
        if i == 0:
            def first_rest(a):
                g = get_w(1, a)
                ln.extend(jnp.transpose(t, (1, 2, 0, 3)).reshape(DEPTH, 3, 1, d) for t in g[1:3])
                return down4(g[0]), ln[0][0, 0], ln[1][0, 0]

            wgu = get_w(0, cur)[0]
            cur, curb, s1, wd = _ffn_forward(cur, curb, wgu, first_rest, "l0_ffn1")
        else:
            g = get_w(3 * i + 1, cur)
            wgu = g[0]
            cur, curb, s1, wd = _ffn_forward(cur, curb, wgu, lambda a, g=g: (down4(g[1]), ln[0][i, 0], ln[1][i, 0]),
                                             f"l{i}_ffn1")
        w1 = (wgu, wd)
        ln_g, ln_b = ln
        g = get_w(3 * i + 2, cur)
        if i == 0:
            wm = (g[0].reshape(-1, d), g[2].reshape(d, -1), _cols_full(g[1]))
            cur, curb, s2 = _mixer_a_forward(cur, curb, memb, wm[0], wm[1], wm[2], ln_g[i, 1], ln_b[i, 1], tabs)
        else:
            wm = (_pack_b_in(g[0].reshape(d, -1)), g[2].reshape(d, -1), g[1].reshape(d, -1))
            cur, curb, s2 = _mixer_b_forward(cur, curb, memb, wm[0], fbias, wm[1], wm[2], ln_g[i, 1], ln_b[i, 1],
                                             tabs)
        g = get_w(3 * i + 3, cur)
        cur, curb, s3, wd = _ffn_forward(cur, curb, g[0], lambda a, g=g: (down4(g[1]), ln_g[i, 2], ln_b[i, 2]),
                                         f"l{i}_ffn2")
        w3 = (g[0], wd)
        saved.append((s1, s2, s3))
        wl.append((w1, wm, w3))

    dy, loss = _loss_head(cur, target, "loss_head")

    dgs = [[None] * 3 for _ in range(DEPTH)]
    dbs = [[None] * 3 for _ in range(DEPTH)]
    sent = ()
    for i in reversed(range(DEPTH)):
        s1, s2, s3 = saved[i]
        w1, wm, w3 = wl[i]
        dy, dgu, dd, dgs[i][2], dbs[i][2] = _ffn_backward(dy, s3, w3[0], w3[1], ln_g[i, 2], f"l{i}_ffn2", sent)
        sent = put_g(3 * i + 2, [dgu, dd])
        if i == 0:
            dy, dw_in, dw_kv, dw_out, dgs[i][1], dbs[i][1] = _mixer_a_backward(
                dy, s2, memb, wm[0], wm[1], wm[2], ln_g[i, 1], tabs_neg, sent)
            sent = put_g(1, [dw_in.reshape(N_DEV, -1, d), _cols_split(dw_out),
                             dw_kv.reshape(N_DEV, d // N_DEV, -1)])
        else:
            dy, dw_in, dfb, dw_kv, dw_out, dgs[i][1], dbs[i][1] = _mixer_b_backward(
                dy, s2, memb, wm[0], wm[1], wm[2], ln_g[i, 1], tabs, sent)
            sent = put_g(4, [_unpack_b_in(dw_in).reshape(N_DEV, d // N_DEV, -1),
                             dw_out.reshape(N_DEV, d // N_DEV, -1), dw_kv.reshape(N_DEV, d // N_DEV, -1),
                             jnp.broadcast_to(dfb[None], (N_DEV,) + dfb.shape)])
        dy, dgu, dd, dgs[i][0], dbs[i][0] = _ffn_backward(dy, s1, w1[0], w1[1], ln_g[i, 0], f"l{i}_ffn1", sent)
        if i == 0:
            ln_pieces = []
            for parts in (dgs, dbs):
                t = jnp.concatenate([parts[a][b] for a in range(DEPTH) for b in range(3)], axis=0)
                ln_pieces.append(jnp.transpose(t.reshape(DEPTH * 3, N_DEV, d // N_DEV), (1, 0, 2)))
            sent = put_g(0, [dgu, dd] + ln_pieces)
        else:
            sent = put_g(3, [dgu, dd])
    return loss, dy


WEIGHTS = ("ffn1_w_gate_up", "ffn1_w_down", "ffn2_w_gate_up", "ffn2_w_down", "ln_gain", "ln_bias", "mem_w_kv",
           "a_w_in", "a_w_out", "b_w_in", "b_forget_bias", "b_w_out")
F32_COMM = ("ln_gain", "ln_bias", "b_forget_bias")
ROWS_OUT = ("ffn1_w_gate_up", "ffn2_w_gate_up", "a_w_in")
GRAD_SLOTS = {
    "ffn1_w_gate_up": [(0, 0), (3, 0)], "ffn1_w_down": [(0, 1), (3, 1)],
    "ffn2_w_gate_up": [(2, 0), (5, 0)], "ffn2_w_down": [(2, 1), (5, 1)],
    "ln_gain": [(0, 2)], "ln_bias": [(0, 3)], "mem_w_kv": [(1, 2), (4, 2)],
    "a_w_in": [(1, 0)], "a_w_out": [(1, 1)], "b_w_in": [(4, 0)], "b_forget_bias": [(4, 3)], "b_w_out": [(4, 1)],
}


def kernel(x, mem, ffn1_w_gate_up, ffn1_w_down, ffn2_w_gate_up, ffn2_w_down, ln_gain, ln_bias, mem_w_kv, a_w_in, a_w_out, b_w_in, b_forget_bias, b_w_out, loss_target, m_ffn1_w_gate_up, m_ffn1_w_down, m_ffn2_w_gate_up, m_ffn2_w_down, m_ln_gain, m_ln_bias, m_mem_w_kv, m_a_w_in, m_a_w_out, m_b_w_in, m_b_forget_bias, m_b_w_out, v_ffn1_w_gate_up, v_ffn1_w_down, v_ffn2_w_gate_up, v_ffn2_w_down, v_ln_gain, v_ln_bias, v_mem_w_kv, v_a_w_in, v_a_w_out, v_b_w_in, v_b_forget_bias, v_b_w_out):
    w = dict(zip(WEIGHTS, (ffn1_w_gate_up, ffn1_w_down, ffn2_w_gate_up, ffn2_w_down, ln_gain, ln_bias, mem_w_kv,
                           a_w_in, a_w_out, b_w_in, b_forget_bias, b_w_out)))
    m = dict(zip(WEIGHTS, (m_ffn1_w_gate_up, m_ffn1_w_down, m_ffn2_w_gate_up, m_ffn2_w_down, m_ln_gain, m_ln_bias,
                           m_mem_w_kv, m_a_w_in, m_a_w_out, m_b_w_in, m_b_forget_bias, m_b_w_out)))
    v = dict(zip(WEIGHTS, (v_ffn1_w_gate_up, v_ffn1_w_down, v_ffn2_w_gate_up, v_ffn2_w_down, v_ln_gain, v_ln_bias,
                           v_mem_w_kv, v_a_w_in, v_a_w_out, v_b_w_in, v_b_forget_bias, v_b_w_out)))

    gathers = []
    for k, grp in enumerate(GATHER_GROUPS):
        params, behind = [w[n] for n, _ in grp], [_started(h) for h in gathers[-1:]]
        if behind:
            params, behind = lax.optimization_barrier((params, behind))
        gathers.append(_xfer_start(_group_shards(grp, params), True, f"gather{k}_start", behind))
    exchanges = {}

    def get_w(k, after):
        behind = [after] + ([_started(h) for h in gathers] if k == 0 else [])
        return _xfer_wait(gathers[k], behind, True, f"gather{k}_wait")

    def put_g(k, pieces):
        exchanges[k] = _xfer_start(pieces, False, f"grads{k}_start")
        return (_started(exchanges[k]),)

    loss, grad_x = _local_step(x[0], mem[0], loss_target[0], b_forget_bias, get_w, put_g)
    loss = lax.psum(loss[0, 0], ("x", "y", "c"))

    outs, landed = {}, {}

    def adamw(names, after):
        for n in names:
            contribs = [landed[g][j] for g, j in GRAD_SLOTS[n]]
            view = (len(contribs),) + contribs[0].shape[1:]
            shape = _stored(w[n], n).shape
            res = _reduce_adamw(contribs, *[_stored(t[n], n).reshape(view) for t in (w, m, v)], f"adamw_{n}")
            outs[n] = [_stored(t.reshape(shape), n) for t in res]
            after = outs[n][0]
        return after

    after = [grad_x, _started(exchanges[0])]
    for k in (5, 4, 3, 2, 1):
        landed[k] = _xfer_wait(exchanges[k], after, False, f"grads{k}_wait")
        after = [landed[k][0]]
    done = adamw(("ffn2_w_gate_up", "ffn2_w_down", "mem_w_kv", "a_w_in", "a_w_out", "b_w_in", "b_forget_bias",
                  "b_w_out"), None)
    landed[0] = _xfer_wait(exchanges[0], [done], False, "grads0_wait")
    adamw(("ffn1_w_gate_up", "ffn1_w_down", "ln_gain", "ln_bias"), None)
    return (loss, grad_x[None], *[outs[n][0] for n in WEIGHTS], *[outs[n][1] for n in WEIGHTS],
            *[outs[n][2] for n in WEIGHTS], *[outs[n][3] for n in WEIGHTS])
```

```python
import functools

import jax
import jax.numpy as jnp
from jax import lax
from jax.experimental import pallas as pl
from jax.experimental.pallas import tpu as pltpu

F32 = jnp.float32
BF = jnp.bfloat16
MESH_ID = pl.DeviceIdType.MESH

N_DEV = 8
DEPTH = 2
HEAD_DIM = 64
LANES = 128
N_MIX_HEADS = 12
N_MEM_HEADS = 4
MIX_W = N_MIX_HEADS * HEAD_DIM
MEM_W = N_MEM_HEADS * HEAD_DIM
N_GROUPS = 3
GROUP_W = MIX_W // N_GROUPS
BLOCK = 128
BAND_SUB = 4
ROT_HALF = 8
ROPE_THETA = 500000.0
ALPHA = (2 * DEPTH) ** 0.25
LN_EPS = 1e-5
SCALE = HEAD_DIM ** -0.5
NEG = -1e30
B_IN_PAD = 2688
ADAM_LR, ADAM_B1, ADAM_B2, ADAM_EPS, ADAM_WD, ADAM_STEP = 0.001, 0.9, 0.999, 1e-08, 0.01, 10
VMEM_LIMIT = 56 * 1024 * 1024


def _cp(*sem):
    return pltpu.CompilerParams(dimension_semantics=sem, vmem_limit_bytes=VMEM_LIMIT)


def _dot(a, b):
    return jnp.dot(a, b, preferred_element_type=F32)


def _dot_nt(a, b):
    return lax.dot_general(a, b, (((1,), (1,)), ((), ())), preferred_element_type=F32)


def _dot_tn(a, b):
    return lax.dot_general(a, b, (((0,), (0,)), ((), ())), preferred_element_type=F32)


def _sigmoid(x):
    return 1.0 / (1.0 + jnp.exp(-x))


def _tile(n, cap=1024):
    if n <= cap:
        return n
    best = LANES
    for t in range(LANES, cap + 1, LANES):
        if n % t == 0:
            best = t
    return best


def _rows(s, cap=512):
    return s if s <= cap else cap


def _mm_nn(a, b, out_dtype, name, b_rows_out=False):
    m, k = a.shape
    n = b.shape[0] if b_rows_out else b.shape[1]
    tm, tn = _rows(m), _tile(n)

    def body(a_ref, b_ref, o_ref):
        prod = _dot_nt(a_ref[...], b_ref[...]) if b_rows_out else _dot(a_ref[...], b_ref[...])
        o_ref[...] = prod.astype(o_ref.dtype)

    b_spec = (pl.BlockSpec((tn, k), lambda j, i: (j, 0)) if b_rows_out
              else pl.BlockSpec((k, tn), lambda j, i: (0, j)))
    return pl.pallas_call(
        body, name=name, grid=(n // tn, m // tm),
        in_specs=[pl.BlockSpec((tm, k), lambda j, i: (i, 0)), b_spec],
        out_specs=pl.BlockSpec((tm, tn), lambda j, i: (i, j)),
        out_shape=jax.ShapeDtypeStruct((m, n), out_dtype),
        compiler_params=_cp("parallel", "parallel"))(a, b)


def _resident(shape, index_map):
    return pl.BlockSpec(shape, index_map, pipeline_mode=pl.Buffered(1))


def _mm_tn(a, b, name, out_dtype=BF):
    na, s, m = a.shape
    nb, _, n = b.shape
    no = max(na, nb)
    tm, tn = _tile(m), _tile(n)

    def body(a_ref, b_ref, o_ref):
        o_ref[...] = _dot_tn(a_ref[...], b_ref[...]).astype(o_ref.dtype)

    def spec(nbatch, width, tile, index_map):
        fixed = nbatch == 1 and width == tile
        return _resident((None, s, tile), index_map) if fixed else pl.BlockSpec((None, s, tile), index_map)

    return pl.pallas_call(
        body, name=name, grid=(no, m // tm, n // tn),
        in_specs=[spec(na, m, tm, lambda j, r, c: (j if na > 1 else 0, 0, r)),
                  spec(nb, n, tn, lambda j, r, c: (j if nb > 1 else 0, 0, c))],
        out_specs=pl.BlockSpec((None, tm, tn), lambda j, r, c: (j, r, c)),
        out_shape=jax.ShapeDtypeStruct((no, m, n), out_dtype),
        compiler_params=_cp("parallel", "parallel", "parallel"))(a, b)


def _mm_nt(dh, w, name, res=None, out_dtype=F32, w_rows_out=True):
    nc, s, kc = dh.shape
    d = w.shape[1] if w_rows_out else w.shape[2]
    ts = _rows(s)
    has_res = res is not None
    mm = _dot_nt if w_rows_out else _dot

    def body(*refs):
        if has_res:
            dh_ref, w_ref, r_ref, o_ref = refs
        else:
            dh_ref, w_ref, o_ref = refs
        out = mm(dh_ref[0], w_ref[0])
        for j in range(1, nc):
            out = out + mm(dh_ref[j], w_ref[j])
        if has_res:
            out = out + ALPHA * r_ref[...]
        o_ref[...] = out.astype(o_ref.dtype)

    in_specs = [pl.BlockSpec((nc, ts, kc), lambda i: (0, i, 0)), _resident(w.shape, lambda i: (0, 0, 0))]
    args = [dh, w]
    if has_res:
        in_specs.append(pl.BlockSpec((ts, d), lambda i: (i, 0)))
        args.append(res)
    return pl.pallas_call(
        body, name=name, grid=(s // ts,), in_specs=in_specs,
        out_specs=pl.BlockSpec((ts, d), lambda i: (i, 0)),
        out_shape=jax.ShapeDtypeStruct((s, d), out_dtype),
        compiler_params=_cp("parallel"))(*args)


def _mm_res_ln(a, w, x, gain, bias, fscale, name):
    nc, s, kc = a.shape
    d = w.shape[2]
    ts = _rows(s)

    def body(a_ref, w_ref, x_ref, g_ref, b_ref, y_ref, yb_ref, xh_ref, r_ref):
        f = _dot(a_ref[0], w_ref[0])
        for j in range(1, nc):
            f = f + _dot(a_ref[j], w_ref[j])
        z = ALPHA * x_ref[...] + fscale * f
        mu = jnp.mean(z, axis=-1, keepdims=True)
        zc = z - mu
        var = jnp.mean(zc * zc, axis=-1, keepdims=True)
        r = lax.rsqrt(var + LN_EPS)
        xh = zc * r
        y = xh * g_ref[...] + b_ref[...]
        y_ref[...] = y
        yb_ref[...] = y.astype(BF)
        xh_ref[...] = xh
        r_ref[...] = r

    row = pl.BlockSpec((ts, d), lambda i: (i, 0))
    vec = pl.BlockSpec((1, d), lambda i: (0, 0))
    return pl.pallas_call(
        body, name=name, grid=(s // ts,),
        in_specs=[pl.BlockSpec((nc, ts, kc), lambda i: (0, i, 0)), _resident((nc, kc, d), lambda i: (0, 0, 0)),
                  row, vec, vec],
        out_specs=[row, row, row, pl.BlockSpec((ts, 1), lambda i: (i, 0))],
        out_shape=[jax.ShapeDtypeStruct((s, d), F32), jax.ShapeDtypeStruct((s, d), BF),
                   jax.ShapeDtypeStruct((s, d), F32), jax.ShapeDtypeStruct((s, 1), F32)],
        compiler_params=_cp("parallel"))(a, w, x, gain, bias)


def _ln_bwd(dy, xh, rstd, gain, fscale, name, after=()):
    s, d = dy.shape
    ts = _rows(s)
    na = len(after)

    def body(*refs):
        dy_ref, xh_ref, r_ref, g_ref = refs[:4]
        dz_ref, dzb_ref, dg_ref, db_ref = refs[4 + na:]
        i = pl.program_id(0)
        dyv = dy_ref[...]
        xhv = xh_ref[...]
        dxh = dyv * g_ref[...]
        m1 = jnp.mean(dxh, axis=-1, keepdims=True)
        m2 = jnp.mean(dxh * xhv, axis=-1, keepdims=True)
        dz = r_ref[...] * (dxh - m1 - xhv * m2)
        dz_ref[...] = dz
        dzb_ref[...] = (fscale * dz).astype(BF)

        @pl.when(i == 0)
        def _():
            dg_ref[...] = jnp.zeros_like(dg_ref)
            db_ref[...] = jnp.zeros_like(db_ref)

        dg_ref[...] += jnp.sum(dyv * xhv, axis=0, keepdims=True)
        db_ref[...] += jnp.sum(dyv, axis=0, keepdims=True)

    row = pl.BlockSpec((ts, d), lambda i: (i, 0))
    vec = pl.BlockSpec((1, d), lambda i: (0, 0))
    return pl.pallas_call(
        body, name=name, grid=(s // ts,),
        in_specs=[row, row, pl.BlockSpec((ts, 1), lambda i: (i, 0)), vec] + [pl.BlockSpec(memory_space=pl.ANY)] * na,
        out_specs=[row, row, vec, vec],
        out_shape=[jax.ShapeDtypeStruct((s, d), F32), jax.ShapeDtypeStruct((s, d), BF),
                   jax.ShapeDtypeStruct((1, d), F32), jax.ShapeDtypeStruct((1, d), F32)],
        compiler_params=_cp("arbitrary"))(dy, xh, rstd, gain, *after)


def _ffn_up(xb, wgu, name):
    s, d = xb.shape
    c = wgu.shape[1]
    nch = wgu.shape[0] // 2
    ts = _rows(s, 1024)
    w4 = wgu.reshape(2, nch, c, d)

    def body(x_ref, w_ref, gu_ref, a_ref):
        x = x_ref[...]
        g = _dot_nt(x, w_ref[0])
        u = _dot_nt(x, w_ref[1])
        sg = _sigmoid(g)
        t = g * sg
        gu_ref[0] = (u * (sg * (1.0 + g - t))).astype(BF)
        gu_ref[1] = t.astype(BF)
        a_ref[...] = (t * u).astype(BF)

    return pl.pallas_call(
        body, name=name, grid=(nch, s // ts),
        in_specs=[pl.BlockSpec((ts, d), lambda j, i: (i, 0)),
                  pl.BlockSpec((2, None, c, d), lambda j, i: (0, j, 0, 0))],
        out_specs=[pl.BlockSpec((2, None, ts, c), lambda j, i: (0, j, i, 0)),
                   pl.BlockSpec((None, ts, c), lambda j, i: (j, i, 0))],
        out_shape=[jax.ShapeDtypeStruct((2, nch, s, c), BF), jax.ShapeDtypeStruct((nch, s, c), BF)],
        compiler_params=_cp("parallel", "parallel"))(xb, w4)


def _ffn_bwd_act(dzb, wd4, gu, name):
    s, d = dzb.shape
    nch, c = wd4.shape[0], wd4.shape[1]
    ts = _rows(s, 1024)

    def body(dz_ref, w_ref, gu_ref, dh_ref):
        da = _dot_nt(dz_ref[...], w_ref[...])
        dh_ref[0] = (da * gu_ref[0].astype(F32)).astype(BF)
        dh_ref[1] = (da * gu_ref[1].astype(F32)).astype(BF)

    return pl.pallas_call(
        body, name=name, grid=(nch, s // ts),
        in_specs=[pl.BlockSpec((ts, d), lambda j, i: (i, 0)),
                  pl.BlockSpec((None, c, d), lambda j, i: (j, 0, 0)),
                  pl.BlockSpec((2, None, ts, c), lambda j, i: (0, j, i, 0))],
        out_specs=pl.BlockSpec((2, None, ts, c), lambda j, i: (0, j, i, 0)),
        out_shape=jax.ShapeDtypeStruct((2, nch, s, c), BF),
        compiler_params=_cp("parallel", "parallel"))(dzb, wd4, gu)


def _rope_tables(s, sign):
    pos = jnp.arange(s, dtype=F32)
    inv_freq = 1.0 / (ROPE_THETA ** (jnp.arange(ROT_HALF, dtype=F32) / ROT_HALF))
    ang = pos[:, None] * inv_freq[None, :]
    cos, sin = jnp.cos(ang), jnp.sin(ang) * sign
    one = jnp.ones((s, HEAD_DIM - 2 * ROT_HALF), F32)
    zero = jnp.zeros((s, HEAD_DIM - 2 * ROT_HALF), F32)
    zh = jnp.zeros((s, ROT_HALF), F32)
    cos_f = jnp.concatenate([cos, cos, one], axis=1)
    sin_a = jnp.concatenate([-sin, zh, zero], axis=1)
    sin_b = jnp.concatenate([zh, sin, zero], axis=1)
    rep = LANES // HEAD_DIM
    return tuple(jnp.tile(t, (1, rep)) for t in (cos_f, sin_a, sin_b))


def _rope_cast(parts, tabs, n_rope, name, transposed=()):
    s = tabs[0].shape[0]
    flip = [i in transposed for i in range(len(parts))]
    widths = [p.shape[0] if f else p.shape[1] for p, f in zip(parts, flip)]
    n = sum(widths)
    npart = len(parts)
    ts = _rows(s, 256)

    def body(*refs):
        part_refs = refs[:npart]
        c_ref, sa_ref, sb_ref, o_ref = refs[npart:]
        col = 0
        for ref, w, f in zip(part_refs, widths, flip):
            for j in range(w // LANES):
                if f:
                    t = jnp.transpose(ref[j * LANES:(j + 1) * LANES, :])
                else:
                    t = ref[:, j * LANES:(j + 1) * LANES]
                if col < n_rope:
                    t = (t * c_ref[...] + pltpu.roll(t, LANES - ROT_HALF, 1) * sa_ref[...]
                         + pltpu.roll(t, ROT_HALF, 1) * sb_ref[...])
                o_ref[:, col * LANES:(col + 1) * LANES] = t.astype(BF)
                col += 1

    tab = pl.BlockSpec((ts, LANES), lambda i: (i, 0))
    return pl.pallas_call(
        body, name=name, grid=(s // ts,),
        in_specs=[pl.BlockSpec((w, ts), lambda i: (0, i)) if f else pl.BlockSpec((ts, w), lambda i: (i, 0))
                  for w, f in zip(widths, flip)] + [tab, tab, tab],
        out_specs=pl.BlockSpec((ts, n), lambda i: (i, 0)),
        out_shape=jax.ShapeDtypeStruct((s, n), BF),
        compiler_params=_cp("parallel"))(*parts, *tabs)


def _head_masks():
    lane = lax.broadcasted_iota(jnp.int32, (1, LANES), 1)
    return [lane < HEAD_DIM, lane >= HEAD_DIM]


def _sel(mask, v):
    return jnp.where(mask, v, jnp.zeros_like(v))


def _pick(mask, wide, fill):
    return jnp.max(jnp.where(mask, wide, fill), axis=1, keepdims=True)


def _band_masks(has_other, prev):
    qi = lax.broadcasted_iota(jnp.int32, (BLOCK, BLOCK), 0)
    kj = lax.broadcasted_iota(jnp.int32, (BLOCK, BLOCK), 1)
    if prev:
        return kj >= qi + jnp.where(has_other, 0, BLOCK)
    return kj <= qi


def _band_fwd(q3, k3, v3, name):
    ng, s, w = q3.shape
    nb = s // BLOCK
    npair = w // LANES
    nsub = BAND_SUB
    tile = nsub * BLOCK

    def body(q_ref, kc_ref, kp_ref, vc_ref, vp_ref, o_ref, l_ref):
        g = pl.program_id(0)
        t = pl.program_id(2)
        nbl = jnp.right_shift(nb, 2 * g)
        mc = _band_masks(None, False)
        hm = _head_masks()
        for i in range(nsub):
            rows = slice(i * BLOCK, (i + 1) * BLOCK)
            has_prev = jnp.bitwise_and(t * nsub + i, nbl - 1) != 0
            mp = _band_masks(has_prev, True)
            q, kc, vc = q_ref[rows, :], kc_ref[rows, :], vc_ref[rows, :]
            if i == 0:
                kp, vp = kp_ref[...], vp_ref[...]
            else:
                prev = slice((i - 1) * BLOCK, i * BLOCK)
                kp, vp = kc_ref[prev, :], vc_ref[prev, :]
            o = jnp.zeros((BLOCK, LANES), F32)
            lse_w = jnp.zeros((BLOCK, LANES), F32)
            for h in range(2):
                qh = _sel(hm[h], q)
                sc = jnp.where(mc, _dot_nt(qh, kc) * SCALE, NEG)
                sp = jnp.where(mp, _dot_nt(qh, kp) * SCALE, NEG)
                m = jnp.maximum(jnp.max(sc, axis=1, keepdims=True), jnp.max(sp, axis=1, keepdims=True))
                pc = jnp.exp(sc - m)
                pp = jnp.exp(sp - m)
                l = jnp.sum(pc, axis=1, keepdims=True) + jnp.sum(pp, axis=1, keepdims=True)
                oh = _dot(pc.astype(BF), _sel(hm[h], vc)) + _dot(pp.astype(BF), _sel(hm[h], vp))
                o = o + oh / l
                lse_w = jnp.where(hm[h], m + jnp.log(l), lse_w)
            o_ref[rows, :] = o
            l_ref[rows, :] = lse_w

    cur = pl.BlockSpec((None, tile, LANES), lambda g, p, t: (g, t, p))
    prv = pl.BlockSpec((None, BLOCK, LANES), lambda g, p, t: (g, jnp.maximum(t * nsub - 1, 0), p))
    return pl.pallas_call(
        body, name=name, grid=(ng, npair, nb // nsub),
        in_specs=[cur, cur, prv, cur, prv], out_specs=[cur, cur],
        out_shape=[jax.ShapeDtypeStruct((ng, s, w), F32), jax.ShapeDtypeStruct((ng, s, w), F32)],
        compiler_params=_cp("parallel", "parallel", "parallel"))(q3, k3, k3, v3, v3)


def _band_combine(o3, l3, name):
    ng, s, w = o3.shape
    ts = _rows(s)

    def body(o_ref, l_ref, oa_ref, lt_ref):
        ls = [l_ref[g] for g in range(ng)]
        m = functools.reduce(jnp.maximum, ls)
        es = [jnp.exp(l - m) for l in ls]
        den = functools.reduce(lambda a, b: a + b, es)
        num = functools.reduce(lambda a, b: a + b, [es[g] * o_ref[g] for g in range(ng)])
        oa_ref[...] = (num / den).astype(BF)
        lt_ref[...] = m + jnp.log(den)

    blk3 = pl.BlockSpec((ng, ts, w), lambda i: (0, i, 0))
    blk = pl.BlockSpec((ts, w), lambda i: (i, 0))
    return pl.pallas_call(
        body, name=name, grid=(s // ts,), in_specs=[blk3, blk3], out_specs=[blk, blk],
        out_shape=[jax.ShapeDtypeStruct((s, w), BF), jax.ShapeDtypeStruct((s, w), F32)],
        compiler_params=_cp("parallel"))(o3, l3)


def _band_bwd(q3, k3, v3, do3, oa3, lt3, name):
    ng, s, w = q3.shape
    nb = s // BLOCK
    npair = w // LANES
    nsub = BAND_SUB
    tile = nsub * BLOCK

    def body(q_ref, qn_ref, kc_ref, kp_ref, vc_ref, vp_ref, do_ref, don_ref, oa_ref, oan_ref, lt_ref, ltn_ref,
             dq_ref, dk_ref, dv_ref):
        g = pl.program_id(0)
        t = pl.program_id(2)
        nbl = jnp.right_shift(nb, 2 * g)
        mc = _band_masks(None, False)
        hm = _head_masks()

        def block(ref, edge_ref, i):
            if i < 0 or i >= nsub:
                return edge_ref[...]
            return ref[i * BLOCK:(i + 1) * BLOCK, :]

        for i in range(nsub):
            b = t * nsub + i
            mp = _band_masks(jnp.bitwise_and(b, nbl - 1) != 0, True)
            mn = _band_masks(jnp.bitwise_and(b + 1, nbl - 1) != 0, True)
            q, qn = block(q_ref, None, i), block(q_ref, qn_ref, i + 1)
            kc, kp = block(kc_ref, None, i), block(kc_ref, kp_ref, i - 1)
            vc, vp = block(vc_ref, None, i), block(vc_ref, vp_ref, i - 1)
            do, don = block(do_ref, None, i), block(do_ref, don_ref, i + 1)
            dd = do.astype(F32) * block(oa_ref, None, i).astype(F32)
            ddn = don.astype(F32) * block(oa_ref, oan_ref, i + 1).astype(F32)
            lt, ltn = block(lt_ref, None, i), block(lt_ref, ltn_ref, i + 1)
            dq = jnp.zeros((BLOCK, LANES), F32)
            dk = jnp.zeros((BLOCK, LANES), F32)
            dv = jnp.zeros((BLOCK, LANES), F32)
            for h in range(2):
                qh, doh = _sel(hm[h], q), _sel(hm[h], do)
                qnh, donh = _sel(hm[h], qn), _sel(hm[h], don)
                kch, kph = _sel(hm[h], kc), _sel(hm[h], kp)
                lse = _pick(hm[h], lt, NEG)
                lsen = _pick(hm[h], ltn, NEG)
                dsum = jnp.sum(_sel(hm[h], dd), axis=1, keepdims=True)
                dsumn = jnp.sum(_sel(hm[h], ddn), axis=1, keepdims=True)
                pc = jnp.exp(jnp.where(mc, _dot_nt(qh, kc) * SCALE, NEG) - lse)
                pp = jnp.exp(jnp.where(mp, _dot_nt(qh, kp) * SCALE, NEG) - lse)
                dsc = pc * (_dot_nt(doh, vc) - dsum)
                dsp = pp * (_dot_nt(doh, vp) - dsum)
                dq = dq + SCALE * (_dot(dsc.astype(BF), kch) + _dot(dsp.astype(BF), kph))
                pn = jnp.exp(jnp.where(mn, _dot_nt(qnh, kc) * SCALE, NEG) - lsen)
                dsn = pn * (_dot_nt(donh, vc) - dsumn)
                dk = dk + SCALE * (_dot_tn(dsc.astype(BF), qh) + _dot_tn(dsn.astype(BF), qnh))
                dv = dv + _dot_tn(pc.astype(BF), doh) + _dot_tn(pn.astype(BF), donh)
            rows = slice(i * BLOCK, (i + 1) * BLOCK)
            dq_ref[rows, :] = dq
            dk_ref[rows, :] = dk
            dv_ref[rows, :] = dv

    cur = pl.BlockSpec((None, tile, LANES), lambda g, p, t: (g, t, p))
    prv = pl.BlockSpec((None, BLOCK, LANES), lambda g, p, t: (g, jnp.maximum(t * nsub - 1, 0), p))
    nxt = pl.BlockSpec((None, BLOCK, LANES), lambda g, p, t: (g, jnp.minimum(t * nsub + nsub, nb - 1), p))
    out = jax.ShapeDtypeStruct((ng, s, w), F32)
    return pl.pallas_call(
        body, name=name, grid=(ng, npair, nb // nsub),
        in_specs=[cur, nxt, cur, prv, cur, prv, cur, nxt, cur, nxt, cur, nxt],
        out_specs=[cur, cur, cur], out_shape=[out, out, out],
        compiler_params=_cp("parallel", "parallel", "parallel"))(
            q3, q3, k3, k3, v3, v3, do3, do3, oa3, oa3, lt3, lt3)


def _mem_fwd(hb, q_blk0, kv, name):
    s = hb.shape[0]
    m = kv.shape[0]
    tq = _rows(s)
    npair = MEM_W // LANES

    def body(q_ref, k_ref, v_ref, o_ref, l_ref):
        q, k, v = q_ref[...], k_ref[...], v_ref[...]
        hm = _head_masks()
        o = jnp.zeros((tq, LANES), F32)
        lse_w = jnp.zeros((tq, LANES), F32)
        for h in range(2):
            sc = _dot_nt(_sel(hm[h], q), k) * SCALE
            mx = jnp.max(sc, axis=1, keepdims=True)
            p = jnp.exp(sc - mx)
            l = jnp.sum(p, axis=1, keepdims=True)
            o = o + _dot(p.astype(BF), _sel(hm[h], v)) / l
            lse_w = jnp.where(hm[h], mx + jnp.log(l), lse_w)
        o_ref[...] = o.astype(BF)
        l_ref[...] = lse_w

    blk = pl.BlockSpec((tq, LANES), lambda p, i: (i, p))
    return pl.pallas_call(
        body, name=name, grid=(npair, s // tq),
        in_specs=[pl.BlockSpec((tq, LANES), lambda p, i: (i, q_blk0 + p)),
                  pl.BlockSpec((m, LANES), lambda p, i: (0, p)),
                  pl.BlockSpec((m, LANES), lambda p, i: (0, npair + p))],
        out_specs=[blk, blk],
        out_shape=[jax.ShapeDtypeStruct((s, MEM_W), BF), jax.ShapeDtypeStruct((s, MEM_W), F32)],
        compiler_params=_cp("parallel", "parallel"))(hb, kv, kv)


def _mem_bwd(hb, q_blk0, kv, dcat, cat, o_blk0, lse, name):
    s = hb.shape[0]
    m = kv.shape[0]
    tq = _rows(s)
    npair = MEM_W // LANES

    def body(q_ref, k_ref, v_ref, do_ref, o_ref, l_ref, dq_ref, dk_ref, dv_ref):
        i = pl.program_id(1)

        @pl.when(i == 0)
        def _():
            dk_ref[...] = jnp.zeros_like(dk_ref)
            dv_ref[...] = jnp.zeros_like(dv_ref)

        q, k, v, do = q_ref[...], k_ref[...], v_ref[...], do_ref[...]
        dd = do.astype(F32) * o_ref[...].astype(F32)
        lt = l_ref[...]
        hm = _head_masks()
        dq = jnp.zeros((tq, LANES), F32)
        dk = jnp.zeros((m, LANES), F32)
        dv = jnp.zeros((m, LANES), F32)
        for h in range(2):
            qh, doh = _sel(hm[h], q), _sel(hm[h], do)
            p = jnp.exp(_dot_nt(qh, k) * SCALE - _pick(hm[h], lt, NEG))
            ds = p * (_dot_nt(doh, v) - jnp.sum(_sel(hm[h], dd), axis=1, keepdims=True))
            dq = dq + SCALE * _dot(ds.astype(BF), _sel(hm[h], k))
            dk = dk + SCALE * _dot_tn(ds.astype(BF), qh)
            dv = dv + _dot_tn(p.astype(BF), doh)
        dq_ref[...] = dq
        dk_ref[...] += dk
        dv_ref[...] += dv

    row = pl.BlockSpec((tq, LANES), lambda p, i: (i, p))
    orow = pl.BlockSpec((tq, LANES), lambda p, i: (i, o_blk0 + p))
    acc = pl.BlockSpec((m, LANES), lambda p, i: (0, p))
    return pl.pallas_call(
        body, name=name, grid=(npair, s // tq),
        in_specs=[pl.BlockSpec((tq, LANES), lambda p, i: (i, q_blk0 + p)),
                  pl.BlockSpec((m, LANES), lambda p, i: (0, p)),
                  pl.BlockSpec((m, LANES), lambda p, i: (0, npair + p)), orow, orow, row],
        out_specs=[row, acc, acc],
        out_shape=[jax.ShapeDtypeStruct((s, MEM_W), F32), jax.ShapeDtypeStruct((m, MEM_W), F32),
                   jax.ShapeDtypeStruct((m, MEM_W), F32)],
        compiler_params=_cp("parallel", "arbitrary"))(hb, kv, kv, dcat, cat, lse)


def _gate_fwd(f_t, bias, name):
    hp, s = f_t.shape
    nblk = s // LANES

    def body(f_ref, b_ref, c_ref):
        lane = lax.broadcasted_iota(jnp.int32, (hp, LANES), 1)

        def step(i, carry):
            off = pl.multiple_of(i * LANES, LANES)
            x = f_ref[:, pl.ds(off, LANES)] + b_ref[...]
            acc = jnp.minimum(x, 0.0) - jnp.log(1.0 + jnp.exp(-jnp.abs(x)))
            sh = 1
            while sh < LANES:
                acc = acc + jnp.where(lane >= sh, pltpu.roll(acc, sh, 1), 0.0)
                sh *= 2
            acc = acc + carry
            c_ref[:, pl.ds(off, LANES)] = acc
            return acc[:, LANES - 1:LANES]

        lax.fori_loop(0, nblk, step, jnp.zeros((hp, 1), F32))

    vm = pl.BlockSpec(memory_space=pltpu.VMEM)
    return pl.pallas_call(body, name=name, in_specs=[vm, vm], out_specs=vm,
                          out_shape=jax.ShapeDtypeStruct((hp, s), F32),
                          compiler_params=pltpu.CompilerParams(vmem_limit_bytes=VMEM_LIMIT))(f_t, bias)


def _gate_bwd(dc_t, f_t, bias, name):
    hp, s = f_t.shape
    nblk = s // LANES

    def body(dc_ref, f_ref, b_ref, df_ref, db_ref):
        lane = lax.broadcasted_iota(jnp.int32, (hp, LANES), 1)

        def step(t, carry):
            suffix, dbias = carry
            off = pl.multiple_of((nblk - 1 - t) * LANES, LANES)
            acc = dc_ref[:, pl.ds(off, LANES)]
            sh = 1
            while sh < LANES:
                acc = acc + jnp.where(lane < LANES - sh, pltpu.roll(acc, LANES - sh, 1), 0.0)
                sh *= 2
            acc = acc + suffix
            x = f_ref[:, pl.ds(off, LANES)] + b_ref[...]
            df = acc * _sigmoid(-x)
            df_ref[:, pl.ds(off, LANES)] = df
            return acc[:, 0:1], dbias + jnp.sum(df, axis=1, keepdims=True)

        _, dbias = lax.fori_loop(0, nblk, step, (jnp.zeros((hp, 1), F32), jnp.zeros((hp, 1), F32)))
        db_ref[...] = dbias

    vm = pl.BlockSpec(memory_space=pltpu.VMEM)
    return pl.pallas_call(body, name=name, in_specs=[vm, vm, vm], out_specs=[vm, vm],
                          out_shape=[jax.ShapeDtypeStruct((hp, s), F32), jax.ShapeDtypeStruct((hp, 1), F32)],
                          compiler_params=pltpu.CompilerParams(vmem_limit_bytes=VMEM_LIMIT))(dc_t, f_t, bias)


def _wide(rep, width):
    return jnp.tile(rep, (1, width // LANES))


def _fold(t):
    part = t[:, :LANES]
    for c in range(1, t.shape[1] // LANES):
        part = part + t[:, c * LANES:(c + 1) * LANES]
    return part


def _fox_logits(q, k, cq_rep, ck_row, mask, hmask):
    s = _dot_nt(_sel(hmask, q), k) + (_wide(cq_rep, ck_row.shape[1]) - ck_row)
    if mask is not None:
        s = jnp.where(mask, s, NEG)
    return s


def _diag_mask(t):
    return lax.broadcasted_iota(jnp.int32, (t, t), 1) <= lax.broadcasted_iota(jnp.int32, (t, t), 0)


def _fox_fwd(hb, c_rep, c_t3, name):
    s = hb.shape[0]
    npair = MIX_W // LANES
    tq = tk = _rows(s)
    nq = s // tq

    def body(q_ref, k_ref, v_ref, cq_ref, ck_ref, o_ref, l_ref, m_s, l_s, acc):
        qi = pl.program_id(1)
        kj = pl.program_id(2)
        hm = _head_masks()

        @pl.when(kj == 0)
        def _():
            m_s[...] = jnp.full_like(m_s, NEG)
            l_s[...] = jnp.zeros_like(l_s)
            acc[...] = jnp.zeros_like(acc)

        def step(mask):
            q, k, v = q_ref[...] * SCALE, k_ref[...], v_ref[...]
            ck = ck_ref[...]
            for h in range(2):
                sc = _fox_logits(q, k, cq_ref[h], ck[h:h + 1, :], mask, hm[h])
                m_old = m_s[h]
                m_new = jnp.maximum(m_old, jnp.max(sc, axis=1, keepdims=True))
                pr = jnp.exp(sc - _wide(m_new, tk))
                corr = jnp.exp(m_old - m_new)
                l_s[h] = l_s[h] * corr + _fold(pr)
                acc[h] = acc[h] * corr + _dot(pr.astype(BF), _sel(hm[h], v))
                m_s[h] = m_new

        @pl.when(kj < qi)
        def _():
            step(None)

        @pl.when(kj == qi)
        def _():
            step(_diag_mask(tq))
            outs = []
            for h in range(2):
                den = jnp.sum(l_s[h], axis=1, keepdims=True)
                outs.append(acc[h] / den)
                l_ref[h] = m_s[h] + jnp.log(den)
            o_ref[...] = jnp.where(hm[0], outs[0], outs[1]).astype(BF)

    def kv_map(off):
        return lambda p, i, j: (jnp.minimum(j, i), off + p)

    blk = pl.BlockSpec((tq, LANES), lambda p, i, j: (i, p))
    return pl.pallas_call(
        body, name=name, grid=(npair, nq, nq),
        in_specs=[blk, pl.BlockSpec((tk, LANES), kv_map(npair)), pl.BlockSpec((tk, LANES), kv_map(2 * npair)),
                  pl.BlockSpec((2, tq, LANES), lambda p, i, j: (p, i, 0)),
                  pl.BlockSpec((None, 2, tk), lambda p, i, j: (p, 0, jnp.minimum(j, i)))],
        out_specs=[blk, pl.BlockSpec((2, tq, LANES), lambda p, i, j: (p, i, 0))],
        out_shape=[jax.ShapeDtypeStruct((s, MIX_W), BF), jax.ShapeDtypeStruct((2 * npair, s, LANES), F32)],
        scratch_shapes=[pltpu.VMEM((2, tq, LANES), F32), pltpu.VMEM((2, tq, LANES), F32),
                        pltpu.VMEM((2, tq, LANES), F32)],
        compiler_params=_cp("parallel", "parallel", "arbitrary"))(hb, hb, hb, c_rep, c_t3)


def _fox_dsum(hb, dcat, lse, c_rep, c_t3, name):
    s = hb.shape[0]
    npair = MIX_W // LANES
    tq = tk = _rows(s)
    nq = s // tq

    def body(q_ref, k_ref, v_ref, do_ref, l_ref, cq_ref, ck_ref, d_ref, acc):
        qi = pl.program_id(1)
        kj = pl.program_id(2)
        hm = _head_masks()

        @pl.when(kj == 0)
        def _():
            acc[...] = jnp.zeros_like(acc)

        def step(mask):
            q, k, v, do = q_ref[...] * SCALE, k_ref[...], v_ref[...], do_ref[...]
            ck = ck_ref[...]
            for h in range(2):
                pr = jnp.exp(_fox_logits(q, k, cq_ref[h], ck[h:h + 1, :], mask, hm[h]) - _wide(l_ref[h], tk))
                acc[h] += _fold(pr * _dot_nt(_sel(hm[h], do), v))

        @pl.when(kj < qi)
        def _():
            step(None)

        @pl.when(kj == qi)
        def _():
            step(_diag_mask(tq))
            for h in range(2):
                d_ref[h] = jnp.broadcast_to(jnp.sum(acc[h], axis=1, keepdims=True), (tq, LANES))

    def kv_map(off):
        return lambda p, i, j: (jnp.minimum(j, i), off + p)

    blk = pl.BlockSpec((tq, LANES), lambda p, i, j: (i, p))
    rep = pl.BlockSpec((2, tq, LANES), lambda p, i, j: (p, i, 0))
    return pl.pallas_call(
        body, name=name, grid=(npair, nq, nq),
        in_specs=[blk, pl.BlockSpec((tk, LANES), kv_map(npair)), pl.BlockSpec((tk, LANES), kv_map(2 * npair)),
                  blk, rep, rep, pl.BlockSpec((None, 2, tk), lambda p, i, j: (p, 0, jnp.minimum(j, i)))],
        out_specs=rep, out_shape=jax.ShapeDtypeStruct((2 * npair, s, LANES), F32),
        scratch_shapes=[pltpu.VMEM((2, tq, LANES), F32)],
        compiler_params=_cp("parallel", "parallel", "arbitrary"))(hb, hb, hb, dcat, lse, c_rep, c_t3)


def _fox_bwd(hb, dcat, dsum, lse, c_rep, c_t3, name):
    s = hb.shape[0]
    npair = MIX_W // LANES
    tq = tk = _rows(s)
    nq = s // tq

    def body(q_ref, k_ref, v_ref, do_ref, d_ref, l_ref, cq_ref, ck_ref, dq_ref, dk_ref, dv_ref, dc_ref):
        kj = pl.program_id(1)
        qi = pl.program_id(2)
        hm = _head_masks()

        @pl.when(qi == 0)
        def _():
            dk_ref[...] = jnp.zeros_like(dk_ref)
            dv_ref[...] = jnp.zeros_like(dv_ref)
            dc_ref[...] = jnp.zeros_like(dc_ref)

        @pl.when((qi == 0) & (kj == 0))
        def _():
            dq_ref[...] = jnp.zeros_like(dq_ref)

        def step(mask):
            q, k, v, do = q_ref[...] * SCALE, k_ref[...], v_ref[...], do_ref[...]
            ck = ck_ref[...]
            dq = jnp.zeros((tq, LANES), F32)
            dk = jnp.zeros((tk, LANES), F32)
            dv = jnp.zeros((tk, LANES), F32)
            dcs = []
            for h in range(2):
                qh, doh = _sel(hm[h], q), _sel(hm[h], do)
                pr = jnp.exp(_fox_logits(q, k, cq_ref[h], ck[h:h + 1, :], mask, hm[h]) - _wide(l_ref[h], tk))
                ds = pr * (_dot_nt(doh, v) - _wide(d_ref[h], tk))
                dsb = ds.astype(BF)
                dq = dq + _dot(dsb, _sel(hm[h], k))
                dk = dk + _dot_tn(dsb, qh)
                dv = dv + _dot_tn(pr.astype(BF), doh)
                dcs.append(jnp.sum(ds, axis=0, keepdims=True))
            rows = pl.ds(pl.multiple_of(qi * tq, tq), tq)
            dq_ref[rows, :] += SCALE * dq
            dk_ref[...] += dk
            dv_ref[...] += dv
            dc_ref[...] -= jnp.concatenate(dcs, axis=0)

        @pl.when(qi > kj)
        def _():
            step(None)

        @pl.when(qi == kj)
        def _():
            step(_diag_mask(tq))

    def q_map(p, j, i):
        return (jnp.maximum(i, j), p)

    kblk = pl.BlockSpec((tk, LANES), lambda p, j, i: (j, p))
    rep = pl.BlockSpec((2, tq, LANES), lambda p, j, i: (p, jnp.maximum(i, j), 0))
    return pl.pallas_call(
        body, name=name, grid=(npair, nq, nq),
        in_specs=[pl.BlockSpec((tq, LANES), q_map),
                  pl.BlockSpec((tk, LANES), lambda p, j, i: (j, npair + p)),
                  pl.BlockSpec((tk, LANES), lambda p, j, i: (j, 2 * npair + p)),
                  pl.BlockSpec((tq, LANES), q_map), rep, rep, rep,
                  pl.BlockSpec((None, 2, tk), lambda p, j, i: (p, 0, j))],
        out_specs=[pl.BlockSpec((s, LANES), lambda p, j, i: (0, p)), kblk, kblk,
                   pl.BlockSpec((None, 2, tk), lambda p, j, i: (p, 0, j))],
        out_shape=[jax.ShapeDtypeStruct((s, MIX_W), F32), jax.ShapeDtypeStruct((s, MIX_W), F32),
                   jax.ShapeDtypeStruct((s, MIX_W), F32), jax.ShapeDtypeStruct((npair, 2, s), F32)],
        compiler_params=_cp("arbitrary", "arbitrary", "arbitrary"))(hb, hb, hb, dcat, dsum, lse, c_rep, c_t3)


def _foxt_logits(q, k, cq_row, ck_rep, mask, hmask):
    s = _dot_nt(_sel(hmask, k), q) + (cq_row - _wide(ck_rep, q.shape[0]))
    if mask is not None:
        s = jnp.where(mask, s, NEG)
    return s


def _causal_t(qi, kj, tq, tk):
    return (kj * tk + lax.broadcasted_iota(jnp.int32, (tk, tq), 0)
            <= qi * tq + lax.broadcasted_iota(jnp.int32, (tk, tq), 1))


def _fox_tiles(s):
    tq = _rows(s, 1024)
    return tq, tq // 2, s // tq


def _count_ge(t, bounds):
    return sum([(t >= b).astype(jnp.int32) for b in bounds], jnp.int32(0))


def _sweep_q_major(t, nq):
    qi = _count_ge(t, [r * (r + 1) for r in range(1, nq)])
    return qi, t - qi * (qi + 1)


def _sweep_k_major(t, nq):
    counts = [nq - j // 2 for j in range(2 * nq)]
    offs = [sum(counts[:j]) for j in range(1, 2 * nq)]
    kj = _count_ge(t, offs)
    start = sum([jnp.where(t >= o, c, 0) for o, c in zip(offs, counts)], jnp.int32(0))
    qi = kj // 2 + (t - start)
    return kj, qi, t == start, qi == nq - 1


def _foxt_fwd(hb, c_rep, c_t3, name):
    s = hb.shape[0]
    npair = MIX_W // LANES
    tq, tk, nq = _fox_tiles(s)

    def body(q_ref, k_ref, v_ref, cq_ref, ck_ref, o_ref, l_ref, m_s, l_s, acc):
        qi, kj = _sweep_q_major(pl.program_id(1), nq)
        hm = _head_masks()

        @pl.when(kj == 0)
        def _():
            m_s[...] = jnp.full_like(m_s, NEG)
            l_s[...] = jnp.zeros_like(l_s)
            acc[...] = jnp.zeros_like(acc)

        def step(mask):
            q, k = q_ref[...] * SCALE, k_ref[...]
            vt = jnp.transpose(v_ref[...])
            cq = cq_ref[...]
            for h in range(2):
                st = _foxt_logits(q, k, cq[h:h + 1, :], ck_ref[h], mask, hm[h])
                m_old = m_s[h]
                m_new = jnp.maximum(m_old, jnp.max(st, axis=0, keepdims=True))
                pt = jnp.exp(st - m_new)
                corr = jnp.exp(m_old - m_new)
                l_s[h] = l_s[h] * corr + jnp.sum(pt, axis=0, keepdims=True)
                acc[h] = acc[h] * corr + _dot(vt[h * HEAD_DIM:(h + 1) * HEAD_DIM, :], pt.astype(BF))
                m_s[h] = m_new

        @pl.when(kj < 2 * qi)
        def _():
            step(None)

        @pl.when(kj >= 2 * qi)
        def _():
            step(_causal_t(qi, kj, tq, tk))

        @pl.when(kj == 2 * qi + 1)
        def _():
            outs = []
            for h in range(2):
                outs.append(acc[h] / l_s[h])
                l_ref[h:h + 1, :] = m_s[h] + jnp.log(l_s[h])
            o_ref[...] = jnp.transpose(jnp.concatenate(outs, axis=0)).astype(BF)

    def q_map(p, t):
        return (_sweep_q_major(t, nq)[0], p)

    def kv_map(off):
        return lambda p, t: (_sweep_q_major(t, nq)[1], off + p)

    blk = pl.BlockSpec((tq, LANES), q_map)
    row = pl.BlockSpec((None, 2, tq), lambda p, t: (p, 0, _sweep_q_major(t, nq)[0]))
    return pl.pallas_call(
        body, name=name, grid=(npair, nq * (nq + 1)),
        in_specs=[blk, pl.BlockSpec((tk, LANES), kv_map(npair)), pl.BlockSpec((tk, LANES), kv_map(2 * npair)), row,
                  pl.BlockSpec((2, tk, LANES), lambda p, t: (p, _sweep_q_major(t, nq)[1], 0))],
        out_specs=[blk, row],
        out_shape=[jax.ShapeDtypeStruct((s, MIX_W), BF), jax.ShapeDtypeStruct((npair, 2, s), F32)],
        scratch_shapes=[pltpu.VMEM((2, 1, tq), F32), pltpu.VMEM((2, 1, tq), F32),
                        pltpu.VMEM((2, HEAD_DIM, tq), F32)],
        compiler_params=_cp("parallel", "arbitrary"))(hb, hb, hb, c_t3, c_rep)


def _foxt_dsum(hb, dcat, lse, c_rep, c_t3, name):
    s = hb.shape[0]
    npair = MIX_W // LANES
    tq, tk, nq = _fox_tiles(s)

    def body(q_ref, k_ref, v_ref, do_ref, l_ref, cq_ref, ck_ref, d_ref, acc):
        qi, kj = _sweep_q_major(pl.program_id(1), nq)
        hm = _head_masks()

        @pl.when(kj == 0)
        def _():
            acc[...] = jnp.zeros_like(acc)

        def step(mask):
            q, k, v, do = q_ref[...] * SCALE, k_ref[...], v_ref[...], do_ref[...]
            cq, lse_rows = cq_ref[...], l_ref[...]
            for h in range(2):
                pt = jnp.exp(_foxt_logits(q, k, cq[h:h + 1, :], ck_ref[h], mask, hm[h]) - lse_rows[h:h + 1, :])
                acc[h] += jnp.sum(pt * _dot_nt(_sel(hm[h], v), do), axis=0, keepdims=True)

        @pl.when(kj < 2 * qi)
        def _():
            step(None)

        @pl.when(kj >= 2 * qi)
        def _():
            step(_causal_t(qi, kj, tq, tk))

        @pl.when(kj == 2 * qi + 1)
        def _():
            for h in range(2):
                d_ref[h:h + 1, :] = acc[h]

    def q_map(p, t):
        return (_sweep_q_major(t, nq)[0], p)

    def kv_map(off):
        return lambda p, t: (_sweep_q_major(t, nq)[1], off + p)

    blk = pl.BlockSpec((tq, LANES), q_map)
    row = pl.BlockSpec((None, 2, tq), lambda p, t: (p, 0, _sweep_q_major(t, nq)[0]))
    return pl.pallas_call(
        body, name=name, grid=(npair, nq * (nq + 1)),
        in_specs=[blk, pl.BlockSpec((tk, LANES), kv_map(npair)), pl.BlockSpec((tk, LANES), kv_map(2 * npair)),
                  blk, row, row, pl.BlockSpec((2, tk, LANES), lambda p, t: (p, _sweep_q_major(t, nq)[1], 0))],
        out_specs=row, out_shape=jax.ShapeDtypeStruct((npair, 2, s), F32),
        scratch_shapes=[pltpu.VMEM((2, 1, tq), F32)],
        compiler_params=_cp("parallel", "arbitrary"))(hb, hb, hb, dcat, lse, c_t3, c_rep)


def _foxt_bwd(hb, dcat, dsum, lse, c_rep, c_t3, name):
    s = hb.shape[0]
    npair = MIX_W // LANES
    tq, tk, nq = _fox_tiles(s)

    def body(q_ref, k_ref, v_ref, do_ref, d_ref, l_ref, cq_ref, ck_ref, dq_ref, dk_ref, dv_ref, dc_ref, dc_s):
        t = pl.program_id(1)
        kj, qi, first, last = _sweep_k_major(t, nq)
        hm = _head_masks()

        @pl.when(first)
        def _():
            dk_ref[...] = jnp.zeros_like(dk_ref)
            dv_ref[...] = jnp.zeros_like(dv_ref)
            dc_s[...] = jnp.zeros_like(dc_s)

        @pl.when(t == 0)
        def _():
            dq_ref[...] = jnp.zeros_like(dq_ref)

        def step(mask):
            q, k, v, do = q_ref[...] * SCALE, k_ref[...], v_ref[...], do_ref[...]
            qt, kt, dot = jnp.transpose(q), jnp.transpose(k), jnp.transpose(do)
            cq, lse_rows, d_rows = cq_ref[...], l_ref[...], d_ref[...]
            dqs, dks, dvs = [], [], []
            for h in range(2):
                rows = slice(h * HEAD_DIM, (h + 1) * HEAD_DIM)
                pt = jnp.exp(_foxt_logits(q, k, cq[h:h + 1, :], ck_ref[h], mask, hm[h]) - lse_rows[h:h + 1, :])
                dst = pt * (_dot_nt(_sel(hm[h], v), do) - d_rows[h:h + 1, :])
                dsb = dst.astype(BF)
                dqs.append(_dot(kt[rows, :], dsb))
                dks.append(_dot_nt(qt[rows, :], dsb))
                dvs.append(_dot_nt(dot[rows, :], pt.astype(BF)))
                dc_s[h] += _fold(dst)
            cols = pl.ds(pl.multiple_of(qi * tq, tq), tq)
            dq_ref[:, cols] += SCALE * jnp.concatenate(dqs, axis=0)
            dk_ref[...] += jnp.concatenate(dks, axis=0)
            dv_ref[...] += jnp.concatenate(dvs, axis=0)

        @pl.when(kj < 2 * qi)
        def _():
            step(None)

        @pl.when(kj >= 2 * qi)
        def _():
            step(_causal_t(qi, kj, tq, tk))

        @pl.when(last)
        def _():
            for h in range(2):
                dc_ref[h:h + 1, :] = -jnp.sum(jnp.transpose(dc_s[h]), axis=0, keepdims=True)

    def kj_of(t):
        return _sweep_k_major(t, nq)[0]

    def qi_of(t):
        return _sweep_k_major(t, nq)[1]

    qblk = pl.BlockSpec((tq, LANES), lambda p, t: (qi_of(t), p))
    row = pl.BlockSpec((None, 2, tq), lambda p, t: (p, 0, qi_of(t)))
    kblk = pl.BlockSpec((LANES, tk), lambda p, t: (p, kj_of(t)))
    rep = pl.BlockSpec((2, tk, LANES), lambda p, t: (p, kj_of(t), 0))
    return pl.pallas_call(
        body, name=name, grid=(npair, nq * (nq + 1)),
        in_specs=[qblk,
                  pl.BlockSpec((tk, LANES), lambda p, t: (kj_of(t), npair + p)),
                  pl.BlockSpec((tk, LANES), lambda p, t: (kj_of(t), 2 * npair + p)),
                  qblk, row, row, row, rep],
        out_specs=[pl.BlockSpec((LANES, s), lambda p, t: (p, 0)), kblk, kblk,
                   pl.BlockSpec((None, 2, tk), lambda p, t: (p, 0, kj_of(t)))],
        out_shape=[jax.ShapeDtypeStruct((MIX_W, s), F32), jax.ShapeDtypeStruct((MIX_W, s), F32),
                   jax.ShapeDtypeStruct((MIX_W, s), F32), jax.ShapeDtypeStruct((npair, 2, s), F32)],
        scratch_shapes=[pltpu.VMEM((2, tk, LANES), F32)],
        compiler_params=_cp("arbitrary", "arbitrary"))(hb, hb, hb, dcat, dsum, lse, c_t3, c_rep)


def _loss_head(y, target, name):
    s, d = y.shape
    ts = _rows(s)

    def body(y_ref, t_ref, dy_ref, l_ref):
        i = pl.program_id(0)
        e = y_ref[...] - t_ref[...]
        dy_ref[...] = e * (1.0 / d)

        @pl.when(i == 0)
        def _():
            l_ref[...] = jnp.zeros_like(l_ref)

        part = jnp.sum(jnp.sum(e * e, axis=1, keepdims=True), axis=0, keepdims=True)
        l_ref[...] += part * (0.5 / d)

    row = pl.BlockSpec((ts, d), lambda i: (i, 0))
    return pl.pallas_call(
        body, name=name, grid=(s // ts,), in_specs=[row, row],
        out_specs=[row, pl.BlockSpec((1, 1), lambda i: (0, 0))],
        out_shape=[jax.ShapeDtypeStruct((s, d), F32), jax.ShapeDtypeStruct((1, 1), F32)],
        compiler_params=_cp("arbitrary"))(y, target)


def _adam_rows(r, c):
    cap = max(8, (1 << 20) // (4 * c))
    if r <= cap:
        return r
    best = None
    for t in range(8, cap + 1, 8):
        if r % t == 0:
            best = t
    return best if best is not None else r


def _reduce_adamw(contribs, w, m, v, name):
    nl = len(contribs)
    nd, r, c = contribs[0].shape
    tr = _adam_rows(r, c)
    bc1 = 1.0 - ADAM_B1 ** ADAM_STEP
    bc2 = 1.0 - ADAM_B2 ** ADAM_STEP

    def body(*refs):
        c_refs = refs[:nl]
        w_ref, m_ref, v_ref, g_ref, d_ref, nm_ref, nv_ref = refs[nl:]
        l = pl.program_id(0)
        for li in range(nl):
            @pl.when(l == li)
            def _(c_ref=c_refs[li]):
                g = c_ref[0].astype(F32)
                for k in range(1, nd):
                    g = g + c_ref[k].astype(F32)
                nm = ADAM_B1 * m_ref[...] + (1.0 - ADAM_B1) * g
                nv = ADAM_B2 * v_ref[...] + (1.0 - ADAM_B2) * (g * g)
                g_ref[...] = g
                nm_ref[...] = nm
                nv_ref[...] = nv
                d_ref[...] = -ADAM_LR * ((nm / bc1) / (jnp.sqrt(nv / bc2) + ADAM_EPS) + ADAM_WD * w_ref[...])

    def c_spec(li):
        return pl.BlockSpec((nd, tr, c), lambda l, i: (0, jnp.where(l == li, i, 0), 0))

    blk = pl.BlockSpec((None, tr, c), lambda l, i: (l, i, 0))
    out = jax.ShapeDtypeStruct((nl, r, c), F32)
    return pl.pallas_call(
        body, name=name, grid=(nl, r // tr),
        in_specs=[c_spec(li) for li in range(nl)] + [blk, blk, blk],
        out_specs=[blk, blk, blk, blk], out_shape=[out, out, out, out],
        compiler_params=_cp("arbitrary", "arbitrary"))(*contribs, w, m, v)


def _mesh_pos():
    return lax.axis_index("x"), lax.axis_index("y"), lax.axis_index("c")


def _peer(pos, k):
    x, y, c = pos
    return (1 - x if k & 4 else x, 1 - y if k & 2 else y, 1 - c if k & 1 else c)


def _linear(pos):
    return 4 * pos[0] + 2 * pos[1] + pos[2]


def _xfer_copies(srcs, lands, send_sems, recv_sems, local_sems, gather):
    pos = _mesh_pos()
    me = _linear(pos)
    local, remote = [], []
    for i, (src, land) in enumerate(zip(srcs, lands)):
        local.append(pltpu.make_async_copy(src if gather else src.at[me], land.at[me], local_sems.at[i]))
        for k in range(1, N_DEV):
            peer = _peer(pos, k)
            remote.append(pltpu.make_async_remote_copy(
                src_ref=src if gather else src.at[_linear(peer)], dst_ref=land.at[me],
                send_sem=send_sems.at[i * (N_DEV - 1) + k - 1], recv_sem=recv_sems.at[i * (N_DEV - 1) + k - 1],
                device_id=peer, device_id_type=MESH_ID))
    return local, remote


_HBM = pl.BlockSpec(memory_space=pltpu.HBM)
_SEM = pl.BlockSpec(memory_space=pltpu.SEMAPHORE)
_EFFECT = pltpu.SideEffectType.DATAFLOW_SIDE_EFFECTING


def _xfer_start(srcs, gather, name, after=()):
    n = len(srcs)
    na = len(after)
    lands = [lax.empty(((N_DEV,) + a.shape) if gather else a.shape, a.dtype) for a in srcs]

    def body(*refs):
        src, land = refs[:n], refs[n:2 * n]
        send_sems, recv_sems, local_sems = refs[2 * n + na:2 * n + na + 3]
        local, remote = _xfer_copies(src, land, send_sems, recv_sems, local_sems, gather)
        for cp in local + remote:
            cp.start()
        refs[-1][...] = jnp.zeros_like(refs[-1])

    nsem = n * (N_DEV - 1)
    out = pl.pallas_call(
        body, name=name,
        out_shape=(pltpu.SemaphoreType.DMA((nsem,)), pltpu.SemaphoreType.DMA((nsem,)), pltpu.SemaphoreType.DMA((n,)),
                   *[pltpu.HBM(a.shape, a.dtype) for a in srcs], *[pltpu.HBM(a.shape, a.dtype) for a in lands],
                   jax.ShapeDtypeStruct((8, LANES), F32)),
        in_specs=[_HBM] * (2 * n) + [pl.BlockSpec(memory_space=pl.ANY)] * na,
        out_specs=(_SEM, _SEM, _SEM, *[_HBM] * (2 * n), pl.BlockSpec(memory_space=pltpu.VMEM)),
        input_output_aliases={i: 3 + i for i in range(2 * n)},
        compiler_params=pltpu.CompilerParams(has_side_effects=_EFFECT))(
            *[pltpu.with_memory_space_constraint(a, pltpu.HBM) for a in srcs],
            *[pltpu.with_memory_space_constraint(a, pltpu.HBM) for a in lands], *after)
    return out[:3], list(out[3:3 + n]), list(out[3 + n:3 + 2 * n]), out[-1]


def _started(handle):
    return handle[3]


def _xfer_wait(handle, after, gather, name):
    sems, srcs, lands, _ = handle
    n = len(srcs)

    def body(*refs):
        src, land = refs[:n], refs[n:2 * n]
        send_sems, recv_sems, local_sems = refs[2 * n:2 * n + 3]
        local, remote = _xfer_copies(src, land, send_sems, recv_sems, local_sems, gather)
        for cp in local:
            cp.wait()
        for cp in remote:
            cp.wait_send()
            cp.wait_recv()

    out = pl.pallas_call(
        body, name=name,
        out_shape=(*[pltpu.HBM(a.shape, a.dtype) for a in srcs], *[pltpu.HBM(a.shape, a.dtype) for a in lands]),
        in_specs=[_HBM] * (2 * n) + [_SEM] * 3 + [pl.BlockSpec(memory_space=pl.ANY)] * len(after),
        out_specs=tuple([_HBM] * (2 * n)), input_output_aliases={i: i for i in range(2 * n)},
        compiler_params=pltpu.CompilerParams(has_side_effects=_EFFECT))(*srcs, *lands, *sems, *after)
    return list(out[n:])


def _cols_full(g):
    nd, r, c = g.shape
    return jnp.transpose(g, (1, 0, 2)).reshape(r, nd * c)


def _cols_split(full):
    r, n = full.shape
    return jnp.transpose(full.reshape(r, N_DEV, n // N_DEV), (1, 0, 2))


def _pack_b_in(w):
    qkv = 3 * MIX_W
    pad = jnp.zeros((w.shape[0], B_IN_PAD - w.shape[1]), w.dtype)
    return jnp.concatenate([w[:, :qkv], w[:, qkv + N_MIX_HEADS:], w[:, qkv:qkv + N_MIX_HEADS], pad], axis=1)


def _unpack_b_in(w):
    qkv = 3 * MIX_W
    return jnp.concatenate([w[:, :qkv], w[:, qkv + MEM_W:qkv + MEM_W + N_MIX_HEADS], w[:, qkv:qkv + MEM_W]], axis=1)


def _to_classes(t, g):
    r = 4 ** g
    s, w = t.shape
    return jnp.transpose(t.reshape(s // r, r, w), (1, 0, 2)).reshape(s, w)


def _from_classes(t, g):
    r = 4 ** g
    s, w = t.shape
    return jnp.transpose(t.reshape(r, s // r, w), (1, 0, 2)).reshape(s, w)


def _group_stack(t):
    return jnp.stack([_to_classes(t[:, g * GROUP_W:(g + 1) * GROUP_W], g) for g in range(N_GROUPS)])


def _group_unstack(t3):
    return jnp.concatenate([_from_classes(t3[g], g) for g in range(N_GROUPS)], axis=1)


def _same_stack(t):
    return jnp.stack([_to_classes(t, g) for g in range(N_GROUPS)])


def _same_unstack(t3):
    return jnp.stack([_from_classes(t3[g], g) for g in range(N_GROUPS)])


def _ffn_forward(x, xb, wgu, get_rest, tag):
    gu, a = _ffn_up(xb, wgu, f"{tag}_up")
    wd4, gain, bias = get_rest(a)
    y, yb, xh, rstd = _mm_res_ln(a, wd4, x, gain, bias, 0.5, f"{tag}_down_ln")
    return y, yb, (xb, gu, a, xh, rstd), wd4


def _ffn_backward(dy, saved, wgu, wd4, gain, tag, after=()):
    xb, gu, a, xh, rstd = saved
    s = xb.shape[0]
    nd, c, d = wgu.shape
    dz, dzb, dgain, dbias = _ln_bwd(dy, xh, rstd, gain, 0.5, f"{tag}_ln_bwd", after)
    dh = _ffn_bwd_act(dzb, wd4, gu, f"{tag}_act_bwd").reshape(nd, s, c)
    dwd = _mm_tn(a, dzb[None], f"{tag}_dwd").reshape(nd, wd4.shape[1] // 2, d)
    dx = _mm_nt(dh, wgu, f"{tag}_dx", res=dz, w_rows_out=False)
    dwgu = _mm_tn(dh, xb[None], f"{tag}_dwgu")
    return dx, dwgu, dwd, dgain, dbias


def _mixer_a_forward(x, xb, memb, w_in, w_kv, w_out, gain, bias, tabs):
    h = _mm_nn(xb, w_in, F32, "a_in", b_rows_out=True)
    hb = _rope_cast([h], tabs, 2 * MIX_W // LANES, "a_rope")
    q3 = _group_stack(hb[:, :MIX_W])
    k3 = _group_stack(hb[:, MIX_W:2 * MIX_W])
    v3 = _group_stack(hb[:, 2 * MIX_W:3 * MIX_W])
    o3, l3 = _band_fwd(q3, k3, v3, "a_band_fwd")
    oa, lt = _band_combine(_same_unstack(o3), _same_unstack(l3), "a_combine")
    kv = _mm_nn(memb, w_kv, BF, "a_mem_kv")
    om, lm = _mem_fwd(hb, 3 * MIX_W // LANES, kv, "a_mem_fwd")
    cat = jnp.concatenate([oa, om], axis=1)
    y, yb, xh, rstd = _mm_res_ln(cat[None], w_out[None], x, gain, bias, 1.0, "a_out_ln")
    return y, yb, (xb, hb, q3, k3, v3, oa, lt, kv, lm, cat, xh, rstd)


def _mixer_a_backward(dy, saved, memb, w_in, w_kv, w_out, gain, tabs_neg, after=()):
    xb, hb, q3, k3, v3, oa, lt, kv, lm, cat, xh, rstd = saved
    dz, dzb, dgain, dbias = _ln_bwd(dy, xh, rstd, gain, 1.0, "a_ln_bwd", after)
    dcat = _mm_nt(dzb[None], w_out[None], "a_dcat", out_dtype=BF)
    dw_out = _mm_tn(cat[None], dzb[None], "a_dwout")[0]
    dqm, dkm, dvm = _mem_bwd(hb, 3 * MIX_W // LANES, kv, dcat, cat, GROUP_W // LANES, lm, "a_mem_bwd")
    dkv = jnp.concatenate([dkm, dvm], axis=1).astype(BF)
    dw_kv = _mm_tn(memb[None], dkv[None], "a_dwkv")[0]
    dq3, dk3, dv3 = _band_bwd(q3, k3, v3, _same_stack(dcat[:, :GROUP_W]), _same_stack(oa), _same_stack(lt),
                              "a_band_bwd")
    dhb = _rope_cast([_group_unstack(dq3), _group_unstack(dk3), _group_unstack(dv3), dqm], tabs_neg,
                     2 * MIX_W // LANES, "a_rope_bwd")
    dw_in = _mm_tn(dhb[None], xb[None], "a_dwin")[0]
    dx = _mm_nt(dhb[None], w_in[None], "a_dx", res=dz, w_rows_out=False)
    return dx, dw_in, dw_kv, dw_out, dgain, dbias


def _pad_rows(t, rows):
    return jnp.concatenate([t, jnp.zeros((rows - t.shape[0], t.shape[1]), t.dtype)], axis=0)


def _pad_cols(t, cols):
    return jnp.concatenate([t, jnp.zeros((t.shape[0], cols - t.shape[1]), t.dtype)], axis=1)


def _mixer_b_forward(x, xb, memb, w_in, fbias, w_kv, w_out, gain, bias, tabs):
    s = x.shape[0]
    h = _mm_nn(xb, w_in, F32, "b_in")
    hb = _rope_cast([h], tabs, 0, "b_cast")
    f0 = 3 * MIX_W + MEM_W
    f_t = _pad_rows(jnp.transpose(h[:, f0:f0 + N_MIX_HEADS]), 16)
    bias16 = _pad_rows(jnp.transpose(fbias), 16)
    c_t = _gate_fwd(f_t, bias16, "b_gate_fwd")
    c_t3 = c_t[:N_MIX_HEADS].reshape(N_MIX_HEADS // 2, 2, s)
    c_rep = jnp.broadcast_to(c_t[:N_MIX_HEADS, :, None], (N_MIX_HEADS, s, LANES))
    ob, lb = _foxt_fwd(hb, c_rep, c_t3, "b_fox_fwd")
    kv = _mm_nn(memb, w_kv, BF, "b_mem_kv")
    om, lm = _mem_fwd(hb, 3 * MIX_W // LANES, kv, "b_mem_fwd")
    cat = jnp.concatenate([ob, om], axis=1)
    y, yb, xh, rstd = _mm_res_ln(cat[None], w_out[None], x, gain, bias, 1.0, "b_out_ln")
    return y, yb, (xb, hb, f_t, bias16, c_rep, c_t3, lb, kv, lm, cat, xh, rstd)


def _mixer_b_backward(dy, saved, memb, w_in, w_kv, w_out, gain, tabs, after=()):
    xb, hb, f_t, bias16, c_rep, c_t3, lb, kv, lm, cat, xh, rstd = saved
    s = xb.shape[0]
    dz, dzb, dgain, dbias = _ln_bwd(dy, xh, rstd, gain, 1.0, "b_ln_bwd", after)
    dcat = _mm_nt(dzb[None], w_out[None], "b_dcat", out_dtype=BF)
    dw_out = _mm_tn(cat[None], dzb[None], "b_dwout")[0]
    dqm, dkm, dvm = _mem_bwd(hb, 3 * MIX_W // LANES, kv, dcat, cat, MIX_W // LANES, lm, "b_mem_bwd")
    dkv = jnp.concatenate([dkm, dvm], axis=1).astype(BF)
    dw_kv = _mm_tn(memb[None], dkv[None], "b_dwkv")[0]
    dsum = _foxt_dsum(hb, dcat, lb, c_rep, c_t3, "b_fox_dsum")
    dq, dk, dv, dc3 = _foxt_bwd(hb, dcat, dsum, lb, c_rep, c_t3, "b_fox_bwd")
    df_t, dfb = _gate_bwd(_pad_rows(dc3.reshape(N_MIX_HEADS, s), 16), f_t, bias16, "b_gate_bwd")
    df = _pad_cols(jnp.transpose(df_t[:N_MIX_HEADS]), B_IN_PAD - 3 * MIX_W - MEM_W)
    dhb = _rope_cast([dq, dk, dv, dqm, df], tabs, 0, "b_cast_bwd", transposed=(0, 1, 2))
    dw_in = _mm_tn(xb[None], dhb[None], "b_dwin")[0]
    dx = _mm_nt(dhb[None], w_in[None], "b_dx", res=dz)
    return dx, dw_in, jnp.transpose(dfb[:N_MIX_HEADS]), dw_kv, dw_out, dgain, dbias


def _stored(t, name):
    return jnp.transpose(t, (0, 2, 1)) if name in ROWS_OUT else t


GATHER_GROUPS = (
    (("ffn1_w_gate_up", 0),),
    (("ffn1_w_down", 0), ("ln_gain", None), ("ln_bias", None)),
    (("a_w_in", 0), ("a_w_out", 0), ("mem_w_kv", 0)),
    (("ffn2_w_gate_up", 0), ("ffn2_w_down", 0)),
    (("ffn1_w_gate_up", 1), ("ffn1_w_down", 1)),
    (("b_w_in", 0), ("b_w_out", 0), ("mem_w_kv", 1)),
    (("ffn2_w_gate_up", 1), ("ffn2_w_down", 1)),
)


def _group_shards(group, params):
    return [t if n in F32_COMM else _stored(t, n)[l].astype(BF) for (n, l), t in zip(group, params)]


def _weight_groups(w):
    return [_group_shards(grp, [w[n] for n, _ in grp]) for grp in GATHER_GROUPS]


def _local_step(x, mem, target, fbias, get_w, put_g):
    s, d = x.shape
    tabs = _rope_tables(s, 1.0)
    tabs_neg = _rope_tables(s, -1.0)
    memb = mem.astype(BF)
    saved, wl = [], []
    cur, curb = x, x.astype(BF)
    ln = []

    def down4(t):
        return t.reshape(N_DEV // 2, -1, d)

    for i in range(DEPTH):
        if i == 0:
            def first_rest(a):
                g = get_w(1, a)
                ln.extend(jnp.transpose(t, (1, 2, 0, 3)).reshape(DEPTH, 3, 1, d) for t in g[1:3])
                return down4(g[0]), ln[0][0, 0], ln[1][0, 0]

            wgu = get_w(0, cur)[0]
            cur, curb, s1, wd = _ffn_forward(cur, curb, wgu, first_rest, "l0_ffn1")
        else:
            g = get_w(3 * i + 1, cur)
            wgu = g[0]
            cur, curb, s1, wd = _ffn_forward(cur, curb, wgu, lambda a, g=g: (down4(g[1]), ln[0][i, 0], ln[1][i, 0]),
                                             f"l{i}_ffn1")
        w1 = (wgu, wd)
        ln_g, ln_b = ln
        g = get_w(3 * i + 2, cur)
        if i == 0:
            wm = (g[0].reshape(-1, d), g[2].reshape(d, -1), _cols_full(g[1]))
            cur, curb, s2 = _mixer_a_forward(cur, curb, memb, wm[0], wm[1], wm[2], ln_g[i, 1], ln_b[i, 1], tabs)
        else:
            wm = (_pack_b_in(g[0].reshape(d, -1)), g[2].reshape(d, -1), g[1].reshape(d, -1))
            cur, curb, s2 = _mixer_b_forward(cur, curb, memb, wm[0], fbias, wm[1], wm[2], ln_g[i, 1], ln_b[i, 1],
                                             tabs)
        g = get_w(3 * i + 3, cur)
        cur, curb, s3, wd = _ffn_forward(cur, curb, g[0], lambda a, g=g: (down4(g[1]), ln_g[i, 2], ln_b[i, 2]),
                                         f"l{i}_ffn2")
        w3 = (g[0], wd)
        saved.append((s1, s2, s3))
        wl.append((w1, wm, w3))

    dy, loss = _loss_head(cur, target, "loss_head")

    dgs = [[None] * 3 for _ in range(DEPTH)]
    dbs = [[None] * 3 for _ in range(DEPTH)]
    sent = ()
    for i in reversed(range(DEPTH)):
        s1, s2, s3 = saved[i]
        w1, wm, w3 = wl[i]
        dy, dgu, dd, dgs[i][2], dbs[i][2] = _ffn_backward(dy, s3, w3[0], w3[1], ln_g[i, 2], f"l{i}_ffn2", sent)
        sent = put_g(3 * i + 2, [dgu, dd])
        if i == 0:
            dy, dw_in, dw_kv, dw_out, dgs[i][1], dbs[i][1] = _mixer_a_backward(
                dy, s2, memb, wm[0], wm[1], wm[2], ln_g[i, 1], tabs_neg, sent)
            sent = put_g(1, [dw_in.reshape(N_DEV, -1, d), _cols_split(dw_out),
                             dw_kv.reshape(N_DEV, d // N_DEV, -1)])
        else:
            dy, dw_in, dfb, dw_kv, dw_out, dgs[i][1], dbs[i][1] = _mixer_b_backward(
                dy, s2, memb, wm[0], wm[1], wm[2], ln_g[i, 1], tabs, sent)
            sent = put_g(4, [_unpack_b_in(dw_in).reshape(N_DEV, d // N_DEV, -1),
                             dw_out.reshape(N_DEV, d // N_DEV, -1), dw_kv.reshape(N_DEV, d // N_DEV, -1),
                             jnp.broadcast_to(dfb[None], (N_DEV,) + dfb.shape)])
        dy, dgu, dd, dgs[i][0], dbs[i][0] = _ffn_backward(dy, s1, w1[0], w1[1], ln_g[i, 0], f"l{i}_ffn1", sent)
        if i == 0:
            ln_pieces = []
            for parts in (dgs, dbs):
                t = jnp.concatenate([parts[a][b] for a in range(DEPTH) for b in range(3)], axis=0)
                ln_pieces.append(jnp.transpose(t.reshape(DEPTH * 3, N_DEV, d // N_DEV), (1, 0, 2)))
            sent = put_g(0, [dgu, dd] + ln_pieces)
        else:
            sent = put_g(3, [dgu, dd])
    return loss, dy


WEIGHTS = ("ffn1_w_gate_up", "ffn1_w_down", "ffn2_w_gate_up", "ffn2_w_down", "ln_gain", "ln_bias", "mem_w_kv",
           "a_w_in", "a_w_out", "b_w_in", "b_forget_bias", "b_w_out")
F32_COMM = ("ln_gain", "ln_bias", "b_forget_bias")
ROWS_OUT = ("ffn1_w_gate_up", "ffn2_w_gate_up", "a_w_in")
GRAD_SLOTS = {
    "ffn1_w_gate_up": [(0, 0), (3, 0)], "ffn1_w_down": [(0, 1), (3, 1)],
    "ffn2_w_gate_up": [(2, 0), (5, 0)], "ffn2_w_down": [(2, 1), (5, 1)],
    "ln_gain": [(0, 2)], "ln_bias": [(0, 3)], "mem_w_kv": [(1, 2), (4, 2)],
    "a_w_in": [(1, 0)], "a_w_out": [(1, 1)], "b_w_in": [(4, 0)], "b_forget_bias": [(4, 3)], "b_w_out": [(4, 1)],
}


def kernel(x, mem, ffn1_w_gate_up, ffn1_w_down, ffn2_w_gate_up, ffn2_w_down, ln_gain, ln_bias, mem_w_kv, a_w_in, a_w_out, b_w_in, b_forget_bias, b_w_out, loss_target, m_ffn1_w_gate_up, m_ffn1_w_down, m_ffn2_w_gate_up, m_ffn2_w_down, m_ln_gain, m_ln_bias, m_mem_w_kv, m_a_w_in, m_a_w_out, m_b_w_in, m_b_forget_bias, m_b_w_out, v_ffn1_w_gate_up, v_ffn1_w_down, v_ffn2_w_gate_up, v_ffn2_w_down, v_ln_gain, v_ln_bias, v_mem_w_kv, v_a_w_in, v_a_w_out, v_b_w_in, v_b_forget_bias, v_b_w_out):
    w = dict(zip(WEIGHTS, (ffn1_w_gate_up, ffn1_w_down, ffn2_w_gate_up, ffn2_w_down, ln_gain, ln_bias, mem_w_kv,
                           a_w_in, a_w_out, b_w_in, b_forget_bias, b_w_out)))
    m = dict(zip(WEIGHTS, (m_ffn1_w_gate_up, m_ffn1_w_down, m_ffn2_w_gate_up, m_ffn2_w_down, m_ln_gain, m_ln_bias,
                           m_mem_w_kv, m_a_w_in, m_a_w_out, m_b_w_in, m_b_forget_bias, m_b_w_out)))
    v = dict(zip(WEIGHTS, (v_ffn1_w_gate_up, v_ffn1_w_down, v_ffn2_w_gate_up, v_ffn2_w_down, v_ln_gain, v_ln_bias,
                           v_mem_w_kv, v_a_w_in, v_a_w_out, v_b_w_in, v_b_forget_bias, v_b_w_out)))

    gathers = []
    for k, grp in enumerate(GATHER_GROUPS):
        params, behind = [w[n] for n, _ in grp], [_started(h) for h in gathers[-1:]]
        if behind:
            params, behind = lax.optimization_barrier((params, behind))
        gathers.append(_xfer_start(_group_shards(grp, params), True, f"gather{k}_start", behind))
    exchanges = {}

    def get_w(k, after):
        behind = [after] + ([_started(h) for h in gathers] if k == 0 else [])
        return _xfer_wait(gathers[k], behind, True, f"gather{k}_wait")

    def put_g(k, pieces):
        exchanges[k] = _xfer_start(pieces, False, f"grads{k}_start")
        return (_started(exchanges[k]),)

    loss, grad_x = _local_step(x[0], mem[0], loss_target[0], b_forget_bias, get_w, put_g)
    loss = lax.psum(loss[0, 0], ("x", "y", "c"))

    outs, landed = {}, {}

    def adamw(names, after):
        for n in names:
            contribs = [landed[g][j] for g, j in GRAD_SLOTS[n]]
            view = (len(contribs),) + contribs[0].shape[1:]
            shape = _stored(w[n], n).shape
            res = _reduce_adamw(contribs, *[_stored(t[n], n).reshape(view) for t in (w, m, v)], f"adamw_{n}")
            outs[n] = [_stored(t.reshape(shape), n) for t in res]
            after = outs[n][0]
        return after

    after = [grad_x, _started(exchanges[0])]
    for k in (5, 4, 3, 2, 1):
        landed[k] = _xfer_wait(exchanges[k], after, False, f"grads{k}_wait")
        after = [landed[k][0]]
    done = adamw(("ffn2_w_gate_up", "ffn2_w_down", "mem_w_kv", "a_w_in", "a_w_out", "b_w_in", "b_forget_bias",
                  "b_w_out"), None)
    landed[0] = _xfer_wait(exchanges[0], [done], False, "grads0_wait")
    adamw(("ffn1_w_gate_up", "ffn1_w_down", "ln_gain", "ln_bias"), None)
    return (loss, grad_x[None], *[outs[n][0] for n in WEIGHTS], *[outs[n][1] for n in WEIGHTS],
            *[outs[n][2] for n in WEIGHTS], *[outs[n][3] for n in WEIGHTS])
```

```python
import functools

import jax
import jax.numpy as jnp
from jax import lax
from jax.experimental import pallas as pl
from jax.experimental.pallas import tpu as pltpu

F32 = jnp.float32
BF = jnp.bfloat16
MESH_ID = pl.DeviceIdType.MESH

N_DEV = 8
DEPTH = 2
HEAD_DIM = 64
LANES = 128
N_MIX_HEADS = 12
N_MEM_HEADS = 4
MIX_W = N_MIX_HEADS * HEAD_DIM
MEM_W = N_MEM_HEADS * HEAD_DIM
N_GROUPS = 3
GROUP_W = MIX_W // N_GROUPS
BLOCK = 128
BAND_SUB = 4
ROT_HALF = 8
ROPE_THETA = 500000.0
ALPHA = (2 * DEPTH) ** 0.25
LN_EPS = 1e-5
SCALE = HEAD_DIM ** -0.5
NEG = -1e30
B_IN_PAD = 2688
ADAM_LR, ADAM_B1, ADAM_B2, ADAM_EPS, ADAM_WD, ADAM_STEP = 0.001, 0.9, 0.999, 1e-08, 0.01, 10
VMEM_LIMIT = 56 * 1024 * 1024


def _cp(*sem):
    return pltpu.CompilerParams(dimension_semantics=sem, vmem_limit_bytes=VMEM_LIMIT)


def _dot(a, b):
    return jnp.dot(a, b, preferred_element_type=F32)


def _dot_nt(a, b):
    return lax.dot_general(a, b, (((1,), (1,)), ((), ())), preferred_element_type=F32)


def _dot_tn(a, b):
    return lax.dot_general(a, b, (((0,), (0,)), ((), ())), preferred_element_type=F32)


def _sigmoid(x):
    return 1.0 / (1.0 + jnp.exp(-x))


def _tile(n, cap=1024):
    if n <= cap:
        return n
    best = LANES
    for t in range(LANES, cap + 1, LANES):
        if n % t == 0:
            best = t
    return best


def _rows(s, cap=512):
    return s if s <= cap else cap


def _mm_nn(a, b, out_dtype, name, b_rows_out=False):
    m, k = a.shape
    n = b.shape[0] if b_rows_out else b.shape[1]
    tm, tn = _rows(m), _tile(n)

    def body(a_ref, b_ref, o_ref):
        prod = _dot_nt(a_ref[...], b_ref[...]) if b_rows_out else _dot(a_ref[...], b_ref[...])
        o_ref[...] = prod.astype(o_ref.dtype)

    b_spec = (pl.BlockSpec((tn, k), lambda j, i: (j, 0)) if b_rows_out
              else pl.BlockSpec((k, tn), lambda j, i: (0, j)))
    return pl.pallas_call(
        body, name=name, grid=(n // tn, m // tm),
        in_specs=[pl.BlockSpec((tm, k), lambda j, i: (i, 0)), b_spec],
        out_specs=pl.BlockSpec((tm, tn), lambda j, i: (i, j)),
        out_shape=jax.ShapeDtypeStruct((m, n), out_dtype),
        compiler_params=_cp("parallel", "parallel"))(a, b)


def _resident(shape, index_map):
    return pl.BlockSpec(shape, index_map, pipeline_mode=pl.Buffered(1))


def _mm_tn(a, b, name, out_dtype=BF):
    na, s, m = a.shape
    nb, _, n = b.shape
    no = max(na, nb)
    tm, tn = _tile(m), _tile(n)

    def body(a_ref, b_ref, o_ref):
        o_ref[...] = _dot_tn(a_ref[...], b_ref[...]).astype(o_ref.dtype)

    def spec(nbatch, width, tile, index_map):
        fixed = nbatch == 1 and width == tile
        return _resident((None, s, tile), index_map) if fixed else pl.BlockSpec((None, s, tile), index_map)

    return pl.pallas_call(
        body, name=name, grid=(no, m // tm, n // tn),
        in_specs=[spec(na, m, tm, lambda j, r, c: (j if na > 1 else 0, 0, r)),
                  spec(nb, n, tn, lambda j, r, c: (j if nb > 1 else 0, 0, c))],
        out_specs=pl.BlockSpec((None, tm, tn), lambda j, r, c: (j, r, c)),
        out_shape=jax.ShapeDtypeStruct((no, m, n), out_dtype),
        compiler_params=_cp("parallel", "parallel", "parallel"))(a, b)


def _mm_nt(dh, w, name, res=None, out_dtype=F32, w_rows_out=True):
    nc, s, kc = dh.shape
    d = w.shape[1] if w_rows_out else w.shape[2]
    ts = _rows(s)
    has_res = res is not None
    mm = _dot_nt if w_rows_out else _dot

    def body(*refs):
        if has_res:
            dh_ref, w_ref, r_ref, o_ref = refs
        else:
            dh_ref, w_ref, o_ref = refs
        out = mm(dh_ref[0], w_ref[0])
        for j in range(1, nc):
            out = out + mm(dh_ref[j], w_ref[j])
        if has_res:
            out = out + ALPHA * r_ref[...]
        o_ref[...] = out.astype(o_ref.dtype)

    in_specs = [pl.BlockSpec((nc, ts, kc), lambda i: (0, i, 0)), _resident(w.shape, lambda i: (0, 0, 0))]
    args = [dh, w]
    if has_res:
        in_specs.append(pl.BlockSpec((ts, d), lambda i: (i, 0)))
        args.append(res)
    return pl.pallas_call(
        body, name=name, grid=(s // ts,), in_specs=in_specs,
        out_specs=pl.BlockSpec((ts, d), lambda i: (i, 0)),
        out_shape=jax.ShapeDtypeStruct((s, d), out_dtype),
        compiler_params=_cp("parallel"))(*args)


def _mm_res_ln(a, w, x, gain, bias, fscale, name):
    nc, s, kc = a.shape
    d = w.shape[2]
    ts = _rows(s)

    def body(a_ref, w_ref, x_ref, g_ref, b_ref, y_ref, yb_ref, xh_ref, r_ref):
        f = _dot(a_ref[0], w_ref[0])
        for j in range(1, nc):
            f = f + _dot(a_ref[j], w_ref[j])
        z = ALPHA * x_ref[...] + fscale * f
        mu = jnp.mean(z, axis=-1, keepdims=True)
        zc = z - mu
        var = jnp.mean(zc * zc, axis=-1, keepdims=True)
        r = lax.rsqrt(var + LN_EPS)
        xh = zc * r
        y = xh * g_ref[...] + b_ref[...]
        y_ref[...] = y
        yb_ref[...] = y.astype(BF)
        xh_ref[...] = xh
        r_ref[...] = r

    row = pl.BlockSpec((ts, d), lambda i: (i, 0))
    vec = pl.BlockSpec((1, d), lambda i: (0, 0))
    return pl.pallas_call(
        body, name=name, grid=(s // ts,),
        in_specs=[pl.BlockSpec((nc, ts, kc), lambda i: (0, i, 0)), _resident((nc, kc, d), lambda i: (0, 0, 0)),
                  row, vec, vec],
        out_specs=[row, row, row, pl.BlockSpec((ts, 1), lambda i: (i, 0))],
        out_shape=[jax.ShapeDtypeStruct((s, d), F32), jax.ShapeDtypeStruct((s, d), BF),
                   jax.ShapeDtypeStruct((s, d), F32), jax.ShapeDtypeStruct((s, 1), F32)],
        compiler_params=_cp("parallel"))(a, w, x, gain, bias)


def _ln_bwd(dy, xh, rstd, gain, fscale, name, after=()):
    s, d = dy.shape
    ts = _rows(s)
    na = len(after)

    def body(*refs):
        dy_ref, xh_ref, r_ref, g_ref = refs[:4]
        dz_ref, dzb_ref, dg_ref, db_ref = refs[4 + na:]
        i = pl.program_id(0)
        dyv = dy_ref[...]
        xhv = xh_ref[...]
        dxh = dyv * g_ref[...]
        m1 = jnp.mean(dxh, axis=-1, keepdims=True)
        m2 = jnp.mean(dxh * xhv, axis=-1, keepdims=True)
        dz = r_ref[...] * (dxh - m1 - xhv * m2)
        dz_ref[...] = dz
        dzb_ref[...] = (fscale * dz).astype(BF)

        @pl.when(i == 0)
        def _():
            dg_ref[...] = jnp.zeros_like(dg_ref)
            db_ref[...] = jnp.zeros_like(db_ref)

        dg_ref[...] += jnp.sum(dyv * xhv, axis=0, keepdims=True)
        db_ref[...] += jnp.sum(dyv, axis=0, keepdims=True)

    row = pl.BlockSpec((ts, d), lambda i: (i, 0))
    vec = pl.BlockSpec((1, d), lambda i: (0, 0))
    return pl.pallas_call(
        body, name=name, grid=(s // ts,),
        in_specs=[row, row, pl.BlockSpec((ts, 1), lambda i: (i, 0)), vec] + [pl.BlockSpec(memory_space=pl.ANY)] * na,
        out_specs=[row, row, vec, vec],
        out_shape=[jax.ShapeDtypeStruct((s, d), F32), jax.ShapeDtypeStruct((s, d), BF),
                   jax.ShapeDtypeStruct((1, d), F32), jax.ShapeDtypeStruct((1, d), F32)],
        compiler_params=_cp("arbitrary"))(dy, xh, rstd, gain, *after)


def _ffn_up(xb, wgu, name):
    s, d = xb.shape
    c = wgu.shape[1]
    nch = wgu.shape[0] // 2
    ts = _rows(s, 1024)
    w4 = wgu.reshape(2, nch, c, d)

    def body(x_ref, w_ref, gu_ref, a_ref):
        x = x_ref[...]
        g = _dot_nt(x, w_ref[0])
        u = _dot_nt(x, w_ref[1])
        sg = _sigmoid(g)
        t = g * sg
        gu_ref[0] = (u * (sg * (1.0 + g - t))).astype(BF)
        gu_ref[1] = t.astype(BF)
        a_ref[...] = (t * u).astype(BF)

    return pl.pallas_call(
        body, name=name, grid=(nch, s // ts),
        in_specs=[pl.BlockSpec((ts, d), lambda j, i: (i, 0)),
                  pl.BlockSpec((2, None, c, d), lambda j, i: (0, j, 0, 0))],
        out_specs=[pl.BlockSpec((2, None, ts, c), lambda j, i: (0, j, i, 0)),
                   pl.BlockSpec((None, ts, c), lambda j, i: (j, i, 0))],
        out_shape=[jax.ShapeDtypeStruct((2, nch, s, c), BF), jax.ShapeDtypeStruct((nch, s, c), BF)],
        compiler_params=_cp("parallel", "parallel"))(xb, w4)


def _ffn_fwd_main(x, xb, wgu, wd4, gain, bias, name):
    s, d = x.shape
    nch, c = wd4.shape[0], wd4.shape[1]
    ts = _rows(s, 256)

    def body(x_ref, xb_ref, wgu_ref, wd_ref, g_ref, b_ref, y_ref, yb_ref, gu_ref, a_ref, xh_ref, r_ref):
        xbv = xb_ref[...]
        f = jnp.zeros((ts, d), F32)
        for j in range(nch):
            g = _dot_nt(xbv, wgu_ref[j])
            u = _dot_nt(xbv, wgu_ref[nch + j])
            sg = _sigmoid(g)
            t = g * sg
            gu_ref[0, j] = (u * (sg * (1.0 + g - t))).astype(BF)
            gu_ref[1, j] = t.astype(BF)
            act = (t * u).astype(BF)
            a_ref[j] = act
            f = f + _dot(act, wd_ref[j])
        z = ALPHA * x_ref[...] + 0.5 * f
        mu = jnp.mean(z, axis=-1, keepdims=True)
        zc = z - mu
        var = jnp.mean(zc * zc, axis=-1, keepdims=True)
        r = lax.rsqrt(var + LN_EPS)
        xh = zc * r
        y = xh * g_ref[...] + b_ref[...]
        y_ref[...] = y
        yb_ref[...] = y.astype(BF)
        xh_ref[...] = xh
        r_ref[...] = r

    row = pl.BlockSpec((ts, d), lambda i: (i, 0))
    vec = pl.BlockSpec((1, d), lambda i: (0, 0))
    return pl.pallas_call(
        body, name=name, grid=(s // ts,),
        in_specs=[row, row, _resident(wgu.shape, lambda i: (0, 0, 0)), _resident(wd4.shape, lambda i: (0, 0, 0)),
                  vec, vec],
        out_specs=[row, row, pl.BlockSpec((2, nch, ts, c), lambda i: (0, 0, i, 0)),
                   pl.BlockSpec((nch, ts, c), lambda i: (0, i, 0)), row, pl.BlockSpec((ts, 1), lambda i: (i, 0))],
        out_shape=[jax.ShapeDtypeStruct((s, d), F32), jax.ShapeDtypeStruct((s, d), BF),
                   jax.ShapeDtypeStruct((2, nch, s, c), BF), jax.ShapeDtypeStruct((nch, s, c), BF),
                   jax.ShapeDtypeStruct((s, d), F32), jax.ShapeDtypeStruct((s, 1), F32)],
        compiler_params=_cp("parallel"))(x, xb, wgu, wd4, gain, bias)


def _ffn_bwd_main(dy, xh, rstd, gain, wd4, wgu, gu, name, after=()):
    s, d = dy.shape
    nch, c = wd4.shape[0], wd4.shape[1]
    ts = _rows(s, 256)
    na = len(after)

    def body(*refs):
        dy_ref, xh_ref, r_ref, g_ref, wd_ref, wgu_ref, gu_ref = refs[:7]
        dx_ref, dzb_ref, dh_ref, dg_ref, db_ref = refs[7 + na:]
        i = pl.program_id(0)
        dyv = dy_ref[...]
        xhv = xh_ref[...]
        dxh = dyv * g_ref[...]
        m1 = jnp.mean(dxh, axis=-1, keepdims=True)
        m2 = jnp.mean(dxh * xhv, axis=-1, keepdims=True)
        dz = r_ref[...] * (dxh - m1 - xhv * m2)
        dzb = (0.5 * dz).astype(BF)
        dzb_ref[...] = dzb

        @pl.when(i == 0)
        def _():
            dg_ref[...] = jnp.zeros_like(dg_ref)
            db_ref[...] = jnp.zeros_like(db_ref)

        dg_ref[...] += jnp.sum(dyv * xhv, axis=0, keepdims=True)
        db_ref[...] += jnp.sum(dyv, axis=0, keepdims=True)

        dx = ALPHA * dz
        for j in range(nch):
            da = _dot_nt(dzb, wd_ref[j])
            dgate = (da * gu_ref[0, j].astype(F32)).astype(BF)
            dup = (da * gu_ref[1, j].astype(F32)).astype(BF)
            dh_ref[0, j] = dgate
            dh_ref[1, j] = dup
            dx = dx + _dot(dgate, wgu_ref[j]) + _dot(dup, wgu_ref[nch + j])
        dx_ref[...] = dx

    row = pl.BlockSpec((ts, d), lambda i: (i, 0))
    vec = pl.BlockSpec((1, d), lambda i: (0, 0))
    act = pl.BlockSpec((2, nch, ts, c), lambda i: (0, 0, i, 0))
    return pl.pallas_call(
        body, name=name, grid=(s // ts,),
        in_specs=[row, row, pl.BlockSpec((ts, 1), lambda i: (i, 0)), vec,
                  _resident(wd4.shape, lambda i: (0, 0, 0)), _resident(wgu.shape, lambda i: (0, 0, 0)), act]
                 + [pl.BlockSpec(memory_space=pl.ANY)] * na,
        out_specs=[row, row, act, vec, vec],
        out_shape=[jax.ShapeDtypeStruct((s, d), F32), jax.ShapeDtypeStruct((s, d), BF),
                   jax.ShapeDtypeStruct((2, nch, s, c), BF),
                   jax.ShapeDtypeStruct((1, d), F32), jax.ShapeDtypeStruct((1, d), F32)],
        compiler_params=_cp("arbitrary"))(dy, xh, rstd, gain, wd4, wgu, gu, *after)


def _rope_tables(s, sign):
    pos = jnp.arange(s, dtype=F32)
    inv_freq = 1.0 / (ROPE_THETA ** (jnp.arange(ROT_HALF, dtype=F32) / ROT_HALF))
    ang = pos[:, None] * inv_freq[None, :]
    cos, sin = jnp.cos(ang), jnp.sin(ang) * sign
    one = jnp.ones((s, HEAD_DIM - 2 * ROT_HALF), F32)
    zero = jnp.zeros((s, HEAD_DIM - 2 * ROT_HALF), F32)
    zh = jnp.zeros((s, ROT_HALF), F32)
    cos_f = jnp.concatenate([cos, cos, one], axis=1)
    sin_a = jnp.concatenate([-sin, zh, zero], axis=1)
    sin_b = jnp.concatenate([zh, sin, zero], axis=1)
    rep = LANES // HEAD_DIM
    return tuple(jnp.tile(t, (1, rep)) for t in (cos_f, sin_a, sin_b))


def _rope_cast(parts, tabs, n_rope, name, transposed=()):
    s = tabs[0].shape[0]
    flip = [i in transposed for i in range(len(parts))]
    widths = [p.shape[0] if f else p.shape[1] for p, f in zip(parts, flip)]
    n = sum(widths)
    npart = len(parts)
    ts = _rows(s, 256)

    def body(*refs):
        part_refs = refs[:npart]
        c_ref, sa_ref, sb_ref, o_ref = refs[npart:]
        col = 0
        for ref, w, f in zip(part_refs, widths, flip):
            for j in range(w // LANES):
                if f:
                    t = jnp.transpose(ref[j * LANES:(j + 1) * LANES, :])
                else:
                    t = ref[:, j * LANES:(j + 1) * LANES]
                if col < n_rope:
                    t = (t * c_ref[...] + pltpu.roll(t, LANES - ROT_HALF, 1) * sa_ref[...]
                         + pltpu.roll(t, ROT_HALF, 1) * sb_ref[...])
                o_ref[:, col * LANES:(col + 1) * LANES] = t.astype(BF)
                col += 1

    tab = pl.BlockSpec((ts, LANES), lambda i: (i, 0))
    return pl.pallas_call(
        body, name=name, grid=(s // ts,),
        in_specs=[pl.BlockSpec((w, ts), lambda i: (0, i)) if f else pl.BlockSpec((ts, w), lambda i: (i, 0))
                  for w, f in zip(widths, flip)] + [tab, tab, tab],
        out_specs=pl.BlockSpec((ts, n), lambda i: (i, 0)),
        out_shape=jax.ShapeDtypeStruct((s, n), BF),
        compiler_params=_cp("parallel"))(*parts, *tabs)


def _head_masks():
    lane = lax.broadcasted_iota(jnp.int32, (1, LANES), 1)
    return [lane < HEAD_DIM, lane >= HEAD_DIM]


def _sel(mask, v):
    return jnp.where(mask, v, jnp.zeros_like(v))


def _pick(mask, wide, fill):
    return jnp.max(jnp.where(mask, wide, fill), axis=1, keepdims=True)


def _band_masks(has_other, prev):
    qi = lax.broadcasted_iota(jnp.int32, (BLOCK, BLOCK), 0)
    kj = lax.broadcasted_iota(jnp.int32, (BLOCK, BLOCK), 1)
    if prev:
        return kj >= qi + jnp.where(has_other, 0, BLOCK)
    return kj <= qi


def _band_fwd(q3, k3, v3, name):
    ng, s, w = q3.shape
    nb = s // BLOCK
    npair = w // LANES
    nsub = BAND_SUB
    tile = nsub * BLOCK

    def body(q_ref, kc_ref, kp_ref, vc_ref, vp_ref, o_ref, l_ref):
        g = pl.program_id(0)
        t = pl.program_id(2)
        nbl = jnp.right_shift(nb, 2 * g)
        mc = _band_masks(None, False)
        hm = _head_masks()
        for i in range(nsub):
            rows = slice(i * BLOCK, (i + 1) * BLOCK)
            has_prev = jnp.bitwise_and(t * nsub + i, nbl - 1) != 0
            mp = _band_masks(has_prev, True)
            q, kc, vc = q_ref[rows, :], kc_ref[rows, :], vc_ref[rows, :]
            if i == 0:
                kp, vp = kp_ref[...], vp_ref[...]
            else:
                prev = slice((i - 1) * BLOCK, i * BLOCK)
                kp, vp = kc_ref[prev, :], vc_ref[prev, :]
            o = jnp.zeros((BLOCK, LANES), F32)
            lse_w = jnp.zeros((BLOCK, LANES), F32)
            for h in range(2):
                qh = _sel(hm[h], q)
                sc = jnp.where(mc, _dot_nt(qh, kc) * SCALE, NEG)
                sp = jnp.where(mp, _dot_nt(qh, kp) * SCALE, NEG)
                m = jnp.maximum(jnp.max(sc, axis=1, keepdims=True), jnp.max(sp, axis=1, keepdims=True))
                pc = jnp.exp(sc - m)
                pp = jnp.exp(sp - m)
                l = jnp.sum(pc, axis=1, keepdims=True) + jnp.sum(pp, axis=1, keepdims=True)
                oh = _dot(pc.astype(BF), _sel(hm[h], vc)) + _dot(pp.astype(BF), _sel(hm[h], vp))
                o = o + oh / l
                lse_w = jnp.where(hm[h], m + jnp.log(l), lse_w)
            o_ref[rows, :] = o
            l_ref[rows, :] = lse_w

    cur = pl.BlockSpec((None, tile, LANES), lambda g, p, t: (g, t, p))
    prv = pl.BlockSpec((None, BLOCK, LANES), lambda g, p, t: (g, jnp.maximum(t * nsub - 1, 0), p))
    return pl.pallas_call(
        body, name=name, grid=(ng, npair, nb // nsub),
        in_specs=[cur, cur, prv, cur, prv], out_specs=[cur, cur],
        out_shape=[jax.ShapeDtypeStruct((ng, s, w), F32), jax.ShapeDtypeStruct((ng, s, w), F32)],
        compiler_params=_cp("parallel", "parallel", "parallel"))(q3, k3, k3, v3, v3)


def _band_combine(o3, l3, name):
    ng, s, w = o3.shape
    ts = _rows(s)

    def body(o_ref, l_ref, oa_ref, lt_ref):
        ls = [l_ref[g] for g in range(ng)]
        m = functools.reduce(jnp.maximum, ls)
        es = [jnp.exp(l - m) for l in ls]
        den = functools.reduce(lambda a, b: a + b, es)
        num = functools.reduce(lambda a, b: a + b, [es[g] * o_ref[g] for g in range(ng)])
        oa_ref[...] = (num / den).astype(BF)
        lt_ref[...] = m + jnp.log(den)

    blk3 = pl.BlockSpec((ng, ts, w), lambda i: (0, i, 0))
    blk = pl.BlockSpec((ts, w), lambda i: (i, 0))
    return pl.pallas_call(
        body, name=name, grid=(s // ts,), in_specs=[blk3, blk3], out_specs=[blk, blk],
        out_shape=[jax.ShapeDtypeStruct((s, w), BF), jax.ShapeDtypeStruct((s, w), F32)],
        compiler_params=_cp("parallel"))(o3, l3)


def _band_bwd(q3, k3, v3, do3, oa3, lt3, name):
    ng, s, w = q3.shape
    nb = s // BLOCK
    npair = w // LANES
    nsub = BAND_SUB
    tile = nsub * BLOCK

    def body(q_ref, qn_ref, kc_ref, kp_ref, vc_ref, vp_ref, do_ref, don_ref, oa_ref, oan_ref, lt_ref, ltn_ref,
             dq_ref, dk_ref, dv_ref):
        g = pl.program_id(0)
        t = pl.program_id(2)
        nbl = jnp.right_shift(nb, 2 * g)
        mc = _band_masks(None, False)
        hm = _head_masks()

        def block(ref, edge_ref, i):
            if i < 0 or i >= nsub:
                return edge_ref[...]
            return ref[i * BLOCK:(i + 1) * BLOCK, :]

        for i in range(nsub):
            b = t * nsub + i
            mp = _band_masks(jnp.bitwise_and(b, nbl - 1) != 0, True)
            mn = _band_masks(jnp.bitwise_and(b + 1, nbl - 1) != 0, True)
            q, qn = block(q_ref, None, i), block(q_ref, qn_ref, i + 1)
            kc, kp = block(kc_ref, None, i), block(kc_ref, kp_ref, i - 1)
            vc, vp = block(vc_ref, None, i), block(vc_ref, vp_ref, i - 1)
            do, don = block(do_ref, None, i), block(do_ref, don_ref, i + 1)
            dd = do.astype(F32) * block(oa_ref, None, i).astype(F32)
            ddn = don.astype(F32) * block(oa_ref, oan_ref, i + 1).astype(F32)
            lt, ltn = block(lt_ref, None, i), block(lt_ref, ltn_ref, i + 1)
            dq = jnp.zeros((BLOCK, LANES), F32)
            dk = jnp.zeros((BLOCK, LANES), F32)
            dv = jnp.zeros((BLOCK, LANES), F32)
            for h in range(2):
                qh, doh = _sel(hm[h], q), _sel(hm[h], do)
                qnh, donh = _sel(hm[h], qn), _sel(hm[h], don)
                kch, kph = _sel(hm[h], kc), _sel(hm[h], kp)
                lse = _pick(hm[h], lt, NEG)
                lsen = _pick(hm[h], ltn, NEG)
                dsum = jnp.sum(_sel(hm[h], dd), axis=1, keepdims=True)
                dsumn = jnp.sum(_sel(hm[h], ddn), axis=1, keepdims=True)
                pc = jnp.exp(jnp.where(mc, _dot_nt(qh, kc) * SCALE, NEG) - lse)
                pp = jnp.exp(jnp.where(mp, _dot_nt(qh, kp) * SCALE, NEG) - lse)
                dsc = pc * (_dot_nt(doh, vc) - dsum)
                dsp = pp * (_dot_nt(doh, vp) - dsum)
                dq = dq + SCALE * (_dot(dsc.astype(BF), kch) + _dot(dsp.astype(BF), kph))
                pn = jnp.exp(jnp.where(mn, _dot_nt(qnh, kc) * SCALE, NEG) - lsen)
                dsn = pn * (_dot_nt(donh, vc) - dsumn)
                dk = dk + SCALE * (_dot_tn(dsc.astype(BF), qh) + _dot_tn(dsn.astype(BF), qnh))
                dv = dv + _dot_tn(pc.astype(BF), doh) + _dot_tn(pn.astype(BF), donh)
            rows = slice(i * BLOCK, (i + 1) * BLOCK)
            dq_ref[rows, :] = dq
            dk_ref[rows, :] = dk
            dv_ref[rows, :] = dv

    cur = pl.BlockSpec((None, tile, LANES), lambda g, p, t: (g, t, p))
    prv = pl.BlockSpec((None, BLOCK, LANES), lambda g, p, t: (g, jnp.maximum(t * nsub - 1, 0), p))
    nxt = pl.BlockSpec((None, BLOCK, LANES), lambda g, p, t: (g, jnp.minimum(t * nsub + nsub, nb - 1), p))
    out = jax.ShapeDtypeStruct((ng, s, w), F32)
    return pl.pallas_call(
        body, name=name, grid=(ng, npair, nb // nsub),
        in_specs=[cur, nxt, cur, prv, cur, prv, cur, nxt, cur, nxt, cur, nxt],
        out_specs=[cur, cur, cur], out_shape=[out, out, out],
        compiler_params=_cp("parallel", "parallel", "parallel"))(
            q3, q3, k3, k3, v3, v3, do3, do3, oa3, oa3, lt3, lt3)


def _mem_fwd(hb, q_blk0, kv, name):
    s = hb.shape[0]
    m = kv.shape[0]
    tq = _rows(s)
    npair = MEM_W // LANES

    def body(q_ref, k_ref, v_ref, o_ref, l_ref):
        q, k, v = q_ref[...], k_ref[...], v_ref[...]
        hm = _head_masks()
        o = jnp.zeros((tq, LANES), F32)
        lse_w = jnp.zeros((tq, LANES), F32)
        for h in range(2):
            sc = _dot_nt(_sel(hm[h], q), k) * SCALE
            mx = jnp.max(sc, axis=1, keepdims=True)
            p = jnp.exp(sc - mx)
            l = jnp.sum(p, axis=1, keepdims=True)
            o = o + _dot(p.astype(BF), _sel(hm[h], v)) / l
            lse_w = jnp.where(hm[h], mx + jnp.log(l), lse_w)
        o_ref[...] = o.astype(BF)
        l_ref[...] = lse_w

    blk = pl.BlockSpec((tq, LANES), lambda p, i: (i, p))
    return pl.pallas_call(
        body, name=name, grid=(npair, s // tq),
        in_specs=[pl.BlockSpec((tq, LANES), lambda p, i: (i, q_blk0 + p)),
                  pl.BlockSpec((m, LANES), lambda p, i: (0, p)),
                  pl.BlockSpec((m, LANES), lambda p, i: (0, npair + p))],
        out_specs=[blk, blk],
        out_shape=[jax.ShapeDtypeStruct((s, MEM_W), BF), jax.ShapeDtypeStruct((s, MEM_W), F32)],
        compiler_params=_cp("parallel", "parallel"))(hb, kv, kv)


def _mem_bwd(hb, q_blk0, kv, dcat, cat, o_blk0, lse, name):
    s = hb.shape[0]
    m = kv.shape[0]
    tq = _rows(s)
    npair = MEM_W // LANES

    def body(q_ref, k_ref, v_ref, do_ref, o_ref, l_ref, dq_ref, dk_ref, dv_ref):
        i = pl.program_id(1)

        @pl.when(i == 0)
        def _():
            dk_ref[...] = jnp.zeros_like(dk_ref)
            dv_ref[...] = jnp.zeros_like(dv_ref)

        q, k, v, do = q_ref[...], k_ref[...], v_ref[...], do_ref[...]
        dd = do.astype(F32) * o_ref[...].astype(F32)
        lt = l_ref[...]
        hm = _head_masks()
        dq = jnp.zeros((tq, LANES), F32)
        dk = jnp.zeros((m, LANES), F32)
        dv = jnp.zeros((m, LANES), F32)
        for h in range(2):
            qh, doh = _sel(hm[h], q), _sel(hm[h], do)
            p = jnp.exp(_dot_nt(qh, k) * SCALE - _pick(hm[h], lt, NEG))
            ds = p * (_dot_nt(doh, v) - jnp.sum(_sel(hm[h], dd), axis=1, keepdims=True))
            dq = dq + SCALE * _dot(ds.astype(BF), _sel(hm[h], k))
            dk = dk + SCALE * _dot_tn(ds.astype(BF), qh)
            dv = dv + _dot_tn(p.astype(BF), doh)
        dq_ref[...] = dq
        dk_ref[...] += dk
        dv_ref[...] += dv

    row = pl.BlockSpec((tq, LANES), lambda p, i: (i, p))
    orow = pl.BlockSpec((tq, LANES), lambda p, i: (i, o_blk0 + p))
    acc = pl.BlockSpec((m, LANES), lambda p, i: (0, p))
    return pl.pallas_call(
        body, name=name, grid=(npair, s // tq),
        in_specs=[pl.BlockSpec((tq, LANES), lambda p, i: (i, q_blk0 + p)),
                  pl.BlockSpec((m, LANES), lambda p, i: (0, p)),
                  pl.BlockSpec((m, LANES), lambda p, i: (0, npair + p)), orow, orow, row],
        out_specs=[row, acc, acc],
        out_shape=[jax.ShapeDtypeStruct((s, MEM_W), F32), jax.ShapeDtypeStruct((m, MEM_W), F32),
                   jax.ShapeDtypeStruct((m, MEM_W), F32)],
        compiler_params=_cp("parallel", "arbitrary"))(hb, kv, kv, dcat, cat, lse)


def _gate_fwd(f_t, bias, name):
    hp, s = f_t.shape
    nblk = s // LANES

    def body(f_ref, b_ref, c_ref):
        lane = lax.broadcasted_iota(jnp.int32, (hp, LANES), 1)

        def step(i, carry):
            off = pl.multiple_of(i * LANES, LANES)
            x = f_ref[:, pl.ds(off, LANES)] + b_ref[...]
            acc = jnp.minimum(x, 0.0) - jnp.log(1.0 + jnp.exp(-jnp.abs(x)))
            sh = 1
            while sh < LANES:
                acc = acc + jnp.where(lane >= sh, pltpu.roll(acc, sh, 1), 0.0)
                sh *= 2
            acc = acc + carry
            c_ref[:, pl.ds(off, LANES)] = acc
            return acc[:, LANES - 1:LANES]

        lax.fori_loop(0, nblk, step, jnp.zeros((hp, 1), F32))

    vm = pl.BlockSpec(memory_space=pltpu.VMEM)
    return pl.pallas_call(body, name=name, in_specs=[vm, vm], out_specs=vm,
                          out_shape=jax.ShapeDtypeStruct((hp, s), F32),
                          compiler_params=pltpu.CompilerParams(vmem_limit_bytes=VMEM_LIMIT))(f_t, bias)


def _gate_bwd(dc_t, f_t, bias, name):
    hp, s = f_t.shape
    nblk = s // LANES

    def body(dc_ref, f_ref, b_ref, df_ref, db_ref):
        lane = lax.broadcasted_iota(jnp.int32, (hp, LANES), 1)

        def step(t, carry):
            suffix, dbias = carry
            off = pl.multiple_of((nblk - 1 - t) * LANES, LANES)
            acc = dc_ref[:, pl.ds(off, LANES)]
            sh = 1
            while sh < LANES:
                acc = acc + jnp.where(lane < LANES - sh, pltpu.roll(acc, LANES - sh, 1), 0.0)
                sh *= 2
            acc = acc + suffix
            x = f_ref[:, pl.ds(off, LANES)] + b_ref[...]
            df = acc * _sigmoid(-x)
            df_ref[:, pl.ds(off, LANES)] = df
            return acc[:, 0:1], dbias + jnp.sum(df, axis=1, keepdims=True)

        _, dbias = lax.fori_loop(0, nblk, step, (jnp.zeros((hp, 1), F32), jnp.zeros((hp, 1), F32)))
        db_ref[...] = dbias

    vm = pl.BlockSpec(memory_space=pltpu.VMEM)
    return pl.pallas_call(body, name=name, in_specs=[vm, vm, vm], out_specs=[vm, vm],
                          out_shape=[jax.ShapeDtypeStruct((hp, s), F32), jax.ShapeDtypeStruct((hp, 1), F32)],
                          compiler_params=pltpu.CompilerParams(vmem_limit_bytes=VMEM_LIMIT))(dc_t, f_t, bias)


def _wide(rep, width):
    return jnp.tile(rep, (1, width // LANES))


def _fold(t):
    part = t[:, :LANES]
    for c in range(1, t.shape[1] // LANES):
        part = part + t[:, c * LANES:(c + 1) * LANES]
    return part


def _fox_logits(q, k, cq_rep, ck_row, mask, hmask):
    s = _dot_nt(_sel(hmask, q), k) + (_wide(cq_rep, ck_row.shape[1]) - ck_row)
    if mask is not None:
        s = jnp.where(mask, s, NEG)
    return s


def _diag_mask(t):
    return lax.broadcasted_iota(jnp.int32, (t, t), 1) <= lax.broadcasted_iota(jnp.int32, (t, t), 0)


def _fox_fwd(hb, c_rep, c_t3, name):
    s = hb.shape[0]
    npair = MIX_W // LANES
    tq = tk = _rows(s)
    nq = s // tq

    def body(q_ref, k_ref, v_ref, cq_ref, ck_ref, o_ref, l_ref, m_s, l_s, acc):
        qi = pl.program_id(1)
        kj = pl.program_id(2)
        hm = _head_masks()

        @pl.when(kj == 0)
        def _():
            m_s[...] = jnp.full_like(m_s, NEG)
            l_s[...] = jnp.zeros_like(l_s)
            acc[...] = jnp.zeros_like(acc)

        def step(mask):
            q, k, v = q_ref[...] * SCALE, k_ref[...], v_ref[...]
            ck = ck_ref[...]
            for h in range(2):
                sc = _fox_logits(q, k, cq_ref[h], ck[h:h + 1, :], mask, hm[h])
                m_old = m_s[h]
                m_new = jnp.maximum(m_old, jnp.max(sc, axis=1, keepdims=True))
                pr = jnp.exp(sc - _wide(m_new, tk))
                corr = jnp.exp(m_old - m_new)
                l_s[h] = l_s[h] * corr + _fold(pr)
                acc[h] = acc[h] * corr + _dot(pr.astype(BF), _sel(hm[h], v))
                m_s[h] = m_new

        @pl.when(kj < qi)
        def _():
            step(None)

        @pl.when(kj == qi)
        def _():
            step(_diag_mask(tq))
            outs = []
            for h in range(2):
                den = jnp.sum(l_s[h], axis=1, keepdims=True)
                outs.append(acc[h] / den)
                l_ref[h] = m_s[h] + jnp.log(den)
            o_ref[...] = jnp.where(hm[0], outs[0], outs[1]).astype(BF)

    def kv_map(off):
        return lambda p, i, j: (jnp.minimum(j, i), off + p)

    blk = pl.BlockSpec((tq, LANES), lambda p, i, j: (i, p))
    return pl.pallas_call(
        body, name=name, grid=(npair, nq, nq),
        in_specs=[blk, pl.BlockSpec((tk, LANES), kv_map(npair)), pl.BlockSpec((tk, LANES), kv_map(2 * npair)),
                  pl.BlockSpec((2, tq, LANES), lambda p, i, j: (p, i, 0)),
                  pl.BlockSpec((None, 2, tk), lambda p, i, j: (p, 0, jnp.minimum(j, i)))],
        out_specs=[blk, pl.BlockSpec((2, tq, LANES), lambda p, i, j: (p, i, 0))],
        out_shape=[jax.ShapeDtypeStruct((s, MIX_W), BF), jax.ShapeDtypeStruct((2 * npair, s, LANES), F32)],
        scratch_shapes=[pltpu.VMEM((2, tq, LANES), F32), pltpu.VMEM((2, tq, LANES), F32),
                        pltpu.VMEM((2, tq, LANES), F32)],
        compiler_params=_cp("parallel", "parallel", "arbitrary"))(hb, hb, hb, c_rep, c_t3)


def _fox_dsum(hb, dcat, lse, c_rep, c_t3, name):
    s = hb.shape[0]
    npair = MIX_W // LANES
    tq = tk = _rows(s)
    nq = s // tq

    def body(q_ref, k_ref, v_ref, do_ref, l_ref, cq_ref, ck_ref, d_ref, acc):
        qi = pl.program_id(1)
        kj = pl.program_id(2)
        hm = _head_masks()

        @pl.when(kj == 0)
        def _():
            acc[...] = jnp.zeros_like(acc)

        def step(mask):
            q, k, v, do = q_ref[...] * SCALE, k_ref[...], v_ref[...], do_ref[...]
            ck = ck_ref[...]
            for h in range(2):
                pr = jnp.exp(_fox_logits(q, k, cq_ref[h], ck[h:h + 1, :], mask, hm[h]) - _wide(l_ref[h], tk))
                acc[h] += _fold(pr * _dot_nt(_sel(hm[h], do), v))

        @pl.when(kj < qi)
        def _():
            step(None)

        @pl.when(kj == qi)
        def _():
            step(_diag_mask(tq))
            for h in range(2):
                d_ref[h] = jnp.broadcast_to(jnp.sum(acc[h], axis=1, keepdims=True), (tq, LANES))

    def kv_map(off):
        return lambda p, i, j: (jnp.minimum(j, i), off + p)

    blk = pl.BlockSpec((tq, LANES), lambda p, i, j: (i, p))
    rep = pl.BlockSpec((2, tq, LANES), lambda p, i, j: (p, i, 0))
    return pl.pallas_call(
        body, name=name, grid=(npair, nq, nq),
        in_specs=[blk, pl.BlockSpec((tk, LANES), kv_map(npair)), pl.BlockSpec((tk, LANES), kv_map(2 * npair)),
                  blk, rep, rep, pl.BlockSpec((None, 2, tk), lambda p, i, j: (p, 0, jnp.minimum(j, i)))],
        out_specs=rep, out_shape=jax.ShapeDtypeStruct((2 * npair, s, LANES), F32),
        scratch_shapes=[pltpu.VMEM((2, tq, LANES), F32)],
        compiler_params=_cp("parallel", "parallel", "arbitrary"))(hb, hb, hb, dcat, lse, c_rep, c_t3)


def _fox_bwd(hb, dcat, dsum, lse, c_rep, c_t3, name):
    s = hb.shape[0]
    npair = MIX_W // LANES
    tq = tk = _rows(s)
    nq = s // tq

    def body(q_ref, k_ref, v_ref, do_ref, d_ref, l_ref, cq_ref, ck_ref, dq_ref, dk_ref, dv_ref, dc_ref):
        kj = pl.program_id(1)
        qi = pl.program_id(2)
        hm = _head_masks()

        @pl.when(qi == 0)
        def _():
            dk_ref[...] = jnp.zeros_like(dk_ref)
            dv_ref[...] = jnp.zeros_like(dv_ref)
            dc_ref[...] = jnp.zeros_like(dc_ref)

        @pl.when((qi == 0) & (kj == 0))
        def _():
            dq_ref[...] = jnp.zeros_like(dq_ref)

        def step(mask):
            q, k, v, do = q_ref[...] * SCALE, k_ref[...], v_ref[...], do_ref[...]
            ck = ck_ref[...]
            dq = jnp.zeros((tq, LANES), F32)
            dk = jnp.zeros((tk, LANES), F32)
            dv = jnp.zeros((tk, LANES), F32)
            dcs = []
            for h in range(2):
                qh, doh = _sel(hm[h], q), _sel(hm[h], do)
                pr = jnp.exp(_fox_logits(q, k, cq_ref[h], ck[h:h + 1, :], mask, hm[h]) - _wide(l_ref[h], tk))
                ds = pr * (_dot_nt(doh, v) - _wide(d_ref[h], tk))
                dsb = ds.astype(BF)
                dq = dq + _dot(dsb, _sel(hm[h], k))
                dk = dk + _dot_tn(dsb, qh)
                dv = dv + _dot_tn(pr.astype(BF), doh)
                dcs.append(jnp.sum(ds, axis=0, keepdims=True))
            rows = pl.ds(pl.multiple_of(qi * tq, tq), tq)
            dq_ref[rows, :] += SCALE * dq
            dk_ref[...] += dk
            dv_ref[...] += dv
            dc_ref[...] -= jnp.concatenate(dcs, axis=0)

        @pl.when(qi > kj)
        def _():
            step(None)

        @pl.when(qi == kj)
        def _():
            step(_diag_mask(tq))

    def q_map(p, j, i):
        return (jnp.maximum(i, j), p)

    kblk = pl.BlockSpec((tk, LANES), lambda p, j, i: (j, p))
    rep = pl.BlockSpec((2, tq, LANES), lambda p, j, i: (p, jnp.maximum(i, j), 0))
    return pl.pallas_call(
        body, name=name, grid=(npair, nq, nq),
        in_specs=[pl.BlockSpec((tq, LANES), q_map),
                  pl.BlockSpec((tk, LANES), lambda p, j, i: (j, npair + p)),
                  pl.BlockSpec((tk, LANES), lambda p, j, i: (j, 2 * npair + p)),
                  pl.BlockSpec((tq, LANES), q_map), rep, rep, rep,
                  pl.BlockSpec((None, 2, tk), lambda p, j, i: (p, 0, j))],
        out_specs=[pl.BlockSpec((s, LANES), lambda p, j, i: (0, p)), kblk, kblk,
                   pl.BlockSpec((None, 2, tk), lambda p, j, i: (p, 0, j))],
        out_shape=[jax.ShapeDtypeStruct((s, MIX_W), F32), jax.ShapeDtypeStruct((s, MIX_W), F32),
                   jax.ShapeDtypeStruct((s, MIX_W), F32), jax.ShapeDtypeStruct((npair, 2, s), F32)],
        compiler_params=_cp("arbitrary", "arbitrary", "arbitrary"))(hb, hb, hb, dcat, dsum, lse, c_rep, c_t3)


def _foxt_logits(q, k, cq_row, ck_rep, mask, hmask):
    s = _dot_nt(_sel(hmask, k), q) + (cq_row - _wide(ck_rep, q.shape[0]))
    if mask is not None:
        s = jnp.where(mask, s, NEG)
    return s


def _causal_t(qi, kj, tq, tk):
    return (kj * tk + lax.broadcasted_iota(jnp.int32, (tk, tq), 0)
            <= qi * tq + lax.broadcasted_iota(jnp.int32, (tk, tq), 1))


def _fox_tiles(s):
    tq = _rows(s, 1024)
    return tq, tq // 2, s // tq


def _count_ge(t, bounds):
    return sum([(t >= b).astype(jnp.int32) for b in bounds], jnp.int32(0))


def _sweep_q_major(t, nq):
    qi = _count_ge(t, [r * (r + 1) for r in range(1, nq)])
    return qi, t - qi * (qi + 1)


def _sweep_k_major(t, nq):
    counts = [nq - j // 2 for j in range(2 * nq)]
    offs = [sum(counts[:j]) for j in range(1, 2 * nq)]
    kj = _count_ge(t, offs)
    start = sum([jnp.where(t >= o, c, 0) for o, c in zip(offs, counts)], jnp.int32(0))
    qi = kj // 2 + (t - start)
    return kj, qi, t == start, qi == nq - 1


def _foxt_fwd(hb, c_rep, c_t3, name):
    s = hb.shape[0]
    npair = MIX_W // LANES
    tq, tk, nq = _fox_tiles(s)

    def body(q_ref, k_ref, v_ref, cq_ref, ck_ref, o_ref, l_ref, m_s, l_s, acc):
        qi, kj = _sweep_q_major(pl.program_id(1), nq)
        hm = _head_masks()

        @pl.when(kj == 0)
        def _():
            m_s[...] = jnp.full_like(m_s, NEG)
            l_s[...] = jnp.zeros_like(l_s)
            acc[...] = jnp.zeros_like(acc)

        def step(mask):
            q, k = q_ref[...] * SCALE, k_ref[...]
            vt = jnp.transpose(v_ref[...])
            cq = cq_ref[...]
            for h in range(2):
                st = _foxt_logits(q, k, cq[h:h + 1, :], ck_ref[h], mask, hm[h])
                m_old = m_s[h]
                m_new = jnp.maximum(m_old, jnp.max(st, axis=0, keepdims=True))
                pt = jnp.exp(st - m_new)
                corr = jnp.exp(m_old - m_new)
                l_s[h] = l_s[h] * corr + jnp.sum(pt, axis=0, keepdims=True)
                acc[h] = acc[h] * corr + _dot(vt[h * HEAD_DIM:(h + 1) * HEAD_DIM, :], pt.astype(BF))
                m_s[h] = m_new

        @pl.when(kj < 2 * qi)
        def _():
            step(None)

        @pl.when(kj >= 2 * qi)
        def _():
            step(_causal_t(qi, kj, tq, tk))

        @pl.when(kj == 2 * qi + 1)
        def _():
            outs = []
            for h in range(2):
                outs.append(acc[h] / l_s[h])
                l_ref[h:h + 1, :] = m_s[h] + jnp.log(l_s[h])
            o_ref[...] = jnp.transpose(jnp.concatenate(outs, axis=0)).astype(BF)

    def q_map(p, t):
        return (_sweep_q_major(t, nq)[0], p)

    def kv_map(off):
        return lambda p, t: (_sweep_q_major(t, nq)[1], off + p)

    blk = pl.BlockSpec((tq, LANES), q_map)
    row = pl.BlockSpec((None, 2, tq), lambda p, t: (p, 0, _sweep_q_major(t, nq)[0]))
    return pl.pallas_call(
        body, name=name, grid=(npair, nq * (nq + 1)),
        in_specs=[blk, pl.BlockSpec((tk, LANES), kv_map(npair)), pl.BlockSpec((tk, LANES), kv_map(2 * npair)), row,
                  pl.BlockSpec((2, tk, LANES), lambda p, t: (p, _sweep_q_major(t, nq)[1], 0))],
        out_specs=[blk, row],
        out_shape=[jax.ShapeDtypeStruct((s, MIX_W), BF), jax.ShapeDtypeStruct((npair, 2, s), F32)],
        scratch_shapes=[pltpu.VMEM((2, 1, tq), F32), pltpu.VMEM((2, 1, tq), F32),
                        pltpu.VMEM((2, HEAD_DIM, tq), F32)],
        compiler_params=_cp("parallel", "arbitrary"))(hb, hb, hb, c_t3, c_rep)


def _foxt_dsum(hb, dcat, lse, c_rep, c_t3, name):
    s = hb.shape[0]
    npair = MIX_W // LANES
    tq, tk, nq = _fox_tiles(s)

    def body(q_ref, k_ref, v_ref, do_ref, l_ref, cq_ref, ck_ref, d_ref, acc):
        qi, kj = _sweep_q_major(pl.program_id(1), nq)
        hm = _head_masks()

        @pl.when(kj == 0)
        def _():
            acc[...] = jnp.zeros_like(acc)

        def step(mask):
            q, k, v, do = q_ref[...] * SCALE, k_ref[...], v_ref[...], do_ref[...]
            cq, lse_rows = cq_ref[...], l_ref[...]
            for h in range(2):
                pt = jnp.exp(_foxt_logits(q, k, cq[h:h + 1, :], ck_ref[h], mask, hm[h]) - lse_rows[h:h + 1, :])
                acc[h] += jnp.sum(pt * _dot_nt(_sel(hm[h], v), do), axis=0, keepdims=True)

        @pl.when(kj < 2 * qi)
        def _():
            step(None)

        @pl.when(kj >= 2 * qi)
        def _():
            step(_causal_t(qi, kj, tq, tk))

        @pl.when(kj == 2 * qi + 1)
        def _():
            for h in range(2):
                d_ref[h:h + 1, :] = acc[h]

    def q_map(p, t):
        return (_sweep_q_major(t, nq)[0], p)

    def kv_map(off):
        return lambda p, t: (_sweep_q_major(t, nq)[1], off + p)

    blk = pl.BlockSpec((tq, LANES), q_map)
    row = pl.BlockSpec((None, 2, tq), lambda p, t: (p, 0, _sweep_q_major(t, nq)[0]))
    return pl.pallas_call(
        body, name=name, grid=(npair, nq * (nq + 1)),
        in_specs=[blk, pl.BlockSpec((tk, LANES), kv_map(npair)), pl.BlockSpec((tk, LANES), kv_map(2 * npair)),
                  blk, row, row, pl.BlockSpec((2, tk, LANES), lambda p, t: (p, _sweep_q_major(t, nq)[1], 0))],
        out_specs=row, out_shape=jax.ShapeDtypeStruct((npair, 2, s), F32),
        scratch_shapes=[pltpu.VMEM((2, 1, tq), F32)],
        compiler_params=_cp("parallel", "arbitrary"))(hb, hb, hb, dcat, lse, c_t3, c_rep)


def _foxt_bwd(hb, dcat, dsum, lse, c_rep, c_t3, name):
    s = hb.shape[0]
    npair = MIX_W // LANES
    tq, tk, nq = _fox_tiles(s)

    def body(q_ref, k_ref, v_ref, do_ref, d_ref, l_ref, cq_ref, ck_ref, dq_ref, dk_ref, dv_ref, dc_ref, dc_s):
        t = pl.program_id(1)
        kj, qi, first, last = _sweep_k_major(t, nq)
        hm = _head_masks()

        @pl.when(first)
        def _():
            dk_ref[...] = jnp.zeros_like(dk_ref)
            dv_ref[...] = jnp.zeros_like(dv_ref)
            dc_s[...] = jnp.zeros_like(dc_s)

        @pl.when(t == 0)
        def _():
            dq_ref[...] = jnp.zeros_like(dq_ref)

        def step(mask):
            q, k, v, do = q_ref[...] * SCALE, k_ref[...], v_ref[...], do_ref[...]
            qt, kt, dot = jnp.transpose(q), jnp.transpose(k), jnp.transpose(do)
            cq, lse_rows, d_rows = cq_ref[...], l_ref[...], d_ref[...]
            dqs, dks, dvs = [], [], []
            for h in range(2):
                rows = slice(h * HEAD_DIM, (h + 1) * HEAD_DIM)
                pt = jnp.exp(_foxt_logits(q, k, cq[h:h + 1, :], ck_ref[h], mask, hm[h]) - lse_rows[h:h + 1, :])
                dst = pt * (_dot_nt(_sel(hm[h], v), do) - d_rows[h:h + 1, :])
                dsb = dst.astype(BF)
                dqs.append(_dot(kt[rows, :], dsb))
                dks.append(_dot_nt(qt[rows, :], dsb))
                dvs.append(_dot_nt(dot[rows, :], pt.astype(BF)))
                dc_s[h] += _fold(dst)
            cols = pl.ds(pl.multiple_of(qi * tq, tq), tq)
            dq_ref[:, cols] += SCALE * jnp.concatenate(dqs, axis=0)
            dk_ref[...] += jnp.concatenate(dks, axis=0)
            dv_ref[...] += jnp.concatenate(dvs, axis=0)

        @pl.when(kj < 2 * qi)
        def _():
            step(None)

        @pl.when(kj >= 2 * qi)
        def _():
            step(_causal_t(qi, kj, tq, tk))

        @pl.when(last)
        def _():
            for h in range(2):
                dc_ref[h:h + 1, :] = -jnp.sum(jnp.transpose(dc_s[h]), axis=0, keepdims=True)

    def kj_of(t):
        return _sweep_k_major(t, nq)[0]

    def qi_of(t):
        return _sweep_k_major(t, nq)[1]

    qblk = pl.BlockSpec((tq, LANES), lambda p, t: (qi_of(t), p))
    row = pl.BlockSpec((None, 2, tq), lambda p, t: (p, 0, qi_of(t)))
    kblk = pl.BlockSpec((LANES, tk), lambda p, t: (p, kj_of(t)))
    rep = pl.BlockSpec((2, tk, LANES), lambda p, t: (p, kj_of(t), 0))
    return pl.pallas_call(
        body, name=name, grid=(npair, nq * (nq + 1)),
        in_specs=[qblk,
                  pl.BlockSpec((tk, LANES), lambda p, t: (kj_of(t), npair + p)),
                  pl.BlockSpec((tk, LANES), lambda p, t: (kj_of(t), 2 * npair + p)),
                  qblk, row, row, row, rep],
        out_specs=[pl.BlockSpec((LANES, s), lambda p, t: (p, 0)), kblk, kblk,
                   pl.BlockSpec((None, 2, tk), lambda p, t: (p, 0, kj_of(t)))],
        out_shape=[jax.ShapeDtypeStruct((MIX_W, s), F32), jax.ShapeDtypeStruct((MIX_W, s), F32),
                   jax.ShapeDtypeStruct((MIX_W, s), F32), jax.ShapeDtypeStruct((npair, 2, s), F32)],
        scratch_shapes=[pltpu.VMEM((2, tk, LANES), F32)],
        compiler_params=_cp("arbitrary", "arbitrary"))(hb, hb, hb, dcat, dsum, lse, c_t3, c_rep)


def _loss_head(y, target, name):
    s, d = y.shape
    ts = _rows(s)

    def body(y_ref, t_ref, dy_ref, l_ref):
        i = pl.program_id(0)
        e = y_ref[...] - t_ref[...]
        dy_ref[...] = e * (1.0 / d)

        @pl.when(i == 0)
        def _():
            l_ref[...] = jnp.zeros_like(l_ref)

        part = jnp.sum(jnp.sum(e * e, axis=1, keepdims=True), axis=0, keepdims=True)
        l_ref[...] += part * (0.5 / d)

    row = pl.BlockSpec((ts, d), lambda i: (i, 0))
    return pl.pallas_call(
        body, name=name, grid=(s // ts,), in_specs=[row, row],
        out_specs=[row, pl.BlockSpec((1, 1), lambda i: (0, 0))],
        out_shape=[jax.ShapeDtypeStruct((s, d), F32), jax.ShapeDtypeStruct((1, 1), F32)],
        compiler_params=_cp("arbitrary"))(y, target)


def _adam_rows(r, c):
    cap = max(8, (1 << 20) // (4 * c))
    if r <= cap:
        return r
    best = None
    for t in range(8, cap + 1, 8):
        if r % t == 0:
            best = t
    return best if best is not None else r


def _reduce_adamw(contribs, w, m, v, name):
    nl = len(contribs)
    nd, r, c = contribs[0].shape
    tr = _adam_rows(r, c)
    bc1 = 1.0 - ADAM_B1 ** ADAM_STEP
    bc2 = 1.0 - ADAM_B2 ** ADAM_STEP

    def body(*refs):
        c_refs = refs[:nl]
        w_ref, m_ref, v_ref, g_ref, d_ref, nm_ref, nv_ref = refs[nl:]
        l = pl.program_id(0)
        for li in range(nl):
            @pl.when(l == li)
            def _(c_ref=c_refs[li]):
                g = c_ref[0].astype(F32)
                for k in range(1, nd):
                    g = g + c_ref[k].astype(F32)
                nm = ADAM_B1 * m_ref[...] + (1.0 - ADAM_B1) * g
                nv = ADAM_B2 * v_ref[...] + (1.0 - ADAM_B2) * (g * g)
                g_ref[...] = g
                nm_ref[...] = nm
                nv_ref[...] = nv
                d_ref[...] = -ADAM_LR * ((nm / bc1) / (jnp.sqrt(nv / bc2) + ADAM_EPS) + ADAM_WD * w_ref[...])

    def c_spec(li):
        return pl.BlockSpec((nd, tr, c), lambda l, i: (0, jnp.where(l == li, i, 0), 0))

    blk = pl.BlockSpec((None, tr, c), lambda l, i: (l, i, 0))
    out = jax.ShapeDtypeStruct((nl, r, c), F32)
    return pl.pallas_call(
        body, name=name, grid=(nl, r // tr),
        in_specs=[c_spec(li) for li in range(nl)] + [blk, blk, blk],
        out_specs=[blk, blk, blk, blk], out_shape=[out, out, out, out],
        compiler_params=_cp("arbitrary", "arbitrary"))(*contribs, w, m, v)


def _mesh_pos():
    return lax.axis_index("x"), lax.axis_index("y"), lax.axis_index("c")


def _peer(pos, k):
    x, y, c = pos
    return (1 - x if k & 4 else x, 1 - y if k & 2 else y, 1 - c if k & 1 else c)


def _linear(pos):
    return 4 * pos[0] + 2 * pos[1] + pos[2]


def _xfer_copies(srcs, lands, send_sems, recv_sems, local_sems, gather):
    pos = _mesh_pos()
    me = _linear(pos)
    local, remote = [], []
    for i, (src, land) in enumerate(zip(srcs, lands)):
        local.append(pltpu.make_async_copy(src if gather else src.at[me], land.at[me], local_sems.at[i]))
        for k in range(1, N_DEV):
            peer = _peer(pos, k)
            remote.append(pltpu.make_async_remote_copy(
                src_ref=src if gather else src.at[_linear(peer)], dst_ref=land.at[me],
                send_sem=send_sems.at[i * (N_DEV - 1) + k - 1], recv_sem=recv_sems.at[i * (N_DEV - 1) + k - 1],
                device_id=peer, device_id_type=MESH_ID))
    return local, remote


_HBM = pl.BlockSpec(memory_space=pltpu.HBM)
_SEM = pl.BlockSpec(memory_space=pltpu.SEMAPHORE)
_EFFECT = pltpu.SideEffectType.DATAFLOW_SIDE_EFFECTING


def _xfer_start(srcs, gather, name, after=()):
    n = len(srcs)
    na = len(after)
    lands = [lax.empty(((N_DEV,) + a.shape) if gather else a.shape, a.dtype) for a in srcs]

    def body(*refs):
        src, land = refs[:n], refs[n:2 * n]
        send_sems, recv_sems, local_sems = refs[2 * n + na:2 * n + na + 3]
        local, remote = _xfer_copies(src, land, send_sems, recv_sems, local_sems, gather)
        for cp in local + remote:
            cp.start()
        refs[-1][...] = jnp.zeros_like(refs[-1])

    nsem = n * (N_DEV - 1)
    out = pl.pallas_call(
        body, name=name,
        out_shape=(pltpu.SemaphoreType.DMA((nsem,)), pltpu.SemaphoreType.DMA((nsem,)), pltpu.SemaphoreType.DMA((n,)),
                   *[pltpu.HBM(a.shape, a.dtype) for a in srcs], *[pltpu.HBM(a.shape, a.dtype) for a in lands],
                   jax.ShapeDtypeStruct((8, LANES), F32)),
        in_specs=[_HBM] * (2 * n) + [pl.BlockSpec(memory_space=pl.ANY)] * na,
        out_specs=(_SEM, _SEM, _SEM, *[_HBM] * (2 * n), pl.BlockSpec(memory_space=pltpu.VMEM)),
        input_output_aliases={i: 3 + i for i in range(2 * n)},
        compiler_params=pltpu.CompilerParams(has_side_effects=_EFFECT))(
            *[pltpu.with_memory_space_constraint(a, pltpu.HBM) for a in srcs],
            *[pltpu.with_memory_space_constraint(a, pltpu.HBM) for a in lands], *after)
    return out[:3], list(out[3:3 + n]), list(out[3 + n:3 + 2 * n]), out[-1]


def _started(handle):
    return handle[3]


def _xfer_wait(handle, after, gather, name):
    sems, srcs, lands, _ = handle
    n = len(srcs)

    def body(*refs):
        src, land = refs[:n], refs[n:2 * n]
        send_sems, recv_sems, local_sems = refs[2 * n:2 * n + 3]
        local, remote = _xfer_copies(src, land, send_sems, recv_sems, local_sems, gather)
        for cp in local:
            cp.wait()
        for cp in remote:
            cp.wait_send()
            cp.wait_recv()

    out = pl.pallas_call(
        body, name=name,
        out_shape=(*[pltpu.HBM(a.shape, a.dtype) for a in srcs], *[pltpu.HBM(a.shape, a.dtype) for a in lands]),
        in_specs=[_HBM] * (2 * n) + [_SEM] * 3 + [pl.BlockSpec(memory_space=pl.ANY)] * len(after),
        out_specs=tuple([_HBM] * (2 * n)), input_output_aliases={i: i for i in range(2 * n)},
        compiler_params=pltpu.CompilerParams(has_side_effects=_EFFECT))(*srcs, *lands, *sems, *after)
    return list(out[n:])


def _cols_full(g):
    nd, r, c = g.shape
    return jnp.transpose(g, (1, 0, 2)).reshape(r, nd * c)


def _cols_split(full):
    r, n = full.shape
    return jnp.transpose(full.reshape(r, N_DEV, n // N_DEV), (1, 0, 2))


def _pack_b_in(w):
    qkv = 3 * MIX_W
    pad = jnp.zeros((w.shape[0], B_IN_PAD - w.shape[1]), w.dtype)
    return jnp.concatenate([w[:, :qkv], w[:, qkv + N_MIX_HEADS:], w[:, qkv:qkv + N_MIX_HEADS], pad], axis=1)


def _unpack_b_in(w):
    qkv = 3 * MIX_W
    return jnp.concatenate([w[:, :qkv], w[:, qkv + MEM_W:qkv + MEM_W + N_MIX_HEADS], w[:, qkv:qkv + MEM_W]], axis=1)


def _to_classes(t, g):
    r = 4 ** g
    s, w = t.shape
    return jnp.transpose(t.reshape(s // r, r, w), (1, 0, 2)).reshape(s, w)


def _from_classes(t, g):
    r = 4 ** g
    s, w = t.shape
    return jnp.transpose(t.reshape(r, s // r, w), (1, 0, 2)).reshape(s, w)


def _group_stack(t):
    return jnp.stack([_to_classes(t[:, g * GROUP_W:(g + 1) * GROUP_W], g) for g in range(N_GROUPS)])


def _group_unstack(t3):
    return jnp.concatenate([_from_classes(t3[g], g) for g in range(N_GROUPS)], axis=1)


def _same_stack(t):
    return jnp.stack([_to_classes(t, g) for g in range(N_GROUPS)])


def _same_unstack(t3):
    return jnp.stack([_from_classes(t3[g], g) for g in range(N_GROUPS)])


def _ffn_forward(x, xb, wgu, get_rest, tag, fused=True):
    if fused:
        wd4, gain, bias = get_rest(x)
        y, yb, gu, a, xh, rstd = _ffn_fwd_main(x, xb, wgu, wd4, gain, bias, f"{tag}_fwd_main")
    else:
        gu, a = _ffn_up(xb, wgu, f"{tag}_up")
        wd4, gain, bias = get_rest(a)
        y, yb, xh, rstd = _mm_res_ln(a, wd4, x, gain, bias, 0.5, f"{tag}_down_ln")
    return y, yb, (xb, gu, a, xh, rstd), wd4


def _ffn_backward(dy, saved, wgu, wd4, gain, tag, after=()):
    xb, gu, a, xh, rstd = saved
    s = xb.shape[0]
    nd, c, d = wgu.shape
    dx, dzb, dh, dgain, dbias = _ffn_bwd_main(dy, xh, rstd, gain, wd4, wgu, gu, f"{tag}_bwd_main", after)
    dwd = _mm_tn(a, dzb[None], f"{tag}_dwd").reshape(nd, wd4.shape[1] // 2, d)
    dwgu = _mm_tn(dh.reshape(nd, s, c), xb[None], f"{tag}_dwgu")
    return dx, dwgu, dwd, dgain, dbias


def _mixer_a_forward(x, xb, memb, w_in, w_kv, w_out, gain, bias, tabs):
    h = _mm_nn(xb, w_in, F32, "a_in", b_rows_out=True)
    hb = _rope_cast([h], tabs, 2 * MIX_W // LANES, "a_rope")
    q3 = _group_stack(hb[:, :MIX_W])
    k3 = _group_stack(hb[:, MIX_W:2 * MIX_W])
    v3 = _group_stack(hb[:, 2 * MIX_W:3 * MIX_W])
    o3, l3 = _band_fwd(q3, k3, v3, "a_band_fwd")
    oa, lt = _band_combine(_same_unstack(o3), _same_unstack(l3), "a_combine")
    kv = _mm_nn(memb, w_kv, BF, "a_mem_kv")
    om, lm = _mem_fwd(hb, 3 * MIX_W // LANES, kv, "a_mem_fwd")
    cat = jnp.concatenate([oa, om], axis=1)
    y, yb, xh, rstd = _mm_res_ln(cat[None], w_out[None], x, gain, bias, 1.0, "a_out_ln")
    return y, yb, (xb, hb, q3, k3, v3, oa, lt, kv, lm, cat, xh, rstd)


def _mixer_a_backward(dy, saved, memb, w_in, w_kv, w_out, gain, tabs_neg, after=()):
    xb, hb, q3, k3, v3, oa, lt, kv, lm, cat, xh, rstd = saved
    dz, dzb, dgain, dbias = _ln_bwd(dy, xh, rstd, gain, 1.0, "a_ln_bwd", after)
    dcat = _mm_nt(dzb[None], w_out[None], "a_dcat", out_dtype=BF)
    dw_out = _mm_tn(cat[None], dzb[None], "a_dwout")[0]
    dqm, dkm, dvm = _mem_bwd(hb, 3 * MIX_W // LANES, kv, dcat, cat, GROUP_W // LANES, lm, "a_mem_bwd")
    dkv = jnp.concatenate([dkm, dvm], axis=1).astype(BF)
    dw_kv = _mm_tn(memb[None], dkv[None], "a_dwkv")[0]
    dq3, dk3, dv3 = _band_bwd(q3, k3, v3, _same_stack(dcat[:, :GROUP_W]), _same_stack(oa), _same_stack(lt),
                              "a_band_bwd")
    dhb = _rope_cast([_group_unstack(dq3), _group_unstack(dk3), _group_unstack(dv3), dqm], tabs_neg,
                     2 * MIX_W // LANES, "a_rope_bwd")
    dw_in = _mm_tn(dhb[None], xb[None], "a_dwin")[0]
    dx = _mm_nt(dhb[None], w_in[None], "a_dx", res=dz, w_rows_out=False)
    return dx, dw_in, dw_kv, dw_out, dgain, dbias


def _pad_rows(t, rows):
    return jnp.concatenate([t, jnp.zeros((rows - t.shape[0], t.shape[1]), t.dtype)], axis=0)


def _pad_cols(t, cols):
    return jnp.concatenate([t, jnp.zeros((t.shape[0], cols - t.shape[1]), t.dtype)], axis=1)


def _mixer_b_forward(x, xb, memb, w_in, fbias, w_kv, w_out, gain, bias, tabs):
    s = x.shape[0]
    h = _mm_nn(xb, w_in, F32, "b_in")
    hb = _rope_cast([h], tabs, 0, "b_cast")
    f0 = 3 * MIX_W + MEM_W
    f_t = _pad_rows(jnp.transpose(h[:, f0:f0 + N_MIX_HEADS]), 16)
    bias16 = _pad_rows(jnp.transpose(fbias), 16)
    c_t = _gate_fwd(f_t, bias16, "b_gate_fwd")
    c_t3 = c_t[:N_MIX_HEADS].reshape(N_MIX_HEADS // 2, 2, s)
    c_rep = jnp.broadcast_to(c_t[:N_MIX_HEADS, :, None], (N_MIX_HEADS, s, LANES))
    ob, lb = _foxt_fwd(hb, c_rep, c_t3, "b_fox_fwd")
    kv = _mm_nn(memb, w_kv, BF, "b_mem_kv")
    om, lm = _mem_fwd(hb, 3 * MIX_W // LANES, kv, "b_mem_fwd")
    cat = jnp.concatenate([ob, om], axis=1)
    y, yb, xh, rstd = _mm_res_ln(cat[None], w_out[None], x, gain, bias, 1.0, "b_out_ln")
    return y, yb, (xb, hb, f_t, bias16, c_rep, c_t3, lb, kv, lm, cat, xh, rstd)


def _mixer_b_backward(dy, saved, memb, w_in, w_kv, w_out, gain, tabs, after=()):
    xb, hb, f_t, bias16, c_rep, c_t3, lb, kv, lm, cat, xh, rstd = saved
    s = xb.shape[0]
    dz, dzb, dgain, dbias = _ln_bwd(dy, xh, rstd, gain, 1.0, "b_ln_bwd", after)
    dcat = _mm_nt(dzb[None], w_out[None], "b_dcat", out_dtype=BF)
    dw_out = _mm_tn(cat[None], dzb[None], "b_dwout")[0]
    dqm, dkm, dvm = _mem_bwd(hb, 3 * MIX_W // LANES, kv, dcat, cat, MIX_W // LANES, lm, "b_mem_bwd")
    dkv = jnp.concatenate([dkm, dvm], axis=1).astype(BF)
    dw_kv = _mm_tn(memb[None], dkv[None], "b_dwkv")[0]
    dsum = _foxt_dsum(hb, dcat, lb, c_rep, c_t3, "b_fox_dsum")
    dq, dk, dv, dc3 = _foxt_bwd(hb, dcat, dsum, lb, c_rep, c_t3, "b_fox_bwd")
    df_t, dfb = _gate_bwd(_pad_rows(dc3.reshape(N_MIX_HEADS, s), 16), f_t, bias16, "b_gate_bwd")
    df = _pad_cols(jnp.transpose(df_t[:N_MIX_HEADS]), B_IN_PAD - 3 * MIX_W - MEM_W)
    dhb = _rope_cast([dq, dk, dv, dqm, df], tabs, 0, "b_cast_bwd", transposed=(0, 1, 2))
    dw_in = _mm_tn(xb[None], dhb[None], "b_dwin")[0]
    dx = _mm_nt(dhb[None], w_in[None], "b_dx", res=dz)
    return dx, dw_in, jnp.transpose(dfb[:N_MIX_HEADS]), dw_kv, dw_out, dgain, dbias


def _stored(t, name):
    return jnp.transpose(t, (0, 2, 1)) if name in ROWS_OUT else t


GATHER_GROUPS = (
    (("ffn1_w_gate_up", 0),),
    (("ffn1_w_down", 0), ("ln_gain", None), ("ln_bias", None)),
    (("a_w_in", 0), ("a_w_out", 0), ("mem_w_kv", 0)),
    (("ffn2_w_gate_up", 0), ("ffn2_w_down", 0)),
    (("ffn1_w_gate_up", 1), ("ffn1_w_down", 1)),
    (("b_w_in", 0), ("b_w_out", 0), ("mem_w_kv", 1)),
    (("ffn2_w_gate_up", 1), ("ffn2_w_down", 1)),
)


def _group_shards(group, params):
    return [t if n in F32_COMM else _stored(t, n)[l].astype(BF) for (n, l), t in zip(group, params)]


def _weight_groups(w):
    return [_group_shards(grp, [w[n] for n, _ in grp]) for grp in GATHER_GROUPS]


def _local_step(x, mem, target, fbias, get_w, put_g):
    s, d = x.shape
    tabs = _rope_tables(s, 1.0)
    tabs_neg = _rope_tables(s, -1.0)
    memb = mem.astype(BF)
    saved, wl = [], []
    cur, curb = x, x.astype(BF)
    ln = []

    def down4(t):
        return t.reshape(N_DEV // 2, -1, d)

    for i in range(DEPTH):
        if i == 0:
            def first_rest(a):
                g = get_w(1, a)
                ln.extend(jnp.transpose(t, (1, 2, 0, 3)).reshape(DEPTH, 3, 1, d) for t in g[1:3])
                return down4(g[0]), ln[0][0, 0], ln[1][0, 0]

            wgu = get_w(0, cur)[0]
            cur, curb, s1, wd = _ffn_forward(cur, curb, wgu, first_rest, "l0_ffn1", fused=False)
        else:
            g = get_w(3 * i + 1, cur)
            wgu = g[0]
            cur, curb, s1, wd = _ffn_forward(cur, curb, wgu, lambda a, g=g: (down4(g[1]), ln[0][i, 0], ln[1][i, 0]),
                                             f"l{i}_ffn1")
        w1 = (wgu, wd)
        ln_g, ln_b = ln
        g = get_w(3 * i + 2, cur)
        if i == 0:
            wm = (g[0].reshape(-1, d), g[2].reshape(d, -1), _cols_full(g[1]))
            cur, curb, s2 = _mixer_a_forward(cur, curb, memb, wm[0], wm[1], wm[2], ln_g[i, 1], ln_b[i, 1], tabs)
        else:
            wm = (_pack_b_in(g[0].reshape(d, -1)), g[2].reshape(d, -1), g[1].reshape(d, -1))
            cur, curb, s2 = _mixer_b_forward(cur, curb, memb, wm[0], fbias, wm[1], wm[2], ln_g[i, 1], ln_b[i, 1],
                                             tabs)
        g = get_w(3 * i + 3, cur)
        cur, curb, s3, wd = _ffn_forward(cur, curb, g[0], lambda a, g=g: (down4(g[1]), ln_g[i, 2], ln_b[i, 2]),
                                         f"l{i}_ffn2")
        w3 = (g[0], wd)
        saved.append((s1, s2, s3))
        wl.append((w1, wm, w3))

    dy, loss = _loss_head(cur, target, "loss_head")

    dgs = [[None] * 3 for _ in range(DEPTH)]
    dbs = [[None] * 3 for _ in range(DEPTH)]
    sent = ()
    for i in reversed(range(DEPTH)):
        s1, s2, s3 = saved[i]
        w1, wm, w3 = wl[i]
        dy, dgu, dd, dgs[i][2], dbs[i][2] = _ffn_backward(dy, s3, w3[0], w3[1], ln_g[i, 2], f"l{i}_ffn2", sent)
        sent = put_g(3 * i + 2, [dgu, dd])
        if i == 0:
            dy, dw_in, dw_kv, dw_out, dgs[i][1], dbs[i][1] = _mixer_a_backward(
                dy, s2, memb, wm[0], wm[1], wm[2], ln_g[i, 1], tabs_neg, sent)
            sent = put_g(1, [dw_in.reshape(N_DEV, -1, d), _cols_split(dw_out),
                             dw_kv.reshape(N_DEV, d // N_DEV, -1)])
        else:
            dy, dw_in, dfb, dw_kv, dw_out, dgs[i][1], dbs[i][1] = _mixer_b_backward(
                dy, s2, memb, wm[0], wm[1], wm[2], ln_g[i, 1], tabs, sent)
            sent = put_g(4, [_unpack_b_in(dw_in).reshape(N_DEV, d // N_DEV, -1),
                             dw_out.reshape(N_DEV, d // N_DEV, -1), dw_kv.reshape(N_DEV, d // N_DEV, -1),
                             jnp.broadcast_to(dfb[None], (N_DEV,) + dfb.shape)])
        dy, dgu, dd, dgs[i][0], dbs[i][0] = _ffn_backward(dy, s1, w1[0], w1[1], ln_g[i, 0], f"l{i}_ffn1", sent)
        if i == 0:
            ln_pieces = []
            for parts in (dgs, dbs):
                t = jnp.concatenate([parts[a][b] for a in range(DEPTH) for b in range(3)], axis=0)
                ln_pieces.append(jnp.transpose(t.reshape(DEPTH * 3, N_DEV, d // N_DEV), (1, 0, 2)))
            sent = put_g(0, [dgu, dd] + ln_pieces)
        else:
            sent = put_g(3, [dgu, dd])
    return loss, dy


WEIGHTS = ("ffn1_w_gate_up", "ffn1_w_down", "ffn2_w_gate_up", "ffn2_w_down", "ln_gain", "ln_bias", "mem_w_kv",
           "a_w_in", "a_w_out", "b_w_in", "b_forget_bias", "b_w_out")
F32_COMM = ("ln_gain", "ln_bias", "b_forget_bias")
ROWS_OUT = ("ffn1_w_gate_up", "ffn2_w_gate_up", "a_w_in")
GRAD_SLOTS = {
    "ffn1_w_gate_up": [(0, 0), (3, 0)], "ffn1_w_down": [(0, 1), (3, 1)],
    "ffn2_w_gate_up": [(2, 0), (5, 0)], "ffn2_w_down": [(2, 1), (5, 1)],
    "ln_gain": [(0, 2)], "ln_bias": [(0, 3)], "mem_w_kv": [(1, 2), (4, 2)],
    "a_w_in": [(1, 0)], "a_w_out": [(1, 1)], "b_w_in": [(4, 0)], "b_forget_bias": [(4, 3)], "b_w_out": [(4, 1)],
}


def kernel(x, mem, ffn1_w_gate_up, ffn1_w_down, ffn2_w_gate_up, ffn2_w_down, ln_gain, ln_bias, mem_w_kv, a_w_in, a_w_out, b_w_in, b_forget_bias, b_w_out, loss_target, m_ffn1_w_gate_up, m_ffn1_w_down, m_ffn2_w_gate_up, m_ffn2_w_down, m_ln_gain, m_ln_bias, m_mem_w_kv, m_a_w_in, m_a_w_out, m_b_w_in, m_b_forget_bias, m_b_w_out, v_ffn1_w_gate_up, v_ffn1_w_down, v_ffn2_w_gate_up, v_ffn2_w_down, v_ln_gain, v_ln_bias, v_mem_w_kv, v_a_w_in, v_a_w_out, v_b_w_in, v_b_forget_bias, v_b_w_out):
    w = dict(zip(WEIGHTS, (ffn1_w_gate_up, ffn1_w_down, ffn2_w_gate_up, ffn2_w_down, ln_gain, ln_bias, mem_w_kv,
                           a_w_in, a_w_out, b_w_in, b_forget_bias, b_w_out)))
    m = dict(zip(WEIGHTS, (m_ffn1_w_gate_up, m_ffn1_w_down, m_ffn2_w_gate_up, m_ffn2_w_down, m_ln_gain, m_ln_bias,
                           m_mem_w_kv, m_a_w_in, m_a_w_out, m_b_w_in, m_b_forget_bias, m_b_w_out)))
    v = dict(zip(WEIGHTS, (v_ffn1_w_gate_up, v_ffn1_w_down, v_ffn2_w_gate_up, v_ffn2_w_down, v_ln_gain, v_ln_bias,
                           v_mem_w_kv, v_a_w_in, v_a_w_out, v_b_w_in, v_b_forget_bias, v_b_w_out)))

    gathers = []
    for k, grp in enumerate(GATHER_GROUPS):
        params, behind = [w[n] for n, _ in grp], [_started(h) for h in gathers[-1:]]
        if behind:
            params, behind = lax.optimization_barrier((params, behind))
        gathers.append(_xfer_start(_group_shards(grp, params), True, f"gather{k}_start", behind))
    exchanges = {}

    def get_w(k, after):
        behind = [after] + ([_started(h) for h in gathers] if k == 0 else [])
        return _xfer_wait(gathers[k], behind, True, f"gather{k}_wait")

    def put_g(k, pieces):
        exchanges[k] = _xfer_start(pieces, False, f"grads{k}_start")
        return (_started(exchanges[k]),)

    loss, grad_x = _local_step(x[0], mem[0], loss_target[0], b_forget_bias, get_w, put_g)
    loss = lax.psum(loss[0, 0], ("x", "y", "c"))

    outs, landed = {}, {}

    def adamw(names, after):
        for n in names:
            contribs = [landed[g][j] for g, j in GRAD_SLOTS[n]]
            view = (len(contribs),) + contribs[0].shape[1:]
            shape = _stored(w[n], n).shape
            res = _reduce_adamw(contribs, *[_stored(t[n], n).reshape(view) for t in (w, m, v)], f"adamw_{n}")
            outs[n] = [_stored(t.reshape(shape), n) for t in res]
            after = outs[n][0]
        return after

    after = [grad_x, _started(exchanges[0])]
    for k in (5, 4, 3, 2, 1):
        landed[k] = _xfer_wait(exchanges[k], after, False, f"grads{k}_wait")
        after = [landed[k][0]]
    done = adamw(("ffn2_w_gate_up", "ffn2_w_down", "mem_w_kv", "a_w_in", "a_w_out", "b_w_in", "b_forget_bias",
                  "b_w_out"), None)
    landed[0] = _xfer_wait(exchanges[0], [done], False, "grads0_wait")
    adamw(("ffn1_w_gate_up", "ffn1_w_down", "ln_gain", "ln_bias"), None)
    return (loss, grad_x[None], *[outs[n][0] for n in WEIGHTS], *[outs[n][1] for n in WEIGHTS],
            *[outs[n][2] for n in WEIGHTS], *[outs[n][3] for n in WEIGHTS])
```

```python
import functools

import jax
import jax.numpy as jnp
from jax import lax
from jax.experimental import pallas as pl
from jax.experimental.pallas import tpu as pltpu

F32 = jnp.float32
BF = jnp.bfloat16
MESH_ID = pl.DeviceIdType.MESH

N_DEV = 8
DEPTH = 2
HEAD_DIM = 64
LANES = 128
N_MIX_HEADS = 12
N_MEM_HEADS = 4
MIX_W = N_MIX_HEADS * HEAD_DIM
MEM_W = N_MEM_HEADS * HEAD_DIM
N_GROUPS = 3
GROUP_W = MIX_W // N_GROUPS
BLOCK = 128
BAND_SUB = 4
ROT_HALF = 8
ROPE_THETA = 500000.0
ALPHA = (2 * DEPTH) ** 0.25
LN_EPS = 1e-5
SCALE = HEAD_DIM ** -0.5
NEG = -1e30
B_IN_PAD = 2688
ADAM_LR, ADAM_B1, ADAM_B2, ADAM_EPS, ADAM_WD, ADAM_STEP = 0.001, 0.9, 0.999, 1e-08, 0.01, 10
VMEM_LIMIT = 56 * 1024 * 1024


def _cp(*sem):
    return pltpu.CompilerParams(dimension_semantics=sem, vmem_limit_bytes=VMEM_LIMIT)


def _dot(a, b):
    return jnp.dot(a, b, preferred_element_type=F32)


def _dot_nt(a, b):
    return lax.dot_general(a, b, (((1,), (1,)), ((), ())), preferred_element_type=F32)


def _dot_tn(a, b):
    return lax.dot_general(a, b, (((0,), (0,)), ((), ())), preferred_element_type=F32)


def _sigmoid(x):
    return 1.0 / (1.0 + jnp.exp(-x))


def _tile(n, cap=1024):
    if n <= cap:
        return n
    best = LANES
    for t in range(LANES, cap + 1, LANES):
        if n % t == 0:
            best = t
    return best


def _rows(s, cap=512):
    return s if s <= cap else cap


def _mm_nn(a, b, out_dtype, name, b_rows_out=False):
    m, k = a.shape
    n = b.shape[0] if b_rows_out else b.shape[1]
    tm, tn = _rows(m), _tile(n)

    def body(a_ref, b_ref, o_ref):
        prod = _dot_nt(a_ref[...], b_ref[...]) if b_rows_out else _dot(a_ref[...], b_ref[...])
        o_ref[...] = prod.astype(o_ref.dtype)

    b_spec = (pl.BlockSpec((tn, k), lambda j, i: (j, 0)) if b_rows_out
              else pl.BlockSpec((k, tn), lambda j, i: (0, j)))
    return pl.pallas_call(
        body, name=name, grid=(n // tn, m // tm),
        in_specs=[pl.BlockSpec((tm, k), lambda j, i: (i, 0)), b_spec],
        out_specs=pl.BlockSpec((tm, tn), lambda j, i: (i, j)),
        out_shape=jax.ShapeDtypeStruct((m, n), out_dtype),
        compiler_params=_cp("parallel", "parallel"))(a, b)


def _resident(shape, index_map):
    return pl.BlockSpec(shape, index_map, pipeline_mode=pl.Buffered(1))


def _mm_tn(a, b, name, out_dtype=BF):
    na, s, m = a.shape
    nb, _, n = b.shape
    no = max(na, nb)
    tm, tn = _tile(m), _tile(n)

    def body(a_ref, b_ref, o_ref):
        o_ref[...] = _dot_tn(a_ref[...], b_ref[...]).astype(o_ref.dtype)

    def spec(nbatch, width, tile, index_map):
        fixed = nbatch == 1 and width == tile
        return _resident((None, s, tile), index_map) if fixed else pl.BlockSpec((None, s, tile), index_map)

    return pl.pallas_call(
        body, name=name, grid=(no, m // tm, n // tn),
        in_specs=[spec(na, m, tm, lambda j, r, c: (j if na > 1 else 0, 0, r)),
                  spec(nb, n, tn, lambda j, r, c: (j if nb > 1 else 0, 0, c))],
        out_specs=pl.BlockSpec((None, tm, tn), lambda j, r, c: (j, r, c)),
        out_shape=jax.ShapeDtypeStruct((no, m, n), out_dtype),
        compiler_params=_cp("parallel", "parallel", "parallel"))(a, b)


def _mm_nt(dh, w, name, res=None, out_dtype=F32, w_rows_out=True, after=()):
    nc, s, kc = dh.shape
    d = w.shape[1] if w_rows_out else w.shape[2]
    ts = _rows(s)
    has_res = res is not None
    mm = _dot_nt if w_rows_out else _dot

    def body(*refs):
        o_ref = refs[-1]
        dh_ref, w_ref = refs[:2]
        if has_res:
            r_ref = refs[2]
        out = mm(dh_ref[0], w_ref[0])
        for j in range(1, nc):
            out = out + mm(dh_ref[j], w_ref[j])
        if has_res:
            out = out + ALPHA * r_ref[...]
        o_ref[...] = out.astype(o_ref.dtype)

    in_specs = [pl.BlockSpec((nc, ts, kc), lambda i: (0, i, 0)), _resident(w.shape, lambda i: (0, 0, 0))]
    args = [dh, w]
    if has_res:
        in_specs.append(pl.BlockSpec((ts, d), lambda i: (i, 0)))
        args.append(res)
    in_specs += [pl.BlockSpec(memory_space=pl.ANY)] * len(after)
    args += list(after)
    return pl.pallas_call(
        body, name=name, grid=(s // ts,), in_specs=in_specs,
        out_specs=pl.BlockSpec((ts, d), lambda i: (i, 0)),
        out_shape=jax.ShapeDtypeStruct((s, d), out_dtype),
        compiler_params=_cp("parallel"))(*args)


def _mm_res_ln(a, w, x, gain, bias, fscale, name):
    nc, s, kc = a.shape
    d = w.shape[2]
    ts = _rows(s)

    def body(a_ref, w_ref, x_ref, g_ref, b_ref, y_ref, yb_ref, xh_ref, r_ref):
        f = _dot(a_ref[0], w_ref[0])
        for j in range(1, nc):
            f = f + _dot(a_ref[j], w_ref[j])
        z = ALPHA * x_ref[...] + fscale * f
        mu = jnp.mean(z, axis=-1, keepdims=True)
        zc = z - mu
        var = jnp.mean(zc * zc, axis=-1, keepdims=True)
        r = lax.rsqrt(var + LN_EPS)
        xh = zc * r
        y = xh * g_ref[...] + b_ref[...]
        y_ref[...] = y
        yb_ref[...] = y.astype(BF)
        xh_ref[...] = xh
        r_ref[...] = r

    row = pl.BlockSpec((ts, d), lambda i: (i, 0))
    vec = pl.BlockSpec((1, d), lambda i: (0, 0))
    return pl.pallas_call(
        body, name=name, grid=(s // ts,),
        in_specs=[pl.BlockSpec((nc, ts, kc), lambda i: (0, i, 0)), _resident((nc, kc, d), lambda i: (0, 0, 0)),
                  row, vec, vec],
        out_specs=[row, row, row, pl.BlockSpec((ts, 1), lambda i: (i, 0))],
        out_shape=[jax.ShapeDtypeStruct((s, d), F32), jax.ShapeDtypeStruct((s, d), BF),
                   jax.ShapeDtypeStruct((s, d), F32), jax.ShapeDtypeStruct((s, 1), F32)],
        compiler_params=_cp("parallel"))(a, w, x, gain, bias)


def _ln_bwd(dy, xh, rstd, gain, fscale, name, after=()):
    s, d = dy.shape
    ts = _rows(s)
    na = len(after)

    def body(*refs):
        dy_ref, xh_ref, r_ref, g_ref = refs[:4]
        dz_ref, dzb_ref, dg_ref, db_ref = refs[4 + na:]
        i = pl.program_id(0)
        dyv = dy_ref[...]
        xhv = xh_ref[...]
        dxh = dyv * g_ref[...]
        m1 = jnp.mean(dxh, axis=-1, keepdims=True)
        m2 = jnp.mean(dxh * xhv, axis=-1, keepdims=True)
        dz = r_ref[...] * (dxh - m1 - xhv * m2)
        dz_ref[...] = dz
        dzb_ref[...] = (fscale * dz).astype(BF)

        @pl.when(i == 0)
        def _():
            dg_ref[...] = jnp.zeros_like(dg_ref)
            db_ref[...] = jnp.zeros_like(db_ref)

        dg_ref[...] += jnp.sum(dyv * xhv, axis=0, keepdims=True)
        db_ref[...] += jnp.sum(dyv, axis=0, keepdims=True)

    row = pl.BlockSpec((ts, d), lambda i: (i, 0))
    vec = pl.BlockSpec((1, d), lambda i: (0, 0))
    return pl.pallas_call(
        body, name=name, grid=(s // ts,),
        in_specs=[row, row, pl.BlockSpec((ts, 1), lambda i: (i, 0)), vec] + [pl.BlockSpec(memory_space=pl.ANY)] * na,
        out_specs=[row, row, vec, vec],
        out_shape=[jax.ShapeDtypeStruct((s, d), F32), jax.ShapeDtypeStruct((s, d), BF),
                   jax.ShapeDtypeStruct((1, d), F32), jax.ShapeDtypeStruct((1, d), F32)],
        compiler_params=_cp("arbitrary"))(dy, xh, rstd, gain, *after)


def _ffn_up(xb, wgu, name):
    s, d = xb.shape
    c = wgu.shape[1]
    nch = wgu.shape[0] // 2
    ts = _rows(s, 1024)
    w4 = wgu.reshape(2, nch, c, d)

    def body(x_ref, w_ref, gu_ref, a_ref):
        x = x_ref[...]
        g = _dot_nt(x, w_ref[0])
        u = _dot_nt(x, w_ref[1])
        sg = _sigmoid(g)
        t = g * sg
        gu_ref[0] = (u * (sg * (1.0 + g - t))).astype(BF)
        gu_ref[1] = t.astype(BF)
        a_ref[...] = (t * u).astype(BF)

    return pl.pallas_call(
        body, name=name, grid=(nch, s // ts),
        in_specs=[pl.BlockSpec((ts, d), lambda j, i: (i, 0)),
                  pl.BlockSpec((2, None, c, d), lambda j, i: (0, j, 0, 0))],
        out_specs=[pl.BlockSpec((2, None, ts, c), lambda j, i: (0, j, i, 0)),
                   pl.BlockSpec((None, ts, c), lambda j, i: (j, i, 0))],
        out_shape=[jax.ShapeDtypeStruct((2, nch, s, c), BF), jax.ShapeDtypeStruct((nch, s, c), BF)],
        compiler_params=_cp("parallel", "parallel"))(xb, w4)


def _ffn_fwd_main(x, xb, wgu, wd4, gain, bias, name):
    s, d = x.shape
    nch, c = wd4.shape[0], wd4.shape[1]
    ts = _rows(s, 256)

    def body(x_ref, xb_ref, wgu_ref, wd_ref, g_ref, b_ref, y_ref, yb_ref, gu_ref, a_ref, xh_ref, r_ref):
        xbv = xb_ref[...]
        f = jnp.zeros((ts, d), F32)
        for j in range(nch):
            g = _dot_nt(xbv, wgu_ref[j])
            u = _dot_nt(xbv, wgu_ref[nch + j])
            sg = _sigmoid(g)
            t = g * sg
            gu_ref[0, j] = (u * (sg * (1.0 + g - t))).astype(BF)
            gu_ref[1, j] = t.astype(BF)
            act = (t * u).astype(BF)
            a_ref[j] = act
            f = f + _dot(act, wd_ref[j])
        z = ALPHA * x_ref[...] + 0.5 * f
        mu = jnp.mean(z, axis=-1, keepdims=True)
        zc = z - mu
        var = jnp.mean(zc * zc, axis=-1, keepdims=True)
        r = lax.rsqrt(var + LN_EPS)
        xh = zc * r
        y = xh * g_ref[...] + b_ref[...]
        y_ref[...] = y
        yb_ref[...] = y.astype(BF)
        xh_ref[...] = xh
        r_ref[...] = r

    row = pl.BlockSpec((ts, d), lambda i: (i, 0))
    vec = pl.BlockSpec((1, d), lambda i: (0, 0))
    return pl.pallas_call(
        body, name=name, grid=(s // ts,),
        in_specs=[row, row, _resident(wgu.shape, lambda i: (0, 0, 0)), _resident(wd4.shape, lambda i: (0, 0, 0)),
                  vec, vec],
        out_specs=[row, row, pl.BlockSpec((2, nch, ts, c), lambda i: (0, 0, i, 0)),
                   pl.BlockSpec((nch, ts, c), lambda i: (0, i, 0)), row, pl.BlockSpec((ts, 1), lambda i: (i, 0))],
        out_shape=[jax.ShapeDtypeStruct((s, d), F32), jax.ShapeDtypeStruct((s, d), BF),
                   jax.ShapeDtypeStruct((2, nch, s, c), BF), jax.ShapeDtypeStruct((nch, s, c), BF),
                   jax.ShapeDtypeStruct((s, d), F32), jax.ShapeDtypeStruct((s, 1), F32)],
        compiler_params=_cp("parallel"))(x, xb, wgu, wd4, gain, bias)


def _ffn_bwd_main(dy, xh, rstd, gain, wd4, wgu, gu, name, after=(), with_dx=True):
    s, d = dy.shape
    nch, c = wd4.shape[0], wd4.shape[1]
    ts = _rows(s, 256)
    na = len(after)

    def body(*refs):
        dy_ref, xh_ref, r_ref, g_ref, wd_ref, wgu_ref, gu_ref = refs[:7]
        dx_ref, dzb_ref, dh_ref, dg_ref, db_ref = refs[7 + na:]
        i = pl.program_id(0)
        dyv = dy_ref[...]
        xhv = xh_ref[...]
        dxh = dyv * g_ref[...]
        m1 = jnp.mean(dxh, axis=-1, keepdims=True)
        m2 = jnp.mean(dxh * xhv, axis=-1, keepdims=True)
        dz = r_ref[...] * (dxh - m1 - xhv * m2)
        dzb = (0.5 * dz).astype(BF)
        dzb_ref[...] = dzb

        @pl.when(i == 0)
        def _():
            dg_ref[...] = jnp.zeros_like(dg_ref)
            db_ref[...] = jnp.zeros_like(db_ref)

        dg_ref[...] += jnp.sum(dyv * xhv, axis=0, keepdims=True)
        db_ref[...] += jnp.sum(dyv, axis=0, keepdims=True)

        dx = ALPHA * dz if with_dx else dz
        for j in range(nch):
            da = _dot_nt(dzb, wd_ref[j])
            dgate = (da * gu_ref[0, j].astype(F32)).astype(BF)
            dup = (da * gu_ref[1, j].astype(F32)).astype(BF)
            dh_ref[0, j] = dgate
            dh_ref[1, j] = dup
            if with_dx:
                dx = dx + _dot(dgate, wgu_ref[j]) + _dot(dup, wgu_ref[nch + j])
        dx_ref[...] = dx

    row = pl.BlockSpec((ts, d), lambda i: (i, 0))
    vec = pl.BlockSpec((1, d), lambda i: (0, 0))
    act = pl.BlockSpec((2, nch, ts, c), lambda i: (0, 0, i, 0))
    return pl.pallas_call(
        body, name=name, grid=(s // ts,),
        in_specs=[row, row, pl.BlockSpec((ts, 1), lambda i: (i, 0)), vec,
                  _resident(wd4.shape, lambda i: (0, 0, 0)), _resident(wgu.shape, lambda i: (0, 0, 0)), act]
                 + [pl.BlockSpec(memory_space=pl.ANY)] * na,
        out_specs=[row, row, act, vec, vec],
        out_shape=[jax.ShapeDtypeStruct((s, d), F32), jax.ShapeDtypeStruct((s, d), BF),
                   jax.ShapeDtypeStruct((2, nch, s, c), BF),
                   jax.ShapeDtypeStruct((1, d), F32), jax.ShapeDtypeStruct((1, d), F32)],
        compiler_params=_cp("arbitrary"))(dy, xh, rstd, gain, wd4, wgu, gu, *after)


def _rope_tables(s, sign):
    pos = jnp.arange(s, dtype=F32)
    inv_freq = 1.0 / (ROPE_THETA ** (jnp.arange(ROT_HALF, dtype=F32) / ROT_HALF))
    ang = pos[:, None] * inv_freq[None, :]
    cos, sin = jnp.cos(ang), jnp.sin(ang) * sign
    one = jnp.ones((s, HEAD_DIM - 2 * ROT_HALF), F32)
    zero = jnp.zeros((s, HEAD_DIM - 2 * ROT_HALF), F32)
    zh = jnp.zeros((s, ROT_HALF), F32)
    cos_f = jnp.concatenate([cos, cos, one], axis=1)
    sin_a = jnp.concatenate([-sin, zh, zero], axis=1)
    sin_b = jnp.concatenate([zh, sin, zero], axis=1)
    rep = LANES // HEAD_DIM
    return tuple(jnp.tile(t, (1, rep)) for t in (cos_f, sin_a, sin_b))


def _rope_cast(parts, tabs, n_rope, name, transposed=()):
    s = tabs[0].shape[0]
    flip = [i in transposed for i in range(len(parts))]
    widths = [p.shape[0] if f else p.shape[1] for p, f in zip(parts, flip)]
    n = sum(widths)
    npart = len(parts)
    ts = _rows(s, 256)

    def body(*refs):
        part_refs = refs[:npart]
        c_ref, sa_ref, sb_ref, o_ref = refs[npart:]
        col = 0
        for ref, w, f in zip(part_refs, widths, flip):
            for j in range(w // LANES):
                if f:
                    t = jnp.transpose(ref[j * LANES:(j + 1) * LANES, :])
                else:
                    t = ref[:, j * LANES:(j + 1) * LANES]
                if col < n_rope:
                    t = (t * c_ref[...] + pltpu.roll(t, LANES - ROT_HALF, 1) * sa_ref[...]
                         + pltpu.roll(t, ROT_HALF, 1) * sb_ref[...])
                o_ref[:, col * LANES:(col + 1) * LANES] = t.astype(BF)
                col += 1

    tab = pl.BlockSpec((ts, LANES), lambda i: (i, 0))
    return pl.pallas_call(
        body, name=name, grid=(s // ts,),
        in_specs=[pl.BlockSpec((w, ts), lambda i: (0, i)) if f else pl.BlockSpec((ts, w), lambda i: (i, 0))
                  for w, f in zip(widths, flip)] + [tab, tab, tab],
        out_specs=pl.BlockSpec((ts, n), lambda i: (i, 0)),
        out_shape=jax.ShapeDtypeStruct((s, n), BF),
        compiler_params=_cp("parallel"))(*parts, *tabs)


def _head_masks():
    lane = lax.broadcasted_iota(jnp.int32, (1, LANES), 1)
    return [lane < HEAD_DIM, lane >= HEAD_DIM]


def _sel(mask, v):
    return jnp.where(mask, v, jnp.zeros_like(v))


def _pick(mask, wide, fill):
    return jnp.max(jnp.where(mask, wide, fill), axis=1, keepdims=True)


def _band_masks(has_other, prev):
    qi = lax.broadcasted_iota(jnp.int32, (BLOCK, BLOCK), 0)
    kj = lax.broadcasted_iota(jnp.int32, (BLOCK, BLOCK), 1)
    if prev:
        return kj >= qi + jnp.where(has_other, 0, BLOCK)
    return kj <= qi


def _band_fwd(q3, k3, v3, name):
    ng, s, w = q3.shape
    nb = s // BLOCK
    npair = w // LANES
    nsub = BAND_SUB
    tile = nsub * BLOCK

    def body(q_ref, kc_ref, kp_ref, vc_ref, vp_ref, o_ref, l_ref):
        g = pl.program_id(0)
        t = pl.program_id(2)
        nbl = jnp.right_shift(nb, 2 * g)
        mc = _band_masks(None, False)
        hm = _head_masks()
        for i in range(nsub):
            rows = slice(i * BLOCK, (i + 1) * BLOCK)
            has_prev = jnp.bitwise_and(t * nsub + i, nbl - 1) != 0
            mp = _band_masks(has_prev, True)
            q, kc, vc = q_ref[rows, :], kc_ref[rows, :], vc_ref[rows, :]
            if i == 0:
                kp, vp = kp_ref[...], vp_ref[...]
            else:
                prev = slice((i - 1) * BLOCK, i * BLOCK)
                kp, vp = kc_ref[prev, :], vc_ref[prev, :]
            o = jnp.zeros((BLOCK, LANES), F32)
            lse_w = jnp.zeros((BLOCK, LANES), F32)
            for h in range(2):
                qh = _sel(hm[h], q)
                sc = jnp.where(mc, _dot_nt(qh, kc) * SCALE, NEG)
                sp = jnp.where(mp, _dot_nt(qh, kp) * SCALE, NEG)
                m = jnp.maximum(jnp.max(sc, axis=1, keepdims=True), jnp.max(sp, axis=1, keepdims=True))
                pc = jnp.exp(sc - m)
                pp = jnp.exp(sp - m)
                l = jnp.sum(pc, axis=1, keepdims=True) + jnp.sum(pp, axis=1, keepdims=True)
                oh = _dot(pc.astype(BF), _sel(hm[h], vc)) + _dot(pp.astype(BF), _sel(hm[h], vp))
                o = o + oh / l
                lse_w = jnp.where(hm[h], m + jnp.log(l), lse_w)
            o_ref[rows, :] = o
            l_ref[rows, :] = lse_w

    cur = pl.BlockSpec((None, tile, LANES), lambda g, p, t: (g, t, p))
    prv = pl.BlockSpec((None, BLOCK, LANES), lambda g, p, t: (g, jnp.maximum(t * nsub - 1, 0), p))
    return pl.pallas_call(
        body, name=name, grid=(ng, npair, nb // nsub),
        in_specs=[cur, cur, prv, cur, prv], out_specs=[cur, cur],
        out_shape=[jax.ShapeDtypeStruct((ng, s, w), F32), jax.ShapeDtypeStruct((ng, s, w), F32)],
        compiler_params=_cp("parallel", "parallel", "parallel"))(q3, k3, k3, v3, v3)


def _band_combine(o3, l3, name):
    ng, s, w = o3.shape
    ts = _rows(s)

    def body(o_ref, l_ref, oa_ref, lt_ref):
        ls = [l_ref[g] for g in range(ng)]
        m = functools.reduce(jnp.maximum, ls)
        es = [jnp.exp(l - m) for l in ls]
        den = functools.reduce(lambda a, b: a + b, es)
        num = functools.reduce(lambda a, b: a + b, [es[g] * o_ref[g] for g in range(ng)])
        oa_ref[...] = (num / den).astype(BF)
        lt_ref[...] = m + jnp.log(den)

    blk3 = pl.BlockSpec((ng, ts, w), lambda i: (0, i, 0))
    blk = pl.BlockSpec((ts, w), lambda i: (i, 0))
    return pl.pallas_call(
        body, name=name, grid=(s // ts,), in_specs=[blk3, blk3], out_specs=[blk, blk],
        out_shape=[jax.ShapeDtypeStruct((s, w), BF), jax.ShapeDtypeStruct((s, w), F32)],
        compiler_params=_cp("parallel"))(o3, l3)


def _band_bwd(q3, k3, v3, do3, oa3, lt3, name):
    ng, s, w = q3.shape
    nb = s // BLOCK
    npair = w // LANES
    nsub = BAND_SUB
    tile = nsub * BLOCK

    def body(q_ref, qn_ref, kc_ref, kp_ref, vc_ref, vp_ref, do_ref, don_ref, oa_ref, oan_ref, lt_ref, ltn_ref,
             dq_ref, dk_ref, dv_ref):
        g = pl.program_id(0)
        t = pl.program_id(2)
        nbl = jnp.right_shift(nb, 2 * g)
        mc = _band_masks(None, False)
        hm = _head_masks()

        def block(ref, edge_ref, i):
            if i < 0 or i >= nsub:
                return edge_ref[...]
            return ref[i * BLOCK:(i + 1) * BLOCK, :]

        for i in range(nsub):
            b = t * nsub + i
            mp = _band_masks(jnp.bitwise_and(b, nbl - 1) != 0, True)
            mn = _band_masks(jnp.bitwise_and(b + 1, nbl - 1) != 0, True)
            q, qn = block(q_ref, None, i), block(q_ref, qn_ref, i + 1)
            kc, kp = block(kc_ref, None, i), block(kc_ref, kp_ref, i - 1)
            vc, vp = block(vc_ref, None, i), block(vc_ref, vp_ref, i - 1)
            do, don = block(do_ref, None, i), block(do_ref, don_ref, i + 1)
            dd = do.astype(F32) * block(oa_ref, None, i).astype(F32)
            ddn = don.astype(F32) * block(oa_ref, oan_ref, i + 1).astype(F32)
            lt, ltn = block(lt_ref, None, i), block(lt_ref, ltn_ref, i + 1)
            dq = jnp.zeros((BLOCK, LANES), F32)
            dk = jnp.zeros((BLOCK, LANES), F32)
            dv = jnp.zeros((BLOCK, LANES), F32)
            for h in range(2):
                qh, doh = _sel(hm[h], q), _sel(hm[h], do)
                qnh, donh = _sel(hm[h], qn), _sel(hm[h], don)
                kch, kph = _sel(hm[h], kc), _sel(hm[h], kp)
                lse = _pick(hm[h], lt, NEG)
                lsen = _pick(hm[h], ltn, NEG)
                dsum = jnp.sum(_sel(hm[h], dd), axis=1, keepdims=True)
                dsumn = jnp.sum(_sel(hm[h], ddn), axis=1, keepdims=True)
                pc = jnp.exp(jnp.where(mc, _dot_nt(qh, kc) * SCALE, NEG) - lse)
                pp = jnp.exp(jnp.where(mp, _dot_nt(qh, kp) * SCALE, NEG) - lse)
                dsc = pc * (_dot_nt(doh, vc) - dsum)
                dsp = pp * (_dot_nt(doh, vp) - dsum)
                dq = dq + SCALE * (_dot(dsc.astype(BF), kch) + _dot(dsp.astype(BF), kph))
                pn = jnp.exp(jnp.where(mn, _dot_nt(qnh, kc) * SCALE, NEG) - lsen)
                dsn = pn * (_dot_nt(donh, vc) - dsumn)
                dk = dk + SCALE * (_dot_tn(dsc.astype(BF), qh) + _dot_tn(dsn.astype(BF), qnh))
                dv = dv + _dot_tn(pc.astype(BF), doh) + _dot_tn(pn.astype(BF), donh)
            rows = slice(i * BLOCK, (i + 1) * BLOCK)
            dq_ref[rows, :] = dq
            dk_ref[rows, :] = dk
            dv_ref[rows, :] = dv

    cur = pl.BlockSpec((None, tile, LANES), lambda g, p, t: (g, t, p))
    prv = pl.BlockSpec((None, BLOCK, LANES), lambda g, p, t: (g, jnp.maximum(t * nsub - 1, 0), p))
    nxt = pl.BlockSpec((None, BLOCK, LANES), lambda g, p, t: (g, jnp.minimum(t * nsub + nsub, nb - 1), p))
    out = jax.ShapeDtypeStruct((ng, s, w), F32)
    return pl.pallas_call(
        body, name=name, grid=(ng, npair, nb // nsub),
        in_specs=[cur, nxt, cur, prv, cur, prv, cur, nxt, cur, nxt, cur, nxt],
        out_specs=[cur, cur, cur], out_shape=[out, out, out],
        compiler_params=_cp("parallel", "parallel", "parallel"))(
            q3, q3, k3, k3, v3, v3, do3, do3, oa3, oa3, lt3, lt3)


def _mem_fwd(hb, q_blk0, kv, name):
    s = hb.shape[0]
    m = kv.shape[0]
    tq = _rows(s)
    npair = MEM_W // LANES

    def body(q_ref, k_ref, v_ref, o_ref, l_ref):
        q, k, v = q_ref[...], k_ref[...], v_ref[...]
        hm = _head_masks()
        o = jnp.zeros((tq, LANES), F32)
        lse_w = jnp.zeros((tq, LANES), F32)
        for h in range(2):
            sc = _dot_nt(_sel(hm[h], q), k) * SCALE
            mx = jnp.max(sc, axis=1, keepdims=True)
            p = jnp.exp(sc - mx)
            l = jnp.sum(p, axis=1, keepdims=True)
            o = o + _dot(p.astype(BF), _sel(hm[h], v)) / l
            lse_w = jnp.where(hm[h], mx + jnp.log(l), lse_w)
        o_ref[...] = o.astype(BF)
        l_ref[...] = lse_w

    blk = pl.BlockSpec((tq, LANES), lambda p, i: (i, p))
    return pl.pallas_call(
        body, name=name, grid=(npair, s // tq),
        in_specs=[pl.BlockSpec((tq, LANES), lambda p, i: (i, q_blk0 + p)),
                  pl.BlockSpec((m, LANES), lambda p, i: (0, p)),
                  pl.BlockSpec((m, LANES), lambda p, i: (0, npair + p))],
        out_specs=[blk, blk],
        out_shape=[jax.ShapeDtypeStruct((s, MEM_W), BF), jax.ShapeDtypeStruct((s, MEM_W), F32)],
        compiler_params=_cp("parallel", "parallel"))(hb, kv, kv)


def _mem_bwd(hb, q_blk0, kv, dcat, cat, o_blk0, lse, name):
    s = hb.shape[0]
    m = kv.shape[0]
    tq = _rows(s)
    npair = MEM_W // LANES

    def body(q_ref, k_ref, v_ref, do_ref, o_ref, l_ref, dq_ref, dk_ref, dv_ref):
        i = pl.program_id(1)

        @pl.when(i == 0)
        def _():
            dk_ref[...] = jnp.zeros_like(dk_ref)
            dv_ref[...] = jnp.zeros_like(dv_ref)

        q, k, v, do = q_ref[...], k_ref[...], v_ref[...], do_ref[...]
        dd = do.astype(F32) * o_ref[...].astype(F32)
        lt = l_ref[...]
        hm = _head_masks()
        dq = jnp.zeros((tq, LANES), F32)
        dk = jnp.zeros((m, LANES), F32)
        dv = jnp.zeros((m, LANES), F32)
        for h in range(2):
            qh, doh = _sel(hm[h], q), _sel(hm[h], do)
            p = jnp.exp(_dot_nt(qh, k) * SCALE - _pick(hm[h], lt, NEG))
            ds = p * (_dot_nt(doh, v) - jnp.sum(_sel(hm[h], dd), axis=1, keepdims=True))
            dq = dq + SCALE * _dot(ds.astype(BF), _sel(hm[h], k))
            dk = dk + SCALE * _dot_tn(ds.astype(BF), qh)
            dv = dv + _dot_tn(p.astype(BF), doh)
        dq_ref[...] = dq
        dk_ref[...] += dk
        dv_ref[...] += dv

    row = pl.BlockSpec((tq, LANES), lambda p, i: (i, p))
    orow = pl.BlockSpec((tq, LANES), lambda p, i: (i, o_blk0 + p))
    acc = pl.BlockSpec((m, LANES), lambda p, i: (0, p))
    return pl.pallas_call(
        body, name=name, grid=(npair, s // tq),
        in_specs=[pl.BlockSpec((tq, LANES), lambda p, i: (i, q_blk0 + p)),
                  pl.BlockSpec((m, LANES), lambda p, i: (0, p)),
                  pl.BlockSpec((m, LANES), lambda p, i: (0, npair + p)), orow, orow, row],
        out_specs=[row, acc, acc],
        out_shape=[jax.ShapeDtypeStruct((s, MEM_W), F32), jax.ShapeDtypeStruct((m, MEM_W), F32),
                   jax.ShapeDtypeStruct((m, MEM_W), F32)],
        compiler_params=_cp("parallel", "arbitrary"))(hb, kv, kv, dcat, cat, lse)


def _gate_fwd(f_t, bias, name):
    hp, s = f_t.shape
    nblk = s // LANES

    def body(f_ref, b_ref, c_ref):
        lane = lax.broadcasted_iota(jnp.int32, (hp, LANES), 1)

        def step(i, carry):
            off = pl.multiple_of(i * LANES, LANES)
            x = f_ref[:, pl.ds(off, LANES)] + b_ref[...]
            acc = jnp.minimum(x, 0.0) - jnp.log(1.0 + jnp.exp(-jnp.abs(x)))
            sh = 1
            while sh < LANES:
                acc = acc + jnp.where(lane >= sh, pltpu.roll(acc, sh, 1), 0.0)
                sh *= 2
            acc = acc + carry
            c_ref[:, pl.ds(off, LANES)] = acc
            return acc[:, LANES - 1:LANES]

        lax.fori_loop(0, nblk, step, jnp.zeros((hp, 1), F32))

    vm = pl.BlockSpec(memory_space=pltpu.VMEM)
    return pl.pallas_call(body, name=name, in_specs=[vm, vm], out_specs=vm,
                          out_shape=jax.ShapeDtypeStruct((hp, s), F32),
                          compiler_params=pltpu.CompilerParams(vmem_limit_bytes=VMEM_LIMIT))(f_t, bias)


def _gate_bwd(dc_t, f_t, bias, name):
    hp, s = f_t.shape
    nblk = s // LANES

    def body(dc_ref, f_ref, b_ref, df_ref, db_ref):
        lane = lax.broadcasted_iota(jnp.int32, (hp, LANES), 1)

        def step(t, carry):
            suffix, dbias = carry
            off = pl.multiple_of((nblk - 1 - t) * LANES, LANES)
            acc = dc_ref[:, pl.ds(off, LANES)]
            sh = 1
            while sh < LANES:
                acc = acc + jnp.where(lane < LANES - sh, pltpu.roll(acc, LANES - sh, 1), 0.0)
                sh *= 2
            acc = acc + suffix
            x = f_ref[:, pl.ds(off, LANES)] + b_ref[...]
            df = acc * _sigmoid(-x)
            df_ref[:, pl.ds(off, LANES)] = df
            return acc[:, 0:1], dbias + jnp.sum(df, axis=1, keepdims=True)

        _, dbias = lax.fori_loop(0, nblk, step, (jnp.zeros((hp, 1), F32), jnp.zeros((hp, 1), F32)))
        db_ref[...] = dbias

    vm = pl.BlockSpec(memory_space=pltpu.VMEM)
    return pl.pallas_call(body, name=name, in_specs=[vm, vm, vm], out_specs=[vm, vm],
                          out_shape=[jax.ShapeDtypeStruct((hp, s), F32), jax.ShapeDtypeStruct((hp, 1), F32)],
                          compiler_params=pltpu.CompilerParams(vmem_limit_bytes=VMEM_LIMIT))(dc_t, f_t, bias)


def _wide(rep, width):
    return jnp.tile(rep, (1, width // LANES))


def _fold(t):
    part = t[:, :LANES]
    for c in range(1, t.shape[1] // LANES):
        part = part + t[:, c * LANES:(c + 1) * LANES]
    return part


def _fox_logits(q, k, cq_rep, ck_row, mask, hmask):
    s = _dot_nt(_sel(hmask, q), k) + (_wide(cq_rep, ck_row.shape[1]) - ck_row)
    if mask is not None:
        s = jnp.where(mask, s, NEG)
    return s


def _diag_mask(t):
    return lax.broadcasted_iota(jnp.int32, (t, t), 1) <= lax.broadcasted_iota(jnp.int32, (t, t), 0)


def _fox_fwd(hb, c_rep, c_t3, name):
    s = hb.shape[0]
    npair = MIX_W // LANES
    tq = tk = _rows(s)
    nq = s // tq

    def body(q_ref, k_ref, v_ref, cq_ref, ck_ref, o_ref, l_ref, m_s, l_s, acc):
        qi = pl.program_id(1)
        kj = pl.program_id(2)
        hm = _head_masks()

        @pl.when(kj == 0)
        def _():
            m_s[...] = jnp.full_like(m_s, NEG)
            l_s[...] = jnp.zeros_like(l_s)
            acc[...] = jnp.zeros_like(acc)

        def step(mask):
            q, k, v = q_ref[...] * SCALE, k_ref[...], v_ref[...]
            ck = ck_ref[...]
            for h in range(2):
                sc = _fox_logits(q, k, cq_ref[h], ck[h:h + 1, :], mask, hm[h])
                m_old = m_s[h]
                m_new = jnp.maximum(m_old, jnp.max(sc, axis=1, keepdims=True))
                pr = jnp.exp(sc - _wide(m_new, tk))
                corr = jnp.exp(m_old - m_new)
                l_s[h] = l_s[h] * corr + _fold(pr)
                acc[h] = acc[h] * corr + _dot(pr.astype(BF), _sel(hm[h], v))
                m_s[h] = m_new

        @pl.when(kj < qi)
        def _():
            step(None)

        @pl.when(kj == qi)
        def _():
            step(_diag_mask(tq))
            outs = []
            for h in range(2):
                den = jnp.sum(l_s[h], axis=1, keepdims=True)
                outs.append(acc[h] / den)
                l_ref[h] = m_s[h] + jnp.log(den)
            o_ref[...] = jnp.where(hm[0], outs[0], outs[1]).astype(BF)

    def kv_map(off):
        return lambda p, i, j: (jnp.minimum(j, i), off + p)

    blk = pl.BlockSpec((tq, LANES), lambda p, i, j: (i, p))
    return pl.pallas_call(
        body, name=name, grid=(npair, nq, nq),
        in_specs=[blk, pl.BlockSpec((tk, LANES), kv_map(npair)), pl.BlockSpec((tk, LANES), kv_map(2 * npair)),
                  pl.BlockSpec((2, tq, LANES), lambda p, i, j: (p, i, 0)),
                  pl.BlockSpec((None, 2, tk), lambda p, i, j: (p, 0, jnp.minimum(j, i)))],
        out_specs=[blk, pl.BlockSpec((2, tq, LANES), lambda p, i, j: (p, i, 0))],
        out_shape=[jax.ShapeDtypeStruct((s, MIX_W), BF), jax.ShapeDtypeStruct((2 * npair, s, LANES), F32)],
        scratch_shapes=[pltpu.VMEM((2, tq, LANES), F32), pltpu.VMEM((2, tq, LANES), F32),
                        pltpu.VMEM((2, tq, LANES), F32)],
        compiler_params=_cp("parallel", "parallel", "arbitrary"))(hb, hb, hb, c_rep, c_t3)


def _fox_dsum(hb, dcat, lse, c_rep, c_t3, name):
    s = hb.shape[0]
    npair = MIX_W // LANES
    tq = tk = _rows(s)
    nq = s // tq

    def body(q_ref, k_ref, v_ref, do_ref, l_ref, cq_ref, ck_ref, d_ref, acc):
        qi = pl.program_id(1)
        kj = pl.program_id(2)
        hm = _head_masks()

        @pl.when(kj == 0)
        def _():
            acc[...] = jnp.zeros_like(acc)

        def step(mask):
            q, k, v, do = q_ref[...] * SCALE, k_ref[...], v_ref[...], do_ref[...]
            ck = ck_ref[...]
            for h in range(2):
                pr = jnp.exp(_fox_logits(q, k, cq_ref[h], ck[h:h + 1, :], mask, hm[h]) - _wide(l_ref[h], tk))
                acc[h] += _fold(pr * _dot_nt(_sel(hm[h], do), v))

        @pl.when(kj < qi)
        def _():
            step(None)

        @pl.when(kj == qi)
        def _():
            step(_diag_mask(tq))
            for h in range(2):
                d_ref[h] = jnp.broadcast_to(jnp.sum(acc[h], axis=1, keepdims=True), (tq, LANES))

    def kv_map(off):
        return lambda p, i, j: (jnp.minimum(j, i), off + p)

    blk = pl.BlockSpec((tq, LANES), lambda p, i, j: (i, p))
    rep = pl.BlockSpec((2, tq, LANES), lambda p, i, j: (p, i, 0))
    return pl.pallas_call(
        body, name=name, grid=(npair, nq, nq),
        in_specs=[blk, pl.BlockSpec((tk, LANES), kv_map(npair)), pl.BlockSpec((tk, LANES), kv_map(2 * npair)),
                  blk, rep, rep, pl.BlockSpec((None, 2, tk), lambda p, i, j: (p, 0, jnp.minimum(j, i)))],
        out_specs=rep, out_shape=jax.ShapeDtypeStruct((2 * npair, s, LANES), F32),
        scratch_shapes=[pltpu.VMEM((2, tq, LANES), F32)],
        compiler_params=_cp("parallel", "parallel", "arbitrary"))(hb, hb, hb, dcat, lse, c_rep, c_t3)


def _fox_bwd(hb, dcat, dsum, lse, c_rep, c_t3, name):
    s = hb.shape[0]
    npair = MIX_W // LANES
    tq = tk = _rows(s)
    nq = s // tq

    def body(q_ref, k_ref, v_ref, do_ref, d_ref, l_ref, cq_ref, ck_ref, dq_ref, dk_ref, dv_ref, dc_ref):
        kj = pl.program_id(1)
        qi = pl.program_id(2)
        hm = _head_masks()

        @pl.when(qi == 0)
        def _():
            dk_ref[...] = jnp.zeros_like(dk_ref)
            dv_ref[...] = jnp.zeros_like(dv_ref)
            dc_ref[...] = jnp.zeros_like(dc_ref)

        @pl.when((qi == 0) & (kj == 0))
        def _():
            dq_ref[...] = jnp.zeros_like(dq_ref)

        def step(mask):
            q, k, v, do = q_ref[...] * SCALE, k_ref[...], v_ref[...], do_ref[...]
            ck = ck_ref[...]
            dq = jnp.zeros((tq, LANES), F32)
            dk = jnp.zeros((tk, LANES), F32)
            dv = jnp.zeros((tk, LANES), F32)
            dcs = []
            for h in range(2):
                qh, doh = _sel(hm[h], q), _sel(hm[h], do)
                pr = jnp.exp(_fox_logits(q, k, cq_ref[h], ck[h:h + 1, :], mask, hm[h]) - _wide(l_ref[h], tk))
                ds = pr * (_dot_nt(doh, v) - _wide(d_ref[h], tk))
                dsb = ds.astype(BF)
                dq = dq + _dot(dsb, _sel(hm[h], k))
                dk = dk + _dot_tn(dsb, qh)
                dv = dv + _dot_tn(pr.astype(BF), doh)
                dcs.append(jnp.sum(ds, axis=0, keepdims=True))
            rows = pl.ds(pl.multiple_of(qi * tq, tq), tq)
            dq_ref[rows, :] += SCALE * dq
            dk_ref[...] += dk
            dv_ref[...] += dv
            dc_ref[...] -= jnp.concatenate(dcs, axis=0)

        @pl.when(qi > kj)
        def _():
            step(None)

        @pl.when(qi == kj)
        def _():
            step(_diag_mask(tq))

    def q_map(p, j, i):
        return (jnp.maximum(i, j), p)

    kblk = pl.BlockSpec((tk, LANES), lambda p, j, i: (j, p))
    rep = pl.BlockSpec((2, tq, LANES), lambda p, j, i: (p, jnp.maximum(i, j), 0))
    return pl.pallas_call(
        body, name=name, grid=(npair, nq, nq),
        in_specs=[pl.BlockSpec((tq, LANES), q_map),
                  pl.BlockSpec((tk, LANES), lambda p, j, i: (j, npair + p)),
                  pl.BlockSpec((tk, LANES), lambda p, j, i: (j, 2 * npair + p)),
                  pl.BlockSpec((tq, LANES), q_map), rep, rep, rep,
                  pl.BlockSpec((None, 2, tk), lambda p, j, i: (p, 0, j))],
        out_specs=[pl.BlockSpec((s, LANES), lambda p, j, i: (0, p)), kblk, kblk,
                   pl.BlockSpec((None, 2, tk), lambda p, j, i: (p, 0, j))],
        out_shape=[jax.ShapeDtypeStruct((s, MIX_W), F32), jax.ShapeDtypeStruct((s, MIX_W), F32),
                   jax.ShapeDtypeStruct((s, MIX_W), F32), jax.ShapeDtypeStruct((npair, 2, s), F32)],
        compiler_params=_cp("arbitrary", "arbitrary", "arbitrary"))(hb, hb, hb, dcat, dsum, lse, c_rep, c_t3)


def _foxt_logits(q, k, cq_row, ck_rep, mask, hmask):
    s = _dot_nt(_sel(hmask, k), q) + (cq_row - _wide(ck_rep, q.shape[0]))
    if mask is not None:
        s = jnp.where(mask, s, NEG)
    return s


def _causal_t(qi, kj, tq, tk):
    return (kj * tk + lax.broadcasted_iota(jnp.int32, (tk, tq), 0)
            <= qi * tq + lax.broadcasted_iota(jnp.int32, (tk, tq), 1))


def _fox_tiles(s):
    tq = _rows(s, 1024)
    return tq, tq // 2, s // tq


def _count_ge(t, bounds):
    return sum([(t >= b).astype(jnp.int32) for b in bounds], jnp.int32(0))


def _sweep_q_major(t, nq):
    qi = _count_ge(t, [r * (r + 1) for r in range(1, nq)])
    return qi, t - qi * (qi + 1)


def _sweep_k_major(t, nq):
    counts = [nq - j // 2 for j in range(2 * nq)]
    offs = [sum(counts[:j]) for j in range(1, 2 * nq)]
    kj = _count_ge(t, offs)
    start = sum([jnp.where(t >= o, c, 0) for o, c in zip(offs, counts)], jnp.int32(0))
    qi = kj // 2 + (t - start)
    return kj, qi, t == start, qi == nq - 1


def _foxt_fwd(hb, c_rep, c_t3, name):
    s = hb.shape[0]
    npair = MIX_W // LANES
    tq, tk, nq = _fox_tiles(s)

    def body(q_ref, k_ref, v_ref, cq_ref, ck_ref, o_ref, l_ref, m_s, l_s, acc):
        qi, kj = _sweep_q_major(pl.program_id(1), nq)
        hm = _head_masks()

        @pl.when(kj == 0)
        def _():
            m_s[...] = jnp.full_like(m_s, NEG)
            l_s[...] = jnp.zeros_like(l_s)
            acc[...] = jnp.zeros_like(acc)

        def step(mask):
            q, k = q_ref[...] * SCALE, k_ref[...]
            vt = jnp.transpose(v_ref[...])
            cq = cq_ref[...]
            for h in range(2):
                st = _foxt_logits(q, k, cq[h:h + 1, :], ck_ref[h], mask, hm[h])
                m_old = m_s[h]
                m_new = jnp.maximum(m_old, jnp.max(st, axis=0, keepdims=True))
                pt = jnp.exp(st - m_new)
                corr = jnp.exp(m_old - m_new)
                l_s[h] = l_s[h] * corr + jnp.sum(pt, axis=0, keepdims=True)
                acc[h] = acc[h] * corr + _dot(vt[h * HEAD_DIM:(h + 1) * HEAD_DIM, :], pt.astype(BF))
                m_s[h] = m_new

        @pl.when(kj < 2 * qi)
        def _():
            step(None)

        @pl.when(kj >= 2 * qi)
        def _():
            step(_causal_t(qi, kj, tq, tk))

        @pl.when(kj == 2 * qi + 1)
        def _():
            outs = []
            for h in range(2):
                outs.append(acc[h] / l_s[h])
                l_ref[h:h + 1, :] = m_s[h] + jnp.log(l_s[h])
            o_ref[...] = jnp.transpose(jnp.concatenate(outs, axis=0)).astype(BF)

    def q_map(p, t):
        return (_sweep_q_major(t, nq)[0], p)

    def kv_map(off):
        return lambda p, t: (_sweep_q_major(t, nq)[1], off + p)

    blk = pl.BlockSpec((tq, LANES), q_map)
    row = pl.BlockSpec((None, 2, tq), lambda p, t: (p, 0, _sweep_q_major(t, nq)[0]))
    return pl.pallas_call(
        body, name=name, grid=(npair, nq * (nq + 1)),
        in_specs=[blk, pl.BlockSpec((tk, LANES), kv_map(npair)), pl.BlockSpec((tk, LANES), kv_map(2 * npair)), row,
                  pl.BlockSpec((2, tk, LANES), lambda p, t: (p, _sweep_q_major(t, nq)[1], 0))],
        out_specs=[blk, row],
        out_shape=[jax.ShapeDtypeStruct((s, MIX_W), BF), jax.ShapeDtypeStruct((npair, 2, s), F32)],
        scratch_shapes=[pltpu.VMEM((2, 1, tq), F32), pltpu.VMEM((2, 1, tq), F32),
                        pltpu.VMEM((2, HEAD_DIM, tq), F32)],
        compiler_params=_cp("parallel", "arbitrary"))(hb, hb, hb, c_t3, c_rep)


def _foxt_dsum(hb, dcat, lse, c_rep, c_t3, name):
    s = hb.shape[0]
    npair = MIX_W // LANES
    tq, tk, nq = _fox_tiles(s)

    def body(q_ref, k_ref, v_ref, do_ref, l_ref, cq_ref, ck_ref, d_ref, acc):
        qi, kj = _sweep_q_major(pl.program_id(1), nq)
        hm = _head_masks()

        @pl.when(kj == 0)
        def _():
            acc[...] = jnp.zeros_like(acc)

        def step(mask):
            q, k, v, do = q_ref[...] * SCALE, k_ref[...], v_ref[...], do_ref[...]
            cq, lse_rows = cq_ref[...], l_ref[...]
            for h in range(2):
                pt = jnp.exp(_foxt_logits(q, k, cq[h:h + 1, :], ck_ref[h], mask, hm[h]) - lse_rows[h:h + 1, :])
                acc[h] += jnp.sum(pt * _dot_nt(_sel(hm[h], v), do), axis=0, keepdims=True)

        @pl.when(kj < 2 * qi)
        def _():
            step(None)

        @pl.when(kj >= 2 * qi)
        def _():
            step(_causal_t(qi, kj, tq, tk))

        @pl.when(kj == 2 * qi + 1)
        def _():
            for h in range(2):
                d_ref[h:h + 1, :] = acc[h]

    def q_map(p, t):
        return (_sweep_q_major(t, nq)[0], p)

    def kv_map(off):
        return lambda p, t: (_sweep_q_major(t, nq)[1], off + p)

    blk = pl.BlockSpec((tq, LANES), q_map)
    row = pl.BlockSpec((None, 2, tq), lambda p, t: (p, 0, _sweep_q_major(t, nq)[0]))
    return pl.pallas_call(
        body, name=name, grid=(npair, nq * (nq + 1)),
        in_specs=[blk, pl.BlockSpec((tk, LANES), kv_map(npair)), pl.BlockSpec((tk, LANES), kv_map(2 * npair)),
                  blk, row, row, pl.BlockSpec((2, tk, LANES), lambda p, t: (p, _sweep_q_major(t, nq)[1], 0))],
        out_specs=row, out_shape=jax.ShapeDtypeStruct((npair, 2, s), F32),
        scratch_shapes=[pltpu.VMEM((2, 1, tq), F32)],
        compiler_params=_cp("parallel", "arbitrary"))(hb, hb, hb, dcat, lse, c_t3, c_rep)


def _foxt_bwd(hb, dcat, dsum, lse, c_rep, c_t3, name):
    s = hb.shape[0]
    npair = MIX_W // LANES
    tq, tk, nq = _fox_tiles(s)

    def body(q_ref, k_ref, v_ref, do_ref, d_ref, l_ref, cq_ref, ck_ref, dq_ref, dk_ref, dv_ref, dc_ref, dc_s):
        t = pl.program_id(1)
        kj, qi, first, last = _sweep_k_major(t, nq)
        hm = _head_masks()

        @pl.when(first)
        def _():
            dk_ref[...] = jnp.zeros_like(dk_ref)
            dv_ref[...] = jnp.zeros_like(dv_ref)
            dc_s[...] = jnp.zeros_like(dc_s)

        @pl.when(t == 0)
        def _():
            dq_ref[...] = jnp.zeros_like(dq_ref)

        def step(mask):
            q, k, v, do = q_ref[...] * SCALE, k_ref[...], v_ref[...], do_ref[...]
            qt, kt, dot = jnp.transpose(q), jnp.transpose(k), jnp.transpose(do)
            cq, lse_rows, d_rows = cq_ref[...], l_ref[...], d_ref[...]
            dqs, dks, dvs = [], [], []
            for h in range(2):
                rows = slice(h * HEAD_DIM, (h + 1) * HEAD_DIM)
                pt = jnp.exp(_foxt_logits(q, k, cq[h:h + 1, :], ck_ref[h], mask, hm[h]) - lse_rows[h:h + 1, :])
                dst = pt * (_dot_nt(_sel(hm[h], v), do) - d_rows[h:h + 1, :])
                dsb = dst.astype(BF)
                dqs.append(_dot(kt[rows, :], dsb))
                dks.append(_dot_nt(qt[rows, :], dsb))
                dvs.append(_dot_nt(dot[rows, :], pt.astype(BF)))
                dc_s[h] += _fold(dst)
            cols = pl.ds(pl.multiple_of(qi * tq, tq), tq)
            dq_ref[:, cols] += SCALE * jnp.concatenate(dqs, axis=0)
            dk_ref[...] += jnp.concatenate(dks, axis=0)
            dv_ref[...] += jnp.concatenate(dvs, axis=0)

        @pl.when(kj < 2 * qi)
        def _():
            step(None)

        @pl.when(kj >= 2 * qi)
        def _():
            step(_causal_t(qi, kj, tq, tk))

        @pl.when(last)
        def _():
            for h in range(2):
                dc_ref[h:h + 1, :] = -jnp.sum(jnp.transpose(dc_s[h]), axis=0, keepdims=True)

    def kj_of(t):
        return _sweep_k_major(t, nq)[0]

    def qi_of(t):
        return _sweep_k_major(t, nq)[1]

    qblk = pl.BlockSpec((tq, LANES), lambda p, t: (qi_of(t), p))
    row = pl.BlockSpec((None, 2, tq), lambda p, t: (p, 0, qi_of(t)))
    kblk = pl.BlockSpec((LANES, tk), lambda p, t: (p, kj_of(t)))
    rep = pl.BlockSpec((2, tk, LANES), lambda p, t: (p, kj_of(t), 0))
    return pl.pallas_call(
        body, name=name, grid=(npair, nq * (nq + 1)),
        in_specs=[qblk,
                  pl.BlockSpec((tk, LANES), lambda p, t: (kj_of(t), npair + p)),
                  pl.BlockSpec((tk, LANES), lambda p, t: (kj_of(t), 2 * npair + p)),
                  qblk, row, row, row, rep],
        out_specs=[pl.BlockSpec((LANES, s), lambda p, t: (p, 0)), kblk, kblk,
                   pl.BlockSpec((None, 2, tk), lambda p, t: (p, 0, kj_of(t)))],
        out_shape=[jax.ShapeDtypeStruct((MIX_W, s), F32), jax.ShapeDtypeStruct((MIX_W, s), F32),
                   jax.ShapeDtypeStruct((MIX_W, s), F32), jax.ShapeDtypeStruct((npair, 2, s), F32)],
        scratch_shapes=[pltpu.VMEM((2, tk, LANES), F32)],
        compiler_params=_cp("arbitrary", "arbitrary"))(hb, hb, hb, dcat, dsum, lse, c_t3, c_rep)


def _loss_head(y, target, name):
    s, d = y.shape
    ts = _rows(s)

    def body(y_ref, t_ref, dy_ref, l_ref):
        i = pl.program_id(0)
        e = y_ref[...] - t_ref[...]
        dy_ref[...] = e * (1.0 / d)

        @pl.when(i == 0)
        def _():
            l_ref[...] = jnp.zeros_like(l_ref)

        part = jnp.sum(jnp.sum(e * e, axis=1, keepdims=True), axis=0, keepdims=True)
        l_ref[...] += part * (0.5 / d)

    row = pl.BlockSpec((ts, d), lambda i: (i, 0))
    return pl.pallas_call(
        body, name=name, grid=(s // ts,), in_specs=[row, row],
        out_specs=[row, pl.BlockSpec((1, 1), lambda i: (0, 0))],
        out_shape=[jax.ShapeDtypeStruct((s, d), F32), jax.ShapeDtypeStruct((1, 1), F32)],
        compiler_params=_cp("arbitrary"))(y, target)


def _adam_rows(r, c):
    cap = max(8, (1 << 20) // (4 * c))
    if r <= cap:
        return r
    best = None
    for t in range(8, cap + 1, 8):
        if r % t == 0:
            best = t
    return best if best is not None else r


def _reduce_adamw(contribs, w, m, v, name):
    nl = len(contribs)
    nd, r, c = contribs[0].shape
    tr = _adam_rows(r, c)
    bc1 = 1.0 - ADAM_B1 ** ADAM_STEP
    bc2 = 1.0 - ADAM_B2 ** ADAM_STEP

    def body(*refs):
        c_refs = refs[:nl]
        w_ref, m_ref, v_ref, g_ref, d_ref, nm_ref, nv_ref = refs[nl:]
        l = pl.program_id(0)
        for li in range(nl):
            @pl.when(l == li)
            def _(c_ref=c_refs[li]):
                g = c_ref[0].astype(F32)
                for k in range(1, nd):
                    g = g + c_ref[k].astype(F32)
                nm = ADAM_B1 * m_ref[...] + (1.0 - ADAM_B1) * g
                nv = ADAM_B2 * v_ref[...] + (1.0 - ADAM_B2) * (g * g)
                g_ref[...] = g
                nm_ref[...] = nm
                nv_ref[...] = nv
                d_ref[...] = -ADAM_LR * ((nm / bc1) / (jnp.sqrt(nv / bc2) + ADAM_EPS) + ADAM_WD * w_ref[...])

    def c_spec(li):
        return pl.BlockSpec((nd, tr, c), lambda l, i: (0, jnp.where(l == li, i, 0), 0))

    blk = pl.BlockSpec((None, tr, c), lambda l, i: (l, i, 0))
    out = jax.ShapeDtypeStruct((nl, r, c), F32)
    return pl.pallas_call(
        body, name=name, grid=(nl, r // tr),
        in_specs=[c_spec(li) for li in range(nl)] + [blk, blk, blk],
        out_specs=[blk, blk, blk, blk], out_shape=[out, out, out, out],
        compiler_params=_cp("arbitrary", "arbitrary"))(*contribs, w, m, v)


def _mesh_pos():
    return lax.axis_index("x"), lax.axis_index("y"), lax.axis_index("c")


def _peer(pos, k):
    x, y, c = pos
    return (1 - x if k & 4 else x, 1 - y if k & 2 else y, 1 - c if k & 1 else c)


def _linear(pos):
    return 4 * pos[0] + 2 * pos[1] + pos[2]


def _xfer_copies(srcs, lands, send_sems, recv_sems, local_sems, gather):
    pos = _mesh_pos()
    me = _linear(pos)
    local, remote = [], []
    for i, (src, land) in enumerate(zip(srcs, lands)):
        local.append(pltpu.make_async_copy(src if gather else src.at[me], land.at[me], local_sems.at[i]))
        for k in range(1, N_DEV):
            peer = _peer(pos, k)
            remote.append(pltpu.make_async_remote_copy(
                src_ref=src if gather else src.at[_linear(peer)], dst_ref=land.at[me],
                send_sem=send_sems.at[i * (N_DEV - 1) + k - 1], recv_sem=recv_sems.at[i * (N_DEV - 1) + k - 1],
                device_id=peer, device_id_type=MESH_ID))
    return local, remote


_HBM = pl.BlockSpec(memory_space=pltpu.HBM)
_SEM = pl.BlockSpec(memory_space=pltpu.SEMAPHORE)
_EFFECT = pltpu.SideEffectType.DATAFLOW_SIDE_EFFECTING


def _xfer_start(srcs, gather, name, after=()):
    n = len(srcs)
    na = len(after)
    lands = [lax.empty(((N_DEV,) + a.shape) if gather else a.shape, a.dtype) for a in srcs]

    def body(*refs):
        src, land = refs[:n], refs[n:2 * n]
        send_sems, recv_sems, local_sems = refs[2 * n + na:2 * n + na + 3]
        local, remote = _xfer_copies(src, land, send_sems, recv_sems, local_sems, gather)
        for cp in local + remote:
            cp.start()
        refs[-1][...] = jnp.zeros_like(refs[-1])

    nsem = n * (N_DEV - 1)
    out = pl.pallas_call(
        body, name=name,
        out_shape=(pltpu.SemaphoreType.DMA((nsem,)), pltpu.SemaphoreType.DMA((nsem,)), pltpu.SemaphoreType.DMA((n,)),
                   *[pltpu.HBM(a.shape, a.dtype) for a in srcs], *[pltpu.HBM(a.shape, a.dtype) for a in lands],
                   jax.ShapeDtypeStruct((8, LANES), F32)),
        in_specs=[_HBM] * (2 * n) + [pl.BlockSpec(memory_space=pl.ANY)] * na,
        out_specs=(_SEM, _SEM, _SEM, *[_HBM] * (2 * n), pl.BlockSpec(memory_space=pltpu.VMEM)),
        input_output_aliases={i: 3 + i for i in range(2 * n)},
        compiler_params=pltpu.CompilerParams(has_side_effects=_EFFECT))(
            *[pltpu.with_memory_space_constraint(a, pltpu.HBM) for a in srcs],
            *[pltpu.with_memory_space_constraint(a, pltpu.HBM) for a in lands], *after)
    return out[:3], list(out[3:3 + n]), list(out[3 + n:3 + 2 * n]), out[-1]


def _started(handle):
    return handle[3]


def _xfer_wait(handle, after, gather, name):
    sems, srcs, lands, _ = handle
    n = len(srcs)

    def body(*refs):
        src, land = refs[:n], refs[n:2 * n]
        send_sems, recv_sems, local_sems = refs[2 * n:2 * n + 3]
        local, remote = _xfer_copies(src, land, send_sems, recv_sems, local_sems, gather)
        for cp in local:
            cp.wait()
        for cp in remote:
            cp.wait_send()
            cp.wait_recv()

    out = pl.pallas_call(
        body, name=name,
        out_shape=(*[pltpu.HBM(a.shape, a.dtype) for a in srcs], *[pltpu.HBM(a.shape, a.dtype) for a in lands]),
        in_specs=[_HBM] * (2 * n) + [_SEM] * 3 + [pl.BlockSpec(memory_space=pl.ANY)] * len(after),
        out_specs=tuple([_HBM] * (2 * n)), input_output_aliases={i: i for i in range(2 * n)},
        compiler_params=pltpu.CompilerParams(has_side_effects=_EFFECT))(*srcs, *lands, *sems, *after)
    return list(out[n:])


def _cols_full(g):
    nd, r, c = g.shape
    return jnp.transpose(g, (1, 0, 2)).reshape(r, nd * c)


def _cols_split(full):
    r, n = full.shape
    return jnp.transpose(full.reshape(r, N_DEV, n // N_DEV), (1, 0, 2))


def _pack_b_in(w):
    qkv = 3 * MIX_W
    pad = jnp.zeros((w.shape[0], B_IN_PAD - w.shape[1]), w.dtype)
    return jnp.concatenate([w[:, :qkv], w[:, qkv + N_MIX_HEADS:], w[:, qkv:qkv + N_MIX_HEADS], pad], axis=1)


def _unpack_b_in(w):
    qkv = 3 * MIX_W
    return jnp.concatenate([w[:, :qkv], w[:, qkv + MEM_W:qkv + MEM_W + N_MIX_HEADS], w[:, qkv:qkv + MEM_W]], axis=1)


def _to_classes(t, g):
    r = 4 ** g
    s, w = t.shape
    return jnp.transpose(t.reshape(s // r, r, w), (1, 0, 2)).reshape(s, w)


def _from_classes(t, g):
    r = 4 ** g
    s, w = t.shape
    return jnp.transpose(t.reshape(r, s // r, w), (1, 0, 2)).reshape(s, w)


def _group_stack(t):
    return jnp.stack([_to_classes(t[:, g * GROUP_W:(g + 1) * GROUP_W], g) for g in range(N_GROUPS)])


def _group_unstack(t3):
    return jnp.concatenate([_from_classes(t3[g], g) for g in range(N_GROUPS)], axis=1)


def _same_stack(t):
    return jnp.stack([_to_classes(t, g) for g in range(N_GROUPS)])


def _same_unstack(t3):
    return jnp.stack([_from_classes(t3[g], g) for g in range(N_GROUPS)])


def _ffn_forward(x, xb, wgu, get_rest, tag, fused=True):
    if fused:
        wd4, gain, bias = get_rest(x)
        y, yb, gu, a, xh, rstd = _ffn_fwd_main(x, xb, wgu, wd4, gain, bias, f"{tag}_fwd_main")
    else:
        gu, a = _ffn_up(xb, wgu, f"{tag}_up")
        wd4, gain, bias = get_rest(a)
        y, yb, xh, rstd = _mm_res_ln(a, wd4, x, gain, bias, 0.5, f"{tag}_down_ln")
    return y, yb, (xb, gu, a, xh, rstd), wd4


def _ffn_backward(dy, saved, wgu, wd4, gain, tag, after=(), send=None):
    xb, gu, a, xh, rstd = saved
    s = xb.shape[0]
    nd, c, d = wgu.shape
    dx, dzb, dh, dgain, dbias = _ffn_bwd_main(dy, xh, rstd, gain, wd4, wgu, gu, f"{tag}_bwd_main", after,
                                               with_dx=send is None)
    dh = dh.reshape(nd, s, c)
    dwd = _mm_tn(a, dzb[None], f"{tag}_dwd").reshape(nd, wd4.shape[1] // 2, d)
    if send is not None:
        send("down", dwd, dgain, dbias)
    dwgu = _mm_tn(dh, xb[None], f"{tag}_dwgu")
    if send is not None:
        sent = send("gate_up", dwgu)
        dx = _mm_nt(dh, wgu, f"{tag}_dx", res=dx, w_rows_out=False, after=sent)
    return dx, dwgu, dwd, dgain, dbias


def _mixer_a_forward(x, xb, memb, w_in, w_kv, w_out, gain, bias, tabs):
    h = _mm_nn(xb, w_in, F32, "a_in", b_rows_out=True)
    hb = _rope_cast([h], tabs, 2 * MIX_W // LANES, "a_rope")
    q3 = _group_stack(hb[:, :MIX_W])
    k3 = _group_stack(hb[:, MIX_W:2 * MIX_W])
    v3 = _group_stack(hb[:, 2 * MIX_W:3 * MIX_W])
    o3, l3 = _band_fwd(q3, k3, v3, "a_band_fwd")
    oa, lt = _band_combine(_same_unstack(o3), _same_unstack(l3), "a_combine")
    kv = _mm_nn(memb, w_kv, BF, "a_mem_kv")
    om, lm = _mem_fwd(hb, 3 * MIX_W // LANES, kv, "a_mem_fwd")
    cat = jnp.concatenate([oa, om], axis=1)
    y, yb, xh, rstd = _mm_res_ln(cat[None], w_out[None], x, gain, bias, 1.0, "a_out_ln")
    return y, yb, (xb, hb, q3, k3, v3, oa, lt, kv, lm, cat, xh, rstd)


def _mixer_a_backward(dy, saved, memb, w_in, w_kv, w_out, gain, tabs_neg, after=()):
    xb, hb, q3, k3, v3, oa, lt, kv, lm, cat, xh, rstd = saved
    dz, dzb, dgain, dbias = _ln_bwd(dy, xh, rstd, gain, 1.0, "a_ln_bwd", after)
    dcat = _mm_nt(dzb[None], w_out[None], "a_dcat", out_dtype=BF)
    dw_out = _mm_tn(cat[None], dzb[None], "a_dwout")[0]
    dqm, dkm, dvm = _mem_bwd(hb, 3 * MIX_W // LANES, kv, dcat, cat, GROUP_W // LANES, lm, "a_mem_bwd")
    dkv = jnp.concatenate([dkm, dvm], axis=1).astype(BF)
    dw_kv = _mm_tn(memb[None], dkv[None], "a_dwkv")[0]
    dq3, dk3, dv3 = _band_bwd(q3, k3, v3, _same_stack(dcat[:, :GROUP_W]), _same_stack(oa), _same_stack(lt),
                              "a_band_bwd")
    dhb = _rope_cast([_group_unstack(dq3), _group_unstack(dk3), _group_unstack(dv3), dqm], tabs_neg,
                     2 * MIX_W // LANES, "a_rope_bwd")
    dw_in = _mm_tn(dhb[None], xb[None], "a_dwin")[0]
    dx = _mm_nt(dhb[None], w_in[None], "a_dx", res=dz, w_rows_out=False)
    return dx, dw_in, dw_kv, dw_out, dgain, dbias


def _pad_rows(t, rows):
    return jnp.concatenate([t, jnp.zeros((rows - t.shape[0], t.shape[1]), t.dtype)], axis=0)


def _pad_cols(t, cols):
    return jnp.concatenate([t, jnp.zeros((t.shape[0], cols - t.shape[1]), t.dtype)], axis=1)


def _mixer_b_forward(x, xb, memb, w_in, fbias, w_kv, w_out, gain, bias, tabs):
    s = x.shape[0]
    h = _mm_nn(xb, w_in, F32, "b_in")
    hb = _rope_cast([h], tabs, 0, "b_cast")
    f0 = 3 * MIX_W + MEM_W
    f_t = _pad_rows(jnp.transpose(h[:, f0:f0 + N_MIX_HEADS]), 16)
    bias16 = _pad_rows(jnp.transpose(fbias), 16)
    c_t = _gate_fwd(f_t, bias16, "b_gate_fwd")
    c_t3 = c_t[:N_MIX_HEADS].reshape(N_MIX_HEADS // 2, 2, s)
    c_rep = jnp.broadcast_to(c_t[:N_MIX_HEADS, :, None], (N_MIX_HEADS, s, LANES))
    ob, lb = _foxt_fwd(hb, c_rep, c_t3, "b_fox_fwd")
    kv = _mm_nn(memb, w_kv, BF, "b_mem_kv")
    om, lm = _mem_fwd(hb, 3 * MIX_W // LANES, kv, "b_mem_fwd")
    cat = jnp.concatenate([ob, om], axis=1)
    y, yb, xh, rstd = _mm_res_ln(cat[None], w_out[None], x, gain, bias, 1.0, "b_out_ln")
    return y, yb, (xb, hb, f_t, bias16, c_rep, c_t3, lb, kv, lm, cat, xh, rstd)


def _mixer_b_backward(dy, saved, memb, w_in, w_kv, w_out, gain, tabs, after=()):
    xb, hb, f_t, bias16, c_rep, c_t3, lb, kv, lm, cat, xh, rstd = saved
    s = xb.shape[0]
    dz, dzb, dgain, dbias = _ln_bwd(dy, xh, rstd, gain, 1.0, "b_ln_bwd", after)
    dcat = _mm_nt(dzb[None], w_out[None], "b_dcat", out_dtype=BF)
    dw_out = _mm_tn(cat[None], dzb[None], "b_dwout")[0]
    dqm, dkm, dvm = _mem_bwd(hb, 3 * MIX_W // LANES, kv, dcat, cat, MIX_W // LANES, lm, "b_mem_bwd")
    dkv = jnp.concatenate([dkm, dvm], axis=1).astype(BF)
    dw_kv = _mm_tn(memb[None], dkv[None], "b_dwkv")[0]
    dsum = _foxt_dsum(hb, dcat, lb, c_rep, c_t3, "b_fox_dsum")
    dq, dk, dv, dc3 = _foxt_bwd(hb, dcat, dsum, lb, c_rep, c_t3, "b_fox_bwd")
    df_t, dfb = _gate_bwd(_pad_rows(dc3.reshape(N_MIX_HEADS, s), 16), f_t, bias16, "b_gate_bwd")
    df = _pad_cols(jnp.transpose(df_t[:N_MIX_HEADS]), B_IN_PAD - 3 * MIX_W - MEM_W)
    dhb = _rope_cast([dq, dk, dv, dqm, df], tabs, 0, "b_cast_bwd", transposed=(0, 1, 2))
    dw_in = _mm_tn(xb[None], dhb[None], "b_dwin")[0]
    dx = _mm_nt(dhb[None], w_in[None], "b_dx", res=dz)
    return dx, dw_in, jnp.transpose(dfb[:N_MIX_HEADS]), dw_kv, dw_out, dgain, dbias


def _stored(t, name):
    return jnp.transpose(t, (0, 2, 1)) if name in ROWS_OUT else t


GATHER_GROUPS = (
    (("ffn1_w_gate_up", 0),),
    (("ffn1_w_down", 0), ("ln_gain", None), ("ln_bias", None)),
    (("a_w_in", 0), ("a_w_out", 0), ("mem_w_kv", 0)),
    (("ffn2_w_gate_up", 0), ("ffn2_w_down", 0)),
    (("ffn1_w_gate_up", 1), ("ffn1_w_down", 1)),
    (("b_w_in", 0), ("b_w_out", 0), ("mem_w_kv", 1)),
    (("ffn2_w_gate_up", 1), ("ffn2_w_down", 1)),
)


def _group_shards(group, params):
    return [t if n in F32_COMM else _stored(t, n)[l].astype(BF) for (n, l), t in zip(group, params)]


def _weight_groups(w):
    return [_group_shards(grp, [w[n] for n, _ in grp]) for grp in GATHER_GROUPS]


def _local_step(x, mem, target, fbias, get_w, put_g):
    s, d = x.shape
    tabs = _rope_tables(s, 1.0)
    tabs_neg = _rope_tables(s, -1.0)
    memb = mem.astype(BF)
    saved, wl = [], []
    cur, curb = x, x.astype(BF)
    ln = []

    def down4(t):
        return t.reshape(N_DEV // 2, -1, d)

    for i in range(DEPTH):
        if i == 0:
            def first_rest(a):
                g = get_w(1, a)
                ln.extend(jnp.transpose(t, (1, 2, 0, 3)).reshape(DEPTH, 3, 1, d) for t in g[1:3])
                return down4(g[0]), ln[0][0, 0], ln[1][0, 0]

            wgu = get_w(0, cur)[0]
            cur, curb, s1, wd = _ffn_forward(cur, curb, wgu, first_rest, "l0_ffn1", fused=False)
        else:
            g = get_w(3 * i + 1, cur)
            wgu = g[0]
            cur, curb, s1, wd = _ffn_forward(cur, curb, wgu, lambda a, g=g: (down4(g[1]), ln[0][i, 0], ln[1][i, 0]),
                                             f"l{i}_ffn1")
        w1 = (wgu, wd)
        ln_g, ln_b = ln
        g = get_w(3 * i + 2, cur)
        if i == 0:
            wm = (g[0].reshape(-1, d), g[2].reshape(d, -1), _cols_full(g[1]))
            cur, curb, s2 = _mixer_a_forward(cur, curb, memb, wm[0], wm[1], wm[2], ln_g[i, 1], ln_b[i, 1], tabs)
        else:
            wm = (_pack_b_in(g[0].reshape(d, -1)), g[2].reshape(d, -1), g[1].reshape(d, -1))
            cur, curb, s2 = _mixer_b_forward(cur, curb, memb, wm[0], fbias, wm[1], wm[2], ln_g[i, 1], ln_b[i, 1],
                                             tabs)
        g = get_w(3 * i + 3, cur)
        cur, curb, s3, wd = _ffn_forward(cur, curb, g[0], lambda a, g=g: (down4(g[1]), ln_g[i, 2], ln_b[i, 2]),
                                         f"l{i}_ffn2")
        w3 = (g[0], wd)
        saved.append((s1, s2, s3))
        wl.append((w1, wm, w3))

    dy, loss = _loss_head(cur, target, "loss_head")

    dgs = [[None] * 3 for _ in range(DEPTH)]
    dbs = [[None] * 3 for _ in range(DEPTH)]
    sent = ()
    for i in reversed(range(DEPTH)):
        s1, s2, s3 = saved[i]
        w1, wm, w3 = wl[i]
        dy, dgu, dd, dgs[i][2], dbs[i][2] = _ffn_backward(dy, s3, w3[0], w3[1], ln_g[i, 2], f"l{i}_ffn2", sent)
        sent = put_g(3 * i + 2, [dgu, dd])
        if i == 0:
            dy, dw_in, dw_kv, dw_out, dgs[i][1], dbs[i][1] = _mixer_a_backward(
                dy, s2, memb, wm[0], wm[1], wm[2], ln_g[i, 1], tabs_neg, sent)
            sent = put_g(1, [dw_in.reshape(N_DEV, -1, d), _cols_split(dw_out),
                             dw_kv.reshape(N_DEV, d // N_DEV, -1)])
        else:
            dy, dw_in, dfb, dw_kv, dw_out, dgs[i][1], dbs[i][1] = _mixer_b_backward(
                dy, s2, memb, wm[0], wm[1], wm[2], ln_g[i, 1], tabs, sent)
            sent = put_g(4, [_unpack_b_in(dw_in).reshape(N_DEV, d // N_DEV, -1),
                             dw_out.reshape(N_DEV, d // N_DEV, -1), dw_kv.reshape(N_DEV, d // N_DEV, -1),
                             jnp.broadcast_to(dfb[None], (N_DEV,) + dfb.shape)])
        if i == 0:
            def send_last(kind, dw, dgain=None, dbias=None):
                if kind == "gate_up":
                    return put_g(6, [dw])
                dgs[0][0], dbs[0][0] = dgain, dbias
                ln_pieces = []
                for parts in (dgs, dbs):
                    t = jnp.concatenate([parts[a][b] for a in range(DEPTH) for b in range(3)], axis=0)
                    ln_pieces.append(jnp.transpose(t.reshape(DEPTH * 3, N_DEV, d // N_DEV), (1, 0, 2)))
                return put_g(0, [dw] + ln_pieces)

            dy = _ffn_backward(dy, s1, w1[0], w1[1], ln_g[i, 0], "l0_ffn1", sent, send_last)[0]
        else:
            dy, dgu, dd, dgs[i][0], dbs[i][0] = _ffn_backward(dy, s1, w1[0], w1[1], ln_g[i, 0], f"l{i}_ffn1", sent)
            sent = put_g(3, [dgu, dd])
    return loss, dy


WEIGHTS = ("ffn1_w_gate_up", "ffn1_w_down", "ffn2_w_gate_up", "ffn2_w_down", "ln_gain", "ln_bias", "mem_w_kv",
           "a_w_in", "a_w_out", "b_w_in", "b_forget_bias", "b_w_out")
F32_COMM = ("ln_gain", "ln_bias", "b_forget_bias")
ROWS_OUT = ("ffn1_w_gate_up", "ffn2_w_gate_up", "a_w_in")
GRAD_SLOTS = {
    "ffn1_w_gate_up": [(6, 0), (3, 0)], "ffn1_w_down": [(0, 0), (3, 1)],
    "ffn2_w_gate_up": [(2, 0), (5, 0)], "ffn2_w_down": [(2, 1), (5, 1)],
    "ln_gain": [(0, 1)], "ln_bias": [(0, 2)], "mem_w_kv": [(1, 2), (4, 2)],
    "a_w_in": [(1, 0)], "a_w_out": [(1, 1)], "b_w_in": [(4, 0)], "b_forget_bias": [(4, 3)], "b_w_out": [(4, 1)],
}


def kernel(x, mem, ffn1_w_gate_up, ffn1_w_down, ffn2_w_gate_up, ffn2_w_down, ln_gain, ln_bias, mem_w_kv, a_w_in, a_w_out, b_w_in, b_forget_bias, b_w_out, loss_target, m_ffn1_w_gate_up, m_ffn1_w_down, m_ffn2_w_gate_up, m_ffn2_w_down, m_ln_gain, m_ln_bias, m_mem_w_kv, m_a_w_in, m_a_w_out, m_b_w_in, m_b_forget_bias, m_b_w_out, v_ffn1_w_gate_up, v_ffn1_w_down, v_ffn2_w_gate_up, v_ffn2_w_down, v_ln_gain, v_ln_bias, v_mem_w_kv, v_a_w_in, v_a_w_out, v_b_w_in, v_b_forget_bias, v_b_w_out):
    w = dict(zip(WEIGHTS, (ffn1_w_gate_up, ffn1_w_down, ffn2_w_gate_up, ffn2_w_down, ln_gain, ln_bias, mem_w_kv,
                           a_w_in, a_w_out, b_w_in, b_forget_bias, b_w_out)))
    m = dict(zip(WEIGHTS, (m_ffn1_w_gate_up, m_ffn1_w_down, m_ffn2_w_gate_up, m_ffn2_w_down, m_ln_gain, m_ln_bias,
                           m_mem_w_kv, m_a_w_in, m_a_w_out, m_b_w_in, m_b_forget_bias, m_b_w_out)))
    v = dict(zip(WEIGHTS, (v_ffn1_w_gate_up, v_ffn1_w_down, v_ffn2_w_gate_up, v_ffn2_w_down, v_ln_gain, v_ln_bias,
                           v_mem_w_kv, v_a_w_in, v_a_w_out, v_b_w_in, v_b_forget_bias, v_b_w_out)))

    gathers = []
    for k, grp in enumerate(GATHER_GROUPS):
        params, behind = [w[n] for n, _ in grp], [_started(h) for h in gathers[-1:]]
        if behind:
            params, behind = lax.optimization_barrier((params, behind))
        gathers.append(_xfer_start(_group_shards(grp, params), True, f"gather{k}_start", behind))
    exchanges = {}

    def get_w(k, after):
        behind = [after] + ([_started(h) for h in gathers] if k == 0 else [])
        return _xfer_wait(gathers[k], behind, True, f"gather{k}_wait")

    def put_g(k, pieces):
        behind = [_started(exchanges[0])] if k == 6 else []
        exchanges[k] = _xfer_start(pieces, False, f"grads{k}_start", behind)
        return (_started(exchanges[k]),)

    loss, grad_x = _local_step(x[0], mem[0], loss_target[0], b_forget_bias, get_w, put_g)
    loss = lax.psum(loss[0, 0], ("x", "y", "c"))

    outs, landed = {}, {}

    def adamw(names):
        for n in names:
            contribs = [landed[g][j] for g, j in GRAD_SLOTS[n]]
            view = (len(contribs),) + contribs[0].shape[1:]
            shape = _stored(w[n], n).shape
            res = _reduce_adamw(contribs, *[_stored(t[n], n).reshape(view) for t in (w, m, v)], f"adamw_{n}")
            outs[n] = [_stored(t.reshape(shape), n) for t in res]
        return [outs[n][3] for n in names]

    after = [grad_x]
    for k in (5, 4, 3, 2, 1):
        landed[k] = _xfer_wait(exchanges[k], after, False, f"grads{k}_wait")
        after = [landed[k][0]]
    done = adamw(("ffn2_w_gate_up", "ffn2_w_down", "mem_w_kv", "a_w_in", "a_w_out", "b_w_in", "b_forget_bias",
                  "b_w_out"))
    landed[0] = _xfer_wait(exchanges[0], done, False, "grads0_wait")
    done = adamw(("ffn1_w_down", "ln_gain", "ln_bias"))
    landed[6] = _xfer_wait(exchanges[6], done, False, "grads6_wait")
    adamw(("ffn1_w_gate_up",))
    return (loss, grad_x[None], *[outs[n][0] for n in WEIGHTS], *[outs[n][1] for n in WEIGHTS],
            *[outs[n][2] for n in WEIGHTS], *[outs[n][3] for n in WEIGHTS])
```

```python
import functools

import jax
import jax.numpy as jnp
from jax import lax
from jax.experimental import pallas as pl
from jax.experimental.pallas import tpu as pltpu

F32 = jnp.float32
BF = jnp.bfloat16
MESH_ID = pl.DeviceIdType.MESH

N_DEV = 8
DEPTH = 2
HEAD_DIM = 64
LANES = 128
N_MIX_HEADS = 12
N_MEM_HEADS = 4
MIX_W = N_MIX_HEADS * HEAD_DIM
MEM_W = N_MEM_HEADS * HEAD_DIM
N_GROUPS = 3
GROUP_W = MIX_W // N_GROUPS
BLOCK = 128
BAND_SUB = 4
ROT_HALF = 8
ROPE_THETA = 500000.0
ALPHA = (2 * DEPTH) ** 0.25
LN_EPS = 1e-5
SCALE = HEAD_DIM ** -0.5
NEG = -1e30
B_IN_PAD = 2688
ADAM_LR, ADAM_B1, ADAM_B2, ADAM_EPS, ADAM_WD, ADAM_STEP = 0.001, 0.9, 0.999, 1e-08, 0.01, 10
VMEM_LIMIT = 56 * 1024 * 1024


def _cp(*sem):
    return pltpu.CompilerParams(dimension_semantics=sem, vmem_limit_bytes=VMEM_LIMIT)


def _dot(a, b):
    return jnp.dot(a, b, preferred_element_type=F32)


def _dot_nt(a, b):
    return lax.dot_general(a, b, (((1,), (1,)), ((), ())), preferred_element_type=F32)


def _dot_tn(a, b):
    return lax.dot_general(a, b, (((0,), (0,)), ((), ())), preferred_element_type=F32)


def _sigmoid(x):
    return 1.0 / (1.0 + jnp.exp(-x))


def _tile(n, cap=1024):
    if n <= cap:
        return n
    best = LANES
    for t in range(LANES, cap + 1, LANES):
        if n % t == 0:
            best = t
    return best


def _rows(s, cap=512):
    return s if s <= cap else cap


def _mm_nn(a, b, out_dtype, name, b_rows_out=False):
    m, k = a.shape
    n = b.shape[0] if b_rows_out else b.shape[1]
    tm, tn = _rows(m), _tile(n)

    def body(a_ref, b_ref, o_ref):
        prod = _dot_nt(a_ref[...], b_ref[...]) if b_rows_out else _dot(a_ref[...], b_ref[...])
        o_ref[...] = prod.astype(o_ref.dtype)

    b_spec = (pl.BlockSpec((tn, k), lambda j, i: (j, 0)) if b_rows_out
              else pl.BlockSpec((k, tn), lambda j, i: (0, j)))
    return pl.pallas_call(
        body, name=name, grid=(n // tn, m // tm),
        in_specs=[pl.BlockSpec((tm, k), lambda j, i: (i, 0)), b_spec],
        out_specs=pl.BlockSpec((tm, tn), lambda j, i: (i, j)),
        out_shape=jax.ShapeDtypeStruct((m, n), out_dtype),
        compiler_params=_cp("parallel", "parallel"))(a, b)


def _resident(shape, index_map):
    return pl.BlockSpec(shape, index_map, pipeline_mode=pl.Buffered(1))


def _mm_tn(a, b, name, out_dtype=BF):
    na, s, m = a.shape
    nb, _, n = b.shape
    no = max(na, nb)
    tm, tn = _tile(m), _tile(n)

    def body(a_ref, b_ref, o_ref):
        o_ref[...] = _dot_tn(a_ref[...], b_ref[...]).astype(o_ref.dtype)

    def spec(nbatch, width, tile, index_map):
        fixed = nbatch == 1 and width == tile
        return _resident((None, s, tile), index_map) if fixed else pl.BlockSpec((None, s, tile), index_map)

    return pl.pallas_call(
        body, name=name, grid=(no, m // tm, n // tn),
        in_specs=[spec(na, m, tm, lambda j, r, c: (j if na > 1 else 0, 0, r)),
                  spec(nb, n, tn, lambda j, r, c: (j if nb > 1 else 0, 0, c))],
        out_specs=pl.BlockSpec((None, tm, tn), lambda j, r, c: (j, r, c)),
        out_shape=jax.ShapeDtypeStruct((no, m, n), out_dtype),
        compiler_params=_cp("parallel", "parallel", "parallel"))(a, b)


def _mm_nt(dh, w, name, res=None, out_dtype=F32, w_rows_out=True, after=()):
    nc, s, kc = dh.shape
    d = w.shape[1] if w_rows_out else w.shape[2]
    ts = _rows(s)
    has_res = res is not None
    mm = _dot_nt if w_rows_out else _dot

    def body(*refs):
        o_ref = refs[-1]
        dh_ref, w_ref = refs[:2]
        if has_res:
            r_ref = refs[2]
        out = mm(dh_ref[0], w_ref[0])
        for j in range(1, nc):
            out = out + mm(dh_ref[j], w_ref[j])
        if has_res:
            out = out + ALPHA * r_ref[...]
        o_ref[...] = out.astype(o_ref.dtype)

    in_specs = [pl.BlockSpec((nc, ts, kc), lambda i: (0, i, 0)), _resident(w.shape, lambda i: (0, 0, 0))]
    args = [dh, w]
    if has_res:
        in_specs.append(pl.BlockSpec((ts, d), lambda i: (i, 0)))
        args.append(res)
    in_specs += [pl.BlockSpec(memory_space=pl.ANY)] * len(after)
    args += list(after)
    return pl.pallas_call(
        body, name=name, grid=(s // ts,), in_specs=in_specs,
        out_specs=pl.BlockSpec((ts, d), lambda i: (i, 0)),
        out_shape=jax.ShapeDtypeStruct((s, d), out_dtype),
        compiler_params=_cp("parallel"))(*args)


def _mm_res_ln(a, w, x, gain, bias, fscale, name):
    nc, s, kc = a.shape
    d = w.shape[2]
    ts = _rows(s)

    def body(a_ref, w_ref, x_ref, g_ref, b_ref, y_ref, yb_ref, xh_ref, r_ref):
        f = _dot(a_ref[0], w_ref[0])
        for j in range(1, nc):
            f = f + _dot(a_ref[j], w_ref[j])
        z = ALPHA * x_ref[...] + fscale * f
        mu = jnp.mean(z, axis=-1, keepdims=True)
        zc = z - mu
        var = jnp.mean(zc * zc, axis=-1, keepdims=True)
        r = lax.rsqrt(var + LN_EPS)
        xh = zc * r
        y = xh * g_ref[...] + b_ref[...]
        y_ref[...] = y
        yb_ref[...] = y.astype(BF)
        xh_ref[...] = xh
        r_ref[...] = r

    row = pl.BlockSpec((ts, d), lambda i: (i, 0))
    vec = pl.BlockSpec((1, d), lambda i: (0, 0))
    return pl.pallas_call(
        body, name=name, grid=(s // ts,),
        in_specs=[pl.BlockSpec((nc, ts, kc), lambda i: (0, i, 0)), _resident((nc, kc, d), lambda i: (0, 0, 0)),
                  row, vec, vec],
        out_specs=[row, row, row, pl.BlockSpec((ts, 1), lambda i: (i, 0))],
        out_shape=[jax.ShapeDtypeStruct((s, d), F32), jax.ShapeDtypeStruct((s, d), BF),
                   jax.ShapeDtypeStruct((s, d), F32), jax.ShapeDtypeStruct((s, 1), F32)],
        compiler_params=_cp("parallel"))(a, w, x, gain, bias)


def _ln_bwd(dy, xh, rstd, gain, fscale, name, after=()):
    s, d = dy.shape
    ts = _rows(s)
    na = len(after)

    def body(*refs):
        dy_ref, xh_ref, r_ref, g_ref = refs[:4]
        dz_ref, dzb_ref, dg_ref, db_ref = refs[4 + na:]
        i = pl.program_id(0)
        dyv = dy_ref[...]
        xhv = xh_ref[...]
        dxh = dyv * g_ref[...]
        m1 = jnp.mean(dxh, axis=-1, keepdims=True)
        m2 = jnp.mean(dxh * xhv, axis=-1, keepdims=True)
        dz = r_ref[...] * (dxh - m1 - xhv * m2)
        dz_ref[...] = dz
        dzb_ref[...] = (fscale * dz).astype(BF)

        @pl.when(i == 0)
        def _():
            dg_ref[...] = jnp.zeros_like(dg_ref)
            db_ref[...] = jnp.zeros_like(db_ref)

        dg_ref[...] += jnp.sum(dyv * xhv, axis=0, keepdims=True)
        db_ref[...] += jnp.sum(dyv, axis=0, keepdims=True)

    row = pl.BlockSpec((ts, d), lambda i: (i, 0))
    vec = pl.BlockSpec((1, d), lambda i: (0, 0))
    return pl.pallas_call(
        body, name=name, grid=(s // ts,),
        in_specs=[row, row, pl.BlockSpec((ts, 1), lambda i: (i, 0)), vec] + [pl.BlockSpec(memory_space=pl.ANY)] * na,
        out_specs=[row, row, vec, vec],
        out_shape=[jax.ShapeDtypeStruct((s, d), F32), jax.ShapeDtypeStruct((s, d), BF),
                   jax.ShapeDtypeStruct((1, d), F32), jax.ShapeDtypeStruct((1, d), F32)],
        compiler_params=_cp("arbitrary"))(dy, xh, rstd, gain, *after)


def _ffn_up(xb, wgu, name):
    s, d = xb.shape
    c = wgu.shape[1]
    nch = wgu.shape[0] // 2
    ts = _rows(s, 1024)
    w4 = wgu.reshape(2, nch, c, d)

    def body(x_ref, w_ref, gu_ref, a_ref):
        x = x_ref[...]
        g = _dot_nt(x, w_ref[0])
        u = _dot_nt(x, w_ref[1])
        sg = _sigmoid(g)
        t = g * sg
        gu_ref[0] = (u * (sg * (1.0 + g - t))).astype(BF)
        gu_ref[1] = t.astype(BF)
        a_ref[...] = (t * u).astype(BF)

    return pl.pallas_call(
        body, name=name, grid=(nch, s // ts),
        in_specs=[pl.BlockSpec((ts, d), lambda j, i: (i, 0)),
                  pl.BlockSpec((2, None, c, d), lambda j, i: (0, j, 0, 0))],
        out_specs=[pl.BlockSpec((2, None, ts, c), lambda j, i: (0, j, i, 0)),
                   pl.BlockSpec((None, ts, c), lambda j, i: (j, i, 0))],
        out_shape=[jax.ShapeDtypeStruct((2, nch, s, c), BF), jax.ShapeDtypeStruct((nch, s, c), BF)],
        compiler_params=_cp("parallel", "parallel"))(xb, w4)


def _ffn_fwd_main(x, xb, wgu, wd4, gain, bias, name):
    s, d = x.shape
    nch, c = wd4.shape[0], wd4.shape[1]
    ts = _rows(s, 256)

    def body(x_ref, xb_ref, wgu_ref, wd_ref, g_ref, b_ref, y_ref, yb_ref, gu_ref, a_ref, xh_ref, r_ref):
        xbv = xb_ref[...]
        f = jnp.zeros((ts, d), F32)
        for j in range(nch):
            g = _dot_nt(xbv, wgu_ref[j])
            u = _dot_nt(xbv, wgu_ref[nch + j])
            sg = _sigmoid(g)
            t = g * sg
            gu_ref[0, j] = (u * (sg * (1.0 + g - t))).astype(BF)
            gu_ref[1, j] = t.astype(BF)
            act = (t * u).astype(BF)
            a_ref[j] = act
            f = f + _dot(act, wd_ref[j])
        z = ALPHA * x_ref[...] + 0.5 * f
        mu = jnp.mean(z, axis=-1, keepdims=True)
        zc = z - mu
        var = jnp.mean(zc * zc, axis=-1, keepdims=True)
        r = lax.rsqrt(var + LN_EPS)
        xh = zc * r
        y = xh * g_ref[...] + b_ref[...]
        y_ref[...] = y
        yb_ref[...] = y.astype(BF)
        xh_ref[...] = xh
        r_ref[...] = r

    row = pl.BlockSpec((ts, d), lambda i: (i, 0))
    vec = pl.BlockSpec((1, d), lambda i: (0, 0))
    return pl.pallas_call(
        body, name=name, grid=(s // ts,),
        in_specs=[row, row, _resident(wgu.shape, lambda i: (0, 0, 0)), _resident(wd4.shape, lambda i: (0, 0, 0)),
                  vec, vec],
        out_specs=[row, row, pl.BlockSpec((2, nch, ts, c), lambda i: (0, 0, i, 0)),
                   pl.BlockSpec((nch, ts, c), lambda i: (0, i, 0)), row, pl.BlockSpec((ts, 1), lambda i: (i, 0))],
        out_shape=[jax.ShapeDtypeStruct((s, d), F32), jax.ShapeDtypeStruct((s, d), BF),
                   jax.ShapeDtypeStruct((2, nch, s, c), BF), jax.ShapeDtypeStruct((nch, s, c), BF),
                   jax.ShapeDtypeStruct((s, d), F32), jax.ShapeDtypeStruct((s, 1), F32)],
        compiler_params=_cp("parallel"))(x, xb, wgu, wd4, gain, bias)


def _ffn_bwd_main(dy, xh, rstd, gain, wd4, wgu, gu, name, after=(), with_dx=True):
    s, d = dy.shape
    nch, c = wd4.shape[0], wd4.shape[1]
    ts = _rows(s, 256)
    na = len(after)

    def body(*refs):
        dy_ref, xh_ref, r_ref, g_ref, wd_ref, wgu_ref, gu_ref = refs[:7]
        dx_ref, dzb_ref, dh_ref, dg_ref, db_ref = refs[7 + na:]
        i = pl.program_id(0)
        dyv = dy_ref[...]
        xhv = xh_ref[...]
        dxh = dyv * g_ref[...]
        m1 = jnp.mean(dxh, axis=-1, keepdims=True)
        m2 = jnp.mean(dxh * xhv, axis=-1, keepdims=True)
        dz = r_ref[...] * (dxh - m1 - xhv * m2)
        dzb = (0.5 * dz).astype(BF)
        dzb_ref[...] = dzb

        @pl.when(i == 0)
        def _():
            dg_ref[...] = jnp.zeros_like(dg_ref)
            db_ref[...] = jnp.zeros_like(db_ref)

        dg_ref[...] += jnp.sum(dyv * xhv, axis=0, keepdims=True)
        db_ref[...] += jnp.sum(dyv, axis=0, keepdims=True)

        dx = ALPHA * dz if with_dx else dz
        for j in range(nch):
            da = _dot_nt(dzb, wd_ref[j])
            dgate = (da * gu_ref[0, j].astype(F32)).astype(BF)
            dup = (da * gu_ref[1, j].astype(F32)).astype(BF)
            dh_ref[0, j] = dgate
            dh_ref[1, j] = dup
            if with_dx:
                dx = dx + _dot(dgate, wgu_ref[j]) + _dot(dup, wgu_ref[nch + j])
        dx_ref[...] = dx

    row = pl.BlockSpec((ts, d), lambda i: (i, 0))
    vec = pl.BlockSpec((1, d), lambda i: (0, 0))
    act = pl.BlockSpec((2, nch, ts, c), lambda i: (0, 0, i, 0))
    return pl.pallas_call(
        body, name=name, grid=(s // ts,),
        in_specs=[row, row, pl.BlockSpec((ts, 1), lambda i: (i, 0)), vec,
                  _resident(wd4.shape, lambda i: (0, 0, 0)), _resident(wgu.shape, lambda i: (0, 0, 0)), act]
                 + [pl.BlockSpec(memory_space=pl.ANY)] * na,
        out_specs=[row, row, act, vec, vec],
        out_shape=[jax.ShapeDtypeStruct((s, d), F32), jax.ShapeDtypeStruct((s, d), BF),
                   jax.ShapeDtypeStruct((2, nch, s, c), BF),
                   jax.ShapeDtypeStruct((1, d), F32), jax.ShapeDtypeStruct((1, d), F32)],
        compiler_params=_cp("arbitrary"))(dy, xh, rstd, gain, wd4, wgu, gu, *after)


def _rope_tables(s, sign):
    pos = jnp.arange(s, dtype=F32)
    inv_freq = 1.0 / (ROPE_THETA ** (jnp.arange(ROT_HALF, dtype=F32) / ROT_HALF))
    ang = pos[:, None] * inv_freq[None, :]
    cos, sin = jnp.cos(ang), jnp.sin(ang) * sign
    one = jnp.ones((s, HEAD_DIM - 2 * ROT_HALF), F32)
    zero = jnp.zeros((s, HEAD_DIM - 2 * ROT_HALF), F32)
    zh = jnp.zeros((s, ROT_HALF), F32)
    cos_f = jnp.concatenate([cos, cos, one], axis=1)
    sin_a = jnp.concatenate([-sin, zh, zero], axis=1)
    sin_b = jnp.concatenate([zh, sin, zero], axis=1)
    rep = LANES // HEAD_DIM
    return tuple(jnp.tile(t, (1, rep)) for t in (cos_f, sin_a, sin_b))


def _rope(t, c_ref, sa_ref, sb_ref):
    return (t * c_ref[...] + pltpu.roll(t, LANES - ROT_HALF, 1) * sa_ref[...]
            + pltpu.roll(t, ROT_HALF, 1) * sb_ref[...])


def _proj_rope(xb, w, tabs, n_rope, name, w_rows_out, tail_block=None):
    s, d = xb.shape
    n = w.shape[0] if w_rows_out else w.shape[1]
    tm = _rows(s, 256)
    has_tail = tail_block is not None

    def body(x_ref, w_ref, c_ref, sa_ref, sb_ref, o_ref, *tail_ref):
        h = (_dot_nt if w_rows_out else _dot)(x_ref[...], w_ref[...])
        for cb in range(n // LANES):
            t = h[:, cb * LANES:(cb + 1) * LANES]
            if cb < n_rope:
                t = _rope(t, c_ref, sa_ref, sb_ref)
            o_ref[:, cb * LANES:(cb + 1) * LANES] = t.astype(BF)
        if has_tail:
            tail_ref[0][...] = h[:, tail_block * LANES:(tail_block + 1) * LANES]

    tab = pl.BlockSpec((tm, LANES), lambda i: (i, 0))
    out_specs = [pl.BlockSpec((tm, n), lambda i: (i, 0))]
    out_shape = [jax.ShapeDtypeStruct((s, n), BF)]
    if has_tail:
        out_specs.append(tab)
        out_shape.append(jax.ShapeDtypeStruct((s, LANES), F32))
    res = pl.pallas_call(
        body, name=name, grid=(s // tm,),
        in_specs=[pl.BlockSpec((tm, d), lambda i: (i, 0)), _resident(w.shape, lambda i: (0, 0)), tab, tab, tab],
        out_specs=out_specs, out_shape=out_shape, compiler_params=_cp("parallel"))(xb, w, *tabs)
    return res if has_tail else res[0]


def _rope_cast(parts, tabs, n_rope, name, transposed=()):
    s = tabs[0].shape[0]
    flip = [i in transposed for i in range(len(parts))]
    widths = [p.shape[0] if f else p.shape[1] for p, f in zip(parts, flip)]
    n = sum(widths)
    npart = len(parts)
    ts = _rows(s, 256)

    def body(*refs):
        part_refs = refs[:npart]
        c_ref, sa_ref, sb_ref, o_ref = refs[npart:]
        col = 0
        for ref, w, f in zip(part_refs, widths, flip):
            for j in range(w // LANES):
                if f:
                    t = jnp.transpose(ref[j * LANES:(j + 1) * LANES, :])
                else:
                    t = ref[:, j * LANES:(j + 1) * LANES]
                if col < n_rope:
                    t = _rope(t, c_ref, sa_ref, sb_ref)
                o_ref[:, col * LANES:(col + 1) * LANES] = t.astype(BF)
                col += 1

    tab = pl.BlockSpec((ts, LANES), lambda i: (i, 0))
    return pl.pallas_call(
        body, name=name, grid=(s // ts,),
        in_specs=[pl.BlockSpec((w, ts), lambda i: (0, i)) if f else pl.BlockSpec((ts, w), lambda i: (i, 0))
                  for w, f in zip(widths, flip)] + [tab, tab, tab],
        out_specs=pl.BlockSpec((ts, n), lambda i: (i, 0)),
        out_shape=jax.ShapeDtypeStruct((s, n), BF),
        compiler_params=_cp("parallel"))(*parts, *tabs)


def _head_masks():
    lane = lax.broadcasted_iota(jnp.int32, (1, LANES), 1)
    return [lane < HEAD_DIM, lane >= HEAD_DIM]


def _sel(mask, v):
    return jnp.where(mask, v, jnp.zeros_like(v))


def _pick(mask, wide, fill):
    return jnp.max(jnp.where(mask, wide, fill), axis=1, keepdims=True)


def _band_masks(has_other, prev):
    qi = lax.broadcasted_iota(jnp.int32, (BLOCK, BLOCK), 0)
    kj = lax.broadcasted_iota(jnp.int32, (BLOCK, BLOCK), 1)
    if prev:
        return kj >= qi + jnp.where(has_other, 0, BLOCK)
    return kj <= qi


class _BandView:
    def __init__(self, s, g):
        self.r = 4 ** g
        self.nl = s // self.r
        self.nblk = self.nl // BLOCK
        self.nsub = min(BAND_SUB, self.nblk)
        self.tile = self.nsub * BLOCK
        self.grid = (self.r, GROUP_W // LANES, self.nblk // self.nsub)

    def view(self, a):
        return a.reshape(self.nl, self.r * a.shape[1])

    def specs(self, width, off):
        nb, nsub, last = width // LANES, self.nsub, self.nblk - 1

        def col(rho, p):
            return rho * nb + off + p

        return (pl.BlockSpec((self.tile, LANES), lambda rho, p, t: (t, col(rho, p))),
                pl.BlockSpec((BLOCK, LANES), lambda rho, p, t: (jnp.maximum(t * nsub - 1, 0), col(rho, p))),
                pl.BlockSpec((BLOCK, LANES), lambda rho, p, t: (jnp.minimum(t * nsub + nsub, last), col(rho, p))))


def _band_fwd(hb, g, name):
    s, n = hb.shape
    bv = _BandView(s, g)
    nsub = bv.nsub
    npair = MIX_W // LANES

    def body(q_ref, kc_ref, kp_ref, vc_ref, vp_ref, o_ref, l_ref):
        t = pl.program_id(2)
        mc = _band_masks(None, False)
        hm = _head_masks()
        for i in range(nsub):
            rows = slice(i * BLOCK, (i + 1) * BLOCK)
            has_prev = t > 0 if i == 0 else True
            mp = _band_masks(has_prev, True)
            q, kc, vc = q_ref[rows, :], kc_ref[rows, :], vc_ref[rows, :]
            if i == 0:
                kp, vp = kp_ref[...], vp_ref[...]
            else:
                prev = slice((i - 1) * BLOCK, i * BLOCK)
                kp, vp = kc_ref[prev, :], vc_ref[prev, :]
            o = jnp.zeros((BLOCK, LANES), F32)
            lse_w = jnp.zeros((BLOCK, LANES), F32)
            for h in range(2):
                qh = _sel(hm[h], q)
                sc = jnp.where(mc, _dot_nt(qh, kc) * SCALE, NEG)
                sp = jnp.where(mp, _dot_nt(qh, kp) * SCALE, NEG)
                m = jnp.maximum(jnp.max(sc, axis=1, keepdims=True), jnp.max(sp, axis=1, keepdims=True))
                pc = jnp.exp(sc - m)
                pp = jnp.exp(sp - m)
                l = jnp.sum(pc, axis=1, keepdims=True) + jnp.sum(pp, axis=1, keepdims=True)
                oh = _dot(pc.astype(BF), _sel(hm[h], vc)) + _dot(pp.astype(BF), _sel(hm[h], vp))
                o = o + oh / l
                lse_w = jnp.where(hm[h], m + jnp.log(l), lse_w)
            o_ref[rows, :] = o
            l_ref[rows, :] = lse_w

    q_cur, _, _ = bv.specs(n, g * GROUP_W // LANES)
    k_cur, k_prv, _ = bv.specs(n, npair + g * GROUP_W // LANES)
    v_cur, v_prv, _ = bv.specs(n, 2 * npair + g * GROUP_W // LANES)
    out_spec = bv.specs(GROUP_W, 0)[0]
    out = jax.ShapeDtypeStruct((bv.nl, bv.r * GROUP_W), F32)
    hv = bv.view(hb)
    o, l = pl.pallas_call(
        body, name=name, grid=bv.grid,
        in_specs=[q_cur, k_cur, k_prv, v_cur, v_prv], out_specs=[out_spec, out_spec], out_shape=[out, out],
        compiler_params=_cp("parallel", "parallel", "parallel"))(hv, hv, hv, hv, hv)
    return o.reshape(s, GROUP_W), l.reshape(s, GROUP_W)


def _band_combine(os, ls, name):
    ng = len(os)
    s, w = os[0].shape
    ts = _rows(s)

    def body(*refs):
        o_refs, l_refs = refs[:ng], refs[ng:2 * ng]
        oa_ref, lt_ref = refs[2 * ng:]
        lv = [r[...] for r in l_refs]
        m = functools.reduce(jnp.maximum, lv)
        es = [jnp.exp(l - m) for l in lv]
        den = functools.reduce(lambda a, b: a + b, es)
        num = functools.reduce(lambda a, b: a + b, [es[g] * o_refs[g][...] for g in range(ng)])
        oa_ref[...] = (num / den).astype(BF)
        lt_ref[...] = m + jnp.log(den)

    blk = pl.BlockSpec((ts, w), lambda i: (i, 0))
    return pl.pallas_call(
        body, name=name, grid=(s // ts,), in_specs=[blk] * (2 * ng), out_specs=[blk, blk],
        out_shape=[jax.ShapeDtypeStruct((s, w), BF), jax.ShapeDtypeStruct((s, w), F32)],
        compiler_params=_cp("parallel"))(*os, *ls)


def _band_bwd(hb, dcat, oa, lt, g, name):
    s, n = hb.shape
    bv = _BandView(s, g)
    nsub = bv.nsub
    npair = MIX_W // LANES
    ntile = bv.grid[2]

    def body(q_ref, qn_ref, kc_ref, kp_ref, vc_ref, vp_ref, do_ref, don_ref, oa_ref, oan_ref, lt_ref, ltn_ref,
             dq_ref, dk_ref, dv_ref):
        t = pl.program_id(2)
        mc = _band_masks(None, False)
        hm = _head_masks()

        def block(ref, edge_ref, i):
            if i < 0 or i >= nsub:
                return edge_ref[...]
            return ref[i * BLOCK:(i + 1) * BLOCK, :]

        for i in range(nsub):
            mp = _band_masks(t > 0 if i == 0 else True, True)
            mn = _band_masks(t < ntile - 1 if i == nsub - 1 else True, True)
            q, qn = block(q_ref, None, i), block(q_ref, qn_ref, i + 1)
            kc, kp = block(kc_ref, None, i), block(kc_ref, kp_ref, i - 1)
            vc, vp = block(vc_ref, None, i), block(vc_ref, vp_ref, i - 1)
            do, don = block(do_ref, None, i), block(do_ref, don_ref, i + 1)
            dd = do.astype(F32) * block(oa_ref, None, i).astype(F32)
            ddn = don.astype(F32) * block(oa_ref, oan_ref, i + 1).astype(F32)
            lt, ltn = block(lt_ref, None, i), block(lt_ref, ltn_ref, i + 1)
            dq = jnp.zeros((BLOCK, LANES), F32)
            dk = jnp.zeros((BLOCK, LANES), F32)
            dv = jnp.zeros((BLOCK, LANES), F32)
            for h in range(2):
                qh, doh = _sel(hm[h], q), _sel(hm[h], do)
                qnh, donh = _sel(hm[h], qn), _sel(hm[h], don)
                kch, kph = _sel(hm[h], kc), _sel(hm[h], kp)
                lse = _pick(hm[h], lt, NEG)
                lsen = _pick(hm[h], ltn, NEG)
                dsum = jnp.sum(_sel(hm[h], dd), axis=1, keepdims=True)
                dsumn = jnp.sum(_sel(hm[h], ddn), axis=1, keepdims=True)
                pc = jnp.exp(jnp.where(mc, _dot_nt(qh, kc) * SCALE, NEG) - lse)
                pp = jnp.exp(jnp.where(mp, _dot_nt(qh, kp) * SCALE, NEG) - lse)
                dsc = pc * (_dot_nt(doh, vc) - dsum)
                dsp = pp * (_dot_nt(doh, vp) - dsum)
                dq = dq + SCALE * (_dot(dsc.astype(BF), kch) + _dot(dsp.astype(BF), kph))
                pn = jnp.exp(jnp.where(mn, _dot_nt(qnh, kc) * SCALE, NEG) - lsen)
                dsn = pn * (_dot_nt(donh, vc) - dsumn)
                dk = dk + SCALE * (_dot_tn(dsc.astype(BF), qh) + _dot_tn(dsn.astype(BF), qnh))
                dv = dv + _dot_tn(pc.astype(BF), doh) + _dot_tn(pn.astype(BF), donh)
            rows = slice(i * BLOCK, (i + 1) * BLOCK)
            dq_ref[rows, :] = dq
            dk_ref[rows, :] = dk
            dv_ref[rows, :] = dv

    q_cur, _, q_nxt = bv.specs(n, g * GROUP_W // LANES)
    k_cur, k_prv, _ = bv.specs(n, npair + g * GROUP_W // LANES)
    v_cur, v_prv, _ = bv.specs(n, 2 * npair + g * GROUP_W // LANES)
    do_cur, _, do_nxt = bv.specs(dcat.shape[1], 0)
    w_cur, _, w_nxt = bv.specs(GROUP_W, 0)
    out = jax.ShapeDtypeStruct((bv.nl, bv.r * GROUP_W), F32)
    hv, dv_, ov, lv = bv.view(hb), bv.view(dcat), bv.view(oa), bv.view(lt)
    res = pl.pallas_call(
        body, name=name, grid=bv.grid,
        in_specs=[q_cur, q_nxt, k_cur, k_prv, v_cur, v_prv, do_cur, do_nxt, w_cur, w_nxt, w_cur, w_nxt],
        out_specs=[w_cur, w_cur, w_cur], out_shape=[out, out, out],
        compiler_params=_cp("parallel", "parallel", "parallel"))(
            hv, hv, hv, hv, hv, hv, dv_, dv_, ov, ov, lv, lv)
    return [t.reshape(s, GROUP_W) for t in res]


def _mem_fwd(hb, q_blk0, kv, name):
    s = hb.shape[0]
    m = kv.shape[0]
    tq = _rows(s)
    npair = MEM_W // LANES

    def body(q_ref, k_ref, v_ref, o_ref, l_ref):
        q, k, v = q_ref[...], k_ref[...], v_ref[...]
        hm = _head_masks()
        o = jnp.zeros((tq, LANES), F32)
        lse_w = jnp.zeros((tq, LANES), F32)
        for h in range(2):
            sc = _dot_nt(_sel(hm[h], q), k) * SCALE
            mx = jnp.max(sc, axis=1, keepdims=True)
            p = jnp.exp(sc - mx)
            l = jnp.sum(p, axis=1, keepdims=True)
            o = o + _dot(p.astype(BF), _sel(hm[h], v)) / l
            lse_w = jnp.where(hm[h], mx + jnp.log(l), lse_w)
        o_ref[...] = o.astype(BF)
        l_ref[...] = lse_w

    blk = pl.BlockSpec((tq, LANES), lambda p, i: (i, p))
    return pl.pallas_call(
        body, name=name, grid=(npair, s // tq),
        in_specs=[pl.BlockSpec((tq, LANES), lambda p, i: (i, q_blk0 + p)),
                  pl.BlockSpec((m, LANES), lambda p, i: (0, p)),
                  pl.BlockSpec((m, LANES), lambda p, i: (0, npair + p))],
        out_specs=[blk, blk],
        out_shape=[jax.ShapeDtypeStruct((s, MEM_W), BF), jax.ShapeDtypeStruct((s, MEM_W), F32)],
        compiler_params=_cp("parallel", "parallel"))(hb, kv, kv)


def _mem_bwd(hb, q_blk0, kv, dcat, cat, o_blk0, lse, name):
    s = hb.shape[0]
    m = kv.shape[0]
    tq = _rows(s)
    npair = MEM_W // LANES

    def body(q_ref, k_ref, v_ref, do_ref, o_ref, l_ref, dq_ref, dk_ref, dv_ref):
        i = pl.program_id(1)

        @pl.when(i == 0)
        def _():
            dk_ref[...] = jnp.zeros_like(dk_ref)
            dv_ref[...] = jnp.zeros_like(dv_ref)

        q, k, v, do = q_ref[...], k_ref[...], v_ref[...], do_ref[...]
        dd = do.astype(F32) * o_ref[...].astype(F32)
        lt = l_ref[...]
        hm = _head_masks()
        dq = jnp.zeros((tq, LANES), F32)
        dk = jnp.zeros((m, LANES), F32)
        dv = jnp.zeros((m, LANES), F32)
        for h in range(2):
            qh, doh = _sel(hm[h], q), _sel(hm[h], do)
            p = jnp.exp(_dot_nt(qh, k) * SCALE - _pick(hm[h], lt, NEG))
            ds = p * (_dot_nt(doh, v) - jnp.sum(_sel(hm[h], dd), axis=1, keepdims=True))
            dq = dq + SCALE * _dot(ds.astype(BF), _sel(hm[h], k))
            dk = dk + SCALE * _dot_tn(ds.astype(BF), qh)
            dv = dv + _dot_tn(p.astype(BF), doh)
        dq_ref[...] = dq
        dk_ref[...] += dk
        dv_ref[...] += dv

    row = pl.BlockSpec((tq, LANES), lambda p, i: (i, p))
    orow = pl.BlockSpec((tq, LANES), lambda p, i: (i, o_blk0 + p))
    acc = pl.BlockSpec((m, LANES), lambda p, i: (0, p))
    return pl.pallas_call(
        body, name=name, grid=(npair, s // tq),
        in_specs=[pl.BlockSpec((tq, LANES), lambda p, i: (i, q_blk0 + p)),
                  pl.BlockSpec((m, LANES), lambda p, i: (0, p)),
                  pl.BlockSpec((m, LANES), lambda p, i: (0, npair + p)), orow, orow, row],
        out_specs=[row, acc, acc],
        out_shape=[jax.ShapeDtypeStruct((s, MEM_W), F32), jax.ShapeDtypeStruct((m, MEM_W), F32),
                   jax.ShapeDtypeStruct((m, MEM_W), F32)],
        compiler_params=_cp("parallel", "arbitrary"))(hb, kv, kv, dcat, cat, lse)


def _gate_fwd(f_t, bias, name):
    hp, s = f_t.shape
    nblk = s // LANES

    def body(f_ref, b_ref, c_ref):
        lane = lax.broadcasted_iota(jnp.int32, (hp, LANES), 1)

        def step(i, carry):
            off = pl.multiple_of(i * LANES, LANES)
            x = f_ref[:, pl.ds(off, LANES)] + b_ref[...]
            acc = jnp.minimum(x, 0.0) - jnp.log(1.0 + jnp.exp(-jnp.abs(x)))
            sh = 1
            while sh < LANES:
                acc = acc + jnp.where(lane >= sh, pltpu.roll(acc, sh, 1), 0.0)
                sh *= 2
            acc = acc + carry
            c_ref[:, pl.ds(off, LANES)] = acc
            return acc[:, LANES - 1:LANES]

        lax.fori_loop(0, nblk, step, jnp.zeros((hp, 1), F32))

    vm = pl.BlockSpec(memory_space=pltpu.VMEM)
    return pl.pallas_call(body, name=name, in_specs=[vm, vm], out_specs=vm,
                          out_shape=jax.ShapeDtypeStruct((hp, s), F32),
                          compiler_params=pltpu.CompilerParams(vmem_limit_bytes=VMEM_LIMIT))(f_t, bias)


def _gate_bwd(dc_t, f_t, bias, name):
    hp, s = f_t.shape
    nblk = s // LANES

    def body(dc_ref, f_ref, b_ref, df_ref, db_ref):
        lane = lax.broadcasted_iota(jnp.int32, (hp, LANES), 1)

        def step(t, carry):
            suffix, dbias = carry
            off = pl.multiple_of((nblk - 1 - t) * LANES, LANES)
            acc = dc_ref[:, pl.ds(off, LANES)]
            sh = 1
            while sh < LANES:
                acc = acc + jnp.where(lane < LANES - sh, pltpu.roll(acc, LANES - sh, 1), 0.0)
                sh *= 2
            acc = acc + suffix
            x = f_ref[:, pl.ds(off, LANES)] + b_ref[...]
            df = acc * _sigmoid(-x)
            df_ref[:, pl.ds(off, LANES)] = df
            return acc[:, 0:1], dbias + jnp.sum(df, axis=1, keepdims=True)

        _, dbias = lax.fori_loop(0, nblk, step, (jnp.zeros((hp, 1), F32), jnp.zeros((hp, 1), F32)))
        db_ref[...] = dbias

    vm = pl.BlockSpec(memory_space=pltpu.VMEM)
    return pl.pallas_call(body, name=name, in_specs=[vm, vm, vm], out_specs=[vm, vm],
                          out_shape=[jax.ShapeDtypeStruct((hp, s), F32), jax.ShapeDtypeStruct((hp, 1), F32)],
                          compiler_params=pltpu.CompilerParams(vmem_limit_bytes=VMEM_LIMIT))(dc_t, f_t, bias)


def _wide(rep, width):
    return jnp.tile(rep, (1, width // LANES))


def _fold(t):
    part = t[:, :LANES]
    for c in range(1, t.shape[1] // LANES):
        part = part + t[:, c * LANES:(c + 1) * LANES]
    return part


def _fox_logits(q, k, cq_rep, ck_row, mask, hmask):
    s = _dot_nt(_sel(hmask, q), k) + (_wide(cq_rep, ck_row.shape[1]) - ck_row)
    if mask is not None:
        s = jnp.where(mask, s, NEG)
    return s


def _diag_mask(t):
    return lax.broadcasted_iota(jnp.int32, (t, t), 1) <= lax.broadcasted_iota(jnp.int32, (t, t), 0)


def _fox_fwd(hb, c_rep, c_t3, name):
    s = hb.shape[0]
    npair = MIX_W // LANES
    tq = tk = _rows(s)
    nq = s // tq

    def body(q_ref, k_ref, v_ref, cq_ref, ck_ref, o_ref, l_ref, m_s, l_s, acc):
        qi = pl.program_id(1)
        kj = pl.program_id(2)
        hm = _head_masks()

        @pl.when(kj == 0)
        def _():
            m_s[...] = jnp.full_like(m_s, NEG)
            l_s[...] = jnp.zeros_like(l_s)
            acc[...] = jnp.zeros_like(acc)

        def step(mask):
            q, k, v = q_ref[...] * SCALE, k_ref[...], v_ref[...]
            ck = ck_ref[...]
            for h in range(2):
                sc = _fox_logits(q, k, cq_ref[h], ck[h:h + 1, :], mask, hm[h])
                m_old = m_s[h]
                m_new = jnp.maximum(m_old, jnp.max(sc, axis=1, keepdims=True))
                pr = jnp.exp(sc - _wide(m_new, tk))
                corr = jnp.exp(m_old - m_new)
                l_s[h] = l_s[h] * corr + _fold(pr)
                acc[h] = acc[h] * corr + _dot(pr.astype(BF), _sel(hm[h], v))
                m_s[h] = m_new

        @pl.when(kj < qi)
        def _():
            step(None)

        @pl.when(kj == qi)
        def _():
            step(_diag_mask(tq))
            outs = []
            for h in range(2):
                den = jnp.sum(l_s[h], axis=1, keepdims=True)
                outs.append(acc[h] / den)
                l_ref[h] = m_s[h] + jnp.log(den)
            o_ref[...] = jnp.where(hm[0], outs[0], outs[1]).astype(BF)

    def kv_map(off):
        return lambda p, i, j: (jnp.minimum(j, i), off + p)

    blk = pl.BlockSpec((tq, LANES), lambda p, i, j: (i, p))
    return pl.pallas_call(
        body, name=name, grid=(npair, nq, nq),
        in_specs=[blk, pl.BlockSpec((tk, LANES), kv_map(npair)), pl.BlockSpec((tk, LANES), kv_map(2 * npair)),
                  pl.BlockSpec((2, tq, LANES), lambda p, i, j: (p, i, 0)),
                  pl.BlockSpec((None, 2, tk), lambda p, i, j: (p, 0, jnp.minimum(j, i)))],
        out_specs=[blk, pl.BlockSpec((2, tq, LANES), lambda p, i, j: (p, i, 0))],
        out_shape=[jax.ShapeDtypeStruct((s, MIX_W), BF), jax.ShapeDtypeStruct((2 * npair, s, LANES), F32)],
        scratch_shapes=[pltpu.VMEM((2, tq, LANES), F32), pltpu.VMEM((2, tq, LANES), F32),
                        pltpu.VMEM((2, tq, LANES), F32)],
        compiler_params=_cp("parallel", "parallel", "arbitrary"))(hb, hb, hb, c_rep, c_t3)


def _fox_dsum(hb, dcat, lse, c_rep, c_t3, name):
    s = hb.shape[0]
    npair = MIX_W // LANES
    tq = tk = _rows(s)
    nq = s // tq

    def body(q_ref, k_ref, v_ref, do_ref, l_ref, cq_ref, ck_ref, d_ref, acc):
        qi = pl.program_id(1)
        kj = pl.program_id(2)
        hm = _head_masks()

        @pl.when(kj == 0)
        def _():
            acc[...] = jnp.zeros_like(acc)

        def step(mask):
            q, k, v, do = q_ref[...] * SCALE, k_ref[...], v_ref[...], do_ref[...]
            ck = ck_ref[...]
            for h in range(2):
                pr = jnp.exp(_fox_logits(q, k, cq_ref[h], ck[h:h + 1, :], mask, hm[h]) - _wide(l_ref[h], tk))
                acc[h] += _fold(pr * _dot_nt(_sel(hm[h], do), v))

        @pl.when(kj < qi)
        def _():
            step(None)

        @pl.when(kj == qi)
        def _():
            step(_diag_mask(tq))
            for h in range(2):
                d_ref[h] = jnp.broadcast_to(jnp.sum(acc[h], axis=1, keepdims=True), (tq, LANES))

    def kv_map(off):
        return lambda p, i, j: (jnp.minimum(j, i), off + p)

    blk = pl.BlockSpec((tq, LANES), lambda p, i, j: (i, p))
    rep = pl.BlockSpec((2, tq, LANES), lambda p, i, j: (p, i, 0))
    return pl.pallas_call(
        body, name=name, grid=(npair, nq, nq),
        in_specs=[blk, pl.BlockSpec((tk, LANES), kv_map(npair)), pl.BlockSpec((tk, LANES), kv_map(2 * npair)),
                  blk, rep, rep, pl.BlockSpec((None, 2, tk), lambda p, i, j: (p, 0, jnp.minimum(j, i)))],
        out_specs=rep, out_shape=jax.ShapeDtypeStruct((2 * npair, s, LANES), F32),
        scratch_shapes=[pltpu.VMEM((2, tq, LANES), F32)],
        compiler_params=_cp("parallel", "parallel", "arbitrary"))(hb, hb, hb, dcat, lse, c_rep, c_t3)


def _fox_bwd(hb, dcat, dsum, lse, c_rep, c_t3, name):
    s = hb.shape[0]
    npair = MIX_W // LANES
    tq = tk = _rows(s)
    nq = s // tq

    def body(q_ref, k_ref, v_ref, do_ref, d_ref, l_ref, cq_ref, ck_ref, dq_ref, dk_ref, dv_ref, dc_ref):
        kj = pl.program_id(1)
        qi = pl.program_id(2)
        hm = _head_masks()

        @pl.when(qi == 0)
        def _():
            dk_ref[...] = jnp.zeros_like(dk_ref)
            dv_ref[...] = jnp.zeros_like(dv_ref)
            dc_ref[...] = jnp.zeros_like(dc_ref)

        @pl.when((qi == 0) & (kj == 0))
        def _():
            dq_ref[...] = jnp.zeros_like(dq_ref)

        def step(mask):
            q, k, v, do = q_ref[...] * SCALE, k_ref[...], v_ref[...], do_ref[...]
            ck = ck_ref[...]
            dq = jnp.zeros((tq, LANES), F32)
            dk = jnp.zeros((tk, LANES), F32)
            dv = jnp.zeros((tk, LANES), F32)
            dcs = []
            for h in range(2):
                qh, doh = _sel(hm[h], q), _sel(hm[h], do)
                pr = jnp.exp(_fox_logits(q, k, cq_ref[h], ck[h:h + 1, :], mask, hm[h]) - _wide(l_ref[h], tk))
                ds = pr * (_dot_nt(doh, v) - _wide(d_ref[h], tk))
                dsb = ds.astype(BF)
                dq = dq + _dot(dsb, _sel(hm[h], k))
                dk = dk + _dot_tn(dsb, qh)
                dv = dv + _dot_tn(pr.astype(BF), doh)
                dcs.append(jnp.sum(ds, axis=0, keepdims=True))
            rows = pl.ds(pl.multiple_of(qi * tq, tq), tq)
            dq_ref[rows, :] += SCALE * dq
            dk_ref[...] += dk
            dv_ref[...] += dv
            dc_ref[...] -= jnp.concatenate(dcs, axis=0)

        @pl.when(qi > kj)
        def _():
            step(None)

        @pl.when(qi == kj)
        def _():
            step(_diag_mask(tq))

    def q_map(p, j, i):
        return (jnp.maximum(i, j), p)

    kblk = pl.BlockSpec((tk, LANES), lambda p, j, i: (j, p))
    rep = pl.BlockSpec((2, tq, LANES), lambda p, j, i: (p, jnp.maximum(i, j), 0))
    return pl.pallas_call(
        body, name=name, grid=(npair, nq, nq),
        in_specs=[pl.BlockSpec((tq, LANES), q_map),
                  pl.BlockSpec((tk, LANES), lambda p, j, i: (j, npair + p)),
                  pl.BlockSpec((tk, LANES), lambda p, j, i: (j, 2 * npair + p)),
                  pl.BlockSpec((tq, LANES), q_map), rep, rep, rep,
                  pl.BlockSpec((None, 2, tk), lambda p, j, i: (p, 0, j))],
        out_specs=[pl.BlockSpec((s, LANES), lambda p, j, i: (0, p)), kblk, kblk,
                   pl.BlockSpec((None, 2, tk), lambda p, j, i: (p, 0, j))],
        out_shape=[jax.ShapeDtypeStruct((s, MIX_W), F32), jax.ShapeDtypeStruct((s, MIX_W), F32),
                   jax.ShapeDtypeStruct((s, MIX_W), F32), jax.ShapeDtypeStruct((npair, 2, s), F32)],
        compiler_params=_cp("arbitrary", "arbitrary", "arbitrary"))(hb, hb, hb, dcat, dsum, lse, c_rep, c_t3)


def _foxt_logits(q, k, cq_row, ck_rep, mask, hmask):
    s = _dot_nt(_sel(hmask, k), q) + (cq_row - _wide(ck_rep, q.shape[0]))
    if mask is not None:
        s = jnp.where(mask, s, NEG)
    return s


def _causal_t(qi, kj, tq, tk):
    return (kj * tk + lax.broadcasted_iota(jnp.int32, (tk, tq), 0)
            <= qi * tq + lax.broadcasted_iota(jnp.int32, (tk, tq), 1))


def _fox_tiles(s):
    tq = _rows(s, 1024)
    return tq, tq // 2, s // tq


def _count_ge(t, bounds):
    return sum([(t >= b).astype(jnp.int32) for b in bounds], jnp.int32(0))


def _sweep_q_major(t, nq):
    qi = _count_ge(t, [r * (r + 1) for r in range(1, nq)])
    return qi, t - qi * (qi + 1)


def _sweep_k_major(t, nq):
    counts = [nq - j // 2 for j in range(2 * nq)]
    offs = [sum(counts[:j]) for j in range(1, 2 * nq)]
    kj = _count_ge(t, offs)
    start = sum([jnp.where(t >= o, c, 0) for o, c in zip(offs, counts)], jnp.int32(0))
    qi = kj // 2 + (t - start)
    return kj, qi, t == start, qi == nq - 1


def _foxt_fwd(hb, c_rep, c_t3, name):
    s = hb.shape[0]
    npair = MIX_W // LANES
    tq, tk, nq = _fox_tiles(s)

    def body(q_ref, k_ref, v_ref, cq_ref, ck_ref, o_ref, l_ref, m_s, l_s, acc):
        qi, kj = _sweep_q_major(pl.program_id(1), nq)
        hm = _head_masks()

        @pl.when(kj == 0)
        def _():
            m_s[...] = jnp.full_like(m_s, NEG)
            l_s[...] = jnp.zeros_like(l_s)
            acc[...] = jnp.zeros_like(acc)

        def step(mask):
            q, k = q_ref[...] * SCALE, k_ref[...]
            vt = jnp.transpose(v_ref[...])
            cq = cq_ref[...]
            for h in range(2):
                st = _foxt_logits(q, k, cq[h:h + 1, :], ck_ref[h], mask, hm[h])
                m_old = m_s[h]
                m_new = jnp.maximum(m_old, jnp.max(st, axis=0, keepdims=True))
                pt = jnp.exp(st - m_new)
                corr = jnp.exp(m_old - m_new)
                l_s[h] = l_s[h] * corr + jnp.sum(pt, axis=0, keepdims=True)
                acc[h] = acc[h] * corr + _dot(vt[h * HEAD_DIM:(h + 1) * HEAD_DIM, :], pt.astype(BF))
                m_s[h] = m_new

        @pl.when(kj < 2 * qi)
        def _():
            step(None)

        @pl.when(kj >= 2 * qi)
        def _():
            step(_causal_t(qi, kj, tq, tk))

        @pl.when(kj == 2 * qi + 1)
        def _():
            outs = []
            for h in range(2):
                outs.append(acc[h] / l_s[h])
                l_ref[h:h + 1, :] = m_s[h] + jnp.log(l_s[h])
            o_ref[...] = jnp.transpose(jnp.concatenate(outs, axis=0)).astype(BF)

    def q_map(p, t):
        return (_sweep_q_major(t, nq)[0], p)

    def kv_map(off):
        return lambda p, t: (_sweep_q_major(t, nq)[1], off + p)

    blk = pl.BlockSpec((tq, LANES), q_map)
    row = pl.BlockSpec((None, 2, tq), lambda p, t: (p, 0, _sweep_q_major(t, nq)[0]))
    return pl.pallas_call(
        body, name=name, grid=(npair, nq * (nq + 1)),
        in_specs=[blk, pl.BlockSpec((tk, LANES), kv_map(npair)), pl.BlockSpec((tk, LANES), kv_map(2 * npair)), row,
                  pl.BlockSpec((2, tk, LANES), lambda p, t: (p, _sweep_q_major(t, nq)[1], 0))],
        out_specs=[blk, row],
        out_shape=[jax.ShapeDtypeStruct((s, MIX_W), BF), jax.ShapeDtypeStruct((npair, 2, s), F32)],
        scratch_shapes=[pltpu.VMEM((2, 1, tq), F32), pltpu.VMEM((2, 1, tq), F32),
                        pltpu.VMEM((2, HEAD_DIM, tq), F32)],
        compiler_params=_cp("parallel", "arbitrary"))(hb, hb, hb, c_t3, c_rep)


def _foxt_dsum(hb, dcat, lse, c_rep, c_t3, name):
    s = hb.shape[0]
    npair = MIX_W // LANES
    tq, tk, nq = _fox_tiles(s)

    def body(q_ref, k_ref, v_ref, do_ref, l_ref, cq_ref, ck_ref, d_ref, acc):
        qi, kj = _sweep_q_major(pl.program_id(1), nq)
        hm = _head_masks()

        @pl.when(kj == 0)
        def _():
            acc[...] = jnp.zeros_like(acc)

        def step(mask):
            q, k, v, do = q_ref[...] * SCALE, k_ref[...], v_ref[...], do_ref[...]
            cq, lse_rows = cq_ref[...], l_ref[...]
            for h in range(2):
                pt = jnp.exp(_foxt_logits(q, k, cq[h:h + 1, :], ck_ref[h], mask, hm[h]) - lse_rows[h:h + 1, :])
                acc[h] += jnp.sum(pt * _dot_nt(_sel(hm[h], v), do), axis=0, keepdims=True)

        @pl.when(kj < 2 * qi)
        def _():
            step(None)

        @pl.when(kj >= 2 * qi)
        def _():
            step(_causal_t(qi, kj, tq, tk))

        @pl.when(kj == 2 * qi + 1)
        def _():
            for h in range(2):
                d_ref[h:h + 1, :] = acc[h]

    def q_map(p, t):
        return (_sweep_q_major(t, nq)[0], p)

    def kv_map(off):
        return lambda p, t: (_sweep_q_major(t, nq)[1], off + p)

    blk = pl.BlockSpec((tq, LANES), q_map)
    row = pl.BlockSpec((None, 2, tq), lambda p, t: (p, 0, _sweep_q_major(t, nq)[0]))
    return pl.pallas_call(
        body, name=name, grid=(npair, nq * (nq + 1)),
        in_specs=[blk, pl.BlockSpec((tk, LANES), kv_map(npair)), pl.BlockSpec((tk, LANES), kv_map(2 * npair)),
                  blk, row, row, pl.BlockSpec((2, tk, LANES), lambda p, t: (p, _sweep_q_major(t, nq)[1], 0))],
        out_specs=row, out_shape=jax.ShapeDtypeStruct((npair, 2, s), F32),
        scratch_shapes=[pltpu.VMEM((2, 1, tq), F32)],
        compiler_params=_cp("parallel", "arbitrary"))(hb, hb, hb, dcat, lse, c_t3, c_rep)


def _foxt_bwd(hb, dcat, dsum, lse, c_rep, c_t3, name):
    s = hb.shape[0]
    npair = MIX_W // LANES
    tq, tk, nq = _fox_tiles(s)

    def body(q_ref, k_ref, v_ref, do_ref, d_ref, l_ref, cq_ref, ck_ref, dq_ref, dk_ref, dv_ref, dc_ref, dc_s):
        t = pl.program_id(1)
        kj, qi, first, last = _sweep_k_major(t, nq)
        hm = _head_masks()

        @pl.when(first)
        def _():
            dk_ref[...] = jnp.zeros_like(dk_ref)
            dv_ref[...] = jnp.zeros_like(dv_ref)
            dc_s[...] = jnp.zeros_like(dc_s)

        @pl.when(t == 0)
        def _():
            dq_ref[...] = jnp.zeros_like(dq_ref)

        def step(mask):
            q, k, v, do = q_ref[...] * SCALE, k_ref[...], v_ref[...], do_ref[...]
            qt, kt, dot = jnp.transpose(q), jnp.transpose(k), jnp.transpose(do)
            cq, lse_rows, d_rows = cq_ref[...], l_ref[...], d_ref[...]
            dqs, dks, dvs = [], [], []
            for h in range(2):
                rows = slice(h * HEAD_DIM, (h + 1) * HEAD_DIM)
                pt = jnp.exp(_foxt_logits(q, k, cq[h:h + 1, :], ck_ref[h], mask, hm[h]) - lse_rows[h:h + 1, :])
                dst = pt * (_dot_nt(_sel(hm[h], v), do) - d_rows[h:h + 1, :])
                dsb = dst.astype(BF)
                dqs.append(_dot(kt[rows, :], dsb))
                dks.append(_dot_nt(qt[rows, :], dsb))
                dvs.append(_dot_nt(dot[rows, :], pt.astype(BF)))
                dc_s[h] += _fold(dst)
            cols = pl.ds(pl.multiple_of(qi * tq, tq), tq)
            dq_ref[:, cols] += SCALE * jnp.concatenate(dqs, axis=0)
            dk_ref[...] += jnp.concatenate(dks, axis=0)
            dv_ref[...] += jnp.concatenate(dvs, axis=0)

        @pl.when(kj < 2 * qi)
        def _():
            step(None)

        @pl.when(kj >= 2 * qi)
        def _():
            step(_causal_t(qi, kj, tq, tk))

        @pl.when(last)
        def _():
            for h in range(2):
                dc_ref[h:h + 1, :] = -jnp.sum(jnp.transpose(dc_s[h]), axis=0, keepdims=True)

    def kj_of(t):
        return _sweep_k_major(t, nq)[0]

    def qi_of(t):
        return _sweep_k_major(t, nq)[1]

    qblk = pl.BlockSpec((tq, LANES), lambda p, t: (qi_of(t), p))
    row = pl.BlockSpec((None, 2, tq), lambda p, t: (p, 0, qi_of(t)))
    kblk = pl.BlockSpec((LANES, tk), lambda p, t: (p, kj_of(t)))
    rep = pl.BlockSpec((2, tk, LANES), lambda p, t: (p, kj_of(t), 0))
    return pl.pallas_call(
        body, name=name, grid=(npair, nq * (nq + 1)),
        in_specs=[qblk,
                  pl.BlockSpec((tk, LANES), lambda p, t: (kj_of(t), npair + p)),
                  pl.BlockSpec((tk, LANES), lambda p, t: (kj_of(t), 2 * npair + p)),
                  qblk, row, row, row, rep],
        out_specs=[pl.BlockSpec((LANES, s), lambda p, t: (p, 0)), kblk, kblk,
                   pl.BlockSpec((None, 2, tk), lambda p, t: (p, 0, kj_of(t)))],
        out_shape=[jax.ShapeDtypeStruct((MIX_W, s), F32), jax.ShapeDtypeStruct((MIX_W, s), F32),
                   jax.ShapeDtypeStruct((MIX_W, s), F32), jax.ShapeDtypeStruct((npair, 2, s), F32)],
        scratch_shapes=[pltpu.VMEM((2, tk, LANES), F32)],
        compiler_params=_cp("arbitrary", "arbitrary"))(hb, hb, hb, dcat, dsum, lse, c_t3, c_rep)


def _loss_head(y, target, name):
    s, d = y.shape
    ts = _rows(s)

    def body(y_ref, t_ref, dy_ref, l_ref):
        i = pl.program_id(0)
        e = y_ref[...] - t_ref[...]
        dy_ref[...] = e * (1.0 / d)

        @pl.when(i == 0)
        def _():
            l_ref[...] = jnp.zeros_like(l_ref)

        part = jnp.sum(jnp.sum(e * e, axis=1, keepdims=True), axis=0, keepdims=True)
        l_ref[...] += part * (0.5 / d)

    row = pl.BlockSpec((ts, d), lambda i: (i, 0))
    return pl.pallas_call(
        body, name=name, grid=(s // ts,), in_specs=[row, row],
        out_specs=[row, pl.BlockSpec((1, 1), lambda i: (0, 0))],
        out_shape=[jax.ShapeDtypeStruct((s, d), F32), jax.ShapeDtypeStruct((1, 1), F32)],
        compiler_params=_cp("arbitrary"))(y, target)


def _adam_rows(r, c):
    cap = max(8, (1 << 20) // (4 * c))
    if r <= cap:
        return r
    best = None
    for t in range(8, cap + 1, 8):
        if r % t == 0:
            best = t
    return best if best is not None else r


def _reduce_adamw(contribs, w, m, v, name):
    nl = len(contribs)
    nd, r, c = contribs[0].shape
    tr = _adam_rows(r, c)
    bc1 = 1.0 - ADAM_B1 ** ADAM_STEP
    bc2 = 1.0 - ADAM_B2 ** ADAM_STEP

    def body(*refs):
        c_refs = refs[:nl]
        w_ref, m_ref, v_ref, g_ref, d_ref, nm_ref, nv_ref = refs[nl:]
        l = pl.program_id(0)
        for li in range(nl):
            @pl.when(l == li)
            def _(c_ref=c_refs[li]):
                g = c_ref[0].astype(F32)
                for k in range(1, nd):
                    g = g + c_ref[k].astype(F32)
                nm = ADAM_B1 * m_ref[...] + (1.0 - ADAM_B1) * g
                nv = ADAM_B2 * v_ref[...] + (1.0 - ADAM_B2) * (g * g)
                g_ref[...] = g
                nm_ref[...] = nm
                nv_ref[...] = nv
                d_ref[...] = -ADAM_LR * ((nm / bc1) / (jnp.sqrt(nv / bc2) + ADAM_EPS) + ADAM_WD * w_ref[...])

    def c_spec(li):
        return pl.BlockSpec((nd, tr, c), lambda l, i: (0, jnp.where(l == li, i, 0), 0))

    blk = pl.BlockSpec((None, tr, c), lambda l, i: (l, i, 0))
    out = jax.ShapeDtypeStruct((nl, r, c), F32)
    return pl.pallas_call(
        body, name=name, grid=(nl, r // tr),
        in_specs=[c_spec(li) for li in range(nl)] + [blk, blk, blk],
        out_specs=[blk, blk, blk, blk], out_shape=[out, out, out, out],
        compiler_params=_cp("arbitrary", "arbitrary"))(*contribs, w, m, v)


def _mesh_pos():
    return lax.axis_index("x"), lax.axis_index("y"), lax.axis_index("c")


def _peer(pos, k):
    x, y, c = pos
    return (1 - x if k & 4 else x, 1 - y if k & 2 else y, 1 - c if k & 1 else c)


def _linear(pos):
    return 4 * pos[0] + 2 * pos[1] + pos[2]


def _xfer_copies(srcs, lands, send_sems, recv_sems, local_sems, gather):
    pos = _mesh_pos()
    me = _linear(pos)
    local, remote = [], []
    for i, (src, land) in enumerate(zip(srcs, lands)):
        local.append(pltpu.make_async_copy(src if gather else src.at[me], land.at[me], local_sems.at[i]))
        for k in range(1, N_DEV):
            peer = _peer(pos, k)
            remote.append(pltpu.make_async_remote_copy(
                src_ref=src if gather else src.at[_linear(peer)], dst_ref=land.at[me],
                send_sem=send_sems.at[i * (N_DEV - 1) + k - 1], recv_sem=recv_sems.at[i * (N_DEV - 1) + k - 1],
                device_id=peer, device_id_type=MESH_ID))
    return local, remote


_HBM = pl.BlockSpec(memory_space=pltpu.HBM)
_SEM = pl.BlockSpec(memory_space=pltpu.SEMAPHORE)
_EFFECT = pltpu.SideEffectType.DATAFLOW_SIDE_EFFECTING


def _xfer_start(srcs, gather, name, after=()):
    n = len(srcs)
    na = len(after)
    lands = [lax.empty(((N_DEV,) + a.shape) if gather else a.shape, a.dtype) for a in srcs]

    def body(*refs):
        src, land = refs[:n], refs[n:2 * n]
        send_sems, recv_sems, local_sems = refs[2 * n + na:2 * n + na + 3]
        local, remote = _xfer_copies(src, land, send_sems, recv_sems, local_sems, gather)
        for cp in local + remote:
            cp.start()
        refs[-1][...] = jnp.zeros_like(refs[-1])

    nsem = n * (N_DEV - 1)
    out = pl.pallas_call(
        body, name=name,
        out_shape=(pltpu.SemaphoreType.DMA((nsem,)), pltpu.SemaphoreType.DMA((nsem,)), pltpu.SemaphoreType.DMA((n,)),
                   *[pltpu.HBM(a.shape, a.dtype) for a in srcs], *[pltpu.HBM(a.shape, a.dtype) for a in lands],
                   jax.ShapeDtypeStruct((8, LANES), F32)),
        in_specs=[_HBM] * (2 * n) + [pl.BlockSpec(memory_space=pl.ANY)] * na,
        out_specs=(_SEM, _SEM, _SEM, *[_HBM] * (2 * n), pl.BlockSpec(memory_space=pltpu.VMEM)),
        input_output_aliases={i: 3 + i for i in range(2 * n)},
        compiler_params=pltpu.CompilerParams(has_side_effects=_EFFECT))(
            *[pltpu.with_memory_space_constraint(a, pltpu.HBM) for a in srcs],
            *[pltpu.with_memory_space_constraint(a, pltpu.HBM) for a in lands], *after)
    return out[:3], list(out[3:3 + n]), list(out[3 + n:3 + 2 * n]), out[-1]


def _started(handle):
    return handle[3]


def _xfer_wait(handle, after, gather, name):
    sems, srcs, lands, _ = handle
    n = len(srcs)

    def body(*refs):
        src, land = refs[:n], refs[n:2 * n]
        send_sems, recv_sems, local_sems = refs[2 * n:2 * n + 3]
        local, remote = _xfer_copies(src, land, send_sems, recv_sems, local_sems, gather)
        for cp in local:
            cp.wait()
        for cp in remote:
            cp.wait_send()
            cp.wait_recv()

    out = pl.pallas_call(
        body, name=name,
        out_shape=(*[pltpu.HBM(a.shape, a.dtype) for a in srcs], *[pltpu.HBM(a.shape, a.dtype) for a in lands]),
        in_specs=[_HBM] * (2 * n) + [_SEM] * 3 + [pl.BlockSpec(memory_space=pl.ANY)] * len(after),
        out_specs=tuple([_HBM] * (2 * n)), input_output_aliases={i: i for i in range(2 * n)},
        compiler_params=pltpu.CompilerParams(has_side_effects=_EFFECT))(*srcs, *lands, *sems, *after)
    return list(out[n:])


def _cols_full(g):
    nd, r, c = g.shape
    return jnp.transpose(g, (1, 0, 2)).reshape(r, nd * c)


def _cols_split(full):
    r, n = full.shape
    return jnp.transpose(full.reshape(r, N_DEV, n // N_DEV), (1, 0, 2))


def _pack_b_in(w):
    qkv = 3 * MIX_W
    pad = jnp.zeros((w.shape[0], B_IN_PAD - w.shape[1]), w.dtype)
    return jnp.concatenate([w[:, :qkv], w[:, qkv + N_MIX_HEADS:], w[:, qkv:qkv + N_MIX_HEADS], pad], axis=1)


def _unpack_b_in(w):
    qkv = 3 * MIX_W
    return jnp.concatenate([w[:, :qkv], w[:, qkv + MEM_W:qkv + MEM_W + N_MIX_HEADS], w[:, qkv:qkv + MEM_W]], axis=1)


def _to_classes(t, g):
    r = 4 ** g
    s, w = t.shape
    return jnp.transpose(t.reshape(s // r, r, w), (1, 0, 2)).reshape(s, w)


def _from_classes(t, g):
    r = 4 ** g
    s, w = t.shape
    return jnp.transpose(t.reshape(r, s // r, w), (1, 0, 2)).reshape(s, w)


def _group_stack(t):
    return jnp.stack([_to_classes(t[:, g * GROUP_W:(g + 1) * GROUP_W], g) for g in range(N_GROUPS)])


def _group_unstack(t3):
    return jnp.concatenate([_from_classes(t3[g], g) for g in range(N_GROUPS)], axis=1)


def _same_stack(t):
    return jnp.stack([_to_classes(t, g) for g in range(N_GROUPS)])


def _same_unstack(t3):
    return jnp.stack([_from_classes(t3[g], g) for g in range(N_GROUPS)])


def _ffn_forward(x, xb, wgu, get_rest, tag, fused=True):
    if fused:
        wd4, gain, bias = get_rest(x)
        y, yb, gu, a, xh, rstd = _ffn_fwd_main(x, xb, wgu, wd4, gain, bias, f"{tag}_fwd_main")
    else:
        gu, a = _ffn_up(xb, wgu, f"{tag}_up")
        wd4, gain, bias = get_rest(a)
        y, yb, xh, rstd = _mm_res_ln(a, wd4, x, gain, bias, 0.5, f"{tag}_down_ln")
    return y, yb, (xb, gu, a, xh, rstd), wd4


def _ffn_backward(dy, saved, wgu, wd4, gain, tag, after=(), send=None):
    xb, gu, a, xh, rstd = saved
    s = xb.shape[0]
    nd, c, d = wgu.shape
    dx, dzb, dh, dgain, dbias = _ffn_bwd_main(dy, xh, rstd, gain, wd4, wgu, gu, f"{tag}_bwd_main", after,
                                               with_dx=send is None)
    dh = dh.reshape(nd, s, c)
    dwd = _mm_tn(a, dzb[None], f"{tag}_dwd").reshape(nd, wd4.shape[1] // 2, d)
    if send is not None:
        send("down", dwd, dgain, dbias)
    dwgu = _mm_tn(dh, xb[None], f"{tag}_dwgu")
    if send is not None:
        sent = send("gate_up", dwgu)
        dx = _mm_nt(dh, wgu, f"{tag}_dx", res=dx, w_rows_out=False, after=sent)
    return dx, dwgu, dwd, dgain, dbias


def _mixer_a_forward(x, xb, memb, w_in, w_kv, w_out, gain, bias, tabs):
    hb = _proj_rope(xb, w_in, tabs, 2 * MIX_W // LANES, "a_in", True)
    groups = [_band_fwd(hb, g, f"a_band_fwd{g}") for g in range(N_GROUPS)]
    oa, lt = _band_combine([o for o, _ in groups], [l for _, l in groups], "a_combine")
    kv = _mm_nn(memb, w_kv, BF, "a_mem_kv")
    om, lm = _mem_fwd(hb, 3 * MIX_W // LANES, kv, "a_mem_fwd")
    cat = jnp.concatenate([oa, om], axis=1)
    y, yb, xh, rstd = _mm_res_ln(cat[None], w_out[None], x, gain, bias, 1.0, "a_out_ln")
    return y, yb, (xb, hb, oa, lt, kv, lm, cat, xh, rstd)


def _mixer_a_backward(dy, saved, memb, w_in, w_kv, w_out, gain, tabs_neg, after=()):
    xb, hb, oa, lt, kv, lm, cat, xh, rstd = saved
    dz, dzb, dgain, dbias = _ln_bwd(dy, xh, rstd, gain, 1.0, "a_ln_bwd", after)
    dcat = _mm_nt(dzb[None], w_out[None], "a_dcat", out_dtype=BF)
    dw_out = _mm_tn(cat[None], dzb[None], "a_dwout")[0]
    dqm, dkm, dvm = _mem_bwd(hb, 3 * MIX_W // LANES, kv, dcat, cat, GROUP_W // LANES, lm, "a_mem_bwd")
    dkv = jnp.concatenate([dkm, dvm], axis=1).astype(BF)
    dw_kv = _mm_tn(memb[None], dkv[None], "a_dwkv")[0]
    grads = [_band_bwd(hb, dcat, oa, lt, g, f"a_band_bwd{g}") for g in range(N_GROUPS)]
    dhb = _rope_cast([grads[g][i] for i in range(3) for g in range(N_GROUPS)] + [dqm], tabs_neg,
                     2 * MIX_W // LANES, "a_rope_bwd")
    dw_in = _mm_tn(dhb[None], xb[None], "a_dwin")[0]
    dx = _mm_nt(dhb[None], w_in[None], "a_dx", res=dz, w_rows_out=False)
    return dx, dw_in, dw_kv, dw_out, dgain, dbias


def _pad_rows(t, rows):
    return jnp.concatenate([t, jnp.zeros((rows - t.shape[0], t.shape[1]), t.dtype)], axis=0)


def _pad_cols(t, cols):
    return jnp.concatenate([t, jnp.zeros((t.shape[0], cols - t.shape[1]), t.dtype)], axis=1)


def _mixer_b_forward(x, xb, memb, w_in, fbias, w_kv, w_out, gain, bias, tabs):
    s = x.shape[0]
    hb, f = _proj_rope(xb, w_in, tabs, 0, "b_in", False, tail_block=(3 * MIX_W + MEM_W) // LANES)
    f_t = _pad_rows(jnp.transpose(f[:, :N_MIX_HEADS]), 16)
    bias16 = _pad_rows(jnp.transpose(fbias), 16)
    c_t = _gate_fwd(f_t, bias16, "b_gate_fwd")
    c_t3 = c_t[:N_MIX_HEADS].reshape(N_MIX_HEADS // 2, 2, s)
    c_rep = jnp.broadcast_to(c_t[:N_MIX_HEADS, :, None], (N_MIX_HEADS, s, LANES))
    ob, lb = _foxt_fwd(hb, c_rep, c_t3, "b_fox_fwd")
    kv = _mm_nn(memb, w_kv, BF, "b_mem_kv")
    om, lm = _mem_fwd(hb, 3 * MIX_W // LANES, kv, "b_mem_fwd")
    cat = jnp.concatenate([ob, om], axis=1)
    y, yb, xh, rstd = _mm_res_ln(cat[None], w_out[None], x, gain, bias, 1.0, "b_out_ln")
    return y, yb, (xb, hb, f_t, bias16, c_rep, c_t3, lb, kv, lm, cat, xh, rstd)


def _mixer_b_backward(dy, saved, memb, w_in, w_kv, w_out, gain, tabs, after=()):
    xb, hb, f_t, bias16, c_rep, c_t3, lb, kv, lm, cat, xh, rstd = saved
    s = xb.shape[0]
    dz, dzb, dgain, dbias = _ln_bwd(dy, xh, rstd, gain, 1.0, "b_ln_bwd", after)
    dcat = _mm_nt(dzb[None], w_out[None], "b_dcat", out_dtype=BF)
    dw_out = _mm_tn(cat[None], dzb[None], "b_dwout")[0]
    dqm, dkm, dvm = _mem_bwd(hb, 3 * MIX_W // LANES, kv, dcat, cat, MIX_W // LANES, lm, "b_mem_bwd")
    dkv = jnp.concatenate([dkm, dvm], axis=1).astype(BF)
    dw_kv = _mm_tn(memb[None], dkv[None], "b_dwkv")[0]
    dsum = _foxt_dsum(hb, dcat, lb, c_rep, c_t3, "b_fox_dsum")
    dq, dk, dv, dc3 = _foxt_bwd(hb, dcat, dsum, lb, c_rep, c_t3, "b_fox_bwd")
    df_t, dfb = _gate_bwd(_pad_rows(dc3.reshape(N_MIX_HEADS, s), 16), f_t, bias16, "b_gate_bwd")
    df = _pad_cols(jnp.transpose(df_t[:N_MIX_HEADS]), B_IN_PAD - 3 * MIX_W - MEM_W)
    dhb = _rope_cast([dq, dk, dv, dqm, df], tabs, 0, "b_cast_bwd", transposed=(0, 1, 2))
    dw_in = _mm_tn(xb[None], dhb[None], "b_dwin")[0]
    dx = _mm_nt(dhb[None], w_in[None], "b_dx", res=dz)
    return dx, dw_in, jnp.transpose(dfb[:N_MIX_HEADS]), dw_kv, dw_out, dgain, dbias


def _stored(t, name):
    return jnp.transpose(t, (0, 2, 1)) if name in ROWS_OUT else t


GATHER_GROUPS = (
    (("ffn1_w_gate_up", 0),),
    (("ffn1_w_down", 0), ("ln_gain", None), ("ln_bias", None)),
    (("a_w_in", 0), ("a_w_out", 0), ("mem_w_kv", 0)),
    (("ffn2_w_gate_up", 0), ("ffn2_w_down", 0)),
    (("ffn1_w_gate_up", 1), ("ffn1_w_down", 1)),
    (("b_w_in", 0), ("b_w_out", 0), ("mem_w_kv", 1)),
    (("ffn2_w_gate_up", 1), ("ffn2_w_down", 1)),
)


def _group_shards(group, params):
    return [t if n in F32_COMM else _stored(t, n)[l].astype(BF) for (n, l), t in zip(group, params)]


def _weight_groups(w):
    return [_group_shards(grp, [w[n] for n, _ in grp]) for grp in GATHER_GROUPS]


def _local_step(x, mem, target, fbias, get_w, put_g):
    s, d = x.shape
    tabs = _rope_tables(s, 1.0)
    tabs_neg = _rope_tables(s, -1.0)
    memb = mem.astype(BF)
    saved, wl = [], []
    cur, curb = x, x.astype(BF)
    ln = []

    def down4(t):
        return t.reshape(N_DEV // 2, -1, d)

    for i in range(DEPTH):
        if i == 0:
            def first_rest(a):
                g = get_w(1, a)
                ln.extend(jnp.transpose(t, (1, 2, 0, 3)).reshape(DEPTH, 3, 1, d) for t in g[1:3])
                return down4(g[0]), ln[0][0, 0], ln[1][0, 0]

            wgu = get_w(0, cur)[0]
            cur, curb, s1, wd = _ffn_forward(cur, curb, wgu, first_rest, "l0_ffn1", fused=False)
        else:
            g = get_w(3 * i + 1, cur)
            wgu = g[0]
            cur, curb, s1, wd = _ffn_forward(cur, curb, wgu, lambda a, g=g: (down4(g[1]), ln[0][i, 0], ln[1][i, 0]),
                                             f"l{i}_ffn1")
        w1 = (wgu, wd)
        ln_g, ln_b = ln
        g = get_w(3 * i + 2, cur)
        if i == 0:
            wm = (g[0].reshape(-1, d), g[2].reshape(d, -1), _cols_full(g[1]))
            cur, curb, s2 = _mixer_a_forward(cur, curb, memb, wm[0], wm[1], wm[2], ln_g[i, 1], ln_b[i, 1], tabs)
        else:
            wm = (_pack_b_in(g[0].reshape(d, -1)), g[2].reshape(d, -1), g[1].reshape(d, -1))
            cur, curb, s2 = _mixer_b_forward(cur, curb, memb, wm[0], fbias, wm[1], wm[2], ln_g[i, 1], ln_b[i, 1],
                                             tabs)
        g = get_w(3 * i + 3, cur)
        cur, curb, s3, wd = _ffn_forward(cur, curb, g[0], lambda a, g=g: (down4(g[1]), ln_g[i, 2], ln_b[i, 2]),
                                         f"l{i}_ffn2")
        w3 = (g[0], wd)
        saved.append((s1, s2, s3))
        wl.append((w1, wm, w3))

    dy, loss = _loss_head(cur, target, "loss_head")

    dgs = [[None] * 3 for _ in range(DEPTH)]
    dbs = [[None] * 3 for _ in range(DEPTH)]
    sent = ()
    for i in reversed(range(DEPTH)):
        s1, s2, s3 = saved[i]
        w1, wm, w3 = wl[i]
        dy, dgu, dd, dgs[i][2], dbs[i][2] = _ffn_backward(dy, s3, w3[0], w3[1], ln_g[i, 2], f"l{i}_ffn2", sent)
        sent = put_g(3 * i + 2, [dgu, dd])
        if i == 0:
            dy, dw_in, dw_kv, dw_out, dgs[i][1], dbs[i][1] = _mixer_a_backward(
                dy, s2, memb, wm[0], wm[1], wm[2], ln_g[i, 1], tabs_neg, sent)
            sent = put_g(1, [dw_in.reshape(N_DEV, -1, d), _cols_split(dw_out),
                             dw_kv.reshape(N_DEV, d // N_DEV, -1)])
        else:
            dy, dw_in, dfb, dw_kv, dw_out, dgs[i][1], dbs[i][1] = _mixer_b_backward(
                dy, s2, memb, wm[0], wm[1], wm[2], ln_g[i, 1], tabs, sent)
            sent = put_g(4, [_unpack_b_in(dw_in).reshape(N_DEV, d // N_DEV, -1),
                             dw_out.reshape(N_DEV, d // N_DEV, -1), dw_kv.reshape(N_DEV, d // N_DEV, -1),
                             jnp.broadcast_to(dfb[None], (N_DEV,) + dfb.shape)])
        if i == 0:
            def send_last(kind, dw, dgain=None, dbias=None):
                if kind == "gate_up":
                    return put_g(6, [dw])
                dgs[0][0], dbs[0][0] = dgain, dbias
                ln_pieces = []
                for parts in (dgs, dbs):
                    t = jnp.concatenate([parts[a][b] for a in range(DEPTH) for b in range(3)], axis=0)
                    ln_pieces.append(jnp.transpose(t.reshape(DEPTH * 3, N_DEV, d // N_DEV), (1, 0, 2)))
                return put_g(0, [dw] + ln_pieces)

            dy = _ffn_backward(dy, s1, w1[0], w1[1], ln_g[i, 0], "l0_ffn1", sent, send_last)[0]
        else:
            dy, dgu, dd, dgs[i][0], dbs[i][0] = _ffn_backward(dy, s1, w1[0], w1[1], ln_g[i, 0], f"l{i}_ffn1", sent)
            sent = put_g(3, [dgu, dd])
    return loss, dy


WEIGHTS = ("ffn1_w_gate_up", "ffn1_w_down", "ffn2_w_gate_up", "ffn2_w_down", "ln_gain", "ln_bias", "mem_w_kv",
           "a_w_in", "a_w_out", "b_w_in", "b_forget_bias", "b_w_out")
F32_COMM = ("ln_gain", "ln_bias", "b_forget_bias")
ROWS_OUT = ("ffn1_w_gate_up", "ffn2_w_gate_up", "a_w_in")
GRAD_SLOTS = {
    "ffn1_w_gate_up": [(6, 0), (3, 0)], "ffn1_w_down": [(0, 0), (3, 1)],
    "ffn2_w_gate_up": [(2, 0), (5, 0)], "ffn2_w_down": [(2, 1), (5, 1)],
    "ln_gain": [(0, 1)], "ln_bias": [(0, 2)], "mem_w_kv": [(1, 2), (4, 2)],
    "a_w_in": [(1, 0)], "a_w_out": [(1, 1)], "b_w_in": [(4, 0)], "b_forget_bias": [(4, 3)], "b_w_out": [(4, 1)],
}


def kernel(x, mem, ffn1_w_gate_up, ffn1_w_down, ffn2_w_gate_up, ffn2_w_down, ln_gain, ln_bias, mem_w_kv, a_w_in, a_w_out, b_w_in, b_forget_bias, b_w_out, loss_target, m_ffn1_w_gate_up, m_ffn1_w_down, m_ffn2_w_gate_up, m_ffn2_w_down, m_ln_gain, m_ln_bias, m_mem_w_kv, m_a_w_in, m_a_w_out, m_b_w_in, m_b_forget_bias, m_b_w_out, v_ffn1_w_gate_up, v_ffn1_w_down, v_ffn2_w_gate_up, v_ffn2_w_down, v_ln_gain, v_ln_bias, v_mem_w_kv, v_a_w_in, v_a_w_out, v_b_w_in, v_b_forget_bias, v_b_w_out):
    w = dict(zip(WEIGHTS, (ffn1_w_gate_up, ffn1_w_down, ffn2_w_gate_up, ffn2_w_down, ln_gain, ln_bias, mem_w_kv,
                           a_w_in, a_w_out, b_w_in, b_forget_bias, b_w_out)))
    m = dict(zip(WEIGHTS, (m_ffn1_w_gate_up, m_ffn1_w_down, m_ffn2_w_gate_up, m_ffn2_w_down, m_ln_gain, m_ln_bias,
                           m_mem_w_kv, m_a_w_in, m_a_w_out, m_b_w_in, m_b_forget_bias, m_b_w_out)))
    v = dict(zip(WEIGHTS, (v_ffn1_w_gate_up, v_ffn1_w_down, v_ffn2_w_gate_up, v_ffn2_w_down, v_ln_gain, v_ln_bias,
                           v_mem_w_kv, v_a_w_in, v_a_w_out, v_b_w_in, v_b_forget_bias, v_b_w_out)))

    gathers = []
    for k, grp in enumerate(GATHER_GROUPS):
        params, behind = [w[n] for n, _ in grp], [_started(h) for h in gathers[-1:]]
        if behind:
            params, behind = lax.optimization_barrier((params, behind))
        gathers.append(_xfer_start(_group_shards(grp, params), True, f"gather{k}_start", behind))
    exchanges = {}

    def get_w(k, after):
        behind = [after] + ([_started(h) for h in gathers] if k == 0 else [])
        return _xfer_wait(gathers[k], behind, True, f"gather{k}_wait")

    def put_g(k, pieces):
        behind = [_started(exchanges[0])] if k == 6 else []
        exchanges[k] = _xfer_start(pieces, False, f"grads{k}_start", behind)
        return (_started(exchanges[k]),)

    loss, grad_x = _local_step(x[0], mem[0], loss_target[0], b_forget_bias, get_w, put_g)
    loss = lax.psum(loss[0, 0], ("x", "y", "c"))

    outs, landed = {}, {}

    def adamw(names):
        for n in names:
            contribs = [landed[g][j] for g, j in GRAD_SLOTS[n]]
            view = (len(contribs),) + contribs[0].shape[1:]
            shape = _stored(w[n], n).shape
            res = _reduce_adamw(contribs, *[_stored(t[n], n).reshape(view) for t in (w, m, v)], f"adamw_{n}")
            outs[n] = [_stored(t.reshape(shape), n) for t in res]
        return [outs[n][3] for n in names]

    after = [grad_x]
    for k in (5, 4, 3, 2, 1):
        landed[k] = _xfer_wait(exchanges[k], after, False, f"grads{k}_wait")
        after = [landed[k][0]]
    done = adamw(("ffn2_w_gate_up", "ffn2_w_down", "mem_w_kv", "a_w_in", "a_w_out", "b_w_in", "b_forget_bias",
                  "b_w_out"))
    landed[0] = _xfer_wait(exchanges[0], done, False, "grads0_wait")
    done = adamw(("ffn1_w_down", "ln_gain", "ln_bias"))
    landed[6] = _xfer_wait(exchanges[6], done, False, "grads6_wait")
    adamw(("ffn1_w_gate_up",))
    return (loss, grad_x[None], *[outs[n][0] for n in WEIGHTS], *[outs[n][1] for n in WEIGHTS],
            *[outs[n][2] for n in WEIGHTS], *[outs[n][3] for n in WEIGHTS])
```

```python
import functools

import jax
import jax.numpy as jnp
from jax import lax
from jax.experimental import pallas as pl
from jax.experimental.pallas import tpu as pltpu

F32 = jnp.float32
BF = jnp.bfloat16
MESH_ID = pl.DeviceIdType.MESH

N_DEV = 8
DEPTH = 2
HEAD_DIM = 64
LANES = 128
N_MIX_HEADS = 12
N_MEM_HEADS = 4
MIX_W = N_MIX_HEADS * HEAD_DIM
MEM_W = N_MEM_HEADS * HEAD_DIM
N_GROUPS = 3
GROUP_W = MIX_W // N_GROUPS
BLOCK = 128
BAND_SUB = 4
ROT_HALF = 8
ROPE_THETA = 500000.0
ALPHA = (2 * DEPTH) ** 0.25
LN_EPS = 1e-5
SCALE = HEAD_DIM ** -0.5
NEG = -1e30
B_IN_PAD = 2688
ADAM_LR, ADAM_B1, ADAM_B2, ADAM_EPS, ADAM_WD, ADAM_STEP = 0.001, 0.9, 0.999, 1e-08, 0.01, 10
VMEM_LIMIT = 56 * 1024 * 1024


def _cp(*sem):
    return pltpu.CompilerParams(dimension_semantics=sem, vmem_limit_bytes=VMEM_LIMIT)


def _dot(a, b):
    return jnp.dot(a, b, preferred_element_type=F32)


def _dot_nt(a, b):
    return lax.dot_general(a, b, (((1,), (1,)), ((), ())), preferred_element_type=F32)


def _dot_tn(a, b):
    return lax.dot_general(a, b, (((0,), (0,)), ((), ())), preferred_element_type=F32)


def _sigmoid(x):
    return 1.0 / (1.0 + jnp.exp(-x))


def _tile(n, cap=1024):
    if n <= cap:
        return n
    best = LANES
    for t in range(LANES, cap + 1, LANES):
        if n % t == 0:
            best = t
    return best


def _rows(s, cap=512):
    return s if s <= cap else cap


def _mm_nn(a, b, out_dtype, name, b_rows_out=False):
    m, k = a.shape
    n = b.shape[0] if b_rows_out else b.shape[1]
    tm, tn = _rows(m), _tile(n)

    def body(a_ref, b_ref, o_ref):
        prod = _dot_nt(a_ref[...], b_ref[...]) if b_rows_out else _dot(a_ref[...], b_ref[...])
        o_ref[...] = prod.astype(o_ref.dtype)

    b_spec = (pl.BlockSpec((tn, k), lambda j, i: (j, 0)) if b_rows_out
              else pl.BlockSpec((k, tn), lambda j, i: (0, j)))
    return pl.pallas_call(
        body, name=name, grid=(n // tn, m // tm),
        in_specs=[pl.BlockSpec((tm, k), lambda j, i: (i, 0)), b_spec],
        out_specs=pl.BlockSpec((tm, tn), lambda j, i: (i, j)),
        out_shape=jax.ShapeDtypeStruct((m, n), out_dtype),
        compiler_params=_cp("parallel", "parallel"))(a, b)


def _resident(shape, index_map):
    return pl.BlockSpec(shape, index_map, pipeline_mode=pl.Buffered(1))


def _mm_tn(a, b, name, out_dtype=BF):
    na, s, m = a.shape
    nb, _, n = b.shape
    no = max(na, nb)
    tm, tn = _tile(m), _tile(n)

    def body(a_ref, b_ref, o_ref):
        o_ref[...] = _dot_tn(a_ref[...], b_ref[...]).astype(o_ref.dtype)

    def spec(nbatch, width, tile, index_map):
        fixed = nbatch == 1 and width == tile
        return _resident((None, s, tile), index_map) if fixed else pl.BlockSpec((None, s, tile), index_map)

    return pl.pallas_call(
        body, name=name, grid=(no, m // tm, n // tn),
        in_specs=[spec(na, m, tm, lambda j, r, c: (j if na > 1 else 0, 0, r)),
                  spec(nb, n, tn, lambda j, r, c: (j if nb > 1 else 0, 0, c))],
        out_specs=pl.BlockSpec((None, tm, tn), lambda j, r, c: (j, r, c)),
        out_shape=jax.ShapeDtypeStruct((no, m, n), out_dtype),
        compiler_params=_cp("parallel", "parallel", "parallel"))(a, b)


def _mm_nt(dh, w, name, res=None, out_dtype=F32, w_rows_out=True, after=()):
    nc, s, kc = dh.shape
    d = w.shape[1] if w_rows_out else w.shape[2]
    ts = _rows(s)
    has_res = res is not None
    mm = _dot_nt if w_rows_out else _dot

    def body(*refs):
        o_ref = refs[-1]
        dh_ref, w_ref = refs[:2]
        if has_res:
            r_ref = refs[2]
        out = mm(dh_ref[0], w_ref[0])
        for j in range(1, nc):
            out = out + mm(dh_ref[j], w_ref[j])
        if has_res:
            out = out + ALPHA * r_ref[...]
        o_ref[...] = out.astype(o_ref.dtype)

    in_specs = [pl.BlockSpec((nc, ts, kc), lambda i: (0, i, 0)), _resident(w.shape, lambda i: (0, 0, 0))]
    args = [dh, w]
    if has_res:
        in_specs.append(pl.BlockSpec((ts, d), lambda i: (i, 0)))
        args.append(res)
    in_specs += [pl.BlockSpec(memory_space=pl.ANY)] * len(after)
    args += list(after)
    return pl.pallas_call(
        body, name=name, grid=(s // ts,), in_specs=in_specs,
        out_specs=pl.BlockSpec((ts, d), lambda i: (i, 0)),
        out_shape=jax.ShapeDtypeStruct((s, d), out_dtype),
        compiler_params=_cp("parallel"))(*args)


def _mm_res_ln(a, w, x, gain, bias, fscale, name):
    nc, s, kc = a.shape
    d = w.shape[2]
    ts = _rows(s)

    def body(a_ref, w_ref, x_ref, g_ref, b_ref, y_ref, yb_ref, xh_ref, r_ref):
        f = _dot(a_ref[0], w_ref[0])
        for j in range(1, nc):
            f = f + _dot(a_ref[j], w_ref[j])
        z = ALPHA * x_ref[...] + fscale * f
        mu = jnp.mean(z, axis=-1, keepdims=True)
        zc = z - mu
        var = jnp.mean(zc * zc, axis=-1, keepdims=True)
        r = lax.rsqrt(var + LN_EPS)
        xh = zc * r
        y = xh * g_ref[...] + b_ref[...]
        y_ref[...] = y
        yb_ref[...] = y.astype(BF)
        xh_ref[...] = xh
        r_ref[...] = r

    row = pl.BlockSpec((ts, d), lambda i: (i, 0))
    vec = pl.BlockSpec((1, d), lambda i: (0, 0))
    return pl.pallas_call(
        body, name=name, grid=(s // ts,),
        in_specs=[pl.BlockSpec((nc, ts, kc), lambda i: (0, i, 0)), _resident((nc, kc, d), lambda i: (0, 0, 0)),
                  row, vec, vec],
        out_specs=[row, row, row, pl.BlockSpec((ts, 1), lambda i: (i, 0))],
        out_shape=[jax.ShapeDtypeStruct((s, d), F32), jax.ShapeDtypeStruct((s, d), BF),
                   jax.ShapeDtypeStruct((s, d), F32), jax.ShapeDtypeStruct((s, 1), F32)],
        compiler_params=_cp("parallel"))(a, w, x, gain, bias)


def _ln_bwd(dy, xh, rstd, gain, fscale, name, after=()):
    s, d = dy.shape
    ts = _rows(s)
    na = len(after)

    def body(*refs):
        dy_ref, xh_ref, r_ref, g_ref = refs[:4]
        dz_ref, dzb_ref, dg_ref, db_ref = refs[4 + na:]
        i = pl.program_id(0)
        dyv = dy_ref[...]
        xhv = xh_ref[...]
        dxh = dyv * g_ref[...]
        m1 = jnp.mean(dxh, axis=-1, keepdims=True)
        m2 = jnp.mean(dxh * xhv, axis=-1, keepdims=True)
        dz = r_ref[...] * (dxh - m1 - xhv * m2)
        dz_ref[...] = dz
        dzb_ref[...] = (fscale * dz).astype(BF)

        @pl.when(i == 0)
        def _():
            dg_ref[...] = jnp.zeros_like(dg_ref)
            db_ref[...] = jnp.zeros_like(db_ref)

        dg_ref[...] += jnp.sum(dyv * xhv, axis=0, keepdims=True)
        db_ref[...] += jnp.sum(dyv, axis=0, keepdims=True)

    row = pl.BlockSpec((ts, d), lambda i: (i, 0))
    vec = pl.BlockSpec((1, d), lambda i: (0, 0))
    return pl.pallas_call(
        body, name=name, grid=(s // ts,),
        in_specs=[row, row, pl.BlockSpec((ts, 1), lambda i: (i, 0)), vec] + [pl.BlockSpec(memory_space=pl.ANY)] * na,
        out_specs=[row, row, vec, vec],
        out_shape=[jax.ShapeDtypeStruct((s, d), F32), jax.ShapeDtypeStruct((s, d), BF),
                   jax.ShapeDtypeStruct((1, d), F32), jax.ShapeDtypeStruct((1, d), F32)],
        compiler_params=_cp("arbitrary"))(dy, xh, rstd, gain, *after)


def _ffn_up(xb, wgu, name):
    s, d = xb.shape
    c = wgu.shape[1]
    nch = wgu.shape[0] // 2
    ts = _rows(s, 1024)
    w4 = wgu.reshape(2, nch, c, d)

    def body(x_ref, w_ref, gu_ref, a_ref):
        x = x_ref[...]
        g = _dot_nt(x, w_ref[0])
        u = _dot_nt(x, w_ref[1])
        sg = _sigmoid(g)
        t = g * sg
        gu_ref[0] = (u * (sg * (1.0 + g - t))).astype(BF)
        gu_ref[1] = t.astype(BF)
        a_ref[...] = (t * u).astype(BF)

    return pl.pallas_call(
        body, name=name, grid=(nch, s // ts),
        in_specs=[pl.BlockSpec((ts, d), lambda j, i: (i, 0)),
                  pl.BlockSpec((2, None, c, d), lambda j, i: (0, j, 0, 0))],
        out_specs=[pl.BlockSpec((2, None, ts, c), lambda j, i: (0, j, i, 0)),
                   pl.BlockSpec((None, ts, c), lambda j, i: (j, i, 0))],
        out_shape=[jax.ShapeDtypeStruct((2, nch, s, c), BF), jax.ShapeDtypeStruct((nch, s, c), BF)],
        compiler_params=_cp("parallel", "parallel"))(xb, w4)


def _ffn_fwd_main(x, xb, wgu, wd4, gain, bias, name):
    s, d = x.shape
    nch, c = wd4.shape[0], wd4.shape[1]
    ts = _rows(s, 256)

    def body(x_ref, xb_ref, wgu_ref, wd_ref, g_ref, b_ref, y_ref, yb_ref, gu_ref, a_ref, xh_ref, r_ref):
        xbv = xb_ref[...]
        f = jnp.zeros((ts, d), F32)
        for j in range(nch):
            g = _dot_nt(xbv, wgu_ref[j])
            u = _dot_nt(xbv, wgu_ref[nch + j])
            sg = _sigmoid(g)
            t = g * sg
            gu_ref[0, j] = (u * (sg * (1.0 + g - t))).astype(BF)
            gu_ref[1, j] = t.astype(BF)
            act = (t * u).astype(BF)
            a_ref[j] = act
            f = f + _dot(act, wd_ref[j])
        z = ALPHA * x_ref[...] + 0.5 * f
        mu = jnp.mean(z, axis=-1, keepdims=True)
        zc = z - mu
        var = jnp.mean(zc * zc, axis=-1, keepdims=True)
        r = lax.rsqrt(var + LN_EPS)
        xh = zc * r
        y = xh * g_ref[...] + b_ref[...]
        y_ref[...] = y
        yb_ref[...] = y.astype(BF)
        xh_ref[...] = xh
        r_ref[...] = r

    row = pl.BlockSpec((ts, d), lambda i: (i, 0))
    vec = pl.BlockSpec((1, d), lambda i: (0, 0))
    return pl.pallas_call(
        body, name=name, grid=(s // ts,),
        in_specs=[row, row, _resident(wgu.shape, lambda i: (0, 0, 0)), _resident(wd4.shape, lambda i: (0, 0, 0)),
                  vec, vec],
        out_specs=[row, row, pl.BlockSpec((2, nch, ts, c), lambda i: (0, 0, i, 0)),
                   pl.BlockSpec((nch, ts, c), lambda i: (0, i, 0)), row, pl.BlockSpec((ts, 1), lambda i: (i, 0))],
        out_shape=[jax.ShapeDtypeStruct((s, d), F32), jax.ShapeDtypeStruct((s, d), BF),
                   jax.ShapeDtypeStruct((2, nch, s, c), BF), jax.ShapeDtypeStruct((nch, s, c), BF),
                   jax.ShapeDtypeStruct((s, d), F32), jax.ShapeDtypeStruct((s, 1), F32)],
        compiler_params=_cp("parallel"))(x, xb, wgu, wd4, gain, bias)


def _ffn_bwd_main(dy, xh, rstd, gain, wd4, wgu, gu, name, after=(), with_dx=True):
    s, d = dy.shape
    nch, c = wd4.shape[0], wd4.shape[1]
    ts = _rows(s, 256)
    na = len(after)

    def body(*refs):
        dy_ref, xh_ref, r_ref, g_ref, wd_ref, wgu_ref, gu_ref = refs[:7]
        dx_ref, dzb_ref, dh_ref, dg_ref, db_ref = refs[7 + na:]
        i = pl.program_id(0)
        dyv = dy_ref[...]
        xhv = xh_ref[...]
        dxh = dyv * g_ref[...]
        m1 = jnp.mean(dxh, axis=-1, keepdims=True)
        m2 = jnp.mean(dxh * xhv, axis=-1, keepdims=True)
        dz = r_ref[...] * (dxh - m1 - xhv * m2)
        dzb = (0.5 * dz).astype(BF)
        dzb_ref[...] = dzb

        @pl.when(i == 0)
        def _():
            dg_ref[...] = jnp.zeros_like(dg_ref)
            db_ref[...] = jnp.zeros_like(db_ref)

        dg_ref[...] += jnp.sum(dyv * xhv, axis=0, keepdims=True)
        db_ref[...] += jnp.sum(dyv, axis=0, keepdims=True)

        dx = ALPHA * dz if with_dx else dz
        for j in range(nch):
            da = _dot_nt(dzb, wd_ref[j])
            dgate = (da * gu_ref[0, j].astype(F32)).astype(BF)
            dup = (da * gu_ref[1, j].astype(F32)).astype(BF)
            dh_ref[0, j] = dgate
            dh_ref[1, j] = dup
            if with_dx:
                dx = dx + _dot(dgate, wgu_ref[j]) + _dot(dup, wgu_ref[nch + j])
        dx_ref[...] = dx

    row = pl.BlockSpec((ts, d), lambda i: (i, 0))
    vec = pl.BlockSpec((1, d), lambda i: (0, 0))
    act = pl.BlockSpec((2, nch, ts, c), lambda i: (0, 0, i, 0))
    return pl.pallas_call(
        body, name=name, grid=(s // ts,),
        in_specs=[row, row, pl.BlockSpec((ts, 1), lambda i: (i, 0)), vec,
                  _resident(wd4.shape, lambda i: (0, 0, 0)), _resident(wgu.shape, lambda i: (0, 0, 0)), act]
                 + [pl.BlockSpec(memory_space=pl.ANY)] * na,
        out_specs=[row, row, act, vec, vec],
        out_shape=[jax.ShapeDtypeStruct((s, d), F32), jax.ShapeDtypeStruct((s, d), BF),
                   jax.ShapeDtypeStruct((2, nch, s, c), BF),
                   jax.ShapeDtypeStruct((1, d), F32), jax.ShapeDtypeStruct((1, d), F32)],
        compiler_params=_cp("arbitrary"))(dy, xh, rstd, gain, wd4, wgu, gu, *after)


def _rope_tables(s, sign):
    pos = jnp.arange(s, dtype=F32)
    inv_freq = 1.0 / (ROPE_THETA ** (jnp.arange(ROT_HALF, dtype=F32) / ROT_HALF))
    ang = pos[:, None] * inv_freq[None, :]
    cos, sin = jnp.cos(ang), jnp.sin(ang) * sign
    one = jnp.ones((s, HEAD_DIM - 2 * ROT_HALF), F32)
    zero = jnp.zeros((s, HEAD_DIM - 2 * ROT_HALF), F32)
    zh = jnp.zeros((s, ROT_HALF), F32)
    cos_f = jnp.concatenate([cos, cos, one], axis=1)
    sin_a = jnp.concatenate([-sin, zh, zero], axis=1)
    sin_b = jnp.concatenate([zh, sin, zero], axis=1)
    rep = LANES // HEAD_DIM
    return tuple(jnp.tile(t, (1, rep)) for t in (cos_f, sin_a, sin_b))


def _rope(t, c_ref, sa_ref, sb_ref):
    return (t * c_ref[...] + pltpu.roll(t, LANES - ROT_HALF, 1) * sa_ref[...]
            + pltpu.roll(t, ROT_HALF, 1) * sb_ref[...])


def _proj_rope(xb, w, tabs, n_rope, name, w_rows_out, tail_block=None):
    s, d = xb.shape
    n = w.shape[0] if w_rows_out else w.shape[1]
    tm = _rows(s, 256)
    has_tail = tail_block is not None

    def body(x_ref, w_ref, c_ref, sa_ref, sb_ref, o_ref, *tail_ref):
        h = (_dot_nt if w_rows_out else _dot)(x_ref[...], w_ref[...])
        for cb in range(n // LANES):
            t = h[:, cb * LANES:(cb + 1) * LANES]
            if cb < n_rope:
                t = _rope(t, c_ref, sa_ref, sb_ref)
            o_ref[:, cb * LANES:(cb + 1) * LANES] = t.astype(BF)
        if has_tail:
            tail_ref[0][...] = h[:, tail_block * LANES:(tail_block + 1) * LANES]

    tab = pl.BlockSpec((tm, LANES), lambda i: (i, 0))
    out_specs = [pl.BlockSpec((tm, n), lambda i: (i, 0))]
    out_shape = [jax.ShapeDtypeStruct((s, n), BF)]
    if has_tail:
        out_specs.append(tab)
        out_shape.append(jax.ShapeDtypeStruct((s, LANES), F32))
    res = pl.pallas_call(
        body, name=name, grid=(s // tm,),
        in_specs=[pl.BlockSpec((tm, d), lambda i: (i, 0)), _resident(w.shape, lambda i: (0, 0)), tab, tab, tab],
        out_specs=out_specs, out_shape=out_shape, compiler_params=_cp("parallel"))(xb, w, *tabs)
    return res if has_tail else res[0]


def _rope_cast(parts, tabs, n_rope, name, transposed=()):
    s = tabs[0].shape[0]
    flip = [i in transposed for i in range(len(parts))]
    widths = [p.shape[0] if f else p.shape[1] for p, f in zip(parts, flip)]
    n = sum(widths)
    npart = len(parts)
    ts = _rows(s, 256)

    def body(*refs):
        part_refs = refs[:npart]
        c_ref, sa_ref, sb_ref, o_ref = refs[npart:]
        col = 0
        for ref, w, f in zip(part_refs, widths, flip):
            for j in range(w // LANES):
                if f:
                    t = jnp.transpose(ref[j * LANES:(j + 1) * LANES, :])
                else:
                    t = ref[:, j * LANES:(j + 1) * LANES]
                if col < n_rope:
                    t = _rope(t, c_ref, sa_ref, sb_ref)
                o_ref[:, col * LANES:(col + 1) * LANES] = t.astype(BF)
                col += 1

    tab = pl.BlockSpec((ts, LANES), lambda i: (i, 0))
    return pl.pallas_call(
        body, name=name, grid=(s // ts,),
        in_specs=[pl.BlockSpec((w, ts), lambda i: (0, i)) if f else pl.BlockSpec((ts, w), lambda i: (i, 0))
                  for w, f in zip(widths, flip)] + [tab, tab, tab],
        out_specs=pl.BlockSpec((ts, n), lambda i: (i, 0)),
        out_shape=jax.ShapeDtypeStruct((s, n), BF),
        compiler_params=_cp("parallel"))(*parts, *tabs)


def _head_masks():
    lane = lax.broadcasted_iota(jnp.int32, (1, LANES), 1)
    return [lane < HEAD_DIM, lane >= HEAD_DIM]


def _sel(mask, v):
    return jnp.where(mask, v, jnp.zeros_like(v))


def _pick(mask, wide, fill):
    return jnp.max(jnp.where(mask, wide, fill), axis=1, keepdims=True)


def _band_masks(has_other, prev):
    qi = lax.broadcasted_iota(jnp.int32, (BLOCK, BLOCK), 0)
    kj = lax.broadcasted_iota(jnp.int32, (BLOCK, BLOCK), 1)
    if prev:
        return kj >= qi + jnp.where(has_other, 0, BLOCK)
    return kj <= qi


class _BandView:
    def __init__(self, s, g):
        self.r = 4 ** g
        self.nl = s // self.r
        self.nblk = self.nl // BLOCK
        self.nsub = min(BAND_SUB, self.nblk)
        self.tile = self.nsub * BLOCK
        self.grid = (self.r, GROUP_W // LANES, self.nblk // self.nsub)

    def view(self, a):
        return a.reshape(self.nl, self.r * a.shape[1])

    def qkv(self, hb, g):
        npair = MIX_W // LANES
        offs = [i * npair + g * GROUP_W // LANES for i in range(3)]
        if self.r == 1:
            return [hb] * 3, hb.shape[1], offs
        return [self.view(hb[:, o * LANES:o * LANES + GROUP_W]) for o in offs], GROUP_W, [0, 0, 0]

    def specs(self, width, off):
        nb, nsub, last = width // LANES, self.nsub, self.nblk - 1

        def col(rho, p):
            return rho * nb + off + p

        return (pl.BlockSpec((self.tile, LANES), lambda rho, p, t: (t, col(rho, p))),
                pl.BlockSpec((BLOCK, LANES), lambda rho, p, t: (jnp.maximum(t * nsub - 1, 0), col(rho, p))),
                pl.BlockSpec((BLOCK, LANES), lambda rho, p, t: (jnp.minimum(t * nsub + nsub, last), col(rho, p))))


def _band_fwd(hb, g, name):
    s, n = hb.shape
    bv = _BandView(s, g)
    nsub = bv.nsub
    npair = MIX_W // LANES

    def body(q_ref, kc_ref, kp_ref, vc_ref, vp_ref, o_ref, l_ref):
        t = pl.program_id(2)
        mc = _band_masks(None, False)
        hm = _head_masks()
        for i in range(nsub):
            rows = slice(i * BLOCK, (i + 1) * BLOCK)
            has_prev = t > 0 if i == 0 else True
            mp = _band_masks(has_prev, True)
            q, kc, vc = q_ref[rows, :], kc_ref[rows, :], vc_ref[rows, :]
            if i == 0:
                kp, vp = kp_ref[...], vp_ref[...]
            else:
                prev = slice((i - 1) * BLOCK, i * BLOCK)
                kp, vp = kc_ref[prev, :], vc_ref[prev, :]
            o = jnp.zeros((BLOCK, LANES), F32)
            lse_w = jnp.zeros((BLOCK, LANES), F32)
            for h in range(2):
                qh = _sel(hm[h], q)
                sc = jnp.where(mc, _dot_nt(qh, kc) * SCALE, NEG)
                sp = jnp.where(mp, _dot_nt(qh, kp) * SCALE, NEG)
                m = jnp.maximum(jnp.max(sc, axis=1, keepdims=True), jnp.max(sp, axis=1, keepdims=True))
                pc = jnp.exp(sc - m)
                pp = jnp.exp(sp - m)
                l = jnp.sum(pc, axis=1, keepdims=True) + jnp.sum(pp, axis=1, keepdims=True)
                oh = _dot(pc.astype(BF), _sel(hm[h], vc)) + _dot(pp.astype(BF), _sel(hm[h], vp))
                o = o + oh / l
                lse_w = jnp.where(hm[h], m + jnp.log(l), lse_w)
            o_ref[rows, :] = o
            l_ref[rows, :] = lse_w

    (qv, kv_, vv), width, (qo, ko, vo) = bv.qkv(hb, g)
    q_cur, _, _ = bv.specs(width, qo)
    k_cur, k_prv, _ = bv.specs(width, ko)
    v_cur, v_prv, _ = bv.specs(width, vo)
    out_spec = bv.specs(GROUP_W, 0)[0]
    out = jax.ShapeDtypeStruct((bv.nl, bv.r * GROUP_W), F32)
    o, l = pl.pallas_call(
        body, name=name, grid=bv.grid,
        in_specs=[q_cur, k_cur, k_prv, v_cur, v_prv], out_specs=[out_spec, out_spec], out_shape=[out, out],
        compiler_params=_cp("parallel", "parallel", "parallel"))(qv, kv_, kv_, vv, vv)
    return o.reshape(s, GROUP_W), l.reshape(s, GROUP_W)


def _band_combine(os, ls, name):
    ng = len(os)
    s, w = os[0].shape
    ts = _rows(s)

    def body(*refs):
        o_refs, l_refs = refs[:ng], refs[ng:2 * ng]
        oa_ref, lt_ref = refs[2 * ng:]
        lv = [r[...] for r in l_refs]
        m = functools.reduce(jnp.maximum, lv)
        es = [jnp.exp(l - m) for l in lv]
        den = functools.reduce(lambda a, b: a + b, es)
        num = functools.reduce(lambda a, b: a + b, [es[g] * o_refs[g][...] for g in range(ng)])
        oa_ref[...] = (num / den).astype(BF)
        lt_ref[...] = m + jnp.log(den)

    blk = pl.BlockSpec((ts, w), lambda i: (i, 0))
    return pl.pallas_call(
        body, name=name, grid=(s // ts,), in_specs=[blk] * (2 * ng), out_specs=[blk, blk],
        out_shape=[jax.ShapeDtypeStruct((s, w), BF), jax.ShapeDtypeStruct((s, w), F32)],
        compiler_params=_cp("parallel"))(*os, *ls)


def _band_bwd(hb, dcat, oa, lt, g, name):
    s, n = hb.shape
    bv = _BandView(s, g)
    nsub = bv.nsub
    npair = MIX_W // LANES
    ntile = bv.grid[2]

    def body(q_ref, qn_ref, kc_ref, kp_ref, vc_ref, vp_ref, do_ref, don_ref, oa_ref, oan_ref, lt_ref, ltn_ref,
             dq_ref, dk_ref, dv_ref):
        t = pl.program_id(2)
        mc = _band_masks(None, False)
        hm = _head_masks()

        def block(ref, edge_ref, i):
            if i < 0 or i >= nsub:
                return edge_ref[...]
            return ref[i * BLOCK:(i + 1) * BLOCK, :]

        for i in range(nsub):
            mp = _band_masks(t > 0 if i == 0 else True, True)
            mn = _band_masks(t < ntile - 1 if i == nsub - 1 else True, True)
            q, qn = block(q_ref, None, i), block(q_ref, qn_ref, i + 1)
            kc, kp = block(kc_ref, None, i), block(kc_ref, kp_ref, i - 1)
            vc, vp = block(vc_ref, None, i), block(vc_ref, vp_ref, i - 1)
            do, don = block(do_ref, None, i), block(do_ref, don_ref, i + 1)
            dd = do.astype(F32) * block(oa_ref, None, i).astype(F32)
            ddn = don.astype(F32) * block(oa_ref, oan_ref, i + 1).astype(F32)
            lt, ltn = block(lt_ref, None, i), block(lt_ref, ltn_ref, i + 1)
            dq = jnp.zeros((BLOCK, LANES), F32)
            dk = jnp.zeros((BLOCK, LANES), F32)
            dv = jnp.zeros((BLOCK, LANES), F32)
            for h in range(2):
                qh, doh = _sel(hm[h], q), _sel(hm[h], do)
                qnh, donh = _sel(hm[h], qn), _sel(hm[h], don)
                kch, kph = _sel(hm[h], kc), _sel(hm[h], kp)
                lse = _pick(hm[h], lt, NEG)
                lsen = _pick(hm[h], ltn, NEG)
                dsum = jnp.sum(_sel(hm[h], dd), axis=1, keepdims=True)
                dsumn = jnp.sum(_sel(hm[h], ddn), axis=1, keepdims=True)
                pc = jnp.exp(jnp.where(mc, _dot_nt(qh, kc) * SCALE, NEG) - lse)
                pp = jnp.exp(jnp.where(mp, _dot_nt(qh, kp) * SCALE, NEG) - lse)
                dsc = pc * (_dot_nt(doh, vc) - dsum)
                dsp = pp * (_dot_nt(doh, vp) - dsum)
                dq = dq + SCALE * (_dot(dsc.astype(BF), kch) + _dot(dsp.astype(BF), kph))
                pn = jnp.exp(jnp.where(mn, _dot_nt(qnh, kc) * SCALE, NEG) - lsen)
                dsn = pn * (_dot_nt(donh, vc) - dsumn)
                dk = dk + SCALE * (_dot_tn(dsc.astype(BF), qh) + _dot_tn(dsn.astype(BF), qnh))
                dv = dv + _dot_tn(pc.astype(BF), doh) + _dot_tn(pn.astype(BF), donh)
            rows = slice(i * BLOCK, (i + 1) * BLOCK)
            dq_ref[rows, :] = dq
            dk_ref[rows, :] = dk
            dv_ref[rows, :] = dv

    (qv, kv_, vv), width, (qo, ko, vo) = bv.qkv(hb, g)
    q_cur, _, q_nxt = bv.specs(width, qo)
    k_cur, k_prv, _ = bv.specs(width, ko)
    v_cur, v_prv, _ = bv.specs(width, vo)
    w_cur, _, w_nxt = bv.specs(GROUP_W, 0)
    out = jax.ShapeDtypeStruct((bv.nl, bv.r * GROUP_W), F32)
    dv_, ov, lv = bv.view(dcat[:, :GROUP_W]), bv.view(oa), bv.view(lt)
    res = pl.pallas_call(
        body, name=name, grid=bv.grid,
        in_specs=[q_cur, q_nxt, k_cur, k_prv, v_cur, v_prv, w_cur, w_nxt, w_cur, w_nxt, w_cur, w_nxt],
        out_specs=[w_cur, w_cur, w_cur], out_shape=[out, out, out],
        compiler_params=_cp("parallel", "parallel", "parallel"))(
            qv, qv, kv_, kv_, vv, vv, dv_, dv_, ov, ov, lv, lv)
    return [t.reshape(s, GROUP_W) for t in res]


def _mem_fwd(hb, q_blk0, kv, name):
    s = hb.shape[0]
    m = kv.shape[0]
    tq = _rows(s)
    npair = MEM_W // LANES

    def body(q_ref, k_ref, v_ref, o_ref, l_ref):
        q, k, v = q_ref[...], k_ref[...], v_ref[...]
        hm = _head_masks()
        o = jnp.zeros((tq, LANES), F32)
        lse_w = jnp.zeros((tq, LANES), F32)
        for h in range(2):
            sc = _dot_nt(_sel(hm[h], q), k) * SCALE
            mx = jnp.max(sc, axis=1, keepdims=True)
            p = jnp.exp(sc - mx)
            l = jnp.sum(p, axis=1, keepdims=True)
            o = o + _dot(p.astype(BF), _sel(hm[h], v)) / l
            lse_w = jnp.where(hm[h], mx + jnp.log(l), lse_w)
        o_ref[...] = o.astype(BF)
        l_ref[...] = lse_w

    blk = pl.BlockSpec((tq, LANES), lambda p, i: (i, p))
    return pl.pallas_call(
        body, name=name, grid=(npair, s // tq),
        in_specs=[pl.BlockSpec((tq, LANES), lambda p, i: (i, q_blk0 + p)),
                  pl.BlockSpec((m, LANES), lambda p, i: (0, p)),
                  pl.BlockSpec((m, LANES), lambda p, i: (0, npair + p))],
        out_specs=[blk, blk],
        out_shape=[jax.ShapeDtypeStruct((s, MEM_W), BF), jax.ShapeDtypeStruct((s, MEM_W), F32)],
        compiler_params=_cp("parallel", "parallel"))(hb, kv, kv)


def _mem_bwd(hb, q_blk0, kv, dcat, cat, o_blk0, lse, name):
    s = hb.shape[0]
    m = kv.shape[0]
    tq = _rows(s)
    npair = MEM_W // LANES

    def body(q_ref, k_ref, v_ref, do_ref, o_ref, l_ref, dq_ref, dk_ref, dv_ref):
        i = pl.program_id(1)

        @pl.when(i == 0)
        def _():
            dk_ref[...] = jnp.zeros_like(dk_ref)
            dv_ref[...] = jnp.zeros_like(dv_ref)

        q, k, v, do = q_ref[...], k_ref[...], v_ref[...], do_ref[...]
        dd = do.astype(F32) * o_ref[...].astype(F32)
        lt = l_ref[...]
        hm = _head_masks()
        dq = jnp.zeros((tq, LANES), F32)
        dk = jnp.zeros((m, LANES), F32)
        dv = jnp.zeros((m, LANES), F32)
        for h in range(2):
            qh, doh = _sel(hm[h], q), _sel(hm[h], do)
            p = jnp.exp(_dot_nt(qh, k) * SCALE - _pick(hm[h], lt, NEG))
            ds = p * (_dot_nt(doh, v) - jnp.sum(_sel(hm[h], dd), axis=1, keepdims=True))
            dq = dq + SCALE * _dot(ds.astype(BF), _sel(hm[h], k))
            dk = dk + SCALE * _dot_tn(ds.astype(BF), qh)
            dv = dv + _dot_tn(p.astype(BF), doh)
        dq_ref[...] = dq
        dk_ref[...] += dk
        dv_ref[...] += dv

    row = pl.BlockSpec((tq, LANES), lambda p, i: (i, p))
    orow = pl.BlockSpec((tq, LANES), lambda p, i: (i, o_blk0 + p))
    acc = pl.BlockSpec((m, LANES), lambda p, i: (0, p))
    return pl.pallas_call(
        body, name=name, grid=(npair, s // tq),
        in_specs=[pl.BlockSpec((tq, LANES), lambda p, i: (i, q_blk0 + p)),
                  pl.BlockSpec((m, LANES), lambda p, i: (0, p)),
                  pl.BlockSpec((m, LANES), lambda p, i: (0, npair + p)), orow, orow, row],
        out_specs=[row, acc, acc],
        out_shape=[jax.ShapeDtypeStruct((s, MEM_W), F32), jax.ShapeDtypeStruct((m, MEM_W), F32),
                   jax.ShapeDtypeStruct((m, MEM_W), F32)],
        compiler_params=_cp("parallel", "arbitrary"))(hb, kv, kv, dcat, cat, lse)


def _gate_fwd(f_t, bias, name):
    hp, s = f_t.shape
    nblk = s // LANES

    def body(f_ref, b_ref, c_ref):
        lane = lax.broadcasted_iota(jnp.int32, (hp, LANES), 1)

        def step(i, carry):
            off = pl.multiple_of(i * LANES, LANES)
            x = f_ref[:, pl.ds(off, LANES)] + b_ref[...]
            acc = jnp.minimum(x, 0.0) - jnp.log(1.0 + jnp.exp(-jnp.abs(x)))
            sh = 1
            while sh < LANES:
                acc = acc + jnp.where(lane >= sh, pltpu.roll(acc, sh, 1), 0.0)
                sh *= 2
            acc = acc + carry
            c_ref[:, pl.ds(off, LANES)] = acc
            return acc[:, LANES - 1:LANES]

        lax.fori_loop(0, nblk, step, jnp.zeros((hp, 1), F32))

    vm = pl.BlockSpec(memory_space=pltpu.VMEM)
    return pl.pallas_call(body, name=name, in_specs=[vm, vm], out_specs=vm,
                          out_shape=jax.ShapeDtypeStruct((hp, s), F32),
                          compiler_params=pltpu.CompilerParams(vmem_limit_bytes=VMEM_LIMIT))(f_t, bias)


def _gate_bwd(dc_t, f_t, bias, name):
    hp, s = f_t.shape
    nblk = s // LANES

    def body(dc_ref, f_ref, b_ref, df_ref, db_ref):
        lane = lax.broadcasted_iota(jnp.int32, (hp, LANES), 1)

        def step(t, carry):
            suffix, dbias = carry
            off = pl.multiple_of((nblk - 1 - t) * LANES, LANES)
            acc = dc_ref[:, pl.ds(off, LANES)]
            sh = 1
            while sh < LANES:
                acc = acc + jnp.where(lane < LANES - sh, pltpu.roll(acc, LANES - sh, 1), 0.0)
                sh *= 2
            acc = acc + suffix
            x = f_ref[:, pl.ds(off, LANES)] + b_ref[...]
            df = acc * _sigmoid(-x)
            df_ref[:, pl.ds(off, LANES)] = df
            return acc[:, 0:1], dbias + jnp.sum(df, axis=1, keepdims=True)

        _, dbias = lax.fori_loop(0, nblk, step, (jnp.zeros((hp, 1), F32), jnp.zeros((hp, 1), F32)))
        db_ref[...] = dbias

    vm = pl.BlockSpec(memory_space=pltpu.VMEM)
    return pl.pallas_call(body, name=name, in_specs=[vm, vm, vm], out_specs=[vm, vm],
                          out_shape=[jax.ShapeDtypeStruct((hp, s), F32), jax.ShapeDtypeStruct((hp, 1), F32)],
                          compiler_params=pltpu.CompilerParams(vmem_limit_bytes=VMEM_LIMIT))(dc_t, f_t, bias)


def _wide(rep, width):
    return jnp.tile(rep, (1, width // LANES))


def _fold(t):
    part = t[:, :LANES]
    for c in range(1, t.shape[1] // LANES):
        part = part + t[:, c * LANES:(c + 1) * LANES]
    return part


def _foxt_logits(q, k, cq_row, ck_rep, mask, hmask):
    s = _dot_nt(_sel(hmask, k), q) + (cq_row - _wide(ck_rep, q.shape[0]))
    if mask is not None:
        s = jnp.where(mask, s, NEG)
    return s


def _causal_t(qi, kj, tq, tk):
    return (kj * tk + lax.broadcasted_iota(jnp.int32, (tk, tq), 0)
            <= qi * tq + lax.broadcasted_iota(jnp.int32, (tk, tq), 1))


def _fox_tiles(s):
    tq = _rows(s, 1024)
    return tq, tq // 2, s // tq


def _count_ge(t, bounds):
    return sum([(t >= b).astype(jnp.int32) for b in bounds], jnp.int32(0))


def _sweep_q_major(t, nq):
    qi = _count_ge(t, [r * (r + 1) for r in range(1, nq)])
    return qi, t - qi * (qi + 1)


def _sweep_k_major(t, nq):
    counts = [nq - j // 2 for j in range(2 * nq)]
    offs = [sum(counts[:j]) for j in range(1, 2 * nq)]
    kj = _count_ge(t, offs)
    start = sum([jnp.where(t >= o, c, 0) for o, c in zip(offs, counts)], jnp.int32(0))
    qi = kj // 2 + (t - start)
    return kj, qi, t == start, qi == nq - 1


def _foxt_fwd(hb, c_rep, c_t3, name):
    s = hb.shape[0]
    npair = MIX_W // LANES
    tq, tk, nq = _fox_tiles(s)

    def body(q_ref, k_ref, v_ref, cq_ref, ck_ref, o_ref, l_ref, m_s, l_s, acc):
        qi, kj = _sweep_q_major(pl.program_id(1), nq)
        hm = _head_masks()

        @pl.when(kj == 0)
        def _():
            m_s[...] = jnp.full_like(m_s, NEG)
            l_s[...] = jnp.zeros_like(l_s)
            acc[...] = jnp.zeros_like(acc)

        def step(mask):
            q, k = q_ref[...] * SCALE, k_ref[...]
            vt = jnp.transpose(v_ref[...])
            cq = cq_ref[...]
            for h in range(2):
                st = _foxt_logits(q, k, cq[h:h + 1, :], ck_ref[h], mask, hm[h])
                m_old = m_s[h]
                m_new = jnp.maximum(m_old, jnp.max(st, axis=0, keepdims=True))
                pt = jnp.exp(st - m_new)
                corr = jnp.exp(m_old - m_new)
                l_s[h] = l_s[h] * corr + jnp.sum(pt, axis=0, keepdims=True)
                acc[h] = acc[h] * corr + _dot(vt[h * HEAD_DIM:(h + 1) * HEAD_DIM, :], pt.astype(BF))
                m_s[h] = m_new

        @pl.when(kj < 2 * qi)
        def _():
            step(None)

        @pl.when(kj >= 2 * qi)
        def _():
            step(_causal_t(qi, kj, tq, tk))

        @pl.when(kj == 2 * qi + 1)
        def _():
            outs = []
            for h in range(2):
                outs.append(acc[h] / l_s[h])
                l_ref[h:h + 1, :] = m_s[h] + jnp.log(l_s[h])
            o_ref[...] = jnp.transpose(jnp.concatenate(outs, axis=0)).astype(BF)

    def q_map(p, t):
        return (_sweep_q_major(t, nq)[0], p)

    def kv_map(off):
        return lambda p, t: (_sweep_q_major(t, nq)[1], off + p)

    blk = pl.BlockSpec((tq, LANES), q_map)
    row = pl.BlockSpec((None, 2, tq), lambda p, t: (p, 0, _sweep_q_major(t, nq)[0]))
    return pl.pallas_call(
        body, name=name, grid=(npair, nq * (nq + 1)),
        in_specs=[blk, pl.BlockSpec((tk, LANES), kv_map(npair)), pl.BlockSpec((tk, LANES), kv_map(2 * npair)), row,
                  pl.BlockSpec((2, tk, LANES), lambda p, t: (p, _sweep_q_major(t, nq)[1], 0))],
        out_specs=[blk, row],
        out_shape=[jax.ShapeDtypeStruct((s, MIX_W), BF), jax.ShapeDtypeStruct((npair, 2, s), F32)],
        scratch_shapes=[pltpu.VMEM((2, 1, tq), F32), pltpu.VMEM((2, 1, tq), F32),
                        pltpu.VMEM((2, HEAD_DIM, tq), F32)],
        compiler_params=_cp("parallel", "arbitrary"))(hb, hb, hb, c_t3, c_rep)


def _foxt_dsum(hb, dcat, lse, c_rep, c_t3, name):
    s = hb.shape[0]
    npair = MIX_W // LANES
    tq, tk, nq = _fox_tiles(s)

    def body(q_ref, k_ref, v_ref, do_ref, l_ref, cq_ref, ck_ref, d_ref, acc):
        qi, kj = _sweep_q_major(pl.program_id(1), nq)
        hm = _head_masks()

        @pl.when(kj == 0)
        def _():
            acc[...] = jnp.zeros_like(acc)

        def step(mask):
            q, k, v, do = q_ref[...] * SCALE, k_ref[...], v_ref[...], do_ref[...]
            cq, lse_rows = cq_ref[...], l_ref[...]
            for h in range(2):
                pt = jnp.exp(_foxt_logits(q, k, cq[h:h + 1, :], ck_ref[h], mask, hm[h]) - lse_rows[h:h + 1, :])
                acc[h] += jnp.sum(pt * _dot_nt(_sel(hm[h], v), do), axis=0, keepdims=True)

        @pl.when(kj < 2 * qi)
        def _():
            step(None)

        @pl.when(kj >= 2 * qi)
        def _():
            step(_causal_t(qi, kj, tq, tk))

        @pl.when(kj == 2 * qi + 1)
        def _():
            for h in range(2):
                d_ref[h:h + 1, :] = acc[h]

    def q_map(p, t):
        return (_sweep_q_major(t, nq)[0], p)

    def kv_map(off):
        return lambda p, t: (_sweep_q_major(t, nq)[1], off + p)

    blk = pl.BlockSpec((tq, LANES), q_map)
    row = pl.BlockSpec((None, 2, tq), lambda p, t: (p, 0, _sweep_q_major(t, nq)[0]))
    return pl.pallas_call(
        body, name=name, grid=(npair, nq * (nq + 1)),
        in_specs=[blk, pl.BlockSpec((tk, LANES), kv_map(npair)), pl.BlockSpec((tk, LANES), kv_map(2 * npair)),
                  blk, row, row, pl.BlockSpec((2, tk, LANES), lambda p, t: (p, _sweep_q_major(t, nq)[1], 0))],
        out_specs=row, out_shape=jax.ShapeDtypeStruct((npair, 2, s), F32),
        scratch_shapes=[pltpu.VMEM((2, 1, tq), F32)],
        compiler_params=_cp("parallel", "arbitrary"))(hb, hb, hb, dcat, lse, c_t3, c_rep)


def _foxt_bwd(hb, dcat, dsum, lse, c_rep, c_t3, name):
    s = hb.shape[0]
    npair = MIX_W // LANES
    tq, tk, nq = _fox_tiles(s)

    def body(q_ref, k_ref, v_ref, do_ref, d_ref, l_ref, cq_ref, ck_ref, dq_ref, dk_ref, dv_ref, dc_ref, dc_s):
        t = pl.program_id(1)
        kj, qi, first, last = _sweep_k_major(t, nq)
        hm = _head_masks()

        @pl.when(first)
        def _():
            dk_ref[...] = jnp.zeros_like(dk_ref)
            dv_ref[...] = jnp.zeros_like(dv_ref)
            dc_s[...] = jnp.zeros_like(dc_s)

        @pl.when(t == 0)
        def _():
            dq_ref[...] = jnp.zeros_like(dq_ref)

        def step(mask):
            q, k, v, do = q_ref[...] * SCALE, k_ref[...], v_ref[...], do_ref[...]
            qt, kt, dot = jnp.transpose(q), jnp.transpose(k), jnp.transpose(do)
            cq, lse_rows, d_rows = cq_ref[...], l_ref[...], d_ref[...]
            dqs, dks, dvs = [], [], []
            for h in range(2):
                rows = slice(h * HEAD_DIM, (h + 1) * HEAD_DIM)
                pt = jnp.exp(_foxt_logits(q, k, cq[h:h + 1, :], ck_ref[h], mask, hm[h]) - lse_rows[h:h + 1, :])
                dst = pt * (_dot_nt(_sel(hm[h], v), do) - d_rows[h:h + 1, :])
                dsb = dst.astype(BF)
                dqs.append(_dot(kt[rows, :], dsb))
                dks.append(_dot_nt(qt[rows, :], dsb))
                dvs.append(_dot_nt(dot[rows, :], pt.astype(BF)))
                dc_s[h] += _fold(dst)
            cols = pl.ds(pl.multiple_of(qi * tq, tq), tq)
            dq_ref[:, cols] += SCALE * jnp.concatenate(dqs, axis=0)
            dk_ref[...] += jnp.concatenate(dks, axis=0)
            dv_ref[...] += jnp.concatenate(dvs, axis=0)

        @pl.when(kj < 2 * qi)
        def _():
            step(None)

        @pl.when(kj >= 2 * qi)
        def _():
            step(_causal_t(qi, kj, tq, tk))

        @pl.when(last)
        def _():
            for h in range(2):
                dc_ref[h:h + 1, :] = -jnp.sum(jnp.transpose(dc_s[h]), axis=0, keepdims=True)

    def kj_of(t):
        return _sweep_k_major(t, nq)[0]

    def qi_of(t):
        return _sweep_k_major(t, nq)[1]

    qblk = pl.BlockSpec((tq, LANES), lambda p, t: (qi_of(t), p))
    row = pl.BlockSpec((None, 2, tq), lambda p, t: (p, 0, qi_of(t)))
    kblk = pl.BlockSpec((LANES, tk), lambda p, t: (p, kj_of(t)))
    rep = pl.BlockSpec((2, tk, LANES), lambda p, t: (p, kj_of(t), 0))
    return pl.pallas_call(
        body, name=name, grid=(npair, nq * (nq + 1)),
        in_specs=[qblk,
                  pl.BlockSpec((tk, LANES), lambda p, t: (kj_of(t), npair + p)),
                  pl.BlockSpec((tk, LANES), lambda p, t: (kj_of(t), 2 * npair + p)),
                  qblk, row, row, row, rep],
        out_specs=[pl.BlockSpec((LANES, s), lambda p, t: (p, 0)), kblk, kblk,
                   pl.BlockSpec((None, 2, tk), lambda p, t: (p, 0, kj_of(t)))],
        out_shape=[jax.ShapeDtypeStruct((MIX_W, s), F32), jax.ShapeDtypeStruct((MIX_W, s), F32),
                   jax.ShapeDtypeStruct((MIX_W, s), F32), jax.ShapeDtypeStruct((npair, 2, s), F32)],
        scratch_shapes=[pltpu.VMEM((2, tk, LANES), F32)],
        compiler_params=_cp("arbitrary", "arbitrary"))(hb, hb, hb, dcat, dsum, lse, c_t3, c_rep)


def _loss_head(y, target, name):
    s, d = y.shape
    ts = _rows(s)

    def body(y_ref, t_ref, dy_ref, l_ref):
        i = pl.program_id(0)
        e = y_ref[...] - t_ref[...]
        dy_ref[...] = e * (1.0 / d)

        @pl.when(i == 0)
        def _():
            l_ref[...] = jnp.zeros_like(l_ref)

        part = jnp.sum(jnp.sum(e * e, axis=1, keepdims=True), axis=0, keepdims=True)
        l_ref[...] += part * (0.5 / d)

    row = pl.BlockSpec((ts, d), lambda i: (i, 0))
    return pl.pallas_call(
        body, name=name, grid=(s // ts,), in_specs=[row, row],
        out_specs=[row, pl.BlockSpec((1, 1), lambda i: (0, 0))],
        out_shape=[jax.ShapeDtypeStruct((s, d), F32), jax.ShapeDtypeStruct((1, 1), F32)],
        compiler_params=_cp("arbitrary"))(y, target)


def _adam_rows(r, c):
    cap = max(8, (1 << 20) // (4 * c))
    if r <= cap:
        return r
    best = None
    for t in range(8, cap + 1, 8):
        if r % t == 0:
            best = t
    return best if best is not None else r


def _reduce_adamw(contribs, w, m, v, name):
    nl = len(contribs)
    nd, r, c = contribs[0].shape
    tr = _adam_rows(r, c)
    bc1 = 1.0 - ADAM_B1 ** ADAM_STEP
    bc2 = 1.0 - ADAM_B2 ** ADAM_STEP

    def body(*refs):
        c_refs = refs[:nl]
        w_ref, m_ref, v_ref, g_ref, d_ref, nm_ref, nv_ref = refs[nl:]
        l = pl.program_id(0)
        for li in range(nl):
            @pl.when(l == li)
            def _(c_ref=c_refs[li]):
                g = c_ref[0].astype(F32)
                for k in range(1, nd):
                    g = g + c_ref[k].astype(F32)
                nm = ADAM_B1 * m_ref[...] + (1.0 - ADAM_B1) * g
                nv = ADAM_B2 * v_ref[...] + (1.0 - ADAM_B2) * (g * g)
                g_ref[...] = g
                nm_ref[...] = nm
                nv_ref[...] = nv
                d_ref[...] = -ADAM_LR * ((nm / bc1) / (jnp.sqrt(nv / bc2) + ADAM_EPS) + ADAM_WD * w_ref[...])

    def c_spec(li):
        return pl.BlockSpec((nd, tr, c), lambda l, i: (0, jnp.where(l == li, i, 0), 0))

    blk = pl.BlockSpec((None, tr, c), lambda l, i: (l, i, 0))
    out = jax.ShapeDtypeStruct((nl, r, c), F32)
    return pl.pallas_call(
        body, name=name, grid=(nl, r // tr),
        in_specs=[c_spec(li) for li in range(nl)] + [blk, blk, blk],
        out_specs=[blk, blk, blk, blk], out_shape=[out, out, out, out],
        compiler_params=_cp("arbitrary", "arbitrary"))(*contribs, w, m, v)


def _mesh_pos():
    return lax.axis_index("x"), lax.axis_index("y"), lax.axis_index("c")


def _peer(pos, k):
    x, y, c = pos
    return (1 - x if k & 4 else x, 1 - y if k & 2 else y, 1 - c if k & 1 else c)


def _linear(pos):
    return 4 * pos[0] + 2 * pos[1] + pos[2]


def _xfer_copies(srcs, lands, send_sems, recv_sems, local_sems, gather):
    pos = _mesh_pos()
    me = _linear(pos)
    local, remote = [], []
    for i, (src, land) in enumerate(zip(srcs, lands)):
        local.append(pltpu.make_async_copy(src if gather else src.at[me], land.at[me], local_sems.at[i]))
        for k in range(1, N_DEV):
            peer = _peer(pos, k)
            remote.append(pltpu.make_async_remote_copy(
                src_ref=src if gather else src.at[_linear(peer)], dst_ref=land.at[me],
                send_sem=send_sems.at[i * (N_DEV - 1) + k - 1], recv_sem=recv_sems.at[i * (N_DEV - 1) + k - 1],
                device_id=peer, device_id_type=MESH_ID))
    return local, remote


_HBM = pl.BlockSpec(memory_space=pltpu.HBM)
_SEM = pl.BlockSpec(memory_space=pltpu.SEMAPHORE)
_EFFECT = pltpu.SideEffectType.DATAFLOW_SIDE_EFFECTING


def _xfer_start(srcs, gather, name, after=()):
    n = len(srcs)
    na = len(after)
    lands = [lax.empty(((N_DEV,) + a.shape) if gather else a.shape, a.dtype) for a in srcs]

    def body(*refs):
        src, land = refs[:n], refs[n:2 * n]
        send_sems, recv_sems, local_sems = refs[2 * n + na:2 * n + na + 3]
        local, remote = _xfer_copies(src, land, send_sems, recv_sems, local_sems, gather)
        for cp in local + remote:
            cp.start()
        refs[-1][...] = jnp.zeros_like(refs[-1])

    nsem = n * (N_DEV - 1)
    out = pl.pallas_call(
        body, name=name,
        out_shape=(pltpu.SemaphoreType.DMA((nsem,)), pltpu.SemaphoreType.DMA((nsem,)), pltpu.SemaphoreType.DMA((n,)),
                   *[pltpu.HBM(a.shape, a.dtype) for a in srcs], *[pltpu.HBM(a.shape, a.dtype) for a in lands],
                   jax.ShapeDtypeStruct((8, LANES), F32)),
        in_specs=[_HBM] * (2 * n) + [pl.BlockSpec(memory_space=pl.ANY)] * na,
        out_specs=(_SEM, _SEM, _SEM, *[_HBM] * (2 * n), pl.BlockSpec(memory_space=pltpu.VMEM)),
        input_output_aliases={i: 3 + i for i in range(2 * n)},
        compiler_params=pltpu.CompilerParams(has_side_effects=_EFFECT))(
            *[pltpu.with_memory_space_constraint(a, pltpu.HBM) for a in srcs],
            *[pltpu.with_memory_space_constraint(a, pltpu.HBM) for a in lands], *after)
    return out[:3], list(out[3:3 + n]), list(out[3 + n:3 + 2 * n]), out[-1]


def _started(handle):
    return handle[3]


def _xfer_wait(handle, after, gather, name):
    sems, srcs, lands, _ = handle
    n = len(srcs)

    def body(*refs):
        src, land = refs[:n], refs[n:2 * n]
        send_sems, recv_sems, local_sems = refs[2 * n:2 * n + 3]
        local, remote = _xfer_copies(src, land, send_sems, recv_sems, local_sems, gather)
        for cp in local:
            cp.wait()
        for cp in remote:
            cp.wait_send()
            cp.wait_recv()

    out = pl.pallas_call(
        body, name=name,
        out_shape=(*[pltpu.HBM(a.shape, a.dtype) for a in srcs], *[pltpu.HBM(a.shape, a.dtype) for a in lands]),
        in_specs=[_HBM] * (2 * n) + [_SEM] * 3 + [pl.BlockSpec(memory_space=pl.ANY)] * len(after),
        out_specs=tuple([_HBM] * (2 * n)), input_output_aliases={i: i for i in range(2 * n)},
        compiler_params=pltpu.CompilerParams(has_side_effects=_EFFECT))(*srcs, *lands, *sems, *after)
    return list(out[n:])


def _cols_full(g):
    nd, r, c = g.shape
    return jnp.transpose(g, (1, 0, 2)).reshape(r, nd * c)


def _cols_split(full):
    r, n = full.shape
    return jnp.transpose(full.reshape(r, N_DEV, n // N_DEV), (1, 0, 2))


def _pack_b_in(w):
    qkv = 3 * MIX_W
    pad = jnp.zeros((w.shape[0], B_IN_PAD - w.shape[1]), w.dtype)
    return jnp.concatenate([w[:, :qkv], w[:, qkv + N_MIX_HEADS:], w[:, qkv:qkv + N_MIX_HEADS], pad], axis=1)


def _unpack_b_in(w):
    qkv = 3 * MIX_W
    return jnp.concatenate([w[:, :qkv], w[:, qkv + MEM_W:qkv + MEM_W + N_MIX_HEADS], w[:, qkv:qkv + MEM_W]], axis=1)


def _ffn_forward(x, xb, wgu, get_rest, tag, fused=True):
    if fused:
        wd4, gain, bias = get_rest(x)
        y, yb, gu, a, xh, rstd = _ffn_fwd_main(x, xb, wgu, wd4, gain, bias, f"{tag}_fwd_main")
    else:
        gu, a = _ffn_up(xb, wgu, f"{tag}_up")
        wd4, gain, bias = get_rest(a)
        y, yb, xh, rstd = _mm_res_ln(a, wd4, x, gain, bias, 0.5, f"{tag}_down_ln")
    return y, yb, (xb, gu, a, xh, rstd), wd4


def _ffn_backward(dy, saved, wgu, wd4, gain, tag, after=(), send=None):
    xb, gu, a, xh, rstd = saved
    s = xb.shape[0]
    nd, c, d = wgu.shape
    dx, dzb, dh, dgain, dbias = _ffn_bwd_main(dy, xh, rstd, gain, wd4, wgu, gu, f"{tag}_bwd_main", after,
                                               with_dx=send is None)
    dh = dh.reshape(nd, s, c)
    dwd = _mm_tn(a, dzb[None], f"{tag}_dwd").reshape(nd, wd4.shape[1] // 2, d)
    if send is not None:
        send("down", dwd, dgain, dbias)
    dwgu = _mm_tn(dh, xb[None], f"{tag}_dwgu")
    if send is not None:
        sent = send("gate_up", dwgu)
        dx = _mm_nt(dh, wgu, f"{tag}_dx", res=dx, w_rows_out=False, after=sent)
    return dx, dwgu, dwd, dgain, dbias


def _mixer_a_forward(x, xb, memb, w_in, w_kv, w_out, gain, bias, tabs):
    hb = _proj_rope(xb, w_in, tabs, 2 * MIX_W // LANES, "a_in", True)
    groups = [_band_fwd(hb, g, f"a_band_fwd{g}") for g in range(N_GROUPS)]
    oa, lt = _band_combine([o for o, _ in groups], [l for _, l in groups], "a_combine")
    kv = _mm_nn(memb, w_kv, BF, "a_mem_kv")
    om, lm = _mem_fwd(hb, 3 * MIX_W // LANES, kv, "a_mem_fwd")
    cat = jnp.concatenate([oa, om], axis=1)
    y, yb, xh, rstd = _mm_res_ln(cat[None], w_out[None], x, gain, bias, 1.0, "a_out_ln")
    return y, yb, (xb, hb, oa, lt, kv, lm, cat, xh, rstd)


def _mixer_a_backward(dy, saved, memb, w_in, w_kv, w_out, gain, tabs_neg, after=()):
    xb, hb, oa, lt, kv, lm, cat, xh, rstd = saved
    dz, dzb, dgain, dbias = _ln_bwd(dy, xh, rstd, gain, 1.0, "a_ln_bwd", after)
    dcat = _mm_nt(dzb[None], w_out[None], "a_dcat", out_dtype=BF)
    dw_out = _mm_tn(cat[None], dzb[None], "a_dwout")[0]
    dqm, dkm, dvm = _mem_bwd(hb, 3 * MIX_W // LANES, kv, dcat, cat, GROUP_W // LANES, lm, "a_mem_bwd")
    dkv = jnp.concatenate([dkm, dvm], axis=1).astype(BF)
    dw_kv = _mm_tn(memb[None], dkv[None], "a_dwkv")[0]
    grads = [_band_bwd(hb, dcat, oa, lt, g, f"a_band_bwd{g}") for g in range(N_GROUPS)]
    dhb = _rope_cast([grads[g][i] for i in range(3) for g in range(N_GROUPS)] + [dqm], tabs_neg,
                     2 * MIX_W // LANES, "a_rope_bwd")
    dw_in = _mm_tn(dhb[None], xb[None], "a_dwin")[0]
    dx = _mm_nt(dhb[None], w_in[None], "a_dx", res=dz, w_rows_out=False)
    return dx, dw_in, dw_kv, dw_out, dgain, dbias


def _pad_rows(t, rows):
    return jnp.concatenate([t, jnp.zeros((rows - t.shape[0], t.shape[1]), t.dtype)], axis=0)


def _pad_cols(t, cols):
    return jnp.concatenate([t, jnp.zeros((t.shape[0], cols - t.shape[1]), t.dtype)], axis=1)


def _mixer_b_forward(x, xb, memb, w_in, fbias, w_kv, w_out, gain, bias, tabs):
    s = x.shape[0]
    hb, f = _proj_rope(xb, w_in, tabs, 0, "b_in", False, tail_block=(3 * MIX_W + MEM_W) // LANES)
    f_t = _pad_rows(jnp.transpose(f[:, :N_MIX_HEADS]), 16)
    bias16 = _pad_rows(jnp.transpose(fbias), 16)
    c_t = _gate_fwd(f_t, bias16, "b_gate_fwd")
    c_t3 = c_t[:N_MIX_HEADS].reshape(N_MIX_HEADS // 2, 2, s)
    c_rep = jnp.broadcast_to(c_t[:N_MIX_HEADS, :, None], (N_MIX_HEADS, s, LANES))
    ob, lb = _foxt_fwd(hb, c_rep, c_t3, "b_fox_fwd")
    kv = _mm_nn(memb, w_kv, BF, "b_mem_kv")
    om, lm = _mem_fwd(hb, 3 * MIX_W // LANES, kv, "b_mem_fwd")
    cat = jnp.concatenate([ob, om], axis=1)
    y, yb, xh, rstd = _mm_res_ln(cat[None], w_out[None], x, gain, bias, 1.0, "b_out_ln")
    return y, yb, (xb, hb, f_t, bias16, c_rep, c_t3, lb, kv, lm, cat, xh, rstd)


def _mixer_b_backward(dy, saved, memb, w_in, w_kv, w_out, gain, tabs, after=()):
    xb, hb, f_t, bias16, c_rep, c_t3, lb, kv, lm, cat, xh, rstd = saved
    s = xb.shape[0]
    dz, dzb, dgain, dbias = _ln_bwd(dy, xh, rstd, gain, 1.0, "b_ln_bwd", after)
    dcat = _mm_nt(dzb[None], w_out[None], "b_dcat", out_dtype=BF)
    dw_out = _mm_tn(cat[None], dzb[None], "b_dwout")[0]
    dqm, dkm, dvm = _mem_bwd(hb, 3 * MIX_W // LANES, kv, dcat, cat, MIX_W // LANES, lm, "b_mem_bwd")
    dkv = jnp.concatenate([dkm, dvm], axis=1).astype(BF)
    dw_kv = _mm_tn(memb[None], dkv[None], "b_dwkv")[0]
    dsum = _foxt_dsum(hb, dcat, lb, c_rep, c_t3, "b_fox_dsum")
    dq, dk, dv, dc3 = _foxt_bwd(hb, dcat, dsum, lb, c_rep, c_t3, "b_fox_bwd")
    df_t, dfb = _gate_bwd(_pad_rows(dc3.reshape(N_MIX_HEADS, s), 16), f_t, bias16, "b_gate_bwd")
    df = _pad_cols(jnp.transpose(df_t[:N_MIX_HEADS]), B_IN_PAD - 3 * MIX_W - MEM_W)
    dhb = _rope_cast([dq, dk, dv, dqm, df], tabs, 0, "b_cast_bwd", transposed=(0, 1, 2))
    dw_in = _mm_tn(xb[None], dhb[None], "b_dwin")[0]
    dx = _mm_nt(dhb[None], w_in[None], "b_dx", res=dz)
    return dx, dw_in, jnp.transpose(dfb[:N_MIX_HEADS]), dw_kv, dw_out, dgain, dbias


def _stored(t, name):
    return jnp.transpose(t, (0, 2, 1)) if name in ROWS_OUT else t


GATHER_GROUPS = (
    (("ffn1_w_gate_up", 0),),
    (("ffn1_w_down", 0), ("ln_gain", None), ("ln_bias", None)),
    (("a_w_in", 0), ("a_w_out", 0), ("mem_w_kv", 0)),
    (("ffn2_w_gate_up", 0), ("ffn2_w_down", 0)),
    (("ffn1_w_gate_up", 1), ("ffn1_w_down", 1)),
    (("b_w_in", 0), ("b_w_out", 0), ("mem_w_kv", 1)),
    (("ffn2_w_gate_up", 1), ("ffn2_w_down", 1)),
)


def _group_shards(group, params):
    return [t if n in F32_COMM else _stored(t, n)[l].astype(BF) for (n, l), t in zip(group, params)]


def _weight_groups(w):
    return [_group_shards(grp, [w[n] for n, _ in grp]) for grp in GATHER_GROUPS]


def _local_step(x, mem, target, fbias, get_w, put_g):
    s, d = x.shape
    tabs = _rope_tables(s, 1.0)
    tabs_neg = _rope_tables(s, -1.0)
    memb = mem.astype(BF)
    saved, wl = [], []
    cur, curb = x, x.astype(BF)
    ln = []

    def down4(t):
        return t.reshape(N_DEV // 2, -1, d)

    for i in range(DEPTH):
        if i == 0:
            def first_rest(a):
                g = get_w(1, a)
                ln.extend(jnp.transpose(t, (1, 2, 0, 3)).reshape(DEPTH, 3, 1, d) for t in g[1:3])
                return down4(g[0]), ln[0][0, 0], ln[1][0, 0]

            wgu = get_w(0, cur)[0]
            cur, curb, s1, wd = _ffn_forward(cur, curb, wgu, first_rest, "l0_ffn1", fused=False)
        else:
            g = get_w(3 * i + 1, cur)
            wgu = g[0]
            cur, curb, s1, wd = _ffn_forward(cur, curb, wgu, lambda a, g=g: (down4(g[1]), ln[0][i, 0], ln[1][i, 0]),
                                             f"l{i}_ffn1")
        w1 = (wgu, wd)
        ln_g, ln_b = ln
        g = get_w(3 * i + 2, cur)
        if i == 0:
            wm = (g[0].reshape(-1, d), g[2].reshape(d, -1), _cols_full(g[1]))
            cur, curb, s2 = _mixer_a_forward(cur, curb, memb, wm[0], wm[1], wm[2], ln_g[i, 1], ln_b[i, 1], tabs)
        else:
            wm = (_pack_b_in(g[0].reshape(d, -1)), g[2].reshape(d, -1), g[1].reshape(d, -1))
            cur, curb, s2 = _mixer_b_forward(cur, curb, memb, wm[0], fbias, wm[1], wm[2], ln_g[i, 1], ln_b[i, 1],
                                             tabs)
        g = get_w(3 * i + 3, cur)
        cur, curb, s3, wd = _ffn_forward(cur, curb, g[0], lambda a, g=g: (down4(g[1]), ln_g[i, 2], ln_b[i, 2]),
                                         f"l{i}_ffn2")
        w3 = (g[0], wd)
        saved.append((s1, s2, s3))
        wl.append((w1, wm, w3))

    dy, loss = _loss_head(cur, target, "loss_head")

    dgs = [[None] * 3 for _ in range(DEPTH)]
    dbs = [[None] * 3 for _ in range(DEPTH)]
    sent = ()
    for i in reversed(range(DEPTH)):
        s1, s2, s3 = saved[i]
        w1, wm, w3 = wl[i]
        dy, dgu, dd, dgs[i][2], dbs[i][2] = _ffn_backward(dy, s3, w3[0], w3[1], ln_g[i, 2], f"l{i}_ffn2", sent)
        sent = put_g(3 * i + 2, [dgu, dd])
        if i == 0:
            dy, dw_in, dw_kv, dw_out, dgs[i][1], dbs[i][1] = _mixer_a_backward(
                dy, s2, memb, wm[0], wm[1], wm[2], ln_g[i, 1], tabs_neg, sent)
            sent = put_g(1, [dw_in.reshape(N_DEV, -1, d), _cols_split(dw_out),
                             dw_kv.reshape(N_DEV, d // N_DEV, -1)])
        else:
            dy, dw_in, dfb, dw_kv, dw_out, dgs[i][1], dbs[i][1] = _mixer_b_backward(
                dy, s2, memb, wm[0], wm[1], wm[2], ln_g[i, 1], tabs, sent)
            sent = put_g(4, [_unpack_b_in(dw_in).reshape(N_DEV, d // N_DEV, -1),
                             dw_out.reshape(N_DEV, d // N_DEV, -1), dw_kv.reshape(N_DEV, d // N_DEV, -1),
                             jnp.broadcast_to(dfb[None], (N_DEV,) + dfb.shape)])
        if i == 0:
            def send_last(kind, dw, dgain=None, dbias=None):
                if kind == "gate_up":
                    return put_g(6, [dw])
                dgs[0][0], dbs[0][0] = dgain, dbias
                ln_pieces = []
                for parts in (dgs, dbs):
                    t = jnp.concatenate([parts[a][b] for a in range(DEPTH) for b in range(3)], axis=0)
                    ln_pieces.append(jnp.transpose(t.reshape(DEPTH * 3, N_DEV, d // N_DEV), (1, 0, 2)))
                return put_g(0, [dw] + ln_pieces)

            dy = _ffn_backward(dy, s1, w1[0], w1[1], ln_g[i, 0], "l0_ffn1", sent, send_last)[0]
        else:
            dy, dgu, dd, dgs[i][0], dbs[i][0] = _ffn_backward(dy, s1, w1[0], w1[1], ln_g[i, 0], f"l{i}_ffn1", sent)
            sent = put_g(3, [dgu, dd])
    return loss, dy


WEIGHTS = ("ffn1_w_gate_up", "ffn1_w_down", "ffn2_w_gate_up", "ffn2_w_down", "ln_gain", "ln_bias", "mem_w_kv",
           "a_w_in", "a_w_out", "b_w_in", "b_forget_bias", "b_w_out")
F32_COMM = ("ln_gain", "ln_bias", "b_forget_bias")
ROWS_OUT = ("ffn1_w_gate_up", "ffn2_w_gate_up", "a_w_in")
GRAD_SLOTS = {
    "ffn1_w_gate_up": [(6, 0), (3, 0)], "ffn1_w_down": [(0, 0), (3, 1)],
    "ffn2_w_gate_up": [(2, 0), (5, 0)], "ffn2_w_down": [(2, 1), (5, 1)],
    "ln_gain": [(0, 1)], "ln_bias": [(0, 2)], "mem_w_kv": [(1, 2), (4, 2)],
    "a_w_in": [(1, 0)], "a_w_out": [(1, 1)], "b_w_in": [(4, 0)], "b_forget_bias": [(4, 3)], "b_w_out": [(4, 1)],
}


def kernel(x, mem, ffn1_w_gate_up, ffn1_w_down, ffn2_w_gate_up, ffn2_w_down, ln_gain, ln_bias, mem_w_kv, a_w_in, a_w_out, b_w_in, b_forget_bias, b_w_out, loss_target, m_ffn1_w_gate_up, m_ffn1_w_down, m_ffn2_w_gate_up, m_ffn2_w_down, m_ln_gain, m_ln_bias, m_mem_w_kv, m_a_w_in, m_a_w_out, m_b_w_in, m_b_forget_bias, m_b_w_out, v_ffn1_w_gate_up, v_ffn1_w_down, v_ffn2_w_gate_up, v_ffn2_w_down, v_ln_gain, v_ln_bias, v_mem_w_kv, v_a_w_in, v_a_w_out, v_b_w_in, v_b_forget_bias, v_b_w_out):
    w = dict(zip(WEIGHTS, (ffn1_w_gate_up, ffn1_w_down, ffn2_w_gate_up, ffn2_w_down, ln_gain, ln_bias, mem_w_kv,
                           a_w_in, a_w_out, b_w_in, b_forget_bias, b_w_out)))
    m = dict(zip(WEIGHTS, (m_ffn1_w_gate_up, m_ffn1_w_down, m_ffn2_w_gate_up, m_ffn2_w_down, m_ln_gain, m_ln_bias,
                           m_mem_w_kv, m_a_w_in, m_a_w_out, m_b_w_in, m_b_forget_bias, m_b_w_out)))
    v = dict(zip(WEIGHTS, (v_ffn1_w_gate_up, v_ffn1_w_down, v_ffn2_w_gate_up, v_ffn2_w_down, v_ln_gain, v_ln_bias,
                           v_mem_w_kv, v_a_w_in, v_a_w_out, v_b_w_in, v_b_forget_bias, v_b_w_out)))

    gathers = []
    for k, grp in enumerate(GATHER_GROUPS):
        params, behind = [w[n] for n, _ in grp], [_started(h) for h in gathers[-1:]]
        if behind:
            params, behind = lax.optimization_barrier((params, behind))
        gathers.append(_xfer_start(_group_shards(grp, params), True, f"gather{k}_start", behind))
    exchanges = {}

    def get_w(k, after):
        behind = [after] + ([_started(h) for h in gathers] if k == 0 else [])
        return _xfer_wait(gathers[k], behind, True, f"gather{k}_wait")

    def put_g(k, pieces):
        behind = [_started(exchanges[0])] if k == 6 else []
        exchanges[k] = _xfer_start(pieces, False, f"grads{k}_start", behind)
        return (_started(exchanges[k]),)

    loss, grad_x = _local_step(x[0], mem[0], loss_target[0], b_forget_bias, get_w, put_g)
    loss = lax.psum(loss[0, 0], ("x", "y", "c"))

    outs, landed = {}, {}

    def adamw(names):
        for n in names:
            contribs = [landed[g][j] for g, j in GRAD_SLOTS[n]]
            view = (len(contribs),) + contribs[0].shape[1:]
            shape = _stored(w[n], n).shape
            res = _reduce_adamw(contribs, *[_stored(t[n], n).reshape(view) for t in (w, m, v)], f"adamw_{n}")
            outs[n] = [_stored(t.reshape(shape), n) for t in res]
        return [outs[n][3] for n in names]

    after = [grad_x]
    for k in (5, 4, 3, 2, 1):
        landed[k] = _xfer_wait(exchanges[k], after, False, f"grads{k}_wait")
        after = [landed[k][0]]
    done = adamw(("ffn2_w_gate_up", "ffn2_w_down", "mem_w_kv", "a_w_in", "a_w_out", "b_w_in", "b_forget_bias",
                  "b_w_out"))
    landed[0] = _xfer_wait(exchanges[0], done, False, "grads0_wait")
    done = adamw(("ffn1_w_down", "ln_gain", "ln_bias"))
    landed[6] = _xfer_wait(exchanges[6], done, False, "grads6_wait")
    adamw(("ffn1_w_gate_up",))
    return (loss, grad_x[None], *[outs[n][0] for n in WEIGHTS], *[outs[n][1] for n in WEIGHTS],
            *[outs[n][2] for n in WEIGHTS], *[outs[n][3] for n in WEIGHTS])
```

```python
import functools

import jax
import jax.numpy as jnp
from jax import lax
from jax.experimental import pallas as pl
from jax.experimental.pallas import tpu as pltpu

F32 = jnp.float32
BF = jnp.bfloat16
MESH_ID = pl.DeviceIdType.MESH

N_DEV = 8
DEPTH = 2
HEAD_DIM = 64
LANES = 128
N_MIX_HEADS = 12
N_MEM_HEADS = 4
MIX_W = N_MIX_HEADS * HEAD_DIM
MEM_W = N_MEM_HEADS * HEAD_DIM
N_GROUPS = 3
GROUP_W = MIX_W // N_GROUPS
BLOCK = 128
BAND_SUB = 4
ROT_HALF = 8
ROPE_THETA = 500000.0
ALPHA = (2 * DEPTH) ** 0.25
LN_EPS = 1e-5
SCALE = HEAD_DIM ** -0.5
NEG = -1e30
B_IN_PAD = 2688
ADAM_LR, ADAM_B1, ADAM_B2, ADAM_EPS, ADAM_WD, ADAM_STEP = 0.001, 0.9, 0.999, 1e-08, 0.01, 10
VMEM_LIMIT = 56 * 1024 * 1024


def _cp(*sem):
    return pltpu.CompilerParams(dimension_semantics=sem, vmem_limit_bytes=VMEM_LIMIT)


def _dot(a, b):
    return jnp.dot(a, b, preferred_element_type=F32)


def _dot_nt(a, b):
    return lax.dot_general(a, b, (((1,), (1,)), ((), ())), preferred_element_type=F32)


def _dot_tn(a, b):
    return lax.dot_general(a, b, (((0,), (0,)), ((), ())), preferred_element_type=F32)


def _sigmoid(x):
    return 1.0 / (1.0 + jnp.exp(-x))


def _tile(n, cap=1024):
    if n <= cap:
        return n
    best = LANES
    for t in range(LANES, cap + 1, LANES):
        if n % t == 0:
            best = t
    return best


def _rows(s, cap=512):
    return s if s <= cap else cap


def _mm_nn(a, b, out_dtype, name, b_rows_out=False):
    m, k = a.shape
    n = b.shape[0] if b_rows_out else b.shape[1]
    tm, tn = _rows(m), _tile(n)

    def body(a_ref, b_ref, o_ref):
        prod = _dot_nt(a_ref[...], b_ref[...]) if b_rows_out else _dot(a_ref[...], b_ref[...])
        o_ref[...] = prod.astype(o_ref.dtype)

    b_spec = (pl.BlockSpec((tn, k), lambda j, i: (j, 0)) if b_rows_out
              else pl.BlockSpec((k, tn), lambda j, i: (0, j)))
    return pl.pallas_call(
        body, name=name, grid=(n // tn, m // tm),
        in_specs=[pl.BlockSpec((tm, k), lambda j, i: (i, 0)), b_spec],
        out_specs=pl.BlockSpec((tm, tn), lambda j, i: (i, j)),
        out_shape=jax.ShapeDtypeStruct((m, n), out_dtype),
        compiler_params=_cp("parallel", "parallel"))(a, b)


def _resident(shape, index_map):
    return pl.BlockSpec(shape, index_map, pipeline_mode=pl.Buffered(1))


def _mm_tn(a, b, name, out_dtype=BF):
    na, s, m = a.shape
    nb, _, n = b.shape
    no = max(na, nb)
    tm, tn = _tile(m), _tile(n)

    def body(a_ref, b_ref, o_ref):
        o_ref[...] = _dot_tn(a_ref[...], b_ref[...]).astype(o_ref.dtype)

    def spec(nbatch, width, tile, index_map):
        fixed = nbatch == 1 and width == tile
        return _resident((None, s, tile), index_map) if fixed else pl.BlockSpec((None, s, tile), index_map)

    return pl.pallas_call(
        body, name=name, grid=(no, m // tm, n // tn),
        in_specs=[spec(na, m, tm, lambda j, r, c: (j if na > 1 else 0, 0, r)),
                  spec(nb, n, tn, lambda j, r, c: (j if nb > 1 else 0, 0, c))],
        out_specs=pl.BlockSpec((None, tm, tn), lambda j, r, c: (j, r, c)),
        out_shape=jax.ShapeDtypeStruct((no, m, n), out_dtype),
        compiler_params=_cp("parallel", "parallel", "parallel"))(a, b)


def _mm_nt(dh, w, name, res=None, out_dtype=F32, w_rows_out=True, after=()):
    nc, s, kc = dh.shape
    d = w.shape[1] if w_rows_out else w.shape[2]
    ts = _rows(s)
    has_res = res is not None
    mm = _dot_nt if w_rows_out else _dot

    def body(*refs):
        o_ref = refs[-1]
        dh_ref, w_ref = refs[:2]
        if has_res:
            r_ref = refs[2]
        out = mm(dh_ref[0], w_ref[0])
        for j in range(1, nc):
            out = out + mm(dh_ref[j], w_ref[j])
        if has_res:
            out = out + ALPHA * r_ref[...]
        o_ref[...] = out.astype(o_ref.dtype)

    in_specs = [pl.BlockSpec((nc, ts, kc), lambda i: (0, i, 0)), _resident(w.shape, lambda i: (0, 0, 0))]
    args = [dh, w]
    if has_res:
        in_specs.append(pl.BlockSpec((ts, d), lambda i: (i, 0)))
        args.append(res)
    in_specs += [pl.BlockSpec(memory_space=pl.ANY)] * len(after)
    args += list(after)
    return pl.pallas_call(
        body, name=name, grid=(s // ts,), in_specs=in_specs,
        out_specs=pl.BlockSpec((ts, d), lambda i: (i, 0)),
        out_shape=jax.ShapeDtypeStruct((s, d), out_dtype),
        compiler_params=_cp("parallel"))(*args)


def _mm_res_ln(a, w, x, gain, bias, fscale, name):
    nc, s, kc = a.shape
    d = w.shape[2]
    ts = _rows(s)

    def body(a_ref, w_ref, x_ref, g_ref, b_ref, y_ref, yb_ref, xh_ref, r_ref):
        f = _dot(a_ref[0], w_ref[0])
        for j in range(1, nc):
            f = f + _dot(a_ref[j], w_ref[j])
        z = ALPHA * x_ref[...] + fscale * f
        mu = jnp.mean(z, axis=-1, keepdims=True)
        zc = z - mu
        var = jnp.mean(zc * zc, axis=-1, keepdims=True)
        r = lax.rsqrt(var + LN_EPS)
        xh = zc * r
        y = xh * g_ref[...] + b_ref[...]
        y_ref[...] = y
        yb_ref[...] = y.astype(BF)
        xh_ref[...] = xh
        r_ref[...] = r

    row = pl.BlockSpec((ts, d), lambda i: (i, 0))
    vec = pl.BlockSpec((1, d), lambda i: (0, 0))
    return pl.pallas_call(
        body, name=name, grid=(s // ts,),
        in_specs=[pl.BlockSpec((nc, ts, kc), lambda i: (0, i, 0)), _resident((nc, kc, d), lambda i: (0, 0, 0)),
                  row, vec, vec],
        out_specs=[row, row, row, pl.BlockSpec((ts, 1), lambda i: (i, 0))],
        out_shape=[jax.ShapeDtypeStruct((s, d), F32), jax.ShapeDtypeStruct((s, d), BF),
                   jax.ShapeDtypeStruct((s, d), F32), jax.ShapeDtypeStruct((s, 1), F32)],
        compiler_params=_cp("parallel"))(a, w, x, gain, bias)


def _ln_bwd(dy, xh, rstd, gain, fscale, name, after=()):
    s, d = dy.shape
    ts = _rows(s)
    na = len(after)

    def body(*refs):
        dy_ref, xh_ref, r_ref, g_ref = refs[:4]
        dz_ref, dzb_ref, dg_ref, db_ref = refs[4 + na:]
        i = pl.program_id(0)
        dyv = dy_ref[...]
        xhv = xh_ref[...]
        dxh = dyv * g_ref[...]
        m1 = jnp.mean(dxh, axis=-1, keepdims=True)
        m2 = jnp.mean(dxh * xhv, axis=-1, keepdims=True)
        dz = r_ref[...] * (dxh - m1 - xhv * m2)
        dz_ref[...] = dz
        dzb_ref[...] = (fscale * dz).astype(BF)

        @pl.when(i == 0)
        def _():
            dg_ref[...] = jnp.zeros_like(dg_ref)
            db_ref[...] = jnp.zeros_like(db_ref)

        dg_ref[...] += jnp.sum(dyv * xhv, axis=0, keepdims=True)
        db_ref[...] += jnp.sum(dyv, axis=0, keepdims=True)

    row = pl.BlockSpec((ts, d), lambda i: (i, 0))
    vec = pl.BlockSpec((1, d), lambda i: (0, 0))
    return pl.pallas_call(
        body, name=name, grid=(s // ts,),
        in_specs=[row, row, pl.BlockSpec((ts, 1), lambda i: (i, 0)), vec] + [pl.BlockSpec(memory_space=pl.ANY)] * na,
        out_specs=[row, row, vec, vec],
        out_shape=[jax.ShapeDtypeStruct((s, d), F32), jax.ShapeDtypeStruct((s, d), BF),
                   jax.ShapeDtypeStruct((1, d), F32), jax.ShapeDtypeStruct((1, d), F32)],
        compiler_params=_cp("arbitrary"))(dy, xh, rstd, gain, *after)


def _ffn_up(xb, wgu, name):
    s, d = xb.shape
    c = wgu.shape[1]
    nch = wgu.shape[0] // 2
    ts = _rows(s, 1024)
    w4 = wgu.reshape(2, nch, c, d)

    def body(x_ref, w_ref, gu_ref, a_ref):
        x = x_ref[...]
        g = _dot_nt(x, w_ref[0])
        u = _dot_nt(x, w_ref[1])
        sg = _sigmoid(g)
        t = g * sg
        gu_ref[0] = (u * (sg * (1.0 + g - t))).astype(BF)
        gu_ref[1] = t.astype(BF)
        a_ref[...] = (t * u).astype(BF)

    return pl.pallas_call(
        body, name=name, grid=(nch, s // ts),
        in_specs=[pl.BlockSpec((ts, d), lambda j, i: (i, 0)),
                  pl.BlockSpec((2, None, c, d), lambda j, i: (0, j, 0, 0))],
        out_specs=[pl.BlockSpec((2, None, ts, c), lambda j, i: (0, j, i, 0)),
                   pl.BlockSpec((None, ts, c), lambda j, i: (j, i, 0))],
        out_shape=[jax.ShapeDtypeStruct((2, nch, s, c), BF), jax.ShapeDtypeStruct((nch, s, c), BF)],
        compiler_params=_cp("parallel", "parallel"))(xb, w4)


def _ffn_fwd_main(x, xb, wgu, wd4, gain, bias, name):
    s, d = x.shape
    nch, c = wd4.shape[0], wd4.shape[1]
    ts = _rows(s, 256)

    def body(x_ref, xb_ref, wgu_ref, wd_ref, g_ref, b_ref, y_ref, yb_ref, gu_ref, a_ref, xh_ref, r_ref):
        xbv = xb_ref[...]
        f = jnp.zeros((ts, d), F32)
        for j in range(nch):
            g = _dot_nt(xbv, wgu_ref[j])
            u = _dot_nt(xbv, wgu_ref[nch + j])
            sg = _sigmoid(g)
            t = g * sg
            gu_ref[0, j] = (u * (sg * (1.0 + g - t))).astype(BF)
            gu_ref[1, j] = t.astype(BF)
            act = (t * u).astype(BF)
            a_ref[j] = act
            f = f + _dot(act, wd_ref[j])
        z = ALPHA * x_ref[...] + 0.5 * f
        mu = jnp.mean(z, axis=-1, keepdims=True)
        zc = z - mu
        var = jnp.mean(zc * zc, axis=-1, keepdims=True)
        r = lax.rsqrt(var + LN_EPS)
        xh = zc * r
        y = xh * g_ref[...] + b_ref[...]
        y_ref[...] = y
        yb_ref[...] = y.astype(BF)
        xh_ref[...] = xh
        r_ref[...] = r

    row = pl.BlockSpec((ts, d), lambda i: (i, 0))
    vec = pl.BlockSpec((1, d), lambda i: (0, 0))
    return pl.pallas_call(
        body, name=name, grid=(s // ts,),
        in_specs=[row, row, _resident(wgu.shape, lambda i: (0, 0, 0)), _resident(wd4.shape, lambda i: (0, 0, 0)),
                  vec, vec],
        out_specs=[row, row, pl.BlockSpec((2, nch, ts, c), lambda i: (0, 0, i, 0)),
                   pl.BlockSpec((nch, ts, c), lambda i: (0, i, 0)), row, pl.BlockSpec((ts, 1), lambda i: (i, 0))],
        out_shape=[jax.ShapeDtypeStruct((s, d), F32), jax.ShapeDtypeStruct((s, d), BF),
                   jax.ShapeDtypeStruct((2, nch, s, c), BF), jax.ShapeDtypeStruct((nch, s, c), BF),
                   jax.ShapeDtypeStruct((s, d), F32), jax.ShapeDtypeStruct((s, 1), F32)],
        compiler_params=_cp("parallel"))(x, xb, wgu, wd4, gain, bias)


def _ffn_bwd_main(dy, xh, rstd, gain, wd4, wgu, gu, name, after=(), with_dx=True):
    s, d = dy.shape
    nch, c = wd4.shape[0], wd4.shape[1]
    ts = _rows(s, 256)
    na = len(after)

    def body(*refs):
        dy_ref, xh_ref, r_ref, g_ref, wd_ref, wgu_ref, gu_ref = refs[:7]
        dx_ref, dzb_ref, dh_ref, dg_ref, db_ref = refs[7 + na:]
        i = pl.program_id(0)
        dyv = dy_ref[...]
        xhv = xh_ref[...]
        dxh = dyv * g_ref[...]
        m1 = jnp.mean(dxh, axis=-1, keepdims=True)
        m2 = jnp.mean(dxh * xhv, axis=-1, keepdims=True)
        dz = r_ref[...] * (dxh - m1 - xhv * m2)
        dzb = (0.5 * dz).astype(BF)
        dzb_ref[...] = dzb

        @pl.when(i == 0)
        def _():
            dg_ref[...] = jnp.zeros_like(dg_ref)
            db_ref[...] = jnp.zeros_like(db_ref)

        dg_ref[...] += jnp.sum(dyv * xhv, axis=0, keepdims=True)
        db_ref[...] += jnp.sum(dyv, axis=0, keepdims=True)

        dx = ALPHA * dz if with_dx else dz
        for j in range(nch):
            da = _dot_nt(dzb, wd_ref[j])
            dgate = (da * gu_ref[0, j].astype(F32)).astype(BF)
            dup = (da * gu_ref[1, j].astype(F32)).astype(BF)
            dh_ref[0, j] = dgate
            dh_ref[1, j] = dup
            if with_dx:
                dx = dx + _dot(dgate, wgu_ref[j]) + _dot(dup, wgu_ref[nch + j])
        dx_ref[...] = dx

    row = pl.BlockSpec((ts, d), lambda i: (i, 0))
    vec = pl.BlockSpec((1, d), lambda i: (0, 0))
    act = pl.BlockSpec((2, nch, ts, c), lambda i: (0, 0, i, 0))
    return pl.pallas_call(
        body, name=name, grid=(s // ts,),
        in_specs=[row, row, pl.BlockSpec((ts, 1), lambda i: (i, 0)), vec,
                  _resident(wd4.shape, lambda i: (0, 0, 0)), _resident(wgu.shape, lambda i: (0, 0, 0)), act]
                 + [pl.BlockSpec(memory_space=pl.ANY)] * na,
        out_specs=[row, row, act, vec, vec],
        out_shape=[jax.ShapeDtypeStruct((s, d), F32), jax.ShapeDtypeStruct((s, d), BF),
                   jax.ShapeDtypeStruct((2, nch, s, c), BF),
                   jax.ShapeDtypeStruct((1, d), F32), jax.ShapeDtypeStruct((1, d), F32)],
        compiler_params=_cp("arbitrary"))(dy, xh, rstd, gain, wd4, wgu, gu, *after)


def _rope_tables(s, sign):
    pos = jnp.arange(s, dtype=F32)
    inv_freq = 1.0 / (ROPE_THETA ** (jnp.arange(ROT_HALF, dtype=F32) / ROT_HALF))
    ang = pos[:, None] * inv_freq[None, :]
    cos, sin = jnp.cos(ang), jnp.sin(ang) * sign
    one = jnp.ones((s, HEAD_DIM - 2 * ROT_HALF), F32)
    zero = jnp.zeros((s, HEAD_DIM - 2 * ROT_HALF), F32)
    zh = jnp.zeros((s, ROT_HALF), F32)
    cos_f = jnp.concatenate([cos, cos, one], axis=1)
    sin_a = jnp.concatenate([-sin, zh, zero], axis=1)
    sin_b = jnp.concatenate([zh, sin, zero], axis=1)
    rep = LANES // HEAD_DIM
    return tuple(jnp.tile(t, (1, rep)) for t in (cos_f, sin_a, sin_b))


def _rope(t, c_ref, sa_ref, sb_ref):
    return (t * c_ref[...] + pltpu.roll(t, LANES - ROT_HALF, 1) * sa_ref[...]
            + pltpu.roll(t, ROT_HALF, 1) * sb_ref[...])


def _proj_rope(xb, w, tabs, n_rope, name, w_rows_out, tail_block=None):
    s, d = xb.shape
    n = w.shape[0] if w_rows_out else w.shape[1]
    tm = _rows(s, 256)
    has_tail = tail_block is not None

    def body(x_ref, w_ref, c_ref, sa_ref, sb_ref, o_ref, *tail_ref):
        h = (_dot_nt if w_rows_out else _dot)(x_ref[...], w_ref[...])
        for cb in range(n // LANES):
            t = h[:, cb * LANES:(cb + 1) * LANES]
            if cb < n_rope:
                t = _rope(t, c_ref, sa_ref, sb_ref)
            o_ref[:, cb * LANES:(cb + 1) * LANES] = t.astype(BF)
        if has_tail:
            tail_ref[0][...] = h[:, tail_block * LANES:(tail_block + 1) * LANES]

    tab = pl.BlockSpec((tm, LANES), lambda i: (i, 0))
    out_specs = [pl.BlockSpec((tm, n), lambda i: (i, 0))]
    out_shape = [jax.ShapeDtypeStruct((s, n), BF)]
    if has_tail:
        out_specs.append(tab)
        out_shape.append(jax.ShapeDtypeStruct((s, LANES), F32))
    res = pl.pallas_call(
        body, name=name, grid=(s // tm,),
        in_specs=[pl.BlockSpec((tm, d), lambda i: (i, 0)), _resident(w.shape, lambda i: (0, 0)), tab, tab, tab],
        out_specs=out_specs, out_shape=out_shape, compiler_params=_cp("parallel"))(xb, w, *tabs)
    return res if has_tail else res[0]


def _rope_cast(parts, tabs, n_rope, name, transposed=()):
    s = tabs[0].shape[0]
    flip = [i in transposed for i in range(len(parts))]
    widths = [p.shape[0] if f else p.shape[1] for p, f in zip(parts, flip)]
    n = sum(widths)
    npart = len(parts)
    ts = _rows(s, 256)

    def body(*refs):
        part_refs = refs[:npart]
        c_ref, sa_ref, sb_ref, o_ref = refs[npart:]
        col = 0
        for ref, w, f in zip(part_refs, widths, flip):
            for j in range(w // LANES):
                if f:
                    t = jnp.transpose(ref[j * LANES:(j + 1) * LANES, :])
                else:
                    t = ref[:, j * LANES:(j + 1) * LANES]
                if col < n_rope:
                    t = _rope(t, c_ref, sa_ref, sb_ref)
                o_ref[:, col * LANES:(col + 1) * LANES] = t.astype(BF)
                col += 1

    tab = pl.BlockSpec((ts, LANES), lambda i: (i, 0))
    return pl.pallas_call(
        body, name=name, grid=(s // ts,),
        in_specs=[pl.BlockSpec((w, ts), lambda i: (0, i)) if f else pl.BlockSpec((ts, w), lambda i: (i, 0))
                  for w, f in zip(widths, flip)] + [tab, tab, tab],
        out_specs=pl.BlockSpec((ts, n), lambda i: (i, 0)),
        out_shape=jax.ShapeDtypeStruct((s, n), BF),
        compiler_params=_cp("parallel"))(*parts, *tabs)


def _head_masks():
    lane = lax.broadcasted_iota(jnp.int32, (1, LANES), 1)
    return [lane < HEAD_DIM, lane >= HEAD_DIM]


def _sel(mask, v):
    return jnp.where(mask, v, jnp.zeros_like(v))


def _pick(mask, wide, fill):
    return jnp.max(jnp.where(mask, wide, fill), axis=1, keepdims=True)


def _band_masks(has_other, prev):
    qi = lax.broadcasted_iota(jnp.int32, (BLOCK, BLOCK), 0)
    kj = lax.broadcasted_iota(jnp.int32, (BLOCK, BLOCK), 1)
    if prev:
        return kj >= qi + jnp.where(has_other, 0, BLOCK)
    return kj <= qi


class _BandView:
    def __init__(self, s, g):
        self.r = 4 ** g
        self.nl = s // self.r
        self.nblk = self.nl // BLOCK
        self.nsub = min(BAND_SUB, self.nblk)
        self.tile = self.nsub * BLOCK
        self.grid = (self.r, GROUP_W // LANES, self.nblk // self.nsub)

    def view(self, a):
        return a.reshape(self.nl, self.r * a.shape[1])

    def qkv(self, hb, g):
        npair = MIX_W // LANES
        offs = [i * npair + g * GROUP_W // LANES for i in range(3)]
        if self.r == 1:
            return [hb] * 3, hb.shape[1], offs
        return [self.view(hb[:, o * LANES:o * LANES + GROUP_W]) for o in offs], GROUP_W, [0, 0, 0]

    def specs(self, width, off):
        nb, nsub, last = width // LANES, self.nsub, self.nblk - 1

        def col(rho, p):
            return rho * nb + off + p

        return (pl.BlockSpec((self.tile, LANES), lambda rho, p, t: (t, col(rho, p))),
                pl.BlockSpec((BLOCK, LANES), lambda rho, p, t: (jnp.maximum(t * nsub - 1, 0), col(rho, p))),
                pl.BlockSpec((BLOCK, LANES), lambda rho, p, t: (jnp.minimum(t * nsub + nsub, last), col(rho, p))))


def _band_fwd(hb, g, name):
    s, n = hb.shape
    bv = _BandView(s, g)
    nsub = bv.nsub
    npair = MIX_W // LANES

    def body(q_ref, kc_ref, kp_ref, vc_ref, vp_ref, o_ref, l_ref):
        t = pl.program_id(2)
        mc = _band_masks(None, False)
        hm = _head_masks()
        for i in range(nsub):
            rows = slice(i * BLOCK, (i + 1) * BLOCK)
            has_prev = t > 0 if i == 0 else True
            mp = _band_masks(has_prev, True)
            q, kc, vc = q_ref[rows, :], kc_ref[rows, :], vc_ref[rows, :]
            if i == 0:
                kp, vp = kp_ref[...], vp_ref[...]
            else:
                prev = slice((i - 1) * BLOCK, i * BLOCK)
                kp, vp = kc_ref[prev, :], vc_ref[prev, :]
            o = jnp.zeros((BLOCK, LANES), F32)
            lse_w = jnp.zeros((BLOCK, LANES), F32)
            for h in range(2):
                qh = _sel(hm[h], q)
                sc = jnp.where(mc, _dot_nt(qh, kc) * SCALE, NEG)
                sp = jnp.where(mp, _dot_nt(qh, kp) * SCALE, NEG)
                m = jnp.maximum(jnp.max(sc, axis=1, keepdims=True), jnp.max(sp, axis=1, keepdims=True))
                pc = jnp.exp(sc - m)
                pp = jnp.exp(sp - m)
                l = jnp.sum(pc, axis=1, keepdims=True) + jnp.sum(pp, axis=1, keepdims=True)
                oh = _dot(pc.astype(BF), _sel(hm[h], vc)) + _dot(pp.astype(BF), _sel(hm[h], vp))
                o = o + oh / l
                lse_w = jnp.where(hm[h], m + jnp.log(l), lse_w)
            o_ref[rows, :] = o
            l_ref[rows, :] = lse_w

    (qv, kv_, vv), width, (qo, ko, vo) = bv.qkv(hb, g)
    q_cur, _, _ = bv.specs(width, qo)
    k_cur, k_prv, _ = bv.specs(width, ko)
    v_cur, v_prv, _ = bv.specs(width, vo)
    out_spec = bv.specs(GROUP_W, 0)[0]
    out = jax.ShapeDtypeStruct((bv.nl, bv.r * GROUP_W), F32)
    o, l = pl.pallas_call(
        body, name=name, grid=bv.grid,
        in_specs=[q_cur, k_cur, k_prv, v_cur, v_prv], out_specs=[out_spec, out_spec], out_shape=[out, out],
        compiler_params=_cp("parallel", "parallel", "parallel"))(qv, kv_, kv_, vv, vv)
    return o.reshape(s, GROUP_W), l.reshape(s, GROUP_W)


def _band_combine(os, ls, name):
    ng = len(os)
    s, w = os[0].shape
    ts = _rows(s)

    def body(*refs):
        o_refs, l_refs = refs[:ng], refs[ng:2 * ng]
        oa_ref, lt_ref = refs[2 * ng:]
        lv = [r[...] for r in l_refs]
        m = functools.reduce(jnp.maximum, lv)
        es = [jnp.exp(l - m) for l in lv]
        den = functools.reduce(lambda a, b: a + b, es)
        num = functools.reduce(lambda a, b: a + b, [es[g] * o_refs[g][...] for g in range(ng)])
        oa_ref[...] = (num / den).astype(BF)
        lt_ref[...] = m + jnp.log(den)

    blk = pl.BlockSpec((ts, w), lambda i: (i, 0))
    return pl.pallas_call(
        body, name=name, grid=(s // ts,), in_specs=[blk] * (2 * ng), out_specs=[blk, blk],
        out_shape=[jax.ShapeDtypeStruct((s, w), BF), jax.ShapeDtypeStruct((s, w), F32)],
        compiler_params=_cp("parallel"))(*os, *ls)


def _band_bwd(hb, dcat, oa, lt, g, name):
    s, n = hb.shape
    bv = _BandView(s, g)
    nsub = bv.nsub
    npair = MIX_W // LANES
    ntile = bv.grid[2]

    def body(q_ref, qn_ref, kc_ref, kp_ref, vc_ref, vp_ref, do_ref, don_ref, oa_ref, oan_ref, lt_ref, ltn_ref,
             dq_ref, dk_ref, dv_ref):
        t = pl.program_id(2)
        mc = _band_masks(None, False)
        hm = _head_masks()

        def block(ref, edge_ref, i):
            if i < 0 or i >= nsub:
                return edge_ref[...]
            return ref[i * BLOCK:(i + 1) * BLOCK, :]

        for i in range(nsub):
            mp = _band_masks(t > 0 if i == 0 else True, True)
            mn = _band_masks(t < ntile - 1 if i == nsub - 1 else True, True)
            q, qn = block(q_ref, None, i), block(q_ref, qn_ref, i + 1)
            kc, kp = block(kc_ref, None, i), block(kc_ref, kp_ref, i - 1)
            vc, vp = block(vc_ref, None, i), block(vc_ref, vp_ref, i - 1)
            do, don = block(do_ref, None, i), block(do_ref, don_ref, i + 1)
            dd = do.astype(F32) * block(oa_ref, None, i).astype(F32)
            ddn = don.astype(F32) * block(oa_ref, oan_ref, i + 1).astype(F32)
            lt, ltn = block(lt_ref, None, i), block(lt_ref, ltn_ref, i + 1)
            dq = jnp.zeros((BLOCK, LANES), F32)
            dk = jnp.zeros((BLOCK, LANES), F32)
            dv = jnp.zeros((BLOCK, LANES), F32)
            for h in range(2):
                qh, doh = _sel(hm[h], q), _sel(hm[h], do)
                qnh, donh = _sel(hm[h], qn), _sel(hm[h], don)
                kch, kph = _sel(hm[h], kc), _sel(hm[h], kp)
                lse = _pick(hm[h], lt, NEG)
                lsen = _pick(hm[h], ltn, NEG)
                dsum = jnp.sum(_sel(hm[h], dd), axis=1, keepdims=True)
                dsumn = jnp.sum(_sel(hm[h], ddn), axis=1, keepdims=True)
                pc = jnp.exp(jnp.where(mc, _dot_nt(qh, kc) * SCALE, NEG) - lse)
                pp = jnp.exp(jnp.where(mp, _dot_nt(qh, kp) * SCALE, NEG) - lse)
                dsc = pc * (_dot_nt(doh, vc) - dsum)
                dsp = pp * (_dot_nt(doh, vp) - dsum)
                dq = dq + SCALE * (_dot(dsc.astype(BF), kch) + _dot(dsp.astype(BF), kph))
                pn = jnp.exp(jnp.where(mn, _dot_nt(qnh, kc) * SCALE, NEG) - lsen)
                dsn = pn * (_dot_nt(donh, vc) - dsumn)
                dk = dk + SCALE * (_dot_tn(dsc.astype(BF), qh) + _dot_tn(dsn.astype(BF), qnh))
                dv = dv + _dot_tn(pc.astype(BF), doh) + _dot_tn(pn.astype(BF), donh)
            rows = slice(i * BLOCK, (i + 1) * BLOCK)
            dq_ref[rows, :] = dq
            dk_ref[rows, :] = dk
            dv_ref[rows, :] = dv

    (qv, kv_, vv), width, (qo, ko, vo) = bv.qkv(hb, g)
    q_cur, _, q_nxt = bv.specs(width, qo)
    k_cur, k_prv, _ = bv.specs(width, ko)
    v_cur, v_prv, _ = bv.specs(width, vo)
    w_cur, _, w_nxt = bv.specs(GROUP_W, 0)
    out = jax.ShapeDtypeStruct((bv.nl, bv.r * GROUP_W), F32)
    dv_, ov, lv = bv.view(dcat[:, :GROUP_W]), bv.view(oa), bv.view(lt)
    res = pl.pallas_call(
        body, name=name, grid=bv.grid,
        in_specs=[q_cur, q_nxt, k_cur, k_prv, v_cur, v_prv, w_cur, w_nxt, w_cur, w_nxt, w_cur, w_nxt],
        out_specs=[w_cur, w_cur, w_cur], out_shape=[out, out, out],
        compiler_params=_cp("parallel", "parallel", "parallel"))(
            qv, qv, kv_, kv_, vv, vv, dv_, dv_, ov, ov, lv, lv)
    return [t.reshape(s, GROUP_W) for t in res]


def _mem_fwd(hb, q_blk0, kv, name):
    s = hb.shape[0]
    m = kv.shape[0]
    tq = _rows(s)
    npair = MEM_W // LANES

    def body(q_ref, k_ref, v_ref, o_ref, l_ref):
        q, k, v = q_ref[...], k_ref[...], v_ref[...]
        hm = _head_masks()
        o = jnp.zeros((tq, LANES), F32)
        lse_w = jnp.zeros((tq, LANES), F32)
        for h in range(2):
            sc = _dot_nt(_sel(hm[h], q), k) * SCALE
            mx = jnp.max(sc, axis=1, keepdims=True)
            p = jnp.exp(sc - mx)
            l = jnp.sum(p, axis=1, keepdims=True)
            o = o + _dot(p.astype(BF), _sel(hm[h], v)) / l
            lse_w = jnp.where(hm[h], mx + jnp.log(l), lse_w)
        o_ref[...] = o.astype(BF)
        l_ref[...] = lse_w

    blk = pl.BlockSpec((tq, LANES), lambda p, i: (i, p))
    return pl.pallas_call(
        body, name=name, grid=(npair, s // tq),
        in_specs=[pl.BlockSpec((tq, LANES), lambda p, i: (i, q_blk0 + p)),
                  pl.BlockSpec((m, LANES), lambda p, i: (0, p)),
                  pl.BlockSpec((m, LANES), lambda p, i: (0, npair + p))],
        out_specs=[blk, blk],
        out_shape=[jax.ShapeDtypeStruct((s, MEM_W), BF), jax.ShapeDtypeStruct((s, MEM_W), F32)],
        compiler_params=_cp("parallel", "parallel"))(hb, kv, kv)


def _mem_bwd(hb, q_blk0, kv, dcat, cat, o_blk0, lse, name):
    s = hb.shape[0]
    m = kv.shape[0]
    tq = _rows(s)
    npair = MEM_W // LANES

    def body(q_ref, k_ref, v_ref, do_ref, o_ref, l_ref, dq_ref, dk_ref, dv_ref):
        i = pl.program_id(1)

        @pl.when(i == 0)
        def _():
            dk_ref[...] = jnp.zeros_like(dk_ref)
            dv_ref[...] = jnp.zeros_like(dv_ref)

        q, k, v, do = q_ref[...], k_ref[...], v_ref[...], do_ref[...]
        dd = do.astype(F32) * o_ref[...].astype(F32)
        lt = l_ref[...]
        hm = _head_masks()
        dq = jnp.zeros((tq, LANES), F32)
        dk = jnp.zeros((m, LANES), F32)
        dv = jnp.zeros((m, LANES), F32)
        for h in range(2):
            qh, doh = _sel(hm[h], q), _sel(hm[h], do)
            p = jnp.exp(_dot_nt(qh, k) * SCALE - _pick(hm[h], lt, NEG))
            ds = p * (_dot_nt(doh, v) - jnp.sum(_sel(hm[h], dd), axis=1, keepdims=True))
            dq = dq + SCALE * _dot(ds.astype(BF), _sel(hm[h], k))
            dk = dk + SCALE * _dot_tn(ds.astype(BF), qh)
            dv = dv + _dot_tn(p.astype(BF), doh)
        dq_ref[...] = dq
        dk_ref[...] += dk
        dv_ref[...] += dv

    row = pl.BlockSpec((tq, LANES), lambda p, i: (i, p))
    orow = pl.BlockSpec((tq, LANES), lambda p, i: (i, o_blk0 + p))
    acc = pl.BlockSpec((m, LANES), lambda p, i: (0, p))
    return pl.pallas_call(
        body, name=name, grid=(npair, s // tq),
        in_specs=[pl.BlockSpec((tq, LANES), lambda p, i: (i, q_blk0 + p)),
                  pl.BlockSpec((m, LANES), lambda p, i: (0, p)),
                  pl.BlockSpec((m, LANES), lambda p, i: (0, npair + p)), orow, orow, row],
        out_specs=[row, acc, acc],
        out_shape=[jax.ShapeDtypeStruct((s, MEM_W), F32), jax.ShapeDtypeStruct((m, MEM_W), F32),
                   jax.ShapeDtypeStruct((m, MEM_W), F32)],
        compiler_params=_cp("parallel", "arbitrary"))(hb, kv, kv, dcat, cat, lse)


def _gate_fwd(f_t, bias, name):
    hp, s = f_t.shape
    nblk = s // LANES

    def body(f_ref, b_ref, c_ref):
        lane = lax.broadcasted_iota(jnp.int32, (hp, LANES), 1)

        def step(i, carry):
            off = pl.multiple_of(i * LANES, LANES)
            x = f_ref[:, pl.ds(off, LANES)] + b_ref[...]
            acc = jnp.minimum(x, 0.0) - jnp.log(1.0 + jnp.exp(-jnp.abs(x)))
            sh = 1
            while sh < LANES:
                acc = acc + jnp.where(lane >= sh, pltpu.roll(acc, sh, 1), 0.0)
                sh *= 2
            acc = acc + carry
            c_ref[:, pl.ds(off, LANES)] = acc
            return acc[:, LANES - 1:LANES]

        lax.fori_loop(0, nblk, step, jnp.zeros((hp, 1), F32))

    vm = pl.BlockSpec(memory_space=pltpu.VMEM)
    return pl.pallas_call(body, name=name, in_specs=[vm, vm], out_specs=vm,
                          out_shape=jax.ShapeDtypeStruct((hp, s), F32),
                          compiler_params=pltpu.CompilerParams(vmem_limit_bytes=VMEM_LIMIT))(f_t, bias)


def _gate_bwd(dc_t, f_t, bias, name):
    hp, s = f_t.shape
    nblk = s // LANES

    def body(dc_ref, f_ref, b_ref, df_ref, db_ref):
        lane = lax.broadcasted_iota(jnp.int32, (hp, LANES), 1)

        def step(t, carry):
            suffix, dbias = carry
            off = pl.multiple_of((nblk - 1 - t) * LANES, LANES)
            acc = dc_ref[:, pl.ds(off, LANES)]
            sh = 1
            while sh < LANES:
                acc = acc + jnp.where(lane < LANES - sh, pltpu.roll(acc, LANES - sh, 1), 0.0)
                sh *= 2
            acc = acc + suffix
            x = f_ref[:, pl.ds(off, LANES)] + b_ref[...]
            df = acc * _sigmoid(-x)
            df_ref[:, pl.ds(off, LANES)] = df
            return acc[:, 0:1], dbias + jnp.sum(df, axis=1, keepdims=True)

        _, dbias = lax.fori_loop(0, nblk, step, (jnp.zeros((hp, 1), F32), jnp.zeros((hp, 1), F32)))
        db_ref[...] = dbias

    vm = pl.BlockSpec(memory_space=pltpu.VMEM)
    return pl.pallas_call(body, name=name, in_specs=[vm, vm, vm], out_specs=[vm, vm],
                          out_shape=[jax.ShapeDtypeStruct((hp, s), F32), jax.ShapeDtypeStruct((hp, 1), F32)],
                          compiler_params=pltpu.CompilerParams(vmem_limit_bytes=VMEM_LIMIT))(dc_t, f_t, bias)


def _wide(rep, width):
    return jnp.tile(rep, (1, width // LANES))


def _fold(t):
    part = t[:, :LANES]
    for c in range(1, t.shape[1] // LANES):
        part = part + t[:, c * LANES:(c + 1) * LANES]
    return part


def _foxt_logits(q, k, cq_row, ck_rep, mask, hmask):
    s = _dot_nt(_sel(hmask, k), q) + (cq_row - _wide(ck_rep, q.shape[0]))
    if mask is not None:
        s = jnp.where(mask, s, NEG)
    return s


def _causal_t(qi, kj, tq, tk):
    return (kj * tk + lax.broadcasted_iota(jnp.int32, (tk, tq), 0)
            <= qi * tq + lax.broadcasted_iota(jnp.int32, (tk, tq), 1))


FOX_SPLIT = 1


def _fox_tiles(s):
    tq = _rows(s, 1024)
    return tq, tq // FOX_SPLIT, s // tq


def _fox_steps(nq):
    return FOX_SPLIT * nq * (nq + 1) // 2


def _count_ge(t, bounds):
    return sum([(t >= b).astype(jnp.int32) for b in bounds], jnp.int32(0))


def _sweep_q_major(t, nq):
    qi = _count_ge(t, [FOX_SPLIT * r * (r + 1) // 2 for r in range(1, nq)])
    return qi, t - FOX_SPLIT * qi * (qi + 1) // 2


def _sweep_k_major(t, nq):
    counts = [nq - j // FOX_SPLIT for j in range(FOX_SPLIT * nq)]
    offs = [sum(counts[:j]) for j in range(1, FOX_SPLIT * nq)]
    kj = _count_ge(t, offs)
    start = sum([jnp.where(t >= o, c, 0) for o, c in zip(offs, counts)], jnp.int32(0))
    qi = kj // FOX_SPLIT + (t - start)
    return kj, qi, t == start, qi == nq - 1


def _foxt_fwd(hb, c_rep, c_t3, name):
    s = hb.shape[0]
    npair = MIX_W // LANES
    tq, tk, nq = _fox_tiles(s)

    def body(q_ref, k_ref, v_ref, cq_ref, ck_ref, o_ref, l_ref, m_s, l_s, acc):
        qi, kj = _sweep_q_major(pl.program_id(1), nq)
        hm = _head_masks()

        @pl.when(kj == 0)
        def _():
            m_s[...] = jnp.full_like(m_s, NEG)
            l_s[...] = jnp.zeros_like(l_s)
            acc[...] = jnp.zeros_like(acc)

        def step(mask):
            q, k = q_ref[...] * SCALE, k_ref[...]
            vt = jnp.transpose(v_ref[...])
            cq = cq_ref[...]
            for h in range(2):
                st = _foxt_logits(q, k, cq[h:h + 1, :], ck_ref[h], mask, hm[h])
                m_old = m_s[h]
                m_new = jnp.maximum(m_old, jnp.max(st, axis=0, keepdims=True))
                pt = jnp.exp(st - m_new)
                corr = jnp.exp(m_old - m_new)
                l_s[h] = l_s[h] * corr + jnp.sum(pt, axis=0, keepdims=True)
                acc[h] = acc[h] * corr + _dot(vt[h * HEAD_DIM:(h + 1) * HEAD_DIM, :], pt.astype(BF))
                m_s[h] = m_new

        @pl.when(kj < FOX_SPLIT * qi)
        def _():
            step(None)

        @pl.when(kj >= FOX_SPLIT * qi)
        def _():
            step(_causal_t(qi, kj, tq, tk))

        @pl.when(kj == FOX_SPLIT * (qi + 1) - 1)
        def _():
            outs = []
            for h in range(2):
                outs.append(acc[h] / l_s[h])
                l_ref[h:h + 1, :] = m_s[h] + jnp.log(l_s[h])
            o_ref[...] = jnp.transpose(jnp.concatenate(outs, axis=0)).astype(BF)

    def q_map(p, t):
        return (_sweep_q_major(t, nq)[0], p)

    def kv_map(off):
        return lambda p, t: (_sweep_q_major(t, nq)[1], off + p)

    blk = pl.BlockSpec((tq, LANES), q_map)
    row = pl.BlockSpec((None, 2, tq), lambda p, t: (p, 0, _sweep_q_major(t, nq)[0]))
    return pl.pallas_call(
        body, name=name, grid=(npair, _fox_steps(nq)),
        in_specs=[blk, pl.BlockSpec((tk, LANES), kv_map(npair)), pl.BlockSpec((tk, LANES), kv_map(2 * npair)), row,
                  pl.BlockSpec((2, tk, LANES), lambda p, t: (p, _sweep_q_major(t, nq)[1], 0))],
        out_specs=[blk, row],
        out_shape=[jax.ShapeDtypeStruct((s, MIX_W), BF), jax.ShapeDtypeStruct((npair, 2, s), F32)],
        scratch_shapes=[pltpu.VMEM((2, 1, tq), F32), pltpu.VMEM((2, 1, tq), F32),
                        pltpu.VMEM((2, HEAD_DIM, tq), F32)],
        compiler_params=_cp("parallel", "arbitrary"))(hb, hb, hb, c_t3, c_rep)


def _foxt_dsum(hb, dcat, lse, c_rep, c_t3, name):
    s = hb.shape[0]
    npair = MIX_W // LANES
    tq, tk, nq = _fox_tiles(s)

    def body(q_ref, k_ref, v_ref, do_ref, l_ref, cq_ref, ck_ref, d_ref, acc):
        qi, kj = _sweep_q_major(pl.program_id(1), nq)
        hm = _head_masks()

        @pl.when(kj == 0)
        def _():
            acc[...] = jnp.zeros_like(acc)

        def step(mask):
            q, k, v, do = q_ref[...] * SCALE, k_ref[...], v_ref[...], do_ref[...]
            cq, lse_rows = cq_ref[...], l_ref[...]
            for h in range(2):
                pt = jnp.exp(_foxt_logits(q, k, cq[h:h + 1, :], ck_ref[h], mask, hm[h]) - lse_rows[h:h + 1, :])
                acc[h] += jnp.sum(pt * _dot_nt(_sel(hm[h], v), do), axis=0, keepdims=True)

        @pl.when(kj < FOX_SPLIT * qi)
        def _():
            step(None)

        @pl.when(kj >= FOX_SPLIT * qi)
        def _():
            step(_causal_t(qi, kj, tq, tk))

        @pl.when(kj == FOX_SPLIT * (qi + 1) - 1)
        def _():
            for h in range(2):
                d_ref[h:h + 1, :] = acc[h]

    def q_map(p, t):
        return (_sweep_q_major(t, nq)[0], p)

    def kv_map(off):
        return lambda p, t: (_sweep_q_major(t, nq)[1], off + p)

    blk = pl.BlockSpec((tq, LANES), q_map)
    row = pl.BlockSpec((None, 2, tq), lambda p, t: (p, 0, _sweep_q_major(t, nq)[0]))
    return pl.pallas_call(
        body, name=name, grid=(npair, _fox_steps(nq)),
        in_specs=[blk, pl.BlockSpec((tk, LANES), kv_map(npair)), pl.BlockSpec((tk, LANES), kv_map(2 * npair)),
                  blk, row, row, pl.BlockSpec((2, tk, LANES), lambda p, t: (p, _sweep_q_major(t, nq)[1], 0))],
        out_specs=row, out_shape=jax.ShapeDtypeStruct((npair, 2, s), F32),
        scratch_shapes=[pltpu.VMEM((2, 1, tq), F32)],
        compiler_params=_cp("parallel", "arbitrary"))(hb, hb, hb, dcat, lse, c_t3, c_rep)


def _foxt_bwd(hb, dcat, dsum, lse, c_rep, c_t3, name):
    s = hb.shape[0]
    npair = MIX_W // LANES
    tq, tk, nq = _fox_tiles(s)

    def body(q_ref, k_ref, v_ref, do_ref, d_ref, l_ref, cq_ref, ck_ref, dq_ref, dk_ref, dv_ref, dc_ref, dc_s):
        t = pl.program_id(1)
        kj, qi, first, last = _sweep_k_major(t, nq)
        hm = _head_masks()

        @pl.when(first)
        def _():
            dk_ref[...] = jnp.zeros_like(dk_ref)
            dv_ref[...] = jnp.zeros_like(dv_ref)
            dc_s[...] = jnp.zeros_like(dc_s)

        @pl.when(t == 0)
        def _():
            dq_ref[...] = jnp.zeros_like(dq_ref)

        def step(mask):
            q, k, v, do = q_ref[...] * SCALE, k_ref[...], v_ref[...], do_ref[...]
            qt, kt, dot = jnp.transpose(q), jnp.transpose(k), jnp.transpose(do)
            cq, lse_rows, d_rows = cq_ref[...], l_ref[...], d_ref[...]
            dqs, dks, dvs = [], [], []
            for h in range(2):
                rows = slice(h * HEAD_DIM, (h + 1) * HEAD_DIM)
                pt = jnp.exp(_foxt_logits(q, k, cq[h:h + 1, :], ck_ref[h], mask, hm[h]) - lse_rows[h:h + 1, :])
                dst = pt * (_dot_nt(_sel(hm[h], v), do) - d_rows[h:h + 1, :])
                dsb = dst.astype(BF)
                dqs.append(_dot(kt[rows, :], dsb))
                dks.append(_dot_nt(qt[rows, :], dsb))
                dvs.append(_dot_nt(dot[rows, :], pt.astype(BF)))
                dc_s[h] += _fold(dst)
            cols = pl.ds(pl.multiple_of(qi * tq, tq), tq)
            dq_ref[:, cols] += SCALE * jnp.concatenate(dqs, axis=0)
            dk_ref[...] += jnp.concatenate(dks, axis=0)
            dv_ref[...] += jnp.concatenate(dvs, axis=0)

        @pl.when(kj < FOX_SPLIT * qi)
        def _():
            step(None)

        @pl.when(kj >= FOX_SPLIT * qi)
        def _():
            step(_causal_t(qi, kj, tq, tk))

        @pl.when(last)
        def _():
            for h in range(2):
                dc_ref[h:h + 1, :] = -jnp.sum(jnp.transpose(dc_s[h]), axis=0, keepdims=True)

    def kj_of(t):
        return _sweep_k_major(t, nq)[0]

    def qi_of(t):
        return _sweep_k_major(t, nq)[1]

    qblk = pl.BlockSpec((tq, LANES), lambda p, t: (qi_of(t), p))
    row = pl.BlockSpec((None, 2, tq), lambda p, t: (p, 0, qi_of(t)))
    kblk = pl.BlockSpec((LANES, tk), lambda p, t: (p, kj_of(t)))
    rep = pl.BlockSpec((2, tk, LANES), lambda p, t: (p, kj_of(t), 0))
    return pl.pallas_call(
        body, name=name, grid=(npair, _fox_steps(nq)),
        in_specs=[qblk,
                  pl.BlockSpec((tk, LANES), lambda p, t: (kj_of(t), npair + p)),
                  pl.BlockSpec((tk, LANES), lambda p, t: (kj_of(t), 2 * npair + p)),
                  qblk, row, row, row, rep],
        out_specs=[pl.BlockSpec((LANES, s), lambda p, t: (p, 0)), kblk, kblk,
                   pl.BlockSpec((None, 2, tk), lambda p, t: (p, 0, kj_of(t)))],
        out_shape=[jax.ShapeDtypeStruct((MIX_W, s), F32), jax.ShapeDtypeStruct((MIX_W, s), F32),
                   jax.ShapeDtypeStruct((MIX_W, s), F32), jax.ShapeDtypeStruct((npair, 2, s), F32)],
        scratch_shapes=[pltpu.VMEM((2, tk, LANES), F32)],
        compiler_params=_cp("arbitrary", "arbitrary"))(hb, hb, hb, dcat, dsum, lse, c_t3, c_rep)


def _loss_head(y, target, name):
    s, d = y.shape
    ts = _rows(s)

    def body(y_ref, t_ref, dy_ref, l_ref):
        i = pl.program_id(0)
        e = y_ref[...] - t_ref[...]
        dy_ref[...] = e * (1.0 / d)

        @pl.when(i == 0)
        def _():
            l_ref[...] = jnp.zeros_like(l_ref)

        part = jnp.sum(jnp.sum(e * e, axis=1, keepdims=True), axis=0, keepdims=True)
        l_ref[...] += part * (0.5 / d)

    row = pl.BlockSpec((ts, d), lambda i: (i, 0))
    return pl.pallas_call(
        body, name=name, grid=(s // ts,), in_specs=[row, row],
        out_specs=[row, pl.BlockSpec((1, 1), lambda i: (0, 0))],
        out_shape=[jax.ShapeDtypeStruct((s, d), F32), jax.ShapeDtypeStruct((1, 1), F32)],
        compiler_params=_cp("arbitrary"))(y, target)


def _adam_rows(r, c):
    cap = max(8, (1 << 20) // (4 * c))
    if r <= cap:
        return r
    best = None
    for t in range(8, cap + 1, 8):
        if r % t == 0:
            best = t
    return best if best is not None else r


def _reduce_adamw(contribs, w, m, v, name):
    nl = len(contribs)
    nd, r, c = contribs[0].shape
    tr = _adam_rows(r, c)
    bc1 = 1.0 - ADAM_B1 ** ADAM_STEP
    bc2 = 1.0 - ADAM_B2 ** ADAM_STEP

    def body(*refs):
        c_refs = refs[:nl]
        w_ref, m_ref, v_ref, g_ref, d_ref, nm_ref, nv_ref = refs[nl:]
        l = pl.program_id(0)
        for li in range(nl):
            @pl.when(l == li)
            def _(c_ref=c_refs[li]):
                g = c_ref[0].astype(F32)
                for k in range(1, nd):
                    g = g + c_ref[k].astype(F32)
                nm = ADAM_B1 * m_ref[...] + (1.0 - ADAM_B1) * g
                nv = ADAM_B2 * v_ref[...] + (1.0 - ADAM_B2) * (g * g)
                g_ref[...] = g
                nm_ref[...] = nm
                nv_ref[...] = nv
                d_ref[...] = -ADAM_LR * ((nm / bc1) / (jnp.sqrt(nv / bc2) + ADAM_EPS) + ADAM_WD * w_ref[...])

    def c_spec(li):
        return pl.BlockSpec((nd, tr, c), lambda l, i: (0, jnp.where(l == li, i, 0), 0))

    blk = pl.BlockSpec((None, tr, c), lambda l, i: (l, i, 0))
    out = jax.ShapeDtypeStruct((nl, r, c), F32)
    return pl.pallas_call(
        body, name=name, grid=(nl, r // tr),
        in_specs=[c_spec(li) for li in range(nl)] + [blk, blk, blk],
        out_specs=[blk, blk, blk, blk], out_shape=[out, out, out, out],
        compiler_params=_cp("arbitrary", "arbitrary"))(*contribs, w, m, v)


def _mesh_pos():
    return lax.axis_index("x"), lax.axis_index("y"), lax.axis_index("c")


def _peer(pos, k):
    x, y, c = pos
    return (1 - x if k & 4 else x, 1 - y if k & 2 else y, 1 - c if k & 1 else c)


def _linear(pos):
    return 4 * pos[0] + 2 * pos[1] + pos[2]


def _xfer_copies(srcs, lands, send_sems, recv_sems, local_sems, gather):
    pos = _mesh_pos()
    me = _linear(pos)
    local, remote = [], []
    for i, (src, land) in enumerate(zip(srcs, lands)):
        local.append(pltpu.make_async_copy(src if gather else src.at[me], land.at[me], local_sems.at[i]))
        for k in range(1, N_DEV):
            peer = _peer(pos, k)
            remote.append(pltpu.make_async_remote_copy(
                src_ref=src if gather else src.at[_linear(peer)], dst_ref=land.at[me],
                send_sem=send_sems.at[i * (N_DEV - 1) + k - 1], recv_sem=recv_sems.at[i * (N_DEV - 1) + k - 1],
                device_id=peer, device_id_type=MESH_ID))
    return local, remote


_HBM = pl.BlockSpec(memory_space=pltpu.HBM)
_SEM = pl.BlockSpec(memory_space=pltpu.SEMAPHORE)
_EFFECT = pltpu.SideEffectType.DATAFLOW_SIDE_EFFECTING


def _xfer_start(srcs, gather, name, after=()):
    n = len(srcs)
    na = len(after)
    lands = [lax.empty(((N_DEV,) + a.shape) if gather else a.shape, a.dtype) for a in srcs]

    def body(*refs):
        src, land = refs[:n], refs[n:2 * n]
        send_sems, recv_sems, local_sems = refs[2 * n + na:2 * n + na + 3]
        local, remote = _xfer_copies(src, land, send_sems, recv_sems, local_sems, gather)
        for cp in local + remote:
            cp.start()
        refs[-1][...] = jnp.zeros_like(refs[-1])

    nsem = n * (N_DEV - 1)
    out = pl.pallas_call(
        body, name=name,
        out_shape=(pltpu.SemaphoreType.DMA((nsem,)), pltpu.SemaphoreType.DMA((nsem,)), pltpu.SemaphoreType.DMA((n,)),
                   *[pltpu.HBM(a.shape, a.dtype) for a in srcs], *[pltpu.HBM(a.shape, a.dtype) for a in lands],
                   jax.ShapeDtypeStruct((8, LANES), F32)),
        in_specs=[_HBM] * (2 * n) + [pl.BlockSpec(memory_space=pl.ANY)] * na,
        out_specs=(_SEM, _SEM, _SEM, *[_HBM] * (2 * n), pl.BlockSpec(memory_space=pltpu.VMEM)),
        input_output_aliases={i: 3 + i for i in range(2 * n)},
        compiler_params=pltpu.CompilerParams(has_side_effects=_EFFECT))(
            *[pltpu.with_memory_space_constraint(a, pltpu.HBM) for a in srcs],
            *[pltpu.with_memory_space_constraint(a, pltpu.HBM) for a in lands], *after)
    return out[:3], list(out[3:3 + n]), list(out[3 + n:3 + 2 * n]), out[-1]


def _started(handle):
    return handle[3]


def _xfer_wait(handle, after, gather, name):
    sems, srcs, lands, _ = handle
    n = len(srcs)

    def body(*refs):
        src, land = refs[:n], refs[n:2 * n]
        send_sems, recv_sems, local_sems = refs[2 * n:2 * n + 3]
        local, remote = _xfer_copies(src, land, send_sems, recv_sems, local_sems, gather)
        for cp in local:
            cp.wait()
        for cp in remote:
            cp.wait_send()
            cp.wait_recv()

    out = pl.pallas_call(
        body, name=name,
        out_shape=(*[pltpu.HBM(a.shape, a.dtype) for a in srcs], *[pltpu.HBM(a.shape, a.dtype) for a in lands]),
        in_specs=[_HBM] * (2 * n) + [_SEM] * 3 + [pl.BlockSpec(memory_space=pl.ANY)] * len(after),
        out_specs=tuple([_HBM] * (2 * n)), input_output_aliases={i: i for i in range(2 * n)},
        compiler_params=pltpu.CompilerParams(has_side_effects=_EFFECT))(*srcs, *lands, *sems, *after)
    return list(out[n:])


def _cols_full(g):
    nd, r, c = g.shape
    return jnp.transpose(g, (1, 0, 2)).reshape(r, nd * c)


def _cols_split(full):
    r, n = full.shape
    return jnp.transpose(full.reshape(r, N_DEV, n // N_DEV), (1, 0, 2))


def _pack_b_in(w):
    qkv = 3 * MIX_W
    pad = jnp.zeros((w.shape[0], B_IN_PAD - w.shape[1]), w.dtype)
    return jnp.concatenate([w[:, :qkv], w[:, qkv + N_MIX_HEADS:], w[:, qkv:qkv + N_MIX_HEADS], pad], axis=1)


def _unpack_b_in(w):
    qkv = 3 * MIX_W
    return jnp.concatenate([w[:, :qkv], w[:, qkv + MEM_W:qkv + MEM_W + N_MIX_HEADS], w[:, qkv:qkv + MEM_W]], axis=1)


def _ffn_forward(x, xb, wgu, get_rest, tag, fused=True):
    if fused:
        wd4, gain, bias = get_rest(x)
        y, yb, gu, a, xh, rstd = _ffn_fwd_main(x, xb, wgu, wd4, gain, bias, f"{tag}_fwd_main")
    else:
        gu, a = _ffn_up(xb, wgu, f"{tag}_up")
        wd4, gain, bias = get_rest(a)
        y, yb, xh, rstd = _mm_res_ln(a, wd4, x, gain, bias, 0.5, f"{tag}_down_ln")
    return y, yb, (xb, gu, a, xh, rstd), wd4


def _ffn_backward(dy, saved, wgu, wd4, gain, tag, after=(), send=None):
    xb, gu, a, xh, rstd = saved
    s = xb.shape[0]
    nd, c, d = wgu.shape
    dx, dzb, dh, dgain, dbias = _ffn_bwd_main(dy, xh, rstd, gain, wd4, wgu, gu, f"{tag}_bwd_main", after,
                                               with_dx=send is None)
    dh = dh.reshape(nd, s, c)
    dwd = _mm_tn(a, dzb[None], f"{tag}_dwd").reshape(nd, wd4.shape[1] // 2, d)
    if send is not None:
        send("down", dwd, dgain, dbias)
    dwgu = _mm_tn(dh, xb[None], f"{tag}_dwgu")
    if send is not None:
        sent = send("gate_up", dwgu)
        dx = _mm_nt(dh, wgu, f"{tag}_dx", res=dx, w_rows_out=False, after=sent)
    return dx, dwgu, dwd, dgain, dbias


def _mixer_a_forward(x, xb, memb, w_in, w_kv, w_out, gain, bias, tabs):
    hb = _proj_rope(xb, w_in, tabs, 2 * MIX_W // LANES, "a_in", True)
    groups = [_band_fwd(hb, g, f"a_band_fwd{g}") for g in range(N_GROUPS)]
    oa, lt = _band_combine([o for o, _ in groups], [l for _, l in groups], "a_combine")
    kv = _mm_nn(memb, w_kv, BF, "a_mem_kv")
    om, lm = _mem_fwd(hb, 3 * MIX_W // LANES, kv, "a_mem_fwd")
    cat = jnp.concatenate([oa, om], axis=1)
    y, yb, xh, rstd = _mm_res_ln(cat[None], w_out[None], x, gain, bias, 1.0, "a_out_ln")
    return y, yb, (xb, hb, oa, lt, kv, lm, cat, xh, rstd)


def _mixer_a_backward(dy, saved, memb, w_in, w_kv, w_out, gain, tabs_neg, after=()):
    xb, hb, oa, lt, kv, lm, cat, xh, rstd = saved
    dz, dzb, dgain, dbias = _ln_bwd(dy, xh, rstd, gain, 1.0, "a_ln_bwd", after)
    dcat = _mm_nt(dzb[None], w_out[None], "a_dcat", out_dtype=BF)
    dw_out = _mm_tn(cat[None], dzb[None], "a_dwout")[0]
    dqm, dkm, dvm = _mem_bwd(hb, 3 * MIX_W // LANES, kv, dcat, cat, GROUP_W // LANES, lm, "a_mem_bwd")
    dkv = jnp.concatenate([dkm, dvm], axis=1).astype(BF)
    dw_kv = _mm_tn(memb[None], dkv[None], "a_dwkv")[0]
    grads = [_band_bwd(hb, dcat, oa, lt, g, f"a_band_bwd{g}") for g in range(N_GROUPS)]
    dhb = _rope_cast([grads[g][i] for i in range(3) for g in range(N_GROUPS)] + [dqm], tabs_neg,
                     2 * MIX_W // LANES, "a_rope_bwd")
    dw_in = _mm_tn(dhb[None], xb[None], "a_dwin")[0]
    dx = _mm_nt(dhb[None], w_in[None], "a_dx", res=dz, w_rows_out=False)
    return dx, dw_in, dw_kv, dw_out, dgain, dbias


def _pad_rows(t, rows):
    return jnp.concatenate([t, jnp.zeros((rows - t.shape[0], t.shape[1]), t.dtype)], axis=0)


def _pad_cols(t, cols):
    return jnp.concatenate([t, jnp.zeros((t.shape[0], cols - t.shape[1]), t.dtype)], axis=1)


def _mixer_b_forward(x, xb, memb, w_in, fbias, w_kv, w_out, gain, bias, tabs):
    s = x.shape[0]
    hb, f = _proj_rope(xb, w_in, tabs, 0, "b_in", False, tail_block=(3 * MIX_W + MEM_W) // LANES)
    f_t = _pad_rows(jnp.transpose(f[:, :N_MIX_HEADS]), 16)
    bias16 = _pad_rows(jnp.transpose(fbias), 16)
    c_t = _gate_fwd(f_t, bias16, "b_gate_fwd")
    c_t3 = c_t[:N_MIX_HEADS].reshape(N_MIX_HEADS // 2, 2, s)
    c_rep = jnp.broadcast_to(c_t[:N_MIX_HEADS, :, None], (N_MIX_HEADS, s, LANES))
    ob, lb = _foxt_fwd(hb, c_rep, c_t3, "b_fox_fwd")
    kv = _mm_nn(memb, w_kv, BF, "b_mem_kv")
    om, lm = _mem_fwd(hb, 3 * MIX_W // LANES, kv, "b_mem_fwd")
    cat = jnp.concatenate([ob, om], axis=1)
    y, yb, xh, rstd = _mm_res_ln(cat[None], w_out[None], x, gain, bias, 1.0, "b_out_ln")
    return y, yb, (xb, hb, f_t, bias16, c_rep, c_t3, lb, kv, lm, cat, xh, rstd)


def _mixer_b_backward(dy, saved, memb, w_in, w_kv, w_out, gain, tabs, after=()):
    xb, hb, f_t, bias16, c_rep, c_t3, lb, kv, lm, cat, xh, rstd = saved
    s = xb.shape[0]
    dz, dzb, dgain, dbias = _ln_bwd(dy, xh, rstd, gain, 1.0, "b_ln_bwd", after)
    dcat = _mm_nt(dzb[None], w_out[None], "b_dcat", out_dtype=BF)
    dw_out = _mm_tn(cat[None], dzb[None], "b_dwout")[0]
    dqm, dkm, dvm = _mem_bwd(hb, 3 * MIX_W // LANES, kv, dcat, cat, MIX_W // LANES, lm, "b_mem_bwd")
    dkv = jnp.concatenate([dkm, dvm], axis=1).astype(BF)
    dw_kv = _mm_tn(memb[None], dkv[None], "b_dwkv")[0]
    dsum = _foxt_dsum(hb, dcat, lb, c_rep, c_t3, "b_fox_dsum")
    dq, dk, dv, dc3 = _foxt_bwd(hb, dcat, dsum, lb, c_rep, c_t3, "b_fox_bwd")
    df_t, dfb = _gate_bwd(_pad_rows(dc3.reshape(N_MIX_HEADS, s), 16), f_t, bias16, "b_gate_bwd")
    df = _pad_cols(jnp.transpose(df_t[:N_MIX_HEADS]), B_IN_PAD - 3 * MIX_W - MEM_W)
    dhb = _rope_cast([dq, dk, dv, dqm, df], tabs, 0, "b_cast_bwd", transposed=(0, 1, 2))
    dw_in = _mm_tn(xb[None], dhb[None], "b_dwin")[0]
    dx = _mm_nt(dhb[None], w_in[None], "b_dx", res=dz)
    return dx, dw_in, jnp.transpose(dfb[:N_MIX_HEADS]), dw_kv, dw_out, dgain, dbias


def _stored(t, name):
    return jnp.transpose(t, (0, 2, 1)) if name in ROWS_OUT else t


GATHER_GROUPS = (
    (("ffn1_w_gate_up", 0),),
    (("ffn1_w_down", 0), ("ln_gain", None), ("ln_bias", None)),
    (("a_w_in", 0), ("a_w_out", 0), ("mem_w_kv", 0)),
    (("ffn2_w_gate_up", 0), ("ffn2_w_down", 0)),
    (("ffn1_w_gate_up", 1), ("ffn1_w_down", 1)),
    (("b_w_in", 0), ("b_w_out", 0), ("mem_w_kv", 1)),
    (("ffn2_w_gate_up", 1), ("ffn2_w_down", 1)),
)


def _group_shards(group, params):
    return [t if n in F32_COMM else _stored(t, n)[l].astype(BF) for (n, l), t in zip(group, params)]


def _weight_groups(w):
    return [_group_shards(grp, [w[n] for n, _ in grp]) for grp in GATHER_GROUPS]


def _local_step(x, mem, target, fbias, get_w, put_g):
    s, d = x.shape
    tabs = _rope_tables(s, 1.0)
    tabs_neg = _rope_tables(s, -1.0)
    memb = mem.astype(BF)
    saved, wl = [], []
    cur, curb = x, x.astype(BF)
    ln = []

    def down4(t):
        return t.reshape(N_DEV // 2, -1, d)

    for i in range(DEPTH):
        if i == 0:
            def first_rest(a):
                g = get_w(1, a)
                ln.extend(jnp.transpose(t, (1, 2, 0, 3)).reshape(DEPTH, 3, 1, d) for t in g[1:3])
                return down4(g[0]), ln[0][0, 0], ln[1][0, 0]

            wgu = get_w(0, cur)[0]
            cur, curb, s1, wd = _ffn_forward(cur, curb, wgu, first_rest, "l0_ffn1", fused=False)
        else:
            g = get_w(3 * i + 1, cur)
            wgu = g[0]
            cur, curb, s1, wd = _ffn_forward(cur, curb, wgu, lambda a, g=g: (down4(g[1]), ln[0][i, 0], ln[1][i, 0]),
                                             f"l{i}_ffn1")
        w1 = (wgu, wd)
        ln_g, ln_b = ln
        g = get_w(3 * i + 2, cur)
        if i == 0:
            wm = (g[0].reshape(-1, d), g[2].reshape(d, -1), _cols_full(g[1]))
            cur, curb, s2 = _mixer_a_forward(cur, curb, memb, wm[0], wm[1], wm[2], ln_g[i, 1], ln_b[i, 1], tabs)
        else:
            wm = (_pack_b_in(g[0].reshape(d, -1)), g[2].reshape(d, -1), g[1].reshape(d, -1))
            cur, curb, s2 = _mixer_b_forward(cur, curb, memb, wm[0], fbias, wm[1], wm[2], ln_g[i, 1], ln_b[i, 1],
                                             tabs)
        g = get_w(3 * i + 3, cur)
        cur, curb, s3, wd = _ffn_forward(cur, curb, g[0], lambda a, g=g: (down4(g[1]), ln_g[i, 2], ln_b[i, 2]),
                                         f"l{i}_ffn2")
        w3 = (g[0], wd)
        saved.append((s1, s2, s3))
        wl.append((w1, wm, w3))

    dy, loss = _loss_head(cur, target, "loss_head")

    dgs = [[None] * 3 for _ in range(DEPTH)]
    dbs = [[None] * 3 for _ in range(DEPTH)]
    sent = ()
    for i in reversed(range(DEPTH)):
        s1, s2, s3 = saved[i]
        w1, wm, w3 = wl[i]
        dy, dgu, dd, dgs[i][2], dbs[i][2] = _ffn_backward(dy, s3, w3[0], w3[1], ln_g[i, 2], f"l{i}_ffn2", sent)
        sent = put_g(3 * i + 2, [dgu, dd])
        if i == 0:
            dy, dw_in, dw_kv, dw_out, dgs[i][1], dbs[i][1] = _mixer_a_backward(
                dy, s2, memb, wm[0], wm[1], wm[2], ln_g[i, 1], tabs_neg, sent)
            sent = put_g(1, [dw_in.reshape(N_DEV, -1, d), _cols_split(dw_out),
                             dw_kv.reshape(N_DEV, d // N_DEV, -1)])
        else:
            dy, dw_in, dfb, dw_kv, dw_out, dgs[i][1], dbs[i][1] = _mixer_b_backward(
                dy, s2, memb, wm[0], wm[1], wm[2], ln_g[i, 1], tabs, sent)
            sent = put_g(4, [_unpack_b_in(dw_in).reshape(N_DEV, d // N_DEV, -1),
                             dw_out.reshape(N_DEV, d // N_DEV, -1), dw_kv.reshape(N_DEV, d // N_DEV, -1),
                             jnp.broadcast_to(dfb[None], (N_DEV,) + dfb.shape)])
        if i == 0:
            def send_last(kind, dw, dgain=None, dbias=None):
                if kind == "gate_up":
                    return put_g(6, [dw])
                dgs[0][0], dbs[0][0] = dgain, dbias
                ln_pieces = []
                for parts in (dgs, dbs):
                    t = jnp.concatenate([parts[a][b] for a in range(DEPTH) for b in range(3)], axis=0)
                    ln_pieces.append(jnp.transpose(t.reshape(DEPTH * 3, N_DEV, d // N_DEV), (1, 0, 2)))
                return put_g(0, [dw] + ln_pieces)

            dy = _ffn_backward(dy, s1, w1[0], w1[1], ln_g[i, 0], "l0_ffn1", sent, send_last)[0]
        else:
            dy, dgu, dd, dgs[i][0], dbs[i][0] = _ffn_backward(dy, s1, w1[0], w1[1], ln_g[i, 0], f"l{i}_ffn1", sent)
            sent = put_g(3, [dgu, dd])
    return loss, dy


WEIGHTS = ("ffn1_w_gate_up", "ffn1_w_down", "ffn2_w_gate_up", "ffn2_w_down", "ln_gain", "ln_bias", "mem_w_kv",
           "a_w_in", "a_w_out", "b_w_in", "b_forget_bias", "b_w_out")
F32_COMM = ("ln_gain", "ln_bias", "b_forget_bias")
ROWS_OUT = ("ffn1_w_gate_up", "ffn2_w_gate_up", "a_w_in")
GRAD_SLOTS = {
    "ffn1_w_gate_up": [(6, 0), (3, 0)], "ffn1_w_down": [(0, 0), (3, 1)],
    "ffn2_w_gate_up": [(2, 0), (5, 0)], "ffn2_w_down": [(2, 1), (5, 1)],
    "ln_gain": [(0, 1)], "ln_bias": [(0, 2)], "mem_w_kv": [(1, 2), (4, 2)],
    "a_w_in": [(1, 0)], "a_w_out": [(1, 1)], "b_w_in": [(4, 0)], "b_forget_bias": [(4, 3)], "b_w_out": [(4, 1)],
}


def kernel(x, mem, ffn1_w_gate_up, ffn1_w_down, ffn2_w_gate_up, ffn2_w_down, ln_gain, ln_bias, mem_w_kv, a_w_in, a_w_out, b_w_in, b_forget_bias, b_w_out, loss_target, m_ffn1_w_gate_up, m_ffn1_w_down, m_ffn2_w_gate_up, m_ffn2_w_down, m_ln_gain, m_ln_bias, m_mem_w_kv, m_a_w_in, m_a_w_out, m_b_w_in, m_b_forget_bias, m_b_w_out, v_ffn1_w_gate_up, v_ffn1_w_down, v_ffn2_w_gate_up, v_ffn2_w_down, v_ln_gain, v_ln_bias, v_mem_w_kv, v_a_w_in, v_a_w_out, v_b_w_in, v_b_forget_bias, v_b_w_out):
    w = dict(zip(WEIGHTS, (ffn1_w_gate_up, ffn1_w_down, ffn2_w_gate_up, ffn2_w_down, ln_gain, ln_bias, mem_w_kv,
                           a_w_in, a_w_out, b_w_in, b_forget_bias, b_w_out)))
    m = dict(zip(WEIGHTS, (m_ffn1_w_gate_up, m_ffn1_w_down, m_ffn2_w_gate_up, m_ffn2_w_down, m_ln_gain, m_ln_bias,
                           m_mem_w_kv, m_a_w_in, m_a_w_out, m_b_w_in, m_b_forget_bias, m_b_w_out)))
    v = dict(zip(WEIGHTS, (v_ffn1_w_gate_up, v_ffn1_w_down, v_ffn2_w_gate_up, v_ffn2_w_down, v_ln_gain, v_ln_bias,
                           v_mem_w_kv, v_a_w_in, v_a_w_out, v_b_w_in, v_b_forget_bias, v_b_w_out)))

    gathers = []
    for k, grp in enumerate(GATHER_GROUPS):
        params, behind = [w[n] for n, _ in grp], [_started(h) for h in gathers[-1:]]
        if behind:
            params, behind = lax.optimization_barrier((params, behind))
        gathers.append(_xfer_start(_group_shards(grp, params), True, f"gather{k}_start", behind))
    exchanges = {}

    def get_w(k, after):
        behind = [after] + ([_started(h) for h in gathers] if k == 0 else [])
        return _xfer_wait(gathers[k], behind, True, f"gather{k}_wait")

    def put_g(k, pieces):
        behind = [_started(exchanges[0])] if k == 6 else []
        exchanges[k] = _xfer_start(pieces, False, f"grads{k}_start", behind)
        return (_started(exchanges[k]),)

    loss, grad_x = _local_step(x[0], mem[0], loss_target[0], b_forget_bias, get_w, put_g)
    loss = lax.psum(loss[0, 0], ("x", "y", "c"))

    outs, landed = {}, {}

    def adamw(names):
        for n in names:
            contribs = [landed[g][j] for g, j in GRAD_SLOTS[n]]
            view = (len(contribs),) + contribs[0].shape[1:]
            shape = _stored(w[n], n).shape
            res = _reduce_adamw(contribs, *[_stored(t[n], n).reshape(view) for t in (w, m, v)], f"adamw_{n}")
            outs[n] = [_stored(t.reshape(shape), n) for t in res]
        return [outs[n][3] for n in names]

    after = [grad_x]
    for k in (5, 4, 3, 2, 1):
        landed[k] = _xfer_wait(exchanges[k], after, False, f"grads{k}_wait")
        after = [landed[k][0]]
    done = adamw(("ffn2_w_gate_up", "ffn2_w_down", "mem_w_kv", "a_w_in", "a_w_out", "b_w_in", "b_forget_bias",
                  "b_w_out"))
    landed[0] = _xfer_wait(exchanges[0], done, False, "grads0_wait")
    done = adamw(("ffn1_w_down", "ln_gain", "ln_bias"))
    landed[6] = _xfer_wait(exchanges[6], done, False, "grads6_wait")
    adamw(("ffn1_w_gate_up",))
    return (loss, grad_x[None], *[outs[n][0] for n in WEIGHTS], *[outs[n][1] for n in WEIGHTS],
            *[outs[n][2] for n in WEIGHTS], *[outs[n][3] for n in WEIGHTS])
```

```python
import functools

import jax
import jax.numpy as jnp
from jax import lax
from jax.experimental import pallas as pl
from jax.experimental.pallas import tpu as pltpu

F32 = jnp.float32
BF = jnp.bfloat16
MESH_ID = pl.DeviceIdType.MESH

N_DEV = 8
DEPTH = 2
HEAD_DIM = 64
LANES = 128
N_MIX_HEADS = 12
N_MEM_HEADS = 4
MIX_W = N_MIX_HEADS * HEAD_DIM
MEM_W = N_MEM_HEADS * HEAD_DIM
N_GROUPS = 3
GROUP_W = MIX_W // N_GROUPS
BLOCK = 128
BAND_SUB = 4
BAND_COLS = 4
ROT_HALF = 8
ROPE_THETA = 500000.0
ALPHA = (2 * DEPTH) ** 0.25
LN_EPS = 1e-5
SCALE = HEAD_DIM ** -0.5
NEG = -1e30
B_IN_PAD = 2688
ADAM_LR, ADAM_B1, ADAM_B2, ADAM_EPS, ADAM_WD, ADAM_STEP = 0.001, 0.9, 0.999, 1e-08, 0.01, 10
VMEM_LIMIT = 56 * 1024 * 1024


def _cp(*sem):
    return pltpu.CompilerParams(dimension_semantics=sem, vmem_limit_bytes=VMEM_LIMIT)


def _dot(a, b):
    return jnp.dot(a, b, preferred_element_type=F32)


def _dot_nt(a, b):
    return lax.dot_general(a, b, (((1,), (1,)), ((), ())), preferred_element_type=F32)


def _dot_tn(a, b):
    return lax.dot_general(a, b, (((0,), (0,)), ((), ())), preferred_element_type=F32)


def _sigmoid(x):
    return 1.0 / (1.0 + jnp.exp(-x))


def _tile(n, cap=1024):
    if n <= cap:
        return n
    best = LANES
    for t in range(LANES, cap + 1, LANES):
        if n % t == 0:
            best = t
    return best


def _rows(s, cap=512):
    return s if s <= cap else cap


def _mm_nn(a, b, out_dtype, name, b_rows_out=False):
    m, k = a.shape
    n = b.shape[0] if b_rows_out else b.shape[1]
    tm, tn = _rows(m), _tile(n)

    def body(a_ref, b_ref, o_ref):
        prod = _dot_nt(a_ref[...], b_ref[...]) if b_rows_out else _dot(a_ref[...], b_ref[...])
        o_ref[...] = prod.astype(o_ref.dtype)

    b_spec = (pl.BlockSpec((tn, k), lambda j, i: (j, 0)) if b_rows_out
              else pl.BlockSpec((k, tn), lambda j, i: (0, j)))
    return pl.pallas_call(
        body, name=name, grid=(n // tn, m // tm),
        in_specs=[pl.BlockSpec((tm, k), lambda j, i: (i, 0)), b_spec],
        out_specs=pl.BlockSpec((tm, tn), lambda j, i: (i, j)),
        out_shape=jax.ShapeDtypeStruct((m, n), out_dtype),
        compiler_params=_cp("parallel", "parallel"))(a, b)


def _resident(shape, index_map):
    return pl.BlockSpec(shape, index_map, pipeline_mode=pl.Buffered(1))


def _mm_tn(a, b, name, out_dtype=BF):
    na, s, m = a.shape
    nb, _, n = b.shape
    no = max(na, nb)
    tm, tn = _tile(m), _tile(n)

    def body(a_ref, b_ref, o_ref):
        o_ref[...] = _dot_tn(a_ref[...], b_ref[...]).astype(o_ref.dtype)

    def spec(nbatch, width, tile, index_map):
        fixed = nbatch == 1 and width == tile
        return _resident((None, s, tile), index_map) if fixed else pl.BlockSpec((None, s, tile), index_map)

    return pl.pallas_call(
        body, name=name, grid=(no, m // tm, n // tn),
        in_specs=[spec(na, m, tm, lambda j, r, c: (j if na > 1 else 0, 0, r)),
                  spec(nb, n, tn, lambda j, r, c: (j if nb > 1 else 0, 0, c))],
        out_specs=pl.BlockSpec((None, tm, tn), lambda j, r, c: (j, r, c)),
        out_shape=jax.ShapeDtypeStruct((no, m, n), out_dtype),
        compiler_params=_cp("parallel", "parallel", "parallel"))(a, b)


def _mm_nt(dh, w, name, res=None, out_dtype=F32, w_rows_out=True, after=()):
    nc, s, kc = dh.shape
    d = w.shape[1] if w_rows_out else w.shape[2]
    ts = _rows(s)
    has_res = res is not None
    mm = _dot_nt if w_rows_out else _dot

    def body(*refs):
        o_ref = refs[-1]
        dh_ref, w_ref = refs[:2]
        if has_res:
            r_ref = refs[2]
        out = mm(dh_ref[0], w_ref[0])
        for j in range(1, nc):
            out = out + mm(dh_ref[j], w_ref[j])
        if has_res:
            out = out + ALPHA * r_ref[...]
        o_ref[...] = out.astype(o_ref.dtype)

    in_specs = [pl.BlockSpec((nc, ts, kc), lambda i: (0, i, 0)), _resident(w.shape, lambda i: (0, 0, 0))]
    args = [dh, w]
    if has_res:
        in_specs.append(pl.BlockSpec((ts, d), lambda i: (i, 0)))
        args.append(res)
    in_specs += [pl.BlockSpec(memory_space=pl.ANY)] * len(after)
    args += list(after)
    return pl.pallas_call(
        body, name=name, grid=(s // ts,), in_specs=in_specs,
        out_specs=pl.BlockSpec((ts, d), lambda i: (i, 0)),
        out_shape=jax.ShapeDtypeStruct((s, d), out_dtype),
        compiler_params=_cp("parallel"))(*args)


def _mm_res_ln(a, w, x, gain, bias, fscale, name):
    nc, s, kc = a.shape
    d = w.shape[2]
    ts = _rows(s)

    def body(a_ref, w_ref, x_ref, g_ref, b_ref, y_ref, yb_ref, xh_ref, r_ref):
        f = _dot(a_ref[0], w_ref[0])
        for j in range(1, nc):
            f = f + _dot(a_ref[j], w_ref[j])
        z = ALPHA * x_ref[...] + fscale * f
        mu = jnp.mean(z, axis=-1, keepdims=True)
        zc = z - mu
        var = jnp.mean(zc * zc, axis=-1, keepdims=True)
        r = lax.rsqrt(var + LN_EPS)
        xh = zc * r
        y = xh * g_ref[...] + b_ref[...]
        y_ref[...] = y
        yb_ref[...] = y.astype(BF)
        xh_ref[...] = xh
        r_ref[...] = r

    row = pl.BlockSpec((ts, d), lambda i: (i, 0))
    vec = pl.BlockSpec((1, d), lambda i: (0, 0))
    return pl.pallas_call(
        body, name=name, grid=(s // ts,),
        in_specs=[pl.BlockSpec((nc, ts, kc), lambda i: (0, i, 0)), _resident((nc, kc, d), lambda i: (0, 0, 0)),
                  row, vec, vec],
        out_specs=[row, row, row, pl.BlockSpec((ts, 1), lambda i: (i, 0))],
        out_shape=[jax.ShapeDtypeStruct((s, d), F32), jax.ShapeDtypeStruct((s, d), BF),
                   jax.ShapeDtypeStruct((s, d), F32), jax.ShapeDtypeStruct((s, 1), F32)],
        compiler_params=_cp("parallel"))(a, w, x, gain, bias)


def _ln_bwd(dy, xh, rstd, gain, fscale, name, after=()):
    s, d = dy.shape
    ts = _rows(s)
    na = len(after)

    def body(*refs):
        dy_ref, xh_ref, r_ref, g_ref = refs[:4]
        dz_ref, dzb_ref, dg_ref, db_ref = refs[4 + na:]
        i = pl.program_id(0)
        dyv = dy_ref[...]
        xhv = xh_ref[...]
        dxh = dyv * g_ref[...]
        m1 = jnp.mean(dxh, axis=-1, keepdims=True)
        m2 = jnp.mean(dxh * xhv, axis=-1, keepdims=True)
        dz = r_ref[...] * (dxh - m1 - xhv * m2)
        dz_ref[...] = dz
        dzb_ref[...] = (fscale * dz).astype(BF)

        @pl.when(i == 0)
        def _():
            dg_ref[...] = jnp.zeros_like(dg_ref)
            db_ref[...] = jnp.zeros_like(db_ref)

        dg_ref[...] += jnp.sum(dyv * xhv, axis=0, keepdims=True)
        db_ref[...] += jnp.sum(dyv, axis=0, keepdims=True)

    row = pl.BlockSpec((ts, d), lambda i: (i, 0))
    vec = pl.BlockSpec((1, d), lambda i: (0, 0))
    return pl.pallas_call(
        body, name=name, grid=(s // ts,),
        in_specs=[row, row, pl.BlockSpec((ts, 1), lambda i: (i, 0)), vec] + [pl.BlockSpec(memory_space=pl.ANY)] * na,
        out_specs=[row, row, vec, vec],
        out_shape=[jax.ShapeDtypeStruct((s, d), F32), jax.ShapeDtypeStruct((s, d), BF),
                   jax.ShapeDtypeStruct((1, d), F32), jax.ShapeDtypeStruct((1, d), F32)],
        compiler_params=_cp("arbitrary"))(dy, xh, rstd, gain, *after)


def _ffn_up(xb, wgu, name):
    s, d = xb.shape
    c = wgu.shape[1]
    nch = wgu.shape[0] // 2
    ts = _rows(s, 1024)
    w4 = wgu.reshape(2, nch, c, d)

    def body(x_ref, w_ref, gu_ref, a_ref):
        x = x_ref[...]
        g = _dot_nt(x, w_ref[0])
        u = _dot_nt(x, w_ref[1])
        sg = _sigmoid(g)
        t = g * sg
        gu_ref[0] = (u * (sg * (1.0 + g - t))).astype(BF)
        gu_ref[1] = t.astype(BF)
        a_ref[...] = (t * u).astype(BF)

    return pl.pallas_call(
        body, name=name, grid=(nch, s // ts),
        in_specs=[pl.BlockSpec((ts, d), lambda j, i: (i, 0)),
                  pl.BlockSpec((2, None, c, d), lambda j, i: (0, j, 0, 0))],
        out_specs=[pl.BlockSpec((2, None, ts, c), lambda j, i: (0, j, i, 0)),
                   pl.BlockSpec((None, ts, c), lambda j, i: (j, i, 0))],
        out_shape=[jax.ShapeDtypeStruct((2, nch, s, c), BF), jax.ShapeDtypeStruct((nch, s, c), BF)],
        compiler_params=_cp("parallel", "parallel"))(xb, w4)


def _ffn_fwd_main(x, xb, wgu, wd4, gain, bias, name):
    s, d = x.shape
    nch, c = wd4.shape[0], wd4.shape[1]
    ts = _rows(s, 256)

    def body(x_ref, xb_ref, wgu_ref, wd_ref, g_ref, b_ref, y_ref, yb_ref, gu_ref, a_ref, xh_ref, r_ref):
        xbv = xb_ref[...]
        f = jnp.zeros((ts, d), F32)
        for j in range(nch):
            g = _dot_nt(xbv, wgu_ref[j])
            u = _dot_nt(xbv, wgu_ref[nch + j])
            sg = _sigmoid(g)
            t = g * sg
            gu_ref[0, j] = (u * (sg * (1.0 + g - t))).astype(BF)
            gu_ref[1, j] = t.astype(BF)
            act = (t * u).astype(BF)
            a_ref[j] = act
            f = f + _dot(act, wd_ref[j])
        z = ALPHA * x_ref[...] + 0.5 * f
        mu = jnp.mean(z, axis=-1, keepdims=True)
        zc = z - mu
        var = jnp.mean(zc * zc, axis=-1, keepdims=True)
        r = lax.rsqrt(var + LN_EPS)
        xh = zc * r
        y = xh * g_ref[...] + b_ref[...]
        y_ref[...] = y
        yb_ref[...] = y.astype(BF)
        xh_ref[...] = xh
        r_ref[...] = r

    row = pl.BlockSpec((ts, d), lambda i: (i, 0))
    vec = pl.BlockSpec((1, d), lambda i: (0, 0))
    return pl.pallas_call(
        body, name=name, grid=(s // ts,),
        in_specs=[row, row, _resident(wgu.shape, lambda i: (0, 0, 0)), _resident(wd4.shape, lambda i: (0, 0, 0)),
                  vec, vec],
        out_specs=[row, row, pl.BlockSpec((2, nch, ts, c), lambda i: (0, 0, i, 0)),
                   pl.BlockSpec((nch, ts, c), lambda i: (0, i, 0)), row, pl.BlockSpec((ts, 1), lambda i: (i, 0))],
        out_shape=[jax.ShapeDtypeStruct((s, d), F32), jax.ShapeDtypeStruct((s, d), BF),
                   jax.ShapeDtypeStruct((2, nch, s, c), BF), jax.ShapeDtypeStruct((nch, s, c), BF),
                   jax.ShapeDtypeStruct((s, d), F32), jax.ShapeDtypeStruct((s, 1), F32)],
        compiler_params=_cp("parallel"))(x, xb, wgu, wd4, gain, bias)


def _ffn_bwd_main(dy, xh, rstd, gain, wd4, wgu, gu, name, after=(), with_dx=True):
    s, d = dy.shape
    nch, c = wd4.shape[0], wd4.shape[1]
    ts = _rows(s, 256)
    na = len(after)

    def body(*refs):
        dy_ref, xh_ref, r_ref, g_ref, wd_ref, wgu_ref, gu_ref = refs[:7]
        dx_ref, dzb_ref, dh_ref, dg_ref, db_ref = refs[7 + na:]
        i = pl.program_id(0)
        dyv = dy_ref[...]
        xhv = xh_ref[...]
        dxh = dyv * g_ref[...]
        m1 = jnp.mean(dxh, axis=-1, keepdims=True)
        m2 = jnp.mean(dxh * xhv, axis=-1, keepdims=True)
        dz = r_ref[...] * (dxh - m1 - xhv * m2)
        dzb = (0.5 * dz).astype(BF)
        dzb_ref[...] = dzb

        @pl.when(i == 0)
        def _():
            dg_ref[...] = jnp.zeros_like(dg_ref)
            db_ref[...] = jnp.zeros_like(db_ref)

        dg_ref[...] += jnp.sum(dyv * xhv, axis=0, keepdims=True)
        db_ref[...] += jnp.sum(dyv, axis=0, keepdims=True)

        dx = ALPHA * dz if with_dx else dz
        for j in range(nch):
            da = _dot_nt(dzb, wd_ref[j])
            dgate = (da * gu_ref[0, j].astype(F32)).astype(BF)
            dup = (da * gu_ref[1, j].astype(F32)).astype(BF)
            dh_ref[0, j] = dgate
            dh_ref[1, j] = dup
            if with_dx:
                dx = dx + _dot(dgate, wgu_ref[j]) + _dot(dup, wgu_ref[nch + j])
        dx_ref[...] = dx

    row = pl.BlockSpec((ts, d), lambda i: (i, 0))
    vec = pl.BlockSpec((1, d), lambda i: (0, 0))
    act = pl.BlockSpec((2, nch, ts, c), lambda i: (0, 0, i, 0))
    return pl.pallas_call(
        body, name=name, grid=(s // ts,),
        in_specs=[row, row, pl.BlockSpec((ts, 1), lambda i: (i, 0)), vec,
                  _resident(wd4.shape, lambda i: (0, 0, 0)), _resident(wgu.shape, lambda i: (0, 0, 0)), act]
                 + [pl.BlockSpec(memory_space=pl.ANY)] * na,
        out_specs=[row, row, act, vec, vec],
        out_shape=[jax.ShapeDtypeStruct((s, d), F32), jax.ShapeDtypeStruct((s, d), BF),
                   jax.ShapeDtypeStruct((2, nch, s, c), BF),
                   jax.ShapeDtypeStruct((1, d), F32), jax.ShapeDtypeStruct((1, d), F32)],
        compiler_params=_cp("arbitrary"))(dy, xh, rstd, gain, wd4, wgu, gu, *after)


def _rope_tables(s, sign):
    pos = jnp.arange(s, dtype=F32)
    inv_freq = 1.0 / (ROPE_THETA ** (jnp.arange(ROT_HALF, dtype=F32) / ROT_HALF))
    ang = pos[:, None] * inv_freq[None, :]
    cos, sin = jnp.cos(ang), jnp.sin(ang) * sign
    one = jnp.ones((s, HEAD_DIM - 2 * ROT_HALF), F32)
    zero = jnp.zeros((s, HEAD_DIM - 2 * ROT_HALF), F32)
    zh = jnp.zeros((s, ROT_HALF), F32)
    cos_f = jnp.concatenate([cos, cos, one], axis=1)
    sin_a = jnp.concatenate([-sin, zh, zero], axis=1)
    sin_b = jnp.concatenate([zh, sin, zero], axis=1)
    rep = LANES // HEAD_DIM
    return tuple(jnp.tile(t, (1, rep)) for t in (cos_f, sin_a, sin_b))


def _rope(t, c_ref, sa_ref, sb_ref):
    return (t * c_ref[...] + pltpu.roll(t, LANES - ROT_HALF, 1) * sa_ref[...]
            + pltpu.roll(t, ROT_HALF, 1) * sb_ref[...])


def _proj_rope(xb, w, tabs, n_rope, name, w_rows_out, tail_block=None):
    s, d = xb.shape
    n = w.shape[0] if w_rows_out else w.shape[1]
    tm = _rows(s, 256)
    has_tail = tail_block is not None

    def body(x_ref, w_ref, c_ref, sa_ref, sb_ref, o_ref, *tail_ref):
        h = (_dot_nt if w_rows_out else _dot)(x_ref[...], w_ref[...])
        for cb in range(n // LANES):
            t = h[:, cb * LANES:(cb + 1) * LANES]
            if cb < n_rope:
                t = _rope(t, c_ref, sa_ref, sb_ref)
            o_ref[:, cb * LANES:(cb + 1) * LANES] = t.astype(BF)
        if has_tail:
            tail_ref[0][...] = h[:, tail_block * LANES:(tail_block + 1) * LANES]

    tab = pl.BlockSpec((tm, LANES), lambda i: (i, 0))
    out_specs = [pl.BlockSpec((tm, n), lambda i: (i, 0))]
    out_shape = [jax.ShapeDtypeStruct((s, n), BF)]
    if has_tail:
        out_specs.append(tab)
        out_shape.append(jax.ShapeDtypeStruct((s, LANES), F32))
    res = pl.pallas_call(
        body, name=name, grid=(s // tm,),
        in_specs=[pl.BlockSpec((tm, d), lambda i: (i, 0)), _resident(w.shape, lambda i: (0, 0)), tab, tab, tab],
        out_specs=out_specs, out_shape=out_shape, compiler_params=_cp("parallel"))(xb, w, *tabs)
    return res if has_tail else res[0]


def _rope_cast(parts, tabs, n_rope, name, transposed=()):
    s = tabs[0].shape[0]
    flip = [i in transposed for i in range(len(parts))]
    widths = [p.shape[0] if f else p.shape[1] for p, f in zip(parts, flip)]
    n = sum(widths)
    npart = len(parts)
    ts = _rows(s, 256)

    def body(*refs):
        part_refs = refs[:npart]
        c_ref, sa_ref, sb_ref, o_ref = refs[npart:]
        col = 0
        for ref, w, f in zip(part_refs, widths, flip):
            for j in range(w // LANES):
                if f:
                    t = jnp.transpose(ref[j * LANES:(j + 1) * LANES, :])
                else:
                    t = ref[:, j * LANES:(j + 1) * LANES]
                if col < n_rope:
                    t = _rope(t, c_ref, sa_ref, sb_ref)
                o_ref[:, col * LANES:(col + 1) * LANES] = t.astype(BF)
                col += 1

    tab = pl.BlockSpec((ts, LANES), lambda i: (i, 0))
    return pl.pallas_call(
        body, name=name, grid=(s // ts,),
        in_specs=[pl.BlockSpec((w, ts), lambda i: (0, i)) if f else pl.BlockSpec((ts, w), lambda i: (i, 0))
                  for w, f in zip(widths, flip)] + [tab, tab, tab],
        out_specs=pl.BlockSpec((ts, n), lambda i: (i, 0)),
        out_shape=jax.ShapeDtypeStruct((s, n), BF),
        compiler_params=_cp("parallel"))(*parts, *tabs)


def _head_masks():
    lane = lax.broadcasted_iota(jnp.int32, (1, LANES), 1)
    return [lane < HEAD_DIM, lane >= HEAD_DIM]


def _sel(mask, v):
    return jnp.where(mask, v, jnp.zeros_like(v))


def _pick(mask, wide, fill):
    return jnp.max(jnp.where(mask, wide, fill), axis=1, keepdims=True)


def _band_masks(has_other, prev):
    qi = lax.broadcasted_iota(jnp.int32, (BLOCK, BLOCK), 0)
    kj = lax.broadcasted_iota(jnp.int32, (BLOCK, BLOCK), 1)
    if prev:
        return kj >= qi + jnp.where(has_other, 0, BLOCK)
    return kj <= qi


class _BandView:
    def __init__(self, s, g):
        self.r = 4 ** g
        self.nl = s // self.r
        self.nblk = self.nl // BLOCK
        self.nsub = min(BAND_SUB, self.nblk)
        self.tile = self.nsub * BLOCK
        self.ncols = min(self.r * GROUP_W // LANES, BAND_COLS)
        self.grid = (self.r * GROUP_W // LANES // self.ncols, self.nblk // self.nsub)

    def view(self, a):
        return a.reshape(self.nl, self.r * a.shape[1])

    def qkv(self, hb, g):
        npair = MIX_W // LANES
        offs = [i * npair + g * GROUP_W // LANES for i in range(3)]
        if self.r == 1:
            return [hb] * 3, hb.shape[1], offs
        return [self.view(hb[:, o * LANES:o * LANES + GROUP_W]) for o in offs], GROUP_W, [0, 0, 0]

    def specs(self, width, off):
        assert off % self.ncols == 0 and (width == GROUP_W or self.r == 1)
        nsub, last, lanes, first = self.nsub, self.nblk - 1, self.ncols * LANES, off // self.ncols
        return (pl.BlockSpec((self.tile, lanes), lambda cg, t: (t, first + cg)),
                pl.BlockSpec((BLOCK, lanes), lambda cg, t: (jnp.maximum(t * nsub - 1, 0), first + cg)),
                pl.BlockSpec((BLOCK, lanes), lambda cg, t: (jnp.minimum(t * nsub + nsub, last), first + cg)))


def _band_fwd(hb, g, name):
    s, n = hb.shape
    bv = _BandView(s, g)
    nsub = bv.nsub
    npair = MIX_W // LANES

    def body(q_ref, kc_ref, kp_ref, vc_ref, vp_ref, o_ref, l_ref):
        t = pl.program_id(1)
        mc = _band_masks(None, False)
        hm = _head_masks()
        for i, c in [(i, c) for i in range(nsub) for c in range(bv.ncols)]:
            rows = slice(i * BLOCK, (i + 1) * BLOCK)
            lanes = slice(c * LANES, (c + 1) * LANES)
            has_prev = t > 0 if i == 0 else True
            mp = _band_masks(has_prev, True)
            q, kc, vc = q_ref[rows, lanes], kc_ref[rows, lanes], vc_ref[rows, lanes]
            if i == 0:
                kp, vp = kp_ref[:, lanes], vp_ref[:, lanes]
            else:
                prev = slice((i - 1) * BLOCK, i * BLOCK)
                kp, vp = kc_ref[prev, lanes], vc_ref[prev, lanes]
            o = jnp.zeros((BLOCK, LANES), F32)
            lse_w = jnp.zeros((BLOCK, LANES), F32)
            for h in range(2):
                qh = _sel(hm[h], q)
                sc = jnp.where(mc, _dot_nt(qh, kc) * SCALE, NEG)
                sp = jnp.where(mp, _dot_nt(qh, kp) * SCALE, NEG)
                m = jnp.maximum(jnp.max(sc, axis=1, keepdims=True), jnp.max(sp, axis=1, keepdims=True))
                pc = jnp.exp(sc - m)
                pp = jnp.exp(sp - m)
                l = jnp.sum(pc, axis=1, keepdims=True) + jnp.sum(pp, axis=1, keepdims=True)
                oh = _dot(pc.astype(BF), _sel(hm[h], vc)) + _dot(pp.astype(BF), _sel(hm[h], vp))
                o = o + oh / l
                lse_w = jnp.where(hm[h], m + jnp.log(l), lse_w)
            o_ref[rows, lanes] = o
            l_ref[rows, lanes] = lse_w

    (qv, kv_, vv), width, (qo, ko, vo) = bv.qkv(hb, g)
    q_cur, _, _ = bv.specs(width, qo)
    k_cur, k_prv, _ = bv.specs(width, ko)
    v_cur, v_prv, _ = bv.specs(width, vo)
    out_spec = bv.specs(GROUP_W, 0)[0]
    out = jax.ShapeDtypeStruct((bv.nl, bv.r * GROUP_W), F32)
    o, l = pl.pallas_call(
        body, name=name, grid=bv.grid,
        in_specs=[q_cur, k_cur, k_prv, v_cur, v_prv], out_specs=[out_spec, out_spec], out_shape=[out, out],
        compiler_params=_cp("parallel", "parallel"))(qv, kv_, kv_, vv, vv)
    return o.reshape(s, GROUP_W), l.reshape(s, GROUP_W)


def _band_combine(os, ls, name):
    ng = len(os)
    s, w = os[0].shape
    ts = _rows(s)

    def body(*refs):
        o_refs, l_refs = refs[:ng], refs[ng:2 * ng]
        oa_ref, lt_ref = refs[2 * ng:]
        lv = [r[...] for r in l_refs]
        m = functools.reduce(jnp.maximum, lv)
        es = [jnp.exp(l - m) for l in lv]
        den = functools.reduce(lambda a, b: a + b, es)
        num = functools.reduce(lambda a, b: a + b, [es[g] * o_refs[g][...] for g in range(ng)])
        oa_ref[...] = (num / den).astype(BF)
        lt_ref[...] = m + jnp.log(den)

    blk = pl.BlockSpec((ts, w), lambda i: (i, 0))
    return pl.pallas_call(
        body, name=name, grid=(s // ts,), in_specs=[blk] * (2 * ng), out_specs=[blk, blk],
        out_shape=[jax.ShapeDtypeStruct((s, w), BF), jax.ShapeDtypeStruct((s, w), F32)],
        compiler_params=_cp("parallel"))(*os, *ls)


def _band_bwd(hb, dcat, oa, lt, g, name):
    s, n = hb.shape
    bv = _BandView(s, g)
    nsub = bv.nsub
    npair = MIX_W // LANES
    ntile = bv.grid[1]

    def body(q_ref, qn_ref, kc_ref, kp_ref, vc_ref, vp_ref, do_ref, don_ref, oa_ref, oan_ref, lt_ref, ltn_ref,
             dq_ref, dk_ref, dv_ref):
        t = pl.program_id(1)
        mc = _band_masks(None, False)
        hm = _head_masks()

        for i, c in [(i, c) for i in range(nsub) for c in range(bv.ncols)]:
            lanes = slice(c * LANES, (c + 1) * LANES)

            def block(ref, edge_ref, i, lanes=lanes):
                if i < 0 or i >= nsub:
                    return edge_ref[:, lanes]
                return ref[i * BLOCK:(i + 1) * BLOCK, lanes]

            mp = _band_masks(t > 0 if i == 0 else True, True)
            mn = _band_masks(t < ntile - 1 if i == nsub - 1 else True, True)
            q, qn = block(q_ref, None, i), block(q_ref, qn_ref, i + 1)
            kc, kp = block(kc_ref, None, i), block(kc_ref, kp_ref, i - 1)
            vc, vp = block(vc_ref, None, i), block(vc_ref, vp_ref, i - 1)
            do, don = block(do_ref, None, i), block(do_ref, don_ref, i + 1)
            dd = do.astype(F32) * block(oa_ref, None, i).astype(F32)
            ddn = don.astype(F32) * block(oa_ref, oan_ref, i + 1).astype(F32)
            lt, ltn = block(lt_ref, None, i), block(lt_ref, ltn_ref, i + 1)
            dq = jnp.zeros((BLOCK, LANES), F32)
            dk = jnp.zeros((BLOCK, LANES), F32)
            dv = jnp.zeros((BLOCK, LANES), F32)
            for h in range(2):
                qh, doh = _sel(hm[h], q), _sel(hm[h], do)
                qnh, donh = _sel(hm[h], qn), _sel(hm[h], don)
                kch, kph = _sel(hm[h], kc), _sel(hm[h], kp)
                lse = _pick(hm[h], lt, NEG)
                lsen = _pick(hm[h], ltn, NEG)
                dsum = jnp.sum(_sel(hm[h], dd), axis=1, keepdims=True)
                dsumn = jnp.sum(_sel(hm[h], ddn), axis=1, keepdims=True)
                pc = jnp.exp(jnp.where(mc, _dot_nt(qh, kc) * SCALE, NEG) - lse)
                pp = jnp.exp(jnp.where(mp, _dot_nt(qh, kp) * SCALE, NEG) - lse)
                dsc = pc * (_dot_nt(doh, vc) - dsum)
                dsp = pp * (_dot_nt(doh, vp) - dsum)
                dq = dq + SCALE * (_dot(dsc.astype(BF), kch) + _dot(dsp.astype(BF), kph))
                pn = jnp.exp(jnp.where(mn, _dot_nt(qnh, kc) * SCALE, NEG) - lsen)
                dsn = pn * (_dot_nt(donh, vc) - dsumn)
                dk = dk + SCALE * (_dot_tn(dsc.astype(BF), qh) + _dot_tn(dsn.astype(BF), qnh))
                dv = dv + _dot_tn(pc.astype(BF), doh) + _dot_tn(pn.astype(BF), donh)
            rows = slice(i * BLOCK, (i + 1) * BLOCK)
            dq_ref[rows, lanes] = dq
            dk_ref[rows, lanes] = dk
            dv_ref[rows, lanes] = dv

    (qv, kv_, vv), width, (qo, ko, vo) = bv.qkv(hb, g)
    q_cur, _, q_nxt = bv.specs(width, qo)
    k_cur, k_prv, _ = bv.specs(width, ko)
    v_cur, v_prv, _ = bv.specs(width, vo)
    w_cur, _, w_nxt = bv.specs(GROUP_W, 0)
    out = jax.ShapeDtypeStruct((bv.nl, bv.r * GROUP_W), F32)
    dv_, ov, lv = bv.view(dcat[:, :GROUP_W]), bv.view(oa), bv.view(lt)
    res = pl.pallas_call(
        body, name=name, grid=bv.grid,
        in_specs=[q_cur, q_nxt, k_cur, k_prv, v_cur, v_prv, w_cur, w_nxt, w_cur, w_nxt, w_cur, w_nxt],
        out_specs=[w_cur, w_cur, w_cur], out_shape=[out, out, out],
        compiler_params=_cp("parallel", "parallel"))(
            qv, qv, kv_, kv_, vv, vv, dv_, dv_, ov, ov, lv, lv)
    return [t.reshape(s, GROUP_W) for t in res]


def _mem_fwd(hb, q_blk0, kv, name):
    s = hb.shape[0]
    m = kv.shape[0]
    tq = _rows(s)
    npair = MEM_W // LANES

    def body(q_ref, k_ref, v_ref, o_ref, l_ref):
        q, k, v = q_ref[...], k_ref[...], v_ref[...]
        hm = _head_masks()
        o = jnp.zeros((tq, LANES), F32)
        lse_w = jnp.zeros((tq, LANES), F32)
        for h in range(2):
            sc = _dot_nt(_sel(hm[h], q), k) * SCALE
            mx = jnp.max(sc, axis=1, keepdims=True)
            p = jnp.exp(sc - mx)
            l = jnp.sum(p, axis=1, keepdims=True)
            o = o + _dot(p.astype(BF), _sel(hm[h], v)) / l
            lse_w = jnp.where(hm[h], mx + jnp.log(l), lse_w)
        o_ref[...] = o.astype(BF)
        l_ref[...] = lse_w

    blk = pl.BlockSpec((tq, LANES), lambda p, i: (i, p))
    return pl.pallas_call(
        body, name=name, grid=(npair, s // tq),
        in_specs=[pl.BlockSpec((tq, LANES), lambda p, i: (i, q_blk0 + p)),
                  pl.BlockSpec((m, LANES), lambda p, i: (0, p)),
                  pl.BlockSpec((m, LANES), lambda p, i: (0, npair + p))],
        out_specs=[blk, blk],
        out_shape=[jax.ShapeDtypeStruct((s, MEM_W), BF), jax.ShapeDtypeStruct((s, MEM_W), F32)],
        compiler_params=_cp("parallel", "parallel"))(hb, kv, kv)


def _mem_bwd(hb, q_blk0, kv, dcat, cat, o_blk0, lse, name):
    s = hb.shape[0]
    m = kv.shape[0]
    tq = _rows(s)
    npair = MEM_W // LANES

    def body(q_ref, k_ref, v_ref, do_ref, o_ref, l_ref, dq_ref, dk_ref, dv_ref):
        i = pl.program_id(1)

        @pl.when(i == 0)
        def _():
            dk_ref[...] = jnp.zeros_like(dk_ref)
            dv_ref[...] = jnp.zeros_like(dv_ref)

        q, k, v, do = q_ref[...], k_ref[...], v_ref[...], do_ref[...]
        dd = do.astype(F32) * o_ref[...].astype(F32)
        lt = l_ref[...]
        hm = _head_masks()
        dq = jnp.zeros((tq, LANES), F32)
        dk = jnp.zeros((m, LANES), F32)
        dv = jnp.zeros((m, LANES), F32)
        for h in range(2):
            qh, doh = _sel(hm[h], q), _sel(hm[h], do)
            p = jnp.exp(_dot_nt(qh, k) * SCALE - _pick(hm[h], lt, NEG))
            ds = p * (_dot_nt(doh, v) - jnp.sum(_sel(hm[h], dd), axis=1, keepdims=True))
            dq = dq + SCALE * _dot(ds.astype(BF), _sel(hm[h], k))
            dk = dk + SCALE * _dot_tn(ds.astype(BF), qh)
            dv = dv + _dot_tn(p.astype(BF), doh)
        dq_ref[...] = dq
        dk_ref[...] += dk
        dv_ref[...] += dv

    row = pl.BlockSpec((tq, LANES), lambda p, i: (i, p))
    orow = pl.BlockSpec((tq, LANES), lambda p, i: (i, o_blk0 + p))
    acc = pl.BlockSpec((m, LANES), lambda p, i: (0, p))
    return pl.pallas_call(
        body, name=name, grid=(npair, s // tq),
        in_specs=[pl.BlockSpec((tq, LANES), lambda p, i: (i, q_blk0 + p)),
                  pl.BlockSpec((m, LANES), lambda p, i: (0, p)),
                  pl.BlockSpec((m, LANES), lambda p, i: (0, npair + p)), orow, orow, row],
        out_specs=[row, acc, acc],
        out_shape=[jax.ShapeDtypeStruct((s, MEM_W), F32), jax.ShapeDtypeStruct((m, MEM_W), F32),
                   jax.ShapeDtypeStruct((m, MEM_W), F32)],
        compiler_params=_cp("parallel", "arbitrary"))(hb, kv, kv, dcat, cat, lse)


def _gate_fwd(f_t, bias, name):
    hp, s = f_t.shape
    nblk = s // LANES

    def body(f_ref, b_ref, c_ref):
        lane = lax.broadcasted_iota(jnp.int32, (hp, LANES), 1)

        def step(i, carry):
            off = pl.multiple_of(i * LANES, LANES)
            x = f_ref[:, pl.ds(off, LANES)] + b_ref[...]
            acc = jnp.minimum(x, 0.0) - jnp.log(1.0 + jnp.exp(-jnp.abs(x)))
            sh = 1
            while sh < LANES:
                acc = acc + jnp.where(lane >= sh, pltpu.roll(acc, sh, 1), 0.0)
                sh *= 2
            acc = acc + carry
            c_ref[:, pl.ds(off, LANES)] = acc
            return acc[:, LANES - 1:LANES]

        lax.fori_loop(0, nblk, step, jnp.zeros((hp, 1), F32))

    vm = pl.BlockSpec(memory_space=pltpu.VMEM)
    return pl.pallas_call(body, name=name, in_specs=[vm, vm], out_specs=vm,
                          out_shape=jax.ShapeDtypeStruct((hp, s), F32),
                          compiler_params=pltpu.CompilerParams(vmem_limit_bytes=VMEM_LIMIT))(f_t, bias)


def _gate_bwd(dc_t, f_t, bias, name):
    hp, s = f_t.shape
    nblk = s // LANES

    def body(dc_ref, f_ref, b_ref, df_ref, db_ref):
        lane = lax.broadcasted_iota(jnp.int32, (hp, LANES), 1)

        def step(t, carry):
            suffix, dbias = carry
            off = pl.multiple_of((nblk - 1 - t) * LANES, LANES)
            acc = dc_ref[:, pl.ds(off, LANES)]
            sh = 1
            while sh < LANES:
                acc = acc + jnp.where(lane < LANES - sh, pltpu.roll(acc, LANES - sh, 1), 0.0)
                sh *= 2
            acc = acc + suffix
            x = f_ref[:, pl.ds(off, LANES)] + b_ref[...]
            df = acc * _sigmoid(-x)
            df_ref[:, pl.ds(off, LANES)] = df
            return acc[:, 0:1], dbias + jnp.sum(df, axis=1, keepdims=True)

        _, dbias = lax.fori_loop(0, nblk, step, (jnp.zeros((hp, 1), F32), jnp.zeros((hp, 1), F32)))
        db_ref[...] = dbias

    vm = pl.BlockSpec(memory_space=pltpu.VMEM)
    return pl.pallas_call(body, name=name, in_specs=[vm, vm, vm], out_specs=[vm, vm],
                          out_shape=[jax.ShapeDtypeStruct((hp, s), F32), jax.ShapeDtypeStruct((hp, 1), F32)],
                          compiler_params=pltpu.CompilerParams(vmem_limit_bytes=VMEM_LIMIT))(dc_t, f_t, bias)


def _wide(rep, width):
    return jnp.tile(rep, (1, width // LANES))


def _fold(t):
    part = t[:, :LANES]
    for c in range(1, t.shape[1] // LANES):
        part = part + t[:, c * LANES:(c + 1) * LANES]
    return part


def _foxt_logits(q, k, cq_row, ck_rep, mask, hmask):
    s = _dot_nt(_sel(hmask, k), q) + (cq_row - _wide(ck_rep, q.shape[0]))
    if mask is not None:
        s = jnp.where(mask, s, NEG)
    return s


def _causal_t(qi, kj, tq, tk):
    return (kj * tk + lax.broadcasted_iota(jnp.int32, (tk, tq), 0)
            <= qi * tq + lax.broadcasted_iota(jnp.int32, (tk, tq), 1))


FOX_SPLIT = 1


def _fox_tiles(s):
    tq = _rows(s, 1024)
    return tq, tq // FOX_SPLIT, s // tq


def _fox_steps(nq):
    return FOX_SPLIT * nq * (nq + 1) // 2


def _count_ge(t, bounds):
    return sum([(t >= b).astype(jnp.int32) for b in bounds], jnp.int32(0))


def _sweep_q_major(t, nq):
    qi = _count_ge(t, [FOX_SPLIT * r * (r + 1) // 2 for r in range(1, nq)])
    return qi, t - FOX_SPLIT * qi * (qi + 1) // 2


def _sweep_k_major(t, nq):
    counts = [nq - j // FOX_SPLIT for j in range(FOX_SPLIT * nq)]
    offs = [sum(counts[:j]) for j in range(1, FOX_SPLIT * nq)]
    kj = _count_ge(t, offs)
    start = sum([jnp.where(t >= o, c, 0) for o, c in zip(offs, counts)], jnp.int32(0))
    qi = kj // FOX_SPLIT + (t - start)
    return kj, qi, t == start, qi == nq - 1


def _foxt_fwd(hb, c_rep, c_t3, name):
    s = hb.shape[0]
    npair = MIX_W // LANES
    tq, tk, nq = _fox_tiles(s)

    def body(q_ref, k_ref, v_ref, cq_ref, ck_ref, o_ref, l_ref, m_s, l_s, acc):
        qi, kj = _sweep_q_major(pl.program_id(1), nq)
        hm = _head_masks()

        @pl.when(kj == 0)
        def _():
            m_s[...] = jnp.full_like(m_s, NEG)
            l_s[...] = jnp.zeros_like(l_s)
            acc[...] = jnp.zeros_like(acc)

        def step(mask):
            q, k = q_ref[...] * SCALE, k_ref[...]
            vt = jnp.transpose(v_ref[...])
            cq = cq_ref[...]
            for h in range(2):
                st = _foxt_logits(q, k, cq[h:h + 1, :], ck_ref[h], mask, hm[h])
                m_old = m_s[h]
                m_new = jnp.maximum(m_old, jnp.max(st, axis=0, keepdims=True))
                pt = jnp.exp(st - m_new)
                corr = jnp.exp(m_old - m_new)
                l_s[h] = l_s[h] * corr + jnp.sum(pt, axis=0, keepdims=True)
                acc[h] = acc[h] * corr + _dot(vt[h * HEAD_DIM:(h + 1) * HEAD_DIM, :], pt.astype(BF))
                m_s[h] = m_new

        @pl.when(kj < FOX_SPLIT * qi)
        def _():
            step(None)

        @pl.when(kj >= FOX_SPLIT * qi)
        def _():
            step(_causal_t(qi, kj, tq, tk))

        @pl.when(kj == FOX_SPLIT * (qi + 1) - 1)
        def _():
            outs = []
            for h in range(2):
                outs.append(acc[h] / l_s[h])
                l_ref[h:h + 1, :] = m_s[h] + jnp.log(l_s[h])
            o_ref[...] = jnp.transpose(jnp.concatenate(outs, axis=0)).astype(BF)

    def q_map(p, t):
        return (_sweep_q_major(t, nq)[0], p)

    def kv_map(off):
        return lambda p, t: (_sweep_q_major(t, nq)[1], off + p)

    blk = pl.BlockSpec((tq, LANES), q_map)
    row = pl.BlockSpec((None, 2, tq), lambda p, t: (p, 0, _sweep_q_major(t, nq)[0]))
    return pl.pallas_call(
        body, name=name, grid=(npair, _fox_steps(nq)),
        in_specs=[blk, pl.BlockSpec((tk, LANES), kv_map(npair)), pl.BlockSpec((tk, LANES), kv_map(2 * npair)), row,
                  pl.BlockSpec((2, tk, LANES), lambda p, t: (p, _sweep_q_major(t, nq)[1], 0))],
        out_specs=[blk, row],
        out_shape=[jax.ShapeDtypeStruct((s, MIX_W), BF), jax.ShapeDtypeStruct((npair, 2, s), F32)],
        scratch_shapes=[pltpu.VMEM((2, 1, tq), F32), pltpu.VMEM((2, 1, tq), F32),
                        pltpu.VMEM((2, HEAD_DIM, tq), F32)],
        compiler_params=_cp("parallel", "arbitrary"))(hb, hb, hb, c_t3, c_rep)


def _foxt_dsum(hb, dcat, lse, c_rep, c_t3, name):
    s = hb.shape[0]
    npair = MIX_W // LANES
    tq, tk, nq = _fox_tiles(s)

    def body(q_ref, k_ref, v_ref, do_ref, l_ref, cq_ref, ck_ref, d_ref, acc):
        qi, kj = _sweep_q_major(pl.program_id(1), nq)
        hm = _head_masks()

        @pl.when(kj == 0)
        def _():
            acc[...] = jnp.zeros_like(acc)

        def step(mask):
            q, k, v, do = q_ref[...] * SCALE, k_ref[...], v_ref[...], do_ref[...]
            cq, lse_rows = cq_ref[...], l_ref[...]
            for h in range(2):
                pt = jnp.exp(_foxt_logits(q, k, cq[h:h + 1, :], ck_ref[h], mask, hm[h]) - lse_rows[h:h + 1, :])
                acc[h] += jnp.sum(pt * _dot_nt(_sel(hm[h], v), do), axis=0, keepdims=True)

        @pl.when(kj < FOX_SPLIT * qi)
        def _():
            step(None)

        @pl.when(kj >= FOX_SPLIT * qi)
        def _():
            step(_causal_t(qi, kj, tq, tk))

        @pl.when(kj == FOX_SPLIT * (qi + 1) - 1)
        def _():
            for h in range(2):
                d_ref[h:h + 1, :] = acc[h]

    def q_map(p, t):
        return (_sweep_q_major(t, nq)[0], p)

    def kv_map(off):
        return lambda p, t: (_sweep_q_major(t, nq)[1], off + p)

    blk = pl.BlockSpec((tq, LANES), q_map)
    row = pl.BlockSpec((None, 2, tq), lambda p, t: (p, 0, _sweep_q_major(t, nq)[0]))
    return pl.pallas_call(
        body, name=name, grid=(npair, _fox_steps(nq)),
        in_specs=[blk, pl.BlockSpec((tk, LANES), kv_map(npair)), pl.BlockSpec((tk, LANES), kv_map(2 * npair)),
                  blk, row, row, pl.BlockSpec((2, tk, LANES), lambda p, t: (p, _sweep_q_major(t, nq)[1], 0))],
        out_specs=row, out_shape=jax.ShapeDtypeStruct((npair, 2, s), F32),
        scratch_shapes=[pltpu.VMEM((2, 1, tq), F32)],
        compiler_params=_cp("parallel", "arbitrary"))(hb, hb, hb, dcat, lse, c_t3, c_rep)


def _foxt_bwd(hb, dcat, dsum, lse, c_rep, c_t3, name):
    s = hb.shape[0]
    npair = MIX_W // LANES
    tq, tk, nq = _fox_tiles(s)

    def body(q_ref, k_ref, v_ref, do_ref, d_ref, l_ref, cq_ref, ck_ref, dq_ref, dk_ref, dv_ref, dc_ref, dc_s):
        t = pl.program_id(1)
        kj, qi, first, last = _sweep_k_major(t, nq)
        hm = _head_masks()

        @pl.when(first)
        def _():
            dk_ref[...] = jnp.zeros_like(dk_ref)
            dv_ref[...] = jnp.zeros_like(dv_ref)
            dc_s[...] = jnp.zeros_like(dc_s)

        @pl.when(t == 0)
        def _():
            dq_ref[...] = jnp.zeros_like(dq_ref)

        def step(mask):
            q, k, v, do = q_ref[...] * SCALE, k_ref[...], v_ref[...], do_ref[...]
            qt, kt, dot = jnp.transpose(q), jnp.transpose(k), jnp.transpose(do)
            cq, lse_rows, d_rows = cq_ref[...], l_ref[...], d_ref[...]
            dqs, dks, dvs = [], [], []
            for h in range(2):
                rows = slice(h * HEAD_DIM, (h + 1) * HEAD_DIM)
                pt = jnp.exp(_foxt_logits(q, k, cq[h:h + 1, :], ck_ref[h], mask, hm[h]) - lse_rows[h:h + 1, :])
                dst = pt * (_dot_nt(_sel(hm[h], v), do) - d_rows[h:h + 1, :])
                dsb = dst.astype(BF)
                dqs.append(_dot(kt[rows, :], dsb))
                dks.append(_dot_nt(qt[rows, :], dsb))
                dvs.append(_dot_nt(dot[rows, :], pt.astype(BF)))
                dc_s[h] += _fold(dst)
            cols = pl.ds(pl.multiple_of(qi * tq, tq), tq)
            dq_ref[:, cols] += SCALE * jnp.concatenate(dqs, axis=0)
            dk_ref[...] += jnp.concatenate(dks, axis=0)
            dv_ref[...] += jnp.concatenate(dvs, axis=0)

        @pl.when(kj < FOX_SPLIT * qi)
        def _():
            step(None)

        @pl.when(kj >= FOX_SPLIT * qi)
        def _():
            step(_causal_t(qi, kj, tq, tk))

        @pl.when(last)
        def _():
            for h in range(2):
                dc_ref[h:h + 1, :] = -jnp.sum(jnp.transpose(dc_s[h]), axis=0, keepdims=True)

    def kj_of(t):
        return _sweep_k_major(t, nq)[0]

    def qi_of(t):
        return _sweep_k_major(t, nq)[1]

    qblk = pl.BlockSpec((tq, LANES), lambda p, t: (qi_of(t), p))
    row = pl.BlockSpec((None, 2, tq), lambda p, t: (p, 0, qi_of(t)))
    kblk = pl.BlockSpec((LANES, tk), lambda p, t: (p, kj_of(t)))
    rep = pl.BlockSpec((2, tk, LANES), lambda p, t: (p, kj_of(t), 0))
    return pl.pallas_call(
        body, name=name, grid=(npair, _fox_steps(nq)),
        in_specs=[qblk,
                  pl.BlockSpec((tk, LANES), lambda p, t: (kj_of(t), npair + p)),
                  pl.BlockSpec((tk, LANES), lambda p, t: (kj_of(t), 2 * npair + p)),
                  qblk, row, row, row, rep],
        out_specs=[pl.BlockSpec((LANES, s), lambda p, t: (p, 0)), kblk, kblk,
                   pl.BlockSpec((None, 2, tk), lambda p, t: (p, 0, kj_of(t)))],
        out_shape=[jax.ShapeDtypeStruct((MIX_W, s), F32), jax.ShapeDtypeStruct((MIX_W, s), F32),
                   jax.ShapeDtypeStruct((MIX_W, s), F32), jax.ShapeDtypeStruct((npair, 2, s), F32)],
        scratch_shapes=[pltpu.VMEM((2, tk, LANES), F32)],
        compiler_params=_cp("arbitrary", "arbitrary"))(hb, hb, hb, dcat, dsum, lse, c_t3, c_rep)


def _loss_head(y, target, name):
    s, d = y.shape
    ts = _rows(s)

    def body(y_ref, t_ref, dy_ref, l_ref):
        i = pl.program_id(0)
        e = y_ref[...] - t_ref[...]
        dy_ref[...] = e * (1.0 / d)

        @pl.when(i == 0)
        def _():
            l_ref[...] = jnp.zeros_like(l_ref)

        part = jnp.sum(jnp.sum(e * e, axis=1, keepdims=True), axis=0, keepdims=True)
        l_ref[...] += part * (0.5 / d)

    row = pl.BlockSpec((ts, d), lambda i: (i, 0))
    return pl.pallas_call(
        body, name=name, grid=(s // ts,), in_specs=[row, row],
        out_specs=[row, pl.BlockSpec((1, 1), lambda i: (0, 0))],
        out_shape=[jax.ShapeDtypeStruct((s, d), F32), jax.ShapeDtypeStruct((1, 1), F32)],
        compiler_params=_cp("arbitrary"))(y, target)


def _adam_rows(r, c):
    cap = max(8, (1 << 20) // (4 * c))
    if r <= cap:
        return r
    best = None
    for t in range(8, cap + 1, 8):
        if r % t == 0:
            best = t
    return best if best is not None else r


def _reduce_adamw(contribs, w, m, v, name):
    nl = len(contribs)
    nd, r, c = contribs[0].shape
    tr = _adam_rows(r, c)
    bc1 = 1.0 - ADAM_B1 ** ADAM_STEP
    bc2 = 1.0 - ADAM_B2 ** ADAM_STEP

    def body(*refs):
        c_refs = refs[:nl]
        w_ref, m_ref, v_ref, g_ref, d_ref, nm_ref, nv_ref = refs[nl:]
        l = pl.program_id(0)
        for li in range(nl):
            @pl.when(l == li)
            def _(c_ref=c_refs[li]):
                g = c_ref[0].astype(F32)
                for k in range(1, nd):
                    g = g + c_ref[k].astype(F32)
                nm = ADAM_B1 * m_ref[...] + (1.0 - ADAM_B1) * g
                nv = ADAM_B2 * v_ref[...] + (1.0 - ADAM_B2) * (g * g)
                g_ref[...] = g
                nm_ref[...] = nm
                nv_ref[...] = nv
                d_ref[...] = -ADAM_LR * ((nm / bc1) / (jnp.sqrt(nv / bc2) + ADAM_EPS) + ADAM_WD * w_ref[...])

    def c_spec(li):
        return pl.BlockSpec((nd, tr, c), lambda l, i: (0, jnp.where(l == li, i, 0), 0))

    blk = pl.BlockSpec((None, tr, c), lambda l, i: (l, i, 0))
    out = jax.ShapeDtypeStruct((nl, r, c), F32)
    return pl.pallas_call(
        body, name=name, grid=(nl, r // tr),
        in_specs=[c_spec(li) for li in range(nl)] + [blk, blk, blk],
        out_specs=[blk, blk, blk, blk], out_shape=[out, out, out, out],
        compiler_params=_cp("arbitrary", "arbitrary"))(*contribs, w, m, v)


def _mesh_pos():
    return lax.axis_index("x"), lax.axis_index("y"), lax.axis_index("c")


def _peer(pos, k):
    x, y, c = pos
    return (1 - x if k & 4 else x, 1 - y if k & 2 else y, 1 - c if k & 1 else c)


def _linear(pos):
    return 4 * pos[0] + 2 * pos[1] + pos[2]


def _xfer_copies(srcs, lands, send_sems, recv_sems, local_sems, gather):
    pos = _mesh_pos()
    me = _linear(pos)
    local, remote = [], []
    for i, (src, land) in enumerate(zip(srcs, lands)):
        local.append(pltpu.make_async_copy(src if gather else src.at[me], land.at[me], local_sems.at[i]))
        for k in range(1, N_DEV):
            peer = _peer(pos, k)
            remote.append(pltpu.make_async_remote_copy(
                src_ref=src if gather else src.at[_linear(peer)], dst_ref=land.at[me],
                send_sem=send_sems.at[i * (N_DEV - 1) + k - 1], recv_sem=recv_sems.at[i * (N_DEV - 1) + k - 1],
                device_id=peer, device_id_type=MESH_ID))
    return local, remote


_HBM = pl.BlockSpec(memory_space=pltpu.HBM)
_SEM = pl.BlockSpec(memory_space=pltpu.SEMAPHORE)
_EFFECT = pltpu.SideEffectType.DATAFLOW_SIDE_EFFECTING


def _xfer_start(srcs, gather, name, after=()):
    n = len(srcs)
    na = len(after)
    lands = [lax.empty(((N_DEV,) + a.shape) if gather else a.shape, a.dtype) for a in srcs]

    def body(*refs):
        src, land = refs[:n], refs[n:2 * n]
        send_sems, recv_sems, local_sems = refs[2 * n + na:2 * n + na + 3]
        local, remote = _xfer_copies(src, land, send_sems, recv_sems, local_sems, gather)
        for cp in local + remote:
            cp.start()
        refs[-1][...] = jnp.zeros_like(refs[-1])

    nsem = n * (N_DEV - 1)
    out = pl.pallas_call(
        body, name=name,
        out_shape=(pltpu.SemaphoreType.DMA((nsem,)), pltpu.SemaphoreType.DMA((nsem,)), pltpu.SemaphoreType.DMA((n,)),
                   *[pltpu.HBM(a.shape, a.dtype) for a in srcs], *[pltpu.HBM(a.shape, a.dtype) for a in lands],
                   jax.ShapeDtypeStruct((8, LANES), F32)),
        in_specs=[_HBM] * (2 * n) + [pl.BlockSpec(memory_space=pl.ANY)] * na,
        out_specs=(_SEM, _SEM, _SEM, *[_HBM] * (2 * n), pl.BlockSpec(memory_space=pltpu.VMEM)),
        input_output_aliases={i: 3 + i for i in range(2 * n)},
        compiler_params=pltpu.CompilerParams(has_side_effects=_EFFECT))(
            *[pltpu.with_memory_space_constraint(a, pltpu.HBM) for a in srcs],
            *[pltpu.with_memory_space_constraint(a, pltpu.HBM) for a in lands], *after)
    return out[:3], list(out[3:3 + n]), list(out[3 + n:3 + 2 * n]), out[-1]


def _started(handle):
    return handle[3]


def _xfer_wait(handle, after, gather, name):
    sems, srcs, lands, _ = handle
    n = len(srcs)

    def body(*refs):
        src, land = refs[:n], refs[n:2 * n]
        send_sems, recv_sems, local_sems = refs[2 * n:2 * n + 3]
        local, remote = _xfer_copies(src, land, send_sems, recv_sems, local_sems, gather)
        for cp in local:
            cp.wait()
        for cp in remote:
            cp.wait_send()
            cp.wait_recv()

    out = pl.pallas_call(
        body, name=name,
        out_shape=(*[pltpu.HBM(a.shape, a.dtype) for a in srcs], *[pltpu.HBM(a.shape, a.dtype) for a in lands]),
        in_specs=[_HBM] * (2 * n) + [_SEM] * 3 + [pl.BlockSpec(memory_space=pl.ANY)] * len(after),
        out_specs=tuple([_HBM] * (2 * n)), input_output_aliases={i: i for i in range(2 * n)},
        compiler_params=pltpu.CompilerParams(has_side_effects=_EFFECT))(*srcs, *lands, *sems, *after)
    return list(out[n:])


def _cols_full(g):
    nd, r, c = g.shape
    return jnp.transpose(g, (1, 0, 2)).reshape(r, nd * c)


def _cols_split(full):
    r, n = full.shape
    return jnp.transpose(full.reshape(r, N_DEV, n // N_DEV), (1, 0, 2))


def _pack_b_in(w):
    qkv = 3 * MIX_W
    pad = jnp.zeros((w.shape[0], B_IN_PAD - w.shape[1]), w.dtype)
    return jnp.concatenate([w[:, :qkv], w[:, qkv + N_MIX_HEADS:], w[:, qkv:qkv + N_MIX_HEADS], pad], axis=1)


def _unpack_b_in(w):
    qkv = 3 * MIX_W
    return jnp.concatenate([w[:, :qkv], w[:, qkv + MEM_W:qkv + MEM_W + N_MIX_HEADS], w[:, qkv:qkv + MEM_W]], axis=1)


def _ffn_forward(x, xb, wgu, get_rest, tag, fused=True):
    if fused:
        wd4, gain, bias = get_rest(x)
        y, yb, gu, a, xh, rstd = _ffn_fwd_main(x, xb, wgu, wd4, gain, bias, f"{tag}_fwd_main")
    else:
        gu, a = _ffn_up(xb, wgu, f"{tag}_up")
        wd4, gain, bias = get_rest(a)
        y, yb, xh, rstd = _mm_res_ln(a, wd4, x, gain, bias, 0.5, f"{tag}_down_ln")
    return y, yb, (xb, gu, a, xh, rstd), wd4


def _ffn_backward(dy, saved, wgu, wd4, gain, tag, after=(), send=None):
    xb, gu, a, xh, rstd = saved
    s = xb.shape[0]
    nd, c, d = wgu.shape
    dx, dzb, dh, dgain, dbias = _ffn_bwd_main(dy, xh, rstd, gain, wd4, wgu, gu, f"{tag}_bwd_main", after,
                                               with_dx=send is None)
    dh = dh.reshape(nd, s, c)
    dwd = _mm_tn(a, dzb[None], f"{tag}_dwd").reshape(nd, wd4.shape[1] // 2, d)
    if send is not None:
        send("down", dwd, dgain, dbias)
    dwgu = _mm_tn(dh, xb[None], f"{tag}_dwgu")
    if send is not None:
        sent = send("gate_up", dwgu)
        dx = _mm_nt(dh, wgu, f"{tag}_dx", res=dx, w_rows_out=False, after=sent)
    return dx, dwgu, dwd, dgain, dbias


def _mixer_a_forward(x, xb, memb, w_in, w_kv, w_out, gain, bias, tabs):
    hb = _proj_rope(xb, w_in, tabs, 2 * MIX_W // LANES, "a_in", True)
    groups = [_band_fwd(hb, g, f"a_band_fwd{g}") for g in range(N_GROUPS)]
    oa, lt = _band_combine([o for o, _ in groups], [l for _, l in groups], "a_combine")
    kv = _mm_nn(memb, w_kv, BF, "a_mem_kv")
    om, lm = _mem_fwd(hb, 3 * MIX_W // LANES, kv, "a_mem_fwd")
    cat = jnp.concatenate([oa, om], axis=1)
    y, yb, xh, rstd = _mm_res_ln(cat[None], w_out[None], x, gain, bias, 1.0, "a_out_ln")
    return y, yb, (xb, hb, oa, lt, kv, lm, cat, xh, rstd)


def _mixer_a_backward(dy, saved, memb, w_in, w_kv, w_out, gain, tabs_neg, after=()):
    xb, hb, oa, lt, kv, lm, cat, xh, rstd = saved
    dz, dzb, dgain, dbias = _ln_bwd(dy, xh, rstd, gain, 1.0, "a_ln_bwd", after)
    dcat = _mm_nt(dzb[None], w_out[None], "a_dcat", out_dtype=BF)
    dw_out = _mm_tn(cat[None], dzb[None], "a_dwout")[0]
    dqm, dkm, dvm = _mem_bwd(hb, 3 * MIX_W // LANES, kv, dcat, cat, GROUP_W // LANES, lm, "a_mem_bwd")
    dkv = jnp.concatenate([dkm, dvm], axis=1).astype(BF)
    dw_kv = _mm_tn(memb[None], dkv[None], "a_dwkv")[0]
    grads = [_band_bwd(hb, dcat, oa, lt, g, f"a_band_bwd{g}") for g in range(N_GROUPS)]
    dhb = _rope_cast([grads[g][i] for i in range(3) for g in range(N_GROUPS)] + [dqm], tabs_neg,
                     2 * MIX_W // LANES, "a_rope_bwd")
    dw_in = _mm_tn(dhb[None], xb[None], "a_dwin")[0]
    dx = _mm_nt(dhb[None], w_in[None], "a_dx", res=dz, w_rows_out=False)
    return dx, dw_in, dw_kv, dw_out, dgain, dbias


def _pad_rows(t, rows):
    return jnp.concatenate([t, jnp.zeros((rows - t.shape[0], t.shape[1]), t.dtype)], axis=0)


def _pad_cols(t, cols):
    return jnp.concatenate([t, jnp.zeros((t.shape[0], cols - t.shape[1]), t.dtype)], axis=1)


def _mixer_b_forward(x, xb, memb, w_in, fbias, w_kv, w_out, gain, bias, tabs):
    s = x.shape[0]
    hb, f = _proj_rope(xb, w_in, tabs, 0, "b_in", False, tail_block=(3 * MIX_W + MEM_W) // LANES)
    f_t = _pad_rows(jnp.transpose(f[:, :N_MIX_HEADS]), 16)
    bias16 = _pad_rows(jnp.transpose(fbias), 16)
    c_t = _gate_fwd(f_t, bias16, "b_gate_fwd")
    c_t3 = c_t[:N_MIX_HEADS].reshape(N_MIX_HEADS // 2, 2, s)
    c_rep = jnp.broadcast_to(c_t[:N_MIX_HEADS, :, None], (N_MIX_HEADS, s, LANES))
    ob, lb = _foxt_fwd(hb, c_rep, c_t3, "b_fox_fwd")
    kv = _mm_nn(memb, w_kv, BF, "b_mem_kv")
    om, lm = _mem_fwd(hb, 3 * MIX_W // LANES, kv, "b_mem_fwd")
    cat = jnp.concatenate([ob, om], axis=1)
    y, yb, xh, rstd = _mm_res_ln(cat[None], w_out[None], x, gain, bias, 1.0, "b_out_ln")
    return y, yb, (xb, hb, f_t, bias16, c_rep, c_t3, lb, kv, lm, cat, xh, rstd)


def _mixer_b_backward(dy, saved, memb, w_in, w_kv, w_out, gain, tabs, after=()):
    xb, hb, f_t, bias16, c_rep, c_t3, lb, kv, lm, cat, xh, rstd = saved
    s = xb.shape[0]
    dz, dzb, dgain, dbias = _ln_bwd(dy, xh, rstd, gain, 1.0, "b_ln_bwd", after)
    dcat = _mm_nt(dzb[None], w_out[None], "b_dcat", out_dtype=BF)
    dw_out = _mm_tn(cat[None], dzb[None], "b_dwout")[0]
    dqm, dkm, dvm = _mem_bwd(hb, 3 * MIX_W // LANES, kv, dcat, cat, MIX_W // LANES, lm, "b_mem_bwd")
    dkv = jnp.concatenate([dkm, dvm], axis=1).astype(BF)
    dw_kv = _mm_tn(memb[None], dkv[None], "b_dwkv")[0]
    dsum = _foxt_dsum(hb, dcat, lb, c_rep, c_t3, "b_fox_dsum")
    dq, dk, dv, dc3 = _foxt_bwd(hb, dcat, dsum, lb, c_rep, c_t3, "b_fox_bwd")
    df_t, dfb = _gate_bwd(_pad_rows(dc3.reshape(N_MIX_HEADS, s), 16), f_t, bias16, "b_gate_bwd")
    df = _pad_cols(jnp.transpose(df_t[:N_MIX_HEADS]), B_IN_PAD - 3 * MIX_W - MEM_W)
    dhb = _rope_cast([dq, dk, dv, dqm, df], tabs, 0, "b_cast_bwd", transposed=(0, 1, 2))
    dw_in = _mm_tn(xb[None], dhb[None], "b_dwin")[0]
    dx = _mm_nt(dhb[None], w_in[None], "b_dx", res=dz)
    return dx, dw_in, jnp.transpose(dfb[:N_MIX_HEADS]), dw_kv, dw_out, dgain, dbias


def _stored(t, name):
    return jnp.transpose(t, (0, 2, 1)) if name in ROWS_OUT else t


GATHER_GROUPS = (
    (("ffn1_w_gate_up", 0),),
    (("ffn1_w_down", 0), ("ln_gain", None), ("ln_bias", None)),
    (("a_w_in", 0), ("a_w_out", 0), ("mem_w_kv", 0)),
    (("ffn2_w_gate_up", 0), ("ffn2_w_down", 0)),
    (("ffn1_w_gate_up", 1), ("ffn1_w_down", 1)),
    (("b_w_in", 0), ("b_w_out", 0), ("mem_w_kv", 1)),
    (("ffn2_w_gate_up", 1), ("ffn2_w_down", 1)),
)


def _group_shards(group, params):
    return [t if n in F32_COMM else _stored(t, n)[l].astype(BF) for (n, l), t in zip(group, params)]


def _weight_groups(w):
    return [_group_shards(grp, [w[n] for n, _ in grp]) for grp in GATHER_GROUPS]


def _local_step(x, mem, target, fbias, get_w, put_g):
    s, d = x.shape
    tabs = _rope_tables(s, 1.0)
    tabs_neg = _rope_tables(s, -1.0)
    memb = mem.astype(BF)
    saved, wl = [], []
    cur, curb = x, x.astype(BF)
    ln = []

    def down4(t):
        return t.reshape(N_DEV // 2, -1, d)

    for i in range(DEPTH):
        if i == 0:
            def first_rest(a):
                g = get_w(1, a)
                ln.extend(jnp.transpose(t, (1, 2, 0, 3)).reshape(DEPTH, 3, 1, d) for t in g[1:3])
                return down4(g[0]), ln[0][0, 0], ln[1][0, 0]

            wgu = get_w(0, cur)[0]
            cur, curb, s1, wd = _ffn_forward(cur, curb, wgu, first_rest, "l0_ffn1", fused=False)
        else:
            g = get_w(3 * i + 1, cur)
            wgu = g[0]
            cur, curb, s1, wd = _ffn_forward(cur, curb, wgu, lambda a, g=g: (down4(g[1]), ln[0][i, 0], ln[1][i, 0]),
                                             f"l{i}_ffn1")
        w1 = (wgu, wd)
        ln_g, ln_b = ln
        g = get_w(3 * i + 2, cur)
        if i == 0:
            wm = (g[0].reshape(-1, d), g[2].reshape(d, -1), _cols_full(g[1]))
            cur, curb, s2 = _mixer_a_forward(cur, curb, memb, wm[0], wm[1], wm[2], ln_g[i, 1], ln_b[i, 1], tabs)
        else:
            wm = (_pack_b_in(g[0].reshape(d, -1)), g[2].reshape(d, -1), g[1].reshape(d, -1))
            cur, curb, s2 = _mixer_b_forward(cur, curb, memb, wm[0], fbias, wm[1], wm[2], ln_g[i, 1], ln_b[i, 1],
                                             tabs)
        g = get_w(3 * i + 3, cur)
        cur, curb, s3, wd = _ffn_forward(cur, curb, g[0], lambda a, g=g: (down4(g[1]), ln_g[i, 2], ln_b[i, 2]),
                                         f"l{i}_ffn2")
        w3 = (g[0], wd)
        saved.append((s1, s2, s3))
        wl.append((w1, wm, w3))

    dy, loss = _loss_head(cur, target, "loss_head")

    dgs = [[None] * 3 for _ in range(DEPTH)]
    dbs = [[None] * 3 for _ in range(DEPTH)]
    sent = ()
    for i in reversed(range(DEPTH)):
        s1, s2, s3 = saved[i]
        w1, wm, w3 = wl[i]
        dy, dgu, dd, dgs[i][2], dbs[i][2] = _ffn_backward(dy, s3, w3[0], w3[1], ln_g[i, 2], f"l{i}_ffn2", sent)
        sent = put_g(3 * i + 2, [dgu, dd])
        if i == 0:
            dy, dw_in, dw_kv, dw_out, dgs[i][1], dbs[i][1] = _mixer_a_backward(
                dy, s2, memb, wm[0], wm[1], wm[2], ln_g[i, 1], tabs_neg, sent)
            sent = put_g(1, [dw_in.reshape(N_DEV, -1, d), _cols_split(dw_out),
                             dw_kv.reshape(N_DEV, d // N_DEV, -1)])
        else:
            dy, dw_in, dfb, dw_kv, dw_out, dgs[i][1], dbs[i][1] = _mixer_b_backward(
                dy, s2, memb, wm[0], wm[1], wm[2], ln_g[i, 1], tabs, sent)
            sent = put_g(4, [_unpack_b_in(dw_in).reshape(N_DEV, d // N_DEV, -1),
                             dw_out.reshape(N_DEV, d // N_DEV, -1), dw_kv.reshape(N_DEV, d // N_DEV, -1),
                             jnp.broadcast_to(dfb[None], (N_DEV,) + dfb.shape)])
        if i == 0:
            def send_last(kind, dw, dgain=None, dbias=None):
                if kind == "gate_up":
                    return put_g(6, [dw])
                dgs[0][0], dbs[0][0] = dgain, dbias
                ln_pieces = []
                for parts in (dgs, dbs):
                    t = jnp.concatenate([parts[a][b] for a in range(DEPTH) for b in range(3)], axis=0)
                    ln_pieces.append(jnp.transpose(t.reshape(DEPTH * 3, N_DEV, d // N_DEV), (1, 0, 2)))
                return put_g(0, [dw] + ln_pieces)

            dy = _ffn_backward(dy, s1, w1[0], w1[1], ln_g[i, 0], "l0_ffn1", sent, send_last)[0]
        else:
            dy, dgu, dd, dgs[i][0], dbs[i][0] = _ffn_backward(dy, s1, w1[0], w1[1], ln_g[i, 0], f"l{i}_ffn1", sent)
            sent = put_g(3, [dgu, dd])
    return loss, dy


WEIGHTS = ("ffn1_w_gate_up", "ffn1_w_down", "ffn2_w_gate_up", "ffn2_w_down", "ln_gain", "ln_bias", "mem_w_kv",
           "a_w_in", "a_w_out", "b_w_in", "b_forget_bias", "b_w_out")
F32_COMM = ("ln_gain", "ln_bias", "b_forget_bias")
ROWS_OUT = ("ffn1_w_gate_up", "ffn2_w_gate_up", "a_w_in")
GRAD_SLOTS = {
    "ffn1_w_gate_up": [(6, 0), (3, 0)], "ffn1_w_down": [(0, 0), (3, 1)],
    "ffn2_w_gate_up": [(2, 0), (5, 0)], "ffn2_w_down": [(2, 1), (5, 1)],
    "ln_gain": [(0, 1)], "ln_bias": [(0, 2)], "mem_w_kv": [(1, 2), (4, 2)],
    "a_w_in": [(1, 0)], "a_w_out": [(1, 1)], "b_w_in": [(4, 0)], "b_forget_bias": [(4, 3)], "b_w_out": [(4, 1)],
}


def kernel(x, mem, ffn1_w_gate_up, ffn1_w_down, ffn2_w_gate_up, ffn2_w_down, ln_gain, ln_bias, mem_w_kv, a_w_in, a_w_out, b_w_in, b_forget_bias, b_w_out, loss_target, m_ffn1_w_gate_up, m_ffn1_w_down, m_ffn2_w_gate_up, m_ffn2_w_down, m_ln_gain, m_ln_bias, m_mem_w_kv, m_a_w_in, m_a_w_out, m_b_w_in, m_b_forget_bias, m_b_w_out, v_ffn1_w_gate_up, v_ffn1_w_down, v_ffn2_w_gate_up, v_ffn2_w_down, v_ln_gain, v_ln_bias, v_mem_w_kv, v_a_w_in, v_a_w_out, v_b_w_in, v_b_forget_bias, v_b_w_out):
    w = dict(zip(WEIGHTS, (ffn1_w_gate_up, ffn1_w_down, ffn2_w_gate_up, ffn2_w_down, ln_gain, ln_bias, mem_w_kv,
                           a_w_in, a_w_out, b_w_in, b_forget_bias, b_w_out)))
    m = dict(zip(WEIGHTS, (m_ffn1_w_gate_up, m_ffn1_w_down, m_ffn2_w_gate_up, m_ffn2_w_down, m_ln_gain, m_ln_bias,
                           m_mem_w_kv, m_a_w_in, m_a_w_out, m_b_w_in, m_b_forget_bias, m_b_w_out)))
    v = dict(zip(WEIGHTS, (v_ffn1_w_gate_up, v_ffn1_w_down, v_ffn2_w_gate_up, v_ffn2_w_down, v_ln_gain, v_ln_bias,
                           v_mem_w_kv, v_a_w_in, v_a_w_out, v_b_w_in, v_b_forget_bias, v_b_w_out)))

    gathers = []
    for k, grp in enumerate(GATHER_GROUPS):
        params, behind = [w[n] for n, _ in grp], [_started(h) for h in gathers[-1:]]
        if behind:
            params, behind = lax.optimization_barrier((params, behind))
        gathers.append(_xfer_start(_group_shards(grp, params), True, f"gather{k}_start", behind))
    exchanges = {}

    def get_w(k, after):
        behind = [after] + ([_started(h) for h in gathers] if k == 0 else [])
        return _xfer_wait(gathers[k], behind, True, f"gather{k}_wait")

    def put_g(k, pieces):
        behind = [_started(exchanges[0])] if k == 6 else []
        exchanges[k] = _xfer_start(pieces, False, f"grads{k}_start", behind)
        return (_started(exchanges[k]),)

    loss, grad_x = _local_step(x[0], mem[0], loss_target[0], b_forget_bias, get_w, put_g)
    loss = lax.psum(loss[0, 0], ("x", "y", "c"))

    outs, landed = {}, {}

    def adamw(names):
        for n in names:
            contribs = [landed[g][j] for g, j in GRAD_SLOTS[n]]
            view = (len(contribs),) + contribs[0].shape[1:]
            shape = _stored(w[n], n).shape
            res = _reduce_adamw(contribs, *[_stored(t[n], n).reshape(view) for t in (w, m, v)], f"adamw_{n}")
            outs[n] = [_stored(t.reshape(shape), n) for t in res]
        return [outs[n][3] for n in names]

    after = [grad_x]
    for k in (5, 4, 3, 2, 1):
        landed[k] = _xfer_wait(exchanges[k], after, False, f"grads{k}_wait")
        after = [landed[k][0]]
    done = adamw(("ffn2_w_gate_up", "ffn2_w_down", "mem_w_kv", "a_w_in", "a_w_out", "b_w_in", "b_forget_bias",
                  "b_w_out"))
    landed[0] = _xfer_wait(exchanges[0], done, False, "grads0_wait")
    done = adamw(("ffn1_w_down", "ln_gain", "ln_bias"))
    landed[6] = _xfer_wait(exchanges[6], done, False, "grads6_wait")
    adamw(("ffn1_w_gate_up",))
    return (loss, grad_x[None], *[outs[n][0] for n in WEIGHTS], *[outs[n][1] for n in WEIGHTS],
            *[outs[n][2] for n in WEIGHTS], *[outs[n][3] for n in WEIGHTS])
```

```python
import functools

import jax
import jax.numpy as jnp
from jax import lax
from jax.experimental import pallas as pl
from jax.experimental.pallas import tpu as pltpu

F32 = jnp.float32
BF = jnp.bfloat16
MESH_ID = pl.DeviceIdType.MESH

N_DEV = 8
DEPTH = 2
HEAD_DIM = 64
LANES = 128
N_MIX_HEADS = 12
N_MEM_HEADS = 4
MIX_W = N_MIX_HEADS * HEAD_DIM
MEM_W = N_MEM_HEADS * HEAD_DIM
N_GROUPS = 3
GROUP_W = MIX_W // N_GROUPS
BLOCK = 128
BAND_SUB = 4
BAND_COLS = 4
ROT_HALF = 8
ROPE_THETA = 500000.0
ALPHA = (2 * DEPTH) ** 0.25
LN_EPS = 1e-5
SCALE = HEAD_DIM ** -0.5
NEG = -1e30
B_IN_PAD = 2688
ADAM_LR, ADAM_B1, ADAM_B2, ADAM_EPS, ADAM_WD, ADAM_STEP = 0.001, 0.9, 0.999, 1e-08, 0.01, 10
VMEM_LIMIT = 56 * 1024 * 1024


def _cp(*sem):
    return pltpu.CompilerParams(dimension_semantics=sem, vmem_limit_bytes=VMEM_LIMIT)


def _dot(a, b):
    return jnp.dot(a, b, preferred_element_type=F32)


def _dot_nt(a, b):
    return lax.dot_general(a, b, (((1,), (1,)), ((), ())), preferred_element_type=F32)


def _dot_tn(a, b):
    return lax.dot_general(a, b, (((0,), (0,)), ((), ())), preferred_element_type=F32)


def _sigmoid(x):
    return 1.0 / (1.0 + jnp.exp(-x))


def _tile(n, cap=1024):
    if n <= cap:
        return n
    best = LANES
    for t in range(LANES, cap + 1, LANES):
        if n % t == 0:
            best = t
    return best


def _rows(s, cap=512):
    return s if s <= cap else cap


def _mm_nn(a, b, out_dtype, name, b_rows_out=False):
    m, k = a.shape
    n = b.shape[0] if b_rows_out else b.shape[1]
    tm, tn = _rows(m), _tile(n)

    def body(a_ref, b_ref, o_ref):
        prod = _dot_nt(a_ref[...], b_ref[...]) if b_rows_out else _dot(a_ref[...], b_ref[...])
        o_ref[...] = prod.astype(o_ref.dtype)

    b_spec = (pl.BlockSpec((tn, k), lambda j, i: (j, 0)) if b_rows_out
              else pl.BlockSpec((k, tn), lambda j, i: (0, j)))
    return pl.pallas_call(
        body, name=name, grid=(n // tn, m // tm),
        in_specs=[pl.BlockSpec((tm, k), lambda j, i: (i, 0)), b_spec],
        out_specs=pl.BlockSpec((tm, tn), lambda j, i: (i, j)),
        out_shape=jax.ShapeDtypeStruct((m, n), out_dtype),
        compiler_params=_cp("parallel", "parallel"))(a, b)


def _resident(shape, index_map):
    return pl.BlockSpec(shape, index_map, pipeline_mode=pl.Buffered(1))


def _mm_tn(a, b, name, out_dtype=BF):
    na, s, m = a.shape
    nb, _, n = b.shape
    no = max(na, nb)
    tm, tn = _tile(m), _tile(n)

    def body(a_ref, b_ref, o_ref):
        o_ref[...] = _dot_tn(a_ref[...], b_ref[...]).astype(o_ref.dtype)

    def spec(nbatch, width, tile, index_map):
        fixed = nbatch == 1 and width == tile
        return _resident((None, s, tile), index_map) if fixed else pl.BlockSpec((None, s, tile), index_map)

    return pl.pallas_call(
        body, name=name, grid=(no, m // tm, n // tn),
        in_specs=[spec(na, m, tm, lambda j, r, c: (j if na > 1 else 0, 0, r)),
                  spec(nb, n, tn, lambda j, r, c: (j if nb > 1 else 0, 0, c))],
        out_specs=pl.BlockSpec((None, tm, tn), lambda j, r, c: (j, r, c)),
        out_shape=jax.ShapeDtypeStruct((no, m, n), out_dtype),
        compiler_params=_cp("parallel", "parallel", "parallel"))(a, b)


def _mm_nt(dh, w, name, res=None, out_dtype=F32, w_rows_out=True, after=()):
    nc, s, kc = dh.shape
    d = w.shape[1] if w_rows_out else w.shape[2]
    ts = _rows(s)
    has_res = res is not None
    mm = _dot_nt if w_rows_out else _dot

    def body(*refs):
        o_ref = refs[-1]
        dh_ref, w_ref = refs[:2]
        if has_res:
            r_ref = refs[2]
        out = mm(dh_ref[0], w_ref[0])
        for j in range(1, nc):
            out = out + mm(dh_ref[j], w_ref[j])
        if has_res:
            out = out + ALPHA * r_ref[...]
        o_ref[...] = out.astype(o_ref.dtype)

    in_specs = [pl.BlockSpec((nc, ts, kc), lambda i: (0, i, 0)), _resident(w.shape, lambda i: (0, 0, 0))]
    args = [dh, w]
    if has_res:
        in_specs.append(pl.BlockSpec((ts, d), lambda i: (i, 0)))
        args.append(res)
    in_specs += [pl.BlockSpec(memory_space=pl.ANY)] * len(after)
    args += list(after)
    return pl.pallas_call(
        body, name=name, grid=(s // ts,), in_specs=in_specs,
        out_specs=pl.BlockSpec((ts, d), lambda i: (i, 0)),
        out_shape=jax.ShapeDtypeStruct((s, d), out_dtype),
        compiler_params=_cp("parallel"))(*args)


def _mm_res_ln(a, w, x, gain, bias, fscale, name):
    nc, s, kc = a.shape
    d = w.shape[2]
    ts = _rows(s)

    def body(a_ref, w_ref, x_ref, g_ref, b_ref, y_ref, yb_ref, xh_ref, r_ref):
        f = _dot(a_ref[0], w_ref[0])
        for j in range(1, nc):
            f = f + _dot(a_ref[j], w_ref[j])
        z = ALPHA * x_ref[...] + fscale * f
        mu = jnp.mean(z, axis=-1, keepdims=True)
        zc = z - mu
        var = jnp.mean(zc * zc, axis=-1, keepdims=True)
        r = lax.rsqrt(var + LN_EPS)
        xh = zc * r
        y = xh * g_ref[...] + b_ref[...]
        y_ref[...] = y
        yb_ref[...] = y.astype(BF)
        xh_ref[...] = xh
        r_ref[...] = r

    row = pl.BlockSpec((ts, d), lambda i: (i, 0))
    vec = pl.BlockSpec((1, d), lambda i: (0, 0))
    return pl.pallas_call(
        body, name=name, grid=(s // ts,),
        in_specs=[pl.BlockSpec((nc, ts, kc), lambda i: (0, i, 0)), _resident((nc, kc, d), lambda i: (0, 0, 0)),
                  row, vec, vec],
        out_specs=[row, row, row, pl.BlockSpec((ts, 1), lambda i: (i, 0))],
        out_shape=[jax.ShapeDtypeStruct((s, d), F32), jax.ShapeDtypeStruct((s, d), BF),
                   jax.ShapeDtypeStruct((s, d), F32), jax.ShapeDtypeStruct((s, 1), F32)],
        compiler_params=_cp("parallel"))(a, w, x, gain, bias)


def _ln_bwd(dy, xh, rstd, gain, fscale, name, after=()):
    s, d = dy.shape
    ts = _rows(s)
    na = len(after)

    def body(*refs):
        dy_ref, xh_ref, r_ref, g_ref = refs[:4]
        dz_ref, dzb_ref, dg_ref, db_ref = refs[4 + na:]
        i = pl.program_id(0)
        dyv = dy_ref[...]
        xhv = xh_ref[...]
        dxh = dyv * g_ref[...]
        m1 = jnp.mean(dxh, axis=-1, keepdims=True)
        m2 = jnp.mean(dxh * xhv, axis=-1, keepdims=True)
        dz = r_ref[...] * (dxh - m1 - xhv * m2)
        dz_ref[...] = dz
        dzb_ref[...] = (fscale * dz).astype(BF)

        @pl.when(i == 0)
        def _():
            dg_ref[...] = jnp.zeros_like(dg_ref)
            db_ref[...] = jnp.zeros_like(db_ref)

        dg_ref[...] += jnp.sum(dyv * xhv, axis=0, keepdims=True)
        db_ref[...] += jnp.sum(dyv, axis=0, keepdims=True)

    row = pl.BlockSpec((ts, d), lambda i: (i, 0))
    vec = pl.BlockSpec((1, d), lambda i: (0, 0))
    return pl.pallas_call(
        body, name=name, grid=(s // ts,),
        in_specs=[row, row, pl.BlockSpec((ts, 1), lambda i: (i, 0)), vec] + [pl.BlockSpec(memory_space=pl.ANY)] * na,
        out_specs=[row, row, vec, vec],
        out_shape=[jax.ShapeDtypeStruct((s, d), F32), jax.ShapeDtypeStruct((s, d), BF),
                   jax.ShapeDtypeStruct((1, d), F32), jax.ShapeDtypeStruct((1, d), F32)],
        compiler_params=_cp("arbitrary"))(dy, xh, rstd, gain, *after)


def _ffn_up(xb, wgu, name):
    s, d = xb.shape
    c = wgu.shape[1]
    nch = wgu.shape[0] // 2
    ts = _rows(s, 1024)
    w4 = wgu.reshape(2, nch, c, d)

    def body(x_ref, w_ref, gu_ref, a_ref):
        x = x_ref[...]
        g = _dot_nt(x, w_ref[0])
        u = _dot_nt(x, w_ref[1])
        sg = _sigmoid(g)
        t = g * sg
        gu_ref[0] = (u * (sg * (1.0 + g - t))).astype(BF)
        gu_ref[1] = t.astype(BF)
        a_ref[...] = (t * u).astype(BF)

    return pl.pallas_call(
        body, name=name, grid=(nch, s // ts),
        in_specs=[pl.BlockSpec((ts, d), lambda j, i: (i, 0)),
                  pl.BlockSpec((2, None, c, d), lambda j, i: (0, j, 0, 0))],
        out_specs=[pl.BlockSpec((2, None, ts, c), lambda j, i: (0, j, i, 0)),
                   pl.BlockSpec((None, ts, c), lambda j, i: (j, i, 0))],
        out_shape=[jax.ShapeDtypeStruct((2, nch, s, c), BF), jax.ShapeDtypeStruct((nch, s, c), BF)],
        compiler_params=_cp("parallel", "parallel"))(xb, w4)


def _ffn_fwd_main(x, xb, wgu, wd4, gain, bias, name):
    s, d = x.shape
    nch, c = wd4.shape[0], wd4.shape[1]
    ts = _rows(s, 256)

    def body(x_ref, xb_ref, wgu_ref, wd_ref, g_ref, b_ref, y_ref, yb_ref, gu_ref, a_ref, xh_ref, r_ref):
        xbv = xb_ref[...]
        f = jnp.zeros((ts, d), F32)
        for j in range(nch):
            g = _dot_nt(xbv, wgu_ref[j])
            u = _dot_nt(xbv, wgu_ref[nch + j])
            sg = _sigmoid(g)
            t = g * sg
            gu_ref[0, j] = (u * (sg * (1.0 + g - t))).astype(BF)
            gu_ref[1, j] = t.astype(BF)
            act = (t * u).astype(BF)
            a_ref[j] = act
            f = f + _dot(act, wd_ref[j])
        z = ALPHA * x_ref[...] + 0.5 * f
        mu = jnp.mean(z, axis=-1, keepdims=True)
        zc = z - mu
        var = jnp.mean(zc * zc, axis=-1, keepdims=True)
        r = lax.rsqrt(var + LN_EPS)
        xh = zc * r
        y = xh * g_ref[...] + b_ref[...]
        y_ref[...] = y
        yb_ref[...] = y.astype(BF)
        xh_ref[...] = xh
        r_ref[...] = r

    row = pl.BlockSpec((ts, d), lambda i: (i, 0))
    vec = pl.BlockSpec((1, d), lambda i: (0, 0))
    return pl.pallas_call(
        body, name=name, grid=(s // ts,),
        in_specs=[row, row, _resident(wgu.shape, lambda i: (0, 0, 0)), _resident(wd4.shape, lambda i: (0, 0, 0)),
                  vec, vec],
        out_specs=[row, row, pl.BlockSpec((2, nch, ts, c), lambda i: (0, 0, i, 0)),
                   pl.BlockSpec((nch, ts, c), lambda i: (0, i, 0)), row, pl.BlockSpec((ts, 1), lambda i: (i, 0))],
        out_shape=[jax.ShapeDtypeStruct((s, d), F32), jax.ShapeDtypeStruct((s, d), BF),
                   jax.ShapeDtypeStruct((2, nch, s, c), BF), jax.ShapeDtypeStruct((nch, s, c), BF),
                   jax.ShapeDtypeStruct((s, d), F32), jax.ShapeDtypeStruct((s, 1), F32)],
        compiler_params=_cp("parallel"))(x, xb, wgu, wd4, gain, bias)


def _ffn_bwd_main(dy, xh, rstd, gain, wd4, wgu, gu, name, after=(), with_dx=True):
    s, d = dy.shape
    nch, c = wd4.shape[0], wd4.shape[1]
    ts = _rows(s, 256)
    na = len(after)

    def body(*refs):
        dy_ref, xh_ref, r_ref, g_ref, wd_ref, wgu_ref, gu_ref = refs[:7]
        dx_ref, dzb_ref, dh_ref, dg_ref, db_ref = refs[7 + na:]
        i = pl.program_id(0)
        dyv = dy_ref[...]
        xhv = xh_ref[...]
        dxh = dyv * g_ref[...]
        m1 = jnp.mean(dxh, axis=-1, keepdims=True)
        m2 = jnp.mean(dxh * xhv, axis=-1, keepdims=True)
        dz = r_ref[...] * (dxh - m1 - xhv * m2)
        dzb = (0.5 * dz).astype(BF)
        dzb_ref[...] = dzb

        @pl.when(i == 0)
        def _():
            dg_ref[...] = jnp.zeros_like(dg_ref)
            db_ref[...] = jnp.zeros_like(db_ref)

        dg_ref[...] += jnp.sum(dyv * xhv, axis=0, keepdims=True)
        db_ref[...] += jnp.sum(dyv, axis=0, keepdims=True)

        dx = ALPHA * dz if with_dx else dz
        for j in range(nch):
            da = _dot_nt(dzb, wd_ref[j])
            dgate = (da * gu_ref[0, j].astype(F32)).astype(BF)
            dup = (da * gu_ref[1, j].astype(F32)).astype(BF)
            dh_ref[0, j] = dgate
            dh_ref[1, j] = dup
            if with_dx:
                dx = dx + _dot(dgate, wgu_ref[j]) + _dot(dup, wgu_ref[nch + j])
        dx_ref[...] = dx

    row = pl.BlockSpec((ts, d), lambda i: (i, 0))
    vec = pl.BlockSpec((1, d), lambda i: (0, 0))
    act = pl.BlockSpec((2, nch, ts, c), lambda i: (0, 0, i, 0))
    return pl.pallas_call(
        body, name=name, grid=(s // ts,),
        in_specs=[row, row, pl.BlockSpec((ts, 1), lambda i: (i, 0)), vec,
                  _resident(wd4.shape, lambda i: (0, 0, 0)), _resident(wgu.shape, lambda i: (0, 0, 0)), act]
                 + [pl.BlockSpec(memory_space=pl.ANY)] * na,
        out_specs=[row, row, act, vec, vec],
        out_shape=[jax.ShapeDtypeStruct((s, d), F32), jax.ShapeDtypeStruct((s, d), BF),
                   jax.ShapeDtypeStruct((2, nch, s, c), BF),
                   jax.ShapeDtypeStruct((1, d), F32), jax.ShapeDtypeStruct((1, d), F32)],
        compiler_params=_cp("arbitrary"))(dy, xh, rstd, gain, wd4, wgu, gu, *after)


def _rope_tables(s, sign):
    pos = jnp.arange(s, dtype=F32)
    inv_freq = 1.0 / (ROPE_THETA ** (jnp.arange(ROT_HALF, dtype=F32) / ROT_HALF))
    ang = pos[:, None] * inv_freq[None, :]
    cos, sin = jnp.cos(ang), jnp.sin(ang) * sign
    one = jnp.ones((s, HEAD_DIM - 2 * ROT_HALF), F32)
    zero = jnp.zeros((s, HEAD_DIM - 2 * ROT_HALF), F32)
    zh = jnp.zeros((s, ROT_HALF), F32)
    cos_f = jnp.concatenate([cos, cos, one], axis=1)
    sin_a = jnp.concatenate([-sin, zh, zero], axis=1)
    sin_b = jnp.concatenate([zh, sin, zero], axis=1)
    rep = LANES // HEAD_DIM
    return tuple(jnp.tile(t, (1, rep)) for t in (cos_f, sin_a, sin_b))


def _rope(t, c_ref, sa_ref, sb_ref):
    return (t * c_ref[...] + pltpu.roll(t, LANES - ROT_HALF, 1) * sa_ref[...]
            + pltpu.roll(t, ROT_HALF, 1) * sb_ref[...])


def _proj_rope(xb, w, tabs, n_rope, name, w_rows_out, tail_block=None):
    s, d = xb.shape
    n = w.shape[0] if w_rows_out else w.shape[1]
    tm = _rows(s, 256)
    has_tail = tail_block is not None

    def body(x_ref, w_ref, c_ref, sa_ref, sb_ref, o_ref, *tail_ref):
        h = (_dot_nt if w_rows_out else _dot)(x_ref[...], w_ref[...])
        for cb in range(n // LANES):
            t = h[:, cb * LANES:(cb + 1) * LANES]
            if cb < n_rope:
                t = _rope(t, c_ref, sa_ref, sb_ref)
            o_ref[:, cb * LANES:(cb + 1) * LANES] = t.astype(BF)
        if has_tail:
            tail_ref[0][...] = h[:, tail_block * LANES:(tail_block + 1) * LANES]

    tab = pl.BlockSpec((tm, LANES), lambda i: (i, 0))
    out_specs = [pl.BlockSpec((tm, n), lambda i: (i, 0))]
    out_shape = [jax.ShapeDtypeStruct((s, n), BF)]
    if has_tail:
        out_specs.append(tab)
        out_shape.append(jax.ShapeDtypeStruct((s, LANES), F32))
    res = pl.pallas_call(
        body, name=name, grid=(s // tm,),
        in_specs=[pl.BlockSpec((tm, d), lambda i: (i, 0)), _resident(w.shape, lambda i: (0, 0)), tab, tab, tab],
        out_specs=out_specs, out_shape=out_shape, compiler_params=_cp("parallel"))(xb, w, *tabs)
    return res if has_tail else res[0]


def _rope_cast(parts, tabs, n_rope, name, transposed=()):
    s = tabs[0].shape[0]
    flip = [i in transposed for i in range(len(parts))]
    widths = [p.shape[0] if f else p.shape[1] for p, f in zip(parts, flip)]
    n = sum(widths)
    npart = len(parts)
    ts = _rows(s, 256)

    def body(*refs):
        part_refs = refs[:npart]
        c_ref, sa_ref, sb_ref, o_ref = refs[npart:]
        col = 0
        for ref, w, f in zip(part_refs, widths, flip):
            for j in range(w // LANES):
                if f:
                    t = jnp.transpose(ref[j * LANES:(j + 1) * LANES, :])
                else:
                    t = ref[:, j * LANES:(j + 1) * LANES]
                if col < n_rope:
                    t = _rope(t, c_ref, sa_ref, sb_ref)
                o_ref[:, col * LANES:(col + 1) * LANES] = t.astype(BF)
                col += 1

    tab = pl.BlockSpec((ts, LANES), lambda i: (i, 0))
    return pl.pallas_call(
        body, name=name, grid=(s // ts,),
        in_specs=[pl.BlockSpec((w, ts), lambda i: (0, i)) if f else pl.BlockSpec((ts, w), lambda i: (i, 0))
                  for w, f in zip(widths, flip)] + [tab, tab, tab],
        out_specs=pl.BlockSpec((ts, n), lambda i: (i, 0)),
        out_shape=jax.ShapeDtypeStruct((s, n), BF),
        compiler_params=_cp("parallel"))(*parts, *tabs)


def _head_masks():
    lane = lax.broadcasted_iota(jnp.int32, (1, LANES), 1)
    return [lane < HEAD_DIM, lane >= HEAD_DIM]


def _sel(mask, v):
    return jnp.where(mask, v, jnp.zeros_like(v))


def _pick(mask, wide, fill):
    return jnp.max(jnp.where(mask, wide, fill), axis=1, keepdims=True)


def _head_stack(hm, a, b):
    return jnp.concatenate([_sel(hm[0], a), _sel(hm[0], b), _sel(hm[1], a), _sel(hm[1], b)], axis=0)


def _band_mask_stack(has_prev):
    qi = lax.broadcasted_iota(jnp.int32, (BLOCK, 4 * BLOCK), 0)
    col = lax.broadcasted_iota(jnp.int32, (BLOCK, 4 * BLOCK), 1)
    d = jnp.bitwise_and(col, BLOCK - 1) - qi
    is_prev = jnp.bitwise_and(col, BLOCK) != 0
    return jnp.where(is_prev, d - jnp.where(has_prev, 0, BLOCK), -d) >= 0


class _BandView:
    def __init__(self, s, g):
        self.r = 4 ** g
        self.nl = s // self.r
        self.nblk = self.nl // BLOCK
        self.nsub = min(BAND_SUB, self.nblk)
        self.tile = self.nsub * BLOCK
        self.ncols = min(self.r * GROUP_W // LANES, BAND_COLS)
        self.grid = (self.r * GROUP_W // LANES // self.ncols, self.nblk // self.nsub)

    def view(self, a):
        return a.reshape(self.nl, self.r * a.shape[1])

    def qkv(self, hb, g):
        npair = MIX_W // LANES
        offs = [i * npair + g * GROUP_W // LANES for i in range(3)]
        if self.r == 1:
            return [hb] * 3, hb.shape[1], offs
        return [self.view(hb[:, o * LANES:o * LANES + GROUP_W]) for o in offs], GROUP_W, [0, 0, 0]

    def specs(self, width, off):
        assert off % self.ncols == 0 and (width == GROUP_W or self.r == 1)
        nsub, last, lanes, first = self.nsub, self.nblk - 1, self.ncols * LANES, off // self.ncols
        return (pl.BlockSpec((self.tile, lanes), lambda cg, t: (t, first + cg)),
                pl.BlockSpec((BLOCK, lanes), lambda cg, t: (jnp.maximum(t * nsub - 1, 0), first + cg)),
                pl.BlockSpec((BLOCK, lanes), lambda cg, t: (jnp.minimum(t * nsub + nsub, last), first + cg)))


def _band_fwd(hb, g, name):
    s, n = hb.shape
    bv = _BandView(s, g)
    nsub = bv.nsub
    npair = MIX_W // LANES

    def body(q_ref, kc_ref, kp_ref, vc_ref, vp_ref, o_ref, l_ref):
        t = pl.program_id(1)
        hm = _head_masks()
        for i, c in [(i, c) for i in range(nsub) for c in range(bv.ncols)]:
            rows = slice(i * BLOCK, (i + 1) * BLOCK)
            lanes = slice(c * LANES, (c + 1) * LANES)
            has_prev = t > 0 if i == 0 else True
            q, kc, vc = q_ref[rows, lanes], kc_ref[rows, lanes], vc_ref[rows, lanes]
            if i == 0:
                kp, vp = kp_ref[:, lanes], vp_ref[:, lanes]
            else:
                prev = slice((i - 1) * BLOCK, i * BLOCK)
                kp, vp = kc_ref[prev, lanes], vc_ref[prev, lanes]
            sc = jnp.where(_band_mask_stack(has_prev), _dot_nt(q, _head_stack(hm, kc, kp)) * SCALE, NEG)
            ps, ms, ls = [], [], []
            for h in range(2):
                sh = sc[:, 2 * h * BLOCK:2 * (h + 1) * BLOCK]
                m = jnp.max(sh, axis=1, keepdims=True)
                p = jnp.exp(sh - m)
                ps.append(p.astype(BF))
                ms.append(m)
                ls.append(jnp.sum(p, axis=1, keepdims=True))
            o = _dot(jnp.concatenate(ps, axis=1), _head_stack(hm, vc, vp))
            o_ref[rows, lanes] = o / jnp.where(hm[0], ls[0], ls[1])
            l_ref[rows, lanes] = jnp.where(hm[0], ms[0] + jnp.log(ls[0]), ms[1] + jnp.log(ls[1]))

    (qv, kv_, vv), width, (qo, ko, vo) = bv.qkv(hb, g)
    q_cur, _, _ = bv.specs(width, qo)
    k_cur, k_prv, _ = bv.specs(width, ko)
    v_cur, v_prv, _ = bv.specs(width, vo)
    out_spec = bv.specs(GROUP_W, 0)[0]
    out = jax.ShapeDtypeStruct((bv.nl, bv.r * GROUP_W), F32)
    o, l = pl.pallas_call(
        body, name=name, grid=bv.grid,
        in_specs=[q_cur, k_cur, k_prv, v_cur, v_prv], out_specs=[out_spec, out_spec], out_shape=[out, out],
        compiler_params=_cp("parallel", "parallel"))(qv, kv_, kv_, vv, vv)
    return o.reshape(s, GROUP_W), l.reshape(s, GROUP_W)


def _band_combine(os, ls, name):
    ng = len(os)
    s, w = os[0].shape
    ts = _rows(s)

    def body(*refs):
        o_refs, l_refs = refs[:ng], refs[ng:2 * ng]
        oa_ref, lt_ref = refs[2 * ng:]
        lv = [r[...] for r in l_refs]
        m = functools.reduce(jnp.maximum, lv)
        es = [jnp.exp(l - m) for l in lv]
        den = functools.reduce(lambda a, b: a + b, es)
        num = functools.reduce(lambda a, b: a + b, [es[g] * o_refs[g][...] for g in range(ng)])
        oa_ref[...] = (num / den).astype(BF)
        lt_ref[...] = m + jnp.log(den)

    blk = pl.BlockSpec((ts, w), lambda i: (i, 0))
    return pl.pallas_call(
        body, name=name, grid=(s // ts,), in_specs=[blk] * (2 * ng), out_specs=[blk, blk],
        out_shape=[jax.ShapeDtypeStruct((s, w), BF), jax.ShapeDtypeStruct((s, w), F32)],
        compiler_params=_cp("parallel"))(*os, *ls)


def _band_bwd(hb, dcat, oa, lt, g, name):
    s, n = hb.shape
    bv = _BandView(s, g)
    nsub = bv.nsub
    npair = MIX_W // LANES
    ntile = bv.grid[1]

    def body(q_ref, qn_ref, kc_ref, kp_ref, vc_ref, vp_ref, do_ref, don_ref, oa_ref, oan_ref, lt_ref, ltn_ref,
             dq_ref, dk_ref, dv_ref):
        t = pl.program_id(1)
        hm = _head_masks()

        for i, c in [(i, c) for i in range(nsub) for c in range(bv.ncols)]:
            lanes = slice(c * LANES, (c + 1) * LANES)

            def block(ref, edge_ref, i, lanes=lanes):
                if i < 0 or i >= nsub:
                    return edge_ref[:, lanes]
                return ref[i * BLOCK:(i + 1) * BLOCK, lanes]

            has_prev = t > 0 if i == 0 else True
            has_next = t < ntile - 1 if i == nsub - 1 else True
            q, qn = block(q_ref, None, i), block(q_ref, qn_ref, i + 1)
            kc, kp = block(kc_ref, None, i), block(kc_ref, kp_ref, i - 1)
            vc, vp = block(vc_ref, None, i), block(vc_ref, vp_ref, i - 1)
            do, don = block(do_ref, None, i), block(do_ref, don_ref, i + 1)
            dd = do.astype(F32) * block(oa_ref, None, i).astype(F32)
            ddn = don.astype(F32) * block(oa_ref, oan_ref, i + 1).astype(F32)
            lt, ltn = block(lt_ref, None, i), block(lt_ref, ltn_ref, i + 1)

            def per_head(wide, width):
                col = lax.broadcasted_iota(jnp.int32, (BLOCK, 2 * width), 1)
                return jnp.where(col < width, _pick(hm[0], wide, NEG), _pick(hm[1], wide, NEG))

            def row_sums(prod, width):
                col = lax.broadcasted_iota(jnp.int32, (BLOCK, 2 * width), 1)
                return jnp.where(col < width, jnp.sum(_sel(hm[0], prod), axis=1, keepdims=True),
                                 jnp.sum(_sel(hm[1], prod), axis=1, keepdims=True))

            kst, vst = _head_stack(hm, kc, kp), _head_stack(hm, vc, vp)
            p = jnp.exp(jnp.where(_band_mask_stack(has_prev), _dot_nt(q, kst) * SCALE, NEG)
                        - per_head(lt, 2 * BLOCK))
            ds = p * (_dot_nt(do, vst) - row_sums(dd, 2 * BLOCK))
            dq_ref[i * BLOCK:(i + 1) * BLOCK, lanes] = SCALE * _dot(ds.astype(BF), kst)
            kcs = jnp.concatenate([_sel(hm[0], kc), _sel(hm[1], kc)], axis=0)
            vcs = jnp.concatenate([_sel(hm[0], vc), _sel(hm[1], vc)], axis=0)
            qi_ = lax.broadcasted_iota(jnp.int32, (BLOCK, 2 * BLOCK), 0)
            kj_ = jnp.bitwise_and(lax.broadcasted_iota(jnp.int32, (BLOCK, 2 * BLOCK), 1), BLOCK - 1)
            mn = kj_ >= qi_ + jnp.where(has_next, 0, BLOCK)
            pn = jnp.exp(jnp.where(mn, _dot_nt(qn, kcs) * SCALE, NEG) - per_head(ltn, BLOCK))
            dsn = pn * (_dot_nt(don, vcs) - row_sums(ddn, BLOCK))
            pb, dsb, pnb, dsnb = p.astype(BF), ds.astype(BF), pn.astype(BF), dsn.astype(BF)

            def own(x, h):
                return x[:, 2 * h * BLOCK:(2 * h + 1) * BLOCK]

            def nxt(x, h):
                return x[:, h * BLOCK:(h + 1) * BLOCK]

            ds_rows = jnp.concatenate([own(dsb, 0), nxt(dsnb, 0), own(dsb, 1), nxt(dsnb, 1)], axis=0)
            p_rows = jnp.concatenate([own(pb, 0), nxt(pnb, 0), own(pb, 1), nxt(pnb, 1)], axis=0)
            dk_ref[i * BLOCK:(i + 1) * BLOCK, lanes] = SCALE * _dot_tn(ds_rows, _head_stack(hm, q, qn))
            dv_ref[i * BLOCK:(i + 1) * BLOCK, lanes] = _dot_tn(p_rows, _head_stack(hm, do, don))

    (qv, kv_, vv), width, (qo, ko, vo) = bv.qkv(hb, g)
    q_cur, _, q_nxt = bv.specs(width, qo)
    k_cur, k_prv, _ = bv.specs(width, ko)
    v_cur, v_prv, _ = bv.specs(width, vo)
    w_cur, _, w_nxt = bv.specs(GROUP_W, 0)
    out = jax.ShapeDtypeStruct((bv.nl, bv.r * GROUP_W), F32)
    dv_, ov, lv = bv.view(dcat[:, :GROUP_W]), bv.view(oa), bv.view(lt)
    res = pl.pallas_call(
        body, name=name, grid=bv.grid,
        in_specs=[q_cur, q_nxt, k_cur, k_prv, v_cur, v_prv, w_cur, w_nxt, w_cur, w_nxt, w_cur, w_nxt],
        out_specs=[w_cur, w_cur, w_cur], out_shape=[out, out, out],
        compiler_params=_cp("parallel", "parallel"))(
            qv, qv, kv_, kv_, vv, vv, dv_, dv_, ov, ov, lv, lv)
    return [t.reshape(s, GROUP_W) for t in res]


def _mem_fwd(hb, q_blk0, kv, name):
    s = hb.shape[0]
    m = kv.shape[0]
    tq = _rows(s)
    npair = MEM_W // LANES

    def body(q_ref, k_ref, v_ref, o_ref, l_ref):
        q, k, v = q_ref[...], k_ref[...], v_ref[...]
        hm = _head_masks()
        o = jnp.zeros((tq, LANES), F32)
        lse_w = jnp.zeros((tq, LANES), F32)
        for h in range(2):
            sc = _dot_nt(_sel(hm[h], q), k) * SCALE
            mx = jnp.max(sc, axis=1, keepdims=True)
            p = jnp.exp(sc - mx)
            l = jnp.sum(p, axis=1, keepdims=True)
            o = o + _dot(p.astype(BF), _sel(hm[h], v)) / l
            lse_w = jnp.where(hm[h], mx + jnp.log(l), lse_w)
        o_ref[...] = o.astype(BF)
        l_ref[...] = lse_w

    blk = pl.BlockSpec((tq, LANES), lambda p, i: (i, p))
    return pl.pallas_call(
        body, name=name, grid=(npair, s // tq),
        in_specs=[pl.BlockSpec((tq, LANES), lambda p, i: (i, q_blk0 + p)),
                  pl.BlockSpec((m, LANES), lambda p, i: (0, p)),
                  pl.BlockSpec((m, LANES), lambda p, i: (0, npair + p))],
        out_specs=[blk, blk],
        out_shape=[jax.ShapeDtypeStruct((s, MEM_W), BF), jax.ShapeDtypeStruct((s, MEM_W), F32)],
        compiler_params=_cp("parallel", "parallel"))(hb, kv, kv)


def _mem_bwd(hb, q_blk0, kv, dcat, cat, o_blk0, lse, name):
    s = hb.shape[0]
    m = kv.shape[0]
    tq = _rows(s)
    npair = MEM_W // LANES

    def body(q_ref, k_ref, v_ref, do_ref, o_ref, l_ref, dq_ref, dk_ref, dv_ref):
        i = pl.program_id(1)

        @pl.when(i == 0)
        def _():
            dk_ref[...] = jnp.zeros_like(dk_ref)
            dv_ref[...] = jnp.zeros_like(dv_ref)

        q, k, v, do = q_ref[...], k_ref[...], v_ref[...], do_ref[...]
        dd = do.astype(F32) * o_ref[...].astype(F32)
        lt = l_ref[...]
        hm = _head_masks()
        dq = jnp.zeros((tq, LANES), F32)
        dk = jnp.zeros((m, LANES), F32)
        dv = jnp.zeros((m, LANES), F32)
        for h in range(2):
            qh, doh = _sel(hm[h], q), _sel(hm[h], do)
            p = jnp.exp(_dot_nt(qh, k) * SCALE - _pick(hm[h], lt, NEG))
            ds = p * (_dot_nt(doh, v) - jnp.sum(_sel(hm[h], dd), axis=1, keepdims=True))
            dq = dq + SCALE * _dot(ds.astype(BF), _sel(hm[h], k))
            dk = dk + SCALE * _dot_tn(ds.astype(BF), qh)
            dv = dv + _dot_tn(p.astype(BF), doh)
        dq_ref[...] = dq
        dk_ref[...] += dk
        dv_ref[...] += dv

    row = pl.BlockSpec((tq, LANES), lambda p, i: (i, p))
    orow = pl.BlockSpec((tq, LANES), lambda p, i: (i, o_blk0 + p))
    acc = pl.BlockSpec((m, LANES), lambda p, i: (0, p))
    return pl.pallas_call(
        body, name=name, grid=(npair, s // tq),
        in_specs=[pl.BlockSpec((tq, LANES), lambda p, i: (i, q_blk0 + p)),
                  pl.BlockSpec((m, LANES), lambda p, i: (0, p)),
                  pl.BlockSpec((m, LANES), lambda p, i: (0, npair + p)), orow, orow, row],
        out_specs=[row, acc, acc],
        out_shape=[jax.ShapeDtypeStruct((s, MEM_W), F32), jax.ShapeDtypeStruct((m, MEM_W), F32),
                   jax.ShapeDtypeStruct((m, MEM_W), F32)],
        compiler_params=_cp("parallel", "arbitrary"))(hb, kv, kv, dcat, cat, lse)


def _gate_fwd(f_t, bias, name):
    hp, s = f_t.shape
    nblk = s // LANES

    def body(f_ref, b_ref, c_ref):
        lane = lax.broadcasted_iota(jnp.int32, (hp, LANES), 1)

        def step(i, carry):
            off = pl.multiple_of(i * LANES, LANES)
            x = f_ref[:, pl.ds(off, LANES)] + b_ref[...]
            acc = jnp.minimum(x, 0.0) - jnp.log(1.0 + jnp.exp(-jnp.abs(x)))
            sh = 1
            while sh < LANES:
                acc = acc + jnp.where(lane >= sh, pltpu.roll(acc, sh, 1), 0.0)
                sh *= 2
            acc = acc + carry
            c_ref[:, pl.ds(off, LANES)] = acc
            return acc[:, LANES - 1:LANES]

        lax.fori_loop(0, nblk, step, jnp.zeros((hp, 1), F32))

    vm = pl.BlockSpec(memory_space=pltpu.VMEM)
    return pl.pallas_call(body, name=name, in_specs=[vm, vm], out_specs=vm,
                          out_shape=jax.ShapeDtypeStruct((hp, s), F32),
                          compiler_params=pltpu.CompilerParams(vmem_limit_bytes=VMEM_LIMIT))(f_t, bias)


def _gate_bwd(dc_t, f_t, bias, name):
    hp, s = f_t.shape
    nblk = s // LANES

    def body(dc_ref, f_ref, b_ref, df_ref, db_ref):
        lane = lax.broadcasted_iota(jnp.int32, (hp, LANES), 1)

        def step(t, carry):
            suffix, dbias = carry
            off = pl.multiple_of((nblk - 1 - t) * LANES, LANES)
            acc = dc_ref[:, pl.ds(off, LANES)]
            sh = 1
            while sh < LANES:
                acc = acc + jnp.where(lane < LANES - sh, pltpu.roll(acc, LANES - sh, 1), 0.0)
                sh *= 2
            acc = acc + suffix
            x = f_ref[:, pl.ds(off, LANES)] + b_ref[...]
            df = acc * _sigmoid(-x)
            df_ref[:, pl.ds(off, LANES)] = df
            return acc[:, 0:1], dbias + jnp.sum(df, axis=1, keepdims=True)

        _, dbias = lax.fori_loop(0, nblk, step, (jnp.zeros((hp, 1), F32), jnp.zeros((hp, 1), F32)))
        db_ref[...] = dbias

    vm = pl.BlockSpec(memory_space=pltpu.VMEM)
    return pl.pallas_call(body, name=name, in_specs=[vm, vm, vm], out_specs=[vm, vm],
                          out_shape=[jax.ShapeDtypeStruct((hp, s), F32), jax.ShapeDtypeStruct((hp, 1), F32)],
                          compiler_params=pltpu.CompilerParams(vmem_limit_bytes=VMEM_LIMIT))(dc_t, f_t, bias)


def _wide(rep, width):
    return jnp.tile(rep, (1, width // LANES))


def _fold(t):
    part = t[:, :LANES]
    for c in range(1, t.shape[1] // LANES):
        part = part + t[:, c * LANES:(c + 1) * LANES]
    return part


def _foxt_logits(q, k, cq_row, ck_rep, mask, hmask):
    s = _dot_nt(_sel(hmask, k), q) + (cq_row - _wide(ck_rep, q.shape[0]))
    if mask is not None:
        s = jnp.where(mask, s, NEG)
    return s


def _causal_t(qi, kj, tq, tk):
    return (kj * tk + lax.broadcasted_iota(jnp.int32, (tk, tq), 0)
            <= qi * tq + lax.broadcasted_iota(jnp.int32, (tk, tq), 1))


FOX_SPLIT = 1


def _fox_tiles(s):
    tq = _rows(s, 1024)
    return tq, tq // FOX_SPLIT, s // tq


def _fox_steps(nq):
    return FOX_SPLIT * nq * (nq + 1) // 2


def _count_ge(t, bounds):
    return sum([(t >= b).astype(jnp.int32) for b in bounds], jnp.int32(0))


def _sweep_q_major(t, nq):
    qi = _count_ge(t, [FOX_SPLIT * r * (r + 1) // 2 for r in range(1, nq)])
    return qi, t - FOX_SPLIT * qi * (qi + 1) // 2


def _sweep_k_major(t, nq):
    counts = [nq - j // FOX_SPLIT for j in range(FOX_SPLIT * nq)]
    offs = [sum(counts[:j]) for j in range(1, FOX_SPLIT * nq)]
    kj = _count_ge(t, offs)
    start = sum([jnp.where(t >= o, c, 0) for o, c in zip(offs, counts)], jnp.int32(0))
    qi = kj // FOX_SPLIT + (t - start)
    return kj, qi, t == start, qi == nq - 1


def _foxt_fwd(hb, c_rep, c_t3, name):
    s = hb.shape[0]
    npair = MIX_W // LANES
    tq, tk, nq = _fox_tiles(s)

    def body(q_ref, k_ref, v_ref, cq_ref, ck_ref, o_ref, l_ref, m_s, l_s, acc):
        qi, kj = _sweep_q_major(pl.program_id(1), nq)
        hm = _head_masks()

        @pl.when(kj == 0)
        def _():
            m_s[...] = jnp.full_like(m_s, NEG)
            l_s[...] = jnp.zeros_like(l_s)
            acc[...] = jnp.zeros_like(acc)

        def step(mask):
            q, k = q_ref[...] * SCALE, k_ref[...]
            vt = jnp.transpose(v_ref[...])
            cq = cq_ref[...]
            for h in range(2):
                st = _foxt_logits(q, k, cq[h:h + 1, :], ck_ref[h], mask, hm[h])
                m_old = m_s[h]
                m_new = jnp.maximum(m_old, jnp.max(st, axis=0, keepdims=True))
                pt = jnp.exp(st - m_new)
                corr = jnp.exp(m_old - m_new)
                l_s[h] = l_s[h] * corr + jnp.sum(pt, axis=0, keepdims=True)
                acc[h] = acc[h] * corr + _dot(vt[h * HEAD_DIM:(h + 1) * HEAD_DIM, :], pt.astype(BF))
                m_s[h] = m_new

        @pl.when(kj < FOX_SPLIT * qi)
        def _():
            step(None)

        @pl.when(kj >= FOX_SPLIT * qi)
        def _():
            step(_causal_t(qi, kj, tq, tk))

        @pl.when(kj == FOX_SPLIT * (qi + 1) - 1)
        def _():
            outs = []
            for h in range(2):
                outs.append(acc[h] / l_s[h])
                l_ref[h:h + 1, :] = m_s[h] + jnp.log(l_s[h])
            o_ref[...] = jnp.transpose(jnp.concatenate(outs, axis=0)).astype(BF)

    def q_map(p, t):
        return (_sweep_q_major(t, nq)[0], p)

    def kv_map(off):
        return lambda p, t: (_sweep_q_major(t, nq)[1], off + p)

    blk = pl.BlockSpec((tq, LANES), q_map)
    row = pl.BlockSpec((None, 2, tq), lambda p, t: (p, 0, _sweep_q_major(t, nq)[0]))
    return pl.pallas_call(
        body, name=name, grid=(npair, _fox_steps(nq)),
        in_specs=[blk, pl.BlockSpec((tk, LANES), kv_map(npair)), pl.BlockSpec((tk, LANES), kv_map(2 * npair)), row,
                  pl.BlockSpec((2, tk, LANES), lambda p, t: (p, _sweep_q_major(t, nq)[1], 0))],
        out_specs=[blk, row],
        out_shape=[jax.ShapeDtypeStruct((s, MIX_W), BF), jax.ShapeDtypeStruct((npair, 2, s), F32)],
        scratch_shapes=[pltpu.VMEM((2, 1, tq), F32), pltpu.VMEM((2, 1, tq), F32),
                        pltpu.VMEM((2, HEAD_DIM, tq), F32)],
        compiler_params=_cp("parallel", "arbitrary"))(hb, hb, hb, c_t3, c_rep)


def _foxt_dsum(hb, dcat, lse, c_rep, c_t3, name):
    s = hb.shape[0]
    npair = MIX_W // LANES
    tq, tk, nq = _fox_tiles(s)

    def body(q_ref, k_ref, v_ref, do_ref, l_ref, cq_ref, ck_ref, d_ref, acc):
        qi, kj = _sweep_q_major(pl.program_id(1), nq)
        hm = _head_masks()

        @pl.when(kj == 0)
        def _():
            acc[...] = jnp.zeros_like(acc)

        def step(mask):
            q, k, v, do = q_ref[...] * SCALE, k_ref[...], v_ref[...], do_ref[...]
            cq, lse_rows = cq_ref[...], l_ref[...]
            for h in range(2):
                pt = jnp.exp(_foxt_logits(q, k, cq[h:h + 1, :], ck_ref[h], mask, hm[h]) - lse_rows[h:h + 1, :])
                acc[h] += jnp.sum(pt * _dot_nt(_sel(hm[h], v), do), axis=0, keepdims=True)

        @pl.when(kj < FOX_SPLIT * qi)
        def _():
            step(None)

        @pl.when(kj >= FOX_SPLIT * qi)
        def _():
            step(_causal_t(qi, kj, tq, tk))

        @pl.when(kj == FOX_SPLIT * (qi + 1) - 1)
        def _():
            for h in range(2):
                d_ref[h:h + 1, :] = acc[h]

    def q_map(p, t):
        return (_sweep_q_major(t, nq)[0], p)

    def kv_map(off):
        return lambda p, t: (_sweep_q_major(t, nq)[1], off + p)

    blk = pl.BlockSpec((tq, LANES), q_map)
    row = pl.BlockSpec((None, 2, tq), lambda p, t: (p, 0, _sweep_q_major(t, nq)[0]))
    return pl.pallas_call(
        body, name=name, grid=(npair, _fox_steps(nq)),
        in_specs=[blk, pl.BlockSpec((tk, LANES), kv_map(npair)), pl.BlockSpec((tk, LANES), kv_map(2 * npair)),
                  blk, row, row, pl.BlockSpec((2, tk, LANES), lambda p, t: (p, _sweep_q_major(t, nq)[1], 0))],
        out_specs=row, out_shape=jax.ShapeDtypeStruct((npair, 2, s), F32),
        scratch_shapes=[pltpu.VMEM((2, 1, tq), F32)],
        compiler_params=_cp("parallel", "arbitrary"))(hb, hb, hb, dcat, lse, c_t3, c_rep)


def _foxt_bwd(hb, dcat, dsum, lse, c_rep, c_t3, name):
    s = hb.shape[0]
    npair = MIX_W // LANES
    tq, tk, nq = _fox_tiles(s)

    def body(q_ref, k_ref, v_ref, do_ref, d_ref, l_ref, cq_ref, ck_ref, dq_ref, dk_ref, dv_ref, dc_ref, dc_s):
        t = pl.program_id(1)
        kj, qi, first, last = _sweep_k_major(t, nq)
        hm = _head_masks()

        @pl.when(first)
        def _():
            dk_ref[...] = jnp.zeros_like(dk_ref)
            dv_ref[...] = jnp.zeros_like(dv_ref)
            dc_s[...] = jnp.zeros_like(dc_s)

        @pl.when(t == 0)
        def _():
            dq_ref[...] = jnp.zeros_like(dq_ref)

        def step(mask):
            q, k, v, do = q_ref[...] * SCALE, k_ref[...], v_ref[...], do_ref[...]
            qt, kt, dot = jnp.transpose(q), jnp.transpose(k), jnp.transpose(do)
            cq, lse_rows, d_rows = cq_ref[...], l_ref[...], d_ref[...]
            dqs, dks, dvs = [], [], []
            for h in range(2):
                rows = slice(h * HEAD_DIM, (h + 1) * HEAD_DIM)
                pt = jnp.exp(_foxt_logits(q, k, cq[h:h + 1, :], ck_ref[h], mask, hm[h]) - lse_rows[h:h + 1, :])
                dst = pt * (_dot_nt(_sel(hm[h], v), do) - d_rows[h:h + 1, :])
                dsb = dst.astype(BF)
                dqs.append(_dot(kt[rows, :], dsb))
                dks.append(_dot_nt(qt[rows, :], dsb))
                dvs.append(_dot_nt(dot[rows, :], pt.astype(BF)))
                dc_s[h] += _fold(dst)
            cols = pl.ds(pl.multiple_of(qi * tq, tq), tq)
            dq_ref[:, cols] += SCALE * jnp.concatenate(dqs, axis=0)
            dk_ref[...] += jnp.concatenate(dks, axis=0)
            dv_ref[...] += jnp.concatenate(dvs, axis=0)

        @pl.when(kj < FOX_SPLIT * qi)
        def _():
            step(None)

        @pl.when(kj >= FOX_SPLIT * qi)
        def _():
            step(_causal_t(qi, kj, tq, tk))

        @pl.when(last)
        def _():
            for h in range(2):
                dc_ref[h:h + 1, :] = -jnp.sum(jnp.transpose(dc_s[h]), axis=0, keepdims=True)

    def kj_of(t):
        return _sweep_k_major(t, nq)[0]

    def qi_of(t):
        return _sweep_k_major(t, nq)[1]

    qblk = pl.BlockSpec((tq, LANES), lambda p, t: (qi_of(t), p))
    row = pl.BlockSpec((None, 2, tq), lambda p, t: (p, 0, qi_of(t)))
    kblk = pl.BlockSpec((LANES, tk), lambda p, t: (p, kj_of(t)))
    rep = pl.BlockSpec((2, tk, LANES), lambda p, t: (p, kj_of(t), 0))
    return pl.pallas_call(
        body, name=name, grid=(npair, _fox_steps(nq)),
        in_specs=[qblk,
                  pl.BlockSpec((tk, LANES), lambda p, t: (kj_of(t), npair + p)),
                  pl.BlockSpec((tk, LANES), lambda p, t: (kj_of(t), 2 * npair + p)),
                  qblk, row, row, row, rep],
        out_specs=[pl.BlockSpec((LANES, s), lambda p, t: (p, 0)), kblk, kblk,
                   pl.BlockSpec((None, 2, tk), lambda p, t: (p, 0, kj_of(t)))],
        out_shape=[jax.ShapeDtypeStruct((MIX_W, s), F32), jax.ShapeDtypeStruct((MIX_W, s), F32),
                   jax.ShapeDtypeStruct((MIX_W, s), F32), jax.ShapeDtypeStruct((npair, 2, s), F32)],
        scratch_shapes=[pltpu.VMEM((2, tk, LANES), F32)],
        compiler_params=_cp("arbitrary", "arbitrary"))(hb, hb, hb, dcat, dsum, lse, c_t3, c_rep)


def _loss_head(y, target, name):
    s, d = y.shape
    ts = _rows(s)

    def body(y_ref, t_ref, dy_ref, l_ref):
        i = pl.program_id(0)
        e = y_ref[...] - t_ref[...]
        dy_ref[...] = e * (1.0 / d)

        @pl.when(i == 0)
        def _():
            l_ref[...] = jnp.zeros_like(l_ref)

        part = jnp.sum(jnp.sum(e * e, axis=1, keepdims=True), axis=0, keepdims=True)
        l_ref[...] += part * (0.5 / d)

    row = pl.BlockSpec((ts, d), lambda i: (i, 0))
    return pl.pallas_call(
        body, name=name, grid=(s // ts,), in_specs=[row, row],
        out_specs=[row, pl.BlockSpec((1, 1), lambda i: (0, 0))],
        out_shape=[jax.ShapeDtypeStruct((s, d), F32), jax.ShapeDtypeStruct((1, 1), F32)],
        compiler_params=_cp("arbitrary"))(y, target)


def _adam_rows(r, c):
    cap = max(8, (1 << 20) // (4 * c))
    if r <= cap:
        return r
    best = None
    for t in range(8, cap + 1, 8):
        if r % t == 0:
            best = t
    return best if best is not None else r


def _reduce_adamw(contribs, w, m, v, name):
    nl = len(contribs)
    nd, r, c = contribs[0].shape
    tr = _adam_rows(r, c)
    bc1 = 1.0 - ADAM_B1 ** ADAM_STEP
    bc2 = 1.0 - ADAM_B2 ** ADAM_STEP

    def body(*refs):
        c_refs = refs[:nl]
        w_ref, m_ref, v_ref, g_ref, d_ref, nm_ref, nv_ref = refs[nl:]
        l = pl.program_id(0)
        for li in range(nl):
            @pl.when(l == li)
            def _(c_ref=c_refs[li]):
                g = c_ref[0].astype(F32)
                for k in range(1, nd):
                    g = g + c_ref[k].astype(F32)
                nm = ADAM_B1 * m_ref[...] + (1.0 - ADAM_B1) * g
                nv = ADAM_B2 * v_ref[...] + (1.0 - ADAM_B2) * (g * g)
                g_ref[...] = g
                nm_ref[...] = nm
                nv_ref[...] = nv
                d_ref[...] = -ADAM_LR * ((nm / bc1) / (jnp.sqrt(nv / bc2) + ADAM_EPS) + ADAM_WD * w_ref[...])

    def c_spec(li):
        return pl.BlockSpec((nd, tr, c), lambda l, i: (0, jnp.where(l == li, i, 0), 0))

    blk = pl.BlockSpec((None, tr, c), lambda l, i: (l, i, 0))
    out = jax.ShapeDtypeStruct((nl, r, c), F32)
    return pl.pallas_call(
        body, name=name, grid=(nl, r // tr),
        in_specs=[c_spec(li) for li in range(nl)] + [blk, blk, blk],
        out_specs=[blk, blk, blk, blk], out_shape=[out, out, out, out],
        compiler_params=_cp("arbitrary", "arbitrary"))(*contribs, w, m, v)


def _mesh_pos():
    return lax.axis_index("x"), lax.axis_index("y"), lax.axis_index("c")


def _peer(pos, k):
    x, y, c = pos
    return (1 - x if k & 4 else x, 1 - y if k & 2 else y, 1 - c if k & 1 else c)


def _linear(pos):
    return 4 * pos[0] + 2 * pos[1] + pos[2]


def _xfer_copies(srcs, lands, send_sems, recv_sems, local_sems, gather):
    pos = _mesh_pos()
    me = _linear(pos)
    local, remote = [], []
    for i, (src, land) in enumerate(zip(srcs, lands)):
        local.append(pltpu.make_async_copy(src if gather else src.at[me], land.at[me], local_sems.at[i]))
        for k in range(1, N_DEV):
            peer = _peer(pos, k)
            remote.append(pltpu.make_async_remote_copy(
                src_ref=src if gather else src.at[_linear(peer)], dst_ref=land.at[me],
                send_sem=send_sems.at[i * (N_DEV - 1) + k - 1], recv_sem=recv_sems.at[i * (N_DEV - 1) + k - 1],
                device_id=peer, device_id_type=MESH_ID))
    return local, remote


_HBM = pl.BlockSpec(memory_space=pltpu.HBM)
_SEM = pl.BlockSpec(memory_space=pltpu.SEMAPHORE)
_EFFECT = pltpu.SideEffectType.DATAFLOW_SIDE_EFFECTING


def _xfer_start(srcs, gather, name, after=()):
    n = len(srcs)
    na = len(after)
    lands = [lax.empty(((N_DEV,) + a.shape) if gather else a.shape, a.dtype) for a in srcs]

    def body(*refs):
        src, land = refs[:n], refs[n:2 * n]
        send_sems, recv_sems, local_sems = refs[2 * n + na:2 * n + na + 3]
        local, remote = _xfer_copies(src, land, send_sems, recv_sems, local_sems, gather)
        for cp in local + remote:
            cp.start()
        refs[-1][...] = jnp.zeros_like(refs[-1])

    nsem = n * (N_DEV - 1)
    out = pl.pallas_call(
        body, name=name,
        out_shape=(pltpu.SemaphoreType.DMA((nsem,)), pltpu.SemaphoreType.DMA((nsem,)), pltpu.SemaphoreType.DMA((n,)),
                   *[pltpu.HBM(a.shape, a.dtype) for a in srcs], *[pltpu.HBM(a.shape, a.dtype) for a in lands],
                   jax.ShapeDtypeStruct((8, LANES), F32)),
        in_specs=[_HBM] * (2 * n) + [pl.BlockSpec(memory_space=pl.ANY)] * na,
        out_specs=(_SEM, _SEM, _SEM, *[_HBM] * (2 * n), pl.BlockSpec(memory_space=pltpu.VMEM)),
        input_output_aliases={i: 3 + i for i in range(2 * n)},
        compiler_params=pltpu.CompilerParams(has_side_effects=_EFFECT))(
            *[pltpu.with_memory_space_constraint(a, pltpu.HBM) for a in srcs],
            *[pltpu.with_memory_space_constraint(a, pltpu.HBM) for a in lands], *after)
    return out[:3], list(out[3:3 + n]), list(out[3 + n:3 + 2 * n]), out[-1]


def _started(handle):
    return handle[3]


def _xfer_wait(handle, after, gather, name):
    sems, srcs, lands, _ = handle
    n = len(srcs)

    def body(*refs):
        src, land = refs[:n], refs[n:2 * n]
        send_sems, recv_sems, local_sems = refs[2 * n:2 * n + 3]
        local, remote = _xfer_copies(src, land, send_sems, recv_sems, local_sems, gather)
        for cp in local:
            cp.wait()
        for cp in remote:
            cp.wait_send()
            cp.wait_recv()

    out = pl.pallas_call(
        body, name=name,
        out_shape=(*[pltpu.HBM(a.shape, a.dtype) for a in srcs], *[pltpu.HBM(a.shape, a.dtype) for a in lands]),
        in_specs=[_HBM] * (2 * n) + [_SEM] * 3 + [pl.BlockSpec(memory_space=pl.ANY)] * len(after),
        out_specs=tuple([_HBM] * (2 * n)), input_output_aliases={i: i for i in range(2 * n)},
        compiler_params=pltpu.CompilerParams(has_side_effects=_EFFECT))(*srcs, *lands, *sems, *after)
    return list(out[n:])


def _cols_full(g):
    nd, r, c = g.shape
    return jnp.transpose(g, (1, 0, 2)).reshape(r, nd * c)


def _cols_split(full):
    r, n = full.shape
    return jnp.transpose(full.reshape(r, N_DEV, n // N_DEV), (1, 0, 2))


def _pack_b_in(w):
    qkv = 3 * MIX_W
    pad = jnp.zeros((w.shape[0], B_IN_PAD - w.shape[1]), w.dtype)
    return jnp.concatenate([w[:, :qkv], w[:, qkv + N_MIX_HEADS:], w[:, qkv:qkv + N_MIX_HEADS], pad], axis=1)


def _unpack_b_in(w):
    qkv = 3 * MIX_W
    return jnp.concatenate([w[:, :qkv], w[:, qkv + MEM_W:qkv + MEM_W + N_MIX_HEADS], w[:, qkv:qkv + MEM_W]], axis=1)


def _ffn_forward(x, xb, wgu, get_rest, tag, fused=True):
    if fused:
        wd4, gain, bias = get_rest(x)
        y, yb, gu, a, xh, rstd = _ffn_fwd_main(x, xb, wgu, wd4, gain, bias, f"{tag}_fwd_main")
    else:
        gu, a = _ffn_up(xb, wgu, f"{tag}_up")
        wd4, gain, bias = get_rest(a)
        y, yb, xh, rstd = _mm_res_ln(a, wd4, x, gain, bias, 0.5, f"{tag}_down_ln")
    return y, yb, (xb, gu, a, xh, rstd), wd4


def _ffn_backward(dy, saved, wgu, wd4, gain, tag, after=(), send=None):
    xb, gu, a, xh, rstd = saved
    s = xb.shape[0]
    nd, c, d = wgu.shape
    dx, dzb, dh, dgain, dbias = _ffn_bwd_main(dy, xh, rstd, gain, wd4, wgu, gu, f"{tag}_bwd_main", after,
                                               with_dx=send is None)
    dh = dh.reshape(nd, s, c)
    dwd = _mm_tn(a, dzb[None], f"{tag}_dwd").reshape(nd, wd4.shape[1] // 2, d)
    if send is not None:
        send("down", dwd, dgain, dbias)
    dwgu = _mm_tn(dh, xb[None], f"{tag}_dwgu")
    if send is not None:
        sent = send("gate_up", dwgu)
        dx = _mm_nt(dh, wgu, f"{tag}_dx", res=dx, w_rows_out=False, after=sent)
    return dx, dwgu, dwd, dgain, dbias


def _mixer_a_forward(x, xb, memb, w_in, w_kv, w_out, gain, bias, tabs):
    hb = _proj_rope(xb, w_in, tabs, 2 * MIX_W // LANES, "a_in", True)
    groups = [_band_fwd(hb, g, f"a_band_fwd{g}") for g in range(N_GROUPS)]
    oa, lt = _band_combine([o for o, _ in groups], [l for _, l in groups], "a_combine")
    kv = _mm_nn(memb, w_kv, BF, "a_mem_kv")
    om, lm = _mem_fwd(hb, 3 * MIX_W // LANES, kv, "a_mem_fwd")
    cat = jnp.concatenate([oa, om], axis=1)
    y, yb, xh, rstd = _mm_res_ln(cat[None], w_out[None], x, gain, bias, 1.0, "a_out_ln")
    return y, yb, (xb, hb, oa, lt, kv, lm, cat, xh, rstd)


def _mixer_a_backward(dy, saved, memb, w_in, w_kv, w_out, gain, tabs_neg, after=()):
    xb, hb, oa, lt, kv, lm, cat, xh, rstd = saved
    dz, dzb, dgain, dbias = _ln_bwd(dy, xh, rstd, gain, 1.0, "a_ln_bwd", after)
    dcat = _mm_nt(dzb[None], w_out[None], "a_dcat", out_dtype=BF)
    dw_out = _mm_tn(cat[None], dzb[None], "a_dwout")[0]
    dqm, dkm, dvm = _mem_bwd(hb, 3 * MIX_W // LANES, kv, dcat, cat, GROUP_W // LANES, lm, "a_mem_bwd")
    dkv = jnp.concatenate([dkm, dvm], axis=1).astype(BF)
    dw_kv = _mm_tn(memb[None], dkv[None], "a_dwkv")[0]
    grads = [_band_bwd(hb, dcat, oa, lt, g, f"a_band_bwd{g}") for g in range(N_GROUPS)]
    dhb = _rope_cast([grads[g][i] for i in range(3) for g in range(N_GROUPS)] + [dqm], tabs_neg,
                     2 * MIX_W // LANES, "a_rope_bwd")
    dw_in = _mm_tn(dhb[None], xb[None], "a_dwin")[0]
    dx = _mm_nt(dhb[None], w_in[None], "a_dx", res=dz, w_rows_out=False)
    return dx, dw_in, dw_kv, dw_out, dgain, dbias


def _pad_rows(t, rows):
    return jnp.concatenate([t, jnp.zeros((rows - t.shape[0], t.shape[1]), t.dtype)], axis=0)


def _pad_cols(t, cols):
    return jnp.concatenate([t, jnp.zeros((t.shape[0], cols - t.shape[1]), t.dtype)], axis=1)


def _mixer_b_forward(x, xb, memb, w_in, fbias, w_kv, w_out, gain, bias, tabs):
    s = x.shape[0]
    hb, f = _proj_rope(xb, w_in, tabs, 0, "b_in", False, tail_block=(3 * MIX_W + MEM_W) // LANES)
    f_t = _pad_rows(jnp.transpose(f[:, :N_MIX_HEADS]), 16)
    bias16 = _pad_rows(jnp.transpose(fbias), 16)
    c_t = _gate_fwd(f_t, bias16, "b_gate_fwd")
    c_t3 = c_t[:N_MIX_HEADS].reshape(N_MIX_HEADS // 2, 2, s)
    c_rep = jnp.broadcast_to(c_t[:N_MIX_HEADS, :, None], (N_MIX_HEADS, s, LANES))
    ob, lb = _foxt_fwd(hb, c_rep, c_t3, "b_fox_fwd")
    kv = _mm_nn(memb, w_kv, BF, "b_mem_kv")
    om, lm = _mem_fwd(hb, 3 * MIX_W // LANES, kv, "b_mem_fwd")
    cat = jnp.concatenate([ob, om], axis=1)
    y, yb, xh, rstd = _mm_res_ln(cat[None], w_out[None], x, gain, bias, 1.0, "b_out_ln")
    return y, yb, (xb, hb, f_t, bias16, c_rep, c_t3, lb, kv, lm, cat, xh, rstd)


def _mixer_b_backward(dy, saved, memb, w_in, w_kv, w_out, gain, tabs, after=()):
    xb, hb, f_t, bias16, c_rep, c_t3, lb, kv, lm, cat, xh, rstd = saved
    s = xb.shape[0]
    dz, dzb, dgain, dbias = _ln_bwd(dy, xh, rstd, gain, 1.0, "b_ln_bwd", after)
    dcat = _mm_nt(dzb[None], w_out[None], "b_dcat", out_dtype=BF)
    dw_out = _mm_tn(cat[None], dzb[None], "b_dwout")[0]
    dqm, dkm, dvm = _mem_bwd(hb, 3 * MIX_W // LANES, kv, dcat, cat, MIX_W // LANES, lm, "b_mem_bwd")
    dkv = jnp.concatenate([dkm, dvm], axis=1).astype(BF)
    dw_kv = _mm_tn(memb[None], dkv[None], "b_dwkv")[0]
    dsum = _foxt_dsum(hb, dcat, lb, c_rep, c_t3, "b_fox_dsum")
    dq, dk, dv, dc3 = _foxt_bwd(hb, dcat, dsum, lb, c_rep, c_t3, "b_fox_bwd")
    df_t, dfb = _gate_bwd(_pad_rows(dc3.reshape(N_MIX_HEADS, s), 16), f_t, bias16, "b_gate_bwd")
    df = _pad_cols(jnp.transpose(df_t[:N_MIX_HEADS]), B_IN_PAD - 3 * MIX_W - MEM_W)
    dhb = _rope_cast([dq, dk, dv, dqm, df], tabs, 0, "b_cast_bwd", transposed=(0, 1, 2))
    dw_in = _mm_tn(xb[None], dhb[None], "b_dwin")[0]
    dx = _mm_nt(dhb[None], w_in[None], "b_dx", res=dz)
    return dx, dw_in, jnp.transpose(dfb[:N_MIX_HEADS]), dw_kv, dw_out, dgain, dbias


def _stored(t, name):
    return jnp.transpose(t, (0, 2, 1)) if name in ROWS_OUT else t


GATHER_GROUPS = (
    (("ffn1_w_gate_up", 0),),
    (("ffn1_w_down", 0), ("ln_gain", None), ("ln_bias", None)),
    (("a_w_in", 0), ("a_w_out", 0), ("mem_w_kv", 0)),
    (("ffn2_w_gate_up", 0), ("ffn2_w_down", 0)),
    (("ffn1_w_gate_up", 1), ("ffn1_w_down", 1)),
    (("b_w_in", 0), ("b_w_out", 0), ("mem_w_kv", 1)),
    (("ffn2_w_gate_up", 1), ("ffn2_w_down", 1)),
)


def _group_shards(group, params):
    return [t if n in F32_COMM else _stored(t, n)[l].astype(BF) for (n, l), t in zip(group, params)]


def _weight_groups(w):
    return [_group_shards(grp, [w[n] for n, _ in grp]) for grp in GATHER_GROUPS]


def _local_step(x, mem, target, fbias, get_w, put_g):
    s, d = x.shape
    tabs = _rope_tables(s, 1.0)
    tabs_neg = _rope_tables(s, -1.0)
    memb = mem.astype(BF)
    saved, wl = [], []
    cur, curb = x, x.astype(BF)
    ln = []

    def down4(t):
        return t.reshape(N_DEV // 2, -1, d)

    for i in range(DEPTH):
        if i == 0:
            def first_rest(a):
                g = get_w(1, a)
                ln.extend(jnp.transpose(t, (1, 2, 0, 3)).reshape(DEPTH, 3, 1, d) for t in g[1:3])
                return down4(g[0]), ln[0][0, 0], ln[1][0, 0]

            wgu = get_w(0, cur)[0]
            cur, curb, s1, wd = _ffn_forward(cur, curb, wgu, first_rest, "l0_ffn1", fused=False)
        else:
            g = get_w(3 * i + 1, cur)
            wgu = g[0]
            cur, curb, s1, wd = _ffn_forward(cur, curb, wgu, lambda a, g=g: (down4(g[1]), ln[0][i, 0], ln[1][i, 0]),
                                             f"l{i}_ffn1")
        w1 = (wgu, wd)
        ln_g, ln_b = ln
        g = get_w(3 * i + 2, cur)
        if i == 0:
            wm = (g[0].reshape(-1, d), g[2].reshape(d, -1), _cols_full(g[1]))
            cur, curb, s2 = _mixer_a_forward(cur, curb, memb, wm[0], wm[1], wm[2], ln_g[i, 1], ln_b[i, 1], tabs)
        else:
            wm = (_pack_b_in(g[0].reshape(d, -1)), g[2].reshape(d, -1), g[1].reshape(d, -1))
            cur, curb, s2 = _mixer_b_forward(cur, curb, memb, wm[0], fbias, wm[1], wm[2], ln_g[i, 1], ln_b[i, 1],
                                             tabs)
        g = get_w(3 * i + 3, cur)
        cur, curb, s3, wd = _ffn_forward(cur, curb, g[0], lambda a, g=g: (down4(g[1]), ln_g[i, 2], ln_b[i, 2]),
                                         f"l{i}_ffn2")
        w3 = (g[0], wd)
        saved.append((s1, s2, s3))
        wl.append((w1, wm, w3))

    dy, loss = _loss_head(cur, target, "loss_head")

    dgs = [[None] * 3 for _ in range(DEPTH)]
    dbs = [[None] * 3 for _ in range(DEPTH)]
    sent = ()
    for i in reversed(range(DEPTH)):
        s1, s2, s3 = saved[i]
        w1, wm, w3 = wl[i]
        dy, dgu, dd, dgs[i][2], dbs[i][2] = _ffn_backward(dy, s3, w3[0], w3[1], ln_g[i, 2], f"l{i}_ffn2", sent)
        sent = put_g(3 * i + 2, [dgu, dd])
        if i == 0:
            dy, dw_in, dw_kv, dw_out, dgs[i][1], dbs[i][1] = _mixer_a_backward(
                dy, s2, memb, wm[0], wm[1], wm[2], ln_g[i, 1], tabs_neg, sent)
            sent = put_g(1, [dw_in.reshape(N_DEV, -1, d), _cols_split(dw_out),
                             dw_kv.reshape(N_DEV, d // N_DEV, -1)])
        else:
            dy, dw_in, dfb, dw_kv, dw_out, dgs[i][1], dbs[i][1] = _mixer_b_backward(
                dy, s2, memb, wm[0], wm[1], wm[2], ln_g[i, 1], tabs, sent)
            sent = put_g(4, [_unpack_b_in(dw_in).reshape(N_DEV, d // N_DEV, -1),
                             dw_out.reshape(N_DEV, d // N_DEV, -1), dw_kv.reshape(N_DEV, d // N_DEV, -1),
                             jnp.broadcast_to(dfb[None], (N_DEV,) + dfb.shape)])
        if i == 0:
            def send_last(kind, dw, dgain=None, dbias=None):
                if kind == "gate_up":
                    return put_g(6, [dw])
                dgs[0][0], dbs[0][0] = dgain, dbias
                ln_pieces = []
                for parts in (dgs, dbs):
                    t = jnp.concatenate([parts[a][b] for a in range(DEPTH) for b in range(3)], axis=0)
                    ln_pieces.append(jnp.transpose(t.reshape(DEPTH * 3, N_DEV, d // N_DEV), (1, 0, 2)))
                return put_g(0, [dw] + ln_pieces)

            dy = _ffn_backward(dy, s1, w1[0], w1[1], ln_g[i, 0], "l0_ffn1", sent, send_last)[0]
        else:
            dy, dgu, dd, dgs[i][0], dbs[i][0] = _ffn_backward(dy, s1, w1[0], w1[1], ln_g[i, 0], f"l{i}_ffn1", sent)
            sent = put_g(3, [dgu, dd])
    return loss, dy


WEIGHTS = ("ffn1_w_gate_up", "ffn1_w_down", "ffn2_w_gate_up", "ffn2_w_down", "ln_gain", "ln_bias", "mem_w_kv",
           "a_w_in", "a_w_out", "b_w_in", "b_forget_bias", "b_w_out")
F32_COMM = ("ln_gain", "ln_bias", "b_forget_bias")
ROWS_OUT = ("ffn1_w_gate_up", "ffn2_w_gate_up", "a_w_in")
GRAD_SLOTS = {
    "ffn1_w_gate_up": [(6, 0), (3, 0)], "ffn1_w_down": [(0, 0), (3, 1)],
    "ffn2_w_gate_up": [(2, 0), (5, 0)], "ffn2_w_down": [(2, 1), (5, 1)],
    "ln_gain": [(0, 1)], "ln_bias": [(0, 2)], "mem_w_kv": [(1, 2), (4, 2)],
    "a_w_in": [(1, 0)], "a_w_out": [(1, 1)], "b_w_in": [(4, 0)], "b_forget_bias": [(4, 3)], "b_w_out": [(4, 1)],
}


def kernel(x, mem, ffn1_w_gate_up, ffn1_w_down, ffn2_w_gate_up, ffn2_w_down, ln_gain, ln_bias, mem_w_kv, a_w_in, a_w_out, b_w_in, b_forget_bias, b_w_out, loss_target, m_ffn1_w_gate_up, m_ffn1_w_down, m_ffn2_w_gate_up, m_ffn2_w_down, m_ln_gain, m_ln_bias, m_mem_w_kv, m_a_w_in, m_a_w_out, m_b_w_in, m_b_forget_bias, m_b_w_out, v_ffn1_w_gate_up, v_ffn1_w_down, v_ffn2_w_gate_up, v_ffn2_w_down, v_ln_gain, v_ln_bias, v_mem_w_kv, v_a_w_in, v_a_w_out, v_b_w_in, v_b_forget_bias, v_b_w_out):
    w = dict(zip(WEIGHTS, (ffn1_w_gate_up, ffn1_w_down, ffn2_w_gate_up, ffn2_w_down, ln_gain, ln_bias, mem_w_kv,
                           a_w_in, a_w_out, b_w_in, b_forget_bias, b_w_out)))
    m = dict(zip(WEIGHTS, (m_ffn1_w_gate_up, m_ffn1_w_down, m_ffn2_w_gate_up, m_ffn2_w_down, m_ln_gain, m_ln_bias,
                           m_mem_w_kv, m_a_w_in, m_a_w_out, m_b_w_in, m_b_forget_bias, m_b_w_out)))
    v = dict(zip(WEIGHTS, (v_ffn1_w_gate_up, v_ffn1_w_down, v_ffn2_w_gate_up, v_ffn2_w_down, v_ln_gain, v_ln_bias,
                           v_mem_w_kv, v_a_w_in, v_a_w_out, v_b_w_in, v_b_forget_bias, v_b_w_out)))

    gathers = []
    for k, grp in enumerate(GATHER_GROUPS):
        params, behind = [w[n] for n, _ in grp], [_started(h) for h in gathers[-1:]]
        if behind:
            params, behind = lax.optimization_barrier((params, behind))
        gathers.append(_xfer_start(_group_shards(grp, params), True, f"gather{k}_start", behind))
    exchanges = {}

    def get_w(k, after):
        behind = [after] + ([_started(h) for h in gathers] if k == 0 else [])
        return _xfer_wait(gathers[k], behind, True, f"gather{k}_wait")

    def put_g(k, pieces):
        behind = [_started(exchanges[0])] if k == 6 else []
        exchanges[k] = _xfer_start(pieces, False, f"grads{k}_start", behind)
        return (_started(exchanges[k]),)

    loss, grad_x = _local_step(x[0], mem[0], loss_target[0], b_forget_bias, get_w, put_g)
    loss = lax.psum(loss[0, 0], ("x", "y", "c"))

    outs, landed = {}, {}

    def adamw(names):
        for n in names:
            contribs = [landed[g][j] for g, j in GRAD_SLOTS[n]]
            view = (len(contribs),) + contribs[0].shape[1:]
            shape = _stored(w[n], n).shape
            res = _reduce_adamw(contribs, *[_stored(t[n], n).reshape(view) for t in (w, m, v)], f"adamw_{n}")
            outs[n] = [_stored(t.reshape(shape), n) for t in res]
        return [outs[n][3] for n in names]

    after = [grad_x]
    for k in (5, 4, 3, 2, 1):
        landed[k] = _xfer_wait(exchanges[k], after, False, f"grads{k}_wait")
        after = [landed[k][0]]
    done = adamw(("ffn2_w_gate_up", "ffn2_w_down", "mem_w_kv", "a_w_in", "a_w_out", "b_w_in", "b_forget_bias",
                  "b_w_out"))
    landed[0] = _xfer_wait(exchanges[0], done, False, "grads0_wait")
    done = adamw(("ffn1_w_down", "ln_gain", "ln_bias"))
    landed[6] = _xfer_wait(exchanges[6], done, False, "grads6_wait")
    adamw(("ffn1_w_gate_up",))
    return (loss, grad_x[None], *[outs[n][0] for n in WEIGHTS], *[outs[n][1] for n in WEIGHTS],
            *[outs[n][2] for n in WEIGHTS], *[outs[n][3] for n in WEIGHTS])
```

```python
import functools

import jax
import jax.numpy as jnp
from jax import lax
from jax.experimental import pallas as pl
from jax.experimental.pallas import tpu as pltpu

F32 = jnp.float32
BF = jnp.bfloat16
MESH_ID = pl.DeviceIdType.MESH

N_DEV = 8
DEPTH = 2
HEAD_DIM = 64
LANES = 128
N_MIX_HEADS = 12
N_MEM_HEADS = 4
MIX_W = N_MIX_HEADS * HEAD_DIM
MEM_W = N_MEM_HEADS * HEAD_DIM
N_GROUPS = 3
GROUP_W = MIX_W // N_GROUPS
BLOCK = 128
BAND_SUB = 4
BAND_COLS = 4
ROT_HALF = 8
ROPE_THETA = 500000.0
ALPHA = (2 * DEPTH) ** 0.25
LN_EPS = 1e-5
SCALE = HEAD_DIM ** -0.5
NEG = -1e30
B_IN_PAD = 2688
ADAM_LR, ADAM_B1, ADAM_B2, ADAM_EPS, ADAM_WD, ADAM_STEP = 0.001, 0.9, 0.999, 1e-08, 0.01, 10
VMEM_LIMIT = 56 * 1024 * 1024


def _cp(*sem):
    return pltpu.CompilerParams(dimension_semantics=sem, vmem_limit_bytes=VMEM_LIMIT)


def _dot(a, b):
    return jnp.dot(a, b, preferred_element_type=F32)


def _dot_nt(a, b):
    return lax.dot_general(a, b, (((1,), (1,)), ((), ())), preferred_element_type=F32)


def _dot_tn(a, b):
    return lax.dot_general(a, b, (((0,), (0,)), ((), ())), preferred_element_type=F32)


def _sigmoid(x):
    return 1.0 / (1.0 + jnp.exp(-x))


def _tile(n, cap=1024):
    if n <= cap:
        return n
    best = LANES
    for t in range(LANES, cap + 1, LANES):
        if n % t == 0:
            best = t
    return best


def _rows(s, cap=512):
    return s if s <= cap else cap


def _mm_nn(a, b, out_dtype, name, b_rows_out=False):
    m, k = a.shape
    n = b.shape[0] if b_rows_out else b.shape[1]
    tm, tn = _rows(m), _tile(n)

    def body(a_ref, b_ref, o_ref):
        prod = _dot_nt(a_ref[...], b_ref[...]) if b_rows_out else _dot(a_ref[...], b_ref[...])
        o_ref[...] = prod.astype(o_ref.dtype)

    b_spec = (pl.BlockSpec((tn, k), lambda j, i: (j, 0)) if b_rows_out
              else pl.BlockSpec((k, tn), lambda j, i: (0, j)))
    return pl.pallas_call(
        body, name=name, grid=(n // tn, m // tm),
        in_specs=[pl.BlockSpec((tm, k), lambda j, i: (i, 0)), b_spec],
        out_specs=pl.BlockSpec((tm, tn), lambda j, i: (i, j)),
        out_shape=jax.ShapeDtypeStruct((m, n), out_dtype),
        compiler_params=_cp("parallel", "parallel"))(a, b)


def _resident(shape, index_map):
    return pl.BlockSpec(shape, index_map, pipeline_mode=pl.Buffered(1))


def _mm_tn(a, b, name, out_dtype=BF):
    na, s, m = a.shape
    nb, _, n = b.shape
    no = max(na, nb)
    tm, tn = _tile(m), _tile(n)

    def body(a_ref, b_ref, o_ref):
        o_ref[...] = _dot_tn(a_ref[...], b_ref[...]).astype(o_ref.dtype)

    def spec(nbatch, width, tile, index_map):
        fixed = nbatch == 1 and width == tile
        return _resident((None, s, tile), index_map) if fixed else pl.BlockSpec((None, s, tile), index_map)

    return pl.pallas_call(
        body, name=name, grid=(no, m // tm, n // tn),
        in_specs=[spec(na, m, tm, lambda j, r, c: (j if na > 1 else 0, 0, r)),
                  spec(nb, n, tn, lambda j, r, c: (j if nb > 1 else 0, 0, c))],
        out_specs=pl.BlockSpec((None, tm, tn), lambda j, r, c: (j, r, c)),
        out_shape=jax.ShapeDtypeStruct((no, m, n), out_dtype),
        compiler_params=_cp("parallel", "parallel", "parallel"))(a, b)


def _mm_nt(dh, w, name, res=None, out_dtype=F32, w_rows_out=True, after=()):
    nc, s, kc = dh.shape
    d = w.shape[1] if w_rows_out else w.shape[2]
    ts = _rows(s)
    has_res = res is not None
    mm = _dot_nt if w_rows_out else _dot

    def body(*refs):
        o_ref = refs[-1]
        dh_ref, w_ref = refs[:2]
        if has_res:
            r_ref = refs[2]
        out = mm(dh_ref[0], w_ref[0])
        for j in range(1, nc):
            out = out + mm(dh_ref[j], w_ref[j])
        if has_res:
            out = out + ALPHA * r_ref[...]
        o_ref[...] = out.astype(o_ref.dtype)

    in_specs = [pl.BlockSpec((nc, ts, kc), lambda i: (0, i, 0)), _resident(w.shape, lambda i: (0, 0, 0))]
    args = [dh, w]
    if has_res:
        in_specs.append(pl.BlockSpec((ts, d), lambda i: (i, 0)))
        args.append(res)
    in_specs += [pl.BlockSpec(memory_space=pl.ANY)] * len(after)
    args += list(after)
    return pl.pallas_call(
        body, name=name, grid=(s // ts,), in_specs=in_specs,
        out_specs=pl.BlockSpec((ts, d), lambda i: (i, 0)),
        out_shape=jax.ShapeDtypeStruct((s, d), out_dtype),
        compiler_params=_cp("parallel"))(*args)


def _mm_res_ln(a, w, x, gain, bias, fscale, name):
    nc, s, kc = a.shape
    d = w.shape[2]
    ts = _rows(s)

    def body(a_ref, w_ref, x_ref, g_ref, b_ref, y_ref, yb_ref, xh_ref, r_ref):
        f = _dot(a_ref[0], w_ref[0])
        for j in range(1, nc):
            f = f + _dot(a_ref[j], w_ref[j])
        z = ALPHA * x_ref[...] + fscale * f
        mu = jnp.mean(z, axis=-1, keepdims=True)
        zc = z - mu
        var = jnp.mean(zc * zc, axis=-1, keepdims=True)
        r = lax.rsqrt(var + LN_EPS)
        xh = zc * r
        y = xh * g_ref[...] + b_ref[...]
        y_ref[...] = y
        yb_ref[...] = y.astype(BF)
        xh_ref[...] = xh
        r_ref[...] = r

    row = pl.BlockSpec((ts, d), lambda i: (i, 0))
    vec = pl.BlockSpec((1, d), lambda i: (0, 0))
    return pl.pallas_call(
        body, name=name, grid=(s // ts,),
        in_specs=[pl.BlockSpec((nc, ts, kc), lambda i: (0, i, 0)), _resident((nc, kc, d), lambda i: (0, 0, 0)),
                  row, vec, vec],
        out_specs=[row, row, row, pl.BlockSpec((ts, 1), lambda i: (i, 0))],
        out_shape=[jax.ShapeDtypeStruct((s, d), F32), jax.ShapeDtypeStruct((s, d), BF),
                   jax.ShapeDtypeStruct((s, d), F32), jax.ShapeDtypeStruct((s, 1), F32)],
        compiler_params=_cp("parallel"))(a, w, x, gain, bias)


def _ln_bwd(dy, xh, rstd, gain, fscale, name, after=()):
    s, d = dy.shape
    ts = _rows(s)
    na = len(after)

    def body(*refs):
        dy_ref, xh_ref, r_ref, g_ref = refs[:4]
        dz_ref, dzb_ref, dg_ref, db_ref = refs[4 + na:]
        i = pl.program_id(0)
        dyv = dy_ref[...]
        xhv = xh_ref[...]
        dxh = dyv * g_ref[...]
        m1 = jnp.mean(dxh, axis=-1, keepdims=True)
        m2 = jnp.mean(dxh * xhv, axis=-1, keepdims=True)
        dz = r_ref[...] * (dxh - m1 - xhv * m2)
        dz_ref[...] = dz
        dzb_ref[...] = (fscale * dz).astype(BF)

        @pl.when(i == 0)
        def _():
            dg_ref[...] = jnp.zeros_like(dg_ref)
            db_ref[...] = jnp.zeros_like(db_ref)

        dg_ref[...] += jnp.sum(dyv * xhv, axis=0, keepdims=True)
        db_ref[...] += jnp.sum(dyv, axis=0, keepdims=True)

    row = pl.BlockSpec((ts, d), lambda i: (i, 0))
    vec = pl.BlockSpec((1, d), lambda i: (0, 0))
    return pl.pallas_call(
        body, name=name, grid=(s // ts,),
        in_specs=[row, row, pl.BlockSpec((ts, 1), lambda i: (i, 0)), vec] + [pl.BlockSpec(memory_space=pl.ANY)] * na,
        out_specs=[row, row, vec, vec],
        out_shape=[jax.ShapeDtypeStruct((s, d), F32), jax.ShapeDtypeStruct((s, d), BF),
                   jax.ShapeDtypeStruct((1, d), F32), jax.ShapeDtypeStruct((1, d), F32)],
        compiler_params=_cp("arbitrary"))(dy, xh, rstd, gain, *after)


def _ffn_up(xb, wgu, name):
    s, d = xb.shape
    c = wgu.shape[1]
    nch = wgu.shape[0] // 2
    ts = _rows(s, 1024)
    w4 = wgu.reshape(2, nch, c, d)

    def body(x_ref, w_ref, gu_ref, a_ref):
        x = x_ref[...]
        g = _dot_nt(x, w_ref[0])
        u = _dot_nt(x, w_ref[1])
        sg = _sigmoid(g)
        t = g * sg
        gu_ref[0] = (u * (sg * (1.0 + g - t))).astype(BF)
        gu_ref[1] = t.astype(BF)
        a_ref[...] = (t * u).astype(BF)

    return pl.pallas_call(
        body, name=name, grid=(nch, s // ts),
        in_specs=[pl.BlockSpec((ts, d), lambda j, i: (i, 0)),
                  pl.BlockSpec((2, None, c, d), lambda j, i: (0, j, 0, 0))],
        out_specs=[pl.BlockSpec((2, None, ts, c), lambda j, i: (0, j, i, 0)),
                   pl.BlockSpec((None, ts, c), lambda j, i: (j, i, 0))],
        out_shape=[jax.ShapeDtypeStruct((2, nch, s, c), BF), jax.ShapeDtypeStruct((nch, s, c), BF)],
        compiler_params=_cp("parallel", "parallel"))(xb, w4)


def _ffn_fwd_main(x, xb, wgu, wd4, gain, bias, name):
    s, d = x.shape
    nch, c = wd4.shape[0], wd4.shape[1]
    ts = _rows(s, 256)

    def body(x_ref, xb_ref, wgu_ref, wd_ref, g_ref, b_ref, y_ref, yb_ref, gu_ref, a_ref, xh_ref, r_ref):
        xbv = xb_ref[...]
        f = jnp.zeros((ts, d), F32)
        for j in range(nch):
            g = _dot_nt(xbv, wgu_ref[j])
            u = _dot_nt(xbv, wgu_ref[nch + j])
            sg = _sigmoid(g)
            t = g * sg
            gu_ref[0, j] = (u * (sg * (1.0 + g - t))).astype(BF)
            gu_ref[1, j] = t.astype(BF)
            act = (t * u).astype(BF)
            a_ref[j] = act
            f = f + _dot(act, wd_ref[j])
        z = ALPHA * x_ref[...] + 0.5 * f
        mu = jnp.mean(z, axis=-1, keepdims=True)
        zc = z - mu
        var = jnp.mean(zc * zc, axis=-1, keepdims=True)
        r = lax.rsqrt(var + LN_EPS)
        xh = zc * r
        y = xh * g_ref[...] + b_ref[...]
        y_ref[...] = y
        yb_ref[...] = y.astype(BF)
        xh_ref[...] = xh
        r_ref[...] = r

    row = pl.BlockSpec((ts, d), lambda i: (i, 0))
    vec = pl.BlockSpec((1, d), lambda i: (0, 0))
    return pl.pallas_call(
        body, name=name, grid=(s // ts,),
        in_specs=[row, row, _resident(wgu.shape, lambda i: (0, 0, 0)), _resident(wd4.shape, lambda i: (0, 0, 0)),
                  vec, vec],
        out_specs=[row, row, pl.BlockSpec((2, nch, ts, c), lambda i: (0, 0, i, 0)),
                   pl.BlockSpec((nch, ts, c), lambda i: (0, i, 0)), row, pl.BlockSpec((ts, 1), lambda i: (i, 0))],
        out_shape=[jax.ShapeDtypeStruct((s, d), F32), jax.ShapeDtypeStruct((s, d), BF),
                   jax.ShapeDtypeStruct((2, nch, s, c), BF), jax.ShapeDtypeStruct((nch, s, c), BF),
                   jax.ShapeDtypeStruct((s, d), F32), jax.ShapeDtypeStruct((s, 1), F32)],
        compiler_params=_cp("parallel"))(x, xb, wgu, wd4, gain, bias)


def _ffn_bwd_main(dy, xh, rstd, gain, wd4, wgu, gu, name, after=(), with_dx=True):
    s, d = dy.shape
    nch, c = wd4.shape[0], wd4.shape[1]
    ts = _rows(s, 256)
    na = len(after)

    def body(*refs):
        dy_ref, xh_ref, r_ref, g_ref, wd_ref, wgu_ref, gu_ref = refs[:7]
        dx_ref, dzb_ref, dh_ref, dg_ref, db_ref = refs[7 + na:]
        i = pl.program_id(0)
        dyv = dy_ref[...]
        xhv = xh_ref[...]
        dxh = dyv * g_ref[...]
        m1 = jnp.mean(dxh, axis=-1, keepdims=True)
        m2 = jnp.mean(dxh * xhv, axis=-1, keepdims=True)
        dz = r_ref[...] * (dxh - m1 - xhv * m2)
        dzb = (0.5 * dz).astype(BF)
        dzb_ref[...] = dzb

        @pl.when(i == 0)
        def _():
            dg_ref[...] = jnp.zeros_like(dg_ref)
            db_ref[...] = jnp.zeros_like(db_ref)

        dg_ref[...] += jnp.sum(dyv * xhv, axis=0, keepdims=True)
        db_ref[...] += jnp.sum(dyv, axis=0, keepdims=True)

        dx = ALPHA * dz if with_dx else dz
        for j in range(nch):
            da = _dot_nt(dzb, wd_ref[j])
            dgate = (da * gu_ref[0, j].astype(F32)).astype(BF)
            dup = (da * gu_ref[1, j].astype(F32)).astype(BF)
            dh_ref[0, j] = dgate
            dh_ref[1, j] = dup
            if with_dx:
                dx = dx + _dot(dgate, wgu_ref[j]) + _dot(dup, wgu_ref[nch + j])
        dx_ref[...] = dx

    row = pl.BlockSpec((ts, d), lambda i: (i, 0))
    vec = pl.BlockSpec((1, d), lambda i: (0, 0))
    act = pl.BlockSpec((2, nch, ts, c), lambda i: (0, 0, i, 0))
    return pl.pallas_call(
        body, name=name, grid=(s // ts,),
        in_specs=[row, row, pl.BlockSpec((ts, 1), lambda i: (i, 0)), vec,
                  _resident(wd4.shape, lambda i: (0, 0, 0)), _resident(wgu.shape, lambda i: (0, 0, 0)), act]
                 + [pl.BlockSpec(memory_space=pl.ANY)] * na,
        out_specs=[row, row, act, vec, vec],
        out_shape=[jax.ShapeDtypeStruct((s, d), F32), jax.ShapeDtypeStruct((s, d), BF),
                   jax.ShapeDtypeStruct((2, nch, s, c), BF),
                   jax.ShapeDtypeStruct((1, d), F32), jax.ShapeDtypeStruct((1, d), F32)],
        compiler_params=_cp("arbitrary"))(dy, xh, rstd, gain, wd4, wgu, gu, *after)


def _rope_tables(s, sign):
    pos = jnp.arange(s, dtype=F32)
    inv_freq = 1.0 / (ROPE_THETA ** (jnp.arange(ROT_HALF, dtype=F32) / ROT_HALF))
    ang = pos[:, None] * inv_freq[None, :]
    cos, sin = jnp.cos(ang), jnp.sin(ang) * sign
    one = jnp.ones((s, HEAD_DIM - 2 * ROT_HALF), F32)
    zero = jnp.zeros((s, HEAD_DIM - 2 * ROT_HALF), F32)
    zh = jnp.zeros((s, ROT_HALF), F32)
    cos_f = jnp.concatenate([cos, cos, one], axis=1)
    sin_a = jnp.concatenate([-sin, zh, zero], axis=1)
    sin_b = jnp.concatenate([zh, sin, zero], axis=1)
    rep = LANES // HEAD_DIM
    return tuple(jnp.tile(t, (1, rep)) for t in (cos_f, sin_a, sin_b))


def _rope(t, c_ref, sa_ref, sb_ref):
    return (t * c_ref[...] + pltpu.roll(t, LANES - ROT_HALF, 1) * sa_ref[...]
            + pltpu.roll(t, ROT_HALF, 1) * sb_ref[...])


def _proj_rope(xb, w, tabs, n_rope, name, w_rows_out, tail_block=None):
    s, d = xb.shape
    n = w.shape[0] if w_rows_out else w.shape[1]
    tm = _rows(s, 256)
    has_tail = tail_block is not None

    def body(x_ref, w_ref, c_ref, sa_ref, sb_ref, o_ref, *tail_ref):
        h = (_dot_nt if w_rows_out else _dot)(x_ref[...], w_ref[...])
        for cb in range(n // LANES):
            t = h[:, cb * LANES:(cb + 1) * LANES]
            if cb < n_rope:
                t = _rope(t, c_ref, sa_ref, sb_ref)
            o_ref[:, cb * LANES:(cb + 1) * LANES] = t.astype(BF)
        if has_tail:
            tail_ref[0][...] = h[:, tail_block * LANES:(tail_block + 1) * LANES]

    tab = pl.BlockSpec((tm, LANES), lambda i: (i, 0))
    out_specs = [pl.BlockSpec((tm, n), lambda i: (i, 0))]
    out_shape = [jax.ShapeDtypeStruct((s, n), BF)]
    if has_tail:
        out_specs.append(tab)
        out_shape.append(jax.ShapeDtypeStruct((s, LANES), F32))
    res = pl.pallas_call(
        body, name=name, grid=(s // tm,),
        in_specs=[pl.BlockSpec((tm, d), lambda i: (i, 0)), _resident(w.shape, lambda i: (0, 0)), tab, tab, tab],
        out_specs=out_specs, out_shape=out_shape, compiler_params=_cp("parallel"))(xb, w, *tabs)
    return res if has_tail else res[0]


def _rope_cast(parts, tabs, n_rope, name, transposed=()):
    s = tabs[0].shape[0]
    flip = [i in transposed for i in range(len(parts))]
    widths = [p.shape[0] if f else p.shape[1] for p, f in zip(parts, flip)]
    n = sum(widths)
    npart = len(parts)
    ts = _rows(s, 256)

    def body(*refs):
        part_refs = refs[:npart]
        c_ref, sa_ref, sb_ref, o_ref = refs[npart:]
        col = 0
        for ref, w, f in zip(part_refs, widths, flip):
            for j in range(w // LANES):
                if f:
                    t = jnp.transpose(ref[j * LANES:(j + 1) * LANES, :])
                else:
                    t = ref[:, j * LANES:(j + 1) * LANES]
                if col < n_rope:
                    t = _rope(t, c_ref, sa_ref, sb_ref)
                o_ref[:, col * LANES:(col + 1) * LANES] = t.astype(BF)
                col += 1

    tab = pl.BlockSpec((ts, LANES), lambda i: (i, 0))
    return pl.pallas_call(
        body, name=name, grid=(s // ts,),
        in_specs=[pl.BlockSpec((w, ts), lambda i: (0, i)) if f else pl.BlockSpec((ts, w), lambda i: (i, 0))
                  for w, f in zip(widths, flip)] + [tab, tab, tab],
        out_specs=pl.BlockSpec((ts, n), lambda i: (i, 0)),
        out_shape=jax.ShapeDtypeStruct((s, n), BF),
        compiler_params=_cp("parallel"))(*parts, *tabs)


def _head_masks():
    lane = lax.broadcasted_iota(jnp.int32, (1, LANES), 1)
    return [lane < HEAD_DIM, lane >= HEAD_DIM]


def _sel(mask, v):
    return jnp.where(mask, v, jnp.zeros_like(v))


def _pick(mask, wide, fill):
    return jnp.max(jnp.where(mask, wide, fill), axis=1, keepdims=True)


def _head_stack(hm, a, b):
    return jnp.concatenate([_sel(hm[0], a), _sel(hm[0], b), _sel(hm[1], a), _sel(hm[1], b)], axis=0)


def _band_mask_stack(has_prev):
    qi = lax.broadcasted_iota(jnp.int32, (BLOCK, 4 * BLOCK), 0)
    col = lax.broadcasted_iota(jnp.int32, (BLOCK, 4 * BLOCK), 1)
    d = jnp.bitwise_and(col, BLOCK - 1) - qi
    is_prev = jnp.bitwise_and(col, BLOCK) != 0
    return jnp.where(is_prev, d - jnp.where(has_prev, 0, BLOCK), -d) >= 0


class _BandView:
    def __init__(self, s, g):
        self.r = 4 ** g
        self.nl = s // self.r
        self.nblk = self.nl // BLOCK
        self.nsub = min(BAND_SUB, self.nblk)
        self.tile = self.nsub * BLOCK
        self.ncols = min(self.r * GROUP_W // LANES, BAND_COLS)
        self.grid = (self.r * GROUP_W // LANES // self.ncols, self.nblk // self.nsub)

    def view(self, a):
        return a.reshape(self.nl, self.r * a.shape[1])

    def qkv(self, hb, g):
        npair = MIX_W // LANES
        offs = [i * npair + g * GROUP_W // LANES for i in range(3)]
        if self.r == 1:
            return [hb] * 3, hb.shape[1], offs
        return [self.view(hb[:, o * LANES:o * LANES + GROUP_W]) for o in offs], GROUP_W, [0, 0, 0]

    def specs(self, width, off):
        assert off % self.ncols == 0 and (width == GROUP_W or self.r == 1)
        nsub, last, lanes, first = self.nsub, self.nblk - 1, self.ncols * LANES, off // self.ncols
        return (pl.BlockSpec((self.tile, lanes), lambda cg, t: (t, first + cg)),
                pl.BlockSpec((BLOCK, lanes), lambda cg, t: (jnp.maximum(t * nsub - 1, 0), first + cg)),
                pl.BlockSpec((BLOCK, lanes), lambda cg, t: (jnp.minimum(t * nsub + nsub, last), first + cg)))


def _band_fwd(hb, g, name):
    s, n = hb.shape
    bv = _BandView(s, g)
    nsub = bv.nsub
    npair = MIX_W // LANES

    def body(q_ref, kc_ref, kp_ref, vc_ref, vp_ref, o_ref, l_ref):
        t = pl.program_id(1)
        hm = _head_masks()
        for i, c in [(i, c) for i in range(nsub) for c in range(bv.ncols)]:
            rows = slice(i * BLOCK, (i + 1) * BLOCK)
            lanes = slice(c * LANES, (c + 1) * LANES)
            has_prev = t > 0 if i == 0 else True
            q, kc, vc = q_ref[rows, lanes], kc_ref[rows, lanes], vc_ref[rows, lanes]
            if i == 0:
                kp, vp = kp_ref[:, lanes], vp_ref[:, lanes]
            else:
                prev = slice((i - 1) * BLOCK, i * BLOCK)
                kp, vp = kc_ref[prev, lanes], vc_ref[prev, lanes]
            sc = jnp.where(_band_mask_stack(has_prev), _dot_nt(q, _head_stack(hm, kc, kp)) * SCALE, NEG)
            ps, ms, ls = [], [], []
            for h in range(2):
                sh = sc[:, 2 * h * BLOCK:2 * (h + 1) * BLOCK]
                m = jnp.max(sh, axis=1, keepdims=True)
                p = jnp.exp(sh - m)
                ps.append(p.astype(BF))
                ms.append(m)
                ls.append(jnp.sum(p, axis=1, keepdims=True))
            o = _dot(jnp.concatenate(ps, axis=1), _head_stack(hm, vc, vp))
            o_ref[rows, lanes] = o / jnp.where(hm[0], ls[0], ls[1])
            l_ref[rows, lanes] = jnp.where(hm[0], ms[0] + jnp.log(ls[0]), ms[1] + jnp.log(ls[1]))

    (qv, kv_, vv), width, (qo, ko, vo) = bv.qkv(hb, g)
    q_cur, _, _ = bv.specs(width, qo)
    k_cur, k_prv, _ = bv.specs(width, ko)
    v_cur, v_prv, _ = bv.specs(width, vo)
    out_spec = bv.specs(GROUP_W, 0)[0]
    out = jax.ShapeDtypeStruct((bv.nl, bv.r * GROUP_W), F32)
    o, l = pl.pallas_call(
        body, name=name, grid=bv.grid,
        in_specs=[q_cur, k_cur, k_prv, v_cur, v_prv], out_specs=[out_spec, out_spec], out_shape=[out, out],
        compiler_params=_cp("parallel", "parallel"))(qv, kv_, kv_, vv, vv)
    return o.reshape(s, GROUP_W), l.reshape(s, GROUP_W)


def _band_combine(os, ls, name):
    ng = len(os)
    s, w = os[0].shape
    ts = _rows(s)

    def body(*refs):
        o_refs, l_refs = refs[:ng], refs[ng:2 * ng]
        oa_ref, lt_ref = refs[2 * ng:]
        lv = [r[...] for r in l_refs]
        m = functools.reduce(jnp.maximum, lv)
        es = [jnp.exp(l - m) for l in lv]
        den = functools.reduce(lambda a, b: a + b, es)
        num = functools.reduce(lambda a, b: a + b, [es[g] * o_refs[g][...] for g in range(ng)])
        oa_ref[...] = (num / den).astype(BF)
        lt_ref[...] = m + jnp.log(den)

    blk = pl.BlockSpec((ts, w), lambda i: (i, 0))
    return pl.pallas_call(
        body, name=name, grid=(s // ts,), in_specs=[blk] * (2 * ng), out_specs=[blk, blk],
        out_shape=[jax.ShapeDtypeStruct((s, w), BF), jax.ShapeDtypeStruct((s, w), F32)],
        compiler_params=_cp("parallel"))(*os, *ls)


def _band_bwd(hb, dcat, oa, lt, g, name):
    s, n = hb.shape
    bv = _BandView(s, g)
    nsub = bv.nsub
    npair = MIX_W // LANES
    ntile = bv.grid[1]

    def body(q_ref, qn_ref, kc_ref, kp_ref, vc_ref, vp_ref, do_ref, don_ref, oa_ref, oan_ref, lt_ref, ltn_ref,
             dq_ref, dk_ref, dv_ref):
        t = pl.program_id(1)
        hm = _head_masks()

        for i, c in [(i, c) for i in range(nsub) for c in range(bv.ncols)]:
            lanes = slice(c * LANES, (c + 1) * LANES)

            def block(ref, edge_ref, i, lanes=lanes):
                if i < 0 or i >= nsub:
                    return edge_ref[:, lanes]
                return ref[i * BLOCK:(i + 1) * BLOCK, lanes]

            has_prev = t > 0 if i == 0 else True
            has_next = t < ntile - 1 if i == nsub - 1 else True
            q, qn = block(q_ref, None, i), block(q_ref, qn_ref, i + 1)
            kc, kp = block(kc_ref, None, i), block(kc_ref, kp_ref, i - 1)
            vc, vp = block(vc_ref, None, i), block(vc_ref, vp_ref, i - 1)
            do, don = block(do_ref, None, i), block(do_ref, don_ref, i + 1)
            dd = do.astype(F32) * block(oa_ref, None, i).astype(F32)
            ddn = don.astype(F32) * block(oa_ref, oan_ref, i + 1).astype(F32)
            lt, ltn = block(lt_ref, None, i), block(lt_ref, ltn_ref, i + 1)

            def per_head(wide, width):
                col = lax.broadcasted_iota(jnp.int32, (BLOCK, 2 * width), 1)
                return jnp.where(col < width, _pick(hm[0], wide, NEG), _pick(hm[1], wide, NEG))

            def row_sums(prod, width):
                col = lax.broadcasted_iota(jnp.int32, (BLOCK, 2 * width), 1)
                return jnp.where(col < width, jnp.sum(_sel(hm[0], prod), axis=1, keepdims=True),
                                 jnp.sum(_sel(hm[1], prod), axis=1, keepdims=True))

            kst, vst = _head_stack(hm, kc, kp), _head_stack(hm, vc, vp)
            p = jnp.exp(jnp.where(_band_mask_stack(has_prev), _dot_nt(q, kst) * SCALE, NEG)
                        - per_head(lt, 2 * BLOCK))
            ds = p * (_dot_nt(do, vst) - row_sums(dd, 2 * BLOCK))
            dq_ref[i * BLOCK:(i + 1) * BLOCK, lanes] = SCALE * _dot(ds.astype(BF), kst)
            kcs = jnp.concatenate([_sel(hm[0], kc), _sel(hm[1], kc)], axis=0)
            vcs = jnp.concatenate([_sel(hm[0], vc), _sel(hm[1], vc)], axis=0)
            qi_ = lax.broadcasted_iota(jnp.int32, (BLOCK, 2 * BLOCK), 0)
            kj_ = jnp.bitwise_and(lax.broadcasted_iota(jnp.int32, (BLOCK, 2 * BLOCK), 1), BLOCK - 1)
            mn = kj_ >= qi_ + jnp.where(has_next, 0, BLOCK)
            pn = jnp.exp(jnp.where(mn, _dot_nt(qn, kcs) * SCALE, NEG) - per_head(ltn, BLOCK))
            dsn = pn * (_dot_nt(don, vcs) - row_sums(ddn, BLOCK))
            pb, dsb, pnb, dsnb = p.astype(BF), ds.astype(BF), pn.astype(BF), dsn.astype(BF)

            def own(x, h):
                return x[:, 2 * h * BLOCK:(2 * h + 1) * BLOCK]

            def nxt(x, h):
                return x[:, h * BLOCK:(h + 1) * BLOCK]

            ds_rows = jnp.concatenate([own(dsb, 0), nxt(dsnb, 0), own(dsb, 1), nxt(dsnb, 1)], axis=0)
            p_rows = jnp.concatenate([own(pb, 0), nxt(pnb, 0), own(pb, 1), nxt(pnb, 1)], axis=0)
            dk_ref[i * BLOCK:(i + 1) * BLOCK, lanes] = SCALE * _dot_tn(ds_rows, _head_stack(hm, q, qn))
            dv_ref[i * BLOCK:(i + 1) * BLOCK, lanes] = _dot_tn(p_rows, _head_stack(hm, do, don))

    (qv, kv_, vv), width, (qo, ko, vo) = bv.qkv(hb, g)
    q_cur, _, q_nxt = bv.specs(width, qo)
    k_cur, k_prv, _ = bv.specs(width, ko)
    v_cur, v_prv, _ = bv.specs(width, vo)
    w_cur, _, w_nxt = bv.specs(GROUP_W, 0)
    out = jax.ShapeDtypeStruct((bv.nl, bv.r * GROUP_W), F32)
    dv_, ov, lv = bv.view(dcat[:, :GROUP_W]), bv.view(oa), bv.view(lt)
    res = pl.pallas_call(
        body, name=name, grid=bv.grid,
        in_specs=[q_cur, q_nxt, k_cur, k_prv, v_cur, v_prv, w_cur, w_nxt, w_cur, w_nxt, w_cur, w_nxt],
        out_specs=[w_cur, w_cur, w_cur], out_shape=[out, out, out],
        compiler_params=_cp("parallel", "parallel"))(
            qv, qv, kv_, kv_, vv, vv, dv_, dv_, ov, ov, lv, lv)
    return [t.reshape(s, GROUP_W) for t in res]


def _mem_fwd(hb, q_blk0, kv, name):
    s = hb.shape[0]
    m = kv.shape[0]
    tq = _rows(s)
    npair = MEM_W // LANES

    def body(q_ref, k_ref, v_ref, o_ref, l_ref):
        q, k, v = q_ref[...], k_ref[...], v_ref[...]
        hm = _head_masks()
        o = jnp.zeros((tq, LANES), F32)
        lse_w = jnp.zeros((tq, LANES), F32)
        for h in range(2):
            sc = _dot_nt(_sel(hm[h], q), k) * SCALE
            mx = jnp.max(sc, axis=1, keepdims=True)
            p = jnp.exp(sc - mx)
            l = jnp.sum(p, axis=1, keepdims=True)
            o = o + _dot(p.astype(BF), _sel(hm[h], v)) / l
            lse_w = jnp.where(hm[h], mx + jnp.log(l), lse_w)
        o_ref[...] = o.astype(BF)
        l_ref[...] = lse_w

    blk = pl.BlockSpec((tq, LANES), lambda p, i: (i, p))
    return pl.pallas_call(
        body, name=name, grid=(npair, s // tq),
        in_specs=[pl.BlockSpec((tq, LANES), lambda p, i: (i, q_blk0 + p)),
                  pl.BlockSpec((m, LANES), lambda p, i: (0, p)),
                  pl.BlockSpec((m, LANES), lambda p, i: (0, npair + p))],
        out_specs=[blk, blk],
        out_shape=[jax.ShapeDtypeStruct((s, MEM_W), BF), jax.ShapeDtypeStruct((s, MEM_W), F32)],
        compiler_params=_cp("parallel", "parallel"))(hb, kv, kv)


def _mem_bwd(hb, q_blk0, kv, dcat, cat, o_blk0, lse, name):
    s = hb.shape[0]
    m = kv.shape[0]
    tq = _rows(s)
    npair = MEM_W // LANES

    def body(q_ref, k_ref, v_ref, do_ref, o_ref, l_ref, dq_ref, dk_ref, dv_ref):
        i = pl.program_id(1)

        @pl.when(i == 0)
        def _():
            dk_ref[...] = jnp.zeros_like(dk_ref)
            dv_ref[...] = jnp.zeros_like(dv_ref)

        q, k, v, do = q_ref[...], k_ref[...], v_ref[...], do_ref[...]
        dd = do.astype(F32) * o_ref[...].astype(F32)
        lt = l_ref[...]
        hm = _head_masks()
        dq = jnp.zeros((tq, LANES), F32)
        dk = jnp.zeros((m, LANES), F32)
        dv = jnp.zeros((m, LANES), F32)
        for h in range(2):
            qh, doh = _sel(hm[h], q), _sel(hm[h], do)
            p = jnp.exp(_dot_nt(qh, k) * SCALE - _pick(hm[h], lt, NEG))
            ds = p * (_dot_nt(doh, v) - jnp.sum(_sel(hm[h], dd), axis=1, keepdims=True))
            dq = dq + SCALE * _dot(ds.astype(BF), _sel(hm[h], k))
            dk = dk + SCALE * _dot_tn(ds.astype(BF), qh)
            dv = dv + _dot_tn(p.astype(BF), doh)
        dq_ref[...] = dq
        dk_ref[...] += dk
        dv_ref[...] += dv

    row = pl.BlockSpec((tq, LANES), lambda p, i: (i, p))
    orow = pl.BlockSpec((tq, LANES), lambda p, i: (i, o_blk0 + p))
    acc = pl.BlockSpec((m, LANES), lambda p, i: (0, p))
    return pl.pallas_call(
        body, name=name, grid=(npair, s // tq),
        in_specs=[pl.BlockSpec((tq, LANES), lambda p, i: (i, q_blk0 + p)),
                  pl.BlockSpec((m, LANES), lambda p, i: (0, p)),
                  pl.BlockSpec((m, LANES), lambda p, i: (0, npair + p)), orow, orow, row],
        out_specs=[row, acc, acc],
        out_shape=[jax.ShapeDtypeStruct((s, MEM_W), F32), jax.ShapeDtypeStruct((m, MEM_W), F32),
                   jax.ShapeDtypeStruct((m, MEM_W), F32)],
        compiler_params=_cp("parallel", "arbitrary"))(hb, kv, kv, dcat, cat, lse)


def _gate_fwd(f_t, bias, name):
    hp, s = f_t.shape
    nblk = s // LANES

    def body(f_ref, b_ref, c_ref):
        lane = lax.broadcasted_iota(jnp.int32, (hp, LANES), 1)

        def step(i, carry):
            off = pl.multiple_of(i * LANES, LANES)
            x = f_ref[:, pl.ds(off, LANES)] + b_ref[...]
            acc = jnp.minimum(x, 0.0) - jnp.log(1.0 + jnp.exp(-jnp.abs(x)))
            sh = 1
            while sh < LANES:
                acc = acc + jnp.where(lane >= sh, pltpu.roll(acc, sh, 1), 0.0)
                sh *= 2
            acc = acc + carry
            c_ref[:, pl.ds(off, LANES)] = acc
            return acc[:, LANES - 1:LANES]

        lax.fori_loop(0, nblk, step, jnp.zeros((hp, 1), F32))

    vm = pl.BlockSpec(memory_space=pltpu.VMEM)
    return pl.pallas_call(body, name=name, in_specs=[vm, vm], out_specs=vm,
                          out_shape=jax.ShapeDtypeStruct((hp, s), F32),
                          compiler_params=pltpu.CompilerParams(vmem_limit_bytes=VMEM_LIMIT))(f_t, bias)


def _gate_bwd(dc_t, f_t, bias, name):
    hp, s = f_t.shape
    nblk = s // LANES

    def body(dc_ref, f_ref, b_ref, df_ref, db_ref):
        lane = lax.broadcasted_iota(jnp.int32, (hp, LANES), 1)

        def step(t, carry):
            suffix, dbias = carry
            off = pl.multiple_of((nblk - 1 - t) * LANES, LANES)
            acc = dc_ref[:, pl.ds(off, LANES)]
            sh = 1
            while sh < LANES:
                acc = acc + jnp.where(lane < LANES - sh, pltpu.roll(acc, LANES - sh, 1), 0.0)
                sh *= 2
            acc = acc + suffix
            x = f_ref[:, pl.ds(off, LANES)] + b_ref[...]
            df = acc * _sigmoid(-x)
            df_ref[:, pl.ds(off, LANES)] = df
            return acc[:, 0:1], dbias + jnp.sum(df, axis=1, keepdims=True)

        _, dbias = lax.fori_loop(0, nblk, step, (jnp.zeros((hp, 1), F32), jnp.zeros((hp, 1), F32)))
        db_ref[...] = dbias

    vm = pl.BlockSpec(memory_space=pltpu.VMEM)
    return pl.pallas_call(body, name=name, in_specs=[vm, vm, vm], out_specs=[vm, vm],
                          out_shape=[jax.ShapeDtypeStruct((hp, s), F32), jax.ShapeDtypeStruct((hp, 1), F32)],
                          compiler_params=pltpu.CompilerParams(vmem_limit_bytes=VMEM_LIMIT))(dc_t, f_t, bias)


def _wide(rep, width):
    return jnp.tile(rep, (1, width // LANES))


def _fold(t):
    part = t[:, :LANES]
    for c in range(1, t.shape[1] // LANES):
        part = part + t[:, c * LANES:(c + 1) * LANES]
    return part


def _foxt_logits(q, k, cq_row, ck_rep, mask, hmask):
    s = _dot_nt(_sel(hmask, k), q) + (cq_row - _wide(ck_rep, q.shape[0]))
    if mask is not None:
        s = jnp.where(mask, s, NEG)
    return s


def _causal_sub(ks, qs):
    shape = (ks.stop - ks.start, qs.stop - qs.start)
    return (ks.start + lax.broadcasted_iota(jnp.int32, shape, 0)
            <= qs.start + lax.broadcasted_iota(jnp.int32, shape, 1))


def _diag_blocks(t):
    h = t // 2
    return [(slice(0, h), slice(0, t)), (slice(h, t), slice(h, t))]


FOX_SPLIT = 1


def _fox_tiles(s):
    tq = _rows(s, 1024)
    return tq, tq // FOX_SPLIT, s // tq


def _fox_steps(nq):
    return FOX_SPLIT * nq * (nq + 1) // 2


def _count_ge(t, bounds):
    return sum([(t >= b).astype(jnp.int32) for b in bounds], jnp.int32(0))


def _sweep_q_major(t, nq):
    qi = _count_ge(t, [FOX_SPLIT * r * (r + 1) // 2 for r in range(1, nq)])
    return qi, t - FOX_SPLIT * qi * (qi + 1) // 2


def _sweep_k_major(t, nq):
    counts = [nq - j // FOX_SPLIT for j in range(FOX_SPLIT * nq)]
    offs = [sum(counts[:j]) for j in range(1, FOX_SPLIT * nq)]
    kj = _count_ge(t, offs)
    start = sum([jnp.where(t >= o, c, 0) for o, c in zip(offs, counts)], jnp.int32(0))
    qi = kj // FOX_SPLIT + (t - start)
    return kj, qi, t == start, qi == nq - 1


def _foxt_fwd(hb, c_rep, c_t3, name):
    s = hb.shape[0]
    npair = MIX_W // LANES
    tq, tk, nq = _fox_tiles(s)

    def body(q_ref, k_ref, v_ref, cq_ref, ck_ref, o_ref, l_ref, m_s, l_s, acc):
        qi, kj = _sweep_q_major(pl.program_id(1), nq)
        hm = _head_masks()

        @pl.when(kj == 0)
        def _():
            m_s[...] = jnp.full_like(m_s, NEG)
            l_s[...] = jnp.zeros_like(l_s)
            acc[...] = jnp.zeros_like(acc)

        def step(ks, qs, masked):
            q, k = q_ref[qs, :] * SCALE, k_ref[ks, :]
            vt = jnp.transpose(v_ref[ks, :])
            cq = cq_ref[:, qs]
            mask = _causal_sub(ks, qs) if masked else None
            for h in range(2):
                st = _foxt_logits(q, k, cq[h:h + 1, :], ck_ref[h, ks, :], mask, hm[h])
                m_old = m_s[h, :, qs]
                m_new = jnp.maximum(m_old, jnp.max(st, axis=0, keepdims=True))
                pt = jnp.exp(st - m_new)
                corr = jnp.exp(m_old - m_new)
                l_s[h, :, qs] = l_s[h, :, qs] * corr + jnp.sum(pt, axis=0, keepdims=True)
                acc[h, :, qs] = acc[h, :, qs] * corr + _dot(vt[h * HEAD_DIM:(h + 1) * HEAD_DIM, :], pt.astype(BF))
                m_s[h, :, qs] = m_new

        @pl.when(kj < qi)
        def _():
            step(slice(0, tk), slice(0, tq), False)

        @pl.when(kj == qi)
        def _():
            for ks, qs in _diag_blocks(tq):
                step(ks, qs, True)
            outs = []
            for h in range(2):
                outs.append(acc[h] / l_s[h])
                l_ref[h:h + 1, :] = m_s[h] + jnp.log(l_s[h])
            o_ref[...] = jnp.transpose(jnp.concatenate(outs, axis=0)).astype(BF)

    def q_map(p, t):
        return (_sweep_q_major(t, nq)[0], p)

    def kv_map(off):
        return lambda p, t: (_sweep_q_major(t, nq)[1], off + p)

    blk = pl.BlockSpec((tq, LANES), q_map)
    row = pl.BlockSpec((None, 2, tq), lambda p, t: (p, 0, _sweep_q_major(t, nq)[0]))
    return pl.pallas_call(
        body, name=name, grid=(npair, _fox_steps(nq)),
        in_specs=[blk, pl.BlockSpec((tk, LANES), kv_map(npair)), pl.BlockSpec((tk, LANES), kv_map(2 * npair)), row,
                  pl.BlockSpec((2, tk, LANES), lambda p, t: (p, _sweep_q_major(t, nq)[1], 0))],
        out_specs=[blk, row],
        out_shape=[jax.ShapeDtypeStruct((s, MIX_W), BF), jax.ShapeDtypeStruct((npair, 2, s), F32)],
        scratch_shapes=[pltpu.VMEM((2, 1, tq), F32), pltpu.VMEM((2, 1, tq), F32),
                        pltpu.VMEM((2, HEAD_DIM, tq), F32)],
        compiler_params=_cp("parallel", "arbitrary"))(hb, hb, hb, c_t3, c_rep)


def _foxt_dsum(hb, dcat, lse, c_rep, c_t3, name):
    s = hb.shape[0]
    npair = MIX_W // LANES
    tq, tk, nq = _fox_tiles(s)

    def body(q_ref, k_ref, v_ref, do_ref, l_ref, cq_ref, ck_ref, d_ref, acc):
        qi, kj = _sweep_q_major(pl.program_id(1), nq)
        hm = _head_masks()

        @pl.when(kj == 0)
        def _():
            acc[...] = jnp.zeros_like(acc)

        def step(ks, qs, masked):
            q, k, v, do = q_ref[qs, :] * SCALE, k_ref[ks, :], v_ref[ks, :], do_ref[qs, :]
            cq, lse_rows = cq_ref[:, qs], l_ref[:, qs]
            mask = _causal_sub(ks, qs) if masked else None
            for h in range(2):
                pt = jnp.exp(_foxt_logits(q, k, cq[h:h + 1, :], ck_ref[h, ks, :], mask, hm[h])
                             - lse_rows[h:h + 1, :])
                acc[h, :, qs] += jnp.sum(pt * _dot_nt(_sel(hm[h], v), do), axis=0, keepdims=True)

        @pl.when(kj < qi)
        def _():
            step(slice(0, tk), slice(0, tq), False)

        @pl.when(kj == qi)
        def _():
            for ks, qs in _diag_blocks(tq):
                step(ks, qs, True)
            for h in range(2):
                d_ref[h:h + 1, :] = acc[h]

    def q_map(p, t):
        return (_sweep_q_major(t, nq)[0], p)

    def kv_map(off):
        return lambda p, t: (_sweep_q_major(t, nq)[1], off + p)

    blk = pl.BlockSpec((tq, LANES), q_map)
    row = pl.BlockSpec((None, 2, tq), lambda p, t: (p, 0, _sweep_q_major(t, nq)[0]))
    return pl.pallas_call(
        body, name=name, grid=(npair, _fox_steps(nq)),
        in_specs=[blk, pl.BlockSpec((tk, LANES), kv_map(npair)), pl.BlockSpec((tk, LANES), kv_map(2 * npair)),
                  blk, row, row, pl.BlockSpec((2, tk, LANES), lambda p, t: (p, _sweep_q_major(t, nq)[1], 0))],
        out_specs=row, out_shape=jax.ShapeDtypeStruct((npair, 2, s), F32),
        scratch_shapes=[pltpu.VMEM((2, 1, tq), F32)],
        compiler_params=_cp("parallel", "arbitrary"))(hb, hb, hb, dcat, lse, c_t3, c_rep)


def _foxt_bwd(hb, dcat, dsum, lse, c_rep, c_t3, name):
    s = hb.shape[0]
    npair = MIX_W // LANES
    tq, tk, nq = _fox_tiles(s)

    def body(q_ref, k_ref, v_ref, do_ref, d_ref, l_ref, cq_ref, ck_ref, dq_ref, dk_ref, dv_ref, dc_ref, dc_s):
        t = pl.program_id(1)
        kj, qi, first, last = _sweep_k_major(t, nq)
        hm = _head_masks()

        @pl.when(first)
        def _():
            dk_ref[...] = jnp.zeros_like(dk_ref)
            dv_ref[...] = jnp.zeros_like(dv_ref)
            dc_s[...] = jnp.zeros_like(dc_s)

        @pl.when(t == 0)
        def _():
            dq_ref[...] = jnp.zeros_like(dq_ref)

        def step(ks, qs, masked):
            q, k, v, do = q_ref[qs, :] * SCALE, k_ref[ks, :], v_ref[ks, :], do_ref[qs, :]
            qt, kt, dot = jnp.transpose(q), jnp.transpose(k), jnp.transpose(do)
            cq, lse_rows, d_rows = cq_ref[:, qs], l_ref[:, qs], d_ref[:, qs]
            mask = _causal_sub(ks, qs) if masked else None
            dqs, dks, dvs = [], [], []
            for h in range(2):
                rows = slice(h * HEAD_DIM, (h + 1) * HEAD_DIM)
                pt = jnp.exp(_foxt_logits(q, k, cq[h:h + 1, :], ck_ref[h, ks, :], mask, hm[h])
                             - lse_rows[h:h + 1, :])
                dst = pt * (_dot_nt(_sel(hm[h], v), do) - d_rows[h:h + 1, :])
                dsb = dst.astype(BF)
                dqs.append(_dot(kt[rows, :], dsb))
                dks.append(_dot_nt(qt[rows, :], dsb))
                dvs.append(_dot_nt(dot[rows, :], pt.astype(BF)))
                dc_s[h, ks, :] += _fold(dst)
            cols = pl.ds(pl.multiple_of(qi * tq + qs.start, qs.stop - qs.start), qs.stop - qs.start)
            dq_ref[:, cols] += SCALE * jnp.concatenate(dqs, axis=0)
            dk_ref[:, ks] += jnp.concatenate(dks, axis=0)
            dv_ref[:, ks] += jnp.concatenate(dvs, axis=0)

        @pl.when(kj < qi)
        def _():
            step(slice(0, tk), slice(0, tq), False)

        @pl.when(kj == qi)
        def _():
            for ks, qs in _diag_blocks(tq):
                step(ks, qs, True)

        @pl.when(last)
        def _():
            for h in range(2):
                dc_ref[h:h + 1, :] = -jnp.sum(jnp.transpose(dc_s[h]), axis=0, keepdims=True)

    def kj_of(t):
        return _sweep_k_major(t, nq)[0]

    def qi_of(t):
        return _sweep_k_major(t, nq)[1]

    qblk = pl.BlockSpec((tq, LANES), lambda p, t: (qi_of(t), p))
    row = pl.BlockSpec((None, 2, tq), lambda p, t: (p, 0, qi_of(t)))
    kblk = pl.BlockSpec((LANES, tk), lambda p, t: (p, kj_of(t)))
    rep = pl.BlockSpec((2, tk, LANES), lambda p, t: (p, kj_of(t), 0))
    return pl.pallas_call(
        body, name=name, grid=(npair, _fox_steps(nq)),
        in_specs=[qblk,
                  pl.BlockSpec((tk, LANES), lambda p, t: (kj_of(t), npair + p)),
                  pl.BlockSpec((tk, LANES), lambda p, t: (kj_of(t), 2 * npair + p)),
                  qblk, row, row, row, rep],
        out_specs=[pl.BlockSpec((LANES, s), lambda p, t: (p, 0)), kblk, kblk,
                   pl.BlockSpec((None, 2, tk), lambda p, t: (p, 0, kj_of(t)))],
        out_shape=[jax.ShapeDtypeStruct((MIX_W, s), F32), jax.ShapeDtypeStruct((MIX_W, s), F32),
                   jax.ShapeDtypeStruct((MIX_W, s), F32), jax.ShapeDtypeStruct((npair, 2, s), F32)],
        scratch_shapes=[pltpu.VMEM((2, tk, LANES), F32)],
        compiler_params=_cp("arbitrary", "arbitrary"))(hb, hb, hb, dcat, dsum, lse, c_t3, c_rep)


def _loss_head(y, target, name):
    s, d = y.shape
    ts = _rows(s)

    def body(y_ref, t_ref, dy_ref, l_ref):
        i = pl.program_id(0)
        e = y_ref[...] - t_ref[...]
        dy_ref[...] = e * (1.0 / d)

        @pl.when(i == 0)
        def _():
            l_ref[...] = jnp.zeros_like(l_ref)

        part = jnp.sum(jnp.sum(e * e, axis=1, keepdims=True), axis=0, keepdims=True)
        l_ref[...] += part * (0.5 / d)

    row = pl.BlockSpec((ts, d), lambda i: (i, 0))
    return pl.pallas_call(
        body, name=name, grid=(s // ts,), in_specs=[row, row],
        out_specs=[row, pl.BlockSpec((1, 1), lambda i: (0, 0))],
        out_shape=[jax.ShapeDtypeStruct((s, d), F32), jax.ShapeDtypeStruct((1, 1), F32)],
        compiler_params=_cp("arbitrary"))(y, target)


def _adam_rows(r, c):
    cap = max(8, (1 << 20) // (4 * c))
    if r <= cap:
        return r
    best = None
    for t in range(8, cap + 1, 8):
        if r % t == 0:
            best = t
    return best if best is not None else r


def _reduce_adamw(contribs, w, m, v, name):
    nl = len(contribs)
    nd, r, c = contribs[0].shape
    tr = _adam_rows(r, c)
    bc1 = 1.0 - ADAM_B1 ** ADAM_STEP
    bc2 = 1.0 - ADAM_B2 ** ADAM_STEP

    def body(*refs):
        c_refs = refs[:nl]
        w_ref, m_ref, v_ref, g_ref, d_ref, nm_ref, nv_ref = refs[nl:]
        l = pl.program_id(0)
        for li in range(nl):
            @pl.when(l == li)
            def _(c_ref=c_refs[li]):
                g = c_ref[0].astype(F32)
                for k in range(1, nd):
                    g = g + c_ref[k].astype(F32)
                nm = ADAM_B1 * m_ref[...] + (1.0 - ADAM_B1) * g
                nv = ADAM_B2 * v_ref[...] + (1.0 - ADAM_B2) * (g * g)
                g_ref[...] = g
                nm_ref[...] = nm
                nv_ref[...] = nv
                d_ref[...] = -ADAM_LR * ((nm / bc1) / (jnp.sqrt(nv / bc2) + ADAM_EPS) + ADAM_WD * w_ref[...])

    def c_spec(li):
        return pl.BlockSpec((nd, tr, c), lambda l, i: (0, jnp.where(l == li, i, 0), 0))

    blk = pl.BlockSpec((None, tr, c), lambda l, i: (l, i, 0))
    out = jax.ShapeDtypeStruct((nl, r, c), F32)
    return pl.pallas_call(
        body, name=name, grid=(nl, r // tr),
        in_specs=[c_spec(li) for li in range(nl)] + [blk, blk, blk],
        out_specs=[blk, blk, blk, blk], out_shape=[out, out, out, out],
        compiler_params=_cp("arbitrary", "arbitrary"))(*contribs, w, m, v)


def _mesh_pos():
    return lax.axis_index("x"), lax.axis_index("y"), lax.axis_index("c")


def _peer(pos, k):
    x, y, c = pos
    return (1 - x if k & 4 else x, 1 - y if k & 2 else y, 1 - c if k & 1 else c)


def _linear(pos):
    return 4 * pos[0] + 2 * pos[1] + pos[2]


def _xfer_copies(srcs, lands, send_sems, recv_sems, local_sems, gather):
    pos = _mesh_pos()
    me = _linear(pos)
    local, remote = [], []
    for i, (src, land) in enumerate(zip(srcs, lands)):
        local.append(pltpu.make_async_copy(src if gather else src.at[me], land.at[me], local_sems.at[i]))
        for k in range(1, N_DEV):
            peer = _peer(pos, k)
            remote.append(pltpu.make_async_remote_copy(
                src_ref=src if gather else src.at[_linear(peer)], dst_ref=land.at[me],
                send_sem=send_sems.at[i * (N_DEV - 1) + k - 1], recv_sem=recv_sems.at[i * (N_DEV - 1) + k - 1],
                device_id=peer, device_id_type=MESH_ID))
    return local, remote


_HBM = pl.BlockSpec(memory_space=pltpu.HBM)
_SEM = pl.BlockSpec(memory_space=pltpu.SEMAPHORE)
_EFFECT = pltpu.SideEffectType.DATAFLOW_SIDE_EFFECTING


def _xfer_start(srcs, gather, name, after=()):
    n = len(srcs)
    na = len(after)
    lands = [lax.empty(((N_DEV,) + a.shape) if gather else a.shape, a.dtype) for a in srcs]

    def body(*refs):
        src, land = refs[:n], refs[n:2 * n]
        send_sems, recv_sems, local_sems = refs[2 * n + na:2 * n + na + 3]
        local, remote = _xfer_copies(src, land, send_sems, recv_sems, local_sems, gather)
        for cp in local + remote:
            cp.start()
        refs[-1][...] = jnp.zeros_like(refs[-1])

    nsem = n * (N_DEV - 1)
    out = pl.pallas_call(
        body, name=name,
        out_shape=(pltpu.SemaphoreType.DMA((nsem,)), pltpu.SemaphoreType.DMA((nsem,)), pltpu.SemaphoreType.DMA((n,)),
                   *[pltpu.HBM(a.shape, a.dtype) for a in srcs], *[pltpu.HBM(a.shape, a.dtype) for a in lands],
                   jax.ShapeDtypeStruct((8, LANES), F32)),
        in_specs=[_HBM] * (2 * n) + [pl.BlockSpec(memory_space=pl.ANY)] * na,
        out_specs=(_SEM, _SEM, _SEM, *[_HBM] * (2 * n), pl.BlockSpec(memory_space=pltpu.VMEM)),
        input_output_aliases={i: 3 + i for i in range(2 * n)},
        compiler_params=pltpu.CompilerParams(has_side_effects=_EFFECT))(
            *[pltpu.with_memory_space_constraint(a, pltpu.HBM) for a in srcs],
            *[pltpu.with_memory_space_constraint(a, pltpu.HBM) for a in lands], *after)
    return out[:3], list(out[3:3 + n]), list(out[3 + n:3 + 2 * n]), out[-1]


def _started(handle):
    return handle[3]


def _xfer_wait(handle, after, gather, name):
    sems, srcs, lands, _ = handle
    n = len(srcs)

    def body(*refs):
        src, land = refs[:n], refs[n:2 * n]
        send_sems, recv_sems, local_sems = refs[2 * n:2 * n + 3]
        local, remote = _xfer_copies(src, land, send_sems, recv_sems, local_sems, gather)
        for cp in local:
            cp.wait()
        for cp in remote:
            cp.wait_send()
            cp.wait_recv()

    out = pl.pallas_call(
        body, name=name,
        out_shape=(*[pltpu.HBM(a.shape, a.dtype) for a in srcs], *[pltpu.HBM(a.shape, a.dtype) for a in lands]),
        in_specs=[_HBM] * (2 * n) + [_SEM] * 3 + [pl.BlockSpec(memory_space=pl.ANY)] * len(after),
        out_specs=tuple([_HBM] * (2 * n)), input_output_aliases={i: i for i in range(2 * n)},
        compiler_params=pltpu.CompilerParams(has_side_effects=_EFFECT))(*srcs, *lands, *sems, *after)
    return list(out[n:])


def _cols_full(g):
    nd, r, c = g.shape
    return jnp.transpose(g, (1, 0, 2)).reshape(r, nd * c)


def _cols_split(full):
    r, n = full.shape
    return jnp.transpose(full.reshape(r, N_DEV, n // N_DEV), (1, 0, 2))


def _pack_b_in(w):
    qkv = 3 * MIX_W
    pad = jnp.zeros((w.shape[0], B_IN_PAD - w.shape[1]), w.dtype)
    return jnp.concatenate([w[:, :qkv], w[:, qkv + N_MIX_HEADS:], w[:, qkv:qkv + N_MIX_HEADS], pad], axis=1)


def _unpack_b_in(w):
    qkv = 3 * MIX_W
    return jnp.concatenate([w[:, :qkv], w[:, qkv + MEM_W:qkv + MEM_W + N_MIX_HEADS], w[:, qkv:qkv + MEM_W]], axis=1)


def _ffn_forward(x, xb, wgu, get_rest, tag, fused=True):
    if fused:
        wd4, gain, bias = get_rest(x)
        y, yb, gu, a, xh, rstd = _ffn_fwd_main(x, xb, wgu, wd4, gain, bias, f"{tag}_fwd_main")
    else:
        gu, a = _ffn_up(xb, wgu, f"{tag}_up")
        wd4, gain, bias = get_rest(a)
        y, yb, xh, rstd = _mm_res_ln(a, wd4, x, gain, bias, 0.5, f"{tag}_down_ln")
    return y, yb, (xb, gu, a, xh, rstd), wd4


def _ffn_backward(dy, saved, wgu, wd4, gain, tag, after=(), send=None):
    xb, gu, a, xh, rstd = saved
    s = xb.shape[0]
    nd, c, d = wgu.shape
    dx, dzb, dh, dgain, dbias = _ffn_bwd_main(dy, xh, rstd, gain, wd4, wgu, gu, f"{tag}_bwd_main", after,
                                               with_dx=send is None)
    dh = dh.reshape(nd, s, c)
    dwd = _mm_tn(a, dzb[None], f"{tag}_dwd").reshape(nd, wd4.shape[1] // 2, d)
    if send is not None:
        send("down", dwd, dgain, dbias)
    dwgu = _mm_tn(dh, xb[None], f"{tag}_dwgu")
    if send is not None:
        sent = send("gate_up", dwgu)
        dx = _mm_nt(dh, wgu, f"{tag}_dx", res=dx, w_rows_out=False, after=sent)
    return dx, dwgu, dwd, dgain, dbias


def _mixer_a_forward(x, xb, memb, w_in, w_kv, w_out, gain, bias, tabs):
    hb = _proj_rope(xb, w_in, tabs, 2 * MIX_W // LANES, "a_in", True)
    groups = [_band_fwd(hb, g, f"a_band_fwd{g}") for g in range(N_GROUPS)]
    oa, lt = _band_combine([o for o, _ in groups], [l for _, l in groups], "a_combine")
    kv = _mm_nn(memb, w_kv, BF, "a_mem_kv")
    om, lm = _mem_fwd(hb, 3 * MIX_W // LANES, kv, "a_mem_fwd")
    cat = jnp.concatenate([oa, om], axis=1)
    y, yb, xh, rstd = _mm_res_ln(cat[None], w_out[None], x, gain, bias, 1.0, "a_out_ln")
    return y, yb, (xb, hb, oa, lt, kv, lm, cat, xh, rstd)


def _mixer_a_backward(dy, saved, memb, w_in, w_kv, w_out, gain, tabs_neg, after=()):
    xb, hb, oa, lt, kv, lm, cat, xh, rstd = saved
    dz, dzb, dgain, dbias = _ln_bwd(dy, xh, rstd, gain, 1.0, "a_ln_bwd", after)
    dcat = _mm_nt(dzb[None], w_out[None], "a_dcat", out_dtype=BF)
    dw_out = _mm_tn(cat[None], dzb[None], "a_dwout")[0]
    dqm, dkm, dvm = _mem_bwd(hb, 3 * MIX_W // LANES, kv, dcat, cat, GROUP_W // LANES, lm, "a_mem_bwd")
    dkv = jnp.concatenate([dkm, dvm], axis=1).astype(BF)
    dw_kv = _mm_tn(memb[None], dkv[None], "a_dwkv")[0]
    grads = [_band_bwd(hb, dcat, oa, lt, g, f"a_band_bwd{g}") for g in range(N_GROUPS)]
    dhb = _rope_cast([grads[g][i] for i in range(3) for g in range(N_GROUPS)] + [dqm], tabs_neg,
                     2 * MIX_W // LANES, "a_rope_bwd")
    dw_in = _mm_tn(dhb[None], xb[None], "a_dwin")[0]
    dx = _mm_nt(dhb[None], w_in[None], "a_dx", res=dz, w_rows_out=False)
    return dx, dw_in, dw_kv, dw_out, dgain, dbias


def _pad_rows(t, rows):
    return jnp.concatenate([t, jnp.zeros((rows - t.shape[0], t.shape[1]), t.dtype)], axis=0)


def _pad_cols(t, cols):
    return jnp.concatenate([t, jnp.zeros((t.shape[0], cols - t.shape[1]), t.dtype)], axis=1)


def _mixer_b_forward(x, xb, memb, w_in, fbias, w_kv, w_out, gain, bias, tabs):
    s = x.shape[0]
    hb, f = _proj_rope(xb, w_in, tabs, 0, "b_in", False, tail_block=(3 * MIX_W + MEM_W) // LANES)
    f_t = _pad_rows(jnp.transpose(f[:, :N_MIX_HEADS]), 16)
    bias16 = _pad_rows(jnp.transpose(fbias), 16)
    c_t = _gate_fwd(f_t, bias16, "b_gate_fwd")
    c_t3 = c_t[:N_MIX_HEADS].reshape(N_MIX_HEADS // 2, 2, s)
    c_rep = jnp.broadcast_to(c_t[:N_MIX_HEADS, :, None], (N_MIX_HEADS, s, LANES))
    ob, lb = _foxt_fwd(hb, c_rep, c_t3, "b_fox_fwd")
    kv = _mm_nn(memb, w_kv, BF, "b_mem_kv")
    om, lm = _mem_fwd(hb, 3 * MIX_W // LANES, kv, "b_mem_fwd")
    cat = jnp.concatenate([ob, om], axis=1)
    y, yb, xh, rstd = _mm_res_ln(cat[None], w_out[None], x, gain, bias, 1.0, "b_out_ln")
    return y, yb, (xb, hb, f_t, bias16, c_rep, c_t3, lb, kv, lm, cat, xh, rstd)


def _mixer_b_backward(dy, saved, memb, w_in, w_kv, w_out, gain, tabs, after=()):
    xb, hb, f_t, bias16, c_rep, c_t3, lb, kv, lm, cat, xh, rstd = saved
    s = xb.shape[0]
    dz, dzb, dgain, dbias = _ln_bwd(dy, xh, rstd, gain, 1.0, "b_ln_bwd", after)
    dcat = _mm_nt(dzb[None], w_out[None], "b_dcat", out_dtype=BF)
    dw_out = _mm_tn(cat[None], dzb[None], "b_dwout")[0]
    dqm, dkm, dvm = _mem_bwd(hb, 3 * MIX_W // LANES, kv, dcat, cat, MIX_W // LANES, lm, "b_mem_bwd")
    dkv = jnp.concatenate([dkm, dvm], axis=1).astype(BF)
    dw_kv = _mm_tn(memb[None], dkv[None], "b_dwkv")[0]
    dsum = _foxt_dsum(hb, dcat, lb, c_rep, c_t3, "b_fox_dsum")
    dq, dk, dv, dc3 = _foxt_bwd(hb, dcat, dsum, lb, c_rep, c_t3, "b_fox_bwd")
    df_t, dfb = _gate_bwd(_pad_rows(dc3.reshape(N_MIX_HEADS, s), 16), f_t, bias16, "b_gate_bwd")
    df = _pad_cols(jnp.transpose(df_t[:N_MIX_HEADS]), B_IN_PAD - 3 * MIX_W - MEM_W)
    dhb = _rope_cast([dq, dk, dv, dqm, df], tabs, 0, "b_cast_bwd", transposed=(0, 1, 2))
    dw_in = _mm_tn(xb[None], dhb[None], "b_dwin")[0]
    dx = _mm_nt(dhb[None], w_in[None], "b_dx", res=dz)
    return dx, dw_in, jnp.transpose(dfb[:N_MIX_HEADS]), dw_kv, dw_out, dgain, dbias


def _stored(t, name):
    return jnp.transpose(t, (0, 2, 1)) if name in ROWS_OUT else t


GATHER_GROUPS = (
    (("ffn1_w_gate_up", 0),),
    (("ffn1_w_down", 0), ("ln_gain", None), ("ln_bias", None)),
    (("a_w_in", 0), ("a_w_out", 0), ("mem_w_kv", 0)),
    (("ffn2_w_gate_up", 0), ("ffn2_w_down", 0)),
    (("ffn1_w_gate_up", 1), ("ffn1_w_down", 1)),
    (("b_w_in", 0), ("b_w_out", 0), ("mem_w_kv", 1)),
    (("ffn2_w_gate_up", 1), ("ffn2_w_down", 1)),
)


def _group_shards(group, params):
    return [t if n in F32_COMM else _stored(t, n)[l].astype(BF) for (n, l), t in zip(group, params)]


def _weight_groups(w):
    return [_group_shards(grp, [w[n] for n, _ in grp]) for grp in GATHER_GROUPS]


def _local_step(x, mem, target, fbias, get_w, put_g):
    s, d = x.shape
    tabs = _rope_tables(s, 1.0)
    tabs_neg = _rope_tables(s, -1.0)
    memb = mem.astype(BF)
    saved, wl = [], []
    cur, curb = x, x.astype(BF)
    ln = []

    def down4(t):
        return t.reshape(N_DEV // 2, -1, d)

    for i in range(DEPTH):
        if i == 0:
            def first_rest(a):
                g = get_w(1, a)
                ln.extend(jnp.transpose(t, (1, 2, 0, 3)).reshape(DEPTH, 3, 1, d) for t in g[1:3])
                return down4(g[0]), ln[0][0, 0], ln[1][0, 0]

            wgu = get_w(0, cur)[0]
            cur, curb, s1, wd = _ffn_forward(cur, curb, wgu, first_rest, "l0_ffn1", fused=False)
        else:
            g = get_w(3 * i + 1, cur)
            wgu = g[0]
            cur, curb, s1, wd = _ffn_forward(cur, curb, wgu, lambda a, g=g: (down4(g[1]), ln[0][i, 0], ln[1][i, 0]),
                                             f"l{i}_ffn1")
        w1 = (wgu, wd)
        ln_g, ln_b = ln
        g = get_w(3 * i + 2, cur)
        if i == 0:
            wm = (g[0].reshape(-1, d), g[2].reshape(d, -1), _cols_full(g[1]))
            cur, curb, s2 = _mixer_a_forward(cur, curb, memb, wm[0], wm[1], wm[2], ln_g[i, 1], ln_b[i, 1], tabs)
        else:
            wm = (_pack_b_in(g[0].reshape(d, -1)), g[2].reshape(d, -1), g[1].reshape(d, -1))
            cur, curb, s2 = _mixer_b_forward(cur, curb, memb, wm[0], fbias, wm[1], wm[2], ln_g[i, 1], ln_b[i, 1],
                                             tabs)
        g = get_w(3 * i + 3, cur)
        cur, curb, s3, wd = _ffn_forward(cur, curb, g[0], lambda a, g=g: (down4(g[1]), ln_g[i, 2], ln_b[i, 2]),
                                         f"l{i}_ffn2")
        w3 = (g[0], wd)
        saved.append((s1, s2, s3))
        wl.append((w1, wm, w3))

    dy, loss = _loss_head(cur, target, "loss_head")

    dgs = [[None] * 3 for _ in range(DEPTH)]
    dbs = [[None] * 3 for _ in range(DEPTH)]
    sent = ()
    for i in reversed(range(DEPTH)):
        s1, s2, s3 = saved[i]
        w1, wm, w3 = wl[i]
        dy, dgu, dd, dgs[i][2], dbs[i][2] = _ffn_backward(dy, s3, w3[0], w3[1], ln_g[i, 2], f"l{i}_ffn2", sent)
        sent = put_g(3 * i + 2, [dgu, dd])
        if i == 0:
            dy, dw_in, dw_kv, dw_out, dgs[i][1], dbs[i][1] = _mixer_a_backward(
                dy, s2, memb, wm[0], wm[1], wm[2], ln_g[i, 1], tabs_neg, sent)
            sent = put_g(1, [dw_in.reshape(N_DEV, -1, d), _cols_split(dw_out),
                             dw_kv.reshape(N_DEV, d // N_DEV, -1)])
        else:
            dy, dw_in, dfb, dw_kv, dw_out, dgs[i][1], dbs[i][1] = _mixer_b_backward(
                dy, s2, memb, wm[0], wm[1], wm[2], ln_g[i, 1], tabs, sent)
            sent = put_g(4, [_unpack_b_in(dw_in).reshape(N_DEV, d // N_DEV, -1),
                             dw_out.reshape(N_DEV, d // N_DEV, -1), dw_kv.reshape(N_DEV, d // N_DEV, -1),
                             jnp.broadcast_to(dfb[None], (N_DEV,) + dfb.shape)])
        if i == 0:
            def send_last(kind, dw, dgain=None, dbias=None):
                if kind == "gate_up":
                    return put_g(6, [dw])
                dgs[0][0], dbs[0][0] = dgain, dbias
                ln_pieces = []
                for parts in (dgs, dbs):
                    t = jnp.concatenate([parts[a][b] for a in range(DEPTH) for b in range(3)], axis=0)
                    ln_pieces.append(jnp.transpose(t.reshape(DEPTH * 3, N_DEV, d // N_DEV), (1, 0, 2)))
                return put_g(0, [dw] + ln_pieces)

            dy = _ffn_backward(dy, s1, w1[0], w1[1], ln_g[i, 0], "l0_ffn1", sent, send_last)[0]
        else:
            dy, dgu, dd, dgs[i][0], dbs[i][0] = _ffn_backward(dy, s1, w1[0], w1[1], ln_g[i, 0], f"l{i}_ffn1", sent)
            sent = put_g(3, [dgu, dd])
    return loss, dy


WEIGHTS = ("ffn1_w_gate_up", "ffn1_w_down", "ffn2_w_gate_up", "ffn2_w_down", "ln_gain", "ln_bias", "mem_w_kv",
           "a_w_in", "a_w_out", "b_w_in", "b_forget_bias", "b_w_out")
F32_COMM = ("ln_gain", "ln_bias", "b_forget_bias")
ROWS_OUT = ("ffn1_w_gate_up", "ffn2_w_gate_up", "a_w_in")
GRAD_SLOTS = {
    "ffn1_w_gate_up": [(6, 0), (3, 0)], "ffn1_w_down": [(0, 0), (3, 1)],
    "ffn2_w_gate_up": [(2, 0), (5, 0)], "ffn2_w_down": [(2, 1), (5, 1)],
    "ln_gain": [(0, 1)], "ln_bias": [(0, 2)], "mem_w_kv": [(1, 2), (4, 2)],
    "a_w_in": [(1, 0)], "a_w_out": [(1, 1)], "b_w_in": [(4, 0)], "b_forget_bias": [(4, 3)], "b_w_out": [(4, 1)],
}


def kernel(x, mem, ffn1_w_gate_up, ffn1_w_down, ffn2_w_gate_up, ffn2_w_down, ln_gain, ln_bias, mem_w_kv, a_w_in, a_w_out, b_w_in, b_forget_bias, b_w_out, loss_target, m_ffn1_w_gate_up, m_ffn1_w_down, m_ffn2_w_gate_up, m_ffn2_w_down, m_ln_gain, m_ln_bias, m_mem_w_kv, m_a_w_in, m_a_w_out, m_b_w_in, m_b_forget_bias, m_b_w_out, v_ffn1_w_gate_up, v_ffn1_w_down, v_ffn2_w_gate_up, v_ffn2_w_down, v_ln_gain, v_ln_bias, v_mem_w_kv, v_a_w_in, v_a_w_out, v_b_w_in, v_b_forget_bias, v_b_w_out):
    w = dict(zip(WEIGHTS, (ffn1_w_gate_up, ffn1_w_down, ffn2_w_gate_up, ffn2_w_down, ln_gain, ln_bias, mem_w_kv,
                           a_w_in, a_w_out, b_w_in, b_forget_bias, b_w_out)))
    m = dict(zip(WEIGHTS, (m_ffn1_w_gate_up, m_ffn1_w_down, m_ffn2_w_gate_up, m_ffn2_w_down, m_ln_gain, m_ln_bias,
                           m_mem_w_kv, m_a_w_in, m_a_w_out, m_b_w_in, m_b_forget_bias, m_b_w_out)))
    v = dict(zip(WEIGHTS, (v_ffn1_w_gate_up, v_ffn1_w_down, v_ffn2_w_gate_up, v_ffn2_w_down, v_ln_gain, v_ln_bias,
                           v_mem_w_kv, v_a_w_in, v_a_w_out, v_b_w_in, v_b_forget_bias, v_b_w_out)))

    gathers = []
    for k, grp in enumerate(GATHER_GROUPS):
        params, behind = [w[n] for n, _ in grp], [_started(h) for h in gathers[-1:]]
        if behind:
            params, behind = lax.optimization_barrier((params, behind))
        gathers.append(_xfer_start(_group_shards(grp, params), True, f"gather{k}_start", behind))
    exchanges = {}

    def get_w(k, after):
        behind = [after] + ([_started(h) for h in gathers] if k == 0 else [])
        return _xfer_wait(gathers[k], behind, True, f"gather{k}_wait")

    def put_g(k, pieces):
        behind = [_started(exchanges[0])] if k == 6 else []
        exchanges[k] = _xfer_start(pieces, False, f"grads{k}_start", behind)
        return (_started(exchanges[k]),)

    loss, grad_x = _local_step(x[0], mem[0], loss_target[0], b_forget_bias, get_w, put_g)
    loss = lax.psum(loss[0, 0], ("x", "y", "c"))

    outs, landed = {}, {}

    def adamw(names):
        for n in names:
            contribs = [landed[g][j] for g, j in GRAD_SLOTS[n]]
            view = (len(contribs),) + contribs[0].shape[1:]
            shape = _stored(w[n], n).shape
            res = _reduce_adamw(contribs, *[_stored(t[n], n).reshape(view) for t in (w, m, v)], f"adamw_{n}")
            outs[n] = [_stored(t.reshape(shape), n) for t in res]
        return [outs[n][3] for n in names]

    after = [grad_x]
    for k in (5, 4, 3, 2, 1):
        landed[k] = _xfer_wait(exchanges[k], after, False, f"grads{k}_wait")
        after = [landed[k][0]]
    done = adamw(("ffn2_w_gate_up", "ffn2_w_down", "mem_w_kv", "a_w_in", "a_w_out", "b_w_in", "b_forget_bias",
                  "b_w_out"))
    landed[0] = _xfer_wait(exchanges[0], done, False, "grads0_wait")
    done = adamw(("ffn1_w_down", "ln_gain", "ln_bias"))
    landed[6] = _xfer_wait(exchanges[6], done, False, "grads6_wait")
    adamw(("ffn1_w_gate_up",))
    return (loss, grad_x[None], *[outs[n][0] for n in WEIGHTS], *[outs[n][1] for n in WEIGHTS],
            *[outs[n][2] for n in WEIGHTS], *[outs[n][3] for n in WEIGHTS])
```

```python
import functools

import jax
import jax.numpy as jnp
from jax import lax
from jax.experimental import pallas as pl
from jax.experimental.pallas import tpu as pltpu

F32 = jnp.float32
BF = jnp.bfloat16
MESH_ID = pl.DeviceIdType.MESH

N_DEV = 8
DEPTH = 2
HEAD_DIM = 64
LANES = 128
N_MIX_HEADS = 12
N_MEM_HEADS = 4
MIX_W = N_MIX_HEADS * HEAD_DIM
MEM_W = N_MEM_HEADS * HEAD_DIM
N_GROUPS = 3
GROUP_W = MIX_W // N_GROUPS
BLOCK = 128
BAND_SUB = 4
BAND_COLS = 4
ROT_HALF = 8
ROPE_THETA = 500000.0
ALPHA = (2 * DEPTH) ** 0.25
LN_EPS = 1e-5
SCALE = HEAD_DIM ** -0.5
NEG = -1e30
B_IN_PAD = 2688
ADAM_LR, ADAM_B1, ADAM_B2, ADAM_EPS, ADAM_WD, ADAM_STEP = 0.001, 0.9, 0.999, 1e-08, 0.01, 10
VMEM_LIMIT = 56 * 1024 * 1024


def _cp(*sem):
    return pltpu.CompilerParams(dimension_semantics=sem, vmem_limit_bytes=VMEM_LIMIT)


def _dot(a, b):
    return jnp.dot(a, b, preferred_element_type=F32)


def _dot_nt(a, b):
    return lax.dot_general(a, b, (((1,), (1,)), ((), ())), preferred_element_type=F32)


def _dot_tn(a, b):
    return lax.dot_general(a, b, (((0,), (0,)), ((), ())), preferred_element_type=F32)


def _sigmoid(x):
    return 1.0 / (1.0 + jnp.exp(-x))


def _tile(n, cap=1024):
    if n <= cap:
        return n
    best = LANES
    for t in range(LANES, cap + 1, LANES):
        if n % t == 0:
            best = t
    return best


def _rows(s, cap=512):
    return s if s <= cap else cap


def _mm_nn(a, b, out_dtype, name, b_rows_out=False):
    m, k = a.shape
    n = b.shape[0] if b_rows_out else b.shape[1]
    tm, tn = _rows(m), _tile(n)

    def body(a_ref, b_ref, o_ref):
        prod = _dot_nt(a_ref[...], b_ref[...]) if b_rows_out else _dot(a_ref[...], b_ref[...])
        o_ref[...] = prod.astype(o_ref.dtype)

    b_spec = (pl.BlockSpec((tn, k), lambda j, i: (j, 0)) if b_rows_out
              else pl.BlockSpec((k, tn), lambda j, i: (0, j)))
    return pl.pallas_call(
        body, name=name, grid=(n // tn, m // tm),
        in_specs=[pl.BlockSpec((tm, k), lambda j, i: (i, 0)), b_spec],
        out_specs=pl.BlockSpec((tm, tn), lambda j, i: (i, j)),
        out_shape=jax.ShapeDtypeStruct((m, n), out_dtype),
        compiler_params=_cp("parallel", "parallel"))(a, b)


def _resident(shape, index_map):
    return pl.BlockSpec(shape, index_map, pipeline_mode=pl.Buffered(1))


def _mm_tn(a, b, name, out_dtype=BF):
    na, s, m = a.shape
    nb, _, n = b.shape
    no = max(na, nb)
    tm, tn = _tile(m), _tile(n)

    def body(a_ref, b_ref, o_ref):
        o_ref[...] = _dot_tn(a_ref[...], b_ref[...]).astype(o_ref.dtype)

    def spec(nbatch, width, tile, index_map):
        fixed = nbatch == 1 and width == tile
        return _resident((None, s, tile), index_map) if fixed else pl.BlockSpec((None, s, tile), index_map)

    return pl.pallas_call(
        body, name=name, grid=(no, m // tm, n // tn),
        in_specs=[spec(na, m, tm, lambda j, r, c: (j if na > 1 else 0, 0, r)),
                  spec(nb, n, tn, lambda j, r, c: (j if nb > 1 else 0, 0, c))],
        out_specs=pl.BlockSpec((None, tm, tn), lambda j, r, c: (j, r, c)),
        out_shape=jax.ShapeDtypeStruct((no, m, n), out_dtype),
        compiler_params=_cp("parallel", "parallel", "parallel"))(a, b)


def _mm_nt(dh, w, name, res=None, out_dtype=F32, w_rows_out=True, after=()):
    nc, s, kc = dh.shape
    d = w.shape[1] if w_rows_out else w.shape[2]
    ts = _rows(s)
    has_res = res is not None
    mm = _dot_nt if w_rows_out else _dot

    def body(*refs):
        o_ref = refs[-1]
        dh_ref, w_ref = refs[:2]
        if has_res:
            r_ref = refs[2]
        out = mm(dh_ref[0], w_ref[0])
        for j in range(1, nc):
            out = out + mm(dh_ref[j], w_ref[j])
        if has_res:
            out = out + ALPHA * r_ref[...]
        o_ref[...] = out.astype(o_ref.dtype)

    in_specs = [pl.BlockSpec((nc, ts, kc), lambda i: (0, i, 0)), _resident(w.shape, lambda i: (0, 0, 0))]
    args = [dh, w]
    if has_res:
        in_specs.append(pl.BlockSpec((ts, d), lambda i: (i, 0)))
        args.append(res)
    in_specs += [pl.BlockSpec(memory_space=pl.ANY)] * len(after)
    args += list(after)
    return pl.pallas_call(
        body, name=name, grid=(s // ts,), in_specs=in_specs,
        out_specs=pl.BlockSpec((ts, d), lambda i: (i, 0)),
        out_shape=jax.ShapeDtypeStruct((s, d), out_dtype),
        compiler_params=_cp("parallel"))(*args)


def _mm_res_ln(a, w, x, gain, bias, fscale, name):
    nc, s, kc = a.shape
    d = w.shape[2]
    ts = _rows(s)

    def body(a_ref, w_ref, x_ref, g_ref, b_ref, y_ref, yb_ref, xh_ref, r_ref):
        f = _dot(a_ref[0], w_ref[0])
        for j in range(1, nc):
            f = f + _dot(a_ref[j], w_ref[j])
        z = ALPHA * x_ref[...] + fscale * f
        mu = jnp.mean(z, axis=-1, keepdims=True)
        zc = z - mu
        var = jnp.mean(zc * zc, axis=-1, keepdims=True)
        r = lax.rsqrt(var + LN_EPS)
        xh = zc * r
        y = xh * g_ref[...] + b_ref[...]
        y_ref[...] = y
        yb_ref[...] = y.astype(BF)
        xh_ref[...] = xh
        r_ref[...] = r

    row = pl.BlockSpec((ts, d), lambda i: (i, 0))
    vec = pl.BlockSpec((1, d), lambda i: (0, 0))
    return pl.pallas_call(
        body, name=name, grid=(s // ts,),
        in_specs=[pl.BlockSpec((nc, ts, kc), lambda i: (0, i, 0)), _resident((nc, kc, d), lambda i: (0, 0, 0)),
                  row, vec, vec],
        out_specs=[row, row, row, pl.BlockSpec((ts, 1), lambda i: (i, 0))],
        out_shape=[jax.ShapeDtypeStruct((s, d), F32), jax.ShapeDtypeStruct((s, d), BF),
                   jax.ShapeDtypeStruct((s, d), F32), jax.ShapeDtypeStruct((s, 1), F32)],
        compiler_params=_cp("parallel"))(a, w, x, gain, bias)


def _ln_bwd_proj(dy, xh, rstd, gain, w_out, name, after=()):
    s, d = dy.shape
    wc = w_out.shape[0]
    ts = _rows(s)
    na = len(after)

    def body(*refs):
        dy_ref, xh_ref, r_ref, g_ref, w_ref = refs[:5]
        dz_ref, dzb_ref, dc_ref, dg_ref, db_ref = refs[5 + na:]
        i = pl.program_id(0)
        dyv = dy_ref[...]
        xhv = xh_ref[...]
        dxh = dyv * g_ref[...]
        m1 = jnp.mean(dxh, axis=-1, keepdims=True)
        m2 = jnp.mean(dxh * xhv, axis=-1, keepdims=True)
        dz = r_ref[...] * (dxh - m1 - xhv * m2)
        dzb = dz.astype(BF)
        dz_ref[...] = dz
        dzb_ref[...] = dzb
        dc_ref[...] = _dot_nt(dzb, w_ref[...]).astype(BF)

        @pl.when(i == 0)
        def _():
            dg_ref[...] = jnp.zeros_like(dg_ref)
            db_ref[...] = jnp.zeros_like(db_ref)

        dg_ref[...] += jnp.sum(dyv * xhv, axis=0, keepdims=True)
        db_ref[...] += jnp.sum(dyv, axis=0, keepdims=True)

    row = pl.BlockSpec((ts, d), lambda i: (i, 0))
    vec = pl.BlockSpec((1, d), lambda i: (0, 0))
    return pl.pallas_call(
        body, name=name, grid=(s // ts,),
        in_specs=[row, row, pl.BlockSpec((ts, 1), lambda i: (i, 0)), vec, _resident((wc, d), lambda i: (0, 0))]
                 + [pl.BlockSpec(memory_space=pl.ANY)] * na,
        out_specs=[row, row, pl.BlockSpec((ts, wc), lambda i: (i, 0)), vec, vec],
        out_shape=[jax.ShapeDtypeStruct((s, d), F32), jax.ShapeDtypeStruct((s, d), BF),
                   jax.ShapeDtypeStruct((s, wc), BF),
                   jax.ShapeDtypeStruct((1, d), F32), jax.ShapeDtypeStruct((1, d), F32)],
        compiler_params=_cp("arbitrary"))(dy, xh, rstd, gain, w_out, *after)


def _ffn_up(xb, wgu, name):
    s, d = xb.shape
    c = wgu.shape[1]
    nch = wgu.shape[0] // 2
    ts = _rows(s, 1024)
    w4 = wgu.reshape(2, nch, c, d)

    def body(x_ref, w_ref, gu_ref, a_ref):
        x = x_ref[...]
        g = _dot_nt(x, w_ref[0])
        u = _dot_nt(x, w_ref[1])
        sg = _sigmoid(g)
        t = g * sg
        gu_ref[0] = (u * (sg * (1.0 + g - t))).astype(BF)
        gu_ref[1] = t.astype(BF)
        a_ref[...] = (t * u).astype(BF)

    return pl.pallas_call(
        body, name=name, grid=(nch, s // ts),
        in_specs=[pl.BlockSpec((ts, d), lambda j, i: (i, 0)),
                  pl.BlockSpec((2, None, c, d), lambda j, i: (0, j, 0, 0))],
        out_specs=[pl.BlockSpec((2, None, ts, c), lambda j, i: (0, j, i, 0)),
                   pl.BlockSpec((None, ts, c), lambda j, i: (j, i, 0))],
        out_shape=[jax.ShapeDtypeStruct((2, nch, s, c), BF), jax.ShapeDtypeStruct((nch, s, c), BF)],
        compiler_params=_cp("parallel", "parallel"))(xb, w4)


def _ffn_fwd_main(x, xb, wgu, wd4, gain, bias, name, target=None):
    s, d = x.shape
    nch, c = wd4.shape[0], wd4.shape[1]
    ts = _rows(s, 256)
    head = target is not None

    def body(*refs):
        x_ref, xb_ref, wgu_ref, wd_ref, g_ref, b_ref = refs[:6]
        y_ref, yb_ref, gu_ref, a_ref, xh_ref, r_ref = refs[6 + head:]
        xbv = xb_ref[...]
        f = jnp.zeros((ts, d), F32)
        for j in range(nch):
            g = _dot_nt(xbv, wgu_ref[j])
            u = _dot_nt(xbv, wgu_ref[nch + j])
            sg = _sigmoid(g)
            t = g * sg
            gu_ref[0, j] = (u * (sg * (1.0 + g - t))).astype(BF)
            gu_ref[1, j] = t.astype(BF)
            act = (t * u).astype(BF)
            a_ref[j] = act
            f = f + _dot(act, wd_ref[j])
        z = ALPHA * x_ref[...] + 0.5 * f
        mu = jnp.mean(z, axis=-1, keepdims=True)
        zc = z - mu
        var = jnp.mean(zc * zc, axis=-1, keepdims=True)
        r = lax.rsqrt(var + LN_EPS)
        xh = zc * r
        y = xh * g_ref[...] + b_ref[...]
        xh_ref[...] = xh
        r_ref[...] = r
        if head:
            e = y - refs[6][...]
            y_ref[...] = e * (1.0 / d)

            @pl.when(pl.program_id(0) == 0)
            def _():
                yb_ref[...] = jnp.zeros_like(yb_ref)

            part = jnp.sum(jnp.sum(e * e, axis=1, keepdims=True), axis=0, keepdims=True)
            yb_ref[...] += part * (0.5 / d)
        else:
            y_ref[...] = y
            yb_ref[...] = y.astype(BF)

    row = pl.BlockSpec((ts, d), lambda i: (i, 0))
    vec = pl.BlockSpec((1, d), lambda i: (0, 0))
    second = ((pl.BlockSpec((1, 1), lambda i: (0, 0)), jax.ShapeDtypeStruct((1, 1), F32)) if head
              else (row, jax.ShapeDtypeStruct((s, d), BF)))
    return pl.pallas_call(
        body, name=name, grid=(s // ts,),
        in_specs=[row, row, _resident(wgu.shape, lambda i: (0, 0, 0)), _resident(wd4.shape, lambda i: (0, 0, 0)),
                  vec, vec] + [row] * head,
        out_specs=[row, second[0], pl.BlockSpec((2, nch, ts, c), lambda i: (0, 0, i, 0)),
                   pl.BlockSpec((nch, ts, c), lambda i: (0, i, 0)), row, pl.BlockSpec((ts, 1), lambda i: (i, 0))],
        out_shape=[jax.ShapeDtypeStruct((s, d), F32), second[1],
                   jax.ShapeDtypeStruct((2, nch, s, c), BF), jax.ShapeDtypeStruct((nch, s, c), BF),
                   jax.ShapeDtypeStruct((s, d), F32), jax.ShapeDtypeStruct((s, 1), F32)],
        compiler_params=_cp("arbitrary" if head else "parallel"))(x, xb, wgu, wd4, gain, bias,
                                                                   *([target] if head else []))


def _ffn_bwd_main(dy, xh, rstd, gain, wd4, wgu, gu, name, after=(), with_dx=True):
    s, d = dy.shape
    nch, c = wd4.shape[0], wd4.shape[1]
    ts = _rows(s, 256)
    na = len(after)

    def body(*refs):
        dy_ref, xh_ref, r_ref, g_ref, wd_ref, wgu_ref, gu_ref = refs[:7]
        dx_ref, dzb_ref, dh_ref, dg_ref, db_ref = refs[7 + na:]
        i = pl.program_id(0)
        dyv = dy_ref[...]
        xhv = xh_ref[...]
        dxh = dyv * g_ref[...]
        m1 = jnp.mean(dxh, axis=-1, keepdims=True)
        m2 = jnp.mean(dxh * xhv, axis=-1, keepdims=True)
        dz = r_ref[...] * (dxh - m1 - xhv * m2)
        dzb = (0.5 * dz).astype(BF)
        dzb_ref[...] = dzb

        @pl.when(i == 0)
        def _():
            dg_ref[...] = jnp.zeros_like(dg_ref)
            db_ref[...] = jnp.zeros_like(db_ref)

        dg_ref[...] += jnp.sum(dyv * xhv, axis=0, keepdims=True)
        db_ref[...] += jnp.sum(dyv, axis=0, keepdims=True)

        dx = ALPHA * dz if with_dx else dz
        for j in range(nch):
            da = _dot_nt(dzb, wd_ref[j])
            dgate = (da * gu_ref[0, j].astype(F32)).astype(BF)
            dup = (da * gu_ref[1, j].astype(F32)).astype(BF)
            dh_ref[0, j] = dgate
            dh_ref[1, j] = dup
            if with_dx:
                dx = dx + _dot(dgate, wgu_ref[j]) + _dot(dup, wgu_ref[nch + j])
        dx_ref[...] = dx

    row = pl.BlockSpec((ts, d), lambda i: (i, 0))
    vec = pl.BlockSpec((1, d), lambda i: (0, 0))
    act = pl.BlockSpec((2, nch, ts, c), lambda i: (0, 0, i, 0))
    return pl.pallas_call(
        body, name=name, grid=(s // ts,),
        in_specs=[row, row, pl.BlockSpec((ts, 1), lambda i: (i, 0)), vec,
                  _resident(wd4.shape, lambda i: (0, 0, 0)), _resident(wgu.shape, lambda i: (0, 0, 0)), act]
                 + [pl.BlockSpec(memory_space=pl.ANY)] * na,
        out_specs=[row, row, act, vec, vec],
        out_shape=[jax.ShapeDtypeStruct((s, d), F32), jax.ShapeDtypeStruct((s, d), BF),
                   jax.ShapeDtypeStruct((2, nch, s, c), BF),
                   jax.ShapeDtypeStruct((1, d), F32), jax.ShapeDtypeStruct((1, d), F32)],
        compiler_params=_cp("arbitrary"))(dy, xh, rstd, gain, wd4, wgu, gu, *after)


def _rope_tables(s, sign):
    pos = jnp.arange(s, dtype=F32)
    inv_freq = 1.0 / (ROPE_THETA ** (jnp.arange(ROT_HALF, dtype=F32) / ROT_HALF))
    ang = pos[:, None] * inv_freq[None, :]
    cos, sin = jnp.cos(ang), jnp.sin(ang) * sign
    one = jnp.ones((s, HEAD_DIM - 2 * ROT_HALF), F32)
    zero = jnp.zeros((s, HEAD_DIM - 2 * ROT_HALF), F32)
    zh = jnp.zeros((s, ROT_HALF), F32)
    cos_f = jnp.concatenate([cos, cos, one], axis=1)
    sin_a = jnp.concatenate([-sin, zh, zero], axis=1)
    sin_b = jnp.concatenate([zh, sin, zero], axis=1)
    rep = LANES // HEAD_DIM
    return tuple(jnp.tile(t, (1, rep)) for t in (cos_f, sin_a, sin_b))


def _rope(t, c_ref, sa_ref, sb_ref):
    return (t * c_ref[...] + pltpu.roll(t, LANES - ROT_HALF, 1) * sa_ref[...]
            + pltpu.roll(t, ROT_HALF, 1) * sb_ref[...])


def _proj_rope(xb, w, tabs, n_rope, name, w_rows_out, tail_block=None):
    s, d = xb.shape
    n = w.shape[0] if w_rows_out else w.shape[1]
    tm = _rows(s, 256)
    has_tail = tail_block is not None

    def body(x_ref, w_ref, c_ref, sa_ref, sb_ref, o_ref, *tail_ref):
        h = (_dot_nt if w_rows_out else _dot)(x_ref[...], w_ref[...])
        for cb in range(n // LANES):
            t = h[:, cb * LANES:(cb + 1) * LANES]
            if cb < n_rope:
                t = _rope(t, c_ref, sa_ref, sb_ref)
            o_ref[:, cb * LANES:(cb + 1) * LANES] = t.astype(BF)
        if has_tail:
            tail_ref[0][...] = h[:, tail_block * LANES:(tail_block + 1) * LANES]

    tab = pl.BlockSpec((tm, LANES), lambda i: (i, 0))
    out_specs = [pl.BlockSpec((tm, n), lambda i: (i, 0))]
    out_shape = [jax.ShapeDtypeStruct((s, n), BF)]
    if has_tail:
        out_specs.append(tab)
        out_shape.append(jax.ShapeDtypeStruct((s, LANES), F32))
    res = pl.pallas_call(
        body, name=name, grid=(s // tm,),
        in_specs=[pl.BlockSpec((tm, d), lambda i: (i, 0)), _resident(w.shape, lambda i: (0, 0)), tab, tab, tab],
        out_specs=out_specs, out_shape=out_shape, compiler_params=_cp("parallel"))(xb, w, *tabs)
    return res if has_tail else res[0]


def _rope_cast(parts, tabs, n_rope, name, transposed=()):
    s = tabs[0].shape[0]
    flip = [i in transposed for i in range(len(parts))]
    widths = [p.shape[0] if f else p.shape[1] for p, f in zip(parts, flip)]
    n = sum(widths)
    npart = len(parts)
    ts = _rows(s, 256)

    def body(*refs):
        part_refs = refs[:npart]
        c_ref, sa_ref, sb_ref, o_ref = refs[npart:]
        col = 0
        for ref, w, f in zip(part_refs, widths, flip):
            for j in range(w // LANES):
                if f:
                    t = jnp.transpose(ref[j * LANES:(j + 1) * LANES, :])
                else:
                    t = ref[:, j * LANES:(j + 1) * LANES]
                if col < n_rope:
                    t = _rope(t, c_ref, sa_ref, sb_ref)
                o_ref[:, col * LANES:(col + 1) * LANES] = t.astype(BF)
                col += 1

    tab = pl.BlockSpec((ts, LANES), lambda i: (i, 0))
    return pl.pallas_call(
        body, name=name, grid=(s // ts,),
        in_specs=[pl.BlockSpec((w, ts), lambda i: (0, i)) if f else pl.BlockSpec((ts, w), lambda i: (i, 0))
                  for w, f in zip(widths, flip)] + [tab, tab, tab],
        out_specs=pl.BlockSpec((ts, n), lambda i: (i, 0)),
        out_shape=jax.ShapeDtypeStruct((s, n), BF),
        compiler_params=_cp("parallel"))(*parts, *tabs)


def _head_masks():
    lane = lax.broadcasted_iota(jnp.int32, (1, LANES), 1)
    return [lane < HEAD_DIM, lane >= HEAD_DIM]


def _sel(mask, v):
    return jnp.where(mask, v, jnp.zeros_like(v))


def _pick(mask, wide, fill):
    return jnp.max(jnp.where(mask, wide, fill), axis=1, keepdims=True)


def _head_stack(hm, a, b):
    return jnp.concatenate([_sel(hm[0], a), _sel(hm[0], b), _sel(hm[1], a), _sel(hm[1], b)], axis=0)


def _band_mask_stack(has_prev):
    qi = lax.broadcasted_iota(jnp.int32, (BLOCK, 4 * BLOCK), 0)
    col = lax.broadcasted_iota(jnp.int32, (BLOCK, 4 * BLOCK), 1)
    d = jnp.bitwise_and(col, BLOCK - 1) - qi
    is_prev = jnp.bitwise_and(col, BLOCK) != 0
    return jnp.where(is_prev, d - jnp.where(has_prev, 0, BLOCK), -d) >= 0


class _BandView:
    def __init__(self, s, g):
        self.r = 4 ** g
        self.nl = s // self.r
        self.nblk = self.nl // BLOCK
        self.nsub = min(BAND_SUB, self.nblk)
        self.tile = self.nsub * BLOCK
        self.ncols = min(self.r * GROUP_W // LANES, BAND_COLS)
        self.grid = (self.r * GROUP_W // LANES // self.ncols, self.nblk // self.nsub)

    def view(self, a):
        return a.reshape(self.nl, self.r * a.shape[1])

    def qkv(self, hb, g):
        npair = MIX_W // LANES
        offs = [i * npair + g * GROUP_W // LANES for i in range(3)]
        if self.r == 1:
            return [hb] * 3, hb.shape[1], offs
        return [self.view(hb[:, o * LANES:o * LANES + GROUP_W]) for o in offs], GROUP_W, [0, 0, 0]

    def specs(self, width, off):
        assert off % self.ncols == 0 and (width == GROUP_W or self.r == 1)
        nsub, last, lanes, first = self.nsub, self.nblk - 1, self.ncols * LANES, off // self.ncols
        return (pl.BlockSpec((self.tile, lanes), lambda cg, t: (t, first + cg)),
                pl.BlockSpec((BLOCK, lanes), lambda cg, t: (jnp.maximum(t * nsub - 1, 0), first + cg)),
                pl.BlockSpec((BLOCK, lanes), lambda cg, t: (jnp.minimum(t * nsub + nsub, last), first + cg)))


def _band_fwd(hb, g, name):
    s, n = hb.shape
    bv = _BandView(s, g)
    nsub = bv.nsub
    npair = MIX_W // LANES

    def body(q_ref, kc_ref, kp_ref, vc_ref, vp_ref, o_ref, l_ref):
        t = pl.program_id(1)
        hm = _head_masks()
        for i, c in [(i, c) for i in range(nsub) for c in range(bv.ncols)]:
            rows = slice(i * BLOCK, (i + 1) * BLOCK)
            lanes = slice(c * LANES, (c + 1) * LANES)
            has_prev = t > 0 if i == 0 else True
            q, kc, vc = q_ref[rows, lanes], kc_ref[rows, lanes], vc_ref[rows, lanes]
            if i == 0:
                kp, vp = kp_ref[:, lanes], vp_ref[:, lanes]
            else:
                prev = slice((i - 1) * BLOCK, i * BLOCK)
                kp, vp = kc_ref[prev, lanes], vc_ref[prev, lanes]
            sc = jnp.where(_band_mask_stack(has_prev), _dot_nt(q, _head_stack(hm, kc, kp)) * SCALE, NEG)
            ps, ms, ls = [], [], []
            for h in range(2):
                sh = sc[:, 2 * h * BLOCK:2 * (h + 1) * BLOCK]
                m = jnp.max(sh, axis=1, keepdims=True)
                p = jnp.exp(sh - m)
                ps.append(p.astype(BF))
                ms.append(m)
                ls.append(jnp.sum(p, axis=1, keepdims=True))
            o = _dot(jnp.concatenate(ps, axis=1), _head_stack(hm, vc, vp))
            o_ref[rows, lanes] = o / jnp.where(hm[0], ls[0], ls[1])
            l_ref[rows, lanes] = jnp.where(hm[0], ms[0] + jnp.log(ls[0]), ms[1] + jnp.log(ls[1]))

    (qv, kv_, vv), width, (qo, ko, vo) = bv.qkv(hb, g)
    q_cur, _, _ = bv.specs(width, qo)
    k_cur, k_prv, _ = bv.specs(width, ko)
    v_cur, v_prv, _ = bv.specs(width, vo)
    out_spec = bv.specs(GROUP_W, 0)[0]
    out = jax.ShapeDtypeStruct((bv.nl, bv.r * GROUP_W), F32)
    o, l = pl.pallas_call(
        body, name=name, grid=bv.grid,
        in_specs=[q_cur, k_cur, k_prv, v_cur, v_prv], out_specs=[out_spec, out_spec], out_shape=[out, out],
        compiler_params=_cp("parallel", "parallel"))(qv, kv_, kv_, vv, vv)
    return o.reshape(s, GROUP_W), l.reshape(s, GROUP_W)


def _band_combine(os, ls, name):
    ng = len(os)
    s, w = os[0].shape
    ts = _rows(s)

    def body(*refs):
        o_refs, l_refs = refs[:ng], refs[ng:2 * ng]
        oa_ref, lt_ref = refs[2 * ng:]
        lv = [r[...] for r in l_refs]
        m = functools.reduce(jnp.maximum, lv)
        es = [jnp.exp(l - m) for l in lv]
        den = functools.reduce(lambda a, b: a + b, es)
        num = functools.reduce(lambda a, b: a + b, [es[g] * o_refs[g][...] for g in range(ng)])
        oa_ref[...] = (num / den).astype(BF)
        lt_ref[...] = m + jnp.log(den)

    blk = pl.BlockSpec((ts, w), lambda i: (i, 0))
    return pl.pallas_call(
        body, name=name, grid=(s // ts,), in_specs=[blk] * (2 * ng), out_specs=[blk, blk],
        out_shape=[jax.ShapeDtypeStruct((s, w), BF), jax.ShapeDtypeStruct((s, w), F32)],
        compiler_params=_cp("parallel"))(*os, *ls)


def _band_bwd(hb, dcat, oa, lt, g, name):
    s, n = hb.shape
    bv = _BandView(s, g)
    nsub = bv.nsub
    npair = MIX_W // LANES
    ntile = bv.grid[1]

    def body(q_ref, qn_ref, kc_ref, kp_ref, vc_ref, vp_ref, do_ref, don_ref, oa_ref, oan_ref, lt_ref, ltn_ref,
             dq_ref, dk_ref, dv_ref):
        t = pl.program_id(1)
        hm = _head_masks()

        for i, c in [(i, c) for i in range(nsub) for c in range(bv.ncols)]:
            lanes = slice(c * LANES, (c + 1) * LANES)

            def block(ref, edge_ref, i, lanes=lanes):
                if i < 0 or i >= nsub:
                    return edge_ref[:, lanes]
                return ref[i * BLOCK:(i + 1) * BLOCK, lanes]

            has_prev = t > 0 if i == 0 else True
            has_next = t < ntile - 1 if i == nsub - 1 else True
            q, qn = block(q_ref, None, i), block(q_ref, qn_ref, i + 1)
            kc, kp = block(kc_ref, None, i), block(kc_ref, kp_ref, i - 1)
            vc, vp = block(vc_ref, None, i), block(vc_ref, vp_ref, i - 1)
            do, don = block(do_ref, None, i), block(do_ref, don_ref, i + 1)
            dd = do.astype(F32) * block(oa_ref, None, i).astype(F32)
            ddn = don.astype(F32) * block(oa_ref, oan_ref, i + 1).astype(F32)
            lt, ltn = block(lt_ref, None, i), block(lt_ref, ltn_ref, i + 1)

            def per_head(wide, width):
                col = lax.broadcasted_iota(jnp.int32, (BLOCK, 2 * width), 1)
                return jnp.where(col < width, _pick(hm[0], wide, NEG), _pick(hm[1], wide, NEG))

            def row_sums(prod, width):
                col = lax.broadcasted_iota(jnp.int32, (BLOCK, 2 * width), 1)
                return jnp.where(col < width, jnp.sum(_sel(hm[0], prod), axis=1, keepdims=True),
                                 jnp.sum(_sel(hm[1], prod), axis=1, keepdims=True))

            kst, vst = _head_stack(hm, kc, kp), _head_stack(hm, vc, vp)
            p = jnp.exp(jnp.where(_band_mask_stack(has_prev), _dot_nt(q, kst) * SCALE, NEG)
                        - per_head(lt, 2 * BLOCK))
            ds = p * (_dot_nt(do, vst) - row_sums(dd, 2 * BLOCK))
            dq_ref[i * BLOCK:(i + 1) * BLOCK, lanes] = SCALE * _dot(ds.astype(BF), kst)
            kcs = jnp.concatenate([_sel(hm[0], kc), _sel(hm[1], kc)], axis=0)
            vcs = jnp.concatenate([_sel(hm[0], vc), _sel(hm[1], vc)], axis=0)
            qi_ = lax.broadcasted_iota(jnp.int32, (BLOCK, 2 * BLOCK), 0)
            kj_ = jnp.bitwise_and(lax.broadcasted_iota(jnp.int32, (BLOCK, 2 * BLOCK), 1), BLOCK - 1)
            mn = kj_ >= qi_ + jnp.where(has_next, 0, BLOCK)
            pn = jnp.exp(jnp.where(mn, _dot_nt(qn, kcs) * SCALE, NEG) - per_head(ltn, BLOCK))
            dsn = pn * (_dot_nt(don, vcs) - row_sums(ddn, BLOCK))
            pb, dsb, pnb, dsnb = p.astype(BF), ds.astype(BF), pn.astype(BF), dsn.astype(BF)

            def own(x, h):
                return x[:, 2 * h * BLOCK:(2 * h + 1) * BLOCK]

            def nxt(x, h):
                return x[:, h * BLOCK:(h + 1) * BLOCK]

            ds_rows = jnp.concatenate([own(dsb, 0), nxt(dsnb, 0), own(dsb, 1), nxt(dsnb, 1)], axis=0)
            p_rows = jnp.concatenate([own(pb, 0), nxt(pnb, 0), own(pb, 1), nxt(pnb, 1)], axis=0)
            dk_ref[i * BLOCK:(i + 1) * BLOCK, lanes] = SCALE * _dot_tn(ds_rows, _head_stack(hm, q, qn))
            dv_ref[i * BLOCK:(i + 1) * BLOCK, lanes] = _dot_tn(p_rows, _head_stack(hm, do, don))

    (qv, kv_, vv), width, (qo, ko, vo) = bv.qkv(hb, g)
    q_cur, _, q_nxt = bv.specs(width, qo)
    k_cur, k_prv, _ = bv.specs(width, ko)
    v_cur, v_prv, _ = bv.specs(width, vo)
    w_cur, _, w_nxt = bv.specs(GROUP_W, 0)
    out = jax.ShapeDtypeStruct((bv.nl, bv.r * GROUP_W), F32)
    dv_, ov, lv = bv.view(dcat[:, :GROUP_W]), bv.view(oa), bv.view(lt)
    res = pl.pallas_call(
        body, name=name, grid=bv.grid,
        in_specs=[q_cur, q_nxt, k_cur, k_prv, v_cur, v_prv, w_cur, w_nxt, w_cur, w_nxt, w_cur, w_nxt],
        out_specs=[w_cur, w_cur, w_cur], out_shape=[out, out, out],
        compiler_params=_cp("parallel", "parallel"))(
            qv, qv, kv_, kv_, vv, vv, dv_, dv_, ov, ov, lv, lv)
    return [t.reshape(s, GROUP_W) for t in res]


def _mem_fwd(hb, q_blk0, kv, name):
    s = hb.shape[0]
    m = kv.shape[0]
    tq = _rows(s)
    npair = MEM_W // LANES

    def body(q_ref, k_ref, v_ref, o_ref, l_ref):
        q, k, v = q_ref[...], k_ref[...], v_ref[...]
        hm = _head_masks()
        o = jnp.zeros((tq, LANES), F32)
        lse_w = jnp.zeros((tq, LANES), F32)
        for h in range(2):
            sc = _dot_nt(_sel(hm[h], q), k) * SCALE
            mx = jnp.max(sc, axis=1, keepdims=True)
            p = jnp.exp(sc - mx)
            l = jnp.sum(p, axis=1, keepdims=True)
            o = o + _dot(p.astype(BF), _sel(hm[h], v)) / l
            lse_w = jnp.where(hm[h], mx + jnp.log(l), lse_w)
        o_ref[...] = o.astype(BF)
        l_ref[...] = lse_w

    blk = pl.BlockSpec((tq, LANES), lambda p, i: (i, p))
    return pl.pallas_call(
        body, name=name, grid=(npair, s // tq),
        in_specs=[pl.BlockSpec((tq, LANES), lambda p, i: (i, q_blk0 + p)),
                  pl.BlockSpec((m, LANES), lambda p, i: (0, p)),
                  pl.BlockSpec((m, LANES), lambda p, i: (0, npair + p))],
        out_specs=[blk, blk],
        out_shape=[jax.ShapeDtypeStruct((s, MEM_W), BF), jax.ShapeDtypeStruct((s, MEM_W), F32)],
        compiler_params=_cp("parallel", "parallel"))(hb, kv, kv)


def _mem_bwd(hb, q_blk0, kv, dcat, cat, o_blk0, lse, name):
    s = hb.shape[0]
    m = kv.shape[0]
    tq = _rows(s)
    npair = MEM_W // LANES

    def body(q_ref, k_ref, v_ref, do_ref, o_ref, l_ref, dq_ref, dk_ref, dv_ref):
        i = pl.program_id(1)

        @pl.when(i == 0)
        def _():
            dk_ref[...] = jnp.zeros_like(dk_ref)
            dv_ref[...] = jnp.zeros_like(dv_ref)

        q, k, v, do = q_ref[...], k_ref[...], v_ref[...], do_ref[...]
        dd = do.astype(F32) * o_ref[...].astype(F32)
        lt = l_ref[...]
        hm = _head_masks()
        dq = jnp.zeros((tq, LANES), F32)
        dk = jnp.zeros((m, LANES), F32)
        dv = jnp.zeros((m, LANES), F32)
        for h in range(2):
            qh, doh = _sel(hm[h], q), _sel(hm[h], do)
            p = jnp.exp(_dot_nt(qh, k) * SCALE - _pick(hm[h], lt, NEG))
            ds = p * (_dot_nt(doh, v) - jnp.sum(_sel(hm[h], dd), axis=1, keepdims=True))
            dq = dq + SCALE * _dot(ds.astype(BF), _sel(hm[h], k))
            dk = dk + SCALE * _dot_tn(ds.astype(BF), qh)
            dv = dv + _dot_tn(p.astype(BF), doh)
        dq_ref[...] = dq
        dk_ref[...] += dk
        dv_ref[...] += dv

    row = pl.BlockSpec((tq, LANES), lambda p, i: (i, p))
    orow = pl.BlockSpec((tq, LANES), lambda p, i: (i, o_blk0 + p))
    acc = pl.BlockSpec((m, LANES), lambda p, i: (0, p))
    return pl.pallas_call(
        body, name=name, grid=(npair, s // tq),
        in_specs=[pl.BlockSpec((tq, LANES), lambda p, i: (i, q_blk0 + p)),
                  pl.BlockSpec((m, LANES), lambda p, i: (0, p)),
                  pl.BlockSpec((m, LANES), lambda p, i: (0, npair + p)), orow, orow, row],
        out_specs=[row, acc, acc],
        out_shape=[jax.ShapeDtypeStruct((s, MEM_W), F32), jax.ShapeDtypeStruct((m, MEM_W), F32),
                   jax.ShapeDtypeStruct((m, MEM_W), F32)],
        compiler_params=_cp("parallel", "arbitrary"))(hb, kv, kv, dcat, cat, lse)


def _gate_fwd(f_t, bias, name):
    hp, s = f_t.shape
    nblk = s // LANES

    def body(f_ref, b_ref, c_ref):
        lane = lax.broadcasted_iota(jnp.int32, (hp, LANES), 1)

        def step(i, carry):
            off = pl.multiple_of(i * LANES, LANES)
            x = f_ref[:, pl.ds(off, LANES)] + b_ref[...]
            acc = jnp.minimum(x, 0.0) - jnp.log(1.0 + jnp.exp(-jnp.abs(x)))
            sh = 1
            while sh < LANES:
                acc = acc + jnp.where(lane >= sh, pltpu.roll(acc, sh, 1), 0.0)
                sh *= 2
            acc = acc + carry
            c_ref[:, pl.ds(off, LANES)] = acc
            return acc[:, LANES - 1:LANES]

        lax.fori_loop(0, nblk, step, jnp.zeros((hp, 1), F32))

    vm = pl.BlockSpec(memory_space=pltpu.VMEM)
    return pl.pallas_call(body, name=name, in_specs=[vm, vm], out_specs=vm,
                          out_shape=jax.ShapeDtypeStruct((hp, s), F32),
                          compiler_params=pltpu.CompilerParams(vmem_limit_bytes=VMEM_LIMIT))(f_t, bias)


def _gate_bwd(dc_t, f_t, bias, name):
    hp, s = f_t.shape
    nblk = s // LANES

    def body(dc_ref, f_ref, b_ref, df_ref, db_ref):
        lane = lax.broadcasted_iota(jnp.int32, (hp, LANES), 1)

        def step(t, carry):
            suffix, dbias = carry
            off = pl.multiple_of((nblk - 1 - t) * LANES, LANES)
            acc = dc_ref[:, pl.ds(off, LANES)]
            sh = 1
            while sh < LANES:
                acc = acc + jnp.where(lane < LANES - sh, pltpu.roll(acc, LANES - sh, 1), 0.0)
                sh *= 2
            acc = acc + suffix
            x = f_ref[:, pl.ds(off, LANES)] + b_ref[...]
            df = acc * _sigmoid(-x)
            df_ref[:, pl.ds(off, LANES)] = df
            return acc[:, 0:1], dbias + jnp.sum(df, axis=1, keepdims=True)

        _, dbias = lax.fori_loop(0, nblk, step, (jnp.zeros((hp, 1), F32), jnp.zeros((hp, 1), F32)))
        db_ref[...] = dbias

    vm = pl.BlockSpec(memory_space=pltpu.VMEM)
    return pl.pallas_call(body, name=name, in_specs=[vm, vm, vm], out_specs=[vm, vm],
                          out_shape=[jax.ShapeDtypeStruct((hp, s), F32), jax.ShapeDtypeStruct((hp, 1), F32)],
                          compiler_params=pltpu.CompilerParams(vmem_limit_bytes=VMEM_LIMIT))(dc_t, f_t, bias)


def _wide(rep, width):
    return jnp.tile(rep, (1, width // LANES))


def _fold(t):
    part = t[:, :LANES]
    for c in range(1, t.shape[1] // LANES):
        part = part + t[:, c * LANES:(c + 1) * LANES]
    return part


def _foxt_logits(q, k, cq_row, ck_rep, mask, hmask):
    s = _dot_nt(_sel(hmask, k), q) + (cq_row - _wide(ck_rep, q.shape[0]))
    if mask is not None:
        s = jnp.where(mask, s, NEG)
    return s


def _causal_sub(ks, qs):
    shape = (ks.stop - ks.start, qs.stop - qs.start)
    return (ks.start + lax.broadcasted_iota(jnp.int32, shape, 0)
            <= qs.start + lax.broadcasted_iota(jnp.int32, shape, 1))


def _diag_blocks(t):
    h = t // 2
    return [(slice(0, h), slice(0, t)), (slice(h, t), slice(h, t))]


FOX_SPLIT = 1


def _fox_tiles(s):
    tq = _rows(s, 1024)
    return tq, tq // FOX_SPLIT, s // tq


def _fox_steps(nq):
    return FOX_SPLIT * nq * (nq + 1) // 2


def _count_ge(t, bounds):
    return sum([(t >= b).astype(jnp.int32) for b in bounds], jnp.int32(0))


def _sweep_q_major(t, nq):
    qi = _count_ge(t, [FOX_SPLIT * r * (r + 1) // 2 for r in range(1, nq)])
    return qi, t - FOX_SPLIT * qi * (qi + 1) // 2


def _sweep_k_major(t, nq):
    counts = [nq - j // FOX_SPLIT for j in range(FOX_SPLIT * nq)]
    offs = [sum(counts[:j]) for j in range(1, FOX_SPLIT * nq)]
    kj = _count_ge(t, offs)
    start = sum([jnp.where(t >= o, c, 0) for o, c in zip(offs, counts)], jnp.int32(0))
    qi = kj // FOX_SPLIT + (t - start)
    return kj, qi, t == start, qi == nq - 1


def _foxt_fwd(hb, c_rep, c_t3, name):
    s = hb.shape[0]
    npair = MIX_W // LANES
    tq, tk, nq = _fox_tiles(s)

    def body(q_ref, k_ref, v_ref, cq_ref, ck_ref, o_ref, l_ref, m_s, l_s, acc):
        qi, kj = _sweep_q_major(pl.program_id(1), nq)
        hm = _head_masks()

        @pl.when(kj == 0)
        def _():
            m_s[...] = jnp.full_like(m_s, NEG)
            l_s[...] = jnp.zeros_like(l_s)
            acc[...] = jnp.zeros_like(acc)

        def step(ks, qs, masked):
            q, k = q_ref[qs, :] * SCALE, k_ref[ks, :]
            vt = jnp.transpose(v_ref[ks, :])
            cq = cq_ref[:, qs]
            mask = _causal_sub(ks, qs) if masked else None
            for h in range(2):
                st = _foxt_logits(q, k, cq[h:h + 1, :], ck_ref[h, ks, :], mask, hm[h])
                m_old = m_s[h, :, qs]
                m_new = jnp.maximum(m_old, jnp.max(st, axis=0, keepdims=True))
                pt = jnp.exp(st - m_new)
                corr = jnp.exp(m_old - m_new)
                l_s[h, :, qs] = l_s[h, :, qs] * corr + jnp.sum(pt, axis=0, keepdims=True)
                acc[h, :, qs] = acc[h, :, qs] * corr + _dot(vt[h * HEAD_DIM:(h + 1) * HEAD_DIM, :], pt.astype(BF))
                m_s[h, :, qs] = m_new

        @pl.when(kj < qi)
        def _():
            step(slice(0, tk), slice(0, tq), False)

        @pl.when(kj == qi)
        def _():
            for ks, qs in _diag_blocks(tq):
                step(ks, qs, True)
            outs = []
            for h in range(2):
                outs.append(acc[h] / l_s[h])
                l_ref[h:h + 1, :] = m_s[h] + jnp.log(l_s[h])
            o_ref[...] = jnp.transpose(jnp.concatenate(outs, axis=0)).astype(BF)

    def q_map(p, t):
        return (_sweep_q_major(t, nq)[0], p)

    def kv_map(off):
        return lambda p, t: (_sweep_q_major(t, nq)[1], off + p)

    blk = pl.BlockSpec((tq, LANES), q_map)
    row = pl.BlockSpec((None, 2, tq), lambda p, t: (p, 0, _sweep_q_major(t, nq)[0]))
    return pl.pallas_call(
        body, name=name, grid=(npair, _fox_steps(nq)),
        in_specs=[blk, pl.BlockSpec((tk, LANES), kv_map(npair)), pl.BlockSpec((tk, LANES), kv_map(2 * npair)), row,
                  pl.BlockSpec((2, tk, LANES), lambda p, t: (p, _sweep_q_major(t, nq)[1], 0))],
        out_specs=[blk, row],
        out_shape=[jax.ShapeDtypeStruct((s, MIX_W), BF), jax.ShapeDtypeStruct((npair, 2, s), F32)],
        scratch_shapes=[pltpu.VMEM((2, 1, tq), F32), pltpu.VMEM((2, 1, tq), F32),
                        pltpu.VMEM((2, HEAD_DIM, tq), F32)],
        compiler_params=_cp("parallel", "arbitrary"))(hb, hb, hb, c_t3, c_rep)


def _foxt_dsum(hb, dcat, lse, c_rep, c_t3, name):
    s = hb.shape[0]
    npair = MIX_W // LANES
    tq, tk, nq = _fox_tiles(s)

    def body(q_ref, k_ref, v_ref, do_ref, l_ref, cq_ref, ck_ref, d_ref, acc):
        qi, kj = _sweep_q_major(pl.program_id(1), nq)
        hm = _head_masks()

        @pl.when(kj == 0)
        def _():
            acc[...] = jnp.zeros_like(acc)

        def step(ks, qs, masked):
            q, k, v, do = q_ref[qs, :] * SCALE, k_ref[ks, :], v_ref[ks, :], do_ref[qs, :]
            cq, lse_rows = cq_ref[:, qs], l_ref[:, qs]
            mask = _causal_sub(ks, qs) if masked else None
            for h in range(2):
                pt = jnp.exp(_foxt_logits(q, k, cq[h:h + 1, :], ck_ref[h, ks, :], mask, hm[h])
                             - lse_rows[h:h + 1, :])
                acc[h, :, qs] += jnp.sum(pt * _dot_nt(_sel(hm[h], v), do), axis=0, keepdims=True)

        @pl.when(kj < qi)
        def _():
            step(slice(0, tk), slice(0, tq), False)

        @pl.when(kj == qi)
        def _():
            for ks, qs in _diag_blocks(tq):
                step(ks, qs, True)
            for h in range(2):
                d_ref[h:h + 1, :] = acc[h]

    def q_map(p, t):
        return (_sweep_q_major(t, nq)[0], p)

    def kv_map(off):
        return lambda p, t: (_sweep_q_major(t, nq)[1], off + p)

    blk = pl.BlockSpec((tq, LANES), q_map)
    row = pl.BlockSpec((None, 2, tq), lambda p, t: (p, 0, _sweep_q_major(t, nq)[0]))
    return pl.pallas_call(
        body, name=name, grid=(npair, _fox_steps(nq)),
        in_specs=[blk, pl.BlockSpec((tk, LANES), kv_map(npair)), pl.BlockSpec((tk, LANES), kv_map(2 * npair)),
                  blk, row, row, pl.BlockSpec((2, tk, LANES), lambda p, t: (p, _sweep_q_major(t, nq)[1], 0))],
        out_specs=row, out_shape=jax.ShapeDtypeStruct((npair, 2, s), F32),
        scratch_shapes=[pltpu.VMEM((2, 1, tq), F32)],
        compiler_params=_cp("parallel", "arbitrary"))(hb, hb, hb, dcat, lse, c_t3, c_rep)


def _foxt_bwd(hb, dcat, dsum, lse, c_rep, c_t3, name):
    s = hb.shape[0]
    npair = MIX_W // LANES
    tq, tk, nq = _fox_tiles(s)

    def body(q_ref, k_ref, v_ref, do_ref, d_ref, l_ref, cq_ref, ck_ref, dq_ref, dk_ref, dv_ref, dc_ref, dc_s):
        t = pl.program_id(1)
        kj, qi, first, last = _sweep_k_major(t, nq)
        hm = _head_masks()

        @pl.when(first)
        def _():
            dk_ref[...] = jnp.zeros_like(dk_ref)
            dv_ref[...] = jnp.zeros_like(dv_ref)
            dc_s[...] = jnp.zeros_like(dc_s)

        @pl.when(t == 0)
        def _():
            dq_ref[...] = jnp.zeros_like(dq_ref)

        def step(ks, qs, masked):
            q, k, v, do = q_ref[qs, :] * SCALE, k_ref[ks, :], v_ref[ks, :], do_ref[qs, :]
            qt, kt, dot = jnp.transpose(q), jnp.transpose(k), jnp.transpose(do)
            cq, lse_rows, d_rows = cq_ref[:, qs], l_ref[:, qs], d_ref[:, qs]
            mask = _causal_sub(ks, qs) if masked else None
            dqs, dks, dvs = [], [], []
            for h in range(2):
                rows = slice(h * HEAD_DIM, (h + 1) * HEAD_DIM)
                pt = jnp.exp(_foxt_logits(q, k, cq[h:h + 1, :], ck_ref[h, ks, :], mask, hm[h])
                             - lse_rows[h:h + 1, :])
                dst = pt * (_dot_nt(_sel(hm[h], v), do) - d_rows[h:h + 1, :])
                dsb = dst.astype(BF)
                dqs.append(_dot(kt[rows, :], dsb))
                dks.append(_dot_nt(qt[rows, :], dsb))
                dvs.append(_dot_nt(dot[rows, :], pt.astype(BF)))
                dc_s[h, ks, :] += _fold(dst)
            cols = pl.ds(pl.multiple_of(qi * tq + qs.start, qs.stop - qs.start), qs.stop - qs.start)
            dq_ref[:, cols] += SCALE * jnp.concatenate(dqs, axis=0)
            dk_ref[:, ks] += jnp.concatenate(dks, axis=0)
            dv_ref[:, ks] += jnp.concatenate(dvs, axis=0)

        @pl.when(kj < qi)
        def _():
            step(slice(0, tk), slice(0, tq), False)

        @pl.when(kj == qi)
        def _():
            for ks, qs in _diag_blocks(tq):
                step(ks, qs, True)

        @pl.when(last)
        def _():
            for h in range(2):
                dc_ref[h:h + 1, :] = -jnp.sum(jnp.transpose(dc_s[h]), axis=0, keepdims=True)

    def kj_of(t):
        return _sweep_k_major(t, nq)[0]

    def qi_of(t):
        return _sweep_k_major(t, nq)[1]

    qblk = pl.BlockSpec((tq, LANES), lambda p, t: (qi_of(t), p))
    row = pl.BlockSpec((None, 2, tq), lambda p, t: (p, 0, qi_of(t)))
    kblk = pl.BlockSpec((LANES, tk), lambda p, t: (p, kj_of(t)))
    rep = pl.BlockSpec((2, tk, LANES), lambda p, t: (p, kj_of(t), 0))
    return pl.pallas_call(
        body, name=name, grid=(npair, _fox_steps(nq)),
        in_specs=[qblk,
                  pl.BlockSpec((tk, LANES), lambda p, t: (kj_of(t), npair + p)),
                  pl.BlockSpec((tk, LANES), lambda p, t: (kj_of(t), 2 * npair + p)),
                  qblk, row, row, row, rep],
        out_specs=[pl.BlockSpec((LANES, s), lambda p, t: (p, 0)), kblk, kblk,
                   pl.BlockSpec((None, 2, tk), lambda p, t: (p, 0, kj_of(t)))],
        out_shape=[jax.ShapeDtypeStruct((MIX_W, s), F32), jax.ShapeDtypeStruct((MIX_W, s), F32),
                   jax.ShapeDtypeStruct((MIX_W, s), F32), jax.ShapeDtypeStruct((npair, 2, s), F32)],
        scratch_shapes=[pltpu.VMEM((2, tk, LANES), F32)],
        compiler_params=_cp("arbitrary", "arbitrary"))(hb, hb, hb, dcat, dsum, lse, c_t3, c_rep)


def _adam_rows(r, c):
    cap = max(8, (1 << 20) // (4 * c))
    if r <= cap:
        return r
    best = None
    for t in range(8, cap + 1, 8):
        if r % t == 0:
            best = t
    return best if best is not None else r


def _reduce_adamw(contribs, w, m, v, name):
    nl = len(contribs)
    nd, r, c = contribs[0].shape
    tr = _adam_rows(r, c)
    bc1 = 1.0 - ADAM_B1 ** ADAM_STEP
    bc2 = 1.0 - ADAM_B2 ** ADAM_STEP

    def body(*refs):
        c_refs = refs[:nl]
        w_ref, m_ref, v_ref, g_ref, d_ref, nm_ref, nv_ref = refs[nl:]
        l = pl.program_id(0)
        for li in range(nl):
            @pl.when(l == li)
            def _(c_ref=c_refs[li]):
                g = c_ref[0].astype(F32)
                for k in range(1, nd):
                    g = g + c_ref[k].astype(F32)
                nm = ADAM_B1 * m_ref[...] + (1.0 - ADAM_B1) * g
                nv = ADAM_B2 * v_ref[...] + (1.0 - ADAM_B2) * (g * g)
                g_ref[...] = g
                nm_ref[...] = nm
                nv_ref[...] = nv
                d_ref[...] = -ADAM_LR * ((nm / bc1) / (jnp.sqrt(nv / bc2) + ADAM_EPS) + ADAM_WD * w_ref[...])

    def c_spec(li):
        return pl.BlockSpec((nd, tr, c), lambda l, i: (0, jnp.where(l == li, i, 0), 0))

    blk = pl.BlockSpec((None, tr, c), lambda l, i: (l, i, 0))
    out = jax.ShapeDtypeStruct((nl, r, c), F32)
    return pl.pallas_call(
        body, name=name, grid=(nl, r // tr),
        in_specs=[c_spec(li) for li in range(nl)] + [blk, blk, blk],
        out_specs=[blk, blk, blk, blk], out_shape=[out, out, out, out],
        compiler_params=_cp("arbitrary", "arbitrary"))(*contribs, w, m, v)


def _mesh_pos():
    return lax.axis_index("x"), lax.axis_index("y"), lax.axis_index("c")


def _peer(pos, k):
    x, y, c = pos
    return (1 - x if k & 4 else x, 1 - y if k & 2 else y, 1 - c if k & 1 else c)


def _linear(pos):
    return 4 * pos[0] + 2 * pos[1] + pos[2]


def _xfer_copies(srcs, lands, send_sems, recv_sems, local_sems, gather):
    pos = _mesh_pos()
    me = _linear(pos)
    local, remote = [], []
    for i, (src, land) in enumerate(zip(srcs, lands)):
        local.append(pltpu.make_async_copy(src if gather else src.at[me], land.at[me], local_sems.at[i]))
        for k in range(1, N_DEV):
            peer = _peer(pos, k)
            remote.append(pltpu.make_async_remote_copy(
                src_ref=src if gather else src.at[_linear(peer)], dst_ref=land.at[me],
                send_sem=send_sems.at[i * (N_DEV - 1) + k - 1], recv_sem=recv_sems.at[i * (N_DEV - 1) + k - 1],
                device_id=peer, device_id_type=MESH_ID))
    return local, remote


_HBM = pl.BlockSpec(memory_space=pltpu.HBM)
_SEM = pl.BlockSpec(memory_space=pltpu.SEMAPHORE)
_EFFECT = pltpu.SideEffectType.DATAFLOW_SIDE_EFFECTING


def _xfer_start(srcs, gather, name, after=()):
    n = len(srcs)
    na = len(after)
    lands = [lax.empty(((N_DEV,) + a.shape) if gather else a.shape, a.dtype) for a in srcs]

    def body(*refs):
        src, land = refs[:n], refs[n:2 * n]
        send_sems, recv_sems, local_sems = refs[2 * n + na:2 * n + na + 3]
        local, remote = _xfer_copies(src, land, send_sems, recv_sems, local_sems, gather)
        for cp in local + remote:
            cp.start()
        refs[-1][...] = jnp.zeros_like(refs[-1])

    nsem = n * (N_DEV - 1)
    out = pl.pallas_call(
        body, name=name,
        out_shape=(pltpu.SemaphoreType.DMA((nsem,)), pltpu.SemaphoreType.DMA((nsem,)), pltpu.SemaphoreType.DMA((n,)),
                   *[pltpu.HBM(a.shape, a.dtype) for a in srcs], *[pltpu.HBM(a.shape, a.dtype) for a in lands],
                   jax.ShapeDtypeStruct((8, LANES), F32)),
        in_specs=[_HBM] * (2 * n) + [pl.BlockSpec(memory_space=pl.ANY)] * na,
        out_specs=(_SEM, _SEM, _SEM, *[_HBM] * (2 * n), pl.BlockSpec(memory_space=pltpu.VMEM)),
        input_output_aliases={i: 3 + i for i in range(2 * n)},
        compiler_params=pltpu.CompilerParams(has_side_effects=_EFFECT))(
            *[pltpu.with_memory_space_constraint(a, pltpu.HBM) for a in srcs],
            *[pltpu.with_memory_space_constraint(a, pltpu.HBM) for a in lands], *after)
    return out[:3], list(out[3:3 + n]), list(out[3 + n:3 + 2 * n]), out[-1]


def _started(handle):
    return handle[3]


def _xfer_wait(handle, after, gather, name):
    sems, srcs, lands, _ = handle
    n = len(srcs)

    def body(*refs):
        src, land = refs[:n], refs[n:2 * n]
        send_sems, recv_sems, local_sems = refs[2 * n:2 * n + 3]
        local, remote = _xfer_copies(src, land, send_sems, recv_sems, local_sems, gather)
        for cp in local:
            cp.wait()
        for cp in remote:
            cp.wait_send()
            cp.wait_recv()

    out = pl.pallas_call(
        body, name=name,
        out_shape=(*[pltpu.HBM(a.shape, a.dtype) for a in srcs], *[pltpu.HBM(a.shape, a.dtype) for a in lands]),
        in_specs=[_HBM] * (2 * n) + [_SEM] * 3 + [pl.BlockSpec(memory_space=pl.ANY)] * len(after),
        out_specs=tuple([_HBM] * (2 * n)), input_output_aliases={i: i for i in range(2 * n)},
        compiler_params=pltpu.CompilerParams(has_side_effects=_EFFECT))(*srcs, *lands, *sems, *after)
    return list(out[n:])


def _cols_full(g):
    nd, r, c = g.shape
    return jnp.transpose(g, (1, 0, 2)).reshape(r, nd * c)


def _cols_split(full):
    r, n = full.shape
    return jnp.transpose(full.reshape(r, N_DEV, n // N_DEV), (1, 0, 2))


def _pack_b_in(w):
    qkv = 3 * MIX_W
    pad = jnp.zeros((w.shape[0], B_IN_PAD - w.shape[1]), w.dtype)
    return jnp.concatenate([w[:, :qkv], w[:, qkv + N_MIX_HEADS:], w[:, qkv:qkv + N_MIX_HEADS], pad], axis=1)


def _unpack_b_in(w):
    qkv = 3 * MIX_W
    return jnp.concatenate([w[:, :qkv], w[:, qkv + MEM_W:qkv + MEM_W + N_MIX_HEADS], w[:, qkv:qkv + MEM_W]], axis=1)


def _ffn_forward(x, xb, wgu, get_rest, tag, fused=True, target=None):
    if fused:
        wd4, gain, bias = get_rest(x)
        y, yb, gu, a, xh, rstd = _ffn_fwd_main(x, xb, wgu, wd4, gain, bias, f"{tag}_fwd_main", target)
    else:
        gu, a = _ffn_up(xb, wgu, f"{tag}_up")
        wd4, gain, bias = get_rest(a)
        y, yb, xh, rstd = _mm_res_ln(a, wd4, x, gain, bias, 0.5, f"{tag}_down_ln")
    return y, yb, (xb, gu, a, xh, rstd), wd4


def _ffn_backward(dy, saved, wgu, wd4, gain, tag, after=(), send=None):
    xb, gu, a, xh, rstd = saved
    s = xb.shape[0]
    nd, c, d = wgu.shape
    dx, dzb, dh, dgain, dbias = _ffn_bwd_main(dy, xh, rstd, gain, wd4, wgu, gu, f"{tag}_bwd_main", after,
                                               with_dx=send is None)
    dh = dh.reshape(nd, s, c)
    dwd = _mm_tn(a, dzb[None], f"{tag}_dwd").reshape(nd, wd4.shape[1] // 2, d)
    if send is not None:
        send("down", dwd, dgain, dbias)
    dwgu = _mm_tn(dh, xb[None], f"{tag}_dwgu")
    if send is not None:
        sent = send("gate_up", dwgu)
        dx = _mm_nt(dh, wgu, f"{tag}_dx", res=dx, w_rows_out=False, after=sent)
    return dx, dwgu, dwd, dgain, dbias


def _mixer_a_forward(x, xb, memb, w_in, w_kv, w_out, gain, bias, tabs):
    hb = _proj_rope(xb, w_in, tabs, 2 * MIX_W // LANES, "a_in", True)
    groups = [_band_fwd(hb, g, f"a_band_fwd{g}") for g in range(N_GROUPS)]
    oa, lt = _band_combine([o for o, _ in groups], [l for _, l in groups], "a_combine")
    kv = _mm_nn(memb, w_kv, BF, "a_mem_kv")
    om, lm = _mem_fwd(hb, 3 * MIX_W // LANES, kv, "a_mem_fwd")
    cat = jnp.concatenate([oa, om], axis=1)
    y, yb, xh, rstd = _mm_res_ln(cat[None], w_out[None], x, gain, bias, 1.0, "a_out_ln")
    return y, yb, (xb, hb, oa, lt, kv, lm, cat, xh, rstd)


def _mixer_a_backward(dy, saved, memb, w_in, w_kv, w_out, gain, tabs_neg, after=()):
    xb, hb, oa, lt, kv, lm, cat, xh, rstd = saved
    dz, dzb, dcat, dgain, dbias = _ln_bwd_proj(dy, xh, rstd, gain, w_out, "a_ln_bwd", after)
    dw_out = _mm_tn(cat[None], dzb[None], "a_dwout")[0]
    dqm, dkm, dvm = _mem_bwd(hb, 3 * MIX_W // LANES, kv, dcat, cat, GROUP_W // LANES, lm, "a_mem_bwd")
    dkv = jnp.concatenate([dkm, dvm], axis=1).astype(BF)
    dw_kv = _mm_tn(memb[None], dkv[None], "a_dwkv")[0]
    grads = [_band_bwd(hb, dcat, oa, lt, g, f"a_band_bwd{g}") for g in range(N_GROUPS)]
    dhb = _rope_cast([grads[g][i] for i in range(3) for g in range(N_GROUPS)] + [dqm], tabs_neg,
                     2 * MIX_W // LANES, "a_rope_bwd")
    dw_in = _mm_tn(dhb[None], xb[None], "a_dwin")[0]
    dx = _mm_nt(dhb[None], w_in[None], "a_dx", res=dz, w_rows_out=False)
    return dx, dw_in, dw_kv, dw_out, dgain, dbias


def _pad_rows(t, rows):
    return jnp.concatenate([t, jnp.zeros((rows - t.shape[0], t.shape[1]), t.dtype)], axis=0)


def _pad_cols(t, cols):
    return jnp.concatenate([t, jnp.zeros((t.shape[0], cols - t.shape[1]), t.dtype)], axis=1)


def _mixer_b_forward(x, xb, memb, w_in, fbias, w_kv, w_out, gain, bias, tabs):
    s = x.shape[0]
    hb, f = _proj_rope(xb, w_in, tabs, 0, "b_in", False, tail_block=(3 * MIX_W + MEM_W) // LANES)
    f_t = _pad_rows(jnp.transpose(f[:, :N_MIX_HEADS]), 16)
    bias16 = _pad_rows(jnp.transpose(fbias), 16)
    c_t = _gate_fwd(f_t, bias16, "b_gate_fwd")
    c_t3 = c_t[:N_MIX_HEADS].reshape(N_MIX_HEADS // 2, 2, s)
    c_rep = jnp.broadcast_to(c_t[:N_MIX_HEADS, :, None], (N_MIX_HEADS, s, LANES))
    ob, lb = _foxt_fwd(hb, c_rep, c_t3, "b_fox_fwd")
    kv = _mm_nn(memb, w_kv, BF, "b_mem_kv")
    om, lm = _mem_fwd(hb, 3 * MIX_W // LANES, kv, "b_mem_fwd")
    cat = jnp.concatenate([ob, om], axis=1)
    y, yb, xh, rstd = _mm_res_ln(cat[None], w_out[None], x, gain, bias, 1.0, "b_out_ln")
    return y, yb, (xb, hb, f_t, bias16, c_rep, c_t3, lb, kv, lm, cat, xh, rstd)


def _mixer_b_backward(dy, saved, memb, w_in, w_kv, w_out, gain, tabs, after=()):
    xb, hb, f_t, bias16, c_rep, c_t3, lb, kv, lm, cat, xh, rstd = saved
    s = xb.shape[0]
    dz, dzb, dcat, dgain, dbias = _ln_bwd_proj(dy, xh, rstd, gain, w_out, "b_ln_bwd", after)
    dw_out = _mm_tn(cat[None], dzb[None], "b_dwout")[0]
    dqm, dkm, dvm = _mem_bwd(hb, 3 * MIX_W // LANES, kv, dcat, cat, MIX_W // LANES, lm, "b_mem_bwd")
    dkv = jnp.concatenate([dkm, dvm], axis=1).astype(BF)
    dw_kv = _mm_tn(memb[None], dkv[None], "b_dwkv")[0]
    dsum = _foxt_dsum(hb, dcat, lb, c_rep, c_t3, "b_fox_dsum")
    dq, dk, dv, dc3 = _foxt_bwd(hb, dcat, dsum, lb, c_rep, c_t3, "b_fox_bwd")
    df_t, dfb = _gate_bwd(_pad_rows(dc3.reshape(N_MIX_HEADS, s), 16), f_t, bias16, "b_gate_bwd")
    df = _pad_cols(jnp.transpose(df_t[:N_MIX_HEADS]), B_IN_PAD - 3 * MIX_W - MEM_W)
    dhb = _rope_cast([dq, dk, dv, dqm, df], tabs, 0, "b_cast_bwd", transposed=(0, 1, 2))
    dw_in = _mm_tn(xb[None], dhb[None], "b_dwin")[0]
    dx = _mm_nt(dhb[None], w_in[None], "b_dx", res=dz)
    return dx, dw_in, jnp.transpose(dfb[:N_MIX_HEADS]), dw_kv, dw_out, dgain, dbias


def _stored(t, name):
    return jnp.transpose(t, (0, 2, 1)) if name in ROWS_OUT else t


GATHER_GROUPS = (
    (("ffn1_w_gate_up", 0),),
    (("ffn1_w_down", 0), ("ln_gain", None), ("ln_bias", None)),
    (("a_w_in", 0), ("a_w_out", 0), ("mem_w_kv", 0)),
    (("ffn2_w_gate_up", 0), ("ffn2_w_down", 0)),
    (("ffn1_w_gate_up", 1), ("ffn1_w_down", 1)),
    (("b_w_in", 0), ("b_w_out", 0), ("mem_w_kv", 1)),
    (("ffn2_w_gate_up", 1), ("ffn2_w_down", 1)),
)


def _group_shards(group, params):
    return [t if n in F32_COMM else _stored(t, n)[l].astype(BF) for (n, l), t in zip(group, params)]


def _weight_groups(w):
    return [_group_shards(grp, [w[n] for n, _ in grp]) for grp in GATHER_GROUPS]


def _local_step(x, mem, target, fbias, get_w, put_g):
    s, d = x.shape
    tabs = _rope_tables(s, 1.0)
    tabs_neg = _rope_tables(s, -1.0)
    memb = mem.astype(BF)
    saved, wl = [], []
    cur, curb = x, x.astype(BF)
    ln = []

    def down4(t):
        return t.reshape(N_DEV // 2, -1, d)

    for i in range(DEPTH):
        if i == 0:
            def first_rest(a):
                g = get_w(1, a)
                ln.extend(jnp.transpose(t, (1, 2, 0, 3)).reshape(DEPTH, 3, 1, d) for t in g[1:3])
                return down4(g[0]), ln[0][0, 0], ln[1][0, 0]

            wgu = get_w(0, cur)[0]
            cur, curb, s1, wd = _ffn_forward(cur, curb, wgu, first_rest, "l0_ffn1", fused=False)
        else:
            g = get_w(3 * i + 1, cur)
            wgu = g[0]
            cur, curb, s1, wd = _ffn_forward(cur, curb, wgu, lambda a, g=g: (down4(g[1]), ln[0][i, 0], ln[1][i, 0]),
                                             f"l{i}_ffn1")
        w1 = (wgu, wd)
        ln_g, ln_b = ln
        g = get_w(3 * i + 2, cur)
        if i == 0:
            wm = (g[0].reshape(-1, d), g[2].reshape(d, -1), _cols_full(g[1]))
            cur, curb, s2 = _mixer_a_forward(cur, curb, memb, wm[0], wm[1], wm[2], ln_g[i, 1], ln_b[i, 1], tabs)
        else:
            wm = (_pack_b_in(g[0].reshape(d, -1)), g[2].reshape(d, -1), g[1].reshape(d, -1))
            cur, curb, s2 = _mixer_b_forward(cur, curb, memb, wm[0], fbias, wm[1], wm[2], ln_g[i, 1], ln_b[i, 1],
                                             tabs)
        g = get_w(3 * i + 3, cur)
        cur, curb, s3, wd = _ffn_forward(cur, curb, g[0], lambda a, g=g: (down4(g[1]), ln_g[i, 2], ln_b[i, 2]),
                                         f"l{i}_ffn2", target=target if i == DEPTH - 1 else None)
        w3 = (g[0], wd)
        saved.append((s1, s2, s3))
        wl.append((w1, wm, w3))

    dy, loss = cur, curb

    dgs = [[None] * 3 for _ in range(DEPTH)]
    dbs = [[None] * 3 for _ in range(DEPTH)]
    sent = ()
    for i in reversed(range(DEPTH)):
        s1, s2, s3 = saved[i]
        w1, wm, w3 = wl[i]
        dy, dgu, dd, dgs[i][2], dbs[i][2] = _ffn_backward(dy, s3, w3[0], w3[1], ln_g[i, 2], f"l{i}_ffn2", sent)
        sent = put_g(3 * i + 2, [dgu, dd])
        if i == 0:
            dy, dw_in, dw_kv, dw_out, dgs[i][1], dbs[i][1] = _mixer_a_backward(
                dy, s2, memb, wm[0], wm[1], wm[2], ln_g[i, 1], tabs_neg, sent)
            sent = put_g(1, [dw_in.reshape(N_DEV, -1, d), _cols_split(dw_out),
                             dw_kv.reshape(N_DEV, d // N_DEV, -1)])
        else:
            dy, dw_in, dfb, dw_kv, dw_out, dgs[i][1], dbs[i][1] = _mixer_b_backward(
                dy, s2, memb, wm[0], wm[1], wm[2], ln_g[i, 1], tabs, sent)
            sent = put_g(4, [_unpack_b_in(dw_in).reshape(N_DEV, d // N_DEV, -1),
                             dw_out.reshape(N_DEV, d // N_DEV, -1), dw_kv.reshape(N_DEV, d // N_DEV, -1),
                             jnp.broadcast_to(dfb[None], (N_DEV,) + dfb.shape)])
        if i == 0:
            def send_last(kind, dw, dgain=None, dbias=None):
                if kind == "gate_up":
                    return put_g(6, [dw])
                dgs[0][0], dbs[0][0] = dgain, dbias
                ln_pieces = []
                for parts in (dgs, dbs):
                    t = jnp.concatenate([parts[a][b] for a in range(DEPTH) for b in range(3)], axis=0)
                    ln_pieces.append(jnp.transpose(t.reshape(DEPTH * 3, N_DEV, d // N_DEV), (1, 0, 2)))
                return put_g(0, [dw] + ln_pieces)

            dy = _ffn_backward(dy, s1, w1[0], w1[1], ln_g[i, 0], "l0_ffn1", sent, send_last)[0]
        else:
            dy, dgu, dd, dgs[i][0], dbs[i][0] = _ffn_backward(dy, s1, w1[0], w1[1], ln_g[i, 0], f"l{i}_ffn1", sent)
            sent = put_g(3, [dgu, dd])
    return loss, dy


WEIGHTS = ("ffn1_w_gate_up", "ffn1_w_down", "ffn2_w_gate_up", "ffn2_w_down", "ln_gain", "ln_bias", "mem_w_kv",
           "a_w_in", "a_w_out", "b_w_in", "b_forget_bias", "b_w_out")
F32_COMM = ("ln_gain", "ln_bias", "b_forget_bias")
ROWS_OUT = ("ffn1_w_gate_up", "ffn2_w_gate_up", "a_w_in")
GRAD_SLOTS = {
    "ffn1_w_gate_up": [(6, 0), (3, 0)], "ffn1_w_down": [(0, 0), (3, 1)],
    "ffn2_w_gate_up": [(2, 0), (5, 0)], "ffn2_w_down": [(2, 1), (5, 1)],
    "ln_gain": [(0, 1)], "ln_bias": [(0, 2)], "mem_w_kv": [(1, 2), (4, 2)],
    "a_w_in": [(1, 0)], "a_w_out": [(1, 1)], "b_w_in": [(4, 0)], "b_forget_bias": [(4, 3)], "b_w_out": [(4, 1)],
}


def kernel(x, mem, ffn1_w_gate_up, ffn1_w_down, ffn2_w_gate_up, ffn2_w_down, ln_gain, ln_bias, mem_w_kv, a_w_in, a_w_out, b_w_in, b_forget_bias, b_w_out, loss_target, m_ffn1_w_gate_up, m_ffn1_w_down, m_ffn2_w_gate_up, m_ffn2_w_down, m_ln_gain, m_ln_bias, m_mem_w_kv, m_a_w_in, m_a_w_out, m_b_w_in, m_b_forget_bias, m_b_w_out, v_ffn1_w_gate_up, v_ffn1_w_down, v_ffn2_w_gate_up, v_ffn2_w_down, v_ln_gain, v_ln_bias, v_mem_w_kv, v_a_w_in, v_a_w_out, v_b_w_in, v_b_forget_bias, v_b_w_out):
    w = dict(zip(WEIGHTS, (ffn1_w_gate_up, ffn1_w_down, ffn2_w_gate_up, ffn2_w_down, ln_gain, ln_bias, mem_w_kv,
                           a_w_in, a_w_out, b_w_in, b_forget_bias, b_w_out)))
    m = dict(zip(WEIGHTS, (m_ffn1_w_gate_up, m_ffn1_w_down, m_ffn2_w_gate_up, m_ffn2_w_down, m_ln_gain, m_ln_bias,
                           m_mem_w_kv, m_a_w_in, m_a_w_out, m_b_w_in, m_b_forget_bias, m_b_w_out)))
    v = dict(zip(WEIGHTS, (v_ffn1_w_gate_up, v_ffn1_w_down, v_ffn2_w_gate_up, v_ffn2_w_down, v_ln_gain, v_ln_bias,
                           v_mem_w_kv, v_a_w_in, v_a_w_out, v_b_w_in, v_b_forget_bias, v_b_w_out)))

    gathers = []
    for k, grp in enumerate(GATHER_GROUPS):
        params, behind = [w[n] for n, _ in grp], [_started(h) for h in gathers[-1:]]
        if behind:
            params, behind = lax.optimization_barrier((params, behind))
        gathers.append(_xfer_start(_group_shards(grp, params), True, f"gather{k}_start", behind))
    exchanges = {}

    def get_w(k, after):
        behind = [after] + ([_started(h) for h in gathers] if k == 0 else [])
        return _xfer_wait(gathers[k], behind, True, f"gather{k}_wait")

    def put_g(k, pieces):
        behind = [_started(exchanges[0])] if k == 6 else []
        exchanges[k] = _xfer_start(pieces, False, f"grads{k}_start", behind)
        return (_started(exchanges[k]),)

    loss, grad_x = _local_step(x[0], mem[0], loss_target[0], b_forget_bias, get_w, put_g)
    loss = lax.psum(loss[0, 0], ("x", "y", "c"))

    outs, landed = {}, {}

    def adamw(names):
        for n in names:
            contribs = [landed[g][j] for g, j in GRAD_SLOTS[n]]
            view = (len(contribs),) + contribs[0].shape[1:]
            shape = _stored(w[n], n).shape
            res = _reduce_adamw(contribs, *[_stored(t[n], n).reshape(view) for t in (w, m, v)], f"adamw_{n}")
            outs[n] = [_stored(t.reshape(shape), n) for t in res]
        return [outs[n][3] for n in names]

    after = [grad_x]
    for k in (5, 4, 3, 2, 1):
        landed[k] = _xfer_wait(exchanges[k], after, False, f"grads{k}_wait")
        after = [landed[k][0]]
    done = adamw(("ffn2_w_gate_up", "ffn2_w_down", "mem_w_kv", "a_w_in", "a_w_out", "b_w_in", "b_forget_bias",
                  "b_w_out"))
    landed[0] = _xfer_wait(exchanges[0], done, False, "grads0_wait")
    done = adamw(("ffn1_w_down", "ln_gain", "ln_bias"))
    landed[6] = _xfer_wait(exchanges[6], done, False, "grads6_wait")
    adamw(("ffn1_w_gate_up",))
    return (loss, grad_x[None], *[outs[n][0] for n in WEIGHTS], *[outs[n][1] for n in WEIGHTS],
            *[outs[n][2] for n in WEIGHTS], *[outs[n][3] for n in WEIGHTS])
```

```python
import functools

import jax
import jax.numpy as jnp
from jax import lax
from jax.experimental import pallas as pl
from jax.experimental.pallas import tpu as pltpu

F32 = jnp.float32
BF = jnp.bfloat16
MESH_ID = pl.DeviceIdType.MESH

N_DEV = 8
DEPTH = 2
HEAD_DIM = 64
LANES = 128
N_MIX_HEADS = 12
N_MEM_HEADS = 4
MIX_W = N_MIX_HEADS * HEAD_DIM
MEM_W = N_MEM_HEADS * HEAD_DIM
N_GROUPS = 3
GROUP_W = MIX_W // N_GROUPS
BLOCK = 128
BAND_SUB = 4
BAND_COLS = 4
ROT_HALF = 8
ROPE_THETA = 500000.0
ALPHA = (2 * DEPTH) ** 0.25
LN_EPS = 1e-5
SCALE = HEAD_DIM ** -0.5
NEG = -1e30
B_IN_PAD = 2688
ADAM_LR, ADAM_B1, ADAM_B2, ADAM_EPS, ADAM_WD, ADAM_STEP = 0.001, 0.9, 0.999, 1e-08, 0.01, 10
VMEM_LIMIT = 56 * 1024 * 1024


def _cp(*sem):
    return pltpu.CompilerParams(dimension_semantics=sem, vmem_limit_bytes=VMEM_LIMIT)


def _dot(a, b):
    return jnp.dot(a, b, preferred_element_type=F32)


def _dot_nt(a, b):
    return lax.dot_general(a, b, (((1,), (1,)), ((), ())), preferred_element_type=F32)


def _dot_tn(a, b):
    return lax.dot_general(a, b, (((0,), (0,)), ((), ())), preferred_element_type=F32)


def _sigmoid(x):
    return 1.0 / (1.0 + jnp.exp(-x))


def _tile(n, cap=1024):
    if n <= cap:
        return n
    best = LANES
    for t in range(LANES, cap + 1, LANES):
        if n % t == 0:
            best = t
    return best


def _rows(s, cap=512):
    return s if s <= cap else cap


def _mm_nn(a, b, out_dtype, name, b_rows_out=False):
    m, k = a.shape
    n = b.shape[0] if b_rows_out else b.shape[1]
    tm, tn = _rows(m), _tile(n)

    def body(a_ref, b_ref, o_ref):
        prod = _dot_nt(a_ref[...], b_ref[...]) if b_rows_out else _dot(a_ref[...], b_ref[...])
        o_ref[...] = prod.astype(o_ref.dtype)

    b_spec = (pl.BlockSpec((tn, k), lambda j, i: (j, 0)) if b_rows_out
              else pl.BlockSpec((k, tn), lambda j, i: (0, j)))
    return pl.pallas_call(
        body, name=name, grid=(n // tn, m // tm),
        in_specs=[pl.BlockSpec((tm, k), lambda j, i: (i, 0)), b_spec],
        out_specs=pl.BlockSpec((tm, tn), lambda j, i: (i, j)),
        out_shape=jax.ShapeDtypeStruct((m, n), out_dtype),
        compiler_params=_cp("parallel", "parallel"))(a, b)


def _resident(shape, index_map):
    return pl.BlockSpec(shape, index_map, pipeline_mode=pl.Buffered(1))


def _mm_tn(a, b, name, out_dtype=BF):
    na, s, m = a.shape
    nb, _, n = b.shape
    no = max(na, nb)
    tm, tn = _tile(m), _tile(n)

    def body(a_ref, b_ref, o_ref):
        o_ref[...] = _dot_tn(a_ref[...], b_ref[...]).astype(o_ref.dtype)

    def spec(nbatch, width, tile, index_map):
        fixed = nbatch == 1 and width == tile
        return _resident((None, s, tile), index_map) if fixed else pl.BlockSpec((None, s, tile), index_map)

    return pl.pallas_call(
        body, name=name, grid=(no, m // tm, n // tn),
        in_specs=[spec(na, m, tm, lambda j, r, c: (j if na > 1 else 0, 0, r)),
                  spec(nb, n, tn, lambda j, r, c: (j if nb > 1 else 0, 0, c))],
        out_specs=pl.BlockSpec((None, tm, tn), lambda j, r, c: (j, r, c)),
        out_shape=jax.ShapeDtypeStruct((no, m, n), out_dtype),
        compiler_params=_cp("parallel", "parallel", "parallel"))(a, b)


def _mm_nt(dh, w, name, res=None, out_dtype=F32, w_rows_out=True, after=()):
    nc, s, kc = dh.shape
    d = w.shape[1] if w_rows_out else w.shape[2]
    ts = _rows(s)
    has_res = res is not None
    mm = _dot_nt if w_rows_out else _dot

    def body(*refs):
        o_ref = refs[-1]
        dh_ref, w_ref = refs[:2]
        if has_res:
            r_ref = refs[2]
        out = mm(dh_ref[0], w_ref[0])
        for j in range(1, nc):
            out = out + mm(dh_ref[j], w_ref[j])
        if has_res:
            out = out + ALPHA * r_ref[...]
        o_ref[...] = out.astype(o_ref.dtype)

    in_specs = [pl.BlockSpec((nc, ts, kc), lambda i: (0, i, 0)), _resident(w.shape, lambda i: (0, 0, 0))]
    args = [dh, w]
    if has_res:
        in_specs.append(pl.BlockSpec((ts, d), lambda i: (i, 0)))
        args.append(res)
    in_specs += [pl.BlockSpec(memory_space=pl.ANY)] * len(after)
    args += list(after)
    return pl.pallas_call(
        body, name=name, grid=(s // ts,), in_specs=in_specs,
        out_specs=pl.BlockSpec((ts, d), lambda i: (i, 0)),
        out_shape=jax.ShapeDtypeStruct((s, d), out_dtype),
        compiler_params=_cp("parallel"))(*args)


def _mm_res_ln(a, w, x, gain, bias, fscale, name):
    nc, s, kc = a.shape
    d = w.shape[2]
    ts = _rows(s)

    def body(a_ref, w_ref, x_ref, g_ref, b_ref, y_ref, yb_ref, xh_ref, r_ref):
        f = _dot(a_ref[0], w_ref[0])
        for j in range(1, nc):
            f = f + _dot(a_ref[j], w_ref[j])
        z = ALPHA * x_ref[...] + fscale * f
        mu = jnp.mean(z, axis=-1, keepdims=True)
        zc = z - mu
        var = jnp.mean(zc * zc, axis=-1, keepdims=True)
        r = lax.rsqrt(var + LN_EPS)
        xh = zc * r
        y = xh * g_ref[...] + b_ref[...]
        y_ref[...] = y
        yb_ref[...] = y.astype(BF)
        xh_ref[...] = xh
        r_ref[...] = r

    row = pl.BlockSpec((ts, d), lambda i: (i, 0))
    vec = pl.BlockSpec((1, d), lambda i: (0, 0))
    return pl.pallas_call(
        body, name=name, grid=(s // ts,),
        in_specs=[pl.BlockSpec((nc, ts, kc), lambda i: (0, i, 0)), _resident((nc, kc, d), lambda i: (0, 0, 0)),
                  row, vec, vec],
        out_specs=[row, row, row, pl.BlockSpec((ts, 1), lambda i: (i, 0))],
        out_shape=[jax.ShapeDtypeStruct((s, d), F32), jax.ShapeDtypeStruct((s, d), BF),
                   jax.ShapeDtypeStruct((s, d), F32), jax.ShapeDtypeStruct((s, 1), F32)],
        compiler_params=_cp("parallel"))(a, w, x, gain, bias)


def _ln_bwd_proj(dy, xh, rstd, gain, w_out, name, after=()):
    s, d = dy.shape
    wc = w_out.shape[0]
    ts = _rows(s)
    na = len(after)

    def body(*refs):
        dy_ref, xh_ref, r_ref, g_ref, w_ref = refs[:5]
        dz_ref, dzb_ref, dc_ref, dg_ref, db_ref = refs[5 + na:]
        i = pl.program_id(0)
        dyv = dy_ref[...]
        xhv = xh_ref[...]
        dxh = dyv * g_ref[...]
        m1 = jnp.mean(dxh, axis=-1, keepdims=True)
        m2 = jnp.mean(dxh * xhv, axis=-1, keepdims=True)
        dz = r_ref[...] * (dxh - m1 - xhv * m2)
        dzb = dz.astype(BF)
        dz_ref[...] = dz
        dzb_ref[...] = dzb
        dc_ref[...] = _dot_nt(dzb, w_ref[...]).astype(BF)

        @pl.when(i == 0)
        def _():
            dg_ref[...] = jnp.zeros_like(dg_ref)
            db_ref[...] = jnp.zeros_like(db_ref)

        dg_ref[...] += jnp.sum(dyv * xhv, axis=0, keepdims=True)
        db_ref[...] += jnp.sum(dyv, axis=0, keepdims=True)

    row = pl.BlockSpec((ts, d), lambda i: (i, 0))
    vec = pl.BlockSpec((1, d), lambda i: (0, 0))
    return pl.pallas_call(
        body, name=name, grid=(s // ts,),
        in_specs=[row, row, pl.BlockSpec((ts, 1), lambda i: (i, 0)), vec, _resident((wc, d), lambda i: (0, 0))]
                 + [pl.BlockSpec(memory_space=pl.ANY)] * na,
        out_specs=[row, row, pl.BlockSpec((ts, wc), lambda i: (i, 0)), vec, vec],
        out_shape=[jax.ShapeDtypeStruct((s, d), F32), jax.ShapeDtypeStruct((s, d), BF),
                   jax.ShapeDtypeStruct((s, wc), BF),
                   jax.ShapeDtypeStruct((1, d), F32), jax.ShapeDtypeStruct((1, d), F32)],
        compiler_params=_cp("arbitrary"))(dy, xh, rstd, gain, w_out, *after)


def _ffn_up(xb, wgu, name):
    s, d = xb.shape
    c = wgu.shape[1]
    nch = wgu.shape[0] // 2
    ts = _rows(s, 1024)
    w4 = wgu.reshape(2, nch, c, d)

    def body(x_ref, w_ref, gu_ref, a_ref):
        x = x_ref[...]
        g = _dot_nt(x, w_ref[0])
        u = _dot_nt(x, w_ref[1])
        sg = _sigmoid(g)
        t = g * sg
        gu_ref[0] = (u * (sg * (1.0 + g - t))).astype(BF)
        gu_ref[1] = t.astype(BF)
        a_ref[...] = (t * u).astype(BF)

    return pl.pallas_call(
        body, name=name, grid=(nch, s // ts),
        in_specs=[pl.BlockSpec((ts, d), lambda j, i: (i, 0)),
                  pl.BlockSpec((2, None, c, d), lambda j, i: (0, j, 0, 0))],
        out_specs=[pl.BlockSpec((2, None, ts, c), lambda j, i: (0, j, i, 0)),
                   pl.BlockSpec((None, ts, c), lambda j, i: (j, i, 0))],
        out_shape=[jax.ShapeDtypeStruct((2, nch, s, c), BF), jax.ShapeDtypeStruct((nch, s, c), BF)],
        compiler_params=_cp("parallel", "parallel"))(xb, w4)


def _ffn_fwd_main(x, xb, wgu, wd4, gain, bias, name, target=None):
    s, d = x.shape
    nch, c = wd4.shape[0], wd4.shape[1]
    ts = _rows(s, 256)
    head = target is not None

    def body(*refs):
        x_ref, xb_ref, wgu_ref, wd_ref, g_ref, b_ref = refs[:6]
        y_ref, yb_ref, gu_ref, a_ref, xh_ref, r_ref = refs[6 + head:]
        xbv = xb_ref[...]
        f = jnp.zeros((ts, d), F32)
        for j in range(nch):
            g = _dot_nt(xbv, wgu_ref[j])
            u = _dot_nt(xbv, wgu_ref[nch + j])
            sg = _sigmoid(g)
            t = g * sg
            gu_ref[0, j] = (u * (sg * (1.0 + g - t))).astype(BF)
            gu_ref[1, j] = t.astype(BF)
            act = (t * u).astype(BF)
            a_ref[j] = act
            f = f + _dot(act, wd_ref[j])
        z = ALPHA * x_ref[...] + 0.5 * f
        mu = jnp.mean(z, axis=-1, keepdims=True)
        zc = z - mu
        var = jnp.mean(zc * zc, axis=-1, keepdims=True)
        r = lax.rsqrt(var + LN_EPS)
        xh = zc * r
        y = xh * g_ref[...] + b_ref[...]
        xh_ref[...] = xh
        r_ref[...] = r
        if head:
            e = y - refs[6][...]
            y_ref[...] = e * (1.0 / d)

            @pl.when(pl.program_id(0) == 0)
            def _():
                yb_ref[...] = jnp.zeros_like(yb_ref)

            part = jnp.sum(jnp.sum(e * e, axis=1, keepdims=True), axis=0, keepdims=True)
            yb_ref[...] += part * (0.5 / d)
        else:
            y_ref[...] = y
            yb_ref[...] = y.astype(BF)

    row = pl.BlockSpec((ts, d), lambda i: (i, 0))
    vec = pl.BlockSpec((1, d), lambda i: (0, 0))
    second = ((pl.BlockSpec((1, 1), lambda i: (0, 0)), jax.ShapeDtypeStruct((1, 1), F32)) if head
              else (row, jax.ShapeDtypeStruct((s, d), BF)))
    return pl.pallas_call(
        body, name=name, grid=(s // ts,),
        in_specs=[row, row, _resident(wgu.shape, lambda i: (0, 0, 0)), _resident(wd4.shape, lambda i: (0, 0, 0)),
                  vec, vec] + [row] * head,
        out_specs=[row, second[0], pl.BlockSpec((2, nch, ts, c), lambda i: (0, 0, i, 0)),
                   pl.BlockSpec((nch, ts, c), lambda i: (0, i, 0)), row, pl.BlockSpec((ts, 1), lambda i: (i, 0))],
        out_shape=[jax.ShapeDtypeStruct((s, d), F32), second[1],
                   jax.ShapeDtypeStruct((2, nch, s, c), BF), jax.ShapeDtypeStruct((nch, s, c), BF),
                   jax.ShapeDtypeStruct((s, d), F32), jax.ShapeDtypeStruct((s, 1), F32)],
        compiler_params=_cp("arbitrary" if head else "parallel"))(x, xb, wgu, wd4, gain, bias,
                                                                   *([target] if head else []))


def _ffn_bwd_main(dy, xh, rstd, gain, wd4, wgu, gu, name, after=(), with_dx=True):
    s, d = dy.shape
    nch, c = wd4.shape[0], wd4.shape[1]
    ts = _rows(s, 256)
    na = len(after)

    def body(*refs):
        dy_ref, xh_ref, r_ref, g_ref, wd_ref, wgu_ref, gu_ref = refs[:7]
        dx_ref, dzb_ref, dh_ref, dg_ref, db_ref = refs[7 + na:]
        i = pl.program_id(0)
        dyv = dy_ref[...]
        xhv = xh_ref[...]
        dxh = dyv * g_ref[...]
        m1 = jnp.mean(dxh, axis=-1, keepdims=True)
        m2 = jnp.mean(dxh * xhv, axis=-1, keepdims=True)
        dz = r_ref[...] * (dxh - m1 - xhv * m2)
        dzb = (0.5 * dz).astype(BF)
        dzb_ref[...] = dzb

        @pl.when(i == 0)
        def _():
            dg_ref[...] = jnp.zeros_like(dg_ref)
            db_ref[...] = jnp.zeros_like(db_ref)

        dg_ref[...] += jnp.sum(dyv * xhv, axis=0, keepdims=True)
        db_ref[...] += jnp.sum(dyv, axis=0, keepdims=True)

        dx = ALPHA * dz if with_dx else dz
        for j in range(nch):
            da = _dot_nt(dzb, wd_ref[j])
            dgate = (da * gu_ref[0, j].astype(F32)).astype(BF)
            dup = (da * gu_ref[1, j].astype(F32)).astype(BF)
            dh_ref[0, j] = dgate
            dh_ref[1, j] = dup
            if with_dx:
                dx = dx + _dot(dgate, wgu_ref[j]) + _dot(dup, wgu_ref[nch + j])
        dx_ref[...] = dx

    row = pl.BlockSpec((ts, d), lambda i: (i, 0))
    vec = pl.BlockSpec((1, d), lambda i: (0, 0))
    act = pl.BlockSpec((2, nch, ts, c), lambda i: (0, 0, i, 0))
    return pl.pallas_call(
        body, name=name, grid=(s // ts,),
        in_specs=[row, row, pl.BlockSpec((ts, 1), lambda i: (i, 0)), vec,
                  _resident(wd4.shape, lambda i: (0, 0, 0)), _resident(wgu.shape, lambda i: (0, 0, 0)), act]
                 + [pl.BlockSpec(memory_space=pl.ANY)] * na,
        out_specs=[row, row, act, vec, vec],
        out_shape=[jax.ShapeDtypeStruct((s, d), F32), jax.ShapeDtypeStruct((s, d), BF),
                   jax.ShapeDtypeStruct((2, nch, s, c), BF),
                   jax.ShapeDtypeStruct((1, d), F32), jax.ShapeDtypeStruct((1, d), F32)],
        compiler_params=_cp("arbitrary"))(dy, xh, rstd, gain, wd4, wgu, gu, *after)


def _rope_tables(s, sign):
    pos = jnp.arange(s, dtype=F32)
    inv_freq = 1.0 / (ROPE_THETA ** (jnp.arange(ROT_HALF, dtype=F32) / ROT_HALF))
    ang = pos[:, None] * inv_freq[None, :]
    cos, sin = jnp.cos(ang), jnp.sin(ang) * sign
    one = jnp.ones((s, HEAD_DIM - 2 * ROT_HALF), F32)
    zero = jnp.zeros((s, HEAD_DIM - 2 * ROT_HALF), F32)
    zh = jnp.zeros((s, ROT_HALF), F32)
    cos_f = jnp.concatenate([cos, cos, one], axis=1)
    sin_a = jnp.concatenate([-sin, zh, zero], axis=1)
    sin_b = jnp.concatenate([zh, sin, zero], axis=1)
    rep = LANES // HEAD_DIM
    return tuple(jnp.tile(t, (1, rep)) for t in (cos_f, sin_a, sin_b))


def _rope(t, c_ref, sa_ref, sb_ref):
    return (t * c_ref[...] + pltpu.roll(t, LANES - ROT_HALF, 1) * sa_ref[...]
            + pltpu.roll(t, ROT_HALF, 1) * sb_ref[...])


def _proj_rope(xb, w, tabs, n_rope, name, w_rows_out, tail_block=None):
    s, d = xb.shape
    n = w.shape[0] if w_rows_out else w.shape[1]
    tm = _rows(s, 256)
    has_tail = tail_block is not None

    def body(x_ref, w_ref, c_ref, sa_ref, sb_ref, o_ref, *tail_ref):
        h = (_dot_nt if w_rows_out else _dot)(x_ref[...], w_ref[...])
        for cb in range(n // LANES):
            t = h[:, cb * LANES:(cb + 1) * LANES]
            if cb < n_rope:
                t = _rope(t, c_ref, sa_ref, sb_ref)
            o_ref[:, cb * LANES:(cb + 1) * LANES] = t.astype(BF)
        if has_tail:
            tail_ref[0][...] = h[:, tail_block * LANES:(tail_block + 1) * LANES]

    tab = pl.BlockSpec((tm, LANES), lambda i: (i, 0))
    out_specs = [pl.BlockSpec((tm, n), lambda i: (i, 0))]
    out_shape = [jax.ShapeDtypeStruct((s, n), BF)]
    if has_tail:
        out_specs.append(tab)
        out_shape.append(jax.ShapeDtypeStruct((s, LANES), F32))
    res = pl.pallas_call(
        body, name=name, grid=(s // tm,),
        in_specs=[pl.BlockSpec((tm, d), lambda i: (i, 0)), _resident(w.shape, lambda i: (0, 0)), tab, tab, tab],
        out_specs=out_specs, out_shape=out_shape, compiler_params=_cp("parallel"))(xb, w, *tabs)
    return res if has_tail else res[0]


def _rope_cast(parts, tabs, n_rope, name, transposed=()):
    s = tabs[0].shape[0]
    flip = [i in transposed for i in range(len(parts))]
    widths = [p.shape[0] if f else p.shape[1] for p, f in zip(parts, flip)]
    n = sum(widths)
    npart = len(parts)
    ts = _rows(s, 256)

    def body(*refs):
        part_refs = refs[:npart]
        c_ref, sa_ref, sb_ref, o_ref = refs[npart:]
        col = 0
        for ref, w, f in zip(part_refs, widths, flip):
            for j in range(w // LANES):
                if f:
                    t = jnp.transpose(ref[j * LANES:(j + 1) * LANES, :])
                else:
                    t = ref[:, j * LANES:(j + 1) * LANES]
                if col < n_rope:
                    t = _rope(t, c_ref, sa_ref, sb_ref)
                o_ref[:, col * LANES:(col + 1) * LANES] = t.astype(BF)
                col += 1

    tab = pl.BlockSpec((ts, LANES), lambda i: (i, 0))
    return pl.pallas_call(
        body, name=name, grid=(s // ts,),
        in_specs=[pl.BlockSpec((w, ts), lambda i: (0, i)) if f else pl.BlockSpec((ts, w), lambda i: (i, 0))
                  for w, f in zip(widths, flip)] + [tab, tab, tab],
        out_specs=pl.BlockSpec((ts, n), lambda i: (i, 0)),
        out_shape=jax.ShapeDtypeStruct((s, n), BF),
        compiler_params=_cp("parallel"))(*parts, *tabs)


def _head_masks():
    lane = lax.broadcasted_iota(jnp.int32, (1, LANES), 1)
    return [lane < HEAD_DIM, lane >= HEAD_DIM]


def _sel(mask, v):
    return jnp.where(mask, v, jnp.zeros_like(v))


def _pick(mask, wide, fill):
    return jnp.max(jnp.where(mask, wide, fill), axis=1, keepdims=True)


def _head_stack(hm, a, b):
    return jnp.concatenate([_sel(hm[0], a), _sel(hm[0], b), _sel(hm[1], a), _sel(hm[1], b)], axis=0)


def _band_mask_stack(has_prev):
    qi = lax.broadcasted_iota(jnp.int32, (BLOCK, 4 * BLOCK), 0)
    col = lax.broadcasted_iota(jnp.int32, (BLOCK, 4 * BLOCK), 1)
    d = jnp.bitwise_and(col, BLOCK - 1) - qi
    is_prev = jnp.bitwise_and(col, BLOCK) != 0
    return jnp.where(is_prev, d - jnp.where(has_prev, 0, BLOCK), -d) >= 0


class _BandView:
    def __init__(self, s, g):
        self.r = 4 ** g
        self.nl = s // self.r
        self.nblk = self.nl // BLOCK
        self.nsub = min(BAND_SUB, self.nblk)
        self.tile = self.nsub * BLOCK
        self.ncols = min(self.r * GROUP_W // LANES, BAND_COLS)
        self.grid = (self.r * GROUP_W // LANES // self.ncols, self.nblk // self.nsub)

    def view(self, a):
        return a.reshape(self.nl, self.r * a.shape[1])

    def qkv(self, hb, g):
        npair = MIX_W // LANES
        offs = [i * npair + g * GROUP_W // LANES for i in range(3)]
        if self.r == 1:
            return [hb] * 3, hb.shape[1], offs
        return [self.view(hb[:, o * LANES:o * LANES + GROUP_W]) for o in offs], GROUP_W, [0, 0, 0]

    def specs(self, width, off):
        assert off % self.ncols == 0 and (width == GROUP_W or self.r == 1)
        nsub, last, lanes, first = self.nsub, self.nblk - 1, self.ncols * LANES, off // self.ncols
        return (pl.BlockSpec((self.tile, lanes), lambda cg, t: (t, first + cg)),
                pl.BlockSpec((BLOCK, lanes), lambda cg, t: (jnp.maximum(t * nsub - 1, 0), first + cg)),
                pl.BlockSpec((BLOCK, lanes), lambda cg, t: (jnp.minimum(t * nsub + nsub, last), first + cg)))


def _band_fwd(hb, g, name):
    s, n = hb.shape
    bv = _BandView(s, g)
    nsub = bv.nsub
    npair = MIX_W // LANES

    def body(q_ref, kc_ref, kp_ref, vc_ref, vp_ref, o_ref, l_ref):
        t = pl.program_id(1)
        hm = _head_masks()
        for i, c in [(i, c) for i in range(nsub) for c in range(bv.ncols)]:
            rows = slice(i * BLOCK, (i + 1) * BLOCK)
            lanes = slice(c * LANES, (c + 1) * LANES)
            has_prev = t > 0 if i == 0 else True
            q, kc, vc = q_ref[rows, lanes], kc_ref[rows, lanes], vc_ref[rows, lanes]
            if i == 0:
                kp, vp = kp_ref[:, lanes], vp_ref[:, lanes]
            else:
                prev = slice((i - 1) * BLOCK, i * BLOCK)
                kp, vp = kc_ref[prev, lanes], vc_ref[prev, lanes]
            sc = jnp.where(_band_mask_stack(has_prev), _dot_nt(q, _head_stack(hm, kc, kp)) * SCALE, NEG)
            ps, ms, ls = [], [], []
            for h in range(2):
                sh = sc[:, 2 * h * BLOCK:2 * (h + 1) * BLOCK]
                m = jnp.max(sh, axis=1, keepdims=True)
                p = jnp.exp(sh - m)
                ps.append(p.astype(BF))
                ms.append(m)
                ls.append(jnp.sum(p, axis=1, keepdims=True))
            o = _dot(jnp.concatenate(ps, axis=1), _head_stack(hm, vc, vp))
            o_ref[rows, lanes] = o / jnp.where(hm[0], ls[0], ls[1])
            l_ref[rows, lanes] = jnp.where(hm[0], ms[0] + jnp.log(ls[0]), ms[1] + jnp.log(ls[1]))

    (qv, kv_, vv), width, (qo, ko, vo) = bv.qkv(hb, g)
    q_cur, _, _ = bv.specs(width, qo)
    k_cur, k_prv, _ = bv.specs(width, ko)
    v_cur, v_prv, _ = bv.specs(width, vo)
    out_spec = bv.specs(GROUP_W, 0)[0]
    out = jax.ShapeDtypeStruct((bv.nl, bv.r * GROUP_W), F32)
    o, l = pl.pallas_call(
        body, name=name, grid=bv.grid,
        in_specs=[q_cur, k_cur, k_prv, v_cur, v_prv], out_specs=[out_spec, out_spec], out_shape=[out, out],
        compiler_params=_cp("parallel", "parallel"))(qv, kv_, kv_, vv, vv)
    return o.reshape(s, GROUP_W), l.reshape(s, GROUP_W)


def _band_combine(os, ls, name):
    ng = len(os)
    s, w = os[0].shape
    ts = _rows(s)

    def body(*refs):
        o_refs, l_refs = refs[:ng], refs[ng:2 * ng]
        oa_ref, lt_ref = refs[2 * ng:]
        lv = [r[...] for r in l_refs]
        m = functools.reduce(jnp.maximum, lv)
        es = [jnp.exp(l - m) for l in lv]
        den = functools.reduce(lambda a, b: a + b, es)
        num = functools.reduce(lambda a, b: a + b, [es[g] * o_refs[g][...] for g in range(ng)])
        oa_ref[...] = (num / den).astype(BF)
        lt_ref[...] = m + jnp.log(den)

    blk = pl.BlockSpec((ts, w), lambda i: (i, 0))
    return pl.pallas_call(
        body, name=name, grid=(s // ts,), in_specs=[blk] * (2 * ng), out_specs=[blk, blk],
        out_shape=[jax.ShapeDtypeStruct((s, w), BF), jax.ShapeDtypeStruct((s, w), F32)],
        compiler_params=_cp("parallel"))(*os, *ls)


def _band_bwd(hb, dcat, oa, lt, g, name):
    s, n = hb.shape
    bv = _BandView(s, g)
    nsub = bv.nsub
    npair = MIX_W // LANES
    ntile = bv.grid[1]

    def body(q_ref, qn_ref, kc_ref, kp_ref, vc_ref, vp_ref, do_ref, don_ref, oa_ref, oan_ref, lt_ref, ltn_ref,
             dq_ref, dk_ref, dv_ref):
        t = pl.program_id(1)
        hm = _head_masks()

        for i, c in [(i, c) for i in range(nsub) for c in range(bv.ncols)]:
            lanes = slice(c * LANES, (c + 1) * LANES)

            def block(ref, edge_ref, i, lanes=lanes):
                if i < 0 or i >= nsub:
                    return edge_ref[:, lanes]
                return ref[i * BLOCK:(i + 1) * BLOCK, lanes]

            has_prev = t > 0 if i == 0 else True
            has_next = t < ntile - 1 if i == nsub - 1 else True
            q, qn = block(q_ref, None, i), block(q_ref, qn_ref, i + 1)
            kc, kp = block(kc_ref, None, i), block(kc_ref, kp_ref, i - 1)
            vc, vp = block(vc_ref, None, i), block(vc_ref, vp_ref, i - 1)
            do, don = block(do_ref, None, i), block(do_ref, don_ref, i + 1)
            dd = do.astype(F32) * block(oa_ref, None, i).astype(F32)
            ddn = don.astype(F32) * block(oa_ref, oan_ref, i + 1).astype(F32)
            lt, ltn = block(lt_ref, None, i), block(lt_ref, ltn_ref, i + 1)

            def per_head(wide, width):
                col = lax.broadcasted_iota(jnp.int32, (BLOCK, 2 * width), 1)
                return jnp.where(col < width, _pick(hm[0], wide, NEG), _pick(hm[1], wide, NEG))

            def row_sums(prod, width):
                col = lax.broadcasted_iota(jnp.int32, (BLOCK, 2 * width), 1)
                return jnp.where(col < width, jnp.sum(_sel(hm[0], prod), axis=1, keepdims=True),
                                 jnp.sum(_sel(hm[1], prod), axis=1, keepdims=True))

            kst, vst = _head_stack(hm, kc, kp), _head_stack(hm, vc, vp)
            p = jnp.exp(jnp.where(_band_mask_stack(has_prev), _dot_nt(q, kst) * SCALE, NEG)
                        - per_head(lt, 2 * BLOCK))
            ds = p * (_dot_nt(do, vst) - row_sums(dd, 2 * BLOCK))
            dq_ref[i * BLOCK:(i + 1) * BLOCK, lanes] = SCALE * _dot(ds.astype(BF), kst)
            kcs = jnp.concatenate([_sel(hm[0], kc), _sel(hm[1], kc)], axis=0)
            vcs = jnp.concatenate([_sel(hm[0], vc), _sel(hm[1], vc)], axis=0)
            qi_ = lax.broadcasted_iota(jnp.int32, (BLOCK, 2 * BLOCK), 0)
            kj_ = jnp.bitwise_and(lax.broadcasted_iota(jnp.int32, (BLOCK, 2 * BLOCK), 1), BLOCK - 1)
            mn = kj_ >= qi_ + jnp.where(has_next, 0, BLOCK)
            pn = jnp.exp(jnp.where(mn, _dot_nt(qn, kcs) * SCALE, NEG) - per_head(ltn, BLOCK))
            dsn = pn * (_dot_nt(don, vcs) - row_sums(ddn, BLOCK))
            pb, dsb, pnb, dsnb = p.astype(BF), ds.astype(BF), pn.astype(BF), dsn.astype(BF)

            def own(x, h):
                return x[:, 2 * h * BLOCK:(2 * h + 1) * BLOCK]

            def nxt(x, h):
                return x[:, h * BLOCK:(h + 1) * BLOCK]

            ds_rows = jnp.concatenate([own(dsb, 0), nxt(dsnb, 0), own(dsb, 1), nxt(dsnb, 1)], axis=0)
            p_rows = jnp.concatenate([own(pb, 0), nxt(pnb, 0), own(pb, 1), nxt(pnb, 1)], axis=0)
            dk_ref[i * BLOCK:(i + 1) * BLOCK, lanes] = SCALE * _dot_tn(ds_rows, _head_stack(hm, q, qn))
            dv_ref[i * BLOCK:(i + 1) * BLOCK, lanes] = _dot_tn(p_rows, _head_stack(hm, do, don))

    (qv, kv_, vv), width, (qo, ko, vo) = bv.qkv(hb, g)
    q_cur, _, q_nxt = bv.specs(width, qo)
    k_cur, k_prv, _ = bv.specs(width, ko)
    v_cur, v_prv, _ = bv.specs(width, vo)
    w_cur, _, w_nxt = bv.specs(GROUP_W, 0)
    out = jax.ShapeDtypeStruct((bv.nl, bv.r * GROUP_W), F32)
    dv_, ov, lv = bv.view(dcat[:, :GROUP_W]), bv.view(oa), bv.view(lt)
    res = pl.pallas_call(
        body, name=name, grid=bv.grid,
        in_specs=[q_cur, q_nxt, k_cur, k_prv, v_cur, v_prv, w_cur, w_nxt, w_cur, w_nxt, w_cur, w_nxt],
        out_specs=[w_cur, w_cur, w_cur], out_shape=[out, out, out],
        compiler_params=_cp("parallel", "parallel"))(
            qv, qv, kv_, kv_, vv, vv, dv_, dv_, ov, ov, lv, lv)
    return [t.reshape(s, GROUP_W) for t in res]


def _mem_fwd(hb, q_blk0, kv, name):
    s = hb.shape[0]
    m = kv.shape[0]
    tq = _rows(s)
    npair = MEM_W // LANES

    def body(q_ref, k_ref, v_ref, o_ref, l_ref):
        q, k, v = q_ref[...], k_ref[...], v_ref[...]
        hm = _head_masks()
        o = jnp.zeros((tq, LANES), F32)
        lse_w = jnp.zeros((tq, LANES), F32)
        for h in range(2):
            sc = _dot_nt(_sel(hm[h], q), k) * SCALE
            mx = jnp.max(sc, axis=1, keepdims=True)
            p = jnp.exp(sc - mx)
            l = jnp.sum(p, axis=1, keepdims=True)
            o = o + _dot(p.astype(BF), _sel(hm[h], v)) / l
            lse_w = jnp.where(hm[h], mx + jnp.log(l), lse_w)
        o_ref[...] = o.astype(BF)
        l_ref[...] = lse_w

    blk = pl.BlockSpec((tq, LANES), lambda p, i: (i, p))
    return pl.pallas_call(
        body, name=name, grid=(npair, s // tq),
        in_specs=[pl.BlockSpec((tq, LANES), lambda p, i: (i, q_blk0 + p)),
                  pl.BlockSpec((m, LANES), lambda p, i: (0, p)),
                  pl.BlockSpec((m, LANES), lambda p, i: (0, npair + p))],
        out_specs=[blk, blk],
        out_shape=[jax.ShapeDtypeStruct((s, MEM_W), BF), jax.ShapeDtypeStruct((s, MEM_W), F32)],
        compiler_params=_cp("parallel", "parallel"))(hb, kv, kv)


def _mem_bwd(hb, q_blk0, kv, dcat, cat, o_blk0, lse, name):
    s = hb.shape[0]
    m = kv.shape[0]
    tq = _rows(s)
    npair = MEM_W // LANES

    def body(q_ref, k_ref, v_ref, do_ref, o_ref, l_ref, dq_ref, dk_ref, dv_ref):
        i = pl.program_id(1)

        @pl.when(i == 0)
        def _():
            dk_ref[...] = jnp.zeros_like(dk_ref)
            dv_ref[...] = jnp.zeros_like(dv_ref)

        q, k, v, do = q_ref[...], k_ref[...], v_ref[...], do_ref[...]
        dd = do.astype(F32) * o_ref[...].astype(F32)
        lt = l_ref[...]
        hm = _head_masks()
        dq = jnp.zeros((tq, LANES), F32)
        dk = jnp.zeros((m, LANES), F32)
        dv = jnp.zeros((m, LANES), F32)
        for h in range(2):
            qh, doh = _sel(hm[h], q), _sel(hm[h], do)
            p = jnp.exp(_dot_nt(qh, k) * SCALE - _pick(hm[h], lt, NEG))
            ds = p * (_dot_nt(doh, v) - jnp.sum(_sel(hm[h], dd), axis=1, keepdims=True))
            dq = dq + SCALE * _dot(ds.astype(BF), _sel(hm[h], k))
            dk = dk + SCALE * _dot_tn(ds.astype(BF), qh)
            dv = dv + _dot_tn(p.astype(BF), doh)
        dq_ref[...] = dq
        dk_ref[...] += dk
        dv_ref[...] += dv

    row = pl.BlockSpec((tq, LANES), lambda p, i: (i, p))
    orow = pl.BlockSpec((tq, LANES), lambda p, i: (i, o_blk0 + p))
    acc = pl.BlockSpec((m, LANES), lambda p, i: (0, p))
    return pl.pallas_call(
        body, name=name, grid=(npair, s // tq),
        in_specs=[pl.BlockSpec((tq, LANES), lambda p, i: (i, q_blk0 + p)),
                  pl.BlockSpec((m, LANES), lambda p, i: (0, p)),
                  pl.BlockSpec((m, LANES), lambda p, i: (0, npair + p)), orow, orow, row],
        out_specs=[row, acc, acc],
        out_shape=[jax.ShapeDtypeStruct((s, MEM_W), F32), jax.ShapeDtypeStruct((m, MEM_W), F32),
                   jax.ShapeDtypeStruct((m, MEM_W), F32)],
        compiler_params=_cp("parallel", "arbitrary"))(hb, kv, kv, dcat, cat, lse)


def _gate_fwd(f_t, bias, name):
    hp, s = f_t.shape
    nblk = s // LANES

    def body(f_ref, b_ref, c_ref):
        lane = lax.broadcasted_iota(jnp.int32, (hp, LANES), 1)

        def step(i, carry):
            off = pl.multiple_of(i * LANES, LANES)
            x = f_ref[:, pl.ds(off, LANES)] + b_ref[...]
            acc = jnp.minimum(x, 0.0) - jnp.log(1.0 + jnp.exp(-jnp.abs(x)))
            sh = 1
            while sh < LANES:
                acc = acc + jnp.where(lane >= sh, pltpu.roll(acc, sh, 1), 0.0)
                sh *= 2
            acc = acc + carry
            c_ref[:, pl.ds(off, LANES)] = acc
            return acc[:, LANES - 1:LANES]

        lax.fori_loop(0, nblk, step, jnp.zeros((hp, 1), F32), unroll=8)

    vm = pl.BlockSpec(memory_space=pltpu.VMEM)
    return pl.pallas_call(body, name=name, in_specs=[vm, vm], out_specs=vm,
                          out_shape=jax.ShapeDtypeStruct((hp, s), F32),
                          compiler_params=pltpu.CompilerParams(vmem_limit_bytes=VMEM_LIMIT))(f_t, bias)


def _gate_bwd(dc_t, f_t, bias, name):
    hp, s = f_t.shape
    nblk = s // LANES

    def body(dc_ref, f_ref, b_ref, df_ref, db_ref):
        lane = lax.broadcasted_iota(jnp.int32, (hp, LANES), 1)

        def step(t, carry):
            suffix, dbias = carry
            off = pl.multiple_of((nblk - 1 - t) * LANES, LANES)
            acc = dc_ref[:, pl.ds(off, LANES)]
            sh = 1
            while sh < LANES:
                acc = acc + jnp.where(lane < LANES - sh, pltpu.roll(acc, LANES - sh, 1), 0.0)
                sh *= 2
            acc = acc + suffix
            x = f_ref[:, pl.ds(off, LANES)] + b_ref[...]
            df = acc * _sigmoid(-x)
            df_ref[:, pl.ds(off, LANES)] = df
            return acc[:, 0:1], dbias + jnp.sum(df, axis=1, keepdims=True)

        _, dbias = lax.fori_loop(0, nblk, step, (jnp.zeros((hp, 1), F32), jnp.zeros((hp, 1), F32)), unroll=8)
        db_ref[...] = dbias

    vm = pl.BlockSpec(memory_space=pltpu.VMEM)
    return pl.pallas_call(body, name=name, in_specs=[vm, vm, vm], out_specs=[vm, vm],
                          out_shape=[jax.ShapeDtypeStruct((hp, s), F32), jax.ShapeDtypeStruct((hp, 1), F32)],
                          compiler_params=pltpu.CompilerParams(vmem_limit_bytes=VMEM_LIMIT))(dc_t, f_t, bias)


def _wide(rep, width):
    return jnp.tile(rep, (1, width // LANES))


def _fold(t):
    part = t[:, :LANES]
    for c in range(1, t.shape[1] // LANES):
        part = part + t[:, c * LANES:(c + 1) * LANES]
    return part


def _foxt_logits(q, k, cq_row, ck_rep, mask, hmask):
    s = _dot_nt(_sel(hmask, k), q) + (cq_row - _wide(ck_rep, q.shape[0]))
    if mask is not None:
        s = jnp.where(mask, s, NEG)
    return s


def _causal_sub(ks, qs):
    shape = (ks.stop - ks.start, qs.stop - qs.start)
    return (ks.start + lax.broadcasted_iota(jnp.int32, shape, 0)
            <= qs.start + lax.broadcasted_iota(jnp.int32, shape, 1))


def _diag_blocks(t):
    h = t // 2
    return [(slice(0, h), slice(0, t)), (slice(h, t), slice(h, t))]


FOX_SPLIT = 1


def _fox_tiles(s):
    tq = _rows(s, 1024)
    return tq, tq // FOX_SPLIT, s // tq


def _fox_steps(nq):
    return FOX_SPLIT * nq * (nq + 1) // 2


def _count_ge(t, bounds):
    return sum([(t >= b).astype(jnp.int32) for b in bounds], jnp.int32(0))


def _sweep_q_major(t, nq):
    qi = _count_ge(t, [FOX_SPLIT * r * (r + 1) // 2 for r in range(1, nq)])
    return qi, t - FOX_SPLIT * qi * (qi + 1) // 2


def _sweep_k_major(t, nq):
    counts = [nq - j // FOX_SPLIT for j in range(FOX_SPLIT * nq)]
    offs = [sum(counts[:j]) for j in range(1, FOX_SPLIT * nq)]
    kj = _count_ge(t, offs)
    start = sum([jnp.where(t >= o, c, 0) for o, c in zip(offs, counts)], jnp.int32(0))
    qi = kj // FOX_SPLIT + (t - start)
    return kj, qi, t == start, qi == nq - 1


def _foxt_fwd(hb, c_rep, c_t3, name):
    s = hb.shape[0]
    npair = MIX_W // LANES
    tq, tk, nq = _fox_tiles(s)

    def body(q_ref, k_ref, v_ref, cq_ref, ck_ref, o_ref, l_ref, m_s, l_s, acc):
        qi, kj = _sweep_q_major(pl.program_id(1), nq)
        hm = _head_masks()

        @pl.when(kj == 0)
        def _():
            m_s[...] = jnp.full_like(m_s, NEG)
            l_s[...] = jnp.zeros_like(l_s)
            acc[...] = jnp.zeros_like(acc)

        def step(ks, qs, masked):
            q, k = q_ref[qs, :] * SCALE, k_ref[ks, :]
            vt = jnp.transpose(v_ref[ks, :])
            cq = cq_ref[:, qs]
            mask = _causal_sub(ks, qs) if masked else None
            for h in range(2):
                st = _foxt_logits(q, k, cq[h:h + 1, :], ck_ref[h, ks, :], mask, hm[h])
                m_old = m_s[h, :, qs]
                m_new = jnp.maximum(m_old, jnp.max(st, axis=0, keepdims=True))
                pt = jnp.exp(st - m_new)
                corr = jnp.exp(m_old - m_new)
                l_s[h, :, qs] = l_s[h, :, qs] * corr + jnp.sum(pt, axis=0, keepdims=True)
                acc[h, :, qs] = acc[h, :, qs] * corr + _dot(vt[h * HEAD_DIM:(h + 1) * HEAD_DIM, :], pt.astype(BF))
                m_s[h, :, qs] = m_new

        @pl.when(kj < qi)
        def _():
            step(slice(0, tk), slice(0, tq), False)

        @pl.when(kj == qi)
        def _():
            for ks, qs in _diag_blocks(tq):
                step(ks, qs, True)
            outs = []
            for h in range(2):
                outs.append(acc[h] / l_s[h])
                l_ref[h:h + 1, :] = m_s[h] + jnp.log(l_s[h])
            o_ref[...] = jnp.transpose(jnp.concatenate(outs, axis=0)).astype(BF)

    def q_map(p, t):
        return (_sweep_q_major(t, nq)[0], p)

    def kv_map(off):
        return lambda p, t: (_sweep_q_major(t, nq)[1], off + p)

    blk = pl.BlockSpec((tq, LANES), q_map)
    row = pl.BlockSpec((None, 2, tq), lambda p, t: (p, 0, _sweep_q_major(t, nq)[0]))
    return pl.pallas_call(
        body, name=name, grid=(npair, _fox_steps(nq)),
        in_specs=[blk, pl.BlockSpec((tk, LANES), kv_map(npair)), pl.BlockSpec((tk, LANES), kv_map(2 * npair)), row,
                  pl.BlockSpec((2, tk, LANES), lambda p, t: (p, _sweep_q_major(t, nq)[1], 0))],
        out_specs=[blk, row],
        out_shape=[jax.ShapeDtypeStruct((s, MIX_W), BF), jax.ShapeDtypeStruct((npair, 2, s), F32)],
        scratch_shapes=[pltpu.VMEM((2, 1, tq), F32), pltpu.VMEM((2, 1, tq), F32),
                        pltpu.VMEM((2, HEAD_DIM, tq), F32)],
        compiler_params=_cp("parallel", "arbitrary"))(hb, hb, hb, c_t3, c_rep)


def _foxt_dsum(hb, dcat, lse, c_rep, c_t3, name):
    s = hb.shape[0]
    npair = MIX_W // LANES
    tq, tk, nq = _fox_tiles(s)

    def body(q_ref, k_ref, v_ref, do_ref, l_ref, cq_ref, ck_ref, d_ref, acc):
        qi, kj = _sweep_q_major(pl.program_id(1), nq)
        hm = _head_masks()

        @pl.when(kj == 0)
        def _():
            acc[...] = jnp.zeros_like(acc)

        def step(ks, qs, masked):
            q, k, v, do = q_ref[qs, :] * SCALE, k_ref[ks, :], v_ref[ks, :], do_ref[qs, :]
            cq, lse_rows = cq_ref[:, qs], l_ref[:, qs]
            mask = _causal_sub(ks, qs) if masked else None
            for h in range(2):
                pt = jnp.exp(_foxt_logits(q, k, cq[h:h + 1, :], ck_ref[h, ks, :], mask, hm[h])
                             - lse_rows[h:h + 1, :])
                acc[h, :, qs] += jnp.sum(pt * _dot_nt(_sel(hm[h], v), do), axis=0, keepdims=True)

        @pl.when(kj < qi)
        def _():
            step(slice(0, tk), slice(0, tq), False)

        @pl.when(kj == qi)
        def _():
            for ks, qs in _diag_blocks(tq):
                step(ks, qs, True)
            for h in range(2):
                d_ref[h:h + 1, :] = acc[h]

    def q_map(p, t):
        return (_sweep_q_major(t, nq)[0], p)

    def kv_map(off):
        return lambda p, t: (_sweep_q_major(t, nq)[1], off + p)

    blk = pl.BlockSpec((tq, LANES), q_map)
    row = pl.BlockSpec((None, 2, tq), lambda p, t: (p, 0, _sweep_q_major(t, nq)[0]))
    return pl.pallas_call(
        body, name=name, grid=(npair, _fox_steps(nq)),
        in_specs=[blk, pl.BlockSpec((tk, LANES), kv_map(npair)), pl.BlockSpec((tk, LANES), kv_map(2 * npair)),
                  blk, row, row, pl.BlockSpec((2, tk, LANES), lambda p, t: (p, _sweep_q_major(t, nq)[1], 0))],
        out_specs=row, out_shape=jax.ShapeDtypeStruct((npair, 2, s), F32),
        scratch_shapes=[pltpu.VMEM((2, 1, tq), F32)],
        compiler_params=_cp("parallel", "arbitrary"))(hb, hb, hb, dcat, lse, c_t3, c_rep)


def _foxt_bwd(hb, dcat, dsum, lse, c_rep, c_t3, name):
    s = hb.shape[0]
    npair = MIX_W // LANES
    tq, tk, nq = _fox_tiles(s)

    def body(q_ref, k_ref, v_ref, do_ref, d_ref, l_ref, cq_ref, ck_ref, dq_ref, dk_ref, dv_ref, dc_ref, dc_s):
        t = pl.program_id(1)
        kj, qi, first, last = _sweep_k_major(t, nq)
        hm = _head_masks()

        @pl.when(first)
        def _():
            dk_ref[...] = jnp.zeros_like(dk_ref)
            dv_ref[...] = jnp.zeros_like(dv_ref)
            dc_s[...] = jnp.zeros_like(dc_s)

        @pl.when(t == 0)
        def _():
            dq_ref[...] = jnp.zeros_like(dq_ref)

        def step(ks, qs, masked):
            q, k, v, do = q_ref[qs, :] * SCALE, k_ref[ks, :], v_ref[ks, :], do_ref[qs, :]
            qt, kt, dot = jnp.transpose(q), jnp.transpose(k), jnp.transpose(do)
            cq, lse_rows, d_rows = cq_ref[:, qs], l_ref[:, qs], d_ref[:, qs]
            mask = _causal_sub(ks, qs) if masked else None
            dqs, dks, dvs = [], [], []
            for h in range(2):
                rows = slice(h * HEAD_DIM, (h + 1) * HEAD_DIM)
                pt = jnp.exp(_foxt_logits(q, k, cq[h:h + 1, :], ck_ref[h, ks, :], mask, hm[h])
                             - lse_rows[h:h + 1, :])
                dst = pt * (_dot_nt(_sel(hm[h], v), do) - d_rows[h:h + 1, :])
                dsb = dst.astype(BF)
                dqs.append(_dot(kt[rows, :], dsb))
                dks.append(_dot_nt(qt[rows, :], dsb))
                dvs.append(_dot_nt(dot[rows, :], pt.astype(BF)))
                dc_s[h, ks, :] += _fold(dst)
            cols = pl.ds(pl.multiple_of(qi * tq + qs.start, qs.stop - qs.start), qs.stop - qs.start)
            dq_ref[:, cols] += SCALE * jnp.concatenate(dqs, axis=0)
            dk_ref[:, ks] += jnp.concatenate(dks, axis=0)
            dv_ref[:, ks] += jnp.concatenate(dvs, axis=0)

        @pl.when(kj < qi)
        def _():
            step(slice(0, tk), slice(0, tq), False)

        @pl.when(kj == qi)
        def _():
            for ks, qs in _diag_blocks(tq):
                step(ks, qs, True)

        @pl.when(last)
        def _():
            for h in range(2):
                dc_ref[h:h + 1, :] = -jnp.sum(jnp.transpose(dc_s[h]), axis=0, keepdims=True)

    def kj_of(t):
        return _sweep_k_major(t, nq)[0]

    def qi_of(t):
        return _sweep_k_major(t, nq)[1]

    qblk = pl.BlockSpec((tq, LANES), lambda p, t: (qi_of(t), p))
    row = pl.BlockSpec((None, 2, tq), lambda p, t: (p, 0, qi_of(t)))
    kblk = pl.BlockSpec((LANES, tk), lambda p, t: (p, kj_of(t)))
    rep = pl.BlockSpec((2, tk, LANES), lambda p, t: (p, kj_of(t), 0))
    return pl.pallas_call(
        body, name=name, grid=(npair, _fox_steps(nq)),
        in_specs=[qblk,
                  pl.BlockSpec((tk, LANES), lambda p, t: (kj_of(t), npair + p)),
                  pl.BlockSpec((tk, LANES), lambda p, t: (kj_of(t), 2 * npair + p)),
                  qblk, row, row, row, rep],
        out_specs=[pl.BlockSpec((LANES, s), lambda p, t: (p, 0)), kblk, kblk,
                   pl.BlockSpec((None, 2, tk), lambda p, t: (p, 0, kj_of(t)))],
        out_shape=[jax.ShapeDtypeStruct((MIX_W, s), F32), jax.ShapeDtypeStruct((MIX_W, s), F32),
                   jax.ShapeDtypeStruct((MIX_W, s), F32), jax.ShapeDtypeStruct((npair, 2, s), F32)],
        scratch_shapes=[pltpu.VMEM((2, tk, LANES), F32)],
        compiler_params=_cp("arbitrary", "arbitrary"))(hb, hb, hb, dcat, dsum, lse, c_t3, c_rep)


def _adam_rows(r, c):
    cap = max(8, (1 << 20) // (4 * c))
    if r <= cap:
        return r
    best = None
    for t in range(8, cap + 1, 8):
        if r % t == 0:
            best = t
    return best if best is not None else r


def _reduce_adamw(contribs, w, m, v, name):
    nl = len(contribs)
    nd, r, c = contribs[0].shape
    tr = _adam_rows(r, c)
    bc1 = 1.0 - ADAM_B1 ** ADAM_STEP
    bc2 = 1.0 - ADAM_B2 ** ADAM_STEP

    def body(*refs):
        c_refs = refs[:nl]
        w_ref, m_ref, v_ref, g_ref, d_ref, nm_ref, nv_ref = refs[nl:]
        l = pl.program_id(0)
        for li in range(nl):
            @pl.when(l == li)
            def _(c_ref=c_refs[li]):
                g = c_ref[0].astype(F32)
                for k in range(1, nd):
                    g = g + c_ref[k].astype(F32)
                nm = ADAM_B1 * m_ref[...] + (1.0 - ADAM_B1) * g
                nv = ADAM_B2 * v_ref[...] + (1.0 - ADAM_B2) * (g * g)
                g_ref[...] = g
                nm_ref[...] = nm
                nv_ref[...] = nv
                d_ref[...] = -ADAM_LR * ((nm / bc1) / (jnp.sqrt(nv / bc2) + ADAM_EPS) + ADAM_WD * w_ref[...])

    def c_spec(li):
        return pl.BlockSpec((nd, tr, c), lambda l, i: (0, jnp.where(l == li, i, 0), 0))

    blk = pl.BlockSpec((None, tr, c), lambda l, i: (l, i, 0))
    out = jax.ShapeDtypeStruct((nl, r, c), F32)
    return pl.pallas_call(
        body, name=name, grid=(nl, r // tr),
        in_specs=[c_spec(li) for li in range(nl)] + [blk, blk, blk],
        out_specs=[blk, blk, blk, blk], out_shape=[out, out, out, out],
        compiler_params=_cp("arbitrary", "arbitrary"))(*contribs, w, m, v)


def _mesh_pos():
    return lax.axis_index("x"), lax.axis_index("y"), lax.axis_index("c")


def _peer(pos, k):
    x, y, c = pos
    return (1 - x if k & 4 else x, 1 - y if k & 2 else y, 1 - c if k & 1 else c)


def _linear(pos):
    return 4 * pos[0] + 2 * pos[1] + pos[2]


def _xfer_copies(srcs, lands, send_sems, recv_sems, local_sems, gather):
    pos = _mesh_pos()
    me = _linear(pos)
    local, remote = [], []
    for i, (src, land) in enumerate(zip(srcs, lands)):
        local.append(pltpu.make_async_copy(src if gather else src.at[me], land.at[me], local_sems.at[i]))
        for k in range(1, N_DEV):
            peer = _peer(pos, k)
            remote.append(pltpu.make_async_remote_copy(
                src_ref=src if gather else src.at[_linear(peer)], dst_ref=land.at[me],
                send_sem=send_sems.at[i * (N_DEV - 1) + k - 1], recv_sem=recv_sems.at[i * (N_DEV - 1) + k - 1],
                device_id=peer, device_id_type=MESH_ID))
    return local, remote


_HBM = pl.BlockSpec(memory_space=pltpu.HBM)
_SEM = pl.BlockSpec(memory_space=pltpu.SEMAPHORE)
_EFFECT = pltpu.SideEffectType.DATAFLOW_SIDE_EFFECTING


def _xfer_start(srcs, gather, name, after=()):
    n = len(srcs)
    na = len(after)
    lands = [lax.empty(((N_DEV,) + a.shape) if gather else a.shape, a.dtype) for a in srcs]

    def body(*refs):
        src, land = refs[:n], refs[n:2 * n]
        send_sems, recv_sems, local_sems = refs[2 * n + na:2 * n + na + 3]
        local, remote = _xfer_copies(src, land, send_sems, recv_sems, local_sems, gather)
        for cp in local + remote:
            cp.start()
        refs[-1][...] = jnp.zeros_like(refs[-1])

    nsem = n * (N_DEV - 1)
    out = pl.pallas_call(
        body, name=name,
        out_shape=(pltpu.SemaphoreType.DMA((nsem,)), pltpu.SemaphoreType.DMA((nsem,)), pltpu.SemaphoreType.DMA((n,)),
                   *[pltpu.HBM(a.shape, a.dtype) for a in srcs], *[pltpu.HBM(a.shape, a.dtype) for a in lands],
                   jax.ShapeDtypeStruct((8, LANES), F32)),
        in_specs=[_HBM] * (2 * n) + [pl.BlockSpec(memory_space=pl.ANY)] * na,
        out_specs=(_SEM, _SEM, _SEM, *[_HBM] * (2 * n), pl.BlockSpec(memory_space=pltpu.VMEM)),
        input_output_aliases={i: 3 + i for i in range(2 * n)},
        compiler_params=pltpu.CompilerParams(has_side_effects=_EFFECT))(
            *[pltpu.with_memory_space_constraint(a, pltpu.HBM) for a in srcs],
            *[pltpu.with_memory_space_constraint(a, pltpu.HBM) for a in lands], *after)
    return out[:3], list(out[3:3 + n]), list(out[3 + n:3 + 2 * n]), out[-1]


def _started(handle):
    return handle[3]


def _xfer_wait(handle, after, gather, name):
    sems, srcs, lands, _ = handle
    n = len(srcs)

    def body(*refs):
        src, land = refs[:n], refs[n:2 * n]
        send_sems, recv_sems, local_sems = refs[2 * n:2 * n + 3]
        local, remote = _xfer_copies(src, land, send_sems, recv_sems, local_sems, gather)
        for cp in local:
            cp.wait()
        for cp in remote:
            cp.wait_send()
            cp.wait_recv()

    out = pl.pallas_call(
        body, name=name,
        out_shape=(*[pltpu.HBM(a.shape, a.dtype) for a in srcs], *[pltpu.HBM(a.shape, a.dtype) for a in lands]),
        in_specs=[_HBM] * (2 * n) + [_SEM] * 3 + [pl.BlockSpec(memory_space=pl.ANY)] * len(after),
        out_specs=tuple([_HBM] * (2 * n)), input_output_aliases={i: i for i in range(2 * n)},
        compiler_params=pltpu.CompilerParams(has_side_effects=_EFFECT))(*srcs, *lands, *sems, *after)
    return list(out[n:])


def _cols_full(g):
    nd, r, c = g.shape
    return jnp.transpose(g, (1, 0, 2)).reshape(r, nd * c)


def _cols_split(full):
    r, n = full.shape
    return jnp.transpose(full.reshape(r, N_DEV, n // N_DEV), (1, 0, 2))


def _pack_b_in(w):
    qkv = 3 * MIX_W
    pad = jnp.zeros((w.shape[0], B_IN_PAD - w.shape[1]), w.dtype)
    return jnp.concatenate([w[:, :qkv], w[:, qkv + N_MIX_HEADS:], w[:, qkv:qkv + N_MIX_HEADS], pad], axis=1)


def _unpack_b_in(w):
    qkv = 3 * MIX_W
    return jnp.concatenate([w[:, :qkv], w[:, qkv + MEM_W:qkv + MEM_W + N_MIX_HEADS], w[:, qkv:qkv + MEM_W]], axis=1)


def _ffn_forward(x, xb, wgu, get_rest, tag, fused=True, target=None):
    if fused:
        wd4, gain, bias = get_rest(x)
        y, yb, gu, a, xh, rstd = _ffn_fwd_main(x, xb, wgu, wd4, gain, bias, f"{tag}_fwd_main", target)
    else:
        gu, a = _ffn_up(xb, wgu, f"{tag}_up")
        wd4, gain, bias = get_rest(a)
        y, yb, xh, rstd = _mm_res_ln(a, wd4, x, gain, bias, 0.5, f"{tag}_down_ln")
    return y, yb, (xb, gu, a, xh, rstd), wd4


def _ffn_backward(dy, saved, wgu, wd4, gain, tag, after=(), send=None):
    xb, gu, a, xh, rstd = saved
    s = xb.shape[0]
    nd, c, d = wgu.shape
    dx, dzb, dh, dgain, dbias = _ffn_bwd_main(dy, xh, rstd, gain, wd4, wgu, gu, f"{tag}_bwd_main", after,
                                               with_dx=send is None)
    dh = dh.reshape(nd, s, c)
    dwd = _mm_tn(a, dzb[None], f"{tag}_dwd").reshape(nd, wd4.shape[1] // 2, d)
    if send is not None:
        send("down", dwd, dgain, dbias)
    dwgu = _mm_tn(dh, xb[None], f"{tag}_dwgu")
    if send is not None:
        sent = send("gate_up", dwgu)
        dx = _mm_nt(dh, wgu, f"{tag}_dx", res=dx, w_rows_out=False, after=sent)
    return dx, dwgu, dwd, dgain, dbias


def _mixer_a_forward(x, xb, memb, w_in, w_kv, w_out, gain, bias, tabs):
    hb = _proj_rope(xb, w_in, tabs, 2 * MIX_W // LANES, "a_in", True)
    groups = [_band_fwd(hb, g, f"a_band_fwd{g}") for g in range(N_GROUPS)]
    oa, lt = _band_combine([o for o, _ in groups], [l for _, l in groups], "a_combine")
    kv = _mm_nn(memb, w_kv, BF, "a_mem_kv")
    om, lm = _mem_fwd(hb, 3 * MIX_W // LANES, kv, "a_mem_fwd")
    cat = jnp.concatenate([oa, om], axis=1)
    y, yb, xh, rstd = _mm_res_ln(cat[None], w_out[None], x, gain, bias, 1.0, "a_out_ln")
    return y, yb, (xb, hb, oa, lt, kv, lm, cat, xh, rstd)


def _mixer_a_backward(dy, saved, memb, w_in, w_kv, w_out, gain, tabs_neg, after=()):
    xb, hb, oa, lt, kv, lm, cat, xh, rstd = saved
    dz, dzb, dcat, dgain, dbias = _ln_bwd_proj(dy, xh, rstd, gain, w_out, "a_ln_bwd", after)
    dw_out = _mm_tn(cat[None], dzb[None], "a_dwout")[0]
    dqm, dkm, dvm = _mem_bwd(hb, 3 * MIX_W // LANES, kv, dcat, cat, GROUP_W // LANES, lm, "a_mem_bwd")
    dkv = jnp.concatenate([dkm, dvm], axis=1).astype(BF)
    dw_kv = _mm_tn(memb[None], dkv[None], "a_dwkv")[0]
    grads = [_band_bwd(hb, dcat, oa, lt, g, f"a_band_bwd{g}") for g in range(N_GROUPS)]
    dhb = _rope_cast([grads[g][i] for i in range(3) for g in range(N_GROUPS)] + [dqm], tabs_neg,
                     2 * MIX_W // LANES, "a_rope_bwd")
    dw_in = _mm_tn(dhb[None], xb[None], "a_dwin")[0]
    dx = _mm_nt(dhb[None], w_in[None], "a_dx", res=dz, w_rows_out=False)
    return dx, dw_in, dw_kv, dw_out, dgain, dbias


def _pad_rows(t, rows):
    return jnp.concatenate([t, jnp.zeros((rows - t.shape[0], t.shape[1]), t.dtype)], axis=0)


def _pad_cols(t, cols):
    return jnp.concatenate([t, jnp.zeros((t.shape[0], cols - t.shape[1]), t.dtype)], axis=1)


def _mixer_b_forward(x, xb, memb, w_in, fbias, w_kv, w_out, gain, bias, tabs):
    s = x.shape[0]
    hb, f = _proj_rope(xb, w_in, tabs, 0, "b_in", False, tail_block=(3 * MIX_W + MEM_W) // LANES)
    f_t = _pad_rows(jnp.transpose(f[:, :N_MIX_HEADS]), 16)
    bias16 = _pad_rows(jnp.transpose(fbias), 16)
    c_t = _gate_fwd(f_t, bias16, "b_gate_fwd")
    c_t3 = c_t[:N_MIX_HEADS].reshape(N_MIX_HEADS // 2, 2, s)
    c_rep = jnp.broadcast_to(c_t[:N_MIX_HEADS, :, None], (N_MIX_HEADS, s, LANES))
    ob, lb = _foxt_fwd(hb, c_rep, c_t3, "b_fox_fwd")
    kv = _mm_nn(memb, w_kv, BF, "b_mem_kv")
    om, lm = _mem_fwd(hb, 3 * MIX_W // LANES, kv, "b_mem_fwd")
    cat = jnp.concatenate([ob, om], axis=1)
    y, yb, xh, rstd = _mm_res_ln(cat[None], w_out[None], x, gain, bias, 1.0, "b_out_ln")
    return y, yb, (xb, hb, f_t, bias16, c_rep, c_t3, lb, kv, lm, cat, xh, rstd)


def _mixer_b_backward(dy, saved, memb, w_in, w_kv, w_out, gain, tabs, after=()):
    xb, hb, f_t, bias16, c_rep, c_t3, lb, kv, lm, cat, xh, rstd = saved
    s = xb.shape[0]
    dz, dzb, dcat, dgain, dbias = _ln_bwd_proj(dy, xh, rstd, gain, w_out, "b_ln_bwd", after)
    dw_out = _mm_tn(cat[None], dzb[None], "b_dwout")[0]
    dqm, dkm, dvm = _mem_bwd(hb, 3 * MIX_W // LANES, kv, dcat, cat, MIX_W // LANES, lm, "b_mem_bwd")
    dkv = jnp.concatenate([dkm, dvm], axis=1).astype(BF)
    dw_kv = _mm_tn(memb[None], dkv[None], "b_dwkv")[0]
    dsum = _foxt_dsum(hb, dcat, lb, c_rep, c_t3, "b_fox_dsum")
    dq, dk, dv, dc3 = _foxt_bwd(hb, dcat, dsum, lb, c_rep, c_t3, "b_fox_bwd")
    df_t, dfb = _gate_bwd(_pad_rows(dc3.reshape(N_MIX_HEADS, s), 16), f_t, bias16, "b_gate_bwd")
    df = _pad_cols(jnp.transpose(df_t[:N_MIX_HEADS]), B_IN_PAD - 3 * MIX_W - MEM_W)
    dhb = _rope_cast([dq, dk, dv, dqm, df], tabs, 0, "b_cast_bwd", transposed=(0, 1, 2))
    dw_in = _mm_tn(xb[None], dhb[None], "b_dwin")[0]
    dx = _mm_nt(dhb[None], w_in[None], "b_dx", res=dz)
    return dx, dw_in, jnp.transpose(dfb[:N_MIX_HEADS]), dw_kv, dw_out, dgain, dbias


def _stored(t, name):
    return jnp.transpose(t, (0, 2, 1)) if name in ROWS_OUT else t


GATHER_GROUPS = (
    (("ffn1_w_gate_up", 0),),
    (("ffn1_w_down", 0), ("ln_gain", None), ("ln_bias", None)),
    (("a_w_in", 0), ("a_w_out", 0), ("mem_w_kv", 0)),
    (("ffn2_w_gate_up", 0), ("ffn2_w_down", 0)),
    (("ffn1_w_gate_up", 1), ("ffn1_w_down", 1)),
    (("b_w_in", 0), ("b_w_out", 0), ("mem_w_kv", 1)),
    (("ffn2_w_gate_up", 1), ("ffn2_w_down", 1)),
)


def _group_shards(group, params):
    return [t if n in F32_COMM else _stored(t, n)[l].astype(BF) for (n, l), t in zip(group, params)]


def _weight_groups(w):
    return [_group_shards(grp, [w[n] for n, _ in grp]) for grp in GATHER_GROUPS]


def _local_step(x, mem, target, fbias, get_w, put_g):
    s, d = x.shape
    tabs = _rope_tables(s, 1.0)
    tabs_neg = _rope_tables(s, -1.0)
    memb = mem.astype(BF)
    saved, wl = [], []
    cur, curb = x, x.astype(BF)
    ln = []

    def down4(t):
        return t.reshape(N_DEV // 2, -1, d)

    for i in range(DEPTH):
        if i == 0:
            def first_rest(a):
                g = get_w(1, a)
                ln.extend(jnp.transpose(t, (1, 2, 0, 3)).reshape(DEPTH, 3, 1, d) for t in g[1:3])
                return down4(g[0]), ln[0][0, 0], ln[1][0, 0]

            wgu = get_w(0, cur)[0]
            cur, curb, s1, wd = _ffn_forward(cur, curb, wgu, first_rest, "l0_ffn1", fused=False)
        else:
            g = get_w(3 * i + 1, cur)
            wgu = g[0]
            cur, curb, s1, wd = _ffn_forward(cur, curb, wgu, lambda a, g=g: (down4(g[1]), ln[0][i, 0], ln[1][i, 0]),
                                             f"l{i}_ffn1")
        w1 = (wgu, wd)
        ln_g, ln_b = ln
        g = get_w(3 * i + 2, cur)
        if i == 0:
            wm = (g[0].reshape(-1, d), g[2].reshape(d, -1), _cols_full(g[1]))
            cur, curb, s2 = _mixer_a_forward(cur, curb, memb, wm[0], wm[1], wm[2], ln_g[i, 1], ln_b[i, 1], tabs)
        else:
            wm = (_pack_b_in(g[0].reshape(d, -1)), g[2].reshape(d, -1), g[1].reshape(d, -1))
            cur, curb, s2 = _mixer_b_forward(cur, curb, memb, wm[0], fbias, wm[1], wm[2], ln_g[i, 1], ln_b[i, 1],
                                             tabs)
        g = get_w(3 * i + 3, cur)
        cur, curb, s3, wd = _ffn_forward(cur, curb, g[0], lambda a, g=g: (down4(g[1]), ln_g[i, 2], ln_b[i, 2]),
                                         f"l{i}_ffn2", target=target if i == DEPTH - 1 else None)
        w3 = (g[0], wd)
        saved.append((s1, s2, s3))
        wl.append((w1, wm, w3))

    dy, loss = cur, curb

    dgs = [[None] * 3 for _ in range(DEPTH)]
    dbs = [[None] * 3 for _ in range(DEPTH)]
    sent = ()
    for i in reversed(range(DEPTH)):
        s1, s2, s3 = saved[i]
        w1, wm, w3 = wl[i]
        dy, dgu, dd, dgs[i][2], dbs[i][2] = _ffn_backward(dy, s3, w3[0], w3[1], ln_g[i, 2], f"l{i}_ffn2", sent)
        sent = put_g(3 * i + 2, [dgu, dd])
        if i == 0:
            dy, dw_in, dw_kv, dw_out, dgs[i][1], dbs[i][1] = _mixer_a_backward(
                dy, s2, memb, wm[0], wm[1], wm[2], ln_g[i, 1], tabs_neg, sent)
            sent = put_g(1, [dw_in.reshape(N_DEV, -1, d), _cols_split(dw_out),
                             dw_kv.reshape(N_DEV, d // N_DEV, -1)])
        else:
            dy, dw_in, dfb, dw_kv, dw_out, dgs[i][1], dbs[i][1] = _mixer_b_backward(
                dy, s2, memb, wm[0], wm[1], wm[2], ln_g[i, 1], tabs, sent)
            sent = put_g(4, [_unpack_b_in(dw_in).reshape(N_DEV, d // N_DEV, -1),
                             dw_out.reshape(N_DEV, d // N_DEV, -1), dw_kv.reshape(N_DEV, d // N_DEV, -1),
                             jnp.broadcast_to(dfb[None], (N_DEV,) + dfb.shape)])
        if i == 0:
            def send_last(kind, dw, dgain=None, dbias=None):
                if kind == "gate_up":
                    return put_g(6, [dw])
                dgs[0][0], dbs[0][0] = dgain, dbias
                ln_pieces = []
                for parts in (dgs, dbs):
                    t = jnp.concatenate([parts[a][b] for a in range(DEPTH) for b in range(3)], axis=0)
                    ln_pieces.append(jnp.transpose(t.reshape(DEPTH * 3, N_DEV, d // N_DEV), (1, 0, 2)))
                return put_g(0, [dw] + ln_pieces)

            dy = _ffn_backward(dy, s1, w1[0], w1[1], ln_g[i, 0], "l0_ffn1", sent, send_last)[0]
        else:
            dy, dgu, dd, dgs[i][0], dbs[i][0] = _ffn_backward(dy, s1, w1[0], w1[1], ln_g[i, 0], f"l{i}_ffn1", sent)
            sent = put_g(3, [dgu, dd])
    return loss, dy


WEIGHTS = ("ffn1_w_gate_up", "ffn1_w_down", "ffn2_w_gate_up", "ffn2_w_down", "ln_gain", "ln_bias", "mem_w_kv",
           "a_w_in", "a_w_out", "b_w_in", "b_forget_bias", "b_w_out")
F32_COMM = ("ln_gain", "ln_bias", "b_forget_bias")
ROWS_OUT = ("ffn1_w_gate_up", "ffn2_w_gate_up", "a_w_in")
GRAD_SLOTS = {
    "ffn1_w_gate_up": [(6, 0), (3, 0)], "ffn1_w_down": [(0, 0), (3, 1)],
    "ffn2_w_gate_up": [(2, 0), (5, 0)], "ffn2_w_down": [(2, 1), (5, 1)],
    "ln_gain": [(0, 1)], "ln_bias": [(0, 2)], "mem_w_kv": [(1, 2), (4, 2)],
    "a_w_in": [(1, 0)], "a_w_out": [(1, 1)], "b_w_in": [(4, 0)], "b_forget_bias": [(4, 3)], "b_w_out": [(4, 1)],
}


def kernel(x, mem, ffn1_w_gate_up, ffn1_w_down, ffn2_w_gate_up, ffn2_w_down, ln_gain, ln_bias, mem_w_kv, a_w_in, a_w_out, b_w_in, b_forget_bias, b_w_out, loss_target, m_ffn1_w_gate_up, m_ffn1_w_down, m_ffn2_w_gate_up, m_ffn2_w_down, m_ln_gain, m_ln_bias, m_mem_w_kv, m_a_w_in, m_a_w_out, m_b_w_in, m_b_forget_bias, m_b_w_out, v_ffn1_w_gate_up, v_ffn1_w_down, v_ffn2_w_gate_up, v_ffn2_w_down, v_ln_gain, v_ln_bias, v_mem_w_kv, v_a_w_in, v_a_w_out, v_b_w_in, v_b_forget_bias, v_b_w_out):
    w = dict(zip(WEIGHTS, (ffn1_w_gate_up, ffn1_w_down, ffn2_w_gate_up, ffn2_w_down, ln_gain, ln_bias, mem_w_kv,
                           a_w_in, a_w_out, b_w_in, b_forget_bias, b_w_out)))
    m = dict(zip(WEIGHTS, (m_ffn1_w_gate_up, m_ffn1_w_down, m_ffn2_w_gate_up, m_ffn2_w_down, m_ln_gain, m_ln_bias,
                           m_mem_w_kv, m_a_w_in, m_a_w_out, m_b_w_in, m_b_forget_bias, m_b_w_out)))
    v = dict(zip(WEIGHTS, (v_ffn1_w_gate_up, v_ffn1_w_down, v_ffn2_w_gate_up, v_ffn2_w_down, v_ln_gain, v_ln_bias,
                           v_mem_w_kv, v_a_w_in, v_a_w_out, v_b_w_in, v_b_forget_bias, v_b_w_out)))

    gathers = []
    for k, grp in enumerate(GATHER_GROUPS):
        params, behind = [w[n] for n, _ in grp], [_started(h) for h in gathers[-1:]]
        if behind:
            params, behind = lax.optimization_barrier((params, behind))
        gathers.append(_xfer_start(_group_shards(grp, params), True, f"gather{k}_start", behind))
    exchanges = {}

    def get_w(k, after):
        behind = [after] + ([_started(h) for h in gathers] if k == 0 else [])
        return _xfer_wait(gathers[k], behind, True, f"gather{k}_wait")

    def put_g(k, pieces):
        behind = [_started(exchanges[0])] if k == 6 else []
        exchanges[k] = _xfer_start(pieces, False, f"grads{k}_start", behind)
        return (_started(exchanges[k]),)

    loss, grad_x = _local_step(x[0], mem[0], loss_target[0], b_forget_bias, get_w, put_g)
    loss = lax.psum(loss[0, 0], ("x", "y", "c"))

    outs, landed = {}, {}

    def adamw(names):
        for n in names:
            contribs = [landed[g][j] for g, j in GRAD_SLOTS[n]]
            view = (len(contribs),) + contribs[0].shape[1:]
            shape = _stored(w[n], n).shape
            res = _reduce_adamw(contribs, *[_stored(t[n], n).reshape(view) for t in (w, m, v)], f"adamw_{n}")
            outs[n] = [_stored(t.reshape(shape), n) for t in res]
        return [outs[n][3] for n in names]

    after = [grad_x]
    for k in (5, 4, 3, 2, 1):
        landed[k] = _xfer_wait(exchanges[k], after, False, f"grads{k}_wait")
        after = [landed[k][0]]
    done = adamw(("ffn2_w_gate_up", "ffn2_w_down", "mem_w_kv", "a_w_in", "a_w_out", "b_w_in", "b_forget_bias",
                  "b_w_out"))
    landed[0] = _xfer_wait(exchanges[0], done, False, "grads0_wait")
    done = adamw(("ffn1_w_down", "ln_gain", "ln_bias"))
    landed[6] = _xfer_wait(exchanges[6], done, False, "grads6_wait")
    adamw(("ffn1_w_gate_up",))
    return (loss, grad_x[None], *[outs[n][0] for n in WEIGHTS], *[outs[n][1] for n in WEIGHTS],
            *[outs[n][2] for n in WEIGHTS], *[outs[n][3] for n in WEIGHTS])
```

```python
import functools

import jax
import jax.numpy as jnp
from jax import lax
from jax.experimental import pallas as pl
from jax.experimental.pallas import tpu as pltpu

F32 = jnp.float32
BF = jnp.bfloat16
MESH_ID = pl.DeviceIdType.MESH

N_DEV = 8
DEPTH = 2
HEAD_DIM = 64
LANES = 128
N_MIX_HEADS = 12
N_MEM_HEADS = 4
MIX_W = N_MIX_HEADS * HEAD_DIM
MEM_W = N_MEM_HEADS * HEAD_DIM
N_GROUPS = 3
GROUP_W = MIX_W // N_GROUPS
BLOCK = 128
BAND_SUB = 4
BAND_COLS = 4
ROT_HALF = 8
ROPE_THETA = 500000.0
ALPHA = (2 * DEPTH) ** 0.25
LN_EPS = 1e-5
SCALE = HEAD_DIM ** -0.5
NEG = -1e30
B_IN_PAD = 2688
ADAM_LR, ADAM_B1, ADAM_B2, ADAM_EPS, ADAM_WD, ADAM_STEP = 0.001, 0.9, 0.999, 1e-08, 0.01, 10
VMEM_LIMIT = 56 * 1024 * 1024


def _cp(*sem):
    return pltpu.CompilerParams(dimension_semantics=sem, vmem_limit_bytes=VMEM_LIMIT)


def _dot(a, b):
    return jnp.dot(a, b, preferred_element_type=F32)


def _dot_nt(a, b):
    return lax.dot_general(a, b, (((1,), (1,)), ((), ())), preferred_element_type=F32)


def _dot_tn(a, b):
    return lax.dot_general(a, b, (((0,), (0,)), ((), ())), preferred_element_type=F32)


def _sigmoid(x):
    return 1.0 / (1.0 + jnp.exp(-x))


def _tile(n, cap=1024):
    if n <= cap:
        return n
    best = LANES
    for t in range(LANES, cap + 1, LANES):
        if n % t == 0:
            best = t
    return best


def _rows(s, cap=512):
    return s if s <= cap else cap


def _mm_nn(a, b, out_dtype, name, b_rows_out=False):
    m, k = a.shape
    n = b.shape[0] if b_rows_out else b.shape[1]
    tm, tn = _rows(m), _tile(n)

    def body(a_ref, b_ref, o_ref):
        prod = _dot_nt(a_ref[...], b_ref[...]) if b_rows_out else _dot(a_ref[...], b_ref[...])
        o_ref[...] = prod.astype(o_ref.dtype)

    b_spec = (pl.BlockSpec((tn, k), lambda j, i: (j, 0)) if b_rows_out
              else pl.BlockSpec((k, tn), lambda j, i: (0, j)))
    return pl.pallas_call(
        body, name=name, grid=(n // tn, m // tm),
        in_specs=[pl.BlockSpec((tm, k), lambda j, i: (i, 0)), b_spec],
        out_specs=pl.BlockSpec((tm, tn), lambda j, i: (i, j)),
        out_shape=jax.ShapeDtypeStruct((m, n), out_dtype),
        compiler_params=_cp("parallel", "parallel"))(a, b)


def _resident(shape, index_map):
    return pl.BlockSpec(shape, index_map, pipeline_mode=pl.Buffered(1))


def _mm_tn(a, b, name, out_dtype=BF):
    na, s, m = a.shape
    nb, _, n = b.shape
    no = max(na, nb)
    tm, tn = _tile(m), _tile(n)

    def body(a_ref, b_ref, o_ref):
        o_ref[...] = _dot_tn(a_ref[...], b_ref[...]).astype(o_ref.dtype)

    def spec(nbatch, width, tile, index_map):
        fixed = nbatch == 1 and width == tile
        return _resident((None, s, tile), index_map) if fixed else pl.BlockSpec((None, s, tile), index_map)

    return pl.pallas_call(
        body, name=name, grid=(no, m // tm, n // tn),
        in_specs=[spec(na, m, tm, lambda j, r, c: (j if na > 1 else 0, 0, r)),
                  spec(nb, n, tn, lambda j, r, c: (j if nb > 1 else 0, 0, c))],
        out_specs=pl.BlockSpec((None, tm, tn), lambda j, r, c: (j, r, c)),
        out_shape=jax.ShapeDtypeStruct((no, m, n), out_dtype),
        compiler_params=_cp("parallel", "parallel", "parallel"))(a, b)


def _mm_nt(dh, w, name, res=None, out_dtype=F32, w_rows_out=True, after=()):
    nc, s, kc = dh.shape
    d = w.shape[1] if w_rows_out else w.shape[2]
    ts = _rows(s)
    has_res = res is not None
    mm = _dot_nt if w_rows_out else _dot

    def body(*refs):
        o_ref = refs[-1]
        dh_ref, w_ref = refs[:2]
        if has_res:
            r_ref = refs[2]
        out = mm(dh_ref[0], w_ref[0])
        for j in range(1, nc):
            out = out + mm(dh_ref[j], w_ref[j])
        if has_res:
            out = out + ALPHA * r_ref[...]
        o_ref[...] = out.astype(o_ref.dtype)

    in_specs = [pl.BlockSpec((nc, ts, kc), lambda i: (0, i, 0)), _resident(w.shape, lambda i: (0, 0, 0))]
    args = [dh, w]
    if has_res:
        in_specs.append(pl.BlockSpec((ts, d), lambda i: (i, 0)))
        args.append(res)
    in_specs += [pl.BlockSpec(memory_space=pl.ANY)] * len(after)
    args += list(after)
    return pl.pallas_call(
        body, name=name, grid=(s // ts,), in_specs=in_specs,
        out_specs=pl.BlockSpec((ts, d), lambda i: (i, 0)),
        out_shape=jax.ShapeDtypeStruct((s, d), out_dtype),
        compiler_params=_cp("parallel"))(*args)


def _mm_res_ln(a, w, x, gain, bias, fscale, name):
    nc, s, kc = a.shape
    d = w.shape[2]
    ts = _rows(s)

    def body(a_ref, w_ref, x_ref, g_ref, b_ref, y_ref, yb_ref, xh_ref, r_ref):
        f = _dot(a_ref[0], w_ref[0])
        for j in range(1, nc):
            f = f + _dot(a_ref[j], w_ref[j])
        z = ALPHA * x_ref[...] + fscale * f
        mu = jnp.mean(z, axis=-1, keepdims=True)
        zc = z - mu
        var = jnp.mean(zc * zc, axis=-1, keepdims=True)
        r = lax.rsqrt(var + LN_EPS)
        xh = zc * r
        y = xh * g_ref[...] + b_ref[...]
        y_ref[...] = y
        yb_ref[...] = y.astype(BF)
        xh_ref[...] = xh
        r_ref[...] = r

    row = pl.BlockSpec((ts, d), lambda i: (i, 0))
    vec = pl.BlockSpec((1, d), lambda i: (0, 0))
    return pl.pallas_call(
        body, name=name, grid=(s // ts,),
        in_specs=[pl.BlockSpec((nc, ts, kc), lambda i: (0, i, 0)), _resident((nc, kc, d), lambda i: (0, 0, 0)),
                  row, vec, vec],
        out_specs=[row, row, row, pl.BlockSpec((ts, 1), lambda i: (i, 0))],
        out_shape=[jax.ShapeDtypeStruct((s, d), F32), jax.ShapeDtypeStruct((s, d), BF),
                   jax.ShapeDtypeStruct((s, d), F32), jax.ShapeDtypeStruct((s, 1), F32)],
        compiler_params=_cp("parallel"))(a, w, x, gain, bias)


def _ln_bwd_proj(dy, xh, rstd, gain, w_out, name, after=()):
    s, d = dy.shape
    wc = w_out.shape[0]
    ts = _rows(s)
    na = len(after)

    def body(*refs):
        dy_ref, xh_ref, r_ref, g_ref, w_ref = refs[:5]
        dz_ref, dzb_ref, dc_ref, dg_ref, db_ref = refs[5 + na:]
        i = pl.program_id(0)
        dyv = dy_ref[...]
        xhv = xh_ref[...]
        dxh = dyv * g_ref[...]
        m1 = jnp.mean(dxh, axis=-1, keepdims=True)
        m2 = jnp.mean(dxh * xhv, axis=-1, keepdims=True)
        dz = r_ref[...] * (dxh - m1 - xhv * m2)
        dzb = dz.astype(BF)
        dz_ref[...] = dz
        dzb_ref[...] = dzb
        dc_ref[...] = _dot_nt(dzb, w_ref[...]).astype(BF)

        @pl.when(i == 0)
        def _():
            dg_ref[...] = jnp.zeros_like(dg_ref)
            db_ref[...] = jnp.zeros_like(db_ref)

        dg_ref[...] += jnp.sum(dyv * xhv, axis=0, keepdims=True)
        db_ref[...] += jnp.sum(dyv, axis=0, keepdims=True)

    row = pl.BlockSpec((ts, d), lambda i: (i, 0))
    vec = pl.BlockSpec((1, d), lambda i: (0, 0))
    return pl.pallas_call(
        body, name=name, grid=(s // ts,),
        in_specs=[row, row, pl.BlockSpec((ts, 1), lambda i: (i, 0)), vec, _resident((wc, d), lambda i: (0, 0))]
                 + [pl.BlockSpec(memory_space=pl.ANY)] * na,
        out_specs=[row, row, pl.BlockSpec((ts, wc), lambda i: (i, 0)), vec, vec],
        out_shape=[jax.ShapeDtypeStruct((s, d), F32), jax.ShapeDtypeStruct((s, d), BF),
                   jax.ShapeDtypeStruct((s, wc), BF),
                   jax.ShapeDtypeStruct((1, d), F32), jax.ShapeDtypeStruct((1, d), F32)],
        compiler_params=_cp("arbitrary"))(dy, xh, rstd, gain, w_out, *after)


def _ffn_up(xb, wgu, name):
    s, d = xb.shape
    c = wgu.shape[1]
    nch = wgu.shape[0] // 2
    ts = _rows(s, 1024)
    w4 = wgu.reshape(2, nch, c, d)

    def body(x_ref, w_ref, gu_ref, a_ref):
        x = x_ref[...]
        g = _dot_nt(x, w_ref[0])
        u = _dot_nt(x, w_ref[1])
        sg = _sigmoid(g)
        t = g * sg
        gu_ref[0] = (u * (sg * (1.0 + g - t))).astype(BF)
        gu_ref[1] = t.astype(BF)
        a_ref[...] = (t * u).astype(BF)

    return pl.pallas_call(
        body, name=name, grid=(nch, s // ts),
        in_specs=[pl.BlockSpec((ts, d), lambda j, i: (i, 0)),
                  pl.BlockSpec((2, None, c, d), lambda j, i: (0, j, 0, 0))],
        out_specs=[pl.BlockSpec((2, None, ts, c), lambda j, i: (0, j, i, 0)),
                   pl.BlockSpec((None, ts, c), lambda j, i: (j, i, 0))],
        out_shape=[jax.ShapeDtypeStruct((2, nch, s, c), BF), jax.ShapeDtypeStruct((nch, s, c), BF)],
        compiler_params=_cp("parallel", "parallel"))(xb, w4)


def _ffn_fwd_main(x, xb, wgu, wd4, gain, bias, name, target=None):
    s, d = x.shape
    nch, c = wd4.shape[0], wd4.shape[1]
    ts = _rows(s, 256)
    head = target is not None

    def body(*refs):
        x_ref, xb_ref, wgu_ref, wd_ref, g_ref, b_ref = refs[:6]
        y_ref, yb_ref, gu_ref, a_ref, xh_ref, r_ref = refs[6 + head:]
        xbv = xb_ref[...]
        f = jnp.zeros((ts, d), F32)
        for j in range(nch):
            g = _dot_nt(xbv, wgu_ref[j])
            u = _dot_nt(xbv, wgu_ref[nch + j])
            sg = _sigmoid(g)
            t = g * sg
            gu_ref[0, j] = (u * (sg * (1.0 + g - t))).astype(BF)
            gu_ref[1, j] = t.astype(BF)
            act = (t * u).astype(BF)
            a_ref[j] = act
            f = f + _dot(act, wd_ref[j])
        z = ALPHA * x_ref[...] + 0.5 * f
        mu = jnp.mean(z, axis=-1, keepdims=True)
        zc = z - mu
        var = jnp.mean(zc * zc, axis=-1, keepdims=True)
        r = lax.rsqrt(var + LN_EPS)
        xh = zc * r
        y = xh * g_ref[...] + b_ref[...]
        xh_ref[...] = xh
        r_ref[...] = r
        if head:
            e = y - refs[6][...]
            y_ref[...] = e * (1.0 / d)

            @pl.when(pl.program_id(0) == 0)
            def _():
                yb_ref[...] = jnp.zeros_like(yb_ref)

            part = jnp.sum(jnp.sum(e * e, axis=1, keepdims=True), axis=0, keepdims=True)
            yb_ref[...] += part * (0.5 / d)
        else:
            y_ref[...] = y
            yb_ref[...] = y.astype(BF)

    row = pl.BlockSpec((ts, d), lambda i: (i, 0))
    vec = pl.BlockSpec((1, d), lambda i: (0, 0))
    second = ((pl.BlockSpec((1, 1), lambda i: (0, 0)), jax.ShapeDtypeStruct((1, 1), F32)) if head
              else (row, jax.ShapeDtypeStruct((s, d), BF)))
    return pl.pallas_call(
        body, name=name, grid=(s // ts,),
        in_specs=[row, row, _resident(wgu.shape, lambda i: (0, 0, 0)), _resident(wd4.shape, lambda i: (0, 0, 0)),
                  vec, vec] + [row] * head,
        out_specs=[row, second[0], pl.BlockSpec((2, nch, ts, c), lambda i: (0, 0, i, 0)),
                   pl.BlockSpec((nch, ts, c), lambda i: (0, i, 0)), row, pl.BlockSpec((ts, 1), lambda i: (i, 0))],
        out_shape=[jax.ShapeDtypeStruct((s, d), F32), second[1],
                   jax.ShapeDtypeStruct((2, nch, s, c), BF), jax.ShapeDtypeStruct((nch, s, c), BF),
                   jax.ShapeDtypeStruct((s, d), F32), jax.ShapeDtypeStruct((s, 1), F32)],
        compiler_params=_cp("arbitrary" if head else "parallel"))(x, xb, wgu, wd4, gain, bias,
                                                                   *([target] if head else []))


def _ffn_bwd_main(dy, xh, rstd, gain, wd4, wgu, gu, name, after=(), with_dx=True):
    s, d = dy.shape
    nch, c = wd4.shape[0], wd4.shape[1]
    ts = _rows(s, 256)
    na = len(after)

    def body(*refs):
        dy_ref, xh_ref, r_ref, g_ref, wd_ref, wgu_ref, gu_ref = refs[:7]
        dx_ref, dzb_ref, dh_ref, dg_ref, db_ref = refs[7 + na:]
        i = pl.program_id(0)
        dyv = dy_ref[...]
        xhv = xh_ref[...]
        dxh = dyv * g_ref[...]
        m1 = jnp.mean(dxh, axis=-1, keepdims=True)
        m2 = jnp.mean(dxh * xhv, axis=-1, keepdims=True)
        dz = r_ref[...] * (dxh - m1 - xhv * m2)
        dzb = (0.5 * dz).astype(BF)
        dzb_ref[...] = dzb

        @pl.when(i == 0)
        def _():
            dg_ref[...] = jnp.zeros_like(dg_ref)
            db_ref[...] = jnp.zeros_like(db_ref)

        dg_ref[...] += jnp.sum(dyv * xhv, axis=0, keepdims=True)
        db_ref[...] += jnp.sum(dyv, axis=0, keepdims=True)

        dx = ALPHA * dz if with_dx else dz
        for j in range(nch):
            da = _dot_nt(dzb, wd_ref[j])
            dgate = (da * gu_ref[0, j].astype(F32)).astype(BF)
            dup = (da * gu_ref[1, j].astype(F32)).astype(BF)
            dh_ref[0, j] = dgate
            dh_ref[1, j] = dup
            if with_dx:
                dx = dx + _dot(dgate, wgu_ref[j]) + _dot(dup, wgu_ref[nch + j])
        dx_ref[...] = dx

    row = pl.BlockSpec((ts, d), lambda i: (i, 0))
    vec = pl.BlockSpec((1, d), lambda i: (0, 0))
    act = pl.BlockSpec((2, nch, ts, c), lambda i: (0, 0, i, 0))
    return pl.pallas_call(
        body, name=name, grid=(s // ts,),
        in_specs=[row, row, pl.BlockSpec((ts, 1), lambda i: (i, 0)), vec,
                  _resident(wd4.shape, lambda i: (0, 0, 0)), _resident(wgu.shape, lambda i: (0, 0, 0)), act]
                 + [pl.BlockSpec(memory_space=pl.ANY)] * na,
        out_specs=[row, row, act, vec, vec],
        out_shape=[jax.ShapeDtypeStruct((s, d), F32), jax.ShapeDtypeStruct((s, d), BF),
                   jax.ShapeDtypeStruct((2, nch, s, c), BF),
                   jax.ShapeDtypeStruct((1, d), F32), jax.ShapeDtypeStruct((1, d), F32)],
        compiler_params=_cp("arbitrary"))(dy, xh, rstd, gain, wd4, wgu, gu, *after)


def _rope_tables(s, sign):
    pos = jnp.arange(s, dtype=F32)
    inv_freq = 1.0 / (ROPE_THETA ** (jnp.arange(ROT_HALF, dtype=F32) / ROT_HALF))
    ang = pos[:, None] * inv_freq[None, :]
    cos, sin = jnp.cos(ang), jnp.sin(ang) * sign
    one = jnp.ones((s, HEAD_DIM - 2 * ROT_HALF), F32)
    zero = jnp.zeros((s, HEAD_DIM - 2 * ROT_HALF), F32)
    zh = jnp.zeros((s, ROT_HALF), F32)
    cos_f = jnp.concatenate([cos, cos, one], axis=1)
    sin_a = jnp.concatenate([-sin, zh, zero], axis=1)
    sin_b = jnp.concatenate([zh, sin, zero], axis=1)
    rep = LANES // HEAD_DIM
    return tuple(jnp.tile(t, (1, rep)) for t in (cos_f, sin_a, sin_b))


def _rope(t, c_ref, sa_ref, sb_ref):
    return (t * c_ref[...] + pltpu.roll(t, LANES - ROT_HALF, 1) * sa_ref[...]
            + pltpu.roll(t, ROT_HALF, 1) * sb_ref[...])


def _proj_rope(xb, w, tabs, n_rope, name, w_rows_out, tail_block=None):
    s, d = xb.shape
    n = w.shape[0] if w_rows_out else w.shape[1]
    tm = _rows(s, 256)
    has_tail = tail_block is not None

    def body(x_ref, w_ref, c_ref, sa_ref, sb_ref, o_ref, *tail_ref):
        h = (_dot_nt if w_rows_out else _dot)(x_ref[...], w_ref[...])
        for cb in range(n // LANES):
            t = h[:, cb * LANES:(cb + 1) * LANES]
            if cb < n_rope:
                t = _rope(t, c_ref, sa_ref, sb_ref)
            o_ref[:, cb * LANES:(cb + 1) * LANES] = t.astype(BF)
        if has_tail:
            tail_ref[0][...] = h[:, tail_block * LANES:(tail_block + 1) * LANES]

    tab = pl.BlockSpec((tm, LANES), lambda i: (i, 0))
    out_specs = [pl.BlockSpec((tm, n), lambda i: (i, 0))]
    out_shape = [jax.ShapeDtypeStruct((s, n), BF)]
    if has_tail:
        out_specs.append(tab)
        out_shape.append(jax.ShapeDtypeStruct((s, LANES), F32))
    res = pl.pallas_call(
        body, name=name, grid=(s // tm,),
        in_specs=[pl.BlockSpec((tm, d), lambda i: (i, 0)), _resident(w.shape, lambda i: (0, 0)), tab, tab, tab],
        out_specs=out_specs, out_shape=out_shape, compiler_params=_cp("parallel"))(xb, w, *tabs)
    return res if has_tail else res[0]


def _rope_cast(parts, tabs, n_rope, name, transposed=()):
    s = tabs[0].shape[0]
    flip = [i in transposed for i in range(len(parts))]
    widths = [p.shape[0] if f else p.shape[1] for p, f in zip(parts, flip)]
    n = sum(widths)
    npart = len(parts)
    ts = _rows(s, 256)

    def body(*refs):
        part_refs = refs[:npart]
        c_ref, sa_ref, sb_ref, o_ref = refs[npart:]
        col = 0
        for ref, w, f in zip(part_refs, widths, flip):
            for j in range(w // LANES):
                if f:
                    t = jnp.transpose(ref[j * LANES:(j + 1) * LANES, :])
                else:
                    t = ref[:, j * LANES:(j + 1) * LANES]
                if col < n_rope:
                    t = _rope(t, c_ref, sa_ref, sb_ref)
                o_ref[:, col * LANES:(col + 1) * LANES] = t.astype(BF)
                col += 1

    tab = pl.BlockSpec((ts, LANES), lambda i: (i, 0))
    return pl.pallas_call(
        body, name=name, grid=(s // ts,),
        in_specs=[pl.BlockSpec((w, ts), lambda i: (0, i)) if f else pl.BlockSpec((ts, w), lambda i: (i, 0))
                  for w, f in zip(widths, flip)] + [tab, tab, tab],
        out_specs=pl.BlockSpec((ts, n), lambda i: (i, 0)),
        out_shape=jax.ShapeDtypeStruct((s, n), BF),
        compiler_params=_cp("parallel"))(*parts, *tabs)


def _head_masks():
    lane = lax.broadcasted_iota(jnp.int32, (1, LANES), 1)
    return [lane < HEAD_DIM, lane >= HEAD_DIM]


def _sel(mask, v):
    return jnp.where(mask, v, jnp.zeros_like(v))


def _pick(mask, wide, fill):
    return jnp.max(jnp.where(mask, wide, fill), axis=1, keepdims=True)


def _head_stack(hm, a, b):
    return jnp.concatenate([_sel(hm[0], a), _sel(hm[0], b), _sel(hm[1], a), _sel(hm[1], b)], axis=0)


def _band_mask_stack(has_prev):
    qi = lax.broadcasted_iota(jnp.int32, (BLOCK, 4 * BLOCK), 0)
    col = lax.broadcasted_iota(jnp.int32, (BLOCK, 4 * BLOCK), 1)
    d = jnp.bitwise_and(col, BLOCK - 1) - qi
    is_prev = jnp.bitwise_and(col, BLOCK) != 0
    return jnp.where(is_prev, d - jnp.where(has_prev, 0, BLOCK), -d) >= 0


class _BandView:
    def __init__(self, s, g):
        self.r = 4 ** g
        self.nl = s // self.r
        self.nblk = self.nl // BLOCK
        self.nsub = min(BAND_SUB, self.nblk)
        self.tile = self.nsub * BLOCK
        self.ncols = min(self.r * GROUP_W // LANES, BAND_COLS)
        self.grid = (self.r * GROUP_W // LANES // self.ncols, self.nblk // self.nsub)

    def view(self, a):
        return a.reshape(self.nl, self.r * a.shape[1])

    def qkv(self, hb, g):
        npair = MIX_W // LANES
        offs = [i * npair + g * GROUP_W // LANES for i in range(3)]
        if self.r == 1:
            return [hb] * 3, hb.shape[1], offs
        return [self.view(hb[:, o * LANES:o * LANES + GROUP_W]) for o in offs], GROUP_W, [0, 0, 0]

    def specs(self, width, off):
        assert off % self.ncols == 0 and (width == GROUP_W or self.r == 1)
        nsub, last, lanes, first = self.nsub, self.nblk - 1, self.ncols * LANES, off // self.ncols
        return (pl.BlockSpec((self.tile, lanes), lambda cg, t: (t, first + cg)),
                pl.BlockSpec((BLOCK, lanes), lambda cg, t: (jnp.maximum(t * nsub - 1, 0), first + cg)),
                pl.BlockSpec((BLOCK, lanes), lambda cg, t: (jnp.minimum(t * nsub + nsub, last), first + cg)))


def _band_fwd(hb, g, name):
    s, n = hb.shape
    bv = _BandView(s, g)
    nsub = bv.nsub
    npair = MIX_W // LANES

    def body(q_ref, kc_ref, kp_ref, vc_ref, vp_ref, o_ref, l_ref):
        t = pl.program_id(1)
        hm = _head_masks()
        for i, c in [(i, c) for i in range(nsub) for c in range(bv.ncols)]:
            rows = slice(i * BLOCK, (i + 1) * BLOCK)
            lanes = slice(c * LANES, (c + 1) * LANES)
            has_prev = t > 0 if i == 0 else True
            q, kc, vc = q_ref[rows, lanes], kc_ref[rows, lanes], vc_ref[rows, lanes]
            if i == 0:
                kp, vp = kp_ref[:, lanes], vp_ref[:, lanes]
            else:
                prev = slice((i - 1) * BLOCK, i * BLOCK)
                kp, vp = kc_ref[prev, lanes], vc_ref[prev, lanes]
            sc = jnp.where(_band_mask_stack(has_prev), _dot_nt(q, _head_stack(hm, kc, kp)) * SCALE, NEG)
            ps, ms, ls = [], [], []
            for h in range(2):
                sh = sc[:, 2 * h * BLOCK:2 * (h + 1) * BLOCK]
                m = jnp.max(sh, axis=1, keepdims=True)
                p = jnp.exp(sh - m)
                ps.append(p.astype(BF))
                ms.append(m)
                ls.append(jnp.sum(p, axis=1, keepdims=True))
            o = _dot(jnp.concatenate(ps, axis=1), _head_stack(hm, vc, vp))
            o_ref[rows, lanes] = o / jnp.where(hm[0], ls[0], ls[1])
            l_ref[rows, lanes] = jnp.where(hm[0], ms[0] + jnp.log(ls[0]), ms[1] + jnp.log(ls[1]))

    (qv, kv_, vv), width, (qo, ko, vo) = bv.qkv(hb, g)
    q_cur, _, _ = bv.specs(width, qo)
    k_cur, k_prv, _ = bv.specs(width, ko)
    v_cur, v_prv, _ = bv.specs(width, vo)
    out_spec = bv.specs(GROUP_W, 0)[0]
    out = jax.ShapeDtypeStruct((bv.nl, bv.r * GROUP_W), F32)
    o, l = pl.pallas_call(
        body, name=name, grid=bv.grid,
        in_specs=[q_cur, k_cur, k_prv, v_cur, v_prv], out_specs=[out_spec, out_spec], out_shape=[out, out],
        compiler_params=_cp("parallel", "parallel"))(qv, kv_, kv_, vv, vv)
    return o.reshape(s, GROUP_W), l.reshape(s, GROUP_W)


def _band_combine(os, ls, name):
    ng = len(os)
    s, w = os[0].shape
    ts = _rows(s)

    def body(*refs):
        o_refs, l_refs = refs[:ng], refs[ng:2 * ng]
        oa_ref, lt_ref = refs[2 * ng:]
        lv = [r[...] for r in l_refs]
        m = functools.reduce(jnp.maximum, lv)
        es = [jnp.exp(l - m) for l in lv]
        den = functools.reduce(lambda a, b: a + b, es)
        num = functools.reduce(lambda a, b: a + b, [es[g] * o_refs[g][...] for g in range(ng)])
        oa_ref[...] = (num / den).astype(BF)
        lt_ref[...] = m + jnp.log(den)

    blk = pl.BlockSpec((ts, w), lambda i: (i, 0))
    return pl.pallas_call(
        body, name=name, grid=(s // ts,), in_specs=[blk] * (2 * ng), out_specs=[blk, blk],
        out_shape=[jax.ShapeDtypeStruct((s, w), BF), jax.ShapeDtypeStruct((s, w), F32)],
        compiler_params=_cp("parallel"))(*os, *ls)


def _band_bwd(hb, dcat, oa, lt, g, name):
    s, n = hb.shape
    bv = _BandView(s, g)
    nsub = bv.nsub
    npair = MIX_W // LANES
    ntile = bv.grid[1]

    def body(q_ref, qn_ref, kc_ref, kp_ref, vc_ref, vp_ref, do_ref, don_ref, oa_ref, oan_ref, lt_ref, ltn_ref,
             dq_ref, dk_ref, dv_ref):
        t = pl.program_id(1)
        hm = _head_masks()

        for i, c in [(i, c) for i in range(nsub) for c in range(bv.ncols)]:
            lanes = slice(c * LANES, (c + 1) * LANES)

            def block(ref, edge_ref, i, lanes=lanes):
                if i < 0 or i >= nsub:
                    return edge_ref[:, lanes]
                return ref[i * BLOCK:(i + 1) * BLOCK, lanes]

            has_prev = t > 0 if i == 0 else True
            has_next = t < ntile - 1 if i == nsub - 1 else True
            q, qn = block(q_ref, None, i), block(q_ref, qn_ref, i + 1)
            kc, kp = block(kc_ref, None, i), block(kc_ref, kp_ref, i - 1)
            vc, vp = block(vc_ref, None, i), block(vc_ref, vp_ref, i - 1)
            do, don = block(do_ref, None, i), block(do_ref, don_ref, i + 1)
            dd = do.astype(F32) * block(oa_ref, None, i).astype(F32)
            ddn = don.astype(F32) * block(oa_ref, oan_ref, i + 1).astype(F32)
            lt, ltn = block(lt_ref, None, i), block(lt_ref, ltn_ref, i + 1)

            def per_head(wide, width):
                col = lax.broadcasted_iota(jnp.int32, (BLOCK, 2 * width), 1)
                return jnp.where(col < width, _pick(hm[0], wide, NEG), _pick(hm[1], wide, NEG))

            def row_sums(prod, width):
                col = lax.broadcasted_iota(jnp.int32, (BLOCK, 2 * width), 1)
                return jnp.where(col < width, jnp.sum(_sel(hm[0], prod), axis=1, keepdims=True),
                                 jnp.sum(_sel(hm[1], prod), axis=1, keepdims=True))

            kst, vst = _head_stack(hm, kc, kp), _head_stack(hm, vc, vp)
            p = jnp.exp(jnp.where(_band_mask_stack(has_prev), _dot_nt(q, kst) * SCALE, NEG)
                        - per_head(lt, 2 * BLOCK))
            ds = p * (_dot_nt(do, vst) - row_sums(dd, 2 * BLOCK))
            dq_ref[i * BLOCK:(i + 1) * BLOCK, lanes] = SCALE * _dot(ds.astype(BF), kst)
            kcs = jnp.concatenate([_sel(hm[0], kc), _sel(hm[1], kc)], axis=0)
            vcs = jnp.concatenate([_sel(hm[0], vc), _sel(hm[1], vc)], axis=0)
            qi_ = lax.broadcasted_iota(jnp.int32, (BLOCK, 2 * BLOCK), 0)
            kj_ = jnp.bitwise_and(lax.broadcasted_iota(jnp.int32, (BLOCK, 2 * BLOCK), 1), BLOCK - 1)
            mn = kj_ >= qi_ + jnp.where(has_next, 0, BLOCK)
            pn = jnp.exp(jnp.where(mn, _dot_nt(qn, kcs) * SCALE, NEG) - per_head(ltn, BLOCK))
            dsn = pn * (_dot_nt(don, vcs) - row_sums(ddn, BLOCK))
            pb, dsb, pnb, dsnb = p.astype(BF), ds.astype(BF), pn.astype(BF), dsn.astype(BF)

            def own(x, h):
                return x[:, 2 * h * BLOCK:(2 * h + 1) * BLOCK]

            def nxt(x, h):
                return x[:, h * BLOCK:(h + 1) * BLOCK]

            ds_rows = jnp.concatenate([own(dsb, 0), nxt(dsnb, 0), own(dsb, 1), nxt(dsnb, 1)], axis=0)
            p_rows = jnp.concatenate([own(pb, 0), nxt(pnb, 0), own(pb, 1), nxt(pnb, 1)], axis=0)
            dk_ref[i * BLOCK:(i + 1) * BLOCK, lanes] = SCALE * _dot_tn(ds_rows, _head_stack(hm, q, qn))
            dv_ref[i * BLOCK:(i + 1) * BLOCK, lanes] = _dot_tn(p_rows, _head_stack(hm, do, don))

    (qv, kv_, vv), width, (qo, ko, vo) = bv.qkv(hb, g)
    q_cur, _, q_nxt = bv.specs(width, qo)
    k_cur, k_prv, _ = bv.specs(width, ko)
    v_cur, v_prv, _ = bv.specs(width, vo)
    w_cur, _, w_nxt = bv.specs(GROUP_W, 0)
    out = jax.ShapeDtypeStruct((bv.nl, bv.r * GROUP_W), F32)
    dv_, ov, lv = bv.view(dcat[:, :GROUP_W]), bv.view(oa), bv.view(lt)
    res = pl.pallas_call(
        body, name=name, grid=bv.grid,
        in_specs=[q_cur, q_nxt, k_cur, k_prv, v_cur, v_prv, w_cur, w_nxt, w_cur, w_nxt, w_cur, w_nxt],
        out_specs=[w_cur, w_cur, w_cur], out_shape=[out, out, out],
        compiler_params=_cp("parallel", "parallel"))(
            qv, qv, kv_, kv_, vv, vv, dv_, dv_, ov, ov, lv, lv)
    return [t.reshape(s, GROUP_W) for t in res]


def _mem_fwd(hb, q_blk0, kv, name):
    s = hb.shape[0]
    m = kv.shape[0]
    tq = _rows(s)
    npair = MEM_W // LANES

    def body(q_ref, k_ref, v_ref, o_ref, l_ref):
        q, k, v = q_ref[...], k_ref[...], v_ref[...]
        hm = _head_masks()
        o = jnp.zeros((tq, LANES), F32)
        lse_w = jnp.zeros((tq, LANES), F32)
        for h in range(2):
            sc = _dot_nt(_sel(hm[h], q), k) * SCALE
            mx = jnp.max(sc, axis=1, keepdims=True)
            p = jnp.exp(sc - mx)
            l = jnp.sum(p, axis=1, keepdims=True)
            o = o + _dot(p.astype(BF), _sel(hm[h], v)) / l
            lse_w = jnp.where(hm[h], mx + jnp.log(l), lse_w)
        o_ref[...] = o.astype(BF)
        l_ref[...] = lse_w

    blk = pl.BlockSpec((tq, LANES), lambda p, i: (i, p))
    return pl.pallas_call(
        body, name=name, grid=(npair, s // tq),
        in_specs=[pl.BlockSpec((tq, LANES), lambda p, i: (i, q_blk0 + p)),
                  pl.BlockSpec((m, LANES), lambda p, i: (0, p)),
                  pl.BlockSpec((m, LANES), lambda p, i: (0, npair + p))],
        out_specs=[blk, blk],
        out_shape=[jax.ShapeDtypeStruct((s, MEM_W), BF), jax.ShapeDtypeStruct((s, MEM_W), F32)],
        compiler_params=_cp("parallel", "parallel"))(hb, kv, kv)


def _mem_bwd(hb, q_blk0, kv, dcat, cat, o_blk0, lse, name):
    s = hb.shape[0]
    m = kv.shape[0]
    tq = _rows(s)
    npair = MEM_W // LANES

    def body(q_ref, k_ref, v_ref, do_ref, o_ref, l_ref, dq_ref, dk_ref, dv_ref):
        i = pl.program_id(1)

        @pl.when(i == 0)
        def _():
            dk_ref[...] = jnp.zeros_like(dk_ref)
            dv_ref[...] = jnp.zeros_like(dv_ref)

        q, k, v, do = q_ref[...], k_ref[...], v_ref[...], do_ref[...]
        dd = do.astype(F32) * o_ref[...].astype(F32)
        lt = l_ref[...]
        hm = _head_masks()
        dq = jnp.zeros((tq, LANES), F32)
        dk = jnp.zeros((m, LANES), F32)
        dv = jnp.zeros((m, LANES), F32)
        for h in range(2):
            qh, doh = _sel(hm[h], q), _sel(hm[h], do)
            p = jnp.exp(_dot_nt(qh, k) * SCALE - _pick(hm[h], lt, NEG))
            ds = p * (_dot_nt(doh, v) - jnp.sum(_sel(hm[h], dd), axis=1, keepdims=True))
            dq = dq + SCALE * _dot(ds.astype(BF), _sel(hm[h], k))
            dk = dk + SCALE * _dot_tn(ds.astype(BF), qh)
            dv = dv + _dot_tn(p.astype(BF), doh)
        dq_ref[...] = dq
        dk_ref[...] += dk
        dv_ref[...] += dv

    row = pl.BlockSpec((tq, LANES), lambda p, i: (i, p))
    orow = pl.BlockSpec((tq, LANES), lambda p, i: (i, o_blk0 + p))
    acc = pl.BlockSpec((m, LANES), lambda p, i: (0, p))
    return pl.pallas_call(
        body, name=name, grid=(npair, s // tq),
        in_specs=[pl.BlockSpec((tq, LANES), lambda p, i: (i, q_blk0 + p)),
                  pl.BlockSpec((m, LANES), lambda p, i: (0, p)),
                  pl.BlockSpec((m, LANES), lambda p, i: (0, npair + p)), orow, orow, row],
        out_specs=[row, acc, acc],
        out_shape=[jax.ShapeDtypeStruct((s, MEM_W), F32), jax.ShapeDtypeStruct((m, MEM_W), F32),
                   jax.ShapeDtypeStruct((m, MEM_W), F32)],
        compiler_params=_cp("parallel", "arbitrary"))(hb, kv, kv, dcat, cat, lse)


def _gate_fwd(f_t, bias, name):
    hp, s = f_t.shape
    nblk = s // LANES
    group = 8 if nblk % 8 == 0 else 1

    def body(f_ref, b_ref, c_ref):
        lane = lax.broadcasted_iota(jnp.int32, (hp, LANES), 1)

        def step(i, carry):
            scans = []
            for u in range(group):
                off = pl.multiple_of((i * group + u) * LANES, LANES)
                x = f_ref[:, pl.ds(off, LANES)] + b_ref[...]
                acc = jnp.minimum(x, 0.0) - jnp.log(1.0 + jnp.exp(-jnp.abs(x)))
                sh = 1
                while sh < LANES:
                    acc = acc + jnp.where(lane >= sh, pltpu.roll(acc, sh, 1), 0.0)
                    sh *= 2
                scans.append((off, acc))
            for off, acc in scans:
                acc = acc + carry
                c_ref[:, pl.ds(off, LANES)] = acc
                carry = acc[:, LANES - 1:LANES]
            return carry

        lax.fori_loop(0, nblk // group, step, jnp.zeros((hp, 1), F32))

    vm = pl.BlockSpec(memory_space=pltpu.VMEM)
    return pl.pallas_call(body, name=name, in_specs=[vm, vm], out_specs=vm,
                          out_shape=jax.ShapeDtypeStruct((hp, s), F32),
                          compiler_params=pltpu.CompilerParams(vmem_limit_bytes=VMEM_LIMIT))(f_t, bias)


def _gate_bwd(dc_t, f_t, bias, name):
    hp, s = f_t.shape
    nblk = s // LANES
    group = 8 if nblk % 8 == 0 else 1

    def body(dc_ref, f_ref, b_ref, df_ref, db_ref):
        lane = lax.broadcasted_iota(jnp.int32, (hp, LANES), 1)

        def step(t, carry):
            suffix, dbias = carry
            scans = []
            for u in range(group):
                off = pl.multiple_of((nblk - 1 - (t * group + u)) * LANES, LANES)
                acc = dc_ref[:, pl.ds(off, LANES)]
                sh = 1
                while sh < LANES:
                    acc = acc + jnp.where(lane < LANES - sh, pltpu.roll(acc, LANES - sh, 1), 0.0)
                    sh *= 2
                x = f_ref[:, pl.ds(off, LANES)] + b_ref[...]
                scans.append((off, acc, _sigmoid(-x)))
            for off, acc, sg in scans:
                acc = acc + suffix
                df = acc * sg
                df_ref[:, pl.ds(off, LANES)] = df
                suffix = acc[:, 0:1]
                dbias = dbias + jnp.sum(df, axis=1, keepdims=True)
            return suffix, dbias

        _, dbias = lax.fori_loop(0, nblk // group, step, (jnp.zeros((hp, 1), F32), jnp.zeros((hp, 1), F32)))
        db_ref[...] = dbias

    vm = pl.BlockSpec(memory_space=pltpu.VMEM)
    return pl.pallas_call(body, name=name, in_specs=[vm, vm, vm], out_specs=[vm, vm],
                          out_shape=[jax.ShapeDtypeStruct((hp, s), F32), jax.ShapeDtypeStruct((hp, 1), F32)],
                          compiler_params=pltpu.CompilerParams(vmem_limit_bytes=VMEM_LIMIT))(dc_t, f_t, bias)


def _wide(rep, width):
    return jnp.tile(rep, (1, width // LANES))


def _fold(t):
    part = t[:, :LANES]
    for c in range(1, t.shape[1] // LANES):
        part = part + t[:, c * LANES:(c + 1) * LANES]
    return part


def _foxt_logits(q, k, cq_row, ck_rep, mask, hmask):
    s = _dot_nt(_sel(hmask, k), q) + (cq_row - _wide(ck_rep, q.shape[0]))
    if mask is not None:
        s = jnp.where(mask, s, NEG)
    return s


def _causal_sub(ks, qs):
    shape = (ks.stop - ks.start, qs.stop - qs.start)
    return (ks.start + lax.broadcasted_iota(jnp.int32, shape, 0)
            <= qs.start + lax.broadcasted_iota(jnp.int32, shape, 1))


def _diag_blocks(t):
    h = t // 2
    return [(slice(0, h), slice(0, t)), (slice(h, t), slice(h, t))]


FOX_SPLIT = 1


def _fox_tiles(s):
    tq = _rows(s, 1024)
    return tq, tq // FOX_SPLIT, s // tq


def _fox_steps(nq):
    return FOX_SPLIT * nq * (nq + 1) // 2


def _count_ge(t, bounds):
    return sum([(t >= b).astype(jnp.int32) for b in bounds], jnp.int32(0))


def _sweep_q_major(t, nq):
    qi = _count_ge(t, [FOX_SPLIT * r * (r + 1) // 2 for r in range(1, nq)])
    return qi, t - FOX_SPLIT * qi * (qi + 1) // 2


def _sweep_k_major(t, nq):
    counts = [nq - j // FOX_SPLIT for j in range(FOX_SPLIT * nq)]
    offs = [sum(counts[:j]) for j in range(1, FOX_SPLIT * nq)]
    kj = _count_ge(t, offs)
    start = sum([jnp.where(t >= o, c, 0) for o, c in zip(offs, counts)], jnp.int32(0))
    qi = kj // FOX_SPLIT + (t - start)
    return kj, qi, t == start, qi == nq - 1


def _foxt_fwd(hb, c_rep, c_t3, name):
    s = hb.shape[0]
    npair = MIX_W // LANES
    tq, tk, nq = _fox_tiles(s)

    def body(q_ref, k_ref, v_ref, cq_ref, ck_ref, o_ref, l_ref, m_s, l_s, acc):
        qi, kj = _sweep_q_major(pl.program_id(1), nq)
        hm = _head_masks()

        @pl.when(kj == 0)
        def _():
            m_s[...] = jnp.full_like(m_s, NEG)
            l_s[...] = jnp.zeros_like(l_s)
            acc[...] = jnp.zeros_like(acc)

        def step(ks, qs, masked):
            q, k = q_ref[qs, :] * SCALE, k_ref[ks, :]
            vt = jnp.transpose(v_ref[ks, :])
            cq = cq_ref[:, qs]
            mask = _causal_sub(ks, qs) if masked else None
            for h in range(2):
                st = _foxt_logits(q, k, cq[h:h + 1, :], ck_ref[h, ks, :], mask, hm[h])
                m_old = m_s[h, :, qs]
                m_new = jnp.maximum(m_old, jnp.max(st, axis=0, keepdims=True))
                pt = jnp.exp(st - m_new)
                corr = jnp.exp(m_old - m_new)
                l_s[h, :, qs] = l_s[h, :, qs] * corr + jnp.sum(pt, axis=0, keepdims=True)
                acc[h, :, qs] = acc[h, :, qs] * corr + _dot(vt[h * HEAD_DIM:(h + 1) * HEAD_DIM, :], pt.astype(BF))
                m_s[h, :, qs] = m_new

        @pl.when(kj < qi)
        def _():
            step(slice(0, tk), slice(0, tq), False)

        @pl.when(kj == qi)
        def _():
            for ks, qs in _diag_blocks(tq):
                step(ks, qs, True)
            outs = []
            for h in range(2):
                outs.append(acc[h] / l_s[h])
                l_ref[h:h + 1, :] = m_s[h] + jnp.log(l_s[h])
            o_ref[...] = jnp.transpose(jnp.concatenate(outs, axis=0)).astype(BF)

    def q_map(p, t):
        return (_sweep_q_major(t, nq)[0], p)

    def kv_map(off):
        return lambda p, t: (_sweep_q_major(t, nq)[1], off + p)

    blk = pl.BlockSpec((tq, LANES), q_map)
    row = pl.BlockSpec((None, 2, tq), lambda p, t: (p, 0, _sweep_q_major(t, nq)[0]))
    return pl.pallas_call(
        body, name=name, grid=(npair, _fox_steps(nq)),
        in_specs=[blk, pl.BlockSpec((tk, LANES), kv_map(npair)), pl.BlockSpec((tk, LANES), kv_map(2 * npair)), row,
                  pl.BlockSpec((2, tk, LANES), lambda p, t: (p, _sweep_q_major(t, nq)[1], 0))],
        out_specs=[blk, row],
        out_shape=[jax.ShapeDtypeStruct((s, MIX_W), BF), jax.ShapeDtypeStruct((npair, 2, s), F32)],
        scratch_shapes=[pltpu.VMEM((2, 1, tq), F32), pltpu.VMEM((2, 1, tq), F32),
                        pltpu.VMEM((2, HEAD_DIM, tq), F32)],
        compiler_params=_cp("parallel", "arbitrary"))(hb, hb, hb, c_t3, c_rep)


def _foxt_dsum(hb, dcat, lse, c_rep, c_t3, name):
    s = hb.shape[0]
    npair = MIX_W // LANES
    tq, tk, nq = _fox_tiles(s)

    def body(q_ref, k_ref, v_ref, do_ref, l_ref, cq_ref, ck_ref, d_ref, acc):
        qi, kj = _sweep_q_major(pl.program_id(1), nq)
        hm = _head_masks()

        @pl.when(kj == 0)
        def _():
            acc[...] = jnp.zeros_like(acc)

        def step(ks, qs, masked):
            q, k, v, do = q_ref[qs, :] * SCALE, k_ref[ks, :], v_ref[ks, :], do_ref[qs, :]
            cq, lse_rows = cq_ref[:, qs], l_ref[:, qs]
            mask = _causal_sub(ks, qs) if masked else None
            for h in range(2):
                pt = jnp.exp(_foxt_logits(q, k, cq[h:h + 1, :], ck_ref[h, ks, :], mask, hm[h])
                             - lse_rows[h:h + 1, :])
                acc[h, :, qs] += jnp.sum(pt * _dot_nt(_sel(hm[h], v), do), axis=0, keepdims=True)

        @pl.when(kj < qi)
        def _():
            step(slice(0, tk), slice(0, tq), False)

        @pl.when(kj == qi)
        def _():
            for ks, qs in _diag_blocks(tq):
                step(ks, qs, True)
            for h in range(2):
                d_ref[h:h + 1, :] = acc[h]

    def q_map(p, t):
        return (_sweep_q_major(t, nq)[0], p)

    def kv_map(off):
        return lambda p, t: (_sweep_q_major(t, nq)[1], off + p)

    blk = pl.BlockSpec((tq, LANES), q_map)
    row = pl.BlockSpec((None, 2, tq), lambda p, t: (p, 0, _sweep_q_major(t, nq)[0]))
    return pl.pallas_call(
        body, name=name, grid=(npair, _fox_steps(nq)),
        in_specs=[blk, pl.BlockSpec((tk, LANES), kv_map(npair)), pl.BlockSpec((tk, LANES), kv_map(2 * npair)),
                  blk, row, row, pl.BlockSpec((2, tk, LANES), lambda p, t: (p, _sweep_q_major(t, nq)[1], 0))],
        out_specs=row, out_shape=jax.ShapeDtypeStruct((npair, 2, s), F32),
        scratch_shapes=[pltpu.VMEM((2, 1, tq), F32)],
        compiler_params=_cp("parallel", "arbitrary"))(hb, hb, hb, dcat, lse, c_t3, c_rep)


def _foxt_bwd(hb, dcat, dsum, lse, c_rep, c_t3, name):
    s = hb.shape[0]
    npair = MIX_W // LANES
    tq, tk, nq = _fox_tiles(s)

    def body(q_ref, k_ref, v_ref, do_ref, d_ref, l_ref, cq_ref, ck_ref, dq_ref, dk_ref, dv_ref, dc_ref, dc_s):
        t = pl.program_id(1)
        kj, qi, first, last = _sweep_k_major(t, nq)
        hm = _head_masks()

        @pl.when(first)
        def _():
            dk_ref[...] = jnp.zeros_like(dk_ref)
            dv_ref[...] = jnp.zeros_like(dv_ref)
            dc_s[...] = jnp.zeros_like(dc_s)

        @pl.when(t == 0)
        def _():
            dq_ref[...] = jnp.zeros_like(dq_ref)

        def step(ks, qs, masked):
            q, k, v, do = q_ref[qs, :] * SCALE, k_ref[ks, :], v_ref[ks, :], do_ref[qs, :]
            qt, kt, dot = jnp.transpose(q), jnp.transpose(k), jnp.transpose(do)
            cq, lse_rows, d_rows = cq_ref[:, qs], l_ref[:, qs], d_ref[:, qs]
            mask = _causal_sub(ks, qs) if masked else None
            dqs, dks, dvs = [], [], []
            for h in range(2):
                rows = slice(h * HEAD_DIM, (h + 1) * HEAD_DIM)
                pt = jnp.exp(_foxt_logits(q, k, cq[h:h + 1, :], ck_ref[h, ks, :], mask, hm[h])
                             - lse_rows[h:h + 1, :])
                dst = pt * (_dot_nt(_sel(hm[h], v), do) - d_rows[h:h + 1, :])
                dsb = dst.astype(BF)
                dqs.append(_dot(kt[rows, :], dsb))
                dks.append(_dot_nt(qt[rows, :], dsb))
                dvs.append(_dot_nt(dot[rows, :], pt.astype(BF)))
                dc_s[h, ks, :] += _fold(dst)
            cols = pl.ds(pl.multiple_of(qi * tq + qs.start, qs.stop - qs.start), qs.stop - qs.start)
            dq_ref[:, cols] += SCALE * jnp.concatenate(dqs, axis=0)
            dk_ref[:, ks] += jnp.concatenate(dks, axis=0)
            dv_ref[:, ks] += jnp.concatenate(dvs, axis=0)

        @pl.when(kj < qi)
        def _():
            step(slice(0, tk), slice(0, tq), False)

        @pl.when(kj == qi)
        def _():
            for ks, qs in _diag_blocks(tq):
                step(ks, qs, True)

        @pl.when(last)
        def _():
            for h in range(2):
                dc_ref[h:h + 1, :] = -jnp.sum(jnp.transpose(dc_s[h]), axis=0, keepdims=True)

    def kj_of(t):
        return _sweep_k_major(t, nq)[0]

    def qi_of(t):
        return _sweep_k_major(t, nq)[1]

    qblk = pl.BlockSpec((tq, LANES), lambda p, t: (qi_of(t), p))
    row = pl.BlockSpec((None, 2, tq), lambda p, t: (p, 0, qi_of(t)))
    kblk = pl.BlockSpec((LANES, tk), lambda p, t: (p, kj_of(t)))
    rep = pl.BlockSpec((2, tk, LANES), lambda p, t: (p, kj_of(t), 0))
    return pl.pallas_call(
        body, name=name, grid=(npair, _fox_steps(nq)),
        in_specs=[qblk,
                  pl.BlockSpec((tk, LANES), lambda p, t: (kj_of(t), npair + p)),
                  pl.BlockSpec((tk, LANES), lambda p, t: (kj_of(t), 2 * npair + p)),
                  qblk, row, row, row, rep],
        out_specs=[pl.BlockSpec((LANES, s), lambda p, t: (p, 0)), kblk, kblk,
                   pl.BlockSpec((None, 2, tk), lambda p, t: (p, 0, kj_of(t)))],
        out_shape=[jax.ShapeDtypeStruct((MIX_W, s), F32), jax.ShapeDtypeStruct((MIX_W, s), F32),
                   jax.ShapeDtypeStruct((MIX_W, s), F32), jax.ShapeDtypeStruct((npair, 2, s), F32)],
        scratch_shapes=[pltpu.VMEM((2, tk, LANES), F32)],
        compiler_params=_cp("arbitrary", "arbitrary"))(hb, hb, hb, dcat, dsum, lse, c_t3, c_rep)


def _adam_rows(r, c):
    cap = max(8, (1 << 20) // (4 * c))
    if r <= cap:
        return r
    best = None
    for t in range(8, cap + 1, 8):
        if r % t == 0:
            best = t
    return best if best is not None else r


def _reduce_adamw(contribs, w, m, v, name):
    nl = len(contribs)
    nd, r, c = contribs[0].shape
    tr = _adam_rows(r, c)
    bc1 = 1.0 - ADAM_B1 ** ADAM_STEP
    bc2 = 1.0 - ADAM_B2 ** ADAM_STEP

    def body(*refs):
        c_refs = refs[:nl]
        w_ref, m_ref, v_ref, g_ref, d_ref, nm_ref, nv_ref = refs[nl:]
        l = pl.program_id(0)
        for li in range(nl):
            @pl.when(l == li)
            def _(c_ref=c_refs[li]):
                g = c_ref[0].astype(F32)
                for k in range(1, nd):
                    g = g + c_ref[k].astype(F32)
                nm = ADAM_B1 * m_ref[...] + (1.0 - ADAM_B1) * g
                nv = ADAM_B2 * v_ref[...] + (1.0 - ADAM_B2) * (g * g)
                g_ref[...] = g
                nm_ref[...] = nm
                nv_ref[...] = nv
                d_ref[...] = -ADAM_LR * ((nm / bc1) / (jnp.sqrt(nv / bc2) + ADAM_EPS) + ADAM_WD * w_ref[...])

    def c_spec(li):
        return pl.BlockSpec((nd, tr, c), lambda l, i: (0, jnp.where(l == li, i, 0), 0))

    blk = pl.BlockSpec((None, tr, c), lambda l, i: (l, i, 0))
    out = jax.ShapeDtypeStruct((nl, r, c), F32)
    return pl.pallas_call(
        body, name=name, grid=(nl, r // tr),
        in_specs=[c_spec(li) for li in range(nl)] + [blk, blk, blk],
        out_specs=[blk, blk, blk, blk], out_shape=[out, out, out, out],
        compiler_params=_cp("arbitrary", "arbitrary"))(*contribs, w, m, v)


def _mesh_pos():
    return lax.axis_index("x"), lax.axis_index("y"), lax.axis_index("c")


def _peer(pos, k):
    x, y, c = pos
    return (1 - x if k & 4 else x, 1 - y if k & 2 else y, 1 - c if k & 1 else c)


def _linear(pos):
    return 4 * pos[0] + 2 * pos[1] + pos[2]


def _xfer_copies(srcs, lands, send_sems, recv_sems, local_sems, gather):
    pos = _mesh_pos()
    me = _linear(pos)
    local, remote = [], []
    for i, (src, land) in enumerate(zip(srcs, lands)):
        local.append(pltpu.make_async_copy(src if gather else src.at[me], land.at[me], local_sems.at[i]))
        for k in range(1, N_DEV):
            peer = _peer(pos, k)
            remote.append(pltpu.make_async_remote_copy(
                src_ref=src if gather else src.at[_linear(peer)], dst_ref=land.at[me],
                send_sem=send_sems.at[i * (N_DEV - 1) + k - 1], recv_sem=recv_sems.at[i * (N_DEV - 1) + k - 1],
                device_id=peer, device_id_type=MESH_ID))
    return local, remote


_HBM = pl.BlockSpec(memory_space=pltpu.HBM)
_SEM = pl.BlockSpec(memory_space=pltpu.SEMAPHORE)
_EFFECT = pltpu.SideEffectType.DATAFLOW_SIDE_EFFECTING


def _xfer_start(srcs, gather, name, after=()):
    n = len(srcs)
    na = len(after)
    lands = [lax.empty(((N_DEV,) + a.shape) if gather else a.shape, a.dtype) for a in srcs]

    def body(*refs):
        src, land = refs[:n], refs[n:2 * n]
        send_sems, recv_sems, local_sems = refs[2 * n + na:2 * n + na + 3]
        local, remote = _xfer_copies(src, land, send_sems, recv_sems, local_sems, gather)
        for cp in local + remote:
            cp.start()
        refs[-1][...] = jnp.zeros_like(refs[-1])

    nsem = n * (N_DEV - 1)
    out = pl.pallas_call(
        body, name=name,
        out_shape=(pltpu.SemaphoreType.DMA((nsem,)), pltpu.SemaphoreType.DMA((nsem,)), pltpu.SemaphoreType.DMA((n,)),
                   *[pltpu.HBM(a.shape, a.dtype) for a in srcs], *[pltpu.HBM(a.shape, a.dtype) for a in lands],
                   jax.ShapeDtypeStruct((8, LANES), F32)),
        in_specs=[_HBM] * (2 * n) + [pl.BlockSpec(memory_space=pl.ANY)] * na,
        out_specs=(_SEM, _SEM, _SEM, *[_HBM] * (2 * n), pl.BlockSpec(memory_space=pltpu.VMEM)),
        input_output_aliases={i: 3 + i for i in range(2 * n)},
        compiler_params=pltpu.CompilerParams(has_side_effects=_EFFECT))(
            *[pltpu.with_memory_space_constraint(a, pltpu.HBM) for a in srcs],
            *[pltpu.with_memory_space_constraint(a, pltpu.HBM) for a in lands], *after)
    return out[:3], list(out[3:3 + n]), list(out[3 + n:3 + 2 * n]), out[-1]


def _started(handle):
    return handle[3]


def _xfer_wait(handle, after, gather, name):
    sems, srcs, lands, _ = handle
    n = len(srcs)

    def body(*refs):
        src, land = refs[:n], refs[n:2 * n]
        send_sems, recv_sems, local_sems = refs[2 * n:2 * n + 3]
        local, remote = _xfer_copies(src, land, send_sems, recv_sems, local_sems, gather)
        for cp in local:
            cp.wait()
        for cp in remote:
            cp.wait_send()
            cp.wait_recv()

    out = pl.pallas_call(
        body, name=name,
        out_shape=(*[pltpu.HBM(a.shape, a.dtype) for a in srcs], *[pltpu.HBM(a.shape, a.dtype) for a in lands]),
        in_specs=[_HBM] * (2 * n) + [_SEM] * 3 + [pl.BlockSpec(memory_space=pl.ANY)] * len(after),
        out_specs=tuple([_HBM] * (2 * n)), input_output_aliases={i: i for i in range(2 * n)},
        compiler_params=pltpu.CompilerParams(has_side_effects=_EFFECT))(*srcs, *lands, *sems, *after)
    return list(out[n:])


def _cols_full(g):
    nd, r, c = g.shape
    return jnp.transpose(g, (1, 0, 2)).reshape(r, nd * c)


def _cols_split(full):
    r, n = full.shape
    return jnp.transpose(full.reshape(r, N_DEV, n // N_DEV), (1, 0, 2))


def _pack_b_in(w):
    qkv = 3 * MIX_W
    pad = jnp.zeros((w.shape[0], B_IN_PAD - w.shape[1]), w.dtype)
    return jnp.concatenate([w[:, :qkv], w[:, qkv + N_MIX_HEADS:], w[:, qkv:qkv + N_MIX_HEADS], pad], axis=1)


def _unpack_b_in(w):
    qkv = 3 * MIX_W
    return jnp.concatenate([w[:, :qkv], w[:, qkv + MEM_W:qkv + MEM_W + N_MIX_HEADS], w[:, qkv:qkv + MEM_W]], axis=1)


def _ffn_forward(x, xb, wgu, get_rest, tag, fused=True, target=None):
    if fused:
        wd4, gain, bias = get_rest(x)
        y, yb, gu, a, xh, rstd = _ffn_fwd_main(x, xb, wgu, wd4, gain, bias, f"{tag}_fwd_main", target)
    else:
        gu, a = _ffn_up(xb, wgu, f"{tag}_up")
        wd4, gain, bias = get_rest(a)
        y, yb, xh, rstd = _mm_res_ln(a, wd4, x, gain, bias, 0.5, f"{tag}_down_ln")
    return y, yb, (xb, gu, a, xh, rstd), wd4


def _ffn_backward(dy, saved, wgu, wd4, gain, tag, after=(), send=None):
    xb, gu, a, xh, rstd = saved
    s = xb.shape[0]
    nd, c, d = wgu.shape
    dx, dzb, dh, dgain, dbias = _ffn_bwd_main(dy, xh, rstd, gain, wd4, wgu, gu, f"{tag}_bwd_main", after,
                                               with_dx=send is None)
    dh = dh.reshape(nd, s, c)
    dwd = _mm_tn(a, dzb[None], f"{tag}_dwd").reshape(nd, wd4.shape[1] // 2, d)
    if send is not None:
        send("down", dwd, dgain, dbias)
    dwgu = _mm_tn(dh, xb[None], f"{tag}_dwgu")
    if send is not None:
        sent = send("gate_up", dwgu)
        dx = _mm_nt(dh, wgu, f"{tag}_dx", res=dx, w_rows_out=False, after=sent)
    return dx, dwgu, dwd, dgain, dbias


def _mixer_a_forward(x, xb, memb, w_in, w_kv, w_out, gain, bias, tabs):
    hb = _proj_rope(xb, w_in, tabs, 2 * MIX_W // LANES, "a_in", True)
    groups = [_band_fwd(hb, g, f"a_band_fwd{g}") for g in range(N_GROUPS)]
    oa, lt = _band_combine([o for o, _ in groups], [l for _, l in groups], "a_combine")
    kv = _mm_nn(memb, w_kv, BF, "a_mem_kv")
    om, lm = _mem_fwd(hb, 3 * MIX_W // LANES, kv, "a_mem_fwd")
    cat = jnp.concatenate([oa, om], axis=1)
    y, yb, xh, rstd = _mm_res_ln(cat[None], w_out[None], x, gain, bias, 1.0, "a_out_ln")
    return y, yb, (xb, hb, oa, lt, kv, lm, cat, xh, rstd)


def _mixer_a_backward(dy, saved, memb, w_in, w_kv, w_out, gain, tabs_neg, after=()):
    xb, hb, oa, lt, kv, lm, cat, xh, rstd = saved
    dz, dzb, dcat, dgain, dbias = _ln_bwd_proj(dy, xh, rstd, gain, w_out, "a_ln_bwd", after)
    dw_out = _mm_tn(cat[None], dzb[None], "a_dwout")[0]
    dqm, dkm, dvm = _mem_bwd(hb, 3 * MIX_W // LANES, kv, dcat, cat, GROUP_W // LANES, lm, "a_mem_bwd")
    dkv = jnp.concatenate([dkm, dvm], axis=1).astype(BF)
    dw_kv = _mm_tn(memb[None], dkv[None], "a_dwkv")[0]
    grads = [_band_bwd(hb, dcat, oa, lt, g, f"a_band_bwd{g}") for g in range(N_GROUPS)]
    dhb = _rope_cast([grads[g][i] for i in range(3) for g in range(N_GROUPS)] + [dqm], tabs_neg,
                     2 * MIX_W // LANES, "a_rope_bwd")
    dw_in = _mm_tn(dhb[None], xb[None], "a_dwin")[0]
    dx = _mm_nt(dhb[None], w_in[None], "a_dx", res=dz, w_rows_out=False)
    return dx, dw_in, dw_kv, dw_out, dgain, dbias


def _pad_rows(t, rows):
    return jnp.concatenate([t, jnp.zeros((rows - t.shape[0], t.shape[1]), t.dtype)], axis=0)


def _pad_cols(t, cols):
    return jnp.concatenate([t, jnp.zeros((t.shape[0], cols - t.shape[1]), t.dtype)], axis=1)


def _mixer_b_forward(x, xb, memb, w_in, fbias, w_kv, w_out, gain, bias, tabs):
    s = x.shape[0]
    hb, f = _proj_rope(xb, w_in, tabs, 0, "b_in", False, tail_block=(3 * MIX_W + MEM_W) // LANES)
    f_t = _pad_rows(jnp.transpose(f[:, :N_MIX_HEADS]), 16)
    bias16 = _pad_rows(jnp.transpose(fbias), 16)
    c_t = _gate_fwd(f_t, bias16, "b_gate_fwd")
    c_t3 = c_t[:N_MIX_HEADS].reshape(N_MIX_HEADS // 2, 2, s)
    c_rep = jnp.broadcast_to(c_t[:N_MIX_HEADS, :, None], (N_MIX_HEADS, s, LANES))
    ob, lb = _foxt_fwd(hb, c_rep, c_t3, "b_fox_fwd")
    kv = _mm_nn(memb, w_kv, BF, "b_mem_kv")
    om, lm = _mem_fwd(hb, 3 * MIX_W // LANES, kv, "b_mem_fwd")
    cat = jnp.concatenate([ob, om], axis=1)
    y, yb, xh, rstd = _mm_res_ln(cat[None], w_out[None], x, gain, bias, 1.0, "b_out_ln")
    return y, yb, (xb, hb, f_t, bias16, c_rep, c_t3, lb, kv, lm, cat, xh, rstd)


def _mixer_b_backward(dy, saved, memb, w_in, w_kv, w_out, gain, tabs, after=()):
    xb, hb, f_t, bias16, c_rep, c_t3, lb, kv, lm, cat, xh, rstd = saved
    s = xb.shape[0]
    dz, dzb, dcat, dgain, dbias = _ln_bwd_proj(dy, xh, rstd, gain, w_out, "b_ln_bwd", after)
    dw_out = _mm_tn(cat[None], dzb[None], "b_dwout")[0]
    dqm, dkm, dvm = _mem_bwd(hb, 3 * MIX_W // LANES, kv, dcat, cat, MIX_W // LANES, lm, "b_mem_bwd")
    dkv = jnp.concatenate([dkm, dvm], axis=1).astype(BF)
    dw_kv = _mm_tn(memb[None], dkv[None], "b_dwkv")[0]
    dsum = _foxt_dsum(hb, dcat, lb, c_rep, c_t3, "b_fox_dsum")
    dq, dk, dv, dc3 = _foxt_bwd(hb, dcat, dsum, lb, c_rep, c_t3, "b_fox_bwd")
    df_t, dfb = _gate_bwd(_pad_rows(dc3.reshape(N_MIX_HEADS, s), 16), f_t, bias16, "b_gate_bwd")
    df = _pad_cols(jnp.transpose(df_t[:N_MIX_HEADS]), B_IN_PAD - 3 * MIX_W - MEM_W)
    dhb = _rope_cast([dq, dk, dv, dqm, df], tabs, 0, "b_cast_bwd", transposed=(0, 1, 2))
    dw_in = _mm_tn(xb[None], dhb[None], "b_dwin")[0]
    dx = _mm_nt(dhb[None], w_in[None], "b_dx", res=dz)
    return dx, dw_in, jnp.transpose(dfb[:N_MIX_HEADS]), dw_kv, dw_out, dgain, dbias


def _stored(t, name):
    return jnp.transpose(t, (0, 2, 1)) if name in ROWS_OUT else t


GATHER_GROUPS = (
    (("ffn1_w_gate_up", 0),),
    (("ffn1_w_down", 0), ("ln_gain", None), ("ln_bias", None)),
    (("a_w_in", 0), ("a_w_out", 0), ("mem_w_kv", 0)),
    (("ffn2_w_gate_up", 0), ("ffn2_w_down", 0)),
    (("ffn1_w_gate_up", 1), ("ffn1_w_down", 1)),
    (("b_w_in", 0), ("b_w_out", 0), ("mem_w_kv", 1)),
    (("ffn2_w_gate_up", 1), ("ffn2_w_down", 1)),
)


def _group_shards(group, params):
    return [t if n in F32_COMM else _stored(t, n)[l].astype(BF) for (n, l), t in zip(group, params)]


def _weight_groups(w):
    return [_group_shards(grp, [w[n] for n, _ in grp]) for grp in GATHER_GROUPS]


def _local_step(x, mem, target, fbias, get_w, put_g):
    s, d = x.shape
    tabs = _rope_tables(s, 1.0)
    tabs_neg = _rope_tables(s, -1.0)
    memb = mem.astype(BF)
    saved, wl = [], []
    cur, curb = x, x.astype(BF)
    ln = []

    def down4(t):
        return t.reshape(N_DEV // 2, -1, d)

    for i in range(DEPTH):
        if i == 0:
            def first_rest(a):
                g = get_w(1, a)
                ln.extend(jnp.transpose(t, (1, 2, 0, 3)).reshape(DEPTH, 3, 1, d) for t in g[1:3])
                return down4(g[0]), ln[0][0, 0], ln[1][0, 0]

            wgu = get_w(0, cur)[0]
            cur, curb, s1, wd = _ffn_forward(cur, curb, wgu, first_rest, "l0_ffn1", fused=False)
        else:
            g = get_w(3 * i + 1, cur)
            wgu = g[0]
            cur, curb, s1, wd = _ffn_forward(cur, curb, wgu, lambda a, g=g: (down4(g[1]), ln[0][i, 0], ln[1][i, 0]),
                                             f"l{i}_ffn1")
        w1 = (wgu, wd)
        ln_g, ln_b = ln
        g = get_w(3 * i + 2, cur)
        if i == 0:
            wm = (g[0].reshape(-1, d), g[2].reshape(d, -1), _cols_full(g[1]))
            cur, curb, s2 = _mixer_a_forward(cur, curb, memb, wm[0], wm[1], wm[2], ln_g[i, 1], ln_b[i, 1], tabs)
        else:
            wm = (_pack_b_in(g[0].reshape(d, -1)), g[2].reshape(d, -1), g[1].reshape(d, -1))
            cur, curb, s2 = _mixer_b_forward(cur, curb, memb, wm[0], fbias, wm[1], wm[2], ln_g[i, 1], ln_b[i, 1],
                                             tabs)
        g = get_w(3 * i + 3, cur)
        cur, curb, s3, wd = _ffn_forward(cur, curb, g[0], lambda a, g=g: (down4(g[1]), ln_g[i, 2], ln_b[i, 2]),
                                         f"l{i}_ffn2", target=target if i == DEPTH - 1 else None)
        w3 = (g[0], wd)
        saved.append((s1, s2, s3))
        wl.append((w1, wm, w3))

    dy, loss = cur, curb

    dgs = [[None] * 3 for _ in range(DEPTH)]
    dbs = [[None] * 3 for _ in range(DEPTH)]
    sent = ()
    for i in reversed(range(DEPTH)):
        s1, s2, s3 = saved[i]
        w1, wm, w3 = wl[i]
        dy, dgu, dd, dgs[i][2], dbs[i][2] = _ffn_backward(dy, s3, w3[0], w3[1], ln_g[i, 2], f"l{i}_ffn2", sent)
        sent = put_g(3 * i + 2, [dgu, dd])
        if i == 0:
            dy, dw_in, dw_kv, dw_out, dgs[i][1], dbs[i][1] = _mixer_a_backward(
                dy, s2, memb, wm[0], wm[1], wm[2], ln_g[i, 1], tabs_neg, sent)
            sent = put_g(1, [dw_in.reshape(N_DEV, -1, d), _cols_split(dw_out),
                             dw_kv.reshape(N_DEV, d // N_DEV, -1)])
        else:
            dy, dw_in, dfb, dw_kv, dw_out, dgs[i][1], dbs[i][1] = _mixer_b_backward(
                dy, s2, memb, wm[0], wm[1], wm[2], ln_g[i, 1], tabs, sent)
            sent = put_g(4, [_unpack_b_in(dw_in).reshape(N_DEV, d // N_DEV, -1),
                             dw_out.reshape(N_DEV, d // N_DEV, -1), dw_kv.reshape(N_DEV, d // N_DEV, -1),
                             jnp.broadcast_to(dfb[None], (N_DEV,) + dfb.shape)])
        if i == 0:
            def send_last(kind, dw, dgain=None, dbias=None):
                if kind == "gate_up":
                    return put_g(6, [dw])
                dgs[0][0], dbs[0][0] = dgain, dbias
                ln_pieces = []
                for parts in (dgs, dbs):
                    t = jnp.concatenate([parts[a][b] for a in range(DEPTH) for b in range(3)], axis=0)
                    ln_pieces.append(jnp.transpose(t.reshape(DEPTH * 3, N_DEV, d // N_DEV), (1, 0, 2)))
                return put_g(0, [dw] + ln_pieces)

            dy = _ffn_backward(dy, s1, w1[0], w1[1], ln_g[i, 0], "l0_ffn1", sent, send_last)[0]
        else:
            dy, dgu, dd, dgs[i][0], dbs[i][0] = _ffn_backward(dy, s1, w1[0], w1[1], ln_g[i, 0], f"l{i}_ffn1", sent)
            sent = put_g(3, [dgu, dd])
    return loss, dy


WEIGHTS = ("ffn1_w_gate_up", "ffn1_w_down", "ffn2_w_gate_up", "ffn2_w_down", "ln_gain", "ln_bias", "mem_w_kv",
           "a_w_in", "a_w_out", "b_w_in", "b_forget_bias", "b_w_out")
F32_COMM = ("ln_gain", "ln_bias", "b_forget_bias")
ROWS_OUT = ("ffn1_w_gate_up", "ffn2_w_gate_up", "a_w_in")
GRAD_SLOTS = {
    "ffn1_w_gate_up": [(6, 0), (3, 0)], "ffn1_w_down": [(0, 0), (3, 1)],
    "ffn2_w_gate_up": [(2, 0), (5, 0)], "ffn2_w_down": [(2, 1), (5, 1)],
    "ln_gain": [(0, 1)], "ln_bias": [(0, 2)], "mem_w_kv": [(1, 2), (4, 2)],
    "a_w_in": [(1, 0)], "a_w_out": [(1, 1)], "b_w_in": [(4, 0)], "b_forget_bias": [(4, 3)], "b_w_out": [(4, 1)],
}


def kernel(x, mem, ffn1_w_gate_up, ffn1_w_down, ffn2_w_gate_up, ffn2_w_down, ln_gain, ln_bias, mem_w_kv, a_w_in, a_w_out, b_w_in, b_forget_bias, b_w_out, loss_target, m_ffn1_w_gate_up, m_ffn1_w_down, m_ffn2_w_gate_up, m_ffn2_w_down, m_ln_gain, m_ln_bias, m_mem_w_kv, m_a_w_in, m_a_w_out, m_b_w_in, m_b_forget_bias, m_b_w_out, v_ffn1_w_gate_up, v_ffn1_w_down, v_ffn2_w_gate_up, v_ffn2_w_down, v_ln_gain, v_ln_bias, v_mem_w_kv, v_a_w_in, v_a_w_out, v_b_w_in, v_b_forget_bias, v_b_w_out):
    w = dict(zip(WEIGHTS, (ffn1_w_gate_up, ffn1_w_down, ffn2_w_gate_up, ffn2_w_down, ln_gain, ln_bias, mem_w_kv,
                           a_w_in, a_w_out, b_w_in, b_forget_bias, b_w_out)))
    m = dict(zip(WEIGHTS, (m_ffn1_w_gate_up, m_ffn1_w_down, m_ffn2_w_gate_up, m_ffn2_w_down, m_ln_gain, m_ln_bias,
                           m_mem_w_kv, m_a_w_in, m_a_w_out, m_b_w_in, m_b_forget_bias, m_b_w_out)))
    v = dict(zip(WEIGHTS, (v_ffn1_w_gate_up, v_ffn1_w_down, v_ffn2_w_gate_up, v_ffn2_w_down, v_ln_gain, v_ln_bias,
                           v_mem_w_kv, v_a_w_in, v_a_w_out, v_b_w_in, v_b_forget_bias, v_b_w_out)))

    gathers = []
    for k, grp in enumerate(GATHER_GROUPS):
        params, behind = [w[n] for n, _ in grp], [_started(h) for h in gathers[-1:]]
        if behind:
            params, behind = lax.optimization_barrier((params, behind))
        gathers.append(_xfer_start(_group_shards(grp, params), True, f"gather{k}_start", behind))
    exchanges = {}

    def get_w(k, after):
        behind = [after] + ([_started(h) for h in gathers] if k == 0 else [])
        return _xfer_wait(gathers[k], behind, True, f"gather{k}_wait")

    def put_g(k, pieces):
        behind = [_started(exchanges[0])] if k == 6 else []
        exchanges[k] = _xfer_start(pieces, False, f"grads{k}_start", behind)
        return (_started(exchanges[k]),)

    loss, grad_x = _local_step(x[0], mem[0], loss_target[0], b_forget_bias, get_w, put_g)
    loss = lax.psum(loss[0, 0], ("x", "y", "c"))

    outs, landed = {}, {}

    def adamw(names):
        for n in names:
            contribs = [landed[g][j] for g, j in GRAD_SLOTS[n]]
            view = (len(contribs),) + contribs[0].shape[1:]
            shape = _stored(w[n], n).shape
            res = _reduce_adamw(contribs, *[_stored(t[n], n).reshape(view) for t in (w, m, v)], f"adamw_{n}")
            outs[n] = [_stored(t.reshape(shape), n) for t in res]
        return [outs[n][3] for n in names]

    after = [grad_x]
    for k in (5, 4, 3, 2, 1):
        landed[k] = _xfer_wait(exchanges[k], after, False, f"grads{k}_wait")
        after = [landed[k][0]]
    done = adamw(("ffn2_w_gate_up", "ffn2_w_down", "mem_w_kv", "a_w_in", "a_w_out", "b_w_in", "b_forget_bias",
                  "b_w_out"))
    landed[0] = _xfer_wait(exchanges[0], done, False, "grads0_wait")
    done = adamw(("ffn1_w_down", "ln_gain", "ln_bias"))
    landed[6] = _xfer_wait(exchanges[6], done, False, "grads6_wait")
    adamw(("ffn1_w_gate_up",))
    return (loss, grad_x[None], *[outs[n][0] for n in WEIGHTS], *[outs[n][1] for n in WEIGHTS],
            *[outs[n][2] for n in WEIGHTS], *[outs[n][3] for n in WEIGHTS])
```

```python
import functools

import jax
import jax.numpy as jnp
from jax import lax
from jax.experimental import pallas as pl
from jax.experimental.pallas import tpu as pltpu

F32 = jnp.float32
BF = jnp.bfloat16
MESH_ID = pl.DeviceIdType.MESH

N_DEV = 8
DEPTH = 2
HEAD_DIM = 64
LANES = 128
N_MIX_HEADS = 12
N_MEM_HEADS = 4
MIX_W = N_MIX_HEADS * HEAD_DIM
MEM_W = N_MEM_HEADS * HEAD_DIM
N_GROUPS = 3
GROUP_W = MIX_W // N_GROUPS
BLOCK = 128
BAND_SUB = 4
BAND_COLS = 4
ROT_HALF = 8
ROPE_THETA = 500000.0
ALPHA = (2 * DEPTH) ** 0.25
LN_EPS = 1e-5
SCALE = HEAD_DIM ** -0.5
NEG = -1e30
B_IN_PAD = 2688
ADAM_LR, ADAM_B1, ADAM_B2, ADAM_EPS, ADAM_WD, ADAM_STEP = 0.001, 0.9, 0.999, 1e-08, 0.01, 10
VMEM_LIMIT = 56 * 1024 * 1024


def _cp(*sem):
    return pltpu.CompilerParams(dimension_semantics=sem, vmem_limit_bytes=VMEM_LIMIT)


def _dot(a, b):
    return jnp.dot(a, b, preferred_element_type=F32)


def _dot_nt(a, b):
    return lax.dot_general(a, b, (((1,), (1,)), ((), ())), preferred_element_type=F32)


def _dot_tn(a, b):
    return lax.dot_general(a, b, (((0,), (0,)), ((), ())), preferred_element_type=F32)


def _sigmoid(x):
    return 1.0 / (1.0 + jnp.exp(-x))


def _tile(n, cap=1024):
    if n <= cap:
        return n
    best = LANES
    for t in range(LANES, cap + 1, LANES):
        if n % t == 0:
            best = t
    return best


def _rows(s, cap=512):
    return s if s <= cap else cap


def _mm_nn(a, b, out_dtype, name, b_rows_out=False):
    m, k = a.shape
    n = b.shape[0] if b_rows_out else b.shape[1]
    tm, tn = _rows(m), _tile(n)

    def body(a_ref, b_ref, o_ref):
        prod = _dot_nt(a_ref[...], b_ref[...]) if b_rows_out else _dot(a_ref[...], b_ref[...])
        o_ref[...] = prod.astype(o_ref.dtype)

    b_spec = (pl.BlockSpec((tn, k), lambda j, i: (j, 0)) if b_rows_out
              else pl.BlockSpec((k, tn), lambda j, i: (0, j)))
    return pl.pallas_call(
        body, name=name, grid=(n // tn, m // tm),
        in_specs=[pl.BlockSpec((tm, k), lambda j, i: (i, 0)), b_spec],
        out_specs=pl.BlockSpec((tm, tn), lambda j, i: (i, j)),
        out_shape=jax.ShapeDtypeStruct((m, n), out_dtype),
        compiler_params=_cp("parallel", "parallel"))(a, b)


def _resident(shape, index_map):
    return pl.BlockSpec(shape, index_map, pipeline_mode=pl.Buffered(1))


def _mm_tn(a, b, name, out_dtype=BF):
    na, s, m = a.shape
    nb, _, n = b.shape
    no = max(na, nb)
    tm, tn = _tile(m), _tile(n)

    def body(a_ref, b_ref, o_ref):
        o_ref[...] = _dot_tn(a_ref[...], b_ref[...]).astype(o_ref.dtype)

    def spec(nbatch, width, tile, index_map):
        fixed = nbatch == 1 and width == tile
        return _resident((None, s, tile), index_map) if fixed else pl.BlockSpec((None, s, tile), index_map)

    return pl.pallas_call(
        body, name=name, grid=(no, m // tm, n // tn),
        in_specs=[spec(na, m, tm, lambda j, r, c: (j if na > 1 else 0, 0, r)),
                  spec(nb, n, tn, lambda j, r, c: (j if nb > 1 else 0, 0, c))],
        out_specs=pl.BlockSpec((None, tm, tn), lambda j, r, c: (j, r, c)),
        out_shape=jax.ShapeDtypeStruct((no, m, n), out_dtype),
        compiler_params=_cp("parallel", "parallel", "parallel"))(a, b)


def _mm_nt(dh, w, name, res=None, out_dtype=F32, w_rows_out=True, after=()):
    nc, s, kc = dh.shape
    d = w.shape[1] if w_rows_out else w.shape[2]
    ts = _rows(s)
    has_res = res is not None
    mm = _dot_nt if w_rows_out else _dot

    def body(*refs):
        o_ref = refs[-1]
        dh_ref, w_ref = refs[:2]
        if has_res:
            r_ref = refs[2]
        out = mm(dh_ref[0], w_ref[0])
        for j in range(1, nc):
            out = out + mm(dh_ref[j], w_ref[j])
        if has_res:
            out = out + ALPHA * r_ref[...]
        o_ref[...] = out.astype(o_ref.dtype)

    in_specs = [pl.BlockSpec((nc, ts, kc), lambda i: (0, i, 0)), _resident(w.shape, lambda i: (0, 0, 0))]
    args = [dh, w]
    if has_res:
        in_specs.append(pl.BlockSpec((ts, d), lambda i: (i, 0)))
        args.append(res)
    in_specs += [pl.BlockSpec(memory_space=pl.ANY)] * len(after)
    args += list(after)
    return pl.pallas_call(
        body, name=name, grid=(s // ts,), in_specs=in_specs,
        out_specs=pl.BlockSpec((ts, d), lambda i: (i, 0)),
        out_shape=jax.ShapeDtypeStruct((s, d), out_dtype),
        compiler_params=_cp("parallel"))(*args)


def _mm_res_ln(a, w, x, gain, bias, fscale, name):
    nc, s, kc = a.shape
    d = w.shape[2]
    ts = _rows(s)

    def body(a_ref, w_ref, x_ref, g_ref, b_ref, y_ref, yb_ref, xh_ref, r_ref):
        f = _dot(a_ref[0], w_ref[0])
        for j in range(1, nc):
            f = f + _dot(a_ref[j], w_ref[j])
        z = ALPHA * x_ref[...] + fscale * f
        mu = jnp.mean(z, axis=-1, keepdims=True)
        zc = z - mu
        var = jnp.mean(zc * zc, axis=-1, keepdims=True)
        r = lax.rsqrt(var + LN_EPS)
        xh = zc * r
        y = xh * g_ref[...] + b_ref[...]
        y_ref[...] = y
        yb_ref[...] = y.astype(BF)
        xh_ref[...] = xh
        r_ref[...] = r

    row = pl.BlockSpec((ts, d), lambda i: (i, 0))
    vec = pl.BlockSpec((1, d), lambda i: (0, 0))
    return pl.pallas_call(
        body, name=name, grid=(s // ts,),
        in_specs=[pl.BlockSpec((nc, ts, kc), lambda i: (0, i, 0)), _resident((nc, kc, d), lambda i: (0, 0, 0)),
                  row, vec, vec],
        out_specs=[row, row, row, pl.BlockSpec((ts, 1), lambda i: (i, 0))],
        out_shape=[jax.ShapeDtypeStruct((s, d), F32), jax.ShapeDtypeStruct((s, d), BF),
                   jax.ShapeDtypeStruct((s, d), F32), jax.ShapeDtypeStruct((s, 1), F32)],
        compiler_params=_cp("parallel"))(a, w, x, gain, bias)


def _ln_bwd_proj(dy, xh, rstd, gain, w_out, name, after=()):
    s, d = dy.shape
    wc = w_out.shape[0]
    ts = _rows(s)
    na = len(after)

    def body(*refs):
        dy_ref, xh_ref, r_ref, g_ref, w_ref = refs[:5]
        dz_ref, dzb_ref, dc_ref, dg_ref, db_ref = refs[5 + na:]
        i = pl.program_id(0)
        dyv = dy_ref[...]
        xhv = xh_ref[...]
        dxh = dyv * g_ref[...]
        m1 = jnp.mean(dxh, axis=-1, keepdims=True)
        m2 = jnp.mean(dxh * xhv, axis=-1, keepdims=True)
        dz = r_ref[...] * (dxh - m1 - xhv * m2)
        dzb = dz.astype(BF)
        dz_ref[...] = dz
        dzb_ref[...] = dzb
        dc_ref[...] = _dot_nt(dzb, w_ref[...]).astype(BF)

        @pl.when(i == 0)
        def _():
            dg_ref[...] = jnp.zeros_like(dg_ref)
            db_ref[...] = jnp.zeros_like(db_ref)

        dg_ref[...] += jnp.sum(dyv * xhv, axis=0, keepdims=True)
        db_ref[...] += jnp.sum(dyv, axis=0, keepdims=True)

    row = pl.BlockSpec((ts, d), lambda i: (i, 0))
    vec = pl.BlockSpec((1, d), lambda i: (0, 0))
    return pl.pallas_call(
        body, name=name, grid=(s // ts,),
        in_specs=[row, row, pl.BlockSpec((ts, 1), lambda i: (i, 0)), vec, _resident((wc, d), lambda i: (0, 0))]
                 + [pl.BlockSpec(memory_space=pl.ANY)] * na,
        out_specs=[row, row, pl.BlockSpec((ts, wc), lambda i: (i, 0)), vec, vec],
        out_shape=[jax.ShapeDtypeStruct((s, d), F32), jax.ShapeDtypeStruct((s, d), BF),
                   jax.ShapeDtypeStruct((s, wc), BF),
                   jax.ShapeDtypeStruct((1, d), F32), jax.ShapeDtypeStruct((1, d), F32)],
        compiler_params=_cp("arbitrary"))(dy, xh, rstd, gain, w_out, *after)


def _ffn_up(xb, wgu, name):
    s, d = xb.shape
    c = wgu.shape[1]
    nch = wgu.shape[0] // 2
    ts = _rows(s, 1024)
    w4 = wgu.reshape(2, nch, c, d)

    def body(x_ref, w_ref, gu_ref, a_ref):
        x = x_ref[...]
        g = _dot_nt(x, w_ref[0])
        u = _dot_nt(x, w_ref[1])
        sg = _sigmoid(g)
        t = g * sg
        gu_ref[0] = (u * (sg * (1.0 + g - t))).astype(BF)
        gu_ref[1] = t.astype(BF)
        a_ref[...] = (t * u).astype(BF)

    return pl.pallas_call(
        body, name=name, grid=(nch, s // ts),
        in_specs=[pl.BlockSpec((ts, d), lambda j, i: (i, 0)),
                  pl.BlockSpec((2, None, c, d), lambda j, i: (0, j, 0, 0))],
        out_specs=[pl.BlockSpec((2, None, ts, c), lambda j, i: (0, j, i, 0)),
                   pl.BlockSpec((None, ts, c), lambda j, i: (j, i, 0))],
        out_shape=[jax.ShapeDtypeStruct((2, nch, s, c), BF), jax.ShapeDtypeStruct((nch, s, c), BF)],
        compiler_params=_cp("parallel", "parallel"))(xb, w4)


def _ffn_fwd_main(x, xb, wgu, wd4, gain, bias, name, target=None):
    s, d = x.shape
    nch, c = wd4.shape[0], wd4.shape[1]
    ts = _rows(s, 256)
    head = target is not None

    def body(*refs):
        x_ref, xb_ref, wgu_ref, wd_ref, g_ref, b_ref = refs[:6]
        y_ref, yb_ref, gu_ref, a_ref, xh_ref, r_ref = refs[6 + head:]
        xbv = xb_ref[...]
        f = jnp.zeros((ts, d), F32)
        for j in range(nch):
            g = _dot_nt(xbv, wgu_ref[j])
            u = _dot_nt(xbv, wgu_ref[nch + j])
            sg = _sigmoid(g)
            t = g * sg
            gu_ref[0, j] = (u * (sg * (1.0 + g - t))).astype(BF)
            gu_ref[1, j] = t.astype(BF)
            act = (t * u).astype(BF)
            a_ref[j] = act
            f = f + _dot(act, wd_ref[j])
        z = ALPHA * x_ref[...] + 0.5 * f
        mu = jnp.mean(z, axis=-1, keepdims=True)
        zc = z - mu
        var = jnp.mean(zc * zc, axis=-1, keepdims=True)
        r = lax.rsqrt(var + LN_EPS)
        xh = zc * r
        y = xh * g_ref[...] + b_ref[...]
        xh_ref[...] = xh
        r_ref[...] = r
        if head:
            e = y - refs[6][...]
            y_ref[...] = e * (1.0 / d)

            @pl.when(pl.program_id(0) == 0)
            def _():
                yb_ref[...] = jnp.zeros_like(yb_ref)

            part = jnp.sum(jnp.sum(e * e, axis=1, keepdims=True), axis=0, keepdims=True)
            yb_ref[...] += part * (0.5 / d)
        else:
            y_ref[...] = y
            yb_ref[...] = y.astype(BF)

    row = pl.BlockSpec((ts, d), lambda i: (i, 0))
    vec = pl.BlockSpec((1, d), lambda i: (0, 0))
    second = ((pl.BlockSpec((1, 1), lambda i: (0, 0)), jax.ShapeDtypeStruct((1, 1), F32)) if head
              else (row, jax.ShapeDtypeStruct((s, d), BF)))
    return pl.pallas_call(
        body, name=name, grid=(s // ts,),
        in_specs=[row, row, _resident(wgu.shape, lambda i: (0, 0, 0)), _resident(wd4.shape, lambda i: (0, 0, 0)),
                  vec, vec] + [row] * head,
        out_specs=[row, second[0], pl.BlockSpec((2, nch, ts, c), lambda i: (0, 0, i, 0)),
                   pl.BlockSpec((nch, ts, c), lambda i: (0, i, 0)), row, pl.BlockSpec((ts, 1), lambda i: (i, 0))],
        out_shape=[jax.ShapeDtypeStruct((s, d), F32), second[1],
                   jax.ShapeDtypeStruct((2, nch, s, c), BF), jax.ShapeDtypeStruct((nch, s, c), BF),
                   jax.ShapeDtypeStruct((s, d), F32), jax.ShapeDtypeStruct((s, 1), F32)],
        compiler_params=_cp("arbitrary" if head else "parallel"))(x, xb, wgu, wd4, gain, bias,
                                                                   *([target] if head else []))


def _ffn_fwd_chunks(x, xb, wgu, wd4, gain, bias, name, target=None):
    s, d = x.shape
    nch, c = wd4.shape[0], wd4.shape[1]
    ts = _rows(s, 512)
    head = target is not None
    last = nch - 1

    def body(*refs):
        x_ref, xb_ref, wg_ref, wu_ref, wd_ref, g_ref, b_ref = refs[:7]
        y_ref, yb_ref, gu_ref, a_ref, xh_ref, r_ref, f_ref = refs[7 + head:]
        j, i = pl.program_id(0), pl.program_id(1)
        xbv = xb_ref[...]
        g = _dot_nt(xbv, wg_ref[...])
        u = _dot_nt(xbv, wu_ref[...])
        sg = _sigmoid(g)
        t = g * sg
        gu_ref[0] = (u * (sg * (1.0 + g - t))).astype(BF)
        gu_ref[1] = t.astype(BF)
        act = (t * u).astype(BF)
        a_ref[...] = act
        part = _dot(act, wd_ref[...])
        rows = pl.ds(pl.multiple_of(i * ts, ts), ts)

        @pl.when(j == 0)
        def _():
            f_ref[rows, :] = part

        @pl.when(jnp.logical_and(j > 0, j < last))
        def _():
            f_ref[rows, :] += part

        @pl.when(j == last)
        def _():
            z = ALPHA * x_ref[...] + 0.5 * (f_ref[rows, :] + part)
            mu = jnp.mean(z, axis=-1, keepdims=True)
            zc = z - mu
            var = jnp.mean(zc * zc, axis=-1, keepdims=True)
            r = lax.rsqrt(var + LN_EPS)
            xh = zc * r
            y = xh * g_ref[...] + b_ref[...]
            xh_ref[...] = xh
            r_ref[...] = r
            if head:
                e = y - refs[7][...]
                y_ref[...] = e * (1.0 / d)

                @pl.when(i == 0)
                def _():
                    yb_ref[...] = jnp.zeros_like(yb_ref)

                err = jnp.sum(jnp.sum(e * e, axis=1, keepdims=True), axis=0, keepdims=True)
                yb_ref[...] += err * (0.5 / d)
            else:
                y_ref[...] = y
                yb_ref[...] = y.astype(BF)

    def late(width):
        return pl.BlockSpec((ts, width), lambda j, i: (jnp.where(j == last, i, 0), 0))

    vec = pl.BlockSpec((1, d), lambda j, i: (0, 0))
    wblk = pl.BlockSpec((None, c, d), lambda j, i: (j, 0, 0))
    second = ((pl.BlockSpec((1, 1), lambda j, i: (0, 0)), jax.ShapeDtypeStruct((1, 1), F32)) if head
              else (late(d), jax.ShapeDtypeStruct((s, d), BF)))
    return pl.pallas_call(
        body, name=name, grid=(nch, s // ts),
        in_specs=[late(d), pl.BlockSpec((ts, d), lambda j, i: (i, 0)), wblk,
                  pl.BlockSpec((None, c, d), lambda j, i: (nch + j, 0, 0)), wblk, vec, vec] + [late(d)] * head,
        out_specs=[late(d), second[0], pl.BlockSpec((2, None, ts, c), lambda j, i: (0, j, i, 0)),
                   pl.BlockSpec((None, ts, c), lambda j, i: (j, i, 0)), late(d), late(1)],
        out_shape=[jax.ShapeDtypeStruct((s, d), F32), second[1],
                   jax.ShapeDtypeStruct((2, nch, s, c), BF), jax.ShapeDtypeStruct((nch, s, c), BF),
                   jax.ShapeDtypeStruct((s, d), F32), jax.ShapeDtypeStruct((s, 1), F32)],
        scratch_shapes=[pltpu.VMEM((s, d), F32)],
        compiler_params=_cp("arbitrary", "arbitrary"))(x, xb, wgu, wgu, wd4, gain, bias,
                                                       *([target] if head else []))


def _ffn_bwd_main(dy, xh, rstd, gain, wd4, wgu, gu, name, after=(), with_dx=True):
    s, d = dy.shape
    nch, c = wd4.shape[0], wd4.shape[1]
    ts = _rows(s, 256)
    na = len(after)

    def body(*refs):
        dy_ref, xh_ref, r_ref, g_ref, wd_ref, wgu_ref, gu_ref = refs[:7]
        dx_ref, dzb_ref, dh_ref, dg_ref, db_ref = refs[7 + na:]
        i = pl.program_id(0)
        dyv = dy_ref[...]
        xhv = xh_ref[...]
        dxh = dyv * g_ref[...]
        m1 = jnp.mean(dxh, axis=-1, keepdims=True)
        m2 = jnp.mean(dxh * xhv, axis=-1, keepdims=True)
        dz = r_ref[...] * (dxh - m1 - xhv * m2)
        dzb = (0.5 * dz).astype(BF)
        dzb_ref[...] = dzb

        @pl.when(i == 0)
        def _():
            dg_ref[...] = jnp.zeros_like(dg_ref)
            db_ref[...] = jnp.zeros_like(db_ref)

        dg_ref[...] += jnp.sum(dyv * xhv, axis=0, keepdims=True)
        db_ref[...] += jnp.sum(dyv, axis=0, keepdims=True)

        dx = ALPHA * dz if with_dx else dz
        for j in range(nch):
            da = _dot_nt(dzb, wd_ref[j])
            dgate = (da * gu_ref[0, j].astype(F32)).astype(BF)
            dup = (da * gu_ref[1, j].astype(F32)).astype(BF)
            dh_ref[0, j] = dgate
            dh_ref[1, j] = dup
            if with_dx:
                dx = dx + _dot(dgate, wgu_ref[j]) + _dot(dup, wgu_ref[nch + j])
        dx_ref[...] = dx

    row = pl.BlockSpec((ts, d), lambda i: (i, 0))
    vec = pl.BlockSpec((1, d), lambda i: (0, 0))
    act = pl.BlockSpec((2, nch, ts, c), lambda i: (0, 0, i, 0))
    return pl.pallas_call(
        body, name=name, grid=(s // ts,),
        in_specs=[row, row, pl.BlockSpec((ts, 1), lambda i: (i, 0)), vec,
                  _resident(wd4.shape, lambda i: (0, 0, 0)), _resident(wgu.shape, lambda i: (0, 0, 0)), act]
                 + [pl.BlockSpec(memory_space=pl.ANY)] * na,
        out_specs=[row, row, act, vec, vec],
        out_shape=[jax.ShapeDtypeStruct((s, d), F32), jax.ShapeDtypeStruct((s, d), BF),
                   jax.ShapeDtypeStruct((2, nch, s, c), BF),
                   jax.ShapeDtypeStruct((1, d), F32), jax.ShapeDtypeStruct((1, d), F32)],
        compiler_params=_cp("arbitrary"))(dy, xh, rstd, gain, wd4, wgu, gu, *after)


def _rope_tables(s, sign):
    pos = jnp.arange(s, dtype=F32)
    inv_freq = 1.0 / (ROPE_THETA ** (jnp.arange(ROT_HALF, dtype=F32) / ROT_HALF))
    ang = pos[:, None] * inv_freq[None, :]
    cos, sin = jnp.cos(ang), jnp.sin(ang) * sign
    one = jnp.ones((s, HEAD_DIM - 2 * ROT_HALF), F32)
    zero = jnp.zeros((s, HEAD_DIM - 2 * ROT_HALF), F32)
    zh = jnp.zeros((s, ROT_HALF), F32)
    cos_f = jnp.concatenate([cos, cos, one], axis=1)
    sin_a = jnp.concatenate([-sin, zh, zero], axis=1)
    sin_b = jnp.concatenate([zh, sin, zero], axis=1)
    rep = LANES // HEAD_DIM
    return tuple(jnp.tile(t, (1, rep)) for t in (cos_f, sin_a, sin_b))


def _rope(t, c_ref, sa_ref, sb_ref):
    return (t * c_ref[...] + pltpu.roll(t, LANES - ROT_HALF, 1) * sa_ref[...]
            + pltpu.roll(t, ROT_HALF, 1) * sb_ref[...])


def _proj_rope(xb, w, tabs, n_rope, name, w_rows_out, tail_block=None):
    s, d = xb.shape
    n = w.shape[0] if w_rows_out else w.shape[1]
    tm = _rows(s, 256)
    has_tail = tail_block is not None

    def body(x_ref, w_ref, c_ref, sa_ref, sb_ref, o_ref, *tail_ref):
        h = (_dot_nt if w_rows_out else _dot)(x_ref[...], w_ref[...])
        for cb in range(n // LANES):
            t = h[:, cb * LANES:(cb + 1) * LANES]
            if cb < n_rope:
                t = _rope(t, c_ref, sa_ref, sb_ref)
            o_ref[:, cb * LANES:(cb + 1) * LANES] = t.astype(BF)
        if has_tail:
            tail_ref[0][...] = h[:, tail_block * LANES:(tail_block + 1) * LANES]

    tab = pl.BlockSpec((tm, LANES), lambda i: (i, 0))
    out_specs = [pl.BlockSpec((tm, n), lambda i: (i, 0))]
    out_shape = [jax.ShapeDtypeStruct((s, n), BF)]
    if has_tail:
        out_specs.append(tab)
        out_shape.append(jax.ShapeDtypeStruct((s, LANES), F32))
    res = pl.pallas_call(
        body, name=name, grid=(s // tm,),
        in_specs=[pl.BlockSpec((tm, d), lambda i: (i, 0)), _resident(w.shape, lambda i: (0, 0)), tab, tab, tab],
        out_specs=out_specs, out_shape=out_shape, compiler_params=_cp("parallel"))(xb, w, *tabs)
    return res if has_tail else res[0]


def _rope_cast(parts, tabs, n_rope, name, transposed=()):
    s = tabs[0].shape[0]
    flip = [i in transposed for i in range(len(parts))]
    widths = [p.shape[0] if f else p.shape[1] for p, f in zip(parts, flip)]
    n = sum(widths)
    npart = len(parts)
    ts = _rows(s, 256)

    def body(*refs):
        part_refs = refs[:npart]
        c_ref, sa_ref, sb_ref, o_ref = refs[npart:]
        col = 0
        for ref, w, f in zip(part_refs, widths, flip):
            for j in range(w // LANES):
                if f:
                    t = jnp.transpose(ref[j * LANES:(j + 1) * LANES, :])
                else:
                    t = ref[:, j * LANES:(j + 1) * LANES]
                if col < n_rope:
                    t = _rope(t, c_ref, sa_ref, sb_ref)
                o_ref[:, col * LANES:(col + 1) * LANES] = t.astype(BF)
                col += 1

    tab = pl.BlockSpec((ts, LANES), lambda i: (i, 0))
    return pl.pallas_call(
        body, name=name, grid=(s // ts,),
        in_specs=[pl.BlockSpec((w, ts), lambda i: (0, i)) if f else pl.BlockSpec((ts, w), lambda i: (i, 0))
                  for w, f in zip(widths, flip)] + [tab, tab, tab],
        out_specs=pl.BlockSpec((ts, n), lambda i: (i, 0)),
        out_shape=jax.ShapeDtypeStruct((s, n), BF),
        compiler_params=_cp("parallel"))(*parts, *tabs)


def _head_masks():
    lane = lax.broadcasted_iota(jnp.int32, (1, LANES), 1)
    return [lane < HEAD_DIM, lane >= HEAD_DIM]


def _sel(mask, v):
    return jnp.where(mask, v, jnp.zeros_like(v))


def _pick(mask, wide, fill):
    return jnp.max(jnp.where(mask, wide, fill), axis=1, keepdims=True)


def _head_stack(hm, a, b):
    return jnp.concatenate([_sel(hm[0], a), _sel(hm[0], b), _sel(hm[1], a), _sel(hm[1], b)], axis=0)


def _band_mask_stack(has_prev):
    qi = lax.broadcasted_iota(jnp.int32, (BLOCK, 4 * BLOCK), 0)
    col = lax.broadcasted_iota(jnp.int32, (BLOCK, 4 * BLOCK), 1)
    d = jnp.bitwise_and(col, BLOCK - 1) - qi
    is_prev = jnp.bitwise_and(col, BLOCK) != 0
    return jnp.where(is_prev, d - jnp.where(has_prev, 0, BLOCK), -d) >= 0


class _BandView:
    def __init__(self, s, g):
        self.r = 4 ** g
        self.nl = s // self.r
        self.nblk = self.nl // BLOCK
        self.nsub = min(BAND_SUB, self.nblk)
        self.tile = self.nsub * BLOCK
        self.ncols = min(self.r * GROUP_W // LANES, BAND_COLS)
        self.grid = (self.r * GROUP_W // LANES // self.ncols, self.nblk // self.nsub)

    def view(self, a):
        return a.reshape(self.nl, self.r * a.shape[1])

    def qkv(self, hb, g):
        npair = MIX_W // LANES
        offs = [i * npair + g * GROUP_W // LANES for i in range(3)]
        if self.r == 1:
            return [hb] * 3, hb.shape[1], offs
        return [self.view(hb[:, o * LANES:o * LANES + GROUP_W]) for o in offs], GROUP_W, [0, 0, 0]

    def specs(self, width, off):
        assert off % self.ncols == 0 and (width == GROUP_W or self.r == 1)
        nsub, last, lanes, first = self.nsub, self.nblk - 1, self.ncols * LANES, off // self.ncols
        return (pl.BlockSpec((self.tile, lanes), lambda cg, t: (t, first + cg)),
                pl.BlockSpec((BLOCK, lanes), lambda cg, t: (jnp.maximum(t * nsub - 1, 0), first + cg)),
                pl.BlockSpec((BLOCK, lanes), lambda cg, t: (jnp.minimum(t * nsub + nsub, last), first + cg)))


def _band_fwd(hb, g, name):
    s, n = hb.shape
    bv = _BandView(s, g)
    nsub = bv.nsub
    npair = MIX_W // LANES

    def body(q_ref, kc_ref, kp_ref, vc_ref, vp_ref, o_ref, l_ref):
        t = pl.program_id(1)
        hm = _head_masks()
        for i, c in [(i, c) for i in range(nsub) for c in range(bv.ncols)]:
            rows = slice(i * BLOCK, (i + 1) * BLOCK)
            lanes = slice(c * LANES, (c + 1) * LANES)
            has_prev = t > 0 if i == 0 else True
            q, kc, vc = q_ref[rows, lanes], kc_ref[rows, lanes], vc_ref[rows, lanes]
            if i == 0:
                kp, vp = kp_ref[:, lanes], vp_ref[:, lanes]
            else:
                prev = slice((i - 1) * BLOCK, i * BLOCK)
                kp, vp = kc_ref[prev, lanes], vc_ref[prev, lanes]
            sc = jnp.where(_band_mask_stack(has_prev), _dot_nt(q, _head_stack(hm, kc, kp)) * SCALE, NEG)
            ps, ms, ls = [], [], []
            for h in range(2):
                sh = sc[:, 2 * h * BLOCK:2 * (h + 1) * BLOCK]
                m = jnp.max(sh, axis=1, keepdims=True)
                p = jnp.exp(sh - m)
                ps.append(p.astype(BF))
                ms.append(m)
                ls.append(jnp.sum(p, axis=1, keepdims=True))
            o = _dot(jnp.concatenate(ps, axis=1), _head_stack(hm, vc, vp))
            o_ref[rows, lanes] = o / jnp.where(hm[0], ls[0], ls[1])
            l_ref[rows, lanes] = jnp.where(hm[0], ms[0] + jnp.log(ls[0]), ms[1] + jnp.log(ls[1]))

    (qv, kv_, vv), width, (qo, ko, vo) = bv.qkv(hb, g)
    q_cur, _, _ = bv.specs(width, qo)
    k_cur, k_prv, _ = bv.specs(width, ko)
    v_cur, v_prv, _ = bv.specs(width, vo)
    out_spec = bv.specs(GROUP_W, 0)[0]
    out = jax.ShapeDtypeStruct((bv.nl, bv.r * GROUP_W), F32)
    o, l = pl.pallas_call(
        body, name=name, grid=bv.grid,
        in_specs=[q_cur, k_cur, k_prv, v_cur, v_prv], out_specs=[out_spec, out_spec], out_shape=[out, out],
        compiler_params=_cp("parallel", "parallel"))(qv, kv_, kv_, vv, vv)
    return o.reshape(s, GROUP_W), l.reshape(s, GROUP_W)


def _band_combine(os, ls, name):
    ng = len(os)
    s, w = os[0].shape
    ts = _rows(s)

    def body(*refs):
        o_refs, l_refs = refs[:ng], refs[ng:2 * ng]
        oa_ref, lt_ref = refs[2 * ng:]
        lv = [r[...] for r in l_refs]
        m = functools.reduce(jnp.maximum, lv)
        es = [jnp.exp(l - m) for l in lv]
        den = functools.reduce(lambda a, b: a + b, es)
        num = functools.reduce(lambda a, b: a + b, [es[g] * o_refs[g][...] for g in range(ng)])
        oa_ref[...] = (num / den).astype(BF)
        lt_ref[...] = m + jnp.log(den)

    blk = pl.BlockSpec((ts, w), lambda i: (i, 0))
    return pl.pallas_call(
        body, name=name, grid=(s // ts,), in_specs=[blk] * (2 * ng), out_specs=[blk, blk],
        out_shape=[jax.ShapeDtypeStruct((s, w), BF), jax.ShapeDtypeStruct((s, w), F32)],
        compiler_params=_cp("parallel"))(*os, *ls)


def _band_bwd(hb, dcat, oa, lt, g, name):
    s, n = hb.shape
    bv = _BandView(s, g)
    nsub = bv.nsub
    npair = MIX_W // LANES
    ntile = bv.grid[1]

    def body(q_ref, qn_ref, kc_ref, kp_ref, vc_ref, vp_ref, do_ref, don_ref, oa_ref, oan_ref, lt_ref, ltn_ref,
             dq_ref, dk_ref, dv_ref):
        t = pl.program_id(1)
        hm = _head_masks()

        for i, c in [(i, c) for i in range(nsub) for c in range(bv.ncols)]:
            lanes = slice(c * LANES, (c + 1) * LANES)

            def block(ref, edge_ref, i, lanes=lanes):
                if i < 0 or i >= nsub:
                    return edge_ref[:, lanes]
                return ref[i * BLOCK:(i + 1) * BLOCK, lanes]

            has_prev = t > 0 if i == 0 else True
            has_next = t < ntile - 1 if i == nsub - 1 else True
            q, qn = block(q_ref, None, i), block(q_ref, qn_ref, i + 1)
            kc, kp = block(kc_ref, None, i), block(kc_ref, kp_ref, i - 1)
            vc, vp = block(vc_ref, None, i), block(vc_ref, vp_ref, i - 1)
            do, don = block(do_ref, None, i), block(do_ref, don_ref, i + 1)
            dd = do.astype(F32) * block(oa_ref, None, i).astype(F32)
            ddn = don.astype(F32) * block(oa_ref, oan_ref, i + 1).astype(F32)
            lt, ltn = block(lt_ref, None, i), block(lt_ref, ltn_ref, i + 1)

            def per_head(wide, width):
                col = lax.broadcasted_iota(jnp.int32, (BLOCK, 2 * width), 1)
                return jnp.where(col < width, _pick(hm[0], wide, NEG), _pick(hm[1], wide, NEG))

            def row_sums(prod, width):
                col = lax.broadcasted_iota(jnp.int32, (BLOCK, 2 * width), 1)
                return jnp.where(col < width, jnp.sum(_sel(hm[0], prod), axis=1, keepdims=True),
                                 jnp.sum(_sel(hm[1], prod), axis=1, keepdims=True))

            kst, vst = _head_stack(hm, kc, kp), _head_stack(hm, vc, vp)
            p = jnp.exp(jnp.where(_band_mask_stack(has_prev), _dot_nt(q, kst) * SCALE, NEG)
                        - per_head(lt, 2 * BLOCK))
            ds = p * (_dot_nt(do, vst) - row_sums(dd, 2 * BLOCK))
            dq_ref[i * BLOCK:(i + 1) * BLOCK, lanes] = SCALE * _dot(ds.astype(BF), kst)
            kcs = jnp.concatenate([_sel(hm[0], kc), _sel(hm[1], kc)], axis=0)
            vcs = jnp.concatenate([_sel(hm[0], vc), _sel(hm[1], vc)], axis=0)
            qi_ = lax.broadcasted_iota(jnp.int32, (BLOCK, 2 * BLOCK), 0)
            kj_ = jnp.bitwise_and(lax.broadcasted_iota(jnp.int32, (BLOCK, 2 * BLOCK), 1), BLOCK - 1)
            mn = kj_ >= qi_ + jnp.where(has_next, 0, BLOCK)
            pn = jnp.exp(jnp.where(mn, _dot_nt(qn, kcs) * SCALE, NEG) - per_head(ltn, BLOCK))
            dsn = pn * (_dot_nt(don, vcs) - row_sums(ddn, BLOCK))
            pb, dsb, pnb, dsnb = p.astype(BF), ds.astype(BF), pn.astype(BF), dsn.astype(BF)

            def own(x, h):
                return x[:, 2 * h * BLOCK:(2 * h + 1) * BLOCK]

            def nxt(x, h):
                return x[:, h * BLOCK:(h + 1) * BLOCK]

            ds_rows = jnp.concatenate([own(dsb, 0), nxt(dsnb, 0), own(dsb, 1), nxt(dsnb, 1)], axis=0)
            p_rows = jnp.concatenate([own(pb, 0), nxt(pnb, 0), own(pb, 1), nxt(pnb, 1)], axis=0)
            dk_ref[i * BLOCK:(i + 1) * BLOCK, lanes] = SCALE * _dot_tn(ds_rows, _head_stack(hm, q, qn))
            dv_ref[i * BLOCK:(i + 1) * BLOCK, lanes] = _dot_tn(p_rows, _head_stack(hm, do, don))

    (qv, kv_, vv), width, (qo, ko, vo) = bv.qkv(hb, g)
    q_cur, _, q_nxt = bv.specs(width, qo)
    k_cur, k_prv, _ = bv.specs(width, ko)
    v_cur, v_prv, _ = bv.specs(width, vo)
    w_cur, _, w_nxt = bv.specs(GROUP_W, 0)
    out = jax.ShapeDtypeStruct((bv.nl, bv.r * GROUP_W), F32)
    dv_, ov, lv = bv.view(dcat[:, :GROUP_W]), bv.view(oa), bv.view(lt)
    res = pl.pallas_call(
        body, name=name, grid=bv.grid,
        in_specs=[q_cur, q_nxt, k_cur, k_prv, v_cur, v_prv, w_cur, w_nxt, w_cur, w_nxt, w_cur, w_nxt],
        out_specs=[w_cur, w_cur, w_cur], out_shape=[out, out, out],
        compiler_params=_cp("parallel", "parallel"))(
            qv, qv, kv_, kv_, vv, vv, dv_, dv_, ov, ov, lv, lv)
    return [t.reshape(s, GROUP_W) for t in res]


def _mem_fwd(hb, q_blk0, kv, name):
    s = hb.shape[0]
    m = kv.shape[0]
    tq = _rows(s)
    npair = MEM_W // LANES

    def body(q_ref, k_ref, v_ref, o_ref, l_ref):
        q, k, v = q_ref[...], k_ref[...], v_ref[...]
        hm = _head_masks()
        o = jnp.zeros((tq, LANES), F32)
        lse_w = jnp.zeros((tq, LANES), F32)
        for h in range(2):
            sc = _dot_nt(_sel(hm[h], q), k) * SCALE
            mx = jnp.max(sc, axis=1, keepdims=True)
            p = jnp.exp(sc - mx)
            l = jnp.sum(p, axis=1, keepdims=True)
            o = o + _dot(p.astype(BF), _sel(hm[h], v)) / l
            lse_w = jnp.where(hm[h], mx + jnp.log(l), lse_w)
        o_ref[...] = o.astype(BF)
        l_ref[...] = lse_w

    blk = pl.BlockSpec((tq, LANES), lambda p, i: (i, p))
    return pl.pallas_call(
        body, name=name, grid=(npair, s // tq),
        in_specs=[pl.BlockSpec((tq, LANES), lambda p, i: (i, q_blk0 + p)),
                  pl.BlockSpec((m, LANES), lambda p, i: (0, p)),
                  pl.BlockSpec((m, LANES), lambda p, i: (0, npair + p))],
        out_specs=[blk, blk],
        out_shape=[jax.ShapeDtypeStruct((s, MEM_W), BF), jax.ShapeDtypeStruct((s, MEM_W), F32)],
        compiler_params=_cp("parallel", "parallel"))(hb, kv, kv)


def _mem_bwd(hb, q_blk0, kv, dcat, cat, o_blk0, lse, name):
    s = hb.shape[0]
    m = kv.shape[0]
    tq = _rows(s)
    npair = MEM_W // LANES

    def body(q_ref, k_ref, v_ref, do_ref, o_ref, l_ref, dq_ref, dk_ref, dv_ref):
        i = pl.program_id(1)

        @pl.when(i == 0)
        def _():
            dk_ref[...] = jnp.zeros_like(dk_ref)
            dv_ref[...] = jnp.zeros_like(dv_ref)

        q, k, v, do = q_ref[...], k_ref[...], v_ref[...], do_ref[...]
        dd = do.astype(F32) * o_ref[...].astype(F32)
        lt = l_ref[...]
        hm = _head_masks()
        dq = jnp.zeros((tq, LANES), F32)
        dk = jnp.zeros((m, LANES), F32)
        dv = jnp.zeros((m, LANES), F32)
        for h in range(2):
            qh, doh = _sel(hm[h], q), _sel(hm[h], do)
            p = jnp.exp(_dot_nt(qh, k) * SCALE - _pick(hm[h], lt, NEG))
            ds = p * (_dot_nt(doh, v) - jnp.sum(_sel(hm[h], dd), axis=1, keepdims=True))
            dq = dq + SCALE * _dot(ds.astype(BF), _sel(hm[h], k))
            dk = dk + SCALE * _dot_tn(ds.astype(BF), qh)
            dv = dv + _dot_tn(p.astype(BF), doh)
        dq_ref[...] = dq
        dk_ref[...] += dk
        dv_ref[...] += dv

    row = pl.BlockSpec((tq, LANES), lambda p, i: (i, p))
    orow = pl.BlockSpec((tq, LANES), lambda p, i: (i, o_blk0 + p))
    acc = pl.BlockSpec((m, LANES), lambda p, i: (0, p))
    return pl.pallas_call(
        body, name=name, grid=(npair, s // tq),
        in_specs=[pl.BlockSpec((tq, LANES), lambda p, i: (i, q_blk0 + p)),
                  pl.BlockSpec((m, LANES), lambda p, i: (0, p)),
                  pl.BlockSpec((m, LANES), lambda p, i: (0, npair + p)), orow, orow, row],
        out_specs=[row, acc, acc],
        out_shape=[jax.ShapeDtypeStruct((s, MEM_W), F32), jax.ShapeDtypeStruct((m, MEM_W), F32),
                   jax.ShapeDtypeStruct((m, MEM_W), F32)],
        compiler_params=_cp("parallel", "arbitrary"))(hb, kv, kv, dcat, cat, lse)


def _gate_fwd(f_t, bias, name):
    hp, s = f_t.shape
    nblk = s // LANES
    group = 8 if nblk % 8 == 0 else 1

    def body(f_ref, b_ref, c_ref):
        lane = lax.broadcasted_iota(jnp.int32, (hp, LANES), 1)

        def step(i, carry):
            scans = []
            for u in range(group):
                off = pl.multiple_of((i * group + u) * LANES, LANES)
                x = f_ref[:, pl.ds(off, LANES)] + b_ref[...]
                acc = jnp.minimum(x, 0.0) - jnp.log(1.0 + jnp.exp(-jnp.abs(x)))
                sh = 1
                while sh < LANES:
                    acc = acc + jnp.where(lane >= sh, pltpu.roll(acc, sh, 1), 0.0)
                    sh *= 2
                scans.append((off, acc))
            for off, acc in scans:
                acc = acc + carry
                c_ref[:, pl.ds(off, LANES)] = acc
                carry = acc[:, LANES - 1:LANES]
            return carry

        lax.fori_loop(0, nblk // group, step, jnp.zeros((hp, 1), F32))

    vm = pl.BlockSpec(memory_space=pltpu.VMEM)
    return pl.pallas_call(body, name=name, in_specs=[vm, vm], out_specs=vm,
                          out_shape=jax.ShapeDtypeStruct((hp, s), F32),
                          compiler_params=pltpu.CompilerParams(vmem_limit_bytes=VMEM_LIMIT))(f_t, bias)


def _gate_bwd(dc_t, f_t, bias, name):
    hp, s = f_t.shape
    nblk = s // LANES
    group = 8 if nblk % 8 == 0 else 1

    def body(dc_ref, f_ref, b_ref, df_ref, db_ref):
        lane = lax.broadcasted_iota(jnp.int32, (hp, LANES), 1)

        def step(t, carry):
            suffix, dbias = carry
            scans = []
            for u in range(group):
                off = pl.multiple_of((nblk - 1 - (t * group + u)) * LANES, LANES)
                acc = dc_ref[:, pl.ds(off, LANES)]
                sh = 1
                while sh < LANES:
                    acc = acc + jnp.where(lane < LANES - sh, pltpu.roll(acc, LANES - sh, 1), 0.0)
                    sh *= 2
                x = f_ref[:, pl.ds(off, LANES)] + b_ref[...]
                scans.append((off, acc, _sigmoid(-x)))
            for off, acc, sg in scans:
                acc = acc + suffix
                df = acc * sg
                df_ref[:, pl.ds(off, LANES)] = df
                suffix = acc[:, 0:1]
                dbias = dbias + jnp.sum(df, axis=1, keepdims=True)
            return suffix, dbias

        _, dbias = lax.fori_loop(0, nblk // group, step, (jnp.zeros((hp, 1), F32), jnp.zeros((hp, 1), F32)))
        db_ref[...] = dbias

    vm = pl.BlockSpec(memory_space=pltpu.VMEM)
    return pl.pallas_call(body, name=name, in_specs=[vm, vm, vm], out_specs=[vm, vm],
                          out_shape=[jax.ShapeDtypeStruct((hp, s), F32), jax.ShapeDtypeStruct((hp, 1), F32)],
                          compiler_params=pltpu.CompilerParams(vmem_limit_bytes=VMEM_LIMIT))(dc_t, f_t, bias)


def _wide(rep, width):
    return jnp.tile(rep, (1, width // LANES))


def _fold(t):
    part = t[:, :LANES]
    for c in range(1, t.shape[1] // LANES):
        part = part + t[:, c * LANES:(c + 1) * LANES]
    return part


def _foxt_logits(q, k, cq_row, ck_rep, mask, hmask):
    s = _dot_nt(_sel(hmask, k), q) + (cq_row - _wide(ck_rep, q.shape[0]))
    if mask is not None:
        s = jnp.where(mask, s, NEG)
    return s


def _causal_sub(ks, qs):
    shape = (ks.stop - ks.start, qs.stop - qs.start)
    return (ks.start + lax.broadcasted_iota(jnp.int32, shape, 0)
            <= qs.start + lax.broadcasted_iota(jnp.int32, shape, 1))


def _diag_blocks(t):
    h = t // 2
    return [(slice(0, h), slice(0, t)), (slice(h, t), slice(h, t))]


FOX_SPLIT = 1


def _fox_tiles(s):
    tq = _rows(s, 1024)
    return tq, tq // FOX_SPLIT, s // tq


def _fox_steps(nq):
    return FOX_SPLIT * nq * (nq + 1) // 2


def _count_ge(t, bounds):
    return sum([(t >= b).astype(jnp.int32) for b in bounds], jnp.int32(0))


def _sweep_q_major(t, nq):
    qi = _count_ge(t, [FOX_SPLIT * r * (r + 1) // 2 for r in range(1, nq)])
    return qi, t - FOX_SPLIT * qi * (qi + 1) // 2


def _sweep_k_major(t, nq):
    counts = [nq - j // FOX_SPLIT for j in range(FOX_SPLIT * nq)]
    offs = [sum(counts[:j]) for j in range(1, FOX_SPLIT * nq)]
    kj = _count_ge(t, offs)
    start = sum([jnp.where(t >= o, c, 0) for o, c in zip(offs, counts)], jnp.int32(0))
    qi = kj // FOX_SPLIT + (t - start)
    return kj, qi, t == start, qi == nq - 1


def _foxt_fwd(hb, c_rep, c_t3, name):
    s = hb.shape[0]
    npair = MIX_W // LANES
    tq, tk, nq = _fox_tiles(s)

    def body(q_ref, k_ref, v_ref, cq_ref, ck_ref, o_ref, l_ref, m_s, l_s, acc):
        qi, kj = _sweep_q_major(pl.program_id(1), nq)
        hm = _head_masks()

        @pl.when(kj == 0)
        def _():
            m_s[...] = jnp.full_like(m_s, NEG)
            l_s[...] = jnp.zeros_like(l_s)
            acc[...] = jnp.zeros_like(acc)

        def step(ks, qs, masked):
            q, k = q_ref[qs, :] * SCALE, k_ref[ks, :]
            vt = jnp.transpose(v_ref[ks, :])
            cq = cq_ref[:, qs]
            mask = _causal_sub(ks, qs) if masked else None
            for h in range(2):
                st = _foxt_logits(q, k, cq[h:h + 1, :], ck_ref[h, ks, :], mask, hm[h])
                m_old = m_s[h, :, qs]
                m_new = jnp.maximum(m_old, jnp.max(st, axis=0, keepdims=True))
                pt = jnp.exp(st - m_new)
                corr = jnp.exp(m_old - m_new)
                l_s[h, :, qs] = l_s[h, :, qs] * corr + jnp.sum(pt, axis=0, keepdims=True)
                acc[h, :, qs] = acc[h, :, qs] * corr + _dot(vt[h * HEAD_DIM:(h + 1) * HEAD_DIM, :], pt.astype(BF))
                m_s[h, :, qs] = m_new

        @pl.when(kj < qi)
        def _():
            step(slice(0, tk), slice(0, tq), False)

        @pl.when(kj == qi)
        def _():
            for ks, qs in _diag_blocks(tq):
                step(ks, qs, True)
            outs = []
            for h in range(2):
                outs.append(acc[h] / l_s[h])
                l_ref[h:h + 1, :] = m_s[h] + jnp.log(l_s[h])
            o_ref[...] = jnp.transpose(jnp.concatenate(outs, axis=0)).astype(BF)

    def q_map(p, t):
        return (_sweep_q_major(t, nq)[0], p)

    def kv_map(off):
        return lambda p, t: (_sweep_q_major(t, nq)[1], off + p)

    blk = pl.BlockSpec((tq, LANES), q_map)
    row = pl.BlockSpec((None, 2, tq), lambda p, t: (p, 0, _sweep_q_major(t, nq)[0]))
    return pl.pallas_call(
        body, name=name, grid=(npair, _fox_steps(nq)),
        in_specs=[blk, pl.BlockSpec((tk, LANES), kv_map(npair)), pl.BlockSpec((tk, LANES), kv_map(2 * npair)), row,
                  pl.BlockSpec((2, tk, LANES), lambda p, t: (p, _sweep_q_major(t, nq)[1], 0))],
        out_specs=[blk, row],
        out_shape=[jax.ShapeDtypeStruct((s, MIX_W), BF), jax.ShapeDtypeStruct((npair, 2, s), F32)],
        scratch_shapes=[pltpu.VMEM((2, 1, tq), F32), pltpu.VMEM((2, 1, tq), F32),
                        pltpu.VMEM((2, HEAD_DIM, tq), F32)],
        compiler_params=_cp("parallel", "arbitrary"))(hb, hb, hb, c_t3, c_rep)


def _foxt_dsum(hb, dcat, lse, c_rep, c_t3, name):
    s = hb.shape[0]
    npair = MIX_W // LANES
    tq, tk, nq = _fox_tiles(s)

    def body(q_ref, k_ref, v_ref, do_ref, l_ref, cq_ref, ck_ref, d_ref, acc):
        qi, kj = _sweep_q_major(pl.program_id(1), nq)
        hm = _head_masks()

        @pl.when(kj == 0)
        def _():
            acc[...] = jnp.zeros_like(acc)

        def step(ks, qs, masked):
            q, k, v, do = q_ref[qs, :] * SCALE, k_ref[ks, :], v_ref[ks, :], do_ref[qs, :]
            cq, lse_rows = cq_ref[:, qs], l_ref[:, qs]
            mask = _causal_sub(ks, qs) if masked else None
            for h in range(2):
                pt = jnp.exp(_foxt_logits(q, k, cq[h:h + 1, :], ck_ref[h, ks, :], mask, hm[h])
                             - lse_rows[h:h + 1, :])
                acc[h, :, qs] += jnp.sum(pt * _dot_nt(_sel(hm[h], v), do), axis=0, keepdims=True)

        @pl.when(kj < qi)
        def _():
            step(slice(0, tk), slice(0, tq), False)

        @pl.when(kj == qi)
        def _():
            for ks, qs in _diag_blocks(tq):
                step(ks, qs, True)
            for h in range(2):
                d_ref[h:h + 1, :] = acc[h]

    def q_map(p, t):
        return (_sweep_q_major(t, nq)[0], p)

    def kv_map(off):
        return lambda p, t: (_sweep_q_major(t, nq)[1], off + p)

    blk = pl.BlockSpec((tq, LANES), q_map)
    row = pl.BlockSpec((None, 2, tq), lambda p, t: (p, 0, _sweep_q_major(t, nq)[0]))
    return pl.pallas_call(
        body, name=name, grid=(npair, _fox_steps(nq)),
        in_specs=[blk, pl.BlockSpec((tk, LANES), kv_map(npair)), pl.BlockSpec((tk, LANES), kv_map(2 * npair)),
                  blk, row, row, pl.BlockSpec((2, tk, LANES), lambda p, t: (p, _sweep_q_major(t, nq)[1], 0))],
        out_specs=row, out_shape=jax.ShapeDtypeStruct((npair, 2, s), F32),
        scratch_shapes=[pltpu.VMEM((2, 1, tq), F32)],
        compiler_params=_cp("parallel", "arbitrary"))(hb, hb, hb, dcat, lse, c_t3, c_rep)


def _foxt_bwd(hb, dcat, dsum, lse, c_rep, c_t3, name):
    s = hb.shape[0]
    npair = MIX_W // LANES
    tq, tk, nq = _fox_tiles(s)

    def body(q_ref, k_ref, v_ref, do_ref, d_ref, l_ref, cq_ref, ck_ref, dq_ref, dk_ref, dv_ref, dc_ref, dc_s):
        t = pl.program_id(1)
        kj, qi, first, last = _sweep_k_major(t, nq)
        hm = _head_masks()

        @pl.when(first)
        def _():
            dk_ref[...] = jnp.zeros_like(dk_ref)
            dv_ref[...] = jnp.zeros_like(dv_ref)
            dc_s[...] = jnp.zeros_like(dc_s)

        @pl.when(t == 0)
        def _():
            dq_ref[...] = jnp.zeros_like(dq_ref)

        def step(ks, qs, masked):
            q, k, v, do = q_ref[qs, :] * SCALE, k_ref[ks, :], v_ref[ks, :], do_ref[qs, :]
            qt, kt, dot = jnp.transpose(q), jnp.transpose(k), jnp.transpose(do)
            cq, lse_rows, d_rows = cq_ref[:, qs], l_ref[:, qs], d_ref[:, qs]
            mask = _causal_sub(ks, qs) if masked else None
            dqs, dks, dvs = [], [], []
            for h in range(2):
                rows = slice(h * HEAD_DIM, (h + 1) * HEAD_DIM)
                pt = jnp.exp(_foxt_logits(q, k, cq[h:h + 1, :], ck_ref[h, ks, :], mask, hm[h])
                             - lse_rows[h:h + 1, :])
                dst = pt * (_dot_nt(_sel(hm[h], v), do) - d_rows[h:h + 1, :])
                dsb = dst.astype(BF)
                dqs.append(_dot(kt[rows, :], dsb))
                dks.append(_dot_nt(qt[rows, :], dsb))
                dvs.append(_dot_nt(dot[rows, :], pt.astype(BF)))
                dc_s[h, ks, :] += _fold(dst)
            cols = pl.ds(pl.multiple_of(qi * tq + qs.start, qs.stop - qs.start), qs.stop - qs.start)
            dq_ref[:, cols] += SCALE * jnp.concatenate(dqs, axis=0)
            dk_ref[:, ks] += jnp.concatenate(dks, axis=0)
            dv_ref[:, ks] += jnp.concatenate(dvs, axis=0)

        @pl.when(kj < qi)
        def _():
            step(slice(0, tk), slice(0, tq), False)

        @pl.when(kj == qi)
        def _():
            for ks, qs in _diag_blocks(tq):
                step(ks, qs, True)

        @pl.when(last)
        def _():
            for h in range(2):
                dc_ref[h:h + 1, :] = -jnp.sum(jnp.transpose(dc_s[h]), axis=0, keepdims=True)

    def kj_of(t):
        return _sweep_k_major(t, nq)[0]

    def qi_of(t):
        return _sweep_k_major(t, nq)[1]

    qblk = pl.BlockSpec((tq, LANES), lambda p, t: (qi_of(t), p))
    row = pl.BlockSpec((None, 2, tq), lambda p, t: (p, 0, qi_of(t)))
    kblk = pl.BlockSpec((LANES, tk), lambda p, t: (p, kj_of(t)))
    rep = pl.BlockSpec((2, tk, LANES), lambda p, t: (p, kj_of(t), 0))
    return pl.pallas_call(
        body, name=name, grid=(npair, _fox_steps(nq)),
        in_specs=[qblk,
                  pl.BlockSpec((tk, LANES), lambda p, t: (kj_of(t), npair + p)),
                  pl.BlockSpec((tk, LANES), lambda p, t: (kj_of(t), 2 * npair + p)),
                  qblk, row, row, row, rep],
        out_specs=[pl.BlockSpec((LANES, s), lambda p, t: (p, 0)), kblk, kblk,
                   pl.BlockSpec((None, 2, tk), lambda p, t: (p, 0, kj_of(t)))],
        out_shape=[jax.ShapeDtypeStruct((MIX_W, s), F32), jax.ShapeDtypeStruct((MIX_W, s), F32),
                   jax.ShapeDtypeStruct((MIX_W, s), F32), jax.ShapeDtypeStruct((npair, 2, s), F32)],
        scratch_shapes=[pltpu.VMEM((2, tk, LANES), F32)],
        compiler_params=_cp("arbitrary", "arbitrary"))(hb, hb, hb, dcat, dsum, lse, c_t3, c_rep)


def _adam_rows(r, c):
    cap = max(8, (1 << 20) // (4 * c))
    if r <= cap:
        return r
    best = None
    for t in range(8, cap + 1, 8):
        if r % t == 0:
            best = t
    return best if best is not None else r


def _reduce_adamw(contribs, w, m, v, name):
    nl = len(contribs)
    nd, r, c = contribs[0].shape
    tr = _adam_rows(r, c)
    bc1 = 1.0 - ADAM_B1 ** ADAM_STEP
    bc2 = 1.0 - ADAM_B2 ** ADAM_STEP

    def body(*refs):
        c_refs = refs[:nl]
        w_ref, m_ref, v_ref, g_ref, d_ref, nm_ref, nv_ref = refs[nl:]
        l = pl.program_id(0)
        for li in range(nl):
            @pl.when(l == li)
            def _(c_ref=c_refs[li]):
                g = c_ref[0].astype(F32)
                for k in range(1, nd):
                    g = g + c_ref[k].astype(F32)
                nm = ADAM_B1 * m_ref[...] + (1.0 - ADAM_B1) * g
                nv = ADAM_B2 * v_ref[...] + (1.0 - ADAM_B2) * (g * g)
                g_ref[...] = g
                nm_ref[...] = nm
                nv_ref[...] = nv
                d_ref[...] = -ADAM_LR * ((nm / bc1) / (jnp.sqrt(nv / bc2) + ADAM_EPS) + ADAM_WD * w_ref[...])

    def c_spec(li):
        return pl.BlockSpec((nd, tr, c), lambda l, i: (0, jnp.where(l == li, i, 0), 0))

    blk = pl.BlockSpec((None, tr, c), lambda l, i: (l, i, 0))
    out = jax.ShapeDtypeStruct((nl, r, c), F32)
    return pl.pallas_call(
        body, name=name, grid=(nl, r // tr),
        in_specs=[c_spec(li) for li in range(nl)] + [blk, blk, blk],
        out_specs=[blk, blk, blk, blk], out_shape=[out, out, out, out],
        compiler_params=_cp("arbitrary", "arbitrary"))(*contribs, w, m, v)


def _mesh_pos():
    return lax.axis_index("x"), lax.axis_index("y"), lax.axis_index("c")


def _peer(pos, k):
    x, y, c = pos
    return (1 - x if k & 4 else x, 1 - y if k & 2 else y, 1 - c if k & 1 else c)


def _linear(pos):
    return 4 * pos[0] + 2 * pos[1] + pos[2]


def _xfer_copies(srcs, lands, send_sems, recv_sems, local_sems, gather):
    pos = _mesh_pos()
    me = _linear(pos)
    local, remote = [], []
    for i, (src, land) in enumerate(zip(srcs, lands)):
        local.append(pltpu.make_async_copy(src if gather else src.at[me], land.at[me], local_sems.at[i]))
        for k in range(1, N_DEV):
            peer = _peer(pos, k)
            remote.append(pltpu.make_async_remote_copy(
                src_ref=src if gather else src.at[_linear(peer)], dst_ref=land.at[me],
                send_sem=send_sems.at[i * (N_DEV - 1) + k - 1], recv_sem=recv_sems.at[i * (N_DEV - 1) + k - 1],
                device_id=peer, device_id_type=MESH_ID))
    return local, remote


_HBM = pl.BlockSpec(memory_space=pltpu.HBM)
_SEM = pl.BlockSpec(memory_space=pltpu.SEMAPHORE)
_EFFECT = pltpu.SideEffectType.DATAFLOW_SIDE_EFFECTING


def _xfer_start(srcs, gather, name, after=()):
    n = len(srcs)
    na = len(after)
    lands = [lax.empty(((N_DEV,) + a.shape) if gather else a.shape, a.dtype) for a in srcs]

    def body(*refs):
        src, land = refs[:n], refs[n:2 * n]
        send_sems, recv_sems, local_sems = refs[2 * n + na:2 * n + na + 3]
        local, remote = _xfer_copies(src, land, send_sems, recv_sems, local_sems, gather)
        for cp in local + remote:
            cp.start()
        refs[-1][...] = jnp.zeros_like(refs[-1])

    nsem = n * (N_DEV - 1)
    out = pl.pallas_call(
        body, name=name,
        out_shape=(pltpu.SemaphoreType.DMA((nsem,)), pltpu.SemaphoreType.DMA((nsem,)), pltpu.SemaphoreType.DMA((n,)),
                   *[pltpu.HBM(a.shape, a.dtype) for a in srcs], *[pltpu.HBM(a.shape, a.dtype) for a in lands],
                   jax.ShapeDtypeStruct((8, LANES), F32)),
        in_specs=[_HBM] * (2 * n) + [pl.BlockSpec(memory_space=pl.ANY)] * na,
        out_specs=(_SEM, _SEM, _SEM, *[_HBM] * (2 * n), pl.BlockSpec(memory_space=pltpu.VMEM)),
        input_output_aliases={i: 3 + i for i in range(2 * n)},
        compiler_params=pltpu.CompilerParams(has_side_effects=_EFFECT))(
            *[pltpu.with_memory_space_constraint(a, pltpu.HBM) for a in srcs],
            *[pltpu.with_memory_space_constraint(a, pltpu.HBM) for a in lands], *after)
    return out[:3], list(out[3:3 + n]), list(out[3 + n:3 + 2 * n]), out[-1]


def _started(handle):
    return handle[3]


def _xfer_wait(handle, after, gather, name):
    sems, srcs, lands, _ = handle
    n = len(srcs)

    def body(*refs):
        src, land = refs[:n], refs[n:2 * n]
        send_sems, recv_sems, local_sems = refs[2 * n:2 * n + 3]
        local, remote = _xfer_copies(src, land, send_sems, recv_sems, local_sems, gather)
        for cp in local:
            cp.wait()
        for cp in remote:
            cp.wait_send()
            cp.wait_recv()

    out = pl.pallas_call(
        body, name=name,
        out_shape=(*[pltpu.HBM(a.shape, a.dtype) for a in srcs], *[pltpu.HBM(a.shape, a.dtype) for a in lands]),
        in_specs=[_HBM] * (2 * n) + [_SEM] * 3 + [pl.BlockSpec(memory_space=pl.ANY)] * len(after),
        out_specs=tuple([_HBM] * (2 * n)), input_output_aliases={i: i for i in range(2 * n)},
        compiler_params=pltpu.CompilerParams(has_side_effects=_EFFECT))(*srcs, *lands, *sems, *after)
    return list(out[n:])


def _cols_full(g):
    nd, r, c = g.shape
    return jnp.transpose(g, (1, 0, 2)).reshape(r, nd * c)


def _cols_split(full):
    r, n = full.shape
    return jnp.transpose(full.reshape(r, N_DEV, n // N_DEV), (1, 0, 2))


def _pack_b_in(w):
    qkv = 3 * MIX_W
    pad = jnp.zeros((w.shape[0], B_IN_PAD - w.shape[1]), w.dtype)
    return jnp.concatenate([w[:, :qkv], w[:, qkv + N_MIX_HEADS:], w[:, qkv:qkv + N_MIX_HEADS], pad], axis=1)


def _unpack_b_in(w):
    qkv = 3 * MIX_W
    return jnp.concatenate([w[:, :qkv], w[:, qkv + MEM_W:qkv + MEM_W + N_MIX_HEADS], w[:, qkv:qkv + MEM_W]], axis=1)


def _ffn_forward(x, xb, wgu, get_rest, tag, fused=True, target=None):
    if fused:
        wd4, gain, bias = get_rest(x)
        y, yb, gu, a, xh, rstd = _ffn_fwd_chunks(x, xb, wgu, wd4, gain, bias, f"{tag}_fwd_chunks", target)
    else:
        gu, a = _ffn_up(xb, wgu, f"{tag}_up")
        wd4, gain, bias = get_rest(a)
        y, yb, xh, rstd = _mm_res_ln(a, wd4, x, gain, bias, 0.5, f"{tag}_down_ln")
    return y, yb, (xb, gu, a, xh, rstd), wd4


def _ffn_backward(dy, saved, wgu, wd4, gain, tag, after=(), send=None):
    xb, gu, a, xh, rstd = saved
    s = xb.shape[0]
    nd, c, d = wgu.shape
    dx, dzb, dh, dgain, dbias = _ffn_bwd_main(dy, xh, rstd, gain, wd4, wgu, gu, f"{tag}_bwd_main", after,
                                               with_dx=send is None)
    dh = dh.reshape(nd, s, c)
    dwd = _mm_tn(a, dzb[None], f"{tag}_dwd").reshape(nd, wd4.shape[1] // 2, d)
    if send is not None:
        send("down", dwd, dgain, dbias)
    dwgu = _mm_tn(dh, xb[None], f"{tag}_dwgu")
    if send is not None:
        sent = send("gate_up", dwgu)
        dx = _mm_nt(dh, wgu, f"{tag}_dx", res=dx, w_rows_out=False, after=sent)
    return dx, dwgu, dwd, dgain, dbias


def _mixer_a_forward(x, xb, memb, w_in, w_kv, w_out, gain, bias, tabs):
    hb = _proj_rope(xb, w_in, tabs, 2 * MIX_W // LANES, "a_in", True)
    groups = [_band_fwd(hb, g, f"a_band_fwd{g}") for g in range(N_GROUPS)]
    oa, lt = _band_combine([o for o, _ in groups], [l for _, l in groups], "a_combine")
    kv = _mm_nn(memb, w_kv, BF, "a_mem_kv")
    om, lm = _mem_fwd(hb, 3 * MIX_W // LANES, kv, "a_mem_fwd")
    cat = jnp.concatenate([oa, om], axis=1)
    y, yb, xh, rstd = _mm_res_ln(cat[None], w_out[None], x, gain, bias, 1.0, "a_out_ln")
    return y, yb, (xb, hb, oa, lt, kv, lm, cat, xh, rstd)


def _mixer_a_backward(dy, saved, memb, w_in, w_kv, w_out, gain, tabs_neg, after=()):
    xb, hb, oa, lt, kv, lm, cat, xh, rstd = saved
    dz, dzb, dcat, dgain, dbias = _ln_bwd_proj(dy, xh, rstd, gain, w_out, "a_ln_bwd", after)
    dw_out = _mm_tn(cat[None], dzb[None], "a_dwout")[0]
    dqm, dkm, dvm = _mem_bwd(hb, 3 * MIX_W // LANES, kv, dcat, cat, GROUP_W // LANES, lm, "a_mem_bwd")
    dkv = jnp.concatenate([dkm, dvm], axis=1).astype(BF)
    dw_kv = _mm_tn(memb[None], dkv[None], "a_dwkv")[0]
    grads = [_band_bwd(hb, dcat, oa, lt, g, f"a_band_bwd{g}") for g in range(N_GROUPS)]
    dhb = _rope_cast([grads[g][i] for i in range(3) for g in range(N_GROUPS)] + [dqm], tabs_neg,
                     2 * MIX_W // LANES, "a_rope_bwd")
    dw_in = _mm_tn(dhb[None], xb[None], "a_dwin")[0]
    dx = _mm_nt(dhb[None], w_in[None], "a_dx", res=dz, w_rows_out=False)
    return dx, dw_in, dw_kv, dw_out, dgain, dbias


def _pad_rows(t, rows):
    return jnp.concatenate([t, jnp.zeros((rows - t.shape[0], t.shape[1]), t.dtype)], axis=0)


def _pad_cols(t, cols):
    return jnp.concatenate([t, jnp.zeros((t.shape[0], cols - t.shape[1]), t.dtype)], axis=1)


def _mixer_b_forward(x, xb, memb, w_in, fbias, w_kv, w_out, gain, bias, tabs):
    s = x.shape[0]
    hb, f = _proj_rope(xb, w_in, tabs, 0, "b_in", False, tail_block=(3 * MIX_W + MEM_W) // LANES)
    f_t = _pad_rows(jnp.transpose(f[:, :N_MIX_HEADS]), 16)
    bias16 = _pad_rows(jnp.transpose(fbias), 16)
    c_t = _gate_fwd(f_t, bias16, "b_gate_fwd")
    c_t3 = c_t[:N_MIX_HEADS].reshape(N_MIX_HEADS // 2, 2, s)
    c_rep = jnp.broadcast_to(c_t[:N_MIX_HEADS, :, None], (N_MIX_HEADS, s, LANES))
    ob, lb = _foxt_fwd(hb, c_rep, c_t3, "b_fox_fwd")
    kv = _mm_nn(memb, w_kv, BF, "b_mem_kv")
    om, lm = _mem_fwd(hb, 3 * MIX_W // LANES, kv, "b_mem_fwd")
    cat = jnp.concatenate([ob, om], axis=1)
    y, yb, xh, rstd = _mm_res_ln(cat[None], w_out[None], x, gain, bias, 1.0, "b_out_ln")
    return y, yb, (xb, hb, f_t, bias16, c_rep, c_t3, lb, kv, lm, cat, xh, rstd)


def _mixer_b_backward(dy, saved, memb, w_in, w_kv, w_out, gain, tabs, after=()):
    xb, hb, f_t, bias16, c_rep, c_t3, lb, kv, lm, cat, xh, rstd = saved
    s = xb.shape[0]
    dz, dzb, dcat, dgain, dbias = _ln_bwd_proj(dy, xh, rstd, gain, w_out, "b_ln_bwd", after)
    dw_out = _mm_tn(cat[None], dzb[None], "b_dwout")[0]
    dqm, dkm, dvm = _mem_bwd(hb, 3 * MIX_W // LANES, kv, dcat, cat, MIX_W // LANES, lm, "b_mem_bwd")
    dkv = jnp.concatenate([dkm, dvm], axis=1).astype(BF)
    dw_kv = _mm_tn(memb[None], dkv[None], "b_dwkv")[0]
    dsum = _foxt_dsum(hb, dcat, lb, c_rep, c_t3, "b_fox_dsum")
    dq, dk, dv, dc3 = _foxt_bwd(hb, dcat, dsum, lb, c_rep, c_t3, "b_fox_bwd")
    df_t, dfb = _gate_bwd(_pad_rows(dc3.reshape(N_MIX_HEADS, s), 16), f_t, bias16, "b_gate_bwd")
    df = _pad_cols(jnp.transpose(df_t[:N_MIX_HEADS]), B_IN_PAD - 3 * MIX_W - MEM_W)
    dhb = _rope_cast([dq, dk, dv, dqm, df], tabs, 0, "b_cast_bwd", transposed=(0, 1, 2))
    dw_in = _mm_tn(xb[None], dhb[None], "b_dwin")[0]
    dx = _mm_nt(dhb[None], w_in[None], "b_dx", res=dz)
    return dx, dw_in, jnp.transpose(dfb[:N_MIX_HEADS]), dw_kv, dw_out, dgain, dbias


def _stored(t, name):
    return jnp.transpose(t, (0, 2, 1)) if name in ROWS_OUT else t


GATHER_GROUPS = (
    (("ffn1_w_gate_up", 0),),
    (("ffn1_w_down", 0), ("ln_gain", None), ("ln_bias", None)),
    (("a_w_in", 0), ("a_w_out", 0), ("mem_w_kv", 0)),
    (("ffn2_w_gate_up", 0), ("ffn2_w_down", 0)),
    (("ffn1_w_gate_up", 1), ("ffn1_w_down", 1)),
    (("b_w_in", 0), ("b_w_out", 0), ("mem_w_kv", 1)),
    (("ffn2_w_gate_up", 1), ("ffn2_w_down", 1)),
)


def _group_shards(group, params):
    return [t if n in F32_COMM else _stored(t, n)[l].astype(BF) for (n, l), t in zip(group, params)]


def _weight_groups(w):
    return [_group_shards(grp, [w[n] for n, _ in grp]) for grp in GATHER_GROUPS]


def _local_step(x, mem, target, fbias, get_w, put_g):
    s, d = x.shape
    tabs = _rope_tables(s, 1.0)
    tabs_neg = _rope_tables(s, -1.0)
    memb = mem.astype(BF)
    saved, wl = [], []
    cur, curb = x, x.astype(BF)
    ln = []

    def down4(t):
        return t.reshape(N_DEV // 2, -1, d)

    for i in range(DEPTH):
        if i == 0:
            def first_rest(a):
                g = get_w(1, a)
                ln.extend(jnp.transpose(t, (1, 2, 0, 3)).reshape(DEPTH, 3, 1, d) for t in g[1:3])
                return down4(g[0]), ln[0][0, 0], ln[1][0, 0]

            wgu = get_w(0, cur)[0]
            cur, curb, s1, wd = _ffn_forward(cur, curb, wgu, first_rest, "l0_ffn1", fused=False)
        else:
            g = get_w(3 * i + 1, cur)
            wgu = g[0]
            cur, curb, s1, wd = _ffn_forward(cur, curb, wgu, lambda a, g=g: (down4(g[1]), ln[0][i, 0], ln[1][i, 0]),
                                             f"l{i}_ffn1")
        w1 = (wgu, wd)
        ln_g, ln_b = ln
        g = get_w(3 * i + 2, cur)
        if i == 0:
            wm = (g[0].reshape(-1, d), g[2].reshape(d, -1), _cols_full(g[1]))
            cur, curb, s2 = _mixer_a_forward(cur, curb, memb, wm[0], wm[1], wm[2], ln_g[i, 1], ln_b[i, 1], tabs)
        else:
            wm = (_pack_b_in(g[0].reshape(d, -1)), g[2].reshape(d, -1), g[1].reshape(d, -1))
            cur, curb, s2 = _mixer_b_forward(cur, curb, memb, wm[0], fbias, wm[1], wm[2], ln_g[i, 1], ln_b[i, 1],
                                             tabs)
        g = get_w(3 * i + 3, cur)
        cur, curb, s3, wd = _ffn_forward(cur, curb, g[0], lambda a, g=g: (down4(g[1]), ln_g[i, 2], ln_b[i, 2]),
                                         f"l{i}_ffn2", target=target if i == DEPTH - 1 else None)
        w3 = (g[0], wd)
        saved.append((s1, s2, s3))
        wl.append((w1, wm, w3))

    dy, loss = cur, curb

    dgs = [[None] * 3 for _ in range(DEPTH)]
    dbs = [[None] * 3 for _ in range(DEPTH)]
    sent = ()
    for i in reversed(range(DEPTH)):
        s1, s2, s3 = saved[i]
        w1, wm, w3 = wl[i]
        dy, dgu, dd, dgs[i][2], dbs[i][2] = _ffn_backward(dy, s3, w3[0], w3[1], ln_g[i, 2], f"l{i}_ffn2", sent)
        sent = put_g(3 * i + 2, [dgu, dd])
        if i == 0:
            dy, dw_in, dw_kv, dw_out, dgs[i][1], dbs[i][1] = _mixer_a_backward(
                dy, s2, memb, wm[0], wm[1], wm[2], ln_g[i, 1], tabs_neg, sent)
            sent = put_g(1, [dw_in.reshape(N_DEV, -1, d), _cols_split(dw_out),
                             dw_kv.reshape(N_DEV, d // N_DEV, -1)])
        else:
            dy, dw_in, dfb, dw_kv, dw_out, dgs[i][1], dbs[i][1] = _mixer_b_backward(
                dy, s2, memb, wm[0], wm[1], wm[2], ln_g[i, 1], tabs, sent)
            sent = put_g(4, [_unpack_b_in(dw_in).reshape(N_DEV, d // N_DEV, -1),
                             dw_out.reshape(N_DEV, d // N_DEV, -1), dw_kv.reshape(N_DEV, d // N_DEV, -1),
                             jnp.broadcast_to(dfb[None], (N_DEV,) + dfb.shape)])
        if i == 0:
            def send_last(kind, dw, dgain=None, dbias=None):
                if kind == "gate_up":
                    return put_g(6, [dw])
                dgs[0][0], dbs[0][0] = dgain, dbias
                ln_pieces = []
                for parts in (dgs, dbs):
                    t = jnp.concatenate([parts[a][b] for a in range(DEPTH) for b in range(3)], axis=0)
                    ln_pieces.append(jnp.transpose(t.reshape(DEPTH * 3, N_DEV, d // N_DEV), (1, 0, 2)))
                return put_g(0, [dw] + ln_pieces)

            dy = _ffn_backward(dy, s1, w1[0], w1[1], ln_g[i, 0], "l0_ffn1", sent, send_last)[0]
        else:
            dy, dgu, dd, dgs[i][0], dbs[i][0] = _ffn_backward(dy, s1, w1[0], w1[1], ln_g[i, 0], f"l{i}_ffn1", sent)
            sent = put_g(3, [dgu, dd])
    return loss, dy


WEIGHTS = ("ffn1_w_gate_up", "ffn1_w_down", "ffn2_w_gate_up", "ffn2_w_down", "ln_gain", "ln_bias", "mem_w_kv",
           "a_w_in", "a_w_out", "b_w_in", "b_forget_bias", "b_w_out")
F32_COMM = ("ln_gain", "ln_bias", "b_forget_bias")
ROWS_OUT = ("ffn1_w_gate_up", "ffn2_w_gate_up", "a_w_in")
GRAD_SLOTS = {
    "ffn1_w_gate_up": [(6, 0), (3, 0)], "ffn1_w_down": [(0, 0), (3, 1)],
    "ffn2_w_gate_up": [(2, 0), (5, 0)], "ffn2_w_down": [(2, 1), (5, 1)],
    "ln_gain": [(0, 1)], "ln_bias": [(0, 2)], "mem_w_kv": [(1, 2), (4, 2)],
    "a_w_in": [(1, 0)], "a_w_out": [(1, 1)], "b_w_in": [(4, 0)], "b_forget_bias": [(4, 3)], "b_w_out": [(4, 1)],
}


def kernel(x, mem, ffn1_w_gate_up, ffn1_w_down, ffn2_w_gate_up, ffn2_w_down, ln_gain, ln_bias, mem_w_kv, a_w_in, a_w_out, b_w_in, b_forget_bias, b_w_out, loss_target, m_ffn1_w_gate_up, m_ffn1_w_down, m_ffn2_w_gate_up, m_ffn2_w_down, m_ln_gain, m_ln_bias, m_mem_w_kv, m_a_w_in, m_a_w_out, m_b_w_in, m_b_forget_bias, m_b_w_out, v_ffn1_w_gate_up, v_ffn1_w_down, v_ffn2_w_gate_up, v_ffn2_w_down, v_ln_gain, v_ln_bias, v_mem_w_kv, v_a_w_in, v_a_w_out, v_b_w_in, v_b_forget_bias, v_b_w_out):
    w = dict(zip(WEIGHTS, (ffn1_w_gate_up, ffn1_w_down, ffn2_w_gate_up, ffn2_w_down, ln_gain, ln_bias, mem_w_kv,
                           a_w_in, a_w_out, b_w_in, b_forget_bias, b_w_out)))
    m = dict(zip(WEIGHTS, (m_ffn1_w_gate_up, m_ffn1_w_down, m_ffn2_w_gate_up, m_ffn2_w_down, m_ln_gain, m_ln_bias,
                           m_mem_w_kv, m_a_w_in, m_a_w_out, m_b_w_in, m_b_forget_bias, m_b_w_out)))
    v = dict(zip(WEIGHTS, (v_ffn1_w_gate_up, v_ffn1_w_down, v_ffn2_w_gate_up, v_ffn2_w_down, v_ln_gain, v_ln_bias,
                           v_mem_w_kv, v_a_w_in, v_a_w_out, v_b_w_in, v_b_forget_bias, v_b_w_out)))

    gathers = []
    for k, grp in enumerate(GATHER_GROUPS):
        params, behind = [w[n] for n, _ in grp], [_started(h) for h in gathers[-1:]]
        if behind:
            params, behind = lax.optimization_barrier((params, behind))
        gathers.append(_xfer_start(_group_shards(grp, params), True, f"gather{k}_start", behind))
    exchanges = {}

    def get_w(k, after):
        behind = [after] + ([_started(h) for h in gathers] if k == 0 else [])
        return _xfer_wait(gathers[k], behind, True, f"gather{k}_wait")

    def put_g(k, pieces):
        behind = [_started(exchanges[0])] if k == 6 else []
        exchanges[k] = _xfer_start(pieces, False, f"grads{k}_start", behind)
        return (_started(exchanges[k]),)

    loss, grad_x = _local_step(x[0], mem[0], loss_target[0], b_forget_bias, get_w, put_g)
    loss = lax.psum(loss[0, 0], ("x", "y", "c"))

    outs, landed = {}, {}

    def adamw(names):
        for n in names:
            contribs = [landed[g][j] for g, j in GRAD_SLOTS[n]]
            view = (len(contribs),) + contribs[0].shape[1:]
            shape = _stored(w[n], n).shape
            res = _reduce_adamw(contribs, *[_stored(t[n], n).reshape(view) for t in (w, m, v)], f"adamw_{n}")
            outs[n] = [_stored(t.reshape(shape), n) for t in res]
        return [outs[n][3] for n in names]

    after = [grad_x]
    for k in (5, 4, 3, 2, 1):
        landed[k] = _xfer_wait(exchanges[k], after, False, f"grads{k}_wait")
        after = [landed[k][0]]
    done = adamw(("ffn2_w_gate_up", "ffn2_w_down", "mem_w_kv", "a_w_in", "a_w_out", "b_w_in", "b_forget_bias",
                  "b_w_out"))
    landed[0] = _xfer_wait(exchanges[0], done, False, "grads0_wait")
    done = adamw(("ffn1_w_down", "ln_gain", "ln_bias"))
    landed[6] = _xfer_wait(exchanges[6], done, False, "grads6_wait")
    adamw(("ffn1_w_gate_up",))
    return (loss, grad_x[None], *[outs[n][0] for n in WEIGHTS], *[outs[n][1] for n in WEIGHTS],
            *[outs[n][2] for n in WEIGHTS], *[outs[n][3] for n in WEIGHTS])
```

```python
import functools

import jax
import jax.numpy as jnp
from jax import lax
from jax.experimental import pallas as pl
from jax.experimental.pallas import tpu as pltpu

F32 = jnp.float32
BF = jnp.bfloat16
MESH_ID = pl.DeviceIdType.MESH

N_DEV = 8
DEPTH = 2
HEAD_DIM = 64
LANES = 128
N_MIX_HEADS = 12
N_MEM_HEADS = 4
MIX_W = N_MIX_HEADS * HEAD_DIM
MEM_W = N_MEM_HEADS * HEAD_DIM
N_GROUPS = 3
GROUP_W = MIX_W // N_GROUPS
BLOCK = 128
BAND_SUB = 4
BAND_COLS = 4
ROT_HALF = 8
ROPE_THETA = 500000.0
ALPHA = (2 * DEPTH) ** 0.25
LN_EPS = 1e-5
SCALE = HEAD_DIM ** -0.5
NEG = -1e30
B_IN_PAD = 2688
ADAM_LR, ADAM_B1, ADAM_B2, ADAM_EPS, ADAM_WD, ADAM_STEP = 0.001, 0.9, 0.999, 1e-08, 0.01, 10
VMEM_LIMIT = 56 * 1024 * 1024


def _cp(*sem):
    return pltpu.CompilerParams(dimension_semantics=sem, vmem_limit_bytes=VMEM_LIMIT)


def _dot(a, b):
    return jnp.dot(a, b, preferred_element_type=F32)


def _dot_nt(a, b):
    return lax.dot_general(a, b, (((1,), (1,)), ((), ())), preferred_element_type=F32)


def _dot_tn(a, b):
    return lax.dot_general(a, b, (((0,), (0,)), ((), ())), preferred_element_type=F32)


def _sigmoid(x):
    return 1.0 / (1.0 + jnp.exp(-x))


def _tile(n, cap=1024):
    if n <= cap:
        return n
    best = LANES
    for t in range(LANES, cap + 1, LANES):
        if n % t == 0:
            best = t
    return best


def _rows(s, cap=512):
    return s if s <= cap else cap


def _mm_nn(a, b, out_dtype, name, b_rows_out=False):
    m, k = a.shape
    n = b.shape[0] if b_rows_out else b.shape[1]
    tm, tn = _rows(m), _tile(n)

    def body(a_ref, b_ref, o_ref):
        prod = _dot_nt(a_ref[...], b_ref[...]) if b_rows_out else _dot(a_ref[...], b_ref[...])
        o_ref[...] = prod.astype(o_ref.dtype)

    b_spec = (pl.BlockSpec((tn, k), lambda j, i: (j, 0)) if b_rows_out
              else pl.BlockSpec((k, tn), lambda j, i: (0, j)))
    return pl.pallas_call(
        body, name=name, grid=(n // tn, m // tm),
        in_specs=[pl.BlockSpec((tm, k), lambda j, i: (i, 0)), b_spec],
        out_specs=pl.BlockSpec((tm, tn), lambda j, i: (i, j)),
        out_shape=jax.ShapeDtypeStruct((m, n), out_dtype),
        compiler_params=_cp("parallel", "parallel"))(a, b)


def _resident(shape, index_map):
    return pl.BlockSpec(shape, index_map, pipeline_mode=pl.Buffered(1))


def _mm_tn(a, b, name, out_dtype=BF):
    na, s, m = a.shape
    nb, _, n = b.shape
    no = max(na, nb)
    tm, tn = _tile(m), _tile(n)

    def body(a_ref, b_ref, o_ref):
        o_ref[...] = _dot_tn(a_ref[...], b_ref[...]).astype(o_ref.dtype)

    def spec(nbatch, width, tile, index_map):
        fixed = nbatch == 1 and width == tile
        return _resident((None, s, tile), index_map) if fixed else pl.BlockSpec((None, s, tile), index_map)

    return pl.pallas_call(
        body, name=name, grid=(no, m // tm, n // tn),
        in_specs=[spec(na, m, tm, lambda j, r, c: (j if na > 1 else 0, 0, r)),
                  spec(nb, n, tn, lambda j, r, c: (j if nb > 1 else 0, 0, c))],
        out_specs=pl.BlockSpec((None, tm, tn), lambda j, r, c: (j, r, c)),
        out_shape=jax.ShapeDtypeStruct((no, m, n), out_dtype),
        compiler_params=_cp("parallel", "parallel", "parallel"))(a, b)


def _mm_nt(dh, w, name, res=None, out_dtype=F32, w_rows_out=True, after=()):
    nc, s, kc = dh.shape
    d = w.shape[1] if w_rows_out else w.shape[2]
    ts = _rows(s)
    has_res = res is not None
    mm = _dot_nt if w_rows_out else _dot

    def body(*refs):
        o_ref = refs[-1]
        dh_ref, w_ref = refs[:2]
        if has_res:
            r_ref = refs[2]
        out = mm(dh_ref[0], w_ref[0])
        for j in range(1, nc):
            out = out + mm(dh_ref[j], w_ref[j])
        if has_res:
            out = out + ALPHA * r_ref[...]
        o_ref[...] = out.astype(o_ref.dtype)

    in_specs = [pl.BlockSpec((nc, ts, kc), lambda i: (0, i, 0)), _resident(w.shape, lambda i: (0, 0, 0))]
    args = [dh, w]
    if has_res:
        in_specs.append(pl.BlockSpec((ts, d), lambda i: (i, 0)))
        args.append(res)
    in_specs += [pl.BlockSpec(memory_space=pl.ANY)] * len(after)
    args += list(after)
    return pl.pallas_call(
        body, name=name, grid=(s // ts,), in_specs=in_specs,
        out_specs=pl.BlockSpec((ts, d), lambda i: (i, 0)),
        out_shape=jax.ShapeDtypeStruct((s, d), out_dtype),
        compiler_params=_cp("parallel"))(*args)


def _mm_res_ln(a, w, x, gain, bias, fscale, name):
    nc, s, kc = a.shape
    d = w.shape[2]
    ts = _rows(s)

    def body(a_ref, w_ref, x_ref, g_ref, b_ref, y_ref, yb_ref, xh_ref, r_ref):
        f = _dot(a_ref[0], w_ref[0])
        for j in range(1, nc):
            f = f + _dot(a_ref[j], w_ref[j])
        z = ALPHA * x_ref[...] + fscale * f
        mu = jnp.mean(z, axis=-1, keepdims=True)
        zc = z - mu
        var = jnp.mean(zc * zc, axis=-1, keepdims=True)
        r = lax.rsqrt(var + LN_EPS)
        xh = zc * r
        y = xh * g_ref[...] + b_ref[...]
        y_ref[...] = y
        yb_ref[...] = y.astype(BF)
        xh_ref[...] = xh
        r_ref[...] = r

    row = pl.BlockSpec((ts, d), lambda i: (i, 0))
    vec = pl.BlockSpec((1, d), lambda i: (0, 0))
    return pl.pallas_call(
        body, name=name, grid=(s // ts,),
        in_specs=[pl.BlockSpec((nc, ts, kc), lambda i: (0, i, 0)), _resident((nc, kc, d), lambda i: (0, 0, 0)),
                  row, vec, vec],
        out_specs=[row, row, row, pl.BlockSpec((ts, 1), lambda i: (i, 0))],
        out_shape=[jax.ShapeDtypeStruct((s, d), F32), jax.ShapeDtypeStruct((s, d), BF),
                   jax.ShapeDtypeStruct((s, d), F32), jax.ShapeDtypeStruct((s, 1), F32)],
        compiler_params=_cp("parallel"))(a, w, x, gain, bias)


def _ln_bwd_proj(dy, xh, rstd, gain, w_out, name, after=()):
    s, d = dy.shape
    wc = w_out.shape[0]
    ts = _rows(s)
    na = len(after)

    def body(*refs):
        dy_ref, xh_ref, r_ref, g_ref, w_ref = refs[:5]
        dz_ref, dzb_ref, dc_ref, dg_ref, db_ref = refs[5 + na:]
        i = pl.program_id(0)
        dyv = dy_ref[...]
        xhv = xh_ref[...]
        dxh = dyv * g_ref[...]
        m1 = jnp.mean(dxh, axis=-1, keepdims=True)
        m2 = jnp.mean(dxh * xhv, axis=-1, keepdims=True)
        dz = r_ref[...] * (dxh - m1 - xhv * m2)
        dzb = dz.astype(BF)
        dz_ref[...] = dz
        dzb_ref[...] = dzb
        dc_ref[...] = _dot_nt(dzb, w_ref[...]).astype(BF)

        @pl.when(i == 0)
        def _():
            dg_ref[...] = jnp.zeros_like(dg_ref)
            db_ref[...] = jnp.zeros_like(db_ref)

        dg_ref[...] += jnp.sum(dyv * xhv, axis=0, keepdims=True)
        db_ref[...] += jnp.sum(dyv, axis=0, keepdims=True)

    row = pl.BlockSpec((ts, d), lambda i: (i, 0))
    vec = pl.BlockSpec((1, d), lambda i: (0, 0))
    return pl.pallas_call(
        body, name=name, grid=(s // ts,),
        in_specs=[row, row, pl.BlockSpec((ts, 1), lambda i: (i, 0)), vec, _resident((wc, d), lambda i: (0, 0))]
                 + [pl.BlockSpec(memory_space=pl.ANY)] * na,
        out_specs=[row, row, pl.BlockSpec((ts, wc), lambda i: (i, 0)), vec, vec],
        out_shape=[jax.ShapeDtypeStruct((s, d), F32), jax.ShapeDtypeStruct((s, d), BF),
                   jax.ShapeDtypeStruct((s, wc), BF),
                   jax.ShapeDtypeStruct((1, d), F32), jax.ShapeDtypeStruct((1, d), F32)],
        compiler_params=_cp("arbitrary"))(dy, xh, rstd, gain, w_out, *after)


def _ffn_up(xb, wgu, name):
    s, d = xb.shape
    c = wgu.shape[1]
    nch = wgu.shape[0] // 2
    ts = _rows(s, 1024)
    w4 = wgu.reshape(2, nch, c, d)

    def body(x_ref, w_ref, gu_ref, a_ref):
        x = x_ref[...]
        g = _dot_nt(x, w_ref[0])
        u = _dot_nt(x, w_ref[1])
        sg = _sigmoid(g)
        t = g * sg
        gu_ref[0] = (u * (sg * (1.0 + g - t))).astype(BF)
        gu_ref[1] = t.astype(BF)
        a_ref[...] = (t * u).astype(BF)

    return pl.pallas_call(
        body, name=name, grid=(nch, s // ts),
        in_specs=[pl.BlockSpec((ts, d), lambda j, i: (i, 0)),
                  pl.BlockSpec((2, None, c, d), lambda j, i: (0, j, 0, 0))],
        out_specs=[pl.BlockSpec((2, None, ts, c), lambda j, i: (0, j, i, 0)),
                   pl.BlockSpec((None, ts, c), lambda j, i: (j, i, 0))],
        out_shape=[jax.ShapeDtypeStruct((2, nch, s, c), BF), jax.ShapeDtypeStruct((nch, s, c), BF)],
        compiler_params=_cp("parallel", "parallel"))(xb, w4)


def _ffn_fwd_main(x, xb, wgu, wd4, gain, bias, name, target=None):
    s, d = x.shape
    nch, c = wd4.shape[0], wd4.shape[1]
    ts = _rows(s, 256)
    head = target is not None

    def body(*refs):
        x_ref, xb_ref, wgu_hbm, wd_hbm, g_ref, b_ref = refs[:6]
        y_ref, yb_ref, gu_ref, a_ref, xh_ref, r_ref, wgu_ref, wd_ref, sem = refs[6 + head:]
        first = pl.program_id(0) == 0

        def copies(j):
            return (pltpu.make_async_copy(wgu_hbm.at[j], wgu_ref.at[j], sem.at[0, j]),
                    pltpu.make_async_copy(wgu_hbm.at[nch + j], wgu_ref.at[nch + j], sem.at[1, j]),
                    pltpu.make_async_copy(wd_hbm.at[j], wd_ref.at[j], sem.at[2, j]))

        @pl.when(first)
        def _():
            for j in range(nch):
                for cp in copies(j):
                    cp.start()
            compute_tile(refs, True, copies)

        @pl.when(jnp.logical_not(first))
        def _():
            compute_tile(refs, False, copies)

    def compute_tile(refs, waits, copies):
        x_ref, xb_ref, _, _, g_ref, b_ref = refs[:6]
        y_ref, yb_ref, gu_ref, a_ref, xh_ref, r_ref, wgu_ref, wd_ref, _ = refs[6 + head:]
        xbv = xb_ref[...]
        f = jnp.zeros((ts, d), F32)
        for j in range(nch):
            if waits:
                for cp in copies(j):
                    cp.wait()
            g = _dot_nt(xbv, wgu_ref[j])
            u = _dot_nt(xbv, wgu_ref[nch + j])
            sg = _sigmoid(g)
            t = g * sg
            gu_ref[0, j] = (u * (sg * (1.0 + g - t))).astype(BF)
            gu_ref[1, j] = t.astype(BF)
            act = (t * u).astype(BF)
            a_ref[j] = act
            f = f + _dot(act, wd_ref[j])
        z = ALPHA * x_ref[...] + 0.5 * f
        mu = jnp.mean(z, axis=-1, keepdims=True)
        zc = z - mu
        var = jnp.mean(zc * zc, axis=-1, keepdims=True)
        r = lax.rsqrt(var + LN_EPS)
        xh = zc * r
        y = xh * g_ref[...] + b_ref[...]
        xh_ref[...] = xh
        r_ref[...] = r
        if head:
            e = y - refs[6][...]
            y_ref[...] = e * (1.0 / d)

            part = jnp.sum(jnp.sum(e * e, axis=1, keepdims=True), axis=0, keepdims=True) * (0.5 / d)
            if waits:
                yb_ref[...] = part
            else:
                yb_ref[...] += part
        else:
            y_ref[...] = y
            yb_ref[...] = y.astype(BF)

    row = pl.BlockSpec((ts, d), lambda i: (i, 0))
    vec = pl.BlockSpec((1, d), lambda i: (0, 0))
    second = ((pl.BlockSpec((1, 1), lambda i: (0, 0)), jax.ShapeDtypeStruct((1, 1), F32)) if head
              else (row, jax.ShapeDtypeStruct((s, d), BF)))
    return pl.pallas_call(
        body, name=name, grid=(s // ts,),
        in_specs=[row, row, pl.BlockSpec(memory_space=pl.ANY), pl.BlockSpec(memory_space=pl.ANY),
                  vec, vec] + [row] * head,
        out_specs=[row, second[0], pl.BlockSpec((2, nch, ts, c), lambda i: (0, 0, i, 0)),
                   pl.BlockSpec((nch, ts, c), lambda i: (0, i, 0)), row, pl.BlockSpec((ts, 1), lambda i: (i, 0))],
        out_shape=[jax.ShapeDtypeStruct((s, d), F32), second[1],
                   jax.ShapeDtypeStruct((2, nch, s, c), BF), jax.ShapeDtypeStruct((nch, s, c), BF),
                   jax.ShapeDtypeStruct((s, d), F32), jax.ShapeDtypeStruct((s, 1), F32)],
        scratch_shapes=[pltpu.VMEM(wgu.shape, BF), pltpu.VMEM(wd4.shape, BF), pltpu.SemaphoreType.DMA((3, nch))],
        compiler_params=_cp("arbitrary"))(x, xb, wgu, wd4, gain, bias, *([target] if head else []))


def _ffn_bwd_main(dy, xh, rstd, gain, wd4, wgu, gu, name, after=(), with_dx=True):
    s, d = dy.shape
    nch, c = wd4.shape[0], wd4.shape[1]
    ts = _rows(s, 256)
    na = len(after)

    def body(*refs):
        dy_ref, xh_ref, r_ref, g_ref, wd_ref, wgu_ref, gu_ref = refs[:7]
        dx_ref, dzb_ref, dh_ref, dg_ref, db_ref = refs[7 + na:]
        i = pl.program_id(0)
        dyv = dy_ref[...]
        xhv = xh_ref[...]
        dxh = dyv * g_ref[...]
        m1 = jnp.mean(dxh, axis=-1, keepdims=True)
        m2 = jnp.mean(dxh * xhv, axis=-1, keepdims=True)
        dz = r_ref[...] * (dxh - m1 - xhv * m2)
        dzb = (0.5 * dz).astype(BF)
        dzb_ref[...] = dzb

        @pl.when(i == 0)
        def _():
            dg_ref[...] = jnp.zeros_like(dg_ref)
            db_ref[...] = jnp.zeros_like(db_ref)

        dg_ref[...] += jnp.sum(dyv * xhv, axis=0, keepdims=True)
        db_ref[...] += jnp.sum(dyv, axis=0, keepdims=True)

        dx = ALPHA * dz if with_dx else dz
        for j in range(nch):
            da = _dot_nt(dzb, wd_ref[j])
            dgate = (da * gu_ref[0, j].astype(F32)).astype(BF)
            dup = (da * gu_ref[1, j].astype(F32)).astype(BF)
            dh_ref[0, j] = dgate
            dh_ref[1, j] = dup
            if with_dx:
                dx = dx + _dot(dgate, wgu_ref[j]) + _dot(dup, wgu_ref[nch + j])
        dx_ref[...] = dx

    row = pl.BlockSpec((ts, d), lambda i: (i, 0))
    vec = pl.BlockSpec((1, d), lambda i: (0, 0))
    act = pl.BlockSpec((2, nch, ts, c), lambda i: (0, 0, i, 0))
    return pl.pallas_call(
        body, name=name, grid=(s // ts,),
        in_specs=[row, row, pl.BlockSpec((ts, 1), lambda i: (i, 0)), vec,
                  _resident(wd4.shape, lambda i: (0, 0, 0)), _resident(wgu.shape, lambda i: (0, 0, 0)), act]
                 + [pl.BlockSpec(memory_space=pl.ANY)] * na,
        out_specs=[row, row, act, vec, vec],
        out_shape=[jax.ShapeDtypeStruct((s, d), F32), jax.ShapeDtypeStruct((s, d), BF),
                   jax.ShapeDtypeStruct((2, nch, s, c), BF),
                   jax.ShapeDtypeStruct((1, d), F32), jax.ShapeDtypeStruct((1, d), F32)],
        compiler_params=_cp("arbitrary"))(dy, xh, rstd, gain, wd4, wgu, gu, *after)


def _rope_tables(s, sign):
    pos = jnp.arange(s, dtype=F32)
    inv_freq = 1.0 / (ROPE_THETA ** (jnp.arange(ROT_HALF, dtype=F32) / ROT_HALF))
    ang = pos[:, None] * inv_freq[None, :]
    cos, sin = jnp.cos(ang), jnp.sin(ang) * sign
    one = jnp.ones((s, HEAD_DIM - 2 * ROT_HALF), F32)
    zero = jnp.zeros((s, HEAD_DIM - 2 * ROT_HALF), F32)
    zh = jnp.zeros((s, ROT_HALF), F32)
    cos_f = jnp.concatenate([cos, cos, one], axis=1)
    sin_a = jnp.concatenate([-sin, zh, zero], axis=1)
    sin_b = jnp.concatenate([zh, sin, zero], axis=1)
    rep = LANES // HEAD_DIM
    return tuple(jnp.tile(t, (1, rep)) for t in (cos_f, sin_a, sin_b))


def _rope(t, c_ref, sa_ref, sb_ref):
    return (t * c_ref[...] + pltpu.roll(t, LANES - ROT_HALF, 1) * sa_ref[...]
            + pltpu.roll(t, ROT_HALF, 1) * sb_ref[...])


def _proj_rope(xb, w, tabs, n_rope, name, w_rows_out, tail_block=None):
    s, d = xb.shape
    n = w.shape[0] if w_rows_out else w.shape[1]
    tm = _rows(s, 256)
    has_tail = tail_block is not None

    def body(x_ref, w_ref, c_ref, sa_ref, sb_ref, o_ref, *tail_ref):
        h = (_dot_nt if w_rows_out else _dot)(x_ref[...], w_ref[...])
        for cb in range(n // LANES):
            t = h[:, cb * LANES:(cb + 1) * LANES]
            if cb < n_rope:
                t = _rope(t, c_ref, sa_ref, sb_ref)
            o_ref[:, cb * LANES:(cb + 1) * LANES] = t.astype(BF)
        if has_tail:
            tail_ref[0][...] = h[:, tail_block * LANES:(tail_block + 1) * LANES]

    tab = pl.BlockSpec((tm, LANES), lambda i: (i, 0))
    out_specs = [pl.BlockSpec((tm, n), lambda i: (i, 0))]
    out_shape = [jax.ShapeDtypeStruct((s, n), BF)]
    if has_tail:
        out_specs.append(tab)
        out_shape.append(jax.ShapeDtypeStruct((s, LANES), F32))
    res = pl.pallas_call(
        body, name=name, grid=(s // tm,),
        in_specs=[pl.BlockSpec((tm, d), lambda i: (i, 0)), _resident(w.shape, lambda i: (0, 0)), tab, tab, tab],
        out_specs=out_specs, out_shape=out_shape, compiler_params=_cp("parallel"))(xb, w, *tabs)
    return res if has_tail else res[0]


def _rope_cast(parts, tabs, n_rope, name, transposed=()):
    s = tabs[0].shape[0]
    flip = [i in transposed for i in range(len(parts))]
    widths = [p.shape[0] if f else p.shape[1] for p, f in zip(parts, flip)]
    n = sum(widths)
    npart = len(parts)
    ts = _rows(s, 256)

    def body(*refs):
        part_refs = refs[:npart]
        c_ref, sa_ref, sb_ref, o_ref = refs[npart:]
        col = 0
        for ref, w, f in zip(part_refs, widths, flip):
            for j in range(w // LANES):
                if f:
                    t = jnp.transpose(ref[j * LANES:(j + 1) * LANES, :])
                else:
                    t = ref[:, j * LANES:(j + 1) * LANES]
                if col < n_rope:
                    t = _rope(t, c_ref, sa_ref, sb_ref)
                o_ref[:, col * LANES:(col + 1) * LANES] = t.astype(BF)
                col += 1

    tab = pl.BlockSpec((ts, LANES), lambda i: (i, 0))
    return pl.pallas_call(
        body, name=name, grid=(s // ts,),
        in_specs=[pl.BlockSpec((w, ts), lambda i: (0, i)) if f else pl.BlockSpec((ts, w), lambda i: (i, 0))
                  for w, f in zip(widths, flip)] + [tab, tab, tab],
        out_specs=pl.BlockSpec((ts, n), lambda i: (i, 0)),
        out_shape=jax.ShapeDtypeStruct((s, n), BF),
        compiler_params=_cp("parallel"))(*parts, *tabs)


def _head_masks():
    lane = lax.broadcasted_iota(jnp.int32, (1, LANES), 1)
    return [lane < HEAD_DIM, lane >= HEAD_DIM]


def _sel(mask, v):
    return jnp.where(mask, v, jnp.zeros_like(v))


def _pick(mask, wide, fill):
    return jnp.max(jnp.where(mask, wide, fill), axis=1, keepdims=True)


def _head_stack(hm, a, b):
    return jnp.concatenate([_sel(hm[0], a), _sel(hm[0], b), _sel(hm[1], a), _sel(hm[1], b)], axis=0)


def _band_mask_stack(has_prev):
    qi = lax.broadcasted_iota(jnp.int32, (BLOCK, 4 * BLOCK), 0)
    col = lax.broadcasted_iota(jnp.int32, (BLOCK, 4 * BLOCK), 1)
    d = jnp.bitwise_and(col, BLOCK - 1) - qi
    is_prev = jnp.bitwise_and(col, BLOCK) != 0
    return jnp.where(is_prev, d - jnp.where(has_prev, 0, BLOCK), -d) >= 0


class _BandView:
    def __init__(self, s, g):
        self.r = 4 ** g
        self.nl = s // self.r
        self.nblk = self.nl // BLOCK
        self.nsub = min(BAND_SUB, self.nblk)
        self.tile = self.nsub * BLOCK
        self.ncols = min(self.r * GROUP_W // LANES, BAND_COLS)
        self.grid = (self.r * GROUP_W // LANES // self.ncols, self.nblk // self.nsub)

    def view(self, a):
        return a.reshape(self.nl, self.r * a.shape[1])

    def qkv(self, hb, g):
        npair = MIX_W // LANES
        offs = [i * npair + g * GROUP_W // LANES for i in range(3)]
        if self.r == 1:
            return [hb] * 3, hb.shape[1], offs
        return [self.view(hb[:, o * LANES:o * LANES + GROUP_W]) for o in offs], GROUP_W, [0, 0, 0]

    def specs(self, width, off):
        assert off % self.ncols == 0 and (width == GROUP_W or self.r == 1)
        nsub, last, lanes, first = self.nsub, self.nblk - 1, self.ncols * LANES, off // self.ncols
        return (pl.BlockSpec((self.tile, lanes), lambda cg, t: (t, first + cg)),
                pl.BlockSpec((BLOCK, lanes), lambda cg, t: (jnp.maximum(t * nsub - 1, 0), first + cg)),
                pl.BlockSpec((BLOCK, lanes), lambda cg, t: (jnp.minimum(t * nsub + nsub, last), first + cg)))


def _band_fwd(hb, g, name):
    s, n = hb.shape
    bv = _BandView(s, g)
    nsub = bv.nsub
    npair = MIX_W // LANES

    def body(q_ref, kc_ref, kp_ref, vc_ref, vp_ref, o_ref, l_ref):
        t = pl.program_id(1)
        hm = _head_masks()
        for i, c in [(i, c) for i in range(nsub) for c in range(bv.ncols)]:
            rows = slice(i * BLOCK, (i + 1) * BLOCK)
            lanes = slice(c * LANES, (c + 1) * LANES)
            has_prev = t > 0 if i == 0 else True
            q, kc, vc = q_ref[rows, lanes], kc_ref[rows, lanes], vc_ref[rows, lanes]
            if i == 0:
                kp, vp = kp_ref[:, lanes], vp_ref[:, lanes]
            else:
                prev = slice((i - 1) * BLOCK, i * BLOCK)
                kp, vp = kc_ref[prev, lanes], vc_ref[prev, lanes]
            sc = jnp.where(_band_mask_stack(has_prev), _dot_nt(q, _head_stack(hm, kc, kp)) * SCALE, NEG)
            ps, ms, ls = [], [], []
            for h in range(2):
                sh = sc[:, 2 * h * BLOCK:2 * (h + 1) * BLOCK]
                m = jnp.max(sh, axis=1, keepdims=True)
                p = jnp.exp(sh - m)
                ps.append(p.astype(BF))
                ms.append(m)
                ls.append(jnp.sum(p, axis=1, keepdims=True))
            o = _dot(jnp.concatenate(ps, axis=1), _head_stack(hm, vc, vp))
            o_ref[rows, lanes] = o / jnp.where(hm[0], ls[0], ls[1])
            l_ref[rows, lanes] = jnp.where(hm[0], ms[0] + jnp.log(ls[0]), ms[1] + jnp.log(ls[1]))

    (qv, kv_, vv), width, (qo, ko, vo) = bv.qkv(hb, g)
    q_cur, _, _ = bv.specs(width, qo)
    k_cur, k_prv, _ = bv.specs(width, ko)
    v_cur, v_prv, _ = bv.specs(width, vo)
    out_spec = bv.specs(GROUP_W, 0)[0]
    out = jax.ShapeDtypeStruct((bv.nl, bv.r * GROUP_W), F32)
    o, l = pl.pallas_call(
        body, name=name, grid=bv.grid,
        in_specs=[q_cur, k_cur, k_prv, v_cur, v_prv], out_specs=[out_spec, out_spec], out_shape=[out, out],
        compiler_params=_cp("parallel", "parallel"))(qv, kv_, kv_, vv, vv)
    return o.reshape(s, GROUP_W), l.reshape(s, GROUP_W)


def _band_combine(os, ls, name):
    ng = len(os)
    s, w = os[0].shape
    ts = _rows(s)

    def body(*refs):
        o_refs, l_refs = refs[:ng], refs[ng:2 * ng]
        oa_ref, lt_ref = refs[2 * ng:]
        lv = [r[...] for r in l_refs]
        m = functools.reduce(jnp.maximum, lv)
        es = [jnp.exp(l - m) for l in lv]
        den = functools.reduce(lambda a, b: a + b, es)
        num = functools.reduce(lambda a, b: a + b, [es[g] * o_refs[g][...] for g in range(ng)])
        oa_ref[...] = (num / den).astype(BF)
        lt_ref[...] = m + jnp.log(den)

    blk = pl.BlockSpec((ts, w), lambda i: (i, 0))
    return pl.pallas_call(
        body, name=name, grid=(s // ts,), in_specs=[blk] * (2 * ng), out_specs=[blk, blk],
        out_shape=[jax.ShapeDtypeStruct((s, w), BF), jax.ShapeDtypeStruct((s, w), F32)],
        compiler_params=_cp("parallel"))(*os, *ls)


def _band_bwd(hb, dcat, oa, lt, g, name):
    s, n = hb.shape
    bv = _BandView(s, g)
    nsub = bv.nsub
    npair = MIX_W // LANES
    ntile = bv.grid[1]

    def body(q_ref, qn_ref, kc_ref, kp_ref, vc_ref, vp_ref, do_ref, don_ref, oa_ref, oan_ref, lt_ref, ltn_ref,
             dq_ref, dk_ref, dv_ref):
        t = pl.program_id(1)
        hm = _head_masks()

        for i, c in [(i, c) for i in range(nsub) for c in range(bv.ncols)]:
            lanes = slice(c * LANES, (c + 1) * LANES)

            def block(ref, edge_ref, i, lanes=lanes):
                if i < 0 or i >= nsub:
                    return edge_ref[:, lanes]
                return ref[i * BLOCK:(i + 1) * BLOCK, lanes]

            has_prev = t > 0 if i == 0 else True
            has_next = t < ntile - 1 if i == nsub - 1 else True
            q, qn = block(q_ref, None, i), block(q_ref, qn_ref, i + 1)
            kc, kp = block(kc_ref, None, i), block(kc_ref, kp_ref, i - 1)
            vc, vp = block(vc_ref, None, i), block(vc_ref, vp_ref, i - 1)
            do, don = block(do_ref, None, i), block(do_ref, don_ref, i + 1)
            dd = do.astype(F32) * block(oa_ref, None, i).astype(F32)
            ddn = don.astype(F32) * block(oa_ref, oan_ref, i + 1).astype(F32)
            lt, ltn = block(lt_ref, None, i), block(lt_ref, ltn_ref, i + 1)

            def per_head(wide, width):
                col = lax.broadcasted_iota(jnp.int32, (BLOCK, 2 * width), 1)
                return jnp.where(col < width, _pick(hm[0], wide, NEG), _pick(hm[1], wide, NEG))

            def row_sums(prod, width):
                col = lax.broadcasted_iota(jnp.int32, (BLOCK, 2 * width), 1)
                return jnp.where(col < width, jnp.sum(_sel(hm[0], prod), axis=1, keepdims=True),
                                 jnp.sum(_sel(hm[1], prod), axis=1, keepdims=True))

            kst, vst = _head_stack(hm, kc, kp), _head_stack(hm, vc, vp)
            p = jnp.exp(jnp.where(_band_mask_stack(has_prev), _dot_nt(q, kst) * SCALE, NEG)
                        - per_head(lt, 2 * BLOCK))
            ds = p * (_dot_nt(do, vst) - row_sums(dd, 2 * BLOCK))
            dq_ref[i * BLOCK:(i + 1) * BLOCK, lanes] = SCALE * _dot(ds.astype(BF), kst)
            kcs = jnp.concatenate([_sel(hm[0], kc), _sel(hm[1], kc)], axis=0)
            vcs = jnp.concatenate([_sel(hm[0], vc), _sel(hm[1], vc)], axis=0)
            qi_ = lax.broadcasted_iota(jnp.int32, (BLOCK, 2 * BLOCK), 0)
            kj_ = jnp.bitwise_and(lax.broadcasted_iota(jnp.int32, (BLOCK, 2 * BLOCK), 1), BLOCK - 1)
            mn = kj_ >= qi_ + jnp.where(has_next, 0, BLOCK)
            pn = jnp.exp(jnp.where(mn, _dot_nt(qn, kcs) * SCALE, NEG) - per_head(ltn, BLOCK))
            dsn = pn * (_dot_nt(don, vcs) - row_sums(ddn, BLOCK))
            pb, dsb, pnb, dsnb = p.astype(BF), ds.astype(BF), pn.astype(BF), dsn.astype(BF)

            def own(x, h):
                return x[:, 2 * h * BLOCK:(2 * h + 1) * BLOCK]

            def nxt(x, h):
                return x[:, h * BLOCK:(h + 1) * BLOCK]

            ds_rows = jnp.concatenate([own(dsb, 0), nxt(dsnb, 0), own(dsb, 1), nxt(dsnb, 1)], axis=0)
            p_rows = jnp.concatenate([own(pb, 0), nxt(pnb, 0), own(pb, 1), nxt(pnb, 1)], axis=0)
            dk_ref[i * BLOCK:(i + 1) * BLOCK, lanes] = SCALE * _dot_tn(ds_rows, _head_stack(hm, q, qn))
            dv_ref[i * BLOCK:(i + 1) * BLOCK, lanes] = _dot_tn(p_rows, _head_stack(hm, do, don))

    (qv, kv_, vv), width, (qo, ko, vo) = bv.qkv(hb, g)
    q_cur, _, q_nxt = bv.specs(width, qo)
    k_cur, k_prv, _ = bv.specs(width, ko)
    v_cur, v_prv, _ = bv.specs(width, vo)
    w_cur, _, w_nxt = bv.specs(GROUP_W, 0)
    out = jax.ShapeDtypeStruct((bv.nl, bv.r * GROUP_W), F32)
    dv_, ov, lv = bv.view(dcat[:, :GROUP_W]), bv.view(oa), bv.view(lt)
    res = pl.pallas_call(
        body, name=name, grid=bv.grid,
        in_specs=[q_cur, q_nxt, k_cur, k_prv, v_cur, v_prv, w_cur, w_nxt, w_cur, w_nxt, w_cur, w_nxt],
        out_specs=[w_cur, w_cur, w_cur], out_shape=[out, out, out],
        compiler_params=_cp("parallel", "parallel"))(
            qv, qv, kv_, kv_, vv, vv, dv_, dv_, ov, ov, lv, lv)
    return [t.reshape(s, GROUP_W) for t in res]


def _mem_fwd(hb, q_blk0, kv, name):
    s = hb.shape[0]
    m = kv.shape[0]
    tq = _rows(s)
    npair = MEM_W // LANES

    def body(q_ref, k_ref, v_ref, o_ref, l_ref):
        q, k, v = q_ref[...], k_ref[...], v_ref[...]
        hm = _head_masks()
        o = jnp.zeros((tq, LANES), F32)
        lse_w = jnp.zeros((tq, LANES), F32)
        for h in range(2):
            sc = _dot_nt(_sel(hm[h], q), k) * SCALE
            mx = jnp.max(sc, axis=1, keepdims=True)
            p = jnp.exp(sc - mx)
            l = jnp.sum(p, axis=1, keepdims=True)
            o = o + _dot(p.astype(BF), _sel(hm[h], v)) / l
            lse_w = jnp.where(hm[h], mx + jnp.log(l), lse_w)
        o_ref[...] = o.astype(BF)
        l_ref[...] = lse_w

    blk = pl.BlockSpec((tq, LANES), lambda p, i: (i, p))
    return pl.pallas_call(
        body, name=name, grid=(npair, s // tq),
        in_specs=[pl.BlockSpec((tq, LANES), lambda p, i: (i, q_blk0 + p)),
                  pl.BlockSpec((m, LANES), lambda p, i: (0, p)),
                  pl.BlockSpec((m, LANES), lambda p, i: (0, npair + p))],
        out_specs=[blk, blk],
        out_shape=[jax.ShapeDtypeStruct((s, MEM_W), BF), jax.ShapeDtypeStruct((s, MEM_W), F32)],
        compiler_params=_cp("parallel", "parallel"))(hb, kv, kv)


def _mem_bwd(hb, q_blk0, kv, dcat, cat, o_blk0, lse, name):
    s = hb.shape[0]
    m = kv.shape[0]
    tq = _rows(s)
    npair = MEM_W // LANES

    def body(q_ref, k_ref, v_ref, do_ref, o_ref, l_ref, dq_ref, dk_ref, dv_ref):
        i = pl.program_id(1)

        @pl.when(i == 0)
        def _():
            dk_ref[...] = jnp.zeros_like(dk_ref)
            dv_ref[...] = jnp.zeros_like(dv_ref)

        q, k, v, do = q_ref[...], k_ref[...], v_ref[...], do_ref[...]
        dd = do.astype(F32) * o_ref[...].astype(F32)
        lt = l_ref[...]
        hm = _head_masks()
        dq = jnp.zeros((tq, LANES), F32)
        dk = jnp.zeros((m, LANES), F32)
        dv = jnp.zeros((m, LANES), F32)
        for h in range(2):
            qh, doh = _sel(hm[h], q), _sel(hm[h], do)
            p = jnp.exp(_dot_nt(qh, k) * SCALE - _pick(hm[h], lt, NEG))
            ds = p * (_dot_nt(doh, v) - jnp.sum(_sel(hm[h], dd), axis=1, keepdims=True))
            dq = dq + SCALE * _dot(ds.astype(BF), _sel(hm[h], k))
            dk = dk + SCALE * _dot_tn(ds.astype(BF), qh)
            dv = dv + _dot_tn(p.astype(BF), doh)
        dq_ref[...] = dq
        dk_ref[...] += dk
        dv_ref[...] += dv

    row = pl.BlockSpec((tq, LANES), lambda p, i: (i, p))
    orow = pl.BlockSpec((tq, LANES), lambda p, i: (i, o_blk0 + p))
    acc = pl.BlockSpec((m, LANES), lambda p, i: (0, p))
    return pl.pallas_call(
        body, name=name, grid=(npair, s // tq),
        in_specs=[pl.BlockSpec((tq, LANES), lambda p, i: (i, q_blk0 + p)),
                  pl.BlockSpec((m, LANES), lambda p, i: (0, p)),
                  pl.BlockSpec((m, LANES), lambda p, i: (0, npair + p)), orow, orow, row],
        out_specs=[row, acc, acc],
        out_shape=[jax.ShapeDtypeStruct((s, MEM_W), F32), jax.ShapeDtypeStruct((m, MEM_W), F32),
                   jax.ShapeDtypeStruct((m, MEM_W), F32)],
        compiler_params=_cp("parallel", "arbitrary"))(hb, kv, kv, dcat, cat, lse)


def _gate_fwd(f_t, bias, name):
    hp, s = f_t.shape
    nblk = s // LANES
    group = 8 if nblk % 8 == 0 else 1

    def body(f_ref, b_ref, c_ref):
        lane = lax.broadcasted_iota(jnp.int32, (hp, LANES), 1)

        def step(i, carry):
            scans = []
            for u in range(group):
                off = pl.multiple_of((i * group + u) * LANES, LANES)
                x = f_ref[:, pl.ds(off, LANES)] + b_ref[...]
                acc = jnp.minimum(x, 0.0) - jnp.log(1.0 + jnp.exp(-jnp.abs(x)))
                sh = 1
                while sh < LANES:
                    acc = acc + jnp.where(lane >= sh, pltpu.roll(acc, sh, 1), 0.0)
                    sh *= 2
                scans.append((off, acc))
            for off, acc in scans:
                acc = acc + carry
                c_ref[:, pl.ds(off, LANES)] = acc
                carry = acc[:, LANES - 1:LANES]
            return carry

        lax.fori_loop(0, nblk // group, step, jnp.zeros((hp, 1), F32))

    vm = pl.BlockSpec(memory_space=pltpu.VMEM)
    return pl.pallas_call(body, name=name, in_specs=[vm, vm], out_specs=vm,
                          out_shape=jax.ShapeDtypeStruct((hp, s), F32),
                          compiler_params=pltpu.CompilerParams(vmem_limit_bytes=VMEM_LIMIT))(f_t, bias)


def _gate_bwd(dc_t, f_t, bias, name):
    hp, s = f_t.shape
    nblk = s // LANES
    group = 8 if nblk % 8 == 0 else 1

    def body(dc_ref, f_ref, b_ref, df_ref, db_ref):
        lane = lax.broadcasted_iota(jnp.int32, (hp, LANES), 1)

        def step(t, carry):
            suffix, dbias = carry
            scans = []
            for u in range(group):
                off = pl.multiple_of((nblk - 1 - (t * group + u)) * LANES, LANES)
                acc = dc_ref[:, pl.ds(off, LANES)]
                sh = 1
                while sh < LANES:
                    acc = acc + jnp.where(lane < LANES - sh, pltpu.roll(acc, LANES - sh, 1), 0.0)
                    sh *= 2
                x = f_ref[:, pl.ds(off, LANES)] + b_ref[...]
                scans.append((off, acc, _sigmoid(-x)))
            for off, acc, sg in scans:
                acc = acc + suffix
                df = acc * sg
                df_ref[:, pl.ds(off, LANES)] = df
                suffix = acc[:, 0:1]
                dbias = dbias + jnp.sum(df, axis=1, keepdims=True)
            return suffix, dbias

        _, dbias = lax.fori_loop(0, nblk // group, step, (jnp.zeros((hp, 1), F32), jnp.zeros((hp, 1), F32)))
        db_ref[...] = dbias

    vm = pl.BlockSpec(memory_space=pltpu.VMEM)
    return pl.pallas_call(body, name=name, in_specs=[vm, vm, vm], out_specs=[vm, vm],
                          out_shape=[jax.ShapeDtypeStruct((hp, s), F32), jax.ShapeDtypeStruct((hp, 1), F32)],
                          compiler_params=pltpu.CompilerParams(vmem_limit_bytes=VMEM_LIMIT))(dc_t, f_t, bias)


def _wide(rep, width):
    return jnp.tile(rep, (1, width // LANES))


def _fold(t):
    part = t[:, :LANES]
    for c in range(1, t.shape[1] // LANES):
        part = part + t[:, c * LANES:(c + 1) * LANES]
    return part


def _foxt_logits(q, k, cq_row, ck_rep, mask, hmask):
    s = _dot_nt(_sel(hmask, k), q) + (cq_row - _wide(ck_rep, q.shape[0]))
    if mask is not None:
        s = jnp.where(mask, s, NEG)
    return s


def _causal_sub(ks, qs):
    shape = (ks.stop - ks.start, qs.stop - qs.start)
    return (ks.start + lax.broadcasted_iota(jnp.int32, shape, 0)
            <= qs.start + lax.broadcasted_iota(jnp.int32, shape, 1))


def _diag_blocks(t):
    h = t // 2
    return [(slice(0, h), slice(0, t)), (slice(h, t), slice(h, t))]


FOX_SPLIT = 1


def _fox_tiles(s):
    tq = _rows(s, 1024)
    return tq, tq // FOX_SPLIT, s // tq


def _fox_steps(nq):
    return FOX_SPLIT * nq * (nq + 1) // 2


def _count_ge(t, bounds):
    return sum([(t >= b).astype(jnp.int32) for b in bounds], jnp.int32(0))


def _sweep_q_major(t, nq):
    qi = _count_ge(t, [FOX_SPLIT * r * (r + 1) // 2 for r in range(1, nq)])
    return qi, t - FOX_SPLIT * qi * (qi + 1) // 2


def _sweep_k_major(t, nq):
    counts = [nq - j // FOX_SPLIT for j in range(FOX_SPLIT * nq)]
    offs = [sum(counts[:j]) for j in range(1, FOX_SPLIT * nq)]
    kj = _count_ge(t, offs)
    start = sum([jnp.where(t >= o, c, 0) for o, c in zip(offs, counts)], jnp.int32(0))
    qi = kj // FOX_SPLIT + (t - start)
    return kj, qi, t == start, qi == nq - 1


def _foxt_fwd(hb, c_rep, c_t3, name):
    s = hb.shape[0]
    npair = MIX_W // LANES
    tq, tk, nq = _fox_tiles(s)

    def body(q_ref, k_ref, v_ref, cq_ref, ck_ref, o_ref, l_ref, m_s, l_s, acc):
        qi, kj = _sweep_q_major(pl.program_id(1), nq)
        hm = _head_masks()

        @pl.when(kj == 0)
        def _():
            m_s[...] = jnp.full_like(m_s, NEG)
            l_s[...] = jnp.zeros_like(l_s)
            acc[...] = jnp.zeros_like(acc)

        def step(ks, qs, masked):
            q, k = q_ref[qs, :] * SCALE, k_ref[ks, :]
            vt = jnp.transpose(v_ref[ks, :])
            cq = cq_ref[:, qs]
            mask = _causal_sub(ks, qs) if masked else None
            for h in range(2):
                st = _foxt_logits(q, k, cq[h:h + 1, :], ck_ref[h, ks, :], mask, hm[h])
                m_old = m_s[h, :, qs]
                m_new = jnp.maximum(m_old, jnp.max(st, axis=0, keepdims=True))
                pt = jnp.exp(st - m_new)
                corr = jnp.exp(m_old - m_new)
                l_s[h, :, qs] = l_s[h, :, qs] * corr + jnp.sum(pt, axis=0, keepdims=True)
                acc[h, :, qs] = acc[h, :, qs] * corr + _dot(vt[h * HEAD_DIM:(h + 1) * HEAD_DIM, :], pt.astype(BF))
                m_s[h, :, qs] = m_new

        @pl.when(kj < qi)
        def _():
            step(slice(0, tk), slice(0, tq), False)

        @pl.when(kj == qi)
        def _():
            for ks, qs in _diag_blocks(tq):
                step(ks, qs, True)
            outs = []
            for h in range(2):
                outs.append(acc[h] / l_s[h])
                l_ref[h:h + 1, :] = m_s[h] + jnp.log(l_s[h])
            o_ref[...] = jnp.transpose(jnp.concatenate(outs, axis=0)).astype(BF)

    def q_map(p, t):
        return (_sweep_q_major(t, nq)[0], p)

    def kv_map(off):
        return lambda p, t: (_sweep_q_major(t, nq)[1], off + p)

    blk = pl.BlockSpec((tq, LANES), q_map)
    row = pl.BlockSpec((None, 2, tq), lambda p, t: (p, 0, _sweep_q_major(t, nq)[0]))
    return pl.pallas_call(
        body, name=name, grid=(npair, _fox_steps(nq)),
        in_specs=[blk, pl.BlockSpec((tk, LANES), kv_map(npair)), pl.BlockSpec((tk, LANES), kv_map(2 * npair)), row,
                  pl.BlockSpec((2, tk, LANES), lambda p, t: (p, _sweep_q_major(t, nq)[1], 0))],
        out_specs=[blk, row],
        out_shape=[jax.ShapeDtypeStruct((s, MIX_W), BF), jax.ShapeDtypeStruct((npair, 2, s), F32)],
        scratch_shapes=[pltpu.VMEM((2, 1, tq), F32), pltpu.VMEM((2, 1, tq), F32),
                        pltpu.VMEM((2, HEAD_DIM, tq), F32)],
        compiler_params=_cp("parallel", "arbitrary"))(hb, hb, hb, c_t3, c_rep)


def _foxt_dsum(hb, dcat, lse, c_rep, c_t3, name):
    s = hb.shape[0]
    npair = MIX_W // LANES
    tq, tk, nq = _fox_tiles(s)

    def body(q_ref, k_ref, v_ref, do_ref, l_ref, cq_ref, ck_ref, d_ref, acc):
        qi, kj = _sweep_q_major(pl.program_id(1), nq)
        hm = _head_masks()

        @pl.when(kj == 0)
        def _():
            acc[...] = jnp.zeros_like(acc)

        def step(ks, qs, masked):
            q, k, v, do = q_ref[qs, :] * SCALE, k_ref[ks, :], v_ref[ks, :], do_ref[qs, :]
            cq, lse_rows = cq_ref[:, qs], l_ref[:, qs]
            mask = _causal_sub(ks, qs) if masked else None
            for h in range(2):
                pt = jnp.exp(_foxt_logits(q, k, cq[h:h + 1, :], ck_ref[h, ks, :], mask, hm[h])
                             - lse_rows[h:h + 1, :])
                acc[h, :, qs] += jnp.sum(pt * _dot_nt(_sel(hm[h], v), do), axis=0, keepdims=True)

        @pl.when(kj < qi)
        def _():
            step(slice(0, tk), slice(0, tq), False)

        @pl.when(kj == qi)
        def _():
            for ks, qs in _diag_blocks(tq):
                step(ks, qs, True)
            for h in range(2):
                d_ref[h:h + 1, :] = acc[h]

    def q_map(p, t):
        return (_sweep_q_major(t, nq)[0], p)

    def kv_map(off):
        return lambda p, t: (_sweep_q_major(t, nq)[1], off + p)

    blk = pl.BlockSpec((tq, LANES), q_map)
    row = pl.BlockSpec((None, 2, tq), lambda p, t: (p, 0, _sweep_q_major(t, nq)[0]))
    return pl.pallas_call(
        body, name=name, grid=(npair, _fox_steps(nq)),
        in_specs=[blk, pl.BlockSpec((tk, LANES), kv_map(npair)), pl.BlockSpec((tk, LANES), kv_map(2 * npair)),
                  blk, row, row, pl.BlockSpec((2, tk, LANES), lambda p, t: (p, _sweep_q_major(t, nq)[1], 0))],
        out_specs=row, out_shape=jax.ShapeDtypeStruct((npair, 2, s), F32),
        scratch_shapes=[pltpu.VMEM((2, 1, tq), F32)],
        compiler_params=_cp("parallel", "arbitrary"))(hb, hb, hb, dcat, lse, c_t3, c_rep)


def _foxt_bwd(hb, dcat, dsum, lse, c_rep, c_t3, name):
    s = hb.shape[0]
    npair = MIX_W // LANES
    tq, tk, nq = _fox_tiles(s)

    def body(q_ref, k_ref, v_ref, do_ref, d_ref, l_ref, cq_ref, ck_ref, dq_ref, dk_ref, dv_ref, dc_ref, dc_s):
        t = pl.program_id(1)
        kj, qi, first, last = _sweep_k_major(t, nq)
        hm = _head_masks()

        @pl.when(first)
        def _():
            dk_ref[...] = jnp.zeros_like(dk_ref)
            dv_ref[...] = jnp.zeros_like(dv_ref)
            dc_s[...] = jnp.zeros_like(dc_s)

        @pl.when(t == 0)
        def _():
            dq_ref[...] = jnp.zeros_like(dq_ref)

        def step(ks, qs, masked):
            q, k, v, do = q_ref[qs, :] * SCALE, k_ref[ks, :], v_ref[ks, :], do_ref[qs, :]
            qt, kt, dot = jnp.transpose(q), jnp.transpose(k), jnp.transpose(do)
            cq, lse_rows, d_rows = cq_ref[:, qs], l_ref[:, qs], d_ref[:, qs]
            mask = _causal_sub(ks, qs) if masked else None
            dqs, dks, dvs = [], [], []
            for h in range(2):
                rows = slice(h * HEAD_DIM, (h + 1) * HEAD_DIM)
                pt = jnp.exp(_foxt_logits(q, k, cq[h:h + 1, :], ck_ref[h, ks, :], mask, hm[h])
                             - lse_rows[h:h + 1, :])
                dst = pt * (_dot_nt(_sel(hm[h], v), do) - d_rows[h:h + 1, :])
                dsb = dst.astype(BF)
                dqs.append(_dot(kt[rows, :], dsb))
                dks.append(_dot_nt(qt[rows, :], dsb))
                dvs.append(_dot_nt(dot[rows, :], pt.astype(BF)))
                dc_s[h, ks, :] += _fold(dst)
            cols = pl.ds(pl.multiple_of(qi * tq + qs.start, qs.stop - qs.start), qs.stop - qs.start)
            dq_ref[:, cols] += SCALE * jnp.concatenate(dqs, axis=0)
            dk_ref[:, ks] += jnp.concatenate(dks, axis=0)
            dv_ref[:, ks] += jnp.concatenate(dvs, axis=0)

        @pl.when(kj < qi)
        def _():
            step(slice(0, tk), slice(0, tq), False)

        @pl.when(kj == qi)
        def _():
            for ks, qs in _diag_blocks(tq):
                step(ks, qs, True)

        @pl.when(last)
        def _():
            for h in range(2):
                dc_ref[h:h + 1, :] = -jnp.sum(jnp.transpose(dc_s[h]), axis=0, keepdims=True)

    def kj_of(t):
        return _sweep_k_major(t, nq)[0]

    def qi_of(t):
        return _sweep_k_major(t, nq)[1]

    qblk = pl.BlockSpec((tq, LANES), lambda p, t: (qi_of(t), p))
    row = pl.BlockSpec((None, 2, tq), lambda p, t: (p, 0, qi_of(t)))
    kblk = pl.BlockSpec((LANES, tk), lambda p, t: (p, kj_of(t)))
    rep = pl.BlockSpec((2, tk, LANES), lambda p, t: (p, kj_of(t), 0))
    return pl.pallas_call(
        body, name=name, grid=(npair, _fox_steps(nq)),
        in_specs=[qblk,
                  pl.BlockSpec((tk, LANES), lambda p, t: (kj_of(t), npair + p)),
                  pl.BlockSpec((tk, LANES), lambda p, t: (kj_of(t), 2 * npair + p)),
                  qblk, row, row, row, rep],
        out_specs=[pl.BlockSpec((LANES, s), lambda p, t: (p, 0)), kblk, kblk,
                   pl.BlockSpec((None, 2, tk), lambda p, t: (p, 0, kj_of(t)))],
        out_shape=[jax.ShapeDtypeStruct((MIX_W, s), F32), jax.ShapeDtypeStruct((MIX_W, s), F32),
                   jax.ShapeDtypeStruct((MIX_W, s), F32), jax.ShapeDtypeStruct((npair, 2, s), F32)],
        scratch_shapes=[pltpu.VMEM((2, tk, LANES), F32)],
        compiler_params=_cp("arbitrary", "arbitrary"))(hb, hb, hb, dcat, dsum, lse, c_t3, c_rep)


def _adam_rows(r, c):
    cap = max(8, (1 << 20) // (4 * c))
    if r <= cap:
        return r
    best = None
    for t in range(8, cap + 1, 8):
        if r % t == 0:
            best = t
    return best if best is not None else r


def _reduce_adamw(contribs, w, m, v, name):
    nl = len(contribs)
    nd, r, c = contribs[0].shape
    tr = _adam_rows(r, c)
    bc1 = 1.0 - ADAM_B1 ** ADAM_STEP
    bc2 = 1.0 - ADAM_B2 ** ADAM_STEP

    def body(*refs):
        c_refs = refs[:nl]
        w_ref, m_ref, v_ref, g_ref, d_ref, nm_ref, nv_ref = refs[nl:]
        l = pl.program_id(0)
        for li in range(nl):
            @pl.when(l == li)
            def _(c_ref=c_refs[li]):
                g = c_ref[0].astype(F32)
                for k in range(1, nd):
                    g = g + c_ref[k].astype(F32)
                nm = ADAM_B1 * m_ref[...] + (1.0 - ADAM_B1) * g
                nv = ADAM_B2 * v_ref[...] + (1.0 - ADAM_B2) * (g * g)
                g_ref[...] = g
                nm_ref[...] = nm
                nv_ref[...] = nv
                d_ref[...] = -ADAM_LR * ((nm / bc1) / (jnp.sqrt(nv / bc2) + ADAM_EPS) + ADAM_WD * w_ref[...])

    def c_spec(li):
        return pl.BlockSpec((nd, tr, c), lambda l, i: (0, jnp.where(l == li, i, 0), 0))

    blk = pl.BlockSpec((None, tr, c), lambda l, i: (l, i, 0))
    out = jax.ShapeDtypeStruct((nl, r, c), F32)
    return pl.pallas_call(
        body, name=name, grid=(nl, r // tr),
        in_specs=[c_spec(li) for li in range(nl)] + [blk, blk, blk],
        out_specs=[blk, blk, blk, blk], out_shape=[out, out, out, out],
        compiler_params=_cp("arbitrary", "arbitrary"))(*contribs, w, m, v)


def _mesh_pos():
    return lax.axis_index("x"), lax.axis_index("y"), lax.axis_index("c")


def _peer(pos, k):
    x, y, c = pos
    return (1 - x if k & 4 else x, 1 - y if k & 2 else y, 1 - c if k & 1 else c)


def _linear(pos):
    return 4 * pos[0] + 2 * pos[1] + pos[2]


def _xfer_copies(srcs, lands, send_sems, recv_sems, local_sems, gather):
    pos = _mesh_pos()
    me = _linear(pos)
    local, remote = [], []
    for i, (src, land) in enumerate(zip(srcs, lands)):
        local.append(pltpu.make_async_copy(src if gather else src.at[me], land.at[me], local_sems.at[i]))
        for k in range(1, N_DEV):
            peer = _peer(pos, k)
            remote.append(pltpu.make_async_remote_copy(
                src_ref=src if gather else src.at[_linear(peer)], dst_ref=land.at[me],
                send_sem=send_sems.at[i * (N_DEV - 1) + k - 1], recv_sem=recv_sems.at[i * (N_DEV - 1) + k - 1],
                device_id=peer, device_id_type=MESH_ID))
    return local, remote


_HBM = pl.BlockSpec(memory_space=pltpu.HBM)
_SEM = pl.BlockSpec(memory_space=pltpu.SEMAPHORE)
_EFFECT = pltpu.SideEffectType.DATAFLOW_SIDE_EFFECTING


def _xfer_start(srcs, gather, name, after=()):
    n = len(srcs)
    na = len(after)
    lands = [lax.empty(((N_DEV,) + a.shape) if gather else a.shape, a.dtype) for a in srcs]

    def body(*refs):
        src, land = refs[:n], refs[n:2 * n]
        send_sems, recv_sems, local_sems = refs[2 * n + na:2 * n + na + 3]
        local, remote = _xfer_copies(src, land, send_sems, recv_sems, local_sems, gather)
        for cp in local + remote:
            cp.start()
        refs[-1][...] = jnp.zeros_like(refs[-1])

    nsem = n * (N_DEV - 1)
    out = pl.pallas_call(
        body, name=name,
        out_shape=(pltpu.SemaphoreType.DMA((nsem,)), pltpu.SemaphoreType.DMA((nsem,)), pltpu.SemaphoreType.DMA((n,)),
                   *[pltpu.HBM(a.shape, a.dtype) for a in srcs], *[pltpu.HBM(a.shape, a.dtype) for a in lands],
                   jax.ShapeDtypeStruct((8, LANES), F32)),
        in_specs=[_HBM] * (2 * n) + [pl.BlockSpec(memory_space=pl.ANY)] * na,
        out_specs=(_SEM, _SEM, _SEM, *[_HBM] * (2 * n), pl.BlockSpec(memory_space=pltpu.VMEM)),
        input_output_aliases={i: 3 + i for i in range(2 * n)},
        compiler_params=pltpu.CompilerParams(has_side_effects=_EFFECT))(
            *[pltpu.with_memory_space_constraint(a, pltpu.HBM) for a in srcs],
            *[pltpu.with_memory_space_constraint(a, pltpu.HBM) for a in lands], *after)
    return out[:3], list(out[3:3 + n]), list(out[3 + n:3 + 2 * n]), out[-1]


def _started(handle):
    return handle[3]


def _xfer_wait(handle, after, gather, name):
    sems, srcs, lands, _ = handle
    n = len(srcs)

    def body(*refs):
        src, land = refs[:n], refs[n:2 * n]
        send_sems, recv_sems, local_sems = refs[2 * n:2 * n + 3]
        local, remote = _xfer_copies(src, land, send_sems, recv_sems, local_sems, gather)
        for cp in local:
            cp.wait()
        for cp in remote:
            cp.wait_send()
            cp.wait_recv()

    out = pl.pallas_call(
        body, name=name,
        out_shape=(*[pltpu.HBM(a.shape, a.dtype) for a in srcs], *[pltpu.HBM(a.shape, a.dtype) for a in lands]),
        in_specs=[_HBM] * (2 * n) + [_SEM] * 3 + [pl.BlockSpec(memory_space=pl.ANY)] * len(after),
        out_specs=tuple([_HBM] * (2 * n)), input_output_aliases={i: i for i in range(2 * n)},
        compiler_params=pltpu.CompilerParams(has_side_effects=_EFFECT))(*srcs, *lands, *sems, *after)
    return list(out[n:])


def _cols_full(g):
    nd, r, c = g.shape
    return jnp.transpose(g, (1, 0, 2)).reshape(r, nd * c)


def _cols_split(full):
    r, n = full.shape
    return jnp.transpose(full.reshape(r, N_DEV, n // N_DEV), (1, 0, 2))


def _pack_b_in(w):
    qkv = 3 * MIX_W
    pad = jnp.zeros((w.shape[0], B_IN_PAD - w.shape[1]), w.dtype)
    return jnp.concatenate([w[:, :qkv], w[:, qkv + N_MIX_HEADS:], w[:, qkv:qkv + N_MIX_HEADS], pad], axis=1)


def _unpack_b_in(w):
    qkv = 3 * MIX_W
    return jnp.concatenate([w[:, :qkv], w[:, qkv + MEM_W:qkv + MEM_W + N_MIX_HEADS], w[:, qkv:qkv + MEM_W]], axis=1)


def _ffn_forward(x, xb, wgu, get_rest, tag, fused=True, target=None):
    if fused:
        wd4, gain, bias = get_rest(x)
        y, yb, gu, a, xh, rstd = _ffn_fwd_main(x, xb, wgu, wd4, gain, bias, f"{tag}_fwd_main", target)
    else:
        gu, a = _ffn_up(xb, wgu, f"{tag}_up")
        wd4, gain, bias = get_rest(a)
        y, yb, xh, rstd = _mm_res_ln(a, wd4, x, gain, bias, 0.5, f"{tag}_down_ln")
    return y, yb, (xb, gu, a, xh, rstd), wd4


def _ffn_backward(dy, saved, wgu, wd4, gain, tag, after=(), send=None):
    xb, gu, a, xh, rstd = saved
    s = xb.shape[0]
    nd, c, d = wgu.shape
    dx, dzb, dh, dgain, dbias = _ffn_bwd_main(dy, xh, rstd, gain, wd4, wgu, gu, f"{tag}_bwd_main", after,
                                               with_dx=send is None)
    dh = dh.reshape(nd, s, c)
    dwd = _mm_tn(a, dzb[None], f"{tag}_dwd").reshape(nd, wd4.shape[1] // 2, d)
    if send is not None:
        send("down", dwd, dgain, dbias)
    dwgu = _mm_tn(dh, xb[None], f"{tag}_dwgu")
    if send is not None:
        sent = send("gate_up", dwgu)
        dx = _mm_nt(dh, wgu, f"{tag}_dx", res=dx, w_rows_out=False, after=sent)
    return dx, dwgu, dwd, dgain, dbias


def _mixer_a_forward(x, xb, memb, w_in, w_kv, w_out, gain, bias, tabs):
    hb = _proj_rope(xb, w_in, tabs, 2 * MIX_W // LANES, "a_in", True)
    groups = [_band_fwd(hb, g, f"a_band_fwd{g}") for g in range(N_GROUPS)]
    oa, lt = _band_combine([o for o, _ in groups], [l for _, l in groups], "a_combine")
    kv = _mm_nn(memb, w_kv, BF, "a_mem_kv")
    om, lm = _mem_fwd(hb, 3 * MIX_W // LANES, kv, "a_mem_fwd")
    cat = jnp.concatenate([oa, om], axis=1)
    y, yb, xh, rstd = _mm_res_ln(cat[None], w_out[None], x, gain, bias, 1.0, "a_out_ln")
    return y, yb, (xb, hb, oa, lt, kv, lm, cat, xh, rstd)


def _mixer_a_backward(dy, saved, memb, w_in, w_kv, w_out, gain, tabs_neg, after=()):
    xb, hb, oa, lt, kv, lm, cat, xh, rstd = saved
    dz, dzb, dcat, dgain, dbias = _ln_bwd_proj(dy, xh, rstd, gain, w_out, "a_ln_bwd", after)
    dw_out = _mm_tn(cat[None], dzb[None], "a_dwout")[0]
    dqm, dkm, dvm = _mem_bwd(hb, 3 * MIX_W // LANES, kv, dcat, cat, GROUP_W // LANES, lm, "a_mem_bwd")
    dkv = jnp.concatenate([dkm, dvm], axis=1).astype(BF)
    dw_kv = _mm_tn(memb[None], dkv[None], "a_dwkv")[0]
    grads = [_band_bwd(hb, dcat, oa, lt, g, f"a_band_bwd{g}") for g in range(N_GROUPS)]
    dhb = _rope_cast([grads[g][i] for i in range(3) for g in range(N_GROUPS)] + [dqm], tabs_neg,
                     2 * MIX_W // LANES, "a_rope_bwd")
    dw_in = _mm_tn(dhb[None], xb[None], "a_dwin")[0]
    dx = _mm_nt(dhb[None], w_in[None], "a_dx", res=dz, w_rows_out=False)
    return dx, dw_in, dw_kv, dw_out, dgain, dbias


def _pad_rows(t, rows):
    return jnp.concatenate([t, jnp.zeros((rows - t.shape[0], t.shape[1]), t.dtype)], axis=0)


def _pad_cols(t, cols):
    return jnp.concatenate([t, jnp.zeros((t.shape[0], cols - t.shape[1]), t.dtype)], axis=1)


def _mixer_b_forward(x, xb, memb, w_in, fbias, w_kv, w_out, gain, bias, tabs):
    s = x.shape[0]
    hb, f = _proj_rope(xb, w_in, tabs, 0, "b_in", False, tail_block=(3 * MIX_W + MEM_W) // LANES)
    f_t = _pad_rows(jnp.transpose(f[:, :N_MIX_HEADS]), 16)
    bias16 = _pad_rows(jnp.transpose(fbias), 16)
    c_t = _gate_fwd(f_t, bias16, "b_gate_fwd")
    c_t3 = c_t[:N_MIX_HEADS].reshape(N_MIX_HEADS // 2, 2, s)
    c_rep = jnp.broadcast_to(c_t[:N_MIX_HEADS, :, None], (N_MIX_HEADS, s, LANES))
    ob, lb = _foxt_fwd(hb, c_rep, c_t3, "b_fox_fwd")
    kv = _mm_nn(memb, w_kv, BF, "b_mem_kv")
    om, lm = _mem_fwd(hb, 3 * MIX_W // LANES, kv, "b_mem_fwd")
    cat = jnp.concatenate([ob, om], axis=1)
    y, yb, xh, rstd = _mm_res_ln(cat[None], w_out[None], x, gain, bias, 1.0, "b_out_ln")
    return y, yb, (xb, hb, f_t, bias16, c_rep, c_t3, lb, kv, lm, cat, xh, rstd)


def _mixer_b_backward(dy, saved, memb, w_in, w_kv, w_out, gain, tabs, after=()):
    xb, hb, f_t, bias16, c_rep, c_t3, lb, kv, lm, cat, xh, rstd = saved
    s = xb.shape[0]
    dz, dzb, dcat, dgain, dbias = _ln_bwd_proj(dy, xh, rstd, gain, w_out, "b_ln_bwd", after)
    dw_out = _mm_tn(cat[None], dzb[None], "b_dwout")[0]
    dqm, dkm, dvm = _mem_bwd(hb, 3 * MIX_W // LANES, kv, dcat, cat, MIX_W // LANES, lm, "b_mem_bwd")
    dkv = jnp.concatenate([dkm, dvm], axis=1).astype(BF)
    dw_kv = _mm_tn(memb[None], dkv[None], "b_dwkv")[0]
    dsum = _foxt_dsum(hb, dcat, lb, c_rep, c_t3, "b_fox_dsum")
    dq, dk, dv, dc3 = _foxt_bwd(hb, dcat, dsum, lb, c_rep, c_t3, "b_fox_bwd")
    df_t, dfb = _gate_bwd(_pad_rows(dc3.reshape(N_MIX_HEADS, s), 16), f_t, bias16, "b_gate_bwd")
    df = _pad_cols(jnp.transpose(df_t[:N_MIX_HEADS]), B_IN_PAD - 3 * MIX_W - MEM_W)
    dhb = _rope_cast([dq, dk, dv, dqm, df], tabs, 0, "b_cast_bwd", transposed=(0, 1, 2))
    dw_in = _mm_tn(xb[None], dhb[None], "b_dwin")[0]
    dx = _mm_nt(dhb[None], w_in[None], "b_dx", res=dz)
    return dx, dw_in, jnp.transpose(dfb[:N_MIX_HEADS]), dw_kv, dw_out, dgain, dbias


def _stored(t, name):
    return jnp.transpose(t, (0, 2, 1)) if name in ROWS_OUT else t


GATHER_GROUPS = (
    (("ffn1_w_gate_up", 0),),
    (("ffn1_w_down", 0), ("ln_gain", None), ("ln_bias", None)),
    (("a_w_in", 0), ("a_w_out", 0), ("mem_w_kv", 0)),
    (("ffn2_w_gate_up", 0), ("ffn2_w_down", 0)),
    (("ffn1_w_gate_up", 1), ("ffn1_w_down", 1)),
    (("b_w_in", 0), ("b_w_out", 0), ("mem_w_kv", 1)),
    (("ffn2_w_gate_up", 1), ("ffn2_w_down", 1)),
)


def _group_shards(group, params):
    return [t if n in F32_COMM else _stored(t, n)[l].astype(BF) for (n, l), t in zip(group, params)]


def _weight_groups(w):
    return [_group_shards(grp, [w[n] for n, _ in grp]) for grp in GATHER_GROUPS]


def _local_step(x, mem, target, fbias, get_w, put_g):
    s, d = x.shape
    tabs = _rope_tables(s, 1.0)
    tabs_neg = _rope_tables(s, -1.0)
    memb = mem.astype(BF)
    saved, wl = [], []
    cur, curb = x, x.astype(BF)
    ln = []

    def down4(t):
        return t.reshape(N_DEV // 2, -1, d)

    for i in range(DEPTH):
        if i == 0:
            def first_rest(a):
                g = get_w(1, a)
                ln.extend(jnp.transpose(t, (1, 2, 0, 3)).reshape(DEPTH, 3, 1, d) for t in g[1:3])
                return down4(g[0]), ln[0][0, 0], ln[1][0, 0]

            wgu = get_w(0, cur)[0]
            cur, curb, s1, wd = _ffn_forward(cur, curb, wgu, first_rest, "l0_ffn1", fused=False)
        else:
            g = get_w(3 * i + 1, cur)
            wgu = g[0]
            cur, curb, s1, wd = _ffn_forward(cur, curb, wgu, lambda a, g=g: (down4(g[1]), ln[0][i, 0], ln[1][i, 0]),
                                             f"l{i}_ffn1")
        w1 = (wgu, wd)
        ln_g, ln_b = ln
        g = get_w(3 * i + 2, cur)
        if i == 0:
            wm = (g[0].reshape(-1, d), g[2].reshape(d, -1), _cols_full(g[1]))
            cur, curb, s2 = _mixer_a_forward(cur, curb, memb, wm[0], wm[1], wm[2], ln_g[i, 1], ln_b[i, 1], tabs)
        else:
            wm = (_pack_b_in(g[0].reshape(d, -1)), g[2].reshape(d, -1), g[1].reshape(d, -1))
            cur, curb, s2 = _mixer_b_forward(cur, curb, memb, wm[0], fbias, wm[1], wm[2], ln_g[i, 1], ln_b[i, 1],
                                             tabs)
        g = get_w(3 * i + 3, cur)
        cur, curb, s3, wd = _ffn_forward(cur, curb, g[0], lambda a, g=g: (down4(g[1]), ln_g[i, 2], ln_b[i, 2]),
                                         f"l{i}_ffn2", target=target if i == DEPTH - 1 else None)
        w3 = (g[0], wd)
        saved.append((s1, s2, s3))
        wl.append((w1, wm, w3))

    dy, loss = cur, curb

    dgs = [[None] * 3 for _ in range(DEPTH)]
    dbs = [[None] * 3 for _ in range(DEPTH)]
    sent = ()
    for i in reversed(range(DEPTH)):
        s1, s2, s3 = saved[i]
        w1, wm, w3 = wl[i]
        dy, dgu, dd, dgs[i][2], dbs[i][2] = _ffn_backward(dy, s3, w3[0], w3[1], ln_g[i, 2], f"l{i}_ffn2", sent)
        sent = put_g(3 * i + 2, [dgu, dd])
        if i == 0:
            dy, dw_in, dw_kv, dw_out, dgs[i][1], dbs[i][1] = _mixer_a_backward(
                dy, s2, memb, wm[0], wm[1], wm[2], ln_g[i, 1], tabs_neg, sent)
            sent = put_g(1, [dw_in.reshape(N_DEV, -1, d), _cols_split(dw_out),
                             dw_kv.reshape(N_DEV, d // N_DEV, -1)])
        else:
            dy, dw_in, dfb, dw_kv, dw_out, dgs[i][1], dbs[i][1] = _mixer_b_backward(
                dy, s2, memb, wm[0], wm[1], wm[2], ln_g[i, 1], tabs, sent)
            sent = put_g(4, [_unpack_b_in(dw_in).reshape(N_DEV, d // N_DEV, -1),
                             dw_out.reshape(N_DEV, d // N_DEV, -1), dw_kv.reshape(N_DEV, d // N_DEV, -1),
                             jnp.broadcast_to(dfb[None], (N_DEV,) + dfb.shape)])
        if i == 0:
            def send_last(kind, dw, dgain=None, dbias=None):
                if kind == "gate_up":
                    return put_g(6, [dw])
                dgs[0][0], dbs[0][0] = dgain, dbias
                ln_pieces = []
                for parts in (dgs, dbs):
                    t = jnp.concatenate([parts[a][b] for a in range(DEPTH) for b in range(3)], axis=0)
                    ln_pieces.append(jnp.transpose(t.reshape(DEPTH * 3, N_DEV, d // N_DEV), (1, 0, 2)))
                return put_g(0, [dw] + ln_pieces)

            dy = _ffn_backward(dy, s1, w1[0], w1[1], ln_g[i, 0], "l0_ffn1", sent, send_last)[0]
        else:
            dy, dgu, dd, dgs[i][0], dbs[i][0] = _ffn_backward(dy, s1, w1[0], w1[1], ln_g[i, 0], f"l{i}_ffn1", sent)
            sent = put_g(3, [dgu, dd])
    return loss, dy


WEIGHTS = ("ffn1_w_gate_up", "ffn1_w_down", "ffn2_w_gate_up", "ffn2_w_down", "ln_gain", "ln_bias", "mem_w_kv",
           "a_w_in", "a_w_out", "b_w_in", "b_forget_bias", "b_w_out")
F32_COMM = ("ln_gain", "ln_bias", "b_forget_bias")
ROWS_OUT = ("ffn1_w_gate_up", "ffn2_w_gate_up", "a_w_in")
GRAD_SLOTS = {
    "ffn1_w_gate_up": [(6, 0), (3, 0)], "ffn1_w_down": [(0, 0), (3, 1)],
    "ffn2_w_gate_up": [(2, 0), (5, 0)], "ffn2_w_down": [(2, 1), (5, 1)],
    "ln_gain": [(0, 1)], "ln_bias": [(0, 2)], "mem_w_kv": [(1, 2), (4, 2)],
    "a_w_in": [(1, 0)], "a_w_out": [(1, 1)], "b_w_in": [(4, 0)], "b_forget_bias": [(4, 3)], "b_w_out": [(4, 1)],
}


def kernel(x, mem, ffn1_w_gate_up, ffn1_w_down, ffn2_w_gate_up, ffn2_w_down, ln_gain, ln_bias, mem_w_kv, a_w_in, a_w_out, b_w_in, b_forget_bias, b_w_out, loss_target, m_ffn1_w_gate_up, m_ffn1_w_down, m_ffn2_w_gate_up, m_ffn2_w_down, m_ln_gain, m_ln_bias, m_mem_w_kv, m_a_w_in, m_a_w_out, m_b_w_in, m_b_forget_bias, m_b_w_out, v_ffn1_w_gate_up, v_ffn1_w_down, v_ffn2_w_gate_up, v_ffn2_w_down, v_ln_gain, v_ln_bias, v_mem_w_kv, v_a_w_in, v_a_w_out, v_b_w_in, v_b_forget_bias, v_b_w_out):
    w = dict(zip(WEIGHTS, (ffn1_w_gate_up, ffn1_w_down, ffn2_w_gate_up, ffn2_w_down, ln_gain, ln_bias, mem_w_kv,
                           a_w_in, a_w_out, b_w_in, b_forget_bias, b_w_out)))
    m = dict(zip(WEIGHTS, (m_ffn1_w_gate_up, m_ffn1_w_down, m_ffn2_w_gate_up, m_ffn2_w_down, m_ln_gain, m_ln_bias,
                           m_mem_w_kv, m_a_w_in, m_a_w_out, m_b_w_in, m_b_forget_bias, m_b_w_out)))
    v = dict(zip(WEIGHTS, (v_ffn1_w_gate_up, v_ffn1_w_down, v_ffn2_w_gate_up, v_ffn2_w_down, v_ln_gain, v_ln_bias,
                           v_mem_w_kv, v_a_w_in, v_a_w_out, v_b_w_in, v_b_forget_bias, v_b_w_out)))

    gathers = []
    for k, grp in enumerate(GATHER_GROUPS):
        params, behind = [w[n] for n, _ in grp], [_started(h) for h in gathers[-1:]]
        if behind:
            params, behind = lax.optimization_barrier((params, behind))
        gathers.append(_xfer_start(_group_shards(grp, params), True, f"gather{k}_start", behind))
    exchanges = {}

    def get_w(k, after):
        behind = [after] + ([_started(h) for h in gathers] if k == 0 else [])
        return _xfer_wait(gathers[k], behind, True, f"gather{k}_wait")

    def put_g(k, pieces):
        behind = [_started(exchanges[0])] if k == 6 else []
        exchanges[k] = _xfer_start(pieces, False, f"grads{k}_start", behind)
        return (_started(exchanges[k]),)

    loss, grad_x = _local_step(x[0], mem[0], loss_target[0], b_forget_bias, get_w, put_g)
    loss = lax.psum(loss[0, 0], ("x", "y", "c"))

    outs, landed = {}, {}

    def adamw(names):
        for n in names:
            contribs = [landed[g][j] for g, j in GRAD_SLOTS[n]]
            view = (len(contribs),) + contribs[0].shape[1:]
            shape = _stored(w[n], n).shape
            res = _reduce_adamw(contribs, *[_stored(t[n], n).reshape(view) for t in (w, m, v)], f"adamw_{n}")
            outs[n] = [_stored(t.reshape(shape), n) for t in res]
        return [outs[n][3] for n in names]

    after = [grad_x]
    for k in (5, 4, 3, 2, 1):
        landed[k] = _xfer_wait(exchanges[k], after, False, f"grads{k}_wait")
        after = [landed[k][0]]
    done = adamw(("ffn2_w_gate_up", "ffn2_w_down", "mem_w_kv", "a_w_in", "a_w_out", "b_w_in", "b_forget_bias",
                  "b_w_out"))
    landed[0] = _xfer_wait(exchanges[0], done, False, "grads0_wait")
    done = adamw(("ffn1_w_down", "ln_gain", "ln_bias"))
    landed[6] = _xfer_wait(exchanges[6], done, False, "grads6_wait")
    adamw(("ffn1_w_gate_up",))
    return (loss, grad_x[None], *[outs[n][0] for n in WEIGHTS], *[outs[n][1] for n in WEIGHTS],
            *[outs[n][2] for n in WEIGHTS], *[outs[n][3] for n in WEIGHTS])
```
